```python
import jax, jax.numpy as jnp
from jax import lax
import numpy as np

D_MODEL = 1024
BATCH = 8
SEQ = 8192
DEPTH = 1

D_CONV = D_MODEL
CONV_WIDTH = 3
HEAD_DIM = 64
HEADS_PER_GROUP = 8
GROUPS = ((128, 1), (512, 4), (2048, 16))
N_GROUPS = len(GROUPS)
N_ATT_HEADS = N_GROUPS * HEADS_PER_GROUP
ATT_GROUP_W = HEADS_PER_GROUP * HEAD_DIM
ATT_QKV_W = N_GROUPS * ATT_GROUP_W
D_FF = 2816
LN_EPS = 1e-5
ALPHA = (2.0 * DEPTH) ** 0.25
BETA = (8.0 * DEPTH) ** -0.25
MASK_VALUE = -1e30

OFF_B = 0
OFF_C = OFF_B + D_CONV
OFF_H = OFF_C + D_CONV
OFF_Q = OFF_H + D_CONV
OFF_K = OFF_Q + ATT_QKV_W
OFF_V = OFF_K + ATT_QKV_W
OFF_GA = OFF_V + ATT_QKV_W
OFF_GB = OFF_GA + D_MODEL
N_IN = OFF_GB + D_MODEL

kernel_name = "hybrid_shortconv_dilated_alibi_deepnorm_encoder"


def layer_norm(x, g, b):
    xf = x.astype(jnp.float32)
    mu = jnp.mean(xf, -1, keepdims=True)
    xc = xf - mu
    var = jnp.mean(xc * xc, -1, keepdims=True)
    return (xc * lax.rsqrt(var + LN_EPS) * g + b).astype(x.dtype)


def dwconv3(u, w):
    up = jnp.pad(u, ((0, 0), (1, 1), (0, 0)))
    return up[:, :-2] * w[0] + up[:, 1:-1] * w[1] + up[:, 2:] * w[2]


def alibi_slopes(n):
    return jnp.exp2(-8.0 * jnp.arange(1, n + 1, dtype=jnp.float32) / n)


def dilated_window_attention(q, k, v, dil, radius, slopes):
    bsz, seq, nh, hd = q.shape
    sub_len = seq // dil
    blk = radius
    nb = -(-sub_len // blk)
    lp = nb * blk

    def to_sub(a):
        a = a.reshape(bsz, sub_len, dil, nh, hd).transpose(0, 2, 1, 3, 4)
        return a.reshape(bsz * dil, sub_len, nh, hd)

    qs, ks, vs = to_sub(q), to_sub(k), to_sub(v)
    qb = jnp.pad(qs, ((0, 0), (0, lp - sub_len), (0, 0), (0, 0))).reshape(bsz * dil, nb, blk, nh, hd)

    def windows(a):
        ap = jnp.pad(a, ((0, 0), (blk, lp - sub_len + blk), (0, 0), (0, 0)))
        ap = ap.reshape(bsz * dil, nb + 2, blk, nh, hd)
        return jnp.concatenate([ap[:, :-2], ap[:, 1:-1], ap[:, 2:]], axis=2)

    kw, vw = windows(ks), windows(vs)
    qpos = jnp.arange(lp).reshape(nb, blk)
    kpos = (jnp.arange(nb)[:, None] - 1) * blk + jnp.arange(3 * blk)[None, :]
    rel = kpos[:, None, :] - qpos[:, :, None]
    valid = (jnp.abs(rel) <= radius) & (kpos[:, None, :] >= 0) & (kpos[:, None, :] < sub_len)
    dist = (jnp.abs(rel) * dil).astype(jnp.float32)
    bias = -slopes[None, :, None, None] * dist[:, None]
    s = jnp.einsum('bnqhd,bnkhd->bnhqk', qb, kw).astype(jnp.float32) * (hd ** -0.5) + bias
    s = jnp.where(valid[:, None], s, MASK_VALUE)
    m = jnp.max(s, -1, keepdims=True)
    p = jnp.exp(s - m)
    den = jnp.sum(p, -1, keepdims=True)
    o = jnp.einsum('bnhqk,bnkhd->bnqhd', (p / den).astype(v.dtype), vw)
    lse = jnp.transpose((m + jnp.log(den))[..., 0], (0, 1, 3, 2))
    o = o.reshape(bsz, dil, lp, nh, hd)[:, :, :sub_len].transpose(0, 2, 1, 3, 4).reshape(bsz, seq, nh, hd)
    lse = lse.reshape(bsz, dil, lp, nh)[:, :, :sub_len].transpose(0, 2, 1, 3).reshape(bsz, seq, nh)
    return o, lse


def _fwd_setup_inputs(seed: int = 0) -> dict:
    key = jax.random.key(seed)
    ks = jax.random.split(key, 24)

    def nrm(k, shape, scale):
        return jax.random.normal(k, shape, jnp.float32) * scale

    col_scale = np.ones((N_IN,), np.float32)
    col_scale[OFF_H:OFF_H + D_CONV] = BETA
    col_scale[OFF_V:OFF_V + ATT_QKV_W] = BETA
    L = DEPTH
    return {
        "x": nrm(ks[0], (BATCH, SEQ, D_MODEL), 1.0),
        "ln0_g": 1.0 + nrm(ks[1], (D_MODEL,), 0.02),
        "ln0_b": nrm(ks[2], (D_MODEL,), 0.02),
        "w_in": nrm(ks[3], (L, D_MODEL, N_IN), D_MODEL ** -0.5) * jnp.asarray(col_scale),
        "b_in": nrm(ks[4], (L, N_IN), 0.02),
        "conv_w": nrm(ks[5], (L, CONV_WIDTH, D_CONV), CONV_WIDTH ** -0.5),
        "w_a": nrm(ks[6], (L, D_CONV, D_MODEL), BETA * D_CONV ** -0.5),
        "w_b": nrm(ks[7], (L, ATT_GROUP_W, D_MODEL), BETA * ATT_GROUP_W ** -0.5),
        "w_o": nrm(ks[8], (L, D_MODEL, D_MODEL), BETA * D_MODEL ** -0.5),
        "b_o": nrm(ks[9], (L, D_MODEL), 0.02),
        "ln1_g": 1.0 + nrm(ks[10], (L, D_MODEL), 0.02),
        "ln1_b": nrm(ks[11], (L, D_MODEL), 0.02),
        "w_up": nrm(ks[12], (L, D_MODEL, 2 * D_FF), BETA * D_MODEL ** -0.5),
        "b_up": nrm(ks[13], (L, 2 * D_FF), 0.02),
        "ffn_conv_w": nrm(ks[14], (L, CONV_WIDTH, D_FF), CONV_WIDTH ** -0.5),
        "ffn_conv_b": nrm(ks[15], (L, D_FF), 0.02),
        "w_down": nrm(ks[16], (L, D_FF, D_MODEL), BETA * D_FF ** -0.5),
        "b_down": nrm(ks[17], (L, D_MODEL), 0.02),
        "ln2_g": 1.0 + nrm(ks[18], (L, D_MODEL), 0.02),
        "ln2_b": nrm(ks[19], (L, D_MODEL), 0.02),
    }


def _fwd_reference(x, ln0_g, ln0_b, w_in, b_in, conv_w, w_a, w_b, w_o, b_o, ln1_g, ln1_b,
              w_up, b_up, ffn_conv_w, ffn_conv_b, w_down, b_down, ln2_g, ln2_b):
    bsz, seq, _ = x.shape
    slopes = alibi_slopes(N_ATT_HEADS).reshape(N_GROUPS, HEADS_PER_GROUP)
    h = layer_norm(x, ln0_g, ln0_b)
    for l in range(DEPTH):
        proj = h @ w_in[l] + b_in[l]
        gate_b = proj[..., OFF_B:OFF_B + D_CONV]
        gate_c = proj[..., OFF_C:OFF_C + D_CONV]
        hin = proj[..., OFF_H:OFF_H + D_CONV]
        y_a = (gate_b * dwconv3(gate_c * hin, conv_w[l])) @ w_a[l]
        q = proj[..., OFF_Q:OFF_Q + ATT_QKV_W].reshape(bsz, seq, N_GROUPS, HEADS_PER_GROUP, HEAD_DIM)
        k = proj[..., OFF_K:OFF_K + ATT_QKV_W].reshape(bsz, seq, N_GROUPS, HEADS_PER_GROUP, HEAD_DIM)
        v = proj[..., OFF_V:OFF_V + ATT_QKV_W].reshape(bsz, seq, N_GROUPS, HEADS_PER_GROUP, HEAD_DIM)
        outs, lses = [], []
        for g, (window, dil) in enumerate(GROUPS):
            o, lse = dilated_window_attention(q[:, :, g], k[:, :, g], v[:, :, g], dil,
                                              window // (2 * dil), slopes[g])
            outs.append(o)
            lses.append(lse)
        wts = jax.nn.softmax(jnp.stack(lses, 0), axis=0)
        comb = jnp.sum(wts[..., None].astype(x.dtype) * jnp.stack(outs, 0), axis=0)
        y_b = comb.reshape(bsz, seq, ATT_GROUP_W) @ w_b[l]
        g_a = jax.nn.sigmoid(proj[..., OFF_GA:OFF_GA + D_MODEL])
        g_b = jax.nn.sigmoid(proj[..., OFF_GB:OFF_GB + D_MODEL])
        mix = (g_a * y_a + g_b * y_b) @ w_o[l] + b_o[l]
        h = layer_norm(ALPHA * h + mix, ln1_g[l], ln1_b[l])
        up = h @ w_up[l] + b_up[l]
        a, gte = up[..., :D_FF], up[..., D_FF:]
        f = jax.nn.gelu(dwconv3(a, ffn_conv_w[l]) + ffn_conv_b[l], approximate=False) * gte
        ffn = f @ w_down[l] + b_down[l]
        h = layer_norm(ALPHA * h + ffn, ln2_g[l], ln2_b[l])
    return h


import jax as _jax
import jax.numpy as _jnp

TWIN_FORMAT = 'train_step'
FWD_PARAMS = ['x', 'ln0_g', 'ln0_b', 'w_in', 'b_in', 'conv_w', 'w_a', 'w_b', 'w_o', 'b_o', 'ln1_g', 'ln1_b', 'w_up', 'b_up', 'ffn_conv_w', 'ffn_conv_b', 'w_down', 'b_down', 'ln2_g', 'ln2_b']
TWIN_WEIGHTS = ['ln0_g', 'ln0_b', 'w_in', 'b_in', 'conv_w', 'w_a', 'w_b', 'w_o', 'b_o', 'ln1_g', 'ln1_b', 'w_up', 'b_up', 'ffn_conv_w', 'ffn_conv_b', 'w_down', 'b_down', 'ln2_g', 'ln2_b']
TWIN_DIFF_INPUT = 'x'
TWIN_INPUTS = ['x', 'ln0_g', 'ln0_b', 'w_in', 'b_in', 'conv_w', 'w_a', 'w_b', 'w_o', 'b_o', 'ln1_g', 'ln1_b', 'w_up', 'b_up', 'ffn_conv_w', 'ffn_conv_b', 'w_down', 'b_down', 'ln2_g', 'ln2_b', 'loss_target', 'm_ln0_g', 'm_ln0_b', 'm_w_in', 'm_b_in', 'm_conv_w', 'm_w_a', 'm_w_b', 'm_w_o', 'm_b_o', 'm_ln1_g', 'm_ln1_b', 'm_w_up', 'm_b_up', 'm_ffn_conv_w', 'm_ffn_conv_b', 'm_w_down', 'm_b_down', 'm_ln2_g', 'm_ln2_b', 'v_ln0_g', 'v_ln0_b', 'v_w_in', 'v_b_in', 'v_conv_w', 'v_w_a', 'v_w_b', 'v_w_o', 'v_b_o', 'v_ln1_g', 'v_ln1_b', 'v_w_up', 'v_b_up', 'v_ffn_conv_w', 'v_ffn_conv_b', 'v_w_down', 'v_b_down', 'v_ln2_g', 'v_ln2_b']
TWIN_OUTPUTS = ['loss', 'grad_x', 'grad_ln0_g', 'grad_ln0_b', 'grad_w_in', 'grad_b_in', 'grad_conv_w', 'grad_w_a', 'grad_w_b', 'grad_w_o', 'grad_b_o', 'grad_ln1_g', 'grad_ln1_b', 'grad_w_up', 'grad_b_up', 'grad_ffn_conv_w', 'grad_ffn_conv_b', 'grad_w_down', 'grad_b_down', 'grad_ln2_g', 'grad_ln2_b', 'delta_ln0_g', 'delta_ln0_b', 'delta_w_in', 'delta_b_in', 'delta_conv_w', 'delta_w_a', 'delta_w_b', 'delta_w_o', 'delta_b_o', 'delta_ln1_g', 'delta_ln1_b', 'delta_w_up', 'delta_b_up', 'delta_ffn_conv_w', 'delta_ffn_conv_b', 'delta_w_down', 'delta_b_down', 'delta_ln2_g', 'delta_ln2_b', 'new_m_ln0_g', 'new_m_ln0_b', 'new_m_w_in', 'new_m_b_in', 'new_m_conv_w', 'new_m_w_a', 'new_m_w_b', 'new_m_w_o', 'new_m_b_o', 'new_m_ln1_g', 'new_m_ln1_b', 'new_m_w_up', 'new_m_b_up', 'new_m_ffn_conv_w', 'new_m_ffn_conv_b', 'new_m_w_down', 'new_m_b_down', 'new_m_ln2_g', 'new_m_ln2_b', 'new_v_ln0_g', 'new_v_ln0_b', 'new_v_w_in', 'new_v_b_in', 'new_v_conv_w', 'new_v_w_a', 'new_v_w_b', 'new_v_w_o', 'new_v_b_o', 'new_v_ln1_g', 'new_v_ln1_b', 'new_v_w_up', 'new_v_b_up', 'new_v_ffn_conv_w', 'new_v_ffn_conv_b', 'new_v_w_down', 'new_v_b_down', 'new_v_ln2_g', 'new_v_ln2_b']
TWIN_LEAF_KINDS = {'loss': 'loss', 'grad_x': 'grad_x', 'grad_ln0_g': 'grad_w', 'grad_ln0_b': 'grad_w', 'grad_w_in': 'grad_w', 'grad_b_in': 'grad_w', 'grad_conv_w': 'grad_w', 'grad_w_a': 'grad_w', 'grad_w_b': 'grad_w', 'grad_w_o': 'grad_w', 'grad_b_o': 'grad_w', 'grad_ln1_g': 'grad_w', 'grad_ln1_b': 'grad_w', 'grad_w_up': 'grad_w', 'grad_b_up': 'grad_w', 'grad_ffn_conv_w': 'grad_w', 'grad_ffn_conv_b': 'grad_w', 'grad_w_down': 'grad_w', 'grad_b_down': 'grad_w', 'grad_ln2_g': 'grad_w', 'grad_ln2_b': 'grad_w', 'delta_ln0_g': 'delta_w', 'delta_ln0_b': 'delta_w', 'delta_w_in': 'delta_w', 'delta_b_in': 'delta_w', 'delta_conv_w': 'delta_w', 'delta_w_a': 'delta_w', 'delta_w_b': 'delta_w', 'delta_w_o': 'delta_w', 'delta_b_o': 'delta_w', 'delta_ln1_g': 'delta_w', 'delta_ln1_b': 'delta_w', 'delta_w_up': 'delta_w', 'delta_b_up': 'delta_w', 'delta_ffn_conv_w': 'delta_w', 'delta_ffn_conv_b': 'delta_w', 'delta_w_down': 'delta_w', 'delta_b_down': 'delta_w', 'delta_ln2_g': 'delta_w', 'delta_ln2_b': 'delta_w', 'new_m_ln0_g': 'new_m', 'new_m_ln0_b': 'new_m', 'new_m_w_in': 'new_m', 'new_m_b_in': 'new_m', 'new_m_conv_w': 'new_m', 'new_m_w_a': 'new_m', 'new_m_w_b': 'new_m', 'new_m_w_o': 'new_m', 'new_m_b_o': 'new_m', 'new_m_ln1_g': 'new_m', 'new_m_ln1_b': 'new_m', 'new_m_w_up': 'new_m', 'new_m_b_up': 'new_m', 'new_m_ffn_conv_w': 'new_m', 'new_m_ffn_conv_b': 'new_m', 'new_m_w_down': 'new_m', 'new_m_b_down': 'new_m', 'new_m_ln2_g': 'new_m', 'new_m_ln2_b': 'new_m', 'new_v_ln0_g': 'new_v', 'new_v_ln0_b': 'new_v', 'new_v_w_in': 'new_v', 'new_v_b_in': 'new_v', 'new_v_conv_w': 'new_v', 'new_v_w_a': 'new_v', 'new_v_w_b': 'new_v', 'new_v_w_o': 'new_v', 'new_v_b_o': 'new_v', 'new_v_ln1_g': 'new_v', 'new_v_ln1_b': 'new_v', 'new_v_w_up': 'new_v', 'new_v_b_up': 'new_v', 'new_v_ffn_conv_w': 'new_v', 'new_v_ffn_conv_b': 'new_v', 'new_v_w_down': 'new_v', 'new_v_b_down': 'new_v', 'new_v_ln2_g': 'new_v', 'new_v_ln2_b': 'new_v'}


def _forward(args):
    return _fwd_reference(*[args[k] for k in FWD_PARAMS])


def _output_shape():
    def fwd():
        inp = _fwd_setup_inputs(0)
        return _fwd_reference(*[inp[k] for k in FWD_PARAMS])
    out = _jax.eval_shape(fwd)
    return out.shape, out.dtype

N_MICROBATCH = 1
ADAM_LR = 0.001
ADAM_B1 = 0.9
ADAM_B2 = 0.999
ADAM_EPS = 1e-08
ADAM_WD = 0.01
ADAM_STEP = 10
PER_EXAMPLE_BATCH_AXIS = {'x': 0, 'loss_target': 0}
SHARED_INPUTS = []
_WEIGHT_DTYPES = {'ln0_g': _jnp.float32, 'ln0_b': _jnp.float32, 'w_in': _jnp.float32, 'b_in': _jnp.float32, 'conv_w': _jnp.float32, 'w_a': _jnp.float32, 'w_b': _jnp.float32, 'w_o': _jnp.float32, 'b_o': _jnp.float32, 'ln1_g': _jnp.float32, 'ln1_b': _jnp.float32, 'w_up': _jnp.float32, 'b_up': _jnp.float32, 'ffn_conv_w': _jnp.float32, 'ffn_conv_b': _jnp.float32, 'w_down': _jnp.float32, 'b_down': _jnp.float32, 'ln2_g': _jnp.float32, 'ln2_b': _jnp.float32}
MOMENT_SCALE = {'ln0_g': 2.181933e+00, 'ln0_b': 1.056731e+00, 'w_in': 1.772320e-02, 'b_in': 2.852274e-02, 'conv_w': 2.362090e-02, 'w_a': 4.040961e-02, 'w_b': 1.018119e-02, 'w_o': 4.142663e-02, 'b_o': 8.898981e-01, 'ln1_g': 2.214496e+00, 'ln1_b': 1.065091e+00, 'w_up': 2.794948e-02, 'b_up': 3.543956e-02, 'ffn_conv_w': 1.740022e-02, 'ffn_conv_b': 2.708824e-02, 'w_down': 4.541286e-02, 'b_down': 8.889055e-01, 'ln2_g': 6.408164e+01, 'ln2_b': 2.277281e+00}


def _to_microbatches(a, axis):
    t = _jnp.moveaxis(a, axis, 0)
    t = t.reshape((N_MICROBATCH, t.shape[0] // N_MICROBATCH) + t.shape[1:])
    return _jnp.moveaxis(t, 1, axis + 1)


def setup_inputs(seed: int = 0) -> dict:
    inp = _fwd_setup_inputs(seed)
    key = _jax.random.fold_in(_jax.random.key(seed), 7919)
    shape, _ = _output_shape()
    out = dict(inp)
    out["loss_target"] = _jax.random.normal(_jax.random.fold_in(key, 0), shape, _jnp.float32)
    for i, name in enumerate(TWIN_WEIGHTS):
        w = inp[name].astype(_jnp.float32)
        if MOMENT_SCALE is None:
            s = _jnp.sqrt(_jnp.mean(_jnp.square(w)) + 1e-30)
        else:
            s = MOMENT_SCALE[name]
        km, kv = _jax.random.split(_jax.random.fold_in(key, i + 1))
        out[name] = w
        out["m_" + name] = s * _jax.random.normal(km, w.shape, _jnp.float32)
        out["v_" + name] = (s * s) * _jax.random.uniform(kv, w.shape, _jnp.float32, 0.5, 1.5)
    if N_MICROBATCH > 1:
        for name, axis in PER_EXAMPLE_BATCH_AXIS.items():
            out[name] = _to_microbatches(out[name], axis)
    return {'x': out['x'], 'ln0_g': out['ln0_g'], 'ln0_b': out['ln0_b'], 'w_in': out['w_in'], 'b_in': out['b_in'], 'conv_w': out['conv_w'], 'w_a': out['w_a'], 'w_b': out['w_b'], 'w_o': out['w_o'], 'b_o': out['b_o'], 'ln1_g': out['ln1_g'], 'ln1_b': out['ln1_b'], 'w_up': out['w_up'], 'b_up': out['b_up'], 'ffn_conv_w': out['ffn_conv_w'], 'ffn_conv_b': out['ffn_conv_b'], 'w_down': out['w_down'], 'b_down': out['b_down'], 'ln2_g': out['ln2_g'], 'ln2_b': out['ln2_b'], 'loss_target': out['loss_target'], 'm_ln0_g': out['m_ln0_g'], 'm_ln0_b': out['m_ln0_b'], 'm_w_in': out['m_w_in'], 'm_b_in': out['m_b_in'], 'm_conv_w': out['m_conv_w'], 'm_w_a': out['m_w_a'], 'm_w_b': out['m_w_b'], 'm_w_o': out['m_w_o'], 'm_b_o': out['m_b_o'], 'm_ln1_g': out['m_ln1_g'], 'm_ln1_b': out['m_ln1_b'], 'm_w_up': out['m_w_up'], 'm_b_up': out['m_b_up'], 'm_ffn_conv_w': out['m_ffn_conv_w'], 'm_ffn_conv_b': out['m_ffn_conv_b'], 'm_w_down': out['m_w_down'], 'm_b_down': out['m_b_down'], 'm_ln2_g': out['m_ln2_g'], 'm_ln2_b': out['m_ln2_b'], 'v_ln0_g': out['v_ln0_g'], 'v_ln0_b': out['v_ln0_b'], 'v_w_in': out['v_w_in'], 'v_b_in': out['v_b_in'], 'v_conv_w': out['v_conv_w'], 'v_w_a': out['v_w_a'], 'v_w_b': out['v_w_b'], 'v_w_o': out['v_w_o'], 'v_b_o': out['v_b_o'], 'v_ln1_g': out['v_ln1_g'], 'v_ln1_b': out['v_ln1_b'], 'v_w_up': out['v_w_up'], 'v_b_up': out['v_b_up'], 'v_ffn_conv_w': out['v_ffn_conv_w'], 'v_ffn_conv_b': out['v_ffn_conv_b'], 'v_w_down': out['v_w_down'], 'v_b_down': out['v_b_down'], 'v_ln2_g': out['v_ln2_g'], 'v_ln2_b': out['v_ln2_b']}


def _loss(weights, diff, rest, loss_target):
    with _jax.named_scope("forward"):
        args = {**rest, TWIN_DIFF_INPUT: diff, **{k: w.astype(_WEIGHT_DTYPES[k]) for k, w in weights.items()}}
        y = _forward(args)
    with _jax.named_scope("loss_head"):
        err = _jnp.square(y.astype(_jnp.float32) - loss_target)
        return 0.5 * _jnp.sum(_jnp.mean(err, axis=-1)) if err.ndim else 0.5 * err


def _adamw(w, g, m, v):
    m = ADAM_B1 * m + (1.0 - ADAM_B1) * g
    v = ADAM_B2 * v + (1.0 - ADAM_B2) * _jnp.square(g)
    m_hat = m / (1.0 - ADAM_B1 ** ADAM_STEP)
    v_hat = v / (1.0 - ADAM_B2 ** ADAM_STEP)
    delta = -ADAM_LR * (m_hat / (_jnp.sqrt(v_hat) + ADAM_EPS) + ADAM_WD * w)
    return delta, m, v


def reference(x, ln0_g, ln0_b, w_in, b_in, conv_w, w_a, w_b, w_o, b_o, ln1_g, ln1_b, w_up, b_up, ffn_conv_w, ffn_conv_b, w_down, b_down, ln2_g, ln2_b, loss_target, m_ln0_g, m_ln0_b, m_w_in, m_b_in, m_conv_w, m_w_a, m_w_b, m_w_o, m_b_o, m_ln1_g, m_ln1_b, m_w_up, m_b_up, m_ffn_conv_w, m_ffn_conv_b, m_w_down, m_b_down, m_ln2_g, m_ln2_b, v_ln0_g, v_ln0_b, v_w_in, v_b_in, v_conv_w, v_w_a, v_w_b, v_w_o, v_b_o, v_ln1_g, v_ln1_b, v_w_up, v_b_up, v_ffn_conv_w, v_ffn_conv_b, v_w_down, v_b_down, v_ln2_g, v_ln2_b):
    given = dict(x=x, ln0_g=ln0_g, ln0_b=ln0_b, w_in=w_in, b_in=b_in, conv_w=conv_w, w_a=w_a, w_b=w_b, w_o=w_o, b_o=b_o, ln1_g=ln1_g, ln1_b=ln1_b, w_up=w_up, b_up=b_up, ffn_conv_w=ffn_conv_w, ffn_conv_b=ffn_conv_b, w_down=w_down, b_down=b_down, ln2_g=ln2_g, ln2_b=ln2_b, loss_target=loss_target, m_ln0_g=m_ln0_g, m_ln0_b=m_ln0_b, m_w_in=m_w_in, m_b_in=m_b_in, m_conv_w=m_conv_w, m_w_a=m_w_a, m_w_b=m_w_b, m_w_o=m_w_o, m_b_o=m_b_o, m_ln1_g=m_ln1_g, m_ln1_b=m_ln1_b, m_w_up=m_w_up, m_b_up=m_b_up, m_ffn_conv_w=m_ffn_conv_w, m_ffn_conv_b=m_ffn_conv_b, m_w_down=m_w_down, m_b_down=m_b_down, m_ln2_g=m_ln2_g, m_ln2_b=m_ln2_b, v_ln0_g=v_ln0_g, v_ln0_b=v_ln0_b, v_w_in=v_w_in, v_b_in=v_b_in, v_conv_w=v_conv_w, v_w_a=v_w_a, v_w_b=v_w_b, v_w_o=v_w_o, v_b_o=v_b_o, v_ln1_g=v_ln1_g, v_ln1_b=v_ln1_b, v_w_up=v_w_up, v_b_up=v_b_up, v_ffn_conv_w=v_ffn_conv_w, v_ffn_conv_b=v_ffn_conv_b, v_w_down=v_w_down, v_b_down=v_b_down, v_ln2_g=v_ln2_g, v_ln2_b=v_ln2_b)
    weights = {n: given[n] for n in TWIN_WEIGHTS}
    shared = {n: given[n] for n in SHARED_INPUTS}
    per_example = {n: given[n] for n in ['x']}
    grad_fn = _jax.value_and_grad(_loss, argnums=(0, 1))

    def one_microbatch(ex, loss_target):
        ex = dict(ex)
        diff = ex.pop(TWIN_DIFF_INPUT)
        return grad_fn(weights, diff, {**shared, **ex}, loss_target)

    if N_MICROBATCH == 1:
        loss, (grad_w, grad_x) = one_microbatch(per_example, given["loss_target"])
    else:
        def body(carry, xs):
            loss_sum, grad_sum = carry
            l_k, (gw_k, gx_k) = one_microbatch(xs[0], xs[1])
            with _jax.named_scope("update"):
                return (loss_sum + l_k, _jax.tree.map(_jnp.add, grad_sum, gw_k)), gx_k

        init = (_jnp.zeros((), _jnp.float32), _jax.tree.map(_jnp.zeros_like, weights))
        (loss, grad_w), grad_x = _jax.lax.scan(body, init, (per_example, given["loss_target"]))
    with _jax.named_scope("update"):
        delta_w, new_m, new_v = {}, {}, {}
        for n in TWIN_WEIGHTS:
            delta_w[n], new_m[n], new_v[n] = _adamw(weights[n], grad_w[n], given["m_" + n], given["v_" + n])
    return (loss, grad_x, *[grad_w[n] for n in TWIN_WEIGHTS], *[delta_w[n] for n in TWIN_WEIGHTS],
            *[new_m[n] for n in TWIN_WEIGHTS], *[new_v[n] for n in TWIN_WEIGHTS])
```

```python
import functools
import math

import numpy as np
import jax
import jax.numpy as jnp
from jax import lax
from jax.experimental import pallas as pl
from jax.experimental.pallas import tpu as pltpu

F32 = jnp.float32
BF16 = jnp.bfloat16

N_DEV = 8
LN_EPS = 1e-5
ALPHA = (2.0 * 1) ** 0.25
MASK_VALUE = -1e30
HEAD_DIM = 64
GROUP_W = 512
QKV_W = 3 * GROUP_W
DILATIONS = (1, 4, 16)
RADIUS = 64
LANES = 128
HALO = 8
ATT_TQ = 128

ADAM_LR = 0.001
ADAM_B1 = 0.9
ADAM_B2 = 0.999
ADAM_EPS = 1e-08
ADAM_WD = 0.01
ADAM_STEP = 10

VMEM_LIMIT = 52 * 1024 * 1024
MESH = pl.DeviceIdType.MESH
NT_DIMS = (((1,), (1,)), ((), ()))
TN_DIMS = (((0,), (0,)), ((), ()))


def _pick(n, target, align=LANES):
    if n <= target:
        return n
    best = None
    for t in range(align, target + 1, align):
        if n % t == 0:
            best = t
    assert best is not None, (n, target, align)
    return best


def _params(sems=None):
    return pltpu.CompilerParams(dimension_semantics=sems, vmem_limit_bytes=VMEM_LIMIT)


def _alibi_slopes():
    n = 3 * 8
    return np.exp2(-8.0 * np.arange(1, n + 1, dtype=np.float64) / n).astype(np.float32).reshape(3, 8)


def _ln_stats(r):
    mu = jnp.mean(r, -1, keepdims=True)
    xc = r - mu
    var = jnp.mean(xc * xc, -1, keepdims=True)
    rstd = lax.rsqrt(var + LN_EPS)
    return xc, rstd


def ln_fwd(a, res, g, b, name):
    T, D = a.shape
    tm = _pick(T, 512, 8)
    has_res = res is not None

    def body(*refs):
        if has_res:
            a_ref, r_ref, g_ref, b_ref, h_ref, hb_ref = refs
            r = ALPHA * a_ref[...] + r_ref[...]
        else:
            a_ref, g_ref, b_ref, h_ref, hb_ref = refs
            r = a_ref[...]
        xc, rstd = _ln_stats(r)
        h = xc * rstd * g_ref[...] + b_ref[...]
        h_ref[...] = h
        hb_ref[...] = h.astype(BF16)

    row = pl.BlockSpec((tm, D), lambda i: (i, 0))
    vec = pl.BlockSpec((1, D), lambda i: (0, 0))
    ins = [a] + ([res] if has_res else []) + [g, b]
    return pl.pallas_call(
        body, name=name, grid=(T // tm,),
        in_specs=[row] * (2 if has_res else 1) + [vec, vec],
        out_specs=[row, row],
        out_shape=[jax.ShapeDtypeStruct((T, D), F32), jax.ShapeDtypeStruct((T, D), BF16)],
        compiler_params=_params(("parallel",)),
    )(*ins)


def ln_bwd(a, res, g, b, d1, d2, tgt, name):
    T, D = a.shape
    tm = _pick(T, 256, 8)
    has_res = res is not None
    loss_mode = tgt is not None

    def body(*refs):
        refs = list(refs)
        a_ref = refs.pop(0)
        r_ref = refs.pop(0) if has_res else None
        g_ref = refs.pop(0)
        b_ref = refs.pop(0)
        if loss_mode:
            t_ref = refs.pop(0)
        else:
            d1_ref = refs.pop(0)
            d2_ref = refs.pop(0)
        dr_ref, drb_ref, dg_ref, db_ref, ds_ref, loss_ref = refs
        i = pl.program_id(0)

        @pl.when(i == 0)
        def _():
            dg_ref[...] = jnp.zeros_like(dg_ref)
            db_ref[...] = jnp.zeros_like(db_ref)
            ds_ref[...] = jnp.zeros_like(ds_ref)
            loss_ref[...] = jnp.zeros_like(loss_ref)

        r = a_ref[...]
        if has_res:
            r = ALPHA * r + r_ref[...]
        xc, rstd = _ln_stats(r)
        xhat = xc * rstd
        gam = g_ref[...]
        if loss_mode:
            err = xhat * gam + b_ref[...] - t_ref[...]
            dy = err * (1.0 / D)
            row_loss = jnp.mean(err * err, -1, keepdims=True)
            loss_ref[...] += 0.5 * jnp.sum(row_loss, 0, keepdims=True)
        else:
            dy = ALPHA * d1_ref[...] + d2_ref[...]
        dyg = dy * gam
        c1 = jnp.mean(dyg, -1, keepdims=True)
        c2 = jnp.mean(dyg * xhat, -1, keepdims=True)
        dr = rstd * (dyg - c1 - xhat * c2)
        dr_ref[...] = dr
        drb_ref[...] = dr.astype(BF16)
        dg_ref[...] += jnp.sum(dy * xhat, 0, keepdims=True)
        db_ref[...] += jnp.sum(dy, 0, keepdims=True)
        ds_ref[...] += jnp.sum(dr, 0, keepdims=True)

    row = pl.BlockSpec((tm, D), lambda i: (i, 0))
    vec = pl.BlockSpec((1, D), lambda i: (0, 0))
    one = pl.BlockSpec((1, 1), lambda i: (0, 0))
    ins = [a] + ([res] if has_res else []) + [g, b] + ([tgt] if loss_mode else [d1, d2])
    in_specs = [row] * (2 if has_res else 1) + [vec, vec] + [row] * (1 if loss_mode else 2)
    return pl.pallas_call(
        body, name=name, grid=(T // tm,),
        in_specs=in_specs,
        out_specs=[row, row, vec, vec, vec, one],
        out_shape=[jax.ShapeDtypeStruct((T, D), F32), jax.ShapeDtypeStruct((T, D), BF16),
                   jax.ShapeDtypeStruct((1, D), F32), jax.ShapeDtypeStruct((1, D), F32),
                   jax.ShapeDtypeStruct((1, D), F32), jax.ShapeDtypeStruct((1, 1), F32)],
        compiler_params=_params(("arbitrary",)),
    )(*ins)


def mm_nn(a, w, bias, out_dtype, name):
    M, K = a.shape
    N = w.shape[1]
    tm = _pick(M, 1024, 8)
    tn = _pick(N, 512)

    def body(a_ref, w_ref, b_ref, o_ref):
        acc = jnp.dot(a_ref[...], w_ref[...], preferred_element_type=F32)
        o_ref[...] = (acc + b_ref[...]).astype(out_dtype)

    return pl.pallas_call(
        body, name=name, grid=(M // tm, N // tn),
        in_specs=[pl.BlockSpec((tm, K), lambda i, j: (i, 0)),
                  pl.BlockSpec((K, tn), lambda i, j: (0, j)),
                  pl.BlockSpec((1, tn), lambda i, j: (0, j))],
        out_specs=pl.BlockSpec((tm, tn), lambda i, j: (i, j)),
        out_shape=jax.ShapeDtypeStruct((M, N), out_dtype),
        compiler_params=_params(("parallel", "parallel")),
    )(a, w, bias)


def mm_nt(a, w, acc_in, name):
    M, K = a.shape
    N = w.shape[0]
    tm = _pick(M, 1024, 8)
    tn = _pick(N, 1408)
    tk = K if K <= 2048 else _pick(K, 1536)
    nk = K // tk
    has_acc = acc_in is not None

    def body(*refs):
        if has_acc:
            a_ref, w_ref, c_ref, o_ref, acc_ref = refs
        else:
            a_ref, w_ref, o_ref, acc_ref = refs
        k = pl.program_id(2)

        @pl.when(k == 0)
        def _():
            acc_ref[...] = jnp.zeros_like(acc_ref)

        acc_ref[...] += lax.dot_general(a_ref[...], w_ref[...], NT_DIMS, preferred_element_type=F32)

        @pl.when(k == nk - 1)
        def _():
            if has_acc:
                o_ref[...] = acc_ref[...] + c_ref[...]
            else:
                o_ref[...] = acc_ref[...]

    out_spec = pl.BlockSpec((tm, tn), lambda i, j, k: (i, j))
    in_specs = [pl.BlockSpec((tm, tk), lambda i, j, k: (i, k)),
                pl.BlockSpec((tn, tk), lambda i, j, k: (j, k))]
    ins = [a, w]
    if has_acc:
        in_specs.append(out_spec)
        ins.append(acc_in)
    return pl.pallas_call(
        body, name=name, grid=(M // tm, N // tn, nk),
        in_specs=in_specs, out_specs=out_spec,
        out_shape=jax.ShapeDtypeStruct((M, N), F32),
        scratch_shapes=[pltpu.VMEM((tm, tn), F32)],
        compiler_params=_params(("parallel", "parallel", "arbitrary")),
    )(*ins)


def mm_tn(a, b, name, out_dtype=BF16):
    T, M = a.shape
    N = b.shape[1]
    tm = _pick(M, 1408)
    tn = _pick(N, 2560 if tm <= 1024 else 1024)
    tk = _pick(T, 512, 8)
    nk = T // tk

    def body(a_ref, b_ref, o_ref, cs_ref, acc_ref):
        m = pl.program_id(1)
        k = pl.program_id(2)

        @pl.when(k == 0)
        def _():
            acc_ref[...] = jnp.zeros_like(acc_ref)

        @pl.when((k == 0) & (m == 0))
        def _():
            cs_ref[...] = jnp.zeros_like(cs_ref)

        bv = b_ref[...]
        acc_ref[...] += lax.dot_general(a_ref[...], bv, TN_DIMS, preferred_element_type=F32)

        @pl.when(m == 0)
        def _():
            cs_ref[...] += jnp.sum(bv.astype(F32), 0, keepdims=True)

        @pl.when(k == nk - 1)
        def _():
            o_ref[...] = acc_ref[...].astype(out_dtype)

    return pl.pallas_call(
        body, name=name, grid=(N // tn, M // tm, nk),
        in_specs=[pl.BlockSpec((tk, tm), lambda n, m, k: (k, m)),
                  pl.BlockSpec((tk, tn), lambda n, m, k: (k, n))],
        out_specs=[pl.BlockSpec((tm, tn), lambda n, m, k: (m, n)),
                   pl.BlockSpec((1, tn), lambda n, m, k: (0, n))],
        out_shape=[jax.ShapeDtypeStruct((M, N), out_dtype), jax.ShapeDtypeStruct((1, N), F32)],
        scratch_shapes=[pltpu.VMEM((tm, tn), F32)],
        compiler_params=_params(("arbitrary", "arbitrary", "arbitrary")),
    )(a, b)


def _ext_rows(prev_ref, main_ref, next_ref, i, tm, T):
    ext = jnp.concatenate([prev_ref[...], main_ref[...], next_ref[...]], axis=0)
    rows = i * tm - HALO + lax.broadcasted_iota(jnp.int32, (tm + 2 * HALO, 1), 0)
    return jnp.where((rows >= 0) & (rows < T), ext, 0.0)


def _prev_row(x):
    return pltpu.roll(x, 1, 0)


def _next_row(x):
    return pltpu.roll(x, x.shape[0] - 1, 0)


def _conv3(u, w_ref):
    return _prev_row(u) * w_ref[0:1, :] + u * w_ref[1:2, :] + _next_row(u) * w_ref[2:3, :]


def _main(x, tm):
    return x[HALO:HALO + tm]


def _halo_specs(tm, tc, T, col, order):
    r = tm // HALO
    last = T // HALO - 1
    if order == "ij":
        return (pl.BlockSpec((HALO, tc), lambda i, j: (jnp.maximum(i * r - 1, 0), col(j))),
                pl.BlockSpec((tm, tc), lambda i, j: (i, col(j))),
                pl.BlockSpec((HALO, tc), lambda i, j: (jnp.minimum((i + 1) * r, last), col(j))))
    return (pl.BlockSpec((HALO, tc), lambda j, i: (jnp.maximum(i * r - 1, 0), col(j))),
            pl.BlockSpec((tm, tc), lambda j, i: (i, col(j))),
            pl.BlockSpec((HALO, tc), lambda j, i: (jnp.minimum((i + 1) * r, last), col(j))))


def conv_a_fwd(proj_a, conv_w, name):
    T, D3 = proj_a.shape
    D = D3 // 3
    tm = _pick(T, 256, 8)

    def body(p_ref, m_ref, n_ref, w_ref, o_ref):
        i = pl.program_id(0)
        ext = _ext_rows(p_ref, m_ref, n_ref, i, tm, T)
        u = ext[:, D:2 * D] * ext[:, 2 * D:]
        cu = _conv3(u, w_ref)
        o_ref[...] = (m_ref[:, :D] * _main(cu, tm)).astype(BF16)

    prev, main, nxt = _halo_specs(tm, D3, T, lambda j: 0, "ij")
    return pl.pallas_call(
        body, name=name, grid=(T // tm, 1),
        in_specs=[prev, main, nxt, pl.BlockSpec((3, D), lambda i, j: (0, 0))],
        out_specs=pl.BlockSpec((tm, D), lambda i, j: (i, 0)),
        out_shape=jax.ShapeDtypeStruct((T, D), BF16),
        compiler_params=_params(("parallel", "arbitrary")),
    )(proj_a, proj_a, proj_a, conv_w)


def conv_a_bwd(ds_a, proj_a, conv_w, name):
    T, D3 = proj_a.shape
    D = D3 // 3
    tm = _pick(T, 256, 8)

    def body(dp_ref, dm_ref, dn_ref, p_ref, m_ref, n_ref, w_ref, o_ref, dw_ref):
        i = pl.program_id(0)

        @pl.when(i == 0)
        def _():
            dw_ref[...] = jnp.zeros_like(dw_ref)

        ext = _ext_rows(p_ref, m_ref, n_ref, i, tm, T)
        dsa = _ext_rows(dp_ref, dm_ref, dn_ref, i, tm, T)
        gb, gc, hin = ext[:, :D], ext[:, D:2 * D], ext[:, 2 * D:]
        u = gc * hin
        u_prev, u_next = _prev_row(u), _next_row(u)
        cu = u_prev * w_ref[0:1, :] + u * w_ref[1:2, :] + u_next * w_ref[2:3, :]
        dcu = dsa * gb
        du = _next_row(dcu) * w_ref[0:1, :] + dcu * w_ref[1:2, :] + _prev_row(dcu) * w_ref[2:3, :]
        o_ref[:, :D] = _main(dsa * cu, tm).astype(BF16)
        o_ref[:, D:2 * D] = _main(du * hin, tm).astype(BF16)
        o_ref[:, 2 * D:] = _main(du * gc, tm).astype(BF16)
        dcu_m = _main(dcu, tm)
        dw_ref[0:1, :] += jnp.sum(dcu_m * _main(u_prev, tm), 0, keepdims=True)
        dw_ref[1:2, :] += jnp.sum(dcu_m * _main(u, tm), 0, keepdims=True)
        dw_ref[2:3, :] += jnp.sum(dcu_m * _main(u_next, tm), 0, keepdims=True)

    dprev, dmain, dnxt = _halo_specs(tm, D, T, lambda j: 0, "ij")
    prev, main, nxt = _halo_specs(tm, D3, T, lambda j: 0, "ij")
    return pl.pallas_call(
        body, name=name, grid=(T // tm, 1),
        in_specs=[dprev, dmain, dnxt, prev, main, nxt, pl.BlockSpec((3, D), lambda i, j: (0, 0))],
        out_specs=[pl.BlockSpec((tm, D3), lambda i, j: (i, 0)), pl.BlockSpec((3, D), lambda i, j: (0, 0))],
        out_shape=[jax.ShapeDtypeStruct((T, D3), BF16), jax.ShapeDtypeStruct((3, D), F32)],
        compiler_params=_params(("arbitrary", "arbitrary")),
    )(ds_a, ds_a, ds_a, proj_a, proj_a, proj_a, conv_w)


_INV_SQRT2 = 1.0 / math.sqrt(2.0)
_INV_SQRT_2PI = 1.0 / math.sqrt(2.0 * math.pi)


def conv_f_fwd(up, fcw, fcb, name):
    T, F2 = up.shape
    F = F2 // 2
    tm = _pick(T, 512, 8)
    tc = _pick(F, 256)
    nc = F // tc

    def body(p_ref, m_ref, n_ref, g_ref, w_ref, b_ref, o_ref):
        i = pl.program_id(0)
        a = _ext_rows(p_ref, m_ref, n_ref, i, tm, T)
        ca = _main(_conv3(a, w_ref), tm) + b_ref[...]
        gl = 0.5 * ca * (1.0 + lax.erf(ca * _INV_SQRT2))
        o_ref[...] = (gl * g_ref[...]).astype(BF16)

    prev, main, nxt = _halo_specs(tm, tc, T, lambda j: j, "ij")
    return pl.pallas_call(
        body, name=name, grid=(T // tm, nc),
        in_specs=[prev, main, nxt,
                  pl.BlockSpec((tm, tc), lambda i, j: (i, nc + j)),
                  pl.BlockSpec((3, tc), lambda i, j: (0, j)),
                  pl.BlockSpec((1, tc), lambda i, j: (0, j))],
        out_specs=pl.BlockSpec((tm, tc), lambda i, j: (i, j)),
        out_shape=jax.ShapeDtypeStruct((T, F), BF16),
        compiler_params=_params(("parallel", "parallel")),
    )(up, up, up, up, fcw, fcb)


def conv_f_bwd(df, up, fcw, fcb, name):
    T, F2 = up.shape
    F = F2 // 2
    tm = _pick(T, 512, 8)
    tc = _pick(F, 256)
    nc = F // tc

    def body(fp_ref, fm_ref, fn_ref, ap_ref, am_ref, an_ref, gp_ref, gm_ref, gn_ref, w_ref, b_ref,
             da_ref, dg_ref, csa_ref, csg_ref, dfb_ref, dfw_ref):
        i = pl.program_id(1)

        @pl.when(i == 0)
        def _():
            csa_ref[...] = jnp.zeros_like(csa_ref)
            csg_ref[...] = jnp.zeros_like(csg_ref)
            dfb_ref[...] = jnp.zeros_like(dfb_ref)
            dfw_ref[...] = jnp.zeros_like(dfw_ref)

        dfe = _ext_rows(fp_ref, fm_ref, fn_ref, i, tm, T)
        a = _ext_rows(ap_ref, am_ref, an_ref, i, tm, T)
        gate = _ext_rows(gp_ref, gm_ref, gn_ref, i, tm, T)
        a_prev, a_next = _prev_row(a), _next_row(a)
        ca = a_prev * w_ref[0:1, :] + a * w_ref[1:2, :] + a_next * w_ref[2:3, :] + b_ref[...]
        cdf = 0.5 * (1.0 + lax.erf(ca * _INV_SQRT2))
        gl = ca * cdf
        gp = cdf + ca * (jnp.exp(-0.5 * ca * ca) * _INV_SQRT_2PI)
        dgate = _main(dfe * gl, tm)
        dca = dfe * gate * gp
        da = _main(_next_row(dca) * w_ref[0:1, :] + dca * w_ref[1:2, :] + _prev_row(dca) * w_ref[2:3, :], tm)
        da_ref[...] = da.astype(BF16)
        dg_ref[...] = dgate.astype(BF16)
        csa_ref[...] += jnp.sum(da, 0, keepdims=True)
        csg_ref[...] += jnp.sum(dgate, 0, keepdims=True)
        dca_m = _main(dca, tm)
        dfb_ref[...] += jnp.sum(dca_m, 0, keepdims=True)
        dfw_ref[0:1, :] += jnp.sum(dca_m * _main(a_prev, tm), 0, keepdims=True)
        dfw_ref[1:2, :] += jnp.sum(dca_m * _main(a, tm), 0, keepdims=True)
        dfw_ref[2:3, :] += jnp.sum(dca_m * _main(a_next, tm), 0, keepdims=True)

    fprev, fmain, fnxt = _halo_specs(tm, tc, T, lambda j: j, "ji")
    gprev, gmain, gnxt = _halo_specs(tm, tc, T, lambda j: nc + j, "ji")
    tile = pl.BlockSpec((tm, tc), lambda j, i: (i, j))
    vec = pl.BlockSpec((1, tc), lambda j, i: (0, j))
    vec3 = pl.BlockSpec((3, tc), lambda j, i: (0, j))
    return pl.pallas_call(
        body, name=name, grid=(nc, T // tm),
        in_specs=[fprev, fmain, fnxt, fprev, fmain, fnxt, gprev, gmain, gnxt, vec3, vec],
        out_specs=[tile, tile, vec, vec, vec, vec3],
        out_shape=[jax.ShapeDtypeStruct((T, F), BF16), jax.ShapeDtypeStruct((T, F), BF16),
                   jax.ShapeDtypeStruct((1, F), F32), jax.ShapeDtypeStruct((1, F), F32),
                   jax.ShapeDtypeStruct((1, F), F32), jax.ShapeDtypeStruct((3, F), F32)],
        compiler_params=_params(("arbitrary", "arbitrary")),
    )(df, df, df, up, up, up, up, up, up, fcw, fcb)


def gate_fwd(proj_g, y_a, y_b, name):
    T, D = y_a.shape
    tm = _pick(T, 512, 8)

    def body(g_ref, a_ref, b_ref, o_ref):
        sa = jax.nn.sigmoid(g_ref[:, :D])
        sb = jax.nn.sigmoid(g_ref[:, D:])
        o_ref[...] = (sa * a_ref[...] + sb * b_ref[...]).astype(BF16)

    row = pl.BlockSpec((tm, D), lambda i: (i, 0))
    return pl.pallas_call(
        body, name=name, grid=(T // tm,),
        in_specs=[pl.BlockSpec((tm, 2 * D), lambda i: (i, 0)), row, row],
        out_specs=row,
        out_shape=jax.ShapeDtypeStruct((T, D), BF16),
        compiler_params=_params(("parallel",)),
    )(proj_g, y_a, y_b)


def gate_bwd(dz, proj_g, y_a, y_b, name):
    T, D = y_a.shape
    tm = _pick(T, 512, 8)

    def body(dz_ref, g_ref, a_ref, b_ref, da_ref, db_ref, dg_ref):
        dzv = dz_ref[...]
        sa = jax.nn.sigmoid(g_ref[:, :D])
        sb = jax.nn.sigmoid(g_ref[:, D:])
        da_ref[...] = (dzv * sa).astype(BF16)
        db_ref[...] = (dzv * sb).astype(BF16)
        dg_ref[:, :D] = (dzv * a_ref[...] * (sa * (1.0 - sa))).astype(BF16)
        dg_ref[:, D:] = (dzv * b_ref[...] * (sb * (1.0 - sb))).astype(BF16)

    row = pl.BlockSpec((tm, D), lambda i: (i, 0))
    wide = pl.BlockSpec((tm, 2 * D), lambda i: (i, 0))
    return pl.pallas_call(
        body, name=name, grid=(T // tm,),
        in_specs=[row, wide, row, row],
        out_specs=[row, row, wide],
        out_shape=[jax.ShapeDtypeStruct((T, D), BF16), jax.ShapeDtypeStruct((T, D), BF16),
                   jax.ShapeDtypeStruct((T, 2 * D), BF16)],
        compiler_params=_params(("parallel",)),
    )(dz, proj_g, y_a, y_b)


def _att_geometry(i, L):
    win = ATT_TQ + 2 * RADIUS
    qs = i * ATT_TQ
    ks = pl.multiple_of(jnp.clip(qs - RADIUS, 0, L - win), RADIUS)
    kpos = ks + lax.broadcasted_iota(jnp.int32, (ATT_TQ, win), 1)
    qpos = qs + lax.broadcasted_iota(jnp.int32, (ATT_TQ, win), 0)
    ad = jnp.abs(kpos - qpos)
    return ks, ad, ad <= RADIUS


def _head_masks():
    lane = lax.broadcasted_iota(jnp.int32, (1, LANES), 1)
    return [lane < HEAD_DIM, lane >= HEAD_DIM]


def att_fwd(qkv, group, name):
    T = qkv.shape[0]
    d = DILATIONS[group]
    L = T // d
    win = ATT_TQ + 2 * RADIUS
    assert L % ATT_TQ == 0 and L >= win
    nq = L // ATT_TQ
    cq = 3 * QKV_W // LANES
    g4 = group * (GROUP_W // LANES)
    slopes = jnp.asarray(_alibi_slopes()[group])
    scale = HEAD_DIM ** -0.5

    def body(sl_ref, q_ref, k_ref, v_ref, o_ref, l_ref):
        hp = pl.program_id(1)
        i = pl.program_id(2)
        ks, ad, valid = _att_geometry(i, L)
        dist = (ad * d).astype(F32)
        q = q_ref[...]
        kw = k_ref[pl.ds(ks, win), :]
        vw = v_ref[pl.ds(ks, win), :]
        o_acc = jnp.zeros((ATT_TQ, LANES), F32)
        l_acc = jnp.zeros((ATT_TQ, LANES), F32)
        for h, hm in enumerate(_head_masks()):
            slope = sl_ref[hp * 2 + h]
            qm = jnp.where(hm, q, jnp.zeros_like(q))
            s = lax.dot_general(qm, kw, NT_DIMS, preferred_element_type=F32) * scale - slope * dist
            s = jnp.where(valid, s, MASK_VALUE)
            m = jnp.max(s, -1, keepdims=True)
            p = jnp.exp(s - m)
            den = jnp.sum(p, -1, keepdims=True)
            pn = (p / den).astype(BF16)
            vm = jnp.where(hm, vw, jnp.zeros_like(vw))
            o_acc = o_acc + jnp.dot(pn, vm, preferred_element_type=F32)
            l_acc = jnp.where(hm, m + jnp.log(den), l_acc)
        o_ref[...] = o_acc
        l_ref[...] = l_acc

    view = qkv.reshape(L, d * 3 * QKV_W)
    cg = GROUP_W // LANES
    out_spec = pl.BlockSpec((ATT_TQ, LANES), lambda r, hp, i: (i, r * cg + hp))
    o, lse = pl.pallas_call(
        body, name=name, grid=(d, cg, nq),
        in_specs=[pl.BlockSpec(memory_space=pltpu.SMEM),
                  pl.BlockSpec((ATT_TQ, LANES), lambda r, hp, i: (i, r * cq + g4 + hp)),
                  pl.BlockSpec((L, LANES), lambda r, hp, i: (0, r * cq + QKV_W // LANES + g4 + hp)),
                  pl.BlockSpec((L, LANES), lambda r, hp, i: (0, r * cq + 2 * QKV_W // LANES + g4 + hp))],
        out_specs=[out_spec, out_spec],
        out_shape=[jax.ShapeDtypeStruct((L, d * GROUP_W), F32)] * 2,
        compiler_params=_params(("parallel", "parallel", "arbitrary")),
    )(slopes, view, view, view)
    return o.reshape(T, GROUP_W), lse.reshape(T, GROUP_W)


def att_bwd(qkv, do, lse, dmat, prev, group, name):
    T = qkv.shape[0]
    d = DILATIONS[group]
    L = T // d
    win = ATT_TQ + 2 * RADIUS
    nq = L // ATT_TQ
    cq = 3 * QKV_W // LANES
    cg = GROUP_W // LANES
    cw = QKV_W // LANES
    g4 = group * cg
    slopes = jnp.asarray(_alibi_slopes()[group])
    scale = HEAD_DIM ** -0.5
    n_prev = 0 if prev is None else 3

    def body(*refs):
        sl_ref, q_ref, k_ref, v_ref, do_ref, l_ref, dm_ref = refs[:7]
        dq_ref, dk_ref, dv_ref, dk_acc, dv_acc = refs[7 + n_prev:]
        hp = pl.program_id(1)
        i = pl.program_id(2)

        @pl.when(i == 0)
        def _():
            dk_acc[...] = jnp.zeros_like(dk_acc)
            dv_acc[...] = jnp.zeros_like(dv_acc)

        ks, ad, valid = _att_geometry(i, L)
        dist = (ad * d).astype(F32)
        q = q_ref[...]
        dov = do_ref[...]
        lse_t = l_ref[...]
        dm_t = dm_ref[...]
        kw = k_ref[pl.ds(ks, win), :]
        vw = v_ref[pl.ds(ks, win), :]
        dq_acc = jnp.zeros((ATT_TQ, LANES), F32)
        dk_new = jnp.zeros((win, LANES), F32)
        dv_new = jnp.zeros((win, LANES), F32)
        for h, hm in enumerate(_head_masks()):
            slope = sl_ref[hp * 2 + h]
            qm = jnp.where(hm, q, jnp.zeros_like(q))
            dom = jnp.where(hm, dov, jnp.zeros_like(dov))
            km = jnp.where(hm, kw, jnp.zeros_like(kw))
            s = lax.dot_general(qm, kw, NT_DIMS, preferred_element_type=F32) * scale - slope * dist
            s = jnp.where(valid, s, MASK_VALUE)
            lse_col = jnp.max(jnp.where(hm, lse_t, -jnp.inf), -1, keepdims=True)
            dm_col = jnp.max(jnp.where(hm, dm_t, -jnp.inf), -1, keepdims=True)
            p = jnp.exp(s - lse_col)
            dp = lax.dot_general(dom, vw, NT_DIMS, preferred_element_type=F32)
            ds = (p * (dp - dm_col)).astype(BF16)
            dq_acc = dq_acc + jnp.dot(ds, km, preferred_element_type=F32)
            dk_new = dk_new + lax.dot_general(ds, qm, TN_DIMS, preferred_element_type=F32)
            dv_new = dv_new + lax.dot_general(p.astype(BF16), dom, TN_DIMS, preferred_element_type=F32)
        dq_ref[...] = (dq_acc * scale).astype(BF16)
        dk_acc[pl.ds(ks, win), :] += dk_new * scale
        dv_acc[pl.ds(ks, win), :] += dv_new

        @pl.when(i == nq - 1)
        def _():
            dk_ref[...] = dk_acc[...].astype(BF16)
            dv_ref[...] = dv_acc[...].astype(BF16)

    view = qkv.reshape(L, d * 3 * QKV_W)
    tile = pl.BlockSpec((ATT_TQ, LANES), lambda r, hp, i: (i, r * cg + hp))
    in_specs = [pl.BlockSpec(memory_space=pltpu.SMEM),
                pl.BlockSpec((ATT_TQ, LANES), lambda r, hp, i: (i, r * cq + g4 + hp)),
                pl.BlockSpec((L, LANES), lambda r, hp, i: (0, r * cq + cw + g4 + hp)),
                pl.BlockSpec((L, LANES), lambda r, hp, i: (0, r * cq + 2 * cw + g4 + hp)),
                tile, tile, tile]
    ins = [slopes, view, view, view, do.reshape(L, d * GROUP_W), lse.reshape(L, d * GROUP_W),
           dmat.reshape(L, d * GROUP_W)]
    aliases = {}
    if prev is not None:
        in_specs += [pl.BlockSpec(memory_space=pl.ANY)] * 3
        ins += [p.reshape(L, d * QKV_W) for p in prev]
        aliases = {7: 0, 8: 1, 9: 2}
    out_shape = [jax.ShapeDtypeStruct((L, d * QKV_W), BF16)] * 3
    dq, dk, dv = pl.pallas_call(
        body, name=name, grid=(d, cg, nq),
        in_specs=in_specs,
        out_specs=[pl.BlockSpec((ATT_TQ, LANES), lambda r, hp, i: (i, r * cw + g4 + hp)),
                   pl.BlockSpec((L, LANES), lambda r, hp, i: (0, r * cw + g4 + hp)),
                   pl.BlockSpec((L, LANES), lambda r, hp, i: (0, r * cw + g4 + hp))],
        out_shape=out_shape,
        scratch_shapes=[pltpu.VMEM((L, LANES), F32), pltpu.VMEM((L, LANES), F32)],
        input_output_aliases=aliases,
        compiler_params=_params(("arbitrary", "arbitrary", "arbitrary")),
    )(*ins)
    return dq.reshape(T, QKV_W), dk.reshape(T, QKV_W), dv.reshape(T, QKV_W)


def _group_weights(l_refs):
    ls = [r[...] for r in l_refs]
    m = jnp.maximum(jnp.maximum(ls[0], ls[1]), ls[2])
    es = [jnp.exp(l - m) for l in ls]
    tot = es[0] + es[1] + es[2]
    return [e / tot for e in es]


def combine_fwd(outs, lses, name):
    T = outs[0].shape[0]
    tm = _pick(T, 512, 8)

    def body(o0, o1, o2, l0, l1, l2, c_ref):
        w = _group_weights((l0, l1, l2))
        c_ref[...] = (w[0] * o0[...] + w[1] * o1[...] + w[2] * o2[...]).astype(BF16)

    row = pl.BlockSpec((tm, GROUP_W), lambda i: (i, 0))
    return pl.pallas_call(
        body, name=name, grid=(T // tm,),
        in_specs=[row] * 6, out_specs=row,
        out_shape=jax.ShapeDtypeStruct((T, GROUP_W), BF16),
        compiler_params=_params(("parallel",)),
    )(*outs, *lses)


def combine_bwd(dcomb, outs, lses, name):
    T = dcomb.shape[0]
    tm = _pick(T, 256, 8)
    head = np.arange(GROUP_W) // HEAD_DIM
    seg = jnp.asarray((head[:, None] == head[None, :]).astype(np.float32))

    def body(dc_ref, o0, o1, o2, l0, l1, l2, e_ref, do0, do1, do2, dm0, dm1, dm2):
        w = _group_weights((l0, l1, l2))
        dc = dc_ref[...]
        e = e_ref[...]
        tot = jnp.zeros_like(dc)
        for wg, o_ref in zip(w, (o0, o1, o2)):
            dw = jnp.dot(dc * o_ref[...], e, preferred_element_type=F32, precision=lax.Precision.HIGHEST)
            tot = tot + wg * dw
        for wg, do_ref, dm_ref in zip(w, (do0, do1, do2), (dm0, dm1, dm2)):
            do_ref[...] = (wg * dc).astype(BF16)
            dm_ref[...] = wg * tot

    row = pl.BlockSpec((tm, GROUP_W), lambda i: (i, 0))
    res = pl.pallas_call(
        body, name=name, grid=(T // tm,),
        in_specs=[row] * 7 + [pl.BlockSpec((GROUP_W, GROUP_W), lambda i: (0, 0))],
        out_specs=[row] * 6,
        out_shape=[jax.ShapeDtypeStruct((T, GROUP_W), BF16)] * 3 + [jax.ShapeDtypeStruct((T, GROUP_W), F32)] * 3,
        compiler_params=_params(("parallel",)),
    )(dcomb, *outs, *lses, seg)
    return res[:3], res[3:]


def _position():
    return lax.axis_index("x"), lax.axis_index("y"), lax.axis_index("c")


def _other_chips(x, y):
    return [(1 - x, y), (x, 1 - y), (1 - x, 1 - y)]


def _remote(src, dst, send_sems, recv_sems, k, to):
    return pltpu.make_async_remote_copy(src_ref=src, dst_ref=dst, send_sem=send_sems.at[k], recv_sem=recv_sems.at[k],
                                        device_id=to, device_id_type=MESH)


def all_gather(shards, name):
    n = len(shards)

    def body(*refs):
        ins, outs = refs[:n], refs[n:2 * n]
        send_sems, recv_sems, local_sems = refs[2 * n:]
        x, y, c = _position()
        sibling = (x, y, 1 - c)
        chips = _other_chips(x, y)

        def block(a, px, py, pc):
            return outs[a].at[4 * px + 2 * py + pc]

        own, first, passed = [], [], []
        for a in range(n):
            cp = pltpu.make_async_copy(ins[a], block(a, x, y, c), local_sems.at[a])
            cp.start()
            own.append(cp)
            k0 = 7 * a
            first.append(_remote(ins[a], block(a, x, y, c), send_sems, recv_sems, k0, sibling))
            for j, chip in enumerate(chips):
                first.append(_remote(ins[a], block(a, x, y, c), send_sems, recv_sems, k0 + 1 + j, (*chip, c)))
        for cp in first:
            cp.start()
        for a in range(n):
            k0 = 7 * a
            for j, chip in enumerate(chips):
                got = block(a, *chip, c)
                _remote(got, got, send_sems, recv_sems, k0 + 1 + j, sibling).wait_recv()
                fwd = _remote(got, got, send_sems, recv_sems, k0 + 4 + j, sibling)
                fwd.start()
                passed.append(fwd)
        for a in range(n):
            k0 = 7 * a
            got = block(a, x, y, 1 - c)
            _remote(got, got, send_sems, recv_sems, k0, sibling).wait_recv()
            for j, chip in enumerate(chips):
                got = block(a, *chip, 1 - c)
                _remote(got, got, send_sems, recv_sems, k0 + 4 + j, sibling).wait_recv()
        for cp in first + passed:
            cp.wait_send()
        for cp in own:
            cp.wait()

    hbm = pl.BlockSpec(memory_space=pl.ANY)
    return pl.pallas_call(
        body, name=name,
        in_specs=[hbm] * n, out_specs=[hbm] * n,
        out_shape=[jax.ShapeDtypeStruct((N_DEV,) + s.shape, s.dtype) for s in shards],
        scratch_shapes=[pltpu.SemaphoreType.DMA((7 * n,)), pltpu.SemaphoreType.DMA((7 * n,)),
                        pltpu.SemaphoreType.DMA((n,))],
    )(*shards)


def exchange_sibling(parts, name):
    n = len(parts)

    def body(*refs):
        ins, outs = refs[:n], refs[n:2 * n]
        send_sems, recv_sems = refs[2 * n:]
        x, y, c = _position()
        sibling = (x, y, 1 - c)
        copies = []
        for a in range(n):
            for q in range(4):
                cp = _remote(ins[a].at[2 * q + (1 - c)], outs[a].at[q], send_sems, recv_sems, 4 * a + q, sibling)
                cp.start()
                copies.append(cp)
        for cp in copies:
            cp.wait_recv()
        for cp in copies:
            cp.wait_send()

    hbm = pl.BlockSpec(memory_space=pl.ANY)
    return pl.pallas_call(
        body, name=name,
        in_specs=[hbm] * n, out_specs=[hbm] * n,
        out_shape=[jax.ShapeDtypeStruct((4,) + p.shape[1:], p.dtype) for p in parts],
        scratch_shapes=[pltpu.SemaphoreType.DMA((4 * n,)), pltpu.SemaphoreType.DMA((4 * n,))],
    )(*parts)


def exchange_chips(sums, name):
    n = len(sums)

    def body(*refs):
        ins, outs = refs[:n], refs[n:2 * n]
        send_sems, recv_sems = refs[2 * n:]
        x, y, c = _position()
        copies = []
        for a in range(n):
            for j, (cx, cy) in enumerate(_other_chips(x, y)):
                cp = _remote(ins[a].at[2 * cx + cy], outs[a].at[j], send_sems, recv_sems, 3 * a + j, (cx, cy, c))
                cp.start()
                copies.append(cp)
        for cp in copies:
            cp.wait_recv()
        for cp in copies:
            cp.wait_send()

    hbm = pl.BlockSpec(memory_space=pl.ANY)
    return pl.pallas_call(
        body, name=name,
        in_specs=[hbm] * n, out_specs=[hbm] * n,
        out_shape=[jax.ShapeDtypeStruct((3,) + s.shape[1:], s.dtype) for s in sums],
        scratch_shapes=[pltpu.SemaphoreType.DMA((3 * n,)), pltpu.SemaphoreType.DMA((3 * n,))],
    )(*sums)


def all_sum_small(vec, name):
    R = vec.shape[0]

    def body(v_ref, tot_ref, all_ref, send_sems, recv_sems):
        x, y, c = _position()
        me = 4 * x + 2 * y + c
        all_ref[me] = v_ref[...]
        copies = []
        for k in range(1, N_DEV):
            fx, fy, fc = (k >> 2) & 1, (k >> 1) & 1, k & 1
            to = (1 - x if fx else x, 1 - y if fy else y, 1 - c if fc else c)
            cp = _remote(v_ref, all_ref.at[me], send_sems, recv_sems, k - 1, to)
            cp.start()
            copies.append(cp)
        for cp in copies:
            cp.wait_recv()
        for cp in copies:
            cp.wait_send()
        tot = all_ref[0]
        for j in range(1, N_DEV):
            tot = tot + all_ref[j]
        tot_ref[...] = tot

    vmem = pl.BlockSpec(memory_space=pltpu.VMEM)
    return pl.pallas_call(
        body, name=name,
        in_specs=[vmem], out_specs=vmem,
        out_shape=jax.ShapeDtypeStruct((R, LANES), F32),
        scratch_shapes=[pltpu.VMEM((N_DEV, R, LANES), F32),
                        pltpu.SemaphoreType.DMA((N_DEV - 1,)), pltpu.SemaphoreType.DMA((N_DEV - 1,))],
        compiler_params=pltpu.CompilerParams(vmem_limit_bytes=VMEM_LIMIT),
    )(vec)


def pair_add(mine, theirs, name):
    _, R, C = mine.shape
    tr = _pick(R, 256, 8)

    def body(a_ref, b_ref, o_ref):
        o_ref[...] = (a_ref[...].astype(F32) + b_ref[...].astype(F32)).astype(BF16)

    blk = pl.BlockSpec((None, tr, C), lambda q, i: (q, i, 0))
    return pl.pallas_call(
        body, name=name, grid=(4, R // tr),
        in_specs=[blk, blk], out_specs=blk,
        out_shape=jax.ShapeDtypeStruct(mine.shape, BF16),
        compiler_params=_params(("parallel", "parallel")),
    )(mine, theirs)


def _adamw_math(w, g, m, v):
    m = ADAM_B1 * m + (1.0 - ADAM_B1) * g
    v = ADAM_B2 * v + (1.0 - ADAM_B2) * jnp.square(g)
    m_hat = m / (1.0 - ADAM_B1 ** ADAM_STEP)
    v_hat = v / (1.0 - ADAM_B2 ** ADAM_STEP)
    delta = -ADAM_LR * (m_hat / (jnp.sqrt(v_hat) + ADAM_EPS) + ADAM_WD * w)
    return delta, m, v


def adamw_sharded(w, m, v, own, sib, others, name):
    R, C = w.shape
    tr = _pick(R, 256, 8)

    def body(w_ref, m_ref, v_ref, a_ref, b_ref, o_ref, g_ref, d_ref, nm_ref, nv_ref):
        g = a_ref[...].astype(F32) + b_ref[...].astype(F32)
        for j in range(3):
            g = g + o_ref[j].astype(F32)
        delta, nm, nv = _adamw_math(w_ref[...], g, m_ref[...], v_ref[...])
        g_ref[...] = g
        d_ref[...] = delta
        nm_ref[...] = nm
        nv_ref[...] = nv

    row = pl.BlockSpec((tr, C), lambda i: (i, 0))
    return pl.pallas_call(
        body, name=name, grid=(R // tr,),
        in_specs=[row] * 5 + [pl.BlockSpec((3, tr, C), lambda i: (0, i, 0))],
        out_specs=[row] * 4,
        out_shape=[jax.ShapeDtypeStruct((R, C), F32)] * 4,
        compiler_params=_params(("parallel",)),
    )(w, m, v, own, sib, others)


def adamw_packed(w, g, m, v, name):
    R = w.shape[0]

    def body(w_ref, g_ref, m_ref, v_ref, d_ref, nm_ref, nv_ref):
        delta, nm, nv = _adamw_math(w_ref[...], g_ref[...], m_ref[...], v_ref[...])
        d_ref[...] = delta
        nm_ref[...] = nm
        nv_ref[...] = nv

    full = pl.BlockSpec((R, LANES), lambda i: (0, 0))
    return pl.pallas_call(
        body, name=name, grid=(1,),
        in_specs=[full] * 4, out_specs=[full] * 3,
        out_shape=[jax.ShapeDtypeStruct((R, LANES), F32)] * 3,
        compiler_params=_params(("arbitrary",)),
    )(w, g, m, v)


def _pack(arrays):
    flat = []
    sizes = []
    for a in arrays:
        f = a.reshape(-1).astype(F32)
        pad = (-f.shape[0]) % LANES
        if pad:
            f = jnp.concatenate([f, jnp.zeros((pad,), F32)])
        flat.append(f)
        sizes.append(f.shape[0])
    rows = sum(sizes) // LANES
    pad_rows = (-rows) % 8
    if pad_rows:
        flat.append(jnp.zeros((pad_rows * LANES,), F32))
    return jnp.concatenate(flat).reshape(-1, LANES), sizes


def _unpack(packed, sizes, shapes):
    flat = packed.reshape(-1)
    out = []
    off = 0
    for size, shape in zip(sizes, shapes):
        n = int(np.prod(shape))
        out.append(flat[off:off + n].reshape(shape))
        off += size
    return out


def _to_blocks(full, axis):
    if axis == 0:
        return full.reshape(N_DEV, full.shape[0] // N_DEV, full.shape[1])
    r, n = full.shape
    return full.reshape(r, N_DEV, n // N_DEV).transpose(1, 0, 2)


def _from_blocks(blocks, axis):
    if axis == 0:
        return blocks.reshape(blocks.shape[0] * blocks.shape[1], blocks.shape[2])
    return blocks.transpose(1, 0, 2).reshape(blocks.shape[1], blocks.shape[0] * blocks.shape[2])


def kernel(x, ln0_g, ln0_b, w_in, b_in, conv_w, w_a, w_b, w_o, b_o, ln1_g, ln1_b, w_up, b_up, ffn_conv_w, ffn_conv_b, w_down, b_down, ln2_g, ln2_b, loss_target, m_ln0_g, m_ln0_b, m_w_in, m_b_in, m_conv_w, m_w_a, m_w_b, m_w_o, m_b_o, m_ln1_g, m_ln1_b, m_w_up, m_b_up, m_ffn_conv_w, m_ffn_conv_b, m_w_down, m_b_down, m_ln2_g, m_ln2_b, v_ln0_g, v_ln0_b, v_w_in, v_b_in, v_conv_w, v_w_a, v_w_b, v_w_o, v_b_o, v_ln1_g, v_ln1_b, v_w_up, v_b_up, v_ffn_conv_w, v_ffn_conv_b, v_w_down, v_b_down, v_ln2_g, v_ln2_b):
    T, D = x.shape[1], x.shape[2]
    F = ffn_conv_b.shape[-1]
    xs = x.reshape(T, D)
    tgt = loss_target.reshape(T, D)
    dev = 4 * lax.axis_index("x") + 2 * lax.axis_index("y") + lax.axis_index("c")
    chip = 2 * lax.axis_index("x") + lax.axis_index("y")
    core = lax.axis_index("c")

    big = dict(w_in=(w_in[0], 1), w_a=(w_a[0], 0), w_b=(w_b[0], 1), w_o=(w_o[0], 0), w_up=(w_up[0], 1),
               w_down=(w_down[0], 0))
    names = list(big)
    gathered = all_gather([big[k][0].astype(BF16) for k in names] + [conv_w[0], ffn_conv_w[0]], "gather_weights")
    full = {k: _from_blocks(g, big[k][1]) for k, g in zip(names, gathered)}
    conv_full = _from_blocks(gathered[-2], 1)
    fcw_full = _from_blocks(gathered[-1], 1)
    o_q = 3 * D
    o_g = o_q + 3 * QKV_W
    w_pa, w_qkv, w_pg = full["w_in"][:, :o_q], full["w_in"][:, o_q:o_g], full["w_in"][:, o_g:]
    b_pa, b_qkv, b_pg = b_in[:, :o_q], b_in[:, o_q:o_g], b_in[:, o_g:]
    ln0g, ln0b = ln0_g.reshape(1, D), ln0_b.reshape(1, D)

    h0, h0b = ln_fwd(xs, None, ln0g, ln0b, "ln0_fwd")
    proj_a = mm_nn(h0b, w_pa, b_pa, F32, "proj_conv")
    qkv = mm_nn(h0b, w_qkv, b_qkv, BF16, "proj_qkv")
    proj_g = mm_nn(h0b, w_pg, b_pg, F32, "proj_gates")
    zero_d = jnp.zeros((1, D), F32)
    s_a = conv_a_fwd(proj_a, conv_full, "conv_a_fwd")
    y_a = mm_nn(s_a, full["w_a"], zero_d, F32, "branch_a_out")
    outs, lses = [], []
    for g in range(3):
        o, l = att_fwd(qkv, g, f"att_fwd_{g}")
        outs.append(o)
        lses.append(l)
    comb = combine_fwd(outs, lses, "combine_fwd")
    y_b = mm_nn(comb, full["w_b"], zero_d, F32, "branch_b_out")
    z = gate_fwd(proj_g, y_a, y_b, "gate_fwd")
    mix = mm_nn(z, full["w_o"], b_o, F32, "mix_out")
    h1, h1b = ln_fwd(h0, mix, ln1_g, ln1_b, "ln1_fwd")
    up = mm_nn(h1b, full["w_up"], b_up, F32, "ffn_up")
    f_act = conv_f_fwd(up, fcw_full, ffn_conv_b, "conv_f_fwd")
    ffn = mm_nn(f_act, full["w_down"], b_down, F32, "ffn_down")

    dr2, dr2b, d_ln2_g, d_ln2_b, d_b_down, loss_part = ln_bwd(h1, ffn, ln2_g, ln2_b, None, None, tgt, "ln2_loss_bwd")
    dw_down, _ = mm_tn(f_act, dr2b, "dw_down")
    df = mm_nt(dr2b, full["w_down"], None, "d_ffn_act")
    d_a, d_gate, cs_a, cs_gate, d_fcb, d_fcw = conv_f_bwd(df, up, fcw_full, ffn_conv_b, "conv_f_bwd")
    dw_up_a, _ = mm_tn(h1b, d_a, "dw_up_a")
    dw_up_g, _ = mm_tn(h1b, d_gate, "dw_up_gate")
    dh1 = mm_nt(d_a, full["w_up"][:, :F], None, "d_h1_a")
    dh1 = mm_nt(d_gate, full["w_up"][:, F:], dh1, "d_h1_gate")
    dr1, dr1b, d_ln1_g, d_ln1_b, d_b_o, _ = ln_bwd(h0, mix, ln1_g, ln1_b, dr2, dh1, None, "ln1_bwd")
    dw_o, _ = mm_tn(z, dr1b, "dw_o")
    dz = mm_nt(dr1b, full["w_o"], None, "d_z")
    dy_a, dy_b, dproj_g = gate_bwd(dz, proj_g, y_a, y_b, "gate_bwd")
    dw_a, _ = mm_tn(s_a, dy_a, "dw_a")
    ds_a = mm_nt(dy_a, full["w_a"], None, "d_s_a")
    dproj_a, d_conv = conv_a_bwd(ds_a, proj_a, conv_full, "conv_a_bwd")
    dw_b, _ = mm_tn(comb, dy_b, "dw_b")
    dcomb = mm_nt(dy_b, full["w_b"], None, "d_comb")
    dos, dms = combine_bwd(dcomb, outs, lses, "combine_bwd")
    dqkv = None
    for g in range(3):
        dqkv = att_bwd(qkv, dos[g], lses[g], dms[g], dqkv, g, f"att_bwd_{g}")
    pieces = [(dproj_a, w_pa), (dqkv[0], w_qkv[:, :QKV_W]), (dqkv[1], w_qkv[:, QKV_W:2 * QKV_W]),
              (dqkv[2], w_qkv[:, 2 * QKV_W:]), (dproj_g, w_pg)]
    dw_in_parts, db_in_parts = [], []
    dh0 = None
    for n, (dp, wp) in enumerate(pieces):
        dwp, cs = mm_tn(h0b, dp, f"dw_in_{n}")
        dw_in_parts.append(dwp)
        db_in_parts.append(cs)
        dh0 = mm_nt(dp, wp, dh0, f"d_h0_{n}")
    dx, _, d_ln0_g, d_ln0_b, _, _ = ln_bwd(xs, None, ln0g, ln0b, dr1, dh0, None, "ln0_bwd")

    small = [d_ln0_g, d_ln0_b, jnp.concatenate(db_in_parts, 1), d_conv, d_b_o, d_ln1_g, d_ln1_b,
             jnp.concatenate([cs_a, cs_gate], 1), d_fcw, d_fcb, d_b_down, d_ln2_g, d_ln2_b, loss_part]
    packed, sizes = _pack(small)
    total = all_sum_small(packed, "sum_small")
    (g_ln0_g, g_ln0_b, g_b_in, g_conv_full, g_b_o, g_ln1_g, g_ln1_b, g_b_up, g_fcw_full, g_fcb, g_b_down, g_ln2_g,
     g_ln2_b, loss) = _unpack(total, sizes, [a.shape for a in small])
    cw = conv_w.shape[-1]
    fw = ffn_conv_w.shape[-1]
    g_conv = lax.dynamic_slice_in_dim(g_conv_full, dev * cw, cw, 1)
    g_fcw = lax.dynamic_slice_in_dim(g_fcw_full, dev * fw, fw, 1)

    dw_full = dict(w_in=jnp.concatenate(dw_in_parts, 1), w_a=dw_a, w_b=dw_b, w_o=dw_o,
                   w_up=jnp.concatenate([dw_up_a, dw_up_g], 1), w_down=dw_down)
    parts = [_to_blocks(dw_full[k], big[k][1]) for k in names]
    from_sib = exchange_sibling(parts, "grads_to_sibling")
    mine = [lax.dynamic_index_in_dim(p.reshape((4, 2) + p.shape[1:]), core, 1, keepdims=False) for p in parts]
    chip_sums = [pair_add(a, b, f"chip_sum_{k}") for k, a, b in zip(names, mine, from_sib)]
    from_chips = exchange_chips(chip_sums, "grads_to_chips")

    moments = dict(w_in=(m_w_in, v_w_in), w_a=(m_w_a, v_w_a), w_b=(m_w_b, v_w_b), w_o=(m_w_o, v_w_o),
                   w_up=(m_w_up, v_w_up), w_down=(m_w_down, v_w_down))
    res_big = {}
    for k, a, b, o in zip(names, mine, from_sib, from_chips):
        own = lax.dynamic_index_in_dim(a, chip, 0, keepdims=False)
        sib = lax.dynamic_index_in_dim(b, chip, 0, keepdims=False)
        res_big[k] = adamw_sharded(big[k][0], moments[k][0][0], moments[k][1][0], own, sib, o, f"adamw_{k}")

    small_names = ["ln0_g", "ln0_b", "b_in", "conv_w", "b_o", "ln1_g", "ln1_b", "b_up", "ffn_conv_w", "ffn_conv_b",
                   "b_down", "ln2_g", "ln2_b"]
    small_w = [ln0_g, ln0_b, b_in, conv_w, b_o, ln1_g, ln1_b, b_up, ffn_conv_w, ffn_conv_b, b_down, ln2_g, ln2_b]
    small_m = [m_ln0_g, m_ln0_b, m_b_in, m_conv_w, m_b_o, m_ln1_g, m_ln1_b, m_b_up, m_ffn_conv_w, m_ffn_conv_b,
               m_b_down, m_ln2_g, m_ln2_b]
    small_v = [v_ln0_g, v_ln0_b, v_b_in, v_conv_w, v_b_o, v_ln1_g, v_ln1_b, v_b_up, v_ffn_conv_w, v_ffn_conv_b,
               v_b_down, v_ln2_g, v_ln2_b]
    small_g = [g_ln0_g, g_ln0_b, g_b_in, g_conv, g_b_o, g_ln1_g, g_ln1_b, g_b_up, g_fcw, g_fcb, g_b_down, g_ln2_g,
               g_ln2_b]
    shapes = [w.shape for w in small_w]
    small_g = [g.reshape(s) for g, s in zip(small_g, shapes)]
    pw, psz = _pack(small_w)
    pg, _ = _pack(small_g)
    pm, _ = _pack(small_m)
    pv, _ = _pack(small_v)
    pd, pnm, pnv = adamw_packed(pw, pg, pm, pv, "adamw_small")
    res_small = {k: (g, d_, m_, v_) for k, g, d_, m_, v_ in zip(
        small_names, small_g, _unpack(pd, psz, shapes), _unpack(pnm, psz, shapes), _unpack(pnv, psz, shapes))}

    order = ["ln0_g", "ln0_b", "w_in", "b_in", "conv_w", "w_a", "w_b", "w_o", "b_o", "ln1_g", "ln1_b", "w_up", "b_up",
             "ffn_conv_w", "ffn_conv_b", "w_down", "b_down", "ln2_g", "ln2_b"]

    def result(k, j):
        if k in res_big:
            return res_big[k][j][None]
        return res_small[k][j]

    out = [loss.reshape(()), dx.reshape(x.shape)]
    for j in range(4):
        out += [result(k, j) for k in order]
    return tuple(out)
```

```python
import functools
import math

import numpy as np
import jax
import jax.numpy as jnp
from jax import lax
from jax.experimental import pallas as pl
from jax.experimental.pallas import tpu as pltpu

F32 = jnp.float32
BF16 = jnp.bfloat16

N_DEV = 8
LN_EPS = 1e-5
ALPHA = (2.0 * 1) ** 0.25
MASK_VALUE = -1e30
HEAD_DIM = 64
GROUP_W = 512
QKV_W = 3 * GROUP_W
DILATIONS = (1, 4, 16)
RADIUS = 64
LANES = 128
HALO = 8
ATT_TQ = 128

ADAM_LR = 0.001
ADAM_B1 = 0.9
ADAM_B2 = 0.999
ADAM_EPS = 1e-08
ADAM_WD = 0.01
ADAM_STEP = 10

VMEM_LIMIT = 52 * 1024 * 1024
MESH = pl.DeviceIdType.MESH
NT_DIMS = (((1,), (1,)), ((), ()))
TN_DIMS = (((0,), (0,)), ((), ()))


def _pick(n, target, align=LANES):
    if n <= target:
        return n
    best = None
    for t in range(align, target + 1, align):
        if n % t == 0:
            best = t
    assert best is not None, (n, target, align)
    return best


def _params(sems=None):
    return pltpu.CompilerParams(dimension_semantics=sems, vmem_limit_bytes=VMEM_LIMIT)


def _alibi_slopes():
    n = 3 * 8
    return np.exp2(-8.0 * np.arange(1, n + 1, dtype=np.float64) / n).astype(np.float32).reshape(3, 8)


def _ln_stats(r):
    mu = jnp.mean(r, -1, keepdims=True)
    xc = r - mu
    var = jnp.mean(xc * xc, -1, keepdims=True)
    rstd = lax.rsqrt(var + LN_EPS)
    return xc, rstd


def _load_natural(ref, d, scr):
    if d == 1:
        return ref[0]
    n, C = ref.shape[1], ref.shape[2]
    for c in range(C // LANES):
        for r in range(d):
            scr[c, pl.ds(r, n, stride=d), :] = ref[r, :, c * LANES:(c + 1) * LANES]
    return jnp.concatenate([scr[c] for c in range(C // LANES)], axis=1)


def _store_by_residue(val, ref, d, scr):
    if d == 1:
        ref[0] = val.astype(ref.dtype)
        return
    n, C = ref.shape[1], ref.shape[2]
    for c in range(C // LANES):
        scr[c] = val[:, c * LANES:(c + 1) * LANES]
    for c in range(C // LANES):
        for r in range(d):
            ref[r, :, c * LANES:(c + 1) * LANES] = scr[c, pl.ds(r, n, stride=d), :].astype(ref.dtype)


def _residue_spec(tm, d, C):
    return pl.BlockSpec((d, tm // d, C), lambda i: (0, i, 0))


def _residue_scratch(tm, C):
    return pltpu.VMEM((C // LANES, tm, LANES), F32)


def ln_fwd(a, res, g, b, name, dilations=()):
    T, D = a.shape
    tm = _pick(T, 512, 8)
    has_res = res is not None
    nd = len(dilations)

    def body(*refs):
        refs = list(refs)
        a_ref = refs.pop(0)
        r = a_ref[...]
        if has_res:
            r = ALPHA * r + refs.pop(0)[...]
        g_ref, b_ref, h_ref, hb_ref = refs[:4]
        xc, rstd = _ln_stats(r)
        h = xc * rstd * g_ref[...] + b_ref[...]
        h_ref[...] = h
        hb_ref[...] = h.astype(BF16)
        for d, p_ref in zip(dilations, refs[4:4 + nd]):
            _store_by_residue(h, p_ref, d, refs[-1])

    row = pl.BlockSpec((tm, D), lambda i: (i, 0))
    vec = pl.BlockSpec((1, D), lambda i: (0, 0))
    ins = [a] + ([res] if has_res else []) + [g, b]
    return pl.pallas_call(
        body, name=name, grid=(T // tm,),
        in_specs=[row] * (2 if has_res else 1) + [vec, vec],
        out_specs=[row, row] + [_residue_spec(tm, d, D) for d in dilations],
        out_shape=[jax.ShapeDtypeStruct((T, D), F32), jax.ShapeDtypeStruct((T, D), BF16)]
        + [jax.ShapeDtypeStruct((d, T // d, D), BF16) for d in dilations],
        scratch_shapes=[_residue_scratch(tm, D)] if nd else [],
        compiler_params=_params(("parallel",)),
    )(*ins)


def ln_bwd(a, res, g, b, d1, d2, tgt, name, by_residue=()):
    T, D = a.shape
    tm = _pick(T, 256, 8)
    has_res = res is not None
    loss_mode = tgt is not None
    nres = len(by_residue)

    def body(*refs):
        refs = list(refs)
        a_ref = refs.pop(0)
        r_ref = refs.pop(0) if has_res else None
        g_ref = refs.pop(0)
        b_ref = refs.pop(0)
        if loss_mode:
            t_ref = refs.pop(0)
        else:
            d1_ref = refs.pop(0)
            d2_ref = refs.pop(0)
        e_refs = [refs.pop(0) for _ in range(nres)]
        dr_ref, drb_ref, dg_ref, db_ref, ds_ref, loss_ref = refs[:6]
        i = pl.program_id(0)

        @pl.when(i == 0)
        def _():
            dg_ref[...] = jnp.zeros_like(dg_ref)
            db_ref[...] = jnp.zeros_like(db_ref)
            ds_ref[...] = jnp.zeros_like(ds_ref)
            loss_ref[...] = jnp.zeros_like(loss_ref)

        r = a_ref[...]
        if has_res:
            r = ALPHA * r + r_ref[...]
        xc, rstd = _ln_stats(r)
        xhat = xc * rstd
        gam = g_ref[...]
        if loss_mode:
            err = xhat * gam + b_ref[...] - t_ref[...]
            dy = err * (1.0 / D)
            row_loss = jnp.mean(err * err, -1, keepdims=True)
            loss_ref[...] += 0.5 * jnp.sum(row_loss, 0, keepdims=True)
        else:
            dy = ALPHA * d1_ref[...] + d2_ref[...]
        for (_, d), e_ref in zip(by_residue, e_refs):
            dy = dy + _load_natural(e_ref, d, refs[-1])
        dyg = dy * gam
        c1 = jnp.mean(dyg, -1, keepdims=True)
        c2 = jnp.mean(dyg * xhat, -1, keepdims=True)
        dr = rstd * (dyg - c1 - xhat * c2)
        dr_ref[...] = dr
        drb_ref[...] = dr.astype(BF16)
        dg_ref[...] += jnp.sum(dy * xhat, 0, keepdims=True)
        db_ref[...] += jnp.sum(dy, 0, keepdims=True)
        ds_ref[...] += jnp.sum(dr, 0, keepdims=True)

    row = pl.BlockSpec((tm, D), lambda i: (i, 0))
    vec = pl.BlockSpec((1, D), lambda i: (0, 0))
    one = pl.BlockSpec((1, 1), lambda i: (0, 0))
    ins = [a] + ([res] if has_res else []) + [g, b] + ([tgt] if loss_mode else [d1, d2]) + [e for e, _ in by_residue]
    in_specs = [row] * (2 if has_res else 1) + [vec, vec] + [row] * (1 if loss_mode else 2)
    in_specs += [_residue_spec(tm, d, D) for _, d in by_residue]
    return pl.pallas_call(
        body, name=name, grid=(T // tm,),
        in_specs=in_specs,
        out_specs=[row, row, vec, vec, vec, one],
        out_shape=[jax.ShapeDtypeStruct((T, D), F32), jax.ShapeDtypeStruct((T, D), BF16),
                   jax.ShapeDtypeStruct((1, D), F32), jax.ShapeDtypeStruct((1, D), F32),
                   jax.ShapeDtypeStruct((1, D), F32), jax.ShapeDtypeStruct((1, 1), F32)],
        scratch_shapes=[_residue_scratch(tm, D)] if nres else [],
        compiler_params=_params(("arbitrary",)),
    )(*ins)


def mm_nn(a, w, bias, out_dtype, name):
    M, K = a.shape
    N = w.shape[1]
    tm = _pick(M, 1024, 8)
    tn = _pick(N, 512)

    def body(a_ref, w_ref, b_ref, o_ref):
        acc = jnp.dot(a_ref[...], w_ref[...], preferred_element_type=F32)
        o_ref[...] = (acc + b_ref[...]).astype(out_dtype)

    return pl.pallas_call(
        body, name=name, grid=(M // tm, N // tn),
        in_specs=[pl.BlockSpec((tm, K), lambda i, j: (i, 0)),
                  pl.BlockSpec((K, tn), lambda i, j: (0, j)),
                  pl.BlockSpec((1, tn), lambda i, j: (0, j))],
        out_specs=pl.BlockSpec((tm, tn), lambda i, j: (i, j)),
        out_shape=jax.ShapeDtypeStruct((M, N), out_dtype),
        compiler_params=_params(("parallel", "parallel")),
    )(a, w, bias)


def mm_nt(a, w, acc_in, name):
    M, K = a.shape
    N = w.shape[0]
    tm = _pick(M, 1024, 8)
    tn = _pick(N, 1408)
    tk = K if K <= 2048 else _pick(K, 1536)
    nk = K // tk
    has_acc = acc_in is not None

    def body(*refs):
        if has_acc:
            a_ref, w_ref, c_ref, o_ref, acc_ref = refs
        else:
            a_ref, w_ref, o_ref, acc_ref = refs
        k = pl.program_id(2)

        @pl.when(k == 0)
        def _():
            acc_ref[...] = jnp.zeros_like(acc_ref)

        acc_ref[...] += lax.dot_general(a_ref[...], w_ref[...], NT_DIMS, preferred_element_type=F32)

        @pl.when(k == nk - 1)
        def _():
            if has_acc:
                o_ref[...] = acc_ref[...] + c_ref[...]
            else:
                o_ref[...] = acc_ref[...]

    out_spec = pl.BlockSpec((tm, tn), lambda i, j, k: (i, j))
    in_specs = [pl.BlockSpec((tm, tk), lambda i, j, k: (i, k)),
                pl.BlockSpec((tn, tk), lambda i, j, k: (j, k))]
    ins = [a, w]
    if has_acc:
        in_specs.append(out_spec)
        ins.append(acc_in)
    return pl.pallas_call(
        body, name=name, grid=(M // tm, N // tn, nk),
        in_specs=in_specs, out_specs=out_spec,
        out_shape=jax.ShapeDtypeStruct((M, N), F32),
        scratch_shapes=[pltpu.VMEM((tm, tn), F32)],
        compiler_params=_params(("parallel", "parallel", "arbitrary")),
    )(*ins)


def mm_tn(a, b, name, out_dtype=BF16):
    T, M = a.shape
    N = b.shape[1]
    tm = _pick(M, 1408)
    tn = _pick(N, 2560 if tm <= 1024 else 1024)
    tk = _pick(T, 512, 8)
    nk = T // tk

    def body(a_ref, b_ref, o_ref, cs_ref, acc_ref):
        m = pl.program_id(1)
        k = pl.program_id(2)

        @pl.when(k == 0)
        def _():
            acc_ref[...] = jnp.zeros_like(acc_ref)

        @pl.when((k == 0) & (m == 0))
        def _():
            cs_ref[...] = jnp.zeros_like(cs_ref)

        bv = b_ref[...]
        acc_ref[...] += lax.dot_general(a_ref[...], bv, TN_DIMS, preferred_element_type=F32)

        @pl.when(m == 0)
        def _():
            cs_ref[...] += jnp.sum(bv.astype(F32), 0, keepdims=True)

        @pl.when(k == nk - 1)
        def _():
            o_ref[...] = acc_ref[...].astype(out_dtype)

    return pl.pallas_call(
        body, name=name, grid=(N // tn, M // tm, nk),
        in_specs=[pl.BlockSpec((tk, tm), lambda n, m, k: (k, m)),
                  pl.BlockSpec((tk, tn), lambda n, m, k: (k, n))],
        out_specs=[pl.BlockSpec((tm, tn), lambda n, m, k: (m, n)),
                   pl.BlockSpec((1, tn), lambda n, m, k: (0, n))],
        out_shape=[jax.ShapeDtypeStruct((M, N), out_dtype), jax.ShapeDtypeStruct((1, N), F32)],
        scratch_shapes=[pltpu.VMEM((tm, tn), F32)],
        compiler_params=_params(("arbitrary", "arbitrary", "arbitrary")),
    )(a, b)


def _ext_rows(prev_ref, main_ref, next_ref, i, tm, T):
    ext = jnp.concatenate([prev_ref[...], main_ref[...], next_ref[...]], axis=0)
    rows = i * tm - HALO + lax.broadcasted_iota(jnp.int32, (tm + 2 * HALO, 1), 0)
    return jnp.where((rows >= 0) & (rows < T), ext, 0.0)


def _prev_row(x):
    return pltpu.roll(x, 1, 0)


def _next_row(x):
    return pltpu.roll(x, x.shape[0] - 1, 0)


def _conv3(u, w_ref):
    return _prev_row(u) * w_ref[0:1, :] + u * w_ref[1:2, :] + _next_row(u) * w_ref[2:3, :]


def _main(x, tm):
    return x[HALO:HALO + tm]


def _halo_specs(tm, tc, T, col, order):
    r = tm // HALO
    last = T // HALO - 1
    if order == "ij":
        return (pl.BlockSpec((HALO, tc), lambda i, j: (jnp.maximum(i * r - 1, 0), col(j))),
                pl.BlockSpec((tm, tc), lambda i, j: (i, col(j))),
                pl.BlockSpec((HALO, tc), lambda i, j: (jnp.minimum((i + 1) * r, last), col(j))))
    return (pl.BlockSpec((HALO, tc), lambda j, i: (jnp.maximum(i * r - 1, 0), col(j))),
            pl.BlockSpec((tm, tc), lambda j, i: (i, col(j))),
            pl.BlockSpec((HALO, tc), lambda j, i: (jnp.minimum((i + 1) * r, last), col(j))))


def conv_a_fwd(proj_a, conv_w, name):
    T, D3 = proj_a.shape
    D = D3 // 3
    tm = _pick(T, 256, 8)

    def body(p_ref, m_ref, n_ref, w_ref, o_ref):
        i = pl.program_id(0)
        ext = _ext_rows(p_ref, m_ref, n_ref, i, tm, T)
        u = ext[:, D:2 * D] * ext[:, 2 * D:]
        cu = _conv3(u, w_ref)
        o_ref[...] = (m_ref[:, :D] * _main(cu, tm)).astype(BF16)

    prev, main, nxt = _halo_specs(tm, D3, T, lambda j: 0, "ij")
    return pl.pallas_call(
        body, name=name, grid=(T // tm, 1),
        in_specs=[prev, main, nxt, pl.BlockSpec((3, D), lambda i, j: (0, 0))],
        out_specs=pl.BlockSpec((tm, D), lambda i, j: (i, 0)),
        out_shape=jax.ShapeDtypeStruct((T, D), BF16),
        compiler_params=_params(("parallel", "arbitrary")),
    )(proj_a, proj_a, proj_a, conv_w)


def conv_a_bwd(ds_a, proj_a, conv_w, name):
    T, D3 = proj_a.shape
    D = D3 // 3
    tm = _pick(T, 256, 8)

    def body(dp_ref, dm_ref, dn_ref, p_ref, m_ref, n_ref, w_ref, o_ref, dw_ref):
        i = pl.program_id(0)

        @pl.when(i == 0)
        def _():
            dw_ref[...] = jnp.zeros_like(dw_ref)

        ext = _ext_rows(p_ref, m_ref, n_ref, i, tm, T)
        dsa = _ext_rows(dp_ref, dm_ref, dn_ref, i, tm, T)
        gb, gc, hin = ext[:, :D], ext[:, D:2 * D], ext[:, 2 * D:]
        u = gc * hin
        u_prev, u_next = _prev_row(u), _next_row(u)
        cu = u_prev * w_ref[0:1, :] + u * w_ref[1:2, :] + u_next * w_ref[2:3, :]
        dcu = dsa * gb
        du = _next_row(dcu) * w_ref[0:1, :] + dcu * w_ref[1:2, :] + _prev_row(dcu) * w_ref[2:3, :]
        o_ref[:, :D] = _main(dsa * cu, tm).astype(BF16)
        o_ref[:, D:2 * D] = _main(du * hin, tm).astype(BF16)
        o_ref[:, 2 * D:] = _main(du * gc, tm).astype(BF16)
        dcu_m = _main(dcu, tm)
        dw_ref[0:1, :] += jnp.sum(dcu_m * _main(u_prev, tm), 0, keepdims=True)
        dw_ref[1:2, :] += jnp.sum(dcu_m * _main(u, tm), 0, keepdims=True)
        dw_ref[2:3, :] += jnp.sum(dcu_m * _main(u_next, tm), 0, keepdims=True)

    dprev, dmain, dnxt = _halo_specs(tm, D, T, lambda j: 0, "ij")
    prev, main, nxt = _halo_specs(tm, D3, T, lambda j: 0, "ij")
    return pl.pallas_call(
        body, name=name, grid=(T // tm, 1),
        in_specs=[dprev, dmain, dnxt, prev, main, nxt, pl.BlockSpec((3, D), lambda i, j: (0, 0))],
        out_specs=[pl.BlockSpec((tm, D3), lambda i, j: (i, 0)), pl.BlockSpec((3, D), lambda i, j: (0, 0))],
        out_shape=[jax.ShapeDtypeStruct((T, D3), BF16), jax.ShapeDtypeStruct((3, D), F32)],
        compiler_params=_params(("arbitrary", "arbitrary")),
    )(ds_a, ds_a, ds_a, proj_a, proj_a, proj_a, conv_w)


_INV_SQRT2 = 1.0 / math.sqrt(2.0)
_INV_SQRT_2PI = 1.0 / math.sqrt(2.0 * math.pi)


def conv_f_fwd(up, fcw, fcb, name):
    T, F2 = up.shape
    F = F2 // 2
    tm = _pick(T, 512, 8)
    tc = _pick(F, 256)
    nc = F // tc

    def body(p_ref, m_ref, n_ref, g_ref, w_ref, b_ref, o_ref):
        i = pl.program_id(0)
        a = _ext_rows(p_ref, m_ref, n_ref, i, tm, T)
        ca = _main(_conv3(a, w_ref), tm) + b_ref[...]
        gl = 0.5 * ca * (1.0 + lax.erf(ca * _INV_SQRT2))
        o_ref[...] = (gl * g_ref[...]).astype(BF16)

    prev, main, nxt = _halo_specs(tm, tc, T, lambda j: j, "ij")
    return pl.pallas_call(
        body, name=name, grid=(T // tm, nc),
        in_specs=[prev, main, nxt,
                  pl.BlockSpec((tm, tc), lambda i, j: (i, nc + j)),
                  pl.BlockSpec((3, tc), lambda i, j: (0, j)),
                  pl.BlockSpec((1, tc), lambda i, j: (0, j))],
        out_specs=pl.BlockSpec((tm, tc), lambda i, j: (i, j)),
        out_shape=jax.ShapeDtypeStruct((T, F), BF16),
        compiler_params=_params(("parallel", "parallel")),
    )(up, up, up, up, fcw, fcb)


def conv_f_bwd(df, up, fcw, fcb, name):
    T, F2 = up.shape
    F = F2 // 2
    tm = _pick(T, 512, 8)
    tc = _pick(F, 256)
    nc = F // tc

    def body(fp_ref, fm_ref, fn_ref, ap_ref, am_ref, an_ref, gp_ref, gm_ref, gn_ref, w_ref, b_ref,
             da_ref, dg_ref, csa_ref, csg_ref, dfb_ref, dfw_ref):
        i = pl.program_id(1)

        @pl.when(i == 0)
        def _():
            csa_ref[...] = jnp.zeros_like(csa_ref)
            csg_ref[...] = jnp.zeros_like(csg_ref)
            dfb_ref[...] = jnp.zeros_like(dfb_ref)
            dfw_ref[...] = jnp.zeros_like(dfw_ref)

        dfe = _ext_rows(fp_ref, fm_ref, fn_ref, i, tm, T)
        a = _ext_rows(ap_ref, am_ref, an_ref, i, tm, T)
        gate = _ext_rows(gp_ref, gm_ref, gn_ref, i, tm, T)
        a_prev, a_next = _prev_row(a), _next_row(a)
        ca = a_prev * w_ref[0:1, :] + a * w_ref[1:2, :] + a_next * w_ref[2:3, :] + b_ref[...]
        cdf = 0.5 * (1.0 + lax.erf(ca * _INV_SQRT2))
        gl = ca * cdf
        gp = cdf + ca * (jnp.exp(-0.5 * ca * ca) * _INV_SQRT_2PI)
        dgate = _main(dfe * gl, tm)
        dca = dfe * gate * gp
        da = _main(_next_row(dca) * w_ref[0:1, :] + dca * w_ref[1:2, :] + _prev_row(dca) * w_ref[2:3, :], tm)
        da_ref[...] = da.astype(BF16)
        dg_ref[...] = dgate.astype(BF16)
        csa_ref[...] += jnp.sum(da, 0, keepdims=True)
        csg_ref[...] += jnp.sum(dgate, 0, keepdims=True)
        dca_m = _main(dca, tm)
        dfb_ref[...] += jnp.sum(dca_m, 0, keepdims=True)
        dfw_ref[0:1, :] += jnp.sum(dca_m * _main(a_prev, tm), 0, keepdims=True)
        dfw_ref[1:2, :] += jnp.sum(dca_m * _main(a, tm), 0, keepdims=True)
        dfw_ref[2:3, :] += jnp.sum(dca_m * _main(a_next, tm), 0, keepdims=True)

    fprev, fmain, fnxt = _halo_specs(tm, tc, T, lambda j: j, "ji")
    gprev, gmain, gnxt = _halo_specs(tm, tc, T, lambda j: nc + j, "ji")
    tile = pl.BlockSpec((tm, tc), lambda j, i: (i, j))
    vec = pl.BlockSpec((1, tc), lambda j, i: (0, j))
    vec3 = pl.BlockSpec((3, tc), lambda j, i: (0, j))
    return pl.pallas_call(
        body, name=name, grid=(nc, T // tm),
        in_specs=[fprev, fmain, fnxt, fprev, fmain, fnxt, gprev, gmain, gnxt, vec3, vec],
        out_specs=[tile, tile, vec, vec, vec, vec3],
        out_shape=[jax.ShapeDtypeStruct((T, F), BF16), jax.ShapeDtypeStruct((T, F), BF16),
                   jax.ShapeDtypeStruct((1, F), F32), jax.ShapeDtypeStruct((1, F), F32),
                   jax.ShapeDtypeStruct((1, F), F32), jax.ShapeDtypeStruct((3, F), F32)],
        compiler_params=_params(("arbitrary", "arbitrary")),
    )(df, df, df, up, up, up, up, up, up, fcw, fcb)


def gate_fwd(proj_g, y_a, y_b, name):
    T, D = y_a.shape
    tm = _pick(T, 512, 8)

    def body(g_ref, a_ref, b_ref, o_ref):
        sa = jax.nn.sigmoid(g_ref[:, :D])
        sb = jax.nn.sigmoid(g_ref[:, D:])
        o_ref[...] = (sa * a_ref[...] + sb * b_ref[...]).astype(BF16)

    row = pl.BlockSpec((tm, D), lambda i: (i, 0))
    return pl.pallas_call(
        body, name=name, grid=(T // tm,),
        in_specs=[pl.BlockSpec((tm, 2 * D), lambda i: (i, 0)), row, row],
        out_specs=row,
        out_shape=jax.ShapeDtypeStruct((T, D), BF16),
        compiler_params=_params(("parallel",)),
    )(proj_g, y_a, y_b)


def gate_bwd(dz, proj_g, y_a, y_b, name):
    T, D = y_a.shape
    tm = _pick(T, 512, 8)

    def body(dz_ref, g_ref, a_ref, b_ref, da_ref, db_ref, dg_ref):
        dzv = dz_ref[...]
        sa = jax.nn.sigmoid(g_ref[:, :D])
        sb = jax.nn.sigmoid(g_ref[:, D:])
        da_ref[...] = (dzv * sa).astype(BF16)
        db_ref[...] = (dzv * sb).astype(BF16)
        dg_ref[:, :D] = (dzv * a_ref[...] * (sa * (1.0 - sa))).astype(BF16)
        dg_ref[:, D:] = (dzv * b_ref[...] * (sb * (1.0 - sb))).astype(BF16)

    row = pl.BlockSpec((tm, D), lambda i: (i, 0))
    wide = pl.BlockSpec((tm, 2 * D), lambda i: (i, 0))
    return pl.pallas_call(
        body, name=name, grid=(T // tm,),
        in_specs=[row, wide, row, row],
        out_specs=[row, row, wide],
        out_shape=[jax.ShapeDtypeStruct((T, D), BF16), jax.ShapeDtypeStruct((T, D), BF16),
                   jax.ShapeDtypeStruct((T, 2 * D), BF16)],
        compiler_params=_params(("parallel",)),
    )(dz, proj_g, y_a, y_b)


ATT_WIN = ATT_TQ + 2 * RADIUS
ATT_STEP = 512
FAR = 1e32


def _att_window(qs, L, d):
    ks = pl.multiple_of(jnp.clip(qs - RADIUS, 0, L - ATT_WIN), RADIUS)
    col_row = (lax.broadcasted_iota(jnp.int32, (ATT_TQ, ATT_WIN), 1)
               - lax.broadcasted_iota(jnp.int32, (ATT_TQ, ATT_WIN), 0))
    ad = jnp.abs(col_row + (ks - qs))
    return ks, jnp.where(ad <= RADIUS, (ad * d).astype(F32), FAR)


def _head_masks():
    lane = lax.broadcasted_iota(jnp.int32, (1, LANES), 1)
    return [lane < HEAD_DIM, lane >= HEAD_DIM]


def _att_step(L):
    step = min(ATT_STEP, L)
    assert L % step == 0 and step % ATT_TQ == 0 and L >= ATT_WIN
    return step


def att_fwd(qkv, group, name):
    d, L, _ = qkv.shape
    step = _att_step(L)
    cg = GROUP_W // LANES
    slopes = jnp.asarray(_alibi_slopes()[group])
    scale = HEAD_DIM ** -0.5

    def body(sl_ref, q_ref, k_ref, v_ref, o_ref, l_ref):
        hp = pl.program_id(1)
        i = pl.program_id(2)
        masks = _head_masks()
        for t in range(step // ATT_TQ):
            rows = slice(t * ATT_TQ, (t + 1) * ATT_TQ)
            ks, dist = _att_window(i * step + t * ATT_TQ, L, d)
            q = q_ref[rows, :] * scale
            kw = k_ref[pl.ds(ks, ATT_WIN), :]
            vw = v_ref[pl.ds(ks, ATT_WIN), :]
            o_acc = jnp.zeros((ATT_TQ, LANES), F32)
            l_acc = jnp.zeros((ATT_TQ, LANES), F32)
            for h, hm in enumerate(masks):
                slope = sl_ref[hp * 2 + h]
                qm = jnp.where(hm, q, jnp.zeros_like(q))
                s = lax.dot_general(qm, kw, NT_DIMS, preferred_element_type=F32) - slope * dist
                m = jnp.max(s, -1, keepdims=True)
                p = jnp.exp(s - m)
                den = jnp.sum(p, -1, keepdims=True)
                pn = (p / den).astype(BF16)
                vm = jnp.where(hm, vw, jnp.zeros_like(vw))
                o_acc = o_acc + jnp.dot(pn, vm, preferred_element_type=F32)
                l_acc = jnp.where(hm, m + jnp.log(den), l_acc)
            o_ref[rows, :] = o_acc
            l_ref[rows, :] = l_acc

    out_spec = pl.BlockSpec((None, step, LANES), lambda r, hp, i: (r, i, hp))
    return pl.pallas_call(
        body, name=name, grid=(d, cg, L // step),
        in_specs=[pl.BlockSpec(memory_space=pltpu.SMEM),
                  pl.BlockSpec((None, step, LANES), lambda r, hp, i: (r, i, hp)),
                  pl.BlockSpec((None, L, LANES), lambda r, hp, i: (r, 0, cg + hp)),
                  pl.BlockSpec((None, L, LANES), lambda r, hp, i: (r, 0, 2 * cg + hp))],
        out_specs=[out_spec, out_spec],
        out_shape=[jax.ShapeDtypeStruct((d, L, GROUP_W), F32)] * 2,
        compiler_params=_params(("parallel", "parallel", "arbitrary")),
    )(slopes, qkv, qkv, qkv)


def att_bwd(qkv, do, lse, dmat, group, name):
    d, L, _ = qkv.shape
    step = _att_step(L)
    nq = L // step
    cg = GROUP_W // LANES
    slopes = jnp.asarray(_alibi_slopes()[group])
    scale = HEAD_DIM ** -0.5

    def body(sl_ref, q_ref, k_ref, v_ref, do_ref, l_ref, dm_ref, dq_ref, dk_ref, dv_ref, dk_acc, dv_acc):
        hp = pl.program_id(1)
        i = pl.program_id(2)

        @pl.when(i == 0)
        def _():
            dk_acc[...] = jnp.zeros_like(dk_acc)
            dv_acc[...] = jnp.zeros_like(dv_acc)

        masks = _head_masks()
        for t in range(step // ATT_TQ):
            rows = slice(t * ATT_TQ, (t + 1) * ATT_TQ)
            ks, dist = _att_window(i * step + t * ATT_TQ, L, d)
            q = q_ref[rows, :] * scale
            dov = do_ref[rows, :]
            lse_t = l_ref[rows, :]
            dm_t = dm_ref[rows, :]
            kw = k_ref[pl.ds(ks, ATT_WIN), :]
            vw = v_ref[pl.ds(ks, ATT_WIN), :]
            dq_acc = jnp.zeros((ATT_TQ, LANES), F32)
            dk_new = jnp.zeros((ATT_WIN, LANES), F32)
            dv_new = jnp.zeros((ATT_WIN, LANES), F32)
            for h, hm in enumerate(masks):
                slope = sl_ref[hp * 2 + h]
                qm = jnp.where(hm, q, jnp.zeros_like(q))
                dom = jnp.where(hm, dov, jnp.zeros_like(dov))
                km = jnp.where(hm, kw, jnp.zeros_like(kw))
                s = lax.dot_general(qm, kw, NT_DIMS, preferred_element_type=F32) - slope * dist
                lse_col = jnp.max(jnp.where(hm, lse_t, -jnp.inf), -1, keepdims=True)
                dm_col = jnp.max(jnp.where(hm, dm_t, -jnp.inf), -1, keepdims=True)
                p = jnp.exp(s - lse_col)
                dp = lax.dot_general(dom, vw, NT_DIMS, preferred_element_type=F32)
                ds = (p * (dp - dm_col)).astype(BF16)
                dq_acc = dq_acc + jnp.dot(ds, km, preferred_element_type=F32)
                dk_new = dk_new + lax.dot_general(ds, qm, TN_DIMS, preferred_element_type=F32)
                dv_new = dv_new + lax.dot_general(p.astype(BF16), dom, TN_DIMS, preferred_element_type=F32)
            dq_ref[rows, :] = (dq_acc * scale).astype(BF16)
            dk_acc[pl.ds(ks, ATT_WIN), :] += dk_new
            dv_acc[pl.ds(ks, ATT_WIN), :] += dv_new

        @pl.when(i == nq - 1)
        def _():
            dk_ref[...] = dk_acc[...].astype(BF16)
            dv_ref[...] = dv_acc[...].astype(BF16)

    tile = pl.BlockSpec((None, step, LANES), lambda r, hp, i: (r, i, hp))
    whole = pl.BlockSpec((None, L, LANES), lambda r, hp, i: (r, 0, hp))
    return pl.pallas_call(
        body, name=name, grid=(d, cg, nq),
        in_specs=[pl.BlockSpec(memory_space=pltpu.SMEM), tile,
                  pl.BlockSpec((None, L, LANES), lambda r, hp, i: (r, 0, cg + hp)),
                  pl.BlockSpec((None, L, LANES), lambda r, hp, i: (r, 0, 2 * cg + hp)),
                  tile, tile, tile],
        out_specs=[tile, whole, whole],
        out_shape=[jax.ShapeDtypeStruct((d, L, GROUP_W), BF16)] * 3,
        scratch_shapes=[pltpu.VMEM((L, LANES), F32), pltpu.VMEM((L, LANES), F32)],
        compiler_params=_params(("arbitrary", "arbitrary", "arbitrary")),
    )(slopes, qkv, qkv, qkv, do, lse, dmat)


def _group_weights(ls):
    m = jnp.maximum(jnp.maximum(ls[0], ls[1]), ls[2])
    es = [jnp.exp(l - m) for l in ls]
    tot = es[0] + es[1] + es[2]
    return [e / tot for e in es]


def combine_fwd(outs, lses, name):
    T = outs[0].shape[0] * outs[0].shape[1]
    tm = _pick(T, 512, 8)
    n_scr = 2 * (len(DILATIONS) - 1)

    def body(*refs):
        o_refs, l_refs, c_ref, scr = refs[:3], refs[3:6], refs[6], refs[7:]
        o = [_load_natural(o_refs[g], d, scr[g - 1] if g else None) for g, d in enumerate(DILATIONS)]
        l = [_load_natural(l_refs[g], d, scr[g + 1] if g else None) for g, d in enumerate(DILATIONS)]
        w = _group_weights(l)
        c_ref[...] = (w[0] * o[0] + w[1] * o[1] + w[2] * o[2]).astype(BF16)

    specs = [_residue_spec(tm, d, GROUP_W) for d in DILATIONS]
    return pl.pallas_call(
        body, name=name, grid=(T // tm,),
        in_specs=specs + specs, out_specs=pl.BlockSpec((tm, GROUP_W), lambda i: (i, 0)),
        out_shape=jax.ShapeDtypeStruct((T, GROUP_W), BF16),
        scratch_shapes=[_residue_scratch(tm, GROUP_W)] * n_scr,
        compiler_params=_params(("parallel",)),
    )(*outs, *lses)


def combine_bwd(dcomb, outs, lses, name):
    T = dcomb.shape[0]
    tm = _pick(T, 256, 8)
    head = np.arange(GROUP_W) // HEAD_DIM
    seg = jnp.asarray((head[:, None] == head[None, :]).astype(np.float32))
    ng = len(DILATIONS)
    n_scr = 4 * (ng - 1)

    def body(*refs):
        dc_ref, o_refs, l_refs, e_ref = refs[0], refs[1:1 + ng], refs[1 + ng:1 + 2 * ng], refs[1 + 2 * ng]
        do_refs, dm_refs = refs[2 + 2 * ng:2 + 3 * ng], refs[2 + 3 * ng:2 + 4 * ng]
        scr = refs[2 + 4 * ng:]
        o = [_load_natural(o_refs[g], d, scr[4 * (g - 1)] if g else None) for g, d in enumerate(DILATIONS)]
        l = [_load_natural(l_refs[g], d, scr[4 * (g - 1) + 1] if g else None) for g, d in enumerate(DILATIONS)]
        w = _group_weights(l)
        dc = dc_ref[...]
        e = e_ref[...]
        tot = jnp.zeros_like(dc)
        for g in range(ng):
            dw = jnp.dot(dc * o[g], e, preferred_element_type=F32, precision=lax.Precision.HIGHEST)
            tot = tot + w[g] * dw
        for g, d in enumerate(DILATIONS):
            _store_by_residue(w[g] * dc, do_refs[g], d, scr[4 * (g - 1) + 2] if g else None)
            _store_by_residue(w[g] * tot, dm_refs[g], d, scr[4 * (g - 1) + 3] if g else None)

    specs = [_residue_spec(tm, d, GROUP_W) for d in DILATIONS]
    res = pl.pallas_call(
        body, name=name, grid=(T // tm,),
        in_specs=[pl.BlockSpec((tm, GROUP_W), lambda i: (i, 0))] + specs + specs
        + [pl.BlockSpec((GROUP_W, GROUP_W), lambda i: (0, 0))],
        out_specs=specs + specs,
        out_shape=[jax.ShapeDtypeStruct(o.shape, BF16) for o in outs] + [jax.ShapeDtypeStruct(o.shape, F32) for o in outs],
        scratch_shapes=[_residue_scratch(tm, GROUP_W)] * n_scr,
        compiler_params=_params(("parallel",)),
    )(dcomb, *outs, *lses, seg)
    return res[:ng], res[ng:]


def _position():
    return lax.axis_index("x"), lax.axis_index("y"), lax.axis_index("c")


def _other_chips(x, y):
    return [(1 - x, y), (x, 1 - y), (1 - x, 1 - y)]


def _remote(src, dst, send_sems, recv_sems, k, to):
    return pltpu.make_async_remote_copy(src_ref=src, dst_ref=dst, send_sem=send_sems.at[k], recv_sem=recv_sems.at[k],
                                        device_id=to, device_id_type=MESH)


def all_gather(shards, name):
    n = len(shards)

    def body(*refs):
        ins, outs = refs[:n], refs[n:2 * n]
        send_sems, recv_sems, local_sems = refs[2 * n:]
        x, y, c = _position()
        sibling = (x, y, 1 - c)
        chips = _other_chips(x, y)

        def block(a, px, py, pc):
            return outs[a].at[4 * px + 2 * py + pc]

        own, first, passed = [], [], []
        for a in range(n):
            cp = pltpu.make_async_copy(ins[a], block(a, x, y, c), local_sems.at[a])
            cp.start()
            own.append(cp)
            k0 = 7 * a
            first.append(_remote(ins[a], block(a, x, y, c), send_sems, recv_sems, k0, sibling))
            for j, chip in enumerate(chips):
                first.append(_remote(ins[a], block(a, x, y, c), send_sems, recv_sems, k0 + 1 + j, (*chip, c)))
        for cp in first:
            cp.start()
        for a in range(n):
            k0 = 7 * a
            for j, chip in enumerate(chips):
                got = block(a, *chip, c)
                _remote(got, got, send_sems, recv_sems, k0 + 1 + j, sibling).wait_recv()
                fwd = _remote(got, got, send_sems, recv_sems, k0 + 4 + j, sibling)
                fwd.start()
                passed.append(fwd)
        for a in range(n):
            k0 = 7 * a
            got = block(a, x, y, 1 - c)
            _remote(got, got, send_sems, recv_sems, k0, sibling).wait_recv()
            for j, chip in enumerate(chips):
                got = block(a, *chip, 1 - c)
                _remote(got, got, send_sems, recv_sems, k0 + 4 + j, sibling).wait_recv()
        for cp in first + passed:
            cp.wait_send()
        for cp in own:
            cp.wait()

    hbm = pl.BlockSpec(memory_space=pl.ANY)
    return pl.pallas_call(
        body, name=name,
        in_specs=[hbm] * n, out_specs=[hbm] * n,
        out_shape=[jax.ShapeDtypeStruct((N_DEV,) + s.shape, s.dtype) for s in shards],
        scratch_shapes=[pltpu.SemaphoreType.DMA((7 * n,)), pltpu.SemaphoreType.DMA((7 * n,)),
                        pltpu.SemaphoreType.DMA((n,))],
    )(*shards)


def exchange_sibling(parts, name):
    n = len(parts)

    def body(*refs):
        ins, outs = refs[:n], refs[n:2 * n]
        send_sems, recv_sems = refs[2 * n:]
        x, y, c = _position()
        sibling = (x, y, 1 - c)
        copies = []
        for a in range(n):
            for q in range(4):
                cp = _remote(ins[a].at[2 * q + (1 - c)], outs[a].at[q], send_sems, recv_sems, 4 * a + q, sibling)
                cp.start()
                copies.append(cp)
        for cp in copies:
            cp.wait_recv()
        for cp in copies:
            cp.wait_send()

    hbm = pl.BlockSpec(memory_space=pl.ANY)
    return pl.pallas_call(
        body, name=name,
        in_specs=[hbm] * n, out_specs=[hbm] * n,
        out_shape=[jax.ShapeDtypeStruct((4,) + p.shape[1:], p.dtype) for p in parts],
        scratch_shapes=[pltpu.SemaphoreType.DMA((4 * n,)), pltpu.SemaphoreType.DMA((4 * n,))],
    )(*parts)


def exchange_chips(sums, name):
    n = len(sums)

    def body(*refs):
        ins, outs = refs[:n], refs[n:2 * n]
        send_sems, recv_sems = refs[2 * n:]
        x, y, c = _position()
        copies = []
        for a in range(n):
            for j, (cx, cy) in enumerate(_other_chips(x, y)):
                cp = _remote(ins[a].at[2 * cx + cy], outs[a].at[j], send_sems, recv_sems, 3 * a + j, (cx, cy, c))
                cp.start()
                copies.append(cp)
        for cp in copies:
            cp.wait_recv()
        for cp in copies:
            cp.wait_send()

    hbm = pl.BlockSpec(memory_space=pl.ANY)
    return pl.pallas_call(
        body, name=name,
        in_specs=[hbm] * n, out_specs=[hbm] * n,
        out_shape=[jax.ShapeDtypeStruct((3,) + s.shape[1:], s.dtype) for s in sums],
        scratch_shapes=[pltpu.SemaphoreType.DMA((3 * n,)), pltpu.SemaphoreType.DMA((3 * n,))],
    )(*sums)


def all_sum_small(vec, name):
    R = vec.shape[0]

    def body(v_ref, tot_ref, all_ref, send_sems, recv_sems):
        x, y, c = _position()
        me = 4 * x + 2 * y + c
        all_ref[me] = v_ref[...]
        copies = []
        for k in range(1, N_DEV):
            fx, fy, fc = (k >> 2) & 1, (k >> 1) & 1, k & 1
            to = (1 - x if fx else x, 1 - y if fy else y, 1 - c if fc else c)
            cp = _remote(v_ref, all_ref.at[me], send_sems, recv_sems, k - 1, to)
            cp.start()
            copies.append(cp)
        for cp in copies:
            cp.wait_recv()
        for cp in copies:
            cp.wait_send()
        tot = all_ref[0]
        for j in range(1, N_DEV):
            tot = tot + all_ref[j]
        tot_ref[...] = tot

    vmem = pl.BlockSpec(memory_space=pltpu.VMEM)
    return pl.pallas_call(
        body, name=name,
        in_specs=[vmem], out_specs=vmem,
        out_shape=jax.ShapeDtypeStruct((R, LANES), F32),
        scratch_shapes=[pltpu.VMEM((N_DEV, R, LANES), F32),
                        pltpu.SemaphoreType.DMA((N_DEV - 1,)), pltpu.SemaphoreType.DMA((N_DEV - 1,))],
        compiler_params=pltpu.CompilerParams(vmem_limit_bytes=VMEM_LIMIT),
    )(vec)


def pair_add(mine, theirs, name):
    _, R, C = mine.shape
    tr = _pick(R, 256, 8)

    def body(a_ref, b_ref, o_ref):
        o_ref[...] = (a_ref[...].astype(F32) + b_ref[...].astype(F32)).astype(BF16)

    blk = pl.BlockSpec((None, tr, C), lambda q, i: (q, i, 0))
    return pl.pallas_call(
        body, name=name, grid=(4, R // tr),
        in_specs=[blk, blk], out_specs=blk,
        out_shape=jax.ShapeDtypeStruct(mine.shape, BF16),
        compiler_params=_params(("parallel", "parallel")),
    )(mine, theirs)


def _adamw_math(w, g, m, v):
    m = ADAM_B1 * m + (1.0 - ADAM_B1) * g
    v = ADAM_B2 * v + (1.0 - ADAM_B2) * jnp.square(g)
    m_hat = m / (1.0 - ADAM_B1 ** ADAM_STEP)
    v_hat = v / (1.0 - ADAM_B2 ** ADAM_STEP)
    delta = -ADAM_LR * (m_hat / (jnp.sqrt(v_hat) + ADAM_EPS) + ADAM_WD * w)
    return delta, m, v


def adamw_sharded(w, m, v, own, sib, others, name):
    R, C = w.shape
    tr = _pick(R, 256, 8)

    def body(w_ref, m_ref, v_ref, a_ref, b_ref, o_ref, g_ref, d_ref, nm_ref, nv_ref):
        g = a_ref[...].astype(F32) + b_ref[...].astype(F32)
        for j in range(3):
            g = g + o_ref[j].astype(F32)
        delta, nm, nv = _adamw_math(w_ref[...], g, m_ref[...], v_ref[...])
        g_ref[...] = g
        d_ref[...] = delta
        nm_ref[...] = nm
        nv_ref[...] = nv

    row = pl.BlockSpec((tr, C), lambda i: (i, 0))
    return pl.pallas_call(
        body, name=name, grid=(R // tr,),
        in_specs=[row] * 5 + [pl.BlockSpec((3, tr, C), lambda i: (0, i, 0))],
        out_specs=[row] * 4,
        out_shape=[jax.ShapeDtypeStruct((R, C), F32)] * 4,
        compiler_params=_params(("parallel",)),
    )(w, m, v, own, sib, others)


def adamw_packed(w, g, m, v, name):
    R = w.shape[0]

    def body(w_ref, g_ref, m_ref, v_ref, d_ref, nm_ref, nv_ref):
        delta, nm, nv = _adamw_math(w_ref[...], g_ref[...], m_ref[...], v_ref[...])
        d_ref[...] = delta
        nm_ref[...] = nm
        nv_ref[...] = nv

    full = pl.BlockSpec((R, LANES), lambda i: (0, 0))
    return pl.pallas_call(
        body, name=name, grid=(1,),
        in_specs=[full] * 4, out_specs=[full] * 3,
        out_shape=[jax.ShapeDtypeStruct((R, LANES), F32)] * 3,
        compiler_params=_params(("arbitrary",)),
    )(w, g, m, v)


def _pack(arrays):
    flat = []
    sizes = []
    for a in arrays:
        f = a.reshape(-1).astype(F32)
        pad = (-f.shape[0]) % LANES
        if pad:
            f = jnp.concatenate([f, jnp.zeros((pad,), F32)])
        flat.append(f)
        sizes.append(f.shape[0])
    rows = sum(sizes) // LANES
    pad_rows = (-rows) % 8
    if pad_rows:
        flat.append(jnp.zeros((pad_rows * LANES,), F32))
    return jnp.concatenate(flat).reshape(-1, LANES), sizes


def _unpack(packed, sizes, shapes):
    flat = packed.reshape(-1)
    out = []
    off = 0
    for size, shape in zip(sizes, shapes):
        n = int(np.prod(shape))
        out.append(flat[off:off + n].reshape(shape))
        off += size
    return out


def _to_blocks(full, axis):
    if axis == 0:
        return full.reshape(N_DEV, full.shape[0] // N_DEV, full.shape[1])
    r, n = full.shape
    return full.reshape(r, N_DEV, n // N_DEV).transpose(1, 0, 2)


def _from_blocks(blocks, axis):
    if axis == 0:
        return blocks.reshape(blocks.shape[0] * blocks.shape[1], blocks.shape[2])
    return blocks.transpose(1, 0, 2).reshape(blocks.shape[1], blocks.shape[0] * blocks.shape[2])


def kernel(x, ln0_g, ln0_b, w_in, b_in, conv_w, w_a, w_b, w_o, b_o, ln1_g, ln1_b, w_up, b_up, ffn_conv_w, ffn_conv_b, w_down, b_down, ln2_g, ln2_b, loss_target, m_ln0_g, m_ln0_b, m_w_in, m_b_in, m_conv_w, m_w_a, m_w_b, m_w_o, m_b_o, m_ln1_g, m_ln1_b, m_w_up, m_b_up, m_ffn_conv_w, m_ffn_conv_b, m_w_down, m_b_down, m_ln2_g, m_ln2_b, v_ln0_g, v_ln0_b, v_w_in, v_b_in, v_conv_w, v_w_a, v_w_b, v_w_o, v_b_o, v_ln1_g, v_ln1_b, v_w_up, v_b_up, v_ffn_conv_w, v_ffn_conv_b, v_w_down, v_b_down, v_ln2_g, v_ln2_b):
    T, D = x.shape[1], x.shape[2]
    F = ffn_conv_b.shape[-1]
    xs = x.reshape(T, D)
    tgt = loss_target.reshape(T, D)
    dev = 4 * lax.axis_index("x") + 2 * lax.axis_index("y") + lax.axis_index("c")
    chip = 2 * lax.axis_index("x") + lax.axis_index("y")
    core = lax.axis_index("c")

    big = dict(w_in=(w_in[0], 1), w_a=(w_a[0], 0), w_b=(w_b[0], 1), w_o=(w_o[0], 0), w_up=(w_up[0], 1),
               w_down=(w_down[0], 0))
    names = list(big)
    gathered = all_gather([big[k][0].astype(BF16) for k in names] + [conv_w[0], ffn_conv_w[0]], "gather_weights")
    full = {k: _from_blocks(g, big[k][1]) for k, g in zip(names, gathered)}
    conv_full = _from_blocks(gathered[-2], 1)
    fcw_full = _from_blocks(gathered[-1], 1)
    o_q = 3 * D
    o_g = o_q + 3 * QKV_W
    w_pa, w_qkv, w_pg = full["w_in"][:, :o_q], full["w_in"][:, o_q:o_g], full["w_in"][:, o_g:]
    b_pa, b_qkv, b_pg = b_in[:, :o_q], b_in[:, o_q:o_g], b_in[:, o_g:]
    ln0g, ln0b = ln0_g.reshape(1, D), ln0_b.reshape(1, D)

    h0, h0b, *h0_res = ln_fwd(xs, None, ln0g, ln0b, "ln0_fwd", dilations=DILATIONS[1:])
    h0_res = [h0b] + [h.reshape(T, D) for h in h0_res]
    proj_a = mm_nn(h0b, w_pa, b_pa, F32, "proj_conv")
    proj_g = mm_nn(h0b, w_pg, b_pg, F32, "proj_gates")
    zero_d = jnp.zeros((1, D), F32)
    s_a = conv_a_fwd(proj_a, conv_full, "conv_a_fwd")
    y_a = mm_nn(s_a, full["w_a"], zero_d, F32, "branch_a_out")

    def group_cols(m, g):
        return jnp.concatenate([m[:, s * QKV_W + g * GROUP_W:s * QKV_W + (g + 1) * GROUP_W] for s in range(3)], 1)

    w_grp = [group_cols(w_qkv, g) for g in range(3)]
    qkvs, outs, lses = [], [], []
    for g, d in enumerate(DILATIONS):
        qkv = mm_nn(h0_res[g], w_grp[g], group_cols(b_qkv, g), BF16, f"proj_qkv_{g}").reshape(d, T // d, 3 * GROUP_W)
        o, l = att_fwd(qkv, g, f"att_fwd_{g}")
        qkvs.append(qkv)
        outs.append(o)
        lses.append(l)
    comb = combine_fwd(outs, lses, "combine_fwd")
    y_b = mm_nn(comb, full["w_b"], zero_d, F32, "branch_b_out")
    z = gate_fwd(proj_g, y_a, y_b, "gate_fwd")
    mix = mm_nn(z, full["w_o"], b_o, F32, "mix_out")
    h1, h1b = ln_fwd(h0, mix, ln1_g, ln1_b, "ln1_fwd")
    up = mm_nn(h1b, full["w_up"], b_up, F32, "ffn_up")
    f_act = conv_f_fwd(up, fcw_full, ffn_conv_b, "conv_f_fwd")
    ffn = mm_nn(f_act, full["w_down"], b_down, F32, "ffn_down")

    dr2, dr2b, d_ln2_g, d_ln2_b, d_b_down, loss_part = ln_bwd(h1, ffn, ln2_g, ln2_b, None, None, tgt, "ln2_loss_bwd")
    dw_down, _ = mm_tn(f_act, dr2b, "dw_down")
    df = mm_nt(dr2b, full["w_down"], None, "d_ffn_act")
    d_a, d_gate, cs_a, cs_gate, d_fcb, d_fcw = conv_f_bwd(df, up, fcw_full, ffn_conv_b, "conv_f_bwd")
    dw_up_a, _ = mm_tn(h1b, d_a, "dw_up_a")
    dw_up_g, _ = mm_tn(h1b, d_gate, "dw_up_gate")
    dh1 = mm_nt(d_a, full["w_up"][:, :F], None, "d_h1_a")
    dh1 = mm_nt(d_gate, full["w_up"][:, F:], dh1, "d_h1_gate")
    dr1, dr1b, d_ln1_g, d_ln1_b, d_b_o, _ = ln_bwd(h0, mix, ln1_g, ln1_b, dr2, dh1, None, "ln1_bwd")
    dw_o, _ = mm_tn(z, dr1b, "dw_o")
    dz = mm_nt(dr1b, full["w_o"], None, "d_z")
    dy_a, dy_b, dproj_g = gate_bwd(dz, proj_g, y_a, y_b, "gate_bwd")
    dw_a, _ = mm_tn(s_a, dy_a, "dw_a")
    ds_a = mm_nt(dy_a, full["w_a"], None, "d_s_a")
    dproj_a, d_conv = conv_a_bwd(ds_a, proj_a, conv_full, "conv_a_bwd")
    dw_b, _ = mm_tn(comb, dy_b, "dw_b")
    dcomb = mm_nt(dy_b, full["w_b"], None, "d_comb")
    dos, dms = combine_bwd(dcomb, outs, lses, "combine_bwd")
    dw_pa, cs_pa = mm_tn(h0b, dproj_a, "dw_in_conv")
    dw_pg, cs_pg = mm_tn(h0b, dproj_g, "dw_in_gates")
    dh0 = mm_nt(dproj_a, w_pa, None, "d_h0_conv")
    dh0 = mm_nt(dproj_g, w_pg, dh0, "d_h0_gates")
    dw_grp, cs_grp, dh0_res = [], [], []
    for g, d in enumerate(DILATIONS):
        dq, dk, dv = att_bwd(qkvs[g], dos[g], lses[g], dms[g], g, f"att_bwd_{g}")
        dqkv = jnp.concatenate([dq, dk, dv], -1).reshape(T, 3 * GROUP_W)
        dwg, csg = mm_tn(h0_res[g], dqkv, f"dw_in_qkv_{g}")
        dw_grp.append(dwg)
        cs_grp.append(csg)
        if g == 0:
            dh0 = mm_nt(dqkv, w_grp[g], dh0, f"d_h0_qkv_{g}")
        else:
            dh0_res.append((mm_nt(dqkv, w_grp[g], None, f"d_h0_qkv_{g}").reshape(d, T // d, D), d))

    def ungroup(parts):
        return jnp.concatenate([p[:, s * GROUP_W:(s + 1) * GROUP_W] for s in range(3) for p in parts], 1)

    dw_in_parts = [dw_pa, ungroup(dw_grp), dw_pg]
    db_in_parts = [cs_pa, ungroup(cs_grp), cs_pg]
    dx, _, d_ln0_g, d_ln0_b, _, _ = ln_bwd(xs, None, ln0g, ln0b, dr1, dh0, None, "ln0_bwd", by_residue=dh0_res)

    small = [d_ln0_g, d_ln0_b, jnp.concatenate(db_in_parts, 1), d_conv, d_b_o, d_ln1_g, d_ln1_b,
             jnp.concatenate([cs_a, cs_gate], 1), d_fcw, d_fcb, d_b_down, d_ln2_g, d_ln2_b, loss_part]
    packed, sizes = _pack(small)
    total = all_sum_small(packed, "sum_small")
    (g_ln0_g, g_ln0_b, g_b_in, g_conv_full, g_b_o, g_ln1_g, g_ln1_b, g_b_up, g_fcw_full, g_fcb, g_b_down, g_ln2_g,
     g_ln2_b, loss) = _unpack(total, sizes, [a.shape for a in small])
    cw = conv_w.shape[-1]
    fw = ffn_conv_w.shape[-1]
    g_conv = lax.dynamic_slice_in_dim(g_conv_full, dev * cw, cw, 1)
    g_fcw = lax.dynamic_slice_in_dim(g_fcw_full, dev * fw, fw, 1)

    dw_full = dict(w_in=jnp.concatenate(dw_in_parts, 1), w_a=dw_a, w_b=dw_b, w_o=dw_o,
                   w_up=jnp.concatenate([dw_up_a, dw_up_g], 1), w_down=dw_down)
    parts = [_to_blocks(dw_full[k], big[k][1]) for k in names]
    from_sib = exchange_sibling(parts, "grads_to_sibling")
    mine = [lax.dynamic_index_in_dim(p.reshape((4, 2) + p.shape[1:]), core, 1, keepdims=False) for p in parts]
    chip_sums = [pair_add(a, b, f"chip_sum_{k}") for k, a, b in zip(names, mine, from_sib)]
    from_chips = exchange_chips(chip_sums, "grads_to_chips")

    moments = dict(w_in=(m_w_in, v_w_in), w_a=(m_w_a, v_w_a), w_b=(m_w_b, v_w_b), w_o=(m_w_o, v_w_o),
                   w_up=(m_w_up, v_w_up), w_down=(m_w_down, v_w_down))
    res_big = {}
    for k, a, b, o in zip(names, mine, from_sib, from_chips):
        own = lax.dynamic_index_in_dim(a, chip, 0, keepdims=False)
        sib = lax.dynamic_index_in_dim(b, chip, 0, keepdims=False)
        res_big[k] = adamw_sharded(big[k][0], moments[k][0][0], moments[k][1][0], own, sib, o, f"adamw_{k}")

    small_names = ["ln0_g", "ln0_b", "b_in", "conv_w", "b_o", "ln1_g", "ln1_b", "b_up", "ffn_conv_w", "ffn_conv_b",
                   "b_down", "ln2_g", "ln2_b"]
    small_w = [ln0_g, ln0_b, b_in, conv_w, b_o, ln1_g, ln1_b, b_up, ffn_conv_w, ffn_conv_b, b_down, ln2_g, ln2_b]
    small_m = [m_ln0_g, m_ln0_b, m_b_in, m_conv_w, m_b_o, m_ln1_g, m_ln1_b, m_b_up, m_ffn_conv_w, m_ffn_conv_b,
               m_b_down, m_ln2_g, m_ln2_b]
    small_v = [v_ln0_g, v_ln0_b, v_b_in, v_conv_w, v_b_o, v_ln1_g, v_ln1_b, v_b_up, v_ffn_conv_w, v_ffn_conv_b,
               v_b_down, v_ln2_g, v_ln2_b]
    small_g = [g_ln0_g, g_ln0_b, g_b_in, g_conv, g_b_o, g_ln1_g, g_ln1_b, g_b_up, g_fcw, g_fcb, g_b_down, g_ln2_g,
               g_ln2_b]
    shapes = [w.shape for w in small_w]
    small_g = [g.reshape(s) for g, s in zip(small_g, shapes)]
    pw, psz = _pack(small_w)
    pg, _ = _pack(small_g)
    pm, _ = _pack(small_m)
    pv, _ = _pack(small_v)
    pd, pnm, pnv = adamw_packed(pw, pg, pm, pv, "adamw_small")
    res_small = {k: (g, d_, m_, v_) for k, g, d_, m_, v_ in zip(
        small_names, small_g, _unpack(pd, psz, shapes), _unpack(pnm, psz, shapes), _unpack(pnv, psz, shapes))}

    order = ["ln0_g", "ln0_b", "w_in", "b_in", "conv_w", "w_a", "w_b", "w_o", "b_o", "ln1_g", "ln1_b", "w_up", "b_up",
             "ffn_conv_w", "ffn_conv_b", "w_down", "b_down", "ln2_g", "ln2_b"]

    def result(k, j):
        if k in res_big:
            return res_big[k][j][None]
        return res_small[k][j]

    out = [loss.reshape(()), dx.reshape(x.shape)]
    for j in range(4):
        out += [result(k, j) for k in order]
    return tuple(out)
```

```python
import functools
import math

import numpy as np
import jax
import jax.numpy as jnp
from jax import lax
from jax.experimental import pallas as pl
from jax.experimental.pallas import tpu as pltpu

F32 = jnp.float32
BF16 = jnp.bfloat16

N_DEV = 8
LN_EPS = 1e-5
ALPHA = (2.0 * 1) ** 0.25
MASK_VALUE = -1e30
HEAD_DIM = 64
GROUP_W = 512
QKV_W = 3 * GROUP_W
DILATIONS = (1, 4, 16)
RADIUS = 64
LANES = 128
HALO = 8
ATT_TQ = 128

ADAM_LR = 0.001
ADAM_B1 = 0.9
ADAM_B2 = 0.999
ADAM_EPS = 1e-08
ADAM_WD = 0.01
ADAM_STEP = 10

VMEM_LIMIT = 52 * 1024 * 1024
OUT_TILE_BYTES = 8 * 1024 * 1024
MESH = pl.DeviceIdType.MESH
NT_DIMS = (((1,), (1,)), ((), ()))
TN_DIMS = (((0,), (0,)), ((), ()))


def _pick(n, target, align=LANES):
    if n <= target:
        return n
    best = None
    for t in range(align, target + 1, align):
        if n % t == 0:
            best = t
    assert best is not None, (n, target, align)
    return best


def _params(sems=None):
    return pltpu.CompilerParams(dimension_semantics=sems, vmem_limit_bytes=VMEM_LIMIT)


def _alibi_slopes():
    n = 3 * 8
    return np.exp2(-8.0 * np.arange(1, n + 1, dtype=np.float64) / n).astype(np.float32).reshape(3, 8)


def _ln_stats(r):
    mu = jnp.mean(r, -1, keepdims=True)
    xc = r - mu
    var = jnp.mean(xc * xc, -1, keepdims=True)
    rstd = lax.rsqrt(var + LN_EPS)
    return xc, rstd


def _load_natural(ref, d, scr):
    if d == 1:
        return ref[0]
    n, C = ref.shape[1], ref.shape[2]
    for c in range(C // LANES):
        for r in range(d):
            scr[c, pl.ds(r, n, stride=d), :] = ref[r, :, c * LANES:(c + 1) * LANES]
    return jnp.concatenate([scr[c] for c in range(C // LANES)], axis=1)


def _store_by_residue(val, ref, d, scr):
    if d == 1:
        ref[0] = val.astype(ref.dtype)
        return
    n, C = ref.shape[1], ref.shape[2]
    for c in range(C // LANES):
        scr[c] = val[:, c * LANES:(c + 1) * LANES]
    for c in range(C // LANES):
        for r in range(d):
            ref[r, :, c * LANES:(c + 1) * LANES] = scr[c, pl.ds(r, n, stride=d), :].astype(ref.dtype)


def _residue_spec(tm, d, C):
    return pl.BlockSpec((d, tm // d, C), lambda i: (0, i, 0))


def _residue_scratch(tm, C):
    return pltpu.VMEM((C // LANES, tm, LANES), F32)


def ln_fwd(a, res, g, b, name, dilations=()):
    T, D = a.shape
    tm = _pick(T, 512, 8)
    has_res = res is not None
    nd = len(dilations)

    def body(*refs):
        refs = list(refs)
        a_ref = refs.pop(0)
        r = a_ref[...]
        if has_res:
            r = ALPHA * r + refs.pop(0)[...]
        g_ref, b_ref, h_ref, hb_ref = refs[:4]
        xc, rstd = _ln_stats(r)
        h = xc * rstd * g_ref[...] + b_ref[...]
        h_ref[...] = h
        hb_ref[...] = h.astype(BF16)
        for d, p_ref in zip(dilations, refs[4:4 + nd]):
            _store_by_residue(h, p_ref, d, refs[-1])

    row = pl.BlockSpec((tm, D), lambda i: (i, 0))
    vec = pl.BlockSpec((1, D), lambda i: (0, 0))
    ins = [a] + ([res] if has_res else []) + [g, b]
    return pl.pallas_call(
        body, name=name, grid=(T // tm,),
        in_specs=[row] * (2 if has_res else 1) + [vec, vec],
        out_specs=[row, row] + [_residue_spec(tm, d, D) for d in dilations],
        out_shape=[jax.ShapeDtypeStruct((T, D), F32), jax.ShapeDtypeStruct((T, D), BF16)]
        + [jax.ShapeDtypeStruct((d, T // d, D), BF16) for d in dilations],
        scratch_shapes=[_residue_scratch(tm, D)] if nd else [],
        compiler_params=_params(("parallel",)),
    )(*ins)


def ln_bwd(a, res, g, b, d1, d2, tgt, name, by_residue=()):
    T, D = a.shape
    tm = _pick(T, 256, 8)
    has_res = res is not None
    loss_mode = tgt is not None
    nres = len(by_residue)

    def body(*refs):
        refs = list(refs)
        a_ref = refs.pop(0)
        r_ref = refs.pop(0) if has_res else None
        g_ref = refs.pop(0)
        b_ref = refs.pop(0)
        if loss_mode:
            t_ref = refs.pop(0)
        else:
            d1_ref = refs.pop(0)
            d2_ref = refs.pop(0)
        e_refs = [refs.pop(0) for _ in range(nres)]
        dr_ref, drb_ref, dg_ref, db_ref, ds_ref, loss_ref = refs[:6]
        i = pl.program_id(0)

        @pl.when(i == 0)
        def _():
            dg_ref[...] = jnp.zeros_like(dg_ref)
            db_ref[...] = jnp.zeros_like(db_ref)
            ds_ref[...] = jnp.zeros_like(ds_ref)
            loss_ref[...] = jnp.zeros_like(loss_ref)

        r = a_ref[...]
        if has_res:
            r = ALPHA * r + r_ref[...]
        xc, rstd = _ln_stats(r)
        xhat = xc * rstd
        gam = g_ref[...]
        if loss_mode:
            err = xhat * gam + b_ref[...] - t_ref[...]
            dy = err * (1.0 / D)
            row_loss = jnp.mean(err * err, -1, keepdims=True)
            loss_ref[...] += 0.5 * jnp.sum(row_loss, 0, keepdims=True)
        else:
            dy = ALPHA * d1_ref[...] + d2_ref[...]
        for (_, d), e_ref in zip(by_residue, e_refs):
            dy = dy + _load_natural(e_ref, d, refs[-1])
        dyg = dy * gam
        c1 = jnp.mean(dyg, -1, keepdims=True)
        c2 = jnp.mean(dyg * xhat, -1, keepdims=True)
        dr = rstd * (dyg - c1 - xhat * c2)
        dr_ref[...] = dr
        drb_ref[...] = dr.astype(BF16)
        dg_ref[...] += jnp.sum(dy * xhat, 0, keepdims=True)
        db_ref[...] += jnp.sum(dy, 0, keepdims=True)
        ds_ref[...] += jnp.sum(dr, 0, keepdims=True)

    row = pl.BlockSpec((tm, D), lambda i: (i, 0))
    vec = pl.BlockSpec((1, D), lambda i: (0, 0))
    one = pl.BlockSpec((1, 1), lambda i: (0, 0))
    ins = [a] + ([res] if has_res else []) + [g, b] + ([tgt] if loss_mode else [d1, d2]) + [e for e, _ in by_residue]
    in_specs = [row] * (2 if has_res else 1) + [vec, vec] + [row] * (1 if loss_mode else 2)
    in_specs += [_residue_spec(tm, d, D) for _, d in by_residue]
    return pl.pallas_call(
        body, name=name, grid=(T // tm,),
        in_specs=in_specs,
        out_specs=[row, row, vec, vec, vec, one],
        out_shape=[jax.ShapeDtypeStruct((T, D), F32), jax.ShapeDtypeStruct((T, D), BF16),
                   jax.ShapeDtypeStruct((1, D), F32), jax.ShapeDtypeStruct((1, D), F32),
                   jax.ShapeDtypeStruct((1, D), F32), jax.ShapeDtypeStruct((1, 1), F32)],
        scratch_shapes=[_residue_scratch(tm, D)] if nres else [],
        compiler_params=_params(("arbitrary",)),
    )(*ins)


def mm_nn(a, w, bias, out_dtype, name):
    M, K = a.shape
    N = w.shape[1]
    tm = _pick(M, max(256, min(1024, OUT_TILE_BYTES // (N * jnp.dtype(out_dtype).itemsize))), 8)
    tc = _pick(N, 512)

    def body(a_ref, w_ref, b_ref, o_ref):
        av = a_ref[...]
        for j in range(N // tc):
            cols = slice(j * tc, (j + 1) * tc)
            acc = jnp.dot(av, w_ref[:, cols], preferred_element_type=F32)
            o_ref[:, cols] = (acc + b_ref[:, cols]).astype(out_dtype)

    return pl.pallas_call(
        body, name=name, grid=(M // tm,),
        in_specs=[pl.BlockSpec((tm, K), lambda i: (i, 0)),
                  pl.BlockSpec((K, N), lambda i: (0, 0)),
                  pl.BlockSpec((1, N), lambda i: (0, 0))],
        out_specs=pl.BlockSpec((tm, N), lambda i: (i, 0)),
        out_shape=jax.ShapeDtypeStruct((M, N), out_dtype),
        compiler_params=_params(("parallel",)),
    )(a, w, bias)


def mm_nt(a, w, acc_in, name):
    M, K = a.shape
    N = w.shape[0]
    tm = _pick(M, 512, 8)
    tc = _pick(N, 512)
    has_acc = acc_in is not None

    def body(*refs):
        if has_acc:
            a_ref, w_ref, c_ref, o_ref = refs
        else:
            a_ref, w_ref, o_ref = refs
        av = a_ref[...]
        for j in range(N // tc):
            cols = slice(j * tc, (j + 1) * tc)
            acc = lax.dot_general(av, w_ref[cols, :], NT_DIMS, preferred_element_type=F32)
            if has_acc:
                acc = acc + c_ref[:, cols]
            o_ref[:, cols] = acc

    out_spec = pl.BlockSpec((tm, N), lambda i: (i, 0))
    in_specs = [pl.BlockSpec((tm, K), lambda i: (i, 0)),
                pl.BlockSpec((N, K), lambda i: (0, 0))]
    ins = [a, w]
    if has_acc:
        in_specs.append(out_spec)
        ins.append(acc_in)
    return pl.pallas_call(
        body, name=name, grid=(M // tm,),
        in_specs=in_specs, out_specs=out_spec,
        out_shape=jax.ShapeDtypeStruct((M, N), F32),
        compiler_params=_params(("parallel",)),
    )(*ins)


def mm_tn(a, b, name, out_dtype=BF16):
    T, M = a.shape
    N = b.shape[1]
    tm = _pick(M, 1408)
    tn = _pick(N, 2560 if tm <= 1024 else 1024)
    tk = _pick(T, 512, 8)
    nk = T // tk

    def body(a_ref, b_ref, o_ref, cs_ref, acc_ref):
        m = pl.program_id(1)
        k = pl.program_id(2)

        @pl.when(k == 0)
        def _():
            acc_ref[...] = jnp.zeros_like(acc_ref)

        @pl.when((k == 0) & (m == 0))
        def _():
            cs_ref[...] = jnp.zeros_like(cs_ref)

        bv = b_ref[...]
        acc_ref[...] += lax.dot_general(a_ref[...], bv, TN_DIMS, preferred_element_type=F32)

        @pl.when(m == 0)
        def _():
            cs_ref[...] += jnp.sum(bv.astype(F32), 0, keepdims=True)

        @pl.when(k == nk - 1)
        def _():
            o_ref[...] = acc_ref[...].astype(out_dtype)

    return pl.pallas_call(
        body, name=name, grid=(N // tn, M // tm, nk),
        in_specs=[pl.BlockSpec((tk, tm), lambda n, m, k: (k, m)),
                  pl.BlockSpec((tk, tn), lambda n, m, k: (k, n))],
        out_specs=[pl.BlockSpec((tm, tn), lambda n, m, k: (m, n)),
                   pl.BlockSpec((1, tn), lambda n, m, k: (0, n))],
        out_shape=[jax.ShapeDtypeStruct((M, N), out_dtype), jax.ShapeDtypeStruct((1, N), F32)],
        scratch_shapes=[pltpu.VMEM((tm, tn), F32)],
        compiler_params=_params(("arbitrary", "arbitrary", "arbitrary")),
    )(a, b)


def _ext_rows(prev_ref, main_ref, next_ref, i, tm, T):
    before = jnp.where(i == 0, 0.0, prev_ref[...])
    after = jnp.where(i == T // tm - 1, 0.0, next_ref[...])
    return jnp.concatenate([before, main_ref[...], after], axis=0)


def _prev_row(x):
    return pltpu.roll(x, 1, 0)


def _next_row(x):
    return pltpu.roll(x, x.shape[0] - 1, 0)


def _conv3(u, w_ref):
    return _prev_row(u) * w_ref[0:1, :] + u * w_ref[1:2, :] + _next_row(u) * w_ref[2:3, :]


def _main(x, tm):
    return x[HALO:HALO + tm]


def _halo_specs(tm, tc, T, col, order):
    r = tm // HALO
    last = T // HALO - 1
    if order == "ij":
        return (pl.BlockSpec((HALO, tc), lambda i, j: (jnp.maximum(i * r - 1, 0), col(j))),
                pl.BlockSpec((tm, tc), lambda i, j: (i, col(j))),
                pl.BlockSpec((HALO, tc), lambda i, j: (jnp.minimum((i + 1) * r, last), col(j))))
    return (pl.BlockSpec((HALO, tc), lambda j, i: (jnp.maximum(i * r - 1, 0), col(j))),
            pl.BlockSpec((tm, tc), lambda j, i: (i, col(j))),
            pl.BlockSpec((HALO, tc), lambda j, i: (jnp.minimum((i + 1) * r, last), col(j))))


def conv_a_fwd(proj_a, conv_w, name):
    T, D3 = proj_a.shape
    D = D3 // 3
    tm = _pick(T, 256, 8)

    def body(p_ref, m_ref, n_ref, w_ref, o_ref):
        i = pl.program_id(0)
        ext = _ext_rows(p_ref, m_ref, n_ref, i, tm, T)
        u = ext[:, D:2 * D] * ext[:, 2 * D:]
        cu = _conv3(u, w_ref)
        o_ref[...] = (m_ref[:, :D] * _main(cu, tm)).astype(BF16)

    prev, main, nxt = _halo_specs(tm, D3, T, lambda j: 0, "ij")
    return pl.pallas_call(
        body, name=name, grid=(T // tm, 1),
        in_specs=[prev, main, nxt, pl.BlockSpec((3, D), lambda i, j: (0, 0))],
        out_specs=pl.BlockSpec((tm, D), lambda i, j: (i, 0)),
        out_shape=jax.ShapeDtypeStruct((T, D), BF16),
        compiler_params=_params(("parallel", "arbitrary")),
    )(proj_a, proj_a, proj_a, conv_w)


def conv_a_bwd(ds_a, proj_a, conv_w, name):
    T, D3 = proj_a.shape
    D = D3 // 3
    tm = _pick(T, 256, 8)

    def body(dp_ref, dm_ref, dn_ref, p_ref, m_ref, n_ref, w_ref, o_ref, dw_ref):
        i = pl.program_id(0)

        @pl.when(i == 0)
        def _():
            dw_ref[...] = jnp.zeros_like(dw_ref)

        ext = _ext_rows(p_ref, m_ref, n_ref, i, tm, T)
        dsa = _ext_rows(dp_ref, dm_ref, dn_ref, i, tm, T)
        gb, gc, hin = ext[:, :D], ext[:, D:2 * D], ext[:, 2 * D:]
        u = gc * hin
        u_prev, u_next = _prev_row(u), _next_row(u)
        cu = u_prev * w_ref[0:1, :] + u * w_ref[1:2, :] + u_next * w_ref[2:3, :]
        dcu = dsa * gb
        du = _next_row(dcu) * w_ref[0:1, :] + dcu * w_ref[1:2, :] + _prev_row(dcu) * w_ref[2:3, :]
        o_ref[:, :D] = _main(dsa * cu, tm).astype(BF16)
        o_ref[:, D:2 * D] = _main(du * hin, tm).astype(BF16)
        o_ref[:, 2 * D:] = _main(du * gc, tm).astype(BF16)
        dcu_m = _main(dcu, tm)
        dw_ref[0:1, :] += jnp.sum(dcu_m * _main(u_prev, tm), 0, keepdims=True)
        dw_ref[1:2, :] += jnp.sum(dcu_m * _main(u, tm), 0, keepdims=True)
        dw_ref[2:3, :] += jnp.sum(dcu_m * _main(u_next, tm), 0, keepdims=True)

    dprev, dmain, dnxt = _halo_specs(tm, D, T, lambda j: 0, "ij")
    prev, main, nxt = _halo_specs(tm, D3, T, lambda j: 0, "ij")
    return pl.pallas_call(
        body, name=name, grid=(T // tm, 1),
        in_specs=[dprev, dmain, dnxt, prev, main, nxt, pl.BlockSpec((3, D), lambda i, j: (0, 0))],
        out_specs=[pl.BlockSpec((tm, D3), lambda i, j: (i, 0)), pl.BlockSpec((3, D), lambda i, j: (0, 0))],
        out_shape=[jax.ShapeDtypeStruct((T, D3), BF16), jax.ShapeDtypeStruct((3, D), F32)],
        compiler_params=_params(("arbitrary", "arbitrary")),
    )(ds_a, ds_a, ds_a, proj_a, proj_a, proj_a, conv_w)


_INV_SQRT2 = 1.0 / math.sqrt(2.0)
_INV_SQRT_2PI = 1.0 / math.sqrt(2.0 * math.pi)


def conv_f_fwd(up, fcw, fcb, name):
    T, F2 = up.shape
    F = F2 // 2
    tm = _pick(T, 512, 8)
    tc = _pick(F, 256)
    nc = F // tc

    def body(p_ref, m_ref, n_ref, g_ref, w_ref, b_ref, o_ref):
        i = pl.program_id(0)
        a = _ext_rows(p_ref, m_ref, n_ref, i, tm, T)
        ca = _main(_conv3(a, w_ref), tm) + b_ref[...]
        gl = 0.5 * ca * (1.0 + lax.erf(ca * _INV_SQRT2))
        o_ref[...] = (gl * g_ref[...]).astype(BF16)

    prev, main, nxt = _halo_specs(tm, tc, T, lambda j: j, "ij")
    return pl.pallas_call(
        body, name=name, grid=(T // tm, nc),
        in_specs=[prev, main, nxt,
                  pl.BlockSpec((tm, tc), lambda i, j: (i, nc + j)),
                  pl.BlockSpec((3, tc), lambda i, j: (0, j)),
                  pl.BlockSpec((1, tc), lambda i, j: (0, j))],
        out_specs=pl.BlockSpec((tm, tc), lambda i, j: (i, j)),
        out_shape=jax.ShapeDtypeStruct((T, F), BF16),
        compiler_params=_params(("parallel", "parallel")),
    )(up, up, up, up, fcw, fcb)


def conv_f_bwd(df, up, fcw, fcb, name):
    T, F2 = up.shape
    F = F2 // 2
    tm = _pick(T, 512, 8)
    tc = _pick(F, 256)
    nc = F // tc

    def body(fp_ref, fm_ref, fn_ref, ap_ref, am_ref, an_ref, gp_ref, gm_ref, gn_ref, w_ref, b_ref,
             da_ref, dg_ref, csa_ref, csg_ref, dfb_ref, dfw_ref):
        i = pl.program_id(1)

        @pl.when(i == 0)
        def _():
            csa_ref[...] = jnp.zeros_like(csa_ref)
            csg_ref[...] = jnp.zeros_like(csg_ref)
            dfb_ref[...] = jnp.zeros_like(dfb_ref)
            dfw_ref[...] = jnp.zeros_like(dfw_ref)

        dfe = _ext_rows(fp_ref, fm_ref, fn_ref, i, tm, T)
        a = _ext_rows(ap_ref, am_ref, an_ref, i, tm, T)
        gate = _ext_rows(gp_ref, gm_ref, gn_ref, i, tm, T)
        a_prev, a_next = _prev_row(a), _next_row(a)
        ca = a_prev * w_ref[0:1, :] + a * w_ref[1:2, :] + a_next * w_ref[2:3, :] + b_ref[...]
        cdf = 0.5 * (1.0 + lax.erf(ca * _INV_SQRT2))
        gl = ca * cdf
        gp = cdf + ca * (jnp.exp(-0.5 * ca * ca) * _INV_SQRT_2PI)
        dgate = _main(dfe * gl, tm)
        dca = dfe * gate * gp
        da = _main(_next_row(dca) * w_ref[0:1, :] + dca * w_ref[1:2, :] + _prev_row(dca) * w_ref[2:3, :], tm)
        da_ref[...] = da.astype(BF16)
        dg_ref[...] = dgate.astype(BF16)
        csa_ref[...] += jnp.sum(da, 0, keepdims=True)
        csg_ref[...] += jnp.sum(dgate, 0, keepdims=True)
        dca_m = _main(dca, tm)
        dfb_ref[...] += jnp.sum(dca_m, 0, keepdims=True)
        dfw_ref[0:1, :] += jnp.sum(dca_m * _main(a_prev, tm), 0, keepdims=True)
        dfw_ref[1:2, :] += jnp.sum(dca_m * _main(a, tm), 0, keepdims=True)
        dfw_ref[2:3, :] += jnp.sum(dca_m * _main(a_next, tm), 0, keepdims=True)

    fprev, fmain, fnxt = _halo_specs(tm, tc, T, lambda j: j, "ji")
    gprev, gmain, gnxt = _halo_specs(tm, tc, T, lambda j: nc + j, "ji")
    tile = pl.BlockSpec((tm, tc), lambda j, i: (i, j))
    vec = pl.BlockSpec((1, tc), lambda j, i: (0, j))
    vec3 = pl.BlockSpec((3, tc), lambda j, i: (0, j))
    return pl.pallas_call(
        body, name=name, grid=(nc, T // tm),
        in_specs=[fprev, fmain, fnxt, fprev, fmain, fnxt, gprev, gmain, gnxt, vec3, vec],
        out_specs=[tile, tile, vec, vec, vec, vec3],
        out_shape=[jax.ShapeDtypeStruct((T, F), BF16), jax.ShapeDtypeStruct((T, F), BF16),
                   jax.ShapeDtypeStruct((1, F), F32), jax.ShapeDtypeStruct((1, F), F32),
                   jax.ShapeDtypeStruct((1, F), F32), jax.ShapeDtypeStruct((3, F), F32)],
        compiler_params=_params(("arbitrary", "arbitrary")),
    )(df, df, df, up, up, up, up, up, up, fcw, fcb)


def gate_fwd(proj_g, y_a, y_b, name):
    T, D = y_a.shape
    tm = _pick(T, 512, 8)

    def body(g_ref, a_ref, b_ref, o_ref):
        sa = jax.nn.sigmoid(g_ref[:, :D])
        sb = jax.nn.sigmoid(g_ref[:, D:])
        o_ref[...] = (sa * a_ref[...] + sb * b_ref[...]).astype(BF16)

    row = pl.BlockSpec((tm, D), lambda i: (i, 0))
    return pl.pallas_call(
        body, name=name, grid=(T // tm,),
        in_specs=[pl.BlockSpec((tm, 2 * D), lambda i: (i, 0)), row, row],
        out_specs=row,
        out_shape=jax.ShapeDtypeStruct((T, D), BF16),
        compiler_params=_params(("parallel",)),
    )(proj_g, y_a, y_b)


def gate_bwd(dz, proj_g, y_a, y_b, name):
    T, D = y_a.shape
    tm = _pick(T, 512, 8)

    def body(dz_ref, g_ref, a_ref, b_ref, da_ref, db_ref, dg_ref):
        dzv = dz_ref[...]
        sa = jax.nn.sigmoid(g_ref[:, :D])
        sb = jax.nn.sigmoid(g_ref[:, D:])
        da_ref[...] = (dzv * sa).astype(BF16)
        db_ref[...] = (dzv * sb).astype(BF16)
        dg_ref[:, :D] = (dzv * a_ref[...] * (sa * (1.0 - sa))).astype(BF16)
        dg_ref[:, D:] = (dzv * b_ref[...] * (sb * (1.0 - sb))).astype(BF16)

    row = pl.BlockSpec((tm, D), lambda i: (i, 0))
    wide = pl.BlockSpec((tm, 2 * D), lambda i: (i, 0))
    return pl.pallas_call(
        body, name=name, grid=(T // tm,),
        in_specs=[row, wide, row, row],
        out_specs=[row, row, wide],
        out_shape=[jax.ShapeDtypeStruct((T, D), BF16), jax.ShapeDtypeStruct((T, D), BF16),
                   jax.ShapeDtypeStruct((T, 2 * D), BF16)],
        compiler_params=_params(("parallel",)),
    )(dz, proj_g, y_a, y_b)


ATT_WIN = ATT_TQ + 2 * RADIUS
ATT_STEP = 512
FAR = 1e32


def _att_window(qs, L, d):
    ks = pl.multiple_of(jnp.clip(qs - RADIUS, 0, L - ATT_WIN), RADIUS)
    col_row = (lax.broadcasted_iota(jnp.int32, (ATT_TQ, ATT_WIN), 1)
               - lax.broadcasted_iota(jnp.int32, (ATT_TQ, ATT_WIN), 0))
    ad = jnp.abs(col_row + (ks - qs))
    return ks, jnp.where(ad <= RADIUS, (ad * d).astype(F32), FAR)


def _head_masks():
    lane = lax.broadcasted_iota(jnp.int32, (1, LANES), 1)
    return [lane < HEAD_DIM, lane >= HEAD_DIM]


def _att_step(L):
    step = min(ATT_STEP, L)
    assert L % step == 0 and step % ATT_TQ == 0 and L >= ATT_WIN
    return step


def att_fwd(qkv, group, name):
    d, L, _ = qkv.shape
    step = _att_step(L)
    cg = GROUP_W // LANES
    slopes = jnp.asarray(_alibi_slopes()[group])
    scale = HEAD_DIM ** -0.5

    def body(sl_ref, q_ref, k_ref, v_ref, o_ref, l_ref):
        hp = pl.program_id(1)
        i = pl.program_id(2)
        masks = _head_masks()
        for t in range(step // ATT_TQ):
            rows = slice(t * ATT_TQ, (t + 1) * ATT_TQ)
            ks, dist = _att_window(i * step + t * ATT_TQ, L, d)
            q = q_ref[rows, :] * scale
            kw = k_ref[pl.ds(ks, ATT_WIN), :]
            vw = v_ref[pl.ds(ks, ATT_WIN), :]
            o_acc = jnp.zeros((ATT_TQ, LANES), F32)
            l_acc = jnp.zeros((ATT_TQ, LANES), F32)
            for h, hm in enumerate(masks):
                slope = sl_ref[hp * 2 + h]
                qm = jnp.where(hm, q, jnp.zeros_like(q))
                s = lax.dot_general(qm, kw, NT_DIMS, preferred_element_type=F32) - slope * dist
                m = jnp.max(s, -1, keepdims=True)
                p = jnp.exp(s - m)
                den = jnp.sum(p, -1, keepdims=True)
                pn = (p / den).astype(BF16)
                vm = jnp.where(hm, vw, jnp.zeros_like(vw))
                o_acc = o_acc + jnp.dot(pn, vm, preferred_element_type=F32)
                l_acc = jnp.where(hm, m + jnp.log(den), l_acc)
            o_ref[rows, :] = o_acc
            l_ref[rows, :] = l_acc

    out_spec = pl.BlockSpec((None, step, LANES), lambda r, hp, i: (r, i, hp))
    return pl.pallas_call(
        body, name=name, grid=(d, cg, L // step),
        in_specs=[pl.BlockSpec(memory_space=pltpu.SMEM),
                  pl.BlockSpec((None, step, LANES), lambda r, hp, i: (r, i, hp)),
                  pl.BlockSpec((None, L, LANES), lambda r, hp, i: (r, 0, cg + hp)),
                  pl.BlockSpec((None, L, LANES), lambda r, hp, i: (r, 0, 2 * cg + hp))],
        out_specs=[out_spec, out_spec],
        out_shape=[jax.ShapeDtypeStruct((d, L, GROUP_W), F32)] * 2,
        compiler_params=_params(("parallel", "parallel", "arbitrary")),
    )(slopes, qkv, qkv, qkv)


def att_bwd(qkv, do, lse, dmat, group, name):
    d, L, _ = qkv.shape
    step = _att_step(L)
    nq = L // step
    cg = GROUP_W // LANES
    slopes = jnp.asarray(_alibi_slopes()[group])
    scale = HEAD_DIM ** -0.5

    def body(sl_ref, q_ref, k_ref, v_ref, do_ref, l_ref, dm_ref, dq_ref, dk_ref, dv_ref, dk_acc, dv_acc):
        hp = pl.program_id(1)
        i = pl.program_id(2)

        @pl.when(i == 0)
        def _():
            dk_acc[...] = jnp.zeros_like(dk_acc)
            dv_acc[...] = jnp.zeros_like(dv_acc)

        masks = _head_masks()
        for t in range(step // ATT_TQ):
            rows = slice(t * ATT_TQ, (t + 1) * ATT_TQ)
            ks, dist = _att_window(i * step + t * ATT_TQ, L, d)
            q = q_ref[rows, :] * scale
            dov = do_ref[rows, :]
            lse_t = l_ref[rows, :]
            dm_t = dm_ref[rows, :]
            kw = k_ref[pl.ds(ks, ATT_WIN), :]
            vw = v_ref[pl.ds(ks, ATT_WIN), :]
            dq_acc = jnp.zeros((ATT_TQ, LANES), F32)
            dk_new = jnp.zeros((ATT_WIN, LANES), F32)
            dv_new = jnp.zeros((ATT_WIN, LANES), F32)
            for h, hm in enumerate(masks):
                slope = sl_ref[hp * 2 + h]
                qm = jnp.where(hm, q, jnp.zeros_like(q))
                dom = jnp.where(hm, dov, jnp.zeros_like(dov))
                km = jnp.where(hm, kw, jnp.zeros_like(kw))
                s = lax.dot_general(qm, kw, NT_DIMS, preferred_element_type=F32) - slope * dist
                lse_col = jnp.max(jnp.where(hm, lse_t, -jnp.inf), -1, keepdims=True)
                dm_col = jnp.max(jnp.where(hm, dm_t, -jnp.inf), -1, keepdims=True)
                p = jnp.exp(s - lse_col)
                dp = lax.dot_general(dom, vw, NT_DIMS, preferred_element_type=F32)
                ds = (p * (dp - dm_col)).astype(BF16)
                dq_acc = dq_acc + jnp.dot(ds, km, preferred_element_type=F32)
                dk_new = dk_new + lax.dot_general(ds, qm, TN_DIMS, preferred_element_type=F32)
                dv_new = dv_new + lax.dot_general(p.astype(BF16), dom, TN_DIMS, preferred_element_type=F32)
            dq_ref[rows, :] = (dq_acc * scale).astype(BF16)
            dk_acc[pl.ds(ks, ATT_WIN), :] += dk_new
            dv_acc[pl.ds(ks, ATT_WIN), :] += dv_new

        @pl.when(i == nq - 1)
        def _():
            dk_ref[...] = dk_acc[...].astype(BF16)
            dv_ref[...] = dv_acc[...].astype(BF16)

    tile = pl.BlockSpec((None, step, LANES), lambda r, hp, i: (r, i, hp))
    whole = pl.BlockSpec((None, L, LANES), lambda r, hp, i: (r, 0, hp))
    return pl.pallas_call(
        body, name=name, grid=(d, cg, nq),
        in_specs=[pl.BlockSpec(memory_space=pltpu.SMEM), tile,
                  pl.BlockSpec((None, L, LANES), lambda r, hp, i: (r, 0, cg + hp)),
                  pl.BlockSpec((None, L, LANES), lambda r, hp, i: (r, 0, 2 * cg + hp)),
                  tile, tile, tile],
        out_specs=[tile, whole, whole],
        out_shape=[jax.ShapeDtypeStruct((d, L, GROUP_W), BF16)] * 3,
        scratch_shapes=[pltpu.VMEM((L, LANES), F32), pltpu.VMEM((L, LANES), F32)],
        compiler_params=_params(("arbitrary", "arbitrary", "arbitrary")),
    )(slopes, qkv, qkv, qkv, do, lse, dmat)


def _group_weights(ls):
    m = jnp.maximum(jnp.maximum(ls[0], ls[1]), ls[2])
    es = [jnp.exp(l - m) for l in ls]
    tot = es[0] + es[1] + es[2]
    return [e / tot for e in es]


def combine_fwd(outs, lses, name):
    T = outs[0].shape[0] * outs[0].shape[1]
    tm = _pick(T, 512, 8)
    n_scr = 2 * (len(DILATIONS) - 1)

    def body(*refs):
        o_refs, l_refs, c_ref, scr = refs[:3], refs[3:6], refs[6], refs[7:]
        o = [_load_natural(o_refs[g], d, scr[g - 1] if g else None) for g, d in enumerate(DILATIONS)]
        l = [_load_natural(l_refs[g], d, scr[g + 1] if g else None) for g, d in enumerate(DILATIONS)]
        w = _group_weights(l)
        c_ref[...] = (w[0] * o[0] + w[1] * o[1] + w[2] * o[2]).astype(BF16)

    specs = [_residue_spec(tm, d, GROUP_W) for d in DILATIONS]
    return pl.pallas_call(
        body, name=name, grid=(T // tm,),
        in_specs=specs + specs, out_specs=pl.BlockSpec((tm, GROUP_W), lambda i: (i, 0)),
        out_shape=jax.ShapeDtypeStruct((T, GROUP_W), BF16),
        scratch_shapes=[_residue_scratch(tm, GROUP_W)] * n_scr,
        compiler_params=_params(("parallel",)),
    )(*outs, *lses)


def combine_bwd(dcomb, outs, lses, name):
    T = dcomb.shape[0]
    tm = _pick(T, 256, 8)
    head = np.arange(GROUP_W) // HEAD_DIM
    seg = jnp.asarray((head[:, None] == head[None, :]).astype(np.float32)).astype(BF16)
    ng = len(DILATIONS)
    n_scr = 4 * (ng - 1)

    def body(*refs):
        dc_ref, o_refs, l_refs, e_ref = refs[0], refs[1:1 + ng], refs[1 + ng:1 + 2 * ng], refs[1 + 2 * ng]
        do_refs, dm_refs = refs[2 + 2 * ng:2 + 3 * ng], refs[2 + 3 * ng:2 + 4 * ng]
        scr = refs[2 + 4 * ng:]
        o = [_load_natural(o_refs[g], d, scr[4 * (g - 1)] if g else None) for g, d in enumerate(DILATIONS)]
        l = [_load_natural(l_refs[g], d, scr[4 * (g - 1) + 1] if g else None) for g, d in enumerate(DILATIONS)]
        w = _group_weights(l)
        dc = dc_ref[...]
        e = e_ref[...]
        tot = jnp.zeros_like(dc)
        for g in range(ng):
            prod = dc * o[g]
            dw = jnp.zeros_like(dc)
            for _ in range(3):
                part = prod.astype(BF16)
                dw = dw + jnp.dot(part, e, preferred_element_type=F32)
                prod = prod - part.astype(F32)
            tot = tot + w[g] * dw
        for g, d in enumerate(DILATIONS):
            _store_by_residue(w[g] * dc, do_refs[g], d, scr[4 * (g - 1) + 2] if g else None)
            _store_by_residue(w[g] * tot, dm_refs[g], d, scr[4 * (g - 1) + 3] if g else None)

    specs = [_residue_spec(tm, d, GROUP_W) for d in DILATIONS]
    res = pl.pallas_call(
        body, name=name, grid=(T // tm,),
        in_specs=[pl.BlockSpec((tm, GROUP_W), lambda i: (i, 0))] + specs + specs
        + [pl.BlockSpec((GROUP_W, GROUP_W), lambda i: (0, 0))],
        out_specs=specs + specs,
        out_shape=[jax.ShapeDtypeStruct(o.shape, BF16) for o in outs] + [jax.ShapeDtypeStruct(o.shape, F32) for o in outs],
        scratch_shapes=[_residue_scratch(tm, GROUP_W)] * n_scr,
        compiler_params=_params(("parallel",)),
    )(dcomb, *outs, *lses, seg)
    return res[:ng], res[ng:]


def _position():
    return lax.axis_index("x"), lax.axis_index("y"), lax.axis_index("c")


def _other_chips(x, y):
    return [(1 - x, y), (x, 1 - y), (1 - x, 1 - y)]


def _remote(src, dst, send_sems, recv_sems, k, to):
    return pltpu.make_async_remote_copy(src_ref=src, dst_ref=dst, send_sem=send_sems.at[k], recv_sem=recv_sems.at[k],
                                        device_id=to, device_id_type=MESH)


def all_gather(shards, name):
    n = len(shards)

    def body(*refs):
        ins, outs = refs[:n], refs[n:2 * n]
        send_sems, recv_sems, local_sems = refs[2 * n:]
        x, y, c = _position()
        sibling = (x, y, 1 - c)
        chips = _other_chips(x, y)

        def block(a, px, py, pc):
            return outs[a].at[4 * px + 2 * py + pc]

        own, first, passed = [], [], []
        for a in range(n):
            cp = pltpu.make_async_copy(ins[a], block(a, x, y, c), local_sems.at[a])
            cp.start()
            own.append(cp)
            k0 = 7 * a
            first.append(_remote(ins[a], block(a, x, y, c), send_sems, recv_sems, k0, sibling))
            for j, chip in enumerate(chips):
                first.append(_remote(ins[a], block(a, x, y, c), send_sems, recv_sems, k0 + 1 + j, (*chip, c)))
        for cp in first:
            cp.start()
        for a in range(n):
            k0 = 7 * a
            for j, chip in enumerate(chips):
                got = block(a, *chip, c)
                _remote(got, got, send_sems, recv_sems, k0 + 1 + j, sibling).wait_recv()
                fwd = _remote(got, got, send_sems, recv_sems, k0 + 4 + j, sibling)
                fwd.start()
                passed.append(fwd)
        for a in range(n):
            k0 = 7 * a
            got = block(a, x, y, 1 - c)
            _remote(got, got, send_sems, recv_sems, k0, sibling).wait_recv()
            for j, chip in enumerate(chips):
                got = block(a, *chip, 1 - c)
                _remote(got, got, send_sems, recv_sems, k0 + 4 + j, sibling).wait_recv()
        for cp in first + passed:
            cp.wait_send()
        for cp in own:
            cp.wait()

    hbm = pl.BlockSpec(memory_space=pl.ANY)
    return pl.pallas_call(
        body, name=name,
        in_specs=[hbm] * n, out_specs=[hbm] * n,
        out_shape=[jax.ShapeDtypeStruct((N_DEV,) + s.shape, s.dtype) for s in shards],
        scratch_shapes=[pltpu.SemaphoreType.DMA((7 * n,)), pltpu.SemaphoreType.DMA((7 * n,)),
                        pltpu.SemaphoreType.DMA((n,))],
    )(*shards)


def exchange_sibling(parts, name):
    n = len(parts)

    def body(*refs):
        ins, outs = refs[:n], refs[n:2 * n]
        send_sems, recv_sems = refs[2 * n:]
        x, y, c = _position()
        sibling = (x, y, 1 - c)
        copies = []
        for a in range(n):
            for q in range(4):
                cp = _remote(ins[a].at[2 * q + (1 - c)], outs[a].at[q], send_sems, recv_sems, 4 * a + q, sibling)
                cp.start()
                copies.append(cp)
        for cp in copies:
            cp.wait_recv()
        for cp in copies:
            cp.wait_send()

    hbm = pl.BlockSpec(memory_space=pl.ANY)
    return pl.pallas_call(
        body, name=name,
        in_specs=[hbm] * n, out_specs=[hbm] * n,
        out_shape=[jax.ShapeDtypeStruct((4,) + p.shape[1:], p.dtype) for p in parts],
        scratch_shapes=[pltpu.SemaphoreType.DMA((4 * n,)), pltpu.SemaphoreType.DMA((4 * n,))],
    )(*parts)


def exchange_chips(sums, name):
    n = len(sums)

    def body(*refs):
        ins, outs = refs[:n], refs[n:2 * n]
        send_sems, recv_sems = refs[2 * n:]
        x, y, c = _position()
        copies = []
        for a in range(n):
            for j, (cx, cy) in enumerate(_other_chips(x, y)):
                cp = _remote(ins[a].at[2 * cx + cy], outs[a].at[j], send_sems, recv_sems, 3 * a + j, (cx, cy, c))
                cp.start()
                copies.append(cp)
        for cp in copies:
            cp.wait_recv()
        for cp in copies:
            cp.wait_send()

    hbm = pl.BlockSpec(memory_space=pl.ANY)
    return pl.pallas_call(
        body, name=name,
        in_specs=[hbm] * n, out_specs=[hbm] * n,
        out_shape=[jax.ShapeDtypeStruct((3,) + s.shape[1:], s.dtype) for s in sums],
        scratch_shapes=[pltpu.SemaphoreType.DMA((3 * n,)), pltpu.SemaphoreType.DMA((3 * n,))],
    )(*sums)


def all_sum_small(vec, name):
    R = vec.shape[0]

    def body(v_ref, tot_ref, all_ref, send_sems, recv_sems):
        x, y, c = _position()
        me = 4 * x + 2 * y + c
        all_ref[me] = v_ref[...]
        copies = []
        for k in range(1, N_DEV):
            fx, fy, fc = (k >> 2) & 1, (k >> 1) & 1, k & 1
            to = (1 - x if fx else x, 1 - y if fy else y, 1 - c if fc else c)
            cp = _remote(v_ref, all_ref.at[me], send_sems, recv_sems, k - 1, to)
            cp.start()
            copies.append(cp)
        for cp in copies:
            cp.wait_recv()
        for cp in copies:
            cp.wait_send()
        tot = all_ref[0]
        for j in range(1, N_DEV):
            tot = tot + all_ref[j]
        tot_ref[...] = tot

    vmem = pl.BlockSpec(memory_space=pltpu.VMEM)
    return pl.pallas_call(
        body, name=name,
        in_specs=[vmem], out_specs=vmem,
        out_shape=jax.ShapeDtypeStruct((R, LANES), F32),
        scratch_shapes=[pltpu.VMEM((N_DEV, R, LANES), F32),
                        pltpu.SemaphoreType.DMA((N_DEV - 1,)), pltpu.SemaphoreType.DMA((N_DEV - 1,))],
        compiler_params=pltpu.CompilerParams(vmem_limit_bytes=VMEM_LIMIT),
    )(vec)


def pair_add(mine, theirs, name):
    _, R, C = mine.shape
    tr = _pick(R, 256, 8)

    def body(a_ref, b_ref, o_ref):
        o_ref[...] = (a_ref[...].astype(F32) + b_ref[...].astype(F32)).astype(BF16)

    blk = pl.BlockSpec((None, tr, C), lambda q, i: (q, i, 0))
    return pl.pallas_call(
        body, name=name, grid=(4, R // tr),
        in_specs=[blk, blk], out_specs=blk,
        out_shape=jax.ShapeDtypeStruct(mine.shape, BF16),
        compiler_params=_params(("parallel", "parallel")),
    )(mine, theirs)


def _adamw_math(w, g, m, v):
    m = ADAM_B1 * m + (1.0 - ADAM_B1) * g
    v = ADAM_B2 * v + (1.0 - ADAM_B2) * jnp.square(g)
    m_hat = m / (1.0 - ADAM_B1 ** ADAM_STEP)
    v_hat = v / (1.0 - ADAM_B2 ** ADAM_STEP)
    delta = -ADAM_LR * (m_hat / (jnp.sqrt(v_hat) + ADAM_EPS) + ADAM_WD * w)
    return delta, m, v


def adamw_sharded(w, m, v, own, sib, others, name):
    R, C = w.shape
    tr = _pick(R, 256, 8)

    def body(w_ref, m_ref, v_ref, a_ref, b_ref, o_ref, g_ref, d_ref, nm_ref, nv_ref):
        g = a_ref[...].astype(F32) + b_ref[...].astype(F32)
        for j in range(3):
            g = g + o_ref[j].astype(F32)
        delta, nm, nv = _adamw_math(w_ref[...], g, m_ref[...], v_ref[...])
        g_ref[...] = g
        d_ref[...] = delta
        nm_ref[...] = nm
        nv_ref[...] = nv

    row = pl.BlockSpec((tr, C), lambda i: (i, 0))
    return pl.pallas_call(
        body, name=name, grid=(R // tr,),
        in_specs=[row] * 5 + [pl.BlockSpec((3, tr, C), lambda i: (0, i, 0))],
        out_specs=[row] * 4,
        out_shape=[jax.ShapeDtypeStruct((R, C), F32)] * 4,
        compiler_params=_params(("parallel",)),
    )(w, m, v, own, sib, others)


def adamw_packed(w, g, m, v, name):
    R = w.shape[0]

    def body(w_ref, g_ref, m_ref, v_ref, d_ref, nm_ref, nv_ref):
        delta, nm, nv = _adamw_math(w_ref[...], g_ref[...], m_ref[...], v_ref[...])
        d_ref[...] = delta
        nm_ref[...] = nm
        nv_ref[...] = nv

    full = pl.BlockSpec((R, LANES), lambda i: (0, 0))
    return pl.pallas_call(
        body, name=name, grid=(1,),
        in_specs=[full] * 4, out_specs=[full] * 3,
        out_shape=[jax.ShapeDtypeStruct((R, LANES), F32)] * 3,
        compiler_params=_params(("arbitrary",)),
    )(w, g, m, v)


def _pack(arrays):
    flat = []
    sizes = []
    for a in arrays:
        f = a.reshape(-1).astype(F32)
        pad = (-f.shape[0]) % LANES
        if pad:
            f = jnp.concatenate([f, jnp.zeros((pad,), F32)])
        flat.append(f)
        sizes.append(f.shape[0])
    rows = sum(sizes) // LANES
    pad_rows = (-rows) % 8
    if pad_rows:
        flat.append(jnp.zeros((pad_rows * LANES,), F32))
    return jnp.concatenate(flat).reshape(-1, LANES), sizes


def _unpack(packed, sizes, shapes):
    flat = packed.reshape(-1)
    out = []
    off = 0
    for size, shape in zip(sizes, shapes):
        n = int(np.prod(shape))
        out.append(flat[off:off + n].reshape(shape))
        off += size
    return out


def _to_blocks(full, axis):
    if axis == 0:
        return full.reshape(N_DEV, full.shape[0] // N_DEV, full.shape[1])
    r, n = full.shape
    return full.reshape(r, N_DEV, n // N_DEV).transpose(1, 0, 2)


def _from_blocks(blocks, axis):
    if axis == 0:
        return blocks.reshape(blocks.shape[0] * blocks.shape[1], blocks.shape[2])
    return blocks.transpose(1, 0, 2).reshape(blocks.shape[1], blocks.shape[0] * blocks.shape[2])


def kernel(x, ln0_g, ln0_b, w_in, b_in, conv_w, w_a, w_b, w_o, b_o, ln1_g, ln1_b, w_up, b_up, ffn_conv_w, ffn_conv_b, w_down, b_down, ln2_g, ln2_b, loss_target, m_ln0_g, m_ln0_b, m_w_in, m_b_in, m_conv_w, m_w_a, m_w_b, m_w_o, m_b_o, m_ln1_g, m_ln1_b, m_w_up, m_b_up, m_ffn_conv_w, m_ffn_conv_b, m_w_down, m_b_down, m_ln2_g, m_ln2_b, v_ln0_g, v_ln0_b, v_w_in, v_b_in, v_conv_w, v_w_a, v_w_b, v_w_o, v_b_o, v_ln1_g, v_ln1_b, v_w_up, v_b_up, v_ffn_conv_w, v_ffn_conv_b, v_w_down, v_b_down, v_ln2_g, v_ln2_b):
    T, D = x.shape[1], x.shape[2]
    F = ffn_conv_b.shape[-1]
    xs = x.reshape(T, D)
    tgt = loss_target.reshape(T, D)
    dev = 4 * lax.axis_index("x") + 2 * lax.axis_index("y") + lax.axis_index("c")
    chip = 2 * lax.axis_index("x") + lax.axis_index("y")
    core = lax.axis_index("c")

    big = dict(w_in=(w_in[0], 1), w_a=(w_a[0], 0), w_b=(w_b[0], 1), w_o=(w_o[0], 0), w_up=(w_up[0], 1),
               w_down=(w_down[0], 0))
    names = list(big)
    gathered = all_gather([big[k][0].astype(BF16) for k in names] + [conv_w[0], ffn_conv_w[0]], "gather_weights")
    full = {k: _from_blocks(g, big[k][1]) for k, g in zip(names, gathered)}
    conv_full = _from_blocks(gathered[-2], 1)
    fcw_full = _from_blocks(gathered[-1], 1)
    o_q = 3 * D
    o_g = o_q + 3 * QKV_W
    w_pa, w_qkv, w_pg = full["w_in"][:, :o_q], full["w_in"][:, o_q:o_g], full["w_in"][:, o_g:]
    b_pa, b_qkv, b_pg = b_in[:, :o_q], b_in[:, o_q:o_g], b_in[:, o_g:]
    ln0g, ln0b = ln0_g.reshape(1, D), ln0_b.reshape(1, D)

    h0, h0b, *h0_res = ln_fwd(xs, None, ln0g, ln0b, "ln0_fwd", dilations=DILATIONS[1:])
    h0_res = [h0b] + [h.reshape(T, D) for h in h0_res]
    proj_a = mm_nn(h0b, w_pa, b_pa, F32, "proj_conv")
    proj_g = mm_nn(h0b, w_pg, b_pg, F32, "proj_gates")
    zero_d = jnp.zeros((1, D), F32)
    s_a = conv_a_fwd(proj_a, conv_full, "conv_a_fwd")
    y_a = mm_nn(s_a, full["w_a"], zero_d, F32, "branch_a_out")

    def group_cols(m, g):
        return jnp.concatenate([m[:, s * QKV_W + g * GROUP_W:s * QKV_W + (g + 1) * GROUP_W] for s in range(3)], 1)

    w_grp = [group_cols(w_qkv, g) for g in range(3)]
    qkvs, outs, lses = [], [], []
    for g, d in enumerate(DILATIONS):
        qkv = mm_nn(h0_res[g], w_grp[g], group_cols(b_qkv, g), BF16, f"proj_qkv_{g}").reshape(d, T // d, 3 * GROUP_W)
        o, l = att_fwd(qkv, g, f"att_fwd_{g}")
        qkvs.append(qkv)
        outs.append(o)
        lses.append(l)
    comb = combine_fwd(outs, lses, "combine_fwd")
    y_b = mm_nn(comb, full["w_b"], zero_d, F32, "branch_b_out")
    z = gate_fwd(proj_g, y_a, y_b, "gate_fwd")
    mix = mm_nn(z, full["w_o"], b_o, F32, "mix_out")
    h1, h1b = ln_fwd(h0, mix, ln1_g, ln1_b, "ln1_fwd")
    up = mm_nn(h1b, full["w_up"], b_up, F32, "ffn_up")
    f_act = conv_f_fwd(up, fcw_full, ffn_conv_b, "conv_f_fwd")
    ffn = mm_nn(f_act, full["w_down"], b_down, F32, "ffn_down")

    dr2, dr2b, d_ln2_g, d_ln2_b, d_b_down, loss_part = ln_bwd(h1, ffn, ln2_g, ln2_b, None, None, tgt, "ln2_loss_bwd")
    dw_down, _ = mm_tn(f_act, dr2b, "dw_down")
    df = mm_nt(dr2b, full["w_down"], None, "d_ffn_act")
    d_a, d_gate, cs_a, cs_gate, d_fcb, d_fcw = conv_f_bwd(df, up, fcw_full, ffn_conv_b, "conv_f_bwd")
    dw_up_a, _ = mm_tn(h1b, d_a, "dw_up_a")
    dw_up_g, _ = mm_tn(h1b, d_gate, "dw_up_gate")
    dh1 = mm_nt(d_a, full["w_up"][:, :F], None, "d_h1_a")
    dh1 = mm_nt(d_gate, full["w_up"][:, F:], dh1, "d_h1_gate")
    dr1, dr1b, d_ln1_g, d_ln1_b, d_b_o, _ = ln_bwd(h0, mix, ln1_g, ln1_b, dr2, dh1, None, "ln1_bwd")
    dw_o, _ = mm_tn(z, dr1b, "dw_o")
    dz = mm_nt(dr1b, full["w_o"], None, "d_z")
    dy_a, dy_b, dproj_g = gate_bwd(dz, proj_g, y_a, y_b, "gate_bwd")
    dw_a, _ = mm_tn(s_a, dy_a, "dw_a")
    ds_a = mm_nt(dy_a, full["w_a"], None, "d_s_a")
    dproj_a, d_conv = conv_a_bwd(ds_a, proj_a, conv_full, "conv_a_bwd")
    dw_b, _ = mm_tn(comb, dy_b, "dw_b")
    dcomb = mm_nt(dy_b, full["w_b"], None, "d_comb")
    dos, dms = combine_bwd(dcomb, outs, lses, "combine_bwd")
    dw_pa, cs_pa = mm_tn(h0b, dproj_a, "dw_in_conv")
    dw_pg, cs_pg = mm_tn(h0b, dproj_g, "dw_in_gates")
    dh0 = mm_nt(dproj_a, w_pa, None, "d_h0_conv")
    dh0 = mm_nt(dproj_g, w_pg, dh0, "d_h0_gates")
    dw_grp, cs_grp, dh0_res = [], [], []
    for g, d in enumerate(DILATIONS):
        dq, dk, dv = att_bwd(qkvs[g], dos[g], lses[g], dms[g], g, f"att_bwd_{g}")
        dqkv = jnp.concatenate([dq, dk, dv], -1).reshape(T, 3 * GROUP_W)
        dwg, csg = mm_tn(h0_res[g], dqkv, f"dw_in_qkv_{g}")
        dw_grp.append(dwg)
        cs_grp.append(csg)
        if g == 0:
            dh0 = mm_nt(dqkv, w_grp[g], dh0, f"d_h0_qkv_{g}")
        else:
            dh0_res.append((mm_nt(dqkv, w_grp[g], None, f"d_h0_qkv_{g}").reshape(d, T // d, D), d))

    def ungroup(parts):
        return jnp.concatenate([p[:, s * GROUP_W:(s + 1) * GROUP_W] for s in range(3) for p in parts], 1)

    dw_in_parts = [dw_pa, ungroup(dw_grp), dw_pg]
    db_in_parts = [cs_pa, ungroup(cs_grp), cs_pg]
    dx, _, d_ln0_g, d_ln0_b, _, _ = ln_bwd(xs, None, ln0g, ln0b, dr1, dh0, None, "ln0_bwd", by_residue=dh0_res)

    small = [d_ln0_g, d_ln0_b, jnp.concatenate(db_in_parts, 1), d_conv, d_b_o, d_ln1_g, d_ln1_b,
             jnp.concatenate([cs_a, cs_gate], 1), d_fcw, d_fcb, d_b_down, d_ln2_g, d_ln2_b, loss_part]
    packed, sizes = _pack(small)
    total = all_sum_small(packed, "sum_small")
    (g_ln0_g, g_ln0_b, g_b_in, g_conv_full, g_b_o, g_ln1_g, g_ln1_b, g_b_up, g_fcw_full, g_fcb, g_b_down, g_ln2_g,
     g_ln2_b, loss) = _unpack(total, sizes, [a.shape for a in small])
    cw = conv_w.shape[-1]
    fw = ffn_conv_w.shape[-1]
    g_conv = lax.dynamic_slice_in_dim(g_conv_full, dev * cw, cw, 1)
    g_fcw = lax.dynamic_slice_in_dim(g_fcw_full, dev * fw, fw, 1)

    dw_full = dict(w_in=jnp.concatenate(dw_in_parts, 1), w_a=dw_a, w_b=dw_b, w_o=dw_o,
                   w_up=jnp.concatenate([dw_up_a, dw_up_g], 1), w_down=dw_down)
    parts = [_to_blocks(dw_full[k], big[k][1]) for k in names]
    from_sib = exchange_sibling(parts, "grads_to_sibling")
    mine = [lax.dynamic_index_in_dim(p.reshape((4, 2) + p.shape[1:]), core, 1, keepdims=False) for p in parts]
    chip_sums = [pair_add(a, b, f"chip_sum_{k}") for k, a, b in zip(names, mine, from_sib)]
    from_chips = exchange_chips(chip_sums, "grads_to_chips")

    moments = dict(w_in=(m_w_in, v_w_in), w_a=(m_w_a, v_w_a), w_b=(m_w_b, v_w_b), w_o=(m_w_o, v_w_o),
                   w_up=(m_w_up, v_w_up), w_down=(m_w_down, v_w_down))
    res_big = {}
    for k, a, b, o in zip(names, mine, from_sib, from_chips):
        own = lax.dynamic_index_in_dim(a, chip, 0, keepdims=False)
        sib = lax.dynamic_index_in_dim(b, chip, 0, keepdims=False)
        res_big[k] = adamw_sharded(big[k][0], moments[k][0][0], moments[k][1][0], own, sib, o, f"adamw_{k}")

    small_names = ["ln0_g", "ln0_b", "b_in", "conv_w", "b_o", "ln1_g", "ln1_b", "b_up", "ffn_conv_w", "ffn_conv_b",
                   "b_down", "ln2_g", "ln2_b"]
    small_w = [ln0_g, ln0_b, b_in, conv_w, b_o, ln1_g, ln1_b, b_up, ffn_conv_w, ffn_conv_b, b_down, ln2_g, ln2_b]
    small_m = [m_ln0_g, m_ln0_b, m_b_in, m_conv_w, m_b_o, m_ln1_g, m_ln1_b, m_b_up, m_ffn_conv_w, m_ffn_conv_b,
               m_b_down, m_ln2_g, m_ln2_b]
    small_v = [v_ln0_g, v_ln0_b, v_b_in, v_conv_w, v_b_o, v_ln1_g, v_ln1_b, v_b_up, v_ffn_conv_w, v_ffn_conv_b,
               v_b_down, v_ln2_g, v_ln2_b]
    small_g = [g_ln0_g, g_ln0_b, g_b_in, g_conv, g_b_o, g_ln1_g, g_ln1_b, g_b_up, g_fcw, g_fcb, g_b_down, g_ln2_g,
               g_ln2_b]
    shapes = [w.shape for w in small_w]
    small_g = [g.reshape(s) for g, s in zip(small_g, shapes)]
    pw, psz = _pack(small_w)
    pg, _ = _pack(small_g)
    pm, _ = _pack(small_m)
    pv, _ = _pack(small_v)
    pd, pnm, pnv = adamw_packed(pw, pg, pm, pv, "adamw_small")
    res_small = {k: (g, d_, m_, v_) for k, g, d_, m_, v_ in zip(
        small_names, small_g, _unpack(pd, psz, shapes), _unpack(pnm, psz, shapes), _unpack(pnv, psz, shapes))}

    order = ["ln0_g", "ln0_b", "w_in", "b_in", "conv_w", "w_a", "w_b", "w_o", "b_o", "ln1_g", "ln1_b", "w_up", "b_up",
             "ffn_conv_w", "ffn_conv_b", "w_down", "b_down", "ln2_g", "ln2_b"]

    def result(k, j):
        if k in res_big:
            return res_big[k][j][None]
        return res_small[k][j]

    out = [loss.reshape(()), dx.reshape(x.shape)]
    for j in range(4):
        out += [result(k, j) for k in order]
    return tuple(out)
```

```python
import functools
import math

import numpy as np
import jax
import jax.numpy as jnp
from jax import lax
from jax.experimental import pallas as pl
from jax.experimental.pallas import tpu as pltpu

F32 = jnp.float32
BF16 = jnp.bfloat16

N_DEV = 8
LN_EPS = 1e-5
ALPHA = (2.0 * 1) ** 0.25
MASK_VALUE = -1e30
HEAD_DIM = 64
GROUP_W = 512
QKV_W = 3 * GROUP_W
DILATIONS = (1, 4, 16)
RADIUS = 64
LANES = 128
HALO = 8
ATT_TQ = 128

ADAM_LR = 0.001
ADAM_B1 = 0.9
ADAM_B2 = 0.999
ADAM_EPS = 1e-08
ADAM_WD = 0.01
ADAM_STEP = 10

VMEM_LIMIT = 52 * 1024 * 1024
OUT_TILE_BYTES = 8 * 1024 * 1024
MESH = pl.DeviceIdType.MESH
NT_DIMS = (((1,), (1,)), ((), ()))
TN_DIMS = (((0,), (0,)), ((), ()))


def _pick(n, target, align=LANES):
    if n <= target:
        return n
    best = None
    for t in range(align, target + 1, align):
        if n % t == 0:
            best = t
    assert best is not None, (n, target, align)
    return best


def _params(sems=None):
    return pltpu.CompilerParams(dimension_semantics=sems, vmem_limit_bytes=VMEM_LIMIT)


def _alibi_slopes():
    n = 3 * 8
    return np.exp2(-8.0 * np.arange(1, n + 1, dtype=np.float64) / n).astype(np.float32).reshape(3, 8)


def _ln_stats(r):
    mu = jnp.mean(r, -1, keepdims=True)
    xc = r - mu
    var = jnp.mean(xc * xc, -1, keepdims=True)
    rstd = lax.rsqrt(var + LN_EPS)
    return xc, rstd


def _load_natural(ref, d, scr):
    if d == 1:
        return ref[0]
    n, C = ref.shape[1], ref.shape[2]
    for c in range(C // LANES):
        for r in range(d):
            scr[c, pl.ds(r, n, stride=d), :] = ref[r, :, c * LANES:(c + 1) * LANES]
    return jnp.concatenate([scr[c] for c in range(C // LANES)], axis=1)


def _store_by_residue(val, ref, d, scr):
    if d == 1:
        ref[0] = val.astype(ref.dtype)
        return
    n, C = ref.shape[1], ref.shape[2]
    for c in range(C // LANES):
        scr[c] = val[:, c * LANES:(c + 1) * LANES]
    for c in range(C // LANES):
        for r in range(d):
            ref[r, :, c * LANES:(c + 1) * LANES] = scr[c, pl.ds(r, n, stride=d), :].astype(ref.dtype)


def _residue_spec(tm, d, C):
    return pl.BlockSpec((d, tm // d, C), lambda i: (0, i, 0))


def _residue_scratch(tm, C):
    return pltpu.VMEM((C // LANES, tm, LANES), F32)


def ln_fwd(a, res, g, b, name, dilations=()):
    T, D = a.shape
    tm = _pick(T, 512, 8)
    has_res = res is not None
    nd = len(dilations)

    def body(*refs):
        refs = list(refs)
        a_ref = refs.pop(0)
        r = a_ref[...]
        if has_res:
            r = ALPHA * r + refs.pop(0)[...]
        g_ref, b_ref, h_ref, hb_ref = refs[:4]
        xc, rstd = _ln_stats(r)
        h = xc * rstd * g_ref[...] + b_ref[...]
        h_ref[...] = h
        hb_ref[...] = h.astype(BF16)
        for d, p_ref in zip(dilations, refs[4:4 + nd]):
            _store_by_residue(h, p_ref, d, refs[-1])

    row = pl.BlockSpec((tm, D), lambda i: (i, 0))
    vec = pl.BlockSpec((1, D), lambda i: (0, 0))
    ins = [a] + ([res] if has_res else []) + [g, b]
    return pl.pallas_call(
        body, name=name, grid=(T // tm,),
        in_specs=[row] * (2 if has_res else 1) + [vec, vec],
        out_specs=[row, row] + [_residue_spec(tm, d, D) for d in dilations],
        out_shape=[jax.ShapeDtypeStruct((T, D), F32), jax.ShapeDtypeStruct((T, D), BF16)]
        + [jax.ShapeDtypeStruct((d, T // d, D), BF16) for d in dilations],
        scratch_shapes=[_residue_scratch(tm, D)] if nd else [],
        compiler_params=_params(("parallel",)),
    )(*ins)


def ln_bwd(a, res, g, b, d1, d2, tgt, name, by_residue=()):
    T, D = a.shape
    tm = _pick(T, 256, 8)
    has_res = res is not None
    loss_mode = tgt is not None
    nres = len(by_residue)

    def body(*refs):
        refs = list(refs)
        a_ref = refs.pop(0)
        r_ref = refs.pop(0) if has_res else None
        g_ref = refs.pop(0)
        b_ref = refs.pop(0)
        if loss_mode:
            t_ref = refs.pop(0)
        else:
            d1_ref = refs.pop(0)
            d2_ref = refs.pop(0)
        e_refs = [refs.pop(0) for _ in range(nres)]
        dr_ref, drb_ref, dg_ref, db_ref, ds_ref, loss_ref = refs[:6]
        i = pl.program_id(0)

        @pl.when(i == 0)
        def _():
            dg_ref[...] = jnp.zeros_like(dg_ref)
            db_ref[...] = jnp.zeros_like(db_ref)
            ds_ref[...] = jnp.zeros_like(ds_ref)
            loss_ref[...] = jnp.zeros_like(loss_ref)

        r = a_ref[...]
        if has_res:
            r = ALPHA * r + r_ref[...]
        xc, rstd = _ln_stats(r)
        xhat = xc * rstd
        gam = g_ref[...]
        if loss_mode:
            err = xhat * gam + b_ref[...] - t_ref[...]
            dy = err * (1.0 / D)
            row_loss = jnp.mean(err * err, -1, keepdims=True)
            loss_ref[...] += 0.5 * jnp.sum(row_loss, 0, keepdims=True)
        else:
            dy = ALPHA * d1_ref[...] + d2_ref[...]
        for (_, d), e_ref in zip(by_residue, e_refs):
            dy = dy + _load_natural(e_ref, d, refs[-1])
        dyg = dy * gam
        c1 = jnp.mean(dyg, -1, keepdims=True)
        c2 = jnp.mean(dyg * xhat, -1, keepdims=True)
        dr = rstd * (dyg - c1 - xhat * c2)
        dr_ref[...] = dr
        drb_ref[...] = dr.astype(BF16)
        dg_ref[...] += jnp.sum(dy * xhat, 0, keepdims=True)
        db_ref[...] += jnp.sum(dy, 0, keepdims=True)
        ds_ref[...] += jnp.sum(dr, 0, keepdims=True)

    row = pl.BlockSpec((tm, D), lambda i: (i, 0))
    vec = pl.BlockSpec((1, D), lambda i: (0, 0))
    one = pl.BlockSpec((1, 1), lambda i: (0, 0))
    ins = [a] + ([res] if has_res else []) + [g, b] + ([tgt] if loss_mode else [d1, d2]) + [e for e, _ in by_residue]
    in_specs = [row] * (2 if has_res else 1) + [vec, vec] + [row] * (1 if loss_mode else 2)
    in_specs += [_residue_spec(tm, d, D) for _, d in by_residue]
    return pl.pallas_call(
        body, name=name, grid=(T // tm,),
        in_specs=in_specs,
        out_specs=[row, row, vec, vec, vec, one],
        out_shape=[jax.ShapeDtypeStruct((T, D), F32), jax.ShapeDtypeStruct((T, D), BF16),
                   jax.ShapeDtypeStruct((1, D), F32), jax.ShapeDtypeStruct((1, D), F32),
                   jax.ShapeDtypeStruct((1, D), F32), jax.ShapeDtypeStruct((1, 1), F32)],
        scratch_shapes=[_residue_scratch(tm, D)] if nres else [],
        compiler_params=_params(("arbitrary",)),
    )(*ins)


_TOKEN_SPEC = pl.BlockSpec((8, LANES), lambda i: (0, 0))


def mm_nn(a, w, bias, out_dtype, name, after=None):
    M, K = a.shape
    N = w.shape[1]
    tm = _pick(M, max(256, min(1024, OUT_TILE_BYTES // (N * jnp.dtype(out_dtype).itemsize))), 8)
    tc = _pick(N, 512)

    def body(a_ref, w_ref, b_ref, *rest):
        o_ref = rest[-1]
        av = a_ref[...]
        for j in range(N // tc):
            cols = slice(j * tc, (j + 1) * tc)
            acc = jnp.dot(av, w_ref[:, cols], preferred_element_type=F32)
            o_ref[:, cols] = (acc + b_ref[:, cols]).astype(out_dtype)

    return pl.pallas_call(
        body, name=name, grid=(M // tm,),
        in_specs=[pl.BlockSpec((tm, K), lambda i: (i, 0)),
                  pl.BlockSpec((K, N), lambda i: (0, 0)),
                  pl.BlockSpec((1, N), lambda i: (0, 0))] + ([] if after is None else [_TOKEN_SPEC]),
        out_specs=pl.BlockSpec((tm, N), lambda i: (i, 0)),
        out_shape=jax.ShapeDtypeStruct((M, N), out_dtype),
        compiler_params=_params(("parallel",)),
    )(a, w, bias, *([] if after is None else [after]))


def mm_nt(a, w, acc_in, name, after=None):
    M, K = a.shape
    N = w.shape[0]
    tm = _pick(M, 512, 8)
    tc = _pick(N, 512)
    has_acc = acc_in is not None

    def body(*refs):
        a_ref, w_ref = refs[:2]
        c_ref = refs[2] if has_acc else None
        o_ref = refs[-1]
        av = a_ref[...]
        for j in range(N // tc):
            cols = slice(j * tc, (j + 1) * tc)
            acc = lax.dot_general(av, w_ref[cols, :], NT_DIMS, preferred_element_type=F32)
            if has_acc:
                acc = acc + c_ref[:, cols]
            o_ref[:, cols] = acc

    out_spec = pl.BlockSpec((tm, N), lambda i: (i, 0))
    in_specs = [pl.BlockSpec((tm, K), lambda i: (i, 0)),
                pl.BlockSpec((N, K), lambda i: (0, 0))]
    ins = [a, w]
    if has_acc:
        in_specs.append(out_spec)
        ins.append(acc_in)
    if after is not None:
        in_specs.append(_TOKEN_SPEC)
        ins.append(after)
    return pl.pallas_call(
        body, name=name, grid=(M // tm,),
        in_specs=in_specs, out_specs=out_spec,
        out_shape=jax.ShapeDtypeStruct((M, N), F32),
        compiler_params=_params(("parallel",)),
    )(*ins)


def mm_tn(a, b, name, out_dtype=BF16):
    T, M = a.shape
    N = b.shape[1]
    tm = _pick(M, 1408)
    tn = _pick(N, 2560 if tm <= 1024 else 1024)
    tk = _pick(T, 512, 8)
    nk = T // tk

    def body(a_ref, b_ref, o_ref, cs_ref, acc_ref):
        m = pl.program_id(1)
        k = pl.program_id(2)

        @pl.when(k == 0)
        def _():
            acc_ref[...] = jnp.zeros_like(acc_ref)

        @pl.when((k == 0) & (m == 0))
        def _():
            cs_ref[...] = jnp.zeros_like(cs_ref)

        bv = b_ref[...]
        acc_ref[...] += lax.dot_general(a_ref[...], bv, TN_DIMS, preferred_element_type=F32)

        @pl.when(m == 0)
        def _():
            cs_ref[...] += jnp.sum(bv.astype(F32), 0, keepdims=True)

        @pl.when(k == nk - 1)
        def _():
            o_ref[...] = acc_ref[...].astype(out_dtype)

    return pl.pallas_call(
        body, name=name, grid=(N // tn, M // tm, nk),
        in_specs=[pl.BlockSpec((tk, tm), lambda n, m, k: (k, m)),
                  pl.BlockSpec((tk, tn), lambda n, m, k: (k, n))],
        out_specs=[pl.BlockSpec((tm, tn), lambda n, m, k: (m, n)),
                   pl.BlockSpec((1, tn), lambda n, m, k: (0, n))],
        out_shape=[jax.ShapeDtypeStruct((M, N), out_dtype), jax.ShapeDtypeStruct((1, N), F32)],
        scratch_shapes=[pltpu.VMEM((tm, tn), F32)],
        compiler_params=_params(("arbitrary", "arbitrary", "arbitrary")),
    )(a, b)


def _ext_rows(prev_ref, main_ref, next_ref, i, tm, T):
    before = jnp.where(i == 0, 0.0, prev_ref[...])
    after = jnp.where(i == T // tm - 1, 0.0, next_ref[...])
    return jnp.concatenate([before, main_ref[...], after], axis=0)


def _prev_row(x):
    return pltpu.roll(x, 1, 0)


def _next_row(x):
    return pltpu.roll(x, x.shape[0] - 1, 0)


def _conv3(u, w_ref):
    return _prev_row(u) * w_ref[0:1, :] + u * w_ref[1:2, :] + _next_row(u) * w_ref[2:3, :]


def _main(x, tm):
    return x[HALO:HALO + tm]


def _halo_specs(tm, tc, T, col, order):
    r = tm // HALO
    last = T // HALO - 1
    if order == "ij":
        return (pl.BlockSpec((HALO, tc), lambda i, j: (jnp.maximum(i * r - 1, 0), col(j))),
                pl.BlockSpec((tm, tc), lambda i, j: (i, col(j))),
                pl.BlockSpec((HALO, tc), lambda i, j: (jnp.minimum((i + 1) * r, last), col(j))))
    return (pl.BlockSpec((HALO, tc), lambda j, i: (jnp.maximum(i * r - 1, 0), col(j))),
            pl.BlockSpec((tm, tc), lambda j, i: (i, col(j))),
            pl.BlockSpec((HALO, tc), lambda j, i: (jnp.minimum((i + 1) * r, last), col(j))))


def conv_a_fwd(proj_a, conv_w, name):
    T, D3 = proj_a.shape
    D = D3 // 3
    tm = _pick(T, 256, 8)

    def body(p_ref, m_ref, n_ref, w_ref, o_ref):
        i = pl.program_id(0)
        ext = _ext_rows(p_ref, m_ref, n_ref, i, tm, T)
        u = ext[:, D:2 * D] * ext[:, 2 * D:]
        cu = _conv3(u, w_ref)
        o_ref[...] = (m_ref[:, :D] * _main(cu, tm)).astype(BF16)

    prev, main, nxt = _halo_specs(tm, D3, T, lambda j: 0, "ij")
    return pl.pallas_call(
        body, name=name, grid=(T // tm, 1),
        in_specs=[prev, main, nxt, pl.BlockSpec((3, D), lambda i, j: (0, 0))],
        out_specs=pl.BlockSpec((tm, D), lambda i, j: (i, 0)),
        out_shape=jax.ShapeDtypeStruct((T, D), BF16),
        compiler_params=_params(("parallel", "arbitrary")),
    )(proj_a, proj_a, proj_a, conv_w)


def conv_a_bwd(ds_a, proj_a, conv_w, name):
    T, D3 = proj_a.shape
    D = D3 // 3
    tm = _pick(T, 256, 8)

    def body(dp_ref, dm_ref, dn_ref, p_ref, m_ref, n_ref, w_ref, o_ref, dw_ref):
        i = pl.program_id(0)

        @pl.when(i == 0)
        def _():
            dw_ref[...] = jnp.zeros_like(dw_ref)

        ext = _ext_rows(p_ref, m_ref, n_ref, i, tm, T)
        dsa = _ext_rows(dp_ref, dm_ref, dn_ref, i, tm, T)
        gb, gc, hin = ext[:, :D], ext[:, D:2 * D], ext[:, 2 * D:]
        u = gc * hin
        u_prev, u_next = _prev_row(u), _next_row(u)
        cu = u_prev * w_ref[0:1, :] + u * w_ref[1:2, :] + u_next * w_ref[2:3, :]
        dcu = dsa * gb
        du = _next_row(dcu) * w_ref[0:1, :] + dcu * w_ref[1:2, :] + _prev_row(dcu) * w_ref[2:3, :]
        o_ref[:, :D] = _main(dsa * cu, tm).astype(BF16)
        o_ref[:, D:2 * D] = _main(du * hin, tm).astype(BF16)
        o_ref[:, 2 * D:] = _main(du * gc, tm).astype(BF16)
        dcu_m = _main(dcu, tm)
        dw_ref[0:1, :] += jnp.sum(dcu_m * _main(u_prev, tm), 0, keepdims=True)
        dw_ref[1:2, :] += jnp.sum(dcu_m * _main(u, tm), 0, keepdims=True)
        dw_ref[2:3, :] += jnp.sum(dcu_m * _main(u_next, tm), 0, keepdims=True)

    dprev, dmain, dnxt = _halo_specs(tm, D, T, lambda j: 0, "ij")
    prev, main, nxt = _halo_specs(tm, D3, T, lambda j: 0, "ij")
    return pl.pallas_call(
        body, name=name, grid=(T // tm, 1),
        in_specs=[dprev, dmain, dnxt, prev, main, nxt, pl.BlockSpec((3, D), lambda i, j: (0, 0))],
        out_specs=[pl.BlockSpec((tm, D3), lambda i, j: (i, 0)), pl.BlockSpec((3, D), lambda i, j: (0, 0))],
        out_shape=[jax.ShapeDtypeStruct((T, D3), BF16), jax.ShapeDtypeStruct((3, D), F32)],
        compiler_params=_params(("arbitrary", "arbitrary")),
    )(ds_a, ds_a, ds_a, proj_a, proj_a, proj_a, conv_w)


_INV_SQRT2 = 1.0 / math.sqrt(2.0)
_INV_SQRT_2PI = 1.0 / math.sqrt(2.0 * math.pi)


def conv_f_fwd(up, fcw, fcb, name):
    T, F2 = up.shape
    F = F2 // 2
    tm = _pick(T, 512, 8)
    tc = _pick(F, 256)
    nc = F // tc

    def body(p_ref, m_ref, n_ref, g_ref, w_ref, b_ref, o_ref):
        i = pl.program_id(0)
        a = _ext_rows(p_ref, m_ref, n_ref, i, tm, T)
        ca = _main(_conv3(a, w_ref), tm) + b_ref[...]
        gl = 0.5 * ca * (1.0 + lax.erf(ca * _INV_SQRT2))
        o_ref[...] = (gl * g_ref[...]).astype(BF16)

    prev, main, nxt = _halo_specs(tm, tc, T, lambda j: j, "ij")
    return pl.pallas_call(
        body, name=name, grid=(T // tm, nc),
        in_specs=[prev, main, nxt,
                  pl.BlockSpec((tm, tc), lambda i, j: (i, nc + j)),
                  pl.BlockSpec((3, tc), lambda i, j: (0, j)),
                  pl.BlockSpec((1, tc), lambda i, j: (0, j))],
        out_specs=pl.BlockSpec((tm, tc), lambda i, j: (i, j)),
        out_shape=jax.ShapeDtypeStruct((T, F), BF16),
        compiler_params=_params(("parallel", "parallel")),
    )(up, up, up, up, fcw, fcb)


def conv_f_bwd(df, up, fcw, fcb, name):
    T, F2 = up.shape
    F = F2 // 2
    tm = _pick(T, 512, 8)
    tc = _pick(F, 256)
    nc = F // tc

    def body(fp_ref, fm_ref, fn_ref, ap_ref, am_ref, an_ref, gp_ref, gm_ref, gn_ref, w_ref, b_ref,
             da_ref, dg_ref, csa_ref, csg_ref, dfb_ref, dfw_ref):
        i = pl.program_id(1)

        @pl.when(i == 0)
        def _():
            csa_ref[...] = jnp.zeros_like(csa_ref)
            csg_ref[...] = jnp.zeros_like(csg_ref)
            dfb_ref[...] = jnp.zeros_like(dfb_ref)
            dfw_ref[...] = jnp.zeros_like(dfw_ref)

        dfe = _ext_rows(fp_ref, fm_ref, fn_ref, i, tm, T)
        a = _ext_rows(ap_ref, am_ref, an_ref, i, tm, T)
        gate = _ext_rows(gp_ref, gm_ref, gn_ref, i, tm, T)
        a_prev, a_next = _prev_row(a), _next_row(a)
        ca = a_prev * w_ref[0:1, :] + a * w_ref[1:2, :] + a_next * w_ref[2:3, :] + b_ref[...]
        cdf = 0.5 * (1.0 + lax.erf(ca * _INV_SQRT2))
        gl = ca * cdf
        gp = cdf + ca * (jnp.exp(-0.5 * ca * ca) * _INV_SQRT_2PI)
        dgate = _main(dfe * gl, tm)
        dca = dfe * gate * gp
        da = _main(_next_row(dca) * w_ref[0:1, :] + dca * w_ref[1:2, :] + _prev_row(dca) * w_ref[2:3, :], tm)
        da_ref[...] = da.astype(BF16)
        dg_ref[...] = dgate.astype(BF16)
        csa_ref[...] += jnp.sum(da, 0, keepdims=True)
        csg_ref[...] += jnp.sum(dgate, 0, keepdims=True)
        dca_m = _main(dca, tm)
        dfb_ref[...] += jnp.sum(dca_m, 0, keepdims=True)
        dfw_ref[0:1, :] += jnp.sum(dca_m * _main(a_prev, tm), 0, keepdims=True)
        dfw_ref[1:2, :] += jnp.sum(dca_m * _main(a, tm), 0, keepdims=True)
        dfw_ref[2:3, :] += jnp.sum(dca_m * _main(a_next, tm), 0, keepdims=True)

    fprev, fmain, fnxt = _halo_specs(tm, tc, T, lambda j: j, "ji")
    gprev, gmain, gnxt = _halo_specs(tm, tc, T, lambda j: nc + j, "ji")
    tile = pl.BlockSpec((tm, tc), lambda j, i: (i, j))
    vec = pl.BlockSpec((1, tc), lambda j, i: (0, j))
    vec3 = pl.BlockSpec((3, tc), lambda j, i: (0, j))
    return pl.pallas_call(
        body, name=name, grid=(nc, T // tm),
        in_specs=[fprev, fmain, fnxt, fprev, fmain, fnxt, gprev, gmain, gnxt, vec3, vec],
        out_specs=[tile, tile, vec, vec, vec, vec3],
        out_shape=[jax.ShapeDtypeStruct((T, F), BF16), jax.ShapeDtypeStruct((T, F), BF16),
                   jax.ShapeDtypeStruct((1, F), F32), jax.ShapeDtypeStruct((1, F), F32),
                   jax.ShapeDtypeStruct((1, F), F32), jax.ShapeDtypeStruct((3, F), F32)],
        compiler_params=_params(("arbitrary", "arbitrary")),
    )(df, df, df, up, up, up, up, up, up, fcw, fcb)


def gate_fwd(proj_g, y_a, y_b, name):
    T, D = y_a.shape
    tm = _pick(T, 512, 8)

    def body(g_ref, a_ref, b_ref, o_ref):
        sa = jax.nn.sigmoid(g_ref[:, :D])
        sb = jax.nn.sigmoid(g_ref[:, D:])
        o_ref[...] = (sa * a_ref[...] + sb * b_ref[...]).astype(BF16)

    row = pl.BlockSpec((tm, D), lambda i: (i, 0))
    return pl.pallas_call(
        body, name=name, grid=(T // tm,),
        in_specs=[pl.BlockSpec((tm, 2 * D), lambda i: (i, 0)), row, row],
        out_specs=row,
        out_shape=jax.ShapeDtypeStruct((T, D), BF16),
        compiler_params=_params(("parallel",)),
    )(proj_g, y_a, y_b)


def gate_bwd(dz, proj_g, y_a, y_b, name):
    T, D = y_a.shape
    tm = _pick(T, 512, 8)

    def body(dz_ref, g_ref, a_ref, b_ref, da_ref, db_ref, dg_ref):
        dzv = dz_ref[...]
        sa = jax.nn.sigmoid(g_ref[:, :D])
        sb = jax.nn.sigmoid(g_ref[:, D:])
        da_ref[...] = (dzv * sa).astype(BF16)
        db_ref[...] = (dzv * sb).astype(BF16)
        dg_ref[:, :D] = (dzv * a_ref[...] * (sa * (1.0 - sa))).astype(BF16)
        dg_ref[:, D:] = (dzv * b_ref[...] * (sb * (1.0 - sb))).astype(BF16)

    row = pl.BlockSpec((tm, D), lambda i: (i, 0))
    wide = pl.BlockSpec((tm, 2 * D), lambda i: (i, 0))
    return pl.pallas_call(
        body, name=name, grid=(T // tm,),
        in_specs=[row, wide, row, row],
        out_specs=[row, row, wide],
        out_shape=[jax.ShapeDtypeStruct((T, D), BF16), jax.ShapeDtypeStruct((T, D), BF16),
                   jax.ShapeDtypeStruct((T, 2 * D), BF16)],
        compiler_params=_params(("parallel",)),
    )(dz, proj_g, y_a, y_b)


ATT_WIN = ATT_TQ + 2 * RADIUS
ATT_STEP = 512
FAR = 1e32


def _att_window(qs, L, d):
    ks = pl.multiple_of(jnp.clip(qs - RADIUS, 0, L - ATT_WIN), RADIUS)
    col_row = (lax.broadcasted_iota(jnp.int32, (ATT_TQ, ATT_WIN), 1)
               - lax.broadcasted_iota(jnp.int32, (ATT_TQ, ATT_WIN), 0))
    ad = jnp.abs(col_row + (ks - qs))
    return ks, jnp.where(ad <= RADIUS, (ad * d).astype(F32), FAR)


def _head_masks():
    lane = lax.broadcasted_iota(jnp.int32, (1, LANES), 1)
    return [lane < HEAD_DIM, lane >= HEAD_DIM]


def _att_step(L):
    step = min(ATT_STEP, L)
    assert L % step == 0 and step % ATT_TQ == 0 and L >= ATT_WIN
    return step


def att_fwd(qkv, group, name):
    d, L, _ = qkv.shape
    step = _att_step(L)
    cg = GROUP_W // LANES
    slopes = jnp.asarray(_alibi_slopes()[group])
    scale = HEAD_DIM ** -0.5

    def body(sl_ref, q_ref, k_ref, v_ref, o_ref, l_ref):
        hp = pl.program_id(1)
        i = pl.program_id(2)
        masks = _head_masks()
        for t in range(step // ATT_TQ):
            rows = slice(t * ATT_TQ, (t + 1) * ATT_TQ)
            ks, dist = _att_window(i * step + t * ATT_TQ, L, d)
            q = q_ref[rows, :] * scale
            kw = k_ref[pl.ds(ks, ATT_WIN), :]
            vw = v_ref[pl.ds(ks, ATT_WIN), :]
            o_acc = jnp.zeros((ATT_TQ, LANES), F32)
            l_acc = jnp.zeros((ATT_TQ, LANES), F32)
            for h, hm in enumerate(masks):
                slope = sl_ref[hp * 2 + h]
                qm = jnp.where(hm, q, jnp.zeros_like(q))
                s = lax.dot_general(qm, kw, NT_DIMS, preferred_element_type=F32) - slope * dist
                m = jnp.max(s, -1, keepdims=True)
                p = jnp.exp(s - m)
                den = jnp.sum(p, -1, keepdims=True)
                pn = (p / den).astype(BF16)
                vm = jnp.where(hm, vw, jnp.zeros_like(vw))
                o_acc = o_acc + jnp.dot(pn, vm, preferred_element_type=F32)
                l_acc = jnp.where(hm, m + jnp.log(den), l_acc)
            o_ref[rows, :] = o_acc
            l_ref[rows, :] = l_acc

    out_spec = pl.BlockSpec((None, step, LANES), lambda r, hp, i: (r, i, hp))
    return pl.pallas_call(
        body, name=name, grid=(d, cg, L // step),
        in_specs=[pl.BlockSpec(memory_space=pltpu.SMEM),
                  pl.BlockSpec((None, step, LANES), lambda r, hp, i: (r, i, hp)),
                  pl.BlockSpec((None, L, LANES), lambda r, hp, i: (r, 0, cg + hp)),
                  pl.BlockSpec((None, L, LANES), lambda r, hp, i: (r, 0, 2 * cg + hp))],
        out_specs=[out_spec, out_spec],
        out_shape=[jax.ShapeDtypeStruct((d, L, GROUP_W), F32)] * 2,
        compiler_params=_params(("parallel", "parallel", "arbitrary")),
    )(slopes, qkv, qkv, qkv)


def att_bwd(qkv, do, lse, dmat, group, name):
    d, L, _ = qkv.shape
    step = _att_step(L)
    nq = L // step
    cg = GROUP_W // LANES
    slopes = jnp.asarray(_alibi_slopes()[group])
    scale = HEAD_DIM ** -0.5

    def body(sl_ref, q_ref, k_ref, v_ref, do_ref, l_ref, dm_ref, dq_ref, dk_ref, dv_ref, dk_acc, dv_acc):
        hp = pl.program_id(1)
        i = pl.program_id(2)

        @pl.when(i == 0)
        def _():
            dk_acc[...] = jnp.zeros_like(dk_acc)
            dv_acc[...] = jnp.zeros_like(dv_acc)

        masks = _head_masks()
        for t in range(step // ATT_TQ):
            rows = slice(t * ATT_TQ, (t + 1) * ATT_TQ)
            ks, dist = _att_window(i * step + t * ATT_TQ, L, d)
            q = q_ref[rows, :] * scale
            dov = do_ref[rows, :]
            lse_t = l_ref[rows, :]
            dm_t = dm_ref[rows, :]
            kw = k_ref[pl.ds(ks, ATT_WIN), :]
            vw = v_ref[pl.ds(ks, ATT_WIN), :]
            dq_acc = jnp.zeros((ATT_TQ, LANES), F32)
            dk_new = jnp.zeros((ATT_WIN, LANES), F32)
            dv_new = jnp.zeros((ATT_WIN, LANES), F32)
            for h, hm in enumerate(masks):
                slope = sl_ref[hp * 2 + h]
                qm = jnp.where(hm, q, jnp.zeros_like(q))
                dom = jnp.where(hm, dov, jnp.zeros_like(dov))
                km = jnp.where(hm, kw, jnp.zeros_like(kw))
                s = lax.dot_general(qm, kw, NT_DIMS, preferred_element_type=F32) - slope * dist
                lse_col = jnp.max(jnp.where(hm, lse_t, -jnp.inf), -1, keepdims=True)
                dm_col = jnp.max(jnp.where(hm, dm_t, -jnp.inf), -1, keepdims=True)
                p = jnp.exp(s - lse_col)
                dp = lax.dot_general(dom, vw, NT_DIMS, preferred_element_type=F32)
                ds = (p * (dp - dm_col)).astype(BF16)
                dq_acc = dq_acc + jnp.dot(ds, km, preferred_element_type=F32)
                dk_new = dk_new + lax.dot_general(ds, qm, TN_DIMS, preferred_element_type=F32)
                dv_new = dv_new + lax.dot_general(p.astype(BF16), dom, TN_DIMS, preferred_element_type=F32)
            dq_ref[rows, :] = (dq_acc * scale).astype(BF16)
            dk_acc[pl.ds(ks, ATT_WIN), :] += dk_new
            dv_acc[pl.ds(ks, ATT_WIN), :] += dv_new

        @pl.when(i == nq - 1)
        def _():
            dk_ref[...] = dk_acc[...].astype(BF16)
            dv_ref[...] = dv_acc[...].astype(BF16)

    tile = pl.BlockSpec((None, step, LANES), lambda r, hp, i: (r, i, hp))
    whole = pl.BlockSpec((None, L, LANES), lambda r, hp, i: (r, 0, hp))
    return pl.pallas_call(
        body, name=name, grid=(d, cg, nq),
        in_specs=[pl.BlockSpec(memory_space=pltpu.SMEM), tile,
                  pl.BlockSpec((None, L, LANES), lambda r, hp, i: (r, 0, cg + hp)),
                  pl.BlockSpec((None, L, LANES), lambda r, hp, i: (r, 0, 2 * cg + hp)),
                  tile, tile, tile],
        out_specs=[tile, whole, whole],
        out_shape=[jax.ShapeDtypeStruct((d, L, GROUP_W), BF16)] * 3,
        scratch_shapes=[pltpu.VMEM((L, LANES), F32), pltpu.VMEM((L, LANES), F32)],
        compiler_params=_params(("arbitrary", "arbitrary", "arbitrary")),
    )(slopes, qkv, qkv, qkv, do, lse, dmat)


def _group_weights(ls):
    m = jnp.maximum(jnp.maximum(ls[0], ls[1]), ls[2])
    es = [jnp.exp(l - m) for l in ls]
    tot = es[0] + es[1] + es[2]
    return [e / tot for e in es]


def combine_fwd(outs, lses, name):
    T = outs[0].shape[0] * outs[0].shape[1]
    tm = _pick(T, 512, 8)
    n_scr = 2 * (len(DILATIONS) - 1)

    def body(*refs):
        o_refs, l_refs, c_ref, scr = refs[:3], refs[3:6], refs[6], refs[7:]
        o = [_load_natural(o_refs[g], d, scr[g - 1] if g else None) for g, d in enumerate(DILATIONS)]
        l = [_load_natural(l_refs[g], d, scr[g + 1] if g else None) for g, d in enumerate(DILATIONS)]
        w = _group_weights(l)
        c_ref[...] = (w[0] * o[0] + w[1] * o[1] + w[2] * o[2]).astype(BF16)

    specs = [_residue_spec(tm, d, GROUP_W) for d in DILATIONS]
    return pl.pallas_call(
        body, name=name, grid=(T // tm,),
        in_specs=specs + specs, out_specs=pl.BlockSpec((tm, GROUP_W), lambda i: (i, 0)),
        out_shape=jax.ShapeDtypeStruct((T, GROUP_W), BF16),
        scratch_shapes=[_residue_scratch(tm, GROUP_W)] * n_scr,
        compiler_params=_params(("parallel",)),
    )(*outs, *lses)


def combine_bwd(dcomb, outs, lses, name):
    T = dcomb.shape[0]
    tm = _pick(T, 256, 8)
    head = np.arange(GROUP_W) // HEAD_DIM
    seg = jnp.asarray((head[:, None] == head[None, :]).astype(np.float32)).astype(BF16)
    ng = len(DILATIONS)
    n_scr = 4 * (ng - 1)

    def body(*refs):
        dc_ref, o_refs, l_refs, e_ref = refs[0], refs[1:1 + ng], refs[1 + ng:1 + 2 * ng], refs[1 + 2 * ng]
        do_refs, dm_refs = refs[2 + 2 * ng:2 + 3 * ng], refs[2 + 3 * ng:2 + 4 * ng]
        scr = refs[2 + 4 * ng:]
        o = [_load_natural(o_refs[g], d, scr[4 * (g - 1)] if g else None) for g, d in enumerate(DILATIONS)]
        l = [_load_natural(l_refs[g], d, scr[4 * (g - 1) + 1] if g else None) for g, d in enumerate(DILATIONS)]
        w = _group_weights(l)
        dc = dc_ref[...]
        e = e_ref[...]
        tot = jnp.zeros_like(dc)
        for g in range(ng):
            prod = dc * o[g]
            dw = jnp.zeros_like(dc)
            for _ in range(3):
                part = prod.astype(BF16)
                dw = dw + jnp.dot(part, e, preferred_element_type=F32)
                prod = prod - part.astype(F32)
            tot = tot + w[g] * dw
        for g, d in enumerate(DILATIONS):
            _store_by_residue(w[g] * dc, do_refs[g], d, scr[4 * (g - 1) + 2] if g else None)
            _store_by_residue(w[g] * tot, dm_refs[g], d, scr[4 * (g - 1) + 3] if g else None)

    specs = [_residue_spec(tm, d, GROUP_W) for d in DILATIONS]
    res = pl.pallas_call(
        body, name=name, grid=(T // tm,),
        in_specs=[pl.BlockSpec((tm, GROUP_W), lambda i: (i, 0))] + specs + specs
        + [pl.BlockSpec((GROUP_W, GROUP_W), lambda i: (0, 0))],
        out_specs=specs + specs,
        out_shape=[jax.ShapeDtypeStruct(o.shape, BF16) for o in outs] + [jax.ShapeDtypeStruct(o.shape, F32) for o in outs],
        scratch_shapes=[_residue_scratch(tm, GROUP_W)] * n_scr,
        compiler_params=_params(("parallel",)),
    )(dcomb, *outs, *lses, seg)
    return res[:ng], res[ng:]


def _position():
    return lax.axis_index("x"), lax.axis_index("y"), lax.axis_index("c")


def _other_chips(x, y):
    return [(1 - x, y), (x, 1 - y), (1 - x, 1 - y)]


def _remote(src, dst, send_sems, recv_sems, k, to):
    return pltpu.make_async_remote_copy(src_ref=src, dst_ref=dst, send_sem=send_sems.at[k], recv_sem=recv_sems.at[k],
                                        device_id=to, device_id_type=MESH)


def all_gather(shards, name):
    n = len(shards)

    def body(*refs):
        ins, outs = refs[:n], refs[n:2 * n]
        send_sems, recv_sems, local_sems = refs[2 * n:]
        x, y, c = _position()
        sibling = (x, y, 1 - c)
        chips = _other_chips(x, y)

        def block(a, px, py, pc):
            return outs[a].at[4 * px + 2 * py + pc]

        own, first, passed = [], [], []
        for a in range(n):
            cp = pltpu.make_async_copy(ins[a], block(a, x, y, c), local_sems.at[a])
            cp.start()
            own.append(cp)
            k0 = 7 * a
            first.append(_remote(ins[a], block(a, x, y, c), send_sems, recv_sems, k0, sibling))
            for j, chip in enumerate(chips):
                first.append(_remote(ins[a], block(a, x, y, c), send_sems, recv_sems, k0 + 1 + j, (*chip, c)))
        for cp in first:
            cp.start()
        for a in range(n):
            k0 = 7 * a
            for j, chip in enumerate(chips):
                got = block(a, *chip, c)
                _remote(got, got, send_sems, recv_sems, k0 + 1 + j, sibling).wait_recv()
                fwd = _remote(got, got, send_sems, recv_sems, k0 + 4 + j, sibling)
                fwd.start()
                passed.append(fwd)
        for a in range(n):
            k0 = 7 * a
            got = block(a, x, y, 1 - c)
            _remote(got, got, send_sems, recv_sems, k0, sibling).wait_recv()
            for j, chip in enumerate(chips):
                got = block(a, *chip, 1 - c)
                _remote(got, got, send_sems, recv_sems, k0 + 4 + j, sibling).wait_recv()
        for cp in first + passed:
            cp.wait_send()
        for cp in own:
            cp.wait()

    hbm = pl.BlockSpec(memory_space=pl.ANY)
    return pl.pallas_call(
        body, name=name,
        in_specs=[hbm] * n, out_specs=[hbm] * n,
        out_shape=[jax.ShapeDtypeStruct((N_DEV,) + s.shape, s.dtype) for s in shards],
        scratch_shapes=[pltpu.SemaphoreType.DMA((7 * n,)), pltpu.SemaphoreType.DMA((7 * n,)),
                        pltpu.SemaphoreType.DMA((n,))],
    )(*shards)


def exchange_sibling(parts, name):
    n = len(parts)

    def body(*refs):
        ins, outs = refs[:n], refs[n:2 * n]
        send_sems, recv_sems = refs[2 * n:]
        x, y, c = _position()
        sibling = (x, y, 1 - c)
        copies = []
        for a in range(n):
            for q in range(4):
                cp = _remote(ins[a].at[2 * q + (1 - c)], outs[a].at[q], send_sems, recv_sems, 4 * a + q, sibling)
                cp.start()
                copies.append(cp)
        for cp in copies:
            cp.wait_recv()
        for cp in copies:
            cp.wait_send()

    hbm = pl.BlockSpec(memory_space=pl.ANY)
    return pl.pallas_call(
        body, name=name,
        in_specs=[hbm] * n, out_specs=[hbm] * n,
        out_shape=[jax.ShapeDtypeStruct((4,) + p.shape[1:], p.dtype) for p in parts],
        scratch_shapes=[pltpu.SemaphoreType.DMA((4 * n,)), pltpu.SemaphoreType.DMA((4 * n,))],
    )(*parts)


def exchange_chips(sums, name):
    n = len(sums)

    def body(*refs):
        ins, outs = refs[:n], refs[n:2 * n]
        send_sems, recv_sems = refs[2 * n:]
        x, y, c = _position()
        copies = []
        for a in range(n):
            for j, (cx, cy) in enumerate(_other_chips(x, y)):
                cp = _remote(ins[a].at[2 * cx + cy], outs[a].at[j], send_sems, recv_sems, 3 * a + j, (cx, cy, c))
                cp.start()
                copies.append(cp)
        for cp in copies:
            cp.wait_recv()
        for cp in copies:
            cp.wait_send()

    hbm = pl.BlockSpec(memory_space=pl.ANY)
    return pl.pallas_call(
        body, name=name,
        in_specs=[hbm] * n, out_specs=[hbm] * n,
        out_shape=[jax.ShapeDtypeStruct((3,) + s.shape[1:], s.dtype) for s in sums],
        scratch_shapes=[pltpu.SemaphoreType.DMA((3 * n,)), pltpu.SemaphoreType.DMA((3 * n,))],
    )(*sums)


_HBM = pl.BlockSpec(memory_space=pltpu.HBM)
_SEM = pl.BlockSpec(memory_space=pltpu.SEMAPHORE)
_DATAFLOW = pltpu.SideEffectType.DATAFLOW_SIDE_EFFECTING


def _to_all_plan(srcs, lands, send_sems, recv_sems):
    x, y, c = _position()
    me = 4 * x + 2 * y + c
    copies = []
    for a in range(len(srcs)):
        for k in range(1, N_DEV):
            fx, fy, fc = (k >> 2) & 1, (k >> 1) & 1, k & 1
            to = (1 - x if fx else x, 1 - y if fy else y, 1 - c if fc else c)
            copies.append(_remote(srcs[a], lands[a].at[me], send_sems, recv_sems, (N_DEV - 1) * a + k - 1, to))
    return copies


def _to_chips_plan(srcs, lands, send_sems, recv_sems):
    x, y, c = _position()
    copies = []
    for a in range(len(srcs)):
        for j, (cx, cy) in enumerate(_other_chips(x, y)):
            copies.append(_remote(srcs[a].at[2 * cx + cy], lands[a].at[j], send_sems, recv_sems, 3 * a + j, (cx, cy, c)))
    return copies


def copies_start(srcs, land_shapes, plan, per_array, name):
    n = len(srcs)
    n_sem = per_array * n
    lands = [lax.empty(s.shape, s.dtype) for s in land_shapes]

    def body(*refs):
        src_refs, land_refs = refs[:n], refs[n:2 * n]
        send_sems, recv_sems = refs[2 * n], refs[2 * n + 1]
        token = refs[-1]
        for cp in plan(src_refs, land_refs, send_sems, recv_sems):
            cp.start()
        token[...] = jnp.zeros_like(token)

    out = pl.pallas_call(
        body, name=name,
        out_shape=(pltpu.SemaphoreType.DMA((n_sem,)), pltpu.SemaphoreType.DMA((n_sem,)))
        + tuple(pltpu.HBM(s.shape, s.dtype) for s in srcs)
        + tuple(pltpu.HBM(s.shape, s.dtype) for s in land_shapes)
        + (jax.ShapeDtypeStruct((8, LANES), F32),),
        in_specs=[_HBM] * (2 * n),
        out_specs=(_SEM, _SEM) + (_HBM,) * (2 * n) + (pl.BlockSpec(memory_space=pltpu.VMEM),),
        input_output_aliases={i: 2 + i for i in range(2 * n)},
        compiler_params=pltpu.CompilerParams(has_side_effects=_DATAFLOW),
    )(*[pltpu.with_memory_space_constraint(s, pltpu.HBM) for s in srcs],
      *[pltpu.with_memory_space_constraint(l, pltpu.HBM) for l in lands])
    return out[:-1], out[-1]


def copies_wait(handles, plan, after, name):
    send_sems, recv_sems = handles[0], handles[1]
    n = (len(handles) - 2) // 2
    thru = handles[2:]

    def body(*refs):
        src_refs, land_refs = refs[:n], refs[n:2 * n]
        send_sems, recv_sems = refs[2 * n], refs[2 * n + 1]
        copies = plan(src_refs, land_refs, send_sems, recv_sems)
        for cp in copies:
            cp.wait_recv()
        for cp in copies:
            cp.wait_send()

    out = pl.pallas_call(
        body, name=name,
        out_shape=tuple(pltpu.HBM(t.shape, t.dtype) for t in thru),
        in_specs=[_HBM] * (2 * n) + [_SEM, _SEM, pl.BlockSpec(memory_space=pl.ANY)],
        out_specs=(_HBM,) * (2 * n),
        input_output_aliases={i: i for i in range(2 * n)},
        compiler_params=pltpu.CompilerParams(has_side_effects=_DATAFLOW),
    )(*thru, send_sems, recv_sems, after)
    return out[n:]


def all_sum_small(vec, name):
    R = vec.shape[0]

    def body(v_ref, tot_ref, all_ref, send_sems, recv_sems):
        x, y, c = _position()
        me = 4 * x + 2 * y + c
        all_ref[me] = v_ref[...]
        copies = []
        for k in range(1, N_DEV):
            fx, fy, fc = (k >> 2) & 1, (k >> 1) & 1, k & 1
            to = (1 - x if fx else x, 1 - y if fy else y, 1 - c if fc else c)
            cp = _remote(v_ref, all_ref.at[me], send_sems, recv_sems, k - 1, to)
            cp.start()
            copies.append(cp)
        for cp in copies:
            cp.wait_recv()
        for cp in copies:
            cp.wait_send()
        tot = all_ref[0]
        for j in range(1, N_DEV):
            tot = tot + all_ref[j]
        tot_ref[...] = tot

    vmem = pl.BlockSpec(memory_space=pltpu.VMEM)
    return pl.pallas_call(
        body, name=name,
        in_specs=[vmem], out_specs=vmem,
        out_shape=jax.ShapeDtypeStruct((R, LANES), F32),
        scratch_shapes=[pltpu.VMEM((N_DEV, R, LANES), F32),
                        pltpu.SemaphoreType.DMA((N_DEV - 1,)), pltpu.SemaphoreType.DMA((N_DEV - 1,))],
        compiler_params=pltpu.CompilerParams(vmem_limit_bytes=VMEM_LIMIT),
    )(vec)


def pair_add(mine, theirs, name):
    _, R, C = mine.shape
    tr = _pick(R, 256, 8)

    def body(a_ref, b_ref, o_ref):
        o_ref[...] = (a_ref[...].astype(F32) + b_ref[...].astype(F32)).astype(BF16)

    blk = pl.BlockSpec((None, tr, C), lambda q, i: (q, i, 0))
    return pl.pallas_call(
        body, name=name, grid=(4, R // tr),
        in_specs=[blk, blk], out_specs=blk,
        out_shape=jax.ShapeDtypeStruct(mine.shape, BF16),
        compiler_params=_params(("parallel", "parallel")),
    )(mine, theirs)


def _adamw_math(w, g, m, v):
    m = ADAM_B1 * m + (1.0 - ADAM_B1) * g
    v = ADAM_B2 * v + (1.0 - ADAM_B2) * jnp.square(g)
    m_hat = m / (1.0 - ADAM_B1 ** ADAM_STEP)
    v_hat = v / (1.0 - ADAM_B2 ** ADAM_STEP)
    delta = -ADAM_LR * (m_hat / (jnp.sqrt(v_hat) + ADAM_EPS) + ADAM_WD * w)
    return delta, m, v


def adamw_sharded(w, m, v, own, sib, others, name):
    R, C = w.shape
    tr = _pick(R, 256, 8)

    def body(w_ref, m_ref, v_ref, a_ref, b_ref, o_ref, g_ref, d_ref, nm_ref, nv_ref):
        g = a_ref[...].astype(F32) + b_ref[...].astype(F32)
        for j in range(3):
            g = g + o_ref[j].astype(F32)
        delta, nm, nv = _adamw_math(w_ref[...], g, m_ref[...], v_ref[...])
        g_ref[...] = g
        d_ref[...] = delta
        nm_ref[...] = nm
        nv_ref[...] = nv

    row = pl.BlockSpec((tr, C), lambda i: (i, 0))
    return pl.pallas_call(
        body, name=name, grid=(R // tr,),
        in_specs=[row] * 5 + [pl.BlockSpec((3, tr, C), lambda i: (0, i, 0))],
        out_specs=[row] * 4,
        out_shape=[jax.ShapeDtypeStruct((R, C), F32)] * 4,
        compiler_params=_params(("parallel",)),
    )(w, m, v, own, sib, others)


def adamw_packed(w, g, m, v, name):
    R = w.shape[0]

    def body(w_ref, g_ref, m_ref, v_ref, d_ref, nm_ref, nv_ref):
        delta, nm, nv = _adamw_math(w_ref[...], g_ref[...], m_ref[...], v_ref[...])
        d_ref[...] = delta
        nm_ref[...] = nm
        nv_ref[...] = nv

    full = pl.BlockSpec((R, LANES), lambda i: (0, 0))
    return pl.pallas_call(
        body, name=name, grid=(1,),
        in_specs=[full] * 4, out_specs=[full] * 3,
        out_shape=[jax.ShapeDtypeStruct((R, LANES), F32)] * 3,
        compiler_params=_params(("arbitrary",)),
    )(w, g, m, v)


def _pack(arrays):
    flat = []
    sizes = []
    for a in arrays:
        f = a.reshape(-1).astype(F32)
        pad = (-f.shape[0]) % LANES
        if pad:
            f = jnp.concatenate([f, jnp.zeros((pad,), F32)])
        flat.append(f)
        sizes.append(f.shape[0])
    rows = sum(sizes) // LANES
    pad_rows = (-rows) % 8
    if pad_rows:
        flat.append(jnp.zeros((pad_rows * LANES,), F32))
    return jnp.concatenate(flat).reshape(-1, LANES), sizes


def _unpack(packed, sizes, shapes):
    flat = packed.reshape(-1)
    out = []
    off = 0
    for size, shape in zip(sizes, shapes):
        n = int(np.prod(shape))
        out.append(flat[off:off + n].reshape(shape))
        off += size
    return out


def _to_blocks(full, axis):
    if axis == 0:
        return full.reshape(N_DEV, full.shape[0] // N_DEV, full.shape[1])
    r, n = full.shape
    return full.reshape(r, N_DEV, n // N_DEV).transpose(1, 0, 2)


def _from_blocks(blocks, axis):
    if axis == 0:
        return blocks.reshape(blocks.shape[0] * blocks.shape[1], blocks.shape[2])
    return blocks.transpose(1, 0, 2).reshape(blocks.shape[1], blocks.shape[0] * blocks.shape[2])


def kernel(x, ln0_g, ln0_b, w_in, b_in, conv_w, w_a, w_b, w_o, b_o, ln1_g, ln1_b, w_up, b_up, ffn_conv_w, ffn_conv_b, w_down, b_down, ln2_g, ln2_b, loss_target, m_ln0_g, m_ln0_b, m_w_in, m_b_in, m_conv_w, m_w_a, m_w_b, m_w_o, m_b_o, m_ln1_g, m_ln1_b, m_w_up, m_b_up, m_ffn_conv_w, m_ffn_conv_b, m_w_down, m_b_down, m_ln2_g, m_ln2_b, v_ln0_g, v_ln0_b, v_w_in, v_b_in, v_conv_w, v_w_a, v_w_b, v_w_o, v_b_o, v_ln1_g, v_ln1_b, v_w_up, v_b_up, v_ffn_conv_w, v_ffn_conv_b, v_w_down, v_b_down, v_ln2_g, v_ln2_b):
    T, D = x.shape[1], x.shape[2]
    F = ffn_conv_b.shape[-1]
    xs = x.reshape(T, D)
    tgt = loss_target.reshape(T, D)
    dev = 4 * lax.axis_index("x") + 2 * lax.axis_index("y") + lax.axis_index("c")
    chip = 2 * lax.axis_index("x") + lax.axis_index("y")
    core = lax.axis_index("c")

    big = dict(w_in=(w_in[0], 1), w_a=(w_a[0], 0), w_b=(w_b[0], 1), w_o=(w_o[0], 0), w_up=(w_up[0], 1),
               w_down=(w_down[0], 0))
    names = list(big)
    shards = {k: big[k][0].astype(BF16) for k in names}
    g_in, g_conv, g_fcw = all_gather([shards["w_in"], conv_w[0], ffn_conv_w[0]], "gather_w_in")
    full = {"w_in": _from_blocks(g_in, 1)}
    conv_full = _from_blocks(g_conv, 1)
    fcw_full = _from_blocks(g_fcw, 1)
    late_groups = (("w_a", "w_b", "w_o"), ("w_up", "w_down"))
    late_handles = []
    token = None
    for n, keys in enumerate(late_groups):
        srcs = [shards[k] if token is None else shards[k] + token[0, 0].astype(BF16) for k in keys]
        handles, token = copies_start(srcs, [jax.ShapeDtypeStruct((N_DEV,) + s.shape, BF16) for s in srcs],
                                      _to_all_plan, N_DEV - 1, f"gather_late_{n}_start")
        late_handles.append(handles)

    def late_weights(n, after):
        lands = copies_wait(late_handles[n], _to_all_plan, after, f"gather_late_{n}_wait")
        for k, land in zip(late_groups[n], lands):
            full[k] = _from_blocks(lax.dynamic_update_index_in_dim(land, shards[k], dev, 0), big[k][1])

    o_q = 3 * D
    o_g = o_q + 3 * QKV_W
    w_pa, w_qkv, w_pg = full["w_in"][:, :o_q], full["w_in"][:, o_q:o_g], full["w_in"][:, o_g:]
    b_pa, b_qkv, b_pg = b_in[:, :o_q], b_in[:, o_q:o_g], b_in[:, o_g:]
    ln0g, ln0b = ln0_g.reshape(1, D), ln0_b.reshape(1, D)

    h0, h0b, *h0_res = ln_fwd(xs, None, ln0g, ln0b, "ln0_fwd", dilations=DILATIONS[1:])
    h0_res = [h0b] + [h.reshape(T, D) for h in h0_res]
    proj_a = mm_nn(h0b, w_pa, b_pa, F32, "proj_conv", after=token)
    proj_g = mm_nn(h0b, w_pg, b_pg, F32, "proj_gates")
    zero_d = jnp.zeros((1, D), F32)
    s_a = conv_a_fwd(proj_a, conv_full, "conv_a_fwd")
    late_weights(0, s_a)
    y_a = mm_nn(s_a, full["w_a"], zero_d, F32, "branch_a_out")

    def group_cols(m, g):
        return jnp.concatenate([m[:, s * QKV_W + g * GROUP_W:s * QKV_W + (g + 1) * GROUP_W] for s in range(3)], 1)

    w_grp = [group_cols(w_qkv, g) for g in range(3)]
    qkvs, outs, lses = [], [], []
    for g, d in enumerate(DILATIONS):
        qkv = mm_nn(h0_res[g], w_grp[g], group_cols(b_qkv, g), BF16, f"proj_qkv_{g}").reshape(d, T // d, 3 * GROUP_W)
        o, l = att_fwd(qkv, g, f"att_fwd_{g}")
        qkvs.append(qkv)
        outs.append(o)
        lses.append(l)
    comb = combine_fwd(outs, lses, "combine_fwd")
    y_b = mm_nn(comb, full["w_b"], zero_d, F32, "branch_b_out")
    z = gate_fwd(proj_g, y_a, y_b, "gate_fwd")
    mix = mm_nn(z, full["w_o"], b_o, F32, "mix_out")
    h1, h1b = ln_fwd(h0, mix, ln1_g, ln1_b, "ln1_fwd")
    late_weights(1, h1b)
    up = mm_nn(h1b, full["w_up"], b_up, F32, "ffn_up")
    f_act = conv_f_fwd(up, fcw_full, ffn_conv_b, "conv_f_fwd")
    ffn = mm_nn(f_act, full["w_down"], b_down, F32, "ffn_down")

    dr2, dr2b, d_ln2_g, d_ln2_b, d_b_down, loss_part = ln_bwd(h1, ffn, ln2_g, ln2_b, None, None, tgt, "ln2_loss_bwd")
    dw_down, _ = mm_tn(f_act, dr2b, "dw_down")
    df = mm_nt(dr2b, full["w_down"], None, "d_ffn_act")
    d_a, d_gate, cs_a, cs_gate, d_fcb, d_fcw = conv_f_bwd(df, up, fcw_full, ffn_conv_b, "conv_f_bwd")
    dw_up_a, _ = mm_tn(h1b, d_a, "dw_up_a")
    dw_up_g, _ = mm_tn(h1b, d_gate, "dw_up_gate")
    dh1 = mm_nt(d_a, full["w_up"][:, :F], None, "d_h1_a")
    dh1 = mm_nt(d_gate, full["w_up"][:, F:], dh1, "d_h1_gate")
    dr1, dr1b, d_ln1_g, d_ln1_b, d_b_o, _ = ln_bwd(h0, mix, ln1_g, ln1_b, dr2, dh1, None, "ln1_bwd")
    dw_o, _ = mm_tn(z, dr1b, "dw_o")
    dz = mm_nt(dr1b, full["w_o"], None, "d_z")
    dy_a, dy_b, dproj_g = gate_bwd(dz, proj_g, y_a, y_b, "gate_bwd")
    dw_a, _ = mm_tn(s_a, dy_a, "dw_a")
    ds_a = mm_nt(dy_a, full["w_a"], None, "d_s_a")
    dproj_a, d_conv = conv_a_bwd(ds_a, proj_a, conv_full, "conv_a_bwd")
    dw_b, _ = mm_tn(comb, dy_b, "dw_b")

    rs_mine, rs_sib, rs_handles = {}, {}, {}

    def reduce_start(keys, grads, tag):
        parts = [_to_blocks(grads[k], big[k][1]) for k in keys]
        from_sib = exchange_sibling(parts, f"grads_to_sibling_{tag}")
        mine = [lax.dynamic_index_in_dim(p.reshape((4, 2) + p.shape[1:]), core, 1, keepdims=False) for p in parts]
        sums = [pair_add(a, b, f"chip_sum_{k}") for k, a, b in zip(keys, mine, from_sib)]
        handles, tok = copies_start(sums, [jax.ShapeDtypeStruct((3,) + s.shape[1:], BF16) for s in sums],
                                    _to_chips_plan, 3, f"grads_to_chips_{tag}_start")
        for k, a, b in zip(keys, mine, from_sib):
            rs_mine[k], rs_sib[k] = a, b
        rs_handles[tag] = (keys, handles)
        return tok

    tok_a = reduce_start(("w_a", "w_b", "w_o", "w_up", "w_down"),
                         dict(w_a=dw_a, w_b=dw_b, w_o=dw_o, w_up=jnp.concatenate([dw_up_a, dw_up_g], 1), w_down=dw_down),
                         "a")
    dcomb = mm_nt(dy_b, full["w_b"], None, "d_comb", after=tok_a)
    dos, dms = combine_bwd(dcomb, outs, lses, "combine_bwd")
    dw_grp, cs_grp, dqkvs = [], [], []
    for g, d in enumerate(DILATIONS):
        dq, dk, dv = att_bwd(qkvs[g], dos[g], lses[g], dms[g], g, f"att_bwd_{g}")
        dqkv = jnp.concatenate([dq, dk, dv], -1).reshape(T, 3 * GROUP_W)
        dwg, csg = mm_tn(h0_res[g], dqkv, f"dw_in_qkv_{g}")
        dqkvs.append(dqkv)
        dw_grp.append(dwg)
        cs_grp.append(csg)
    dw_pa, cs_pa = mm_tn(h0b, dproj_a, "dw_in_conv")
    dw_pg, cs_pg = mm_tn(h0b, dproj_g, "dw_in_gates")

    def ungroup(parts):
        return jnp.concatenate([p[:, s * GROUP_W:(s + 1) * GROUP_W] for s in range(3) for p in parts], 1)

    db_in_parts = [cs_pa, ungroup(cs_grp), cs_pg]
    tok_b = reduce_start(("w_in",), dict(w_in=jnp.concatenate([dw_pa, ungroup(dw_grp), dw_pg], 1)), "b")
    dh0 = mm_nt(dproj_a, w_pa, None, "d_h0_conv", after=tok_b)
    dh0 = mm_nt(dproj_g, w_pg, dh0, "d_h0_gates")
    dh0 = mm_nt(dqkvs[0], w_grp[0], dh0, "d_h0_qkv_0")
    dh0_res = [(mm_nt(dqkvs[g], w_grp[g], None, f"d_h0_qkv_{g}").reshape(d, T // d, D), d)
               for g, d in enumerate(DILATIONS) if g > 0]
    dx, _, d_ln0_g, d_ln0_b, _, _ = ln_bwd(xs, None, ln0g, ln0b, dr1, dh0, None, "ln0_bwd", by_residue=dh0_res)

    small = [d_ln0_g, d_ln0_b, jnp.concatenate(db_in_parts, 1), d_conv, d_b_o, d_ln1_g, d_ln1_b,
             jnp.concatenate([cs_a, cs_gate], 1), d_fcw, d_fcb, d_b_down, d_ln2_g, d_ln2_b, loss_part]
    packed, sizes = _pack(small)
    total = all_sum_small(packed, "sum_small")
    (g_ln0_g, g_ln0_b, g_b_in, g_conv_full, g_b_o, g_ln1_g, g_ln1_b, g_b_up, g_fcw_full, g_fcb, g_b_down, g_ln2_g,
     g_ln2_b, loss) = _unpack(total, sizes, [a.shape for a in small])
    cw = conv_w.shape[-1]
    fw = ffn_conv_w.shape[-1]
    g_conv = lax.dynamic_slice_in_dim(g_conv_full, dev * cw, cw, 1)
    g_fcw = lax.dynamic_slice_in_dim(g_fcw_full, dev * fw, fw, 1)

    from_chips = {}
    for tag, (keys, handles) in rs_handles.items():
        lands = copies_wait(handles, _to_chips_plan, total, f"grads_to_chips_{tag}_wait")
        from_chips.update(zip(keys, lands))

    moments = dict(w_in=(m_w_in, v_w_in), w_a=(m_w_a, v_w_a), w_b=(m_w_b, v_w_b), w_o=(m_w_o, v_w_o),
                   w_up=(m_w_up, v_w_up), w_down=(m_w_down, v_w_down))
    res_big = {}
    for k in names:
        own = lax.dynamic_index_in_dim(rs_mine[k], chip, 0, keepdims=False)
        sib = lax.dynamic_index_in_dim(rs_sib[k], chip, 0, keepdims=False)
        res_big[k] = adamw_sharded(big[k][0], moments[k][0][0], moments[k][1][0], own, sib, from_chips[k], f"adamw_{k}")

    small_names = ["ln0_g", "ln0_b", "b_in", "conv_w", "b_o", "ln1_g", "ln1_b", "b_up", "ffn_conv_w", "ffn_conv_b",
                   "b_down", "ln2_g", "ln2_b"]
    small_w = [ln0_g, ln0_b, b_in, conv_w, b_o, ln1_g, ln1_b, b_up, ffn_conv_w, ffn_conv_b, b_down, ln2_g, ln2_b]
    small_m = [m_ln0_g, m_ln0_b, m_b_in, m_conv_w, m_b_o, m_ln1_g, m_ln1_b, m_b_up, m_ffn_conv_w, m_ffn_conv_b,
               m_b_down, m_ln2_g, m_ln2_b]
    small_v = [v_ln0_g, v_ln0_b, v_b_in, v_conv_w, v_b_o, v_ln1_g, v_ln1_b, v_b_up, v_ffn_conv_w, v_ffn_conv_b,
               v_b_down, v_ln2_g, v_ln2_b]
    small_g = [g_ln0_g, g_ln0_b, g_b_in, g_conv, g_b_o, g_ln1_g, g_ln1_b, g_b_up, g_fcw, g_fcb, g_b_down, g_ln2_g,
               g_ln2_b]
    shapes = [w.shape for w in small_w]
    small_g = [g.reshape(s) for g, s in zip(small_g, shapes)]
    pw, psz = _pack(small_w)
    pg, _ = _pack(small_g)
    pm, _ = _pack(small_m)
    pv, _ = _pack(small_v)
    pd, pnm, pnv = adamw_packed(pw, pg, pm, pv, "adamw_small")
    res_small = {k: (g, d_, m_, v_) for k, g, d_, m_, v_ in zip(
        small_names, small_g, _unpack(pd, psz, shapes), _unpack(pnm, psz, shapes), _unpack(pnv, psz, shapes))}

    order = ["ln0_g", "ln0_b", "w_in", "b_in", "conv_w", "w_a", "w_b", "w_o", "b_o", "ln1_g", "ln1_b", "w_up", "b_up",
             "ffn_conv_w", "ffn_conv_b", "w_down", "b_down", "ln2_g", "ln2_b"]

    def result(k, j):
        if k in res_big:
            return res_big[k][j][None]
        return res_small[k][j]

    out = [loss.reshape(()), dx.reshape(x.shape)]
    for j in range(4):
        out += [result(k, j) for k in order]
    return tuple(out)
```

```python
import functools
import math

import numpy as np
import jax
import jax.numpy as jnp
from jax import lax
from jax.experimental import pallas as pl
from jax.experimental.pallas import tpu as pltpu

F32 = jnp.float32
BF16 = jnp.bfloat16

N_DEV = 8
LN_EPS = 1e-5
ALPHA = (2.0 * 1) ** 0.25
MASK_VALUE = -1e30
HEAD_DIM = 64
GROUP_W = 512
QKV_W = 3 * GROUP_W
DILATIONS = (1, 4, 16)
RADIUS = 64
LANES = 128
HALO = 8
ATT_TQ = 128

ADAM_LR = 0.001
ADAM_B1 = 0.9
ADAM_B2 = 0.999
ADAM_EPS = 1e-08
ADAM_WD = 0.01
ADAM_STEP = 10

VMEM_LIMIT = 52 * 1024 * 1024
OUT_TILE_BYTES = 8 * 1024 * 1024
MESH = pl.DeviceIdType.MESH
NT_DIMS = (((1,), (1,)), ((), ()))
TN_DIMS = (((0,), (0,)), ((), ()))


def _pick(n, target, align=LANES):
    if n <= target:
        return n
    best = None
    for t in range(align, target + 1, align):
        if n % t == 0:
            best = t
    assert best is not None, (n, target, align)
    return best


def _params(sems=None):
    return pltpu.CompilerParams(dimension_semantics=sems, vmem_limit_bytes=VMEM_LIMIT)


def _alibi_slopes():
    n = 3 * 8
    return np.exp2(-8.0 * np.arange(1, n + 1, dtype=np.float64) / n).astype(np.float32).reshape(3, 8)


def _ln_stats(r):
    mu = jnp.mean(r, -1, keepdims=True)
    xc = r - mu
    var = jnp.mean(xc * xc, -1, keepdims=True)
    rstd = lax.rsqrt(var + LN_EPS)
    return xc, rstd


def _load_natural(ref, d, scr):
    if d == 1:
        return ref[0]
    n, C = ref.shape[1], ref.shape[2]
    for c in range(C // LANES):
        for r in range(d):
            scr[c, pl.ds(r, n, stride=d), :] = ref[r, :, c * LANES:(c + 1) * LANES]
    return jnp.concatenate([scr[c] for c in range(C // LANES)], axis=1)


def _store_by_residue(val, ref, d, scr):
    if d == 1:
        ref[0] = val.astype(ref.dtype)
        return
    n, C = ref.shape[1], ref.shape[2]
    for c in range(C // LANES):
        scr[c] = val[:, c * LANES:(c + 1) * LANES]
    for c in range(C // LANES):
        for r in range(d):
            ref[r, :, c * LANES:(c + 1) * LANES] = scr[c, pl.ds(r, n, stride=d), :].astype(ref.dtype)


def _residue_spec(tm, d, C):
    return pl.BlockSpec((d, tm // d, C), lambda i: (0, i, 0))


def _residue_scratch(tm, C):
    return pltpu.VMEM((C // LANES, tm, LANES), F32)


def ln_fwd(a, res, g, b, name, dilations=()):
    T, D = a.shape
    tm = _pick(T, 512, 8)
    has_res = res is not None
    nd = len(dilations)

    def body(*refs):
        refs = list(refs)
        a_ref = refs.pop(0)
        r = a_ref[...]
        if has_res:
            r = ALPHA * r + refs.pop(0)[...]
        g_ref, b_ref, h_ref, hb_ref = refs[:4]
        xc, rstd = _ln_stats(r)
        h = xc * rstd * g_ref[...] + b_ref[...]
        h_ref[...] = h
        hb_ref[...] = h.astype(BF16)
        for d, p_ref in zip(dilations, refs[4:4 + nd]):
            _store_by_residue(h, p_ref, d, refs[-1])

    row = pl.BlockSpec((tm, D), lambda i: (i, 0))
    vec = pl.BlockSpec((1, D), lambda i: (0, 0))
    ins = [a] + ([res] if has_res else []) + [g, b]
    return pl.pallas_call(
        body, name=name, grid=(T // tm,),
        in_specs=[row] * (2 if has_res else 1) + [vec, vec],
        out_specs=[row, row] + [_residue_spec(tm, d, D) for d in dilations],
        out_shape=[jax.ShapeDtypeStruct((T, D), F32), jax.ShapeDtypeStruct((T, D), BF16)]
        + [jax.ShapeDtypeStruct((d, T // d, D), BF16) for d in dilations],
        scratch_shapes=[_residue_scratch(tm, D)] if nd else [],
        compiler_params=_params(("parallel",)),
    )(*ins)


def ln_bwd(a, res, g, b, d1, d2, tgt, name, by_residue=()):
    T, D = a.shape
    tm = _pick(T, 256, 8)
    has_res = res is not None
    loss_mode = tgt is not None
    nres = len(by_residue)

    def body(*refs):
        refs = list(refs)
        a_ref = refs.pop(0)
        r_ref = refs.pop(0) if has_res else None
        g_ref = refs.pop(0)
        b_ref = refs.pop(0)
        if loss_mode:
            t_ref = refs.pop(0)
        else:
            d1_ref = refs.pop(0)
            d2_ref = refs.pop(0)
        e_refs = [refs.pop(0) for _ in range(nres)]
        dr_ref, drb_ref, dg_ref, db_ref, ds_ref, loss_ref = refs[:6]
        i = pl.program_id(0)

        @pl.when(i == 0)
        def _():
            dg_ref[...] = jnp.zeros_like(dg_ref)
            db_ref[...] = jnp.zeros_like(db_ref)
            ds_ref[...] = jnp.zeros_like(ds_ref)
            loss_ref[...] = jnp.zeros_like(loss_ref)

        r = a_ref[...]
        if has_res:
            r = ALPHA * r + r_ref[...]
        xc, rstd = _ln_stats(r)
        xhat = xc * rstd
        gam = g_ref[...]
        if loss_mode:
            err = xhat * gam + b_ref[...] - t_ref[...]
            dy = err * (1.0 / D)
            row_loss = jnp.mean(err * err, -1, keepdims=True)
            loss_ref[...] += 0.5 * jnp.sum(row_loss, 0, keepdims=True)
        else:
            dy = ALPHA * d1_ref[...] + d2_ref[...]
        for (_, d), e_ref in zip(by_residue, e_refs):
            dy = dy + _load_natural(e_ref, d, refs[-1])
        dyg = dy * gam
        c1 = jnp.mean(dyg, -1, keepdims=True)
        c2 = jnp.mean(dyg * xhat, -1, keepdims=True)
        dr = rstd * (dyg - c1 - xhat * c2)
        dr_ref[...] = dr
        drb_ref[...] = dr.astype(BF16)
        dg_ref[...] += jnp.sum(dy * xhat, 0, keepdims=True)
        db_ref[...] += jnp.sum(dy, 0, keepdims=True)
        ds_ref[...] += jnp.sum(dr, 0, keepdims=True)

    row = pl.BlockSpec((tm, D), lambda i: (i, 0))
    vec = pl.BlockSpec((1, D), lambda i: (0, 0))
    one = pl.BlockSpec((1, 1), lambda i: (0, 0))
    ins = [a] + ([res] if has_res else []) + [g, b] + ([tgt] if loss_mode else [d1, d2]) + [e for e, _ in by_residue]
    in_specs = [row] * (2 if has_res else 1) + [vec, vec] + [row] * (1 if loss_mode else 2)
    in_specs += [_residue_spec(tm, d, D) for _, d in by_residue]
    return pl.pallas_call(
        body, name=name, grid=(T // tm,),
        in_specs=in_specs,
        out_specs=[row, row, vec, vec, vec, one],
        out_shape=[jax.ShapeDtypeStruct((T, D), F32), jax.ShapeDtypeStruct((T, D), BF16),
                   jax.ShapeDtypeStruct((1, D), F32), jax.ShapeDtypeStruct((1, D), F32),
                   jax.ShapeDtypeStruct((1, D), F32), jax.ShapeDtypeStruct((1, 1), F32)],
        scratch_shapes=[_residue_scratch(tm, D)] if nres else [],
        compiler_params=_params(("arbitrary",)),
    )(*ins)


_TOKEN_SPEC = pl.BlockSpec((8, LANES), lambda i: (0, 0))


def mm_nn(a, w, bias, out_dtype, name, after=None):
    M, K = a.shape
    N = w.shape[1]
    tm = _pick(M, max(256, min(1024, OUT_TILE_BYTES // (N * jnp.dtype(out_dtype).itemsize))), 8)
    tc = _pick(N, 512)

    def body(a_ref, w_ref, b_ref, *rest):
        o_ref = rest[-1]
        av = a_ref[...]
        for j in range(N // tc):
            cols = slice(j * tc, (j + 1) * tc)
            acc = jnp.dot(av, w_ref[:, cols], preferred_element_type=F32)
            o_ref[:, cols] = (acc + b_ref[:, cols]).astype(out_dtype)

    return pl.pallas_call(
        body, name=name, grid=(M // tm,),
        in_specs=[pl.BlockSpec((tm, K), lambda i: (i, 0)),
                  pl.BlockSpec((K, N), lambda i: (0, 0)),
                  pl.BlockSpec((1, N), lambda i: (0, 0))] + ([] if after is None else [_TOKEN_SPEC]),
        out_specs=pl.BlockSpec((tm, N), lambda i: (i, 0)),
        out_shape=jax.ShapeDtypeStruct((M, N), out_dtype),
        compiler_params=_params(("parallel",)),
    )(a, w, bias, *([] if after is None else [after]))


def mm_nt(a, w, acc_in, name, after=None):
    M, K = a.shape
    N = w.shape[0]
    tm = _pick(M, 512, 8)
    tc = _pick(N, 512)
    has_acc = acc_in is not None

    def body(*refs):
        a_ref, w_ref = refs[:2]
        c_ref = refs[2] if has_acc else None
        o_ref = refs[-1]
        av = a_ref[...]
        for j in range(N // tc):
            cols = slice(j * tc, (j + 1) * tc)
            acc = lax.dot_general(av, w_ref[cols, :], NT_DIMS, preferred_element_type=F32)
            if has_acc:
                acc = acc + c_ref[:, cols]
            o_ref[:, cols] = acc

    out_spec = pl.BlockSpec((tm, N), lambda i: (i, 0))
    in_specs = [pl.BlockSpec((tm, K), lambda i: (i, 0)),
                pl.BlockSpec((N, K), lambda i: (0, 0))]
    ins = [a, w]
    if has_acc:
        in_specs.append(out_spec)
        ins.append(acc_in)
    if after is not None:
        in_specs.append(_TOKEN_SPEC)
        ins.append(after)
    return pl.pallas_call(
        body, name=name, grid=(M // tm,),
        in_specs=in_specs, out_specs=out_spec,
        out_shape=jax.ShapeDtypeStruct((M, N), F32),
        compiler_params=_params(("parallel",)),
    )(*ins)


def mm_tn(a, b, name, out_dtype=BF16):
    T, M = a.shape
    N = b.shape[1]
    tm = _pick(M, 1408)
    tn = _pick(N, 2560 if tm <= 1024 else 1024)
    tk = _pick(T, 512, 8)
    nk = T // tk

    def body(a_ref, b_ref, o_ref, cs_ref, acc_ref):
        m = pl.program_id(1)
        k = pl.program_id(2)

        @pl.when(k == 0)
        def _():
            acc_ref[...] = jnp.zeros_like(acc_ref)

        @pl.when((k == 0) & (m == 0))
        def _():
            cs_ref[...] = jnp.zeros_like(cs_ref)

        bv = b_ref[...]
        acc_ref[...] += lax.dot_general(a_ref[...], bv, TN_DIMS, preferred_element_type=F32)

        @pl.when(m == 0)
        def _():
            cs_ref[...] += jnp.sum(bv.astype(F32), 0, keepdims=True)

        @pl.when(k == nk - 1)
        def _():
            o_ref[...] = acc_ref[...].astype(out_dtype)

    return pl.pallas_call(
        body, name=name, grid=(N // tn, M // tm, nk),
        in_specs=[pl.BlockSpec((tk, tm), lambda n, m, k: (k, m)),
                  pl.BlockSpec((tk, tn), lambda n, m, k: (k, n))],
        out_specs=[pl.BlockSpec((tm, tn), lambda n, m, k: (m, n)),
                   pl.BlockSpec((1, tn), lambda n, m, k: (0, n))],
        out_shape=[jax.ShapeDtypeStruct((M, N), out_dtype), jax.ShapeDtypeStruct((1, N), F32)],
        scratch_shapes=[pltpu.VMEM((tm, tn), F32)],
        compiler_params=_params(("arbitrary", "arbitrary", "arbitrary")),
    )(a, b)


def _ext_rows(prev_ref, main_ref, next_ref, i, tm, T):
    before = jnp.where(i == 0, 0.0, prev_ref[...])
    after = jnp.where(i == T // tm - 1, 0.0, next_ref[...])
    return jnp.concatenate([before, main_ref[...], after], axis=0)


def _prev_row(x):
    return pltpu.roll(x, 1, 0)


def _next_row(x):
    return pltpu.roll(x, x.shape[0] - 1, 0)


def _conv3(u, w_ref):
    return _prev_row(u) * w_ref[0:1, :] + u * w_ref[1:2, :] + _next_row(u) * w_ref[2:3, :]


def _main(x, tm):
    return x[HALO:HALO + tm]


def _halo_specs(tm, tc, T, col, order):
    r = tm // HALO
    last = T // HALO - 1
    if order == "ij":
        return (pl.BlockSpec((HALO, tc), lambda i, j: (jnp.maximum(i * r - 1, 0), col(j))),
                pl.BlockSpec((tm, tc), lambda i, j: (i, col(j))),
                pl.BlockSpec((HALO, tc), lambda i, j: (jnp.minimum((i + 1) * r, last), col(j))))
    return (pl.BlockSpec((HALO, tc), lambda j, i: (jnp.maximum(i * r - 1, 0), col(j))),
            pl.BlockSpec((tm, tc), lambda j, i: (i, col(j))),
            pl.BlockSpec((HALO, tc), lambda j, i: (jnp.minimum((i + 1) * r, last), col(j))))


def conv_a_fwd(proj_a, conv_w, name):
    T, D3 = proj_a.shape
    D = D3 // 3
    tm = _pick(T, 256, 8)

    def body(p_ref, m_ref, n_ref, w_ref, o_ref):
        i = pl.program_id(0)
        ext = _ext_rows(p_ref, m_ref, n_ref, i, tm, T)
        u = ext[:, D:2 * D] * ext[:, 2 * D:]
        cu = _conv3(u, w_ref)
        o_ref[...] = (m_ref[:, :D] * _main(cu, tm)).astype(BF16)

    prev, main, nxt = _halo_specs(tm, D3, T, lambda j: 0, "ij")
    return pl.pallas_call(
        body, name=name, grid=(T // tm, 1),
        in_specs=[prev, main, nxt, pl.BlockSpec((3, D), lambda i, j: (0, 0))],
        out_specs=pl.BlockSpec((tm, D), lambda i, j: (i, 0)),
        out_shape=jax.ShapeDtypeStruct((T, D), BF16),
        compiler_params=_params(("parallel", "arbitrary")),
    )(proj_a, proj_a, proj_a, conv_w)


def conv_a_bwd(ds_a, proj_a, conv_w, name):
    T, D3 = proj_a.shape
    D = D3 // 3
    tm = _pick(T, 256, 8)

    def body(dp_ref, dm_ref, dn_ref, p_ref, m_ref, n_ref, w_ref, o_ref, dw_ref):
        i = pl.program_id(0)

        @pl.when(i == 0)
        def _():
            dw_ref[...] = jnp.zeros_like(dw_ref)

        ext = _ext_rows(p_ref, m_ref, n_ref, i, tm, T)
        dsa = _ext_rows(dp_ref, dm_ref, dn_ref, i, tm, T)
        gb, gc, hin = ext[:, :D], ext[:, D:2 * D], ext[:, 2 * D:]
        u = gc * hin
        u_prev, u_next = _prev_row(u), _next_row(u)
        cu = u_prev * w_ref[0:1, :] + u * w_ref[1:2, :] + u_next * w_ref[2:3, :]
        dcu = dsa * gb
        du = _next_row(dcu) * w_ref[0:1, :] + dcu * w_ref[1:2, :] + _prev_row(dcu) * w_ref[2:3, :]
        o_ref[:, :D] = _main(dsa * cu, tm).astype(BF16)
        o_ref[:, D:2 * D] = _main(du * hin, tm).astype(BF16)
        o_ref[:, 2 * D:] = _main(du * gc, tm).astype(BF16)
        dcu_m = _main(dcu, tm)
        dw_ref[0:1, :] += jnp.sum(dcu_m * _main(u_prev, tm), 0, keepdims=True)
        dw_ref[1:2, :] += jnp.sum(dcu_m * _main(u, tm), 0, keepdims=True)
        dw_ref[2:3, :] += jnp.sum(dcu_m * _main(u_next, tm), 0, keepdims=True)

    dprev, dmain, dnxt = _halo_specs(tm, D, T, lambda j: 0, "ij")
    prev, main, nxt = _halo_specs(tm, D3, T, lambda j: 0, "ij")
    return pl.pallas_call(
        body, name=name, grid=(T // tm, 1),
        in_specs=[dprev, dmain, dnxt, prev, main, nxt, pl.BlockSpec((3, D), lambda i, j: (0, 0))],
        out_specs=[pl.BlockSpec((tm, D3), lambda i, j: (i, 0)), pl.BlockSpec((3, D), lambda i, j: (0, 0))],
        out_shape=[jax.ShapeDtypeStruct((T, D3), BF16), jax.ShapeDtypeStruct((3, D), F32)],
        compiler_params=_params(("arbitrary", "arbitrary")),
    )(ds_a, ds_a, ds_a, proj_a, proj_a, proj_a, conv_w)


_INV_SQRT2 = 1.0 / math.sqrt(2.0)
_INV_SQRT_2PI = 1.0 / math.sqrt(2.0 * math.pi)


def conv_f_fwd(up, fcw, fcb, name):
    T, F2 = up.shape
    F = F2 // 2
    tm = _pick(T, 256, 8)
    tc = _pick(F, 1408)
    nc = F // tc

    def body(p_ref, m_ref, n_ref, g_ref, w_ref, b_ref, o_ref):
        i = pl.program_id(0)
        a = _ext_rows(p_ref, m_ref, n_ref, i, tm, T)
        ca = _main(_conv3(a, w_ref), tm) + b_ref[...]
        gl = 0.5 * ca * (1.0 + lax.erf(ca * _INV_SQRT2))
        o_ref[...] = (gl * g_ref[...]).astype(BF16)

    prev, main, nxt = _halo_specs(tm, tc, T, lambda j: j, "ij")
    return pl.pallas_call(
        body, name=name, grid=(T // tm, nc),
        in_specs=[prev, main, nxt,
                  pl.BlockSpec((tm, tc), lambda i, j: (i, nc + j)),
                  pl.BlockSpec((3, tc), lambda i, j: (0, j)),
                  pl.BlockSpec((1, tc), lambda i, j: (0, j))],
        out_specs=pl.BlockSpec((tm, tc), lambda i, j: (i, j)),
        out_shape=jax.ShapeDtypeStruct((T, F), BF16),
        compiler_params=_params(("parallel", "parallel")),
    )(up, up, up, up, fcw, fcb)


def conv_f_bwd(df, up, fcw, fcb, name):
    T, F2 = up.shape
    F = F2 // 2
    tm = _pick(T, 256, 8)
    tc = _pick(F, 1408)
    nc = F // tc

    def body(fp_ref, fm_ref, fn_ref, ap_ref, am_ref, an_ref, gp_ref, gm_ref, gn_ref, w_ref, b_ref,
             da_ref, dg_ref, csa_ref, csg_ref, dfb_ref, dfw_ref):
        i = pl.program_id(1)

        @pl.when(i == 0)
        def _():
            csa_ref[...] = jnp.zeros_like(csa_ref)
            csg_ref[...] = jnp.zeros_like(csg_ref)
            dfb_ref[...] = jnp.zeros_like(dfb_ref)
            dfw_ref[...] = jnp.zeros_like(dfw_ref)

        dfe = _ext_rows(fp_ref, fm_ref, fn_ref, i, tm, T)
        a = _ext_rows(ap_ref, am_ref, an_ref, i, tm, T)
        gate = _ext_rows(gp_ref, gm_ref, gn_ref, i, tm, T)
        a_prev, a_next = _prev_row(a), _next_row(a)
        ca = a_prev * w_ref[0:1, :] + a * w_ref[1:2, :] + a_next * w_ref[2:3, :] + b_ref[...]
        cdf = 0.5 * (1.0 + lax.erf(ca * _INV_SQRT2))
        gl = ca * cdf
        gp = cdf + ca * (jnp.exp(-0.5 * ca * ca) * _INV_SQRT_2PI)
        dgate = _main(dfe * gl, tm)
        dca = dfe * gate * gp
        da = _main(_next_row(dca) * w_ref[0:1, :] + dca * w_ref[1:2, :] + _prev_row(dca) * w_ref[2:3, :], tm)
        da_ref[...] = da.astype(BF16)
        dg_ref[...] = dgate.astype(BF16)
        csa_ref[...] += jnp.sum(da, 0, keepdims=True)
        csg_ref[...] += jnp.sum(dgate, 0, keepdims=True)
        dca_m = _main(dca, tm)
        dfb_ref[...] += jnp.sum(dca_m, 0, keepdims=True)
        dfw_ref[0:1, :] += jnp.sum(dca_m * _main(a_prev, tm), 0, keepdims=True)
        dfw_ref[1:2, :] += jnp.sum(dca_m * _main(a, tm), 0, keepdims=True)
        dfw_ref[2:3, :] += jnp.sum(dca_m * _main(a_next, tm), 0, keepdims=True)

    fprev, fmain, fnxt = _halo_specs(tm, tc, T, lambda j: j, "ji")
    gprev, gmain, gnxt = _halo_specs(tm, tc, T, lambda j: nc + j, "ji")
    tile = pl.BlockSpec((tm, tc), lambda j, i: (i, j))
    vec = pl.BlockSpec((1, tc), lambda j, i: (0, j))
    vec3 = pl.BlockSpec((3, tc), lambda j, i: (0, j))
    return pl.pallas_call(
        body, name=name, grid=(nc, T // tm),
        in_specs=[fprev, fmain, fnxt, fprev, fmain, fnxt, gprev, gmain, gnxt, vec3, vec],
        out_specs=[tile, tile, vec, vec, vec, vec3],
        out_shape=[jax.ShapeDtypeStruct((T, F), BF16), jax.ShapeDtypeStruct((T, F), BF16),
                   jax.ShapeDtypeStruct((1, F), F32), jax.ShapeDtypeStruct((1, F), F32),
                   jax.ShapeDtypeStruct((1, F), F32), jax.ShapeDtypeStruct((3, F), F32)],
        compiler_params=_params(("arbitrary", "arbitrary")),
    )(df, df, df, up, up, up, up, up, up, fcw, fcb)


def gate_fwd(proj_g, y_a, y_b, name):
    T, D = y_a.shape
    tm = _pick(T, 512, 8)

    def body(g_ref, a_ref, b_ref, o_ref):
        sa = jax.nn.sigmoid(g_ref[:, :D])
        sb = jax.nn.sigmoid(g_ref[:, D:])
        o_ref[...] = (sa * a_ref[...] + sb * b_ref[...]).astype(BF16)

    row = pl.BlockSpec((tm, D), lambda i: (i, 0))
    return pl.pallas_call(
        body, name=name, grid=(T // tm,),
        in_specs=[pl.BlockSpec((tm, 2 * D), lambda i: (i, 0)), row, row],
        out_specs=row,
        out_shape=jax.ShapeDtypeStruct((T, D), BF16),
        compiler_params=_params(("parallel",)),
    )(proj_g, y_a, y_b)


def gate_bwd(dz, proj_g, y_a, y_b, name):
    T, D = y_a.shape
    tm = _pick(T, 512, 8)

    def body(dz_ref, g_ref, a_ref, b_ref, da_ref, db_ref, dg_ref):
        dzv = dz_ref[...]
        sa = jax.nn.sigmoid(g_ref[:, :D])
        sb = jax.nn.sigmoid(g_ref[:, D:])
        da_ref[...] = (dzv * sa).astype(BF16)
        db_ref[...] = (dzv * sb).astype(BF16)
        dg_ref[:, :D] = (dzv * a_ref[...] * (sa * (1.0 - sa))).astype(BF16)
        dg_ref[:, D:] = (dzv * b_ref[...] * (sb * (1.0 - sb))).astype(BF16)

    row = pl.BlockSpec((tm, D), lambda i: (i, 0))
    wide = pl.BlockSpec((tm, 2 * D), lambda i: (i, 0))
    return pl.pallas_call(
        body, name=name, grid=(T // tm,),
        in_specs=[row, wide, row, row],
        out_specs=[row, row, wide],
        out_shape=[jax.ShapeDtypeStruct((T, D), BF16), jax.ShapeDtypeStruct((T, D), BF16),
                   jax.ShapeDtypeStruct((T, 2 * D), BF16)],
        compiler_params=_params(("parallel",)),
    )(dz, proj_g, y_a, y_b)


ATT_WIN = ATT_TQ + 2 * RADIUS
ATT_STEP = 512
FAR = 1e32


def _att_window(qs, L):
    ks = pl.multiple_of(jnp.clip(qs - RADIUS, 0, L - ATT_WIN), RADIUS)
    return ks, jnp.where(qs == 0, 0, jnp.where(qs == L - ATT_TQ, 2, 1))


def _fill_bias_tables(bias_ref, sl_ref, hp, d):
    col_row = (lax.broadcasted_iota(jnp.int32, (ATT_TQ, ATT_WIN), 1)
               - lax.broadcasted_iota(jnp.int32, (ATT_TQ, ATT_WIN), 0))
    for v in range(3):
        ad = jnp.abs(col_row - v * RADIUS)
        dist = jnp.where(ad <= RADIUS, (ad * d).astype(F32), FAR)
        bias_ref[v, 0:ATT_TQ, :] = sl_ref[hp * 2] * dist
        bias_ref[v, ATT_TQ:2 * ATT_TQ, :] = sl_ref[hp * 2 + 1] * dist


def _head_masks():
    lane = lax.broadcasted_iota(jnp.int32, (1, LANES), 1)
    return [lane < HEAD_DIM, lane >= HEAD_DIM]


def _stack_heads(x, masks):
    zero = jnp.zeros_like(x)
    return jnp.concatenate([jnp.where(masks[0], x, zero), jnp.where(masks[1], x, zero)], axis=0)


def _unstack_heads(x2, masks):
    n = x2.shape[0] // 2
    return jnp.where(masks[0], x2[:n], x2[n:])


def _att_step(L):
    step = min(ATT_STEP, L)
    assert L % step == 0 and step % ATT_TQ == 0 and L >= ATT_WIN
    return step


def att_fwd(qkv, group, name):
    d, L, _ = qkv.shape
    step = _att_step(L)
    cg = GROUP_W // LANES
    slopes = jnp.asarray(_alibi_slopes()[group])
    scale = HEAD_DIM ** -0.5

    def body(sl_ref, q_ref, k_ref, v_ref, o_ref, l_ref, bias_ref):
        hp = pl.program_id(1)
        i = pl.program_id(2)

        @pl.when(i == 0)
        def _():
            _fill_bias_tables(bias_ref, sl_ref, hp, d)

        masks = _head_masks()
        for t in range(step // ATT_TQ):
            rows = slice(t * ATT_TQ, (t + 1) * ATT_TQ)
            ks, table = _att_window(i * step + t * ATT_TQ, L)
            q2 = _stack_heads(q_ref[rows, :] * scale, masks)
            kw = k_ref[pl.ds(ks, ATT_WIN), :]
            vw = v_ref[pl.ds(ks, ATT_WIN), :]
            s = lax.dot_general(q2, kw, NT_DIMS, preferred_element_type=F32) - bias_ref[table]
            m = jnp.max(s, -1, keepdims=True)
            p = jnp.exp(s - m)
            den = jnp.sum(p, -1, keepdims=True)
            pn = (p / den).astype(BF16)
            o2 = jnp.dot(pn, vw, preferred_element_type=F32)
            o_ref[rows, :] = _unstack_heads(o2, masks)
            l_ref[rows, :] = _unstack_heads(m + jnp.log(den), masks)

    out_spec = pl.BlockSpec((None, step, LANES), lambda r, hp, i: (r, i, hp))
    return pl.pallas_call(
        body, name=name, grid=(d, cg, L // step),
        in_specs=[pl.BlockSpec(memory_space=pltpu.SMEM),
                  pl.BlockSpec((None, step, LANES), lambda r, hp, i: (r, i, hp)),
                  pl.BlockSpec((None, L, LANES), lambda r, hp, i: (r, 0, cg + hp)),
                  pl.BlockSpec((None, L, LANES), lambda r, hp, i: (r, 0, 2 * cg + hp))],
        out_specs=[out_spec, out_spec],
        out_shape=[jax.ShapeDtypeStruct((d, L, GROUP_W), F32)] * 2,
        scratch_shapes=[pltpu.VMEM((3, 2 * ATT_TQ, ATT_WIN), F32)],
        compiler_params=_params(("arbitrary", "arbitrary", "arbitrary")),
    )(slopes, qkv, qkv, qkv)


def att_bwd(qkv, do, lse, dmat, group, name):
    d, L, _ = qkv.shape
    step = _att_step(L)
    nq = L // step
    cg = GROUP_W // LANES
    slopes = jnp.asarray(_alibi_slopes()[group])
    scale = HEAD_DIM ** -0.5

    def body(sl_ref, q_ref, k_ref, v_ref, do_ref, l_ref, dm_ref, dq_ref, dk_ref, dv_ref, dk_acc, dv_acc, bias_ref):
        hp = pl.program_id(1)
        i = pl.program_id(2)

        @pl.when(i == 0)
        def _():
            dk_acc[...] = jnp.zeros_like(dk_acc)
            dv_acc[...] = jnp.zeros_like(dv_acc)
            _fill_bias_tables(bias_ref, sl_ref, hp, d)

        masks = _head_masks()

        def head_cols(x):
            return jnp.concatenate([jnp.max(jnp.where(hm, x, -jnp.inf), -1, keepdims=True) for hm in masks], axis=0)

        for t in range(step // ATT_TQ):
            rows = slice(t * ATT_TQ, (t + 1) * ATT_TQ)
            ks, table = _att_window(i * step + t * ATT_TQ, L)
            q2 = _stack_heads(q_ref[rows, :] * scale, masks)
            do2 = _stack_heads(do_ref[rows, :], masks)
            kw = k_ref[pl.ds(ks, ATT_WIN), :]
            vw = v_ref[pl.ds(ks, ATT_WIN), :]
            s = lax.dot_general(q2, kw, NT_DIMS, preferred_element_type=F32) - bias_ref[table]
            p = jnp.exp(s - head_cols(l_ref[rows, :]))
            dp = lax.dot_general(do2, vw, NT_DIMS, preferred_element_type=F32)
            ds = (p * (dp - head_cols(dm_ref[rows, :]))).astype(BF16)
            dq2 = jnp.dot(ds, kw, preferred_element_type=F32)
            dq_ref[rows, :] = (_unstack_heads(dq2, masks) * scale).astype(BF16)
            dk_acc[pl.ds(ks, ATT_WIN), :] += lax.dot_general(ds, q2, TN_DIMS, preferred_element_type=F32)
            dv_acc[pl.ds(ks, ATT_WIN), :] += lax.dot_general(p.astype(BF16), do2, TN_DIMS, preferred_element_type=F32)

        @pl.when(i == nq - 1)
        def _():
            dk_ref[...] = dk_acc[...].astype(BF16)
            dv_ref[...] = dv_acc[...].astype(BF16)

    tile = pl.BlockSpec((None, step, LANES), lambda r, hp, i: (r, i, hp))
    whole = pl.BlockSpec((None, L, LANES), lambda r, hp, i: (r, 0, hp))
    return pl.pallas_call(
        body, name=name, grid=(d, cg, nq),
        in_specs=[pl.BlockSpec(memory_space=pltpu.SMEM), tile,
                  pl.BlockSpec((None, L, LANES), lambda r, hp, i: (r, 0, cg + hp)),
                  pl.BlockSpec((None, L, LANES), lambda r, hp, i: (r, 0, 2 * cg + hp)),
                  tile, tile, tile],
        out_specs=[tile, whole, whole],
        out_shape=[jax.ShapeDtypeStruct((d, L, GROUP_W), BF16)] * 3,
        scratch_shapes=[pltpu.VMEM((L, LANES), F32), pltpu.VMEM((L, LANES), F32),
                        pltpu.VMEM((3, 2 * ATT_TQ, ATT_WIN), F32)],
        compiler_params=_params(("arbitrary", "arbitrary", "arbitrary")),
    )(slopes, qkv, qkv, qkv, do, lse, dmat)


def _group_weights(ls):
    m = jnp.maximum(jnp.maximum(ls[0], ls[1]), ls[2])
    es = [jnp.exp(l - m) for l in ls]
    tot = es[0] + es[1] + es[2]
    return [e / tot for e in es]


def combine_fwd(outs, lses, name):
    T = outs[0].shape[0] * outs[0].shape[1]
    tm = _pick(T, 512, 8)
    n_scr = 2 * (len(DILATIONS) - 1)

    def body(*refs):
        o_refs, l_refs, c_ref, scr = refs[:3], refs[3:6], refs[6], refs[7:]
        o = [_load_natural(o_refs[g], d, scr[g - 1] if g else None) for g, d in enumerate(DILATIONS)]
        l = [_load_natural(l_refs[g], d, scr[g + 1] if g else None) for g, d in enumerate(DILATIONS)]
        w = _group_weights(l)
        c_ref[...] = (w[0] * o[0] + w[1] * o[1] + w[2] * o[2]).astype(BF16)

    specs = [_residue_spec(tm, d, GROUP_W) for d in DILATIONS]
    return pl.pallas_call(
        body, name=name, grid=(T // tm,),
        in_specs=specs + specs, out_specs=pl.BlockSpec((tm, GROUP_W), lambda i: (i, 0)),
        out_shape=jax.ShapeDtypeStruct((T, GROUP_W), BF16),
        scratch_shapes=[_residue_scratch(tm, GROUP_W)] * n_scr,
        compiler_params=_params(("parallel",)),
    )(*outs, *lses)


def combine_bwd(dcomb, outs, lses, name):
    T = dcomb.shape[0]
    tm = _pick(T, 256, 8)
    head = np.arange(GROUP_W) // HEAD_DIM
    seg = jnp.asarray((head[:, None] == head[None, :]).astype(np.float32)).astype(BF16)
    ng = len(DILATIONS)
    n_scr = 4 * (ng - 1)

    def body(*refs):
        dc_ref, o_refs, l_refs, e_ref = refs[0], refs[1:1 + ng], refs[1 + ng:1 + 2 * ng], refs[1 + 2 * ng]
        do_refs, dm_refs = refs[2 + 2 * ng:2 + 3 * ng], refs[2 + 3 * ng:2 + 4 * ng]
        scr = refs[2 + 4 * ng:]
        o = [_load_natural(o_refs[g], d, scr[4 * (g - 1)] if g else None) for g, d in enumerate(DILATIONS)]
        l = [_load_natural(l_refs[g], d, scr[4 * (g - 1) + 1] if g else None) for g, d in enumerate(DILATIONS)]
        w = _group_weights(l)
        dc = dc_ref[...]
        e = e_ref[...]
        tot = jnp.zeros_like(dc)
        for g in range(ng):
            prod = dc * o[g]
            dw = jnp.zeros_like(dc)
            for _ in range(3):
                part = prod.astype(BF16)
                dw = dw + jnp.dot(part, e, preferred_element_type=F32)
                prod = prod - part.astype(F32)
            tot = tot + w[g] * dw
        for g, d in enumerate(DILATIONS):
            _store_by_residue(w[g] * dc, do_refs[g], d, scr[4 * (g - 1) + 2] if g else None)
            _store_by_residue(w[g] * tot, dm_refs[g], d, scr[4 * (g - 1) + 3] if g else None)

    specs = [_residue_spec(tm, d, GROUP_W) for d in DILATIONS]
    res = pl.pallas_call(
        body, name=name, grid=(T // tm,),
        in_specs=[pl.BlockSpec((tm, GROUP_W), lambda i: (i, 0))] + specs + specs
        + [pl.BlockSpec((GROUP_W, GROUP_W), lambda i: (0, 0))],
        out_specs=specs + specs,
        out_shape=[jax.ShapeDtypeStruct(o.shape, BF16) for o in outs] + [jax.ShapeDtypeStruct(o.shape, F32) for o in outs],
        scratch_shapes=[_residue_scratch(tm, GROUP_W)] * n_scr,
        compiler_params=_params(("parallel",)),
    )(dcomb, *outs, *lses, seg)
    return res[:ng], res[ng:]


def _position():
    return lax.axis_index("x"), lax.axis_index("y"), lax.axis_index("c")


def _other_chips(x, y):
    return [(1 - x, y), (x, 1 - y), (1 - x, 1 - y)]


def _remote(src, dst, send_sems, recv_sems, k, to):
    return pltpu.make_async_remote_copy(src_ref=src, dst_ref=dst, send_sem=send_sems.at[k], recv_sem=recv_sems.at[k],
                                        device_id=to, device_id_type=MESH)


def all_gather(shards, name):
    n = len(shards)

    def body(*refs):
        ins, outs = refs[:n], refs[n:2 * n]
        send_sems, recv_sems, local_sems = refs[2 * n:]
        x, y, c = _position()
        sibling = (x, y, 1 - c)
        chips = _other_chips(x, y)

        def block(a, px, py, pc):
            return outs[a].at[4 * px + 2 * py + pc]

        own, first, passed = [], [], []
        for a in range(n):
            cp = pltpu.make_async_copy(ins[a], block(a, x, y, c), local_sems.at[a])
            cp.start()
            own.append(cp)
            k0 = 7 * a
            first.append(_remote(ins[a], block(a, x, y, c), send_sems, recv_sems, k0, sibling))
            for j, chip in enumerate(chips):
                first.append(_remote(ins[a], block(a, x, y, c), send_sems, recv_sems, k0 + 1 + j, (*chip, c)))
        for cp in first:
            cp.start()
        for a in range(n):
            k0 = 7 * a
            for j, chip in enumerate(chips):
                got = block(a, *chip, c)
                _remote(got, got, send_sems, recv_sems, k0 + 1 + j, sibling).wait_recv()
                fwd = _remote(got, got, send_sems, recv_sems, k0 + 4 + j, sibling)
                fwd.start()
                passed.append(fwd)
        for a in range(n):
            k0 = 7 * a
            got = block(a, x, y, 1 - c)
            _remote(got, got, send_sems, recv_sems, k0, sibling).wait_recv()
            for j, chip in enumerate(chips):
                got = block(a, *chip, 1 - c)
                _remote(got, got, send_sems, recv_sems, k0 + 4 + j, sibling).wait_recv()
        for cp in first + passed:
            cp.wait_send()
        for cp in own:
            cp.wait()

    hbm = pl.BlockSpec(memory_space=pl.ANY)
    return pl.pallas_call(
        body, name=name,
        in_specs=[hbm] * n, out_specs=[hbm] * n,
        out_shape=[jax.ShapeDtypeStruct((N_DEV,) + s.shape, s.dtype) for s in shards],
        scratch_shapes=[pltpu.SemaphoreType.DMA((7 * n,)), pltpu.SemaphoreType.DMA((7 * n,)),
                        pltpu.SemaphoreType.DMA((n,))],
    )(*shards)


def exchange_sibling(parts, name):
    n = len(parts)

    def body(*refs):
        ins, outs = refs[:n], refs[n:2 * n]
        send_sems, recv_sems = refs[2 * n:]
        x, y, c = _position()
        sibling = (x, y, 1 - c)
        copies = []
        for a in range(n):
            for q in range(4):
                cp = _remote(ins[a].at[2 * q + (1 - c)], outs[a].at[q], send_sems, recv_sems, 4 * a + q, sibling)
                cp.start()
                copies.append(cp)
        for cp in copies:
            cp.wait_recv()
        for cp in copies:
            cp.wait_send()

    hbm = pl.BlockSpec(memory_space=pl.ANY)
    return pl.pallas_call(
        body, name=name,
        in_specs=[hbm] * n, out_specs=[hbm] * n,
        out_shape=[jax.ShapeDtypeStruct((4,) + p.shape[1:], p.dtype) for p in parts],
        scratch_shapes=[pltpu.SemaphoreType.DMA((4 * n,)), pltpu.SemaphoreType.DMA((4 * n,))],
    )(*parts)


def exchange_chips(sums, name):
    n = len(sums)

    def body(*refs):
        ins, outs = refs[:n], refs[n:2 * n]
        send_sems, recv_sems = refs[2 * n:]
        x, y, c = _position()
        copies = []
        for a in range(n):
            for j, (cx, cy) in enumerate(_other_chips(x, y)):
                cp = _remote(ins[a].at[2 * cx + cy], outs[a].at[j], send_sems, recv_sems, 3 * a + j, (cx, cy, c))
                cp.start()
                copies.append(cp)
        for cp in copies:
            cp.wait_recv()
        for cp in copies:
            cp.wait_send()

    hbm = pl.BlockSpec(memory_space=pl.ANY)
    return pl.pallas_call(
        body, name=name,
        in_specs=[hbm] * n, out_specs=[hbm] * n,
        out_shape=[jax.ShapeDtypeStruct((3,) + s.shape[1:], s.dtype) for s in sums],
        scratch_shapes=[pltpu.SemaphoreType.DMA((3 * n,)), pltpu.SemaphoreType.DMA((3 * n,))],
    )(*sums)


_HBM = pl.BlockSpec(memory_space=pltpu.HBM)
_SEM = pl.BlockSpec(memory_space=pltpu.SEMAPHORE)
_DATAFLOW = pltpu.SideEffectType.DATAFLOW_SIDE_EFFECTING


def _to_all_plan(srcs, lands, send_sems, recv_sems):
    x, y, c = _position()
    me = 4 * x + 2 * y + c
    copies = []
    for a in range(len(srcs)):
        for k in range(1, N_DEV):
            fx, fy, fc = (k >> 2) & 1, (k >> 1) & 1, k & 1
            to = (1 - x if fx else x, 1 - y if fy else y, 1 - c if fc else c)
            copies.append(_remote(srcs[a], lands[a].at[me], send_sems, recv_sems, (N_DEV - 1) * a + k - 1, to))
    return copies


def _to_chips_plan(srcs, lands, send_sems, recv_sems):
    x, y, c = _position()
    copies = []
    for a in range(len(srcs)):
        for j, (cx, cy) in enumerate(_other_chips(x, y)):
            copies.append(_remote(srcs[a].at[2 * cx + cy], lands[a].at[j], send_sems, recv_sems, 3 * a + j, (cx, cy, c)))
    return copies


def copies_start(srcs, land_shapes, plan, per_array, name):
    n = len(srcs)
    n_sem = per_array * n
    lands = [lax.empty(s.shape, s.dtype) for s in land_shapes]

    def body(*refs):
        src_refs, land_refs = refs[:n], refs[n:2 * n]
        send_sems, recv_sems = refs[2 * n], refs[2 * n + 1]
        token = refs[-1]
        for cp in plan(src_refs, land_refs, send_sems, recv_sems):
            cp.start()
        token[...] = jnp.zeros_like(token)

    out = pl.pallas_call(
        body, name=name,
        out_shape=(pltpu.SemaphoreType.DMA((n_sem,)), pltpu.SemaphoreType.DMA((n_sem,)))
        + tuple(pltpu.HBM(s.shape, s.dtype) for s in srcs)
        + tuple(pltpu.HBM(s.shape, s.dtype) for s in land_shapes)
        + (jax.ShapeDtypeStruct((8, LANES), F32),),
        in_specs=[_HBM] * (2 * n),
        out_specs=(_SEM, _SEM) + (_HBM,) * (2 * n) + (pl.BlockSpec(memory_space=pltpu.VMEM),),
        input_output_aliases={i: 2 + i for i in range(2 * n)},
        compiler_params=pltpu.CompilerParams(has_side_effects=_DATAFLOW),
    )(*[pltpu.with_memory_space_constraint(s, pltpu.HBM) for s in srcs],
      *[pltpu.with_memory_space_constraint(l, pltpu.HBM) for l in lands])
    return out[:-1], out[-1]


def copies_wait(handles, plan, after, name):
    send_sems, recv_sems = handles[0], handles[1]
    n = (len(handles) - 2) // 2
    thru = handles[2:]

    def body(*refs):
        src_refs, land_refs = refs[:n], refs[n:2 * n]
        send_sems, recv_sems = refs[2 * n], refs[2 * n + 1]
        copies = plan(src_refs, land_refs, send_sems, recv_sems)
        for cp in copies:
            cp.wait_recv()
        for cp in copies:
            cp.wait_send()

    out = pl.pallas_call(
        body, name=name,
        out_shape=tuple(pltpu.HBM(t.shape, t.dtype) for t in thru),
        in_specs=[_HBM] * (2 * n) + [_SEM, _SEM, pl.BlockSpec(memory_space=pl.ANY)],
        out_specs=(_HBM,) * (2 * n),
        input_output_aliases={i: i for i in range(2 * n)},
        compiler_params=pltpu.CompilerParams(has_side_effects=_DATAFLOW),
    )(*thru, send_sems, recv_sems, after)
    return out[n:]


def all_sum_small(vec, name):
    R = vec.shape[0]

    def body(v_ref, tot_ref, all_ref, send_sems, recv_sems):
        x, y, c = _position()
        me = 4 * x + 2 * y + c
        all_ref[me] = v_ref[...]
        copies = []
        for k in range(1, N_DEV):
            fx, fy, fc = (k >> 2) & 1, (k >> 1) & 1, k & 1
            to = (1 - x if fx else x, 1 - y if fy else y, 1 - c if fc else c)
            cp = _remote(v_ref, all_ref.at[me], send_sems, recv_sems, k - 1, to)
            cp.start()
            copies.append(cp)
        for cp in copies:
            cp.wait_recv()
        for cp in copies:
            cp.wait_send()
        tot = all_ref[0]
        for j in range(1, N_DEV):
            tot = tot + all_ref[j]
        tot_ref[...] = tot

    vmem = pl.BlockSpec(memory_space=pltpu.VMEM)
    return pl.pallas_call(
        body, name=name,
        in_specs=[vmem], out_specs=vmem,
        out_shape=jax.ShapeDtypeStruct((R, LANES), F32),
        scratch_shapes=[pltpu.VMEM((N_DEV, R, LANES), F32),
                        pltpu.SemaphoreType.DMA((N_DEV - 1,)), pltpu.SemaphoreType.DMA((N_DEV - 1,))],
        compiler_params=pltpu.CompilerParams(vmem_limit_bytes=VMEM_LIMIT),
    )(vec)


def pair_add(mine, theirs, name):
    _, R, C = mine.shape
    tr = _pick(R, 256, 8)

    def body(a_ref, b_ref, o_ref):
        o_ref[...] = (a_ref[...].astype(F32) + b_ref[...].astype(F32)).astype(BF16)

    blk = pl.BlockSpec((None, tr, C), lambda q, i: (q, i, 0))
    return pl.pallas_call(
        body, name=name, grid=(4, R // tr),
        in_specs=[blk, blk], out_specs=blk,
        out_shape=jax.ShapeDtypeStruct(mine.shape, BF16),
        compiler_params=_params(("parallel", "parallel")),
    )(mine, theirs)


def _adamw_math(w, g, m, v):
    m = ADAM_B1 * m + (1.0 - ADAM_B1) * g
    v = ADAM_B2 * v + (1.0 - ADAM_B2) * jnp.square(g)
    m_hat = m / (1.0 - ADAM_B1 ** ADAM_STEP)
    v_hat = v / (1.0 - ADAM_B2 ** ADAM_STEP)
    delta = -ADAM_LR * (m_hat / (jnp.sqrt(v_hat) + ADAM_EPS) + ADAM_WD * w)
    return delta, m, v


def adamw_sharded(w, m, v, own, sib, others, name):
    R, C = w.shape
    tr = _pick(R, 256, 8)

    def body(w_ref, m_ref, v_ref, a_ref, b_ref, o_ref, g_ref, d_ref, nm_ref, nv_ref):
        g = a_ref[...].astype(F32) + b_ref[...].astype(F32)
        for j in range(3):
            g = g + o_ref[j].astype(F32)
        delta, nm, nv = _adamw_math(w_ref[...], g, m_ref[...], v_ref[...])
        g_ref[...] = g
        d_ref[...] = delta
        nm_ref[...] = nm
        nv_ref[...] = nv

    row = pl.BlockSpec((tr, C), lambda i: (i, 0))
    return pl.pallas_call(
        body, name=name, grid=(R // tr,),
        in_specs=[row] * 5 + [pl.BlockSpec((3, tr, C), lambda i: (0, i, 0))],
        out_specs=[row] * 4,
        out_shape=[jax.ShapeDtypeStruct((R, C), F32)] * 4,
        compiler_params=_params(("parallel",)),
    )(w, m, v, own, sib, others)


def adamw_packed(w, g, m, v, name):
    R = w.shape[0]

    def body(w_ref, g_ref, m_ref, v_ref, d_ref, nm_ref, nv_ref):
        delta, nm, nv = _adamw_math(w_ref[...], g_ref[...], m_ref[...], v_ref[...])
        d_ref[...] = delta
        nm_ref[...] = nm
        nv_ref[...] = nv

    full = pl.BlockSpec((R, LANES), lambda i: (0, 0))
    return pl.pallas_call(
        body, name=name, grid=(1,),
        in_specs=[full] * 4, out_specs=[full] * 3,
        out_shape=[jax.ShapeDtypeStruct((R, LANES), F32)] * 3,
        compiler_params=_params(("arbitrary",)),
    )(w, g, m, v)


def _pack(arrays):
    flat = []
    sizes = []
    for a in arrays:
        f = a.reshape(-1).astype(F32)
        pad = (-f.shape[0]) % LANES
        if pad:
            f = jnp.concatenate([f, jnp.zeros((pad,), F32)])
        flat.append(f)
        sizes.append(f.shape[0])
    rows = sum(sizes) // LANES
    pad_rows = (-rows) % 8
    if pad_rows:
        flat.append(jnp.zeros((pad_rows * LANES,), F32))
    return jnp.concatenate(flat).reshape(-1, LANES), sizes


def _unpack(packed, sizes, shapes):
    flat = packed.reshape(-1)
    out = []
    off = 0
    for size, shape in zip(sizes, shapes):
        n = int(np.prod(shape))
        out.append(flat[off:off + n].reshape(shape))
        off += size
    return out


def _to_blocks(full, axis):
    if axis == 0:
        return full.reshape(N_DEV, full.shape[0] // N_DEV, full.shape[1])
    r, n = full.shape
    return full.reshape(r, N_DEV, n // N_DEV).transpose(1, 0, 2)


def _from_blocks(blocks, axis):
    if axis == 0:
        return blocks.reshape(blocks.shape[0] * blocks.shape[1], blocks.shape[2])
    return blocks.transpose(1, 0, 2).reshape(blocks.shape[1], blocks.shape[0] * blocks.shape[2])


def kernel(x, ln0_g, ln0_b, w_in, b_in, conv_w, w_a, w_b, w_o, b_o, ln1_g, ln1_b, w_up, b_up, ffn_conv_w, ffn_conv_b, w_down, b_down, ln2_g, ln2_b, loss_target, m_ln0_g, m_ln0_b, m_w_in, m_b_in, m_conv_w, m_w_a, m_w_b, m_w_o, m_b_o, m_ln1_g, m_ln1_b, m_w_up, m_b_up, m_ffn_conv_w, m_ffn_conv_b, m_w_down, m_b_down, m_ln2_g, m_ln2_b, v_ln0_g, v_ln0_b, v_w_in, v_b_in, v_conv_w, v_w_a, v_w_b, v_w_o, v_b_o, v_ln1_g, v_ln1_b, v_w_up, v_b_up, v_ffn_conv_w, v_ffn_conv_b, v_w_down, v_b_down, v_ln2_g, v_ln2_b):
    T, D = x.shape[1], x.shape[2]
    F = ffn_conv_b.shape[-1]
    xs = x.reshape(T, D)
    tgt = loss_target.reshape(T, D)
    dev = 4 * lax.axis_index("x") + 2 * lax.axis_index("y") + lax.axis_index("c")
    chip = 2 * lax.axis_index("x") + lax.axis_index("y")
    core = lax.axis_index("c")

    big = dict(w_in=(w_in[0], 1), w_a=(w_a[0], 0), w_b=(w_b[0], 1), w_o=(w_o[0], 0), w_up=(w_up[0], 1),
               w_down=(w_down[0], 0))
    names = list(big)
    shards = {k: big[k][0].astype(BF16) for k in names}
    g_in, g_conv, g_fcw = all_gather([shards["w_in"], conv_w[0], ffn_conv_w[0]], "gather_w_in")
    full = {"w_in": _from_blocks(g_in, 1)}
    conv_full = _from_blocks(g_conv, 1)
    fcw_full = _from_blocks(g_fcw, 1)
    late_groups = (("w_a", "w_b", "w_o"), ("w_up", "w_down"))
    late_handles = []
    token = None
    for n, keys in enumerate(late_groups):
        srcs = [shards[k] if token is None else shards[k] + token[0, 0].astype(BF16) for k in keys]
        handles, token = copies_start(srcs, [jax.ShapeDtypeStruct((N_DEV,) + s.shape, BF16) for s in srcs],
                                      _to_all_plan, N_DEV - 1, f"gather_late_{n}_start")
        late_handles.append(handles)

    def late_weights(n, after):
        lands = copies_wait(late_handles[n], _to_all_plan, after, f"gather_late_{n}_wait")
        for k, land in zip(late_groups[n], lands):
            full[k] = _from_blocks(lax.dynamic_update_index_in_dim(land, shards[k], dev, 0), big[k][1])

    o_q = 3 * D
    o_g = o_q + 3 * QKV_W
    w_pa, w_qkv, w_pg = full["w_in"][:, :o_q], full["w_in"][:, o_q:o_g], full["w_in"][:, o_g:]
    b_pa, b_qkv, b_pg = b_in[:, :o_q], b_in[:, o_q:o_g], b_in[:, o_g:]
    ln0g, ln0b = ln0_g.reshape(1, D), ln0_b.reshape(1, D)

    h0, h0b, *h0_res = ln_fwd(xs, None, ln0g, ln0b, "ln0_fwd", dilations=DILATIONS[1:])
    h0_res = [h0b] + [h.reshape(T, D) for h in h0_res]
    proj_a = mm_nn(h0b, w_pa, b_pa, F32, "proj_conv", after=token)
    proj_g = mm_nn(h0b, w_pg, b_pg, F32, "proj_gates")
    zero_d = jnp.zeros((1, D), F32)
    s_a = conv_a_fwd(proj_a, conv_full, "conv_a_fwd")
    late_weights(0, s_a)
    y_a = mm_nn(s_a, full["w_a"], zero_d, F32, "branch_a_out")

    def group_cols(m, g):
        return jnp.concatenate([m[:, s * QKV_W + g * GROUP_W:s * QKV_W + (g + 1) * GROUP_W] for s in range(3)], 1)

    w_grp = [group_cols(w_qkv, g) for g in range(3)]
    qkvs, outs, lses = [], [], []
    for g, d in enumerate(DILATIONS):
        qkv = mm_nn(h0_res[g], w_grp[g], group_cols(b_qkv, g), BF16, f"proj_qkv_{g}").reshape(d, T // d, 3 * GROUP_W)
        o, l = att_fwd(qkv, g, f"att_fwd_{g}")
        qkvs.append(qkv)
        outs.append(o)
        lses.append(l)
    comb = combine_fwd(outs, lses, "combine_fwd")
    y_b = mm_nn(comb, full["w_b"], zero_d, F32, "branch_b_out")
    z = gate_fwd(proj_g, y_a, y_b, "gate_fwd")
    mix = mm_nn(z, full["w_o"], b_o, F32, "mix_out")
    h1, h1b = ln_fwd(h0, mix, ln1_g, ln1_b, "ln1_fwd")
    late_weights(1, h1b)
    up = mm_nn(h1b, full["w_up"], b_up, F32, "ffn_up")
    f_act = conv_f_fwd(up, fcw_full, ffn_conv_b, "conv_f_fwd")
    ffn = mm_nn(f_act, full["w_down"], b_down, F32, "ffn_down")

    dr2, dr2b, d_ln2_g, d_ln2_b, d_b_down, loss_part = ln_bwd(h1, ffn, ln2_g, ln2_b, None, None, tgt, "ln2_loss_bwd")
    dw_down, _ = mm_tn(f_act, dr2b, "dw_down")
    df = mm_nt(dr2b, full["w_down"], None, "d_ffn_act")
    d_a, d_gate, cs_a, cs_gate, d_fcb, d_fcw = conv_f_bwd(df, up, fcw_full, ffn_conv_b, "conv_f_bwd")
    dw_up_a, _ = mm_tn(h1b, d_a, "dw_up_a")
    dw_up_g, _ = mm_tn(h1b, d_gate, "dw_up_gate")
    dh1 = mm_nt(d_a, full["w_up"][:, :F], None, "d_h1_a")
    dh1 = mm_nt(d_gate, full["w_up"][:, F:], dh1, "d_h1_gate")
    dr1, dr1b, d_ln1_g, d_ln1_b, d_b_o, _ = ln_bwd(h0, mix, ln1_g, ln1_b, dr2, dh1, None, "ln1_bwd")
    dw_o, _ = mm_tn(z, dr1b, "dw_o")
    dz = mm_nt(dr1b, full["w_o"], None, "d_z")
    dy_a, dy_b, dproj_g = gate_bwd(dz, proj_g, y_a, y_b, "gate_bwd")
    dw_a, _ = mm_tn(s_a, dy_a, "dw_a")
    ds_a = mm_nt(dy_a, full["w_a"], None, "d_s_a")
    dproj_a, d_conv = conv_a_bwd(ds_a, proj_a, conv_full, "conv_a_bwd")
    dw_b, _ = mm_tn(comb, dy_b, "dw_b")

    rs_mine, rs_sib, rs_handles = {}, {}, {}

    def reduce_start(keys, grads, tag):
        parts = [_to_blocks(grads[k], big[k][1]) for k in keys]
        from_sib = exchange_sibling(parts, f"grads_to_sibling_{tag}")
        mine = [lax.dynamic_index_in_dim(p.reshape((4, 2) + p.shape[1:]), core, 1, keepdims=False) for p in parts]
        sums = [pair_add(a, b, f"chip_sum_{k}") for k, a, b in zip(keys, mine, from_sib)]
        handles, tok = copies_start(sums, [jax.ShapeDtypeStruct((3,) + s.shape[1:], BF16) for s in sums],
                                    _to_chips_plan, 3, f"grads_to_chips_{tag}_start")
        for k, a, b in zip(keys, mine, from_sib):
            rs_mine[k], rs_sib[k] = a, b
        rs_handles[tag] = (keys, handles)
        return tok

    tok_a = reduce_start(("w_a", "w_b", "w_o", "w_up", "w_down"),
                         dict(w_a=dw_a, w_b=dw_b, w_o=dw_o, w_up=jnp.concatenate([dw_up_a, dw_up_g], 1), w_down=dw_down),
                         "a")
    dcomb = mm_nt(dy_b, full["w_b"], None, "d_comb", after=tok_a)
    dos, dms = combine_bwd(dcomb, outs, lses, "combine_bwd")
    dw_grp, cs_grp, dqkvs = [], [], []
    for g, d in enumerate(DILATIONS):
        dq, dk, dv = att_bwd(qkvs[g], dos[g], lses[g], dms[g], g, f"att_bwd_{g}")
        dqkv = jnp.concatenate([dq, dk, dv], -1).reshape(T, 3 * GROUP_W)
        dwg, csg = mm_tn(h0_res[g], dqkv, f"dw_in_qkv_{g}")
        dqkvs.append(dqkv)
        dw_grp.append(dwg)
        cs_grp.append(csg)
    dw_pa, cs_pa = mm_tn(h0b, dproj_a, "dw_in_conv")
    dw_pg, cs_pg = mm_tn(h0b, dproj_g, "dw_in_gates")

    def ungroup(parts):
        return jnp.concatenate([p[:, s * GROUP_W:(s + 1) * GROUP_W] for s in range(3) for p in parts], 1)

    db_in_parts = [cs_pa, ungroup(cs_grp), cs_pg]
    tok_b = reduce_start(("w_in",), dict(w_in=jnp.concatenate([dw_pa, ungroup(dw_grp), dw_pg], 1)), "b")
    dh0 = mm_nt(dproj_a, w_pa, None, "d_h0_conv", after=tok_b)
    dh0 = mm_nt(dproj_g, w_pg, dh0, "d_h0_gates")
    dh0 = mm_nt(dqkvs[0], w_grp[0], dh0, "d_h0_qkv_0")
    dh0_res = [(mm_nt(dqkvs[g], w_grp[g], None, f"d_h0_qkv_{g}").reshape(d, T // d, D), d)
               for g, d in enumerate(DILATIONS) if g > 0]
    dx, _, d_ln0_g, d_ln0_b, _, _ = ln_bwd(xs, None, ln0g, ln0b, dr1, dh0, None, "ln0_bwd", by_residue=dh0_res)

    small = [d_ln0_g, d_ln0_b, jnp.concatenate(db_in_parts, 1), d_conv, d_b_o, d_ln1_g, d_ln1_b,
             jnp.concatenate([cs_a, cs_gate], 1), d_fcw, d_fcb, d_b_down, d_ln2_g, d_ln2_b, loss_part]
    packed, sizes = _pack(small)
    total = all_sum_small(packed, "sum_small")
    (g_ln0_g, g_ln0_b, g_b_in, g_conv_full, g_b_o, g_ln1_g, g_ln1_b, g_b_up, g_fcw_full, g_fcb, g_b_down, g_ln2_g,
     g_ln2_b, loss) = _unpack(total, sizes, [a.shape for a in small])
    cw = conv_w.shape[-1]
    fw = ffn_conv_w.shape[-1]
    g_conv = lax.dynamic_slice_in_dim(g_conv_full, dev * cw, cw, 1)
    g_fcw = lax.dynamic_slice_in_dim(g_fcw_full, dev * fw, fw, 1)

    from_chips = {}
    for tag, (keys, handles) in rs_handles.items():
        lands = copies_wait(handles, _to_chips_plan, total, f"grads_to_chips_{tag}_wait")
        from_chips.update(zip(keys, lands))

    moments = dict(w_in=(m_w_in, v_w_in), w_a=(m_w_a, v_w_a), w_b=(m_w_b, v_w_b), w_o=(m_w_o, v_w_o),
                   w_up=(m_w_up, v_w_up), w_down=(m_w_down, v_w_down))
    res_big = {}
    for k in names:
        own = lax.dynamic_index_in_dim(rs_mine[k], chip, 0, keepdims=False)
        sib = lax.dynamic_index_in_dim(rs_sib[k], chip, 0, keepdims=False)
        res_big[k] = adamw_sharded(big[k][0], moments[k][0][0], moments[k][1][0], own, sib, from_chips[k], f"adamw_{k}")

    small_names = ["ln0_g", "ln0_b", "b_in", "conv_w", "b_o", "ln1_g", "ln1_b", "b_up", "ffn_conv_w", "ffn_conv_b",
                   "b_down", "ln2_g", "ln2_b"]
    small_w = [ln0_g, ln0_b, b_in, conv_w, b_o, ln1_g, ln1_b, b_up, ffn_conv_w, ffn_conv_b, b_down, ln2_g, ln2_b]
    small_m = [m_ln0_g, m_ln0_b, m_b_in, m_conv_w, m_b_o, m_ln1_g, m_ln1_b, m_b_up, m_ffn_conv_w, m_ffn_conv_b,
               m_b_down, m_ln2_g, m_ln2_b]
    small_v = [v_ln0_g, v_ln0_b, v_b_in, v_conv_w, v_b_o, v_ln1_g, v_ln1_b, v_b_up, v_ffn_conv_w, v_ffn_conv_b,
               v_b_down, v_ln2_g, v_ln2_b]
    small_g = [g_ln0_g, g_ln0_b, g_b_in, g_conv, g_b_o, g_ln1_g, g_ln1_b, g_b_up, g_fcw, g_fcb, g_b_down, g_ln2_g,
               g_ln2_b]
    shapes = [w.shape for w in small_w]
    small_g = [g.reshape(s) for g, s in zip(small_g, shapes)]
    pw, psz = _pack(small_w)
    pg, _ = _pack(small_g)
    pm, _ = _pack(small_m)
    pv, _ = _pack(small_v)
    pd, pnm, pnv = adamw_packed(pw, pg, pm, pv, "adamw_small")
    res_small = {k: (g, d_, m_, v_) for k, g, d_, m_, v_ in zip(
        small_names, small_g, _unpack(pd, psz, shapes), _unpack(pnm, psz, shapes), _unpack(pnv, psz, shapes))}

    order = ["ln0_g", "ln0_b", "w_in", "b_in", "conv_w", "w_a", "w_b", "w_o", "b_o", "ln1_g", "ln1_b", "w_up", "b_up",
             "ffn_conv_w", "ffn_conv_b", "w_down", "b_down", "ln2_g", "ln2_b"]

    def result(k, j):
        if k in res_big:
            return res_big[k][j][None]
        return res_small[k][j]

    out = [loss.reshape(()), dx.reshape(x.shape)]
    for j in range(4):
        out += [result(k, j) for k in order]
    return tuple(out)
```

```python
import functools
import math

import numpy as np
import jax
import jax.numpy as jnp
from jax import lax
from jax.experimental import pallas as pl
from jax.experimental.pallas import tpu as pltpu

F32 = jnp.float32
BF16 = jnp.bfloat16

N_DEV = 8
LN_EPS = 1e-5
ALPHA = (2.0 * 1) ** 0.25
MASK_VALUE = -1e30
HEAD_DIM = 64
GROUP_W = 512
QKV_W = 3 * GROUP_W
DILATIONS = (1, 4, 16)
RADIUS = 64
LANES = 128
HALO = 8
ATT_TQ = 128

ADAM_LR = 0.001
ADAM_B1 = 0.9
ADAM_B2 = 0.999
ADAM_EPS = 1e-08
ADAM_WD = 0.01
ADAM_STEP = 10

VMEM_LIMIT = 52 * 1024 * 1024
OUT_TILE_BYTES = 8 * 1024 * 1024
MESH = pl.DeviceIdType.MESH
NT_DIMS = (((1,), (1,)), ((), ()))
TN_DIMS = (((0,), (0,)), ((), ()))


def _pick(n, target, align=LANES):
    if n <= target:
        return n
    best = None
    for t in range(align, target + 1, align):
        if n % t == 0:
            best = t
    assert best is not None, (n, target, align)
    return best


def _params(sems=None):
    return pltpu.CompilerParams(dimension_semantics=sems, vmem_limit_bytes=VMEM_LIMIT)


def _alibi_slopes():
    n = 3 * 8
    return np.exp2(-8.0 * np.arange(1, n + 1, dtype=np.float64) / n).astype(np.float32).reshape(3, 8)


def _ln_stats(r):
    mu = jnp.mean(r, -1, keepdims=True)
    xc = r - mu
    var = jnp.mean(xc * xc, -1, keepdims=True)
    rstd = lax.rsqrt(var + LN_EPS)
    return xc, rstd


def _load_natural(ref, d, scr):
    if d == 1:
        return ref[0]
    n, C = ref.shape[1], ref.shape[2]
    for c in range(C // LANES):
        for r in range(d):
            scr[c, pl.ds(r, n, stride=d), :] = ref[r, :, c * LANES:(c + 1) * LANES]
    return jnp.concatenate([scr[c] for c in range(C // LANES)], axis=1)


def _store_by_residue(val, ref, d, scr):
    if d == 1:
        ref[0] = val.astype(ref.dtype)
        return
    n, C = ref.shape[1], ref.shape[2]
    for c in range(C // LANES):
        scr[c] = val[:, c * LANES:(c + 1) * LANES]
    for c in range(C // LANES):
        for r in range(d):
            ref[r, :, c * LANES:(c + 1) * LANES] = scr[c, pl.ds(r, n, stride=d), :].astype(ref.dtype)


def _residue_spec(tm, d, C):
    return pl.BlockSpec((d, tm // d, C), lambda i: (0, i, 0))


def _residue_scratch(tm, C):
    return pltpu.VMEM((C // LANES, tm, LANES), F32)


def ln_fwd(a, res, g, b, name, dilations=()):
    T, D = a.shape
    tm = _pick(T, 512, 8)
    has_res = res is not None
    nd = len(dilations)

    def body(*refs):
        refs = list(refs)
        a_ref = refs.pop(0)
        r = a_ref[...]
        if has_res:
            r = ALPHA * r + refs.pop(0)[...]
        g_ref, b_ref, h_ref, hb_ref = refs[:4]
        xc, rstd = _ln_stats(r)
        h = xc * rstd * g_ref[...] + b_ref[...]
        h_ref[...] = h
        hb_ref[...] = h.astype(BF16)
        for d, p_ref in zip(dilations, refs[4:4 + nd]):
            _store_by_residue(h, p_ref, d, refs[-1])

    row = pl.BlockSpec((tm, D), lambda i: (i, 0))
    vec = pl.BlockSpec((1, D), lambda i: (0, 0))
    ins = [a] + ([res] if has_res else []) + [g, b]
    return pl.pallas_call(
        body, name=name, grid=(T // tm,),
        in_specs=[row] * (2 if has_res else 1) + [vec, vec],
        out_specs=[row, row] + [_residue_spec(tm, d, D) for d in dilations],
        out_shape=[jax.ShapeDtypeStruct((T, D), F32), jax.ShapeDtypeStruct((T, D), BF16)]
        + [jax.ShapeDtypeStruct((d, T // d, D), BF16) for d in dilations],
        scratch_shapes=[_residue_scratch(tm, D)] if nd else [],
        compiler_params=_params(("parallel",)),
    )(*ins)


def ln_bwd(a, res, g, b, d1, d2, tgt, name, by_residue=()):
    T, D = a.shape
    tm = _pick(T, 256, 8)
    has_res = res is not None
    loss_mode = tgt is not None
    nres = len(by_residue)

    def body(*refs):
        refs = list(refs)
        a_ref = refs.pop(0)
        r_ref = refs.pop(0) if has_res else None
        g_ref = refs.pop(0)
        b_ref = refs.pop(0)
        if loss_mode:
            t_ref = refs.pop(0)
        else:
            d1_ref = refs.pop(0)
            d2_ref = refs.pop(0)
        e_refs = [refs.pop(0) for _ in range(nres)]
        dr_ref, drb_ref, dg_ref, db_ref, ds_ref, loss_ref = refs[:6]
        i = pl.program_id(0)

        @pl.when(i == 0)
        def _():
            dg_ref[...] = jnp.zeros_like(dg_ref)
            db_ref[...] = jnp.zeros_like(db_ref)
            ds_ref[...] = jnp.zeros_like(ds_ref)
            loss_ref[...] = jnp.zeros_like(loss_ref)

        r = a_ref[...]
        if has_res:
            r = ALPHA * r + r_ref[...]
        xc, rstd = _ln_stats(r)
        xhat = xc * rstd
        gam = g_ref[...]
        if loss_mode:
            err = xhat * gam + b_ref[...] - t_ref[...]
            dy = err * (1.0 / D)
            row_loss = jnp.mean(err * err, -1, keepdims=True)
            loss_ref[...] += 0.5 * jnp.sum(row_loss, 0, keepdims=True)
        else:
            dy = ALPHA * d1_ref[...] + d2_ref[...]
        for (_, d), e_ref in zip(by_residue, e_refs):
            dy = dy + _load_natural(e_ref, d, refs[-1])
        dyg = dy * gam
        c1 = jnp.mean(dyg, -1, keepdims=True)
        c2 = jnp.mean(dyg * xhat, -1, keepdims=True)
        dr = rstd * (dyg - c1 - xhat * c2)
        dr_ref[...] = dr
        drb_ref[...] = dr.astype(BF16)
        dg_ref[...] += jnp.sum(dy * xhat, 0, keepdims=True)
        db_ref[...] += jnp.sum(dy, 0, keepdims=True)
        ds_ref[...] += jnp.sum(dr, 0, keepdims=True)

    row = pl.BlockSpec((tm, D), lambda i: (i, 0))
    vec = pl.BlockSpec((1, D), lambda i: (0, 0))
    one = pl.BlockSpec((1, 1), lambda i: (0, 0))
    ins = [a] + ([res] if has_res else []) + [g, b] + ([tgt] if loss_mode else [d1, d2]) + [e for e, _ in by_residue]
    in_specs = [row] * (2 if has_res else 1) + [vec, vec] + [row] * (1 if loss_mode else 2)
    in_specs += [_residue_spec(tm, d, D) for _, d in by_residue]
    return pl.pallas_call(
        body, name=name, grid=(T // tm,),
        in_specs=in_specs,
        out_specs=[row, row, vec, vec, vec, one],
        out_shape=[jax.ShapeDtypeStruct((T, D), F32), jax.ShapeDtypeStruct((T, D), BF16),
                   jax.ShapeDtypeStruct((1, D), F32), jax.ShapeDtypeStruct((1, D), F32),
                   jax.ShapeDtypeStruct((1, D), F32), jax.ShapeDtypeStruct((1, 1), F32)],
        scratch_shapes=[_residue_scratch(tm, D)] if nres else [],
        compiler_params=_params(("arbitrary",)),
    )(*ins)


_TOKEN_SPEC = pl.BlockSpec((8, LANES), lambda i: (0, 0))


def mm_nn(a, w, bias, out_dtype, name, after=None):
    M, K = a.shape
    N = w.shape[1]
    tm = _pick(M, max(256, min(1024, OUT_TILE_BYTES // (N * jnp.dtype(out_dtype).itemsize))), 8)
    tc = _pick(N, 512)

    def body(a_ref, w_ref, b_ref, *rest):
        o_ref = rest[-1]
        av = a_ref[...]
        for j in range(N // tc):
            cols = slice(j * tc, (j + 1) * tc)
            acc = jnp.dot(av, w_ref[:, cols], preferred_element_type=F32)
            o_ref[:, cols] = (acc + b_ref[:, cols]).astype(out_dtype)

    return pl.pallas_call(
        body, name=name, grid=(M // tm,),
        in_specs=[pl.BlockSpec((tm, K), lambda i: (i, 0)),
                  pl.BlockSpec((K, N), lambda i: (0, 0)),
                  pl.BlockSpec((1, N), lambda i: (0, 0))] + ([] if after is None else [_TOKEN_SPEC]),
        out_specs=pl.BlockSpec((tm, N), lambda i: (i, 0)),
        out_shape=jax.ShapeDtypeStruct((M, N), out_dtype),
        compiler_params=_params(("parallel",)),
    )(a, w, bias, *([] if after is None else [after]))


def mm_nt(a, w, acc_in, name, after=None, w_block=0):
    pieces = list(a) if isinstance(a, (list, tuple)) else [a]
    M = pieces[0].shape[0]
    widths = [p.shape[1] for p in pieces]
    K = sum(widths)
    N = w.shape[0]
    tm = _pick(M, 512, 8)
    tc = _pick(N, 512)
    has_acc = acc_in is not None
    n_a = len(pieces)

    def body(*refs):
        a_refs, w_ref = refs[:n_a], refs[n_a]
        c_ref = refs[n_a + 1] if has_acc else None
        o_ref = refs[-1]
        for j in range(N // tc):
            cols = slice(j * tc, (j + 1) * tc)
            acc = c_ref[:, cols] if has_acc else None
            off = 0
            for a_ref, kw in zip(a_refs, widths):
                part = lax.dot_general(a_ref[...], w_ref[cols, off:off + kw], NT_DIMS, preferred_element_type=F32)
                acc = part if acc is None else acc + part
                off += kw
            o_ref[:, cols] = acc

    out_spec = pl.BlockSpec((tm, N), lambda i: (i, 0))
    in_specs = [pl.BlockSpec((tm, kw), lambda i: (i, 0)) for kw in widths]
    in_specs.append(pl.BlockSpec((N, K), lambda i: (0, w_block)))
    ins = pieces + [w]
    if has_acc:
        in_specs.append(out_spec)
        ins.append(acc_in)
    if after is not None:
        in_specs.append(_TOKEN_SPEC)
        ins.append(after)
    return pl.pallas_call(
        body, name=name, grid=(M // tm,),
        in_specs=in_specs, out_specs=out_spec,
        out_shape=jax.ShapeDtypeStruct((M, N), F32),
        compiler_params=_params(("parallel",)),
    )(*ins)


def mm_tn(a, b, name, out_dtype=BF16):
    pieces = list(b) if isinstance(b, (list, tuple)) else [b]
    T, M = a.shape
    widths = [p.shape[1] for p in pieces]
    N = sum(widths)
    tk = _pick(T, 512, 8)
    nk = T // tk
    tc = _pick(M, 256)
    n_b = len(pieces)

    def body(*refs):
        a_ref, b_refs = refs[0], refs[1:1 + n_b]
        o_ref, cs_ref, acc_ref = refs[1 + n_b:]
        k = pl.program_id(0)

        @pl.when(k == 0)
        def _():
            acc_ref[...] = jnp.zeros_like(acc_ref)
            cs_ref[...] = jnp.zeros_like(cs_ref)

        off = 0
        for b_ref, wd in zip(b_refs, widths):
            cols = slice(off, off + wd)
            bv = b_ref[...]
            cs_ref[:, cols] += jnp.sum(bv.astype(F32), 0, keepdims=True)
            for mi in range(M // tc):
                rows = slice(mi * tc, (mi + 1) * tc)
                acc_ref[rows, cols] += lax.dot_general(a_ref[:, rows], bv, TN_DIMS, preferred_element_type=F32)
            off += wd

        @pl.when(k == nk - 1)
        def _():
            o_ref[...] = acc_ref[...].astype(out_dtype)

    return pl.pallas_call(
        body, name=name, grid=(nk,),
        in_specs=[pl.BlockSpec((tk, M), lambda k: (k, 0))] + [pl.BlockSpec((tk, wd), lambda k: (k, 0)) for wd in widths],
        out_specs=[pl.BlockSpec((M, N), lambda k: (0, 0)), pl.BlockSpec((1, N), lambda k: (0, 0))],
        out_shape=[jax.ShapeDtypeStruct((M, N), out_dtype), jax.ShapeDtypeStruct((1, N), F32)],
        scratch_shapes=[pltpu.VMEM((M, N), F32)],
        compiler_params=_params(("arbitrary",)),
    )(a, *pieces)


def _ext_rows(prev_ref, main_ref, next_ref, i, tm, T):
    before = jnp.where(i == 0, 0.0, prev_ref[...])
    after = jnp.where(i == T // tm - 1, 0.0, next_ref[...])
    return jnp.concatenate([before, main_ref[...], after], axis=0)


def _prev_row(x):
    return pltpu.roll(x, 1, 0)


def _next_row(x):
    return pltpu.roll(x, x.shape[0] - 1, 0)


def _conv3(u, w_ref):
    return _prev_row(u) * w_ref[0:1, :] + u * w_ref[1:2, :] + _next_row(u) * w_ref[2:3, :]


def _main(x, tm):
    return x[HALO:HALO + tm]


def _halo_specs(tm, tc, T, col, order):
    r = tm // HALO
    last = T // HALO - 1
    if order == "ij":
        return (pl.BlockSpec((HALO, tc), lambda i, j: (jnp.maximum(i * r - 1, 0), col(j))),
                pl.BlockSpec((tm, tc), lambda i, j: (i, col(j))),
                pl.BlockSpec((HALO, tc), lambda i, j: (jnp.minimum((i + 1) * r, last), col(j))))
    return (pl.BlockSpec((HALO, tc), lambda j, i: (jnp.maximum(i * r - 1, 0), col(j))),
            pl.BlockSpec((tm, tc), lambda j, i: (i, col(j))),
            pl.BlockSpec((HALO, tc), lambda j, i: (jnp.minimum((i + 1) * r, last), col(j))))


def conv_a_fwd(proj_a, conv_w, name):
    T, D3 = proj_a.shape
    D = D3 // 3
    tm = _pick(T, 256, 8)

    def body(p_ref, m_ref, n_ref, w_ref, o_ref):
        i = pl.program_id(0)
        ext = _ext_rows(p_ref, m_ref, n_ref, i, tm, T)
        u = ext[:, D:2 * D] * ext[:, 2 * D:]
        cu = _conv3(u, w_ref)
        o_ref[...] = (m_ref[:, :D] * _main(cu, tm)).astype(BF16)

    prev, main, nxt = _halo_specs(tm, D3, T, lambda j: 0, "ij")
    return pl.pallas_call(
        body, name=name, grid=(T // tm, 1),
        in_specs=[prev, main, nxt, pl.BlockSpec((3, D), lambda i, j: (0, 0))],
        out_specs=pl.BlockSpec((tm, D), lambda i, j: (i, 0)),
        out_shape=jax.ShapeDtypeStruct((T, D), BF16),
        compiler_params=_params(("parallel", "arbitrary")),
    )(proj_a, proj_a, proj_a, conv_w)


def conv_a_bwd(ds_a, proj_a, conv_w, name):
    T, D3 = proj_a.shape
    D = D3 // 3
    tm = _pick(T, 256, 8)

    def body(dp_ref, dm_ref, dn_ref, p_ref, m_ref, n_ref, w_ref, o_ref, dw_ref):
        i = pl.program_id(0)

        @pl.when(i == 0)
        def _():
            dw_ref[...] = jnp.zeros_like(dw_ref)

        ext = _ext_rows(p_ref, m_ref, n_ref, i, tm, T)
        dsa = _ext_rows(dp_ref, dm_ref, dn_ref, i, tm, T)
        gb, gc, hin = ext[:, :D], ext[:, D:2 * D], ext[:, 2 * D:]
        u = gc * hin
        u_prev, u_next = _prev_row(u), _next_row(u)
        cu = u_prev * w_ref[0:1, :] + u * w_ref[1:2, :] + u_next * w_ref[2:3, :]
        dcu = dsa * gb
        du = _next_row(dcu) * w_ref[0:1, :] + dcu * w_ref[1:2, :] + _prev_row(dcu) * w_ref[2:3, :]
        o_ref[:, :D] = _main(dsa * cu, tm).astype(BF16)
        o_ref[:, D:2 * D] = _main(du * hin, tm).astype(BF16)
        o_ref[:, 2 * D:] = _main(du * gc, tm).astype(BF16)
        dcu_m = _main(dcu, tm)
        dw_ref[0:1, :] += jnp.sum(dcu_m * _main(u_prev, tm), 0, keepdims=True)
        dw_ref[1:2, :] += jnp.sum(dcu_m * _main(u, tm), 0, keepdims=True)
        dw_ref[2:3, :] += jnp.sum(dcu_m * _main(u_next, tm), 0, keepdims=True)

    dprev, dmain, dnxt = _halo_specs(tm, D, T, lambda j: 0, "ij")
    prev, main, nxt = _halo_specs(tm, D3, T, lambda j: 0, "ij")
    return pl.pallas_call(
        body, name=name, grid=(T // tm, 1),
        in_specs=[dprev, dmain, dnxt, prev, main, nxt, pl.BlockSpec((3, D), lambda i, j: (0, 0))],
        out_specs=[pl.BlockSpec((tm, D3), lambda i, j: (i, 0)), pl.BlockSpec((3, D), lambda i, j: (0, 0))],
        out_shape=[jax.ShapeDtypeStruct((T, D3), BF16), jax.ShapeDtypeStruct((3, D), F32)],
        compiler_params=_params(("arbitrary", "arbitrary")),
    )(ds_a, ds_a, ds_a, proj_a, proj_a, proj_a, conv_w)


_INV_SQRT2 = 1.0 / math.sqrt(2.0)
_INV_SQRT_2PI = 1.0 / math.sqrt(2.0 * math.pi)


def conv_f_fwd(up, fcw, fcb, name):
    T, F2 = up.shape
    F = F2 // 2
    tm = _pick(T, 256, 8)
    tc = _pick(F, 1408)
    nc = F // tc

    def body(p_ref, m_ref, n_ref, g_ref, w_ref, b_ref, o_ref):
        i = pl.program_id(0)
        a = _ext_rows(p_ref, m_ref, n_ref, i, tm, T)
        ca = _main(_conv3(a, w_ref), tm) + b_ref[...]
        gl = 0.5 * ca * (1.0 + lax.erf(ca * _INV_SQRT2))
        o_ref[...] = (gl * g_ref[...]).astype(BF16)

    prev, main, nxt = _halo_specs(tm, tc, T, lambda j: j, "ij")
    return pl.pallas_call(
        body, name=name, grid=(T // tm, nc),
        in_specs=[prev, main, nxt,
                  pl.BlockSpec((tm, tc), lambda i, j: (i, nc + j)),
                  pl.BlockSpec((3, tc), lambda i, j: (0, j)),
                  pl.BlockSpec((1, tc), lambda i, j: (0, j))],
        out_specs=pl.BlockSpec((tm, tc), lambda i, j: (i, j)),
        out_shape=jax.ShapeDtypeStruct((T, F), BF16),
        compiler_params=_params(("parallel", "parallel")),
    )(up, up, up, up, fcw, fcb)


def conv_f_bwd(df, up, fcw, fcb, name):
    T, F2 = up.shape
    F = F2 // 2
    tm = _pick(T, 256, 8)
    tc = _pick(F, 1408)
    nc = F // tc

    def body(fp_ref, fm_ref, fn_ref, ap_ref, am_ref, an_ref, gp_ref, gm_ref, gn_ref, w_ref, b_ref,
             da_ref, dg_ref, csa_ref, csg_ref, dfb_ref, dfw_ref):
        i = pl.program_id(1)

        @pl.when(i == 0)
        def _():
            csa_ref[...] = jnp.zeros_like(csa_ref)
            csg_ref[...] = jnp.zeros_like(csg_ref)
            dfb_ref[...] = jnp.zeros_like(dfb_ref)
            dfw_ref[...] = jnp.zeros_like(dfw_ref)

        dfe = _ext_rows(fp_ref, fm_ref, fn_ref, i, tm, T)
        a = _ext_rows(ap_ref, am_ref, an_ref, i, tm, T)
        gate = _ext_rows(gp_ref, gm_ref, gn_ref, i, tm, T)
        a_prev, a_next = _prev_row(a), _next_row(a)
        ca = a_prev * w_ref[0:1, :] + a * w_ref[1:2, :] + a_next * w_ref[2:3, :] + b_ref[...]
        cdf = 0.5 * (1.0 + lax.erf(ca * _INV_SQRT2))
        gl = ca * cdf
        gp = cdf + ca * (jnp.exp(-0.5 * ca * ca) * _INV_SQRT_2PI)
        dgate = _main(dfe * gl, tm)
        dca = dfe * gate * gp
        da = _main(_next_row(dca) * w_ref[0:1, :] + dca * w_ref[1:2, :] + _prev_row(dca) * w_ref[2:3, :], tm)
        da_ref[...] = da.astype(BF16)
        dg_ref[...] = dgate.astype(BF16)
        csa_ref[...] += jnp.sum(da, 0, keepdims=True)
        csg_ref[...] += jnp.sum(dgate, 0, keepdims=True)
        dca_m = _main(dca, tm)
        dfb_ref[...] += jnp.sum(dca_m, 0, keepdims=True)
        dfw_ref[0:1, :] += jnp.sum(dca_m * _main(a_prev, tm), 0, keepdims=True)
        dfw_ref[1:2, :] += jnp.sum(dca_m * _main(a, tm), 0, keepdims=True)
        dfw_ref[2:3, :] += jnp.sum(dca_m * _main(a_next, tm), 0, keepdims=True)

    fprev, fmain, fnxt = _halo_specs(tm, tc, T, lambda j: j, "ji")
    gprev, gmain, gnxt = _halo_specs(tm, tc, T, lambda j: nc + j, "ji")
    tile = pl.BlockSpec((tm, tc), lambda j, i: (i, j))
    vec = pl.BlockSpec((1, tc), lambda j, i: (0, j))
    vec3 = pl.BlockSpec((3, tc), lambda j, i: (0, j))
    return pl.pallas_call(
        body, name=name, grid=(nc, T // tm),
        in_specs=[fprev, fmain, fnxt, fprev, fmain, fnxt, gprev, gmain, gnxt, vec3, vec],
        out_specs=[tile, tile, vec, vec, vec, vec3],
        out_shape=[jax.ShapeDtypeStruct((T, F), BF16), jax.ShapeDtypeStruct((T, F), BF16),
                   jax.ShapeDtypeStruct((1, F), F32), jax.ShapeDtypeStruct((1, F), F32),
                   jax.ShapeDtypeStruct((1, F), F32), jax.ShapeDtypeStruct((3, F), F32)],
        compiler_params=_params(("arbitrary", "arbitrary")),
    )(df, df, df, up, up, up, up, up, up, fcw, fcb)


def gate_fwd(proj_g, y_a, y_b, name):
    T, D = y_a.shape
    tm = _pick(T, 512, 8)

    def body(g_ref, a_ref, b_ref, o_ref):
        sa = jax.nn.sigmoid(g_ref[:, :D])
        sb = jax.nn.sigmoid(g_ref[:, D:])
        o_ref[...] = (sa * a_ref[...] + sb * b_ref[...]).astype(BF16)

    row = pl.BlockSpec((tm, D), lambda i: (i, 0))
    return pl.pallas_call(
        body, name=name, grid=(T // tm,),
        in_specs=[pl.BlockSpec((tm, 2 * D), lambda i: (i, 0)), row, row],
        out_specs=row,
        out_shape=jax.ShapeDtypeStruct((T, D), BF16),
        compiler_params=_params(("parallel",)),
    )(proj_g, y_a, y_b)


def gate_bwd(dz, proj_g, y_a, y_b, name):
    T, D = y_a.shape
    tm = _pick(T, 512, 8)

    def body(dz_ref, g_ref, a_ref, b_ref, da_ref, db_ref, dg_ref):
        dzv = dz_ref[...]
        sa = jax.nn.sigmoid(g_ref[:, :D])
        sb = jax.nn.sigmoid(g_ref[:, D:])
        da_ref[...] = (dzv * sa).astype(BF16)
        db_ref[...] = (dzv * sb).astype(BF16)
        dg_ref[:, :D] = (dzv * a_ref[...] * (sa * (1.0 - sa))).astype(BF16)
        dg_ref[:, D:] = (dzv * b_ref[...] * (sb * (1.0 - sb))).astype(BF16)

    row = pl.BlockSpec((tm, D), lambda i: (i, 0))
    wide = pl.BlockSpec((tm, 2 * D), lambda i: (i, 0))
    return pl.pallas_call(
        body, name=name, grid=(T // tm,),
        in_specs=[row, wide, row, row],
        out_specs=[row, row, wide],
        out_shape=[jax.ShapeDtypeStruct((T, D), BF16), jax.ShapeDtypeStruct((T, D), BF16),
                   jax.ShapeDtypeStruct((T, 2 * D), BF16)],
        compiler_params=_params(("parallel",)),
    )(dz, proj_g, y_a, y_b)


ATT_WIN = ATT_TQ + 2 * RADIUS
ATT_STEP = 512
FAR = 1e32


def _att_window(qs, L):
    ks = pl.multiple_of(jnp.clip(qs - RADIUS, 0, L - ATT_WIN), RADIUS)
    return ks, jnp.where(qs == 0, 0, jnp.where(qs == L - ATT_TQ, 2, 1))


def _fill_bias_tables(bias_ref, sl_ref, hp, d):
    col_row = (lax.broadcasted_iota(jnp.int32, (ATT_TQ, ATT_WIN), 1)
               - lax.broadcasted_iota(jnp.int32, (ATT_TQ, ATT_WIN), 0))
    for v in range(3):
        ad = jnp.abs(col_row - v * RADIUS)
        dist = jnp.where(ad <= RADIUS, (ad * d).astype(F32), FAR)
        bias_ref[v, 0:ATT_TQ, :] = sl_ref[hp * 2] * dist
        bias_ref[v, ATT_TQ:2 * ATT_TQ, :] = sl_ref[hp * 2 + 1] * dist


def _head_masks():
    lane = lax.broadcasted_iota(jnp.int32, (1, LANES), 1)
    return [lane < HEAD_DIM, lane >= HEAD_DIM]


def _stack_heads(x, masks):
    zero = jnp.zeros_like(x)
    return jnp.concatenate([jnp.where(masks[0], x, zero), jnp.where(masks[1], x, zero)], axis=0)


def _unstack_heads(x2, masks):
    n = x2.shape[0] // 2
    return jnp.where(masks[0], x2[:n], x2[n:])


def _att_step(L):
    step = min(ATT_STEP, L)
    assert L % step == 0 and step % ATT_TQ == 0 and L >= ATT_WIN
    return step


def att_fwd(qkv, group, name):
    d, L, _ = qkv.shape
    step = _att_step(L)
    cg = GROUP_W // LANES
    slopes = jnp.asarray(_alibi_slopes()[group])
    scale = HEAD_DIM ** -0.5

    def body(sl_ref, q_ref, k_ref, v_ref, o_ref, l_ref, bias_ref):
        hp = pl.program_id(1)
        i = pl.program_id(2)

        @pl.when(i == 0)
        def _():
            _fill_bias_tables(bias_ref, sl_ref, hp, d)

        masks = _head_masks()
        for t in range(step // ATT_TQ):
            rows = slice(t * ATT_TQ, (t + 1) * ATT_TQ)
            ks, table = _att_window(i * step + t * ATT_TQ, L)
            q2 = _stack_heads(q_ref[rows, :] * scale, masks)
            kw = k_ref[pl.ds(ks, ATT_WIN), :]
            vw = v_ref[pl.ds(ks, ATT_WIN), :]
            s = lax.dot_general(q2, kw, NT_DIMS, preferred_element_type=F32) - bias_ref[table]
            m = jnp.max(s, -1, keepdims=True)
            p = jnp.exp(s - m)
            den = jnp.sum(p, -1, keepdims=True)
            pn = (p / den).astype(BF16)
            o2 = jnp.dot(pn, vw, preferred_element_type=F32)
            o_ref[rows, :] = _unstack_heads(o2, masks)
            l_ref[rows, :] = _unstack_heads(m + jnp.log(den), masks)

    out_spec = pl.BlockSpec((None, step, LANES), lambda r, hp, i: (r, i, hp))
    return pl.pallas_call(
        body, name=name, grid=(d, cg, L // step),
        in_specs=[pl.BlockSpec(memory_space=pltpu.SMEM),
                  pl.BlockSpec((None, step, LANES), lambda r, hp, i: (r, i, hp)),
                  pl.BlockSpec((None, L, LANES), lambda r, hp, i: (r, 0, cg + hp)),
                  pl.BlockSpec((None, L, LANES), lambda r, hp, i: (r, 0, 2 * cg + hp))],
        out_specs=[out_spec, out_spec],
        out_shape=[jax.ShapeDtypeStruct((d, L, GROUP_W), F32)] * 2,
        scratch_shapes=[pltpu.VMEM((3, 2 * ATT_TQ, ATT_WIN), F32)],
        compiler_params=_params(("arbitrary", "arbitrary", "arbitrary")),
    )(slopes, qkv, qkv, qkv)


def att_bwd(qkv, do, lse, dmat, group, name):
    d, L, _ = qkv.shape
    step = _att_step(L)
    nq = L // step
    cg = GROUP_W // LANES
    slopes = jnp.asarray(_alibi_slopes()[group])
    scale = HEAD_DIM ** -0.5

    def body(sl_ref, q_ref, k_ref, v_ref, do_ref, l_ref, dm_ref, dq_ref, dk_ref, dv_ref, dk_acc, dv_acc, bias_ref):
        hp = pl.program_id(1)
        i = pl.program_id(2)

        @pl.when(i == 0)
        def _():
            dk_acc[...] = jnp.zeros_like(dk_acc)
            dv_acc[...] = jnp.zeros_like(dv_acc)
            _fill_bias_tables(bias_ref, sl_ref, hp, d)

        masks = _head_masks()

        def head_cols(x):
            return jnp.concatenate([jnp.max(jnp.where(hm, x, -jnp.inf), -1, keepdims=True) for hm in masks], axis=0)

        for t in range(step // ATT_TQ):
            rows = slice(t * ATT_TQ, (t + 1) * ATT_TQ)
            ks, table = _att_window(i * step + t * ATT_TQ, L)
            q2 = _stack_heads(q_ref[rows, :] * scale, masks)
            do2 = _stack_heads(do_ref[rows, :], masks)
            kw = k_ref[pl.ds(ks, ATT_WIN), :]
            vw = v_ref[pl.ds(ks, ATT_WIN), :]
            s = lax.dot_general(q2, kw, NT_DIMS, preferred_element_type=F32) - bias_ref[table]
            p = jnp.exp(s - head_cols(l_ref[rows, :]))
            dp = lax.dot_general(do2, vw, NT_DIMS, preferred_element_type=F32)
            ds = (p * (dp - head_cols(dm_ref[rows, :]))).astype(BF16)
            dq2 = jnp.dot(ds, kw, preferred_element_type=F32)
            dq_ref[rows, :] = (_unstack_heads(dq2, masks) * scale).astype(BF16)
            dk_acc[pl.ds(ks, ATT_WIN), :] += lax.dot_general(ds, q2, TN_DIMS, preferred_element_type=F32)
            dv_acc[pl.ds(ks, ATT_WIN), :] += lax.dot_general(p.astype(BF16), do2, TN_DIMS, preferred_element_type=F32)

        @pl.when(i == nq - 1)
        def _():
            dk_ref[...] = dk_acc[...].astype(BF16)
            dv_ref[...] = dv_acc[...].astype(BF16)

    tile = pl.BlockSpec((None, step, LANES), lambda r, hp, i: (r, i, hp))
    whole = pl.BlockSpec((None, L, LANES), lambda r, hp, i: (r, 0, hp))
    return pl.pallas_call(
        body, name=name, grid=(d, cg, nq),
        in_specs=[pl.BlockSpec(memory_space=pltpu.SMEM), tile,
                  pl.BlockSpec((None, L, LANES), lambda r, hp, i: (r, 0, cg + hp)),
                  pl.BlockSpec((None, L, LANES), lambda r, hp, i: (r, 0, 2 * cg + hp)),
                  tile, tile, tile],
        out_specs=[tile, whole, whole],
        out_shape=[jax.ShapeDtypeStruct((d, L, GROUP_W), BF16)] * 3,
        scratch_shapes=[pltpu.VMEM((L, LANES), F32), pltpu.VMEM((L, LANES), F32),
                        pltpu.VMEM((3, 2 * ATT_TQ, ATT_WIN), F32)],
        compiler_params=_params(("arbitrary", "arbitrary", "arbitrary")),
    )(slopes, qkv, qkv, qkv, do, lse, dmat)


def _group_weights(ls):
    m = jnp.maximum(jnp.maximum(ls[0], ls[1]), ls[2])
    es = [jnp.exp(l - m) for l in ls]
    tot = es[0] + es[1] + es[2]
    return [e / tot for e in es]


def combine_fwd(outs, lses, name):
    T = outs[0].shape[0] * outs[0].shape[1]
    tm = _pick(T, 512, 8)
    n_scr = 2 * (len(DILATIONS) - 1)

    def body(*refs):
        o_refs, l_refs, c_ref, scr = refs[:3], refs[3:6], refs[6], refs[7:]
        o = [_load_natural(o_refs[g], d, scr[g - 1] if g else None) for g, d in enumerate(DILATIONS)]
        l = [_load_natural(l_refs[g], d, scr[g + 1] if g else None) for g, d in enumerate(DILATIONS)]
        w = _group_weights(l)
        c_ref[...] = (w[0] * o[0] + w[1] * o[1] + w[2] * o[2]).astype(BF16)

    specs = [_residue_spec(tm, d, GROUP_W) for d in DILATIONS]
    return pl.pallas_call(
        body, name=name, grid=(T // tm,),
        in_specs=specs + specs, out_specs=pl.BlockSpec((tm, GROUP_W), lambda i: (i, 0)),
        out_shape=jax.ShapeDtypeStruct((T, GROUP_W), BF16),
        scratch_shapes=[_residue_scratch(tm, GROUP_W)] * n_scr,
        compiler_params=_params(("parallel",)),
    )(*outs, *lses)


def combine_bwd(dcomb, outs, lses, name):
    T = dcomb.shape[0]
    tm = _pick(T, 256, 8)
    head = np.arange(GROUP_W) // HEAD_DIM
    seg = jnp.asarray((head[:, None] == head[None, :]).astype(np.float32)).astype(BF16)
    ng = len(DILATIONS)
    n_scr = 4 * (ng - 1)

    def body(*refs):
        dc_ref, o_refs, l_refs, e_ref = refs[0], refs[1:1 + ng], refs[1 + ng:1 + 2 * ng], refs[1 + 2 * ng]
        do_refs, dm_refs = refs[2 + 2 * ng:2 + 3 * ng], refs[2 + 3 * ng:2 + 4 * ng]
        scr = refs[2 + 4 * ng:]
        o = [_load_natural(o_refs[g], d, scr[4 * (g - 1)] if g else None) for g, d in enumerate(DILATIONS)]
        l = [_load_natural(l_refs[g], d, scr[4 * (g - 1) + 1] if g else None) for g, d in enumerate(DILATIONS)]
        w = _group_weights(l)
        dc = dc_ref[...]
        e = e_ref[...]
        tot = jnp.zeros_like(dc)
        for g in range(ng):
            prod = dc * o[g]
            dw = jnp.zeros_like(dc)
            for _ in range(3):
                part = prod.astype(BF16)
                dw = dw + jnp.dot(part, e, preferred_element_type=F32)
                prod = prod - part.astype(F32)
            tot = tot + w[g] * dw
        for g, d in enumerate(DILATIONS):
            _store_by_residue(w[g] * dc, do_refs[g], d, scr[4 * (g - 1) + 2] if g else None)
            _store_by_residue(w[g] * tot, dm_refs[g], d, scr[4 * (g - 1) + 3] if g else None)

    specs = [_residue_spec(tm, d, GROUP_W) for d in DILATIONS]
    res = pl.pallas_call(
        body, name=name, grid=(T // tm,),
        in_specs=[pl.BlockSpec((tm, GROUP_W), lambda i: (i, 0))] + specs + specs
        + [pl.BlockSpec((GROUP_W, GROUP_W), lambda i: (0, 0))],
        out_specs=specs + specs,
        out_shape=[jax.ShapeDtypeStruct(o.shape, BF16) for o in outs] + [jax.ShapeDtypeStruct(o.shape, F32) for o in outs],
        scratch_shapes=[_residue_scratch(tm, GROUP_W)] * n_scr,
        compiler_params=_params(("parallel",)),
    )(dcomb, *outs, *lses, seg)
    return res[:ng], res[ng:]


def _position():
    return lax.axis_index("x"), lax.axis_index("y"), lax.axis_index("c")


def _other_chips(x, y):
    return [(1 - x, y), (x, 1 - y), (1 - x, 1 - y)]


def _remote(src, dst, send_sems, recv_sems, k, to):
    return pltpu.make_async_remote_copy(src_ref=src, dst_ref=dst, send_sem=send_sems.at[k], recv_sem=recv_sems.at[k],
                                        device_id=to, device_id_type=MESH)


def all_gather(shards, name):
    n = len(shards)

    def body(*refs):
        ins, outs = refs[:n], refs[n:2 * n]
        send_sems, recv_sems, local_sems = refs[2 * n:]
        x, y, c = _position()
        sibling = (x, y, 1 - c)
        chips = _other_chips(x, y)

        def block(a, px, py, pc):
            return outs[a].at[4 * px + 2 * py + pc]

        own, first, passed = [], [], []
        for a in range(n):
            cp = pltpu.make_async_copy(ins[a], block(a, x, y, c), local_sems.at[a])
            cp.start()
            own.append(cp)
            k0 = 7 * a
            first.append(_remote(ins[a], block(a, x, y, c), send_sems, recv_sems, k0, sibling))
            for j, chip in enumerate(chips):
                first.append(_remote(ins[a], block(a, x, y, c), send_sems, recv_sems, k0 + 1 + j, (*chip, c)))
        for cp in first:
            cp.start()
        for a in range(n):
            k0 = 7 * a
            for j, chip in enumerate(chips):
                got = block(a, *chip, c)
                _remote(got, got, send_sems, recv_sems, k0 + 1 + j, sibling).wait_recv()
                fwd = _remote(got, got, send_sems, recv_sems, k0 + 4 + j, sibling)
                fwd.start()
                passed.append(fwd)
        for a in range(n):
            k0 = 7 * a
            got = block(a, x, y, 1 - c)
            _remote(got, got, send_sems, recv_sems, k0, sibling).wait_recv()
            for j, chip in enumerate(chips):
                got = block(a, *chip, 1 - c)
                _remote(got, got, send_sems, recv_sems, k0 + 4 + j, sibling).wait_recv()
        for cp in first + passed:
            cp.wait_send()
        for cp in own:
            cp.wait()

    hbm = pl.BlockSpec(memory_space=pl.ANY)
    return pl.pallas_call(
        body, name=name,
        in_specs=[hbm] * n, out_specs=[hbm] * n,
        out_shape=[jax.ShapeDtypeStruct((N_DEV,) + s.shape, s.dtype) for s in shards],
        scratch_shapes=[pltpu.SemaphoreType.DMA((7 * n,)), pltpu.SemaphoreType.DMA((7 * n,)),
                        pltpu.SemaphoreType.DMA((n,))],
    )(*shards)


def exchange_sibling(parts, name):
    n = len(parts)

    def body(*refs):
        ins, outs = refs[:n], refs[n:2 * n]
        send_sems, recv_sems = refs[2 * n:]
        x, y, c = _position()
        sibling = (x, y, 1 - c)
        copies = []
        for a in range(n):
            for q in range(4):
                cp = _remote(ins[a].at[2 * q + (1 - c)], outs[a].at[q], send_sems, recv_sems, 4 * a + q, sibling)
                cp.start()
                copies.append(cp)
        for cp in copies:
            cp.wait_recv()
        for cp in copies:
            cp.wait_send()

    hbm = pl.BlockSpec(memory_space=pl.ANY)
    return pl.pallas_call(
        body, name=name,
        in_specs=[hbm] * n, out_specs=[hbm] * n,
        out_shape=[jax.ShapeDtypeStruct((4,) + p.shape[1:], p.dtype) for p in parts],
        scratch_shapes=[pltpu.SemaphoreType.DMA((4 * n,)), pltpu.SemaphoreType.DMA((4 * n,))],
    )(*parts)


def exchange_chips(sums, name):
    n = len(sums)

    def body(*refs):
        ins, outs = refs[:n], refs[n:2 * n]
        send_sems, recv_sems = refs[2 * n:]
        x, y, c = _position()
        copies = []
        for a in range(n):
            for j, (cx, cy) in enumerate(_other_chips(x, y)):
                cp = _remote(ins[a].at[2 * cx + cy], outs[a].at[j], send_sems, recv_sems, 3 * a + j, (cx, cy, c))
                cp.start()
                copies.append(cp)
        for cp in copies:
            cp.wait_recv()
        for cp in copies:
            cp.wait_send()

    hbm = pl.BlockSpec(memory_space=pl.ANY)
    return pl.pallas_call(
        body, name=name,
        in_specs=[hbm] * n, out_specs=[hbm] * n,
        out_shape=[jax.ShapeDtypeStruct((3,) + s.shape[1:], s.dtype) for s in sums],
        scratch_shapes=[pltpu.SemaphoreType.DMA((3 * n,)), pltpu.SemaphoreType.DMA((3 * n,))],
    )(*sums)


_HBM = pl.BlockSpec(memory_space=pltpu.HBM)
_SEM = pl.BlockSpec(memory_space=pltpu.SEMAPHORE)
_DATAFLOW = pltpu.SideEffectType.DATAFLOW_SIDE_EFFECTING


def _to_all_plan(srcs, lands, send_sems, recv_sems):
    x, y, c = _position()
    me = 4 * x + 2 * y + c
    copies = []
    for a in range(len(srcs)):
        for k in range(1, N_DEV):
            fx, fy, fc = (k >> 2) & 1, (k >> 1) & 1, k & 1
            to = (1 - x if fx else x, 1 - y if fy else y, 1 - c if fc else c)
            copies.append(_remote(srcs[a], lands[a].at[me], send_sems, recv_sems, (N_DEV - 1) * a + k - 1, to))
    return copies


def _to_chips_plan(srcs, lands, send_sems, recv_sems):
    x, y, c = _position()
    copies = []
    for a in range(len(srcs)):
        for j, (cx, cy) in enumerate(_other_chips(x, y)):
            copies.append(_remote(srcs[a].at[2 * cx + cy], lands[a].at[j], send_sems, recv_sems, 3 * a + j, (cx, cy, c)))
    return copies


def copies_start(srcs, land_shapes, plan, per_array, name):
    n = len(srcs)
    n_sem = per_array * n
    lands = [lax.empty(s.shape, s.dtype) for s in land_shapes]

    def body(*refs):
        src_refs, land_refs = refs[:n], refs[n:2 * n]
        send_sems, recv_sems = refs[2 * n], refs[2 * n + 1]
        token = refs[-1]
        for cp in plan(src_refs, land_refs, send_sems, recv_sems):
            cp.start()
        token[...] = jnp.zeros_like(token)

    out = pl.pallas_call(
        body, name=name,
        out_shape=(pltpu.SemaphoreType.DMA((n_sem,)), pltpu.SemaphoreType.DMA((n_sem,)))
        + tuple(pltpu.HBM(s.shape, s.dtype) for s in srcs)
        + tuple(pltpu.HBM(s.shape, s.dtype) for s in land_shapes)
        + (jax.ShapeDtypeStruct((8, LANES), F32),),
        in_specs=[_HBM] * (2 * n),
        out_specs=(_SEM, _SEM) + (_HBM,) * (2 * n) + (pl.BlockSpec(memory_space=pltpu.VMEM),),
        input_output_aliases={i: 2 + i for i in range(2 * n)},
        compiler_params=pltpu.CompilerParams(has_side_effects=_DATAFLOW),
    )(*[pltpu.with_memory_space_constraint(s, pltpu.HBM) for s in srcs],
      *[pltpu.with_memory_space_constraint(l, pltpu.HBM) for l in lands])
    return out[:-1], out[-1]


def copies_wait(handles, plan, after, name):
    send_sems, recv_sems = handles[0], handles[1]
    n = (len(handles) - 2) // 2
    thru = handles[2:]

    def body(*refs):
        src_refs, land_refs = refs[:n], refs[n:2 * n]
        send_sems, recv_sems = refs[2 * n], refs[2 * n + 1]
        copies = plan(src_refs, land_refs, send_sems, recv_sems)
        for cp in copies:
            cp.wait_recv()
        for cp in copies:
            cp.wait_send()

    out = pl.pallas_call(
        body, name=name,
        out_shape=tuple(pltpu.HBM(t.shape, t.dtype) for t in thru),
        in_specs=[_HBM] * (2 * n) + [_SEM, _SEM, pl.BlockSpec(memory_space=pl.ANY)],
        out_specs=(_HBM,) * (2 * n),
        input_output_aliases={i: i for i in range(2 * n)},
        compiler_params=pltpu.CompilerParams(has_side_effects=_DATAFLOW),
    )(*thru, send_sems, recv_sems, after)
    return out[n:]


def all_sum_small(vec, name):
    R = vec.shape[0]

    def body(v_ref, tot_ref, all_ref, send_sems, recv_sems):
        x, y, c = _position()
        me = 4 * x + 2 * y + c
        all_ref[me] = v_ref[...]
        copies = []
        for k in range(1, N_DEV):
            fx, fy, fc = (k >> 2) & 1, (k >> 1) & 1, k & 1
            to = (1 - x if fx else x, 1 - y if fy else y, 1 - c if fc else c)
            cp = _remote(v_ref, all_ref.at[me], send_sems, recv_sems, k - 1, to)
            cp.start()
            copies.append(cp)
        for cp in copies:
            cp.wait_recv()
        for cp in copies:
            cp.wait_send()
        tot = all_ref[0]
        for j in range(1, N_DEV):
            tot = tot + all_ref[j]
        tot_ref[...] = tot

    vmem = pl.BlockSpec(memory_space=pltpu.VMEM)
    return pl.pallas_call(
        body, name=name,
        in_specs=[vmem], out_specs=vmem,
        out_shape=jax.ShapeDtypeStruct((R, LANES), F32),
        scratch_shapes=[pltpu.VMEM((N_DEV, R, LANES), F32),
                        pltpu.SemaphoreType.DMA((N_DEV - 1,)), pltpu.SemaphoreType.DMA((N_DEV - 1,))],
        compiler_params=pltpu.CompilerParams(vmem_limit_bytes=VMEM_LIMIT),
    )(vec)


def pair_add(parts, theirs, place, name):
    _, R, C = theirs.shape
    tr = _pick(R, 256, 8)

    def body(place_ref, a_ref, b_ref, o_ref):
        o_ref[...] = (a_ref[...].astype(F32) + b_ref[...].astype(F32)).astype(BF16)

    blk = pl.BlockSpec((None, tr, C), lambda q, i, place_ref: (q, i, 0))
    return pl.pallas_call(
        body, name=name,
        grid_spec=pltpu.PrefetchScalarGridSpec(
            num_scalar_prefetch=1, grid=(4, R // tr),
            in_specs=[pl.BlockSpec((None, tr, C), lambda q, i, place_ref: (2 * q + place_ref[2], i, 0)), blk],
            out_specs=blk),
        out_shape=jax.ShapeDtypeStruct(theirs.shape, BF16),
        compiler_params=_params(("parallel", "parallel")),
    )(place, parts, theirs)


def _adamw_math(w, g, m, v):
    m = ADAM_B1 * m + (1.0 - ADAM_B1) * g
    v = ADAM_B2 * v + (1.0 - ADAM_B2) * jnp.square(g)
    m_hat = m / (1.0 - ADAM_B1 ** ADAM_STEP)
    v_hat = v / (1.0 - ADAM_B2 ** ADAM_STEP)
    delta = -ADAM_LR * (m_hat / (jnp.sqrt(v_hat) + ADAM_EPS) + ADAM_WD * w)
    return delta, m, v


def adamw_sharded(w, m, v, parts, sib, others, place, name):
    R, C = w.shape
    tr = _pick(R, 256, 8)

    def body(place_ref, w_ref, m_ref, v_ref, a_ref, b_ref, o_ref, g_ref, d_ref, nm_ref, nv_ref):
        g = a_ref[...].astype(F32) + b_ref[...].astype(F32)
        for j in range(3):
            g = g + o_ref[j].astype(F32)
        delta, nm, nv = _adamw_math(w_ref[...], g, m_ref[...], v_ref[...])
        g_ref[...] = g
        d_ref[...] = delta
        nm_ref[...] = nm
        nv_ref[...] = nv

    row = pl.BlockSpec((tr, C), lambda i, place_ref: (i, 0))
    return pl.pallas_call(
        body, name=name,
        grid_spec=pltpu.PrefetchScalarGridSpec(
            num_scalar_prefetch=1, grid=(R // tr,),
            in_specs=[row] * 3 + [pl.BlockSpec((None, tr, C), lambda i, place_ref: (place_ref[0], i, 0)),
                                  pl.BlockSpec((None, tr, C), lambda i, place_ref: (place_ref[1], i, 0)),
                                  pl.BlockSpec((3, tr, C), lambda i, place_ref: (0, i, 0))],
            out_specs=[row] * 4),
        out_shape=[jax.ShapeDtypeStruct((R, C), F32)] * 4,
        compiler_params=_params(("parallel",)),
    )(place, w, m, v, parts, sib, others)


def adamw_packed(w, g, m, v, name):
    R = w.shape[0]

    def body(w_ref, g_ref, m_ref, v_ref, d_ref, nm_ref, nv_ref):
        delta, nm, nv = _adamw_math(w_ref[...], g_ref[...], m_ref[...], v_ref[...])
        d_ref[...] = delta
        nm_ref[...] = nm
        nv_ref[...] = nv

    full = pl.BlockSpec((R, LANES), lambda i: (0, 0))
    return pl.pallas_call(
        body, name=name, grid=(1,),
        in_specs=[full] * 4, out_specs=[full] * 3,
        out_shape=[jax.ShapeDtypeStruct((R, LANES), F32)] * 3,
        compiler_params=_params(("arbitrary",)),
    )(w, g, m, v)


def _pack(arrays):
    flat = []
    sizes = []
    for a in arrays:
        f = a.reshape(-1).astype(F32)
        pad = (-f.shape[0]) % LANES
        if pad:
            f = jnp.concatenate([f, jnp.zeros((pad,), F32)])
        flat.append(f)
        sizes.append(f.shape[0])
    rows = sum(sizes) // LANES
    pad_rows = (-rows) % 8
    if pad_rows:
        flat.append(jnp.zeros((pad_rows * LANES,), F32))
    return jnp.concatenate(flat).reshape(-1, LANES), sizes


def _unpack(packed, sizes, shapes):
    flat = packed.reshape(-1)
    out = []
    off = 0
    for size, shape in zip(sizes, shapes):
        n = int(np.prod(shape))
        out.append(flat[off:off + n].reshape(shape))
        off += size
    return out


def _to_blocks(full, axis):
    if axis == 0:
        return full.reshape(N_DEV, full.shape[0] // N_DEV, full.shape[1])
    r, n = full.shape
    return full.reshape(r, N_DEV, n // N_DEV).transpose(1, 0, 2)


def _from_blocks(blocks, axis):
    if axis == 0:
        return blocks.reshape(blocks.shape[0] * blocks.shape[1], blocks.shape[2])
    return blocks.transpose(1, 0, 2).reshape(blocks.shape[1], blocks.shape[0] * blocks.shape[2])


def kernel(x, ln0_g, ln0_b, w_in, b_in, conv_w, w_a, w_b, w_o, b_o, ln1_g, ln1_b, w_up, b_up, ffn_conv_w, ffn_conv_b, w_down, b_down, ln2_g, ln2_b, loss_target, m_ln0_g, m_ln0_b, m_w_in, m_b_in, m_conv_w, m_w_a, m_w_b, m_w_o, m_b_o, m_ln1_g, m_ln1_b, m_w_up, m_b_up, m_ffn_conv_w, m_ffn_conv_b, m_w_down, m_b_down, m_ln2_g, m_ln2_b, v_ln0_g, v_ln0_b, v_w_in, v_b_in, v_conv_w, v_w_a, v_w_b, v_w_o, v_b_o, v_ln1_g, v_ln1_b, v_w_up, v_b_up, v_ffn_conv_w, v_ffn_conv_b, v_w_down, v_b_down, v_ln2_g, v_ln2_b):
    T, D = x.shape[1], x.shape[2]
    F = ffn_conv_b.shape[-1]
    xs = x.reshape(T, D)
    tgt = loss_target.reshape(T, D)
    dev = 4 * lax.axis_index("x") + 2 * lax.axis_index("y") + lax.axis_index("c")
    chip = 2 * lax.axis_index("x") + lax.axis_index("y")
    core = lax.axis_index("c")
    place = jnp.stack([dev, chip, core]).astype(jnp.int32)

    big = dict(w_in=(w_in[0], 1), w_a=(w_a[0], 0), w_b=(w_b[0], 1), w_o=(w_o[0], 0), w_up=(w_up[0], 1),
               w_down=(w_down[0], 0))
    names = list(big)
    shards = {k: big[k][0].astype(BF16) for k in names}
    g_in, g_conv, g_fcw = all_gather([shards["w_in"], conv_w[0], ffn_conv_w[0]], "gather_w_in")
    full = {"w_in": _from_blocks(g_in, 1)}
    conv_full = _from_blocks(g_conv, 1)
    fcw_full = _from_blocks(g_fcw, 1)
    late_groups = (("w_a", "w_b", "w_o"), ("w_up", "w_down"))
    late_handles = []
    token = None
    for n, keys in enumerate(late_groups):
        srcs = [shards[k] if token is None else shards[k] + token[0, 0].astype(BF16) for k in keys]
        handles, token = copies_start(srcs, [jax.ShapeDtypeStruct((N_DEV,) + s.shape, BF16) for s in srcs],
                                      _to_all_plan, N_DEV - 1, f"gather_late_{n}_start")
        late_handles.append(handles)

    def late_weights(n, after):
        lands = copies_wait(late_handles[n], _to_all_plan, after, f"gather_late_{n}_wait")
        for k, land in zip(late_groups[n], lands):
            full[k] = _from_blocks(lax.dynamic_update_index_in_dim(land, shards[k], dev, 0), big[k][1])

    o_q = 3 * D
    o_g = o_q + 3 * QKV_W
    w_pa, w_qkv, w_pg = full["w_in"][:, :o_q], full["w_in"][:, o_q:o_g], full["w_in"][:, o_g:]
    b_pa, b_qkv, b_pg = b_in[:, :o_q], b_in[:, o_q:o_g], b_in[:, o_g:]
    ln0g, ln0b = ln0_g.reshape(1, D), ln0_b.reshape(1, D)

    h0, h0b, *h0_res = ln_fwd(xs, None, ln0g, ln0b, "ln0_fwd", dilations=DILATIONS[1:])
    h0_res = [h0b] + [h.reshape(T, D) for h in h0_res]
    proj_a = mm_nn(h0b, w_pa, b_pa, F32, "proj_conv", after=token)
    proj_g = mm_nn(h0b, w_pg, b_pg, F32, "proj_gates")
    zero_d = jnp.zeros((1, D), F32)
    s_a = conv_a_fwd(proj_a, conv_full, "conv_a_fwd")
    late_weights(0, s_a)
    y_a = mm_nn(s_a, full["w_a"], zero_d, F32, "branch_a_out")

    def group_cols(m, g):
        return jnp.concatenate([m[:, s * QKV_W + g * GROUP_W:s * QKV_W + (g + 1) * GROUP_W] for s in range(3)], 1)

    w_grp = [group_cols(w_qkv, g) for g in range(3)]
    qkvs, outs, lses = [], [], []
    for g, d in enumerate(DILATIONS):
        qkv = mm_nn(h0_res[g], w_grp[g], group_cols(b_qkv, g), BF16, f"proj_qkv_{g}").reshape(d, T // d, 3 * GROUP_W)
        o, l = att_fwd(qkv, g, f"att_fwd_{g}")
        qkvs.append(qkv)
        outs.append(o)
        lses.append(l)
    comb = combine_fwd(outs, lses, "combine_fwd")
    y_b = mm_nn(comb, full["w_b"], zero_d, F32, "branch_b_out")
    z = gate_fwd(proj_g, y_a, y_b, "gate_fwd")
    mix = mm_nn(z, full["w_o"], b_o, F32, "mix_out")
    h1, h1b = ln_fwd(h0, mix, ln1_g, ln1_b, "ln1_fwd")
    late_weights(1, h1b)
    up = mm_nn(h1b, full["w_up"], b_up, F32, "ffn_up")
    f_act = conv_f_fwd(up, fcw_full, ffn_conv_b, "conv_f_fwd")
    ffn = mm_nn(f_act, full["w_down"], b_down, F32, "ffn_down")

    dr2, dr2b, d_ln2_g, d_ln2_b, d_b_down, loss_part = ln_bwd(h1, ffn, ln2_g, ln2_b, None, None, tgt, "ln2_loss_bwd")
    dw_down, _ = mm_tn(f_act, dr2b, "dw_down")
    df = mm_nt(dr2b, full["w_down"], None, "d_ffn_act")
    d_a, d_gate, cs_a, cs_gate, d_fcb, d_fcw = conv_f_bwd(df, up, fcw_full, ffn_conv_b, "conv_f_bwd")
    dw_up_a, _ = mm_tn(h1b, d_a, "dw_up_a")
    dw_up_g, _ = mm_tn(h1b, d_gate, "dw_up_gate")
    dh1 = mm_nt([d_a, d_gate], full["w_up"], None, "d_h1")
    dr1, dr1b, d_ln1_g, d_ln1_b, d_b_o, _ = ln_bwd(h0, mix, ln1_g, ln1_b, dr2, dh1, None, "ln1_bwd")
    dw_o, _ = mm_tn(z, dr1b, "dw_o")
    dz = mm_nt(dr1b, full["w_o"], None, "d_z")
    dy_a, dy_b, dproj_g = gate_bwd(dz, proj_g, y_a, y_b, "gate_bwd")
    dw_a, _ = mm_tn(s_a, dy_a, "dw_a")
    ds_a = mm_nt(dy_a, full["w_a"], None, "d_s_a")
    dproj_a, d_conv = conv_a_bwd(ds_a, proj_a, conv_full, "conv_a_bwd")
    dw_b, _ = mm_tn(comb, dy_b, "dw_b")

    rs_mine, rs_sib, rs_handles = {}, {}, {}

    def reduce_start(keys, grads, tag):
        parts = [_to_blocks(grads[k], big[k][1]) for k in keys]
        from_sib = exchange_sibling(parts, f"grads_to_sibling_{tag}")
        sums = [pair_add(a, b, place, f"chip_sum_{k}") for k, a, b in zip(keys, parts, from_sib)]
        handles, tok = copies_start(sums, [jax.ShapeDtypeStruct((3,) + s.shape[1:], BF16) for s in sums],
                                    _to_chips_plan, 3, f"grads_to_chips_{tag}_start")
        for k, a, b in zip(keys, parts, from_sib):
            rs_mine[k], rs_sib[k] = a, b
        rs_handles[tag] = (keys, handles)
        return tok

    tok_a = reduce_start(("w_a", "w_b", "w_o", "w_up", "w_down"),
                         dict(w_a=dw_a, w_b=dw_b, w_o=dw_o, w_up=jnp.concatenate([dw_up_a, dw_up_g], 1), w_down=dw_down),
                         "a")
    dcomb = mm_nt(dy_b, full["w_b"], None, "d_comb", after=tok_a)
    dos, dms = combine_bwd(dcomb, outs, lses, "combine_bwd")
    dw_grp, cs_grp, dqkvs = [], [], []
    for g, d in enumerate(DILATIONS):
        dq, dk, dv = att_bwd(qkvs[g], dos[g], lses[g], dms[g], g, f"att_bwd_{g}")
        dqkv = [t.reshape(T, GROUP_W) for t in (dq, dk, dv)]
        dwg, csg = mm_tn(h0_res[g], dqkv, f"dw_in_qkv_{g}")
        dqkvs.append(dqkv)
        dw_grp.append(dwg)
        cs_grp.append(csg)
    dw_pa, cs_pa = mm_tn(h0b, dproj_a, "dw_in_conv")
    dw_pg, cs_pg = mm_tn(h0b, dproj_g, "dw_in_gates")

    def ungroup(parts):
        return jnp.concatenate([p[:, s * GROUP_W:(s + 1) * GROUP_W] for s in range(3) for p in parts], 1)

    db_in_parts = [cs_pa, ungroup(cs_grp), cs_pg]
    tok_b = reduce_start(("w_in",), dict(w_in=jnp.concatenate([dw_pa, ungroup(dw_grp), dw_pg], 1)), "b")
    dh0 = mm_nt(dproj_a, w_pa, None, "d_h0_conv", after=tok_b)
    dh0 = mm_nt(dproj_g, w_pg, dh0, "d_h0_gates")
    dh0 = mm_nt(dqkvs[0], w_grp[0], dh0, "d_h0_qkv_0")
    dh0_res = [(mm_nt(dqkvs[g], w_grp[g], None, f"d_h0_qkv_{g}").reshape(d, T // d, D), d)
               for g, d in enumerate(DILATIONS) if g > 0]
    dx, _, d_ln0_g, d_ln0_b, _, _ = ln_bwd(xs, None, ln0g, ln0b, dr1, dh0, None, "ln0_bwd", by_residue=dh0_res)

    small = [d_ln0_g, d_ln0_b, jnp.concatenate(db_in_parts, 1), d_conv, d_b_o, d_ln1_g, d_ln1_b,
             jnp.concatenate([cs_a, cs_gate], 1), d_fcw, d_fcb, d_b_down, d_ln2_g, d_ln2_b, loss_part]
    packed, sizes = _pack(small)
    total = all_sum_small(packed, "sum_small")
    (g_ln0_g, g_ln0_b, g_b_in, g_conv_full, g_b_o, g_ln1_g, g_ln1_b, g_b_up, g_fcw_full, g_fcb, g_b_down, g_ln2_g,
     g_ln2_b, loss) = _unpack(total, sizes, [a.shape for a in small])
    cw = conv_w.shape[-1]
    fw = ffn_conv_w.shape[-1]
    g_conv = lax.dynamic_slice_in_dim(g_conv_full, dev * cw, cw, 1)
    g_fcw = lax.dynamic_slice_in_dim(g_fcw_full, dev * fw, fw, 1)

    from_chips = {}
    for tag, (keys, handles) in rs_handles.items():
        lands = copies_wait(handles, _to_chips_plan, total, f"grads_to_chips_{tag}_wait")
        from_chips.update(zip(keys, lands))

    moments = dict(w_in=(m_w_in, v_w_in), w_a=(m_w_a, v_w_a), w_b=(m_w_b, v_w_b), w_o=(m_w_o, v_w_o),
                   w_up=(m_w_up, v_w_up), w_down=(m_w_down, v_w_down))
    res_big = {}
    for k in names:
        res_big[k] = adamw_sharded(big[k][0], moments[k][0][0], moments[k][1][0], rs_mine[k], rs_sib[k], from_chips[k],
                                   place, f"adamw_{k}")

    small_names = ["ln0_g", "ln0_b", "b_in", "conv_w", "b_o", "ln1_g", "ln1_b", "b_up", "ffn_conv_w", "ffn_conv_b",
                   "b_down", "ln2_g", "ln2_b"]
    small_w = [ln0_g, ln0_b, b_in, conv_w, b_o, ln1_g, ln1_b, b_up, ffn_conv_w, ffn_conv_b, b_down, ln2_g, ln2_b]
    small_m = [m_ln0_g, m_ln0_b, m_b_in, m_conv_w, m_b_o, m_ln1_g, m_ln1_b, m_b_up, m_ffn_conv_w, m_ffn_conv_b,
               m_b_down, m_ln2_g, m_ln2_b]
    small_v = [v_ln0_g, v_ln0_b, v_b_in, v_conv_w, v_b_o, v_ln1_g, v_ln1_b, v_b_up, v_ffn_conv_w, v_ffn_conv_b,
               v_b_down, v_ln2_g, v_ln2_b]
    small_g = [g_ln0_g, g_ln0_b, g_b_in, g_conv, g_b_o, g_ln1_g, g_ln1_b, g_b_up, g_fcw, g_fcb, g_b_down, g_ln2_g,
               g_ln2_b]
    shapes = [w.shape for w in small_w]
    small_g = [g.reshape(s) for g, s in zip(small_g, shapes)]
    pw, psz = _pack(small_w)
    pg, _ = _pack(small_g)
    pm, _ = _pack(small_m)
    pv, _ = _pack(small_v)
    pd, pnm, pnv = adamw_packed(pw, pg, pm, pv, "adamw_small")
    res_small = {k: (g, d_, m_, v_) for k, g, d_, m_, v_ in zip(
        small_names, small_g, _unpack(pd, psz, shapes), _unpack(pnm, psz, shapes), _unpack(pnv, psz, shapes))}

    order = ["ln0_g", "ln0_b", "w_in", "b_in", "conv_w", "w_a", "w_b", "w_o", "b_o", "ln1_g", "ln1_b", "w_up", "b_up",
             "ffn_conv_w", "ffn_conv_b", "w_down", "b_down", "ln2_g", "ln2_b"]

    def result(k, j):
        if k in res_big:
            return res_big[k][j][None]
        return res_small[k][j]

    out = [loss.reshape(()), dx.reshape(x.shape)]
    for j in range(4):
        out += [result(k, j) for k in order]
    return tuple(out)
```

```python
import functools
import math

import numpy as np
import jax
import jax.numpy as jnp
from jax import lax
from jax.experimental import pallas as pl
from jax.experimental.pallas import tpu as pltpu

F32 = jnp.float32
BF16 = jnp.bfloat16
ACT = BF16

N_DEV = 8
LN_EPS = 1e-5
ALPHA = (2.0 * 1) ** 0.25
MASK_VALUE = -1e30
HEAD_DIM = 64
GROUP_W = 512
QKV_W = 3 * GROUP_W
DILATIONS = (1, 4, 16)
RADIUS = 64
LANES = 128
HALO = 8
HALO_BF16 = 16
ATT_TQ = 128

ADAM_LR = 0.001
ADAM_B1 = 0.9
ADAM_B2 = 0.999
ADAM_EPS = 1e-08
ADAM_WD = 0.01
ADAM_STEP = 10

VMEM_LIMIT = 52 * 1024 * 1024
OUT_TILE_BYTES = 8 * 1024 * 1024
MESH = pl.DeviceIdType.MESH
NT_DIMS = (((1,), (1,)), ((), ()))
TN_DIMS = (((0,), (0,)), ((), ()))


def _pick(n, target, align=LANES):
    if n <= target:
        return n
    best = None
    for t in range(align, target + 1, align):
        if n % t == 0:
            best = t
    assert best is not None, (n, target, align)
    return best


def _params(sems=None):
    return pltpu.CompilerParams(dimension_semantics=sems, vmem_limit_bytes=VMEM_LIMIT)


def _alibi_slopes():
    n = 3 * 8
    return np.exp2(-8.0 * np.arange(1, n + 1, dtype=np.float64) / n).astype(np.float32).reshape(3, 8)


def _ln_stats(r):
    mu = jnp.mean(r, -1, keepdims=True)
    xc = r - mu
    var = jnp.mean(xc * xc, -1, keepdims=True)
    rstd = lax.rsqrt(var + LN_EPS)
    return xc, rstd


def _load_natural(ref, d, scr):
    if d == 1:
        return ref[0]
    n, C = ref.shape[1], ref.shape[2]
    for c in range(C // LANES):
        for r in range(d):
            scr[c, pl.ds(r, n, stride=d), :] = ref[r, :, c * LANES:(c + 1) * LANES]
    return jnp.concatenate([scr[c] for c in range(C // LANES)], axis=1)


def _store_by_residue(val, ref, d, scr):
    if d == 1:
        ref[0] = val.astype(ref.dtype)
        return
    n, C = ref.shape[1], ref.shape[2]
    for c in range(C // LANES):
        scr[c] = val[:, c * LANES:(c + 1) * LANES]
    for c in range(C // LANES):
        for r in range(d):
            ref[r, :, c * LANES:(c + 1) * LANES] = scr[c, pl.ds(r, n, stride=d), :].astype(ref.dtype)


def _residue_spec(tm, d, C):
    return pl.BlockSpec((d, tm // d, C), lambda i: (0, i, 0))


def _residue_scratch(tm, C):
    return pltpu.VMEM((C // LANES, tm, LANES), F32)


def ln_fwd(a, res, g, b, name, dilations=()):
    T, D = a.shape
    tm = _pick(T, 512, 8)
    has_res = res is not None
    nd = len(dilations)

    def body(*refs):
        refs = list(refs)
        a_ref = refs.pop(0)
        r = a_ref[...]
        if has_res:
            r = ALPHA * r + refs.pop(0)[...]
        g_ref, b_ref, h_ref, hb_ref = refs[:4]
        xc, rstd = _ln_stats(r)
        h = xc * rstd * g_ref[...] + b_ref[...]
        h_ref[...] = h
        hb_ref[...] = h.astype(BF16)
        for d, p_ref in zip(dilations, refs[4:4 + nd]):
            _store_by_residue(h, p_ref, d, refs[-1])

    row = pl.BlockSpec((tm, D), lambda i: (i, 0))
    vec = pl.BlockSpec((1, D), lambda i: (0, 0))
    ins = [a] + ([res] if has_res else []) + [g, b]
    return pl.pallas_call(
        body, name=name, grid=(T // tm,),
        in_specs=[row] * (2 if has_res else 1) + [vec, vec],
        out_specs=[row, row] + [_residue_spec(tm, d, D) for d in dilations],
        out_shape=[jax.ShapeDtypeStruct((T, D), F32), jax.ShapeDtypeStruct((T, D), BF16)]
        + [jax.ShapeDtypeStruct((d, T // d, D), BF16) for d in dilations],
        scratch_shapes=[_residue_scratch(tm, D)] if nd else [],
        compiler_params=_params(("parallel",)),
    )(*ins)


def ln_bwd(a, res, g, b, d1, d2, tgt, name, by_residue=()):
    T, D = a.shape
    tm = _pick(T, 256, 8)
    has_res = res is not None
    loss_mode = tgt is not None
    nres = len(by_residue)

    def body(*refs):
        refs = list(refs)
        a_ref = refs.pop(0)
        r_ref = refs.pop(0) if has_res else None
        g_ref = refs.pop(0)
        b_ref = refs.pop(0)
        if loss_mode:
            t_ref = refs.pop(0)
        else:
            d1_ref = refs.pop(0)
            d2_ref = refs.pop(0)
        e_refs = [refs.pop(0) for _ in range(nres)]
        dr_ref, drb_ref, dg_ref, db_ref, ds_ref, loss_ref = refs[:6]
        i = pl.program_id(0)

        @pl.when(i == 0)
        def _():
            dg_ref[...] = jnp.zeros_like(dg_ref)
            db_ref[...] = jnp.zeros_like(db_ref)
            ds_ref[...] = jnp.zeros_like(ds_ref)
            loss_ref[...] = jnp.zeros_like(loss_ref)

        r = a_ref[...]
        if has_res:
            r = ALPHA * r + r_ref[...]
        xc, rstd = _ln_stats(r)
        xhat = xc * rstd
        gam = g_ref[...]
        if loss_mode:
            err = xhat * gam + b_ref[...] - t_ref[...]
            dy = err * (1.0 / D)
            row_loss = jnp.mean(err * err, -1, keepdims=True)
            loss_ref[...] += 0.5 * jnp.sum(row_loss, 0, keepdims=True)
        else:
            dy = ALPHA * d1_ref[...] + d2_ref[...]
        for (_, d), e_ref in zip(by_residue, e_refs):
            dy = dy + _load_natural(e_ref, d, refs[-1])
        dyg = dy * gam
        c1 = jnp.mean(dyg, -1, keepdims=True)
        c2 = jnp.mean(dyg * xhat, -1, keepdims=True)
        dr = rstd * (dyg - c1 - xhat * c2)
        dr_ref[...] = dr
        drb_ref[...] = dr.astype(BF16)
        dg_ref[...] += jnp.sum(dy * xhat, 0, keepdims=True)
        db_ref[...] += jnp.sum(dy, 0, keepdims=True)
        ds_ref[...] += jnp.sum(dr, 0, keepdims=True)

    row = pl.BlockSpec((tm, D), lambda i: (i, 0))
    vec = pl.BlockSpec((1, D), lambda i: (0, 0))
    one = pl.BlockSpec((1, 1), lambda i: (0, 0))
    ins = [a] + ([res] if has_res else []) + [g, b] + ([tgt] if loss_mode else [d1, d2]) + [e for e, _ in by_residue]
    in_specs = [row] * (2 if has_res else 1) + [vec, vec] + [row] * (1 if loss_mode else 2)
    in_specs += [_residue_spec(tm, d, D) for _, d in by_residue]
    return pl.pallas_call(
        body, name=name, grid=(T // tm,),
        in_specs=in_specs,
        out_specs=[row, row, vec, vec, vec, one],
        out_shape=[jax.ShapeDtypeStruct((T, D), F32), jax.ShapeDtypeStruct((T, D), BF16),
                   jax.ShapeDtypeStruct((1, D), F32), jax.ShapeDtypeStruct((1, D), F32),
                   jax.ShapeDtypeStruct((1, D), F32), jax.ShapeDtypeStruct((1, 1), F32)],
        scratch_shapes=[_residue_scratch(tm, D)] if nres else [],
        compiler_params=_params(("arbitrary",)),
    )(*ins)


_TOKEN_SPEC = pl.BlockSpec((8, LANES), lambda i: (0, 0))


def mm_nn(a, w, bias, out_dtype, name, after=None):
    M, K = a.shape
    N = w.shape[1]
    tm = _pick(M, max(256, min(1024, OUT_TILE_BYTES // (N * jnp.dtype(out_dtype).itemsize))), 8)
    tc = _pick(N, 512)

    def body(a_ref, w_ref, b_ref, *rest):
        o_ref = rest[-1]
        av = a_ref[...]
        for j in range(N // tc):
            cols = slice(j * tc, (j + 1) * tc)
            acc = jnp.dot(av, w_ref[:, cols], preferred_element_type=F32)
            o_ref[:, cols] = (acc + b_ref[:, cols]).astype(out_dtype)

    return pl.pallas_call(
        body, name=name, grid=(M // tm,),
        in_specs=[pl.BlockSpec((tm, K), lambda i: (i, 0)),
                  pl.BlockSpec((K, N), lambda i: (0, 0)),
                  pl.BlockSpec((1, N), lambda i: (0, 0))] + ([] if after is None else [_TOKEN_SPEC]),
        out_specs=pl.BlockSpec((tm, N), lambda i: (i, 0)),
        out_shape=jax.ShapeDtypeStruct((M, N), out_dtype),
        compiler_params=_params(("parallel",)),
    )(a, w, bias, *([] if after is None else [after]))


def mm_nt(a, w, acc_in, name, after=None, w_block=0, out_dtype=F32):
    pieces = list(a) if isinstance(a, (list, tuple)) else [a]
    M = pieces[0].shape[0]
    widths = [p.shape[1] for p in pieces]
    K = sum(widths)
    N = w.shape[0]
    tm = _pick(M, 512, 8)
    tc = _pick(N, 512)
    has_acc = acc_in is not None
    n_a = len(pieces)

    def body(*refs):
        a_refs, w_ref = refs[:n_a], refs[n_a]
        c_ref = refs[n_a + 1] if has_acc else None
        o_ref = refs[-1]
        av = a_refs[0][...] if n_a == 1 else jnp.concatenate([r[...] for r in a_refs], axis=1)
        for j in range(N // tc):
            cols = slice(j * tc, (j + 1) * tc)
            acc = lax.dot_general(av, w_ref[cols, :], NT_DIMS, preferred_element_type=F32)
            if has_acc:
                acc = acc + c_ref[:, cols]
            o_ref[:, cols] = acc.astype(out_dtype)

    out_spec = pl.BlockSpec((tm, N), lambda i: (i, 0))
    in_specs = [pl.BlockSpec((tm, kw), lambda i: (i, 0)) for kw in widths]
    in_specs.append(pl.BlockSpec((N, K), lambda i: (0, w_block)))
    ins = pieces + [w]
    if has_acc:
        in_specs.append(out_spec)
        ins.append(acc_in)
    if after is not None:
        in_specs.append(_TOKEN_SPEC)
        ins.append(after)
    return pl.pallas_call(
        body, name=name, grid=(M // tm,),
        in_specs=in_specs, out_specs=out_spec,
        out_shape=jax.ShapeDtypeStruct((M, N), out_dtype),
        compiler_params=_params(("parallel",)),
    )(*ins)


def mm_tn(a, b, name, out_dtype=BF16):
    pieces = list(b) if isinstance(b, (list, tuple)) else [b]
    T, M = a.shape
    widths = [p.shape[1] for p in pieces]
    N = sum(widths)
    tk = _pick(T, 512, 8)
    nk = T // tk
    tc = _pick(M, 256)
    n_b = len(pieces)

    def body(*refs):
        a_ref, b_refs = refs[0], refs[1:1 + n_b]
        o_ref, cs_ref, acc_ref = refs[1 + n_b:]
        k = pl.program_id(0)

        @pl.when(k == 0)
        def _():
            acc_ref[...] = jnp.zeros_like(acc_ref)
            cs_ref[...] = jnp.zeros_like(cs_ref)

        bv = b_refs[0][...] if n_b == 1 else jnp.concatenate([r[...] for r in b_refs], axis=1)
        cs_ref[...] += jnp.sum(bv.astype(F32), 0, keepdims=True)
        for mi in range(M // tc):
            rows = slice(mi * tc, (mi + 1) * tc)
            acc_ref[rows, :] += lax.dot_general(a_ref[:, rows], bv, TN_DIMS, preferred_element_type=F32)

        @pl.when(k == nk - 1)
        def _():
            o_ref[...] = acc_ref[...].astype(out_dtype)

    return pl.pallas_call(
        body, name=name, grid=(nk,),
        in_specs=[pl.BlockSpec((tk, M), lambda k: (k, 0))] + [pl.BlockSpec((tk, wd), lambda k: (k, 0)) for wd in widths],
        out_specs=[pl.BlockSpec((M, N), lambda k: (0, 0)), pl.BlockSpec((1, N), lambda k: (0, 0))],
        out_shape=[jax.ShapeDtypeStruct((M, N), out_dtype), jax.ShapeDtypeStruct((1, N), F32)],
        scratch_shapes=[pltpu.VMEM((M, N), F32)],
        compiler_params=_params(("arbitrary",)),
    )(a, *pieces)


def _ext_rows(prev_ref, main_ref, next_ref, i, tm, T):
    before = jnp.where(i == 0, 0.0, prev_ref[...])
    after = jnp.where(i == T // tm - 1, 0.0, next_ref[...])
    return jnp.concatenate([before, main_ref[...], after], axis=0).astype(F32)


def _prev_row(x):
    return pltpu.roll(x, 1, 0)


def _next_row(x):
    return pltpu.roll(x, x.shape[0] - 1, 0)


def _conv3(u, w_ref):
    return _prev_row(u) * w_ref[0:1, :] + u * w_ref[1:2, :] + _next_row(u) * w_ref[2:3, :]


def _main(x, tm, halo=HALO):
    return x[halo:halo + tm]


def _halo_specs(tm, tc, T, col, order, halo=HALO):
    r = tm // halo
    last = T // halo - 1
    if order == "ij":
        return (pl.BlockSpec((halo, tc), lambda i, j: (jnp.maximum(i * r - 1, 0), col(j))),
                pl.BlockSpec((tm, tc), lambda i, j: (i, col(j))),
                pl.BlockSpec((halo, tc), lambda i, j: (jnp.minimum((i + 1) * r, last), col(j))))
    return (pl.BlockSpec((halo, tc), lambda j, i: (jnp.maximum(i * r - 1, 0), col(j))),
            pl.BlockSpec((tm, tc), lambda j, i: (i, col(j))),
            pl.BlockSpec((halo, tc), lambda j, i: (jnp.minimum((i + 1) * r, last), col(j))))


def conv_a_fwd(proj_a, conv_w, name):
    T, D3 = proj_a.shape
    D = D3 // 3
    tm = _pick(T, 256, 8)

    def body(p_ref, m_ref, n_ref, w_ref, o_ref):
        i = pl.program_id(0)
        ext = _ext_rows(p_ref, m_ref, n_ref, i, tm, T)
        u = ext[:, D:2 * D] * ext[:, 2 * D:]
        cu = _conv3(u, w_ref)
        o_ref[...] = (m_ref[:, :D].astype(F32) * _main(cu, tm, HALO_BF16)).astype(BF16)

    prev, main, nxt = _halo_specs(tm, D3, T, lambda j: 0, "ij", HALO_BF16)
    return pl.pallas_call(
        body, name=name, grid=(T // tm, 1),
        in_specs=[prev, main, nxt, pl.BlockSpec((3, D), lambda i, j: (0, 0))],
        out_specs=pl.BlockSpec((tm, D), lambda i, j: (i, 0)),
        out_shape=jax.ShapeDtypeStruct((T, D), BF16),
        compiler_params=_params(("parallel", "arbitrary")),
    )(proj_a, proj_a, proj_a, conv_w)


def conv_a_bwd(ds_a, proj_a, conv_w, name):
    T, D3 = proj_a.shape
    D = D3 // 3
    tm = _pick(T, 256, 8)

    def body(dp_ref, dm_ref, dn_ref, p_ref, m_ref, n_ref, w_ref, o_ref, dw_ref):
        i = pl.program_id(0)

        @pl.when(i == 0)
        def _():
            dw_ref[...] = jnp.zeros_like(dw_ref)

        ext = _ext_rows(p_ref, m_ref, n_ref, i, tm, T)
        dsa = _ext_rows(dp_ref, dm_ref, dn_ref, i, tm, T)
        gb, gc, hin = ext[:, :D], ext[:, D:2 * D], ext[:, 2 * D:]
        u = gc * hin
        u_prev, u_next = _prev_row(u), _next_row(u)
        cu = u_prev * w_ref[0:1, :] + u * w_ref[1:2, :] + u_next * w_ref[2:3, :]
        dcu = dsa * gb
        du = _next_row(dcu) * w_ref[0:1, :] + dcu * w_ref[1:2, :] + _prev_row(dcu) * w_ref[2:3, :]
        h = HALO_BF16
        o_ref[:, :D] = _main(dsa * cu, tm, h).astype(BF16)
        o_ref[:, D:2 * D] = _main(du * hin, tm, h).astype(BF16)
        o_ref[:, 2 * D:] = _main(du * gc, tm, h).astype(BF16)
        dcu_m = _main(dcu, tm, h)
        dw_ref[0:1, :] += jnp.sum(dcu_m * _main(u_prev, tm, h), 0, keepdims=True)
        dw_ref[1:2, :] += jnp.sum(dcu_m * _main(u, tm, h), 0, keepdims=True)
        dw_ref[2:3, :] += jnp.sum(dcu_m * _main(u_next, tm, h), 0, keepdims=True)

    dprev, dmain, dnxt = _halo_specs(tm, D, T, lambda j: 0, "ij", HALO_BF16)
    prev, main, nxt = _halo_specs(tm, D3, T, lambda j: 0, "ij", HALO_BF16)
    return pl.pallas_call(
        body, name=name, grid=(T // tm, 1),
        in_specs=[dprev, dmain, dnxt, prev, main, nxt, pl.BlockSpec((3, D), lambda i, j: (0, 0))],
        out_specs=[pl.BlockSpec((tm, D3), lambda i, j: (i, 0)), pl.BlockSpec((3, D), lambda i, j: (0, 0))],
        out_shape=[jax.ShapeDtypeStruct((T, D3), BF16), jax.ShapeDtypeStruct((3, D), F32)],
        compiler_params=_params(("arbitrary", "arbitrary")),
    )(ds_a, ds_a, ds_a, proj_a, proj_a, proj_a, conv_w)


_INV_SQRT2 = 1.0 / math.sqrt(2.0)
_INV_SQRT_2PI = 1.0 / math.sqrt(2.0 * math.pi)


def conv_f_fwd(up, fcw, fcb, name):
    T, F2 = up.shape
    F = F2 // 2
    tm = _pick(T, 256, 8)
    tc = _pick(F, 1408)
    nc = F // tc

    def body(p_ref, m_ref, n_ref, g_ref, w_ref, b_ref, o_ref):
        i = pl.program_id(0)
        a = _ext_rows(p_ref, m_ref, n_ref, i, tm, T)
        ca = _main(_conv3(a, w_ref), tm) + b_ref[...]
        gl = 0.5 * ca * (1.0 + lax.erf(ca * _INV_SQRT2))
        o_ref[...] = (gl * g_ref[...]).astype(BF16)

    prev, main, nxt = _halo_specs(tm, tc, T, lambda j: j, "ij")
    return pl.pallas_call(
        body, name=name, grid=(T // tm, nc),
        in_specs=[prev, main, nxt,
                  pl.BlockSpec((tm, tc), lambda i, j: (i, nc + j)),
                  pl.BlockSpec((3, tc), lambda i, j: (0, j)),
                  pl.BlockSpec((1, tc), lambda i, j: (0, j))],
        out_specs=pl.BlockSpec((tm, tc), lambda i, j: (i, j)),
        out_shape=jax.ShapeDtypeStruct((T, F), BF16),
        compiler_params=_params(("parallel", "parallel")),
    )(up, up, up, up, fcw, fcb)


def conv_f_bwd(df, up, fcw, fcb, name):
    T, F2 = up.shape
    F = F2 // 2
    tm = _pick(T, 256, 8)
    tc = _pick(F, 1408)
    nc = F // tc

    def body(fp_ref, fm_ref, fn_ref, ap_ref, am_ref, an_ref, gp_ref, gm_ref, gn_ref, w_ref, b_ref,
             da_ref, dg_ref, csa_ref, csg_ref, dfb_ref, dfw_ref):
        i = pl.program_id(1)

        @pl.when(i == 0)
        def _():
            csa_ref[...] = jnp.zeros_like(csa_ref)
            csg_ref[...] = jnp.zeros_like(csg_ref)
            dfb_ref[...] = jnp.zeros_like(dfb_ref)
            dfw_ref[...] = jnp.zeros_like(dfw_ref)

        dfe = _ext_rows(fp_ref, fm_ref, fn_ref, i, tm, T)
        a = _ext_rows(ap_ref, am_ref, an_ref, i, tm, T)
        gate = _ext_rows(gp_ref, gm_ref, gn_ref, i, tm, T)
        a_prev, a_next = _prev_row(a), _next_row(a)
        ca = a_prev * w_ref[0:1, :] + a * w_ref[1:2, :] + a_next * w_ref[2:3, :] + b_ref[...]
        cdf = 0.5 * (1.0 + lax.erf(ca * _INV_SQRT2))
        gl = ca * cdf
        gp = cdf + ca * (jnp.exp(-0.5 * ca * ca) * _INV_SQRT_2PI)
        dgate = _main(dfe * gl, tm)
        dca = dfe * gate * gp
        da = _main(_next_row(dca) * w_ref[0:1, :] + dca * w_ref[1:2, :] + _prev_row(dca) * w_ref[2:3, :], tm)
        da_ref[...] = da.astype(BF16)
        dg_ref[...] = dgate.astype(BF16)
        csa_ref[...] += jnp.sum(da, 0, keepdims=True)
        csg_ref[...] += jnp.sum(dgate, 0, keepdims=True)
        dca_m = _main(dca, tm)
        dfb_ref[...] += jnp.sum(dca_m, 0, keepdims=True)
        dfw_ref[0:1, :] += jnp.sum(dca_m * _main(a_prev, tm), 0, keepdims=True)
        dfw_ref[1:2, :] += jnp.sum(dca_m * _main(a, tm), 0, keepdims=True)
        dfw_ref[2:3, :] += jnp.sum(dca_m * _main(a_next, tm), 0, keepdims=True)

    fprev, fmain, fnxt = _halo_specs(tm, tc, T, lambda j: j, "ji")
    gprev, gmain, gnxt = _halo_specs(tm, tc, T, lambda j: nc + j, "ji")
    tile = pl.BlockSpec((tm, tc), lambda j, i: (i, j))
    vec = pl.BlockSpec((1, tc), lambda j, i: (0, j))
    vec3 = pl.BlockSpec((3, tc), lambda j, i: (0, j))
    return pl.pallas_call(
        body, name=name, grid=(nc, T // tm),
        in_specs=[fprev, fmain, fnxt, fprev, fmain, fnxt, gprev, gmain, gnxt, vec3, vec],
        out_specs=[tile, tile, vec, vec, vec, vec3],
        out_shape=[jax.ShapeDtypeStruct((T, F), BF16), jax.ShapeDtypeStruct((T, F), BF16),
                   jax.ShapeDtypeStruct((1, F), F32), jax.ShapeDtypeStruct((1, F), F32),
                   jax.ShapeDtypeStruct((1, F), F32), jax.ShapeDtypeStruct((3, F), F32)],
        compiler_params=_params(("arbitrary", "arbitrary")),
    )(df, df, df, up, up, up, up, up, up, fcw, fcb)


def gate_fwd(proj_g, y_a, y_b, name):
    T, D = y_a.shape
    tm = _pick(T, 512, 8)

    def body(g_ref, a_ref, b_ref, o_ref):
        sa = jax.nn.sigmoid(g_ref[:, :D].astype(F32))
        sb = jax.nn.sigmoid(g_ref[:, D:].astype(F32))
        o_ref[...] = (sa * a_ref[...].astype(F32) + sb * b_ref[...].astype(F32)).astype(BF16)

    row = pl.BlockSpec((tm, D), lambda i: (i, 0))
    return pl.pallas_call(
        body, name=name, grid=(T // tm,),
        in_specs=[pl.BlockSpec((tm, 2 * D), lambda i: (i, 0)), row, row],
        out_specs=row,
        out_shape=jax.ShapeDtypeStruct((T, D), BF16),
        compiler_params=_params(("parallel",)),
    )(proj_g, y_a, y_b)


def gate_bwd(dz, proj_g, y_a, y_b, name):
    T, D = y_a.shape
    tm = _pick(T, 512, 8)

    def body(dz_ref, g_ref, a_ref, b_ref, da_ref, db_ref, dg_ref):
        dzv = dz_ref[...].astype(F32)
        sa = jax.nn.sigmoid(g_ref[:, :D].astype(F32))
        sb = jax.nn.sigmoid(g_ref[:, D:].astype(F32))
        da_ref[...] = (dzv * sa).astype(BF16)
        db_ref[...] = (dzv * sb).astype(BF16)
        dg_ref[:, :D] = (dzv * a_ref[...].astype(F32) * (sa * (1.0 - sa))).astype(BF16)
        dg_ref[:, D:] = (dzv * b_ref[...].astype(F32) * (sb * (1.0 - sb))).astype(BF16)

    row = pl.BlockSpec((tm, D), lambda i: (i, 0))
    wide = pl.BlockSpec((tm, 2 * D), lambda i: (i, 0))
    return pl.pallas_call(
        body, name=name, grid=(T // tm,),
        in_specs=[row, wide, row, row],
        out_specs=[row, row, wide],
        out_shape=[jax.ShapeDtypeStruct((T, D), BF16), jax.ShapeDtypeStruct((T, D), BF16),
                   jax.ShapeDtypeStruct((T, 2 * D), BF16)],
        compiler_params=_params(("parallel",)),
    )(dz, proj_g, y_a, y_b)


ATT_WIN = ATT_TQ + 2 * RADIUS
ATT_STEP = 512
FAR = 1e32


def _att_window(qs, L):
    ks = pl.multiple_of(jnp.clip(qs - RADIUS, 0, L - ATT_WIN), RADIUS)
    return ks, jnp.where(qs == 0, 0, jnp.where(qs == L - ATT_TQ, 2, 1))


def _fill_bias_tables(bias_ref, sl_ref, hp, d):
    col_row = (lax.broadcasted_iota(jnp.int32, (ATT_TQ, ATT_WIN), 1)
               - lax.broadcasted_iota(jnp.int32, (ATT_TQ, ATT_WIN), 0))
    for v in range(3):
        ad = jnp.abs(col_row - v * RADIUS)
        dist = jnp.where(ad <= RADIUS, (ad * d).astype(F32), FAR)
        bias_ref[v, 0:ATT_TQ, :] = sl_ref[hp * 2] * dist
        bias_ref[v, ATT_TQ:2 * ATT_TQ, :] = sl_ref[hp * 2 + 1] * dist


def _head_masks():
    lane = lax.broadcasted_iota(jnp.int32, (1, LANES), 1)
    return [lane < HEAD_DIM, lane >= HEAD_DIM]


def _stack_heads(x, masks):
    zero = jnp.zeros_like(x)
    return jnp.concatenate([jnp.where(masks[0], x, zero), jnp.where(masks[1], x, zero)], axis=0)


def _unstack_heads(x2, masks):
    n = x2.shape[0] // 2
    return jnp.where(masks[0], x2[:n], x2[n:])


def _att_step(L):
    step = min(ATT_STEP, L)
    assert L % step == 0 and step % ATT_TQ == 0 and L >= ATT_WIN
    return step


def att_fwd(qkv, group, name):
    d, L, _ = qkv.shape
    step = _att_step(L)
    cg = GROUP_W // LANES
    slopes = jnp.asarray(_alibi_slopes()[group])
    scale = HEAD_DIM ** -0.5

    def body(sl_ref, q_ref, k_ref, v_ref, o_ref, l_ref, bias_ref):
        hp = pl.program_id(1)
        i = pl.program_id(2)

        @pl.when(i == 0)
        def _():
            _fill_bias_tables(bias_ref, sl_ref, hp, d)

        masks = _head_masks()
        for t in range(step // ATT_TQ):
            rows = slice(t * ATT_TQ, (t + 1) * ATT_TQ)
            ks, table = _att_window(i * step + t * ATT_TQ, L)
            q2 = _stack_heads(q_ref[rows, :] * scale, masks)
            kw = k_ref[pl.ds(ks, ATT_WIN), :]
            vw = v_ref[pl.ds(ks, ATT_WIN), :]
            s = lax.dot_general(q2, kw, NT_DIMS, preferred_element_type=F32) - bias_ref[table]
            m = jnp.max(s, -1, keepdims=True)
            p = jnp.exp(s - m)
            den = jnp.sum(p, -1, keepdims=True)
            pn = (p / den).astype(BF16)
            o2 = jnp.dot(pn, vw, preferred_element_type=F32)
            o_ref[rows, :] = _unstack_heads(o2, masks)
            l_ref[rows, :] = _unstack_heads(m + jnp.log(den), masks)

    out_spec = pl.BlockSpec((None, step, LANES), lambda r, hp, i: (r, i, hp))
    return pl.pallas_call(
        body, name=name, grid=(d, cg, L // step),
        in_specs=[pl.BlockSpec(memory_space=pltpu.SMEM),
                  pl.BlockSpec((None, step, LANES), lambda r, hp, i: (r, i, hp)),
                  pl.BlockSpec((None, L, LANES), lambda r, hp, i: (r, 0, cg + hp)),
                  pl.BlockSpec((None, L, LANES), lambda r, hp, i: (r, 0, 2 * cg + hp))],
        out_specs=[out_spec, out_spec],
        out_shape=[jax.ShapeDtypeStruct((d, L, GROUP_W), F32)] * 2,
        scratch_shapes=[pltpu.VMEM((3, 2 * ATT_TQ, ATT_WIN), F32)],
        compiler_params=_params(("arbitrary", "arbitrary", "arbitrary")),
    )(slopes, qkv, qkv, qkv)


def att_bwd(qkv, do, lse, dmat, group, name):
    d, L, _ = qkv.shape
    step = _att_step(L)
    nq = L // step
    cg = GROUP_W // LANES
    slopes = jnp.asarray(_alibi_slopes()[group])
    scale = HEAD_DIM ** -0.5

    def body(sl_ref, q_ref, k_ref, v_ref, do_ref, l_ref, dm_ref, dq_ref, dk_ref, dv_ref, dk_acc, dv_acc, bias_ref):
        hp = pl.program_id(1)
        i = pl.program_id(2)

        @pl.when(i == 0)
        def _():
            dk_acc[...] = jnp.zeros_like(dk_acc)
            dv_acc[...] = jnp.zeros_like(dv_acc)
            _fill_bias_tables(bias_ref, sl_ref, hp, d)

        masks = _head_masks()

        def head_cols(x):
            return jnp.concatenate([jnp.max(jnp.where(hm, x, -jnp.inf), -1, keepdims=True) for hm in masks], axis=0)

        for t in range(step // ATT_TQ):
            rows = slice(t * ATT_TQ, (t + 1) * ATT_TQ)
            ks, table = _att_window(i * step + t * ATT_TQ, L)
            q2 = _stack_heads(q_ref[rows, :] * scale, masks)
            do2 = _stack_heads(do_ref[rows, :], masks)
            kw = k_ref[pl.ds(ks, ATT_WIN), :]
            vw = v_ref[pl.ds(ks, ATT_WIN), :]
            s = lax.dot_general(q2, kw, NT_DIMS, preferred_element_type=F32) - bias_ref[table]
            p = jnp.exp(s - head_cols(l_ref[rows, :]))
            dp = lax.dot_general(do2, vw, NT_DIMS, preferred_element_type=F32)
            ds = (p * (dp - head_cols(dm_ref[rows, :]))).astype(BF16)
            dq2 = jnp.dot(ds, kw, preferred_element_type=F32)
            dq_ref[rows, :] = (_unstack_heads(dq2, masks) * scale).astype(BF16)
            dk_acc[pl.ds(ks, ATT_WIN), :] += lax.dot_general(ds, q2, TN_DIMS, preferred_element_type=F32)
            dv_acc[pl.ds(ks, ATT_WIN), :] += lax.dot_general(p.astype(BF16), do2, TN_DIMS, preferred_element_type=F32)

        @pl.when(i == nq - 1)
        def _():
            dk_ref[...] = dk_acc[...].astype(BF16)
            dv_ref[...] = dv_acc[...].astype(BF16)

    tile = pl.BlockSpec((None, step, LANES), lambda r, hp, i: (r, i, hp))
    whole = pl.BlockSpec((None, L, LANES), lambda r, hp, i: (r, 0, hp))
    return pl.pallas_call(
        body, name=name, grid=(d, cg, nq),
        in_specs=[pl.BlockSpec(memory_space=pltpu.SMEM), tile,
                  pl.BlockSpec((None, L, LANES), lambda r, hp, i: (r, 0, cg + hp)),
                  pl.BlockSpec((None, L, LANES), lambda r, hp, i: (r, 0, 2 * cg + hp)),
                  tile, tile, tile],
        out_specs=[tile, whole, whole],
        out_shape=[jax.ShapeDtypeStruct((d, L, GROUP_W), BF16)] * 3,
        scratch_shapes=[pltpu.VMEM((L, LANES), F32), pltpu.VMEM((L, LANES), F32),
                        pltpu.VMEM((3, 2 * ATT_TQ, ATT_WIN), F32)],
        compiler_params=_params(("arbitrary", "arbitrary", "arbitrary")),
    )(slopes, qkv, qkv, qkv, do, lse, dmat)


def _group_weights(ls):
    m = jnp.maximum(jnp.maximum(ls[0], ls[1]), ls[2])
    es = [jnp.exp(l - m) for l in ls]
    tot = es[0] + es[1] + es[2]
    return [e / tot for e in es]


def combine_fwd(outs, lses, name):
    T = outs[0].shape[0] * outs[0].shape[1]
    tm = _pick(T, 512, 8)
    n_scr = 2 * (len(DILATIONS) - 1)

    def body(*refs):
        o_refs, l_refs, c_ref, scr = refs[:3], refs[3:6], refs[6], refs[7:]
        o = [_load_natural(o_refs[g], d, scr[g - 1] if g else None) for g, d in enumerate(DILATIONS)]
        l = [_load_natural(l_refs[g], d, scr[g + 1] if g else None) for g, d in enumerate(DILATIONS)]
        w = _group_weights(l)
        c_ref[...] = (w[0] * o[0] + w[1] * o[1] + w[2] * o[2]).astype(BF16)

    specs = [_residue_spec(tm, d, GROUP_W) for d in DILATIONS]
    return pl.pallas_call(
        body, name=name, grid=(T // tm,),
        in_specs=specs + specs, out_specs=pl.BlockSpec((tm, GROUP_W), lambda i: (i, 0)),
        out_shape=jax.ShapeDtypeStruct((T, GROUP_W), BF16),
        scratch_shapes=[_residue_scratch(tm, GROUP_W)] * n_scr,
        compiler_params=_params(("parallel",)),
    )(*outs, *lses)


def combine_bwd(dcomb, outs, lses, name):
    T = dcomb.shape[0]
    tm = _pick(T, 256, 8)
    head = np.arange(GROUP_W) // HEAD_DIM
    seg = jnp.asarray((head[:, None] == head[None, :]).astype(np.float32)).astype(BF16)
    ng = len(DILATIONS)
    n_scr = 4 * (ng - 1)

    def body(*refs):
        dc_ref, o_refs, l_refs, e_ref = refs[0], refs[1:1 + ng], refs[1 + ng:1 + 2 * ng], refs[1 + 2 * ng]
        do_refs, dm_refs = refs[2 + 2 * ng:2 + 3 * ng], refs[2 + 3 * ng:2 + 4 * ng]
        scr = refs[2 + 4 * ng:]
        o = [_load_natural(o_refs[g], d, scr[4 * (g - 1)] if g else None) for g, d in enumerate(DILATIONS)]
        l = [_load_natural(l_refs[g], d, scr[4 * (g - 1) + 1] if g else None) for g, d in enumerate(DILATIONS)]
        w = _group_weights(l)
        dc = dc_ref[...].astype(F32)
        e = e_ref[...]
        tot = jnp.zeros_like(dc)
        for g in range(ng):
            prod = dc * o[g]
            dw = jnp.zeros_like(dc)
            for _ in range(3):
                part = prod.astype(BF16)
                dw = dw + jnp.dot(part, e, preferred_element_type=F32)
                prod = prod - part.astype(F32)
            tot = tot + w[g] * dw
        for g, d in enumerate(DILATIONS):
            _store_by_residue(w[g] * dc, do_refs[g], d, scr[4 * (g - 1) + 2] if g else None)
            _store_by_residue(w[g] * tot, dm_refs[g], d, scr[4 * (g - 1) + 3] if g else None)

    specs = [_residue_spec(tm, d, GROUP_W) for d in DILATIONS]
    res = pl.pallas_call(
        body, name=name, grid=(T // tm,),
        in_specs=[pl.BlockSpec((tm, GROUP_W), lambda i: (i, 0))] + specs + specs
        + [pl.BlockSpec((GROUP_W, GROUP_W), lambda i: (0, 0))],
        out_specs=specs + specs,
        out_shape=[jax.ShapeDtypeStruct(o.shape, BF16) for o in outs] + [jax.ShapeDtypeStruct(o.shape, F32) for o in outs],
        scratch_shapes=[_residue_scratch(tm, GROUP_W)] * n_scr,
        compiler_params=_params(("parallel",)),
    )(dcomb, *outs, *lses, seg)
    return res[:ng], res[ng:]


def _position():
    return lax.axis_index("x"), lax.axis_index("y"), lax.axis_index("c")


def _other_chips(x, y):
    return [(1 - x, y), (x, 1 - y), (1 - x, 1 - y)]


def _remote(src, dst, send_sems, recv_sems, k, to):
    return pltpu.make_async_remote_copy(src_ref=src, dst_ref=dst, send_sem=send_sems.at[k], recv_sem=recv_sems.at[k],
                                        device_id=to, device_id_type=MESH)


def all_gather(shards, name):
    n = len(shards)

    def body(*refs):
        ins, outs = refs[:n], refs[n:2 * n]
        send_sems, recv_sems, local_sems = refs[2 * n:]
        x, y, c = _position()
        sibling = (x, y, 1 - c)
        chips = _other_chips(x, y)

        def block(a, px, py, pc):
            return outs[a].at[4 * px + 2 * py + pc]

        own, first, passed = [], [], []
        for a in range(n):
            cp = pltpu.make_async_copy(ins[a], block(a, x, y, c), local_sems.at[a])
            cp.start()
            own.append(cp)
            k0 = 7 * a
            first.append(_remote(ins[a], block(a, x, y, c), send_sems, recv_sems, k0, sibling))
            for j, chip in enumerate(chips):
                first.append(_remote(ins[a], block(a, x, y, c), send_sems, recv_sems, k0 + 1 + j, (*chip, c)))
        for cp in first:
            cp.start()
        for a in range(n):
            k0 = 7 * a
            for j, chip in enumerate(chips):
                got = block(a, *chip, c)
                _remote(got, got, send_sems, recv_sems, k0 + 1 + j, sibling).wait_recv()
                fwd = _remote(got, got, send_sems, recv_sems, k0 + 4 + j, sibling)
                fwd.start()
                passed.append(fwd)
        for a in range(n):
            k0 = 7 * a
            got = block(a, x, y, 1 - c)
            _remote(got, got, send_sems, recv_sems, k0, sibling).wait_recv()
            for j, chip in enumerate(chips):
                got = block(a, *chip, 1 - c)
                _remote(got, got, send_sems, recv_sems, k0 + 4 + j, sibling).wait_recv()
        for cp in first + passed:
            cp.wait_send()
        for cp in own:
            cp.wait()

    hbm = pl.BlockSpec(memory_space=pl.ANY)
    return pl.pallas_call(
        body, name=name,
        in_specs=[hbm] * n, out_specs=[hbm] * n,
        out_shape=[jax.ShapeDtypeStruct((N_DEV,) + s.shape, s.dtype) for s in shards],
        scratch_shapes=[pltpu.SemaphoreType.DMA((7 * n,)), pltpu.SemaphoreType.DMA((7 * n,)),
                        pltpu.SemaphoreType.DMA((n,))],
    )(*shards)


def exchange_sibling(parts, name):
    n = len(parts)

    def body(*refs):
        ins, outs = refs[:n], refs[n:2 * n]
        send_sems, recv_sems = refs[2 * n:]
        x, y, c = _position()
        sibling = (x, y, 1 - c)
        copies = []
        for a in range(n):
            for q in range(4):
                cp = _remote(ins[a].at[2 * q + (1 - c)], outs[a].at[q], send_sems, recv_sems, 4 * a + q, sibling)
                cp.start()
                copies.append(cp)
        for cp in copies:
            cp.wait_recv()
        for cp in copies:
            cp.wait_send()

    hbm = pl.BlockSpec(memory_space=pl.ANY)
    return pl.pallas_call(
        body, name=name,
        in_specs=[hbm] * n, out_specs=[hbm] * n,
        out_shape=[jax.ShapeDtypeStruct((4,) + p.shape[1:], p.dtype) for p in parts],
        scratch_shapes=[pltpu.SemaphoreType.DMA((4 * n,)), pltpu.SemaphoreType.DMA((4 * n,))],
    )(*parts)


def exchange_chips(sums, name):
    n = len(sums)

    def body(*refs):
        ins, outs = refs[:n], refs[n:2 * n]
        send_sems, recv_sems = refs[2 * n:]
        x, y, c = _position()
        copies = []
        for a in range(n):
            for j, (cx, cy) in enumerate(_other_chips(x, y)):
                cp = _remote(ins[a].at[2 * cx + cy], outs[a].at[j], send_sems, recv_sems, 3 * a + j, (cx, cy, c))
                cp.start()
                copies.append(cp)
        for cp in copies:
            cp.wait_recv()
        for cp in copies:
            cp.wait_send()

    hbm = pl.BlockSpec(memory_space=pl.ANY)
    return pl.pallas_call(
        body, name=name,
        in_specs=[hbm] * n, out_specs=[hbm] * n,
        out_shape=[jax.ShapeDtypeStruct((3,) + s.shape[1:], s.dtype) for s in sums],
        scratch_shapes=[pltpu.SemaphoreType.DMA((3 * n,)), pltpu.SemaphoreType.DMA((3 * n,))],
    )(*sums)


_HBM = pl.BlockSpec(memory_space=pltpu.HBM)
_SEM = pl.BlockSpec(memory_space=pltpu.SEMAPHORE)
_DATAFLOW = pltpu.SideEffectType.DATAFLOW_SIDE_EFFECTING


def _to_all_plan(srcs, lands, send_sems, recv_sems):
    x, y, c = _position()
    me = 4 * x + 2 * y + c
    copies = []
    for a in range(len(srcs)):
        for k in range(1, N_DEV):
            fx, fy, fc = (k >> 2) & 1, (k >> 1) & 1, k & 1
            to = (1 - x if fx else x, 1 - y if fy else y, 1 - c if fc else c)
            copies.append(_remote(srcs[a], lands[a].at[me], send_sems, recv_sems, (N_DEV - 1) * a + k - 1, to))
    return copies


def _to_chips_plan(srcs, lands, send_sems, recv_sems):
    x, y, c = _position()
    copies = []
    for a in range(len(srcs)):
        for j, (cx, cy) in enumerate(_other_chips(x, y)):
            copies.append(_remote(srcs[a].at[2 * cx + cy], lands[a].at[j], send_sems, recv_sems, 3 * a + j, (cx, cy, c)))
    return copies


def copies_start(srcs, land_shapes, plan, per_array, name):
    n = len(srcs)
    n_sem = per_array * n
    lands = [lax.empty(s.shape, s.dtype) for s in land_shapes]

    def body(*refs):
        src_refs, land_refs = refs[:n], refs[n:2 * n]
        send_sems, recv_sems = refs[2 * n], refs[2 * n + 1]
        token = refs[-1]
        for cp in plan(src_refs, land_refs, send_sems, recv_sems):
            cp.start()
        token[...] = jnp.zeros_like(token)

    out = pl.pallas_call(
        body, name=name,
        out_shape=(pltpu.SemaphoreType.DMA((n_sem,)), pltpu.SemaphoreType.DMA((n_sem,)))
        + tuple(pltpu.HBM(s.shape, s.dtype) for s in srcs)
        + tuple(pltpu.HBM(s.shape, s.dtype) for s in land_shapes)
        + (jax.ShapeDtypeStruct((8, LANES), F32),),
        in_specs=[_HBM] * (2 * n),
        out_specs=(_SEM, _SEM) + (_HBM,) * (2 * n) + (pl.BlockSpec(memory_space=pltpu.VMEM),),
        input_output_aliases={i: 2 + i for i in range(2 * n)},
        compiler_params=pltpu.CompilerParams(has_side_effects=_DATAFLOW),
    )(*[pltpu.with_memory_space_constraint(s, pltpu.HBM) for s in srcs],
      *[pltpu.with_memory_space_constraint(l, pltpu.HBM) for l in lands])
    return out[:-1], out[-1]


def copies_wait(handles, plan, after, name):
    send_sems, recv_sems = handles[0], handles[1]
    n = (len(handles) - 2) // 2
    thru = handles[2:]

    def body(*refs):
        src_refs, land_refs = refs[:n], refs[n:2 * n]
        send_sems, recv_sems = refs[2 * n], refs[2 * n + 1]
        copies = plan(src_refs, land_refs, send_sems, recv_sems)
        for cp in copies:
            cp.wait_recv()
        for cp in copies:
            cp.wait_send()

    out = pl.pallas_call(
        body, name=name,
        out_shape=tuple(pltpu.HBM(t.shape, t.dtype) for t in thru),
        in_specs=[_HBM] * (2 * n) + [_SEM, _SEM, pl.BlockSpec(memory_space=pl.ANY)],
        out_specs=(_HBM,) * (2 * n),
        input_output_aliases={i: i for i in range(2 * n)},
        compiler_params=pltpu.CompilerParams(has_side_effects=_DATAFLOW),
    )(*thru, send_sems, recv_sems, after)
    return out[n:]


def all_sum_small(vec, name):
    R = vec.shape[0]

    def body(v_ref, tot_ref, all_ref, send_sems, recv_sems):
        x, y, c = _position()
        me = 4 * x + 2 * y + c
        all_ref[me] = v_ref[...]
        copies = []
        for k in range(1, N_DEV):
            fx, fy, fc = (k >> 2) & 1, (k >> 1) & 1, k & 1
            to = (1 - x if fx else x, 1 - y if fy else y, 1 - c if fc else c)
            cp = _remote(v_ref, all_ref.at[me], send_sems, recv_sems, k - 1, to)
            cp.start()
            copies.append(cp)
        for cp in copies:
            cp.wait_recv()
        for cp in copies:
            cp.wait_send()
        tot = all_ref[0]
        for j in range(1, N_DEV):
            tot = tot + all_ref[j]
        tot_ref[...] = tot

    vmem = pl.BlockSpec(memory_space=pltpu.VMEM)
    return pl.pallas_call(
        body, name=name,
        in_specs=[vmem], out_specs=vmem,
        out_shape=jax.ShapeDtypeStruct((R, LANES), F32),
        scratch_shapes=[pltpu.VMEM((N_DEV, R, LANES), F32),
                        pltpu.SemaphoreType.DMA((N_DEV - 1,)), pltpu.SemaphoreType.DMA((N_DEV - 1,))],
        compiler_params=pltpu.CompilerParams(vmem_limit_bytes=VMEM_LIMIT),
    )(vec)


def pair_add(parts, theirs, place, name):
    _, R, C = theirs.shape
    tr = _pick(R, 256, 8)

    def body(place_ref, a_ref, b_ref, o_ref):
        o_ref[...] = (a_ref[...].astype(F32) + b_ref[...].astype(F32)).astype(BF16)

    blk = pl.BlockSpec((None, tr, C), lambda q, i, place_ref: (q, i, 0))
    return pl.pallas_call(
        body, name=name,
        grid_spec=pltpu.PrefetchScalarGridSpec(
            num_scalar_prefetch=1, grid=(4, R // tr),
            in_specs=[pl.BlockSpec((None, tr, C), lambda q, i, place_ref: (2 * q + place_ref[2], i, 0)), blk],
            out_specs=blk),
        out_shape=jax.ShapeDtypeStruct(theirs.shape, BF16),
        compiler_params=_params(("parallel", "parallel")),
    )(place, parts, theirs)


def _adamw_math(w, g, m, v):
    m = ADAM_B1 * m + (1.0 - ADAM_B1) * g
    v = ADAM_B2 * v + (1.0 - ADAM_B2) * jnp.square(g)
    m_hat = m / (1.0 - ADAM_B1 ** ADAM_STEP)
    v_hat = v / (1.0 - ADAM_B2 ** ADAM_STEP)
    delta = -ADAM_LR * (m_hat / (jnp.sqrt(v_hat) + ADAM_EPS) + ADAM_WD * w)
    return delta, m, v


def adamw_sharded(w, m, v, parts, sib, others, place, name):
    R, C = w.shape
    tr = _pick(R, 256, 8)

    def body(place_ref, w_ref, m_ref, v_ref, a_ref, b_ref, o_ref, g_ref, d_ref, nm_ref, nv_ref):
        g = a_ref[...].astype(F32) + b_ref[...].astype(F32)
        for j in range(3):
            g = g + o_ref[j].astype(F32)
        delta, nm, nv = _adamw_math(w_ref[...], g, m_ref[...], v_ref[...])
        g_ref[...] = g
        d_ref[...] = delta
        nm_ref[...] = nm
        nv_ref[...] = nv

    row = pl.BlockSpec((tr, C), lambda i, place_ref: (i, 0))
    return pl.pallas_call(
        body, name=name,
        grid_spec=pltpu.PrefetchScalarGridSpec(
            num_scalar_prefetch=1, grid=(R // tr,),
            in_specs=[row] * 3 + [pl.BlockSpec((None, tr, C), lambda i, place_ref: (place_ref[0], i, 0)),
                                  pl.BlockSpec((None, tr, C), lambda i, place_ref: (place_ref[1], i, 0)),
                                  pl.BlockSpec((3, tr, C), lambda i, place_ref: (0, i, 0))],
            out_specs=[row] * 4),
        out_shape=[jax.ShapeDtypeStruct((R, C), F32)] * 4,
        compiler_params=_params(("parallel",)),
    )(place, w, m, v, parts, sib, others)


def adamw_packed(w, g, m, v, name):
    R = w.shape[0]

    def body(w_ref, g_ref, m_ref, v_ref, d_ref, nm_ref, nv_ref):
        delta, nm, nv = _adamw_math(w_ref[...], g_ref[...], m_ref[...], v_ref[...])
        d_ref[...] = delta
        nm_ref[...] = nm
        nv_ref[...] = nv

    full = pl.BlockSpec((R, LANES), lambda i: (0, 0))
    return pl.pallas_call(
        body, name=name, grid=(1,),
        in_specs=[full] * 4, out_specs=[full] * 3,
        out_shape=[jax.ShapeDtypeStruct((R, LANES), F32)] * 3,
        compiler_params=_params(("arbitrary",)),
    )(w, g, m, v)


def _pack(arrays):
    flat = []
    sizes = []
    for a in arrays:
        f = a.reshape(-1).astype(F32)
        pad = (-f.shape[0]) % LANES
        if pad:
            f = jnp.concatenate([f, jnp.zeros((pad,), F32)])
        flat.append(f)
        sizes.append(f.shape[0])
    rows = sum(sizes) // LANES
    pad_rows = (-rows) % 8
    if pad_rows:
        flat.append(jnp.zeros((pad_rows * LANES,), F32))
    return jnp.concatenate(flat).reshape(-1, LANES), sizes


def _unpack(packed, sizes, shapes):
    flat = packed.reshape(-1)
    out = []
    off = 0
    for size, shape in zip(sizes, shapes):
        n = int(np.prod(shape))
        out.append(flat[off:off + n].reshape(shape))
        off += size
    return out


def _to_blocks(full, axis):
    if axis == 0:
        return full.reshape(N_DEV, full.shape[0] // N_DEV, full.shape[1])
    r, n = full.shape
    return full.reshape(r, N_DEV, n // N_DEV).transpose(1, 0, 2)


def _from_blocks(blocks, axis):
    if axis == 0:
        return blocks.reshape(blocks.shape[0] * blocks.shape[1], blocks.shape[2])
    return blocks.transpose(1, 0, 2).reshape(blocks.shape[1], blocks.shape[0] * blocks.shape[2])


def kernel(x, ln0_g, ln0_b, w_in, b_in, conv_w, w_a, w_b, w_o, b_o, ln1_g, ln1_b, w_up, b_up, ffn_conv_w, ffn_conv_b, w_down, b_down, ln2_g, ln2_b, loss_target, m_ln0_g, m_ln0_b, m_w_in, m_b_in, m_conv_w, m_w_a, m_w_b, m_w_o, m_b_o, m_ln1_g, m_ln1_b, m_w_up, m_b_up, m_ffn_conv_w, m_ffn_conv_b, m_w_down, m_b_down, m_ln2_g, m_ln2_b, v_ln0_g, v_ln0_b, v_w_in, v_b_in, v_conv_w, v_w_a, v_w_b, v_w_o, v_b_o, v_ln1_g, v_ln1_b, v_w_up, v_b_up, v_ffn_conv_w, v_ffn_conv_b, v_w_down, v_b_down, v_ln2_g, v_ln2_b):
    T, D = x.shape[1], x.shape[2]
    F = ffn_conv_b.shape[-1]
    xs = x.reshape(T, D)
    tgt = loss_target.reshape(T, D)
    dev = 4 * lax.axis_index("x") + 2 * lax.axis_index("y") + lax.axis_index("c")
    chip = 2 * lax.axis_index("x") + lax.axis_index("y")
    core = lax.axis_index("c")
    place = jnp.stack([dev, chip, core]).astype(jnp.int32)

    big = dict(w_in=(w_in[0], 1), w_a=(w_a[0], 0), w_b=(w_b[0], 1), w_o=(w_o[0], 0), w_up=(w_up[0], 1),
               w_down=(w_down[0], 0))
    names = list(big)
    shards = {k: big[k][0].astype(BF16) for k in names}
    g_in, g_conv, g_fcw = all_gather([shards["w_in"], conv_w[0], ffn_conv_w[0]], "gather_w_in")
    full = {"w_in": _from_blocks(g_in, 1)}
    conv_full = _from_blocks(g_conv, 1)
    fcw_full = _from_blocks(g_fcw, 1)
    late_groups = (("w_a", "w_b", "w_o"), ("w_up", "w_down"))
    late_handles = []
    token = None
    for n, keys in enumerate(late_groups):
        srcs = [shards[k] if token is None else shards[k] + token[0, 0].astype(BF16) for k in keys]
        handles, token = copies_start(srcs, [jax.ShapeDtypeStruct((N_DEV,) + s.shape, BF16) for s in srcs],
                                      _to_all_plan, N_DEV - 1, f"gather_late_{n}_start")
        late_handles.append(handles)

    def late_weights(n, after):
        lands = copies_wait(late_handles[n], _to_all_plan, after, f"gather_late_{n}_wait")
        for k, land in zip(late_groups[n], lands):
            full[k] = _from_blocks(lax.dynamic_update_index_in_dim(land, shards[k], dev, 0), big[k][1])

    o_q = 3 * D
    o_g = o_q + 3 * QKV_W
    w_pa, w_qkv, w_pg = full["w_in"][:, :o_q], full["w_in"][:, o_q:o_g], full["w_in"][:, o_g:]
    b_pa, b_qkv, b_pg = b_in[:, :o_q], b_in[:, o_q:o_g], b_in[:, o_g:]
    ln0g, ln0b = ln0_g.reshape(1, D), ln0_b.reshape(1, D)

    h0, h0b, *h0_res = ln_fwd(xs, None, ln0g, ln0b, "ln0_fwd", dilations=DILATIONS[1:])
    h0_res = [h0b] + [h.reshape(T, D) for h in h0_res]
    proj_a = mm_nn(h0b, w_pa, b_pa, ACT, "proj_conv", after=token)
    proj_g = mm_nn(h0b, w_pg, b_pg, ACT, "proj_gates")
    zero_d = jnp.zeros((1, D), F32)
    s_a = conv_a_fwd(proj_a, conv_full, "conv_a_fwd")
    late_weights(0, s_a)
    y_a = mm_nn(s_a, full["w_a"], zero_d, ACT, "branch_a_out")

    def group_cols(m, g):
        return jnp.concatenate([m[:, s * QKV_W + g * GROUP_W:s * QKV_W + (g + 1) * GROUP_W] for s in range(3)], 1)

    w_grp = [group_cols(w_qkv, g) for g in range(3)]
    qkvs, outs, lses = [], [], []
    for g, d in enumerate(DILATIONS):
        qkv = mm_nn(h0_res[g], w_grp[g], group_cols(b_qkv, g), BF16, f"proj_qkv_{g}").reshape(d, T // d, 3 * GROUP_W)
        o, l = att_fwd(qkv, g, f"att_fwd_{g}")
        qkvs.append(qkv)
        outs.append(o)
        lses.append(l)
    comb = combine_fwd(outs, lses, "combine_fwd")
    y_b = mm_nn(comb, full["w_b"], zero_d, ACT, "branch_b_out")
    z = gate_fwd(proj_g, y_a, y_b, "gate_fwd")
    mix = mm_nn(z, full["w_o"], b_o, F32, "mix_out")
    h1, h1b = ln_fwd(h0, mix, ln1_g, ln1_b, "ln1_fwd")
    late_weights(1, h1b)
    up = mm_nn(h1b, full["w_up"], b_up, F32, "ffn_up")
    f_act = conv_f_fwd(up, fcw_full, ffn_conv_b, "conv_f_fwd")
    ffn = mm_nn(f_act, full["w_down"], b_down, F32, "ffn_down")

    dr2, dr2b, d_ln2_g, d_ln2_b, d_b_down, loss_part = ln_bwd(h1, ffn, ln2_g, ln2_b, None, None, tgt, "ln2_loss_bwd")
    dw_down, _ = mm_tn(f_act, dr2b, "dw_down")
    df = mm_nt(dr2b, full["w_down"], None, "d_ffn_act")
    d_a, d_gate, cs_a, cs_gate, d_fcb, d_fcw = conv_f_bwd(df, up, fcw_full, ffn_conv_b, "conv_f_bwd")
    dw_up_a, _ = mm_tn(h1b, d_a, "dw_up_a")
    dw_up_g, _ = mm_tn(h1b, d_gate, "dw_up_gate")
    dh1 = mm_nt([d_a, d_gate], full["w_up"], None, "d_h1")
    dr1, dr1b, d_ln1_g, d_ln1_b, d_b_o, _ = ln_bwd(h0, mix, ln1_g, ln1_b, dr2, dh1, None, "ln1_bwd")
    dw_o, _ = mm_tn(z, dr1b, "dw_o")
    dz = mm_nt(dr1b, full["w_o"], None, "d_z", out_dtype=ACT)
    dy_a, dy_b, dproj_g = gate_bwd(dz, proj_g, y_a, y_b, "gate_bwd")
    dw_a, _ = mm_tn(s_a, dy_a, "dw_a")
    ds_a = mm_nt(dy_a, full["w_a"], None, "d_s_a", out_dtype=ACT)
    dproj_a, d_conv = conv_a_bwd(ds_a, proj_a, conv_full, "conv_a_bwd")
    dw_b, _ = mm_tn(comb, dy_b, "dw_b")

    rs_mine, rs_sib, rs_handles = {}, {}, {}

    def reduce_start(keys, grads, tag):
        parts = [_to_blocks(grads[k], big[k][1]) for k in keys]
        from_sib = exchange_sibling(parts, f"grads_to_sibling_{tag}")
        sums = [pair_add(a, b, place, f"chip_sum_{k}") for k, a, b in zip(keys, parts, from_sib)]
        handles, tok = copies_start(sums, [jax.ShapeDtypeStruct((3,) + s.shape[1:], BF16) for s in sums],
                                    _to_chips_plan, 3, f"grads_to_chips_{tag}_start")
        for k, a, b in zip(keys, parts, from_sib):
            rs_mine[k], rs_sib[k] = a, b
        rs_handles[tag] = (keys, handles)
        return tok

    tok_a = reduce_start(("w_a", "w_b", "w_o", "w_up", "w_down"),
                         dict(w_a=dw_a, w_b=dw_b, w_o=dw_o, w_up=jnp.concatenate([dw_up_a, dw_up_g], 1), w_down=dw_down),
                         "a")
    dcomb = mm_nt(dy_b, full["w_b"], None, "d_comb", after=tok_a, out_dtype=ACT)
    dos, dms = combine_bwd(dcomb, outs, lses, "combine_bwd")
    dw_grp, cs_grp, dqkvs = [], [], []
    for g, d in enumerate(DILATIONS):
        dq, dk, dv = att_bwd(qkvs[g], dos[g], lses[g], dms[g], g, f"att_bwd_{g}")
        dqkv = [t.reshape(T, GROUP_W) for t in (dq, dk, dv)]
        dwg, csg = mm_tn(h0_res[g], dqkv, f"dw_in_qkv_{g}")
        dqkvs.append(dqkv)
        dw_grp.append(dwg)
        cs_grp.append(csg)
    dw_pa, cs_pa = mm_tn(h0b, dproj_a, "dw_in_conv")
    dw_pg, cs_pg = mm_tn(h0b, dproj_g, "dw_in_gates")

    def ungroup(parts):
        return jnp.concatenate([p[:, s * GROUP_W:(s + 1) * GROUP_W] for s in range(3) for p in parts], 1)

    db_in_parts = [cs_pa, ungroup(cs_grp), cs_pg]
    tok_b = reduce_start(("w_in",), dict(w_in=jnp.concatenate([dw_pa, ungroup(dw_grp), dw_pg], 1)), "b")
    dh0 = mm_nt(dproj_a, w_pa, None, "d_h0_conv", after=tok_b)
    dh0 = mm_nt(dproj_g, w_pg, dh0, "d_h0_gates")
    dh0 = mm_nt(dqkvs[0], w_grp[0], dh0, "d_h0_qkv_0")
    dh0_res = [(mm_nt(dqkvs[g], w_grp[g], None, f"d_h0_qkv_{g}").reshape(d, T // d, D), d)
               for g, d in enumerate(DILATIONS) if g > 0]
    dx, _, d_ln0_g, d_ln0_b, _, _ = ln_bwd(xs, None, ln0g, ln0b, dr1, dh0, None, "ln0_bwd", by_residue=dh0_res)

    small = [d_ln0_g, d_ln0_b, jnp.concatenate(db_in_parts, 1), d_conv, d_b_o, d_ln1_g, d_ln1_b,
             jnp.concatenate([cs_a, cs_gate], 1), d_fcw, d_fcb, d_b_down, d_ln2_g, d_ln2_b, loss_part]
    packed, sizes = _pack(small)
    total = all_sum_small(packed, "sum_small")
    (g_ln0_g, g_ln0_b, g_b_in, g_conv_full, g_b_o, g_ln1_g, g_ln1_b, g_b_up, g_fcw_full, g_fcb, g_b_down, g_ln2_g,
     g_ln2_b, loss) = _unpack(total, sizes, [a.shape for a in small])
    cw = conv_w.shape[-1]
    fw = ffn_conv_w.shape[-1]
    g_conv = lax.dynamic_slice_in_dim(g_conv_full, dev * cw, cw, 1)
    g_fcw = lax.dynamic_slice_in_dim(g_fcw_full, dev * fw, fw, 1)

    from_chips = {}
    for tag, (keys, handles) in rs_handles.items():
        lands = copies_wait(handles, _to_chips_plan, total, f"grads_to_chips_{tag}_wait")
        from_chips.update(zip(keys, lands))

    moments = dict(w_in=(m_w_in, v_w_in), w_a=(m_w_a, v_w_a), w_b=(m_w_b, v_w_b), w_o=(m_w_o, v_w_o),
                   w_up=(m_w_up, v_w_up), w_down=(m_w_down, v_w_down))
    res_big = {}
    for k in names:
        res_big[k] = adamw_sharded(big[k][0], moments[k][0][0], moments[k][1][0], rs_mine[k], rs_sib[k], from_chips[k],
                                   place, f"adamw_{k}")

    small_names = ["ln0_g", "ln0_b", "b_in", "conv_w", "b_o", "ln1_g", "ln1_b", "b_up", "ffn_conv_w", "ffn_conv_b",
                   "b_down", "ln2_g", "ln2_b"]
    small_w = [ln0_g, ln0_b, b_in, conv_w, b_o, ln1_g, ln1_b, b_up, ffn_conv_w, ffn_conv_b, b_down, ln2_g, ln2_b]
    small_m = [m_ln0_g, m_ln0_b, m_b_in, m_conv_w, m_b_o, m_ln1_g, m_ln1_b, m_b_up, m_ffn_conv_w, m_ffn_conv_b,
               m_b_down, m_ln2_g, m_ln2_b]
    small_v = [v_ln0_g, v_ln0_b, v_b_in, v_conv_w, v_b_o, v_ln1_g, v_ln1_b, v_b_up, v_ffn_conv_w, v_ffn_conv_b,
               v_b_down, v_ln2_g, v_ln2_b]
    small_g = [g_ln0_g, g_ln0_b, g_b_in, g_conv, g_b_o, g_ln1_g, g_ln1_b, g_b_up, g_fcw, g_fcb, g_b_down, g_ln2_g,
               g_ln2_b]
    shapes = [w.shape for w in small_w]
    small_g = [g.reshape(s) for g, s in zip(small_g, shapes)]
    pw, psz = _pack(small_w)
    pg, _ = _pack(small_g)
    pm, _ = _pack(small_m)
    pv, _ = _pack(small_v)
    pd, pnm, pnv = adamw_packed(pw, pg, pm, pv, "adamw_small")
    res_small = {k: (g, d_, m_, v_) for k, g, d_, m_, v_ in zip(
        small_names, small_g, _unpack(pd, psz, shapes), _unpack(pnm, psz, shapes), _unpack(pnv, psz, shapes))}

    order = ["ln0_g", "ln0_b", "w_in", "b_in", "conv_w", "w_a", "w_b", "w_o", "b_o", "ln1_g", "ln1_b", "w_up", "b_up",
             "ffn_conv_w", "ffn_conv_b", "w_down", "b_down", "ln2_g", "ln2_b"]

    def result(k, j):
        if k in res_big:
            return res_big[k][j][None]
        return res_small[k][j]

    out = [loss.reshape(()), dx.reshape(x.shape)]
    for j in range(4):
        out += [result(k, j) for k in order]
    return tuple(out)
```

```python
import functools
import math

import numpy as np
import jax
import jax.numpy as jnp
from jax import lax
from jax.experimental import pallas as pl
from jax.experimental.pallas import tpu as pltpu

F32 = jnp.float32
BF16 = jnp.bfloat16
ACT = BF16

N_DEV = 8
LN_EPS = 1e-5
ALPHA = (2.0 * 1) ** 0.25
MASK_VALUE = -1e30
HEAD_DIM = 64
GROUP_W = 512
QKV_W = 3 * GROUP_W
DILATIONS = (1, 4, 16)
RADIUS = 64
LANES = 128
HALO = 8
HALO_BF16 = 16
ATT_TQ = 128

ADAM_LR = 0.001
ADAM_B1 = 0.9
ADAM_B2 = 0.999
ADAM_EPS = 1e-08
ADAM_WD = 0.01
ADAM_STEP = 10

VMEM_LIMIT = 52 * 1024 * 1024
OUT_TILE_BYTES = 8 * 1024 * 1024
MESH = pl.DeviceIdType.MESH
NT_DIMS = (((1,), (1,)), ((), ()))
TN_DIMS = (((0,), (0,)), ((), ()))


def _pick(n, target, align=LANES):
    if n <= target:
        return n
    best = None
    for t in range(align, target + 1, align):
        if n % t == 0:
            best = t
    assert best is not None, (n, target, align)
    return best


def _params(sems=None):
    return pltpu.CompilerParams(dimension_semantics=sems, vmem_limit_bytes=VMEM_LIMIT)


def _alibi_slopes():
    n = 3 * 8
    return np.exp2(-8.0 * np.arange(1, n + 1, dtype=np.float64) / n).astype(np.float32).reshape(3, 8)


def _ln_stats(r):
    mu = jnp.mean(r, -1, keepdims=True)
    xc = r - mu
    var = jnp.mean(xc * xc, -1, keepdims=True)
    rstd = lax.rsqrt(var + LN_EPS)
    return xc, rstd


def _load_natural(ref, d, scr):
    if d == 1:
        return ref[0]
    n, C = ref.shape[1], ref.shape[2]
    for c in range(C // LANES):
        for r in range(d):
            scr[c, pl.ds(r, n, stride=d), :] = ref[r, :, c * LANES:(c + 1) * LANES]
    return jnp.concatenate([scr[c] for c in range(C // LANES)], axis=1)


def _store_by_residue(val, ref, d, scr):
    if d == 1:
        ref[0] = val.astype(ref.dtype)
        return
    n, C = ref.shape[1], ref.shape[2]
    for c in range(C // LANES):
        scr[c] = val[:, c * LANES:(c + 1) * LANES]
    for c in range(C // LANES):
        for r in range(d):
            ref[r, :, c * LANES:(c + 1) * LANES] = scr[c, pl.ds(r, n, stride=d), :].astype(ref.dtype)


def _residue_spec(tm, d, C):
    return pl.BlockSpec((d, tm // d, C), lambda i: (0, i, 0))


def _residue_scratch(tm, C):
    return pltpu.VMEM((C // LANES, tm, LANES), F32)


def ln_fwd(a, res, g, b, name, dilations=(), gather=()):
    T, D = a.shape
    tm = _pick(T, 512, 8)
    has_res = res is not None
    nd = len(dilations)
    ng = len(gather)
    n_in = (2 if has_res else 1) + 2
    last = T // tm - 1

    def body(*refs):
        a_ref = refs[0]
        r = a_ref[...]
        if has_res:
            r = ALPHA * r + refs[1][...]
        g_ref, b_ref = refs[n_in - 2], refs[n_in - 1]
        shard_refs = refs[n_in:n_in + ng]
        h_ref, hb_ref = refs[n_in + ng], refs[n_in + ng + 1]
        p_refs = refs[n_in + ng + 2:n_in + ng + 2 + nd]
        full_refs = refs[n_in + ng + 2 + nd:n_in + 2 * ng + 2 + nd]
        scratch = refs[n_in + 2 * ng + 2 + nd:]
        sems = scratch[len(scratch) - 3:] if ng else ()

        if ng:
            @pl.when(pl.program_id(0) == 0)
            def _():
                _gather_begin(shard_refs, full_refs, *sems)

        xc, rstd = _ln_stats(r)
        h = xc * rstd * g_ref[...] + b_ref[...]
        h_ref[...] = h
        hb_ref[...] = h.astype(BF16)
        for d, p_ref in zip(dilations, p_refs):
            _store_by_residue(h, p_ref, d, scratch[0])

        if ng:
            @pl.when(pl.program_id(0) == last)
            def _():
                _gather_finish(shard_refs, full_refs, *sems)

    row = pl.BlockSpec((tm, D), lambda i: (i, 0))
    vec = pl.BlockSpec((1, D), lambda i: (0, 0))
    hbm = pl.BlockSpec(memory_space=pl.ANY)
    ins = [a] + ([res] if has_res else []) + [g, b] + list(gather)
    return pl.pallas_call(
        body, name=name, grid=(T // tm,),
        in_specs=[row] * (2 if has_res else 1) + [vec, vec] + [hbm] * ng,
        out_specs=[row, row] + [_residue_spec(tm, d, D) for d in dilations] + [hbm] * ng,
        out_shape=[jax.ShapeDtypeStruct((T, D), F32), jax.ShapeDtypeStruct((T, D), BF16)]
        + [jax.ShapeDtypeStruct((d, T // d, D), BF16) for d in dilations]
        + [jax.ShapeDtypeStruct((N_DEV,) + s.shape, s.dtype) for s in gather],
        scratch_shapes=([_residue_scratch(tm, D)] if nd else []) + (_gather_scratch(ng) if ng else []),
        compiler_params=_params(("arbitrary",) if ng else ("parallel",)),
    )(*ins)


def ln_bwd(a, res, g, b, d1, d2, tgt, name, by_residue=()):
    T, D = a.shape
    tm = _pick(T, 256, 8)
    has_res = res is not None
    loss_mode = tgt is not None
    nres = len(by_residue)

    def body(*refs):
        refs = list(refs)
        a_ref = refs.pop(0)
        r_ref = refs.pop(0) if has_res else None
        g_ref = refs.pop(0)
        b_ref = refs.pop(0)
        if loss_mode:
            t_ref = refs.pop(0)
        else:
            d1_ref = refs.pop(0)
            d2_ref = refs.pop(0)
        e_refs = [refs.pop(0) for _ in range(nres)]
        dr_ref, drb_ref, dg_ref, db_ref, ds_ref, loss_ref = refs[:6]
        i = pl.program_id(0)

        @pl.when(i == 0)
        def _():
            dg_ref[...] = jnp.zeros_like(dg_ref)
            db_ref[...] = jnp.zeros_like(db_ref)
            ds_ref[...] = jnp.zeros_like(ds_ref)
            loss_ref[...] = jnp.zeros_like(loss_ref)

        r = a_ref[...]
        if has_res:
            r = ALPHA * r + r_ref[...]
        xc, rstd = _ln_stats(r)
        xhat = xc * rstd
        gam = g_ref[...]
        if loss_mode:
            err = xhat * gam + b_ref[...] - t_ref[...]
            dy = err * (1.0 / D)
            row_loss = jnp.mean(err * err, -1, keepdims=True)
            loss_ref[...] += 0.5 * jnp.sum(row_loss, 0, keepdims=True)
        else:
            dy = ALPHA * d1_ref[...] + d2_ref[...]
        for (_, d), e_ref in zip(by_residue, e_refs):
            dy = dy + _load_natural(e_ref, d, refs[-1])
        dyg = dy * gam
        c1 = jnp.mean(dyg, -1, keepdims=True)
        c2 = jnp.mean(dyg * xhat, -1, keepdims=True)
        dr = rstd * (dyg - c1 - xhat * c2)
        dr_ref[...] = dr
        drb_ref[...] = dr.astype(BF16)
        dg_ref[...] += jnp.sum(dy * xhat, 0, keepdims=True)
        db_ref[...] += jnp.sum(dy, 0, keepdims=True)
        ds_ref[...] += jnp.sum(dr, 0, keepdims=True)

    row = pl.BlockSpec((tm, D), lambda i: (i, 0))
    vec = pl.BlockSpec((1, D), lambda i: (0, 0))
    one = pl.BlockSpec((1, 1), lambda i: (0, 0))
    ins = [a] + ([res] if has_res else []) + [g, b] + ([tgt] if loss_mode else [d1, d2]) + [e for e, _ in by_residue]
    in_specs = [row] * (2 if has_res else 1) + [vec, vec] + [row] * (1 if loss_mode else 2)
    in_specs += [_residue_spec(tm, d, D) for _, d in by_residue]
    return pl.pallas_call(
        body, name=name, grid=(T // tm,),
        in_specs=in_specs,
        out_specs=[row, row, vec, vec, vec, one],
        out_shape=[jax.ShapeDtypeStruct((T, D), F32), jax.ShapeDtypeStruct((T, D), BF16),
                   jax.ShapeDtypeStruct((1, D), F32), jax.ShapeDtypeStruct((1, D), F32),
                   jax.ShapeDtypeStruct((1, D), F32), jax.ShapeDtypeStruct((1, 1), F32)],
        scratch_shapes=[_residue_scratch(tm, D)] if nres else [],
        compiler_params=_params(("arbitrary",)),
    )(*ins)


_TOKEN_SPEC = pl.BlockSpec((8, LANES), lambda i: (0, 0))


def mm_nn(a, w, bias, out_dtype, name, after=None):
    M, K = a.shape
    N = w.shape[1]
    tm = _pick(M, max(256, min(1024, OUT_TILE_BYTES // (N * jnp.dtype(out_dtype).itemsize))), 8)
    tc = _pick(N, 512)

    def body(a_ref, w_ref, b_ref, *rest):
        o_ref = rest[-1]
        av = a_ref[...]
        for j in range(N // tc):
            cols = slice(j * tc, (j + 1) * tc)
            acc = jnp.dot(av, w_ref[:, cols], preferred_element_type=F32)
            o_ref[:, cols] = (acc + b_ref[:, cols]).astype(out_dtype)

    return pl.pallas_call(
        body, name=name, grid=(M // tm,),
        in_specs=[pl.BlockSpec((tm, K), lambda i: (i, 0)),
                  pl.BlockSpec((K, N), lambda i: (0, 0)),
                  pl.BlockSpec((1, N), lambda i: (0, 0))] + ([] if after is None else [_TOKEN_SPEC]),
        out_specs=pl.BlockSpec((tm, N), lambda i: (i, 0)),
        out_shape=jax.ShapeDtypeStruct((M, N), out_dtype),
        compiler_params=_params(("parallel",)),
    )(a, w, bias, *([] if after is None else [after]))


def mm_nt(a, w, acc_in, name, after=None, w_block=0, out_dtype=F32):
    pieces = list(a) if isinstance(a, (list, tuple)) else [a]
    M = pieces[0].shape[0]
    widths = [p.shape[1] for p in pieces]
    K = sum(widths)
    N = w.shape[0]
    tm = _pick(M, 512, 8)
    tc = _pick(N, 512)
    has_acc = acc_in is not None
    n_a = len(pieces)

    def body(*refs):
        a_refs, w_ref = refs[:n_a], refs[n_a]
        c_ref = refs[n_a + 1] if has_acc else None
        o_ref = refs[-1]
        av = a_refs[0][...] if n_a == 1 else jnp.concatenate([r[...] for r in a_refs], axis=1)
        for j in range(N // tc):
            cols = slice(j * tc, (j + 1) * tc)
            acc = lax.dot_general(av, w_ref[cols, :], NT_DIMS, preferred_element_type=F32)
            if has_acc:
                acc = acc + c_ref[:, cols]
            o_ref[:, cols] = acc.astype(out_dtype)

    out_spec = pl.BlockSpec((tm, N), lambda i: (i, 0))
    in_specs = [pl.BlockSpec((tm, kw), lambda i: (i, 0)) for kw in widths]
    in_specs.append(pl.BlockSpec((N, K), lambda i: (0, w_block)))
    ins = pieces + [w]
    if has_acc:
        in_specs.append(out_spec)
        ins.append(acc_in)
    if after is not None:
        in_specs.append(_TOKEN_SPEC)
        ins.append(after)
    return pl.pallas_call(
        body, name=name, grid=(M // tm,),
        in_specs=in_specs, out_specs=out_spec,
        out_shape=jax.ShapeDtypeStruct((M, N), out_dtype),
        compiler_params=_params(("parallel",)),
    )(*ins)


def mm_tn(a, b, name, out_dtype=BF16):
    pieces = list(b) if isinstance(b, (list, tuple)) else [b]
    T, M = a.shape
    widths = [p.shape[1] for p in pieces]
    N = sum(widths)
    tk = _pick(T, 512, 8)
    nk = T // tk
    tc = _pick(M, 256)
    n_b = len(pieces)

    def body(*refs):
        a_ref, b_refs = refs[0], refs[1:1 + n_b]
        o_ref, cs_ref, acc_ref = refs[1 + n_b:]
        k = pl.program_id(0)

        @pl.when(k == 0)
        def _():
            acc_ref[...] = jnp.zeros_like(acc_ref)
            cs_ref[...] = jnp.zeros_like(cs_ref)

        bv = b_refs[0][...] if n_b == 1 else jnp.concatenate([r[...] for r in b_refs], axis=1)
        cs_ref[...] += jnp.sum(bv.astype(F32), 0, keepdims=True)
        for mi in range(M // tc):
            rows = slice(mi * tc, (mi + 1) * tc)
            acc_ref[rows, :] += lax.dot_general(a_ref[:, rows], bv, TN_DIMS, preferred_element_type=F32)

        @pl.when(k == nk - 1)
        def _():
            o_ref[...] = acc_ref[...].astype(out_dtype)

    return pl.pallas_call(
        body, name=name, grid=(nk,),
        in_specs=[pl.BlockSpec((tk, M), lambda k: (k, 0))] + [pl.BlockSpec((tk, wd), lambda k: (k, 0)) for wd in widths],
        out_specs=[pl.BlockSpec((M, N), lambda k: (0, 0)), pl.BlockSpec((1, N), lambda k: (0, 0))],
        out_shape=[jax.ShapeDtypeStruct((M, N), out_dtype), jax.ShapeDtypeStruct((1, N), F32)],
        scratch_shapes=[pltpu.VMEM((M, N), F32)],
        compiler_params=_params(("arbitrary",)),
    )(a, *pieces)


def _ext_rows(prev_ref, main_ref, next_ref, i, tm, T):
    before = jnp.where(i == 0, 0.0, prev_ref[...])
    after = jnp.where(i == T // tm - 1, 0.0, next_ref[...])
    return jnp.concatenate([before, main_ref[...], after], axis=0).astype(F32)


def _prev_row(x):
    return pltpu.roll(x, 1, 0)


def _next_row(x):
    return pltpu.roll(x, x.shape[0] - 1, 0)


def _conv3(u, w_ref):
    return _prev_row(u) * w_ref[0:1, :] + u * w_ref[1:2, :] + _next_row(u) * w_ref[2:3, :]


def _main(x, tm, halo=HALO):
    return x[halo:halo + tm]


def _halo_specs(tm, tc, T, col, order, halo=HALO):
    r = tm // halo
    last = T // halo - 1
    if order == "ij":
        return (pl.BlockSpec((halo, tc), lambda i, j: (jnp.maximum(i * r - 1, 0), col(j))),
                pl.BlockSpec((tm, tc), lambda i, j: (i, col(j))),
                pl.BlockSpec((halo, tc), lambda i, j: (jnp.minimum((i + 1) * r, last), col(j))))
    return (pl.BlockSpec((halo, tc), lambda j, i: (jnp.maximum(i * r - 1, 0), col(j))),
            pl.BlockSpec((tm, tc), lambda j, i: (i, col(j))),
            pl.BlockSpec((halo, tc), lambda j, i: (jnp.minimum((i + 1) * r, last), col(j))))


def conv_a_fwd(proj_a, conv_w, name):
    T, D3 = proj_a.shape
    D = D3 // 3
    tm = _pick(T, 256, 8)

    def body(p_ref, m_ref, n_ref, w_ref, o_ref):
        i = pl.program_id(0)
        ext = _ext_rows(p_ref, m_ref, n_ref, i, tm, T)
        u = ext[:, D:2 * D] * ext[:, 2 * D:]
        cu = _conv3(u, w_ref)
        o_ref[...] = (m_ref[:, :D].astype(F32) * _main(cu, tm, HALO_BF16)).astype(BF16)

    prev, main, nxt = _halo_specs(tm, D3, T, lambda j: 0, "ij", HALO_BF16)
    return pl.pallas_call(
        body, name=name, grid=(T // tm, 1),
        in_specs=[prev, main, nxt, pl.BlockSpec((3, D), lambda i, j: (0, 0))],
        out_specs=pl.BlockSpec((tm, D), lambda i, j: (i, 0)),
        out_shape=jax.ShapeDtypeStruct((T, D), BF16),
        compiler_params=_params(("parallel", "arbitrary")),
    )(proj_a, proj_a, proj_a, conv_w)


def conv_a_bwd(ds_a, proj_a, conv_w, name):
    T, D3 = proj_a.shape
    D = D3 // 3
    tm = _pick(T, 256, 8)

    def body(dp_ref, dm_ref, dn_ref, p_ref, m_ref, n_ref, w_ref, o_ref, dw_ref):
        i = pl.program_id(0)

        @pl.when(i == 0)
        def _():
            dw_ref[...] = jnp.zeros_like(dw_ref)

        ext = _ext_rows(p_ref, m_ref, n_ref, i, tm, T)
        dsa = _ext_rows(dp_ref, dm_ref, dn_ref, i, tm, T)
        gb, gc, hin = ext[:, :D], ext[:, D:2 * D], ext[:, 2 * D:]
        u = gc * hin
        u_prev, u_next = _prev_row(u), _next_row(u)
        cu = u_prev * w_ref[0:1, :] + u * w_ref[1:2, :] + u_next * w_ref[2:3, :]
        dcu = dsa * gb
        du = _next_row(dcu) * w_ref[0:1, :] + dcu * w_ref[1:2, :] + _prev_row(dcu) * w_ref[2:3, :]
        h = HALO_BF16
        o_ref[:, :D] = _main(dsa * cu, tm, h).astype(BF16)
        o_ref[:, D:2 * D] = _main(du * hin, tm, h).astype(BF16)
        o_ref[:, 2 * D:] = _main(du * gc, tm, h).astype(BF16)
        dcu_m = _main(dcu, tm, h)
        dw_ref[0:1, :] += jnp.sum(dcu_m * _main(u_prev, tm, h), 0, keepdims=True)
        dw_ref[1:2, :] += jnp.sum(dcu_m * _main(u, tm, h), 0, keepdims=True)
        dw_ref[2:3, :] += jnp.sum(dcu_m * _main(u_next, tm, h), 0, keepdims=True)

    dprev, dmain, dnxt = _halo_specs(tm, D, T, lambda j: 0, "ij", HALO_BF16)
    prev, main, nxt = _halo_specs(tm, D3, T, lambda j: 0, "ij", HALO_BF16)
    return pl.pallas_call(
        body, name=name, grid=(T // tm, 1),
        in_specs=[dprev, dmain, dnxt, prev, main, nxt, pl.BlockSpec((3, D), lambda i, j: (0, 0))],
        out_specs=[pl.BlockSpec((tm, D3), lambda i, j: (i, 0)), pl.BlockSpec((3, D), lambda i, j: (0, 0))],
        out_shape=[jax.ShapeDtypeStruct((T, D3), BF16), jax.ShapeDtypeStruct((3, D), F32)],
        compiler_params=_params(("arbitrary", "arbitrary")),
    )(ds_a, ds_a, ds_a, proj_a, proj_a, proj_a, conv_w)


_INV_SQRT2 = 1.0 / math.sqrt(2.0)
_INV_SQRT_2PI = 1.0 / math.sqrt(2.0 * math.pi)


def conv_f_fwd(up, fcw, fcb, name):
    T, F2 = up.shape
    F = F2 // 2
    tm = _pick(T, 256, 8)
    tc = _pick(F, 1408)
    nc = F // tc

    def body(p_ref, m_ref, n_ref, g_ref, w_ref, b_ref, o_ref):
        i = pl.program_id(0)
        a = _ext_rows(p_ref, m_ref, n_ref, i, tm, T)
        ca = _main(_conv3(a, w_ref), tm) + b_ref[...]
        gl = 0.5 * ca * (1.0 + lax.erf(ca * _INV_SQRT2))
        o_ref[...] = (gl * g_ref[...]).astype(BF16)

    prev, main, nxt = _halo_specs(tm, tc, T, lambda j: j, "ij")
    return pl.pallas_call(
        body, name=name, grid=(T // tm, nc),
        in_specs=[prev, main, nxt,
                  pl.BlockSpec((tm, tc), lambda i, j: (i, nc + j)),
                  pl.BlockSpec((3, tc), lambda i, j: (0, j)),
                  pl.BlockSpec((1, tc), lambda i, j: (0, j))],
        out_specs=pl.BlockSpec((tm, tc), lambda i, j: (i, j)),
        out_shape=jax.ShapeDtypeStruct((T, F), BF16),
        compiler_params=_params(("parallel", "parallel")),
    )(up, up, up, up, fcw, fcb)


def conv_f_bwd(df, up, fcw, fcb, name):
    T, F2 = up.shape
    F = F2 // 2
    tm = _pick(T, 256, 8)
    tc = _pick(F, 1408)
    nc = F // tc

    def body(fp_ref, fm_ref, fn_ref, ap_ref, am_ref, an_ref, gp_ref, gm_ref, gn_ref, w_ref, b_ref,
             da_ref, dg_ref, csa_ref, csg_ref, dfb_ref, dfw_ref):
        i = pl.program_id(1)

        @pl.when(i == 0)
        def _():
            csa_ref[...] = jnp.zeros_like(csa_ref)
            csg_ref[...] = jnp.zeros_like(csg_ref)
            dfb_ref[...] = jnp.zeros_like(dfb_ref)
            dfw_ref[...] = jnp.zeros_like(dfw_ref)

        dfe = _ext_rows(fp_ref, fm_ref, fn_ref, i, tm, T)
        a = _ext_rows(ap_ref, am_ref, an_ref, i, tm, T)
        gate = _ext_rows(gp_ref, gm_ref, gn_ref, i, tm, T)
        a_prev, a_next = _prev_row(a), _next_row(a)
        ca = a_prev * w_ref[0:1, :] + a * w_ref[1:2, :] + a_next * w_ref[2:3, :] + b_ref[...]
        cdf = 0.5 * (1.0 + lax.erf(ca * _INV_SQRT2))
        gl = ca * cdf
        gp = cdf + ca * (jnp.exp(-0.5 * ca * ca) * _INV_SQRT_2PI)
        dgate = _main(dfe * gl, tm)
        dca = dfe * gate * gp
        da = _main(_next_row(dca) * w_ref[0:1, :] + dca * w_ref[1:2, :] + _prev_row(dca) * w_ref[2:3, :], tm)
        da_ref[...] = da.astype(BF16)
        dg_ref[...] = dgate.astype(BF16)
        csa_ref[...] += jnp.sum(da, 0, keepdims=True)
        csg_ref[...] += jnp.sum(dgate, 0, keepdims=True)
        dca_m = _main(dca, tm)
        dfb_ref[...] += jnp.sum(dca_m, 0, keepdims=True)
        dfw_ref[0:1, :] += jnp.sum(dca_m * _main(a_prev, tm), 0, keepdims=True)
        dfw_ref[1:2, :] += jnp.sum(dca_m * _main(a, tm), 0, keepdims=True)
        dfw_ref[2:3, :] += jnp.sum(dca_m * _main(a_next, tm), 0, keepdims=True)

    fprev, fmain, fnxt = _halo_specs(tm, tc, T, lambda j: j, "ji")
    gprev, gmain, gnxt = _halo_specs(tm, tc, T, lambda j: nc + j, "ji")
    tile = pl.BlockSpec((tm, tc), lambda j, i: (i, j))
    vec = pl.BlockSpec((1, tc), lambda j, i: (0, j))
    vec3 = pl.BlockSpec((3, tc), lambda j, i: (0, j))
    return pl.pallas_call(
        body, name=name, grid=(nc, T // tm),
        in_specs=[fprev, fmain, fnxt, fprev, fmain, fnxt, gprev, gmain, gnxt, vec3, vec],
        out_specs=[tile, tile, vec, vec, vec, vec3],
        out_shape=[jax.ShapeDtypeStruct((T, F), BF16), jax.ShapeDtypeStruct((T, F), BF16),
                   jax.ShapeDtypeStruct((1, F), F32), jax.ShapeDtypeStruct((1, F), F32),
                   jax.ShapeDtypeStruct((1, F), F32), jax.ShapeDtypeStruct((3, F), F32)],
        compiler_params=_params(("arbitrary", "arbitrary")),
    )(df, df, df, up, up, up, up, up, up, fcw, fcb)


def gate_fwd(proj_g, y_a, y_b, name):
    T, D = y_a.shape
    tm = _pick(T, 512, 8)

    def body(g_ref, a_ref, b_ref, o_ref):
        sa = jax.nn.sigmoid(g_ref[:, :D].astype(F32))
        sb = jax.nn.sigmoid(g_ref[:, D:].astype(F32))
        o_ref[...] = (sa * a_ref[...].astype(F32) + sb * b_ref[...].astype(F32)).astype(BF16)

    row = pl.BlockSpec((tm, D), lambda i: (i, 0))
    return pl.pallas_call(
        body, name=name, grid=(T // tm,),
        in_specs=[pl.BlockSpec((tm, 2 * D), lambda i: (i, 0)), row, row],
        out_specs=row,
        out_shape=jax.ShapeDtypeStruct((T, D), BF16),
        compiler_params=_params(("parallel",)),
    )(proj_g, y_a, y_b)


def gate_bwd(dz, proj_g, y_a, y_b, name):
    T, D = y_a.shape
    tm = _pick(T, 512, 8)

    def body(dz_ref, g_ref, a_ref, b_ref, da_ref, db_ref, dg_ref):
        dzv = dz_ref[...].astype(F32)
        sa = jax.nn.sigmoid(g_ref[:, :D].astype(F32))
        sb = jax.nn.sigmoid(g_ref[:, D:].astype(F32))
        da_ref[...] = (dzv * sa).astype(BF16)
        db_ref[...] = (dzv * sb).astype(BF16)
        dg_ref[:, :D] = (dzv * a_ref[...].astype(F32) * (sa * (1.0 - sa))).astype(BF16)
        dg_ref[:, D:] = (dzv * b_ref[...].astype(F32) * (sb * (1.0 - sb))).astype(BF16)

    row = pl.BlockSpec((tm, D), lambda i: (i, 0))
    wide = pl.BlockSpec((tm, 2 * D), lambda i: (i, 0))
    return pl.pallas_call(
        body, name=name, grid=(T // tm,),
        in_specs=[row, wide, row, row],
        out_specs=[row, row, wide],
        out_shape=[jax.ShapeDtypeStruct((T, D), BF16), jax.ShapeDtypeStruct((T, D), BF16),
                   jax.ShapeDtypeStruct((T, 2 * D), BF16)],
        compiler_params=_params(("parallel",)),
    )(dz, proj_g, y_a, y_b)


ATT_WIN = ATT_TQ + 2 * RADIUS
ATT_STEP = 512
FAR = 1e32


def _att_window(qs, L):
    ks = pl.multiple_of(jnp.clip(qs - RADIUS, 0, L - ATT_WIN), RADIUS)
    return ks, jnp.where(qs == 0, 0, jnp.where(qs == L - ATT_TQ, 2, 1))


def _fill_bias_tables(bias_ref, sl_ref, hp, d):
    col_row = (lax.broadcasted_iota(jnp.int32, (ATT_TQ, ATT_WIN), 1)
               - lax.broadcasted_iota(jnp.int32, (ATT_TQ, ATT_WIN), 0))
    for v in range(3):
        ad = jnp.abs(col_row - v * RADIUS)
        dist = jnp.where(ad <= RADIUS, (ad * d).astype(F32), FAR)
        bias_ref[v, 0:ATT_TQ, :] = sl_ref[hp * 2] * dist
        bias_ref[v, ATT_TQ:2 * ATT_TQ, :] = sl_ref[hp * 2 + 1] * dist


def _head_masks():
    lane = lax.broadcasted_iota(jnp.int32, (1, LANES), 1)
    return [lane < HEAD_DIM, lane >= HEAD_DIM]


def _stack_heads(x, masks):
    zero = jnp.zeros_like(x)
    return jnp.concatenate([jnp.where(masks[0], x, zero), jnp.where(masks[1], x, zero)], axis=0)


def _unstack_heads(x2, masks):
    n = x2.shape[0] // 2
    return jnp.where(masks[0], x2[:n], x2[n:])


def _att_step(L):
    step = min(ATT_STEP, L)
    assert L % step == 0 and step % ATT_TQ == 0 and L >= ATT_WIN
    return step


def att_fwd(qkv, group, name):
    d, L, _ = qkv.shape
    step = _att_step(L)
    cg = GROUP_W // LANES
    slopes = jnp.asarray(_alibi_slopes()[group])
    scale = HEAD_DIM ** -0.5

    def body(sl_ref, q_ref, k_ref, v_ref, o_ref, l_ref, bias_ref):
        hp = pl.program_id(1)
        i = pl.program_id(2)

        @pl.when(i == 0)
        def _():
            _fill_bias_tables(bias_ref, sl_ref, hp, d)

        masks = _head_masks()
        for t in range(step // ATT_TQ):
            rows = slice(t * ATT_TQ, (t + 1) * ATT_TQ)
            ks, table = _att_window(i * step + t * ATT_TQ, L)
            q2 = _stack_heads(q_ref[rows, :] * scale, masks)
            kw = k_ref[pl.ds(ks, ATT_WIN), :]
            vw = v_ref[pl.ds(ks, ATT_WIN), :]
            s = lax.dot_general(q2, kw, NT_DIMS, preferred_element_type=F32) - bias_ref[table]
            m = jnp.max(s, -1, keepdims=True)
            p = jnp.exp(s - m)
            den = jnp.sum(p, -1, keepdims=True)
            pn = (p / den).astype(BF16)
            o2 = jnp.dot(pn, vw, preferred_element_type=F32)
            o_ref[rows, :] = _unstack_heads(o2, masks)
            l_ref[rows, :] = _unstack_heads(m + jnp.log(den), masks)

    out_spec = pl.BlockSpec((None, step, LANES), lambda r, hp, i: (r, i, hp))
    return pl.pallas_call(
        body, name=name, grid=(d, cg, L // step),
        in_specs=[pl.BlockSpec(memory_space=pltpu.SMEM),
                  pl.BlockSpec((None, step, LANES), lambda r, hp, i: (r, i, hp)),
                  pl.BlockSpec((None, L, LANES), lambda r, hp, i: (r, 0, cg + hp)),
                  pl.BlockSpec((None, L, LANES), lambda r, hp, i: (r, 0, 2 * cg + hp))],
        out_specs=[out_spec, out_spec],
        out_shape=[jax.ShapeDtypeStruct((d, L, GROUP_W), F32)] * 2,
        scratch_shapes=[pltpu.VMEM((3, 2 * ATT_TQ, ATT_WIN), F32)],
        compiler_params=_params(("arbitrary", "arbitrary", "arbitrary")),
    )(slopes, qkv, qkv, qkv)


def att_bwd(qkv, do, lse, dmat, group, name):
    d, L, _ = qkv.shape
    step = _att_step(L)
    nq = L // step
    cg = GROUP_W // LANES
    slopes = jnp.asarray(_alibi_slopes()[group])
    scale = HEAD_DIM ** -0.5

    def body(sl_ref, q_ref, k_ref, v_ref, do_ref, l_ref, dm_ref, dq_ref, dk_ref, dv_ref, dk_acc, dv_acc, bias_ref):
        hp = pl.program_id(1)
        i = pl.program_id(2)

        @pl.when(i == 0)
        def _():
            dk_acc[...] = jnp.zeros_like(dk_acc)
            dv_acc[...] = jnp.zeros_like(dv_acc)
            _fill_bias_tables(bias_ref, sl_ref, hp, d)

        masks = _head_masks()

        def head_cols(x):
            return jnp.concatenate([jnp.max(jnp.where(hm, x, -jnp.inf), -1, keepdims=True) for hm in masks], axis=0)

        for t in range(step // ATT_TQ):
            rows = slice(t * ATT_TQ, (t + 1) * ATT_TQ)
            ks, table = _att_window(i * step + t * ATT_TQ, L)
            q2 = _stack_heads(q_ref[rows, :] * scale, masks)
            do2 = _stack_heads(do_ref[rows, :], masks)
            kw = k_ref[pl.ds(ks, ATT_WIN), :]
            vw = v_ref[pl.ds(ks, ATT_WIN), :]
            s = lax.dot_general(q2, kw, NT_DIMS, preferred_element_type=F32) - bias_ref[table]
            p = jnp.exp(s - head_cols(l_ref[rows, :]))
            dp = lax.dot_general(do2, vw, NT_DIMS, preferred_element_type=F32)
            ds = (p * (dp - head_cols(dm_ref[rows, :]))).astype(BF16)
            dq2 = jnp.dot(ds, kw, preferred_element_type=F32)
            dq_ref[rows, :] = (_unstack_heads(dq2, masks) * scale).astype(BF16)
            dk_acc[pl.ds(ks, ATT_WIN), :] += lax.dot_general(ds, q2, TN_DIMS, preferred_element_type=F32)
            dv_acc[pl.ds(ks, ATT_WIN), :] += lax.dot_general(p.astype(BF16), do2, TN_DIMS, preferred_element_type=F32)

        @pl.when(i == nq - 1)
        def _():
            dk_ref[...] = dk_acc[...].astype(BF16)
            dv_ref[...] = dv_acc[...].astype(BF16)

    tile = pl.BlockSpec((None, step, LANES), lambda r, hp, i: (r, i, hp))
    whole = pl.BlockSpec((None, L, LANES), lambda r, hp, i: (r, 0, hp))
    return pl.pallas_call(
        body, name=name, grid=(d, cg, nq),
        in_specs=[pl.BlockSpec(memory_space=pltpu.SMEM), tile,
                  pl.BlockSpec((None, L, LANES), lambda r, hp, i: (r, 0, cg + hp)),
                  pl.BlockSpec((None, L, LANES), lambda r, hp, i: (r, 0, 2 * cg + hp)),
                  tile, tile, tile],
        out_specs=[tile, whole, whole],
        out_shape=[jax.ShapeDtypeStruct((d, L, GROUP_W), BF16)] * 3,
        scratch_shapes=[pltpu.VMEM((L, LANES), F32), pltpu.VMEM((L, LANES), F32),
                        pltpu.VMEM((3, 2 * ATT_TQ, ATT_WIN), F32)],
        compiler_params=_params(("arbitrary", "arbitrary", "arbitrary")),
    )(slopes, qkv, qkv, qkv, do, lse, dmat)


def _group_weights(ls):
    m = jnp.maximum(jnp.maximum(ls[0], ls[1]), ls[2])
    es = [jnp.exp(l - m) for l in ls]
    tot = es[0] + es[1] + es[2]
    return [e / tot for e in es]


def combine_fwd(outs, lses, name):
    T = outs[0].shape[0] * outs[0].shape[1]
    tm = _pick(T, 512, 8)
    n_scr = 2 * (len(DILATIONS) - 1)

    def body(*refs):
        o_refs, l_refs, c_ref, scr = refs[:3], refs[3:6], refs[6], refs[7:]
        o = [_load_natural(o_refs[g], d, scr[g - 1] if g else None) for g, d in enumerate(DILATIONS)]
        l = [_load_natural(l_refs[g], d, scr[g + 1] if g else None) for g, d in enumerate(DILATIONS)]
        w = _group_weights(l)
        c_ref[...] = (w[0] * o[0] + w[1] * o[1] + w[2] * o[2]).astype(BF16)

    specs = [_residue_spec(tm, d, GROUP_W) for d in DILATIONS]
    return pl.pallas_call(
        body, name=name, grid=(T // tm,),
        in_specs=specs + specs, out_specs=pl.BlockSpec((tm, GROUP_W), lambda i: (i, 0)),
        out_shape=jax.ShapeDtypeStruct((T, GROUP_W), BF16),
        scratch_shapes=[_residue_scratch(tm, GROUP_W)] * n_scr,
        compiler_params=_params(("parallel",)),
    )(*outs, *lses)


def combine_bwd(dcomb, outs, lses, name):
    T = dcomb.shape[0]
    tm = _pick(T, 256, 8)
    head = np.arange(GROUP_W) // HEAD_DIM
    seg = jnp.asarray((head[:, None] == head[None, :]).astype(np.float32)).astype(BF16)
    ng = len(DILATIONS)
    n_scr = 4 * (ng - 1)

    def body(*refs):
        dc_ref, o_refs, l_refs, e_ref = refs[0], refs[1:1 + ng], refs[1 + ng:1 + 2 * ng], refs[1 + 2 * ng]
        do_refs, dm_refs = refs[2 + 2 * ng:2 + 3 * ng], refs[2 + 3 * ng:2 + 4 * ng]
        scr = refs[2 + 4 * ng:]
        o = [_load_natural(o_refs[g], d, scr[4 * (g - 1)] if g else None) for g, d in enumerate(DILATIONS)]
        l = [_load_natural(l_refs[g], d, scr[4 * (g - 1) + 1] if g else None) for g, d in enumerate(DILATIONS)]
        w = _group_weights(l)
        dc = dc_ref[...].astype(F32)
        e = e_ref[...]
        tot = jnp.zeros_like(dc)
        for g in range(ng):
            prod = dc * o[g]
            dw = jnp.zeros_like(dc)
            for _ in range(3):
                part = prod.astype(BF16)
                dw = dw + jnp.dot(part, e, preferred_element_type=F32)
                prod = prod - part.astype(F32)
            tot = tot + w[g] * dw
        for g, d in enumerate(DILATIONS):
            _store_by_residue(w[g] * dc, do_refs[g], d, scr[4 * (g - 1) + 2] if g else None)
            _store_by_residue(w[g] * tot, dm_refs[g], d, scr[4 * (g - 1) + 3] if g else None)

    specs = [_residue_spec(tm, d, GROUP_W) for d in DILATIONS]
    res = pl.pallas_call(
        body, name=name, grid=(T // tm,),
        in_specs=[pl.BlockSpec((tm, GROUP_W), lambda i: (i, 0))] + specs + specs
        + [pl.BlockSpec((GROUP_W, GROUP_W), lambda i: (0, 0))],
        out_specs=specs + specs,
        out_shape=[jax.ShapeDtypeStruct(o.shape, BF16) for o in outs] + [jax.ShapeDtypeStruct(o.shape, F32) for o in outs],
        scratch_shapes=[_residue_scratch(tm, GROUP_W)] * n_scr,
        compiler_params=_params(("parallel",)),
    )(dcomb, *outs, *lses, seg)
    return res[:ng], res[ng:]


def _position():
    return lax.axis_index("x"), lax.axis_index("y"), lax.axis_index("c")


def _other_chips(x, y):
    return [(1 - x, y), (x, 1 - y), (1 - x, 1 - y)]


def _remote(src, dst, send_sems, recv_sems, k, to):
    return pltpu.make_async_remote_copy(src_ref=src, dst_ref=dst, send_sem=send_sems.at[k], recv_sem=recv_sems.at[k],
                                        device_id=to, device_id_type=MESH)


def _gather_descriptors(ins, outs, send_sems, recv_sems, local_sems):
    n = len(ins)
    x, y, c = _position()
    sibling = (x, y, 1 - c)
    chips = _other_chips(x, y)

    def block(a, px, py, pc):
        return outs[a].at[4 * px + 2 * py + pc]

    own, first, arrivals = [], [], []
    for a in range(n):
        k0 = 7 * a
        mine = block(a, x, y, c)
        own.append(pltpu.make_async_copy(ins[a], mine, local_sems.at[a]))
        first.append(_remote(ins[a], mine, send_sems, recv_sems, k0, sibling))
        row = []
        for j, chip in enumerate(chips):
            first.append(_remote(ins[a], mine, send_sems, recv_sems, k0 + 1 + j, (*chip, c)))
            got = block(a, *chip, c)
            row.append((_remote(got, got, send_sems, recv_sems, k0 + 1 + j, sibling),
                        _remote(got, got, send_sems, recv_sems, k0 + 4 + j, sibling)))
        arrivals.append(row)
    return own, first, arrivals


def _gather_begin(ins, outs, send_sems, recv_sems, local_sems):
    own, first, _ = _gather_descriptors(ins, outs, send_sems, recv_sems, local_sems)
    for cp in own + first:
        cp.start()


def _gather_finish(ins, outs, send_sems, recv_sems, local_sems):
    own, first, arrivals = _gather_descriptors(ins, outs, send_sems, recv_sems, local_sems)
    passed = []
    for row in arrivals:
        for arrived, onward in row:
            arrived.wait_recv()
            onward.start()
            passed.append(onward)
    for a in range(len(ins)):
        first[4 * a].wait_recv()
        for _, onward in arrivals[a]:
            onward.wait_recv()
    for cp in first + passed:
        cp.wait_send()
    for cp in own:
        cp.wait()


def _gather_scratch(n):
    return [pltpu.SemaphoreType.DMA((7 * n,)), pltpu.SemaphoreType.DMA((7 * n,)), pltpu.SemaphoreType.DMA((n,))]


def all_gather(shards, name):
    n = len(shards)

    def body(*refs):
        ins, outs, sems = refs[:n], refs[n:2 * n], refs[2 * n:]
        _gather_begin(ins, outs, *sems)
        _gather_finish(ins, outs, *sems)

    hbm = pl.BlockSpec(memory_space=pl.ANY)
    return pl.pallas_call(
        body, name=name,
        in_specs=[hbm] * n, out_specs=[hbm] * n,
        out_shape=[jax.ShapeDtypeStruct((N_DEV,) + s.shape, s.dtype) for s in shards],
        scratch_shapes=_gather_scratch(n),
    )(*shards)


def exchange_sibling(parts, name):
    n = len(parts)

    def body(*refs):
        ins, outs = refs[:n], refs[n:2 * n]
        send_sems, recv_sems = refs[2 * n:]
        x, y, c = _position()
        sibling = (x, y, 1 - c)
        copies = []
        for a in range(n):
            for q in range(4):
                cp = _remote(ins[a].at[2 * q + (1 - c)], outs[a].at[q], send_sems, recv_sems, 4 * a + q, sibling)
                cp.start()
                copies.append(cp)
        for cp in copies:
            cp.wait_recv()
        for cp in copies:
            cp.wait_send()

    hbm = pl.BlockSpec(memory_space=pl.ANY)
    return pl.pallas_call(
        body, name=name,
        in_specs=[hbm] * n, out_specs=[hbm] * n,
        out_shape=[jax.ShapeDtypeStruct((4,) + p.shape[1:], p.dtype) for p in parts],
        scratch_shapes=[pltpu.SemaphoreType.DMA((4 * n,)), pltpu.SemaphoreType.DMA((4 * n,))],
    )(*parts)


def exchange_chips(sums, name):
    n = len(sums)

    def body(*refs):
        ins, outs = refs[:n], refs[n:2 * n]
        send_sems, recv_sems = refs[2 * n:]
        x, y, c = _position()
        copies = []
        for a in range(n):
            for j, (cx, cy) in enumerate(_other_chips(x, y)):
                cp = _remote(ins[a].at[2 * cx + cy], outs[a].at[j], send_sems, recv_sems, 3 * a + j, (cx, cy, c))
                cp.start()
                copies.append(cp)
        for cp in copies:
            cp.wait_recv()
        for cp in copies:
            cp.wait_send()

    hbm = pl.BlockSpec(memory_space=pl.ANY)
    return pl.pallas_call(
        body, name=name,
        in_specs=[hbm] * n, out_specs=[hbm] * n,
        out_shape=[jax.ShapeDtypeStruct((3,) + s.shape[1:], s.dtype) for s in sums],
        scratch_shapes=[pltpu.SemaphoreType.DMA((3 * n,)), pltpu.SemaphoreType.DMA((3 * n,))],
    )(*sums)


_HBM = pl.BlockSpec(memory_space=pltpu.HBM)
_SEM = pl.BlockSpec(memory_space=pltpu.SEMAPHORE)
_DATAFLOW = pltpu.SideEffectType.DATAFLOW_SIDE_EFFECTING


def _to_all_plan(srcs, lands, send_sems, recv_sems):
    x, y, c = _position()
    me = 4 * x + 2 * y + c
    copies = []
    for a in range(len(srcs)):
        for k in range(1, N_DEV):
            fx, fy, fc = (k >> 2) & 1, (k >> 1) & 1, k & 1
            to = (1 - x if fx else x, 1 - y if fy else y, 1 - c if fc else c)
            copies.append(_remote(srcs[a], lands[a].at[me], send_sems, recv_sems, (N_DEV - 1) * a + k - 1, to))
    return copies


def _to_chips_plan(srcs, lands, send_sems, recv_sems):
    x, y, c = _position()
    copies = []
    for a in range(len(srcs)):
        for j, (cx, cy) in enumerate(_other_chips(x, y)):
            copies.append(_remote(srcs[a].at[2 * cx + cy], lands[a].at[j], send_sems, recv_sems, 3 * a + j, (cx, cy, c)))
    return copies


def copies_start(srcs, land_shapes, plan, per_array, name):
    n = len(srcs)
    n_sem = per_array * n
    lands = [lax.empty(s.shape, s.dtype) for s in land_shapes]

    def body(*refs):
        src_refs, land_refs = refs[:n], refs[n:2 * n]
        send_sems, recv_sems = refs[2 * n], refs[2 * n + 1]
        token = refs[-1]
        for cp in plan(src_refs, land_refs, send_sems, recv_sems):
            cp.start()
        token[...] = jnp.zeros_like(token)

    out = pl.pallas_call(
        body, name=name,
        out_shape=(pltpu.SemaphoreType.DMA((n_sem,)), pltpu.SemaphoreType.DMA((n_sem,)))
        + tuple(pltpu.HBM(s.shape, s.dtype) for s in srcs)
        + tuple(pltpu.HBM(s.shape, s.dtype) for s in land_shapes)
        + (jax.ShapeDtypeStruct((8, LANES), F32),),
        in_specs=[_HBM] * (2 * n),
        out_specs=(_SEM, _SEM) + (_HBM,) * (2 * n) + (pl.BlockSpec(memory_space=pltpu.VMEM),),
        input_output_aliases={i: 2 + i for i in range(2 * n)},
        compiler_params=pltpu.CompilerParams(has_side_effects=_DATAFLOW),
    )(*[pltpu.with_memory_space_constraint(s, pltpu.HBM) for s in srcs],
      *[pltpu.with_memory_space_constraint(l, pltpu.HBM) for l in lands])
    return out[:-1], out[-1]


def copies_wait(handles, plan, after, name):
    send_sems, recv_sems = handles[0], handles[1]
    n = (len(handles) - 2) // 2
    thru = handles[2:]

    def body(*refs):
        src_refs, land_refs = refs[:n], refs[n:2 * n]
        send_sems, recv_sems = refs[2 * n], refs[2 * n + 1]
        copies = plan(src_refs, land_refs, send_sems, recv_sems)
        for cp in copies:
            cp.wait_recv()
        for cp in copies:
            cp.wait_send()

    out = pl.pallas_call(
        body, name=name,
        out_shape=tuple(pltpu.HBM(t.shape, t.dtype) for t in thru),
        in_specs=[_HBM] * (2 * n) + [_SEM, _SEM, pl.BlockSpec(memory_space=pl.ANY)],
        out_specs=(_HBM,) * (2 * n),
        input_output_aliases={i: i for i in range(2 * n)},
        compiler_params=pltpu.CompilerParams(has_side_effects=_DATAFLOW),
    )(*thru, send_sems, recv_sems, after)
    return out[n:]


def all_sum_small(vec, name):
    R = vec.shape[0]

    def body(v_ref, tot_ref, all_ref, send_sems, recv_sems):
        x, y, c = _position()
        me = 4 * x + 2 * y + c
        all_ref[me] = v_ref[...]
        copies = []
        for k in range(1, N_DEV):
            fx, fy, fc = (k >> 2) & 1, (k >> 1) & 1, k & 1
            to = (1 - x if fx else x, 1 - y if fy else y, 1 - c if fc else c)
            cp = _remote(v_ref, all_ref.at[me], send_sems, recv_sems, k - 1, to)
            cp.start()
            copies.append(cp)
        for cp in copies:
            cp.wait_recv()
        for cp in copies:
            cp.wait_send()
        tot = all_ref[0]
        for j in range(1, N_DEV):
            tot = tot + all_ref[j]
        tot_ref[...] = tot

    vmem = pl.BlockSpec(memory_space=pltpu.VMEM)
    return pl.pallas_call(
        body, name=name,
        in_specs=[vmem], out_specs=vmem,
        out_shape=jax.ShapeDtypeStruct((R, LANES), F32),
        scratch_shapes=[pltpu.VMEM((N_DEV, R, LANES), F32),
                        pltpu.SemaphoreType.DMA((N_DEV - 1,)), pltpu.SemaphoreType.DMA((N_DEV - 1,))],
        compiler_params=pltpu.CompilerParams(vmem_limit_bytes=VMEM_LIMIT),
    )(vec)


def pair_add(parts, theirs, place, name):
    _, R, C = theirs.shape
    tr = _pick(R, 256, 8)

    def body(place_ref, a_ref, b_ref, o_ref):
        o_ref[...] = (a_ref[...].astype(F32) + b_ref[...].astype(F32)).astype(BF16)

    blk = pl.BlockSpec((None, tr, C), lambda q, i, place_ref: (q, i, 0))
    return pl.pallas_call(
        body, name=name,
        grid_spec=pltpu.PrefetchScalarGridSpec(
            num_scalar_prefetch=1, grid=(4, R // tr),
            in_specs=[pl.BlockSpec((None, tr, C), lambda q, i, place_ref: (2 * q + place_ref[2], i, 0)), blk],
            out_specs=blk),
        out_shape=jax.ShapeDtypeStruct(theirs.shape, BF16),
        compiler_params=_params(("parallel", "parallel")),
    )(place, parts, theirs)


def _adamw_math(w, g, m, v):
    m = ADAM_B1 * m + (1.0 - ADAM_B1) * g
    v = ADAM_B2 * v + (1.0 - ADAM_B2) * jnp.square(g)
    m_hat = m / (1.0 - ADAM_B1 ** ADAM_STEP)
    v_hat = v / (1.0 - ADAM_B2 ** ADAM_STEP)
    delta = -ADAM_LR * (m_hat / (jnp.sqrt(v_hat) + ADAM_EPS) + ADAM_WD * w)
    return delta, m, v


def adamw_sharded(w, m, v, parts, sib, others, place, name):
    R, C = w.shape
    tr = _pick(R, 256, 8)

    def body(place_ref, w_ref, m_ref, v_ref, a_ref, b_ref, o_ref, g_ref, d_ref, nm_ref, nv_ref):
        g = a_ref[...].astype(F32) + b_ref[...].astype(F32)
        for j in range(3):
            g = g + o_ref[j].astype(F32)
        delta, nm, nv = _adamw_math(w_ref[...], g, m_ref[...], v_ref[...])
        g_ref[...] = g
        d_ref[...] = delta
        nm_ref[...] = nm
        nv_ref[...] = nv

    row = pl.BlockSpec((tr, C), lambda i, place_ref: (i, 0))
    return pl.pallas_call(
        body, name=name,
        grid_spec=pltpu.PrefetchScalarGridSpec(
            num_scalar_prefetch=1, grid=(R // tr,),
            in_specs=[row] * 3 + [pl.BlockSpec((None, tr, C), lambda i, place_ref: (place_ref[0], i, 0)),
                                  pl.BlockSpec((None, tr, C), lambda i, place_ref: (place_ref[1], i, 0)),
                                  pl.BlockSpec((3, tr, C), lambda i, place_ref: (0, i, 0))],
            out_specs=[row] * 4),
        out_shape=[jax.ShapeDtypeStruct((R, C), F32)] * 4,
        compiler_params=_params(("parallel",)),
    )(place, w, m, v, parts, sib, others)


def adamw_packed(w, g, m, v, name):
    R = w.shape[0]

    def body(w_ref, g_ref, m_ref, v_ref, d_ref, nm_ref, nv_ref):
        delta, nm, nv = _adamw_math(w_ref[...], g_ref[...], m_ref[...], v_ref[...])
        d_ref[...] = delta
        nm_ref[...] = nm
        nv_ref[...] = nv

    full = pl.BlockSpec((R, LANES), lambda i: (0, 0))
    return pl.pallas_call(
        body, name=name, grid=(1,),
        in_specs=[full] * 4, out_specs=[full] * 3,
        out_shape=[jax.ShapeDtypeStruct((R, LANES), F32)] * 3,
        compiler_params=_params(("arbitrary",)),
    )(w, g, m, v)


def _pack(arrays):
    flat = []
    sizes = []
    for a in arrays:
        f = a.reshape(-1).astype(F32)
        pad = (-f.shape[0]) % LANES
        if pad:
            f = jnp.concatenate([f, jnp.zeros((pad,), F32)])
        flat.append(f)
        sizes.append(f.shape[0])
    rows = sum(sizes) // LANES
    pad_rows = (-rows) % 8
    if pad_rows:
        flat.append(jnp.zeros((pad_rows * LANES,), F32))
    return jnp.concatenate(flat).reshape(-1, LANES), sizes


def _unpack(packed, sizes, shapes):
    flat = packed.reshape(-1)
    out = []
    off = 0
    for size, shape in zip(sizes, shapes):
        n = int(np.prod(shape))
        out.append(flat[off:off + n].reshape(shape))
        off += size
    return out


def _to_blocks(full, axis):
    if axis == 0:
        return full.reshape(N_DEV, full.shape[0] // N_DEV, full.shape[1])
    r, n = full.shape
    return full.reshape(r, N_DEV, n // N_DEV).transpose(1, 0, 2)


def _from_blocks(blocks, axis):
    if axis == 0:
        return blocks.reshape(blocks.shape[0] * blocks.shape[1], blocks.shape[2])
    return blocks.transpose(1, 0, 2).reshape(blocks.shape[1], blocks.shape[0] * blocks.shape[2])


def kernel(x, ln0_g, ln0_b, w_in, b_in, conv_w, w_a, w_b, w_o, b_o, ln1_g, ln1_b, w_up, b_up, ffn_conv_w, ffn_conv_b, w_down, b_down, ln2_g, ln2_b, loss_target, m_ln0_g, m_ln0_b, m_w_in, m_b_in, m_conv_w, m_w_a, m_w_b, m_w_o, m_b_o, m_ln1_g, m_ln1_b, m_w_up, m_b_up, m_ffn_conv_w, m_ffn_conv_b, m_w_down, m_b_down, m_ln2_g, m_ln2_b, v_ln0_g, v_ln0_b, v_w_in, v_b_in, v_conv_w, v_w_a, v_w_b, v_w_o, v_b_o, v_ln1_g, v_ln1_b, v_w_up, v_b_up, v_ffn_conv_w, v_ffn_conv_b, v_w_down, v_b_down, v_ln2_g, v_ln2_b):
    T, D = x.shape[1], x.shape[2]
    F = ffn_conv_b.shape[-1]
    xs = x.reshape(T, D)
    tgt = loss_target.reshape(T, D)
    dev = 4 * lax.axis_index("x") + 2 * lax.axis_index("y") + lax.axis_index("c")
    chip = 2 * lax.axis_index("x") + lax.axis_index("y")
    core = lax.axis_index("c")
    place = jnp.stack([dev, chip, core]).astype(jnp.int32)

    big = dict(w_in=(w_in[0], 1), w_a=(w_a[0], 0), w_b=(w_b[0], 1), w_o=(w_o[0], 0), w_up=(w_up[0], 1),
               w_down=(w_down[0], 0))
    names = list(big)
    shards = {k: big[k][0].astype(BF16) for k in names}
    ln0g, ln0b = ln0_g.reshape(1, D), ln0_b.reshape(1, D)
    h0, h0b, *rest = ln_fwd(xs, None, ln0g, ln0b, "ln0_fwd_gather_w_in", dilations=DILATIONS[1:],
                            gather=[shards["w_in"], conv_w[0], ffn_conv_w[0]])
    h0_res = [h0b] + [h.reshape(T, D) for h in rest[:2]]
    g_in, g_conv, g_fcw = rest[2:]
    full = {"w_in": _from_blocks(g_in, 1)}
    conv_full = _from_blocks(g_conv, 1)
    fcw_full = _from_blocks(g_fcw, 1)
    late_groups = (("w_a", "w_b", "w_o"), ("w_up", "w_down"))
    late_handles = []
    token = conv_full[:1, :1] * 0.0
    for n, keys in enumerate(late_groups):
        srcs = [shards[k] + token[0, 0].astype(BF16) for k in keys]
        handles, token = copies_start(srcs, [jax.ShapeDtypeStruct((N_DEV,) + s.shape, BF16) for s in srcs],
                                      _to_all_plan, N_DEV - 1, f"gather_late_{n}_start")
        late_handles.append(handles)

    def late_weights(n, after):
        lands = copies_wait(late_handles[n], _to_all_plan, after, f"gather_late_{n}_wait")
        for k, land in zip(late_groups[n], lands):
            full[k] = _from_blocks(lax.dynamic_update_index_in_dim(land, shards[k], dev, 0), big[k][1])

    o_q = 3 * D
    o_g = o_q + 3 * QKV_W
    w_pa, w_qkv, w_pg = full["w_in"][:, :o_q], full["w_in"][:, o_q:o_g], full["w_in"][:, o_g:]
    b_pa, b_qkv, b_pg = b_in[:, :o_q], b_in[:, o_q:o_g], b_in[:, o_g:]

    proj_a = mm_nn(h0b, w_pa, b_pa, ACT, "proj_conv", after=token)
    proj_g = mm_nn(h0b, w_pg, b_pg, ACT, "proj_gates")
    zero_d = jnp.zeros((1, D), F32)
    s_a = conv_a_fwd(proj_a, conv_full, "conv_a_fwd")
    late_weights(0, s_a)
    y_a = mm_nn(s_a, full["w_a"], zero_d, ACT, "branch_a_out")

    def group_cols(m, g):
        return jnp.concatenate([m[:, s * QKV_W + g * GROUP_W:s * QKV_W + (g + 1) * GROUP_W] for s in range(3)], 1)

    w_grp = [group_cols(w_qkv, g) for g in range(3)]
    qkvs, outs, lses = [], [], []
    for g, d in enumerate(DILATIONS):
        qkv = mm_nn(h0_res[g], w_grp[g], group_cols(b_qkv, g), BF16, f"proj_qkv_{g}").reshape(d, T // d, 3 * GROUP_W)
        o, l = att_fwd(qkv, g, f"att_fwd_{g}")
        qkvs.append(qkv)
        outs.append(o)
        lses.append(l)
    comb = combine_fwd(outs, lses, "combine_fwd")
    y_b = mm_nn(comb, full["w_b"], zero_d, ACT, "branch_b_out")
    z = gate_fwd(proj_g, y_a, y_b, "gate_fwd")
    mix = mm_nn(z, full["w_o"], b_o, F32, "mix_out")
    h1, h1b = ln_fwd(h0, mix, ln1_g, ln1_b, "ln1_fwd")
    late_weights(1, h1b)
    up = mm_nn(h1b, full["w_up"], b_up, F32, "ffn_up")
    f_act = conv_f_fwd(up, fcw_full, ffn_conv_b, "conv_f_fwd")
    ffn = mm_nn(f_act, full["w_down"], b_down, F32, "ffn_down")

    dr2, dr2b, d_ln2_g, d_ln2_b, d_b_down, loss_part = ln_bwd(h1, ffn, ln2_g, ln2_b, None, None, tgt, "ln2_loss_bwd")
    dw_down, _ = mm_tn(f_act, dr2b, "dw_down")
    df = mm_nt(dr2b, full["w_down"], None, "d_ffn_act")
    d_a, d_gate, cs_a, cs_gate, d_fcb, d_fcw = conv_f_bwd(df, up, fcw_full, ffn_conv_b, "conv_f_bwd")
    dw_up_a, _ = mm_tn(h1b, d_a, "dw_up_a")
    dw_up_g, _ = mm_tn(h1b, d_gate, "dw_up_gate")
    dh1 = mm_nt([d_a, d_gate], full["w_up"], None, "d_h1")
    dr1, dr1b, d_ln1_g, d_ln1_b, d_b_o, _ = ln_bwd(h0, mix, ln1_g, ln1_b, dr2, dh1, None, "ln1_bwd")
    dw_o, _ = mm_tn(z, dr1b, "dw_o")
    dz = mm_nt(dr1b, full["w_o"], None, "d_z", out_dtype=ACT)
    dy_a, dy_b, dproj_g = gate_bwd(dz, proj_g, y_a, y_b, "gate_bwd")
    dw_a, _ = mm_tn(s_a, dy_a, "dw_a")
    ds_a = mm_nt(dy_a, full["w_a"], None, "d_s_a", out_dtype=ACT)
    dproj_a, d_conv = conv_a_bwd(ds_a, proj_a, conv_full, "conv_a_bwd")
    dw_b, _ = mm_tn(comb, dy_b, "dw_b")

    rs_mine, rs_sib, rs_handles = {}, {}, {}

    def reduce_start(keys, grads, tag):
        parts = [_to_blocks(grads[k], big[k][1]) for k in keys]
        from_sib = exchange_sibling(parts, f"grads_to_sibling_{tag}")
        sums = [pair_add(a, b, place, f"chip_sum_{k}") for k, a, b in zip(keys, parts, from_sib)]
        handles, tok = copies_start(sums, [jax.ShapeDtypeStruct((3,) + s.shape[1:], BF16) for s in sums],
                                    _to_chips_plan, 3, f"grads_to_chips_{tag}_start")
        for k, a, b in zip(keys, parts, from_sib):
            rs_mine[k], rs_sib[k] = a, b
        rs_handles[tag] = (keys, handles)
        return tok

    tok_a = reduce_start(("w_a", "w_b", "w_o", "w_up", "w_down"),
                         dict(w_a=dw_a, w_b=dw_b, w_o=dw_o, w_up=jnp.concatenate([dw_up_a, dw_up_g], 1), w_down=dw_down),
                         "a")
    dcomb = mm_nt(dy_b, full["w_b"], None, "d_comb", after=tok_a, out_dtype=ACT)
    dos, dms = combine_bwd(dcomb, outs, lses, "combine_bwd")
    dw_grp, cs_grp, dqkvs = [], [], []
    for g, d in enumerate(DILATIONS):
        dq, dk, dv = att_bwd(qkvs[g], dos[g], lses[g], dms[g], g, f"att_bwd_{g}")
        dqkv = [t.reshape(T, GROUP_W) for t in (dq, dk, dv)]
        dwg, csg = mm_tn(h0_res[g], dqkv, f"dw_in_qkv_{g}")
        dqkvs.append(dqkv)
        dw_grp.append(dwg)
        cs_grp.append(csg)
    dw_pa, cs_pa = mm_tn(h0b, dproj_a, "dw_in_conv")
    dw_pg, cs_pg = mm_tn(h0b, dproj_g, "dw_in_gates")

    def ungroup(parts):
        return jnp.concatenate([p[:, s * GROUP_W:(s + 1) * GROUP_W] for s in range(3) for p in parts], 1)

    db_in_parts = [cs_pa, ungroup(cs_grp), cs_pg]
    tok_b = reduce_start(("w_in",), dict(w_in=jnp.concatenate([dw_pa, ungroup(dw_grp), dw_pg], 1)), "b")
    dh0 = mm_nt(dproj_a, w_pa, None, "d_h0_conv", after=tok_b)
    dh0 = mm_nt(dproj_g, w_pg, dh0, "d_h0_gates")
    dh0 = mm_nt(dqkvs[0], w_grp[0], dh0, "d_h0_qkv_0")
    dh0_res = [(mm_nt(dqkvs[g], w_grp[g], None, f"d_h0_qkv_{g}").reshape(d, T // d, D), d)
               for g, d in enumerate(DILATIONS) if g > 0]
    dx, _, d_ln0_g, d_ln0_b, _, _ = ln_bwd(xs, None, ln0g, ln0b, dr1, dh0, None, "ln0_bwd", by_residue=dh0_res)

    small = [d_ln0_g, d_ln0_b, jnp.concatenate(db_in_parts, 1), d_conv, d_b_o, d_ln1_g, d_ln1_b,
             jnp.concatenate([cs_a, cs_gate], 1), d_fcw, d_fcb, d_b_down, d_ln2_g, d_ln2_b, loss_part]
    packed, sizes = _pack(small)
    total = all_sum_small(packed, "sum_small")
    (g_ln0_g, g_ln0_b, g_b_in, g_conv_full, g_b_o, g_ln1_g, g_ln1_b, g_b_up, g_fcw_full, g_fcb, g_b_down, g_ln2_g,
     g_ln2_b, loss) = _unpack(total, sizes, [a.shape for a in small])
    cw = conv_w.shape[-1]
    fw = ffn_conv_w.shape[-1]
    g_conv = lax.dynamic_slice_in_dim(g_conv_full, dev * cw, cw, 1)
    g_fcw = lax.dynamic_slice_in_dim(g_fcw_full, dev * fw, fw, 1)

    from_chips = {}
    for tag, (keys, handles) in rs_handles.items():
        lands = copies_wait(handles, _to_chips_plan, total, f"grads_to_chips_{tag}_wait")
        from_chips.update(zip(keys, lands))

    moments = dict(w_in=(m_w_in, v_w_in), w_a=(m_w_a, v_w_a), w_b=(m_w_b, v_w_b), w_o=(m_w_o, v_w_o),
                   w_up=(m_w_up, v_w_up), w_down=(m_w_down, v_w_down))
    res_big = {}
    for k in names:
        res_big[k] = adamw_sharded(big[k][0], moments[k][0][0], moments[k][1][0], rs_mine[k], rs_sib[k], from_chips[k],
                                   place, f"adamw_{k}")

    small_names = ["ln0_g", "ln0_b", "b_in", "conv_w", "b_o", "ln1_g", "ln1_b", "b_up", "ffn_conv_w", "ffn_conv_b",
                   "b_down", "ln2_g", "ln2_b"]
    small_w = [ln0_g, ln0_b, b_in, conv_w, b_o, ln1_g, ln1_b, b_up, ffn_conv_w, ffn_conv_b, b_down, ln2_g, ln2_b]
    small_m = [m_ln0_g, m_ln0_b, m_b_in, m_conv_w, m_b_o, m_ln1_g, m_ln1_b, m_b_up, m_ffn_conv_w, m_ffn_conv_b,
               m_b_down, m_ln2_g, m_ln2_b]
    small_v = [v_ln0_g, v_ln0_b, v_b_in, v_conv_w, v_b_o, v_ln1_g, v_ln1_b, v_b_up, v_ffn_conv_w, v_ffn_conv_b,
               v_b_down, v_ln2_g, v_ln2_b]
    small_g = [g_ln0_g, g_ln0_b, g_b_in, g_conv, g_b_o, g_ln1_g, g_ln1_b, g_b_up, g_fcw, g_fcb, g_b_down, g_ln2_g,
               g_ln2_b]
    shapes = [w.shape for w in small_w]
    small_g = [g.reshape(s) for g, s in zip(small_g, shapes)]
    pw, psz = _pack(small_w)
    pg, _ = _pack(small_g)
    pm, _ = _pack(small_m)
    pv, _ = _pack(small_v)
    pd, pnm, pnv = adamw_packed(pw, pg, pm, pv, "adamw_small")
    res_small = {k: (g, d_, m_, v_) for k, g, d_, m_, v_ in zip(
        small_names, small_g, _unpack(pd, psz, shapes), _unpack(pnm, psz, shapes), _unpack(pnv, psz, shapes))}

    order = ["ln0_g", "ln0_b", "w_in", "b_in", "conv_w", "w_a", "w_b", "w_o", "b_o", "ln1_g", "ln1_b", "w_up", "b_up",
             "ffn_conv_w", "ffn_conv_b", "w_down", "b_down", "ln2_g", "ln2_b"]

    def result(k, j):
        if k in res_big:
            return res_big[k][j][None]
        return res_small[k][j]

    out = [loss.reshape(()), dx.reshape(x.shape)]
    for j in range(4):
        out += [result(k, j) for k in order]
    return tuple(out)
```

```python
import functools
import math

import numpy as np
import jax
import jax.numpy as jnp
from jax import lax
from jax.experimental import pallas as pl
from jax.experimental.pallas import tpu as pltpu

F32 = jnp.float32
BF16 = jnp.bfloat16
ACT = BF16

N_DEV = 8
LN_EPS = 1e-5
ALPHA = (2.0 * 1) ** 0.25
MASK_VALUE = -1e30
HEAD_DIM = 64
GROUP_W = 512
QKV_W = 3 * GROUP_W
DILATIONS = (1, 4, 16)
RADIUS = 64
LANES = 128
HALO = 8
HALO_BF16 = 16
ATT_TQ = 128

ADAM_LR = 0.001
ADAM_B1 = 0.9
ADAM_B2 = 0.999
ADAM_EPS = 1e-08
ADAM_WD = 0.01
ADAM_STEP = 10

VMEM_LIMIT = 52 * 1024 * 1024
OUT_TILE_BYTES = 8 * 1024 * 1024
MESH = pl.DeviceIdType.MESH
NT_DIMS = (((1,), (1,)), ((), ()))
TN_DIMS = (((0,), (0,)), ((), ()))


def _pick(n, target, align=LANES):
    if n <= target:
        return n
    best = None
    for t in range(align, target + 1, align):
        if n % t == 0:
            best = t
    assert best is not None, (n, target, align)
    return best


def _params(sems=None):
    return pltpu.CompilerParams(dimension_semantics=sems, vmem_limit_bytes=VMEM_LIMIT)


def _alibi_slopes():
    n = 3 * 8
    return np.exp2(-8.0 * np.arange(1, n + 1, dtype=np.float64) / n).astype(np.float32).reshape(3, 8)


def _ln_stats(r):
    mu = jnp.mean(r, -1, keepdims=True)
    xc = r - mu
    var = jnp.mean(xc * xc, -1, keepdims=True)
    rstd = lax.rsqrt(var + LN_EPS)
    return xc, rstd


def _load_natural(ref, d, scr):
    if d == 1:
        return ref[0]
    n, C = ref.shape[1], ref.shape[2]
    for c in range(C // LANES):
        for r in range(d):
            scr[c, pl.ds(r, n, stride=d), :] = ref[r, :, c * LANES:(c + 1) * LANES]
    return jnp.concatenate([scr[c] for c in range(C // LANES)], axis=1)


def _store_by_residue(val, ref, d, scr):
    if d == 1:
        ref[0] = val.astype(ref.dtype)
        return
    n, C = ref.shape[1], ref.shape[2]
    for c in range(C // LANES):
        scr[c] = val[:, c * LANES:(c + 1) * LANES]
    for c in range(C // LANES):
        for r in range(d):
            ref[r, :, c * LANES:(c + 1) * LANES] = scr[c, pl.ds(r, n, stride=d), :].astype(ref.dtype)


def _residue_spec(tm, d, C):
    return pl.BlockSpec((d, tm // d, C), lambda i: (0, i, 0))


def _residue_scratch(tm, C):
    return pltpu.VMEM((C // LANES, tm, LANES), F32)


def ln_fwd(a, res, g, b, name, dilations=(), gather=()):
    T, D = a.shape
    tm = _pick(T, 512, 8)
    has_res = res is not None
    nd = len(dilations)
    ng = len(gather)
    n_in = (2 if has_res else 1) + 2
    last = T // tm - 1

    def body(*refs):
        a_ref = refs[0]
        r = a_ref[...]
        if has_res:
            r = ALPHA * r + refs[1][...]
        g_ref, b_ref = refs[n_in - 2], refs[n_in - 1]
        shard_refs = refs[n_in:n_in + ng]
        h_ref, hb_ref = refs[n_in + ng], refs[n_in + ng + 1]
        p_refs = refs[n_in + ng + 2:n_in + ng + 2 + nd]
        full_refs = refs[n_in + ng + 2 + nd:n_in + 2 * ng + 2 + nd]
        scratch = refs[n_in + 2 * ng + 2 + nd:]
        sems = scratch[len(scratch) - 3:] if ng else ()

        if ng:
            @pl.when(pl.program_id(0) == 0)
            def _():
                _gather_begin(shard_refs, full_refs, *sems)

        xc, rstd = _ln_stats(r)
        h = xc * rstd * g_ref[...] + b_ref[...]
        h_ref[...] = h
        hb_ref[...] = h.astype(BF16)
        for d, p_ref in zip(dilations, p_refs):
            _store_by_residue(h, p_ref, d, scratch[0])

        if ng:
            @pl.when(pl.program_id(0) == last)
            def _():
                _gather_finish(shard_refs, full_refs, *sems)

    row = pl.BlockSpec((tm, D), lambda i: (i, 0))
    vec = pl.BlockSpec((1, D), lambda i: (0, 0))
    hbm = pl.BlockSpec(memory_space=pl.ANY)
    ins = [a] + ([res] if has_res else []) + [g, b] + list(gather)
    return pl.pallas_call(
        body, name=name, grid=(T // tm,),
        in_specs=[row] * (2 if has_res else 1) + [vec, vec] + [hbm] * ng,
        out_specs=[row, row] + [_residue_spec(tm, d, D) for d in dilations] + [hbm] * ng,
        out_shape=[jax.ShapeDtypeStruct((T, D), F32), jax.ShapeDtypeStruct((T, D), BF16)]
        + [jax.ShapeDtypeStruct((d, T // d, D), BF16) for d in dilations]
        + [jax.ShapeDtypeStruct((N_DEV,) + s.shape, s.dtype) for s in gather],
        scratch_shapes=([_residue_scratch(tm, D)] if nd else []) + (_gather_scratch(ng) if ng else []),
        compiler_params=_params(("arbitrary",) if ng else ("parallel",)),
    )(*ins)


def ln_bwd(a, res, g, b, d1, d2, tgt, name, by_residue=()):
    T, D = a.shape
    tm = _pick(T, 256, 8)
    has_res = res is not None
    loss_mode = tgt is not None
    nres = len(by_residue)

    def body(*refs):
        refs = list(refs)
        a_ref = refs.pop(0)
        r_ref = refs.pop(0) if has_res else None
        g_ref = refs.pop(0)
        b_ref = refs.pop(0)
        if loss_mode:
            t_ref = refs.pop(0)
        else:
            d1_ref = refs.pop(0)
            d2_ref = refs.pop(0)
        e_refs = [refs.pop(0) for _ in range(nres)]
        dr_ref, drb_ref, dg_ref, db_ref, ds_ref, loss_ref = refs[:6]
        i = pl.program_id(0)

        @pl.when(i == 0)
        def _():
            dg_ref[...] = jnp.zeros_like(dg_ref)
            db_ref[...] = jnp.zeros_like(db_ref)
            ds_ref[...] = jnp.zeros_like(ds_ref)
            loss_ref[...] = jnp.zeros_like(loss_ref)

        r = a_ref[...]
        if has_res:
            r = ALPHA * r + r_ref[...]
        xc, rstd = _ln_stats(r)
        xhat = xc * rstd
        gam = g_ref[...]
        if loss_mode:
            err = xhat * gam + b_ref[...] - t_ref[...]
            dy = err * (1.0 / D)
            row_loss = jnp.mean(err * err, -1, keepdims=True)
            loss_ref[...] += 0.5 * jnp.sum(row_loss, 0, keepdims=True)
        else:
            dy = ALPHA * d1_ref[...] + d2_ref[...]
        for (_, d), e_ref in zip(by_residue, e_refs):
            dy = dy + _load_natural(e_ref, d, refs[-1])
        dyg = dy * gam
        c1 = jnp.mean(dyg, -1, keepdims=True)
        c2 = jnp.mean(dyg * xhat, -1, keepdims=True)
        dr = rstd * (dyg - c1 - xhat * c2)
        dr_ref[...] = dr
        drb_ref[...] = dr.astype(BF16)
        dg_ref[...] += jnp.sum(dy * xhat, 0, keepdims=True)
        db_ref[...] += jnp.sum(dy, 0, keepdims=True)
        ds_ref[...] += jnp.sum(dr, 0, keepdims=True)

    row = pl.BlockSpec((tm, D), lambda i: (i, 0))
    vec = pl.BlockSpec((1, D), lambda i: (0, 0))
    one = pl.BlockSpec((1, 1), lambda i: (0, 0))
    ins = [a] + ([res] if has_res else []) + [g, b] + ([tgt] if loss_mode else [d1, d2]) + [e for e, _ in by_residue]
    in_specs = [row] * (2 if has_res else 1) + [vec, vec] + [row] * (1 if loss_mode else 2)
    in_specs += [_residue_spec(tm, d, D) for _, d in by_residue]
    return pl.pallas_call(
        body, name=name, grid=(T // tm,),
        in_specs=in_specs,
        out_specs=[row, row, vec, vec, vec, one],
        out_shape=[jax.ShapeDtypeStruct((T, D), F32), jax.ShapeDtypeStruct((T, D), BF16),
                   jax.ShapeDtypeStruct((1, D), F32), jax.ShapeDtypeStruct((1, D), F32),
                   jax.ShapeDtypeStruct((1, D), F32), jax.ShapeDtypeStruct((1, 1), F32)],
        scratch_shapes=[_residue_scratch(tm, D)] if nres else [],
        compiler_params=_params(("arbitrary",)),
    )(*ins)


_TOKEN_SPEC = pl.BlockSpec((8, LANES), lambda i: (0, 0))


def mm_nn(a, w, bias, out_dtype, name, after=None):
    M, K = a.shape
    N = w.shape[1]
    tm = _pick(M, max(256, min(1024, OUT_TILE_BYTES // (N * jnp.dtype(out_dtype).itemsize))), 8)
    tc = _pick(N, 512)

    def body(a_ref, w_ref, b_ref, *rest):
        o_ref = rest[-1]
        av = a_ref[...]
        for j in range(N // tc):
            cols = slice(j * tc, (j + 1) * tc)
            acc = jnp.dot(av, w_ref[:, cols], preferred_element_type=F32)
            o_ref[:, cols] = (acc + b_ref[:, cols]).astype(out_dtype)

    return pl.pallas_call(
        body, name=name, grid=(M // tm,),
        in_specs=[pl.BlockSpec((tm, K), lambda i: (i, 0)),
                  pl.BlockSpec((K, N), lambda i: (0, 0)),
                  pl.BlockSpec((1, N), lambda i: (0, 0))] + ([] if after is None else [_TOKEN_SPEC]),
        out_specs=pl.BlockSpec((tm, N), lambda i: (i, 0)),
        out_shape=jax.ShapeDtypeStruct((M, N), out_dtype),
        compiler_params=_params(("parallel",)),
    )(a, w, bias, *([] if after is None else [after]))


def mm_nt(a, w, acc_in, name, after=None, w_block=0, out_dtype=F32):
    pieces = list(a) if isinstance(a, (list, tuple)) else [a]
    M = pieces[0].shape[0]
    widths = [p.shape[1] for p in pieces]
    K = sum(widths)
    N = w.shape[0]
    tm = _pick(M, 512, 8)
    tc = _pick(N, 512)
    has_acc = acc_in is not None
    n_a = len(pieces)

    def body(*refs):
        a_refs, w_ref = refs[:n_a], refs[n_a]
        c_ref = refs[n_a + 1] if has_acc else None
        o_ref = refs[-1]
        av = a_refs[0][...] if n_a == 1 else jnp.concatenate([r[...] for r in a_refs], axis=1)
        for j in range(N // tc):
            cols = slice(j * tc, (j + 1) * tc)
            acc = lax.dot_general(av, w_ref[cols, :], NT_DIMS, preferred_element_type=F32)
            if has_acc:
                acc = acc + c_ref[:, cols]
            o_ref[:, cols] = acc.astype(out_dtype)

    out_spec = pl.BlockSpec((tm, N), lambda i: (i, 0))
    in_specs = [pl.BlockSpec((tm, kw), lambda i: (i, 0)) for kw in widths]
    in_specs.append(pl.BlockSpec((N, K), lambda i: (0, w_block)))
    ins = pieces + [w]
    if has_acc:
        in_specs.append(out_spec)
        ins.append(acc_in)
    if after is not None:
        in_specs.append(_TOKEN_SPEC)
        ins.append(after)
    return pl.pallas_call(
        body, name=name, grid=(M // tm,),
        in_specs=in_specs, out_specs=out_spec,
        out_shape=jax.ShapeDtypeStruct((M, N), out_dtype),
        compiler_params=_params(("parallel",)),
    )(*ins)


def mm_tn(a, b, name, out_dtype=BF16):
    pieces = list(b) if isinstance(b, (list, tuple)) else [b]
    T, M = a.shape
    widths = [p.shape[1] for p in pieces]
    N = sum(widths)
    tk = _pick(T, 512, 8)
    nk = T // tk
    tc = _pick(M, 256)
    n_b = len(pieces)

    def body(*refs):
        a_ref, b_refs = refs[0], refs[1:1 + n_b]
        o_ref, cs_ref, acc_ref = refs[1 + n_b:]
        k = pl.program_id(0)

        @pl.when(k == 0)
        def _():
            acc_ref[...] = jnp.zeros_like(acc_ref)
            cs_ref[...] = jnp.zeros_like(cs_ref)

        bv = b_refs[0][...] if n_b == 1 else jnp.concatenate([r[...] for r in b_refs], axis=1)
        cs_ref[...] += jnp.sum(bv.astype(F32), 0, keepdims=True)
        for mi in range(M // tc):
            rows = slice(mi * tc, (mi + 1) * tc)
            acc_ref[rows, :] += lax.dot_general(a_ref[:, rows], bv, TN_DIMS, preferred_element_type=F32)

        @pl.when(k == nk - 1)
        def _():
            o_ref[...] = acc_ref[...].astype(out_dtype)

    return pl.pallas_call(
        body, name=name, grid=(nk,),
        in_specs=[pl.BlockSpec((tk, M), lambda k: (k, 0))] + [pl.BlockSpec((tk, wd), lambda k: (k, 0)) for wd in widths],
        out_specs=[pl.BlockSpec((M, N), lambda k: (0, 0)), pl.BlockSpec((1, N), lambda k: (0, 0))],
        out_shape=[jax.ShapeDtypeStruct((M, N), out_dtype), jax.ShapeDtypeStruct((1, N), F32)],
        scratch_shapes=[pltpu.VMEM((M, N), F32)],
        compiler_params=_params(("arbitrary",)),
    )(a, *pieces)


def _ext_rows(prev_ref, main_ref, next_ref, i, tm, T):
    before = jnp.where(i == 0, 0.0, prev_ref[...])
    after = jnp.where(i == T // tm - 1, 0.0, next_ref[...])
    return jnp.concatenate([before, main_ref[...], after], axis=0).astype(F32)


def _prev_row(x):
    return pltpu.roll(x, 1, 0)


def _next_row(x):
    return pltpu.roll(x, x.shape[0] - 1, 0)


def _conv3(u, w_ref):
    return _prev_row(u) * w_ref[0:1, :] + u * w_ref[1:2, :] + _next_row(u) * w_ref[2:3, :]


def _main(x, tm, halo=HALO):
    return x[halo:halo + tm]


def _halo_specs(tm, tc, T, col, order, halo=HALO):
    r = tm // halo
    last = T // halo - 1
    if order == "ij":
        return (pl.BlockSpec((halo, tc), lambda i, j: (jnp.maximum(i * r - 1, 0), col(j))),
                pl.BlockSpec((tm, tc), lambda i, j: (i, col(j))),
                pl.BlockSpec((halo, tc), lambda i, j: (jnp.minimum((i + 1) * r, last), col(j))))
    return (pl.BlockSpec((halo, tc), lambda j, i: (jnp.maximum(i * r - 1, 0), col(j))),
            pl.BlockSpec((tm, tc), lambda j, i: (i, col(j))),
            pl.BlockSpec((halo, tc), lambda j, i: (jnp.minimum((i + 1) * r, last), col(j))))


def conv_a_fwd(proj_a, conv_w, name):
    T, D3 = proj_a.shape
    D = D3 // 3
    tm = _pick(T, 256, 8)

    def body(p_ref, m_ref, n_ref, w_ref, o_ref):
        i = pl.program_id(0)
        ext = _ext_rows(p_ref, m_ref, n_ref, i, tm, T)
        u = ext[:, D:2 * D] * ext[:, 2 * D:]
        cu = _conv3(u, w_ref)
        o_ref[...] = (m_ref[:, :D].astype(F32) * _main(cu, tm, HALO_BF16)).astype(BF16)

    prev, main, nxt = _halo_specs(tm, D3, T, lambda j: 0, "ij", HALO_BF16)
    return pl.pallas_call(
        body, name=name, grid=(T // tm, 1),
        in_specs=[prev, main, nxt, pl.BlockSpec((3, D), lambda i, j: (0, 0))],
        out_specs=pl.BlockSpec((tm, D), lambda i, j: (i, 0)),
        out_shape=jax.ShapeDtypeStruct((T, D), BF16),
        compiler_params=_params(("parallel", "arbitrary")),
    )(proj_a, proj_a, proj_a, conv_w)


def conv_a_bwd(ds_a, proj_a, conv_w, name):
    T, D3 = proj_a.shape
    D = D3 // 3
    tm = _pick(T, 256, 8)

    def body(dp_ref, dm_ref, dn_ref, p_ref, m_ref, n_ref, w_ref, o_ref, dw_ref):
        i = pl.program_id(0)

        @pl.when(i == 0)
        def _():
            dw_ref[...] = jnp.zeros_like(dw_ref)

        ext = _ext_rows(p_ref, m_ref, n_ref, i, tm, T)
        dsa = _ext_rows(dp_ref, dm_ref, dn_ref, i, tm, T)
        gb, gc, hin = ext[:, :D], ext[:, D:2 * D], ext[:, 2 * D:]
        u = gc * hin
        u_prev, u_next = _prev_row(u), _next_row(u)
        cu = u_prev * w_ref[0:1, :] + u * w_ref[1:2, :] + u_next * w_ref[2:3, :]
        dcu = dsa * gb
        du = _next_row(dcu) * w_ref[0:1, :] + dcu * w_ref[1:2, :] + _prev_row(dcu) * w_ref[2:3, :]
        h = HALO_BF16
        o_ref[:, :D] = _main(dsa * cu, tm, h).astype(BF16)
        o_ref[:, D:2 * D] = _main(du * hin, tm, h).astype(BF16)
        o_ref[:, 2 * D:] = _main(du * gc, tm, h).astype(BF16)
        dcu_m = _main(dcu, tm, h)
        dw_ref[0:1, :] += jnp.sum(dcu_m * _main(u_prev, tm, h), 0, keepdims=True)
        dw_ref[1:2, :] += jnp.sum(dcu_m * _main(u, tm, h), 0, keepdims=True)
        dw_ref[2:3, :] += jnp.sum(dcu_m * _main(u_next, tm, h), 0, keepdims=True)

    dprev, dmain, dnxt = _halo_specs(tm, D, T, lambda j: 0, "ij", HALO_BF16)
    prev, main, nxt = _halo_specs(tm, D3, T, lambda j: 0, "ij", HALO_BF16)
    return pl.pallas_call(
        body, name=name, grid=(T // tm, 1),
        in_specs=[dprev, dmain, dnxt, prev, main, nxt, pl.BlockSpec((3, D), lambda i, j: (0, 0))],
        out_specs=[pl.BlockSpec((tm, D3), lambda i, j: (i, 0)), pl.BlockSpec((3, D), lambda i, j: (0, 0))],
        out_shape=[jax.ShapeDtypeStruct((T, D3), BF16), jax.ShapeDtypeStruct((3, D), F32)],
        compiler_params=_params(("arbitrary", "arbitrary")),
    )(ds_a, ds_a, ds_a, proj_a, proj_a, proj_a, conv_w)


_INV_SQRT2 = 1.0 / math.sqrt(2.0)
_INV_SQRT_2PI = 1.0 / math.sqrt(2.0 * math.pi)


def conv_f_fwd(up, fcw, fcb, name):
    T, F2 = up.shape
    F = F2 // 2
    tm = _pick(T, 256, 8)
    tc = _pick(F, 1408)
    nc = F // tc

    def body(p_ref, m_ref, n_ref, g_ref, w_ref, b_ref, o_ref):
        i = pl.program_id(0)
        a = _ext_rows(p_ref, m_ref, n_ref, i, tm, T)
        ca = _main(_conv3(a, w_ref), tm) + b_ref[...]
        gl = 0.5 * ca * (1.0 + lax.erf(ca * _INV_SQRT2))
        o_ref[...] = (gl * g_ref[...]).astype(BF16)

    prev, main, nxt = _halo_specs(tm, tc, T, lambda j: j, "ij")
    return pl.pallas_call(
        body, name=name, grid=(T // tm, nc),
        in_specs=[prev, main, nxt,
                  pl.BlockSpec((tm, tc), lambda i, j: (i, nc + j)),
                  pl.BlockSpec((3, tc), lambda i, j: (0, j)),
                  pl.BlockSpec((1, tc), lambda i, j: (0, j))],
        out_specs=pl.BlockSpec((tm, tc), lambda i, j: (i, j)),
        out_shape=jax.ShapeDtypeStruct((T, F), BF16),
        compiler_params=_params(("parallel", "parallel")),
    )(up, up, up, up, fcw, fcb)


def conv_f_bwd(df, up, fcw, fcb, name):
    T, F2 = up.shape
    F = F2 // 2
    tm = _pick(T, 256, 8)
    tc = _pick(F, 1408)
    nc = F // tc

    def body(fp_ref, fm_ref, fn_ref, ap_ref, am_ref, an_ref, gp_ref, gm_ref, gn_ref, w_ref, b_ref,
             da_ref, dg_ref, csa_ref, csg_ref, dfb_ref, dfw_ref):
        i = pl.program_id(1)

        @pl.when(i == 0)
        def _():
            csa_ref[...] = jnp.zeros_like(csa_ref)
            csg_ref[...] = jnp.zeros_like(csg_ref)
            dfb_ref[...] = jnp.zeros_like(dfb_ref)
            dfw_ref[...] = jnp.zeros_like(dfw_ref)

        dfe = _ext_rows(fp_ref, fm_ref, fn_ref, i, tm, T)
        a = _ext_rows(ap_ref, am_ref, an_ref, i, tm, T)
        gate = _ext_rows(gp_ref, gm_ref, gn_ref, i, tm, T)
        a_prev, a_next = _prev_row(a), _next_row(a)
        ca = a_prev * w_ref[0:1, :] + a * w_ref[1:2, :] + a_next * w_ref[2:3, :] + b_ref[...]
        cdf = 0.5 * (1.0 + lax.erf(ca * _INV_SQRT2))
        gl = ca * cdf
        gp = cdf + ca * (jnp.exp(-0.5 * ca * ca) * _INV_SQRT_2PI)
        dgate = _main(dfe * gl, tm)
        dca = dfe * gate * gp
        da = _main(_next_row(dca) * w_ref[0:1, :] + dca * w_ref[1:2, :] + _prev_row(dca) * w_ref[2:3, :], tm)
        da_ref[...] = da.astype(BF16)
        dg_ref[...] = dgate.astype(BF16)
        csa_ref[...] += jnp.sum(da, 0, keepdims=True)
        csg_ref[...] += jnp.sum(dgate, 0, keepdims=True)
        dca_m = _main(dca, tm)
        dfb_ref[...] += jnp.sum(dca_m, 0, keepdims=True)
        dfw_ref[0:1, :] += jnp.sum(dca_m * _main(a_prev, tm), 0, keepdims=True)
        dfw_ref[1:2, :] += jnp.sum(dca_m * _main(a, tm), 0, keepdims=True)
        dfw_ref[2:3, :] += jnp.sum(dca_m * _main(a_next, tm), 0, keepdims=True)

    fprev, fmain, fnxt = _halo_specs(tm, tc, T, lambda j: j, "ji")
    gprev, gmain, gnxt = _halo_specs(tm, tc, T, lambda j: nc + j, "ji")
    tile = pl.BlockSpec((tm, tc), lambda j, i: (i, j))
    vec = pl.BlockSpec((1, tc), lambda j, i: (0, j))
    vec3 = pl.BlockSpec((3, tc), lambda j, i: (0, j))
    return pl.pallas_call(
        body, name=name, grid=(nc, T // tm),
        in_specs=[fprev, fmain, fnxt, fprev, fmain, fnxt, gprev, gmain, gnxt, vec3, vec],
        out_specs=[tile, tile, vec, vec, vec, vec3],
        out_shape=[jax.ShapeDtypeStruct((T, F), BF16), jax.ShapeDtypeStruct((T, F), BF16),
                   jax.ShapeDtypeStruct((1, F), F32), jax.ShapeDtypeStruct((1, F), F32),
                   jax.ShapeDtypeStruct((1, F), F32), jax.ShapeDtypeStruct((3, F), F32)],
        compiler_params=_params(("arbitrary", "arbitrary")),
    )(df, df, df, up, up, up, up, up, up, fcw, fcb)


def gate_fwd(proj_g, y_a, y_b, name):
    T, D = y_a.shape
    tm = _pick(T, 512, 8)

    def body(g_ref, a_ref, b_ref, o_ref):
        sa = jax.nn.sigmoid(g_ref[:, :D].astype(F32))
        sb = jax.nn.sigmoid(g_ref[:, D:].astype(F32))
        o_ref[...] = (sa * a_ref[...].astype(F32) + sb * b_ref[...].astype(F32)).astype(BF16)

    row = pl.BlockSpec((tm, D), lambda i: (i, 0))
    return pl.pallas_call(
        body, name=name, grid=(T // tm,),
        in_specs=[pl.BlockSpec((tm, 2 * D), lambda i: (i, 0)), row, row],
        out_specs=row,
        out_shape=jax.ShapeDtypeStruct((T, D), BF16),
        compiler_params=_params(("parallel",)),
    )(proj_g, y_a, y_b)


def gate_bwd(dz, proj_g, y_a, y_b, name):
    T, D = y_a.shape
    tm = _pick(T, 512, 8)

    def body(dz_ref, g_ref, a_ref, b_ref, da_ref, db_ref, dg_ref):
        dzv = dz_ref[...].astype(F32)
        sa = jax.nn.sigmoid(g_ref[:, :D].astype(F32))
        sb = jax.nn.sigmoid(g_ref[:, D:].astype(F32))
        da_ref[...] = (dzv * sa).astype(BF16)
        db_ref[...] = (dzv * sb).astype(BF16)
        dg_ref[:, :D] = (dzv * a_ref[...].astype(F32) * (sa * (1.0 - sa))).astype(BF16)
        dg_ref[:, D:] = (dzv * b_ref[...].astype(F32) * (sb * (1.0 - sb))).astype(BF16)

    row = pl.BlockSpec((tm, D), lambda i: (i, 0))
    wide = pl.BlockSpec((tm, 2 * D), lambda i: (i, 0))
    return pl.pallas_call(
        body, name=name, grid=(T // tm,),
        in_specs=[row, wide, row, row],
        out_specs=[row, row, wide],
        out_shape=[jax.ShapeDtypeStruct((T, D), BF16), jax.ShapeDtypeStruct((T, D), BF16),
                   jax.ShapeDtypeStruct((T, 2 * D), BF16)],
        compiler_params=_params(("parallel",)),
    )(dz, proj_g, y_a, y_b)


ATT_WIN = ATT_TQ + 2 * RADIUS
ATT_STEP = 512
FAR = 1e32


def _att_window(qs, L):
    ks = pl.multiple_of(jnp.clip(qs - RADIUS, 0, L - ATT_WIN), RADIUS)
    return ks, jnp.where(qs == 0, 0, jnp.where(qs == L - ATT_TQ, 2, 1))


def _fill_bias_tables(bias_ref, sl_ref, hp, d):
    col_row = (lax.broadcasted_iota(jnp.int32, (ATT_TQ, ATT_WIN), 1)
               - lax.broadcasted_iota(jnp.int32, (ATT_TQ, ATT_WIN), 0))
    for v in range(3):
        ad = jnp.abs(col_row - v * RADIUS)
        dist = jnp.where(ad <= RADIUS, (ad * d).astype(F32), FAR)
        bias_ref[v, 0:ATT_TQ, :] = sl_ref[hp * 2] * dist
        bias_ref[v, ATT_TQ:2 * ATT_TQ, :] = sl_ref[hp * 2 + 1] * dist


def _head_masks():
    lane = lax.broadcasted_iota(jnp.int32, (1, LANES), 1)
    return [lane < HEAD_DIM, lane >= HEAD_DIM]


def _stack_heads(x, masks):
    zero = jnp.zeros_like(x)
    return jnp.concatenate([jnp.where(masks[0], x, zero), jnp.where(masks[1], x, zero)], axis=0)


def _unstack_heads(x2, masks):
    n = x2.shape[0] // 2
    return jnp.where(masks[0], x2[:n], x2[n:])


def _att_step(L):
    step = min(ATT_STEP, L)
    assert L % step == 0 and step % ATT_TQ == 0 and L >= ATT_WIN
    return step


def att_fwd(qkv, group, name):
    d, L, _ = qkv.shape
    step = _att_step(L)
    cg = GROUP_W // LANES
    slopes = jnp.asarray(_alibi_slopes()[group])
    scale = HEAD_DIM ** -0.5

    def body(sl_ref, q_ref, k_ref, v_ref, o_ref, l_ref, bias_ref, s_ref, p_ref):
        hp = pl.program_id(1)
        i = pl.program_id(2)

        @pl.when(i == 0)
        def _():
            _fill_bias_tables(bias_ref, sl_ref, hp, d)

        masks = _head_masks()
        tiles = range(step // ATT_TQ)
        windows = [_att_window(i * step + t * ATT_TQ, L) for t in tiles]
        for t in tiles:
            rows = slice(t * ATT_TQ, (t + 1) * ATT_TQ)
            ks, table = windows[t]
            q2 = _stack_heads(q_ref[rows, :] * scale, masks)
            kw = k_ref[pl.ds(ks, ATT_WIN), :]
            s_ref[t] = lax.dot_general(q2, kw, NT_DIMS, preferred_element_type=F32) - bias_ref[table]
        for t in tiles:
            rows = slice(t * ATT_TQ, (t + 1) * ATT_TQ)
            s = s_ref[t]
            m = jnp.max(s, -1, keepdims=True)
            p = jnp.exp(s - m)
            den = jnp.sum(p, -1, keepdims=True)
            p_ref[t] = (p / den).astype(BF16)
            l_ref[rows, :] = _unstack_heads(m + jnp.log(den), masks)
        for t in tiles:
            rows = slice(t * ATT_TQ, (t + 1) * ATT_TQ)
            vw = v_ref[pl.ds(windows[t][0], ATT_WIN), :]
            o2 = jnp.dot(p_ref[t], vw, preferred_element_type=F32)
            o_ref[rows, :] = _unstack_heads(o2, masks)

    n_tiles = step // ATT_TQ
    out_spec = pl.BlockSpec((None, step, LANES), lambda r, hp, i: (r, i, hp))
    return pl.pallas_call(
        body, name=name, grid=(d, cg, L // step),
        in_specs=[pl.BlockSpec(memory_space=pltpu.SMEM),
                  pl.BlockSpec((None, step, LANES), lambda r, hp, i: (r, i, hp)),
                  pl.BlockSpec((None, L, LANES), lambda r, hp, i: (r, 0, cg + hp)),
                  pl.BlockSpec((None, L, LANES), lambda r, hp, i: (r, 0, 2 * cg + hp))],
        out_specs=[out_spec, out_spec],
        out_shape=[jax.ShapeDtypeStruct((d, L, GROUP_W), F32)] * 2,
        scratch_shapes=[pltpu.VMEM((3, 2 * ATT_TQ, ATT_WIN), F32),
                        pltpu.VMEM((n_tiles, 2 * ATT_TQ, ATT_WIN), F32),
                        pltpu.VMEM((n_tiles, 2 * ATT_TQ, ATT_WIN), BF16)],
        compiler_params=_params(("arbitrary", "arbitrary", "arbitrary")),
    )(slopes, qkv, qkv, qkv)


def att_bwd(qkv, do, lse, dmat, group, name):
    d, L, _ = qkv.shape
    step = _att_step(L)
    nq = L // step
    cg = GROUP_W // LANES
    slopes = jnp.asarray(_alibi_slopes()[group])
    scale = HEAD_DIM ** -0.5

    def body(sl_ref, q_ref, k_ref, v_ref, do_ref, l_ref, dm_ref, dq_ref, dk_ref, dv_ref, dk_acc, dv_acc, bias_ref,
             s_ref, dp_ref, p_ref, ds_ref):
        hp = pl.program_id(1)
        i = pl.program_id(2)

        @pl.when(i == 0)
        def _():
            dk_acc[...] = jnp.zeros_like(dk_acc)
            dv_acc[...] = jnp.zeros_like(dv_acc)
            _fill_bias_tables(bias_ref, sl_ref, hp, d)

        masks = _head_masks()

        def head_cols(x):
            return jnp.concatenate([jnp.max(jnp.where(hm, x, -jnp.inf), -1, keepdims=True) for hm in masks], axis=0)

        tiles = range(step // ATT_TQ)
        windows = [_att_window(i * step + t * ATT_TQ, L) for t in tiles]

        def stacked(ref, t, factor=None):
            x = ref[t * ATT_TQ:(t + 1) * ATT_TQ, :]
            return _stack_heads(x if factor is None else x * factor, masks)

        for t in tiles:
            ks, table = windows[t]
            q2 = stacked(q_ref, t, scale)
            s_ref[t] = lax.dot_general(q2, k_ref[pl.ds(ks, ATT_WIN), :], NT_DIMS,
                                       preferred_element_type=F32) - bias_ref[table]
            dp_ref[t] = lax.dot_general(stacked(do_ref, t), v_ref[pl.ds(ks, ATT_WIN), :], NT_DIMS,
                                        preferred_element_type=F32)
        for t in tiles:
            rows = slice(t * ATT_TQ, (t + 1) * ATT_TQ)
            p = jnp.exp(s_ref[t] - head_cols(l_ref[rows, :]))
            p_ref[t] = p.astype(BF16)
            ds_ref[t] = (p * (dp_ref[t] - head_cols(dm_ref[rows, :]))).astype(BF16)
        for t in tiles:
            rows = slice(t * ATT_TQ, (t + 1) * ATT_TQ)
            ks = windows[t][0]
            ds = ds_ref[t]
            dq2 = jnp.dot(ds, k_ref[pl.ds(ks, ATT_WIN), :], preferred_element_type=F32)
            dq_ref[rows, :] = (_unstack_heads(dq2, masks) * scale).astype(BF16)
            dk_acc[pl.ds(ks, ATT_WIN), :] += lax.dot_general(ds, stacked(q_ref, t, scale), TN_DIMS,
                                                             preferred_element_type=F32)
            dv_acc[pl.ds(ks, ATT_WIN), :] += lax.dot_general(p_ref[t], stacked(do_ref, t), TN_DIMS,
                                                             preferred_element_type=F32)

        @pl.when(i == nq - 1)
        def _():
            dk_ref[...] = dk_acc[...].astype(BF16)
            dv_ref[...] = dv_acc[...].astype(BF16)

    tile = pl.BlockSpec((None, step, LANES), lambda r, hp, i: (r, i, hp))
    whole = pl.BlockSpec((None, L, LANES), lambda r, hp, i: (r, 0, hp))
    return pl.pallas_call(
        body, name=name, grid=(d, cg, nq),
        in_specs=[pl.BlockSpec(memory_space=pltpu.SMEM), tile,
                  pl.BlockSpec((None, L, LANES), lambda r, hp, i: (r, 0, cg + hp)),
                  pl.BlockSpec((None, L, LANES), lambda r, hp, i: (r, 0, 2 * cg + hp)),
                  tile, tile, tile],
        out_specs=[tile, whole, whole],
        out_shape=[jax.ShapeDtypeStruct((d, L, GROUP_W), BF16)] * 3,
        scratch_shapes=[pltpu.VMEM((L, LANES), F32), pltpu.VMEM((L, LANES), F32),
                        pltpu.VMEM((3, 2 * ATT_TQ, ATT_WIN), F32)]
        + [pltpu.VMEM((step // ATT_TQ, 2 * ATT_TQ, ATT_WIN), dt) for dt in (F32, F32, BF16, BF16)],
        compiler_params=_params(("arbitrary", "arbitrary", "arbitrary")),
    )(slopes, qkv, qkv, qkv, do, lse, dmat)


def _group_weights(ls):
    m = jnp.maximum(jnp.maximum(ls[0], ls[1]), ls[2])
    es = [jnp.exp(l - m) for l in ls]
    tot = es[0] + es[1] + es[2]
    return [e / tot for e in es]


def combine_fwd(outs, lses, name):
    T = outs[0].shape[0] * outs[0].shape[1]
    tm = _pick(T, 512, 8)
    n_scr = 2 * (len(DILATIONS) - 1)

    def body(*refs):
        o_refs, l_refs, c_ref, scr = refs[:3], refs[3:6], refs[6], refs[7:]
        o = [_load_natural(o_refs[g], d, scr[g - 1] if g else None) for g, d in enumerate(DILATIONS)]
        l = [_load_natural(l_refs[g], d, scr[g + 1] if g else None) for g, d in enumerate(DILATIONS)]
        w = _group_weights(l)
        c_ref[...] = (w[0] * o[0] + w[1] * o[1] + w[2] * o[2]).astype(BF16)

    specs = [_residue_spec(tm, d, GROUP_W) for d in DILATIONS]
    return pl.pallas_call(
        body, name=name, grid=(T // tm,),
        in_specs=specs + specs, out_specs=pl.BlockSpec((tm, GROUP_W), lambda i: (i, 0)),
        out_shape=jax.ShapeDtypeStruct((T, GROUP_W), BF16),
        scratch_shapes=[_residue_scratch(tm, GROUP_W)] * n_scr,
        compiler_params=_params(("parallel",)),
    )(*outs, *lses)


def combine_bwd(dcomb, outs, lses, name):
    T = dcomb.shape[0]
    tm = _pick(T, 256, 8)
    head = np.arange(GROUP_W) // HEAD_DIM
    seg = jnp.asarray((head[:, None] == head[None, :]).astype(np.float32)).astype(BF16)
    ng = len(DILATIONS)
    n_scr = 4 * (ng - 1)

    def body(*refs):
        dc_ref, o_refs, l_refs, e_ref = refs[0], refs[1:1 + ng], refs[1 + ng:1 + 2 * ng], refs[1 + 2 * ng]
        do_refs, dm_refs = refs[2 + 2 * ng:2 + 3 * ng], refs[2 + 3 * ng:2 + 4 * ng]
        scr = refs[2 + 4 * ng:]
        o = [_load_natural(o_refs[g], d, scr[4 * (g - 1)] if g else None) for g, d in enumerate(DILATIONS)]
        l = [_load_natural(l_refs[g], d, scr[4 * (g - 1) + 1] if g else None) for g, d in enumerate(DILATIONS)]
        w = _group_weights(l)
        dc = dc_ref[...].astype(F32)
        e = e_ref[...]
        tot = jnp.zeros_like(dc)
        for g in range(ng):
            prod = dc * o[g]
            dw = jnp.zeros_like(dc)
            for _ in range(3):
                part = prod.astype(BF16)
                dw = dw + jnp.dot(part, e, preferred_element_type=F32)
                prod = prod - part.astype(F32)
            tot = tot + w[g] * dw
        for g, d in enumerate(DILATIONS):
            _store_by_residue(w[g] * dc, do_refs[g], d, scr[4 * (g - 1) + 2] if g else None)
            _store_by_residue(w[g] * tot, dm_refs[g], d, scr[4 * (g - 1) + 3] if g else None)

    specs = [_residue_spec(tm, d, GROUP_W) for d in DILATIONS]
    res = pl.pallas_call(
        body, name=name, grid=(T // tm,),
        in_specs=[pl.BlockSpec((tm, GROUP_W), lambda i: (i, 0))] + specs + specs
        + [pl.BlockSpec((GROUP_W, GROUP_W), lambda i: (0, 0))],
        out_specs=specs + specs,
        out_shape=[jax.ShapeDtypeStruct(o.shape, BF16) for o in outs] + [jax.ShapeDtypeStruct(o.shape, F32) for o in outs],
        scratch_shapes=[_residue_scratch(tm, GROUP_W)] * n_scr,
        compiler_params=_params(("parallel",)),
    )(dcomb, *outs, *lses, seg)
    return res[:ng], res[ng:]


def _position():
    return lax.axis_index("x"), lax.axis_index("y"), lax.axis_index("c")


def _other_chips(x, y):
    return [(1 - x, y), (x, 1 - y), (1 - x, 1 - y)]


def _remote(src, dst, send_sems, recv_sems, k, to):
    return pltpu.make_async_remote_copy(src_ref=src, dst_ref=dst, send_sem=send_sems.at[k], recv_sem=recv_sems.at[k],
                                        device_id=to, device_id_type=MESH)


def _gather_descriptors(ins, outs, send_sems, recv_sems, local_sems):
    n = len(ins)
    x, y, c = _position()
    sibling = (x, y, 1 - c)
    chips = _other_chips(x, y)

    def block(a, px, py, pc):
        return outs[a].at[4 * px + 2 * py + pc]

    own, first, arrivals = [], [], []
    for a in range(n):
        k0 = 7 * a
        mine = block(a, x, y, c)
        own.append(pltpu.make_async_copy(ins[a], mine, local_sems.at[a]))
        first.append(_remote(ins[a], mine, send_sems, recv_sems, k0, sibling))
        row = []
        for j, chip in enumerate(chips):
            first.append(_remote(ins[a], mine, send_sems, recv_sems, k0 + 1 + j, (*chip, c)))
            got = block(a, *chip, c)
            row.append((_remote(got, got, send_sems, recv_sems, k0 + 1 + j, sibling),
                        _remote(got, got, send_sems, recv_sems, k0 + 4 + j, sibling)))
        arrivals.append(row)
    return own, first, arrivals


def _gather_begin(ins, outs, send_sems, recv_sems, local_sems):
    own, first, _ = _gather_descriptors(ins, outs, send_sems, recv_sems, local_sems)
    for cp in own + first:
        cp.start()


def _gather_finish(ins, outs, send_sems, recv_sems, local_sems):
    own, first, arrivals = _gather_descriptors(ins, outs, send_sems, recv_sems, local_sems)
    passed = []
    for row in arrivals:
        for arrived, onward in row:
            arrived.wait_recv()
            onward.start()
            passed.append(onward)
    for a in range(len(ins)):
        first[4 * a].wait_recv()
        for _, onward in arrivals[a]:
            onward.wait_recv()
    for cp in first + passed:
        cp.wait_send()
    for cp in own:
        cp.wait()


def _gather_scratch(n):
    return [pltpu.SemaphoreType.DMA((7 * n,)), pltpu.SemaphoreType.DMA((7 * n,)), pltpu.SemaphoreType.DMA((n,))]


def all_gather(shards, name):
    n = len(shards)

    def body(*refs):
        ins, outs, sems = refs[:n], refs[n:2 * n], refs[2 * n:]
        _gather_begin(ins, outs, *sems)
        _gather_finish(ins, outs, *sems)

    hbm = pl.BlockSpec(memory_space=pl.ANY)
    return pl.pallas_call(
        body, name=name,
        in_specs=[hbm] * n, out_specs=[hbm] * n,
        out_shape=[jax.ShapeDtypeStruct((N_DEV,) + s.shape, s.dtype) for s in shards],
        scratch_shapes=_gather_scratch(n),
    )(*shards)


def exchange_sibling(parts, name):
    n = len(parts)

    def body(*refs):
        ins, outs = refs[:n], refs[n:2 * n]
        send_sems, recv_sems = refs[2 * n:]
        x, y, c = _position()
        sibling = (x, y, 1 - c)
        copies = []
        for a in range(n):
            for q in range(4):
                cp = _remote(ins[a].at[2 * q + (1 - c)], outs[a].at[q], send_sems, recv_sems, 4 * a + q, sibling)
                cp.start()
                copies.append(cp)
        for cp in copies:
            cp.wait_recv()
        for cp in copies:
            cp.wait_send()

    hbm = pl.BlockSpec(memory_space=pl.ANY)
    return pl.pallas_call(
        body, name=name,
        in_specs=[hbm] * n, out_specs=[hbm] * n,
        out_shape=[jax.ShapeDtypeStruct((4,) + p.shape[1:], p.dtype) for p in parts],
        scratch_shapes=[pltpu.SemaphoreType.DMA((4 * n,)), pltpu.SemaphoreType.DMA((4 * n,))],
    )(*parts)


def exchange_chips(sums, name):
    n = len(sums)

    def body(*refs):
        ins, outs = refs[:n], refs[n:2 * n]
        send_sems, recv_sems = refs[2 * n:]
        x, y, c = _position()
        copies = []
        for a in range(n):
            for j, (cx, cy) in enumerate(_other_chips(x, y)):
                cp = _remote(ins[a].at[2 * cx + cy], outs[a].at[j], send_sems, recv_sems, 3 * a + j, (cx, cy, c))
                cp.start()
                copies.append(cp)
        for cp in copies:
            cp.wait_recv()
        for cp in copies:
            cp.wait_send()

    hbm = pl.BlockSpec(memory_space=pl.ANY)
    return pl.pallas_call(
        body, name=name,
        in_specs=[hbm] * n, out_specs=[hbm] * n,
        out_shape=[jax.ShapeDtypeStruct((3,) + s.shape[1:], s.dtype) for s in sums],
        scratch_shapes=[pltpu.SemaphoreType.DMA((3 * n,)), pltpu.SemaphoreType.DMA((3 * n,))],
    )(*sums)


_HBM = pl.BlockSpec(memory_space=pltpu.HBM)
_SEM = pl.BlockSpec(memory_space=pltpu.SEMAPHORE)
_DATAFLOW = pltpu.SideEffectType.DATAFLOW_SIDE_EFFECTING


def _to_all_plan(srcs, lands, send_sems, recv_sems):
    x, y, c = _position()
    me = 4 * x + 2 * y + c
    copies = []
    for a in range(len(srcs)):
        for k in range(1, N_DEV):
            fx, fy, fc = (k >> 2) & 1, (k >> 1) & 1, k & 1
            to = (1 - x if fx else x, 1 - y if fy else y, 1 - c if fc else c)
            copies.append(_remote(srcs[a], lands[a].at[me], send_sems, recv_sems, (N_DEV - 1) * a + k - 1, to))
    return copies


def _to_chips_plan(srcs, lands, send_sems, recv_sems):
    x, y, c = _position()
    copies = []
    for a in range(len(srcs)):
        for j, (cx, cy) in enumerate(_other_chips(x, y)):
            copies.append(_remote(srcs[a].at[2 * cx + cy], lands[a].at[j], send_sems, recv_sems, 3 * a + j, (cx, cy, c)))
    return copies


def copies_start(srcs, land_shapes, plan, per_array, name):
    n = len(srcs)
    n_sem = per_array * n
    lands = [lax.empty(s.shape, s.dtype) for s in land_shapes]

    def body(*refs):
        src_refs, land_refs = refs[:n], refs[n:2 * n]
        send_sems, recv_sems = refs[2 * n], refs[2 * n + 1]
        token = refs[-1]
        for cp in plan(src_refs, land_refs, send_sems, recv_sems):
            cp.start()
        token[...] = jnp.zeros_like(token)

    out = pl.pallas_call(
        body, name=name,
        out_shape=(pltpu.SemaphoreType.DMA((n_sem,)), pltpu.SemaphoreType.DMA((n_sem,)))
        + tuple(pltpu.HBM(s.shape, s.dtype) for s in srcs)
        + tuple(pltpu.HBM(s.shape, s.dtype) for s in land_shapes)
        + (jax.ShapeDtypeStruct((8, LANES), F32),),
        in_specs=[_HBM] * (2 * n),
        out_specs=(_SEM, _SEM) + (_HBM,) * (2 * n) + (pl.BlockSpec(memory_space=pltpu.VMEM),),
        input_output_aliases={i: 2 + i for i in range(2 * n)},
        compiler_params=pltpu.CompilerParams(has_side_effects=_DATAFLOW),
    )(*[pltpu.with_memory_space_constraint(s, pltpu.HBM) for s in srcs],
      *[pltpu.with_memory_space_constraint(l, pltpu.HBM) for l in lands])
    return out[:-1], out[-1]


def copies_wait(handles, plan, after, name):
    send_sems, recv_sems = handles[0], handles[1]
    n = (len(handles) - 2) // 2
    thru = handles[2:]

    def body(*refs):
        src_refs, land_refs = refs[:n], refs[n:2 * n]
        send_sems, recv_sems = refs[2 * n], refs[2 * n + 1]
        copies = plan(src_refs, land_refs, send_sems, recv_sems)
        for cp in copies:
            cp.wait_recv()
        for cp in copies:
            cp.wait_send()

    out = pl.pallas_call(
        body, name=name,
        out_shape=tuple(pltpu.HBM(t.shape, t.dtype) for t in thru),
        in_specs=[_HBM] * (2 * n) + [_SEM, _SEM, pl.BlockSpec(memory_space=pl.ANY)],
        out_specs=(_HBM,) * (2 * n),
        input_output_aliases={i: i for i in range(2 * n)},
        compiler_params=pltpu.CompilerParams(has_side_effects=_DATAFLOW),
    )(*thru, send_sems, recv_sems, after)
    return out[n:]


def all_sum_small(vec, name):
    R = vec.shape[0]

    def body(v_ref, tot_ref, all_ref, send_sems, recv_sems):
        x, y, c = _position()
        me = 4 * x + 2 * y + c
        all_ref[me] = v_ref[...]
        copies = []
        for k in range(1, N_DEV):
            fx, fy, fc = (k >> 2) & 1, (k >> 1) & 1, k & 1
            to = (1 - x if fx else x, 1 - y if fy else y, 1 - c if fc else c)
            cp = _remote(v_ref, all_ref.at[me], send_sems, recv_sems, k - 1, to)
            cp.start()
            copies.append(cp)
        for cp in copies:
            cp.wait_recv()
        for cp in copies:
            cp.wait_send()
        tot = all_ref[0]
        for j in range(1, N_DEV):
            tot = tot + all_ref[j]
        tot_ref[...] = tot

    vmem = pl.BlockSpec(memory_space=pltpu.VMEM)
    return pl.pallas_call(
        body, name=name,
        in_specs=[vmem], out_specs=vmem,
        out_shape=jax.ShapeDtypeStruct((R, LANES), F32),
        scratch_shapes=[pltpu.VMEM((N_DEV, R, LANES), F32),
                        pltpu.SemaphoreType.DMA((N_DEV - 1,)), pltpu.SemaphoreType.DMA((N_DEV - 1,))],
        compiler_params=pltpu.CompilerParams(vmem_limit_bytes=VMEM_LIMIT),
    )(vec)


def pair_add(parts, theirs, place, name):
    _, R, C = theirs.shape
    tr = _pick(R, 256, 8)

    def body(place_ref, a_ref, b_ref, o_ref):
        o_ref[...] = (a_ref[...].astype(F32) + b_ref[...].astype(F32)).astype(BF16)

    blk = pl.BlockSpec((None, tr, C), lambda q, i, place_ref: (q, i, 0))
    return pl.pallas_call(
        body, name=name,
        grid_spec=pltpu.PrefetchScalarGridSpec(
            num_scalar_prefetch=1, grid=(4, R // tr),
            in_specs=[pl.BlockSpec((None, tr, C), lambda q, i, place_ref: (2 * q + place_ref[2], i, 0)), blk],
            out_specs=blk),
        out_shape=jax.ShapeDtypeStruct(theirs.shape, BF16),
        compiler_params=_params(("parallel", "parallel")),
    )(place, parts, theirs)


def _adamw_math(w, g, m, v):
    m = ADAM_B1 * m + (1.0 - ADAM_B1) * g
    v = ADAM_B2 * v + (1.0 - ADAM_B2) * jnp.square(g)
    m_hat = m / (1.0 - ADAM_B1 ** ADAM_STEP)
    v_hat = v / (1.0 - ADAM_B2 ** ADAM_STEP)
    delta = -ADAM_LR * (m_hat / (jnp.sqrt(v_hat) + ADAM_EPS) + ADAM_WD * w)
    return delta, m, v


def adamw_sharded(w, m, v, parts, sib, others, place, name):
    R, C = w.shape
    tr = _pick(R, 256, 8)

    def body(place_ref, w_ref, m_ref, v_ref, a_ref, b_ref, o_ref, g_ref, d_ref, nm_ref, nv_ref):
        g = a_ref[...].astype(F32) + b_ref[...].astype(F32)
        for j in range(3):
            g = g + o_ref[j].astype(F32)
        delta, nm, nv = _adamw_math(w_ref[...], g, m_ref[...], v_ref[...])
        g_ref[...] = g
        d_ref[...] = delta
        nm_ref[...] = nm
        nv_ref[...] = nv

    row = pl.BlockSpec((tr, C), lambda i, place_ref: (i, 0))
    return pl.pallas_call(
        body, name=name,
        grid_spec=pltpu.PrefetchScalarGridSpec(
            num_scalar_prefetch=1, grid=(R // tr,),
            in_specs=[row] * 3 + [pl.BlockSpec((None, tr, C), lambda i, place_ref: (place_ref[0], i, 0)),
                                  pl.BlockSpec((None, tr, C), lambda i, place_ref: (place_ref[1], i, 0)),
                                  pl.BlockSpec((3, tr, C), lambda i, place_ref: (0, i, 0))],
            out_specs=[row] * 4),
        out_shape=[jax.ShapeDtypeStruct((R, C), F32)] * 4,
        compiler_params=_params(("parallel",)),
    )(place, w, m, v, parts, sib, others)


def adamw_packed(w, g, m, v, name):
    R = w.shape[0]

    def body(w_ref, g_ref, m_ref, v_ref, d_ref, nm_ref, nv_ref):
        delta, nm, nv = _adamw_math(w_ref[...], g_ref[...], m_ref[...], v_ref[...])
        d_ref[...] = delta
        nm_ref[...] = nm
        nv_ref[...] = nv

    full = pl.BlockSpec((R, LANES), lambda i: (0, 0))
    return pl.pallas_call(
        body, name=name, grid=(1,),
        in_specs=[full] * 4, out_specs=[full] * 3,
        out_shape=[jax.ShapeDtypeStruct((R, LANES), F32)] * 3,
        compiler_params=_params(("arbitrary",)),
    )(w, g, m, v)


def _pack(arrays):
    flat = []
    sizes = []
    for a in arrays:
        f = a.reshape(-1).astype(F32)
        pad = (-f.shape[0]) % LANES
        if pad:
            f = jnp.concatenate([f, jnp.zeros((pad,), F32)])
        flat.append(f)
        sizes.append(f.shape[0])
    rows = sum(sizes) // LANES
    pad_rows = (-rows) % 8
    if pad_rows:
        flat.append(jnp.zeros((pad_rows * LANES,), F32))
    return jnp.concatenate(flat).reshape(-1, LANES), sizes


def _unpack(packed, sizes, shapes):
    flat = packed.reshape(-1)
    out = []
    off = 0
    for size, shape in zip(sizes, shapes):
        n = int(np.prod(shape))
        out.append(flat[off:off + n].reshape(shape))
        off += size
    return out


def _to_blocks(full, axis):
    if axis == 0:
        return full.reshape(N_DEV, full.shape[0] // N_DEV, full.shape[1])
    r, n = full.shape
    return full.reshape(r, N_DEV, n // N_DEV).transpose(1, 0, 2)


def _from_blocks(blocks, axis):
    if axis == 0:
        return blocks.reshape(blocks.shape[0] * blocks.shape[1], blocks.shape[2])
    return blocks.transpose(1, 0, 2).reshape(blocks.shape[1], blocks.shape[0] * blocks.shape[2])


def kernel(x, ln0_g, ln0_b, w_in, b_in, conv_w, w_a, w_b, w_o, b_o, ln1_g, ln1_b, w_up, b_up, ffn_conv_w, ffn_conv_b, w_down, b_down, ln2_g, ln2_b, loss_target, m_ln0_g, m_ln0_b, m_w_in, m_b_in, m_conv_w, m_w_a, m_w_b, m_w_o, m_b_o, m_ln1_g, m_ln1_b, m_w_up, m_b_up, m_ffn_conv_w, m_ffn_conv_b, m_w_down, m_b_down, m_ln2_g, m_ln2_b, v_ln0_g, v_ln0_b, v_w_in, v_b_in, v_conv_w, v_w_a, v_w_b, v_w_o, v_b_o, v_ln1_g, v_ln1_b, v_w_up, v_b_up, v_ffn_conv_w, v_ffn_conv_b, v_w_down, v_b_down, v_ln2_g, v_ln2_b):
    T, D = x.shape[1], x.shape[2]
    F = ffn_conv_b.shape[-1]
    xs = x.reshape(T, D)
    tgt = loss_target.reshape(T, D)
    dev = 4 * lax.axis_index("x") + 2 * lax.axis_index("y") + lax.axis_index("c")
    chip = 2 * lax.axis_index("x") + lax.axis_index("y")
    core = lax.axis_index("c")
    place = jnp.stack([dev, chip, core]).astype(jnp.int32)

    big = dict(w_in=(w_in[0], 1), w_a=(w_a[0], 0), w_b=(w_b[0], 1), w_o=(w_o[0], 0), w_up=(w_up[0], 1),
               w_down=(w_down[0], 0))
    names = list(big)
    shards = {k: big[k][0].astype(BF16) for k in names}
    ln0g, ln0b = ln0_g.reshape(1, D), ln0_b.reshape(1, D)
    h0, h0b, *rest = ln_fwd(xs, None, ln0g, ln0b, "ln0_fwd_gather_w_in", dilations=DILATIONS[1:],
                            gather=[shards["w_in"], conv_w[0], ffn_conv_w[0]])
    h0_res = [h0b] + [h.reshape(T, D) for h in rest[:2]]
    g_in, g_conv, g_fcw = rest[2:]
    full = {"w_in": _from_blocks(g_in, 1)}
    conv_full = _from_blocks(g_conv, 1)
    fcw_full = _from_blocks(g_fcw, 1)
    late_groups = (("w_a", "w_b", "w_o"), ("w_up", "w_down"))
    late_handles = []
    token = conv_full[:1, :1] * 0.0
    for n, keys in enumerate(late_groups):
        srcs = [shards[k] + token[0, 0].astype(BF16) for k in keys]
        handles, token = copies_start(srcs, [jax.ShapeDtypeStruct((N_DEV,) + s.shape, BF16) for s in srcs],
                                      _to_all_plan, N_DEV - 1, f"gather_late_{n}_start")
        late_handles.append(handles)

    def late_weights(n, after):
        lands = copies_wait(late_handles[n], _to_all_plan, after, f"gather_late_{n}_wait")
        for k, land in zip(late_groups[n], lands):
            full[k] = _from_blocks(lax.dynamic_update_index_in_dim(land, shards[k], dev, 0), big[k][1])

    o_q = 3 * D
    o_g = o_q + 3 * QKV_W
    w_pa, w_qkv, w_pg = full["w_in"][:, :o_q], full["w_in"][:, o_q:o_g], full["w_in"][:, o_g:]
    b_pa, b_qkv, b_pg = b_in[:, :o_q], b_in[:, o_q:o_g], b_in[:, o_g:]

    proj_a = mm_nn(h0b, w_pa, b_pa, ACT, "proj_conv", after=token)
    proj_g = mm_nn(h0b, w_pg, b_pg, ACT, "proj_gates")
    zero_d = jnp.zeros((1, D), F32)
    s_a = conv_a_fwd(proj_a, conv_full, "conv_a_fwd")
    late_weights(0, s_a)
    y_a = mm_nn(s_a, full["w_a"], zero_d, ACT, "branch_a_out")

    def group_cols(m, g):
        return jnp.concatenate([m[:, s * QKV_W + g * GROUP_W:s * QKV_W + (g + 1) * GROUP_W] for s in range(3)], 1)

    w_grp = [group_cols(w_qkv, g) for g in range(3)]
    qkvs, outs, lses = [], [], []
    for g, d in enumerate(DILATIONS):
        qkv = mm_nn(h0_res[g], w_grp[g], group_cols(b_qkv, g), BF16, f"proj_qkv_{g}").reshape(d, T // d, 3 * GROUP_W)
        o, l = att_fwd(qkv, g, f"att_fwd_{g}")
        qkvs.append(qkv)
        outs.append(o)
        lses.append(l)
    comb = combine_fwd(outs, lses, "combine_fwd")
    y_b = mm_nn(comb, full["w_b"], zero_d, ACT, "branch_b_out")
    z = gate_fwd(proj_g, y_a, y_b, "gate_fwd")
    mix = mm_nn(z, full["w_o"], b_o, F32, "mix_out")
    h1, h1b = ln_fwd(h0, mix, ln1_g, ln1_b, "ln1_fwd")
    late_weights(1, h1b)
    up = mm_nn(h1b, full["w_up"], b_up, F32, "ffn_up")
    f_act = conv_f_fwd(up, fcw_full, ffn_conv_b, "conv_f_fwd")
    ffn = mm_nn(f_act, full["w_down"], b_down, F32, "ffn_down")

    dr2, dr2b, d_ln2_g, d_ln2_b, d_b_down, loss_part = ln_bwd(h1, ffn, ln2_g, ln2_b, None, None, tgt, "ln2_loss_bwd")
    dw_down, _ = mm_tn(f_act, dr2b, "dw_down")
    df = mm_nt(dr2b, full["w_down"], None, "d_ffn_act")
    d_a, d_gate, cs_a, cs_gate, d_fcb, d_fcw = conv_f_bwd(df, up, fcw_full, ffn_conv_b, "conv_f_bwd")
    dw_up_a, _ = mm_tn(h1b, d_a, "dw_up_a")
    dw_up_g, _ = mm_tn(h1b, d_gate, "dw_up_gate")
    dh1 = mm_nt([d_a, d_gate], full["w_up"], None, "d_h1")
    dr1, dr1b, d_ln1_g, d_ln1_b, d_b_o, _ = ln_bwd(h0, mix, ln1_g, ln1_b, dr2, dh1, None, "ln1_bwd")
    dw_o, _ = mm_tn(z, dr1b, "dw_o")
    dz = mm_nt(dr1b, full["w_o"], None, "d_z", out_dtype=ACT)
    dy_a, dy_b, dproj_g = gate_bwd(dz, proj_g, y_a, y_b, "gate_bwd")
    dw_a, _ = mm_tn(s_a, dy_a, "dw_a")
    ds_a = mm_nt(dy_a, full["w_a"], None, "d_s_a", out_dtype=ACT)
    dproj_a, d_conv = conv_a_bwd(ds_a, proj_a, conv_full, "conv_a_bwd")
    dw_b, _ = mm_tn(comb, dy_b, "dw_b")

    rs_mine, rs_sib, rs_handles = {}, {}, {}

    def reduce_start(keys, grads, tag):
        parts = [_to_blocks(grads[k], big[k][1]) for k in keys]
        from_sib = exchange_sibling(parts, f"grads_to_sibling_{tag}")
        sums = [pair_add(a, b, place, f"chip_sum_{k}") for k, a, b in zip(keys, parts, from_sib)]
        handles, tok = copies_start(sums, [jax.ShapeDtypeStruct((3,) + s.shape[1:], BF16) for s in sums],
                                    _to_chips_plan, 3, f"grads_to_chips_{tag}_start")
        for k, a, b in zip(keys, parts, from_sib):
            rs_mine[k], rs_sib[k] = a, b
        rs_handles[tag] = (keys, handles)
        return tok

    tok_a = reduce_start(("w_a", "w_b", "w_o", "w_up", "w_down"),
                         dict(w_a=dw_a, w_b=dw_b, w_o=dw_o, w_up=jnp.concatenate([dw_up_a, dw_up_g], 1), w_down=dw_down),
                         "a")
    dcomb = mm_nt(dy_b, full["w_b"], None, "d_comb", after=tok_a, out_dtype=ACT)
    dos, dms = combine_bwd(dcomb, outs, lses, "combine_bwd")
    dw_grp, cs_grp, dqkvs = [], [], []
    for g, d in enumerate(DILATIONS):
        dq, dk, dv = att_bwd(qkvs[g], dos[g], lses[g], dms[g], g, f"att_bwd_{g}")
        dqkv = [t.reshape(T, GROUP_W) for t in (dq, dk, dv)]
        dwg, csg = mm_tn(h0_res[g], dqkv, f"dw_in_qkv_{g}")
        dqkvs.append(dqkv)
        dw_grp.append(dwg)
        cs_grp.append(csg)
    dw_pa, cs_pa = mm_tn(h0b, dproj_a, "dw_in_conv")
    dw_pg, cs_pg = mm_tn(h0b, dproj_g, "dw_in_gates")

    def ungroup(parts):
        return jnp.concatenate([p[:, s * GROUP_W:(s + 1) * GROUP_W] for s in range(3) for p in parts], 1)

    db_in_parts = [cs_pa, ungroup(cs_grp), cs_pg]
    tok_b = reduce_start(("w_in",), dict(w_in=jnp.concatenate([dw_pa, ungroup(dw_grp), dw_pg], 1)), "b")
    dh0 = mm_nt(dproj_a, w_pa, None, "d_h0_conv", after=tok_b)
    dh0 = mm_nt(dproj_g, w_pg, dh0, "d_h0_gates")
    dh0 = mm_nt(dqkvs[0], w_grp[0], dh0, "d_h0_qkv_0")
    dh0_res = [(mm_nt(dqkvs[g], w_grp[g], None, f"d_h0_qkv_{g}").reshape(d, T // d, D), d)
               for g, d in enumerate(DILATIONS) if g > 0]
    dx, _, d_ln0_g, d_ln0_b, _, _ = ln_bwd(xs, None, ln0g, ln0b, dr1, dh0, None, "ln0_bwd", by_residue=dh0_res)

    small = [d_ln0_g, d_ln0_b, jnp.concatenate(db_in_parts, 1), d_conv, d_b_o, d_ln1_g, d_ln1_b,
             jnp.concatenate([cs_a, cs_gate], 1), d_fcw, d_fcb, d_b_down, d_ln2_g, d_ln2_b, loss_part]
    packed, sizes = _pack(small)
    total = all_sum_small(packed, "sum_small")
    (g_ln0_g, g_ln0_b, g_b_in, g_conv_full, g_b_o, g_ln1_g, g_ln1_b, g_b_up, g_fcw_full, g_fcb, g_b_down, g_ln2_g,
     g_ln2_b, loss) = _unpack(total, sizes, [a.shape for a in small])
    cw = conv_w.shape[-1]
    fw = ffn_conv_w.shape[-1]
    g_conv = lax.dynamic_slice_in_dim(g_conv_full, dev * cw, cw, 1)
    g_fcw = lax.dynamic_slice_in_dim(g_fcw_full, dev * fw, fw, 1)

    from_chips = {}
    for tag, (keys, handles) in rs_handles.items():
        lands = copies_wait(handles, _to_chips_plan, total, f"grads_to_chips_{tag}_wait")
        from_chips.update(zip(keys, lands))

    moments = dict(w_in=(m_w_in, v_w_in), w_a=(m_w_a, v_w_a), w_b=(m_w_b, v_w_b), w_o=(m_w_o, v_w_o),
                   w_up=(m_w_up, v_w_up), w_down=(m_w_down, v_w_down))
    res_big = {}
    for k in names:
        res_big[k] = adamw_sharded(big[k][0], moments[k][0][0], moments[k][1][0], rs_mine[k], rs_sib[k], from_chips[k],
                                   place, f"adamw_{k}")

    small_names = ["ln0_g", "ln0_b", "b_in", "conv_w", "b_o", "ln1_g", "ln1_b", "b_up", "ffn_conv_w", "ffn_conv_b",
                   "b_down", "ln2_g", "ln2_b"]
    small_w = [ln0_g, ln0_b, b_in, conv_w, b_o, ln1_g, ln1_b, b_up, ffn_conv_w, ffn_conv_b, b_down, ln2_g, ln2_b]
    small_m = [m_ln0_g, m_ln0_b, m_b_in, m_conv_w, m_b_o, m_ln1_g, m_ln1_b, m_b_up, m_ffn_conv_w, m_ffn_conv_b,
               m_b_down, m_ln2_g, m_ln2_b]
    small_v = [v_ln0_g, v_ln0_b, v_b_in, v_conv_w, v_b_o, v_ln1_g, v_ln1_b, v_b_up, v_ffn_conv_w, v_ffn_conv_b,
               v_b_down, v_ln2_g, v_ln2_b]
    small_g = [g_ln0_g, g_ln0_b, g_b_in, g_conv, g_b_o, g_ln1_g, g_ln1_b, g_b_up, g_fcw, g_fcb, g_b_down, g_ln2_g,
               g_ln2_b]
    shapes = [w.shape for w in small_w]
    small_g = [g.reshape(s) for g, s in zip(small_g, shapes)]
    pw, psz = _pack(small_w)
    pg, _ = _pack(small_g)
    pm, _ = _pack(small_m)
    pv, _ = _pack(small_v)
    pd, pnm, pnv = adamw_packed(pw, pg, pm, pv, "adamw_small")
    res_small = {k: (g, d_, m_, v_) for k, g, d_, m_, v_ in zip(
        small_names, small_g, _unpack(pd, psz, shapes), _unpack(pnm, psz, shapes), _unpack(pnv, psz, shapes))}

    order = ["ln0_g", "ln0_b", "w_in", "b_in", "conv_w", "w_a", "w_b", "w_o", "b_o", "ln1_g", "ln1_b", "w_up", "b_up",
             "ffn_conv_w", "ffn_conv_b", "w_down", "b_down", "ln2_g", "ln2_b"]

    def result(k, j):
        if k in res_big:
            return res_big[k][j][None]
        return res_small[k][j]

    out = [loss.reshape(()), dx.reshape(x.shape)]
    for j in range(4):
        out += [result(k, j) for k in order]
    return tuple(out)
```

```python
import functools
import math

import numpy as np
import jax
import jax.numpy as jnp
from jax import lax
from jax.experimental import pallas as pl
from jax.experimental.pallas import tpu as pltpu

F32 = jnp.float32
BF16 = jnp.bfloat16
ACT = BF16

N_DEV = 8
LN_EPS = 1e-5
ALPHA = (2.0 * 1) ** 0.25
MASK_VALUE = -1e30
HEAD_DIM = 64
GROUP_W = 512
QKV_W = 3 * GROUP_W
DILATIONS = (1, 4, 16)
RADIUS = 64
LANES = 128
HALO = 8
HALO_BF16 = 16
ATT_TQ = 128

ADAM_LR = 0.001
ADAM_B1 = 0.9
ADAM_B2 = 0.999
ADAM_EPS = 1e-08
ADAM_WD = 0.01
ADAM_STEP = 10

VMEM_LIMIT = 52 * 1024 * 1024
OUT_TILE_BYTES = 8 * 1024 * 1024
MESH = pl.DeviceIdType.MESH
NT_DIMS = (((1,), (1,)), ((), ()))
TN_DIMS = (((0,), (0,)), ((), ()))


def _pick(n, target, align=LANES):
    if n <= target:
        return n
    best = None
    for t in range(align, target + 1, align):
        if n % t == 0:
            best = t
    assert best is not None, (n, target, align)
    return best


def _params(sems=None):
    return pltpu.CompilerParams(dimension_semantics=sems, vmem_limit_bytes=VMEM_LIMIT)


def _alibi_slopes():
    n = 3 * 8
    return np.exp2(-8.0 * np.arange(1, n + 1, dtype=np.float64) / n).astype(np.float32).reshape(3, 8)


def _ln_stats(r):
    mu = jnp.mean(r, -1, keepdims=True)
    xc = r - mu
    var = jnp.mean(xc * xc, -1, keepdims=True)
    rstd = lax.rsqrt(var + LN_EPS)
    return xc, rstd


def _load_natural(ref, d, scr):
    if d == 1:
        return ref[0]
    n, C = ref.shape[1], ref.shape[2]
    for c in range(C // LANES):
        for r in range(d):
            scr[c, pl.ds(r, n, stride=d), :] = ref[r, :, c * LANES:(c + 1) * LANES]
    return jnp.concatenate([scr[c] for c in range(C // LANES)], axis=1)


def _store_by_residue(val, ref, d, scr):
    if d == 1:
        ref[0] = val.astype(ref.dtype)
        return
    n, C = ref.shape[1], ref.shape[2]
    for c in range(C // LANES):
        scr[c] = val[:, c * LANES:(c + 1) * LANES]
    for c in range(C // LANES):
        for r in range(d):
            ref[r, :, c * LANES:(c + 1) * LANES] = scr[c, pl.ds(r, n, stride=d), :].astype(ref.dtype)


def _residue_spec(tm, d, C):
    return pl.BlockSpec((d, tm // d, C), lambda i: (0, i, 0))


def _residue_scratch(tm, C):
    return pltpu.VMEM((C // LANES, tm, LANES), F32)


def ln_fwd(a, res, g, b, name, dilations=(), gather=()):
    T, D = a.shape
    tm = _pick(T, 512, 8)
    has_res = res is not None
    nd = len(dilations)
    ng = len(gather)
    n_in = (2 if has_res else 1) + 2
    last = T // tm - 1

    def body(*refs):
        a_ref = refs[0]
        r = a_ref[...]
        if has_res:
            r = ALPHA * r + refs[1][...]
        g_ref, b_ref = refs[n_in - 2], refs[n_in - 1]
        shard_refs = refs[n_in:n_in + ng]
        h_ref, hb_ref = refs[n_in + ng], refs[n_in + ng + 1]
        p_refs = refs[n_in + ng + 2:n_in + ng + 2 + nd]
        full_refs = refs[n_in + ng + 2 + nd:n_in + 2 * ng + 2 + nd]
        scratch = refs[n_in + 2 * ng + 2 + nd:]
        sems = scratch[len(scratch) - 3:] if ng else ()

        if ng:
            @pl.when(pl.program_id(0) == 0)
            def _():
                _gather_begin(shard_refs, full_refs, *sems)

        xc, rstd = _ln_stats(r)
        h = xc * rstd * g_ref[...] + b_ref[...]
        h_ref[...] = h
        hb_ref[...] = h.astype(BF16)
        for d, p_ref in zip(dilations, p_refs):
            _store_by_residue(h, p_ref, d, scratch[0])

        if ng:
            @pl.when(pl.program_id(0) == last)
            def _():
                _gather_finish(shard_refs, full_refs, *sems)

    row = pl.BlockSpec((tm, D), lambda i: (i, 0))
    vec = pl.BlockSpec((1, D), lambda i: (0, 0))
    hbm = pl.BlockSpec(memory_space=pl.ANY)
    ins = [a] + ([res] if has_res else []) + [g, b] + list(gather)
    return pl.pallas_call(
        body, name=name, grid=(T // tm,),
        in_specs=[row] * (2 if has_res else 1) + [vec, vec] + [hbm] * ng,
        out_specs=[row, row] + [_residue_spec(tm, d, D) for d in dilations] + [hbm] * ng,
        out_shape=[jax.ShapeDtypeStruct((T, D), F32), jax.ShapeDtypeStruct((T, D), BF16)]
        + [jax.ShapeDtypeStruct((d, T // d, D), BF16) for d in dilations]
        + [jax.ShapeDtypeStruct((N_DEV,) + s.shape, s.dtype) for s in gather],
        scratch_shapes=([_residue_scratch(tm, D)] if nd else []) + (_gather_scratch(ng) if ng else []),
        compiler_params=_params(("arbitrary",) if ng else ("parallel",)),
    )(*ins)


def ln_bwd(a, res, g, b, d1, d2, tgt, name, by_residue=()):
    T, D = a.shape
    tm = _pick(T, 256, 8)
    loss_mode = tgt is not None
    nres = len(by_residue)
    row = pl.BlockSpec((tm, D), lambda i: (i, 0))
    vec = pl.BlockSpec((1, D), lambda i: (0, 0))
    one = pl.BlockSpec((1, 1), lambda i: (0, 0))

    def rows_of(x):
        return pl.BlockSpec((tm, x.shape[1]), lambda i: (i, 0))

    def whole(x):
        return pl.BlockSpec(x.shape, lambda i: (0, 0))

    ins, in_specs, slots = [], [], {}

    def operand(key, arrays, specs):
        slots[key] = (len(ins), len(arrays))
        ins.extend(arrays)
        in_specs.extend(specs)

    operand("a", [a], [row])
    if isinstance(res, tuple):
        _, x, w, bias = res
        operand("res_mm", [x, w, bias], [rows_of(x), whole(w), vec])
    elif res is not None:
        operand("res", [res], [row])
    operand("gb", [g, b], [vec, vec])
    if loss_mode:
        operand("tgt", [tgt], [row])
    else:
        operand("d1", [d1], [row])
        if isinstance(d2, tuple):
            _, pieces, w = d2
            operand("d2_mm", list(pieces) + [w], [rows_of(p) for p in pieces] + [whole(w)])
        else:
            operand("d2", [d2], [row])
    operand("by_residue", [e for e, _ in by_residue], [_residue_spec(tm, d, D) for _, d in by_residue])
    n_in = len(ins)

    def body(*refs):
        def get(key):
            first, count = slots[key]
            return refs[first:first + count]

        dr_ref, drb_ref, dg_ref, db_ref, ds_ref, loss_ref = refs[n_in:n_in + 6]
        i = pl.program_id(0)

        @pl.when(i == 0)
        def _():
            dg_ref[...] = jnp.zeros_like(dg_ref)
            db_ref[...] = jnp.zeros_like(db_ref)
            ds_ref[...] = jnp.zeros_like(ds_ref)
            loss_ref[...] = jnp.zeros_like(loss_ref)

        r = get("a")[0][...]
        if "res_mm" in slots:
            x_ref, w_ref, bias_ref = get("res_mm")
            r = ALPHA * r + (jnp.dot(x_ref[...], w_ref[...], preferred_element_type=F32) + bias_ref[...])
        elif "res" in slots:
            r = ALPHA * r + get("res")[0][...]
        g_ref, b_ref = get("gb")
        xc, rstd = _ln_stats(r)
        xhat = xc * rstd
        gam = g_ref[...]
        if loss_mode:
            err = xhat * gam + b_ref[...] - get("tgt")[0][...]
            dy = err * (1.0 / D)
            row_loss = jnp.mean(err * err, -1, keepdims=True)
            loss_ref[...] += 0.5 * jnp.sum(row_loss, 0, keepdims=True)
        else:
            if "d2_mm" in slots:
                *p_refs, w_ref = get("d2_mm")
                av = p_refs[0][...] if len(p_refs) == 1 else jnp.concatenate([p[...] for p in p_refs], axis=1)
                d2v = lax.dot_general(av, w_ref[...], NT_DIMS, preferred_element_type=F32)
            else:
                d2v = get("d2")[0][...]
            dy = ALPHA * get("d1")[0][...] + d2v
        for (_, d), e_ref in zip(by_residue, get("by_residue")):
            dy = dy + _load_natural(e_ref, d, refs[-1])
        dyg = dy * gam
        c1 = jnp.mean(dyg, -1, keepdims=True)
        c2 = jnp.mean(dyg * xhat, -1, keepdims=True)
        dr = rstd * (dyg - c1 - xhat * c2)
        dr_ref[...] = dr
        drb_ref[...] = dr.astype(BF16)
        dg_ref[...] += jnp.sum(dy * xhat, 0, keepdims=True)
        db_ref[...] += jnp.sum(dy, 0, keepdims=True)
        ds_ref[...] += jnp.sum(dr, 0, keepdims=True)

    return pl.pallas_call(
        body, name=name, grid=(T // tm,),
        in_specs=in_specs,
        out_specs=[row, row, vec, vec, vec, one],
        out_shape=[jax.ShapeDtypeStruct((T, D), F32), jax.ShapeDtypeStruct((T, D), BF16),
                   jax.ShapeDtypeStruct((1, D), F32), jax.ShapeDtypeStruct((1, D), F32),
                   jax.ShapeDtypeStruct((1, D), F32), jax.ShapeDtypeStruct((1, 1), F32)],
        scratch_shapes=[_residue_scratch(tm, D)] if nres else [],
        compiler_params=_params(("arbitrary",)),
    )(*ins)


_TOKEN_SPEC = pl.BlockSpec((8, LANES), lambda i: (0, 0))


def mm_nn(a, w, bias, out_dtype, name, after=None):
    M, K = a.shape
    N = w.shape[1]
    tm = _pick(M, max(256, min(1024, OUT_TILE_BYTES // (N * jnp.dtype(out_dtype).itemsize))), 8)
    tc = _pick(N, 512)

    def body(a_ref, w_ref, b_ref, *rest):
        o_ref = rest[-1]
        av = a_ref[...]
        for j in range(N // tc):
            cols = slice(j * tc, (j + 1) * tc)
            acc = jnp.dot(av, w_ref[:, cols], preferred_element_type=F32)
            o_ref[:, cols] = (acc + b_ref[:, cols]).astype(out_dtype)

    return pl.pallas_call(
        body, name=name, grid=(M // tm,),
        in_specs=[pl.BlockSpec((tm, K), lambda i: (i, 0)),
                  pl.BlockSpec((K, N), lambda i: (0, 0)),
                  pl.BlockSpec((1, N), lambda i: (0, 0))] + ([] if after is None else [_TOKEN_SPEC]),
        out_specs=pl.BlockSpec((tm, N), lambda i: (i, 0)),
        out_shape=jax.ShapeDtypeStruct((M, N), out_dtype),
        compiler_params=_params(("parallel",)),
    )(a, w, bias, *([] if after is None else [after]))


def mm_nt(a, w, acc_in, name, after=None, w_block=0, out_dtype=F32):
    pieces = list(a) if isinstance(a, (list, tuple)) else [a]
    M = pieces[0].shape[0]
    widths = [p.shape[1] for p in pieces]
    K = sum(widths)
    N = w.shape[0]
    tm = _pick(M, 512, 8)
    tc = _pick(N, 512)
    has_acc = acc_in is not None
    n_a = len(pieces)

    def body(*refs):
        a_refs, w_ref = refs[:n_a], refs[n_a]
        c_ref = refs[n_a + 1] if has_acc else None
        o_ref = refs[-1]
        av = a_refs[0][...] if n_a == 1 else jnp.concatenate([r[...] for r in a_refs], axis=1)
        for j in range(N // tc):
            cols = slice(j * tc, (j + 1) * tc)
            acc = lax.dot_general(av, w_ref[cols, :], NT_DIMS, preferred_element_type=F32)
            if has_acc:
                acc = acc + c_ref[:, cols]
            o_ref[:, cols] = acc.astype(out_dtype)

    out_spec = pl.BlockSpec((tm, N), lambda i: (i, 0))
    in_specs = [pl.BlockSpec((tm, kw), lambda i: (i, 0)) for kw in widths]
    in_specs.append(pl.BlockSpec((N, K), lambda i: (0, w_block)))
    ins = pieces + [w]
    if has_acc:
        in_specs.append(out_spec)
        ins.append(acc_in)
    if after is not None:
        in_specs.append(_TOKEN_SPEC)
        ins.append(after)
    return pl.pallas_call(
        body, name=name, grid=(M // tm,),
        in_specs=in_specs, out_specs=out_spec,
        out_shape=jax.ShapeDtypeStruct((M, N), out_dtype),
        compiler_params=_params(("parallel",)),
    )(*ins)


def mm_tn(a, b, name, out_dtype=BF16):
    pieces = list(b) if isinstance(b, (list, tuple)) else [b]
    T, M = a.shape
    widths = [p.shape[1] for p in pieces]
    N = sum(widths)
    tk = _pick(T, 512, 8)
    nk = T // tk
    tc = _pick(M, 256)
    n_b = len(pieces)

    def body(*refs):
        a_ref, b_refs = refs[0], refs[1:1 + n_b]
        o_ref, cs_ref, acc_ref = refs[1 + n_b:]
        k = pl.program_id(0)

        @pl.when(k == 0)
        def _():
            acc_ref[...] = jnp.zeros_like(acc_ref)
            cs_ref[...] = jnp.zeros_like(cs_ref)

        bv = b_refs[0][...] if n_b == 1 else jnp.concatenate([r[...] for r in b_refs], axis=1)
        cs_ref[...] += jnp.sum(bv.astype(F32), 0, keepdims=True)
        for mi in range(M // tc):
            rows = slice(mi * tc, (mi + 1) * tc)
            acc_ref[rows, :] += lax.dot_general(a_ref[:, rows], bv, TN_DIMS, preferred_element_type=F32)

        @pl.when(k == nk - 1)
        def _():
            o_ref[...] = acc_ref[...].astype(out_dtype)

    return pl.pallas_call(
        body, name=name, grid=(nk,),
        in_specs=[pl.BlockSpec((tk, M), lambda k: (k, 0))] + [pl.BlockSpec((tk, wd), lambda k: (k, 0)) for wd in widths],
        out_specs=[pl.BlockSpec((M, N), lambda k: (0, 0)), pl.BlockSpec((1, N), lambda k: (0, 0))],
        out_shape=[jax.ShapeDtypeStruct((M, N), out_dtype), jax.ShapeDtypeStruct((1, N), F32)],
        scratch_shapes=[pltpu.VMEM((M, N), F32)],
        compiler_params=_params(("arbitrary",)),
    )(a, *pieces)


def _ext_rows(prev_ref, main_ref, next_ref, i, tm, T):
    before = jnp.where(i == 0, 0.0, prev_ref[...])
    after = jnp.where(i == T // tm - 1, 0.0, next_ref[...])
    return jnp.concatenate([before, main_ref[...], after], axis=0).astype(F32)


def _prev_row(x):
    return pltpu.roll(x, 1, 0)


def _next_row(x):
    return pltpu.roll(x, x.shape[0] - 1, 0)


def _conv3(u, w_ref):
    return _prev_row(u) * w_ref[0:1, :] + u * w_ref[1:2, :] + _next_row(u) * w_ref[2:3, :]


def _main(x, tm, halo=HALO):
    return x[halo:halo + tm]


def _halo_specs(tm, tc, T, col, order, halo=HALO):
    r = tm // halo
    last = T // halo - 1
    if order == "ij":
        return (pl.BlockSpec((halo, tc), lambda i, j: (jnp.maximum(i * r - 1, 0), col(j))),
                pl.BlockSpec((tm, tc), lambda i, j: (i, col(j))),
                pl.BlockSpec((halo, tc), lambda i, j: (jnp.minimum((i + 1) * r, last), col(j))))
    return (pl.BlockSpec((halo, tc), lambda j, i: (jnp.maximum(i * r - 1, 0), col(j))),
            pl.BlockSpec((tm, tc), lambda j, i: (i, col(j))),
            pl.BlockSpec((halo, tc), lambda j, i: (jnp.minimum((i + 1) * r, last), col(j))))


def conv_a_fwd(proj_a, conv_w, name):
    T, D3 = proj_a.shape
    D = D3 // 3
    tm = _pick(T, 256, 8)

    def body(p_ref, m_ref, n_ref, w_ref, o_ref):
        i = pl.program_id(0)
        ext = _ext_rows(p_ref, m_ref, n_ref, i, tm, T)
        u = ext[:, D:2 * D] * ext[:, 2 * D:]
        cu = _conv3(u, w_ref)
        o_ref[...] = (m_ref[:, :D].astype(F32) * _main(cu, tm, HALO_BF16)).astype(BF16)

    prev, main, nxt = _halo_specs(tm, D3, T, lambda j: 0, "ij", HALO_BF16)
    return pl.pallas_call(
        body, name=name, grid=(T // tm, 1),
        in_specs=[prev, main, nxt, pl.BlockSpec((3, D), lambda i, j: (0, 0))],
        out_specs=pl.BlockSpec((tm, D), lambda i, j: (i, 0)),
        out_shape=jax.ShapeDtypeStruct((T, D), BF16),
        compiler_params=_params(("parallel", "arbitrary")),
    )(proj_a, proj_a, proj_a, conv_w)


def conv_a_bwd(ds_a, proj_a, conv_w, name):
    T, D3 = proj_a.shape
    D = D3 // 3
    tm = _pick(T, 256, 8)

    def body(dp_ref, dm_ref, dn_ref, p_ref, m_ref, n_ref, w_ref, o_ref, dw_ref):
        i = pl.program_id(0)

        @pl.when(i == 0)
        def _():
            dw_ref[...] = jnp.zeros_like(dw_ref)

        ext = _ext_rows(p_ref, m_ref, n_ref, i, tm, T)
        dsa = _ext_rows(dp_ref, dm_ref, dn_ref, i, tm, T)
        gb, gc, hin = ext[:, :D], ext[:, D:2 * D], ext[:, 2 * D:]
        u = gc * hin
        u_prev, u_next = _prev_row(u), _next_row(u)
        cu = u_prev * w_ref[0:1, :] + u * w_ref[1:2, :] + u_next * w_ref[2:3, :]
        dcu = dsa * gb
        du = _next_row(dcu) * w_ref[0:1, :] + dcu * w_ref[1:2, :] + _prev_row(dcu) * w_ref[2:3, :]
        h = HALO_BF16
        o_ref[:, :D] = _main(dsa * cu, tm, h).astype(BF16)
        o_ref[:, D:2 * D] = _main(du * hin, tm, h).astype(BF16)
        o_ref[:, 2 * D:] = _main(du * gc, tm, h).astype(BF16)
        dcu_m = _main(dcu, tm, h)
        dw_ref[0:1, :] += jnp.sum(dcu_m * _main(u_prev, tm, h), 0, keepdims=True)
        dw_ref[1:2, :] += jnp.sum(dcu_m * _main(u, tm, h), 0, keepdims=True)
        dw_ref[2:3, :] += jnp.sum(dcu_m * _main(u_next, tm, h), 0, keepdims=True)

    dprev, dmain, dnxt = _halo_specs(tm, D, T, lambda j: 0, "ij", HALO_BF16)
    prev, main, nxt = _halo_specs(tm, D3, T, lambda j: 0, "ij", HALO_BF16)
    return pl.pallas_call(
        body, name=name, grid=(T // tm, 1),
        in_specs=[dprev, dmain, dnxt, prev, main, nxt, pl.BlockSpec((3, D), lambda i, j: (0, 0))],
        out_specs=[pl.BlockSpec((tm, D3), lambda i, j: (i, 0)), pl.BlockSpec((3, D), lambda i, j: (0, 0))],
        out_shape=[jax.ShapeDtypeStruct((T, D3), BF16), jax.ShapeDtypeStruct((3, D), F32)],
        compiler_params=_params(("arbitrary", "arbitrary")),
    )(ds_a, ds_a, ds_a, proj_a, proj_a, proj_a, conv_w)


_INV_SQRT2 = 1.0 / math.sqrt(2.0)
_INV_SQRT_2PI = 1.0 / math.sqrt(2.0 * math.pi)


def conv_f_fwd(up, fcw, fcb, name):
    T, F2 = up.shape
    F = F2 // 2
    tm = _pick(T, 256, 8)
    tc = _pick(F, 1408)
    nc = F // tc

    def body(p_ref, m_ref, n_ref, g_ref, w_ref, b_ref, o_ref):
        i = pl.program_id(0)
        a = _ext_rows(p_ref, m_ref, n_ref, i, tm, T)
        ca = _main(_conv3(a, w_ref), tm) + b_ref[...]
        gl = 0.5 * ca * (1.0 + lax.erf(ca * _INV_SQRT2))
        o_ref[...] = (gl * g_ref[...]).astype(BF16)

    prev, main, nxt = _halo_specs(tm, tc, T, lambda j: j, "ij")
    return pl.pallas_call(
        body, name=name, grid=(T // tm, nc),
        in_specs=[prev, main, nxt,
                  pl.BlockSpec((tm, tc), lambda i, j: (i, nc + j)),
                  pl.BlockSpec((3, tc), lambda i, j: (0, j)),
                  pl.BlockSpec((1, tc), lambda i, j: (0, j))],
        out_specs=pl.BlockSpec((tm, tc), lambda i, j: (i, j)),
        out_shape=jax.ShapeDtypeStruct((T, F), BF16),
        compiler_params=_params(("parallel", "parallel")),
    )(up, up, up, up, fcw, fcb)


def conv_f_bwd(df, up, fcw, fcb, name):
    T, F2 = up.shape
    F = F2 // 2
    tm = _pick(T, 256, 8)
    tc = _pick(F, 1408)
    nc = F // tc

    def body(fp_ref, fm_ref, fn_ref, ap_ref, am_ref, an_ref, gp_ref, gm_ref, gn_ref, w_ref, b_ref,
             da_ref, dg_ref, csa_ref, csg_ref, dfb_ref, dfw_ref):
        i = pl.program_id(1)

        @pl.when(i == 0)
        def _():
            csa_ref[...] = jnp.zeros_like(csa_ref)
            csg_ref[...] = jnp.zeros_like(csg_ref)
            dfb_ref[...] = jnp.zeros_like(dfb_ref)
            dfw_ref[...] = jnp.zeros_like(dfw_ref)

        dfe = _ext_rows(fp_ref, fm_ref, fn_ref, i, tm, T)
        a = _ext_rows(ap_ref, am_ref, an_ref, i, tm, T)
        gate = _ext_rows(gp_ref, gm_ref, gn_ref, i, tm, T)
        a_prev, a_next = _prev_row(a), _next_row(a)
        ca = a_prev * w_ref[0:1, :] + a * w_ref[1:2, :] + a_next * w_ref[2:3, :] + b_ref[...]
        cdf = 0.5 * (1.0 + lax.erf(ca * _INV_SQRT2))
        gl = ca * cdf
        gp = cdf + ca * (jnp.exp(-0.5 * ca * ca) * _INV_SQRT_2PI)
        dgate = _main(dfe * gl, tm)
        dca = dfe * gate * gp
        da = _main(_next_row(dca) * w_ref[0:1, :] + dca * w_ref[1:2, :] + _prev_row(dca) * w_ref[2:3, :], tm)
        da_ref[...] = da.astype(BF16)
        dg_ref[...] = dgate.astype(BF16)
        csa_ref[...] += jnp.sum(da, 0, keepdims=True)
        csg_ref[...] += jnp.sum(dgate, 0, keepdims=True)
        dca_m = _main(dca, tm)
        dfb_ref[...] += jnp.sum(dca_m, 0, keepdims=True)
        dfw_ref[0:1, :] += jnp.sum(dca_m * _main(a_prev, tm), 0, keepdims=True)
        dfw_ref[1:2, :] += jnp.sum(dca_m * _main(a, tm), 0, keepdims=True)
        dfw_ref[2:3, :] += jnp.sum(dca_m * _main(a_next, tm), 0, keepdims=True)

    fprev, fmain, fnxt = _halo_specs(tm, tc, T, lambda j: j, "ji")
    gprev, gmain, gnxt = _halo_specs(tm, tc, T, lambda j: nc + j, "ji")
    tile = pl.BlockSpec((tm, tc), lambda j, i: (i, j))
    vec = pl.BlockSpec((1, tc), lambda j, i: (0, j))
    vec3 = pl.BlockSpec((3, tc), lambda j, i: (0, j))
    return pl.pallas_call(
        body, name=name, grid=(nc, T // tm),
        in_specs=[fprev, fmain, fnxt, fprev, fmain, fnxt, gprev, gmain, gnxt, vec3, vec],
        out_specs=[tile, tile, vec, vec, vec, vec3],
        out_shape=[jax.ShapeDtypeStruct((T, F), BF16), jax.ShapeDtypeStruct((T, F), BF16),
                   jax.ShapeDtypeStruct((1, F), F32), jax.ShapeDtypeStruct((1, F), F32),
                   jax.ShapeDtypeStruct((1, F), F32), jax.ShapeDtypeStruct((3, F), F32)],
        compiler_params=_params(("arbitrary", "arbitrary")),
    )(df, df, df, up, up, up, up, up, up, fcw, fcb)


def gate_fwd(proj_g, y_a, y_b, name):
    T, D = y_a.shape
    tm = _pick(T, 512, 8)

    def body(g_ref, a_ref, b_ref, o_ref):
        sa = jax.nn.sigmoid(g_ref[:, :D].astype(F32))
        sb = jax.nn.sigmoid(g_ref[:, D:].astype(F32))
        o_ref[...] = (sa * a_ref[...].astype(F32) + sb * b_ref[...].astype(F32)).astype(BF16)

    row = pl.BlockSpec((tm, D), lambda i: (i, 0))
    return pl.pallas_call(
        body, name=name, grid=(T // tm,),
        in_specs=[pl.BlockSpec((tm, 2 * D), lambda i: (i, 0)), row, row],
        out_specs=row,
        out_shape=jax.ShapeDtypeStruct((T, D), BF16),
        compiler_params=_params(("parallel",)),
    )(proj_g, y_a, y_b)


def gate_bwd(dz, proj_g, y_a, y_b, name):
    T, D = y_a.shape
    tm = _pick(T, 512, 8)

    def body(dz_ref, g_ref, a_ref, b_ref, da_ref, db_ref, dg_ref):
        dzv = dz_ref[...].astype(F32)
        sa = jax.nn.sigmoid(g_ref[:, :D].astype(F32))
        sb = jax.nn.sigmoid(g_ref[:, D:].astype(F32))
        da_ref[...] = (dzv * sa).astype(BF16)
        db_ref[...] = (dzv * sb).astype(BF16)
        dg_ref[:, :D] = (dzv * a_ref[...].astype(F32) * (sa * (1.0 - sa))).astype(BF16)
        dg_ref[:, D:] = (dzv * b_ref[...].astype(F32) * (sb * (1.0 - sb))).astype(BF16)

    row = pl.BlockSpec((tm, D), lambda i: (i, 0))
    wide = pl.BlockSpec((tm, 2 * D), lambda i: (i, 0))
    return pl.pallas_call(
        body, name=name, grid=(T // tm,),
        in_specs=[row, wide, row, row],
        out_specs=[row, row, wide],
        out_shape=[jax.ShapeDtypeStruct((T, D), BF16), jax.ShapeDtypeStruct((T, D), BF16),
                   jax.ShapeDtypeStruct((T, 2 * D), BF16)],
        compiler_params=_params(("parallel",)),
    )(dz, proj_g, y_a, y_b)


ATT_WIN = ATT_TQ + 2 * RADIUS
ATT_STEP = 1024
FAR = 1e32


def _att_window(qs, L):
    ks = pl.multiple_of(jnp.clip(qs - RADIUS, 0, L - ATT_WIN), RADIUS)
    return ks, jnp.where(qs == 0, 0, jnp.where(qs == L - ATT_TQ, 2, 1))


def _fill_bias_tables(bias_ref, sl_ref, hp, d):
    col_row = (lax.broadcasted_iota(jnp.int32, (ATT_TQ, ATT_WIN), 1)
               - lax.broadcasted_iota(jnp.int32, (ATT_TQ, ATT_WIN), 0))
    for v in range(3):
        ad = jnp.abs(col_row - v * RADIUS)
        dist = jnp.where(ad <= RADIUS, (ad * d).astype(F32), FAR)
        bias_ref[v, 0:ATT_TQ, :] = sl_ref[hp * 2] * dist
        bias_ref[v, ATT_TQ:2 * ATT_TQ, :] = sl_ref[hp * 2 + 1] * dist


def _head_masks():
    lane = lax.broadcasted_iota(jnp.int32, (1, LANES), 1)
    return [lane < HEAD_DIM, lane >= HEAD_DIM]


def _stack_heads(x, masks):
    zero = jnp.zeros_like(x)
    return jnp.concatenate([jnp.where(masks[0], x, zero), jnp.where(masks[1], x, zero)], axis=0)


def _unstack_heads(x2, masks):
    n = x2.shape[0] // 2
    return jnp.where(masks[0], x2[:n], x2[n:])


def _att_step(L):
    step = min(ATT_STEP, L)
    assert L % step == 0 and step % ATT_TQ == 0 and L >= ATT_WIN
    return step


def att_fwd(qkv, group, name):
    d, L, _ = qkv.shape
    step = _att_step(L)
    cg = GROUP_W // LANES
    slopes = jnp.asarray(_alibi_slopes()[group])
    scale = HEAD_DIM ** -0.5

    def body(sl_ref, q_ref, k_ref, v_ref, o_ref, l_ref, bias_ref, s_ref, p_ref):
        hp = pl.program_id(1)
        i = pl.program_id(2)

        @pl.when(i == 0)
        def _():
            _fill_bias_tables(bias_ref, sl_ref, hp, d)

        masks = _head_masks()
        tiles = range(step // ATT_TQ)
        windows = [_att_window(i * step + t * ATT_TQ, L) for t in tiles]
        for t in tiles:
            rows = slice(t * ATT_TQ, (t + 1) * ATT_TQ)
            ks, table = windows[t]
            q2 = _stack_heads(q_ref[rows, :] * scale, masks)
            kw = k_ref[pl.ds(ks, ATT_WIN), :]
            s_ref[t] = lax.dot_general(q2, kw, NT_DIMS, preferred_element_type=F32) - bias_ref[table]
        for t in tiles:
            rows = slice(t * ATT_TQ, (t + 1) * ATT_TQ)
            s = s_ref[t]
            m = jnp.max(s, -1, keepdims=True)
            p = jnp.exp(s - m)
            den = jnp.sum(p, -1, keepdims=True)
            p_ref[t] = (p / den).astype(BF16)
            l_ref[rows, :] = _unstack_heads(m + jnp.log(den), masks)
        for t in tiles:
            rows = slice(t * ATT_TQ, (t + 1) * ATT_TQ)
            vw = v_ref[pl.ds(windows[t][0], ATT_WIN), :]
            o2 = jnp.dot(p_ref[t], vw, preferred_element_type=F32)
            o_ref[rows, :] = _unstack_heads(o2, masks)

    n_tiles = step // ATT_TQ
    out_spec = pl.BlockSpec((None, step, LANES), lambda r, hp, i: (r, i, hp))
    return pl.pallas_call(
        body, name=name, grid=(d, cg, L // step),
        in_specs=[pl.BlockSpec(memory_space=pltpu.SMEM),
                  pl.BlockSpec((None, step, LANES), lambda r, hp, i: (r, i, hp)),
                  pl.BlockSpec((None, L, LANES), lambda r, hp, i: (r, 0, cg + hp)),
                  pl.BlockSpec((None, L, LANES), lambda r, hp, i: (r, 0, 2 * cg + hp))],
        out_specs=[out_spec, out_spec],
        out_shape=[jax.ShapeDtypeStruct((d, L, GROUP_W), F32)] * 2,
        scratch_shapes=[pltpu.VMEM((3, 2 * ATT_TQ, ATT_WIN), F32),
                        pltpu.VMEM((n_tiles, 2 * ATT_TQ, ATT_WIN), F32),
                        pltpu.VMEM((n_tiles, 2 * ATT_TQ, ATT_WIN), BF16)],
        compiler_params=_params(("arbitrary", "arbitrary", "arbitrary")),
    )(slopes, qkv, qkv, qkv)


def att_bwd(qkv, do, lse, dmat, group, name):
    d, L, _ = qkv.shape
    step = _att_step(L)
    nq = L // step
    cg = GROUP_W // LANES
    slopes = jnp.asarray(_alibi_slopes()[group])
    scale = HEAD_DIM ** -0.5

    def body(sl_ref, q_ref, k_ref, v_ref, do_ref, l_ref, dm_ref, dq_ref, dk_ref, dv_ref, dk_acc, dv_acc, bias_ref,
             s_ref, dp_ref, p_ref, ds_ref):
        hp = pl.program_id(1)
        i = pl.program_id(2)

        @pl.when(i == 0)
        def _():
            dk_acc[...] = jnp.zeros_like(dk_acc)
            dv_acc[...] = jnp.zeros_like(dv_acc)
            _fill_bias_tables(bias_ref, sl_ref, hp, d)

        masks = _head_masks()

        def head_cols(x):
            return jnp.concatenate([jnp.max(jnp.where(hm, x, -jnp.inf), -1, keepdims=True) for hm in masks], axis=0)

        tiles = range(step // ATT_TQ)
        windows = [_att_window(i * step + t * ATT_TQ, L) for t in tiles]

        def stacked(ref, t, factor=None):
            x = ref[t * ATT_TQ:(t + 1) * ATT_TQ, :]
            return _stack_heads(x if factor is None else x * factor, masks)

        for t in tiles:
            ks, table = windows[t]
            q2 = stacked(q_ref, t, scale)
            s_ref[t] = lax.dot_general(q2, k_ref[pl.ds(ks, ATT_WIN), :], NT_DIMS,
                                       preferred_element_type=F32) - bias_ref[table]
            dp_ref[t] = lax.dot_general(stacked(do_ref, t), v_ref[pl.ds(ks, ATT_WIN), :], NT_DIMS,
                                        preferred_element_type=F32)
        for t in tiles:
            rows = slice(t * ATT_TQ, (t + 1) * ATT_TQ)
            p = jnp.exp(s_ref[t] - head_cols(l_ref[rows, :]))
            p_ref[t] = p.astype(BF16)
            ds_ref[t] = (p * (dp_ref[t] - head_cols(dm_ref[rows, :]))).astype(BF16)
        for t in tiles:
            rows = slice(t * ATT_TQ, (t + 1) * ATT_TQ)
            ks = windows[t][0]
            ds = ds_ref[t]
            dq2 = jnp.dot(ds, k_ref[pl.ds(ks, ATT_WIN), :], preferred_element_type=F32)
            dq_ref[rows, :] = (_unstack_heads(dq2, masks) * scale).astype(BF16)
            dk_acc[pl.ds(ks, ATT_WIN), :] += lax.dot_general(ds, stacked(q_ref, t, scale), TN_DIMS,
                                                             preferred_element_type=F32)
            dv_acc[pl.ds(ks, ATT_WIN), :] += lax.dot_general(p_ref[t], stacked(do_ref, t), TN_DIMS,
                                                             preferred_element_type=F32)

        @pl.when(i == nq - 1)
        def _():
            dk_ref[...] = dk_acc[...].astype(BF16)
            dv_ref[...] = dv_acc[...].astype(BF16)

    tile = pl.BlockSpec((None, step, LANES), lambda r, hp, i: (r, i, hp))
    whole = pl.BlockSpec((None, L, LANES), lambda r, hp, i: (r, 0, hp))
    return pl.pallas_call(
        body, name=name, grid=(d, cg, nq),
        in_specs=[pl.BlockSpec(memory_space=pltpu.SMEM), tile,
                  pl.BlockSpec((None, L, LANES), lambda r, hp, i: (r, 0, cg + hp)),
                  pl.BlockSpec((None, L, LANES), lambda r, hp, i: (r, 0, 2 * cg + hp)),
                  tile, tile, tile],
        out_specs=[tile, whole, whole],
        out_shape=[jax.ShapeDtypeStruct((d, L, GROUP_W), BF16)] * 3,
        scratch_shapes=[pltpu.VMEM((L, LANES), F32), pltpu.VMEM((L, LANES), F32),
                        pltpu.VMEM((3, 2 * ATT_TQ, ATT_WIN), F32)]
        + [pltpu.VMEM((step // ATT_TQ, 2 * ATT_TQ, ATT_WIN), dt) for dt in (F32, F32, BF16, BF16)],
        compiler_params=_params(("arbitrary", "arbitrary", "arbitrary")),
    )(slopes, qkv, qkv, qkv, do, lse, dmat)


def _group_weights(ls):
    m = jnp.maximum(jnp.maximum(ls[0], ls[1]), ls[2])
    es = [jnp.exp(l - m) for l in ls]
    tot = es[0] + es[1] + es[2]
    return [e / tot for e in es]


def combine_fwd(outs, lses, name):
    T = outs[0].shape[0] * outs[0].shape[1]
    tm = _pick(T, 512, 8)
    n_scr = 2 * (len(DILATIONS) - 1)

    def body(*refs):
        o_refs, l_refs, c_ref, scr = refs[:3], refs[3:6], refs[6], refs[7:]
        o = [_load_natural(o_refs[g], d, scr[g - 1] if g else None) for g, d in enumerate(DILATIONS)]
        l = [_load_natural(l_refs[g], d, scr[g + 1] if g else None) for g, d in enumerate(DILATIONS)]
        w = _group_weights(l)
        c_ref[...] = (w[0] * o[0] + w[1] * o[1] + w[2] * o[2]).astype(BF16)

    specs = [_residue_spec(tm, d, GROUP_W) for d in DILATIONS]
    return pl.pallas_call(
        body, name=name, grid=(T // tm,),
        in_specs=specs + specs, out_specs=pl.BlockSpec((tm, GROUP_W), lambda i: (i, 0)),
        out_shape=jax.ShapeDtypeStruct((T, GROUP_W), BF16),
        scratch_shapes=[_residue_scratch(tm, GROUP_W)] * n_scr,
        compiler_params=_params(("parallel",)),
    )(*outs, *lses)


def combine_bwd(dcomb, outs, lses, name):
    T = dcomb.shape[0]
    tm = _pick(T, 256, 8)
    head = np.arange(GROUP_W) // HEAD_DIM
    seg = jnp.asarray((head[:, None] == head[None, :]).astype(np.float32)).astype(BF16)
    ng = len(DILATIONS)
    n_scr = 4 * (ng - 1)

    def body(*refs):
        dc_ref, o_refs, l_refs, e_ref = refs[0], refs[1:1 + ng], refs[1 + ng:1 + 2 * ng], refs[1 + 2 * ng]
        do_refs, dm_refs = refs[2 + 2 * ng:2 + 3 * ng], refs[2 + 3 * ng:2 + 4 * ng]
        scr = refs[2 + 4 * ng:]
        o = [_load_natural(o_refs[g], d, scr[4 * (g - 1)] if g else None) for g, d in enumerate(DILATIONS)]
        l = [_load_natural(l_refs[g], d, scr[4 * (g - 1) + 1] if g else None) for g, d in enumerate(DILATIONS)]
        w = _group_weights(l)
        dc = dc_ref[...].astype(F32)
        e = e_ref[...]
        prod = dc * (w[0] * o[0] + w[1] * o[1] + w[2] * o[2])
        tot = jnp.zeros_like(dc)
        for _ in range(3):
            part = prod.astype(BF16)
            tot = tot + jnp.dot(part, e, preferred_element_type=F32)
            prod = prod - part.astype(F32)
        for g, d in enumerate(DILATIONS):
            _store_by_residue(w[g] * dc, do_refs[g], d, scr[4 * (g - 1) + 2] if g else None)
            _store_by_residue(w[g] * tot, dm_refs[g], d, scr[4 * (g - 1) + 3] if g else None)

    specs = [_residue_spec(tm, d, GROUP_W) for d in DILATIONS]
    res = pl.pallas_call(
        body, name=name, grid=(T // tm,),
        in_specs=[pl.BlockSpec((tm, GROUP_W), lambda i: (i, 0))] + specs + specs
        + [pl.BlockSpec((GROUP_W, GROUP_W), lambda i: (0, 0))],
        out_specs=specs + specs,
        out_shape=[jax.ShapeDtypeStruct(o.shape, BF16) for o in outs] + [jax.ShapeDtypeStruct(o.shape, F32) for o in outs],
        scratch_shapes=[_residue_scratch(tm, GROUP_W)] * n_scr,
        compiler_params=_params(("parallel",)),
    )(dcomb, *outs, *lses, seg)
    return res[:ng], res[ng:]


def _position():
    return lax.axis_index("x"), lax.axis_index("y"), lax.axis_index("c")


def _other_chips(x, y):
    return [(1 - x, y), (x, 1 - y), (1 - x, 1 - y)]


def _remote(src, dst, send_sems, recv_sems, k, to):
    return pltpu.make_async_remote_copy(src_ref=src, dst_ref=dst, send_sem=send_sems.at[k], recv_sem=recv_sems.at[k],
                                        device_id=to, device_id_type=MESH)


def _gather_descriptors(ins, outs, send_sems, recv_sems, local_sems):
    n = len(ins)
    x, y, c = _position()
    sibling = (x, y, 1 - c)
    chips = _other_chips(x, y)

    def block(a, px, py, pc):
        return outs[a].at[4 * px + 2 * py + pc]

    own, first, arrivals = [], [], []
    for a in range(n):
        k0 = 7 * a
        mine = block(a, x, y, c)
        own.append(pltpu.make_async_copy(ins[a], mine, local_sems.at[a]))
        first.append(_remote(ins[a], mine, send_sems, recv_sems, k0, sibling))
        row = []
        for j, chip in enumerate(chips):
            first.append(_remote(ins[a], mine, send_sems, recv_sems, k0 + 1 + j, (*chip, c)))
            got = block(a, *chip, c)
            row.append((_remote(got, got, send_sems, recv_sems, k0 + 1 + j, sibling),
                        _remote(got, got, send_sems, recv_sems, k0 + 4 + j, sibling)))
        arrivals.append(row)
    return own, first, arrivals


def _gather_begin(ins, outs, send_sems, recv_sems, local_sems):
    own, first, _ = _gather_descriptors(ins, outs, send_sems, recv_sems, local_sems)
    for cp in own + first:
        cp.start()


def _gather_finish(ins, outs, send_sems, recv_sems, local_sems):
    own, first, arrivals = _gather_descriptors(ins, outs, send_sems, recv_sems, local_sems)
    passed = []
    for row in arrivals:
        for arrived, onward in row:
            arrived.wait_recv()
            onward.start()
            passed.append(onward)
    for a in range(len(ins)):
        first[4 * a].wait_recv()
        for _, onward in arrivals[a]:
            onward.wait_recv()
    for cp in first + passed:
        cp.wait_send()
    for cp in own:
        cp.wait()


def _gather_scratch(n):
    return [pltpu.SemaphoreType.DMA((7 * n,)), pltpu.SemaphoreType.DMA((7 * n,)), pltpu.SemaphoreType.DMA((n,))]


def all_gather(shards, name):
    n = len(shards)

    def body(*refs):
        ins, outs, sems = refs[:n], refs[n:2 * n], refs[2 * n:]
        _gather_begin(ins, outs, *sems)
        _gather_finish(ins, outs, *sems)

    hbm = pl.BlockSpec(memory_space=pl.ANY)
    return pl.pallas_call(
        body, name=name,
        in_specs=[hbm] * n, out_specs=[hbm] * n,
        out_shape=[jax.ShapeDtypeStruct((N_DEV,) + s.shape, s.dtype) for s in shards],
        scratch_shapes=_gather_scratch(n),
    )(*shards)


def exchange_sibling(parts, name):
    n = len(parts)

    def body(*refs):
        ins, outs = refs[:n], refs[n:2 * n]
        send_sems, recv_sems = refs[2 * n:]
        x, y, c = _position()
        sibling = (x, y, 1 - c)
        copies = []
        for a in range(n):
            for q in range(4):
                cp = _remote(ins[a].at[2 * q + (1 - c)], outs[a].at[q], send_sems, recv_sems, 4 * a + q, sibling)
                cp.start()
                copies.append(cp)
        for cp in copies:
            cp.wait_recv()
        for cp in copies:
            cp.wait_send()

    hbm = pl.BlockSpec(memory_space=pl.ANY)
    return pl.pallas_call(
        body, name=name,
        in_specs=[hbm] * n, out_specs=[hbm] * n,
        out_shape=[jax.ShapeDtypeStruct((4,) + p.shape[1:], p.dtype) for p in parts],
        scratch_shapes=[pltpu.SemaphoreType.DMA((4 * n,)), pltpu.SemaphoreType.DMA((4 * n,))],
    )(*parts)


def exchange_chips(sums, name):
    n = len(sums)

    def body(*refs):
        ins, outs = refs[:n], refs[n:2 * n]
        send_sems, recv_sems = refs[2 * n:]
        x, y, c = _position()
        copies = []
        for a in range(n):
            for j, (cx, cy) in enumerate(_other_chips(x, y)):
                cp = _remote(ins[a].at[2 * cx + cy], outs[a].at[j], send_sems, recv_sems, 3 * a + j, (cx, cy, c))
                cp.start()
                copies.append(cp)
        for cp in copies:
            cp.wait_recv()
        for cp in copies:
            cp.wait_send()

    hbm = pl.BlockSpec(memory_space=pl.ANY)
    return pl.pallas_call(
        body, name=name,
        in_specs=[hbm] * n, out_specs=[hbm] * n,
        out_shape=[jax.ShapeDtypeStruct((3,) + s.shape[1:], s.dtype) for s in sums],
        scratch_shapes=[pltpu.SemaphoreType.DMA((3 * n,)), pltpu.SemaphoreType.DMA((3 * n,))],
    )(*sums)


_HBM = pl.BlockSpec(memory_space=pltpu.HBM)
_SEM = pl.BlockSpec(memory_space=pltpu.SEMAPHORE)
_DATAFLOW = pltpu.SideEffectType.DATAFLOW_SIDE_EFFECTING


def _to_all_plan(srcs, lands, send_sems, recv_sems):
    x, y, c = _position()
    me = 4 * x + 2 * y + c
    copies = []
    for a in range(len(srcs)):
        for k in range(1, N_DEV):
            fx, fy, fc = (k >> 2) & 1, (k >> 1) & 1, k & 1
            to = (1 - x if fx else x, 1 - y if fy else y, 1 - c if fc else c)
            copies.append(_remote(srcs[a], lands[a].at[me], send_sems, recv_sems, (N_DEV - 1) * a + k - 1, to))
    return copies


def _to_chips_plan(srcs, lands, send_sems, recv_sems):
    x, y, c = _position()
    copies = []
    for a in range(len(srcs)):
        for j, (cx, cy) in enumerate(_other_chips(x, y)):
            copies.append(_remote(srcs[a].at[2 * cx + cy], lands[a].at[j], send_sems, recv_sems, 3 * a + j, (cx, cy, c)))
    return copies


def copies_start(srcs, land_shapes, plan, per_array, name):
    n = len(srcs)
    n_sem = per_array * n
    lands = [lax.empty(s.shape, s.dtype) for s in land_shapes]

    def body(*refs):
        src_refs, land_refs = refs[:n], refs[n:2 * n]
        send_sems, recv_sems = refs[2 * n], refs[2 * n + 1]
        token = refs[-1]
        for cp in plan(src_refs, land_refs, send_sems, recv_sems):
            cp.start()
        token[...] = jnp.zeros_like(token)

    out = pl.pallas_call(
        body, name=name,
        out_shape=(pltpu.SemaphoreType.DMA((n_sem,)), pltpu.SemaphoreType.DMA((n_sem,)))
        + tuple(pltpu.HBM(s.shape, s.dtype) for s in srcs)
        + tuple(pltpu.HBM(s.shape, s.dtype) for s in land_shapes)
        + (jax.ShapeDtypeStruct((8, LANES), F32),),
        in_specs=[_HBM] * (2 * n),
        out_specs=(_SEM, _SEM) + (_HBM,) * (2 * n) + (pl.BlockSpec(memory_space=pltpu.VMEM),),
        input_output_aliases={i: 2 + i for i in range(2 * n)},
        compiler_params=pltpu.CompilerParams(has_side_effects=_DATAFLOW),
    )(*[pltpu.with_memory_space_constraint(s, pltpu.HBM) for s in srcs],
      *[pltpu.with_memory_space_constraint(l, pltpu.HBM) for l in lands])
    return out[:-1], out[-1]


def copies_wait(handles, plan, after, name):
    send_sems, recv_sems = handles[0], handles[1]
    n = (len(handles) - 2) // 2
    thru = handles[2:]

    def body(*refs):
        src_refs, land_refs = refs[:n], refs[n:2 * n]
        send_sems, recv_sems = refs[2 * n], refs[2 * n + 1]
        copies = plan(src_refs, land_refs, send_sems, recv_sems)
        for cp in copies:
            cp.wait_recv()
        for cp in copies:
            cp.wait_send()

    out = pl.pallas_call(
        body, name=name,
        out_shape=tuple(pltpu.HBM(t.shape, t.dtype) for t in thru),
        in_specs=[_HBM] * (2 * n) + [_SEM, _SEM, pl.BlockSpec(memory_space=pl.ANY)],
        out_specs=(_HBM,) * (2 * n),
        input_output_aliases={i: i for i in range(2 * n)},
        compiler_params=pltpu.CompilerParams(has_side_effects=_DATAFLOW),
    )(*thru, send_sems, recv_sems, after)
    return out[n:]


def all_sum_small(vec, name):
    R = vec.shape[0]

    def body(v_ref, tot_ref, all_ref, send_sems, recv_sems):
        x, y, c = _position()
        me = 4 * x + 2 * y + c
        all_ref[me] = v_ref[...]
        copies = []
        for k in range(1, N_DEV):
            fx, fy, fc = (k >> 2) & 1, (k >> 1) & 1, k & 1
            to = (1 - x if fx else x, 1 - y if fy else y, 1 - c if fc else c)
            cp = _remote(v_ref, all_ref.at[me], send_sems, recv_sems, k - 1, to)
            cp.start()
            copies.append(cp)
        for cp in copies:
            cp.wait_recv()
        for cp in copies:
            cp.wait_send()
        tot = all_ref[0]
        for j in range(1, N_DEV):
            tot = tot + all_ref[j]
        tot_ref[...] = tot

    vmem = pl.BlockSpec(memory_space=pltpu.VMEM)
    return pl.pallas_call(
        body, name=name,
        in_specs=[vmem], out_specs=vmem,
        out_shape=jax.ShapeDtypeStruct((R, LANES), F32),
        scratch_shapes=[pltpu.VMEM((N_DEV, R, LANES), F32),
                        pltpu.SemaphoreType.DMA((N_DEV - 1,)), pltpu.SemaphoreType.DMA((N_DEV - 1,))],
        compiler_params=pltpu.CompilerParams(vmem_limit_bytes=VMEM_LIMIT),
    )(vec)


def pair_add(parts, theirs, place, name):
    _, R, C = theirs.shape
    tr = _pick(R, 256, 8)

    def body(place_ref, a_ref, b_ref, o_ref):
        o_ref[...] = (a_ref[...].astype(F32) + b_ref[...].astype(F32)).astype(BF16)

    blk = pl.BlockSpec((None, tr, C), lambda q, i, place_ref: (q, i, 0))
    return pl.pallas_call(
        body, name=name,
        grid_spec=pltpu.PrefetchScalarGridSpec(
            num_scalar_prefetch=1, grid=(4, R // tr),
            in_specs=[pl.BlockSpec((None, tr, C), lambda q, i, place_ref: (2 * q + place_ref[2], i, 0)), blk],
            out_specs=blk),
        out_shape=jax.ShapeDtypeStruct(theirs.shape, BF16),
        compiler_params=_params(("parallel", "parallel")),
    )(place, parts, theirs)


def _adamw_math(w, g, m, v):
    m = ADAM_B1 * m + (1.0 - ADAM_B1) * g
    v = ADAM_B2 * v + (1.0 - ADAM_B2) * jnp.square(g)
    m_hat = m / (1.0 - ADAM_B1 ** ADAM_STEP)
    v_hat = v / (1.0 - ADAM_B2 ** ADAM_STEP)
    delta = -ADAM_LR * (m_hat / (jnp.sqrt(v_hat) + ADAM_EPS) + ADAM_WD * w)
    return delta, m, v


def adamw_sharded(w, m, v, parts, sib, others, place, name):
    R, C = w.shape
    tr = _pick(R, 256, 8)

    def body(place_ref, w_ref, m_ref, v_ref, a_ref, b_ref, o_ref, g_ref, d_ref, nm_ref, nv_ref):
        g = a_ref[...].astype(F32) + b_ref[...].astype(F32)
        for j in range(3):
            g = g + o_ref[j].astype(F32)
        delta, nm, nv = _adamw_math(w_ref[...], g, m_ref[...], v_ref[...])
        g_ref[...] = g
        d_ref[...] = delta
        nm_ref[...] = nm
        nv_ref[...] = nv

    row = pl.BlockSpec((tr, C), lambda i, place_ref: (i, 0))
    return pl.pallas_call(
        body, name=name,
        grid_spec=pltpu.PrefetchScalarGridSpec(
            num_scalar_prefetch=1, grid=(R // tr,),
            in_specs=[row] * 3 + [pl.BlockSpec((None, tr, C), lambda i, place_ref: (place_ref[0], i, 0)),
                                  pl.BlockSpec((None, tr, C), lambda i, place_ref: (place_ref[1], i, 0)),
                                  pl.BlockSpec((3, tr, C), lambda i, place_ref: (0, i, 0))],
            out_specs=[row] * 4),
        out_shape=[jax.ShapeDtypeStruct((R, C), F32)] * 4,
        compiler_params=_params(("parallel",)),
    )(place, w, m, v, parts, sib, others)


def adamw_packed(w, g, m, v, name):
    R = w.shape[0]

    def body(w_ref, g_ref, m_ref, v_ref, d_ref, nm_ref, nv_ref):
        delta, nm, nv = _adamw_math(w_ref[...], g_ref[...], m_ref[...], v_ref[...])
        d_ref[...] = delta
        nm_ref[...] = nm
        nv_ref[...] = nv

    full = pl.BlockSpec((R, LANES), lambda i: (0, 0))
    return pl.pallas_call(
        body, name=name, grid=(1,),
        in_specs=[full] * 4, out_specs=[full] * 3,
        out_shape=[jax.ShapeDtypeStruct((R, LANES), F32)] * 3,
        compiler_params=_params(("arbitrary",)),
    )(w, g, m, v)


def _pack(arrays):
    flat = []
    sizes = []
    for a in arrays:
        f = a.reshape(-1).astype(F32)
        pad = (-f.shape[0]) % LANES
        if pad:
            f = jnp.concatenate([f, jnp.zeros((pad,), F32)])
        flat.append(f)
        sizes.append(f.shape[0])
    rows = sum(sizes) // LANES
    pad_rows = (-rows) % 8
    if pad_rows:
        flat.append(jnp.zeros((pad_rows * LANES,), F32))
    return jnp.concatenate(flat).reshape(-1, LANES), sizes


def _unpack(packed, sizes, shapes):
    flat = packed.reshape(-1)
    out = []
    off = 0
    for size, shape in zip(sizes, shapes):
        n = int(np.prod(shape))
        out.append(flat[off:off + n].reshape(shape))
        off += size
    return out


def _to_blocks(full, axis):
    if axis == 0:
        return full.reshape(N_DEV, full.shape[0] // N_DEV, full.shape[1])
    r, n = full.shape
    return full.reshape(r, N_DEV, n // N_DEV).transpose(1, 0, 2)


def _from_blocks(blocks, axis):
    if axis == 0:
        return blocks.reshape(blocks.shape[0] * blocks.shape[1], blocks.shape[2])
    return blocks.transpose(1, 0, 2).reshape(blocks.shape[1], blocks.shape[0] * blocks.shape[2])


def kernel(x, ln0_g, ln0_b, w_in, b_in, conv_w, w_a, w_b, w_o, b_o, ln1_g, ln1_b, w_up, b_up, ffn_conv_w, ffn_conv_b, w_down, b_down, ln2_g, ln2_b, loss_target, m_ln0_g, m_ln0_b, m_w_in, m_b_in, m_conv_w, m_w_a, m_w_b, m_w_o, m_b_o, m_ln1_g, m_ln1_b, m_w_up, m_b_up, m_ffn_conv_w, m_ffn_conv_b, m_w_down, m_b_down, m_ln2_g, m_ln2_b, v_ln0_g, v_ln0_b, v_w_in, v_b_in, v_conv_w, v_w_a, v_w_b, v_w_o, v_b_o, v_ln1_g, v_ln1_b, v_w_up, v_b_up, v_ffn_conv_w, v_ffn_conv_b, v_w_down, v_b_down, v_ln2_g, v_ln2_b):
    T, D = x.shape[1], x.shape[2]
    F = ffn_conv_b.shape[-1]
    xs = x.reshape(T, D)
    tgt = loss_target.reshape(T, D)
    dev = 4 * lax.axis_index("x") + 2 * lax.axis_index("y") + lax.axis_index("c")
    chip = 2 * lax.axis_index("x") + lax.axis_index("y")
    core = lax.axis_index("c")
    place = jnp.stack([dev, chip, core]).astype(jnp.int32)

    big = dict(w_in=(w_in[0], 1), w_a=(w_a[0], 0), w_b=(w_b[0], 1), w_o=(w_o[0], 0), w_up=(w_up[0], 1),
               w_down=(w_down[0], 0))
    names = list(big)
    shards = {k: big[k][0].astype(BF16) for k in names}
    ln0g, ln0b = ln0_g.reshape(1, D), ln0_b.reshape(1, D)
    h0, h0b, *rest = ln_fwd(xs, None, ln0g, ln0b, "ln0_fwd_gather_w_in", dilations=DILATIONS[1:],
                            gather=[shards["w_in"], conv_w[0], ffn_conv_w[0]])
    h0_res = [h0b] + [h.reshape(T, D) for h in rest[:2]]
    g_in, g_conv, g_fcw = rest[2:]
    full = {"w_in": _from_blocks(g_in, 1)}
    conv_full = _from_blocks(g_conv, 1)
    fcw_full = _from_blocks(g_fcw, 1)
    late_groups = (("w_a", "w_b", "w_o"), ("w_up", "w_down"))
    late_handles = []
    token = conv_full[:1, :1] * 0.0
    for n, keys in enumerate(late_groups):
        srcs = [shards[k] + token[0, 0].astype(BF16) for k in keys]
        handles, token = copies_start(srcs, [jax.ShapeDtypeStruct((N_DEV,) + s.shape, BF16) for s in srcs],
                                      _to_all_plan, N_DEV - 1, f"gather_late_{n}_start")
        late_handles.append(handles)

    def late_weights(n, after):
        lands = copies_wait(late_handles[n], _to_all_plan, after, f"gather_late_{n}_wait")
        for k, land in zip(late_groups[n], lands):
            full[k] = _from_blocks(lax.dynamic_update_index_in_dim(land, shards[k], dev, 0), big[k][1])

    o_q = 3 * D
    o_g = o_q + 3 * QKV_W
    w_pa, w_qkv, w_pg = full["w_in"][:, :o_q], full["w_in"][:, o_q:o_g], full["w_in"][:, o_g:]
    b_pa, b_qkv, b_pg = b_in[:, :o_q], b_in[:, o_q:o_g], b_in[:, o_g:]

    proj_a = mm_nn(h0b, w_pa, b_pa, ACT, "proj_conv", after=token)
    proj_g = mm_nn(h0b, w_pg, b_pg, ACT, "proj_gates")
    zero_d = jnp.zeros((1, D), F32)
    s_a = conv_a_fwd(proj_a, conv_full, "conv_a_fwd")
    late_weights(0, s_a)
    y_a = mm_nn(s_a, full["w_a"], zero_d, ACT, "branch_a_out")

    def group_cols(m, g):
        return jnp.concatenate([m[:, s * QKV_W + g * GROUP_W:s * QKV_W + (g + 1) * GROUP_W] for s in range(3)], 1)

    w_grp = [group_cols(w_qkv, g) for g in range(3)]
    qkvs, outs, lses = [], [], []
    for g, d in enumerate(DILATIONS):
        qkv = mm_nn(h0_res[g], w_grp[g], group_cols(b_qkv, g), BF16, f"proj_qkv_{g}").reshape(d, T // d, 3 * GROUP_W)
        o, l = att_fwd(qkv, g, f"att_fwd_{g}")
        qkvs.append(qkv)
        outs.append(o)
        lses.append(l)
    comb = combine_fwd(outs, lses, "combine_fwd")
    y_b = mm_nn(comb, full["w_b"], zero_d, ACT, "branch_b_out")
    z = gate_fwd(proj_g, y_a, y_b, "gate_fwd")
    mix = mm_nn(z, full["w_o"], b_o, F32, "mix_out")
    h1, h1b = ln_fwd(h0, mix, ln1_g, ln1_b, "ln1_fwd")
    late_weights(1, h1b)
    up = mm_nn(h1b, full["w_up"], b_up, F32, "ffn_up")
    f_act = conv_f_fwd(up, fcw_full, ffn_conv_b, "conv_f_fwd")

    dr2, dr2b, d_ln2_g, d_ln2_b, d_b_down, loss_part = ln_bwd(
        h1, ("nn", f_act, full["w_down"], b_down), ln2_g, ln2_b, None, None, tgt, "ffn_down_ln2_loss_bwd")
    dw_down, _ = mm_tn(f_act, dr2b, "dw_down")
    df = mm_nt(dr2b, full["w_down"], None, "d_ffn_act")
    d_a, d_gate, cs_a, cs_gate, d_fcb, d_fcw = conv_f_bwd(df, up, fcw_full, ffn_conv_b, "conv_f_bwd")
    dw_up_a, _ = mm_tn(h1b, d_a, "dw_up_a")
    dw_up_g, _ = mm_tn(h1b, d_gate, "dw_up_gate")
    dr1, dr1b, d_ln1_g, d_ln1_b, d_b_o, _ = ln_bwd(h0, mix, ln1_g, ln1_b, dr2, ("nt", [d_a, d_gate], full["w_up"]), None,
                                                   "d_h1_ln1_bwd")
    dw_o, _ = mm_tn(z, dr1b, "dw_o")
    dz = mm_nt(dr1b, full["w_o"], None, "d_z", out_dtype=ACT)
    dy_a, dy_b, dproj_g = gate_bwd(dz, proj_g, y_a, y_b, "gate_bwd")
    dw_a, _ = mm_tn(s_a, dy_a, "dw_a")
    ds_a = mm_nt(dy_a, full["w_a"], None, "d_s_a", out_dtype=ACT)
    dproj_a, d_conv = conv_a_bwd(ds_a, proj_a, conv_full, "conv_a_bwd")
    dw_b, _ = mm_tn(comb, dy_b, "dw_b")

    rs_mine, rs_sib, rs_handles = {}, {}, {}

    def reduce_start(keys, grads, tag):
        parts = [_to_blocks(grads[k], big[k][1]) for k in keys]
        from_sib = exchange_sibling(parts, f"grads_to_sibling_{tag}")
        sums = [pair_add(a, b, place, f"chip_sum_{k}") for k, a, b in zip(keys, parts, from_sib)]
        handles, tok = copies_start(sums, [jax.ShapeDtypeStruct((3,) + s.shape[1:], BF16) for s in sums],
                                    _to_chips_plan, 3, f"grads_to_chips_{tag}_start")
        for k, a, b in zip(keys, parts, from_sib):
            rs_mine[k], rs_sib[k] = a, b
        rs_handles[tag] = (keys, handles)
        return tok

    tok_a = reduce_start(("w_a", "w_b", "w_o", "w_up", "w_down"),
                         dict(w_a=dw_a, w_b=dw_b, w_o=dw_o, w_up=jnp.concatenate([dw_up_a, dw_up_g], 1), w_down=dw_down),
                         "a")
    dcomb = mm_nt(dy_b, full["w_b"], None, "d_comb", after=tok_a, out_dtype=ACT)
    dos, dms = combine_bwd(dcomb, outs, lses, "combine_bwd")
    dw_grp, cs_grp, dqkvs = [], [], []
    for g, d in enumerate(DILATIONS):
        dq, dk, dv = att_bwd(qkvs[g], dos[g], lses[g], dms[g], g, f"att_bwd_{g}")
        dqkv = [t.reshape(T, GROUP_W) for t in (dq, dk, dv)]
        dwg, csg = mm_tn(h0_res[g], dqkv, f"dw_in_qkv_{g}")
        dqkvs.append(dqkv)
        dw_grp.append(dwg)
        cs_grp.append(csg)
    dw_pa, cs_pa = mm_tn(h0b, dproj_a, "dw_in_conv")
    dw_pg, cs_pg = mm_tn(h0b, dproj_g, "dw_in_gates")

    def ungroup(parts):
        return jnp.concatenate([p[:, s * GROUP_W:(s + 1) * GROUP_W] for s in range(3) for p in parts], 1)

    db_in_parts = [cs_pa, ungroup(cs_grp), cs_pg]
    tok_b = reduce_start(("w_in",), dict(w_in=jnp.concatenate([dw_pa, ungroup(dw_grp), dw_pg], 1)), "b")
    dh0 = mm_nt(dproj_a, w_pa, None, "d_h0_conv", after=tok_b)
    dh0 = mm_nt(dproj_g, w_pg, dh0, "d_h0_gates")
    dh0 = mm_nt(dqkvs[0], w_grp[0], dh0, "d_h0_qkv_0")
    dh0_res = [(mm_nt(dqkvs[g], w_grp[g], None, f"d_h0_qkv_{g}").reshape(d, T // d, D), d)
               for g, d in enumerate(DILATIONS) if g > 0]
    dx, _, d_ln0_g, d_ln0_b, _, _ = ln_bwd(xs, None, ln0g, ln0b, dr1, dh0, None, "ln0_bwd", by_residue=dh0_res)

    small = [d_ln0_g, d_ln0_b, jnp.concatenate(db_in_parts, 1), d_conv, d_b_o, d_ln1_g, d_ln1_b,
             jnp.concatenate([cs_a, cs_gate], 1), d_fcw, d_fcb, d_b_down, d_ln2_g, d_ln2_b, loss_part]
    packed, sizes = _pack(small)
    total = all_sum_small(packed, "sum_small")
    (g_ln0_g, g_ln0_b, g_b_in, g_conv_full, g_b_o, g_ln1_g, g_ln1_b, g_b_up, g_fcw_full, g_fcb, g_b_down, g_ln2_g,
     g_ln2_b, loss) = _unpack(total, sizes, [a.shape for a in small])
    cw = conv_w.shape[-1]
    fw = ffn_conv_w.shape[-1]
    g_conv = lax.dynamic_slice_in_dim(g_conv_full, dev * cw, cw, 1)
    g_fcw = lax.dynamic_slice_in_dim(g_fcw_full, dev * fw, fw, 1)

    from_chips = {}
    for tag, (keys, handles) in rs_handles.items():
        lands = copies_wait(handles, _to_chips_plan, total, f"grads_to_chips_{tag}_wait")
        from_chips.update(zip(keys, lands))

    moments = dict(w_in=(m_w_in, v_w_in), w_a=(m_w_a, v_w_a), w_b=(m_w_b, v_w_b), w_o=(m_w_o, v_w_o),
                   w_up=(m_w_up, v_w_up), w_down=(m_w_down, v_w_down))
    res_big = {}
    for k in names:
        res_big[k] = adamw_sharded(big[k][0], moments[k][0][0], moments[k][1][0], rs_mine[k], rs_sib[k], from_chips[k],
                                   place, f"adamw_{k}")

    small_names = ["ln0_g", "ln0_b", "b_in", "conv_w", "b_o", "ln1_g", "ln1_b", "b_up", "ffn_conv_w", "ffn_conv_b",
                   "b_down", "ln2_g", "ln2_b"]
    small_w = [ln0_g, ln0_b, b_in, conv_w, b_o, ln1_g, ln1_b, b_up, ffn_conv_w, ffn_conv_b, b_down, ln2_g, ln2_b]
    small_m = [m_ln0_g, m_ln0_b, m_b_in, m_conv_w, m_b_o, m_ln1_g, m_ln1_b, m_b_up, m_ffn_conv_w, m_ffn_conv_b,
               m_b_down, m_ln2_g, m_ln2_b]
    small_v = [v_ln0_g, v_ln0_b, v_b_in, v_conv_w, v_b_o, v_ln1_g, v_ln1_b, v_b_up, v_ffn_conv_w, v_ffn_conv_b,
               v_b_down, v_ln2_g, v_ln2_b]
    small_g = [g_ln0_g, g_ln0_b, g_b_in, g_conv, g_b_o, g_ln1_g, g_ln1_b, g_b_up, g_fcw, g_fcb, g_b_down, g_ln2_g,
               g_ln2_b]
    shapes = [w.shape for w in small_w]
    small_g = [g.reshape(s) for g, s in zip(small_g, shapes)]
    pw, psz = _pack(small_w)
    pg, _ = _pack(small_g)
    pm, _ = _pack(small_m)
    pv, _ = _pack(small_v)
    pd, pnm, pnv = adamw_packed(pw, pg, pm, pv, "adamw_small")
    res_small = {k: (g, d_, m_, v_) for k, g, d_, m_, v_ in zip(
        small_names, small_g, _unpack(pd, psz, shapes), _unpack(pnm, psz, shapes), _unpack(pnv, psz, shapes))}

    order = ["ln0_g", "ln0_b", "w_in", "b_in", "conv_w", "w_a", "w_b", "w_o", "b_o", "ln1_g", "ln1_b", "w_up", "b_up",
             "ffn_conv_w", "ffn_conv_b", "w_down", "b_down", "ln2_g", "ln2_b"]

    def result(k, j):
        if k in res_big:
            return res_big[k][j][None]
        return res_small[k][j]

    out = [loss.reshape(()), dx.reshape(x.shape)]
    for j in range(4):
        out += [result(k, j) for k in order]
    return tuple(out)
```

```python
import functools
import math

import numpy as np
import jax
import jax.numpy as jnp
from jax import lax
from jax.experimental import pallas as pl
from jax.experimental.pallas import tpu as pltpu

F32 = jnp.float32
BF16 = jnp.bfloat16
ACT = BF16

N_DEV = 8
LN_EPS = 1e-5
ALPHA = (2.0 * 1) ** 0.25
MASK_VALUE = -1e30
HEAD_DIM = 64
GROUP_W = 512
QKV_W = 3 * GROUP_W
DILATIONS = (1, 4, 16)
RADIUS = 64
LANES = 128
HALO = 8
HALO_BF16 = 16
ATT_TQ = 128

ADAM_LR = 0.001
ADAM_B1 = 0.9
ADAM_B2 = 0.999
ADAM_EPS = 1e-08
ADAM_WD = 0.01
ADAM_STEP = 10

VMEM_LIMIT = 52 * 1024 * 1024
OUT_TILE_BYTES = 8 * 1024 * 1024
MESH = pl.DeviceIdType.MESH
NT_DIMS = (((1,), (1,)), ((), ()))
TN_DIMS = (((0,), (0,)), ((), ()))


def _pick(n, target, align=LANES):
    if n <= target:
        return n
    best = None
    for t in range(align, target + 1, align):
        if n % t == 0:
            best = t
    assert best is not None, (n, target, align)
    return best


def _params(sems=None):
    return pltpu.CompilerParams(dimension_semantics=sems, vmem_limit_bytes=VMEM_LIMIT)


def _alibi_slopes():
    n = 3 * 8
    return np.exp2(-8.0 * np.arange(1, n + 1, dtype=np.float64) / n).astype(np.float32).reshape(3, 8)


def _ln_stats(r):
    mu = jnp.mean(r, -1, keepdims=True)
    xc = r - mu
    var = jnp.mean(xc * xc, -1, keepdims=True)
    rstd = lax.rsqrt(var + LN_EPS)
    return xc, rstd


def _load_natural(ref, d, scr):
    if d == 1:
        return ref[0]
    n, C = ref.shape[1], ref.shape[2]
    for c in range(C // LANES):
        for r in range(d):
            scr[c, pl.ds(r, n, stride=d), :] = ref[r, :, c * LANES:(c + 1) * LANES]
    return jnp.concatenate([scr[c] for c in range(C // LANES)], axis=1)


def _store_by_residue(val, ref, d, scr):
    if d == 1:
        ref[0] = val.astype(ref.dtype)
        return
    n, C = ref.shape[1], ref.shape[2]
    for c in range(C // LANES):
        scr[c] = val[:, c * LANES:(c + 1) * LANES]
    for c in range(C // LANES):
        for r in range(d):
            ref[r, :, c * LANES:(c + 1) * LANES] = scr[c, pl.ds(r, n, stride=d), :].astype(ref.dtype)


def _residue_spec(tm, d, C):
    return pl.BlockSpec((d, tm // d, C), lambda i: (0, i, 0))


def _residue_scratch(tm, C):
    return pltpu.VMEM((C // LANES, tm, LANES), F32)


def ln_fwd(a, res, g, b, name, dilations=(), gather=()):
    T, D = a.shape
    res_mm = isinstance(res, tuple)
    tm = _pick(T, 256 if res_mm else 512, 8)
    res_ins = list(res[1:]) if res_mm else ([] if res is None else [res])
    nd = len(dilations)
    ng = len(gather)
    n_in = 1 + len(res_ins) + 2
    last = T // tm - 1

    def body(*refs):
        a_ref = refs[0]
        r = a_ref[...]
        if res_mm:
            res_val = jnp.dot(refs[1][...], refs[2][...], preferred_element_type=F32) + refs[3][...]
            refs[-1 - n_scratch][...] = res_val
            r = ALPHA * r + res_val
        elif res_ins:
            r = ALPHA * r + refs[1][...]
        g_ref, b_ref = refs[n_in - 2], refs[n_in - 1]
        shard_refs = refs[n_in:n_in + ng]
        h_ref, hb_ref = refs[n_in + ng], refs[n_in + ng + 1]
        p_refs = refs[n_in + ng + 2:n_in + ng + 2 + nd]
        full_refs = refs[n_in + ng + 2 + nd:n_in + 2 * ng + 2 + nd]
        scratch = refs[len(refs) - n_scratch:]
        sems = scratch[len(scratch) - 3:] if ng else ()

        if ng:
            @pl.when(pl.program_id(0) == 0)
            def _():
                _gather_begin(shard_refs, full_refs, *sems)

        xc, rstd = _ln_stats(r)
        h = xc * rstd * g_ref[...] + b_ref[...]
        h_ref[...] = h
        hb_ref[...] = h.astype(BF16)
        for d, p_ref in zip(dilations, p_refs):
            _store_by_residue(h, p_ref, d, scratch[0])

        if ng:
            @pl.when(pl.program_id(0) == last)
            def _():
                _gather_finish(shard_refs, full_refs, *sems)

    row = pl.BlockSpec((tm, D), lambda i: (i, 0))
    vec = pl.BlockSpec((1, D), lambda i: (0, 0))
    hbm = pl.BlockSpec(memory_space=pl.ANY)
    if res_mm:
        res_specs = [pl.BlockSpec((tm, res[1].shape[1]), lambda i: (i, 0)), pl.BlockSpec(res[2].shape, lambda i: (0, 0)), vec]
    else:
        res_specs = [row] * len(res_ins)
    scratch_shapes = ([_residue_scratch(tm, D)] if nd else []) + (_gather_scratch(ng) if ng else [])
    n_scratch = len(scratch_shapes)
    ins = [a] + res_ins + [g, b] + list(gather)
    return pl.pallas_call(
        body, name=name, grid=(T // tm,),
        in_specs=[row] + res_specs + [vec, vec] + [hbm] * ng,
        out_specs=[row, row] + [_residue_spec(tm, d, D) for d in dilations] + [hbm] * ng + ([row] if res_mm else []),
        out_shape=[jax.ShapeDtypeStruct((T, D), F32), jax.ShapeDtypeStruct((T, D), BF16)]
        + [jax.ShapeDtypeStruct((d, T // d, D), BF16) for d in dilations]
        + [jax.ShapeDtypeStruct((N_DEV,) + s.shape, s.dtype) for s in gather]
        + ([jax.ShapeDtypeStruct((T, D), F32)] if res_mm else []),
        scratch_shapes=scratch_shapes,
        compiler_params=_params(("arbitrary",) if ng else ("parallel",)),
    )(*ins)


def ln_bwd(a, res, g, b, d1, d2, tgt, name, by_residue=()):
    T, D = a.shape
    tm = _pick(T, 256, 8)
    loss_mode = tgt is not None
    nres = len(by_residue)
    row = pl.BlockSpec((tm, D), lambda i: (i, 0))
    vec = pl.BlockSpec((1, D), lambda i: (0, 0))
    one = pl.BlockSpec((1, 1), lambda i: (0, 0))

    def rows_of(x):
        return pl.BlockSpec((tm, x.shape[1]), lambda i: (i, 0))

    def whole(x):
        return pl.BlockSpec(x.shape, lambda i: (0, 0))

    ins, in_specs, slots = [], [], {}

    def operand(key, arrays, specs):
        slots[key] = (len(ins), len(arrays))
        ins.extend(arrays)
        in_specs.extend(specs)

    operand("a", [a], [row])
    if isinstance(res, tuple):
        _, x, w, bias = res
        operand("res_mm", [x, w, bias], [rows_of(x), whole(w), vec])
    elif res is not None:
        operand("res", [res], [row])
    operand("gb", [g, b], [vec, vec])
    if loss_mode:
        operand("tgt", [tgt], [row])
    else:
        operand("d1", [d1], [row])
        if isinstance(d2, tuple):
            _, pieces, w = d2
            operand("d2_mm", list(pieces) + [w], [rows_of(p) for p in pieces] + [whole(w)])
        else:
            operand("d2", [d2], [row])
    operand("by_residue", [e for e, _ in by_residue], [_residue_spec(tm, d, D) for _, d in by_residue])
    n_in = len(ins)

    def body(*refs):
        def get(key):
            first, count = slots[key]
            return refs[first:first + count]

        dr_ref, drb_ref, dg_ref, db_ref, ds_ref, loss_ref = refs[n_in:n_in + 6]
        i = pl.program_id(0)

        @pl.when(i == 0)
        def _():
            dg_ref[...] = jnp.zeros_like(dg_ref)
            db_ref[...] = jnp.zeros_like(db_ref)
            ds_ref[...] = jnp.zeros_like(ds_ref)
            loss_ref[...] = jnp.zeros_like(loss_ref)

        r = get("a")[0][...]
        if "res_mm" in slots:
            x_ref, w_ref, bias_ref = get("res_mm")
            r = ALPHA * r + (jnp.dot(x_ref[...], w_ref[...], preferred_element_type=F32) + bias_ref[...])
        elif "res" in slots:
            r = ALPHA * r + get("res")[0][...]
        g_ref, b_ref = get("gb")
        xc, rstd = _ln_stats(r)
        xhat = xc * rstd
        gam = g_ref[...]
        if loss_mode:
            err = xhat * gam + b_ref[...] - get("tgt")[0][...]
            dy = err * (1.0 / D)
            row_loss = jnp.mean(err * err, -1, keepdims=True)
            loss_ref[...] += 0.5 * jnp.sum(row_loss, 0, keepdims=True)
        else:
            if "d2_mm" in slots:
                *p_refs, w_ref = get("d2_mm")
                av = p_refs[0][...] if len(p_refs) == 1 else jnp.concatenate([p[...] for p in p_refs], axis=1)
                d2v = lax.dot_general(av, w_ref[...], NT_DIMS, preferred_element_type=F32)
            else:
                d2v = get("d2")[0][...]
            dy = ALPHA * get("d1")[0][...] + d2v
        for (_, d), e_ref in zip(by_residue, get("by_residue")):
            dy = dy + _load_natural(e_ref, d, refs[-1])
        dyg = dy * gam
        c1 = jnp.mean(dyg, -1, keepdims=True)
        c2 = jnp.mean(dyg * xhat, -1, keepdims=True)
        dr = rstd * (dyg - c1 - xhat * c2)
        dr_ref[...] = dr
        drb_ref[...] = dr.astype(BF16)
        dg_ref[...] += jnp.sum(dy * xhat, 0, keepdims=True)
        db_ref[...] += jnp.sum(dy, 0, keepdims=True)
        ds_ref[...] += jnp.sum(dr, 0, keepdims=True)

    return pl.pallas_call(
        body, name=name, grid=(T // tm,),
        in_specs=in_specs,
        out_specs=[row, row, vec, vec, vec, one],
        out_shape=[jax.ShapeDtypeStruct((T, D), F32), jax.ShapeDtypeStruct((T, D), BF16),
                   jax.ShapeDtypeStruct((1, D), F32), jax.ShapeDtypeStruct((1, D), F32),
                   jax.ShapeDtypeStruct((1, D), F32), jax.ShapeDtypeStruct((1, 1), F32)],
        scratch_shapes=[_residue_scratch(tm, D)] if nres else [],
        compiler_params=_params(("arbitrary",)),
    )(*ins)


_TOKEN_SPEC = pl.BlockSpec((8, LANES), lambda i: (0, 0))


def mm_nn(a, w, bias, out_dtype, name, after=None):
    M, K = a.shape
    N = w.shape[1]
    tm = _pick(M, max(256, min(1024, OUT_TILE_BYTES // (N * jnp.dtype(out_dtype).itemsize))), 8)
    tc = _pick(N, 512)

    def body(a_ref, w_ref, b_ref, *rest):
        o_ref = rest[-1]
        av = a_ref[...]
        for j in range(N // tc):
            cols = slice(j * tc, (j + 1) * tc)
            acc = jnp.dot(av, w_ref[:, cols], preferred_element_type=F32)
            o_ref[:, cols] = (acc + b_ref[:, cols]).astype(out_dtype)

    return pl.pallas_call(
        body, name=name, grid=(M // tm,),
        in_specs=[pl.BlockSpec((tm, K), lambda i: (i, 0)),
                  pl.BlockSpec((K, N), lambda i: (0, 0)),
                  pl.BlockSpec((1, N), lambda i: (0, 0))] + ([] if after is None else [_TOKEN_SPEC]),
        out_specs=pl.BlockSpec((tm, N), lambda i: (i, 0)),
        out_shape=jax.ShapeDtypeStruct((M, N), out_dtype),
        compiler_params=_params(("parallel",)),
    )(a, w, bias, *([] if after is None else [after]))


def mm_nt(a, w, acc_in, name, after=None, w_block=0, out_dtype=F32):
    pieces = list(a) if isinstance(a, (list, tuple)) else [a]
    M = pieces[0].shape[0]
    widths = [p.shape[1] for p in pieces]
    K = sum(widths)
    N = w.shape[0]
    tm = _pick(M, 512, 8)
    tc = _pick(N, 512)
    has_acc = acc_in is not None
    n_a = len(pieces)

    def body(*refs):
        a_refs, w_ref = refs[:n_a], refs[n_a]
        c_ref = refs[n_a + 1] if has_acc else None
        o_ref = refs[-1]
        av = a_refs[0][...] if n_a == 1 else jnp.concatenate([r[...] for r in a_refs], axis=1)
        for j in range(N // tc):
            cols = slice(j * tc, (j + 1) * tc)
            acc = lax.dot_general(av, w_ref[cols, :], NT_DIMS, preferred_element_type=F32)
            if has_acc:
                acc = acc + c_ref[:, cols]
            o_ref[:, cols] = acc.astype(out_dtype)

    out_spec = pl.BlockSpec((tm, N), lambda i: (i, 0))
    in_specs = [pl.BlockSpec((tm, kw), lambda i: (i, 0)) for kw in widths]
    in_specs.append(pl.BlockSpec((N, K), lambda i: (0, w_block)))
    ins = pieces + [w]
    if has_acc:
        in_specs.append(out_spec)
        ins.append(acc_in)
    if after is not None:
        in_specs.append(_TOKEN_SPEC)
        ins.append(after)
    return pl.pallas_call(
        body, name=name, grid=(M // tm,),
        in_specs=in_specs, out_specs=out_spec,
        out_shape=jax.ShapeDtypeStruct((M, N), out_dtype),
        compiler_params=_params(("parallel",)),
    )(*ins)


def mm_tn(a, b, name, out_dtype=BF16):
    pieces = list(b) if isinstance(b, (list, tuple)) else [b]
    T, M = a.shape
    widths = [p.shape[1] for p in pieces]
    N = sum(widths)
    tk = _pick(T, 512, 8)
    nk = T // tk
    tc = _pick(M, 256)
    n_b = len(pieces)

    def body(*refs):
        a_ref, b_refs = refs[0], refs[1:1 + n_b]
        o_ref, cs_ref, acc_ref = refs[1 + n_b:]
        k = pl.program_id(0)

        @pl.when(k == 0)
        def _():
            acc_ref[...] = jnp.zeros_like(acc_ref)
            cs_ref[...] = jnp.zeros_like(cs_ref)

        bv = b_refs[0][...] if n_b == 1 else jnp.concatenate([r[...] for r in b_refs], axis=1)
        cs_ref[...] += jnp.sum(bv.astype(F32), 0, keepdims=True)
        for mi in range(M // tc):
            rows = slice(mi * tc, (mi + 1) * tc)
            acc_ref[rows, :] += lax.dot_general(a_ref[:, rows], bv, TN_DIMS, preferred_element_type=F32)

        @pl.when(k == nk - 1)
        def _():
            o_ref[...] = acc_ref[...].astype(out_dtype)

    return pl.pallas_call(
        body, name=name, grid=(nk,),
        in_specs=[pl.BlockSpec((tk, M), lambda k: (k, 0))] + [pl.BlockSpec((tk, wd), lambda k: (k, 0)) for wd in widths],
        out_specs=[pl.BlockSpec((M, N), lambda k: (0, 0)), pl.BlockSpec((1, N), lambda k: (0, 0))],
        out_shape=[jax.ShapeDtypeStruct((M, N), out_dtype), jax.ShapeDtypeStruct((1, N), F32)],
        scratch_shapes=[pltpu.VMEM((M, N), F32)],
        compiler_params=_params(("arbitrary",)),
    )(a, *pieces)


def _ext_rows(prev_ref, main_ref, next_ref, i, tm, T):
    before = jnp.where(i == 0, 0.0, prev_ref[...])
    after = jnp.where(i == T // tm - 1, 0.0, next_ref[...])
    return jnp.concatenate([before, main_ref[...], after], axis=0).astype(F32)


def _prev_row(x):
    return pltpu.roll(x, 1, 0)


def _next_row(x):
    return pltpu.roll(x, x.shape[0] - 1, 0)


def _conv3(u, w_ref):
    return _prev_row(u) * w_ref[0:1, :] + u * w_ref[1:2, :] + _next_row(u) * w_ref[2:3, :]


def _main(x, tm, halo=HALO):
    return x[halo:halo + tm]


def _halo_specs(tm, tc, T, col, order, halo=HALO):
    r = tm // halo
    last = T // halo - 1
    if order == "ij":
        return (pl.BlockSpec((halo, tc), lambda i, j: (jnp.maximum(i * r - 1, 0), col(j))),
                pl.BlockSpec((tm, tc), lambda i, j: (i, col(j))),
                pl.BlockSpec((halo, tc), lambda i, j: (jnp.minimum((i + 1) * r, last), col(j))))
    return (pl.BlockSpec((halo, tc), lambda j, i: (jnp.maximum(i * r - 1, 0), col(j))),
            pl.BlockSpec((tm, tc), lambda j, i: (i, col(j))),
            pl.BlockSpec((halo, tc), lambda j, i: (jnp.minimum((i + 1) * r, last), col(j))))


def conv_a_fwd(proj_a, conv_w, name):
    T, D3 = proj_a.shape
    D = D3 // 3
    tm = _pick(T, 256, 8)

    def body(p_ref, m_ref, n_ref, w_ref, o_ref):
        i = pl.program_id(0)
        ext = _ext_rows(p_ref, m_ref, n_ref, i, tm, T)
        u = ext[:, D:2 * D] * ext[:, 2 * D:]
        cu = _conv3(u, w_ref)
        o_ref[...] = (m_ref[:, :D].astype(F32) * _main(cu, tm, HALO_BF16)).astype(BF16)

    prev, main, nxt = _halo_specs(tm, D3, T, lambda j: 0, "ij", HALO_BF16)
    return pl.pallas_call(
        body, name=name, grid=(T // tm, 1),
        in_specs=[prev, main, nxt, pl.BlockSpec((3, D), lambda i, j: (0, 0))],
        out_specs=pl.BlockSpec((tm, D), lambda i, j: (i, 0)),
        out_shape=jax.ShapeDtypeStruct((T, D), BF16),
        compiler_params=_params(("parallel", "arbitrary")),
    )(proj_a, proj_a, proj_a, conv_w)


def conv_a_bwd(ds_a, proj_a, conv_w, name):
    T, D3 = proj_a.shape
    D = D3 // 3
    tm = _pick(T, 256, 8)

    def body(dp_ref, dm_ref, dn_ref, p_ref, m_ref, n_ref, w_ref, o_ref, dw_ref):
        i = pl.program_id(0)

        @pl.when(i == 0)
        def _():
            dw_ref[...] = jnp.zeros_like(dw_ref)

        ext = _ext_rows(p_ref, m_ref, n_ref, i, tm, T)
        dsa = _ext_rows(dp_ref, dm_ref, dn_ref, i, tm, T)
        gb, gc, hin = ext[:, :D], ext[:, D:2 * D], ext[:, 2 * D:]
        u = gc * hin
        u_prev, u_next = _prev_row(u), _next_row(u)
        cu = u_prev * w_ref[0:1, :] + u * w_ref[1:2, :] + u_next * w_ref[2:3, :]
        dcu = dsa * gb
        du = _next_row(dcu) * w_ref[0:1, :] + dcu * w_ref[1:2, :] + _prev_row(dcu) * w_ref[2:3, :]
        h = HALO_BF16
        o_ref[:, :D] = _main(dsa * cu, tm, h).astype(BF16)
        o_ref[:, D:2 * D] = _main(du * hin, tm, h).astype(BF16)
        o_ref[:, 2 * D:] = _main(du * gc, tm, h).astype(BF16)
        dcu_m = _main(dcu, tm, h)
        dw_ref[0:1, :] += jnp.sum(dcu_m * _main(u_prev, tm, h), 0, keepdims=True)
        dw_ref[1:2, :] += jnp.sum(dcu_m * _main(u, tm, h), 0, keepdims=True)
        dw_ref[2:3, :] += jnp.sum(dcu_m * _main(u_next, tm, h), 0, keepdims=True)

    dprev, dmain, dnxt = _halo_specs(tm, D, T, lambda j: 0, "ij", HALO_BF16)
    prev, main, nxt = _halo_specs(tm, D3, T, lambda j: 0, "ij", HALO_BF16)
    return pl.pallas_call(
        body, name=name, grid=(T // tm, 1),
        in_specs=[dprev, dmain, dnxt, prev, main, nxt, pl.BlockSpec((3, D), lambda i, j: (0, 0))],
        out_specs=[pl.BlockSpec((tm, D3), lambda i, j: (i, 0)), pl.BlockSpec((3, D), lambda i, j: (0, 0))],
        out_shape=[jax.ShapeDtypeStruct((T, D3), BF16), jax.ShapeDtypeStruct((3, D), F32)],
        compiler_params=_params(("arbitrary", "arbitrary")),
    )(ds_a, ds_a, ds_a, proj_a, proj_a, proj_a, conv_w)


_INV_SQRT2 = 1.0 / math.sqrt(2.0)
_INV_SQRT_2PI = 1.0 / math.sqrt(2.0 * math.pi)


def conv_f_fwd(up, fcw, fcb, name):
    T, F2 = up.shape
    F = F2 // 2
    tm = _pick(T, 256, 8)
    tc = _pick(F, 1408)
    nc = F // tc

    def body(p_ref, m_ref, n_ref, g_ref, w_ref, b_ref, o_ref):
        i = pl.program_id(0)
        a = _ext_rows(p_ref, m_ref, n_ref, i, tm, T)
        ca = _main(_conv3(a, w_ref), tm) + b_ref[...]
        gl = 0.5 * ca * (1.0 + lax.erf(ca * _INV_SQRT2))
        o_ref[...] = (gl * g_ref[...]).astype(BF16)

    prev, main, nxt = _halo_specs(tm, tc, T, lambda j: j, "ij")
    return pl.pallas_call(
        body, name=name, grid=(T // tm, nc),
        in_specs=[prev, main, nxt,
                  pl.BlockSpec((tm, tc), lambda i, j: (i, nc + j)),
                  pl.BlockSpec((3, tc), lambda i, j: (0, j)),
                  pl.BlockSpec((1, tc), lambda i, j: (0, j))],
        out_specs=pl.BlockSpec((tm, tc), lambda i, j: (i, j)),
        out_shape=jax.ShapeDtypeStruct((T, F), BF16),
        compiler_params=_params(("parallel", "parallel")),
    )(up, up, up, up, fcw, fcb)


def conv_f_bwd(df, up, fcw, fcb, name):
    T, F2 = up.shape
    F = F2 // 2
    tm = _pick(T, 256, 8)
    tc = _pick(F, 1408)
    nc = F // tc

    def body(fp_ref, fm_ref, fn_ref, ap_ref, am_ref, an_ref, gp_ref, gm_ref, gn_ref, w_ref, b_ref,
             da_ref, dg_ref, csa_ref, csg_ref, dfb_ref, dfw_ref):
        i = pl.program_id(1)

        @pl.when(i == 0)
        def _():
            csa_ref[...] = jnp.zeros_like(csa_ref)
            csg_ref[...] = jnp.zeros_like(csg_ref)
            dfb_ref[...] = jnp.zeros_like(dfb_ref)
            dfw_ref[...] = jnp.zeros_like(dfw_ref)

        dfe = _ext_rows(fp_ref, fm_ref, fn_ref, i, tm, T)
        a = _ext_rows(ap_ref, am_ref, an_ref, i, tm, T)
        gate = _ext_rows(gp_ref, gm_ref, gn_ref, i, tm, T)
        a_prev, a_next = _prev_row(a), _next_row(a)
        ca = a_prev * w_ref[0:1, :] + a * w_ref[1:2, :] + a_next * w_ref[2:3, :] + b_ref[...]
        cdf = 0.5 * (1.0 + lax.erf(ca * _INV_SQRT2))
        gl = ca * cdf
        gp = cdf + ca * (jnp.exp(-0.5 * ca * ca) * _INV_SQRT_2PI)
        dgate = _main(dfe * gl, tm)
        dca = dfe * gate * gp
        da = _main(_next_row(dca) * w_ref[0:1, :] + dca * w_ref[1:2, :] + _prev_row(dca) * w_ref[2:3, :], tm)
        da_ref[...] = da.astype(BF16)
        dg_ref[...] = dgate.astype(BF16)
        csa_ref[...] += jnp.sum(da, 0, keepdims=True)
        csg_ref[...] += jnp.sum(dgate, 0, keepdims=True)
        dca_m = _main(dca, tm)
        dfb_ref[...] += jnp.sum(dca_m, 0, keepdims=True)
        dfw_ref[0:1, :] += jnp.sum(dca_m * _main(a_prev, tm), 0, keepdims=True)
        dfw_ref[1:2, :] += jnp.sum(dca_m * _main(a, tm), 0, keepdims=True)
        dfw_ref[2:3, :] += jnp.sum(dca_m * _main(a_next, tm), 0, keepdims=True)

    fprev, fmain, fnxt = _halo_specs(tm, tc, T, lambda j: j, "ji")
    gprev, gmain, gnxt = _halo_specs(tm, tc, T, lambda j: nc + j, "ji")
    tile = pl.BlockSpec((tm, tc), lambda j, i: (i, j))
    vec = pl.BlockSpec((1, tc), lambda j, i: (0, j))
    vec3 = pl.BlockSpec((3, tc), lambda j, i: (0, j))
    return pl.pallas_call(
        body, name=name, grid=(nc, T // tm),
        in_specs=[fprev, fmain, fnxt, fprev, fmain, fnxt, gprev, gmain, gnxt, vec3, vec],
        out_specs=[tile, tile, vec, vec, vec, vec3],
        out_shape=[jax.ShapeDtypeStruct((T, F), BF16), jax.ShapeDtypeStruct((T, F), BF16),
                   jax.ShapeDtypeStruct((1, F), F32), jax.ShapeDtypeStruct((1, F), F32),
                   jax.ShapeDtypeStruct((1, F), F32), jax.ShapeDtypeStruct((3, F), F32)],
        compiler_params=_params(("arbitrary", "arbitrary")),
    )(df, df, df, up, up, up, up, up, up, fcw, fcb)


def gate_fwd(proj_g, y_a, y_b, name):
    T, D = y_a.shape
    tm = _pick(T, 512, 8)

    def body(g_ref, a_ref, b_ref, o_ref):
        sa = jax.nn.sigmoid(g_ref[:, :D].astype(F32))
        sb = jax.nn.sigmoid(g_ref[:, D:].astype(F32))
        o_ref[...] = (sa * a_ref[...].astype(F32) + sb * b_ref[...].astype(F32)).astype(BF16)

    row = pl.BlockSpec((tm, D), lambda i: (i, 0))
    return pl.pallas_call(
        body, name=name, grid=(T // tm,),
        in_specs=[pl.BlockSpec((tm, 2 * D), lambda i: (i, 0)), row, row],
        out_specs=row,
        out_shape=jax.ShapeDtypeStruct((T, D), BF16),
        compiler_params=_params(("parallel",)),
    )(proj_g, y_a, y_b)


def gate_bwd(dz, proj_g, y_a, y_b, name):
    T, D = y_a.shape
    tm = _pick(T, 512, 8)

    def body(dz_ref, g_ref, a_ref, b_ref, da_ref, db_ref, dg_ref):
        dzv = dz_ref[...].astype(F32)
        sa = jax.nn.sigmoid(g_ref[:, :D].astype(F32))
        sb = jax.nn.sigmoid(g_ref[:, D:].astype(F32))
        da_ref[...] = (dzv * sa).astype(BF16)
        db_ref[...] = (dzv * sb).astype(BF16)
        dg_ref[:, :D] = (dzv * a_ref[...].astype(F32) * (sa * (1.0 - sa))).astype(BF16)
        dg_ref[:, D:] = (dzv * b_ref[...].astype(F32) * (sb * (1.0 - sb))).astype(BF16)

    row = pl.BlockSpec((tm, D), lambda i: (i, 0))
    wide = pl.BlockSpec((tm, 2 * D), lambda i: (i, 0))
    return pl.pallas_call(
        body, name=name, grid=(T // tm,),
        in_specs=[row, wide, row, row],
        out_specs=[row, row, wide],
        out_shape=[jax.ShapeDtypeStruct((T, D), BF16), jax.ShapeDtypeStruct((T, D), BF16),
                   jax.ShapeDtypeStruct((T, 2 * D), BF16)],
        compiler_params=_params(("parallel",)),
    )(dz, proj_g, y_a, y_b)


ATT_WIN = ATT_TQ + 2 * RADIUS
ATT_STEP = 1024
FAR = 1e32


def _att_window(qs, L):
    ks = pl.multiple_of(jnp.clip(qs - RADIUS, 0, L - ATT_WIN), RADIUS)
    return ks, jnp.where(qs == 0, 0, jnp.where(qs == L - ATT_TQ, 2, 1))


def _fill_bias_tables(bias_ref, sl_ref, hp, d):
    col_row = (lax.broadcasted_iota(jnp.int32, (ATT_TQ, ATT_WIN), 1)
               - lax.broadcasted_iota(jnp.int32, (ATT_TQ, ATT_WIN), 0))
    for v in range(3):
        ad = jnp.abs(col_row - v * RADIUS)
        dist = jnp.where(ad <= RADIUS, (ad * d).astype(F32), FAR)
        bias_ref[v, 0:ATT_TQ, :] = sl_ref[hp * 2] * dist
        bias_ref[v, ATT_TQ:2 * ATT_TQ, :] = sl_ref[hp * 2 + 1] * dist


def _head_masks():
    lane = lax.broadcasted_iota(jnp.int32, (1, LANES), 1)
    return [lane < HEAD_DIM, lane >= HEAD_DIM]


def _stack_heads(x, masks):
    zero = jnp.zeros_like(x)
    return jnp.concatenate([jnp.where(masks[0], x, zero), jnp.where(masks[1], x, zero)], axis=0)


def _unstack_heads(x2, masks):
    n = x2.shape[0] // 2
    return jnp.where(masks[0], x2[:n], x2[n:])


def _att_step(L):
    step = min(ATT_STEP, L)
    assert L % step == 0 and step % ATT_TQ == 0 and L >= ATT_WIN
    return step


def _residues_per_step(d, L):
    rps = max(1, min(d, ATT_STEP // L))
    assert d % rps == 0
    return rps


def att_fwd(qkv, group, name):
    d, L, _ = qkv.shape
    step = _att_step(L)
    rps = _residues_per_step(d, L)
    cg = GROUP_W // LANES
    slopes = jnp.asarray(_alibi_slopes()[group])
    scale = HEAD_DIM ** -0.5

    def body(sl_ref, q_ref, k_ref, v_ref, o_ref, l_ref, bias_ref, s_ref, p_ref):
        hp = pl.program_id(1)
        i = pl.program_id(2)

        @pl.when(i == 0)
        def _():
            _fill_bias_tables(bias_ref, sl_ref, hp, d)

        masks = _head_masks()
        per = step // ATT_TQ
        tiles = [(rr, t) for rr in range(rps) for t in range(per)]
        windows = [_att_window(i * step + t * ATT_TQ, L) for t in range(per)]
        for n, (rr, t) in enumerate(tiles):
            rows = slice(t * ATT_TQ, (t + 1) * ATT_TQ)
            ks, table = windows[t]
            q2 = _stack_heads(q_ref[rr, rows, :] * scale, masks)
            kw = k_ref[rr, pl.ds(ks, ATT_WIN), :]
            s_ref[n] = lax.dot_general(q2, kw, NT_DIMS, preferred_element_type=F32) - bias_ref[table]
        for n, (rr, t) in enumerate(tiles):
            rows = slice(t * ATT_TQ, (t + 1) * ATT_TQ)
            s = s_ref[n]
            m = jnp.max(s, -1, keepdims=True)
            p = jnp.exp(s - m)
            den = jnp.sum(p, -1, keepdims=True)
            p_ref[n] = (p / den).astype(BF16)
            l_ref[rr, rows, :] = _unstack_heads(m + jnp.log(den), masks)
        for n, (rr, t) in enumerate(tiles):
            rows = slice(t * ATT_TQ, (t + 1) * ATT_TQ)
            vw = v_ref[rr, pl.ds(windows[t][0], ATT_WIN), :]
            o2 = jnp.dot(p_ref[n], vw, preferred_element_type=F32)
            o_ref[rr, rows, :] = _unstack_heads(o2, masks)

    n_tiles = rps * step // ATT_TQ
    out_spec = pl.BlockSpec((rps, step, LANES), lambda r, hp, i: (r, i, hp))
    return pl.pallas_call(
        body, name=name, grid=(d // rps, cg, L // step),
        in_specs=[pl.BlockSpec(memory_space=pltpu.SMEM),
                  pl.BlockSpec((rps, step, LANES), lambda r, hp, i: (r, i, hp)),
                  pl.BlockSpec((rps, L, LANES), lambda r, hp, i: (r, 0, cg + hp)),
                  pl.BlockSpec((rps, L, LANES), lambda r, hp, i: (r, 0, 2 * cg + hp))],
        out_specs=[out_spec, out_spec],
        out_shape=[jax.ShapeDtypeStruct((d, L, GROUP_W), F32)] * 2,
        scratch_shapes=[pltpu.VMEM((3, 2 * ATT_TQ, ATT_WIN), F32),
                        pltpu.VMEM((n_tiles, 2 * ATT_TQ, ATT_WIN), F32),
                        pltpu.VMEM((n_tiles, 2 * ATT_TQ, ATT_WIN), BF16)],
        compiler_params=_params(("arbitrary", "arbitrary", "arbitrary")),
    )(slopes, qkv, qkv, qkv)


def att_bwd(qkv, do, lse, dmat, group, name):
    d, L, _ = qkv.shape
    step = _att_step(L)
    rps = _residues_per_step(d, L)
    nq = L // step
    cg = GROUP_W // LANES
    slopes = jnp.asarray(_alibi_slopes()[group])
    scale = HEAD_DIM ** -0.5

    def body(sl_ref, q_ref, k_ref, v_ref, do_ref, l_ref, dm_ref, dq_ref, dk_ref, dv_ref, dk_acc, dv_acc, bias_ref,
             s_ref, dp_ref, p_ref, ds_ref):
        hp = pl.program_id(1)
        i = pl.program_id(2)

        @pl.when(i == 0)
        def _():
            dk_acc[...] = jnp.zeros_like(dk_acc)
            dv_acc[...] = jnp.zeros_like(dv_acc)
            _fill_bias_tables(bias_ref, sl_ref, hp, d)

        masks = _head_masks()

        def head_cols(x):
            return jnp.concatenate([jnp.max(jnp.where(hm, x, -jnp.inf), -1, keepdims=True) for hm in masks], axis=0)

        per = step // ATT_TQ
        tiles = [(rr, t) for rr in range(rps) for t in range(per)]
        windows = [_att_window(i * step + t * ATT_TQ, L) for t in range(per)]

        def stacked(ref, rr, t, factor=None):
            x = ref[rr, t * ATT_TQ:(t + 1) * ATT_TQ, :]
            return _stack_heads(x if factor is None else x * factor, masks)

        for n, (rr, t) in enumerate(tiles):
            ks, table = windows[t]
            q2 = stacked(q_ref, rr, t, scale)
            s_ref[n] = lax.dot_general(q2, k_ref[rr, pl.ds(ks, ATT_WIN), :], NT_DIMS,
                                       preferred_element_type=F32) - bias_ref[table]
            dp_ref[n] = lax.dot_general(stacked(do_ref, rr, t), v_ref[rr, pl.ds(ks, ATT_WIN), :], NT_DIMS,
                                        preferred_element_type=F32)
        for n, (rr, t) in enumerate(tiles):
            rows = slice(t * ATT_TQ, (t + 1) * ATT_TQ)
            p = jnp.exp(s_ref[n] - head_cols(l_ref[rr, rows, :]))
            p_ref[n] = p.astype(BF16)
            ds_ref[n] = (p * (dp_ref[n] - head_cols(dm_ref[rr, rows, :]))).astype(BF16)
        for n, (rr, t) in enumerate(tiles):
            rows = slice(t * ATT_TQ, (t + 1) * ATT_TQ)
            ks = windows[t][0]
            ds = ds_ref[n]
            dq2 = jnp.dot(ds, k_ref[rr, pl.ds(ks, ATT_WIN), :], preferred_element_type=F32)
            dq_ref[rr, rows, :] = (_unstack_heads(dq2, masks) * scale).astype(BF16)
            dk_acc[rr, pl.ds(ks, ATT_WIN), :] += lax.dot_general(ds, stacked(q_ref, rr, t, scale), TN_DIMS,
                                                                 preferred_element_type=F32)
            dv_acc[rr, pl.ds(ks, ATT_WIN), :] += lax.dot_general(p_ref[n], stacked(do_ref, rr, t), TN_DIMS,
                                                                 preferred_element_type=F32)

        @pl.when(i == nq - 1)
        def _():
            dk_ref[...] = dk_acc[...].astype(BF16)
            dv_ref[...] = dv_acc[...].astype(BF16)

    tile = pl.BlockSpec((rps, step, LANES), lambda r, hp, i: (r, i, hp))
    whole = pl.BlockSpec((rps, L, LANES), lambda r, hp, i: (r, 0, hp))
    return pl.pallas_call(
        body, name=name, grid=(d // rps, cg, nq),
        in_specs=[pl.BlockSpec(memory_space=pltpu.SMEM), tile,
                  pl.BlockSpec((rps, L, LANES), lambda r, hp, i: (r, 0, cg + hp)),
                  pl.BlockSpec((rps, L, LANES), lambda r, hp, i: (r, 0, 2 * cg + hp)),
                  tile, tile, tile],
        out_specs=[tile, whole, whole],
        out_shape=[jax.ShapeDtypeStruct((d, L, GROUP_W), BF16)] * 3,
        scratch_shapes=[pltpu.VMEM((rps, L, LANES), F32), pltpu.VMEM((rps, L, LANES), F32),
                        pltpu.VMEM((3, 2 * ATT_TQ, ATT_WIN), F32)]
        + [pltpu.VMEM((rps * step // ATT_TQ, 2 * ATT_TQ, ATT_WIN), dt) for dt in (F32, F32, BF16, BF16)],
        compiler_params=_params(("arbitrary", "arbitrary", "arbitrary")),
    )(slopes, qkv, qkv, qkv, do, lse, dmat)


def _group_weights(ls):
    m = jnp.maximum(jnp.maximum(ls[0], ls[1]), ls[2])
    es = [jnp.exp(l - m) for l in ls]
    tot = es[0] + es[1] + es[2]
    return [e / tot for e in es]


def combine_fwd(outs, lses, name):
    T = outs[0].shape[0] * outs[0].shape[1]
    tm = _pick(T, 512, 8)
    n_scr = 2 * (len(DILATIONS) - 1)

    def body(*refs):
        o_refs, l_refs, c_ref, scr = refs[:3], refs[3:6], refs[6], refs[7:]
        o = [_load_natural(o_refs[g], d, scr[g - 1] if g else None) for g, d in enumerate(DILATIONS)]
        l = [_load_natural(l_refs[g], d, scr[g + 1] if g else None) for g, d in enumerate(DILATIONS)]
        w = _group_weights(l)
        c_ref[...] = (w[0] * o[0] + w[1] * o[1] + w[2] * o[2]).astype(BF16)

    specs = [_residue_spec(tm, d, GROUP_W) for d in DILATIONS]
    return pl.pallas_call(
        body, name=name, grid=(T // tm,),
        in_specs=specs + specs, out_specs=pl.BlockSpec((tm, GROUP_W), lambda i: (i, 0)),
        out_shape=jax.ShapeDtypeStruct((T, GROUP_W), BF16),
        scratch_shapes=[_residue_scratch(tm, GROUP_W)] * n_scr,
        compiler_params=_params(("parallel",)),
    )(*outs, *lses)


def combine_bwd(dcomb, outs, lses, name):
    T = dcomb.shape[0]
    tm = _pick(T, 256, 8)
    head = np.arange(GROUP_W) // HEAD_DIM
    seg = jnp.asarray((head[:, None] == head[None, :]).astype(np.float32)).astype(BF16)
    ng = len(DILATIONS)
    n_scr = 4 * (ng - 1)

    def body(*refs):
        dc_ref, o_refs, l_refs, e_ref = refs[0], refs[1:1 + ng], refs[1 + ng:1 + 2 * ng], refs[1 + 2 * ng]
        do_refs, dm_refs = refs[2 + 2 * ng:2 + 3 * ng], refs[2 + 3 * ng:2 + 4 * ng]
        scr = refs[2 + 4 * ng:]
        o = [_load_natural(o_refs[g], d, scr[4 * (g - 1)] if g else None) for g, d in enumerate(DILATIONS)]
        l = [_load_natural(l_refs[g], d, scr[4 * (g - 1) + 1] if g else None) for g, d in enumerate(DILATIONS)]
        w = _group_weights(l)
        dc = dc_ref[...].astype(F32)
        e = e_ref[...]
        prod = dc * (w[0] * o[0] + w[1] * o[1] + w[2] * o[2])
        tot = jnp.zeros_like(dc)
        for _ in range(3):
            part = prod.astype(BF16)
            tot = tot + jnp.dot(part, e, preferred_element_type=F32)
            prod = prod - part.astype(F32)
        for g, d in enumerate(DILATIONS):
            _store_by_residue(w[g] * dc, do_refs[g], d, scr[4 * (g - 1) + 2] if g else None)
            _store_by_residue(w[g] * tot, dm_refs[g], d, scr[4 * (g - 1) + 3] if g else None)

    specs = [_residue_spec(tm, d, GROUP_W) for d in DILATIONS]
    res = pl.pallas_call(
        body, name=name, grid=(T // tm,),
        in_specs=[pl.BlockSpec((tm, GROUP_W), lambda i: (i, 0))] + specs + specs
        + [pl.BlockSpec((GROUP_W, GROUP_W), lambda i: (0, 0))],
        out_specs=specs + specs,
        out_shape=[jax.ShapeDtypeStruct(o.shape, BF16) for o in outs] + [jax.ShapeDtypeStruct(o.shape, F32) for o in outs],
        scratch_shapes=[_residue_scratch(tm, GROUP_W)] * n_scr,
        compiler_params=_params(("parallel",)),
    )(dcomb, *outs, *lses, seg)
    return res[:ng], res[ng:]


def _position():
    return lax.axis_index("x"), lax.axis_index("y"), lax.axis_index("c")


def _other_chips(x, y):
    return [(1 - x, y), (x, 1 - y), (1 - x, 1 - y)]


def _remote(src, dst, send_sems, recv_sems, k, to):
    return pltpu.make_async_remote_copy(src_ref=src, dst_ref=dst, send_sem=send_sems.at[k], recv_sem=recv_sems.at[k],
                                        device_id=to, device_id_type=MESH)


def _gather_descriptors(ins, outs, send_sems, recv_sems, local_sems):
    n = len(ins)
    x, y, c = _position()
    sibling = (x, y, 1 - c)
    chips = _other_chips(x, y)

    def block(a, px, py, pc):
        return outs[a].at[4 * px + 2 * py + pc]

    own, first, arrivals = [], [], []
    for a in range(n):
        k0 = 7 * a
        mine = block(a, x, y, c)
        own.append(pltpu.make_async_copy(ins[a], mine, local_sems.at[a]))
        first.append(_remote(ins[a], mine, send_sems, recv_sems, k0, sibling))
        row = []
        for j, chip in enumerate(chips):
            first.append(_remote(ins[a], mine, send_sems, recv_sems, k0 + 1 + j, (*chip, c)))
            got = block(a, *chip, c)
            row.append((_remote(got, got, send_sems, recv_sems, k0 + 1 + j, sibling),
                        _remote(got, got, send_sems, recv_sems, k0 + 4 + j, sibling)))
        arrivals.append(row)
    return own, first, arrivals


def _gather_begin(ins, outs, send_sems, recv_sems, local_sems):
    own, first, _ = _gather_descriptors(ins, outs, send_sems, recv_sems, local_sems)
    for cp in own + first:
        cp.start()


def _gather_finish(ins, outs, send_sems, recv_sems, local_sems):
    own, first, arrivals = _gather_descriptors(ins, outs, send_sems, recv_sems, local_sems)
    passed = []
    for row in arrivals:
        for arrived, onward in row:
            arrived.wait_recv()
            onward.start()
            passed.append(onward)
    for a in range(len(ins)):
        first[4 * a].wait_recv()
        for _, onward in arrivals[a]:
            onward.wait_recv()
    for cp in first + passed:
        cp.wait_send()
    for cp in own:
        cp.wait()


def _gather_scratch(n):
    return [pltpu.SemaphoreType.DMA((7 * n,)), pltpu.SemaphoreType.DMA((7 * n,)), pltpu.SemaphoreType.DMA((n,))]


def all_gather(shards, name):
    n = len(shards)

    def body(*refs):
        ins, outs, sems = refs[:n], refs[n:2 * n], refs[2 * n:]
        _gather_begin(ins, outs, *sems)
        _gather_finish(ins, outs, *sems)

    hbm = pl.BlockSpec(memory_space=pl.ANY)
    return pl.pallas_call(
        body, name=name,
        in_specs=[hbm] * n, out_specs=[hbm] * n,
        out_shape=[jax.ShapeDtypeStruct((N_DEV,) + s.shape, s.dtype) for s in shards],
        scratch_shapes=_gather_scratch(n),
    )(*shards)


def exchange_sibling(parts, name):
    n = len(parts)

    def body(*refs):
        ins, outs = refs[:n], refs[n:2 * n]
        send_sems, recv_sems = refs[2 * n:]
        x, y, c = _position()
        sibling = (x, y, 1 - c)
        copies = []
        for a in range(n):
            for q in range(4):
                cp = _remote(ins[a].at[2 * q + (1 - c)], outs[a].at[q], send_sems, recv_sems, 4 * a + q, sibling)
                cp.start()
                copies.append(cp)
        for cp in copies:
            cp.wait_recv()
        for cp in copies:
            cp.wait_send()

    hbm = pl.BlockSpec(memory_space=pl.ANY)
    return pl.pallas_call(
        body, name=name,
        in_specs=[hbm] * n, out_specs=[hbm] * n,
        out_shape=[jax.ShapeDtypeStruct((4,) + p.shape[1:], p.dtype) for p in parts],
        scratch_shapes=[pltpu.SemaphoreType.DMA((4 * n,)), pltpu.SemaphoreType.DMA((4 * n,))],
    )(*parts)


def exchange_chips(sums, name):
    n = len(sums)

    def body(*refs):
        ins, outs = refs[:n], refs[n:2 * n]
        send_sems, recv_sems = refs[2 * n:]
        x, y, c = _position()
        copies = []
        for a in range(n):
            for j, (cx, cy) in enumerate(_other_chips(x, y)):
                cp = _remote(ins[a].at[2 * cx + cy], outs[a].at[j], send_sems, recv_sems, 3 * a + j, (cx, cy, c))
                cp.start()
                copies.append(cp)
        for cp in copies:
            cp.wait_recv()
        for cp in copies:
            cp.wait_send()

    hbm = pl.BlockSpec(memory_space=pl.ANY)
    return pl.pallas_call(
        body, name=name,
        in_specs=[hbm] * n, out_specs=[hbm] * n,
        out_shape=[jax.ShapeDtypeStruct((3,) + s.shape[1:], s.dtype) for s in sums],
        scratch_shapes=[pltpu.SemaphoreType.DMA((3 * n,)), pltpu.SemaphoreType.DMA((3 * n,))],
    )(*sums)


_HBM = pl.BlockSpec(memory_space=pltpu.HBM)
_SEM = pl.BlockSpec(memory_space=pltpu.SEMAPHORE)
_DATAFLOW = pltpu.SideEffectType.DATAFLOW_SIDE_EFFECTING


def _to_all_plan(srcs, lands, send_sems, recv_sems):
    x, y, c = _position()
    me = 4 * x + 2 * y + c
    copies = []
    for a in range(len(srcs)):
        for k in range(1, N_DEV):
            fx, fy, fc = (k >> 2) & 1, (k >> 1) & 1, k & 1
            to = (1 - x if fx else x, 1 - y if fy else y, 1 - c if fc else c)
            copies.append(_remote(srcs[a], lands[a].at[me], send_sems, recv_sems, (N_DEV - 1) * a + k - 1, to))
    return copies


def _to_chips_plan(srcs, lands, send_sems, recv_sems):
    x, y, c = _position()
    copies = []
    for a in range(len(srcs)):
        for j, (cx, cy) in enumerate(_other_chips(x, y)):
            copies.append(_remote(srcs[a].at[2 * cx + cy], lands[a].at[j], send_sems, recv_sems, 3 * a + j, (cx, cy, c)))
    return copies


def copies_start(srcs, land_shapes, plan, per_array, name):
    n = len(srcs)
    n_sem = per_array * n
    lands = [lax.empty(s.shape, s.dtype) for s in land_shapes]

    def body(*refs):
        src_refs, land_refs = refs[:n], refs[n:2 * n]
        send_sems, recv_sems = refs[2 * n], refs[2 * n + 1]
        token = refs[-1]
        for cp in plan(src_refs, land_refs, send_sems, recv_sems):
            cp.start()
        token[...] = jnp.zeros_like(token)

    out = pl.pallas_call(
        body, name=name,
        out_shape=(pltpu.SemaphoreType.DMA((n_sem,)), pltpu.SemaphoreType.DMA((n_sem,)))
        + tuple(pltpu.HBM(s.shape, s.dtype) for s in srcs)
        + tuple(pltpu.HBM(s.shape, s.dtype) for s in land_shapes)
        + (jax.ShapeDtypeStruct((8, LANES), F32),),
        in_specs=[_HBM] * (2 * n),
        out_specs=(_SEM, _SEM) + (_HBM,) * (2 * n) + (pl.BlockSpec(memory_space=pltpu.VMEM),),
        input_output_aliases={i: 2 + i for i in range(2 * n)},
        compiler_params=pltpu.CompilerParams(has_side_effects=_DATAFLOW),
    )(*[pltpu.with_memory_space_constraint(s, pltpu.HBM) for s in srcs],
      *[pltpu.with_memory_space_constraint(l, pltpu.HBM) for l in lands])
    return out[:-1], out[-1]


def copies_wait(handles, plan, after, name):
    send_sems, recv_sems = handles[0], handles[1]
    n = (len(handles) - 2) // 2
    thru = handles[2:]

    def body(*refs):
        src_refs, land_refs = refs[:n], refs[n:2 * n]
        send_sems, recv_sems = refs[2 * n], refs[2 * n + 1]
        copies = plan(src_refs, land_refs, send_sems, recv_sems)
        for cp in copies:
            cp.wait_recv()
        for cp in copies:
            cp.wait_send()

    out = pl.pallas_call(
        body, name=name,
        out_shape=tuple(pltpu.HBM(t.shape, t.dtype) for t in thru),
        in_specs=[_HBM] * (2 * n) + [_SEM, _SEM, pl.BlockSpec(memory_space=pl.ANY)],
        out_specs=(_HBM,) * (2 * n),
        input_output_aliases={i: i for i in range(2 * n)},
        compiler_params=pltpu.CompilerParams(has_side_effects=_DATAFLOW),
    )(*thru, send_sems, recv_sems, after)
    return out[n:]


def all_sum_small(vec, name):
    R = vec.shape[0]

    def body(v_ref, tot_ref, all_ref, send_sems, recv_sems):
        x, y, c = _position()
        me = 4 * x + 2 * y + c
        all_ref[me] = v_ref[...]
        copies = []
        for k in range(1, N_DEV):
            fx, fy, fc = (k >> 2) & 1, (k >> 1) & 1, k & 1
            to = (1 - x if fx else x, 1 - y if fy else y, 1 - c if fc else c)
            cp = _remote(v_ref, all_ref.at[me], send_sems, recv_sems, k - 1, to)
            cp.start()
            copies.append(cp)
        for cp in copies:
            cp.wait_recv()
        for cp in copies:
            cp.wait_send()
        tot = all_ref[0]
        for j in range(1, N_DEV):
            tot = tot + all_ref[j]
        tot_ref[...] = tot

    vmem = pl.BlockSpec(memory_space=pltpu.VMEM)
    return pl.pallas_call(
        body, name=name,
        in_specs=[vmem], out_specs=vmem,
        out_shape=jax.ShapeDtypeStruct((R, LANES), F32),
        scratch_shapes=[pltpu.VMEM((N_DEV, R, LANES), F32),
                        pltpu.SemaphoreType.DMA((N_DEV - 1,)), pltpu.SemaphoreType.DMA((N_DEV - 1,))],
        compiler_params=pltpu.CompilerParams(vmem_limit_bytes=VMEM_LIMIT),
    )(vec)


def pair_add(parts, theirs, place, name):
    _, R, C = theirs.shape
    tr = _pick(R, 256, 8)

    def body(place_ref, a_ref, b_ref, o_ref):
        o_ref[...] = (a_ref[...].astype(F32) + b_ref[...].astype(F32)).astype(BF16)

    blk = pl.BlockSpec((None, tr, C), lambda q, i, place_ref: (q, i, 0))
    return pl.pallas_call(
        body, name=name,
        grid_spec=pltpu.PrefetchScalarGridSpec(
            num_scalar_prefetch=1, grid=(4, R // tr),
            in_specs=[pl.BlockSpec((None, tr, C), lambda q, i, place_ref: (2 * q + place_ref[2], i, 0)), blk],
            out_specs=blk),
        out_shape=jax.ShapeDtypeStruct(theirs.shape, BF16),
        compiler_params=_params(("parallel", "parallel")),
    )(place, parts, theirs)


def _adamw_math(w, g, m, v):
    m = ADAM_B1 * m + (1.0 - ADAM_B1) * g
    v = ADAM_B2 * v + (1.0 - ADAM_B2) * jnp.square(g)
    m_hat = m / (1.0 - ADAM_B1 ** ADAM_STEP)
    v_hat = v / (1.0 - ADAM_B2 ** ADAM_STEP)
    delta = -ADAM_LR * (m_hat / (jnp.sqrt(v_hat) + ADAM_EPS) + ADAM_WD * w)
    return delta, m, v


def adamw_sharded(w, m, v, parts, sib, others, place, name):
    R, C = w.shape
    tr = _pick(R, 256, 8)

    def body(place_ref, w_ref, m_ref, v_ref, a_ref, b_ref, o_ref, g_ref, d_ref, nm_ref, nv_ref):
        g = a_ref[...].astype(F32) + b_ref[...].astype(F32)
        for j in range(3):
            g = g + o_ref[j].astype(F32)
        delta, nm, nv = _adamw_math(w_ref[...], g, m_ref[...], v_ref[...])
        g_ref[...] = g
        d_ref[...] = delta
        nm_ref[...] = nm
        nv_ref[...] = nv

    row = pl.BlockSpec((tr, C), lambda i, place_ref: (i, 0))
    return pl.pallas_call(
        body, name=name,
        grid_spec=pltpu.PrefetchScalarGridSpec(
            num_scalar_prefetch=1, grid=(R // tr,),
            in_specs=[row] * 3 + [pl.BlockSpec((None, tr, C), lambda i, place_ref: (place_ref[0], i, 0)),
                                  pl.BlockSpec((None, tr, C), lambda i, place_ref: (place_ref[1], i, 0)),
                                  pl.BlockSpec((3, tr, C), lambda i, place_ref: (0, i, 0))],
            out_specs=[row] * 4),
        out_shape=[jax.ShapeDtypeStruct((R, C), F32)] * 4,
        compiler_params=_params(("parallel",)),
    )(place, w, m, v, parts, sib, others)


def adamw_packed(w, g, m, v, name):
    R = w.shape[0]

    def body(w_ref, g_ref, m_ref, v_ref, d_ref, nm_ref, nv_ref):
        delta, nm, nv = _adamw_math(w_ref[...], g_ref[...], m_ref[...], v_ref[...])
        d_ref[...] = delta
        nm_ref[...] = nm
        nv_ref[...] = nv

    full = pl.BlockSpec((R, LANES), lambda i: (0, 0))
    return pl.pallas_call(
        body, name=name, grid=(1,),
        in_specs=[full] * 4, out_specs=[full] * 3,
        out_shape=[jax.ShapeDtypeStruct((R, LANES), F32)] * 3,
        compiler_params=_params(("arbitrary",)),
    )(w, g, m, v)


def _pack(arrays):
    flat = []
    sizes = []
    for a in arrays:
        f = a.reshape(-1).astype(F32)
        pad = (-f.shape[0]) % LANES
        if pad:
            f = jnp.concatenate([f, jnp.zeros((pad,), F32)])
        flat.append(f)
        sizes.append(f.shape[0])
    rows = sum(sizes) // LANES
    pad_rows = (-rows) % 8
    if pad_rows:
        flat.append(jnp.zeros((pad_rows * LANES,), F32))
    return jnp.concatenate(flat).reshape(-1, LANES), sizes


def _unpack(packed, sizes, shapes):
    flat = packed.reshape(-1)
    out = []
    off = 0
    for size, shape in zip(sizes, shapes):
        n = int(np.prod(shape))
        out.append(flat[off:off + n].reshape(shape))
        off += size
    return out


def _to_blocks(full, axis):
    if axis == 0:
        return full.reshape(N_DEV, full.shape[0] // N_DEV, full.shape[1])
    r, n = full.shape
    return full.reshape(r, N_DEV, n // N_DEV).transpose(1, 0, 2)


def _from_blocks(blocks, axis):
    if axis == 0:
        return blocks.reshape(blocks.shape[0] * blocks.shape[1], blocks.shape[2])
    return blocks.transpose(1, 0, 2).reshape(blocks.shape[1], blocks.shape[0] * blocks.shape[2])


def kernel(x, ln0_g, ln0_b, w_in, b_in, conv_w, w_a, w_b, w_o, b_o, ln1_g, ln1_b, w_up, b_up, ffn_conv_w, ffn_conv_b, w_down, b_down, ln2_g, ln2_b, loss_target, m_ln0_g, m_ln0_b, m_w_in, m_b_in, m_conv_w, m_w_a, m_w_b, m_w_o, m_b_o, m_ln1_g, m_ln1_b, m_w_up, m_b_up, m_ffn_conv_w, m_ffn_conv_b, m_w_down, m_b_down, m_ln2_g, m_ln2_b, v_ln0_g, v_ln0_b, v_w_in, v_b_in, v_conv_w, v_w_a, v_w_b, v_w_o, v_b_o, v_ln1_g, v_ln1_b, v_w_up, v_b_up, v_ffn_conv_w, v_ffn_conv_b, v_w_down, v_b_down, v_ln2_g, v_ln2_b):
    T, D = x.shape[1], x.shape[2]
    F = ffn_conv_b.shape[-1]
    xs = x.reshape(T, D)
    tgt = loss_target.reshape(T, D)
    dev = 4 * lax.axis_index("x") + 2 * lax.axis_index("y") + lax.axis_index("c")
    chip = 2 * lax.axis_index("x") + lax.axis_index("y")
    core = lax.axis_index("c")
    place = jnp.stack([dev, chip, core]).astype(jnp.int32)

    big = dict(w_in=(w_in[0], 1), w_a=(w_a[0], 0), w_b=(w_b[0], 1), w_o=(w_o[0], 0), w_up=(w_up[0], 1),
               w_down=(w_down[0], 0))
    names = list(big)
    shards = {k: big[k][0].astype(BF16) for k in names}
    ln0g, ln0b = ln0_g.reshape(1, D), ln0_b.reshape(1, D)
    h0, h0b, *rest = ln_fwd(xs, None, ln0g, ln0b, "ln0_fwd_gather_w_in", dilations=DILATIONS[1:],
                            gather=[shards["w_in"], conv_w[0], ffn_conv_w[0]])
    h0_res = [h0b] + [h.reshape(T, D) for h in rest[:2]]
    g_in, g_conv, g_fcw = rest[2:]
    full = {"w_in": _from_blocks(g_in, 1)}
    conv_full = _from_blocks(g_conv, 1)
    fcw_full = _from_blocks(g_fcw, 1)
    late_groups = (("w_a", "w_b", "w_o"), ("w_up", "w_down"))
    late_handles = []
    token = conv_full[:1, :1] * 0.0
    for n, keys in enumerate(late_groups):
        srcs = [shards[k] + token[0, 0].astype(BF16) for k in keys]
        handles, token = copies_start(srcs, [jax.ShapeDtypeStruct((N_DEV,) + s.shape, BF16) for s in srcs],
                                      _to_all_plan, N_DEV - 1, f"gather_late_{n}_start")
        late_handles.append(handles)

    def late_weights(n, after):
        lands = copies_wait(late_handles[n], _to_all_plan, after, f"gather_late_{n}_wait")
        for k, land in zip(late_groups[n], lands):
            full[k] = _from_blocks(lax.dynamic_update_index_in_dim(land, shards[k], dev, 0), big[k][1])

    o_q = 3 * D
    o_g = o_q + 3 * QKV_W
    w_pa, w_qkv, w_pg = full["w_in"][:, :o_q], full["w_in"][:, o_q:o_g], full["w_in"][:, o_g:]
    b_pa, b_qkv, b_pg = b_in[:, :o_q], b_in[:, o_q:o_g], b_in[:, o_g:]

    proj_a = mm_nn(h0b, w_pa, b_pa, ACT, "proj_conv", after=token)
    proj_g = mm_nn(h0b, w_pg, b_pg, ACT, "proj_gates")
    zero_d = jnp.zeros((1, D), F32)
    s_a = conv_a_fwd(proj_a, conv_full, "conv_a_fwd")
    late_weights(0, s_a)
    y_a = mm_nn(s_a, full["w_a"], zero_d, ACT, "branch_a_out")

    def group_cols(m, g):
        return jnp.concatenate([m[:, s * QKV_W + g * GROUP_W:s * QKV_W + (g + 1) * GROUP_W] for s in range(3)], 1)

    w_grp = [group_cols(w_qkv, g) for g in range(3)]
    qkvs, outs, lses = [], [], []
    for g, d in enumerate(DILATIONS):
        qkv = mm_nn(h0_res[g], w_grp[g], group_cols(b_qkv, g), BF16, f"proj_qkv_{g}").reshape(d, T // d, 3 * GROUP_W)
        o, l = att_fwd(qkv, g, f"att_fwd_{g}")
        qkvs.append(qkv)
        outs.append(o)
        lses.append(l)
    comb = combine_fwd(outs, lses, "combine_fwd")
    y_b = mm_nn(comb, full["w_b"], zero_d, ACT, "branch_b_out")
    z = gate_fwd(proj_g, y_a, y_b, "gate_fwd")
    h1, h1b, mix = ln_fwd(h0, ("nn", z, full["w_o"], b_o), ln1_g, ln1_b, "mix_out_ln1_fwd")
    late_weights(1, h1b)
    up = mm_nn(h1b, full["w_up"], b_up, F32, "ffn_up")
    f_act = conv_f_fwd(up, fcw_full, ffn_conv_b, "conv_f_fwd")

    dr2, dr2b, d_ln2_g, d_ln2_b, d_b_down, loss_part = ln_bwd(
        h1, ("nn", f_act, full["w_down"], b_down), ln2_g, ln2_b, None, None, tgt, "ffn_down_ln2_loss_bwd")
    dw_down, _ = mm_tn(f_act, dr2b, "dw_down")
    df = mm_nt(dr2b, full["w_down"], None, "d_ffn_act")
    d_a, d_gate, cs_a, cs_gate, d_fcb, d_fcw = conv_f_bwd(df, up, fcw_full, ffn_conv_b, "conv_f_bwd")
    dw_up_a, _ = mm_tn(h1b, d_a, "dw_up_a")
    dw_up_g, _ = mm_tn(h1b, d_gate, "dw_up_gate")
    dr1, dr1b, d_ln1_g, d_ln1_b, d_b_o, _ = ln_bwd(h0, mix, ln1_g, ln1_b, dr2, ("nt", [d_a, d_gate], full["w_up"]), None,
                                                   "d_h1_ln1_bwd")
    dw_o, _ = mm_tn(z, dr1b, "dw_o")
    dz = mm_nt(dr1b, full["w_o"], None, "d_z", out_dtype=ACT)
    dy_a, dy_b, dproj_g = gate_bwd(dz, proj_g, y_a, y_b, "gate_bwd")
    dw_a, _ = mm_tn(s_a, dy_a, "dw_a")
    ds_a = mm_nt(dy_a, full["w_a"], None, "d_s_a", out_dtype=ACT)
    dproj_a, d_conv = conv_a_bwd(ds_a, proj_a, conv_full, "conv_a_bwd")
    dw_b, _ = mm_tn(comb, dy_b, "dw_b")

    rs_mine, rs_sib, rs_handles = {}, {}, {}

    def reduce_start(keys, grads, tag):
        parts = [_to_blocks(grads[k], big[k][1]) for k in keys]
        from_sib = exchange_sibling(parts, f"grads_to_sibling_{tag}")
        sums = [pair_add(a, b, place, f"chip_sum_{k}") for k, a, b in zip(keys, parts, from_sib)]
        handles, tok = copies_start(sums, [jax.ShapeDtypeStruct((3,) + s.shape[1:], BF16) for s in sums],
                                    _to_chips_plan, 3, f"grads_to_chips_{tag}_start")
        for k, a, b in zip(keys, parts, from_sib):
            rs_mine[k], rs_sib[k] = a, b
        rs_handles[tag] = (keys, handles)
        return tok

    tok_a = reduce_start(("w_a", "w_b", "w_o", "w_up", "w_down"),
                         dict(w_a=dw_a, w_b=dw_b, w_o=dw_o, w_up=jnp.concatenate([dw_up_a, dw_up_g], 1), w_down=dw_down),
                         "a")
    dcomb = mm_nt(dy_b, full["w_b"], None, "d_comb", after=tok_a, out_dtype=ACT)
    dos, dms = combine_bwd(dcomb, outs, lses, "combine_bwd")
    dw_grp, cs_grp, dqkvs = [], [], []
    for g, d in enumerate(DILATIONS):
        dq, dk, dv = att_bwd(qkvs[g], dos[g], lses[g], dms[g], g, f"att_bwd_{g}")
        dqkv = [t.reshape(T, GROUP_W) for t in (dq, dk, dv)]
        dwg, csg = mm_tn(h0_res[g], dqkv, f"dw_in_qkv_{g}")
        dqkvs.append(dqkv)
        dw_grp.append(dwg)
        cs_grp.append(csg)
    dw_pa, cs_pa = mm_tn(h0b, dproj_a, "dw_in_conv")
    dw_pg, cs_pg = mm_tn(h0b, dproj_g, "dw_in_gates")

    def ungroup(parts):
        return jnp.concatenate([p[:, s * GROUP_W:(s + 1) * GROUP_W] for s in range(3) for p in parts], 1)

    db_in_parts = [cs_pa, ungroup(cs_grp), cs_pg]
    tok_b = reduce_start(("w_in",), dict(w_in=jnp.concatenate([dw_pa, ungroup(dw_grp), dw_pg], 1)), "b")
    dh0 = mm_nt(dproj_a, w_pa, None, "d_h0_conv", after=tok_b)
    dh0 = mm_nt(dproj_g, w_pg, dh0, "d_h0_gates")
    dh0_res = [(mm_nt(dqkvs[g], w_grp[g], None, f"d_h0_qkv_{g}").reshape(d, T // d, D), d)
               for g, d in enumerate(DILATIONS) if g > 0]
    dx, _, d_ln0_g, d_ln0_b, _, _ = ln_bwd(xs, None, ln0g, ln0b, dr1, ("nt", dqkvs[0], w_grp[0]), None, "d_h0_ln0_bwd",
                                           by_residue=[(dh0.reshape(1, T, D), 1)] + dh0_res)

    small = [d_ln0_g, d_ln0_b, jnp.concatenate(db_in_parts, 1), d_conv, d_b_o, d_ln1_g, d_ln1_b,
             jnp.concatenate([cs_a, cs_gate], 1), d_fcw, d_fcb, d_b_down, d_ln2_g, d_ln2_b, loss_part]
    packed, sizes = _pack(small)
    total = all_sum_small(packed, "sum_small")
    (g_ln0_g, g_ln0_b, g_b_in, g_conv_full, g_b_o, g_ln1_g, g_ln1_b, g_b_up, g_fcw_full, g_fcb, g_b_down, g_ln2_g,
     g_ln2_b, loss) = _unpack(total, sizes, [a.shape for a in small])
    cw = conv_w.shape[-1]
    fw = ffn_conv_w.shape[-1]
    g_conv = lax.dynamic_slice_in_dim(g_conv_full, dev * cw, cw, 1)
    g_fcw = lax.dynamic_slice_in_dim(g_fcw_full, dev * fw, fw, 1)

    from_chips = {}
    for tag, (keys, handles) in rs_handles.items():
        lands = copies_wait(handles, _to_chips_plan, total, f"grads_to_chips_{tag}_wait")
        from_chips.update(zip(keys, lands))

    moments = dict(w_in=(m_w_in, v_w_in), w_a=(m_w_a, v_w_a), w_b=(m_w_b, v_w_b), w_o=(m_w_o, v_w_o),
                   w_up=(m_w_up, v_w_up), w_down=(m_w_down, v_w_down))
    res_big = {}
    for k in names:
        res_big[k] = adamw_sharded(big[k][0], moments[k][0][0], moments[k][1][0], rs_mine[k], rs_sib[k], from_chips[k],
                                   place, f"adamw_{k}")

    small_names = ["ln0_g", "ln0_b", "b_in", "conv_w", "b_o", "ln1_g", "ln1_b", "b_up", "ffn_conv_w", "ffn_conv_b",
                   "b_down", "ln2_g", "ln2_b"]
    small_w = [ln0_g, ln0_b, b_in, conv_w, b_o, ln1_g, ln1_b, b_up, ffn_conv_w, ffn_conv_b, b_down, ln2_g, ln2_b]
    small_m = [m_ln0_g, m_ln0_b, m_b_in, m_conv_w, m_b_o, m_ln1_g, m_ln1_b, m_b_up, m_ffn_conv_w, m_ffn_conv_b,
               m_b_down, m_ln2_g, m_ln2_b]
    small_v = [v_ln0_g, v_ln0_b, v_b_in, v_conv_w, v_b_o, v_ln1_g, v_ln1_b, v_b_up, v_ffn_conv_w, v_ffn_conv_b,
               v_b_down, v_ln2_g, v_ln2_b]
    small_g = [g_ln0_g, g_ln0_b, g_b_in, g_conv, g_b_o, g_ln1_g, g_ln1_b, g_b_up, g_fcw, g_fcb, g_b_down, g_ln2_g,
               g_ln2_b]
    shapes = [w.shape for w in small_w]
    small_g = [g.reshape(s) for g, s in zip(small_g, shapes)]
    pw, psz = _pack(small_w)
    pg, _ = _pack(small_g)
    pm, _ = _pack(small_m)
    pv, _ = _pack(small_v)
    pd, pnm, pnv = adamw_packed(pw, pg, pm, pv, "adamw_small")
    res_small = {k: (g, d_, m_, v_) for k, g, d_, m_, v_ in zip(
        small_names, small_g, _unpack(pd, psz, shapes), _unpack(pnm, psz, shapes), _unpack(pnv, psz, shapes))}

    order = ["ln0_g", "ln0_b", "w_in", "b_in", "conv_w", "w_a", "w_b", "w_o", "b_o", "ln1_g", "ln1_b", "w_up", "b_up",
             "ffn_conv_w", "ffn_conv_b", "w_down", "b_down", "ln2_g", "ln2_b"]

    def result(k, j):
        if k in res_big:
            return res_big[k][j][None]
        return res_small[k][j]

    out = [loss.reshape(()), dx.reshape(x.shape)]
    for j in range(4):
        out += [result(k, j) for k in order]
    return tuple(out)
```

```python
import functools
import math

import numpy as np
import jax
import jax.numpy as jnp
from jax import lax
from jax.experimental import pallas as pl
from jax.experimental.pallas import tpu as pltpu

F32 = jnp.float32
BF16 = jnp.bfloat16
ACT = BF16

N_DEV = 8
LN_EPS = 1e-5
ALPHA = (2.0 * 1) ** 0.25
MASK_VALUE = -1e30
HEAD_DIM = 64
GROUP_W = 512
QKV_W = 3 * GROUP_W
DILATIONS = (1, 4, 16)
RADIUS = 64
LANES = 128
HALO = 8
HALO_BF16 = 16
ATT_TQ = 128

ADAM_LR = 0.001
ADAM_B1 = 0.9
ADAM_B2 = 0.999
ADAM_EPS = 1e-08
ADAM_WD = 0.01
ADAM_STEP = 10

VMEM_LIMIT = 52 * 1024 * 1024
OUT_TILE_BYTES = 8 * 1024 * 1024
MESH = pl.DeviceIdType.MESH
NT_DIMS = (((1,), (1,)), ((), ()))
TN_DIMS = (((0,), (0,)), ((), ()))


def _pick(n, target, align=LANES):
    if n <= target:
        return n
    best = None
    for t in range(align, target + 1, align):
        if n % t == 0:
            best = t
    assert best is not None, (n, target, align)
    return best


def _params(sems=None):
    return pltpu.CompilerParams(dimension_semantics=sems, vmem_limit_bytes=VMEM_LIMIT)


def _alibi_slopes():
    n = 3 * 8
    return np.exp2(-8.0 * np.arange(1, n + 1, dtype=np.float64) / n).astype(np.float32).reshape(3, 8)


def _ln_stats(r):
    mu = jnp.mean(r, -1, keepdims=True)
    xc = r - mu
    var = jnp.mean(xc * xc, -1, keepdims=True)
    rstd = lax.rsqrt(var + LN_EPS)
    return xc, rstd


def _load_natural(ref, d, scr):
    if d == 1:
        return ref[0]
    n, C = ref.shape[1], ref.shape[2]
    for c in range(C // LANES):
        for r in range(d):
            scr[c, pl.ds(r, n, stride=d), :] = ref[r, :, c * LANES:(c + 1) * LANES]
    return jnp.concatenate([scr[c] for c in range(C // LANES)], axis=1)


def _store_by_residue(val, ref, d, scr):
    if d == 1:
        ref[0] = val.astype(ref.dtype)
        return
    n, C = ref.shape[1], ref.shape[2]
    for c in range(C // LANES):
        scr[c] = val[:, c * LANES:(c + 1) * LANES]
    for c in range(C // LANES):
        for r in range(d):
            ref[r, :, c * LANES:(c + 1) * LANES] = scr[c, pl.ds(r, n, stride=d), :].astype(ref.dtype)


def _residue_spec(tm, d, C):
    return pl.BlockSpec((d, tm // d, C), lambda i: (0, i, 0))


def _residue_scratch(tm, C):
    return pltpu.VMEM((C // LANES, tm, LANES), F32)


def ln_fwd(a, res, g, b, name, dilations=(), gather=()):
    T, D = a.shape
    res_mm = isinstance(res, tuple)
    tm = _pick(T, 256 if res_mm else 512, 8)
    res_ins = list(res[1:]) if res_mm else ([] if res is None else [res])
    nd = len(dilations)
    ng = len(gather)
    n_in = 1 + len(res_ins) + 2
    last = T // tm - 1

    def body(*refs):
        a_ref = refs[0]
        r = a_ref[...]
        if res_mm:
            res_val = jnp.dot(refs[1][...], refs[2][...], preferred_element_type=F32) + refs[3][...]
            refs[-1 - n_scratch][...] = res_val
            r = ALPHA * r + res_val
        elif res_ins:
            r = ALPHA * r + refs[1][...]
        g_ref, b_ref = refs[n_in - 2], refs[n_in - 1]
        shard_refs = refs[n_in:n_in + ng]
        h_ref, hb_ref = refs[n_in + ng], refs[n_in + ng + 1]
        p_refs = refs[n_in + ng + 2:n_in + ng + 2 + nd]
        full_refs = refs[n_in + ng + 2 + nd:n_in + 2 * ng + 2 + nd]
        scratch = refs[len(refs) - n_scratch:]
        sems = scratch[len(scratch) - 3:] if ng else ()

        if ng:
            @pl.when(pl.program_id(0) == 0)
            def _():
                _gather_begin(shard_refs, full_refs, *sems)

        xc, rstd = _ln_stats(r)
        h = xc * rstd * g_ref[...] + b_ref[...]
        h_ref[...] = h
        hb_ref[...] = h.astype(BF16)
        for d, p_ref in zip(dilations, p_refs):
            _store_by_residue(h, p_ref, d, scratch[0])

        if ng:
            @pl.when(pl.program_id(0) == last)
            def _():
                _gather_finish(shard_refs, full_refs, *sems)

    row = pl.BlockSpec((tm, D), lambda i: (i, 0))
    vec = pl.BlockSpec((1, D), lambda i: (0, 0))
    hbm = pl.BlockSpec(memory_space=pl.ANY)
    if res_mm:
        res_specs = [pl.BlockSpec((tm, res[1].shape[1]), lambda i: (i, 0)), pl.BlockSpec(res[2].shape, lambda i: (0, 0)), vec]
    else:
        res_specs = [row] * len(res_ins)
    scratch_shapes = ([_residue_scratch(tm, D)] if nd else []) + (_gather_scratch(ng) if ng else [])
    n_scratch = len(scratch_shapes)
    ins = [a] + res_ins + [g, b] + list(gather)
    return pl.pallas_call(
        body, name=name, grid=(T // tm,),
        in_specs=[row] + res_specs + [vec, vec] + [hbm] * ng,
        out_specs=[row, row] + [_residue_spec(tm, d, D) for d in dilations] + [hbm] * ng + ([row] if res_mm else []),
        out_shape=[jax.ShapeDtypeStruct((T, D), F32), jax.ShapeDtypeStruct((T, D), BF16)]
        + [jax.ShapeDtypeStruct((d, T // d, D), BF16) for d in dilations]
        + [jax.ShapeDtypeStruct((N_DEV,) + s.shape, s.dtype) for s in gather]
        + ([jax.ShapeDtypeStruct((T, D), F32)] if res_mm else []),
        scratch_shapes=scratch_shapes,
        compiler_params=_params(("arbitrary",) if ng else ("parallel",)),
    )(*ins)


def ln_bwd(a, res, g, b, d1, d2, tgt, name, by_residue=()):
    T, D = a.shape
    tm = _pick(T, 256, 8)
    loss_mode = tgt is not None
    nres = len(by_residue)
    row = pl.BlockSpec((tm, D), lambda i: (i, 0))
    vec = pl.BlockSpec((1, D), lambda i: (0, 0))
    one = pl.BlockSpec((1, 1), lambda i: (0, 0))

    def rows_of(x):
        return pl.BlockSpec((tm, x.shape[1]), lambda i: (i, 0))

    def whole(x):
        return pl.BlockSpec(x.shape, lambda i: (0, 0))

    ins, in_specs, slots = [], [], {}

    def operand(key, arrays, specs):
        slots[key] = (len(ins), len(arrays))
        ins.extend(arrays)
        in_specs.extend(specs)

    operand("a", [a], [row])
    if isinstance(res, tuple):
        _, x, w, bias = res
        operand("res_mm", [x, w, bias], [rows_of(x), whole(w), vec])
    elif res is not None:
        operand("res", [res], [row])
    operand("gb", [g, b], [vec, vec])
    if loss_mode:
        operand("tgt", [tgt], [row])
    else:
        operand("d1", [d1], [row])
        if isinstance(d2, tuple):
            _, pieces, w = d2
            operand("d2_mm", list(pieces) + [w], [rows_of(p) for p in pieces] + [whole(w)])
        else:
            operand("d2", [d2], [row])
    operand("by_residue", [e for e, _ in by_residue], [_residue_spec(tm, d, D) for _, d in by_residue])
    n_in = len(ins)

    def body(*refs):
        def get(key):
            first, count = slots[key]
            return refs[first:first + count]

        dr_ref, drb_ref, dg_ref, db_ref, ds_ref, loss_ref = refs[n_in:n_in + 6]
        i = pl.program_id(0)

        @pl.when(i == 0)
        def _():
            dg_ref[...] = jnp.zeros_like(dg_ref)
            db_ref[...] = jnp.zeros_like(db_ref)
            ds_ref[...] = jnp.zeros_like(ds_ref)
            loss_ref[...] = jnp.zeros_like(loss_ref)

        r = get("a")[0][...]
        if "res_mm" in slots:
            x_ref, w_ref, bias_ref = get("res_mm")
            r = ALPHA * r + (jnp.dot(x_ref[...], w_ref[...], preferred_element_type=F32) + bias_ref[...])
        elif "res" in slots:
            r = ALPHA * r + get("res")[0][...]
        g_ref, b_ref = get("gb")
        xc, rstd = _ln_stats(r)
        xhat = xc * rstd
        gam = g_ref[...]
        if loss_mode:
            err = xhat * gam + b_ref[...] - get("tgt")[0][...]
            dy = err * (1.0 / D)
            row_loss = jnp.mean(err * err, -1, keepdims=True)
            loss_ref[...] += 0.5 * jnp.sum(row_loss, 0, keepdims=True)
        else:
            if "d2_mm" in slots:
                *p_refs, w_ref = get("d2_mm")
                av = p_refs[0][...] if len(p_refs) == 1 else jnp.concatenate([p[...] for p in p_refs], axis=1)
                d2v = lax.dot_general(av, w_ref[...], NT_DIMS, preferred_element_type=F32)
            else:
                d2v = get("d2")[0][...]
            dy = ALPHA * get("d1")[0][...] + d2v
        for (_, d), e_ref in zip(by_residue, get("by_residue")):
            dy = dy + _load_natural(e_ref, d, refs[-1])
        dyg = dy * gam
        c1 = jnp.mean(dyg, -1, keepdims=True)
        c2 = jnp.mean(dyg * xhat, -1, keepdims=True)
        dr = rstd * (dyg - c1 - xhat * c2)
        dr_ref[...] = dr
        drb_ref[...] = dr.astype(BF16)
        dg_ref[...] += jnp.sum(dy * xhat, 0, keepdims=True)
        db_ref[...] += jnp.sum(dy, 0, keepdims=True)
        ds_ref[...] += jnp.sum(dr, 0, keepdims=True)

    return pl.pallas_call(
        body, name=name, grid=(T // tm,),
        in_specs=in_specs,
        out_specs=[row, row, vec, vec, vec, one],
        out_shape=[jax.ShapeDtypeStruct((T, D), F32), jax.ShapeDtypeStruct((T, D), BF16),
                   jax.ShapeDtypeStruct((1, D), F32), jax.ShapeDtypeStruct((1, D), F32),
                   jax.ShapeDtypeStruct((1, D), F32), jax.ShapeDtypeStruct((1, 1), F32)],
        scratch_shapes=[_residue_scratch(tm, D)] if nres else [],
        compiler_params=_params(("arbitrary",)),
    )(*ins)


_TOKEN_SPEC = pl.BlockSpec((8, LANES), lambda i: (0, 0))


def mm_nn(a, w, bias, out_dtype, name, after=None):
    M, K = a.shape
    N = w.shape[1]
    tm = _pick(M, max(256, min(1024, OUT_TILE_BYTES // (N * jnp.dtype(out_dtype).itemsize))), 8)
    tc = _pick(N, 512)

    def body(a_ref, w_ref, b_ref, *rest):
        o_ref = rest[-1]
        av = a_ref[...]
        for j in range(N // tc):
            cols = slice(j * tc, (j + 1) * tc)
            acc = jnp.dot(av, w_ref[:, cols], preferred_element_type=F32)
            o_ref[:, cols] = (acc + b_ref[:, cols]).astype(out_dtype)

    return pl.pallas_call(
        body, name=name, grid=(M // tm,),
        in_specs=[pl.BlockSpec((tm, K), lambda i: (i, 0)),
                  pl.BlockSpec((K, N), lambda i: (0, 0)),
                  pl.BlockSpec((1, N), lambda i: (0, 0))] + ([] if after is None else [_TOKEN_SPEC]),
        out_specs=pl.BlockSpec((tm, N), lambda i: (i, 0)),
        out_shape=jax.ShapeDtypeStruct((M, N), out_dtype),
        compiler_params=_params(("parallel",)),
    )(a, w, bias, *([] if after is None else [after]))


def mm_nt(a, w, acc_in, name, after=None, w_block=0, out_dtype=F32):
    pieces = list(a) if isinstance(a, (list, tuple)) else [a]
    M = pieces[0].shape[0]
    widths = [p.shape[1] for p in pieces]
    K = sum(widths)
    N = w.shape[0]
    tm = _pick(M, 512, 8)
    tc = _pick(N, 512)
    has_acc = acc_in is not None
    n_a = len(pieces)

    def body(*refs):
        a_refs, w_ref = refs[:n_a], refs[n_a]
        c_ref = refs[n_a + 1] if has_acc else None
        o_ref = refs[-1]
        av = a_refs[0][...] if n_a == 1 else jnp.concatenate([r[...] for r in a_refs], axis=1)
        for j in range(N // tc):
            cols = slice(j * tc, (j + 1) * tc)
            acc = lax.dot_general(av, w_ref[cols, :], NT_DIMS, preferred_element_type=F32)
            if has_acc:
                acc = acc + c_ref[:, cols]
            o_ref[:, cols] = acc.astype(out_dtype)

    out_spec = pl.BlockSpec((tm, N), lambda i: (i, 0))
    in_specs = [pl.BlockSpec((tm, kw), lambda i: (i, 0)) for kw in widths]
    in_specs.append(pl.BlockSpec((N, K), lambda i: (0, w_block)))
    ins = pieces + [w]
    if has_acc:
        in_specs.append(out_spec)
        ins.append(acc_in)
    if after is not None:
        in_specs.append(_TOKEN_SPEC)
        ins.append(after)
    return pl.pallas_call(
        body, name=name, grid=(M // tm,),
        in_specs=in_specs, out_specs=out_spec,
        out_shape=jax.ShapeDtypeStruct((M, N), out_dtype),
        compiler_params=_params(("parallel",)),
    )(*ins)


def mm_tn(a, b, name, out_dtype=BF16):
    pieces = list(b) if isinstance(b, (list, tuple)) else [b]
    T, M = a.shape
    widths = [p.shape[1] for p in pieces]
    N = sum(widths)
    tk = _pick(T, 512, 8)
    nk = T // tk
    tc = _pick(M, 256)
    n_b = len(pieces)

    def body(*refs):
        a_ref, b_refs = refs[0], refs[1:1 + n_b]
        o_ref, cs_ref, acc_ref = refs[1 + n_b:]
        k = pl.program_id(0)

        @pl.when(k == 0)
        def _():
            acc_ref[...] = jnp.zeros_like(acc_ref)
            cs_ref[...] = jnp.zeros_like(cs_ref)

        bv = b_refs[0][...] if n_b == 1 else jnp.concatenate([r[...] for r in b_refs], axis=1)
        cs_ref[...] += jnp.sum(bv.astype(F32), 0, keepdims=True)
        for mi in range(M // tc):
            rows = slice(mi * tc, (mi + 1) * tc)
            acc_ref[rows, :] += lax.dot_general(a_ref[:, rows], bv, TN_DIMS, preferred_element_type=F32)

        @pl.when(k == nk - 1)
        def _():
            o_ref[...] = acc_ref[...].astype(out_dtype)

    return pl.pallas_call(
        body, name=name, grid=(nk,),
        in_specs=[pl.BlockSpec((tk, M), lambda k: (k, 0))] + [pl.BlockSpec((tk, wd), lambda k: (k, 0)) for wd in widths],
        out_specs=[pl.BlockSpec((M, N), lambda k: (0, 0)), pl.BlockSpec((1, N), lambda k: (0, 0))],
        out_shape=[jax.ShapeDtypeStruct((M, N), out_dtype), jax.ShapeDtypeStruct((1, N), F32)],
        scratch_shapes=[pltpu.VMEM((M, N), F32)],
        compiler_params=_params(("arbitrary",)),
    )(a, *pieces)


def _ext_rows(prev_ref, main_ref, next_ref, i, tm, T, dtype=F32):
    before = jnp.where(i == 0, 0.0, prev_ref[...])
    after = jnp.where(i == T // tm - 1, 0.0, next_ref[...])
    return jnp.concatenate([before, main_ref[...], after], axis=0).astype(dtype)


def _prev_row(x):
    return pltpu.roll(x, 1, 0)


def _next_row(x):
    return pltpu.roll(x, x.shape[0] - 1, 0)


def _conv3(u, w_ref):
    return _prev_row(u) * w_ref[0:1, :] + u * w_ref[1:2, :] + _next_row(u) * w_ref[2:3, :]


def _main(x, tm, halo=HALO):
    return x[halo:halo + tm]


def _halo_specs(tm, tc, T, col, order, halo=HALO):
    r = tm // halo
    last = T // halo - 1
    if order == "ij":
        return (pl.BlockSpec((halo, tc), lambda i, j: (jnp.maximum(i * r - 1, 0), col(j))),
                pl.BlockSpec((tm, tc), lambda i, j: (i, col(j))),
                pl.BlockSpec((halo, tc), lambda i, j: (jnp.minimum((i + 1) * r, last), col(j))))
    return (pl.BlockSpec((halo, tc), lambda j, i: (jnp.maximum(i * r - 1, 0), col(j))),
            pl.BlockSpec((tm, tc), lambda j, i: (i, col(j))),
            pl.BlockSpec((halo, tc), lambda j, i: (jnp.minimum((i + 1) * r, last), col(j))))


def conv_a_fwd(proj_a, conv_w, name):
    T, D3 = proj_a.shape
    D = D3 // 3
    tm = _pick(T, 256, 8)

    def body(p_ref, m_ref, n_ref, w_ref, o_ref):
        i = pl.program_id(0)
        ext = _ext_rows(p_ref, m_ref, n_ref, i, tm, T)
        u = ext[:, D:2 * D] * ext[:, 2 * D:]
        cu = _conv3(u, w_ref)
        o_ref[...] = (m_ref[:, :D].astype(F32) * _main(cu, tm, HALO_BF16)).astype(BF16)

    prev, main, nxt = _halo_specs(tm, D3, T, lambda j: 0, "ij", HALO_BF16)
    return pl.pallas_call(
        body, name=name, grid=(T // tm, 1),
        in_specs=[prev, main, nxt, pl.BlockSpec((3, D), lambda i, j: (0, 0))],
        out_specs=pl.BlockSpec((tm, D), lambda i, j: (i, 0)),
        out_shape=jax.ShapeDtypeStruct((T, D), BF16),
        compiler_params=_params(("parallel", "arbitrary")),
    )(proj_a, proj_a, proj_a, conv_w)


def conv_a_bwd(ds_a, proj_a, conv_w, name):
    T, D3 = proj_a.shape
    D = D3 // 3
    tm = _pick(T, 256, 8)

    def body(dp_ref, dm_ref, dn_ref, p_ref, m_ref, n_ref, w_ref, o_ref, dw_ref):
        i = pl.program_id(0)

        @pl.when(i == 0)
        def _():
            dw_ref[...] = jnp.zeros_like(dw_ref)

        ext = _ext_rows(p_ref, m_ref, n_ref, i, tm, T)
        dsa = _ext_rows(dp_ref, dm_ref, dn_ref, i, tm, T)
        gb, gc, hin = ext[:, :D], ext[:, D:2 * D], ext[:, 2 * D:]
        u = gc * hin
        u_prev, u_next = _prev_row(u), _next_row(u)
        cu = u_prev * w_ref[0:1, :] + u * w_ref[1:2, :] + u_next * w_ref[2:3, :]
        dcu = dsa * gb
        du = _next_row(dcu) * w_ref[0:1, :] + dcu * w_ref[1:2, :] + _prev_row(dcu) * w_ref[2:3, :]
        h = HALO_BF16
        o_ref[:, :D] = _main(dsa * cu, tm, h).astype(BF16)
        o_ref[:, D:2 * D] = _main(du * hin, tm, h).astype(BF16)
        o_ref[:, 2 * D:] = _main(du * gc, tm, h).astype(BF16)
        dcu_m = _main(dcu, tm, h)
        dw_ref[0:1, :] += jnp.sum(dcu_m * _main(u_prev, tm, h), 0, keepdims=True)
        dw_ref[1:2, :] += jnp.sum(dcu_m * _main(u, tm, h), 0, keepdims=True)
        dw_ref[2:3, :] += jnp.sum(dcu_m * _main(u_next, tm, h), 0, keepdims=True)

    dprev, dmain, dnxt = _halo_specs(tm, D, T, lambda j: 0, "ij", HALO_BF16)
    prev, main, nxt = _halo_specs(tm, D3, T, lambda j: 0, "ij", HALO_BF16)
    return pl.pallas_call(
        body, name=name, grid=(T // tm, 1),
        in_specs=[dprev, dmain, dnxt, prev, main, nxt, pl.BlockSpec((3, D), lambda i, j: (0, 0))],
        out_specs=[pl.BlockSpec((tm, D3), lambda i, j: (i, 0)), pl.BlockSpec((3, D), lambda i, j: (0, 0))],
        out_shape=[jax.ShapeDtypeStruct((T, D3), BF16), jax.ShapeDtypeStruct((3, D), F32)],
        compiler_params=_params(("arbitrary", "arbitrary")),
    )(ds_a, ds_a, ds_a, proj_a, proj_a, proj_a, conv_w)


_INV_SQRT2 = 1.0 / math.sqrt(2.0)
_INV_SQRT_2PI = 1.0 / math.sqrt(2.0 * math.pi)


def conv_f_fwd(up, fcw, fcb, name):
    T, F2 = up.shape
    F = F2 // 2
    tm = _pick(T, 256, 8)
    tc = _pick(F, 1408)
    nc = F // tc

    def body(p_ref, m_ref, n_ref, g_ref, w_ref, b_ref, o_ref):
        i = pl.program_id(0)
        a = _ext_rows(p_ref, m_ref, n_ref, i, tm, T)
        ca = _main(_conv3(a, w_ref), tm) + b_ref[...]
        gl = 0.5 * ca * (1.0 + lax.erf(ca * _INV_SQRT2))
        o_ref[...] = (gl * g_ref[...]).astype(BF16)

    prev, main, nxt = _halo_specs(tm, tc, T, lambda j: j, "ij")
    return pl.pallas_call(
        body, name=name, grid=(T // tm, nc),
        in_specs=[prev, main, nxt,
                  pl.BlockSpec((tm, tc), lambda i, j: (i, nc + j)),
                  pl.BlockSpec((3, tc), lambda i, j: (0, j)),
                  pl.BlockSpec((1, tc), lambda i, j: (0, j))],
        out_specs=pl.BlockSpec((tm, tc), lambda i, j: (i, j)),
        out_shape=jax.ShapeDtypeStruct((T, F), BF16),
        compiler_params=_params(("parallel", "parallel")),
    )(up, up, up, up, fcw, fcb)


def ffn_up_conv_f(h, w_up, b_up, fcw, fcb, name):
    T, D = h.shape
    F = fcb.shape[1]
    tm = _pick(T, 256, 8)
    tc = _pick(F, 256)
    halo = HALO_BF16

    def body(hp_ref, hm_ref, hn_ref, w_ref, b_ref, cw_ref, cb_ref, up_ref, f_ref):
        i = pl.program_id(0)
        h_ext = _ext_rows(hp_ref, hm_ref, hn_ref, i, tm, T, dtype=BF16)
        h_main = hm_ref[...]
        rows = i * tm - halo + lax.broadcasted_iota(jnp.int32, (tm + 2 * halo, 1), 0)
        inside = (rows >= 0) & (rows < T)
        for c in range(F // tc):
            cols = slice(c * tc, (c + 1) * tc)
            gcols = slice(F + c * tc, F + (c + 1) * tc)
            a_ext = jnp.dot(h_ext, w_ref[:, cols], preferred_element_type=F32) + b_ref[:, cols]
            a_ext = jnp.where(inside, a_ext, 0.0)
            gate = jnp.dot(h_main, w_ref[:, gcols], preferred_element_type=F32) + b_ref[:, gcols]
            up_ref[:, cols] = _main(a_ext, tm, halo)
            up_ref[:, gcols] = gate
            ca = _main(_prev_row(a_ext) * cw_ref[0:1, cols] + a_ext * cw_ref[1:2, cols]
                       + _next_row(a_ext) * cw_ref[2:3, cols], tm, halo) + cb_ref[:, cols]
            gl = 0.5 * ca * (1.0 + lax.erf(ca * _INV_SQRT2))
            f_ref[:, cols] = (gl * gate).astype(BF16)

    prev, main, nxt = _halo_specs(tm, D, T, lambda j: 0, "ij", halo)
    whole = lambda x: pl.BlockSpec(x.shape, lambda i, j: (0, 0))
    return pl.pallas_call(
        body, name=name, grid=(T // tm, 1),
        in_specs=[prev, main, nxt, whole(w_up), whole(b_up), whole(fcw), whole(fcb)],
        out_specs=[pl.BlockSpec((tm, 2 * F), lambda i, j: (i, 0)), pl.BlockSpec((tm, F), lambda i, j: (i, 0))],
        out_shape=[jax.ShapeDtypeStruct((T, 2 * F), F32), jax.ShapeDtypeStruct((T, F), BF16)],
        compiler_params=_params(("parallel", "arbitrary")),
    )(h, h, h, w_up, b_up, fcw, fcb)


def conv_f_bwd(dy, w_down, up, fcw, fcb, name):
    T, F2 = up.shape
    F = F2 // 2
    tm = _pick(T, 256, 8)
    tc = _pick(F, 1408)
    nc = F // tc

    def body(yp_ref, ym_ref, yn_ref, wd_ref, ap_ref, am_ref, an_ref, gp_ref, gm_ref, gn_ref, w_ref, b_ref,
             da_ref, dg_ref, csa_ref, csg_ref, dfb_ref, dfw_ref):
        i = pl.program_id(1)

        @pl.when(i == 0)
        def _():
            csa_ref[...] = jnp.zeros_like(csa_ref)
            csg_ref[...] = jnp.zeros_like(csg_ref)
            dfb_ref[...] = jnp.zeros_like(dfb_ref)
            dfw_ref[...] = jnp.zeros_like(dfw_ref)

        dy_ext = _ext_rows(yp_ref, ym_ref, yn_ref, i, tm, T, dtype=BF16)
        dfe = lax.dot_general(dy_ext, wd_ref[...], NT_DIMS, preferred_element_type=F32)
        dfe = dfe[HALO_BF16 - HALO:HALO_BF16 + tm + HALO]
        a = _ext_rows(ap_ref, am_ref, an_ref, i, tm, T)
        gate = _ext_rows(gp_ref, gm_ref, gn_ref, i, tm, T)
        a_prev, a_next = _prev_row(a), _next_row(a)
        ca = a_prev * w_ref[0:1, :] + a * w_ref[1:2, :] + a_next * w_ref[2:3, :] + b_ref[...]
        cdf = 0.5 * (1.0 + lax.erf(ca * _INV_SQRT2))
        gl = ca * cdf
        gp = cdf + ca * (jnp.exp(-0.5 * ca * ca) * _INV_SQRT_2PI)
        dgate = _main(dfe * gl, tm)
        dca = dfe * gate * gp
        da = _main(_next_row(dca) * w_ref[0:1, :] + dca * w_ref[1:2, :] + _prev_row(dca) * w_ref[2:3, :], tm)
        da_ref[...] = da.astype(BF16)
        dg_ref[...] = dgate.astype(BF16)
        csa_ref[...] += jnp.sum(da, 0, keepdims=True)
        csg_ref[...] += jnp.sum(dgate, 0, keepdims=True)
        dca_m = _main(dca, tm)
        dfb_ref[...] += jnp.sum(dca_m, 0, keepdims=True)
        dfw_ref[0:1, :] += jnp.sum(dca_m * _main(a_prev, tm), 0, keepdims=True)
        dfw_ref[1:2, :] += jnp.sum(dca_m * _main(a, tm), 0, keepdims=True)
        dfw_ref[2:3, :] += jnp.sum(dca_m * _main(a_next, tm), 0, keepdims=True)

    fprev, fmain, fnxt = _halo_specs(tm, tc, T, lambda j: j, "ji")
    gprev, gmain, gnxt = _halo_specs(tm, tc, T, lambda j: nc + j, "ji")
    yprev, ymain, ynxt = _halo_specs(tm, dy.shape[1], T, lambda j: 0, "ji", HALO_BF16)
    tile = pl.BlockSpec((tm, tc), lambda j, i: (i, j))
    vec = pl.BlockSpec((1, tc), lambda j, i: (0, j))
    vec3 = pl.BlockSpec((3, tc), lambda j, i: (0, j))
    return pl.pallas_call(
        body, name=name, grid=(nc, T // tm),
        in_specs=[yprev, ymain, ynxt, pl.BlockSpec((tc, dy.shape[1]), lambda j, i: (j, 0)),
                  fprev, fmain, fnxt, gprev, gmain, gnxt, vec3, vec],
        out_specs=[tile, tile, vec, vec, vec, vec3],
        out_shape=[jax.ShapeDtypeStruct((T, F), BF16), jax.ShapeDtypeStruct((T, F), BF16),
                   jax.ShapeDtypeStruct((1, F), F32), jax.ShapeDtypeStruct((1, F), F32),
                   jax.ShapeDtypeStruct((1, F), F32), jax.ShapeDtypeStruct((3, F), F32)],
        compiler_params=_params(("arbitrary", "arbitrary")),
    )(dy, dy, dy, w_down, up, up, up, up, up, up, fcw, fcb)


def gate_fwd(proj_g, y_a, y_b, name):
    T, D = y_a.shape
    tm = _pick(T, 512, 8)

    def body(g_ref, a_ref, b_ref, o_ref):
        sa = jax.nn.sigmoid(g_ref[:, :D].astype(F32))
        sb = jax.nn.sigmoid(g_ref[:, D:].astype(F32))
        o_ref[...] = (sa * a_ref[...].astype(F32) + sb * b_ref[...].astype(F32)).astype(BF16)

    row = pl.BlockSpec((tm, D), lambda i: (i, 0))
    return pl.pallas_call(
        body, name=name, grid=(T // tm,),
        in_specs=[pl.BlockSpec((tm, 2 * D), lambda i: (i, 0)), row, row],
        out_specs=row,
        out_shape=jax.ShapeDtypeStruct((T, D), BF16),
        compiler_params=_params(("parallel",)),
    )(proj_g, y_a, y_b)


def gate_bwd(dz, proj_g, y_a, y_b, name):
    T, D = y_a.shape
    tm = _pick(T, 512, 8)

    def body(dz_ref, g_ref, a_ref, b_ref, da_ref, db_ref, dg_ref):
        dzv = dz_ref[...].astype(F32)
        sa = jax.nn.sigmoid(g_ref[:, :D].astype(F32))
        sb = jax.nn.sigmoid(g_ref[:, D:].astype(F32))
        da_ref[...] = (dzv * sa).astype(BF16)
        db_ref[...] = (dzv * sb).astype(BF16)
        dg_ref[:, :D] = (dzv * a_ref[...].astype(F32) * (sa * (1.0 - sa))).astype(BF16)
        dg_ref[:, D:] = (dzv * b_ref[...].astype(F32) * (sb * (1.0 - sb))).astype(BF16)

    row = pl.BlockSpec((tm, D), lambda i: (i, 0))
    wide = pl.BlockSpec((tm, 2 * D), lambda i: (i, 0))
    return pl.pallas_call(
        body, name=name, grid=(T // tm,),
        in_specs=[row, wide, row, row],
        out_specs=[row, row, wide],
        out_shape=[jax.ShapeDtypeStruct((T, D), BF16), jax.ShapeDtypeStruct((T, D), BF16),
                   jax.ShapeDtypeStruct((T, 2 * D), BF16)],
        compiler_params=_params(("parallel",)),
    )(dz, proj_g, y_a, y_b)


ATT_WIN = ATT_TQ + 2 * RADIUS
ATT_STEP = 1024
FAR = 1e32


def _att_window(qs, L):
    ks = pl.multiple_of(jnp.clip(qs - RADIUS, 0, L - ATT_WIN), RADIUS)
    return ks, jnp.where(qs == 0, 0, jnp.where(qs == L - ATT_TQ, 2, 1))


def _fill_bias_tables(bias_ref, sl_ref, hp, d):
    col_row = (lax.broadcasted_iota(jnp.int32, (ATT_TQ, ATT_WIN), 1)
               - lax.broadcasted_iota(jnp.int32, (ATT_TQ, ATT_WIN), 0))
    for v in range(3):
        ad = jnp.abs(col_row - v * RADIUS)
        dist = jnp.where(ad <= RADIUS, (ad * d).astype(F32), FAR)
        bias_ref[v, 0:ATT_TQ, :] = sl_ref[hp * 2] * dist
        bias_ref[v, ATT_TQ:2 * ATT_TQ, :] = sl_ref[hp * 2 + 1] * dist


def _head_masks():
    lane = lax.broadcasted_iota(jnp.int32, (1, LANES), 1)
    return [lane < HEAD_DIM, lane >= HEAD_DIM]


def _stack_heads(x, masks):
    zero = jnp.zeros_like(x)
    return jnp.concatenate([jnp.where(masks[0], x, zero), jnp.where(masks[1], x, zero)], axis=0)


def _unstack_heads(x2, masks):
    n = x2.shape[0] // 2
    return jnp.where(masks[0], x2[:n], x2[n:])


def _att_step(L):
    step = min(ATT_STEP, L)
    assert L % step == 0 and step % ATT_TQ == 0 and L >= ATT_WIN
    return step


def _residues_per_step(d, L):
    rps = max(1, min(d, ATT_STEP // L))
    assert d % rps == 0
    return rps


def att_fwd(qkv, group, name):
    d, L, _ = qkv.shape
    step = _att_step(L)
    rps = _residues_per_step(d, L)
    cg = GROUP_W // LANES
    slopes = jnp.asarray(_alibi_slopes()[group])
    scale = HEAD_DIM ** -0.5

    def body(sl_ref, q_ref, k_ref, v_ref, o_ref, l_ref, bias_ref, s_ref, p_ref):
        hp = pl.program_id(1)
        i = pl.program_id(2)

        @pl.when(i == 0)
        def _():
            _fill_bias_tables(bias_ref, sl_ref, hp, d)

        masks = _head_masks()
        per = step // ATT_TQ
        tiles = [(rr, t) for rr in range(rps) for t in range(per)]
        windows = [_att_window(i * step + t * ATT_TQ, L) for t in range(per)]
        for n, (rr, t) in enumerate(tiles):
            rows = slice(t * ATT_TQ, (t + 1) * ATT_TQ)
            ks, table = windows[t]
            q2 = _stack_heads(q_ref[rr, rows, :] * scale, masks)
            kw = k_ref[rr, pl.ds(ks, ATT_WIN), :]
            s_ref[n] = lax.dot_general(q2, kw, NT_DIMS, preferred_element_type=F32) - bias_ref[table]
        for n, (rr, t) in enumerate(tiles):
            rows = slice(t * ATT_TQ, (t + 1) * ATT_TQ)
            s = s_ref[n]
            m = jnp.max(s, -1, keepdims=True)
            p = jnp.exp(s - m)
            den = jnp.sum(p, -1, keepdims=True)
            p_ref[n] = (p / den).astype(BF16)
            l_ref[rr, rows, :] = _unstack_heads(m + jnp.log(den), masks)
        for n, (rr, t) in enumerate(tiles):
            rows = slice(t * ATT_TQ, (t + 1) * ATT_TQ)
            vw = v_ref[rr, pl.ds(windows[t][0], ATT_WIN), :]
            o2 = jnp.dot(p_ref[n], vw, preferred_element_type=F32)
            o_ref[rr, rows, :] = _unstack_heads(o2, masks)

    n_tiles = rps * step // ATT_TQ
    out_spec = pl.BlockSpec((rps, step, LANES), lambda r, hp, i: (r, i, hp))
    return pl.pallas_call(
        body, name=name, grid=(d // rps, cg, L // step),
        in_specs=[pl.BlockSpec(memory_space=pltpu.SMEM),
                  pl.BlockSpec((rps, step, LANES), lambda r, hp, i: (r, i, hp)),
                  pl.BlockSpec((rps, L, LANES), lambda r, hp, i: (r, 0, cg + hp)),
                  pl.BlockSpec((rps, L, LANES), lambda r, hp, i: (r, 0, 2 * cg + hp))],
        out_specs=[out_spec, out_spec],
        out_shape=[jax.ShapeDtypeStruct((d, L, GROUP_W), F32)] * 2,
        scratch_shapes=[pltpu.VMEM((3, 2 * ATT_TQ, ATT_WIN), F32),
                        pltpu.VMEM((n_tiles, 2 * ATT_TQ, ATT_WIN), F32),
                        pltpu.VMEM((n_tiles, 2 * ATT_TQ, ATT_WIN), BF16)],
        compiler_params=_params(("arbitrary", "arbitrary", "arbitrary")),
    )(slopes, qkv, qkv, qkv)


def att_bwd(qkv, do, lse, dmat, group, name):
    d, L, _ = qkv.shape
    step = _att_step(L)
    rps = _residues_per_step(d, L)
    nq = L // step
    cg = GROUP_W // LANES
    slopes = jnp.asarray(_alibi_slopes()[group])
    scale = HEAD_DIM ** -0.5

    def body(sl_ref, q_ref, k_ref, v_ref, do_ref, l_ref, dm_ref, dq_ref, dk_ref, dv_ref, dk_acc, dv_acc, bias_ref,
             s_ref, dp_ref, p_ref, ds_ref):
        hp = pl.program_id(1)
        i = pl.program_id(2)

        @pl.when(i == 0)
        def _():
            dk_acc[...] = jnp.zeros_like(dk_acc)
            dv_acc[...] = jnp.zeros_like(dv_acc)
            _fill_bias_tables(bias_ref, sl_ref, hp, d)

        masks = _head_masks()

        def head_cols(x):
            return jnp.concatenate([jnp.max(jnp.where(hm, x, -jnp.inf), -1, keepdims=True) for hm in masks], axis=0)

        per = step // ATT_TQ
        tiles = [(rr, t) for rr in range(rps) for t in range(per)]
        windows = [_att_window(i * step + t * ATT_TQ, L) for t in range(per)]

        def stacked(ref, rr, t, factor=None):
            x = ref[rr, t * ATT_TQ:(t + 1) * ATT_TQ, :]
            return _stack_heads(x if factor is None else x * factor, masks)

        for n, (rr, t) in enumerate(tiles):
            ks, table = windows[t]
            q2 = stacked(q_ref, rr, t, scale)
            s_ref[n] = lax.dot_general(q2, k_ref[rr, pl.ds(ks, ATT_WIN), :], NT_DIMS,
                                       preferred_element_type=F32) - bias_ref[table]
            dp_ref[n] = lax.dot_general(stacked(do_ref, rr, t), v_ref[rr, pl.ds(ks, ATT_WIN), :], NT_DIMS,
                                        preferred_element_type=F32)
        for n, (rr, t) in enumerate(tiles):
            rows = slice(t * ATT_TQ, (t + 1) * ATT_TQ)
            p = jnp.exp(s_ref[n] - head_cols(l_ref[rr, rows, :]))
            p_ref[n] = p.astype(BF16)
            ds_ref[n] = (p * (dp_ref[n] - head_cols(dm_ref[rr, rows, :]))).astype(BF16)
        for n, (rr, t) in enumerate(tiles):
            rows = slice(t * ATT_TQ, (t + 1) * ATT_TQ)
            ks = windows[t][0]
            ds = ds_ref[n]
            dq2 = jnp.dot(ds, k_ref[rr, pl.ds(ks, ATT_WIN), :], preferred_element_type=F32)
            dq_ref[rr, rows, :] = (_unstack_heads(dq2, masks) * scale).astype(BF16)
            dk_acc[rr, pl.ds(ks, ATT_WIN), :] += lax.dot_general(ds, stacked(q_ref, rr, t, scale), TN_DIMS,
                                                                 preferred_element_type=F32)
            dv_acc[rr, pl.ds(ks, ATT_WIN), :] += lax.dot_general(p_ref[n], stacked(do_ref, rr, t), TN_DIMS,
                                                                 preferred_element_type=F32)

        @pl.when(i == nq - 1)
        def _():
            dk_ref[...] = dk_acc[...].astype(BF16)
            dv_ref[...] = dv_acc[...].astype(BF16)

    tile = pl.BlockSpec((rps, step, LANES), lambda r, hp, i: (r, i, hp))
    whole = pl.BlockSpec((rps, L, LANES), lambda r, hp, i: (r, 0, hp))
    return pl.pallas_call(
        body, name=name, grid=(d // rps, cg, nq),
        in_specs=[pl.BlockSpec(memory_space=pltpu.SMEM), tile,
                  pl.BlockSpec((rps, L, LANES), lambda r, hp, i: (r, 0, cg + hp)),
                  pl.BlockSpec((rps, L, LANES), lambda r, hp, i: (r, 0, 2 * cg + hp)),
                  tile, tile, tile],
        out_specs=[tile, whole, whole],
        out_shape=[jax.ShapeDtypeStruct((d, L, GROUP_W), BF16)] * 3,
        scratch_shapes=[pltpu.VMEM((rps, L, LANES), F32), pltpu.VMEM((rps, L, LANES), F32),
                        pltpu.VMEM((3, 2 * ATT_TQ, ATT_WIN), F32)]
        + [pltpu.VMEM((rps * step // ATT_TQ, 2 * ATT_TQ, ATT_WIN), dt) for dt in (F32, F32, BF16, BF16)],
        compiler_params=_params(("arbitrary", "arbitrary", "arbitrary")),
    )(slopes, qkv, qkv, qkv, do, lse, dmat)


def _group_weights(ls):
    m = jnp.maximum(jnp.maximum(ls[0], ls[1]), ls[2])
    es = [jnp.exp(l - m) for l in ls]
    tot = es[0] + es[1] + es[2]
    return [e / tot for e in es]


def combine_fwd(outs, lses, name):
    T = outs[0].shape[0] * outs[0].shape[1]
    tm = _pick(T, 512, 8)
    n_scr = 2 * (len(DILATIONS) - 1)

    def body(*refs):
        o_refs, l_refs, c_ref, scr = refs[:3], refs[3:6], refs[6], refs[7:]
        o = [_load_natural(o_refs[g], d, scr[g - 1] if g else None) for g, d in enumerate(DILATIONS)]
        l = [_load_natural(l_refs[g], d, scr[g + 1] if g else None) for g, d in enumerate(DILATIONS)]
        w = _group_weights(l)
        c_ref[...] = (w[0] * o[0] + w[1] * o[1] + w[2] * o[2]).astype(BF16)

    specs = [_residue_spec(tm, d, GROUP_W) for d in DILATIONS]
    return pl.pallas_call(
        body, name=name, grid=(T // tm,),
        in_specs=specs + specs, out_specs=pl.BlockSpec((tm, GROUP_W), lambda i: (i, 0)),
        out_shape=jax.ShapeDtypeStruct((T, GROUP_W), BF16),
        scratch_shapes=[_residue_scratch(tm, GROUP_W)] * n_scr,
        compiler_params=_params(("parallel",)),
    )(*outs, *lses)


def combine_bwd(dcomb, outs, lses, name):
    T = dcomb.shape[0]
    tm = _pick(T, 256, 8)
    head = np.arange(GROUP_W) // HEAD_DIM
    seg = jnp.asarray((head[:, None] == head[None, :]).astype(np.float32)).astype(BF16)
    ng = len(DILATIONS)
    n_scr = 4 * (ng - 1)

    def body(*refs):
        dc_ref, o_refs, l_refs, e_ref = refs[0], refs[1:1 + ng], refs[1 + ng:1 + 2 * ng], refs[1 + 2 * ng]
        do_refs, dm_refs = refs[2 + 2 * ng:2 + 3 * ng], refs[2 + 3 * ng:2 + 4 * ng]
        scr = refs[2 + 4 * ng:]
        o = [_load_natural(o_refs[g], d, scr[4 * (g - 1)] if g else None) for g, d in enumerate(DILATIONS)]
        l = [_load_natural(l_refs[g], d, scr[4 * (g - 1) + 1] if g else None) for g, d in enumerate(DILATIONS)]
        w = _group_weights(l)
        dc = dc_ref[...].astype(F32)
        e = e_ref[...]
        prod = dc * (w[0] * o[0] + w[1] * o[1] + w[2] * o[2])
        tot = jnp.zeros_like(dc)
        for _ in range(3):
            part = prod.astype(BF16)
            tot = tot + jnp.dot(part, e, preferred_element_type=F32)
            prod = prod - part.astype(F32)
        for g, d in enumerate(DILATIONS):
            _store_by_residue(w[g] * dc, do_refs[g], d, scr[4 * (g - 1) + 2] if g else None)
            _store_by_residue(w[g] * tot, dm_refs[g], d, scr[4 * (g - 1) + 3] if g else None)

    specs = [_residue_spec(tm, d, GROUP_W) for d in DILATIONS]
    res = pl.pallas_call(
        body, name=name, grid=(T // tm,),
        in_specs=[pl.BlockSpec((tm, GROUP_W), lambda i: (i, 0))] + specs + specs
        + [pl.BlockSpec((GROUP_W, GROUP_W), lambda i: (0, 0))],
        out_specs=specs + specs,
        out_shape=[jax.ShapeDtypeStruct(o.shape, BF16) for o in outs] + [jax.ShapeDtypeStruct(o.shape, F32) for o in outs],
        scratch_shapes=[_residue_scratch(tm, GROUP_W)] * n_scr,
        compiler_params=_params(("parallel",)),
    )(dcomb, *outs, *lses, seg)
    return res[:ng], res[ng:]


def _position():
    return lax.axis_index("x"), lax.axis_index("y"), lax.axis_index("c")


def _other_chips(x, y):
    return [(1 - x, y), (x, 1 - y), (1 - x, 1 - y)]


def _remote(src, dst, send_sems, recv_sems, k, to):
    return pltpu.make_async_remote_copy(src_ref=src, dst_ref=dst, send_sem=send_sems.at[k], recv_sem=recv_sems.at[k],
                                        device_id=to, device_id_type=MESH)


def _gather_descriptors(ins, outs, send_sems, recv_sems, local_sems):
    n = len(ins)
    x, y, c = _position()
    sibling = (x, y, 1 - c)
    chips = _other_chips(x, y)

    def block(a, px, py, pc):
        return outs[a].at[4 * px + 2 * py + pc]

    own, first, arrivals = [], [], []
    for a in range(n):
        k0 = 7 * a
        mine = block(a, x, y, c)
        own.append(pltpu.make_async_copy(ins[a], mine, local_sems.at[a]))
        first.append(_remote(ins[a], mine, send_sems, recv_sems, k0, sibling))
        row = []
        for j, chip in enumerate(chips):
            first.append(_remote(ins[a], mine, send_sems, recv_sems, k0 + 1 + j, (*chip, c)))
            got = block(a, *chip, c)
            row.append((_remote(got, got, send_sems, recv_sems, k0 + 1 + j, sibling),
                        _remote(got, got, send_sems, recv_sems, k0 + 4 + j, sibling)))
        arrivals.append(row)
    return own, first, arrivals


def _gather_begin(ins, outs, send_sems, recv_sems, local_sems):
    own, first, _ = _gather_descriptors(ins, outs, send_sems, recv_sems, local_sems)
    for cp in own + first:
        cp.start()


def _gather_finish(ins, outs, send_sems, recv_sems, local_sems):
    own, first, arrivals = _gather_descriptors(ins, outs, send_sems, recv_sems, local_sems)
    passed = []
    for row in arrivals:
        for arrived, onward in row:
            arrived.wait_recv()
            onward.start()
            passed.append(onward)
    for a in range(len(ins)):
        first[4 * a].wait_recv()
        for _, onward in arrivals[a]:
            onward.wait_recv()
    for cp in first + passed:
        cp.wait_send()
    for cp in own:
        cp.wait()


def _gather_scratch(n):
    return [pltpu.SemaphoreType.DMA((7 * n,)), pltpu.SemaphoreType.DMA((7 * n,)), pltpu.SemaphoreType.DMA((n,))]


def all_gather(shards, name):
    n = len(shards)

    def body(*refs):
        ins, outs, sems = refs[:n], refs[n:2 * n], refs[2 * n:]
        _gather_begin(ins, outs, *sems)
        _gather_finish(ins, outs, *sems)

    hbm = pl.BlockSpec(memory_space=pl.ANY)
    return pl.pallas_call(
        body, name=name,
        in_specs=[hbm] * n, out_specs=[hbm] * n,
        out_shape=[jax.ShapeDtypeStruct((N_DEV,) + s.shape, s.dtype) for s in shards],
        scratch_shapes=_gather_scratch(n),
    )(*shards)


def exchange_sibling(parts, name):
    n = len(parts)

    def body(*refs):
        ins, outs = refs[:n], refs[n:2 * n]
        send_sems, recv_sems = refs[2 * n:]
        x, y, c = _position()
        sibling = (x, y, 1 - c)
        copies = []
        for a in range(n):
            for q in range(4):
                cp = _remote(ins[a].at[2 * q + (1 - c)], outs[a].at[q], send_sems, recv_sems, 4 * a + q, sibling)
                cp.start()
                copies.append(cp)
        for cp in copies:
            cp.wait_recv()
        for cp in copies:
            cp.wait_send()

    hbm = pl.BlockSpec(memory_space=pl.ANY)
    return pl.pallas_call(
        body, name=name,
        in_specs=[hbm] * n, out_specs=[hbm] * n,
        out_shape=[jax.ShapeDtypeStruct((4,) + p.shape[1:], p.dtype) for p in parts],
        scratch_shapes=[pltpu.SemaphoreType.DMA((4 * n,)), pltpu.SemaphoreType.DMA((4 * n,))],
    )(*parts)


def exchange_chips(sums, name):
    n = len(sums)

    def body(*refs):
        ins, outs = refs[:n], refs[n:2 * n]
        send_sems, recv_sems = refs[2 * n:]
        x, y, c = _position()
        copies = []
        for a in range(n):
            for j, (cx, cy) in enumerate(_other_chips(x, y)):
                cp = _remote(ins[a].at[2 * cx + cy], outs[a].at[j], send_sems, recv_sems, 3 * a + j, (cx, cy, c))
                cp.start()
                copies.append(cp)
        for cp in copies:
            cp.wait_recv()
        for cp in copies:
            cp.wait_send()

    hbm = pl.BlockSpec(memory_space=pl.ANY)
    return pl.pallas_call(
        body, name=name,
        in_specs=[hbm] * n, out_specs=[hbm] * n,
        out_shape=[jax.ShapeDtypeStruct((3,) + s.shape[1:], s.dtype) for s in sums],
        scratch_shapes=[pltpu.SemaphoreType.DMA((3 * n,)), pltpu.SemaphoreType.DMA((3 * n,))],
    )(*sums)


_HBM = pl.BlockSpec(memory_space=pltpu.HBM)
_SEM = pl.BlockSpec(memory_space=pltpu.SEMAPHORE)
_DATAFLOW = pltpu.SideEffectType.DATAFLOW_SIDE_EFFECTING


def _to_all_plan(srcs, lands, send_sems, recv_sems):
    x, y, c = _position()
    me = 4 * x + 2 * y + c
    copies = []
    for a in range(len(srcs)):
        for k in range(1, N_DEV):
            fx, fy, fc = (k >> 2) & 1, (k >> 1) & 1, k & 1
            to = (1 - x if fx else x, 1 - y if fy else y, 1 - c if fc else c)
            copies.append(_remote(srcs[a], lands[a].at[me], send_sems, recv_sems, (N_DEV - 1) * a + k - 1, to))
    return copies


def _to_chips_plan(srcs, lands, send_sems, recv_sems):
    x, y, c = _position()
    copies = []
    for a in range(len(srcs)):
        for j, (cx, cy) in enumerate(_other_chips(x, y)):
            copies.append(_remote(srcs[a].at[2 * cx + cy], lands[a].at[j], send_sems, recv_sems, 3 * a + j, (cx, cy, c)))
    return copies


def copies_start(srcs, land_shapes, plan, per_array, name):
    n = len(srcs)
    n_sem = per_array * n
    lands = [lax.empty(s.shape, s.dtype) for s in land_shapes]

    def body(*refs):
        src_refs, land_refs = refs[:n], refs[n:2 * n]
        send_sems, recv_sems = refs[2 * n], refs[2 * n + 1]
        token = refs[-1]
        for cp in plan(src_refs, land_refs, send_sems, recv_sems):
            cp.start()
        token[...] = jnp.zeros_like(token)

    out = pl.pallas_call(
        body, name=name,
        out_shape=(pltpu.SemaphoreType.DMA((n_sem,)), pltpu.SemaphoreType.DMA((n_sem,)))
        + tuple(pltpu.HBM(s.shape, s.dtype) for s in srcs)
        + tuple(pltpu.HBM(s.shape, s.dtype) for s in land_shapes)
        + (jax.ShapeDtypeStruct((8, LANES), F32),),
        in_specs=[_HBM] * (2 * n),
        out_specs=(_SEM, _SEM) + (_HBM,) * (2 * n) + (pl.BlockSpec(memory_space=pltpu.VMEM),),
        input_output_aliases={i: 2 + i for i in range(2 * n)},
        compiler_params=pltpu.CompilerParams(has_side_effects=_DATAFLOW),
    )(*[pltpu.with_memory_space_constraint(s, pltpu.HBM) for s in srcs],
      *[pltpu.with_memory_space_constraint(l, pltpu.HBM) for l in lands])
    return out[:-1], out[-1]


def copies_wait(handles, plan, after, name):
    send_sems, recv_sems = handles[0], handles[1]
    n = (len(handles) - 2) // 2
    thru = handles[2:]

    def body(*refs):
        src_refs, land_refs = refs[:n], refs[n:2 * n]
        send_sems, recv_sems = refs[2 * n], refs[2 * n + 1]
        copies = plan(src_refs, land_refs, send_sems, recv_sems)
        for cp in copies:
            cp.wait_recv()
        for cp in copies:
            cp.wait_send()

    out = pl.pallas_call(
        body, name=name,
        out_shape=tuple(pltpu.HBM(t.shape, t.dtype) for t in thru),
        in_specs=[_HBM] * (2 * n) + [_SEM, _SEM, pl.BlockSpec(memory_space=pl.ANY)],
        out_specs=(_HBM,) * (2 * n),
        input_output_aliases={i: i for i in range(2 * n)},
        compiler_params=pltpu.CompilerParams(has_side_effects=_DATAFLOW),
    )(*thru, send_sems, recv_sems, after)
    return out[n:]


def all_sum_small(vec, name):
    R = vec.shape[0]

    def body(v_ref, tot_ref, all_ref, send_sems, recv_sems):
        x, y, c = _position()
        me = 4 * x + 2 * y + c
        all_ref[me] = v_ref[...]
        copies = []
        for k in range(1, N_DEV):
            fx, fy, fc = (k >> 2) & 1, (k >> 1) & 1, k & 1
            to = (1 - x if fx else x, 1 - y if fy else y, 1 - c if fc else c)
            cp = _remote(v_ref, all_ref.at[me], send_sems, recv_sems, k - 1, to)
            cp.start()
            copies.append(cp)
        for cp in copies:
            cp.wait_recv()
        for cp in copies:
            cp.wait_send()
        tot = all_ref[0]
        for j in range(1, N_DEV):
            tot = tot + all_ref[j]
        tot_ref[...] = tot

    vmem = pl.BlockSpec(memory_space=pltpu.VMEM)
    return pl.pallas_call(
        body, name=name,
        in_specs=[vmem], out_specs=vmem,
        out_shape=jax.ShapeDtypeStruct((R, LANES), F32),
        scratch_shapes=[pltpu.VMEM((N_DEV, R, LANES), F32),
                        pltpu.SemaphoreType.DMA((N_DEV - 1,)), pltpu.SemaphoreType.DMA((N_DEV - 1,))],
        compiler_params=pltpu.CompilerParams(vmem_limit_bytes=VMEM_LIMIT),
    )(vec)


def pair_add(parts, theirs, place, name):
    _, R, C = theirs.shape
    tr = _pick(R, 256, 8)

    def body(place_ref, a_ref, b_ref, o_ref):
        o_ref[...] = (a_ref[...].astype(F32) + b_ref[...].astype(F32)).astype(BF16)

    blk = pl.BlockSpec((None, tr, C), lambda q, i, place_ref: (q, i, 0))
    return pl.pallas_call(
        body, name=name,
        grid_spec=pltpu.PrefetchScalarGridSpec(
            num_scalar_prefetch=1, grid=(4, R // tr),
            in_specs=[pl.BlockSpec((None, tr, C), lambda q, i, place_ref: (2 * q + place_ref[2], i, 0)), blk],
            out_specs=blk),
        out_shape=jax.ShapeDtypeStruct(theirs.shape, BF16),
        compiler_params=_params(("parallel", "parallel")),
    )(place, parts, theirs)


def _adamw_math(w, g, m, v):
    m = ADAM_B1 * m + (1.0 - ADAM_B1) * g
    v = ADAM_B2 * v + (1.0 - ADAM_B2) * jnp.square(g)
    m_hat = m / (1.0 - ADAM_B1 ** ADAM_STEP)
    v_hat = v / (1.0 - ADAM_B2 ** ADAM_STEP)
    delta = -ADAM_LR * (m_hat / (jnp.sqrt(v_hat) + ADAM_EPS) + ADAM_WD * w)
    return delta, m, v


def adamw_sharded(w, m, v, parts, sib, others, place, name):
    R, C = w.shape
    tr = _pick(R, 256, 8)

    def body(place_ref, w_ref, m_ref, v_ref, a_ref, b_ref, o_ref, g_ref, d_ref, nm_ref, nv_ref):
        g = a_ref[...].astype(F32) + b_ref[...].astype(F32)
        for j in range(3):
            g = g + o_ref[j].astype(F32)
        delta, nm, nv = _adamw_math(w_ref[...], g, m_ref[...], v_ref[...])
        g_ref[...] = g
        d_ref[...] = delta
        nm_ref[...] = nm
        nv_ref[...] = nv

    row = pl.BlockSpec((tr, C), lambda i, place_ref: (i, 0))
    return pl.pallas_call(
        body, name=name,
        grid_spec=pltpu.PrefetchScalarGridSpec(
            num_scalar_prefetch=1, grid=(R // tr,),
            in_specs=[row] * 3 + [pl.BlockSpec((None, tr, C), lambda i, place_ref: (place_ref[0], i, 0)),
                                  pl.BlockSpec((None, tr, C), lambda i, place_ref: (place_ref[1], i, 0)),
                                  pl.BlockSpec((3, tr, C), lambda i, place_ref: (0, i, 0))],
            out_specs=[row] * 4),
        out_shape=[jax.ShapeDtypeStruct((R, C), F32)] * 4,
        compiler_params=_params(("parallel",)),
    )(place, w, m, v, parts, sib, others)


def adamw_packed(w, g, m, v, name):
    R = w.shape[0]

    def body(w_ref, g_ref, m_ref, v_ref, d_ref, nm_ref, nv_ref):
        delta, nm, nv = _adamw_math(w_ref[...], g_ref[...], m_ref[...], v_ref[...])
        d_ref[...] = delta
        nm_ref[...] = nm
        nv_ref[...] = nv

    full = pl.BlockSpec((R, LANES), lambda i: (0, 0))
    return pl.pallas_call(
        body, name=name, grid=(1,),
        in_specs=[full] * 4, out_specs=[full] * 3,
        out_shape=[jax.ShapeDtypeStruct((R, LANES), F32)] * 3,
        compiler_params=_params(("arbitrary",)),
    )(w, g, m, v)


def _pack(arrays):
    flat = []
    sizes = []
    for a in arrays:
        f = a.reshape(-1).astype(F32)
        pad = (-f.shape[0]) % LANES
        if pad:
            f = jnp.concatenate([f, jnp.zeros((pad,), F32)])
        flat.append(f)
        sizes.append(f.shape[0])
    rows = sum(sizes) // LANES
    pad_rows = (-rows) % 8
    if pad_rows:
        flat.append(jnp.zeros((pad_rows * LANES,), F32))
    return jnp.concatenate(flat).reshape(-1, LANES), sizes


def _unpack(packed, sizes, shapes):
    flat = packed.reshape(-1)
    out = []
    off = 0
    for size, shape in zip(sizes, shapes):
        n = int(np.prod(shape))
        out.append(flat[off:off + n].reshape(shape))
        off += size
    return out


def _to_blocks(full, axis):
    if axis == 0:
        return full.reshape(N_DEV, full.shape[0] // N_DEV, full.shape[1])
    r, n = full.shape
    return full.reshape(r, N_DEV, n // N_DEV).transpose(1, 0, 2)


def _from_blocks(blocks, axis):
    if axis == 0:
        return blocks.reshape(blocks.shape[0] * blocks.shape[1], blocks.shape[2])
    return blocks.transpose(1, 0, 2).reshape(blocks.shape[1], blocks.shape[0] * blocks.shape[2])


def kernel(x, ln0_g, ln0_b, w_in, b_in, conv_w, w_a, w_b, w_o, b_o, ln1_g, ln1_b, w_up, b_up, ffn_conv_w, ffn_conv_b, w_down, b_down, ln2_g, ln2_b, loss_target, m_ln0_g, m_ln0_b, m_w_in, m_b_in, m_conv_w, m_w_a, m_w_b, m_w_o, m_b_o, m_ln1_g, m_ln1_b, m_w_up, m_b_up, m_ffn_conv_w, m_ffn_conv_b, m_w_down, m_b_down, m_ln2_g, m_ln2_b, v_ln0_g, v_ln0_b, v_w_in, v_b_in, v_conv_w, v_w_a, v_w_b, v_w_o, v_b_o, v_ln1_g, v_ln1_b, v_w_up, v_b_up, v_ffn_conv_w, v_ffn_conv_b, v_w_down, v_b_down, v_ln2_g, v_ln2_b):
    T, D = x.shape[1], x.shape[2]
    F = ffn_conv_b.shape[-1]
    xs = x.reshape(T, D)
    tgt = loss_target.reshape(T, D)
    dev = 4 * lax.axis_index("x") + 2 * lax.axis_index("y") + lax.axis_index("c")
    chip = 2 * lax.axis_index("x") + lax.axis_index("y")
    core = lax.axis_index("c")
    place = jnp.stack([dev, chip, core]).astype(jnp.int32)

    big = dict(w_in=(w_in[0], 1), w_a=(w_a[0], 0), w_b=(w_b[0], 1), w_o=(w_o[0], 0), w_up=(w_up[0], 1),
               w_down=(w_down[0], 0))
    names = list(big)
    shards = {k: big[k][0].astype(BF16) for k in names}
    ln0g, ln0b = ln0_g.reshape(1, D), ln0_b.reshape(1, D)
    h0, h0b, *rest = ln_fwd(xs, None, ln0g, ln0b, "ln0_fwd_gather_w_in", dilations=DILATIONS[1:],
                            gather=[shards["w_in"], conv_w[0], ffn_conv_w[0]])
    h0_res = [h0b] + [h.reshape(T, D) for h in rest[:2]]
    g_in, g_conv, g_fcw = rest[2:]
    full = {"w_in": _from_blocks(g_in, 1)}
    conv_full = _from_blocks(g_conv, 1)
    fcw_full = _from_blocks(g_fcw, 1)
    late_groups = (("w_a", "w_b", "w_o"), ("w_up", "w_down"))
    late_handles = []
    token = conv_full[:1, :1] * 0.0
    for n, keys in enumerate(late_groups):
        srcs = [shards[k] + token[0, 0].astype(BF16) for k in keys]
        handles, token = copies_start(srcs, [jax.ShapeDtypeStruct((N_DEV,) + s.shape, BF16) for s in srcs],
                                      _to_all_plan, N_DEV - 1, f"gather_late_{n}_start")
        late_handles.append(handles)

    def late_weights(n, after):
        lands = copies_wait(late_handles[n], _to_all_plan, after, f"gather_late_{n}_wait")
        for k, land in zip(late_groups[n], lands):
            full[k] = _from_blocks(lax.dynamic_update_index_in_dim(land, shards[k], dev, 0), big[k][1])

    o_q = 3 * D
    o_g = o_q + 3 * QKV_W
    w_pa, w_qkv, w_pg = full["w_in"][:, :o_q], full["w_in"][:, o_q:o_g], full["w_in"][:, o_g:]
    b_pa, b_qkv, b_pg = b_in[:, :o_q], b_in[:, o_q:o_g], b_in[:, o_g:]

    proj_a = mm_nn(h0b, w_pa, b_pa, ACT, "proj_conv", after=token)
    proj_g = mm_nn(h0b, w_pg, b_pg, ACT, "proj_gates")
    zero_d = jnp.zeros((1, D), F32)
    s_a = conv_a_fwd(proj_a, conv_full, "conv_a_fwd")
    late_weights(0, s_a)
    y_a = mm_nn(s_a, full["w_a"], zero_d, ACT, "branch_a_out")

    def group_cols(m, g):
        return jnp.concatenate([m[:, s * QKV_W + g * GROUP_W:s * QKV_W + (g + 1) * GROUP_W] for s in range(3)], 1)

    w_grp = [group_cols(w_qkv, g) for g in range(3)]
    qkvs, outs, lses = [], [], []
    for g, d in enumerate(DILATIONS):
        qkv = mm_nn(h0_res[g], w_grp[g], group_cols(b_qkv, g), BF16, f"proj_qkv_{g}").reshape(d, T // d, 3 * GROUP_W)
        o, l = att_fwd(qkv, g, f"att_fwd_{g}")
        qkvs.append(qkv)
        outs.append(o)
        lses.append(l)
    comb = combine_fwd(outs, lses, "combine_fwd")
    y_b = mm_nn(comb, full["w_b"], zero_d, ACT, "branch_b_out")
    z = gate_fwd(proj_g, y_a, y_b, "gate_fwd")
    h1, h1b, mix = ln_fwd(h0, ("nn", z, full["w_o"], b_o), ln1_g, ln1_b, "mix_out_ln1_fwd")
    late_weights(1, h1b)
    up, f_act = ffn_up_conv_f(h1b, full["w_up"], b_up, fcw_full, ffn_conv_b, "ffn_up_conv_f")

    dr2, dr2b, d_ln2_g, d_ln2_b, d_b_down, loss_part = ln_bwd(
        h1, ("nn", f_act, full["w_down"], b_down), ln2_g, ln2_b, None, None, tgt, "ffn_down_ln2_loss_bwd")
    dw_down, _ = mm_tn(f_act, dr2b, "dw_down")
    d_a, d_gate, cs_a, cs_gate, d_fcb, d_fcw = conv_f_bwd(dr2b, full["w_down"], up, fcw_full, ffn_conv_b,
                                                          "d_ffn_act_conv_f_bwd")
    dw_up_a, _ = mm_tn(h1b, d_a, "dw_up_a")
    dw_up_g, _ = mm_tn(h1b, d_gate, "dw_up_gate")
    dr1, dr1b, d_ln1_g, d_ln1_b, d_b_o, _ = ln_bwd(h0, mix, ln1_g, ln1_b, dr2, ("nt", [d_a, d_gate], full["w_up"]), None,
                                                   "d_h1_ln1_bwd")
    dw_o, _ = mm_tn(z, dr1b, "dw_o")
    dz = mm_nt(dr1b, full["w_o"], None, "d_z", out_dtype=ACT)
    dy_a, dy_b, dproj_g = gate_bwd(dz, proj_g, y_a, y_b, "gate_bwd")
    dw_a, _ = mm_tn(s_a, dy_a, "dw_a")
    ds_a = mm_nt(dy_a, full["w_a"], None, "d_s_a", out_dtype=ACT)
    dproj_a, d_conv = conv_a_bwd(ds_a, proj_a, conv_full, "conv_a_bwd")
    dw_b, _ = mm_tn(comb, dy_b, "dw_b")

    rs_mine, rs_sib, rs_handles = {}, {}, {}

    def reduce_start(keys, grads, tag):
        parts = [_to_blocks(grads[k], big[k][1]) for k in keys]
        from_sib = exchange_sibling(parts, f"grads_to_sibling_{tag}")
        sums = [pair_add(a, b, place, f"chip_sum_{k}") for k, a, b in zip(keys, parts, from_sib)]
        handles, tok = copies_start(sums, [jax.ShapeDtypeStruct((3,) + s.shape[1:], BF16) for s in sums],
                                    _to_chips_plan, 3, f"grads_to_chips_{tag}_start")
        for k, a, b in zip(keys, parts, from_sib):
            rs_mine[k], rs_sib[k] = a, b
        rs_handles[tag] = (keys, handles)
        return tok

    tok_a = reduce_start(("w_a", "w_b", "w_o", "w_up", "w_down"),
                         dict(w_a=dw_a, w_b=dw_b, w_o=dw_o, w_up=jnp.concatenate([dw_up_a, dw_up_g], 1), w_down=dw_down),
                         "a")
    dcomb = mm_nt(dy_b, full["w_b"], None, "d_comb", after=tok_a, out_dtype=ACT)
    dos, dms = combine_bwd(dcomb, outs, lses, "combine_bwd")
    dw_grp, cs_grp, dqkvs = [], [], []
    for g, d in enumerate(DILATIONS):
        dq, dk, dv = att_bwd(qkvs[g], dos[g], lses[g], dms[g], g, f"att_bwd_{g}")
        dqkv = [t.reshape(T, GROUP_W) for t in (dq, dk, dv)]
        dwg, csg = mm_tn(h0_res[g], dqkv, f"dw_in_qkv_{g}")
        dqkvs.append(dqkv)
        dw_grp.append(dwg)
        cs_grp.append(csg)
    dw_pa, cs_pa = mm_tn(h0b, dproj_a, "dw_in_conv")
    dw_pg, cs_pg = mm_tn(h0b, dproj_g, "dw_in_gates")

    def ungroup(parts):
        return jnp.concatenate([p[:, s * GROUP_W:(s + 1) * GROUP_W] for s in range(3) for p in parts], 1)

    db_in_parts = [cs_pa, ungroup(cs_grp), cs_pg]
    tok_b = reduce_start(("w_in",), dict(w_in=jnp.concatenate([dw_pa, ungroup(dw_grp), dw_pg], 1)), "b")
    dh0 = mm_nt(dproj_a, w_pa, None, "d_h0_conv", after=tok_b)
    dh0 = mm_nt(dproj_g, w_pg, dh0, "d_h0_gates")
    dh0_res = [(mm_nt(dqkvs[g], w_grp[g], None, f"d_h0_qkv_{g}").reshape(d, T // d, D), d)
               for g, d in enumerate(DILATIONS) if g > 0]
    dx, _, d_ln0_g, d_ln0_b, _, _ = ln_bwd(xs, None, ln0g, ln0b, dr1, ("nt", dqkvs[0], w_grp[0]), None, "d_h0_ln0_bwd",
                                           by_residue=[(dh0.reshape(1, T, D), 1)] + dh0_res)

    small = [d_ln0_g, d_ln0_b, jnp.concatenate(db_in_parts, 1), d_conv, d_b_o, d_ln1_g, d_ln1_b,
             jnp.concatenate([cs_a, cs_gate], 1), d_fcw, d_fcb, d_b_down, d_ln2_g, d_ln2_b, loss_part]
    packed, sizes = _pack(small)
    total = all_sum_small(packed, "sum_small")
    (g_ln0_g, g_ln0_b, g_b_in, g_conv_full, g_b_o, g_ln1_g, g_ln1_b, g_b_up, g_fcw_full, g_fcb, g_b_down, g_ln2_g,
     g_ln2_b, loss) = _unpack(total, sizes, [a.shape for a in small])
    cw = conv_w.shape[-1]
    fw = ffn_conv_w.shape[-1]
    g_conv = lax.dynamic_slice_in_dim(g_conv_full, dev * cw, cw, 1)
    g_fcw = lax.dynamic_slice_in_dim(g_fcw_full, dev * fw, fw, 1)

    from_chips = {}
    for tag, (keys, handles) in rs_handles.items():
        lands = copies_wait(handles, _to_chips_plan, total, f"grads_to_chips_{tag}_wait")
        from_chips.update(zip(keys, lands))

    moments = dict(w_in=(m_w_in, v_w_in), w_a=(m_w_a, v_w_a), w_b=(m_w_b, v_w_b), w_o=(m_w_o, v_w_o),
                   w_up=(m_w_up, v_w_up), w_down=(m_w_down, v_w_down))
    res_big = {}
    for k in names:
        res_big[k] = adamw_sharded(big[k][0], moments[k][0][0], moments[k][1][0], rs_mine[k], rs_sib[k], from_chips[k],
                                   place, f"adamw_{k}")

    small_names = ["ln0_g", "ln0_b", "b_in", "conv_w", "b_o", "ln1_g", "ln1_b", "b_up", "ffn_conv_w", "ffn_conv_b",
                   "b_down", "ln2_g", "ln2_b"]
    small_w = [ln0_g, ln0_b, b_in, conv_w, b_o, ln1_g, ln1_b, b_up, ffn_conv_w, ffn_conv_b, b_down, ln2_g, ln2_b]
    small_m = [m_ln0_g, m_ln0_b, m_b_in, m_conv_w, m_b_o, m_ln1_g, m_ln1_b, m_b_up, m_ffn_conv_w, m_ffn_conv_b,
               m_b_down, m_ln2_g, m_ln2_b]
    small_v = [v_ln0_g, v_ln0_b, v_b_in, v_conv_w, v_b_o, v_ln1_g, v_ln1_b, v_b_up, v_ffn_conv_w, v_ffn_conv_b,
               v_b_down, v_ln2_g, v_ln2_b]
    small_g = [g_ln0_g, g_ln0_b, g_b_in, g_conv, g_b_o, g_ln1_g, g_ln1_b, g_b_up, g_fcw, g_fcb, g_b_down, g_ln2_g,
               g_ln2_b]
    shapes = [w.shape for w in small_w]
    small_g = [g.reshape(s) for g, s in zip(small_g, shapes)]
    pw, psz = _pack(small_w)
    pg, _ = _pack(small_g)
    pm, _ = _pack(small_m)
    pv, _ = _pack(small_v)
    pd, pnm, pnv = adamw_packed(pw, pg, pm, pv, "adamw_small")
    res_small = {k: (g, d_, m_, v_) for k, g, d_, m_, v_ in zip(
        small_names, small_g, _unpack(pd, psz, shapes), _unpack(pnm, psz, shapes), _unpack(pnv, psz, shapes))}

    order = ["ln0_g", "ln0_b", "w_in", "b_in", "conv_w", "w_a", "w_b", "w_o", "b_o", "ln1_g", "ln1_b", "w_up", "b_up",
             "ffn_conv_w", "ffn_conv_b", "w_down", "b_down", "ln2_g", "ln2_b"]

    def result(k, j):
        if k in res_big:
            return res_big[k][j][None]
        return res_small[k][j]

    out = [loss.reshape(()), dx.reshape(x.shape)]
    for j in range(4):
        out += [result(k, j) for k in order]
    return tuple(out)
```

```python
import functools
import math

import numpy as np
import jax
import jax.numpy as jnp
from jax import lax
from jax.experimental import pallas as pl
from jax.experimental.pallas import tpu as pltpu

F32 = jnp.float32
BF16 = jnp.bfloat16
ACT = BF16

N_DEV = 8
LN_EPS = 1e-5
ALPHA = (2.0 * 1) ** 0.25
MASK_VALUE = -1e30
HEAD_DIM = 64
GROUP_W = 512
QKV_W = 3 * GROUP_W
DILATIONS = (1, 4, 16)
RADIUS = 64
LANES = 128
HALO = 8
HALO_BF16 = 16
ATT_TQ = 128

ADAM_LR = 0.001
ADAM_B1 = 0.9
ADAM_B2 = 0.999
ADAM_EPS = 1e-08
ADAM_WD = 0.01
ADAM_STEP = 10

VMEM_LIMIT = 52 * 1024 * 1024
OUT_TILE_BYTES = 8 * 1024 * 1024
MESH = pl.DeviceIdType.MESH
NT_DIMS = (((1,), (1,)), ((), ()))
TN_DIMS = (((0,), (0,)), ((), ()))


def _pick(n, target, align=LANES):
    if n <= target:
        return n
    best = None
    for t in range(align, target + 1, align):
        if n % t == 0:
            best = t
    assert best is not None, (n, target, align)
    return best


def _params(sems=None):
    return pltpu.CompilerParams(dimension_semantics=sems, vmem_limit_bytes=VMEM_LIMIT)


def _alibi_slopes():
    n = 3 * 8
    return np.exp2(-8.0 * np.arange(1, n + 1, dtype=np.float64) / n).astype(np.float32).reshape(3, 8)


def _ln_stats(r):
    mu = jnp.mean(r, -1, keepdims=True)
    xc = r - mu
    var = jnp.mean(xc * xc, -1, keepdims=True)
    rstd = lax.rsqrt(var + LN_EPS)
    return xc, rstd


def _load_natural(ref, d, scr):
    if d == 1:
        return ref[0]
    n, C = ref.shape[1], ref.shape[2]
    for c in range(C // LANES):
        for r in range(d):
            scr[c, pl.ds(r, n, stride=d), :] = ref[r, :, c * LANES:(c + 1) * LANES]
    return jnp.concatenate([scr[c] for c in range(C // LANES)], axis=1)


def _store_by_residue(val, ref, d, scr):
    if d == 1:
        ref[0] = val.astype(ref.dtype)
        return
    n, C = ref.shape[1], ref.shape[2]
    for c in range(C // LANES):
        scr[c] = val[:, c * LANES:(c + 1) * LANES]
    for c in range(C // LANES):
        for r in range(d):
            ref[r, :, c * LANES:(c + 1) * LANES] = scr[c, pl.ds(r, n, stride=d), :].astype(ref.dtype)


def _residue_spec(tm, d, C):
    return pl.BlockSpec((d, tm // d, C), lambda i: (0, i, 0))


def _residue_scratch(tm, C):
    return pltpu.VMEM((C // LANES, tm, LANES), F32)


def ln_fwd(a, res, g, b, name, dilations=(), gather=()):
    T, D = a.shape
    res_mm = isinstance(res, tuple)
    tm = _pick(T, 256 if res_mm else 512, 8)
    res_ins = list(res[1:]) if res_mm else ([] if res is None else [res])
    nd = len(dilations)
    ng = len(gather)
    n_in = 1 + len(res_ins) + 2
    last = T // tm - 1

    def body(*refs):
        a_ref = refs[0]
        r = a_ref[...]
        if res_mm:
            res_val = jnp.dot(refs[1][...], refs[2][...], preferred_element_type=F32) + refs[3][...]
            refs[-1 - n_scratch][...] = res_val
            r = ALPHA * r + res_val
        elif res_ins:
            r = ALPHA * r + refs[1][...]
        g_ref, b_ref = refs[n_in - 2], refs[n_in - 1]
        shard_refs = refs[n_in:n_in + ng]
        h_ref, hb_ref = refs[n_in + ng], refs[n_in + ng + 1]
        p_refs = refs[n_in + ng + 2:n_in + ng + 2 + nd]
        full_refs = refs[n_in + ng + 2 + nd:n_in + 2 * ng + 2 + nd]
        scratch = refs[len(refs) - n_scratch:]
        sems = scratch[len(scratch) - 3:] if ng else ()

        if ng:
            @pl.when(pl.program_id(0) == 0)
            def _():
                _gather_begin(shard_refs, full_refs, *sems)

        xc, rstd = _ln_stats(r)
        h = xc * rstd * g_ref[...] + b_ref[...]
        h_ref[...] = h
        hb_ref[...] = h.astype(BF16)
        for d, p_ref in zip(dilations, p_refs):
            _store_by_residue(h, p_ref, d, scratch[0])

        if ng:
            @pl.when(pl.program_id(0) == last)
            def _():
                _gather_finish(shard_refs, full_refs, *sems)

    row = pl.BlockSpec((tm, D), lambda i: (i, 0))
    vec = pl.BlockSpec((1, D), lambda i: (0, 0))
    hbm = pl.BlockSpec(memory_space=pl.ANY)
    if res_mm:
        res_specs = [pl.BlockSpec((tm, res[1].shape[1]), lambda i: (i, 0)), pl.BlockSpec(res[2].shape, lambda i: (0, 0)), vec]
    else:
        res_specs = [row] * len(res_ins)
    scratch_shapes = ([_residue_scratch(tm, D)] if nd else []) + (_gather_scratch(ng) if ng else [])
    n_scratch = len(scratch_shapes)
    ins = [a] + res_ins + [g, b] + list(gather)
    return pl.pallas_call(
        body, name=name, grid=(T // tm,),
        in_specs=[row] + res_specs + [vec, vec] + [hbm] * ng,
        out_specs=[row, row] + [_residue_spec(tm, d, D) for d in dilations] + [hbm] * ng + ([row] if res_mm else []),
        out_shape=[jax.ShapeDtypeStruct((T, D), F32), jax.ShapeDtypeStruct((T, D), BF16)]
        + [jax.ShapeDtypeStruct((d, T // d, D), BF16) for d in dilations]
        + [jax.ShapeDtypeStruct((N_DEV,) + s.shape, s.dtype) for s in gather]
        + ([jax.ShapeDtypeStruct((T, D), F32)] if res_mm else []),
        scratch_shapes=scratch_shapes,
        compiler_params=_params(("arbitrary",) if ng else ("parallel",)),
    )(*ins)


def ln_bwd(a, res, g, b, d1, d2, tgt, name, by_residue=()):
    T, D = a.shape
    tm = _pick(T, 256, 8)
    loss_mode = tgt is not None
    nres = len(by_residue)
    row = pl.BlockSpec((tm, D), lambda i: (i, 0))
    vec = pl.BlockSpec((1, D), lambda i: (0, 0))
    one = pl.BlockSpec((1, 1), lambda i: (0, 0))

    def rows_of(x):
        return pl.BlockSpec((tm, x.shape[1]), lambda i: (i, 0))

    def whole(x):
        return pl.BlockSpec(x.shape, lambda i: (0, 0))

    ins, in_specs, slots = [], [], {}

    def operand(key, arrays, specs):
        slots[key] = (len(ins), len(arrays))
        ins.extend(arrays)
        in_specs.extend(specs)

    operand("a", [a], [row])
    if isinstance(res, tuple):
        _, x, w, bias = res
        operand("res_mm", [x, w, bias], [rows_of(x), whole(w), vec])
    elif res is not None:
        operand("res", [res], [row])
    operand("gb", [g, b], [vec, vec])
    if loss_mode:
        operand("tgt", [tgt], [row])
    else:
        operand("d1", [d1], [row])
        if isinstance(d2, tuple):
            _, pieces, w = d2
            operand("d2_mm", list(pieces) + [w], [rows_of(p) for p in pieces] + [whole(w)])
        else:
            operand("d2", [d2], [row])
    operand("by_residue", [e for e, _ in by_residue], [_residue_spec(tm, d, D) for _, d in by_residue])
    n_in = len(ins)

    def body(*refs):
        def get(key):
            first, count = slots[key]
            return refs[first:first + count]

        dr_ref, drb_ref, dg_ref, db_ref, ds_ref, loss_ref = refs[n_in:n_in + 6]
        i = pl.program_id(0)

        @pl.when(i == 0)
        def _():
            dg_ref[...] = jnp.zeros_like(dg_ref)
            db_ref[...] = jnp.zeros_like(db_ref)
            ds_ref[...] = jnp.zeros_like(ds_ref)
            loss_ref[...] = jnp.zeros_like(loss_ref)

        r = get("a")[0][...]
        if "res_mm" in slots:
            x_ref, w_ref, bias_ref = get("res_mm")
            r = ALPHA * r + (jnp.dot(x_ref[...], w_ref[...], preferred_element_type=F32) + bias_ref[...])
        elif "res" in slots:
            r = ALPHA * r + get("res")[0][...]
        g_ref, b_ref = get("gb")
        xc, rstd = _ln_stats(r)
        xhat = xc * rstd
        gam = g_ref[...]
        if loss_mode:
            err = xhat * gam + b_ref[...] - get("tgt")[0][...]
            dy = err * (1.0 / D)
            row_loss = jnp.mean(err * err, -1, keepdims=True)
            loss_ref[...] += 0.5 * jnp.sum(row_loss, 0, keepdims=True)
        else:
            if "d2_mm" in slots:
                *p_refs, w_ref = get("d2_mm")
                av = p_refs[0][...] if len(p_refs) == 1 else jnp.concatenate([p[...] for p in p_refs], axis=1)
                d2v = lax.dot_general(av, w_ref[...], NT_DIMS, preferred_element_type=F32)
            else:
                d2v = get("d2")[0][...]
            dy = ALPHA * get("d1")[0][...] + d2v
        for (_, d), e_ref in zip(by_residue, get("by_residue")):
            dy = dy + _load_natural(e_ref, d, refs[-1])
        dyg = dy * gam
        c1 = jnp.mean(dyg, -1, keepdims=True)
        c2 = jnp.mean(dyg * xhat, -1, keepdims=True)
        dr = rstd * (dyg - c1 - xhat * c2)
        dr_ref[...] = dr
        drb_ref[...] = dr.astype(BF16)
        dg_ref[...] += jnp.sum(dy * xhat, 0, keepdims=True)
        db_ref[...] += jnp.sum(dy, 0, keepdims=True)
        ds_ref[...] += jnp.sum(dr, 0, keepdims=True)

    return pl.pallas_call(
        body, name=name, grid=(T // tm,),
        in_specs=in_specs,
        out_specs=[row, row, vec, vec, vec, one],
        out_shape=[jax.ShapeDtypeStruct((T, D), F32), jax.ShapeDtypeStruct((T, D), BF16),
                   jax.ShapeDtypeStruct((1, D), F32), jax.ShapeDtypeStruct((1, D), F32),
                   jax.ShapeDtypeStruct((1, D), F32), jax.ShapeDtypeStruct((1, 1), F32)],
        scratch_shapes=[_residue_scratch(tm, D)] if nres else [],
        compiler_params=_params(("arbitrary",)),
    )(*ins)


_TOKEN_SPEC = pl.BlockSpec((8, LANES), lambda i: (0, 0))


def mm_nn(a, w, bias, out_dtype, name, after=None):
    M, K = a.shape
    N = w.shape[1]
    tm = _pick(M, max(256, min(1024, OUT_TILE_BYTES // (N * jnp.dtype(out_dtype).itemsize))), 8)
    tc = _pick(N, 512)

    def body(a_ref, w_ref, b_ref, *rest):
        o_ref = rest[-1]
        av = a_ref[...]
        for j in range(N // tc):
            cols = slice(j * tc, (j + 1) * tc)
            acc = jnp.dot(av, w_ref[:, cols], preferred_element_type=F32)
            o_ref[:, cols] = (acc + b_ref[:, cols]).astype(out_dtype)

    return pl.pallas_call(
        body, name=name, grid=(M // tm,),
        in_specs=[pl.BlockSpec((tm, K), lambda i: (i, 0)),
                  pl.BlockSpec((K, N), lambda i: (0, 0)),
                  pl.BlockSpec((1, N), lambda i: (0, 0))] + ([] if after is None else [_TOKEN_SPEC]),
        out_specs=pl.BlockSpec((tm, N), lambda i: (i, 0)),
        out_shape=jax.ShapeDtypeStruct((M, N), out_dtype),
        compiler_params=_params(("parallel",)),
    )(a, w, bias, *([] if after is None else [after]))


def mm_nt(a, w, acc_in, name, after=None, w_block=0, out_dtype=F32):
    pieces = list(a) if isinstance(a, (list, tuple)) else [a]
    M = pieces[0].shape[0]
    widths = [p.shape[1] for p in pieces]
    K = sum(widths)
    N = w.shape[0]
    tm = _pick(M, 512, 8)
    tc = _pick(N, 512)
    has_acc = acc_in is not None
    n_a = len(pieces)

    def body(*refs):
        a_refs, w_ref = refs[:n_a], refs[n_a]
        c_ref = refs[n_a + 1] if has_acc else None
        o_ref = refs[-1]
        av = a_refs[0][...] if n_a == 1 else jnp.concatenate([r[...] for r in a_refs], axis=1)
        for j in range(N // tc):
            cols = slice(j * tc, (j + 1) * tc)
            acc = lax.dot_general(av, w_ref[cols, :], NT_DIMS, preferred_element_type=F32)
            if has_acc:
                acc = acc + c_ref[:, cols]
            o_ref[:, cols] = acc.astype(out_dtype)

    out_spec = pl.BlockSpec((tm, N), lambda i: (i, 0))
    in_specs = [pl.BlockSpec((tm, kw), lambda i: (i, 0)) for kw in widths]
    in_specs.append(pl.BlockSpec((N, K), lambda i: (0, w_block)))
    ins = pieces + [w]
    if has_acc:
        in_specs.append(out_spec)
        ins.append(acc_in)
    if after is not None:
        in_specs.append(_TOKEN_SPEC)
        ins.append(after)
    return pl.pallas_call(
        body, name=name, grid=(M // tm,),
        in_specs=in_specs, out_specs=out_spec,
        out_shape=jax.ShapeDtypeStruct((M, N), out_dtype),
        compiler_params=_params(("parallel",)),
    )(*ins)


def mm_tn(a, b, name, out_dtype=BF16):
    pieces = list(b) if isinstance(b, (list, tuple)) else [b]
    T, M = a.shape
    widths = [p.shape[1] for p in pieces]
    N = sum(widths)
    tk = _pick(T, 512, 8)
    nk = T // tk
    tc = _pick(M, 256)
    n_b = len(pieces)

    def body(*refs):
        a_ref, b_refs = refs[0], refs[1:1 + n_b]
        o_ref, cs_ref, acc_ref = refs[1 + n_b:]
        k = pl.program_id(0)

        @pl.when(k == 0)
        def _():
            acc_ref[...] = jnp.zeros_like(acc_ref)
            cs_ref[...] = jnp.zeros_like(cs_ref)

        bv = b_refs[0][...] if n_b == 1 else jnp.concatenate([r[...] for r in b_refs], axis=1)
        cs_ref[...] += jnp.sum(bv.astype(F32), 0, keepdims=True)
        for mi in range(M // tc):
            rows = slice(mi * tc, (mi + 1) * tc)
            acc_ref[rows, :] += lax.dot_general(a_ref[:, rows], bv, TN_DIMS, preferred_element_type=F32)

        @pl.when(k == nk - 1)
        def _():
            o_ref[...] = acc_ref[...].astype(out_dtype)

    return pl.pallas_call(
        body, name=name, grid=(nk,),
        in_specs=[pl.BlockSpec((tk, M), lambda k: (k, 0))] + [pl.BlockSpec((tk, wd), lambda k: (k, 0)) for wd in widths],
        out_specs=[pl.BlockSpec((M, N), lambda k: (0, 0)), pl.BlockSpec((1, N), lambda k: (0, 0))],
        out_shape=[jax.ShapeDtypeStruct((M, N), out_dtype), jax.ShapeDtypeStruct((1, N), F32)],
        scratch_shapes=[pltpu.VMEM((M, N), F32)],
        compiler_params=_params(("arbitrary",)),
    )(a, *pieces)


def _ext_rows(prev_ref, main_ref, next_ref, i, tm, T, dtype=F32):
    before = jnp.where(i == 0, 0.0, prev_ref[...])
    after = jnp.where(i == T // tm - 1, 0.0, next_ref[...])
    return jnp.concatenate([before, main_ref[...], after], axis=0).astype(dtype)


def _prev_row(x):
    return pltpu.roll(x, 1, 0)


def _next_row(x):
    return pltpu.roll(x, x.shape[0] - 1, 0)


def _conv3(u, w_ref):
    return _prev_row(u) * w_ref[0:1, :] + u * w_ref[1:2, :] + _next_row(u) * w_ref[2:3, :]


def _main(x, tm, halo=HALO):
    return x[halo:halo + tm]


def _halo_specs(tm, tc, T, col, order, halo=HALO):
    r = tm // halo
    last = T // halo - 1
    if order == "ij":
        return (pl.BlockSpec((halo, tc), lambda i, j: (jnp.maximum(i * r - 1, 0), col(j))),
                pl.BlockSpec((tm, tc), lambda i, j: (i, col(j))),
                pl.BlockSpec((halo, tc), lambda i, j: (jnp.minimum((i + 1) * r, last), col(j))))
    return (pl.BlockSpec((halo, tc), lambda j, i: (jnp.maximum(i * r - 1, 0), col(j))),
            pl.BlockSpec((tm, tc), lambda j, i: (i, col(j))),
            pl.BlockSpec((halo, tc), lambda j, i: (jnp.minimum((i + 1) * r, last), col(j))))


def conv_a_fwd(proj_a, conv_w, name):
    T, D3 = proj_a.shape
    D = D3 // 3
    tm = _pick(T, 256, 8)

    def body(p_ref, m_ref, n_ref, w_ref, o_ref):
        i = pl.program_id(0)
        ext = _ext_rows(p_ref, m_ref, n_ref, i, tm, T)
        u = ext[:, D:2 * D] * ext[:, 2 * D:]
        cu = _conv3(u, w_ref)
        o_ref[...] = (m_ref[:, :D].astype(F32) * _main(cu, tm, HALO_BF16)).astype(BF16)

    prev, main, nxt = _halo_specs(tm, D3, T, lambda j: 0, "ij", HALO_BF16)
    return pl.pallas_call(
        body, name=name, grid=(T // tm, 1),
        in_specs=[prev, main, nxt, pl.BlockSpec((3, D), lambda i, j: (0, 0))],
        out_specs=pl.BlockSpec((tm, D), lambda i, j: (i, 0)),
        out_shape=jax.ShapeDtypeStruct((T, D), BF16),
        compiler_params=_params(("parallel", "arbitrary")),
    )(proj_a, proj_a, proj_a, conv_w)


def conv_a_bwd(ds_a, proj_a, conv_w, name):
    T, D3 = proj_a.shape
    D = D3 // 3
    tm = _pick(T, 256, 8)

    def body(dp_ref, dm_ref, dn_ref, p_ref, m_ref, n_ref, w_ref, o_ref, dw_ref):
        i = pl.program_id(0)

        @pl.when(i == 0)
        def _():
            dw_ref[...] = jnp.zeros_like(dw_ref)

        ext = _ext_rows(p_ref, m_ref, n_ref, i, tm, T)
        dsa = _ext_rows(dp_ref, dm_ref, dn_ref, i, tm, T)
        gb, gc, hin = ext[:, :D], ext[:, D:2 * D], ext[:, 2 * D:]
        u = gc * hin
        u_prev, u_next = _prev_row(u), _next_row(u)
        cu = u_prev * w_ref[0:1, :] + u * w_ref[1:2, :] + u_next * w_ref[2:3, :]
        dcu = dsa * gb
        du = _next_row(dcu) * w_ref[0:1, :] + dcu * w_ref[1:2, :] + _prev_row(dcu) * w_ref[2:3, :]
        h = HALO_BF16
        o_ref[:, :D] = _main(dsa * cu, tm, h).astype(BF16)
        o_ref[:, D:2 * D] = _main(du * hin, tm, h).astype(BF16)
        o_ref[:, 2 * D:] = _main(du * gc, tm, h).astype(BF16)
        dcu_m = _main(dcu, tm, h)
        dw_ref[0:1, :] += jnp.sum(dcu_m * _main(u_prev, tm, h), 0, keepdims=True)
        dw_ref[1:2, :] += jnp.sum(dcu_m * _main(u, tm, h), 0, keepdims=True)
        dw_ref[2:3, :] += jnp.sum(dcu_m * _main(u_next, tm, h), 0, keepdims=True)

    dprev, dmain, dnxt = _halo_specs(tm, D, T, lambda j: 0, "ij", HALO_BF16)
    prev, main, nxt = _halo_specs(tm, D3, T, lambda j: 0, "ij", HALO_BF16)
    return pl.pallas_call(
        body, name=name, grid=(T // tm, 1),
        in_specs=[dprev, dmain, dnxt, prev, main, nxt, pl.BlockSpec((3, D), lambda i, j: (0, 0))],
        out_specs=[pl.BlockSpec((tm, D3), lambda i, j: (i, 0)), pl.BlockSpec((3, D), lambda i, j: (0, 0))],
        out_shape=[jax.ShapeDtypeStruct((T, D3), BF16), jax.ShapeDtypeStruct((3, D), F32)],
        compiler_params=_params(("arbitrary", "arbitrary")),
    )(ds_a, ds_a, ds_a, proj_a, proj_a, proj_a, conv_w)


_INV_SQRT2 = 1.0 / math.sqrt(2.0)
_INV_SQRT_2PI = 1.0 / math.sqrt(2.0 * math.pi)


def conv_f_fwd(up, fcw, fcb, name):
    T, F2 = up.shape
    F = F2 // 2
    tm = _pick(T, 256, 8)
    tc = _pick(F, 1408)
    nc = F // tc

    def body(p_ref, m_ref, n_ref, g_ref, w_ref, b_ref, o_ref):
        i = pl.program_id(0)
        a = _ext_rows(p_ref, m_ref, n_ref, i, tm, T)
        ca = _main(_conv3(a, w_ref), tm) + b_ref[...]
        gl = 0.5 * ca * (1.0 + lax.erf(ca * _INV_SQRT2))
        o_ref[...] = (gl * g_ref[...]).astype(BF16)

    prev, main, nxt = _halo_specs(tm, tc, T, lambda j: j, "ij")
    return pl.pallas_call(
        body, name=name, grid=(T // tm, nc),
        in_specs=[prev, main, nxt,
                  pl.BlockSpec((tm, tc), lambda i, j: (i, nc + j)),
                  pl.BlockSpec((3, tc), lambda i, j: (0, j)),
                  pl.BlockSpec((1, tc), lambda i, j: (0, j))],
        out_specs=pl.BlockSpec((tm, tc), lambda i, j: (i, j)),
        out_shape=jax.ShapeDtypeStruct((T, F), BF16),
        compiler_params=_params(("parallel", "parallel")),
    )(up, up, up, up, fcw, fcb)


def ffn_up_conv_f(h, w_up, b_up, fcw, fcb, name):
    T, D = h.shape
    F = fcb.shape[1]
    tm = _pick(T, 256, 8)
    tc = _pick(F, 256)
    halo = HALO_BF16

    def body(hp_ref, hm_ref, hn_ref, w_ref, b_ref, cw_ref, cb_ref, up_ref, f_ref):
        i = pl.program_id(0)
        h_ext = _ext_rows(hp_ref, hm_ref, hn_ref, i, tm, T, dtype=BF16)
        h_main = hm_ref[...]
        rows = i * tm - halo + lax.broadcasted_iota(jnp.int32, (tm + 2 * halo, 1), 0)
        inside = (rows >= 0) & (rows < T)
        for c in range(F // tc):
            cols = slice(c * tc, (c + 1) * tc)
            gcols = slice(F + c * tc, F + (c + 1) * tc)
            a_ext = jnp.dot(h_ext, w_ref[:, cols], preferred_element_type=F32) + b_ref[:, cols]
            a_ext = jnp.where(inside, a_ext, 0.0)
            gate = jnp.dot(h_main, w_ref[:, gcols], preferred_element_type=F32) + b_ref[:, gcols]
            up_ref[:, cols] = _main(a_ext, tm, halo)
            up_ref[:, gcols] = gate
            ca = _main(_prev_row(a_ext) * cw_ref[0:1, cols] + a_ext * cw_ref[1:2, cols]
                       + _next_row(a_ext) * cw_ref[2:3, cols], tm, halo) + cb_ref[:, cols]
            gl = 0.5 * ca * (1.0 + lax.erf(ca * _INV_SQRT2))
            f_ref[:, cols] = (gl * gate).astype(BF16)

    prev, main, nxt = _halo_specs(tm, D, T, lambda j: 0, "ij", halo)
    whole = lambda x: pl.BlockSpec(x.shape, lambda i, j: (0, 0))
    return pl.pallas_call(
        body, name=name, grid=(T // tm, 1),
        in_specs=[prev, main, nxt, whole(w_up), whole(b_up), whole(fcw), whole(fcb)],
        out_specs=[pl.BlockSpec((tm, 2 * F), lambda i, j: (i, 0)), pl.BlockSpec((tm, F), lambda i, j: (i, 0))],
        out_shape=[jax.ShapeDtypeStruct((T, 2 * F), F32), jax.ShapeDtypeStruct((T, F), BF16)],
        compiler_params=_params(("parallel", "arbitrary")),
    )(h, h, h, w_up, b_up, fcw, fcb)


def conv_f_bwd(dy, w_down, up, fcw, fcb, name):
    T, F2 = up.shape
    F = F2 // 2
    D = dy.shape[1]
    tm = _pick(T, 256, 8)
    tc = _pick(F, 256)

    def body(yp_ref, ym_ref, yn_ref, wd_ref, up_ref, um_ref, un_ref, w_ref, b_ref,
             da_ref, dg_ref, csa_ref, csg_ref, dfb_ref, dfw_ref):
        i = pl.program_id(0)
        first, last = i == 0, i == T // tm - 1

        @pl.when(first)
        def _():
            csa_ref[...] = jnp.zeros_like(csa_ref)
            csg_ref[...] = jnp.zeros_like(csg_ref)
            dfb_ref[...] = jnp.zeros_like(dfb_ref)
            dfw_ref[...] = jnp.zeros_like(dfw_ref)

        def ext(cols):
            return jnp.concatenate([jnp.where(first, 0.0, up_ref[:, cols]), um_ref[:, cols],
                                    jnp.where(last, 0.0, un_ref[:, cols])], axis=0)

        dy_ext = _ext_rows(yp_ref, ym_ref, yn_ref, i, tm, T, dtype=BF16)
        for c in range(F // tc):
            cols = slice(c * tc, (c + 1) * tc)
            dfe = lax.dot_general(dy_ext, wd_ref[cols, :], NT_DIMS, preferred_element_type=F32)
            dfe = dfe[HALO_BF16 - HALO:HALO_BF16 + tm + HALO]
            a = ext(cols)
            gate = ext(slice(F + c * tc, F + (c + 1) * tc))
            a_prev, a_next = _prev_row(a), _next_row(a)
            ca = a_prev * w_ref[0:1, cols] + a * w_ref[1:2, cols] + a_next * w_ref[2:3, cols] + b_ref[:, cols]
            cdf = 0.5 * (1.0 + lax.erf(ca * _INV_SQRT2))
            gl = ca * cdf
            gp = cdf + ca * (jnp.exp(-0.5 * ca * ca) * _INV_SQRT_2PI)
            dgate = _main(dfe * gl, tm)
            dca = dfe * gate * gp
            da = _main(_next_row(dca) * w_ref[0:1, cols] + dca * w_ref[1:2, cols] + _prev_row(dca) * w_ref[2:3, cols],
                       tm)
            da_ref[:, cols] = da.astype(BF16)
            dg_ref[:, cols] = dgate.astype(BF16)
            csa_ref[:, cols] += jnp.sum(da, 0, keepdims=True)
            csg_ref[:, cols] += jnp.sum(dgate, 0, keepdims=True)
            dca_m = _main(dca, tm)
            dfb_ref[:, cols] += jnp.sum(dca_m, 0, keepdims=True)
            dfw_ref[0:1, cols] += jnp.sum(dca_m * _main(a_prev, tm), 0, keepdims=True)
            dfw_ref[1:2, cols] += jnp.sum(dca_m * _main(a, tm), 0, keepdims=True)
            dfw_ref[2:3, cols] += jnp.sum(dca_m * _main(a_next, tm), 0, keepdims=True)

    uprev, umain, unxt = _halo_specs(tm, F2, T, lambda j: 0, "ij")
    yprev, ymain, ynxt = _halo_specs(tm, D, T, lambda j: 0, "ij", HALO_BF16)
    whole = lambda shape: pl.BlockSpec(shape, lambda i, j: (0, 0))
    tile = pl.BlockSpec((tm, F), lambda i, j: (i, 0))
    return pl.pallas_call(
        body, name=name, grid=(T // tm, 1),
        in_specs=[yprev, ymain, ynxt, whole((F, D)), uprev, umain, unxt, whole((3, F)), whole((1, F))],
        out_specs=[tile, tile, whole((1, F)), whole((1, F)), whole((1, F)), whole((3, F))],
        out_shape=[jax.ShapeDtypeStruct((T, F), BF16), jax.ShapeDtypeStruct((T, F), BF16),
                   jax.ShapeDtypeStruct((1, F), F32), jax.ShapeDtypeStruct((1, F), F32),
                   jax.ShapeDtypeStruct((1, F), F32), jax.ShapeDtypeStruct((3, F), F32)],
        compiler_params=_params(("arbitrary", "arbitrary")),
    )(dy, dy, dy, w_down, up, up, up, fcw, fcb)


def gate_fwd(proj_g, y_a, y_b, name):
    T, D = y_a.shape
    tm = _pick(T, 512, 8)

    def body(g_ref, a_ref, b_ref, o_ref):
        sa = jax.nn.sigmoid(g_ref[:, :D].astype(F32))
        sb = jax.nn.sigmoid(g_ref[:, D:].astype(F32))
        o_ref[...] = (sa * a_ref[...].astype(F32) + sb * b_ref[...].astype(F32)).astype(BF16)

    row = pl.BlockSpec((tm, D), lambda i: (i, 0))
    return pl.pallas_call(
        body, name=name, grid=(T // tm,),
        in_specs=[pl.BlockSpec((tm, 2 * D), lambda i: (i, 0)), row, row],
        out_specs=row,
        out_shape=jax.ShapeDtypeStruct((T, D), BF16),
        compiler_params=_params(("parallel",)),
    )(proj_g, y_a, y_b)


def gate_bwd(dz, proj_g, y_a, y_b, name):
    T, D = y_a.shape
    tm = _pick(T, 512, 8)

    def body(dz_ref, g_ref, a_ref, b_ref, da_ref, db_ref, dg_ref):
        dzv = dz_ref[...].astype(F32)
        sa = jax.nn.sigmoid(g_ref[:, :D].astype(F32))
        sb = jax.nn.sigmoid(g_ref[:, D:].astype(F32))
        da_ref[...] = (dzv * sa).astype(BF16)
        db_ref[...] = (dzv * sb).astype(BF16)
        dg_ref[:, :D] = (dzv * a_ref[...].astype(F32) * (sa * (1.0 - sa))).astype(BF16)
        dg_ref[:, D:] = (dzv * b_ref[...].astype(F32) * (sb * (1.0 - sb))).astype(BF16)

    row = pl.BlockSpec((tm, D), lambda i: (i, 0))
    wide = pl.BlockSpec((tm, 2 * D), lambda i: (i, 0))
    return pl.pallas_call(
        body, name=name, grid=(T // tm,),
        in_specs=[row, wide, row, row],
        out_specs=[row, row, wide],
        out_shape=[jax.ShapeDtypeStruct((T, D), BF16), jax.ShapeDtypeStruct((T, D), BF16),
                   jax.ShapeDtypeStruct((T, 2 * D), BF16)],
        compiler_params=_params(("parallel",)),
    )(dz, proj_g, y_a, y_b)


ATT_WIN = ATT_TQ + 2 * RADIUS
ATT_STEP = 1024
FAR = 1e32


def _att_window(qs, L):
    ks = pl.multiple_of(jnp.clip(qs - RADIUS, 0, L - ATT_WIN), RADIUS)
    return ks, jnp.where(qs == 0, 0, jnp.where(qs == L - ATT_TQ, 2, 1))


def _fill_bias_tables(bias_ref, sl_ref, hp, d):
    col_row = (lax.broadcasted_iota(jnp.int32, (ATT_TQ, ATT_WIN), 1)
               - lax.broadcasted_iota(jnp.int32, (ATT_TQ, ATT_WIN), 0))
    for v in range(3):
        ad = jnp.abs(col_row - v * RADIUS)
        dist = jnp.where(ad <= RADIUS, (ad * d).astype(F32), FAR)
        bias_ref[v, 0:ATT_TQ, :] = sl_ref[hp * 2] * dist
        bias_ref[v, ATT_TQ:2 * ATT_TQ, :] = sl_ref[hp * 2 + 1] * dist


def _head_masks():
    lane = lax.broadcasted_iota(jnp.int32, (1, LANES), 1)
    return [lane < HEAD_DIM, lane >= HEAD_DIM]


def _stack_heads(x, masks):
    zero = jnp.zeros_like(x)
    return jnp.concatenate([jnp.where(masks[0], x, zero), jnp.where(masks[1], x, zero)], axis=0)


def _unstack_heads(x2, masks):
    n = x2.shape[0] // 2
    return jnp.where(masks[0], x2[:n], x2[n:])


def _att_step(L):
    step = min(ATT_STEP, L)
    assert L % step == 0 and step % ATT_TQ == 0 and L >= ATT_WIN
    return step


def _residues_per_step(d, L):
    rps = max(1, min(d, ATT_STEP // L))
    assert d % rps == 0
    return rps


def att_fwd(qkv, group, name):
    d, L, _ = qkv.shape
    step = _att_step(L)
    rps = _residues_per_step(d, L)
    cg = GROUP_W // LANES
    slopes = jnp.asarray(_alibi_slopes()[group])
    scale = HEAD_DIM ** -0.5

    def body(sl_ref, q_ref, k_ref, v_ref, o_ref, l_ref, bias_ref, s_ref, p_ref):
        hp = pl.program_id(1)
        i = pl.program_id(2)

        @pl.when(i == 0)
        def _():
            _fill_bias_tables(bias_ref, sl_ref, hp, d)

        masks = _head_masks()
        per = step // ATT_TQ
        tiles = [(rr, t) for rr in range(rps) for t in range(per)]
        windows = [_att_window(i * step + t * ATT_TQ, L) for t in range(per)]
        for n, (rr, t) in enumerate(tiles):
            rows = slice(t * ATT_TQ, (t + 1) * ATT_TQ)
            ks, table = windows[t]
            q2 = _stack_heads(q_ref[rr, rows, :] * scale, masks)
            kw = k_ref[rr, pl.ds(ks, ATT_WIN), :]
            s_ref[n] = lax.dot_general(q2, kw, NT_DIMS, preferred_element_type=F32) - bias_ref[table]
        for n, (rr, t) in enumerate(tiles):
            rows = slice(t * ATT_TQ, (t + 1) * ATT_TQ)
            s = s_ref[n]
            m = jnp.max(s, -1, keepdims=True)
            p = jnp.exp(s - m)
            den = jnp.sum(p, -1, keepdims=True)
            p_ref[n] = (p / den).astype(BF16)
            l_ref[rr, rows, :] = _unstack_heads(m + jnp.log(den), masks)
        for n, (rr, t) in enumerate(tiles):
            rows = slice(t * ATT_TQ, (t + 1) * ATT_TQ)
            vw = v_ref[rr, pl.ds(windows[t][0], ATT_WIN), :]
            o2 = jnp.dot(p_ref[n], vw, preferred_element_type=F32)
            o_ref[rr, rows, :] = _unstack_heads(o2, masks)

    n_tiles = rps * step // ATT_TQ
    out_spec = pl.BlockSpec((rps, step, LANES), lambda r, hp, i: (r, i, hp))
    return pl.pallas_call(
        body, name=name, grid=(d // rps, cg, L // step),
        in_specs=[pl.BlockSpec(memory_space=pltpu.SMEM),
                  pl.BlockSpec((rps, step, LANES), lambda r, hp, i: (r, i, hp)),
                  pl.BlockSpec((rps, L, LANES), lambda r, hp, i: (r, 0, cg + hp)),
                  pl.BlockSpec((rps, L, LANES), lambda r, hp, i: (r, 0, 2 * cg + hp))],
        out_specs=[out_spec, out_spec],
        out_shape=[jax.ShapeDtypeStruct((d, L, GROUP_W), F32)] * 2,
        scratch_shapes=[pltpu.VMEM((3, 2 * ATT_TQ, ATT_WIN), F32),
                        pltpu.VMEM((n_tiles, 2 * ATT_TQ, ATT_WIN), F32),
                        pltpu.VMEM((n_tiles, 2 * ATT_TQ, ATT_WIN), BF16)],
        compiler_params=_params(("arbitrary", "arbitrary", "arbitrary")),
    )(slopes, qkv, qkv, qkv)


def att_bwd(qkv, do, lse, dmat, group, name):
    d, L, _ = qkv.shape
    step = _att_step(L)
    rps = _residues_per_step(d, L)
    nq = L // step
    cg = GROUP_W // LANES
    slopes = jnp.asarray(_alibi_slopes()[group])
    scale = HEAD_DIM ** -0.5

    def body(sl_ref, q_ref, k_ref, v_ref, do_ref, l_ref, dm_ref, dq_ref, dk_ref, dv_ref, dk_acc, dv_acc, bias_ref,
             s_ref, dp_ref, p_ref, ds_ref):
        hp = pl.program_id(1)
        i = pl.program_id(2)

        @pl.when(i == 0)
        def _():
            dk_acc[...] = jnp.zeros_like(dk_acc)
            dv_acc[...] = jnp.zeros_like(dv_acc)
            _fill_bias_tables(bias_ref, sl_ref, hp, d)

        masks = _head_masks()

        def head_cols(x):
            return jnp.concatenate([jnp.max(jnp.where(hm, x, -jnp.inf), -1, keepdims=True) for hm in masks], axis=0)

        per = step // ATT_TQ
        tiles = [(rr, t) for rr in range(rps) for t in range(per)]
        windows = [_att_window(i * step + t * ATT_TQ, L) for t in range(per)]

        def stacked(ref, rr, t, factor=None):
            x = ref[rr, t * ATT_TQ:(t + 1) * ATT_TQ, :]
            return _stack_heads(x if factor is None else x * factor, masks)

        for n, (rr, t) in enumerate(tiles):
            ks, table = windows[t]
            q2 = stacked(q_ref, rr, t, scale)
            s_ref[n] = lax.dot_general(q2, k_ref[rr, pl.ds(ks, ATT_WIN), :], NT_DIMS,
                                       preferred_element_type=F32) - bias_ref[table]
            dp_ref[n] = lax.dot_general(stacked(do_ref, rr, t), v_ref[rr, pl.ds(ks, ATT_WIN), :], NT_DIMS,
                                        preferred_element_type=F32)
        for n, (rr, t) in enumerate(tiles):
            rows = slice(t * ATT_TQ, (t + 1) * ATT_TQ)
            p = jnp.exp(s_ref[n] - head_cols(l_ref[rr, rows, :]))
            p_ref[n] = p.astype(BF16)
            ds_ref[n] = (p * (dp_ref[n] - head_cols(dm_ref[rr, rows, :]))).astype(BF16)
        for n, (rr, t) in enumerate(tiles):
            rows = slice(t * ATT_TQ, (t + 1) * ATT_TQ)
            ks = windows[t][0]
            ds = ds_ref[n]
            dq2 = jnp.dot(ds, k_ref[rr, pl.ds(ks, ATT_WIN), :], preferred_element_type=F32)
            dq_ref[rr, rows, :] = (_unstack_heads(dq2, masks) * scale).astype(BF16)
            dk_acc[rr, pl.ds(ks, ATT_WIN), :] += lax.dot_general(ds, stacked(q_ref, rr, t, scale), TN_DIMS,
                                                                 preferred_element_type=F32)
            dv_acc[rr, pl.ds(ks, ATT_WIN), :] += lax.dot_general(p_ref[n], stacked(do_ref, rr, t), TN_DIMS,
                                                                 preferred_element_type=F32)

        @pl.when(i == nq - 1)
        def _():
            dk_ref[...] = dk_acc[...].astype(BF16)
            dv_ref[...] = dv_acc[...].astype(BF16)

    tile = pl.BlockSpec((rps, step, LANES), lambda r, hp, i: (r, i, hp))
    whole = pl.BlockSpec((rps, L, LANES), lambda r, hp, i: (r, 0, hp))
    return pl.pallas_call(
        body, name=name, grid=(d // rps, cg, nq),
        in_specs=[pl.BlockSpec(memory_space=pltpu.SMEM), tile,
                  pl.BlockSpec((rps, L, LANES), lambda r, hp, i: (r, 0, cg + hp)),
                  pl.BlockSpec((rps, L, LANES), lambda r, hp, i: (r, 0, 2 * cg + hp)),
                  tile, tile, tile],
        out_specs=[tile, whole, whole],
        out_shape=[jax.ShapeDtypeStruct((d, L, GROUP_W), BF16)] * 3,
        scratch_shapes=[pltpu.VMEM((rps, L, LANES), F32), pltpu.VMEM((rps, L, LANES), F32),
                        pltpu.VMEM((3, 2 * ATT_TQ, ATT_WIN), F32)]
        + [pltpu.VMEM((rps * step // ATT_TQ, 2 * ATT_TQ, ATT_WIN), dt) for dt in (F32, F32, BF16, BF16)],
        compiler_params=_params(("arbitrary", "arbitrary", "arbitrary")),
    )(slopes, qkv, qkv, qkv, do, lse, dmat)


def _group_weights(ls):
    m = jnp.maximum(jnp.maximum(ls[0], ls[1]), ls[2])
    es = [jnp.exp(l - m) for l in ls]
    tot = es[0] + es[1] + es[2]
    return [e / tot for e in es]


def combine_fwd(outs, lses, name):
    T = outs[0].shape[0] * outs[0].shape[1]
    tm = _pick(T, 512, 8)
    n_scr = 2 * (len(DILATIONS) - 1)

    def body(*refs):
        o_refs, l_refs, c_ref, scr = refs[:3], refs[3:6], refs[6], refs[7:]
        o = [_load_natural(o_refs[g], d, scr[g - 1] if g else None) for g, d in enumerate(DILATIONS)]
        l = [_load_natural(l_refs[g], d, scr[g + 1] if g else None) for g, d in enumerate(DILATIONS)]
        w = _group_weights(l)
        c_ref[...] = (w[0] * o[0] + w[1] * o[1] + w[2] * o[2]).astype(BF16)

    specs = [_residue_spec(tm, d, GROUP_W) for d in DILATIONS]
    return pl.pallas_call(
        body, name=name, grid=(T // tm,),
        in_specs=specs + specs, out_specs=pl.BlockSpec((tm, GROUP_W), lambda i: (i, 0)),
        out_shape=jax.ShapeDtypeStruct((T, GROUP_W), BF16),
        scratch_shapes=[_residue_scratch(tm, GROUP_W)] * n_scr,
        compiler_params=_params(("parallel",)),
    )(*outs, *lses)


def combine_bwd(dcomb, outs, lses, name):
    T = dcomb.shape[0]
    tm = _pick(T, 256, 8)
    head = np.arange(GROUP_W) // HEAD_DIM
    seg = jnp.asarray((head[:, None] == head[None, :]).astype(np.float32)).astype(BF16)
    ng = len(DILATIONS)
    n_scr = 4 * (ng - 1)

    def body(*refs):
        dc_ref, o_refs, l_refs, e_ref = refs[0], refs[1:1 + ng], refs[1 + ng:1 + 2 * ng], refs[1 + 2 * ng]
        do_refs, dm_refs = refs[2 + 2 * ng:2 + 3 * ng], refs[2 + 3 * ng:2 + 4 * ng]
        scr = refs[2 + 4 * ng:]
        o = [_load_natural(o_refs[g], d, scr[4 * (g - 1)] if g else None) for g, d in enumerate(DILATIONS)]
        l = [_load_natural(l_refs[g], d, scr[4 * (g - 1) + 1] if g else None) for g, d in enumerate(DILATIONS)]
        w = _group_weights(l)
        dc = dc_ref[...].astype(F32)
        e = e_ref[...]
        prod = dc * (w[0] * o[0] + w[1] * o[1] + w[2] * o[2])
        tot = jnp.zeros_like(dc)
        for _ in range(3):
            part = prod.astype(BF16)
            tot = tot + jnp.dot(part, e, preferred_element_type=F32)
            prod = prod - part.astype(F32)
        for g, d in enumerate(DILATIONS):
            _store_by_residue(w[g] * dc, do_refs[g], d, scr[4 * (g - 1) + 2] if g else None)
            _store_by_residue(w[g] * tot, dm_refs[g], d, scr[4 * (g - 1) + 3] if g else None)

    specs = [_residue_spec(tm, d, GROUP_W) for d in DILATIONS]
    res = pl.pallas_call(
        body, name=name, grid=(T // tm,),
        in_specs=[pl.BlockSpec((tm, GROUP_W), lambda i: (i, 0))] + specs + specs
        + [pl.BlockSpec((GROUP_W, GROUP_W), lambda i: (0, 0))],
        out_specs=specs + specs,
        out_shape=[jax.ShapeDtypeStruct(o.shape, BF16) for o in outs] + [jax.ShapeDtypeStruct(o.shape, F32) for o in outs],
        scratch_shapes=[_residue_scratch(tm, GROUP_W)] * n_scr,
        compiler_params=_params(("parallel",)),
    )(dcomb, *outs, *lses, seg)
    return res[:ng], res[ng:]


def _position():
    return lax.axis_index("x"), lax.axis_index("y"), lax.axis_index("c")


def _other_chips(x, y):
    return [(1 - x, y), (x, 1 - y), (1 - x, 1 - y)]


def _remote(src, dst, send_sems, recv_sems, k, to):
    return pltpu.make_async_remote_copy(src_ref=src, dst_ref=dst, send_sem=send_sems.at[k], recv_sem=recv_sems.at[k],
                                        device_id=to, device_id_type=MESH)


def _gather_descriptors(ins, outs, send_sems, recv_sems, local_sems):
    n = len(ins)
    x, y, c = _position()
    sibling = (x, y, 1 - c)
    chips = _other_chips(x, y)

    def block(a, px, py, pc):
        return outs[a].at[4 * px + 2 * py + pc]

    own, first, arrivals = [], [], []
    for a in range(n):
        k0 = 7 * a
        mine = block(a, x, y, c)
        own.append(pltpu.make_async_copy(ins[a], mine, local_sems.at[a]))
        first.append(_remote(ins[a], mine, send_sems, recv_sems, k0, sibling))
        row = []
        for j, chip in enumerate(chips):
            first.append(_remote(ins[a], mine, send_sems, recv_sems, k0 + 1 + j, (*chip, c)))
            got = block(a, *chip, c)
            row.append((_remote(got, got, send_sems, recv_sems, k0 + 1 + j, sibling),
                        _remote(got, got, send_sems, recv_sems, k0 + 4 + j, sibling)))
        arrivals.append(row)
    return own, first, arrivals


def _gather_begin(ins, outs, send_sems, recv_sems, local_sems):
    own, first, _ = _gather_descriptors(ins, outs, send_sems, recv_sems, local_sems)
    for cp in own + first:
        cp.start()


def _gather_finish(ins, outs, send_sems, recv_sems, local_sems):
    own, first, arrivals = _gather_descriptors(ins, outs, send_sems, recv_sems, local_sems)
    passed = []
    for row in arrivals:
        for arrived, onward in row:
            arrived.wait_recv()
            onward.start()
            passed.append(onward)
    for a in range(len(ins)):
        first[4 * a].wait_recv()
        for _, onward in arrivals[a]:
            onward.wait_recv()
    for cp in first + passed:
        cp.wait_send()
    for cp in own:
        cp.wait()


def _gather_scratch(n):
    return [pltpu.SemaphoreType.DMA((7 * n,)), pltpu.SemaphoreType.DMA((7 * n,)), pltpu.SemaphoreType.DMA((n,))]


def all_gather(shards, name):
    n = len(shards)

    def body(*refs):
        ins, outs, sems = refs[:n], refs[n:2 * n], refs[2 * n:]
        _gather_begin(ins, outs, *sems)
        _gather_finish(ins, outs, *sems)

    hbm = pl.BlockSpec(memory_space=pl.ANY)
    return pl.pallas_call(
        body, name=name,
        in_specs=[hbm] * n, out_specs=[hbm] * n,
        out_shape=[jax.ShapeDtypeStruct((N_DEV,) + s.shape, s.dtype) for s in shards],
        scratch_shapes=_gather_scratch(n),
    )(*shards)


def exchange_sibling(parts, name):
    n = len(parts)

    def body(*refs):
        ins, outs = refs[:n], refs[n:2 * n]
        send_sems, recv_sems = refs[2 * n:]
        x, y, c = _position()
        sibling = (x, y, 1 - c)
        copies = []
        for a in range(n):
            for q in range(4):
                cp = _remote(ins[a].at[2 * q + (1 - c)], outs[a].at[q], send_sems, recv_sems, 4 * a + q, sibling)
                cp.start()
                copies.append(cp)
        for cp in copies:
            cp.wait_recv()
        for cp in copies:
            cp.wait_send()

    hbm = pl.BlockSpec(memory_space=pl.ANY)
    return pl.pallas_call(
        body, name=name,
        in_specs=[hbm] * n, out_specs=[hbm] * n,
        out_shape=[jax.ShapeDtypeStruct((4,) + p.shape[1:], p.dtype) for p in parts],
        scratch_shapes=[pltpu.SemaphoreType.DMA((4 * n,)), pltpu.SemaphoreType.DMA((4 * n,))],
    )(*parts)


def exchange_chips(sums, name):
    n = len(sums)

    def body(*refs):
        ins, outs = refs[:n], refs[n:2 * n]
        send_sems, recv_sems = refs[2 * n:]
        x, y, c = _position()
        copies = []
        for a in range(n):
            for j, (cx, cy) in enumerate(_other_chips(x, y)):
                cp = _remote(ins[a].at[2 * cx + cy], outs[a].at[j], send_sems, recv_sems, 3 * a + j, (cx, cy, c))
                cp.start()
                copies.append(cp)
        for cp in copies:
            cp.wait_recv()
        for cp in copies:
            cp.wait_send()

    hbm = pl.BlockSpec(memory_space=pl.ANY)
    return pl.pallas_call(
        body, name=name,
        in_specs=[hbm] * n, out_specs=[hbm] * n,
        out_shape=[jax.ShapeDtypeStruct((3,) + s.shape[1:], s.dtype) for s in sums],
        scratch_shapes=[pltpu.SemaphoreType.DMA((3 * n,)), pltpu.SemaphoreType.DMA((3 * n,))],
    )(*sums)


_HBM = pl.BlockSpec(memory_space=pltpu.HBM)
_SEM = pl.BlockSpec(memory_space=pltpu.SEMAPHORE)
_DATAFLOW = pltpu.SideEffectType.DATAFLOW_SIDE_EFFECTING


def _to_all_plan(srcs, lands, send_sems, recv_sems):
    x, y, c = _position()
    me = 4 * x + 2 * y + c
    copies = []
    for a in range(len(srcs)):
        for k in range(1, N_DEV):
            fx, fy, fc = (k >> 2) & 1, (k >> 1) & 1, k & 1
            to = (1 - x if fx else x, 1 - y if fy else y, 1 - c if fc else c)
            copies.append(_remote(srcs[a], lands[a].at[me], send_sems, recv_sems, (N_DEV - 1) * a + k - 1, to))
    return copies


def _to_chips_plan(srcs, lands, send_sems, recv_sems):
    x, y, c = _position()
    copies = []
    for a in range(len(srcs)):
        for j, (cx, cy) in enumerate(_other_chips(x, y)):
            copies.append(_remote(srcs[a].at[2 * cx + cy], lands[a].at[j], send_sems, recv_sems, 3 * a + j, (cx, cy, c)))
    return copies


def copies_start(srcs, land_shapes, plan, per_array, name):
    n = len(srcs)
    n_sem = per_array * n
    lands = [lax.empty(s.shape, s.dtype) for s in land_shapes]

    def body(*refs):
        src_refs, land_refs = refs[:n], refs[n:2 * n]
        send_sems, recv_sems = refs[2 * n], refs[2 * n + 1]
        token = refs[-1]
        for cp in plan(src_refs, land_refs, send_sems, recv_sems):
            cp.start()
        token[...] = jnp.zeros_like(token)

    out = pl.pallas_call(
        body, name=name,
        out_shape=(pltpu.SemaphoreType.DMA((n_sem,)), pltpu.SemaphoreType.DMA((n_sem,)))
        + tuple(pltpu.HBM(s.shape, s.dtype) for s in srcs)
        + tuple(pltpu.HBM(s.shape, s.dtype) for s in land_shapes)
        + (jax.ShapeDtypeStruct((8, LANES), F32),),
        in_specs=[_HBM] * (2 * n),
        out_specs=(_SEM, _SEM) + (_HBM,) * (2 * n) + (pl.BlockSpec(memory_space=pltpu.VMEM),),
        input_output_aliases={i: 2 + i for i in range(2 * n)},
        compiler_params=pltpu.CompilerParams(has_side_effects=_DATAFLOW),
    )(*[pltpu.with_memory_space_constraint(s, pltpu.HBM) for s in srcs],
      *[pltpu.with_memory_space_constraint(l, pltpu.HBM) for l in lands])
    return out[:-1], out[-1]


def copies_wait(handles, plan, after, name):
    send_sems, recv_sems = handles[0], handles[1]
    n = (len(handles) - 2) // 2
    thru = handles[2:]

    def body(*refs):
        src_refs, land_refs = refs[:n], refs[n:2 * n]
        send_sems, recv_sems = refs[2 * n], refs[2 * n + 1]
        copies = plan(src_refs, land_refs, send_sems, recv_sems)
        for cp in copies:
            cp.wait_recv()
        for cp in copies:
            cp.wait_send()

    out = pl.pallas_call(
        body, name=name,
        out_shape=tuple(pltpu.HBM(t.shape, t.dtype) for t in thru),
        in_specs=[_HBM] * (2 * n) + [_SEM, _SEM, pl.BlockSpec(memory_space=pl.ANY)],
        out_specs=(_HBM,) * (2 * n),
        input_output_aliases={i: i for i in range(2 * n)},
        compiler_params=pltpu.CompilerParams(has_side_effects=_DATAFLOW),
    )(*thru, send_sems, recv_sems, after)
    return out[n:]


def all_sum_small(vec, name):
    R = vec.shape[0]

    def body(v_ref, tot_ref, all_ref, send_sems, recv_sems):
        x, y, c = _position()
        me = 4 * x + 2 * y + c
        all_ref[me] = v_ref[...]
        copies = []
        for k in range(1, N_DEV):
            fx, fy, fc = (k >> 2) & 1, (k >> 1) & 1, k & 1
            to = (1 - x if fx else x, 1 - y if fy else y, 1 - c if fc else c)
            cp = _remote(v_ref, all_ref.at[me], send_sems, recv_sems, k - 1, to)
            cp.start()
            copies.append(cp)
        for cp in copies:
            cp.wait_recv()
        for cp in copies:
            cp.wait_send()
        tot = all_ref[0]
        for j in range(1, N_DEV):
            tot = tot + all_ref[j]
        tot_ref[...] = tot

    vmem = pl.BlockSpec(memory_space=pltpu.VMEM)
    return pl.pallas_call(
        body, name=name,
        in_specs=[vmem], out_specs=vmem,
        out_shape=jax.ShapeDtypeStruct((R, LANES), F32),
        scratch_shapes=[pltpu.VMEM((N_DEV, R, LANES), F32),
                        pltpu.SemaphoreType.DMA((N_DEV - 1,)), pltpu.SemaphoreType.DMA((N_DEV - 1,))],
        compiler_params=pltpu.CompilerParams(vmem_limit_bytes=VMEM_LIMIT),
    )(vec)


def pair_add(parts, theirs, place, name):
    _, R, C = theirs.shape
    tr = _pick(R, 256, 8)

    def body(place_ref, a_ref, b_ref, o_ref):
        o_ref[...] = (a_ref[...].astype(F32) + b_ref[...].astype(F32)).astype(BF16)

    blk = pl.BlockSpec((None, tr, C), lambda q, i, place_ref: (q, i, 0))
    return pl.pallas_call(
        body, name=name,
        grid_spec=pltpu.PrefetchScalarGridSpec(
            num_scalar_prefetch=1, grid=(4, R // tr),
            in_specs=[pl.BlockSpec((None, tr, C), lambda q, i, place_ref: (2 * q + place_ref[2], i, 0)), blk],
            out_specs=blk),
        out_shape=jax.ShapeDtypeStruct(theirs.shape, BF16),
        compiler_params=_params(("parallel", "parallel")),
    )(place, parts, theirs)


def _adamw_math(w, g, m, v):
    m = ADAM_B1 * m + (1.0 - ADAM_B1) * g
    v = ADAM_B2 * v + (1.0 - ADAM_B2) * jnp.square(g)
    m_hat = m / (1.0 - ADAM_B1 ** ADAM_STEP)
    v_hat = v / (1.0 - ADAM_B2 ** ADAM_STEP)
    delta = -ADAM_LR * (m_hat / (jnp.sqrt(v_hat) + ADAM_EPS) + ADAM_WD * w)
    return delta, m, v


def adamw_sharded(w, m, v, parts, sib, others, place, name):
    R, C = w.shape
    tr = _pick(R, 256, 8)

    def body(place_ref, w_ref, m_ref, v_ref, a_ref, b_ref, o_ref, g_ref, d_ref, nm_ref, nv_ref):
        g = a_ref[...].astype(F32) + b_ref[...].astype(F32)
        for j in range(3):
            g = g + o_ref[j].astype(F32)
        delta, nm, nv = _adamw_math(w_ref[...], g, m_ref[...], v_ref[...])
        g_ref[...] = g
        d_ref[...] = delta
        nm_ref[...] = nm
        nv_ref[...] = nv

    row = pl.BlockSpec((tr, C), lambda i, place_ref: (i, 0))
    return pl.pallas_call(
        body, name=name,
        grid_spec=pltpu.PrefetchScalarGridSpec(
            num_scalar_prefetch=1, grid=(R // tr,),
            in_specs=[row] * 3 + [pl.BlockSpec((None, tr, C), lambda i, place_ref: (place_ref[0], i, 0)),
                                  pl.BlockSpec((None, tr, C), lambda i, place_ref: (place_ref[1], i, 0)),
                                  pl.BlockSpec((3, tr, C), lambda i, place_ref: (0, i, 0))],
            out_specs=[row] * 4),
        out_shape=[jax.ShapeDtypeStruct((R, C), F32)] * 4,
        compiler_params=_params(("parallel",)),
    )(place, w, m, v, parts, sib, others)


def adamw_packed(w, g, m, v, name):
    R = w.shape[0]

    def body(w_ref, g_ref, m_ref, v_ref, d_ref, nm_ref, nv_ref):
        delta, nm, nv = _adamw_math(w_ref[...], g_ref[...], m_ref[...], v_ref[...])
        d_ref[...] = delta
        nm_ref[...] = nm
        nv_ref[...] = nv

    full = pl.BlockSpec((R, LANES), lambda i: (0, 0))
    return pl.pallas_call(
        body, name=name, grid=(1,),
        in_specs=[full] * 4, out_specs=[full] * 3,
        out_shape=[jax.ShapeDtypeStruct((R, LANES), F32)] * 3,
        compiler_params=_params(("arbitrary",)),
    )(w, g, m, v)


def _pack(arrays):
    flat = []
    sizes = []
    for a in arrays:
        f = a.reshape(-1).astype(F32)
        pad = (-f.shape[0]) % LANES
        if pad:
            f = jnp.concatenate([f, jnp.zeros((pad,), F32)])
        flat.append(f)
        sizes.append(f.shape[0])
    rows = sum(sizes) // LANES
    pad_rows = (-rows) % 8
    if pad_rows:
        flat.append(jnp.zeros((pad_rows * LANES,), F32))
    return jnp.concatenate(flat).reshape(-1, LANES), sizes


def _unpack(packed, sizes, shapes):
    flat = packed.reshape(-1)
    out = []
    off = 0
    for size, shape in zip(sizes, shapes):
        n = int(np.prod(shape))
        out.append(flat[off:off + n].reshape(shape))
        off += size
    return out


def _to_blocks(full, axis):
    if axis == 0:
        return full.reshape(N_DEV, full.shape[0] // N_DEV, full.shape[1])
    r, n = full.shape
    return full.reshape(r, N_DEV, n // N_DEV).transpose(1, 0, 2)


def _from_blocks(blocks, axis):
    if axis == 0:
        return blocks.reshape(blocks.shape[0] * blocks.shape[1], blocks.shape[2])
    return blocks.transpose(1, 0, 2).reshape(blocks.shape[1], blocks.shape[0] * blocks.shape[2])


def kernel(x, ln0_g, ln0_b, w_in, b_in, conv_w, w_a, w_b, w_o, b_o, ln1_g, ln1_b, w_up, b_up, ffn_conv_w, ffn_conv_b, w_down, b_down, ln2_g, ln2_b, loss_target, m_ln0_g, m_ln0_b, m_w_in, m_b_in, m_conv_w, m_w_a, m_w_b, m_w_o, m_b_o, m_ln1_g, m_ln1_b, m_w_up, m_b_up, m_ffn_conv_w, m_ffn_conv_b, m_w_down, m_b_down, m_ln2_g, m_ln2_b, v_ln0_g, v_ln0_b, v_w_in, v_b_in, v_conv_w, v_w_a, v_w_b, v_w_o, v_b_o, v_ln1_g, v_ln1_b, v_w_up, v_b_up, v_ffn_conv_w, v_ffn_conv_b, v_w_down, v_b_down, v_ln2_g, v_ln2_b):
    T, D = x.shape[1], x.shape[2]
    F = ffn_conv_b.shape[-1]
    xs = x.reshape(T, D)
    tgt = loss_target.reshape(T, D)
    dev = 4 * lax.axis_index("x") + 2 * lax.axis_index("y") + lax.axis_index("c")
    chip = 2 * lax.axis_index("x") + lax.axis_index("y")
    core = lax.axis_index("c")
    place = jnp.stack([dev, chip, core]).astype(jnp.int32)

    big = dict(w_in=(w_in[0], 1), w_a=(w_a[0], 0), w_b=(w_b[0], 1), w_o=(w_o[0], 0), w_up=(w_up[0], 1),
               w_down=(w_down[0], 0))
    names = list(big)
    shards = {k: big[k][0].astype(BF16) for k in names}
    ln0g, ln0b = ln0_g.reshape(1, D), ln0_b.reshape(1, D)
    h0, h0b, *rest = ln_fwd(xs, None, ln0g, ln0b, "ln0_fwd_gather_w_in", dilations=DILATIONS[1:],
                            gather=[shards["w_in"], conv_w[0], ffn_conv_w[0]])
    h0_res = [h0b] + [h.reshape(T, D) for h in rest[:2]]
    g_in, g_conv, g_fcw = rest[2:]
    full = {"w_in": _from_blocks(g_in, 1)}
    conv_full = _from_blocks(g_conv, 1)
    fcw_full = _from_blocks(g_fcw, 1)
    late_groups = (("w_a", "w_b", "w_o"), ("w_up", "w_down"))
    late_handles = []
    token = conv_full[:1, :1] * 0.0
    for n, keys in enumerate(late_groups):
        srcs = [shards[k] + token[0, 0].astype(BF16) for k in keys]
        handles, token = copies_start(srcs, [jax.ShapeDtypeStruct((N_DEV,) + s.shape, BF16) for s in srcs],
                                      _to_all_plan, N_DEV - 1, f"gather_late_{n}_start")
        late_handles.append(handles)

    def late_weights(n, after):
        lands = copies_wait(late_handles[n], _to_all_plan, after, f"gather_late_{n}_wait")
        for k, land in zip(late_groups[n], lands):
            full[k] = _from_blocks(lax.dynamic_update_index_in_dim(land, shards[k], dev, 0), big[k][1])

    o_q = 3 * D
    o_g = o_q + 3 * QKV_W
    w_pa, w_qkv, w_pg = full["w_in"][:, :o_q], full["w_in"][:, o_q:o_g], full["w_in"][:, o_g:]
    b_pa, b_qkv, b_pg = b_in[:, :o_q], b_in[:, o_q:o_g], b_in[:, o_g:]

    proj_a = mm_nn(h0b, w_pa, b_pa, ACT, "proj_conv", after=token)
    proj_g = mm_nn(h0b, w_pg, b_pg, ACT, "proj_gates")
    zero_d = jnp.zeros((1, D), F32)
    s_a = conv_a_fwd(proj_a, conv_full, "conv_a_fwd")
    late_weights(0, s_a)
    y_a = mm_nn(s_a, full["w_a"], zero_d, ACT, "branch_a_out")

    def group_cols(m, g):
        return jnp.concatenate([m[:, s * QKV_W + g * GROUP_W:s * QKV_W + (g + 1) * GROUP_W] for s in range(3)], 1)

    w_grp = [group_cols(w_qkv, g) for g in range(3)]
    qkvs, outs, lses = [], [], []
    for g, d in enumerate(DILATIONS):
        qkv = mm_nn(h0_res[g], w_grp[g], group_cols(b_qkv, g), BF16, f"proj_qkv_{g}").reshape(d, T // d, 3 * GROUP_W)
        o, l = att_fwd(qkv, g, f"att_fwd_{g}")
        qkvs.append(qkv)
        outs.append(o)
        lses.append(l)
    comb = combine_fwd(outs, lses, "combine_fwd")
    y_b = mm_nn(comb, full["w_b"], zero_d, ACT, "branch_b_out")
    z = gate_fwd(proj_g, y_a, y_b, "gate_fwd")
    h1, h1b, mix = ln_fwd(h0, ("nn", z, full["w_o"], b_o), ln1_g, ln1_b, "mix_out_ln1_fwd")
    late_weights(1, h1b)
    up, f_act = ffn_up_conv_f(h1b, full["w_up"], b_up, fcw_full, ffn_conv_b, "ffn_up_conv_f")

    dr2, dr2b, d_ln2_g, d_ln2_b, d_b_down, loss_part = ln_bwd(
        h1, ("nn", f_act, full["w_down"], b_down), ln2_g, ln2_b, None, None, tgt, "ffn_down_ln2_loss_bwd")
    dw_down, _ = mm_tn(f_act, dr2b, "dw_down")
    d_a, d_gate, cs_a, cs_gate, d_fcb, d_fcw = conv_f_bwd(dr2b, full["w_down"], up, fcw_full, ffn_conv_b,
                                                          "d_ffn_act_conv_f_bwd")
    dw_up_a, _ = mm_tn(h1b, d_a, "dw_up_a")
    dw_up_g, _ = mm_tn(h1b, d_gate, "dw_up_gate")
    dr1, dr1b, d_ln1_g, d_ln1_b, d_b_o, _ = ln_bwd(h0, mix, ln1_g, ln1_b, dr2, ("nt", [d_a, d_gate], full["w_up"]), None,
                                                   "d_h1_ln1_bwd")
    dw_o, _ = mm_tn(z, dr1b, "dw_o")
    dz = mm_nt(dr1b, full["w_o"], None, "d_z", out_dtype=ACT)
    dy_a, dy_b, dproj_g = gate_bwd(dz, proj_g, y_a, y_b, "gate_bwd")
    dw_a, _ = mm_tn(s_a, dy_a, "dw_a")
    ds_a = mm_nt(dy_a, full["w_a"], None, "d_s_a", out_dtype=ACT)
    dproj_a, d_conv = conv_a_bwd(ds_a, proj_a, conv_full, "conv_a_bwd")
    dw_b, _ = mm_tn(comb, dy_b, "dw_b")

    rs_mine, rs_sib, rs_handles = {}, {}, {}

    def reduce_start(keys, grads, tag):
        parts = [_to_blocks(grads[k], big[k][1]) for k in keys]
        from_sib = exchange_sibling(parts, f"grads_to_sibling_{tag}")
        sums = [pair_add(a, b, place, f"chip_sum_{k}") for k, a, b in zip(keys, parts, from_sib)]
        handles, tok = copies_start(sums, [jax.ShapeDtypeStruct((3,) + s.shape[1:], BF16) for s in sums],
                                    _to_chips_plan, 3, f"grads_to_chips_{tag}_start")
        for k, a, b in zip(keys, parts, from_sib):
            rs_mine[k], rs_sib[k] = a, b
        rs_handles[tag] = (keys, handles)
        return tok

    tok_a = reduce_start(("w_a", "w_b", "w_o", "w_up", "w_down"),
                         dict(w_a=dw_a, w_b=dw_b, w_o=dw_o, w_up=jnp.concatenate([dw_up_a, dw_up_g], 1), w_down=dw_down),
                         "a")
    dcomb = mm_nt(dy_b, full["w_b"], None, "d_comb", after=tok_a, out_dtype=ACT)
    dos, dms = combine_bwd(dcomb, outs, lses, "combine_bwd")
    dw_grp, cs_grp, dqkvs = [], [], []
    for g, d in enumerate(DILATIONS):
        dq, dk, dv = att_bwd(qkvs[g], dos[g], lses[g], dms[g], g, f"att_bwd_{g}")
        dqkv = [t.reshape(T, GROUP_W) for t in (dq, dk, dv)]
        dwg, csg = mm_tn(h0_res[g], dqkv, f"dw_in_qkv_{g}")
        dqkvs.append(dqkv)
        dw_grp.append(dwg)
        cs_grp.append(csg)
    dw_pa, cs_pa = mm_tn(h0b, dproj_a, "dw_in_conv")
    dw_pg, cs_pg = mm_tn(h0b, dproj_g, "dw_in_gates")

    def ungroup(parts):
        return jnp.concatenate([p[:, s * GROUP_W:(s + 1) * GROUP_W] for s in range(3) for p in parts], 1)

    db_in_parts = [cs_pa, ungroup(cs_grp), cs_pg]
    tok_b = reduce_start(("w_in",), dict(w_in=jnp.concatenate([dw_pa, ungroup(dw_grp), dw_pg], 1)), "b")
    dh0 = mm_nt(dproj_a, w_pa, None, "d_h0_conv", after=tok_b)
    dh0 = mm_nt(dproj_g, w_pg, dh0, "d_h0_gates")
    dh0_res = [(mm_nt(dqkvs[g], w_grp[g], None, f"d_h0_qkv_{g}").reshape(d, T // d, D), d)
               for g, d in enumerate(DILATIONS) if g > 0]
    dx, _, d_ln0_g, d_ln0_b, _, _ = ln_bwd(xs, None, ln0g, ln0b, dr1, ("nt", dqkvs[0], w_grp[0]), None, "d_h0_ln0_bwd",
                                           by_residue=[(dh0.reshape(1, T, D), 1)] + dh0_res)

    small = [d_ln0_g, d_ln0_b, jnp.concatenate(db_in_parts, 1), d_conv, d_b_o, d_ln1_g, d_ln1_b,
             jnp.concatenate([cs_a, cs_gate], 1), d_fcw, d_fcb, d_b_down, d_ln2_g, d_ln2_b, loss_part]
    packed, sizes = _pack(small)
    total = all_sum_small(packed, "sum_small")
    (g_ln0_g, g_ln0_b, g_b_in, g_conv_full, g_b_o, g_ln1_g, g_ln1_b, g_b_up, g_fcw_full, g_fcb, g_b_down, g_ln2_g,
     g_ln2_b, loss) = _unpack(total, sizes, [a.shape for a in small])
    cw = conv_w.shape[-1]
    fw = ffn_conv_w.shape[-1]
    g_conv = lax.dynamic_slice_in_dim(g_conv_full, dev * cw, cw, 1)
    g_fcw = lax.dynamic_slice_in_dim(g_fcw_full, dev * fw, fw, 1)

    from_chips = {}
    for tag, (keys, handles) in rs_handles.items():
        lands = copies_wait(handles, _to_chips_plan, total, f"grads_to_chips_{tag}_wait")
        from_chips.update(zip(keys, lands))

    moments = dict(w_in=(m_w_in, v_w_in), w_a=(m_w_a, v_w_a), w_b=(m_w_b, v_w_b), w_o=(m_w_o, v_w_o),
                   w_up=(m_w_up, v_w_up), w_down=(m_w_down, v_w_down))
    res_big = {}
    for k in names:
        res_big[k] = adamw_sharded(big[k][0], moments[k][0][0], moments[k][1][0], rs_mine[k], rs_sib[k], from_chips[k],
                                   place, f"adamw_{k}")

    small_names = ["ln0_g", "ln0_b", "b_in", "conv_w", "b_o", "ln1_g", "ln1_b", "b_up", "ffn_conv_w", "ffn_conv_b",
                   "b_down", "ln2_g", "ln2_b"]
    small_w = [ln0_g, ln0_b, b_in, conv_w, b_o, ln1_g, ln1_b, b_up, ffn_conv_w, ffn_conv_b, b_down, ln2_g, ln2_b]
    small_m = [m_ln0_g, m_ln0_b, m_b_in, m_conv_w, m_b_o, m_ln1_g, m_ln1_b, m_b_up, m_ffn_conv_w, m_ffn_conv_b,
               m_b_down, m_ln2_g, m_ln2_b]
    small_v = [v_ln0_g, v_ln0_b, v_b_in, v_conv_w, v_b_o, v_ln1_g, v_ln1_b, v_b_up, v_ffn_conv_w, v_ffn_conv_b,
               v_b_down, v_ln2_g, v_ln2_b]
    small_g = [g_ln0_g, g_ln0_b, g_b_in, g_conv, g_b_o, g_ln1_g, g_ln1_b, g_b_up, g_fcw, g_fcb, g_b_down, g_ln2_g,
               g_ln2_b]
    shapes = [w.shape for w in small_w]
    small_g = [g.reshape(s) for g, s in zip(small_g, shapes)]
    pw, psz = _pack(small_w)
    pg, _ = _pack(small_g)
    pm, _ = _pack(small_m)
    pv, _ = _pack(small_v)
    pd, pnm, pnv = adamw_packed(pw, pg, pm, pv, "adamw_small")
    res_small = {k: (g, d_, m_, v_) for k, g, d_, m_, v_ in zip(
        small_names, small_g, _unpack(pd, psz, shapes), _unpack(pnm, psz, shapes), _unpack(pnv, psz, shapes))}

    order = ["ln0_g", "ln0_b", "w_in", "b_in", "conv_w", "w_a", "w_b", "w_o", "b_o", "ln1_g", "ln1_b", "w_up", "b_up",
             "ffn_conv_w", "ffn_conv_b", "w_down", "b_down", "ln2_g", "ln2_b"]

    def result(k, j):
        if k in res_big:
            return res_big[k][j][None]
        return res_small[k][j]

    out = [loss.reshape(()), dx.reshape(x.shape)]
    for j in range(4):
        out += [result(k, j) for k in order]
    return tuple(out)
```

```python
import functools
import math

import numpy as np
import jax
import jax.numpy as jnp
from jax import lax
from jax.experimental import pallas as pl
from jax.experimental.pallas import tpu as pltpu

F32 = jnp.float32
BF16 = jnp.bfloat16
ACT = BF16

N_DEV = 8
LN_EPS = 1e-5
ALPHA = (2.0 * 1) ** 0.25
MASK_VALUE = -1e30
HEAD_DIM = 64
GROUP_W = 512
QKV_W = 3 * GROUP_W
DILATIONS = (1, 4, 16)
RADIUS = 64
LANES = 128
HALO = 8
HALO_BF16 = 16
ATT_TQ = 128

ADAM_LR = 0.001
ADAM_B1 = 0.9
ADAM_B2 = 0.999
ADAM_EPS = 1e-08
ADAM_WD = 0.01
ADAM_STEP = 10

VMEM_LIMIT = 52 * 1024 * 1024
OUT_TILE_BYTES = 8 * 1024 * 1024
MESH = pl.DeviceIdType.MESH
NT_DIMS = (((1,), (1,)), ((), ()))
TN_DIMS = (((0,), (0,)), ((), ()))


def _pick(n, target, align=LANES):
    if n <= target:
        return n
    best = None
    for t in range(align, target + 1, align):
        if n % t == 0:
            best = t
    assert best is not None, (n, target, align)
    return best


def _params(sems=None):
    return pltpu.CompilerParams(dimension_semantics=sems, vmem_limit_bytes=VMEM_LIMIT)


def _alibi_slopes():
    n = 3 * 8
    return np.exp2(-8.0 * np.arange(1, n + 1, dtype=np.float64) / n).astype(np.float32).reshape(3, 8)


def _ln_stats(r):
    mu = jnp.mean(r, -1, keepdims=True)
    xc = r - mu
    var = jnp.mean(xc * xc, -1, keepdims=True)
    rstd = lax.rsqrt(var + LN_EPS)
    return xc, rstd


def _load_natural(ref, d, scr):
    if d == 1:
        return ref[0]
    n, C = ref.shape[1], ref.shape[2]
    for c in range(C // LANES):
        for r in range(d):
            scr[c, pl.ds(r, n, stride=d), :] = ref[r, :, c * LANES:(c + 1) * LANES]
    return jnp.concatenate([scr[c] for c in range(C // LANES)], axis=1)


def _store_by_residue(val, ref, d, scr):
    if d == 1:
        ref[0] = val.astype(ref.dtype)
        return
    n, C = ref.shape[1], ref.shape[2]
    for c in range(C // LANES):
        scr[c] = val[:, c * LANES:(c + 1) * LANES]
    for c in range(C // LANES):
        for r in range(d):
            ref[r, :, c * LANES:(c + 1) * LANES] = scr[c, pl.ds(r, n, stride=d), :].astype(ref.dtype)


def _residue_spec(tm, d, C):
    return pl.BlockSpec((d, tm // d, C), lambda i: (0, i, 0))


def _residue_scratch(tm, C):
    return pltpu.VMEM((C // LANES, tm, LANES), F32)


def ln_fwd(a, res, g, b, name, dilations=(), gather=()):
    T, D = a.shape
    res_mm = isinstance(res, tuple)
    tm = _pick(T, 256 if res_mm else 512, 8)
    res_ins = list(res[1:]) if res_mm else ([] if res is None else [res])
    nd = len(dilations)
    ng = len(gather)
    n_in = 1 + len(res_ins) + 2
    last = T // tm - 1

    def body(*refs):
        a_ref = refs[0]
        r = a_ref[...]
        if res_mm:
            res_val = jnp.dot(refs[1][...], refs[2][...], preferred_element_type=F32) + refs[3][...]
            refs[-1 - n_scratch][...] = res_val
            r = ALPHA * r + res_val
        elif res_ins:
            r = ALPHA * r + refs[1][...]
        g_ref, b_ref = refs[n_in - 2], refs[n_in - 1]
        shard_refs = refs[n_in:n_in + ng]
        h_ref, hb_ref = refs[n_in + ng], refs[n_in + ng + 1]
        p_refs = refs[n_in + ng + 2:n_in + ng + 2 + nd]
        full_refs = refs[n_in + ng + 2 + nd:n_in + 2 * ng + 2 + nd]
        scratch = refs[len(refs) - n_scratch:]
        sems = scratch[len(scratch) - 3:] if ng else ()

        if ng:
            @pl.when(pl.program_id(0) == 0)
            def _():
                _gather_begin(shard_refs, full_refs, *sems)

        xc, rstd = _ln_stats(r)
        h = xc * rstd * g_ref[...] + b_ref[...]
        h_ref[...] = h
        hb_ref[...] = h.astype(BF16)
        for d, p_ref in zip(dilations, p_refs):
            _store_by_residue(h, p_ref, d, scratch[0])

        if ng:
            @pl.when(pl.program_id(0) == last)
            def _():
                _gather_finish(shard_refs, full_refs, *sems)

    row = pl.BlockSpec((tm, D), lambda i: (i, 0))
    vec = pl.BlockSpec((1, D), lambda i: (0, 0))
    hbm = pl.BlockSpec(memory_space=pl.ANY)
    if res_mm:
        res_specs = [pl.BlockSpec((tm, res[1].shape[1]), lambda i: (i, 0)), pl.BlockSpec(res[2].shape, lambda i: (0, 0)), vec]
    else:
        res_specs = [row] * len(res_ins)
    scratch_shapes = ([_residue_scratch(tm, D)] if nd else []) + (_gather_scratch(ng) if ng else [])
    n_scratch = len(scratch_shapes)
    ins = [a] + res_ins + [g, b] + list(gather)
    return pl.pallas_call(
        body, name=name, grid=(T // tm,),
        in_specs=[row] + res_specs + [vec, vec] + [hbm] * ng,
        out_specs=[row, row] + [_residue_spec(tm, d, D) for d in dilations] + [hbm] * ng + ([row] if res_mm else []),
        out_shape=[jax.ShapeDtypeStruct((T, D), F32), jax.ShapeDtypeStruct((T, D), BF16)]
        + [jax.ShapeDtypeStruct((d, T // d, D), BF16) for d in dilations]
        + [jax.ShapeDtypeStruct((N_DEV,) + s.shape, s.dtype) for s in gather]
        + ([jax.ShapeDtypeStruct((T, D), F32)] if res_mm else []),
        scratch_shapes=scratch_shapes,
        compiler_params=_params(("arbitrary",) if ng else ("parallel",)),
    )(*ins)


def ln_bwd(a, res, g, b, d1, d2, tgt, name, by_residue=()):
    T, D = a.shape
    tm = _pick(T, 256, 8)
    loss_mode = tgt is not None
    nres = len(by_residue)
    row = pl.BlockSpec((tm, D), lambda i: (i, 0))
    vec = pl.BlockSpec((1, D), lambda i: (0, 0))
    one = pl.BlockSpec((1, 1), lambda i: (0, 0))

    def rows_of(x):
        return pl.BlockSpec((tm, x.shape[1]), lambda i: (i, 0))

    def whole(x):
        return pl.BlockSpec(x.shape, lambda i: (0, 0))

    ins, in_specs, slots = [], [], {}

    def operand(key, arrays, specs):
        slots[key] = (len(ins), len(arrays))
        ins.extend(arrays)
        in_specs.extend(specs)

    operand("a", [a], [row])
    if isinstance(res, tuple):
        _, x, w, bias = res
        operand("res_mm", [x, w, bias], [rows_of(x), whole(w), vec])
    elif res is not None:
        operand("res", [res], [row])
    operand("gb", [g, b], [vec, vec])
    if loss_mode:
        operand("tgt", [tgt], [row])
    else:
        operand("d1", [d1], [row])
        if isinstance(d2, tuple):
            _, pieces, w = d2
            operand("d2_mm", list(pieces) + [w], [rows_of(p) for p in pieces] + [whole(w)])
        else:
            operand("d2", [d2], [row])
    operand("by_residue", [e for e, _ in by_residue], [_residue_spec(tm, d, D) for _, d in by_residue])
    n_in = len(ins)

    def body(*refs):
        def get(key):
            first, count = slots[key]
            return refs[first:first + count]

        dr_ref, drb_ref, dg_ref, db_ref, ds_ref, loss_ref = refs[n_in:n_in + 6]
        i = pl.program_id(0)

        @pl.when(i == 0)
        def _():
            dg_ref[...] = jnp.zeros_like(dg_ref)
            db_ref[...] = jnp.zeros_like(db_ref)
            ds_ref[...] = jnp.zeros_like(ds_ref)
            loss_ref[...] = jnp.zeros_like(loss_ref)

        r = get("a")[0][...]
        if "res_mm" in slots:
            x_ref, w_ref, bias_ref = get("res_mm")
            r = ALPHA * r + (jnp.dot(x_ref[...], w_ref[...], preferred_element_type=F32) + bias_ref[...])
        elif "res" in slots:
            r = ALPHA * r + get("res")[0][...]
        g_ref, b_ref = get("gb")
        xc, rstd = _ln_stats(r)
        xhat = xc * rstd
        gam = g_ref[...]
        if loss_mode:
            err = xhat * gam + b_ref[...] - get("tgt")[0][...]
            dy = err * (1.0 / D)
            row_loss = jnp.mean(err * err, -1, keepdims=True)
            loss_ref[...] += 0.5 * jnp.sum(row_loss, 0, keepdims=True)
        else:
            if "d2_mm" in slots:
                *p_refs, w_ref = get("d2_mm")
                av = p_refs[0][...] if len(p_refs) == 1 else jnp.concatenate([p[...] for p in p_refs], axis=1)
                d2v = lax.dot_general(av, w_ref[...], NT_DIMS, preferred_element_type=F32)
            else:
                d2v = get("d2")[0][...]
            dy = ALPHA * get("d1")[0][...] + d2v
        for (_, d), e_ref in zip(by_residue, get("by_residue")):
            dy = dy + _load_natural(e_ref, d, refs[-1])
        dyg = dy * gam
        c1 = jnp.mean(dyg, -1, keepdims=True)
        c2 = jnp.mean(dyg * xhat, -1, keepdims=True)
        dr = rstd * (dyg - c1 - xhat * c2)
        dr_ref[...] = dr
        drb_ref[...] = dr.astype(BF16)
        dg_ref[...] += jnp.sum(dy * xhat, 0, keepdims=True)
        db_ref[...] += jnp.sum(dy, 0, keepdims=True)
        ds_ref[...] += jnp.sum(dr, 0, keepdims=True)

    return pl.pallas_call(
        body, name=name, grid=(T // tm,),
        in_specs=in_specs,
        out_specs=[row, row, vec, vec, vec, one],
        out_shape=[jax.ShapeDtypeStruct((T, D), F32), jax.ShapeDtypeStruct((T, D), BF16),
                   jax.ShapeDtypeStruct((1, D), F32), jax.ShapeDtypeStruct((1, D), F32),
                   jax.ShapeDtypeStruct((1, D), F32), jax.ShapeDtypeStruct((1, 1), F32)],
        scratch_shapes=[_residue_scratch(tm, D)] if nres else [],
        compiler_params=_params(("arbitrary",)),
    )(*ins)


_TOKEN_SPEC = pl.BlockSpec((8, LANES), lambda i: (0, 0))


def mm_nn(a, w, bias, out_dtype, name, after=None):
    M, K = a.shape
    N = w.shape[1]
    tm = _pick(M, max(256, min(1024, OUT_TILE_BYTES // (N * jnp.dtype(out_dtype).itemsize))), 8)
    tc = _pick(N, 512)

    def body(a_ref, w_ref, b_ref, *rest):
        o_ref = rest[-1]
        av = a_ref[...]
        for j in range(N // tc):
            cols = slice(j * tc, (j + 1) * tc)
            acc = jnp.dot(av, w_ref[:, cols], preferred_element_type=F32)
            o_ref[:, cols] = (acc + b_ref[:, cols]).astype(out_dtype)

    return pl.pallas_call(
        body, name=name, grid=(M // tm,),
        in_specs=[pl.BlockSpec((tm, K), lambda i: (i, 0)),
                  pl.BlockSpec((K, N), lambda i: (0, 0)),
                  pl.BlockSpec((1, N), lambda i: (0, 0))] + ([] if after is None else [_TOKEN_SPEC]),
        out_specs=pl.BlockSpec((tm, N), lambda i: (i, 0)),
        out_shape=jax.ShapeDtypeStruct((M, N), out_dtype),
        compiler_params=_params(("parallel",)),
    )(a, w, bias, *([] if after is None else [after]))


def mm_nt(a, w, acc_in, name, after=None, w_block=0, out_dtype=F32):
    pieces = list(a) if isinstance(a, (list, tuple)) else [a]
    M = pieces[0].shape[0]
    widths = [p.shape[1] for p in pieces]
    K = sum(widths)
    N = w.shape[0]
    tm = _pick(M, 1024, 8)
    tc = _pick(N, 512)
    has_acc = acc_in is not None
    n_a = len(pieces)

    def body(*refs):
        a_refs, w_ref = refs[:n_a], refs[n_a]
        c_ref = refs[n_a + 1] if has_acc else None
        o_ref = refs[-1]
        av = a_refs[0][...] if n_a == 1 else jnp.concatenate([r[...] for r in a_refs], axis=1)
        for j in range(N // tc):
            cols = slice(j * tc, (j + 1) * tc)
            acc = lax.dot_general(av, w_ref[cols, :], NT_DIMS, preferred_element_type=F32)
            if has_acc:
                acc = acc + c_ref[:, cols]
            o_ref[:, cols] = acc.astype(out_dtype)

    out_spec = pl.BlockSpec((tm, N), lambda i: (i, 0))
    in_specs = [pl.BlockSpec((tm, kw), lambda i: (i, 0)) for kw in widths]
    in_specs.append(pl.BlockSpec((N, K), lambda i: (0, w_block)))
    ins = pieces + [w]
    if has_acc:
        in_specs.append(out_spec)
        ins.append(acc_in)
    if after is not None:
        in_specs.append(_TOKEN_SPEC)
        ins.append(after)
    return pl.pallas_call(
        body, name=name, grid=(M // tm,),
        in_specs=in_specs, out_specs=out_spec,
        out_shape=jax.ShapeDtypeStruct((M, N), out_dtype),
        compiler_params=_params(("parallel",)),
    )(*ins)


def mm_tn(a, b, name, out_dtype=BF16):
    pieces = list(b) if isinstance(b, (list, tuple)) else [b]
    T, M = a.shape
    widths = [p.shape[1] for p in pieces]
    N = sum(widths)
    tk = _pick(T, 512, 8)
    nk = T // tk
    tc = _pick(M, 256)
    n_b = len(pieces)

    def body(*refs):
        a_ref, b_refs = refs[0], refs[1:1 + n_b]
        o_ref, cs_ref, acc_ref = refs[1 + n_b:]
        k = pl.program_id(0)

        @pl.when(k == 0)
        def _():
            acc_ref[...] = jnp.zeros_like(acc_ref)
            cs_ref[...] = jnp.zeros_like(cs_ref)

        bv = b_refs[0][...] if n_b == 1 else jnp.concatenate([r[...] for r in b_refs], axis=1)
        cs_ref[...] += jnp.sum(bv.astype(F32), 0, keepdims=True)
        for mi in range(M // tc):
            rows = slice(mi * tc, (mi + 1) * tc)
            acc_ref[rows, :] += lax.dot_general(a_ref[:, rows], bv, TN_DIMS, preferred_element_type=F32)

        @pl.when(k == nk - 1)
        def _():
            o_ref[...] = acc_ref[...].astype(out_dtype)

    return pl.pallas_call(
        body, name=name, grid=(nk,),
        in_specs=[pl.BlockSpec((tk, M), lambda k: (k, 0))] + [pl.BlockSpec((tk, wd), lambda k: (k, 0)) for wd in widths],
        out_specs=[pl.BlockSpec((M, N), lambda k: (0, 0)), pl.BlockSpec((1, N), lambda k: (0, 0))],
        out_shape=[jax.ShapeDtypeStruct((M, N), out_dtype), jax.ShapeDtypeStruct((1, N), F32)],
        scratch_shapes=[pltpu.VMEM((M, N), F32)],
        compiler_params=_params(("arbitrary",)),
    )(a, *pieces)


def _ext_rows(prev_ref, main_ref, next_ref, i, tm, T, dtype=F32):
    before = jnp.where(i == 0, 0.0, prev_ref[...])
    after = jnp.where(i == T // tm - 1, 0.0, next_ref[...])
    return jnp.concatenate([before, main_ref[...], after], axis=0).astype(dtype)


def _prev_row(x):
    return pltpu.roll(x, 1, 0)


def _next_row(x):
    return pltpu.roll(x, x.shape[0] - 1, 0)


def _conv3(u, w_ref):
    return _prev_row(u) * w_ref[0:1, :] + u * w_ref[1:2, :] + _next_row(u) * w_ref[2:3, :]


def _main(x, tm, halo=HALO):
    return x[halo:halo + tm]


def _halo_specs(tm, tc, T, col, order, halo=HALO):
    r = tm // halo
    last = T // halo - 1
    if order == "ij":
        return (pl.BlockSpec((halo, tc), lambda i, j: (jnp.maximum(i * r - 1, 0), col(j))),
                pl.BlockSpec((tm, tc), lambda i, j: (i, col(j))),
                pl.BlockSpec((halo, tc), lambda i, j: (jnp.minimum((i + 1) * r, last), col(j))))
    return (pl.BlockSpec((halo, tc), lambda j, i: (jnp.maximum(i * r - 1, 0), col(j))),
            pl.BlockSpec((tm, tc), lambda j, i: (i, col(j))),
            pl.BlockSpec((halo, tc), lambda j, i: (jnp.minimum((i + 1) * r, last), col(j))))


def conv_a_fwd(proj_a, conv_w, name):
    T, D3 = proj_a.shape
    D = D3 // 3
    tm = _pick(T, 256, 8)

    def body(p_ref, m_ref, n_ref, w_ref, o_ref):
        i = pl.program_id(0)
        ext = _ext_rows(p_ref, m_ref, n_ref, i, tm, T)
        u = ext[:, D:2 * D] * ext[:, 2 * D:]
        cu = _conv3(u, w_ref)
        o_ref[...] = (m_ref[:, :D].astype(F32) * _main(cu, tm, HALO_BF16)).astype(BF16)

    prev, main, nxt = _halo_specs(tm, D3, T, lambda j: 0, "ij", HALO_BF16)
    return pl.pallas_call(
        body, name=name, grid=(T // tm, 1),
        in_specs=[prev, main, nxt, pl.BlockSpec((3, D), lambda i, j: (0, 0))],
        out_specs=pl.BlockSpec((tm, D), lambda i, j: (i, 0)),
        out_shape=jax.ShapeDtypeStruct((T, D), BF16),
        compiler_params=_params(("parallel", "arbitrary")),
    )(proj_a, proj_a, proj_a, conv_w)


def conv_a_bwd(ds_a, proj_a, conv_w, name):
    T, D3 = proj_a.shape
    D = D3 // 3
    tm = _pick(T, 256, 8)

    def body(dp_ref, dm_ref, dn_ref, p_ref, m_ref, n_ref, w_ref, o_ref, dw_ref):
        i = pl.program_id(0)

        @pl.when(i == 0)
        def _():
            dw_ref[...] = jnp.zeros_like(dw_ref)

        ext = _ext_rows(p_ref, m_ref, n_ref, i, tm, T)
        dsa = _ext_rows(dp_ref, dm_ref, dn_ref, i, tm, T)
        gb, gc, hin = ext[:, :D], ext[:, D:2 * D], ext[:, 2 * D:]
        u = gc * hin
        u_prev, u_next = _prev_row(u), _next_row(u)
        cu = u_prev * w_ref[0:1, :] + u * w_ref[1:2, :] + u_next * w_ref[2:3, :]
        dcu = dsa * gb
        du = _next_row(dcu) * w_ref[0:1, :] + dcu * w_ref[1:2, :] + _prev_row(dcu) * w_ref[2:3, :]
        h = HALO_BF16
        o_ref[:, :D] = _main(dsa * cu, tm, h).astype(BF16)
        o_ref[:, D:2 * D] = _main(du * hin, tm, h).astype(BF16)
        o_ref[:, 2 * D:] = _main(du * gc, tm, h).astype(BF16)
        dcu_m = _main(dcu, tm, h)
        dw_ref[0:1, :] += jnp.sum(dcu_m * _main(u_prev, tm, h), 0, keepdims=True)
        dw_ref[1:2, :] += jnp.sum(dcu_m * _main(u, tm, h), 0, keepdims=True)
        dw_ref[2:3, :] += jnp.sum(dcu_m * _main(u_next, tm, h), 0, keepdims=True)

    dprev, dmain, dnxt = _halo_specs(tm, D, T, lambda j: 0, "ij", HALO_BF16)
    prev, main, nxt = _halo_specs(tm, D3, T, lambda j: 0, "ij", HALO_BF16)
    return pl.pallas_call(
        body, name=name, grid=(T // tm, 1),
        in_specs=[dprev, dmain, dnxt, prev, main, nxt, pl.BlockSpec((3, D), lambda i, j: (0, 0))],
        out_specs=[pl.BlockSpec((tm, D3), lambda i, j: (i, 0)), pl.BlockSpec((3, D), lambda i, j: (0, 0))],
        out_shape=[jax.ShapeDtypeStruct((T, D3), BF16), jax.ShapeDtypeStruct((3, D), F32)],
        compiler_params=_params(("arbitrary", "arbitrary")),
    )(ds_a, ds_a, ds_a, proj_a, proj_a, proj_a, conv_w)


_INV_SQRT2 = 1.0 / math.sqrt(2.0)
_INV_SQRT_2PI = 1.0 / math.sqrt(2.0 * math.pi)


def ffn_up_conv_f(h, w_up, b_up, fcw, fcb, name):
    T, D = h.shape
    F = fcb.shape[1]
    tm = _pick(T, 256, 8)
    tc = _pick(F, 256)
    halo = HALO_BF16

    def body(hp_ref, hm_ref, hn_ref, w_ref, b_ref, cw_ref, cb_ref, up_ref, f_ref):
        i = pl.program_id(0)
        h_ext = _ext_rows(hp_ref, hm_ref, hn_ref, i, tm, T, dtype=BF16)
        h_main = hm_ref[...]
        rows = i * tm - halo + lax.broadcasted_iota(jnp.int32, (tm + 2 * halo, 1), 0)
        inside = (rows >= 0) & (rows < T)
        for c in range(F // tc):
            cols = slice(c * tc, (c + 1) * tc)
            gcols = slice(F + c * tc, F + (c + 1) * tc)
            a_ext = jnp.dot(h_ext, w_ref[:, cols], preferred_element_type=F32) + b_ref[:, cols]
            a_ext = jnp.where(inside, a_ext, 0.0)
            gate = jnp.dot(h_main, w_ref[:, gcols], preferred_element_type=F32) + b_ref[:, gcols]
            up_ref[:, cols] = _main(a_ext, tm, halo)
            up_ref[:, gcols] = gate
            ca = _main(_prev_row(a_ext) * cw_ref[0:1, cols] + a_ext * cw_ref[1:2, cols]
                       + _next_row(a_ext) * cw_ref[2:3, cols], tm, halo) + cb_ref[:, cols]
            gl = 0.5 * ca * (1.0 + lax.erf(ca * _INV_SQRT2))
            f_ref[:, cols] = (gl * gate).astype(BF16)

    prev, main, nxt = _halo_specs(tm, D, T, lambda j: 0, "ij", halo)
    whole = lambda x: pl.BlockSpec(x.shape, lambda i, j: (0, 0))
    return pl.pallas_call(
        body, name=name, grid=(T // tm, 1),
        in_specs=[prev, main, nxt, whole(w_up), whole(b_up), whole(fcw), whole(fcb)],
        out_specs=[pl.BlockSpec((tm, 2 * F), lambda i, j: (i, 0)), pl.BlockSpec((tm, F), lambda i, j: (i, 0))],
        out_shape=[jax.ShapeDtypeStruct((T, 2 * F), F32), jax.ShapeDtypeStruct((T, F), BF16)],
        compiler_params=_params(("parallel", "arbitrary")),
    )(h, h, h, w_up, b_up, fcw, fcb)


def conv_f_bwd(dy, w_down, up, fcw, fcb, name):
    T, F2 = up.shape
    F = F2 // 2
    D = dy.shape[1]
    tm = _pick(T, 256, 8)
    tc = _pick(F, 256)

    def body(yp_ref, ym_ref, yn_ref, wd_ref, up_ref, um_ref, un_ref, w_ref, b_ref,
             da_ref, dg_ref, csa_ref, csg_ref, dfb_ref, dfw_ref):
        i = pl.program_id(0)
        first, last = i == 0, i == T // tm - 1

        @pl.when(first)
        def _():
            csa_ref[...] = jnp.zeros_like(csa_ref)
            csg_ref[...] = jnp.zeros_like(csg_ref)
            dfb_ref[...] = jnp.zeros_like(dfb_ref)
            dfw_ref[...] = jnp.zeros_like(dfw_ref)

        def ext(cols):
            return jnp.concatenate([jnp.where(first, 0.0, up_ref[:, cols]), um_ref[:, cols],
                                    jnp.where(last, 0.0, un_ref[:, cols])], axis=0)

        dy_ext = _ext_rows(yp_ref, ym_ref, yn_ref, i, tm, T, dtype=BF16)
        for c in range(F // tc):
            cols = slice(c * tc, (c + 1) * tc)
            dfe = lax.dot_general(dy_ext, wd_ref[cols, :], NT_DIMS, preferred_element_type=F32)
            dfe = dfe[HALO_BF16 - HALO:HALO_BF16 + tm + HALO]
            a = ext(cols)
            gate = ext(slice(F + c * tc, F + (c + 1) * tc))
            a_prev, a_next = _prev_row(a), _next_row(a)
            ca = a_prev * w_ref[0:1, cols] + a * w_ref[1:2, cols] + a_next * w_ref[2:3, cols] + b_ref[:, cols]
            cdf = 0.5 * (1.0 + lax.erf(ca * _INV_SQRT2))
            gl = ca * cdf
            gp = cdf + ca * (jnp.exp(-0.5 * ca * ca) * _INV_SQRT_2PI)
            dgate = _main(dfe * gl, tm)
            dca = dfe * gate * gp
            da = _main(_next_row(dca) * w_ref[0:1, cols] + dca * w_ref[1:2, cols] + _prev_row(dca) * w_ref[2:3, cols],
                       tm)
            da_ref[:, cols] = da.astype(BF16)
            dg_ref[:, cols] = dgate.astype(BF16)
            csa_ref[:, cols] += jnp.sum(da, 0, keepdims=True)
            csg_ref[:, cols] += jnp.sum(dgate, 0, keepdims=True)
            dca_m = _main(dca, tm)
            dfb_ref[:, cols] += jnp.sum(dca_m, 0, keepdims=True)
            dfw_ref[0:1, cols] += jnp.sum(dca_m * _main(a_prev, tm), 0, keepdims=True)
            dfw_ref[1:2, cols] += jnp.sum(dca_m * _main(a, tm), 0, keepdims=True)
            dfw_ref[2:3, cols] += jnp.sum(dca_m * _main(a_next, tm), 0, keepdims=True)

    uprev, umain, unxt = _halo_specs(tm, F2, T, lambda j: 0, "ij")
    yprev, ymain, ynxt = _halo_specs(tm, D, T, lambda j: 0, "ij", HALO_BF16)
    whole = lambda shape: pl.BlockSpec(shape, lambda i, j: (0, 0))
    tile = pl.BlockSpec((tm, F), lambda i, j: (i, 0))
    return pl.pallas_call(
        body, name=name, grid=(T // tm, 1),
        in_specs=[yprev, ymain, ynxt, whole((F, D)), uprev, umain, unxt, whole((3, F)), whole((1, F))],
        out_specs=[tile, tile, whole((1, F)), whole((1, F)), whole((1, F)), whole((3, F))],
        out_shape=[jax.ShapeDtypeStruct((T, F), BF16), jax.ShapeDtypeStruct((T, F), BF16),
                   jax.ShapeDtypeStruct((1, F), F32), jax.ShapeDtypeStruct((1, F), F32),
                   jax.ShapeDtypeStruct((1, F), F32), jax.ShapeDtypeStruct((3, F), F32)],
        compiler_params=_params(("arbitrary", "arbitrary")),
    )(dy, dy, dy, w_down, up, up, up, fcw, fcb)


def gate_fwd(proj_g, y_a, y_b, name):
    T, D = y_a.shape
    tm = _pick(T, 512, 8)

    def body(g_ref, a_ref, b_ref, o_ref):
        sa = jax.nn.sigmoid(g_ref[:, :D].astype(F32))
        sb = jax.nn.sigmoid(g_ref[:, D:].astype(F32))
        o_ref[...] = (sa * a_ref[...].astype(F32) + sb * b_ref[...].astype(F32)).astype(BF16)

    row = pl.BlockSpec((tm, D), lambda i: (i, 0))
    return pl.pallas_call(
        body, name=name, grid=(T // tm,),
        in_specs=[pl.BlockSpec((tm, 2 * D), lambda i: (i, 0)), row, row],
        out_specs=row,
        out_shape=jax.ShapeDtypeStruct((T, D), BF16),
        compiler_params=_params(("parallel",)),
    )(proj_g, y_a, y_b)


def gate_bwd(dz, proj_g, y_a, y_b, name):
    T, D = y_a.shape
    tm = _pick(T, 512, 8)

    def body(dz_ref, g_ref, a_ref, b_ref, da_ref, db_ref, dg_ref):
        dzv = dz_ref[...].astype(F32)
        sa = jax.nn.sigmoid(g_ref[:, :D].astype(F32))
        sb = jax.nn.sigmoid(g_ref[:, D:].astype(F32))
        da_ref[...] = (dzv * sa).astype(BF16)
        db_ref[...] = (dzv * sb).astype(BF16)
        dg_ref[:, :D] = (dzv * a_ref[...].astype(F32) * (sa * (1.0 - sa))).astype(BF16)
        dg_ref[:, D:] = (dzv * b_ref[...].astype(F32) * (sb * (1.0 - sb))).astype(BF16)

    row = pl.BlockSpec((tm, D), lambda i: (i, 0))
    wide = pl.BlockSpec((tm, 2 * D), lambda i: (i, 0))
    return pl.pallas_call(
        body, name=name, grid=(T // tm,),
        in_specs=[row, wide, row, row],
        out_specs=[row, row, wide],
        out_shape=[jax.ShapeDtypeStruct((T, D), BF16), jax.ShapeDtypeStruct((T, D), BF16),
                   jax.ShapeDtypeStruct((T, 2 * D), BF16)],
        compiler_params=_params(("parallel",)),
    )(dz, proj_g, y_a, y_b)


ATT_WIN = ATT_TQ + 2 * RADIUS
ATT_STEP = 1024
FAR = 1e32


def _att_window(qs, L):
    ks = pl.multiple_of(jnp.clip(qs - RADIUS, 0, L - ATT_WIN), RADIUS)
    return ks, jnp.where(qs == 0, 0, jnp.where(qs == L - ATT_TQ, 2, 1))


def _fill_bias_tables(bias_ref, sl_ref, hp, d):
    col_row = (lax.broadcasted_iota(jnp.int32, (ATT_TQ, ATT_WIN), 1)
               - lax.broadcasted_iota(jnp.int32, (ATT_TQ, ATT_WIN), 0))
    for v in range(3):
        ad = jnp.abs(col_row - v * RADIUS)
        dist = jnp.where(ad <= RADIUS, (ad * d).astype(F32), FAR)
        bias_ref[v, 0:ATT_TQ, :] = sl_ref[hp * 2] * dist
        bias_ref[v, ATT_TQ:2 * ATT_TQ, :] = sl_ref[hp * 2 + 1] * dist


def _head_masks():
    lane = lax.broadcasted_iota(jnp.int32, (1, LANES), 1)
    return [lane < HEAD_DIM, lane >= HEAD_DIM]


def _stack_heads(x, masks):
    zero = jnp.zeros_like(x)
    return jnp.concatenate([jnp.where(masks[0], x, zero), jnp.where(masks[1], x, zero)], axis=0)


def _unstack_heads(x2, masks):
    n = x2.shape[0] // 2
    return jnp.where(masks[0], x2[:n], x2[n:])


def _att_step(L):
    step = min(ATT_STEP, L)
    assert L % step == 0 and step % ATT_TQ == 0 and L >= ATT_WIN
    return step


def _residues_per_step(d, L):
    rps = max(1, min(d, ATT_STEP // L))
    assert d % rps == 0
    return rps


def att_fwd(qkv, group, name):
    d, L, _ = qkv.shape
    step = _att_step(L)
    rps = _residues_per_step(d, L)
    cg = GROUP_W // LANES
    slopes = jnp.asarray(_alibi_slopes()[group])
    scale = HEAD_DIM ** -0.5

    def body(sl_ref, q_ref, k_ref, v_ref, o_ref, l_ref, bias_ref, s_ref, p_ref):
        hp = pl.program_id(1)
        i = pl.program_id(2)

        @pl.when(i == 0)
        def _():
            _fill_bias_tables(bias_ref, sl_ref, hp, d)

        masks = _head_masks()
        per = step // ATT_TQ
        tiles = [(rr, t) for rr in range(rps) for t in range(per)]
        windows = [_att_window(i * step + t * ATT_TQ, L) for t in range(per)]
        for n, (rr, t) in enumerate(tiles):
            rows = slice(t * ATT_TQ, (t + 1) * ATT_TQ)
            ks, table = windows[t]
            q2 = _stack_heads(q_ref[rr, rows, :] * scale, masks)
            kw = k_ref[rr, pl.ds(ks, ATT_WIN), :]
            s_ref[n] = lax.dot_general(q2, kw, NT_DIMS, preferred_element_type=F32) - bias_ref[table]
        for n, (rr, t) in enumerate(tiles):
            rows = slice(t * ATT_TQ, (t + 1) * ATT_TQ)
            s = s_ref[n]
            m = jnp.max(s, -1, keepdims=True)
            p = jnp.exp(s - m)
            den = jnp.sum(p, -1, keepdims=True)
            p_ref[n] = (p / den).astype(BF16)
            l_ref[rr, rows, :] = _unstack_heads(m + jnp.log(den), masks)
        for n, (rr, t) in enumerate(tiles):
            rows = slice(t * ATT_TQ, (t + 1) * ATT_TQ)
            vw = v_ref[rr, pl.ds(windows[t][0], ATT_WIN), :]
            o2 = jnp.dot(p_ref[n], vw, preferred_element_type=F32)
            o_ref[rr, rows, :] = _unstack_heads(o2, masks)

    n_tiles = rps * step // ATT_TQ
    out_spec = pl.BlockSpec((rps, step, LANES), lambda r, hp, i: (r, i, hp))
    return pl.pallas_call(
        body, name=name, grid=(d // rps, cg, L // step),
        in_specs=[pl.BlockSpec(memory_space=pltpu.SMEM),
                  pl.BlockSpec((rps, step, LANES), lambda r, hp, i: (r, i, hp)),
                  pl.BlockSpec((rps, L, LANES), lambda r, hp, i: (r, 0, cg + hp)),
                  pl.BlockSpec((rps, L, LANES), lambda r, hp, i: (r, 0, 2 * cg + hp))],
        out_specs=[out_spec, out_spec],
        out_shape=[jax.ShapeDtypeStruct((d, L, GROUP_W), F32)] * 2,
        scratch_shapes=[pltpu.VMEM((3, 2 * ATT_TQ, ATT_WIN), F32),
                        pltpu.VMEM((n_tiles, 2 * ATT_TQ, ATT_WIN), F32),
                        pltpu.VMEM((n_tiles, 2 * ATT_TQ, ATT_WIN), BF16)],
        compiler_params=_params(("arbitrary", "arbitrary", "arbitrary")),
    )(slopes, qkv, qkv, qkv)


def att_bwd(qkv, do, lse, dmat, group, name, after=None):
    d, L, _ = qkv.shape
    step = _att_step(L)
    rps = _residues_per_step(d, L)
    nq = L // step
    cg = GROUP_W // LANES
    slopes = jnp.asarray(_alibi_slopes()[group])
    scale = HEAD_DIM ** -0.5

    def body(sl_ref, q_ref, k_ref, v_ref, do_ref, l_ref, dm_ref, *rest):
        dq_ref, dk_ref, dv_ref, dk_acc, dv_acc, bias_ref, s_ref, dp_ref, p_ref, ds_ref = rest[len(rest) - 10:]
        hp = pl.program_id(1)
        i = pl.program_id(2)

        @pl.when(i == 0)
        def _():
            dk_acc[...] = jnp.zeros_like(dk_acc)
            dv_acc[...] = jnp.zeros_like(dv_acc)
            _fill_bias_tables(bias_ref, sl_ref, hp, d)

        masks = _head_masks()

        def head_cols(x):
            return jnp.concatenate([jnp.max(jnp.where(hm, x, -jnp.inf), -1, keepdims=True) for hm in masks], axis=0)

        per = step // ATT_TQ
        tiles = [(rr, t) for rr in range(rps) for t in range(per)]
        windows = [_att_window(i * step + t * ATT_TQ, L) for t in range(per)]

        def stacked(ref, rr, t, factor=None):
            x = ref[rr, t * ATT_TQ:(t + 1) * ATT_TQ, :]
            return _stack_heads(x if factor is None else x * factor, masks)

        for n, (rr, t) in enumerate(tiles):
            ks, table = windows[t]
            q2 = stacked(q_ref, rr, t, scale)
            s_ref[n] = lax.dot_general(q2, k_ref[rr, pl.ds(ks, ATT_WIN), :], NT_DIMS,
                                       preferred_element_type=F32) - bias_ref[table]
            dp_ref[n] = lax.dot_general(stacked(do_ref, rr, t), v_ref[rr, pl.ds(ks, ATT_WIN), :], NT_DIMS,
                                        preferred_element_type=F32)
        for n, (rr, t) in enumerate(tiles):
            rows = slice(t * ATT_TQ, (t + 1) * ATT_TQ)
            p = jnp.exp(s_ref[n] - head_cols(l_ref[rr, rows, :]))
            p_ref[n] = p.astype(BF16)
            ds_ref[n] = (p * (dp_ref[n] - head_cols(dm_ref[rr, rows, :]))).astype(BF16)
        for n, (rr, t) in enumerate(tiles):
            rows = slice(t * ATT_TQ, (t + 1) * ATT_TQ)
            ks = windows[t][0]
            ds = ds_ref[n]
            dq2 = jnp.dot(ds, k_ref[rr, pl.ds(ks, ATT_WIN), :], preferred_element_type=F32)
            dq_ref[rr, rows, :] = (_unstack_heads(dq2, masks) * scale).astype(BF16)
            dk_acc[rr, pl.ds(ks, ATT_WIN), :] += lax.dot_general(ds, stacked(q_ref, rr, t, scale), TN_DIMS,
                                                                 preferred_element_type=F32)
            dv_acc[rr, pl.ds(ks, ATT_WIN), :] += lax.dot_general(p_ref[n], stacked(do_ref, rr, t), TN_DIMS,
                                                                 preferred_element_type=F32)

        @pl.when(i == nq - 1)
        def _():
            dk_ref[...] = dk_acc[...].astype(BF16)
            dv_ref[...] = dv_acc[...].astype(BF16)

    tile = pl.BlockSpec((rps, step, LANES), lambda r, hp, i: (r, i, hp))
    whole = pl.BlockSpec((rps, L, LANES), lambda r, hp, i: (r, 0, hp))
    return pl.pallas_call(
        body, name=name, grid=(d // rps, cg, nq),
        in_specs=[pl.BlockSpec(memory_space=pltpu.SMEM), tile,
                  pl.BlockSpec((rps, L, LANES), lambda r, hp, i: (r, 0, cg + hp)),
                  pl.BlockSpec((rps, L, LANES), lambda r, hp, i: (r, 0, 2 * cg + hp)),
                  tile, tile, tile] + ([] if after is None else [pl.BlockSpec((8, LANES), lambda r, hp, i: (0, 0))]),
        out_specs=[tile, whole, whole],
        out_shape=[jax.ShapeDtypeStruct((d, L, GROUP_W), BF16)] * 3,
        scratch_shapes=[pltpu.VMEM((rps, L, LANES), F32), pltpu.VMEM((rps, L, LANES), F32),
                        pltpu.VMEM((3, 2 * ATT_TQ, ATT_WIN), F32)]
        + [pltpu.VMEM((rps * step // ATT_TQ, 2 * ATT_TQ, ATT_WIN), dt) for dt in (F32, F32, BF16, BF16)],
        compiler_params=_params(("arbitrary", "arbitrary", "arbitrary")),
    )(slopes, qkv, qkv, qkv, do, lse, dmat, *([] if after is None else [after]))


def _group_weights(ls):
    m = jnp.maximum(jnp.maximum(ls[0], ls[1]), ls[2])
    es = [jnp.exp(l - m) for l in ls]
    tot = es[0] + es[1] + es[2]
    return [e / tot for e in es]


def combine_fwd(outs, lses, name):
    T = outs[0].shape[0] * outs[0].shape[1]
    tm = _pick(T, 512, 8)
    n_scr = 2 * (len(DILATIONS) - 1)

    def body(*refs):
        o_refs, l_refs, c_ref, scr = refs[:3], refs[3:6], refs[6], refs[7:]
        o = [_load_natural(o_refs[g], d, scr[g - 1] if g else None) for g, d in enumerate(DILATIONS)]
        l = [_load_natural(l_refs[g], d, scr[g + 1] if g else None) for g, d in enumerate(DILATIONS)]
        w = _group_weights(l)
        c_ref[...] = (w[0] * o[0] + w[1] * o[1] + w[2] * o[2]).astype(BF16)

    specs = [_residue_spec(tm, d, GROUP_W) for d in DILATIONS]
    return pl.pallas_call(
        body, name=name, grid=(T // tm,),
        in_specs=specs + specs, out_specs=pl.BlockSpec((tm, GROUP_W), lambda i: (i, 0)),
        out_shape=jax.ShapeDtypeStruct((T, GROUP_W), BF16),
        scratch_shapes=[_residue_scratch(tm, GROUP_W)] * n_scr,
        compiler_params=_params(("parallel",)),
    )(*outs, *lses)


def combine_bwd(dcomb, outs, lses, name):
    T = dcomb.shape[0]
    tm = _pick(T, 256, 8)
    head = np.arange(GROUP_W) // HEAD_DIM
    seg = jnp.asarray((head[:, None] == head[None, :]).astype(np.float32)).astype(BF16)
    ng = len(DILATIONS)
    n_scr = 4 * (ng - 1)

    def body(*refs):
        dc_ref, o_refs, l_refs, e_ref = refs[0], refs[1:1 + ng], refs[1 + ng:1 + 2 * ng], refs[1 + 2 * ng]
        do_refs, dm_refs = refs[2 + 2 * ng:2 + 3 * ng], refs[2 + 3 * ng:2 + 4 * ng]
        scr = refs[2 + 4 * ng:]
        o = [_load_natural(o_refs[g], d, scr[4 * (g - 1)] if g else None) for g, d in enumerate(DILATIONS)]
        l = [_load_natural(l_refs[g], d, scr[4 * (g - 1) + 1] if g else None) for g, d in enumerate(DILATIONS)]
        w = _group_weights(l)
        dc = dc_ref[...].astype(F32)
        e = e_ref[...]
        prod = dc * (w[0] * o[0] + w[1] * o[1] + w[2] * o[2])
        tot = jnp.zeros_like(dc)
        for _ in range(3):
            part = prod.astype(BF16)
            tot = tot + jnp.dot(part, e, preferred_element_type=F32)
            prod = prod - part.astype(F32)
        for g, d in enumerate(DILATIONS):
            _store_by_residue(w[g] * dc, do_refs[g], d, scr[4 * (g - 1) + 2] if g else None)
            _store_by_residue(w[g] * tot, dm_refs[g], d, scr[4 * (g - 1) + 3] if g else None)

    specs = [_residue_spec(tm, d, GROUP_W) for d in DILATIONS]
    res = pl.pallas_call(
        body, name=name, grid=(T // tm,),
        in_specs=[pl.BlockSpec((tm, GROUP_W), lambda i: (i, 0))] + specs + specs
        + [pl.BlockSpec((GROUP_W, GROUP_W), lambda i: (0, 0))],
        out_specs=specs + specs,
        out_shape=[jax.ShapeDtypeStruct(o.shape, BF16) for o in outs] + [jax.ShapeDtypeStruct(o.shape, F32) for o in outs],
        scratch_shapes=[_residue_scratch(tm, GROUP_W)] * n_scr,
        compiler_params=_params(("parallel",)),
    )(dcomb, *outs, *lses, seg)
    return res[:ng], res[ng:]


def _position():
    return lax.axis_index("x"), lax.axis_index("y"), lax.axis_index("c")


def _other_chips(x, y):
    return [(1 - x, y), (x, 1 - y), (1 - x, 1 - y)]


def _remote(src, dst, send_sems, recv_sems, k, to):
    return pltpu.make_async_remote_copy(src_ref=src, dst_ref=dst, send_sem=send_sems.at[k], recv_sem=recv_sems.at[k],
                                        device_id=to, device_id_type=MESH)


def _gather_descriptors(ins, outs, send_sems, recv_sems, local_sems):
    n = len(ins)
    x, y, c = _position()
    sibling = (x, y, 1 - c)
    chips = _other_chips(x, y)

    def block(a, px, py, pc):
        return outs[a].at[4 * px + 2 * py + pc]

    own, first, arrivals = [], [], []
    for a in range(n):
        k0 = 7 * a
        mine = block(a, x, y, c)
        own.append(pltpu.make_async_copy(ins[a], mine, local_sems.at[a]))
        first.append(_remote(ins[a], mine, send_sems, recv_sems, k0, sibling))
        row = []
        for j, chip in enumerate(chips):
            first.append(_remote(ins[a], mine, send_sems, recv_sems, k0 + 1 + j, (*chip, c)))
            got = block(a, *chip, c)
            row.append((_remote(got, got, send_sems, recv_sems, k0 + 1 + j, sibling),
                        _remote(got, got, send_sems, recv_sems, k0 + 4 + j, sibling)))
        arrivals.append(row)
    return own, first, arrivals


def _gather_begin(ins, outs, send_sems, recv_sems, local_sems):
    own, first, _ = _gather_descriptors(ins, outs, send_sems, recv_sems, local_sems)
    for cp in own + first:
        cp.start()


def _gather_finish(ins, outs, send_sems, recv_sems, local_sems):
    own, first, arrivals = _gather_descriptors(ins, outs, send_sems, recv_sems, local_sems)
    passed = []
    for row in arrivals:
        for arrived, onward in row:
            arrived.wait_recv()
            onward.start()
            passed.append(onward)
    for a in range(len(ins)):
        first[4 * a].wait_recv()
        for _, onward in arrivals[a]:
            onward.wait_recv()
    for cp in first + passed:
        cp.wait_send()
    for cp in own:
        cp.wait()


def _gather_scratch(n):
    return [pltpu.SemaphoreType.DMA((7 * n,)), pltpu.SemaphoreType.DMA((7 * n,)), pltpu.SemaphoreType.DMA((n,))]


_HBM = pl.BlockSpec(memory_space=pltpu.HBM)
_SEM = pl.BlockSpec(memory_space=pltpu.SEMAPHORE)
_DATAFLOW = pltpu.SideEffectType.DATAFLOW_SIDE_EFFECTING


def _to_all_plan(srcs, lands, send_sems, recv_sems):
    x, y, c = _position()
    me = 4 * x + 2 * y + c
    copies = []
    for a in range(len(srcs)):
        for k in range(1, N_DEV):
            fx, fy, fc = (k >> 2) & 1, (k >> 1) & 1, k & 1
            to = (1 - x if fx else x, 1 - y if fy else y, 1 - c if fc else c)
            copies.append(_remote(srcs[a], lands[a].at[me], send_sems, recv_sems, (N_DEV - 1) * a + k - 1, to))
    return copies


def _to_sibling_plan(srcs, lands, send_sems, recv_sems):
    x, y, c = _position()
    copies = []
    for a in range(len(srcs)):
        for q in range(4):
            copies.append(_remote(srcs[a].at[2 * q + (1 - c)], lands[a].at[q], send_sems, recv_sems, 4 * a + q,
                                  (x, y, 1 - c)))
    return copies


def _to_chips_plan(srcs, lands, send_sems, recv_sems):
    x, y, c = _position()
    copies = []
    for a in range(len(srcs)):
        for j, (cx, cy) in enumerate(_other_chips(x, y)):
            copies.append(_remote(srcs[a].at[2 * cx + cy], lands[a].at[j], send_sems, recv_sems, 3 * a + j, (cx, cy, c)))
    return copies


def copies_start(srcs, land_shapes, plan, per_array, name):
    n = len(srcs)
    n_sem = per_array * n
    lands = [lax.empty(s.shape, s.dtype) for s in land_shapes]

    def body(*refs):
        src_refs, land_refs = refs[:n], refs[n:2 * n]
        send_sems, recv_sems = refs[2 * n], refs[2 * n + 1]
        token = refs[-1]
        for cp in plan(src_refs, land_refs, send_sems, recv_sems):
            cp.start()
        token[...] = jnp.zeros_like(token)

    out = pl.pallas_call(
        body, name=name,
        out_shape=(pltpu.SemaphoreType.DMA((n_sem,)), pltpu.SemaphoreType.DMA((n_sem,)))
        + tuple(pltpu.HBM(s.shape, s.dtype) for s in srcs)
        + tuple(pltpu.HBM(s.shape, s.dtype) for s in land_shapes)
        + (jax.ShapeDtypeStruct((8, LANES), F32),),
        in_specs=[_HBM] * (2 * n),
        out_specs=(_SEM, _SEM) + (_HBM,) * (2 * n) + (pl.BlockSpec(memory_space=pltpu.VMEM),),
        input_output_aliases={i: 2 + i for i in range(2 * n)},
        compiler_params=pltpu.CompilerParams(has_side_effects=_DATAFLOW),
    )(*[pltpu.with_memory_space_constraint(s, pltpu.HBM) for s in srcs],
      *[pltpu.with_memory_space_constraint(l, pltpu.HBM) for l in lands])
    return out[:-1], out[-1]


def copies_wait(handles, plan, after, name):
    send_sems, recv_sems = handles[0], handles[1]
    n = (len(handles) - 2) // 2
    thru = handles[2:]

    def body(*refs):
        src_refs, land_refs = refs[:n], refs[n:2 * n]
        send_sems, recv_sems = refs[2 * n], refs[2 * n + 1]
        copies = plan(src_refs, land_refs, send_sems, recv_sems)
        for cp in copies:
            cp.wait_recv()
        for cp in copies:
            cp.wait_send()

    out = pl.pallas_call(
        body, name=name,
        out_shape=tuple(pltpu.HBM(t.shape, t.dtype) for t in thru),
        in_specs=[_HBM] * (2 * n) + [_SEM, _SEM, pl.BlockSpec(memory_space=pl.ANY)],
        out_specs=(_HBM,) * (2 * n),
        input_output_aliases={i: i for i in range(2 * n)},
        compiler_params=pltpu.CompilerParams(has_side_effects=_DATAFLOW),
    )(*thru, send_sems, recv_sems, after)
    return out[:n], out[n:]


def all_sum_small(vec, name):
    R = vec.shape[0]

    def body(v_ref, tot_ref, all_ref, send_sems, recv_sems):
        x, y, c = _position()
        me = 4 * x + 2 * y + c
        all_ref[me] = v_ref[...]
        copies = []
        for k in range(1, N_DEV):
            fx, fy, fc = (k >> 2) & 1, (k >> 1) & 1, k & 1
            to = (1 - x if fx else x, 1 - y if fy else y, 1 - c if fc else c)
            cp = _remote(v_ref, all_ref.at[me], send_sems, recv_sems, k - 1, to)
            cp.start()
            copies.append(cp)
        for cp in copies:
            cp.wait_recv()
        for cp in copies:
            cp.wait_send()
        tot = all_ref[0]
        for j in range(1, N_DEV):
            tot = tot + all_ref[j]
        tot_ref[...] = tot

    vmem = pl.BlockSpec(memory_space=pltpu.VMEM)
    return pl.pallas_call(
        body, name=name,
        in_specs=[vmem], out_specs=vmem,
        out_shape=jax.ShapeDtypeStruct((R, LANES), F32),
        scratch_shapes=[pltpu.VMEM((N_DEV, R, LANES), F32),
                        pltpu.SemaphoreType.DMA((N_DEV - 1,)), pltpu.SemaphoreType.DMA((N_DEV - 1,))],
        compiler_params=pltpu.CompilerParams(vmem_limit_bytes=VMEM_LIMIT),
    )(vec)


def pair_add(parts, theirs, place, name):
    _, R, C = theirs.shape
    tr = _pick(R, 256, 8)

    def body(place_ref, a_ref, b_ref, o_ref):
        o_ref[...] = (a_ref[...].astype(F32) + b_ref[...].astype(F32)).astype(BF16)

    blk = pl.BlockSpec((None, tr, C), lambda q, i, place_ref: (q, i, 0))
    return pl.pallas_call(
        body, name=name,
        grid_spec=pltpu.PrefetchScalarGridSpec(
            num_scalar_prefetch=1, grid=(4, R // tr),
            in_specs=[pl.BlockSpec((None, tr, C), lambda q, i, place_ref: (2 * q + place_ref[2], i, 0)), blk],
            out_specs=blk),
        out_shape=jax.ShapeDtypeStruct(theirs.shape, BF16),
        compiler_params=_params(("parallel", "parallel")),
    )(place, parts, theirs)


def _adamw_math(w, g, m, v):
    m = ADAM_B1 * m + (1.0 - ADAM_B1) * g
    v = ADAM_B2 * v + (1.0 - ADAM_B2) * jnp.square(g)
    m_hat = m / (1.0 - ADAM_B1 ** ADAM_STEP)
    v_hat = v / (1.0 - ADAM_B2 ** ADAM_STEP)
    delta = -ADAM_LR * (m_hat / (jnp.sqrt(v_hat) + ADAM_EPS) + ADAM_WD * w)
    return delta, m, v


def adamw_sharded(w, m, v, parts, sib, others, place, name):
    R, C = w.shape
    tr = _pick(R, 256, 8)

    def body(place_ref, w_ref, m_ref, v_ref, a_ref, b_ref, o_ref, g_ref, d_ref, nm_ref, nv_ref):
        g = a_ref[...].astype(F32) + b_ref[...].astype(F32)
        for j in range(3):
            g = g + o_ref[j].astype(F32)
        delta, nm, nv = _adamw_math(w_ref[...], g, m_ref[...], v_ref[...])
        g_ref[...] = g
        d_ref[...] = delta
        nm_ref[...] = nm
        nv_ref[...] = nv

    row = pl.BlockSpec((tr, C), lambda i, place_ref: (i, 0))
    return pl.pallas_call(
        body, name=name,
        grid_spec=pltpu.PrefetchScalarGridSpec(
            num_scalar_prefetch=1, grid=(R // tr,),
            in_specs=[row] * 3 + [pl.BlockSpec((None, tr, C), lambda i, place_ref: (place_ref[0], i, 0)),
                                  pl.BlockSpec((None, tr, C), lambda i, place_ref: (place_ref[1], i, 0)),
                                  pl.BlockSpec((3, tr, C), lambda i, place_ref: (0, i, 0))],
            out_specs=[row] * 4),
        out_shape=[jax.ShapeDtypeStruct((R, C), F32)] * 4,
        compiler_params=_params(("parallel",)),
    )(place, w, m, v, parts, sib, others)


def adamw_packed(w, g, m, v, name):
    R = w.shape[0]

    def body(w_ref, g_ref, m_ref, v_ref, d_ref, nm_ref, nv_ref):
        delta, nm, nv = _adamw_math(w_ref[...], g_ref[...], m_ref[...], v_ref[...])
        d_ref[...] = delta
        nm_ref[...] = nm
        nv_ref[...] = nv

    full = pl.BlockSpec((R, LANES), lambda i: (0, 0))
    return pl.pallas_call(
        body, name=name, grid=(1,),
        in_specs=[full] * 4, out_specs=[full] * 3,
        out_shape=[jax.ShapeDtypeStruct((R, LANES), F32)] * 3,
        compiler_params=_params(("arbitrary",)),
    )(w, g, m, v)


def _pack(arrays):
    flat = []
    sizes = []
    for a in arrays:
        f = a.reshape(-1).astype(F32)
        pad = (-f.shape[0]) % LANES
        if pad:
            f = jnp.concatenate([f, jnp.zeros((pad,), F32)])
        flat.append(f)
        sizes.append(f.shape[0])
    rows = sum(sizes) // LANES
    pad_rows = (-rows) % 8
    if pad_rows:
        flat.append(jnp.zeros((pad_rows * LANES,), F32))
    return jnp.concatenate(flat).reshape(-1, LANES), sizes


def _unpack(packed, sizes, shapes):
    flat = packed.reshape(-1)
    out = []
    off = 0
    for size, shape in zip(sizes, shapes):
        n = int(np.prod(shape))
        out.append(flat[off:off + n].reshape(shape))
        off += size
    return out


def _to_blocks(full, axis):
    if axis == 0:
        return full.reshape(N_DEV, full.shape[0] // N_DEV, full.shape[1])
    r, n = full.shape
    return full.reshape(r, N_DEV, n // N_DEV).transpose(1, 0, 2)


def _from_blocks(blocks, axis):
    if axis == 0:
        return blocks.reshape(blocks.shape[0] * blocks.shape[1], blocks.shape[2])
    return blocks.transpose(1, 0, 2).reshape(blocks.shape[1], blocks.shape[0] * blocks.shape[2])


def kernel(x, ln0_g, ln0_b, w_in, b_in, conv_w, w_a, w_b, w_o, b_o, ln1_g, ln1_b, w_up, b_up, ffn_conv_w, ffn_conv_b, w_down, b_down, ln2_g, ln2_b, loss_target, m_ln0_g, m_ln0_b, m_w_in, m_b_in, m_conv_w, m_w_a, m_w_b, m_w_o, m_b_o, m_ln1_g, m_ln1_b, m_w_up, m_b_up, m_ffn_conv_w, m_ffn_conv_b, m_w_down, m_b_down, m_ln2_g, m_ln2_b, v_ln0_g, v_ln0_b, v_w_in, v_b_in, v_conv_w, v_w_a, v_w_b, v_w_o, v_b_o, v_ln1_g, v_ln1_b, v_w_up, v_b_up, v_ffn_conv_w, v_ffn_conv_b, v_w_down, v_b_down, v_ln2_g, v_ln2_b):
    T, D = x.shape[1], x.shape[2]
    F = ffn_conv_b.shape[-1]
    xs = x.reshape(T, D)
    tgt = loss_target.reshape(T, D)
    dev = 4 * lax.axis_index("x") + 2 * lax.axis_index("y") + lax.axis_index("c")
    chip = 2 * lax.axis_index("x") + lax.axis_index("y")
    core = lax.axis_index("c")
    place = jnp.stack([dev, chip, core]).astype(jnp.int32)

    big = dict(w_in=(w_in[0], 1), w_a=(w_a[0], 0), w_b=(w_b[0], 1), w_o=(w_o[0], 0), w_up=(w_up[0], 1),
               w_down=(w_down[0], 0))
    names = list(big)
    shards = {k: big[k][0].astype(BF16) for k in names}
    ln0g, ln0b = ln0_g.reshape(1, D), ln0_b.reshape(1, D)
    h0, h0b, *rest = ln_fwd(xs, None, ln0g, ln0b, "ln0_fwd_gather_w_in", dilations=DILATIONS[1:],
                            gather=[shards["w_in"], conv_w[0], ffn_conv_w[0]])
    h0_res = [h0b] + [h.reshape(T, D) for h in rest[:2]]
    g_in, g_conv, g_fcw = rest[2:]
    full = {"w_in": _from_blocks(g_in, 1)}
    conv_full = _from_blocks(g_conv, 1)
    fcw_full = _from_blocks(g_fcw, 1)
    late_groups = (("w_a", "w_b", "w_o"), ("w_up", "w_down"))
    late_handles = []
    token = conv_full[:1, :1] * 0.0
    for n, keys in enumerate(late_groups):
        srcs = [shards[k] + token[0, 0].astype(BF16) for k in keys]
        handles, token = copies_start(srcs, [jax.ShapeDtypeStruct((N_DEV,) + s.shape, BF16) for s in srcs],
                                      _to_all_plan, N_DEV - 1, f"gather_late_{n}_start")
        late_handles.append(handles)

    def late_weights(n, after):
        _, lands = copies_wait(late_handles[n], _to_all_plan, after, f"gather_late_{n}_wait")
        for k, land in zip(late_groups[n], lands):
            full[k] = _from_blocks(lax.dynamic_update_index_in_dim(land, shards[k], dev, 0), big[k][1])

    o_q = 3 * D
    o_g = o_q + 3 * QKV_W
    w_pa, w_qkv, w_pg = full["w_in"][:, :o_q], full["w_in"][:, o_q:o_g], full["w_in"][:, o_g:]
    b_pa, b_qkv, b_pg = b_in[:, :o_q], b_in[:, o_q:o_g], b_in[:, o_g:]

    proj_a = mm_nn(h0b, w_pa, b_pa, ACT, "proj_conv", after=token)
    proj_g = mm_nn(h0b, w_pg, b_pg, ACT, "proj_gates")
    zero_d = jnp.zeros((1, D), F32)
    s_a = conv_a_fwd(proj_a, conv_full, "conv_a_fwd")
    late_weights(0, s_a)
    y_a = mm_nn(s_a, full["w_a"], zero_d, ACT, "branch_a_out")

    def group_cols(m, g):
        return jnp.concatenate([m[:, s * QKV_W + g * GROUP_W:s * QKV_W + (g + 1) * GROUP_W] for s in range(3)], 1)

    w_grp = [group_cols(w_qkv, g) for g in range(3)]
    qkvs, outs, lses = [], [], []
    for g, d in enumerate(DILATIONS):
        qkv = mm_nn(h0_res[g], w_grp[g], group_cols(b_qkv, g), BF16, f"proj_qkv_{g}").reshape(d, T // d, 3 * GROUP_W)
        o, l = att_fwd(qkv, g, f"att_fwd_{g}")
        qkvs.append(qkv)
        outs.append(o)
        lses.append(l)
    comb = combine_fwd(outs, lses, "combine_fwd")
    y_b = mm_nn(comb, full["w_b"], zero_d, ACT, "branch_b_out")
    z = gate_fwd(proj_g, y_a, y_b, "gate_fwd")
    h1, h1b, mix = ln_fwd(h0, ("nn", z, full["w_o"], b_o), ln1_g, ln1_b, "mix_out_ln1_fwd")
    late_weights(1, h1b)
    up, f_act = ffn_up_conv_f(h1b, full["w_up"], b_up, fcw_full, ffn_conv_b, "ffn_up_conv_f")

    dr2, dr2b, d_ln2_g, d_ln2_b, d_b_down, loss_part = ln_bwd(
        h1, ("nn", f_act, full["w_down"], b_down), ln2_g, ln2_b, None, None, tgt, "ffn_down_ln2_loss_bwd")
    dw_down, _ = mm_tn(f_act, dr2b, "dw_down")
    d_a, d_gate, cs_a, cs_gate, d_fcb, d_fcw = conv_f_bwd(dr2b, full["w_down"], up, fcw_full, ffn_conv_b,
                                                          "d_ffn_act_conv_f_bwd")
    dw_up_a, _ = mm_tn(h1b, d_a, "dw_up_a")
    dw_up_g, _ = mm_tn(h1b, d_gate, "dw_up_gate")
    dr1, dr1b, d_ln1_g, d_ln1_b, d_b_o, _ = ln_bwd(h0, mix, ln1_g, ln1_b, dr2, ("nt", [d_a, d_gate], full["w_up"]), None,
                                                   "d_h1_ln1_bwd")
    dw_o, _ = mm_tn(z, dr1b, "dw_o")
    dz = mm_nt(dr1b, full["w_o"], None, "d_z", out_dtype=ACT)
    dy_a, dy_b, dproj_g = gate_bwd(dz, proj_g, y_a, y_b, "gate_bwd")
    dw_a, _ = mm_tn(s_a, dy_a, "dw_a")
    ds_a = mm_nt(dy_a, full["w_a"], None, "d_s_a", out_dtype=ACT)
    dproj_a, d_conv = conv_a_bwd(ds_a, proj_a, conv_full, "conv_a_bwd")
    dw_b, _ = mm_tn(comb, dy_b, "dw_b")

    rs_mine, rs_sib, rs_handles = {}, {}, {}

    sib_handles = {}

    def to_sibling_start(keys, grads, tag):
        parts = [_to_blocks(grads[k], big[k][1]) for k in keys]
        handles, tok = copies_start(parts, [jax.ShapeDtypeStruct((4,) + p.shape[1:], BF16) for p in parts],
                                    _to_sibling_plan, 4, f"grads_to_sibling_{tag}_start")
        sib_handles[tag] = (keys, handles)
        return tok

    def to_chips_start(tag, after):
        keys, handles = sib_handles[tag]
        parts, from_sib = copies_wait(handles, _to_sibling_plan, after, f"grads_to_sibling_{tag}_wait")
        sums = [pair_add(a, b, place, f"chip_sum_{k}") for k, a, b in zip(keys, parts, from_sib)]
        handles, tok = copies_start(sums, [jax.ShapeDtypeStruct((3,) + s.shape[1:], BF16) for s in sums],
                                    _to_chips_plan, 3, f"grads_to_chips_{tag}_start")
        for k, a, b in zip(keys, parts, from_sib):
            rs_mine[k], rs_sib[k] = a, b
        rs_handles[tag] = (keys, handles)
        return tok

    tok_a = to_sibling_start(("w_a", "w_b", "w_o", "w_up", "w_down"),
                             dict(w_a=dw_a, w_b=dw_b, w_o=dw_o, w_up=jnp.concatenate([dw_up_a, dw_up_g], 1),
                                  w_down=dw_down), "a")
    dcomb = mm_nt(dy_b, full["w_b"], None, "d_comb", after=tok_a, out_dtype=ACT)
    dos, dms = combine_bwd(dcomb, outs, lses, "combine_bwd")
    tok_a = to_chips_start("a", dms[0])
    dw_grp, cs_grp, dqkvs = [], [], []
    for g, d in enumerate(DILATIONS):
        dq, dk, dv = att_bwd(qkvs[g], dos[g], lses[g], dms[g], g, f"att_bwd_{g}", after=tok_a if g == 0 else None)
        dqkv = [t.reshape(T, GROUP_W) for t in (dq, dk, dv)]
        dwg, csg = mm_tn(h0_res[g], dqkv, f"dw_in_qkv_{g}")
        dqkvs.append(dqkv)
        dw_grp.append(dwg)
        cs_grp.append(csg)
    dw_pa, cs_pa = mm_tn(h0b, dproj_a, "dw_in_conv")
    dw_pg, cs_pg = mm_tn(h0b, dproj_g, "dw_in_gates")

    def ungroup(parts):
        return jnp.concatenate([p[:, s * GROUP_W:(s + 1) * GROUP_W] for s in range(3) for p in parts], 1)

    db_in_parts = [cs_pa, ungroup(cs_grp), cs_pg]
    tok_b = to_sibling_start(("w_in",), dict(w_in=jnp.concatenate([dw_pa, ungroup(dw_grp), dw_pg], 1)), "b")
    dh0 = mm_nt(dproj_a, w_pa, None, "d_h0_conv", after=tok_b)
    tok_b = to_chips_start("b", dh0)
    dh0 = mm_nt(dproj_g, w_pg, dh0, "d_h0_gates", after=tok_b)
    dh0_res = [(mm_nt(dqkvs[g], w_grp[g], None, f"d_h0_qkv_{g}").reshape(d, T // d, D), d)
               for g, d in enumerate(DILATIONS) if g > 0]
    dx, _, d_ln0_g, d_ln0_b, _, _ = ln_bwd(xs, None, ln0g, ln0b, dr1, ("nt", dqkvs[0], w_grp[0]), None, "d_h0_ln0_bwd",
                                           by_residue=[(dh0.reshape(1, T, D), 1)] + dh0_res)

    small = [d_ln0_g, d_ln0_b, jnp.concatenate(db_in_parts, 1), d_conv, d_b_o, d_ln1_g, d_ln1_b,
             jnp.concatenate([cs_a, cs_gate], 1), d_fcw, d_fcb, d_b_down, d_ln2_g, d_ln2_b, loss_part]
    packed, sizes = _pack(small)
    total = all_sum_small(packed, "sum_small")
    (g_ln0_g, g_ln0_b, g_b_in, g_conv_full, g_b_o, g_ln1_g, g_ln1_b, g_b_up, g_fcw_full, g_fcb, g_b_down, g_ln2_g,
     g_ln2_b, loss) = _unpack(total, sizes, [a.shape for a in small])
    cw = conv_w.shape[-1]
    fw = ffn_conv_w.shape[-1]
    g_conv = lax.dynamic_slice_in_dim(g_conv_full, dev * cw, cw, 1)
    g_fcw = lax.dynamic_slice_in_dim(g_fcw_full, dev * fw, fw, 1)

    from_chips = {}
    for tag, (keys, handles) in rs_handles.items():
        _, lands = copies_wait(handles, _to_chips_plan, total, f"grads_to_chips_{tag}_wait")
        from_chips.update(zip(keys, lands))

    moments = dict(w_in=(m_w_in, v_w_in), w_a=(m_w_a, v_w_a), w_b=(m_w_b, v_w_b), w_o=(m_w_o, v_w_o),
                   w_up=(m_w_up, v_w_up), w_down=(m_w_down, v_w_down))
    res_big = {}
    for k in names:
        res_big[k] = adamw_sharded(big[k][0], moments[k][0][0], moments[k][1][0], rs_mine[k], rs_sib[k], from_chips[k],
                                   place, f"adamw_{k}")

    small_names = ["ln0_g", "ln0_b", "b_in", "conv_w", "b_o", "ln1_g", "ln1_b", "b_up", "ffn_conv_w", "ffn_conv_b",
                   "b_down", "ln2_g", "ln2_b"]
    small_w = [ln0_g, ln0_b, b_in, conv_w, b_o, ln1_g, ln1_b, b_up, ffn_conv_w, ffn_conv_b, b_down, ln2_g, ln2_b]
    small_m = [m_ln0_g, m_ln0_b, m_b_in, m_conv_w, m_b_o, m_ln1_g, m_ln1_b, m_b_up, m_ffn_conv_w, m_ffn_conv_b,
               m_b_down, m_ln2_g, m_ln2_b]
    small_v = [v_ln0_g, v_ln0_b, v_b_in, v_conv_w, v_b_o, v_ln1_g, v_ln1_b, v_b_up, v_ffn_conv_w, v_ffn_conv_b,
               v_b_down, v_ln2_g, v_ln2_b]
    small_g = [g_ln0_g, g_ln0_b, g_b_in, g_conv, g_b_o, g_ln1_g, g_ln1_b, g_b_up, g_fcw, g_fcb, g_b_down, g_ln2_g,
               g_ln2_b]
    shapes = [w.shape for w in small_w]
    small_g = [g.reshape(s) for g, s in zip(small_g, shapes)]
    pw, psz = _pack(small_w)
    pg, _ = _pack(small_g)
    pm, _ = _pack(small_m)
    pv, _ = _pack(small_v)
    pd, pnm, pnv = adamw_packed(pw, pg, pm, pv, "adamw_small")
    res_small = {k: (g, d_, m_, v_) for k, g, d_, m_, v_ in zip(
        small_names, small_g, _unpack(pd, psz, shapes), _unpack(pnm, psz, shapes), _unpack(pnv, psz, shapes))}

    order = ["ln0_g", "ln0_b", "w_in", "b_in", "conv_w", "w_a", "w_b", "w_o", "b_o", "ln1_g", "ln1_b", "w_up", "b_up",
             "ffn_conv_w", "ffn_conv_b", "w_down", "b_down", "ln2_g", "ln2_b"]

    def result(k, j):
        if k in res_big:
            return res_big[k][j][None]
        return res_small[k][j]

    out = [loss.reshape(()), dx.reshape(x.shape)]
    for j in range(4):
        out += [result(k, j) for k in order]
    return tuple(out)
```

```python
import math

import numpy as np
import jax
import jax.numpy as jnp
from jax import lax
from jax.experimental import pallas as pl
from jax.experimental.pallas import tpu as pltpu

F32 = jnp.float32
BF16 = jnp.bfloat16
ACT = BF16

N_DEV = 8
LN_EPS = 1e-5
ALPHA = (2.0 * 1) ** 0.25
HEAD_DIM = 64
GROUP_W = 512
QKV_W = 3 * GROUP_W
DILATIONS = (1, 4, 16)
RADIUS = 64
LANES = 128
HALO = 8
HALO_BF16 = 16
ATT_TQ = 128

ADAM_LR = 0.001
ADAM_B1 = 0.9
ADAM_B2 = 0.999
ADAM_EPS = 1e-08
ADAM_WD = 0.01
ADAM_STEP = 10

VMEM_LIMIT = 52 * 1024 * 1024
OUT_TILE_BYTES = 8 * 1024 * 1024
MESH = pl.DeviceIdType.MESH
NT_DIMS = (((1,), (1,)), ((), ()))
TN_DIMS = (((0,), (0,)), ((), ()))


def _pick(n, target, align=LANES):
    if n <= target:
        return n
    best = None
    for t in range(align, target + 1, align):
        if n % t == 0:
            best = t
    assert best is not None, (n, target, align)
    return best


def _params(sems=None):
    return pltpu.CompilerParams(dimension_semantics=sems, vmem_limit_bytes=VMEM_LIMIT)


def _alibi_slopes():
    n = 3 * 8
    return np.exp2(-8.0 * np.arange(1, n + 1, dtype=np.float64) / n).astype(np.float32).reshape(3, 8)


def _ln_stats(r):
    mu = jnp.mean(r, -1, keepdims=True)
    xc = r - mu
    var = jnp.mean(xc * xc, -1, keepdims=True)
    rstd = lax.rsqrt(var + LN_EPS)
    return xc, rstd


def _load_natural(ref, d, scr):
    if d == 1:
        return ref[0]
    n, C = ref.shape[1], ref.shape[2]
    for c in range(C // LANES):
        for r in range(d):
            scr[c, pl.ds(r, n, stride=d), :] = ref[r, :, c * LANES:(c + 1) * LANES]
    return jnp.concatenate([scr[c] for c in range(C // LANES)], axis=1)


def _store_by_residue(val, ref, d, scr):
    if d == 1:
        ref[0] = val.astype(ref.dtype)
        return
    n, C = ref.shape[1], ref.shape[2]
    for c in range(C // LANES):
        scr[c] = val[:, c * LANES:(c + 1) * LANES]
    for c in range(C // LANES):
        for r in range(d):
            ref[r, :, c * LANES:(c + 1) * LANES] = scr[c, pl.ds(r, n, stride=d), :].astype(ref.dtype)


def _residue_spec(tm, d, C):
    return pl.BlockSpec((d, tm // d, C), lambda i: (0, i, 0))


def _residue_scratch(tm, C):
    return pltpu.VMEM((C // LANES, tm, LANES), F32)


def ln_fwd(a, res, g, b, name, dilations=(), gather=()):
    T, D = a.shape
    res_mm = isinstance(res, tuple)
    tm = _pick(T, 256 if res_mm else 512, 8)
    res_ins = list(res[1:]) if res_mm else ([] if res is None else [res])
    nd = len(dilations)
    ng = len(gather)
    n_in = 1 + len(res_ins) + 2
    last = T // tm - 1

    def body(*refs):
        a_ref = refs[0]
        r = a_ref[...]
        if res_mm:
            res_val = jnp.dot(refs[1][...], refs[2][...], preferred_element_type=F32) + refs[3][...]
            refs[-1 - n_scratch][...] = res_val
            r = ALPHA * r + res_val
        elif res_ins:
            r = ALPHA * r + refs[1][...]
        g_ref, b_ref = refs[n_in - 2], refs[n_in - 1]
        shard_refs = refs[n_in:n_in + ng]
        h_ref, hb_ref = refs[n_in + ng], refs[n_in + ng + 1]
        p_refs = refs[n_in + ng + 2:n_in + ng + 2 + nd]
        full_refs = refs[n_in + ng + 2 + nd:n_in + 2 * ng + 2 + nd]
        scratch = refs[len(refs) - n_scratch:]
        sems = scratch[len(scratch) - 3:] if ng else ()

        if ng:
            @pl.when(pl.program_id(0) == 0)
            def _():
                _gather_begin(shard_refs, full_refs, *sems)

        xc, rstd = _ln_stats(r)
        h = xc * rstd * g_ref[...] + b_ref[...]
        h_ref[...] = h
        hb_ref[...] = h.astype(BF16)
        for d, p_ref in zip(dilations, p_refs):
            _store_by_residue(h, p_ref, d, scratch[0])

        if ng:
            @pl.when(pl.program_id(0) == last)
            def _():
                _gather_finish(shard_refs, full_refs, *sems)

    row = pl.BlockSpec((tm, D), lambda i: (i, 0))
    vec = pl.BlockSpec((1, D), lambda i: (0, 0))
    hbm = pl.BlockSpec(memory_space=pl.ANY)
    if res_mm:
        res_specs = [pl.BlockSpec((tm, res[1].shape[1]), lambda i: (i, 0)), pl.BlockSpec(res[2].shape, lambda i: (0, 0)), vec]
    else:
        res_specs = [row] * len(res_ins)
    scratch_shapes = ([_residue_scratch(tm, D)] if nd else []) + (_gather_scratch(ng) if ng else [])
    n_scratch = len(scratch_shapes)
    ins = [a] + res_ins + [g, b] + list(gather)
    return pl.pallas_call(
        body, name=name, grid=(T // tm,),
        in_specs=[row] + res_specs + [vec, vec] + [hbm] * ng,
        out_specs=[row, row] + [_residue_spec(tm, d, D) for d in dilations] + [hbm] * ng + ([row] if res_mm else []),
        out_shape=[jax.ShapeDtypeStruct((T, D), F32), jax.ShapeDtypeStruct((T, D), BF16)]
        + [jax.ShapeDtypeStruct((d, T // d, D), BF16) for d in dilations]
        + [jax.ShapeDtypeStruct((N_DEV,) + s.shape, s.dtype) for s in gather]
        + ([jax.ShapeDtypeStruct((T, D), F32)] if res_mm else []),
        scratch_shapes=scratch_shapes,
        compiler_params=_params(("arbitrary",) if ng else ("parallel",)),
    )(*ins)


def ln_bwd(a, res, g, b, d1, d2, tgt, name, by_residue=()):
    T, D = a.shape
    tm = _pick(T, 256, 8)
    loss_mode = tgt is not None
    nres = len(by_residue)
    row = pl.BlockSpec((tm, D), lambda i: (i, 0))
    vec = pl.BlockSpec((1, D), lambda i: (0, 0))
    one = pl.BlockSpec((1, 1), lambda i: (0, 0))

    def rows_of(x):
        return pl.BlockSpec((tm, x.shape[1]), lambda i: (i, 0))

    def whole(x):
        return pl.BlockSpec(x.shape, lambda i: (0, 0))

    ins, in_specs, slots = [], [], {}

    def operand(key, arrays, specs):
        slots[key] = (len(ins), len(arrays))
        ins.extend(arrays)
        in_specs.extend(specs)

    operand("a", [a], [row])
    if isinstance(res, tuple):
        _, x, w, bias = res
        operand("res_mm", [x, w, bias], [rows_of(x), whole(w), vec])
    elif res is not None:
        operand("res", [res], [row])
    operand("gb", [g, b], [vec, vec])
    if loss_mode:
        operand("tgt", [tgt], [row])
    else:
        operand("d1", [d1], [row])
        if isinstance(d2, tuple):
            _, pieces, w = d2
            operand("d2_mm", list(pieces) + [w], [rows_of(p) for p in pieces] + [whole(w)])
        else:
            operand("d2", [d2], [row])
    operand("by_residue", [e for e, _ in by_residue], [_residue_spec(tm, d, D) for _, d in by_residue])
    n_in = len(ins)

    def body(*refs):
        def get(key):
            first, count = slots[key]
            return refs[first:first + count]

        dr_ref, drb_ref, dg_ref, db_ref, ds_ref, loss_ref = refs[n_in:n_in + 6]
        i = pl.program_id(0)

        @pl.when(i == 0)
        def _():
            dg_ref[...] = jnp.zeros_like(dg_ref)
            db_ref[...] = jnp.zeros_like(db_ref)
            ds_ref[...] = jnp.zeros_like(ds_ref)
            loss_ref[...] = jnp.zeros_like(loss_ref)

        r = get("a")[0][...]
        if "res_mm" in slots:
            x_ref, w_ref, bias_ref = get("res_mm")
            r = ALPHA * r + (jnp.dot(x_ref[...], w_ref[...], preferred_element_type=F32) + bias_ref[...])
        elif "res" in slots:
            r = ALPHA * r + get("res")[0][...]
        g_ref, b_ref = get("gb")
        xc, rstd = _ln_stats(r)
        xhat = xc * rstd
        gam = g_ref[...]
        if loss_mode:
            err = xhat * gam + b_ref[...] - get("tgt")[0][...]
            dy = err * (1.0 / D)
            row_loss = jnp.mean(err * err, -1, keepdims=True)
            loss_ref[...] += 0.5 * jnp.sum(row_loss, 0, keepdims=True)
        else:
            if "d2_mm" in slots:
                *p_refs, w_ref = get("d2_mm")
                av = p_refs[0][...] if len(p_refs) == 1 else jnp.concatenate([p[...] for p in p_refs], axis=1)
                d2v = lax.dot_general(av, w_ref[...], NT_DIMS, preferred_element_type=F32)
            else:
                d2v = get("d2")[0][...]
            dy = ALPHA * get("d1")[0][...] + d2v
        for (_, d), e_ref in zip(by_residue, get("by_residue")):
            dy = dy + _load_natural(e_ref, d, refs[-1])
        dyg = dy * gam
        c1 = jnp.mean(dyg, -1, keepdims=True)
        c2 = jnp.mean(dyg * xhat, -1, keepdims=True)
        dr = rstd * (dyg - c1 - xhat * c2)
        dr_ref[...] = dr
        drb_ref[...] = dr.astype(BF16)
        dg_ref[...] += jnp.sum(dy * xhat, 0, keepdims=True)
        db_ref[...] += jnp.sum(dy, 0, keepdims=True)
        ds_ref[...] += jnp.sum(dr, 0, keepdims=True)

    return pl.pallas_call(
        body, name=name, grid=(T // tm,),
        in_specs=in_specs,
        out_specs=[row, row, vec, vec, vec, one],
        out_shape=[jax.ShapeDtypeStruct((T, D), F32), jax.ShapeDtypeStruct((T, D), BF16),
                   jax.ShapeDtypeStruct((1, D), F32), jax.ShapeDtypeStruct((1, D), F32),
                   jax.ShapeDtypeStruct((1, D), F32), jax.ShapeDtypeStruct((1, 1), F32)],
        scratch_shapes=[_residue_scratch(tm, D)] if nres else [],
        compiler_params=_params(("arbitrary",)),
    )(*ins)


_TOKEN_SPEC = pl.BlockSpec((8, LANES), lambda i: (0, 0))


def mm_nn(a, w, bias, out_dtype, name, after=None):
    M, K = a.shape
    N = w.shape[1]
    tm = _pick(M, max(256, min(1024, OUT_TILE_BYTES // (N * jnp.dtype(out_dtype).itemsize))), 8)
    tc = _pick(N, 512)

    def body(a_ref, w_ref, b_ref, *rest):
        o_ref = rest[-1]
        av = a_ref[...]
        for j in range(N // tc):
            cols = slice(j * tc, (j + 1) * tc)
            acc = jnp.dot(av, w_ref[:, cols], preferred_element_type=F32)
            o_ref[:, cols] = (acc + b_ref[:, cols]).astype(out_dtype)

    return pl.pallas_call(
        body, name=name, grid=(M // tm,),
        in_specs=[pl.BlockSpec((tm, K), lambda i: (i, 0)),
                  pl.BlockSpec((K, N), lambda i: (0, 0)),
                  pl.BlockSpec((1, N), lambda i: (0, 0))] + ([] if after is None else [_TOKEN_SPEC]),
        out_specs=pl.BlockSpec((tm, N), lambda i: (i, 0)),
        out_shape=jax.ShapeDtypeStruct((M, N), out_dtype),
        compiler_params=_params(("parallel",)),
    )(a, w, bias, *([] if after is None else [after]))


def mm_nt(a, w, acc_in, name, after=None, w_block=0, out_dtype=F32):
    pieces = list(a) if isinstance(a, (list, tuple)) else [a]
    M = pieces[0].shape[0]
    widths = [p.shape[1] for p in pieces]
    K = sum(widths)
    N = w.shape[0]
    tm = _pick(M, 1024, 8)
    tc = _pick(N, 512)
    has_acc = acc_in is not None
    n_a = len(pieces)

    def body(*refs):
        a_refs, w_ref = refs[:n_a], refs[n_a]
        c_ref = refs[n_a + 1] if has_acc else None
        o_ref = refs[-1]
        av = a_refs[0][...] if n_a == 1 else jnp.concatenate([r[...] for r in a_refs], axis=1)
        for j in range(N // tc):
            cols = slice(j * tc, (j + 1) * tc)
            acc = lax.dot_general(av, w_ref[cols, :], NT_DIMS, preferred_element_type=F32)
            if has_acc:
                acc = acc + c_ref[:, cols]
            o_ref[:, cols] = acc.astype(out_dtype)

    out_spec = pl.BlockSpec((tm, N), lambda i: (i, 0))
    in_specs = [pl.BlockSpec((tm, kw), lambda i: (i, 0)) for kw in widths]
    in_specs.append(pl.BlockSpec((N, K), lambda i: (0, w_block)))
    ins = pieces + [w]
    if has_acc:
        in_specs.append(out_spec)
        ins.append(acc_in)
    if after is not None:
        in_specs.append(_TOKEN_SPEC)
        ins.append(after)
    return pl.pallas_call(
        body, name=name, grid=(M // tm,),
        in_specs=in_specs, out_specs=out_spec,
        out_shape=jax.ShapeDtypeStruct((M, N), out_dtype),
        compiler_params=_params(("parallel",)),
    )(*ins)


def mm_tn(a, b, name, out_dtype=BF16):
    pieces = list(b) if isinstance(b, (list, tuple)) else [b]
    T, M = a.shape
    widths = [p.shape[1] for p in pieces]
    N = sum(widths)
    tk = _pick(T, 512, 8)
    nk = T // tk
    tc = _pick(M, 256)
    n_b = len(pieces)

    def body(*refs):
        a_ref, b_refs = refs[0], refs[1:1 + n_b]
        o_ref, cs_ref, acc_ref = refs[1 + n_b:]
        k = pl.program_id(0)

        @pl.when(k == 0)
        def _():
            acc_ref[...] = jnp.zeros_like(acc_ref)
            cs_ref[...] = jnp.zeros_like(cs_ref)

        bv = b_refs[0][...] if n_b == 1 else jnp.concatenate([r[...] for r in b_refs], axis=1)
        cs_ref[...] += jnp.sum(bv.astype(F32), 0, keepdims=True)
        for mi in range(M // tc):
            rows = slice(mi * tc, (mi + 1) * tc)
            acc_ref[rows, :] += lax.dot_general(a_ref[:, rows], bv, TN_DIMS, preferred_element_type=F32)

        @pl.when(k == nk - 1)
        def _():
            o_ref[...] = acc_ref[...].astype(out_dtype)

    return pl.pallas_call(
        body, name=name, grid=(nk,),
        in_specs=[pl.BlockSpec((tk, M), lambda k: (k, 0))] + [pl.BlockSpec((tk, wd), lambda k: (k, 0)) for wd in widths],
        out_specs=[pl.BlockSpec((M, N), lambda k: (0, 0)), pl.BlockSpec((1, N), lambda k: (0, 0))],
        out_shape=[jax.ShapeDtypeStruct((M, N), out_dtype), jax.ShapeDtypeStruct((1, N), F32)],
        scratch_shapes=[pltpu.VMEM((M, N), F32)],
        compiler_params=_params(("arbitrary",)),
    )(a, *pieces)


def _ext_rows(prev_ref, main_ref, next_ref, i, tm, T, dtype=F32):
    before = jnp.where(i == 0, 0.0, prev_ref[...])
    after = jnp.where(i == T // tm - 1, 0.0, next_ref[...])
    return jnp.concatenate([before, main_ref[...], after], axis=0).astype(dtype)


def _prev_row(x):
    return pltpu.roll(x, 1, 0)


def _next_row(x):
    return pltpu.roll(x, x.shape[0] - 1, 0)


def _conv3(u, w_ref):
    return _prev_row(u) * w_ref[0:1, :] + u * w_ref[1:2, :] + _next_row(u) * w_ref[2:3, :]


def _main(x, tm, halo=HALO):
    return x[halo:halo + tm]


def _halo_specs(tm, tc, T, col, order, halo=HALO):
    r = tm // halo
    last = T // halo - 1
    if order == "ij":
        return (pl.BlockSpec((halo, tc), lambda i, j: (jnp.maximum(i * r - 1, 0), col(j))),
                pl.BlockSpec((tm, tc), lambda i, j: (i, col(j))),
                pl.BlockSpec((halo, tc), lambda i, j: (jnp.minimum((i + 1) * r, last), col(j))))
    return (pl.BlockSpec((halo, tc), lambda j, i: (jnp.maximum(i * r - 1, 0), col(j))),
            pl.BlockSpec((tm, tc), lambda j, i: (i, col(j))),
            pl.BlockSpec((halo, tc), lambda j, i: (jnp.minimum((i + 1) * r, last), col(j))))


def conv_a_fwd(proj_a, conv_w, name):
    T, D3 = proj_a.shape
    D = D3 // 3
    tm = _pick(T, 256, 8)

    def body(p_ref, m_ref, n_ref, w_ref, o_ref):
        i = pl.program_id(0)
        ext = _ext_rows(p_ref, m_ref, n_ref, i, tm, T)
        u = ext[:, D:2 * D] * ext[:, 2 * D:]
        cu = _conv3(u, w_ref)
        o_ref[...] = (m_ref[:, :D].astype(F32) * _main(cu, tm, HALO_BF16)).astype(BF16)

    prev, main, nxt = _halo_specs(tm, D3, T, lambda j: 0, "ij", HALO_BF16)
    return pl.pallas_call(
        body, name=name, grid=(T // tm, 1),
        in_specs=[prev, main, nxt, pl.BlockSpec((3, D), lambda i, j: (0, 0))],
        out_specs=pl.BlockSpec((tm, D), lambda i, j: (i, 0)),
        out_shape=jax.ShapeDtypeStruct((T, D), BF16),
        compiler_params=_params(("parallel", "arbitrary")),
    )(proj_a, proj_a, proj_a, conv_w)


def conv_a_bwd(ds_a, proj_a, conv_w, name):
    T, D3 = proj_a.shape
    D = D3 // 3
    tm = _pick(T, 256, 8)

    def body(dp_ref, dm_ref, dn_ref, p_ref, m_ref, n_ref, w_ref, o_ref, dw_ref):
        i = pl.program_id(0)

        @pl.when(i == 0)
        def _():
            dw_ref[...] = jnp.zeros_like(dw_ref)

        ext = _ext_rows(p_ref, m_ref, n_ref, i, tm, T)
        dsa = _ext_rows(dp_ref, dm_ref, dn_ref, i, tm, T)
        gb, gc, hin = ext[:, :D], ext[:, D:2 * D], ext[:, 2 * D:]
        u = gc * hin
        u_prev, u_next = _prev_row(u), _next_row(u)
        cu = u_prev * w_ref[0:1, :] + u * w_ref[1:2, :] + u_next * w_ref[2:3, :]
        dcu = dsa * gb
        du = _next_row(dcu) * w_ref[0:1, :] + dcu * w_ref[1:2, :] + _prev_row(dcu) * w_ref[2:3, :]
        h = HALO_BF16
        o_ref[:, :D] = _main(dsa * cu, tm, h).astype(BF16)
        o_ref[:, D:2 * D] = _main(du * hin, tm, h).astype(BF16)
        o_ref[:, 2 * D:] = _main(du * gc, tm, h).astype(BF16)
        dcu_m = _main(dcu, tm, h)
        dw_ref[0:1, :] += jnp.sum(dcu_m * _main(u_prev, tm, h), 0, keepdims=True)
        dw_ref[1:2, :] += jnp.sum(dcu_m * _main(u, tm, h), 0, keepdims=True)
        dw_ref[2:3, :] += jnp.sum(dcu_m * _main(u_next, tm, h), 0, keepdims=True)

    dprev, dmain, dnxt = _halo_specs(tm, D, T, lambda j: 0, "ij", HALO_BF16)
    prev, main, nxt = _halo_specs(tm, D3, T, lambda j: 0, "ij", HALO_BF16)
    return pl.pallas_call(
        body, name=name, grid=(T // tm, 1),
        in_specs=[dprev, dmain, dnxt, prev, main, nxt, pl.BlockSpec((3, D), lambda i, j: (0, 0))],
        out_specs=[pl.BlockSpec((tm, D3), lambda i, j: (i, 0)), pl.BlockSpec((3, D), lambda i, j: (0, 0))],
        out_shape=[jax.ShapeDtypeStruct((T, D3), BF16), jax.ShapeDtypeStruct((3, D), F32)],
        compiler_params=_params(("arbitrary", "arbitrary")),
    )(ds_a, ds_a, ds_a, proj_a, proj_a, proj_a, conv_w)


_INV_SQRT2 = 1.0 / math.sqrt(2.0)
_INV_SQRT_2PI = 1.0 / math.sqrt(2.0 * math.pi)


def ffn_up_conv_f(h, w_up, b_up, fcw, fcb, name):
    T, D = h.shape
    F = fcb.shape[1]
    tm = _pick(T, 256, 8)
    tc = _pick(F, 256)
    halo = HALO_BF16

    def body(hp_ref, hm_ref, hn_ref, w_ref, b_ref, cw_ref, cb_ref, up_ref, f_ref):
        i = pl.program_id(0)
        h_ext = _ext_rows(hp_ref, hm_ref, hn_ref, i, tm, T, dtype=BF16)
        h_main = hm_ref[...]
        rows = i * tm - halo + lax.broadcasted_iota(jnp.int32, (tm + 2 * halo, 1), 0)
        inside = (rows >= 0) & (rows < T)
        for c in range(F // tc):
            cols = slice(c * tc, (c + 1) * tc)
            gcols = slice(F + c * tc, F + (c + 1) * tc)
            a_ext = jnp.dot(h_ext, w_ref[:, cols], preferred_element_type=F32) + b_ref[:, cols]
            a_ext = jnp.where(inside, a_ext, 0.0)
            gate = jnp.dot(h_main, w_ref[:, gcols], preferred_element_type=F32) + b_ref[:, gcols]
            up_ref[:, cols] = _main(a_ext, tm, halo)
            up_ref[:, gcols] = gate
            ca = _main(_prev_row(a_ext) * cw_ref[0:1, cols] + a_ext * cw_ref[1:2, cols]
                       + _next_row(a_ext) * cw_ref[2:3, cols], tm, halo) + cb_ref[:, cols]
            gl = 0.5 * ca * (1.0 + lax.erf(ca * _INV_SQRT2))
            f_ref[:, cols] = (gl * gate).astype(BF16)

    prev, main, nxt = _halo_specs(tm, D, T, lambda j: 0, "ij", halo)
    whole = lambda x: pl.BlockSpec(x.shape, lambda i, j: (0, 0))
    return pl.pallas_call(
        body, name=name, grid=(T // tm, 1),
        in_specs=[prev, main, nxt, whole(w_up), whole(b_up), whole(fcw), whole(fcb)],
        out_specs=[pl.BlockSpec((tm, 2 * F), lambda i, j: (i, 0)), pl.BlockSpec((tm, F), lambda i, j: (i, 0))],
        out_shape=[jax.ShapeDtypeStruct((T, 2 * F), F32), jax.ShapeDtypeStruct((T, F), BF16)],
        compiler_params=_params(("parallel", "arbitrary")),
    )(h, h, h, w_up, b_up, fcw, fcb)


def conv_f_bwd(dy, w_down, up, fcw, fcb, name):
    T, F2 = up.shape
    F = F2 // 2
    D = dy.shape[1]
    tm = _pick(T, 256, 8)
    tc = _pick(F, 256)

    def body(yp_ref, ym_ref, yn_ref, wd_ref, up_ref, um_ref, un_ref, w_ref, b_ref,
             da_ref, dg_ref, csa_ref, csg_ref, dfb_ref, dfw_ref):
        i = pl.program_id(0)
        first, last = i == 0, i == T // tm - 1

        @pl.when(first)
        def _():
            csa_ref[...] = jnp.zeros_like(csa_ref)
            csg_ref[...] = jnp.zeros_like(csg_ref)
            dfb_ref[...] = jnp.zeros_like(dfb_ref)
            dfw_ref[...] = jnp.zeros_like(dfw_ref)

        def ext(cols):
            return jnp.concatenate([jnp.where(first, 0.0, up_ref[:, cols]), um_ref[:, cols],
                                    jnp.where(last, 0.0, un_ref[:, cols])], axis=0)

        dy_ext = _ext_rows(yp_ref, ym_ref, yn_ref, i, tm, T, dtype=BF16)
        for c in range(F // tc):
            cols = slice(c * tc, (c + 1) * tc)
            dfe = lax.dot_general(dy_ext, wd_ref[cols, :], NT_DIMS, preferred_element_type=F32)
            dfe = dfe[HALO_BF16 - HALO:HALO_BF16 + tm + HALO]
            a = ext(cols)
            gate = ext(slice(F + c * tc, F + (c + 1) * tc))
            a_prev, a_next = _prev_row(a), _next_row(a)
            ca = a_prev * w_ref[0:1, cols] + a * w_ref[1:2, cols] + a_next * w_ref[2:3, cols] + b_ref[:, cols]
            cdf = 0.5 * (1.0 + lax.erf(ca * _INV_SQRT2))
            gl = ca * cdf
            gp = cdf + ca * (jnp.exp(-0.5 * ca * ca) * _INV_SQRT_2PI)
            dgate = _main(dfe * gl, tm)
            dca = dfe * gate * gp
            da = _main(_next_row(dca) * w_ref[0:1, cols] + dca * w_ref[1:2, cols] + _prev_row(dca) * w_ref[2:3, cols],
                       tm)
            da_ref[:, cols] = da.astype(BF16)
            dg_ref[:, cols] = dgate.astype(BF16)
            csa_ref[:, cols] += jnp.sum(da, 0, keepdims=True)
            csg_ref[:, cols] += jnp.sum(dgate, 0, keepdims=True)
            dca_m = _main(dca, tm)
            dfb_ref[:, cols] += jnp.sum(dca_m, 0, keepdims=True)
            dfw_ref[0:1, cols] += jnp.sum(dca_m * _main(a_prev, tm), 0, keepdims=True)
            dfw_ref[1:2, cols] += jnp.sum(dca_m * _main(a, tm), 0, keepdims=True)
            dfw_ref[2:3, cols] += jnp.sum(dca_m * _main(a_next, tm), 0, keepdims=True)

    uprev, umain, unxt = _halo_specs(tm, F2, T, lambda j: 0, "ij")
    yprev, ymain, ynxt = _halo_specs(tm, D, T, lambda j: 0, "ij", HALO_BF16)
    whole = lambda shape: pl.BlockSpec(shape, lambda i, j: (0, 0))
    tile = pl.BlockSpec((tm, F), lambda i, j: (i, 0))
    return pl.pallas_call(
        body, name=name, grid=(T // tm, 1),
        in_specs=[yprev, ymain, ynxt, whole((F, D)), uprev, umain, unxt, whole((3, F)), whole((1, F))],
        out_specs=[tile, tile, whole((1, F)), whole((1, F)), whole((1, F)), whole((3, F))],
        out_shape=[jax.ShapeDtypeStruct((T, F), BF16), jax.ShapeDtypeStruct((T, F), BF16),
                   jax.ShapeDtypeStruct((1, F), F32), jax.ShapeDtypeStruct((1, F), F32),
                   jax.ShapeDtypeStruct((1, F), F32), jax.ShapeDtypeStruct((3, F), F32)],
        compiler_params=_params(("arbitrary", "arbitrary")),
    )(dy, dy, dy, w_down, up, up, up, fcw, fcb)


def gate_fwd(proj_g, y_a, y_b, name):
    T, D = y_a.shape
    tm = _pick(T, 512, 8)

    def body(g_ref, a_ref, b_ref, o_ref):
        sa = jax.nn.sigmoid(g_ref[:, :D].astype(F32))
        sb = jax.nn.sigmoid(g_ref[:, D:].astype(F32))
        o_ref[...] = (sa * a_ref[...].astype(F32) + sb * b_ref[...].astype(F32)).astype(BF16)

    row = pl.BlockSpec((tm, D), lambda i: (i, 0))
    return pl.pallas_call(
        body, name=name, grid=(T // tm,),
        in_specs=[pl.BlockSpec((tm, 2 * D), lambda i: (i, 0)), row, row],
        out_specs=row,
        out_shape=jax.ShapeDtypeStruct((T, D), BF16),
        compiler_params=_params(("parallel",)),
    )(proj_g, y_a, y_b)


def gate_bwd(dz, proj_g, y_a, y_b, name):
    T, D = y_a.shape
    tm = _pick(T, 512, 8)

    def body(dz_ref, g_ref, a_ref, b_ref, da_ref, db_ref, dg_ref):
        dzv = dz_ref[...].astype(F32)
        sa = jax.nn.sigmoid(g_ref[:, :D].astype(F32))
        sb = jax.nn.sigmoid(g_ref[:, D:].astype(F32))
        da_ref[...] = (dzv * sa).astype(BF16)
        db_ref[...] = (dzv * sb).astype(BF16)
        dg_ref[:, :D] = (dzv * a_ref[...].astype(F32) * (sa * (1.0 - sa))).astype(BF16)
        dg_ref[:, D:] = (dzv * b_ref[...].astype(F32) * (sb * (1.0 - sb))).astype(BF16)

    row = pl.BlockSpec((tm, D), lambda i: (i, 0))
    wide = pl.BlockSpec((tm, 2 * D), lambda i: (i, 0))
    return pl.pallas_call(
        body, name=name, grid=(T // tm,),
        in_specs=[row, wide, row, row],
        out_specs=[row, row, wide],
        out_shape=[jax.ShapeDtypeStruct((T, D), BF16), jax.ShapeDtypeStruct((T, D), BF16),
                   jax.ShapeDtypeStruct((T, 2 * D), BF16)],
        compiler_params=_params(("parallel",)),
    )(dz, proj_g, y_a, y_b)


ATT_WIN = ATT_TQ + 2 * RADIUS
ATT_STEP = 1024
FAR = 1e32


def _att_window(qs, L):
    ks = pl.multiple_of(jnp.clip(qs - RADIUS, 0, L - ATT_WIN), RADIUS)
    return ks, jnp.where(qs == 0, 0, jnp.where(qs == L - ATT_TQ, 2, 1))


def _fill_bias_tables(bias_ref, sl_ref, hp, d):
    col_row = (lax.broadcasted_iota(jnp.int32, (ATT_TQ, ATT_WIN), 1)
               - lax.broadcasted_iota(jnp.int32, (ATT_TQ, ATT_WIN), 0))
    for v in range(3):
        ad = jnp.abs(col_row - v * RADIUS)
        dist = jnp.where(ad <= RADIUS, (ad * d).astype(F32), FAR)
        bias_ref[v, 0:ATT_TQ, :] = sl_ref[hp * 2] * dist
        bias_ref[v, ATT_TQ:2 * ATT_TQ, :] = sl_ref[hp * 2 + 1] * dist


def _head_masks():
    lane = lax.broadcasted_iota(jnp.int32, (1, LANES), 1)
    return [lane < HEAD_DIM, lane >= HEAD_DIM]


def _stack_heads(x, masks):
    zero = jnp.zeros_like(x)
    return jnp.concatenate([jnp.where(masks[0], x, zero), jnp.where(masks[1], x, zero)], axis=0)


def _unstack_heads(x2, masks):
    n = x2.shape[0] // 2
    return jnp.where(masks[0], x2[:n], x2[n:])


def _att_step(L):
    step = min(ATT_STEP, L)
    assert L % step == 0 and step % ATT_TQ == 0 and L >= ATT_WIN
    return step


def _residues_per_step(d, L):
    rps = max(1, min(d, ATT_STEP // L))
    assert d % rps == 0
    return rps


def att_fwd(qkv, group, name):
    d, L, _ = qkv.shape
    step = _att_step(L)
    rps = _residues_per_step(d, L)
    cg = GROUP_W // LANES
    slopes = jnp.asarray(_alibi_slopes()[group])
    scale = HEAD_DIM ** -0.5

    def body(sl_ref, q_ref, k_ref, v_ref, o_ref, l_ref, bias_ref, s_ref, p_ref):
        hp = pl.program_id(1)
        i = pl.program_id(2)

        @pl.when(i == 0)
        def _():
            _fill_bias_tables(bias_ref, sl_ref, hp, d)

        masks = _head_masks()
        per = step // ATT_TQ
        tiles = [(rr, t) for rr in range(rps) for t in range(per)]
        windows = [_att_window(i * step + t * ATT_TQ, L) for t in range(per)]
        for n, (rr, t) in enumerate(tiles):
            rows = slice(t * ATT_TQ, (t + 1) * ATT_TQ)
            ks, table = windows[t]
            q2 = _stack_heads(q_ref[rr, rows, :] * scale, masks)
            kw = k_ref[rr, pl.ds(ks, ATT_WIN), :]
            s_ref[n] = lax.dot_general(q2, kw, NT_DIMS, preferred_element_type=F32) - bias_ref[table]
        for n, (rr, t) in enumerate(tiles):
            rows = slice(t * ATT_TQ, (t + 1) * ATT_TQ)
            s = s_ref[n]
            m = jnp.max(s, -1, keepdims=True)
            p = jnp.exp(s - m)
            den = jnp.sum(p, -1, keepdims=True)
            p_ref[n] = (p / den).astype(BF16)
            l_ref[rr, rows, :] = _unstack_heads(m + jnp.log(den), masks)
        for n, (rr, t) in enumerate(tiles):
            rows = slice(t * ATT_TQ, (t + 1) * ATT_TQ)
            vw = v_ref[rr, pl.ds(windows[t][0], ATT_WIN), :]
            o2 = jnp.dot(p_ref[n], vw, preferred_element_type=F32)
            o_ref[rr, rows, :] = _unstack_heads(o2, masks)

    n_tiles = rps * step // ATT_TQ
    out_spec = pl.BlockSpec((rps, step, LANES), lambda r, hp, i: (r, i, hp))
    return pl.pallas_call(
        body, name=name, grid=(d // rps, cg, L // step),
        in_specs=[pl.BlockSpec(memory_space=pltpu.SMEM),
                  pl.BlockSpec((rps, step, LANES), lambda r, hp, i: (r, i, hp)),
                  pl.BlockSpec((rps, L, LANES), lambda r, hp, i: (r, 0, cg + hp)),
                  pl.BlockSpec((rps, L, LANES), lambda r, hp, i: (r, 0, 2 * cg + hp))],
        out_specs=[out_spec, out_spec],
        out_shape=[jax.ShapeDtypeStruct((d, L, GROUP_W), F32)] * 2,
        scratch_shapes=[pltpu.VMEM((3, 2 * ATT_TQ, ATT_WIN), F32),
                        pltpu.VMEM((n_tiles, 2 * ATT_TQ, ATT_WIN), F32),
                        pltpu.VMEM((n_tiles, 2 * ATT_TQ, ATT_WIN), BF16)],
        compiler_params=_params(("arbitrary", "arbitrary", "arbitrary")),
    )(slopes, qkv, qkv, qkv)


def att_bwd(qkv, do, lse, dmat, group, name, after=None):
    d, L, _ = qkv.shape
    step = _att_step(L)
    rps = _residues_per_step(d, L)
    nq = L // step
    cg = GROUP_W // LANES
    slopes = jnp.asarray(_alibi_slopes()[group])
    scale = HEAD_DIM ** -0.5

    def body(sl_ref, q_ref, k_ref, v_ref, do_ref, l_ref, dm_ref, *rest):
        dq_ref, dk_ref, dv_ref, dk_acc, dv_acc, bias_ref, s_ref, dp_ref, p_ref, ds_ref = rest[len(rest) - 10:]
        hp = pl.program_id(1)
        i = pl.program_id(2)

        @pl.when(i == 0)
        def _():
            dk_acc[...] = jnp.zeros_like(dk_acc)
            dv_acc[...] = jnp.zeros_like(dv_acc)
            _fill_bias_tables(bias_ref, sl_ref, hp, d)

        masks = _head_masks()

        def head_cols(x):
            return jnp.concatenate([jnp.max(jnp.where(hm, x, -jnp.inf), -1, keepdims=True) for hm in masks], axis=0)

        per = step // ATT_TQ
        tiles = [(rr, t) for rr in range(rps) for t in range(per)]
        windows = [_att_window(i * step + t * ATT_TQ, L) for t in range(per)]

        def stacked(ref, rr, t, factor=None):
            x = ref[rr, t * ATT_TQ:(t + 1) * ATT_TQ, :]
            return _stack_heads(x if factor is None else x * factor, masks)

        for n, (rr, t) in enumerate(tiles):
            ks, table = windows[t]
            q2 = stacked(q_ref, rr, t, scale)
            s_ref[n] = lax.dot_general(q2, k_ref[rr, pl.ds(ks, ATT_WIN), :], NT_DIMS,
                                       preferred_element_type=F32) - bias_ref[table]
            dp_ref[n] = lax.dot_general(stacked(do_ref, rr, t), v_ref[rr, pl.ds(ks, ATT_WIN), :], NT_DIMS,
                                        preferred_element_type=F32)
        for n, (rr, t) in enumerate(tiles):
            rows = slice(t * ATT_TQ, (t + 1) * ATT_TQ)
            p = jnp.exp(s_ref[n] - head_cols(l_ref[rr, rows, :]))
            p_ref[n] = p.astype(BF16)
            ds_ref[n] = (p * (dp_ref[n] - head_cols(dm_ref[rr, rows, :]))).astype(BF16)
        for n, (rr, t) in enumerate(tiles):
            rows = slice(t * ATT_TQ, (t + 1) * ATT_TQ)
            ks = windows[t][0]
            ds = ds_ref[n]
            dq2 = jnp.dot(ds, k_ref[rr, pl.ds(ks, ATT_WIN), :], preferred_element_type=F32)
            dq_ref[rr, rows, :] = (_unstack_heads(dq2, masks) * scale).astype(BF16)
            dk_acc[rr, pl.ds(ks, ATT_WIN), :] += lax.dot_general(ds, stacked(q_ref, rr, t, scale), TN_DIMS,
                                                                 preferred_element_type=F32)
            dv_acc[rr, pl.ds(ks, ATT_WIN), :] += lax.dot_general(p_ref[n], stacked(do_ref, rr, t), TN_DIMS,
                                                                 preferred_element_type=F32)

        @pl.when(i == nq - 1)
        def _():
            dk_ref[...] = dk_acc[...].astype(BF16)
            dv_ref[...] = dv_acc[...].astype(BF16)

    tile = pl.BlockSpec((rps, step, LANES), lambda r, hp, i: (r, i, hp))
    whole = pl.BlockSpec((rps, L, LANES), lambda r, hp, i: (r, 0, hp))
    return pl.pallas_call(
        body, name=name, grid=(d // rps, cg, nq),
        in_specs=[pl.BlockSpec(memory_space=pltpu.SMEM), tile,
                  pl.BlockSpec((rps, L, LANES), lambda r, hp, i: (r, 0, cg + hp)),
                  pl.BlockSpec((rps, L, LANES), lambda r, hp, i: (r, 0, 2 * cg + hp)),
                  tile, tile, tile] + ([] if after is None else [pl.BlockSpec((8, LANES), lambda r, hp, i: (0, 0))]),
        out_specs=[tile, whole, whole],
        out_shape=[jax.ShapeDtypeStruct((d, L, GROUP_W), BF16)] * 3,
        scratch_shapes=[pltpu.VMEM((rps, L, LANES), F32), pltpu.VMEM((rps, L, LANES), F32),
                        pltpu.VMEM((3, 2 * ATT_TQ, ATT_WIN), F32)]
        + [pltpu.VMEM((rps * step // ATT_TQ, 2 * ATT_TQ, ATT_WIN), dt) for dt in (F32, F32, BF16, BF16)],
        compiler_params=_params(("arbitrary", "arbitrary", "arbitrary")),
    )(slopes, qkv, qkv, qkv, do, lse, dmat, *([] if after is None else [after]))


def _group_weights(ls):
    m = jnp.maximum(jnp.maximum(ls[0], ls[1]), ls[2])
    es = [jnp.exp(l - m) for l in ls]
    tot = es[0] + es[1] + es[2]
    return [e / tot for e in es]


def combine_fwd(outs, lses, name):
    T = outs[0].shape[0] * outs[0].shape[1]
    tm = _pick(T, 512, 8)
    n_scr = 2 * (len(DILATIONS) - 1)

    def body(*refs):
        o_refs, l_refs, c_ref, scr = refs[:3], refs[3:6], refs[6], refs[7:]
        o = [_load_natural(o_refs[g], d, scr[g - 1] if g else None) for g, d in enumerate(DILATIONS)]
        l = [_load_natural(l_refs[g], d, scr[g + 1] if g else None) for g, d in enumerate(DILATIONS)]
        w = _group_weights(l)
        c_ref[...] = (w[0] * o[0] + w[1] * o[1] + w[2] * o[2]).astype(BF16)

    specs = [_residue_spec(tm, d, GROUP_W) for d in DILATIONS]
    return pl.pallas_call(
        body, name=name, grid=(T // tm,),
        in_specs=specs + specs, out_specs=pl.BlockSpec((tm, GROUP_W), lambda i: (i, 0)),
        out_shape=jax.ShapeDtypeStruct((T, GROUP_W), BF16),
        scratch_shapes=[_residue_scratch(tm, GROUP_W)] * n_scr,
        compiler_params=_params(("parallel",)),
    )(*outs, *lses)


def combine_bwd(dcomb, outs, lses, name):
    T = dcomb.shape[0]
    tm = _pick(T, 256, 8)
    head = np.arange(GROUP_W) // HEAD_DIM
    seg = jnp.asarray((head[:, None] == head[None, :]).astype(np.float32)).astype(BF16)
    ng = len(DILATIONS)
    n_scr = 4 * (ng - 1)

    def body(*refs):
        dc_ref, o_refs, l_refs, e_ref = refs[0], refs[1:1 + ng], refs[1 + ng:1 + 2 * ng], refs[1 + 2 * ng]
        do_refs, dm_refs = refs[2 + 2 * ng:2 + 3 * ng], refs[2 + 3 * ng:2 + 4 * ng]
        scr = refs[2 + 4 * ng:]
        o = [_load_natural(o_refs[g], d, scr[4 * (g - 1)] if g else None) for g, d in enumerate(DILATIONS)]
        l = [_load_natural(l_refs[g], d, scr[4 * (g - 1) + 1] if g else None) for g, d in enumerate(DILATIONS)]
        w = _group_weights(l)
        dc = dc_ref[...].astype(F32)
        e = e_ref[...]
        prod = dc * (w[0] * o[0] + w[1] * o[1] + w[2] * o[2])
        tot = jnp.zeros_like(dc)
        for _ in range(3):
            part = prod.astype(BF16)
            tot = tot + jnp.dot(part, e, preferred_element_type=F32)
            prod = prod - part.astype(F32)
        for g, d in enumerate(DILATIONS):
            _store_by_residue(w[g] * dc, do_refs[g], d, scr[4 * (g - 1) + 2] if g else None)
            _store_by_residue(w[g] * tot, dm_refs[g], d, scr[4 * (g - 1) + 3] if g else None)

    specs = [_residue_spec(tm, d, GROUP_W) for d in DILATIONS]
    res = pl.pallas_call(
        body, name=name, grid=(T // tm,),
        in_specs=[pl.BlockSpec((tm, GROUP_W), lambda i: (i, 0))] + specs + specs
        + [pl.BlockSpec((GROUP_W, GROUP_W), lambda i: (0, 0))],
        out_specs=specs + specs,
        out_shape=[jax.ShapeDtypeStruct(o.shape, BF16) for o in outs] + [jax.ShapeDtypeStruct(o.shape, F32) for o in outs],
        scratch_shapes=[_residue_scratch(tm, GROUP_W)] * n_scr,
        compiler_params=_params(("parallel",)),
    )(dcomb, *outs, *lses, seg)
    return res[:ng], res[ng:]


def _position():
    return lax.axis_index("x"), lax.axis_index("y"), lax.axis_index("c")


def _other_chips(x, y):
    return [(1 - x, y), (x, 1 - y), (1 - x, 1 - y)]


def _remote(src, dst, send_sems, recv_sems, k, to):
    return pltpu.make_async_remote_copy(src_ref=src, dst_ref=dst, send_sem=send_sems.at[k], recv_sem=recv_sems.at[k],
                                        device_id=to, device_id_type=MESH)


GATHER_SEMS = 10
SPLIT_ROWS = 32


def _gather_plan(ins, outs, send_sems, recv_sems, local_sems):
    x, y, c = _position()
    sibling = (x, y, 1 - c)
    nbr_x, nbr_y, diag = (1 - x, y, c), (x, 1 - y, c), (1 - x, 1 - y, c)
    local, begin, stages, last = [], [], [], []
    for a in range(len(ins)):
        k0 = GATHER_SEMS * a
        rows = ins[a].shape[0]
        half = rows // 2

        def block(dev):
            return outs[a].at[4 * dev[0] + 2 * dev[1] + dev[2]]

        def part(ref, h):
            return ref.at[pl.ds(h * half, half)]

        def copy(k, src, dst, to):
            return _remote(src, dst, send_sems, recv_sems, k0 + k, to)

        me = (x, y, c)
        local.append(pltpu.make_async_copy(ins[a], block(me), local_sems.at[a]))
        begin.append(copy(0, ins[a], block(me), sibling))
        pass_on = [copy(7 + j, block(dev), block(dev), sibling) for j, dev in enumerate((nbr_x, nbr_y, diag))]
        if rows >= SPLIT_ROWS and rows % SPLIT_ROWS == 0:
            for h in range(2):
                begin.append(copy(1 + h, part(ins[a], h), part(block(me), h), nbr_x))
                begin.append(copy(3 + h, part(ins[a], h), part(block(me), h), nbr_y))
            from_x = [copy(1 + h, part(block(nbr_x), h), part(block(nbr_x), h), sibling) for h in range(2)]
            from_y = [copy(3 + h, part(block(nbr_y), h), part(block(nbr_y), h), sibling) for h in range(2)]
            fwd_0 = copy(5, part(block(nbr_x), 0), part(block(nbr_x), 0), nbr_y)
            fwd_1 = copy(6, part(block(nbr_y), 1), part(block(nbr_y), 1), nbr_x)
            got_0 = copy(5, part(block(diag), 0), part(block(diag), 0), sibling)
            got_1 = copy(6, part(block(diag), 1), part(block(diag), 1), sibling)
            stages.append(([from_x[0]], [fwd_0]))
            stages.append(([from_y[1]], [fwd_1]))
            stages.append(([from_x[1]], [pass_on[0]]))
            stages.append(([from_y[0]], [pass_on[1]]))
            stages.append(([got_0, got_1], [pass_on[2]]))
        else:
            for j, dev in enumerate((nbr_x, nbr_y, diag)):
                begin.append(copy(1 + 2 * j, ins[a], block(me), dev))
                stages.append(([copy(1 + 2 * j, block(dev), block(dev), sibling)], [pass_on[j]]))
        other = (x, y, 1 - c)
        last.append(copy(0, block(other), block(other), sibling))
        for j, dev in enumerate((nbr_x, nbr_y, diag)):
            theirs = (dev[0], dev[1], 1 - c)
            last.append(copy(7 + j, block(theirs), block(theirs), sibling))
    return local, begin, stages, last


def _gather_begin(ins, outs, send_sems, recv_sems, local_sems):
    local, begin, _, _ = _gather_plan(ins, outs, send_sems, recv_sems, local_sems)
    for cp in local + begin:
        cp.start()


def _gather_finish(ins, outs, send_sems, recv_sems, local_sems):
    local, begin, stages, last = _gather_plan(ins, outs, send_sems, recv_sems, local_sems)
    started = []
    for arrivals, onward in stages:
        for cp in arrivals:
            cp.wait_recv()
        for cp in onward:
            cp.start()
            started.append(cp)
    for cp in last:
        cp.wait_recv()
    for cp in begin + started:
        cp.wait_send()
    for cp in local:
        cp.wait()


def _gather_scratch(n):
    return [pltpu.SemaphoreType.DMA((GATHER_SEMS * n,)), pltpu.SemaphoreType.DMA((GATHER_SEMS * n,)),
            pltpu.SemaphoreType.DMA((n,))]


_HBM = pl.BlockSpec(memory_space=pltpu.HBM)
_SEM = pl.BlockSpec(memory_space=pltpu.SEMAPHORE)
_DATAFLOW = pltpu.SideEffectType.DATAFLOW_SIDE_EFFECTING


def _to_all_plan(srcs, lands, send_sems, recv_sems):
    x, y, c = _position()
    me = 4 * x + 2 * y + c
    copies = []
    for a in range(len(srcs)):
        for k in range(1, N_DEV):
            fx, fy, fc = (k >> 2) & 1, (k >> 1) & 1, k & 1
            to = (1 - x if fx else x, 1 - y if fy else y, 1 - c if fc else c)
            copies.append(_remote(srcs[a], lands[a].at[me], send_sems, recv_sems, (N_DEV - 1) * a + k - 1, to))
    return copies


def _to_sibling_plan(srcs, lands, send_sems, recv_sems):
    x, y, c = _position()
    copies = []
    for a in range(len(srcs)):
        for q in range(4):
            copies.append(_remote(srcs[a].at[2 * q + (1 - c)], lands[a].at[q], send_sems, recv_sems, 4 * a + q,
                                  (x, y, 1 - c)))
    return copies


def _to_chips_plan(srcs, lands, send_sems, recv_sems):
    x, y, c = _position()
    copies = []
    for a in range(len(srcs)):
        for j, (cx, cy) in enumerate(_other_chips(x, y)):
            copies.append(_remote(srcs[a].at[2 * cx + cy], lands[a].at[j], send_sems, recv_sems, 3 * a + j, (cx, cy, c)))
    return copies


def copies_start(srcs, land_shapes, plan, per_array, name):
    n = len(srcs)
    n_sem = per_array * n
    lands = [lax.empty(s.shape, s.dtype) for s in land_shapes]

    def body(*refs):
        src_refs, land_refs = refs[:n], refs[n:2 * n]
        send_sems, recv_sems = refs[2 * n], refs[2 * n + 1]
        token = refs[-1]
        for cp in plan(src_refs, land_refs, send_sems, recv_sems):
            cp.start()
        token[...] = jnp.zeros_like(token)

    out = pl.pallas_call(
        body, name=name,
        out_shape=(pltpu.SemaphoreType.DMA((n_sem,)), pltpu.SemaphoreType.DMA((n_sem,)))
        + tuple(pltpu.HBM(s.shape, s.dtype) for s in srcs)
        + tuple(pltpu.HBM(s.shape, s.dtype) for s in land_shapes)
        + (jax.ShapeDtypeStruct((8, LANES), F32),),
        in_specs=[_HBM] * (2 * n),
        out_specs=(_SEM, _SEM) + (_HBM,) * (2 * n) + (pl.BlockSpec(memory_space=pltpu.VMEM),),
        input_output_aliases={i: 2 + i for i in range(2 * n)},
        compiler_params=pltpu.CompilerParams(has_side_effects=_DATAFLOW),
    )(*[pltpu.with_memory_space_constraint(s, pltpu.HBM) for s in srcs],
      *[pltpu.with_memory_space_constraint(l, pltpu.HBM) for l in lands])
    return out[:-1], out[-1]


def copies_wait(handles, plan, after, name):
    send_sems, recv_sems = handles[0], handles[1]
    n = (len(handles) - 2) // 2
    thru = handles[2:]

    def body(*refs):
        src_refs, land_refs = refs[:n], refs[n:2 * n]
        send_sems, recv_sems = refs[2 * n], refs[2 * n + 1]
        copies = plan(src_refs, land_refs, send_sems, recv_sems)
        for cp in copies:
            cp.wait_recv()
        for cp in copies:
            cp.wait_send()

    out = pl.pallas_call(
        body, name=name,
        out_shape=tuple(pltpu.HBM(t.shape, t.dtype) for t in thru),
        in_specs=[_HBM] * (2 * n) + [_SEM, _SEM, pl.BlockSpec(memory_space=pl.ANY)],
        out_specs=(_HBM,) * (2 * n),
        input_output_aliases={i: i for i in range(2 * n)},
        compiler_params=pltpu.CompilerParams(has_side_effects=_DATAFLOW),
    )(*thru, send_sems, recv_sems, after)
    return out[:n], out[n:]


def all_sum_small(vec, name):
    R = vec.shape[0]

    def body(v_ref, tot_ref, all_ref, send_sems, recv_sems):
        x, y, c = _position()
        me = 4 * x + 2 * y + c
        all_ref[me] = v_ref[...]
        copies = []
        for k in range(1, N_DEV):
            fx, fy, fc = (k >> 2) & 1, (k >> 1) & 1, k & 1
            to = (1 - x if fx else x, 1 - y if fy else y, 1 - c if fc else c)
            cp = _remote(v_ref, all_ref.at[me], send_sems, recv_sems, k - 1, to)
            cp.start()
            copies.append(cp)
        for cp in copies:
            cp.wait_recv()
        for cp in copies:
            cp.wait_send()
        tot = all_ref[0]
        for j in range(1, N_DEV):
            tot = tot + all_ref[j]
        tot_ref[...] = tot

    vmem = pl.BlockSpec(memory_space=pltpu.VMEM)
    return pl.pallas_call(
        body, name=name,
        in_specs=[vmem], out_specs=vmem,
        out_shape=jax.ShapeDtypeStruct((R, LANES), F32),
        scratch_shapes=[pltpu.VMEM((N_DEV, R, LANES), F32),
                        pltpu.SemaphoreType.DMA((N_DEV - 1,)), pltpu.SemaphoreType.DMA((N_DEV - 1,))],
        compiler_params=pltpu.CompilerParams(vmem_limit_bytes=VMEM_LIMIT),
    )(vec)


def pair_add(parts, theirs, place, name):
    _, R, C = theirs.shape
    tr = _pick(R, 256, 8)

    def body(place_ref, a_ref, b_ref, o_ref):
        o_ref[...] = (a_ref[...].astype(F32) + b_ref[...].astype(F32)).astype(BF16)

    blk = pl.BlockSpec((None, tr, C), lambda q, i, place_ref: (q, i, 0))
    return pl.pallas_call(
        body, name=name,
        grid_spec=pltpu.PrefetchScalarGridSpec(
            num_scalar_prefetch=1, grid=(4, R // tr),
            in_specs=[pl.BlockSpec((None, tr, C), lambda q, i, place_ref: (2 * q + place_ref[2], i, 0)), blk],
            out_specs=blk),
        out_shape=jax.ShapeDtypeStruct(theirs.shape, BF16),
        compiler_params=_params(("parallel", "parallel")),
    )(place, parts, theirs)


def _adamw_math(w, g, m, v):
    m = ADAM_B1 * m + (1.0 - ADAM_B1) * g
    v = ADAM_B2 * v + (1.0 - ADAM_B2) * jnp.square(g)
    m_hat = m / (1.0 - ADAM_B1 ** ADAM_STEP)
    v_hat = v / (1.0 - ADAM_B2 ** ADAM_STEP)
    delta = -ADAM_LR * (m_hat / (jnp.sqrt(v_hat) + ADAM_EPS) + ADAM_WD * w)
    return delta, m, v


def adamw_sharded(w, m, v, parts, sib, others, place, name):
    R, C = w.shape
    tr = _pick(R, 256, 8)

    def body(place_ref, w_ref, m_ref, v_ref, a_ref, b_ref, o_ref, g_ref, d_ref, nm_ref, nv_ref):
        g = a_ref[...].astype(F32) + b_ref[...].astype(F32)
        for j in range(3):
            g = g + o_ref[j].astype(F32)
        delta, nm, nv = _adamw_math(w_ref[...], g, m_ref[...], v_ref[...])
        g_ref[...] = g
        d_ref[...] = delta
        nm_ref[...] = nm
        nv_ref[...] = nv

    row = pl.BlockSpec((tr, C), lambda i, place_ref: (i, 0))
    return pl.pallas_call(
        body, name=name,
        grid_spec=pltpu.PrefetchScalarGridSpec(
            num_scalar_prefetch=1, grid=(R // tr,),
            in_specs=[row] * 3 + [pl.BlockSpec((None, tr, C), lambda i, place_ref: (place_ref[0], i, 0)),
                                  pl.BlockSpec((None, tr, C), lambda i, place_ref: (place_ref[1], i, 0)),
                                  pl.BlockSpec((3, tr, C), lambda i, place_ref: (0, i, 0))],
            out_specs=[row] * 4),
        out_shape=[jax.ShapeDtypeStruct((R, C), F32)] * 4,
        compiler_params=_params(("parallel",)),
    )(place, w, m, v, parts, sib, others)


def adamw_packed(w, g, m, v, name):
    R = w.shape[0]

    def body(w_ref, g_ref, m_ref, v_ref, d_ref, nm_ref, nv_ref):
        delta, nm, nv = _adamw_math(w_ref[...], g_ref[...], m_ref[...], v_ref[...])
        d_ref[...] = delta
        nm_ref[...] = nm
        nv_ref[...] = nv

    full = pl.BlockSpec((R, LANES), lambda i: (0, 0))
    return pl.pallas_call(
        body, name=name, grid=(1,),
        in_specs=[full] * 4, out_specs=[full] * 3,
        out_shape=[jax.ShapeDtypeStruct((R, LANES), F32)] * 3,
        compiler_params=_params(("arbitrary",)),
    )(w, g, m, v)


def _pack(arrays):
    flat = []
    sizes = []
    for a in arrays:
        f = a.reshape(-1).astype(F32)
        pad = (-f.shape[0]) % LANES
        if pad:
            f = jnp.concatenate([f, jnp.zeros((pad,), F32)])
        flat.append(f)
        sizes.append(f.shape[0])
    rows = sum(sizes) // LANES
    pad_rows = (-rows) % 8
    if pad_rows:
        flat.append(jnp.zeros((pad_rows * LANES,), F32))
    return jnp.concatenate(flat).reshape(-1, LANES), sizes


def _unpack(packed, sizes, shapes):
    flat = packed.reshape(-1)
    out = []
    off = 0
    for size, shape in zip(sizes, shapes):
        n = int(np.prod(shape))
        out.append(flat[off:off + n].reshape(shape))
        off += size
    return out


def _to_blocks(full, axis):
    if axis == 0:
        return full.reshape(N_DEV, full.shape[0] // N_DEV, full.shape[1])
    r, n = full.shape
    return full.reshape(r, N_DEV, n // N_DEV).transpose(1, 0, 2)


def _from_blocks(blocks, axis):
    if axis == 0:
        return blocks.reshape(blocks.shape[0] * blocks.shape[1], blocks.shape[2])
    return blocks.transpose(1, 0, 2).reshape(blocks.shape[1], blocks.shape[0] * blocks.shape[2])


def kernel(x, ln0_g, ln0_b, w_in, b_in, conv_w, w_a, w_b, w_o, b_o, ln1_g, ln1_b, w_up, b_up, ffn_conv_w, ffn_conv_b, w_down, b_down, ln2_g, ln2_b, loss_target, m_ln0_g, m_ln0_b, m_w_in, m_b_in, m_conv_w, m_w_a, m_w_b, m_w_o, m_b_o, m_ln1_g, m_ln1_b, m_w_up, m_b_up, m_ffn_conv_w, m_ffn_conv_b, m_w_down, m_b_down, m_ln2_g, m_ln2_b, v_ln0_g, v_ln0_b, v_w_in, v_b_in, v_conv_w, v_w_a, v_w_b, v_w_o, v_b_o, v_ln1_g, v_ln1_b, v_w_up, v_b_up, v_ffn_conv_w, v_ffn_conv_b, v_w_down, v_b_down, v_ln2_g, v_ln2_b):
    T, D = x.shape[1], x.shape[2]
    F = ffn_conv_b.shape[-1]
    xs = x.reshape(T, D)
    tgt = loss_target.reshape(T, D)
    dev = 4 * lax.axis_index("x") + 2 * lax.axis_index("y") + lax.axis_index("c")
    chip = 2 * lax.axis_index("x") + lax.axis_index("y")
    core = lax.axis_index("c")
    place = jnp.stack([dev, chip, core]).astype(jnp.int32)

    big = dict(w_in=(w_in[0], 1), w_a=(w_a[0], 0), w_b=(w_b[0], 1), w_o=(w_o[0], 0), w_up=(w_up[0], 1),
               w_down=(w_down[0], 0))
    names = list(big)
    shards = {k: big[k][0].astype(BF16) for k in names}
    ln0g, ln0b = ln0_g.reshape(1, D), ln0_b.reshape(1, D)
    h0, h0b, *rest = ln_fwd(xs, None, ln0g, ln0b, "ln0_fwd_gather_w_in", dilations=DILATIONS[1:],
                            gather=[shards["w_in"], conv_w[0], ffn_conv_w[0]])
    h0_res = [h0b] + [h.reshape(T, D) for h in rest[:2]]
    g_in, g_conv, g_fcw = rest[2:]
    full = {"w_in": _from_blocks(g_in, 1)}
    conv_full = _from_blocks(g_conv, 1)
    fcw_full = _from_blocks(g_fcw, 1)
    late_groups = (("w_a", "w_b", "w_o"), ("w_up", "w_down"))
    late_handles = []
    token = conv_full[:1, :1] * 0.0
    for n, keys in enumerate(late_groups):
        srcs = [shards[k] + token[0, 0].astype(BF16) for k in keys]
        handles, token = copies_start(srcs, [jax.ShapeDtypeStruct((N_DEV,) + s.shape, BF16) for s in srcs],
                                      _to_all_plan, N_DEV - 1, f"gather_late_{n}_start")
        late_handles.append(handles)

    def late_weights(n, after):
        _, lands = copies_wait(late_handles[n], _to_all_plan, after, f"gather_late_{n}_wait")
        for k, land in zip(late_groups[n], lands):
            full[k] = _from_blocks(lax.dynamic_update_index_in_dim(land, shards[k], dev, 0), big[k][1])

    o_q = 3 * D
    o_g = o_q + 3 * QKV_W
    w_pa, w_qkv, w_pg = full["w_in"][:, :o_q], full["w_in"][:, o_q:o_g], full["w_in"][:, o_g:]
    b_pa, b_qkv, b_pg = b_in[:, :o_q], b_in[:, o_q:o_g], b_in[:, o_g:]

    proj_a = mm_nn(h0b, w_pa, b_pa, ACT, "proj_conv", after=token)
    proj_g = mm_nn(h0b, w_pg, b_pg, ACT, "proj_gates")
    zero_d = jnp.zeros((1, D), F32)
    s_a = conv_a_fwd(proj_a, conv_full, "conv_a_fwd")
    late_weights(0, s_a)
    y_a = mm_nn(s_a, full["w_a"], zero_d, ACT, "branch_a_out")

    def group_cols(m, g):
        return jnp.concatenate([m[:, s * QKV_W + g * GROUP_W:s * QKV_W + (g + 1) * GROUP_W] for s in range(3)], 1)

    w_grp = [group_cols(w_qkv, g) for g in range(3)]
    qkvs, outs, lses = [], [], []
    for g, d in enumerate(DILATIONS):
        qkv = mm_nn(h0_res[g], w_grp[g], group_cols(b_qkv, g), BF16, f"proj_qkv_{g}").reshape(d, T // d, 3 * GROUP_W)
        o, l = att_fwd(qkv, g, f"att_fwd_{g}")
        qkvs.append(qkv)
        outs.append(o)
        lses.append(l)
    comb = combine_fwd(outs, lses, "combine_fwd")
    y_b = mm_nn(comb, full["w_b"], zero_d, ACT, "branch_b_out")
    z = gate_fwd(proj_g, y_a, y_b, "gate_fwd")
    h1, h1b, mix = ln_fwd(h0, ("nn", z, full["w_o"], b_o), ln1_g, ln1_b, "mix_out_ln1_fwd")
    late_weights(1, h1b)
    up, f_act = ffn_up_conv_f(h1b, full["w_up"], b_up, fcw_full, ffn_conv_b, "ffn_up_conv_f")

    dr2, dr2b, d_ln2_g, d_ln2_b, d_b_down, loss_part = ln_bwd(
        h1, ("nn", f_act, full["w_down"], b_down), ln2_g, ln2_b, None, None, tgt, "ffn_down_ln2_loss_bwd")
    dw_down, _ = mm_tn(f_act, dr2b, "dw_down")
    d_a, d_gate, cs_a, cs_gate, d_fcb, d_fcw = conv_f_bwd(dr2b, full["w_down"], up, fcw_full, ffn_conv_b,
                                                          "d_ffn_act_conv_f_bwd")
    dw_up_a, _ = mm_tn(h1b, d_a, "dw_up_a")
    dw_up_g, _ = mm_tn(h1b, d_gate, "dw_up_gate")
    dr1, dr1b, d_ln1_g, d_ln1_b, d_b_o, _ = ln_bwd(h0, mix, ln1_g, ln1_b, dr2, ("nt", [d_a, d_gate], full["w_up"]), None,
                                                   "d_h1_ln1_bwd")
    dw_o, _ = mm_tn(z, dr1b, "dw_o")
    dz = mm_nt(dr1b, full["w_o"], None, "d_z", out_dtype=ACT)
    dy_a, dy_b, dproj_g = gate_bwd(dz, proj_g, y_a, y_b, "gate_bwd")
    dw_a, _ = mm_tn(s_a, dy_a, "dw_a")
    ds_a = mm_nt(dy_a, full["w_a"], None, "d_s_a", out_dtype=ACT)
    dproj_a, d_conv = conv_a_bwd(ds_a, proj_a, conv_full, "conv_a_bwd")
    dw_b, _ = mm_tn(comb, dy_b, "dw_b")

    rs_mine, rs_sib, rs_handles = {}, {}, {}

    sib_handles = {}

    def to_sibling_start(keys, grads, tag):
        parts = [_to_blocks(grads[k], big[k][1]) for k in keys]
        handles, tok = copies_start(parts, [jax.ShapeDtypeStruct((4,) + p.shape[1:], BF16) for p in parts],
                                    _to_sibling_plan, 4, f"grads_to_sibling_{tag}_start")
        sib_handles[tag] = (keys, handles)
        return tok

    def to_chips_start(tag, after):
        keys, handles = sib_handles[tag]
        parts, from_sib = copies_wait(handles, _to_sibling_plan, after, f"grads_to_sibling_{tag}_wait")
        sums = [pair_add(a, b, place, f"chip_sum_{k}") for k, a, b in zip(keys, parts, from_sib)]
        handles, tok = copies_start(sums, [jax.ShapeDtypeStruct((3,) + s.shape[1:], BF16) for s in sums],
                                    _to_chips_plan, 3, f"grads_to_chips_{tag}_start")
        for k, a, b in zip(keys, parts, from_sib):
            rs_mine[k], rs_sib[k] = a, b
        rs_handles[tag] = (keys, handles)
        return tok

    tok_a = to_sibling_start(("w_a", "w_b", "w_o", "w_up", "w_down"),
                             dict(w_a=dw_a, w_b=dw_b, w_o=dw_o, w_up=jnp.concatenate([dw_up_a, dw_up_g], 1),
                                  w_down=dw_down), "a")
    dcomb = mm_nt(dy_b, full["w_b"], None, "d_comb", after=tok_a, out_dtype=ACT)
    dos, dms = combine_bwd(dcomb, outs, lses, "combine_bwd")
    tok_a = to_chips_start("a", dms[0])
    dw_grp, cs_grp, dqkvs = [], [], []
    for g, d in enumerate(DILATIONS):
        dq, dk, dv = att_bwd(qkvs[g], dos[g], lses[g], dms[g], g, f"att_bwd_{g}", after=tok_a if g == 0 else None)
        dqkv = [t.reshape(T, GROUP_W) for t in (dq, dk, dv)]
        dwg, csg = mm_tn(h0_res[g], dqkv, f"dw_in_qkv_{g}")
        dqkvs.append(dqkv)
        dw_grp.append(dwg)
        cs_grp.append(csg)
    dw_pa, cs_pa = mm_tn(h0b, dproj_a, "dw_in_conv")
    dw_pg, cs_pg = mm_tn(h0b, dproj_g, "dw_in_gates")

    def ungroup(parts):
        return jnp.concatenate([p[:, s * GROUP_W:(s + 1) * GROUP_W] for s in range(3) for p in parts], 1)

    db_in_parts = [cs_pa, ungroup(cs_grp), cs_pg]
    tok_b = to_sibling_start(("w_in",), dict(w_in=jnp.concatenate([dw_pa, ungroup(dw_grp), dw_pg], 1)), "b")
    dh0 = mm_nt(dproj_a, w_pa, None, "d_h0_conv", after=tok_b)
    tok_b = to_chips_start("b", dh0)
    dh0 = mm_nt(dproj_g, w_pg, dh0, "d_h0_gates", after=tok_b)
    dh0_res = [(mm_nt(dqkvs[g], w_grp[g], None, f"d_h0_qkv_{g}").reshape(d, T // d, D), d)
               for g, d in enumerate(DILATIONS) if g > 0]
    dx, _, d_ln0_g, d_ln0_b, _, _ = ln_bwd(xs, None, ln0g, ln0b, dr1, ("nt", dqkvs[0], w_grp[0]), None, "d_h0_ln0_bwd",
                                           by_residue=[(dh0.reshape(1, T, D), 1)] + dh0_res)

    small = [d_ln0_g, d_ln0_b, jnp.concatenate(db_in_parts, 1), d_conv, d_b_o, d_ln1_g, d_ln1_b,
             jnp.concatenate([cs_a, cs_gate], 1), d_fcw, d_fcb, d_b_down, d_ln2_g, d_ln2_b, loss_part]
    packed, sizes = _pack(small)
    total = all_sum_small(packed, "sum_small")
    (g_ln0_g, g_ln0_b, g_b_in, g_conv_full, g_b_o, g_ln1_g, g_ln1_b, g_b_up, g_fcw_full, g_fcb, g_b_down, g_ln2_g,
     g_ln2_b, loss) = _unpack(total, sizes, [a.shape for a in small])
    cw = conv_w.shape[-1]
    fw = ffn_conv_w.shape[-1]
    g_conv = lax.dynamic_slice_in_dim(g_conv_full, dev * cw, cw, 1)
    g_fcw = lax.dynamic_slice_in_dim(g_fcw_full, dev * fw, fw, 1)

    from_chips = {}
    for tag, (keys, handles) in rs_handles.items():
        _, lands = copies_wait(handles, _to_chips_plan, total, f"grads_to_chips_{tag}_wait")
        from_chips.update(zip(keys, lands))

    moments = dict(w_in=(m_w_in, v_w_in), w_a=(m_w_a, v_w_a), w_b=(m_w_b, v_w_b), w_o=(m_w_o, v_w_o),
                   w_up=(m_w_up, v_w_up), w_down=(m_w_down, v_w_down))
    res_big = {}
    for k in names:
        res_big[k] = adamw_sharded(big[k][0], moments[k][0][0], moments[k][1][0], rs_mine[k], rs_sib[k], from_chips[k],
                                   place, f"adamw_{k}")

    small_names = ["ln0_g", "ln0_b", "b_in", "conv_w", "b_o", "ln1_g", "ln1_b", "b_up", "ffn_conv_w", "ffn_conv_b",
                   "b_down", "ln2_g", "ln2_b"]
    small_w = [ln0_g, ln0_b, b_in, conv_w, b_o, ln1_g, ln1_b, b_up, ffn_conv_w, ffn_conv_b, b_down, ln2_g, ln2_b]
    small_m = [m_ln0_g, m_ln0_b, m_b_in, m_conv_w, m_b_o, m_ln1_g, m_ln1_b, m_b_up, m_ffn_conv_w, m_ffn_conv_b,
               m_b_down, m_ln2_g, m_ln2_b]
    small_v = [v_ln0_g, v_ln0_b, v_b_in, v_conv_w, v_b_o, v_ln1_g, v_ln1_b, v_b_up, v_ffn_conv_w, v_ffn_conv_b,
               v_b_down, v_ln2_g, v_ln2_b]
    small_g = [g_ln0_g, g_ln0_b, g_b_in, g_conv, g_b_o, g_ln1_g, g_ln1_b, g_b_up, g_fcw, g_fcb, g_b_down, g_ln2_g,
               g_ln2_b]
    shapes = [w.shape for w in small_w]
    small_g = [g.reshape(s) for g, s in zip(small_g, shapes)]
    pw, psz = _pack(small_w)
    pg, _ = _pack(small_g)
    pm, _ = _pack(small_m)
    pv, _ = _pack(small_v)
    pd, pnm, pnv = adamw_packed(pw, pg, pm, pv, "adamw_small")
    res_small = {k: (g, d_, m_, v_) for k, g, d_, m_, v_ in zip(
        small_names, small_g, _unpack(pd, psz, shapes), _unpack(pnm, psz, shapes), _unpack(pnv, psz, shapes))}

    order = ["ln0_g", "ln0_b", "w_in", "b_in", "conv_w", "w_a", "w_b", "w_o", "b_o", "ln1_g", "ln1_b", "w_up", "b_up",
             "ffn_conv_w", "ffn_conv_b", "w_down", "b_down", "ln2_g", "ln2_b"]

    def result(k, j):
        if k in res_big:
            return res_big[k][j][None]
        return res_small[k][j]

    out = [loss.reshape(()), dx.reshape(x.shape)]
    for j in range(4):
        out += [result(k, j) for k in order]
    return tuple(out)
```

```python
import math

import numpy as np
import jax
import jax.numpy as jnp
from jax import lax
from jax.experimental import pallas as pl
from jax.experimental.pallas import tpu as pltpu

F32 = jnp.float32
BF16 = jnp.bfloat16
ACT = BF16

N_DEV = 8
LN_EPS = 1e-5
ALPHA = (2.0 * 1) ** 0.25
HEAD_DIM = 64
GROUP_W = 512
QKV_W = 3 * GROUP_W
DILATIONS = (1, 4, 16)
RADIUS = 64
LANES = 128
HALO = 8
HALO_BF16 = 16
ATT_TQ = 128

ADAM_LR = 0.001
ADAM_B1 = 0.9
ADAM_B2 = 0.999
ADAM_EPS = 1e-08
ADAM_WD = 0.01
ADAM_STEP = 10

VMEM_LIMIT = 52 * 1024 * 1024
OUT_TILE_BYTES = 8 * 1024 * 1024
MESH = pl.DeviceIdType.MESH
NT_DIMS = (((1,), (1,)), ((), ()))
TN_DIMS = (((0,), (0,)), ((), ()))


def _pick(n, target, align=LANES):
    if n <= target:
        return n
    best = None
    for t in range(align, target + 1, align):
        if n % t == 0:
            best = t
    assert best is not None, (n, target, align)
    return best


def _params(sems=None):
    return pltpu.CompilerParams(dimension_semantics=sems, vmem_limit_bytes=VMEM_LIMIT)


def _alibi_slopes():
    n = 3 * 8
    return np.exp2(-8.0 * np.arange(1, n + 1, dtype=np.float64) / n).astype(np.float32).reshape(3, 8)


def _ln_stats(r):
    mu = jnp.mean(r, -1, keepdims=True)
    xc = r - mu
    var = jnp.mean(xc * xc, -1, keepdims=True)
    rstd = lax.rsqrt(var + LN_EPS)
    return xc, rstd


def _load_natural(ref, d, scr):
    if d == 1:
        return ref[0]
    n, C = ref.shape[1], ref.shape[2]
    for c in range(C // LANES):
        for r in range(d):
            scr[c, pl.ds(r, n, stride=d), :] = ref[r, :, c * LANES:(c + 1) * LANES]
    return jnp.concatenate([scr[c] for c in range(C // LANES)], axis=1)


def _store_by_residue(val, ref, d, scr):
    if d == 1:
        ref[0] = val.astype(ref.dtype)
        return
    n, C = ref.shape[1], ref.shape[2]
    for c in range(C // LANES):
        scr[c] = val[:, c * LANES:(c + 1) * LANES]
    for c in range(C // LANES):
        for r in range(d):
            ref[r, :, c * LANES:(c + 1) * LANES] = scr[c, pl.ds(r, n, stride=d), :].astype(ref.dtype)


def _residue_spec(tm, d, C):
    return pl.BlockSpec((d, tm // d, C), lambda i: (0, i, 0))


def _residue_scratch(tm, C):
    return pltpu.VMEM((C // LANES, tm, LANES), F32)


def ln_fwd(a, res, g, b, name, dilations=(), gather=()):
    T, D = a.shape
    res_mm = isinstance(res, tuple)
    tm = _pick(T, 256 if res_mm else 512, 8)
    res_ins = list(res[1:]) if res_mm else ([] if res is None else [res])
    nd = len(dilations)
    ng = len(gather)
    n_in = 1 + len(res_ins) + 2
    last = T // tm - 1

    def body(*refs):
        a_ref = refs[0]
        r = a_ref[...]
        if res_mm:
            res_val = jnp.dot(refs[1][...], refs[2][...], preferred_element_type=F32) + refs[3][...]
            refs[-1 - n_scratch][...] = res_val
            r = ALPHA * r + res_val
        elif res_ins:
            r = ALPHA * r + refs[1][...]
        g_ref, b_ref = refs[n_in - 2], refs[n_in - 1]
        shard_refs = refs[n_in:n_in + ng]
        h_ref, hb_ref = refs[n_in + ng], refs[n_in + ng + 1]
        p_refs = refs[n_in + ng + 2:n_in + ng + 2 + nd]
        full_refs = refs[n_in + ng + 2 + nd:n_in + 2 * ng + 2 + nd]
        scratch = refs[len(refs) - n_scratch:]
        sems = scratch[len(scratch) - 3:] if ng else ()

        if ng:
            @pl.when(pl.program_id(0) == 0)
            def _():
                _gather_begin(shard_refs, full_refs, *sems)

        xc, rstd = _ln_stats(r)
        h = xc * rstd * g_ref[...] + b_ref[...]
        h_ref[...] = h
        hb_ref[...] = h.astype(BF16)
        for d, p_ref in zip(dilations, p_refs):
            _store_by_residue(h, p_ref, d, scratch[0])

        if ng:
            @pl.when(pl.program_id(0) == last)
            def _():
                _gather_finish(shard_refs, full_refs, *sems)

    row = pl.BlockSpec((tm, D), lambda i: (i, 0))
    vec = pl.BlockSpec((1, D), lambda i: (0, 0))
    hbm = pl.BlockSpec(memory_space=pl.ANY)
    if res_mm:
        res_specs = [pl.BlockSpec((tm, res[1].shape[1]), lambda i: (i, 0)), pl.BlockSpec(res[2].shape, lambda i: (0, 0)), vec]
    else:
        res_specs = [row] * len(res_ins)
    scratch_shapes = ([_residue_scratch(tm, D)] if nd else []) + (_gather_scratch(ng) if ng else [])
    n_scratch = len(scratch_shapes)
    ins = [a] + res_ins + [g, b] + list(gather)
    return pl.pallas_call(
        body, name=name, grid=(T // tm,),
        in_specs=[row] + res_specs + [vec, vec] + [hbm] * ng,
        out_specs=[row, row] + [_residue_spec(tm, d, D) for d in dilations] + [hbm] * ng + ([row] if res_mm else []),
        out_shape=[jax.ShapeDtypeStruct((T, D), F32), jax.ShapeDtypeStruct((T, D), BF16)]
        + [jax.ShapeDtypeStruct((d, T // d, D), BF16) for d in dilations]
        + [jax.ShapeDtypeStruct((N_DEV,) + s.shape, s.dtype) for s in gather]
        + ([jax.ShapeDtypeStruct((T, D), F32)] if res_mm else []),
        scratch_shapes=scratch_shapes,
        compiler_params=_params(("arbitrary",) if ng else ("parallel",)),
    )(*ins)


def ln_bwd(a, res, g, b, d1, d2, tgt, name, by_residue=()):
    T, D = a.shape
    tm = _pick(T, 256, 8)
    loss_mode = tgt is not None
    nres = len(by_residue)
    row = pl.BlockSpec((tm, D), lambda i: (i, 0))
    vec = pl.BlockSpec((1, D), lambda i: (0, 0))
    one = pl.BlockSpec((1, 1), lambda i: (0, 0))

    def rows_of(x):
        return pl.BlockSpec((tm, x.shape[1]), lambda i: (i, 0))

    def whole(x):
        return pl.BlockSpec(x.shape, lambda i: (0, 0))

    ins, in_specs, slots = [], [], {}

    def operand(key, arrays, specs):
        slots[key] = (len(ins), len(arrays))
        ins.extend(arrays)
        in_specs.extend(specs)

    operand("a", [a], [row])
    if isinstance(res, tuple):
        _, x, w, bias = res
        operand("res_mm", [x, w, bias], [rows_of(x), whole(w), vec])
    elif res is not None:
        operand("res", [res], [row])
    operand("gb", [g, b], [vec, vec])
    if loss_mode:
        operand("tgt", [tgt], [row])
    else:
        operand("d1", [d1], [row])
        if isinstance(d2, tuple):
            _, pieces, w = d2
            operand("d2_mm", list(pieces) + [w], [rows_of(p) for p in pieces] + [whole(w)])
        else:
            operand("d2", [d2], [row])
    operand("by_residue", [e for e, _ in by_residue], [_residue_spec(tm, d, D) for _, d in by_residue])
    n_in = len(ins)

    def body(*refs):
        def get(key):
            first, count = slots[key]
            return refs[first:first + count]

        dr_ref, drb_ref, dg_ref, db_ref, ds_ref, loss_ref = refs[n_in:n_in + 6]
        i = pl.program_id(0)

        @pl.when(i == 0)
        def _():
            dg_ref[...] = jnp.zeros_like(dg_ref)
            db_ref[...] = jnp.zeros_like(db_ref)
            ds_ref[...] = jnp.zeros_like(ds_ref)
            loss_ref[...] = jnp.zeros_like(loss_ref)

        r = get("a")[0][...]
        if "res_mm" in slots:
            x_ref, w_ref, bias_ref = get("res_mm")
            r = ALPHA * r + (jnp.dot(x_ref[...], w_ref[...], preferred_element_type=F32) + bias_ref[...])
        elif "res" in slots:
            r = ALPHA * r + get("res")[0][...]
        g_ref, b_ref = get("gb")
        xc, rstd = _ln_stats(r)
        xhat = xc * rstd
        gam = g_ref[...]
        if loss_mode:
            err = xhat * gam + b_ref[...] - get("tgt")[0][...]
            dy = err * (1.0 / D)
            row_loss = jnp.mean(err * err, -1, keepdims=True)
            loss_ref[...] += 0.5 * jnp.sum(row_loss, 0, keepdims=True)
        else:
            if "d2_mm" in slots:
                *p_refs, w_ref = get("d2_mm")
                av = p_refs[0][...] if len(p_refs) == 1 else jnp.concatenate([p[...] for p in p_refs], axis=1)
                d2v = lax.dot_general(av, w_ref[...], NT_DIMS, preferred_element_type=F32)
            else:
                d2v = get("d2")[0][...]
            dy = ALPHA * get("d1")[0][...] + d2v
        for (_, d), e_ref in zip(by_residue, get("by_residue")):
            dy = dy + _load_natural(e_ref, d, refs[-1])
        dyg = dy * gam
        c1 = jnp.mean(dyg, -1, keepdims=True)
        c2 = jnp.mean(dyg * xhat, -1, keepdims=True)
        dr = rstd * (dyg - c1 - xhat * c2)
        dr_ref[...] = dr
        drb_ref[...] = dr.astype(BF16)
        dg_ref[...] += jnp.sum(dy * xhat, 0, keepdims=True)
        db_ref[...] += jnp.sum(dy, 0, keepdims=True)
        ds_ref[...] += jnp.sum(dr, 0, keepdims=True)

    return pl.pallas_call(
        body, name=name, grid=(T // tm,),
        in_specs=in_specs,
        out_specs=[row, row, vec, vec, vec, one],
        out_shape=[jax.ShapeDtypeStruct((T, D), F32), jax.ShapeDtypeStruct((T, D), BF16),
                   jax.ShapeDtypeStruct((1, D), F32), jax.ShapeDtypeStruct((1, D), F32),
                   jax.ShapeDtypeStruct((1, D), F32), jax.ShapeDtypeStruct((1, 1), F32)],
        scratch_shapes=[_residue_scratch(tm, D)] if nres else [],
        compiler_params=_params(("arbitrary",)),
    )(*ins)


_TOKEN_SPEC = pl.BlockSpec((8, LANES), lambda i: (0, 0))


def mm_nn(a, w, bias, out_dtype, name, after=None):
    M, K = a.shape
    N = w.shape[1]
    tm = _pick(M, max(256, min(1024, OUT_TILE_BYTES // (N * jnp.dtype(out_dtype).itemsize))), 8)
    tc = _pick(N, 512)

    def body(a_ref, w_ref, b_ref, *rest):
        o_ref = rest[-1]
        av = a_ref[...]
        for j in range(N // tc):
            cols = slice(j * tc, (j + 1) * tc)
            acc = jnp.dot(av, w_ref[:, cols], preferred_element_type=F32)
            o_ref[:, cols] = (acc + b_ref[:, cols]).astype(out_dtype)

    return pl.pallas_call(
        body, name=name, grid=(M // tm,),
        in_specs=[pl.BlockSpec((tm, K), lambda i: (i, 0)),
                  pl.BlockSpec((K, N), lambda i: (0, 0)),
                  pl.BlockSpec((1, N), lambda i: (0, 0))] + ([] if after is None else [_TOKEN_SPEC]),
        out_specs=pl.BlockSpec((tm, N), lambda i: (i, 0)),
        out_shape=jax.ShapeDtypeStruct((M, N), out_dtype),
        compiler_params=_params(("parallel",)),
    )(a, w, bias, *([] if after is None else [after]))


def mm_nt(a, w, acc_in, name, after=None, w_block=0, out_dtype=F32):
    pieces = list(a) if isinstance(a, (list, tuple)) else [a]
    M = pieces[0].shape[0]
    widths = [p.shape[1] for p in pieces]
    K = sum(widths)
    N = w.shape[0]
    tm = _pick(M, 1024, 8)
    tc = _pick(N, 512)
    has_acc = acc_in is not None
    n_a = len(pieces)

    def body(*refs):
        a_refs, w_ref = refs[:n_a], refs[n_a]
        c_ref = refs[n_a + 1] if has_acc else None
        o_ref = refs[-1]
        av = a_refs[0][...] if n_a == 1 else jnp.concatenate([r[...] for r in a_refs], axis=1)
        for j in range(N // tc):
            cols = slice(j * tc, (j + 1) * tc)
            acc = lax.dot_general(av, w_ref[cols, :], NT_DIMS, preferred_element_type=F32)
            if has_acc:
                acc = acc + c_ref[:, cols]
            o_ref[:, cols] = acc.astype(out_dtype)

    out_spec = pl.BlockSpec((tm, N), lambda i: (i, 0))
    in_specs = [pl.BlockSpec((tm, kw), lambda i: (i, 0)) for kw in widths]
    in_specs.append(pl.BlockSpec((N, K), lambda i: (0, w_block)))
    ins = pieces + [w]
    if has_acc:
        in_specs.append(out_spec)
        ins.append(acc_in)
    if after is not None:
        in_specs.append(_TOKEN_SPEC)
        ins.append(after)
    return pl.pallas_call(
        body, name=name, grid=(M // tm,),
        in_specs=in_specs, out_specs=out_spec,
        out_shape=jax.ShapeDtypeStruct((M, N), out_dtype),
        compiler_params=_params(("parallel",)),
    )(*ins)


def mm_tn(a, b, name, out_dtype=BF16):
    pieces = list(b) if isinstance(b, (list, tuple)) else [b]
    T, M = a.shape
    widths = [p.shape[1] for p in pieces]
    N = sum(widths)
    tk = _pick(T, 1024, 8)
    nk = T // tk
    tc = _pick(M, 256)
    n_b = len(pieces)

    def body(*refs):
        a_ref, b_refs = refs[0], refs[1:1 + n_b]
        o_ref, cs_ref, acc_ref = refs[1 + n_b:]
        k = pl.program_id(0)

        @pl.when(k == 0)
        def _():
            acc_ref[...] = jnp.zeros_like(acc_ref)
            cs_ref[...] = jnp.zeros_like(cs_ref)

        bv = b_refs[0][...] if n_b == 1 else jnp.concatenate([r[...] for r in b_refs], axis=1)
        cs_ref[...] += jnp.sum(bv.astype(F32), 0, keepdims=True)
        for mi in range(M // tc):
            rows = slice(mi * tc, (mi + 1) * tc)
            acc_ref[rows, :] += lax.dot_general(a_ref[:, rows], bv, TN_DIMS, preferred_element_type=F32)

        @pl.when(k == nk - 1)
        def _():
            o_ref[...] = acc_ref[...].astype(out_dtype)

    return pl.pallas_call(
        body, name=name, grid=(nk,),
        in_specs=[pl.BlockSpec((tk, M), lambda k: (k, 0))] + [pl.BlockSpec((tk, wd), lambda k: (k, 0)) for wd in widths],
        out_specs=[pl.BlockSpec((M, N), lambda k: (0, 0)), pl.BlockSpec((1, N), lambda k: (0, 0))],
        out_shape=[jax.ShapeDtypeStruct((M, N), out_dtype), jax.ShapeDtypeStruct((1, N), F32)],
        scratch_shapes=[pltpu.VMEM((M, N), F32)],
        compiler_params=_params(("arbitrary",)),
    )(a, *pieces)


def _ext_rows(prev_ref, main_ref, next_ref, i, tm, T, dtype=F32):
    before = jnp.where(i == 0, 0.0, prev_ref[...])
    after = jnp.where(i == T // tm - 1, 0.0, next_ref[...])
    return jnp.concatenate([before, main_ref[...], after], axis=0).astype(dtype)


def _prev_row(x):
    return pltpu.roll(x, 1, 0)


def _next_row(x):
    return pltpu.roll(x, x.shape[0] - 1, 0)


def _conv3(u, w_ref):
    return _prev_row(u) * w_ref[0:1, :] + u * w_ref[1:2, :] + _next_row(u) * w_ref[2:3, :]


def _main(x, tm, halo=HALO):
    return x[halo:halo + tm]


def _halo_specs(tm, tc, T, col, order, halo=HALO):
    r = tm // halo
    last = T // halo - 1
    if order == "ij":
        return (pl.BlockSpec((halo, tc), lambda i, j: (jnp.maximum(i * r - 1, 0), col(j))),
                pl.BlockSpec((tm, tc), lambda i, j: (i, col(j))),
                pl.BlockSpec((halo, tc), lambda i, j: (jnp.minimum((i + 1) * r, last), col(j))))
    return (pl.BlockSpec((halo, tc), lambda j, i: (jnp.maximum(i * r - 1, 0), col(j))),
            pl.BlockSpec((tm, tc), lambda j, i: (i, col(j))),
            pl.BlockSpec((halo, tc), lambda j, i: (jnp.minimum((i + 1) * r, last), col(j))))


def conv_a_fwd(proj_a, conv_w, name):
    T, D3 = proj_a.shape
    D = D3 // 3
    tm = _pick(T, 256, 8)

    def body(p_ref, m_ref, n_ref, w_ref, o_ref):
        i = pl.program_id(0)
        ext = _ext_rows(p_ref, m_ref, n_ref, i, tm, T)
        u = ext[:, D:2 * D] * ext[:, 2 * D:]
        cu = _conv3(u, w_ref)
        o_ref[...] = (m_ref[:, :D].astype(F32) * _main(cu, tm, HALO_BF16)).astype(BF16)

    prev, main, nxt = _halo_specs(tm, D3, T, lambda j: 0, "ij", HALO_BF16)
    return pl.pallas_call(
        body, name=name, grid=(T // tm, 1),
        in_specs=[prev, main, nxt, pl.BlockSpec((3, D), lambda i, j: (0, 0))],
        out_specs=pl.BlockSpec((tm, D), lambda i, j: (i, 0)),
        out_shape=jax.ShapeDtypeStruct((T, D), BF16),
        compiler_params=_params(("parallel", "arbitrary")),
    )(proj_a, proj_a, proj_a, conv_w)


def conv_a_bwd(ds_a, proj_a, conv_w, name):
    T, D3 = proj_a.shape
    D = D3 // 3
    tm = _pick(T, 256, 8)

    def body(dp_ref, dm_ref, dn_ref, p_ref, m_ref, n_ref, w_ref, o_ref, dw_ref):
        i = pl.program_id(0)

        @pl.when(i == 0)
        def _():
            dw_ref[...] = jnp.zeros_like(dw_ref)

        ext = _ext_rows(p_ref, m_ref, n_ref, i, tm, T)
        dsa = _ext_rows(dp_ref, dm_ref, dn_ref, i, tm, T)
        gb, gc, hin = ext[:, :D], ext[:, D:2 * D], ext[:, 2 * D:]
        u = gc * hin
        u_prev, u_next = _prev_row(u), _next_row(u)
        cu = u_prev * w_ref[0:1, :] + u * w_ref[1:2, :] + u_next * w_ref[2:3, :]
        dcu = dsa * gb
        du = _next_row(dcu) * w_ref[0:1, :] + dcu * w_ref[1:2, :] + _prev_row(dcu) * w_ref[2:3, :]
        h = HALO_BF16
        o_ref[:, :D] = _main(dsa * cu, tm, h).astype(BF16)
        o_ref[:, D:2 * D] = _main(du * hin, tm, h).astype(BF16)
        o_ref[:, 2 * D:] = _main(du * gc, tm, h).astype(BF16)
        dcu_m = _main(dcu, tm, h)
        dw_ref[0:1, :] += jnp.sum(dcu_m * _main(u_prev, tm, h), 0, keepdims=True)
        dw_ref[1:2, :] += jnp.sum(dcu_m * _main(u, tm, h), 0, keepdims=True)
        dw_ref[2:3, :] += jnp.sum(dcu_m * _main(u_next, tm, h), 0, keepdims=True)

    dprev, dmain, dnxt = _halo_specs(tm, D, T, lambda j: 0, "ij", HALO_BF16)
    prev, main, nxt = _halo_specs(tm, D3, T, lambda j: 0, "ij", HALO_BF16)
    return pl.pallas_call(
        body, name=name, grid=(T // tm, 1),
        in_specs=[dprev, dmain, dnxt, prev, main, nxt, pl.BlockSpec((3, D), lambda i, j: (0, 0))],
        out_specs=[pl.BlockSpec((tm, D3), lambda i, j: (i, 0)), pl.BlockSpec((3, D), lambda i, j: (0, 0))],
        out_shape=[jax.ShapeDtypeStruct((T, D3), BF16), jax.ShapeDtypeStruct((3, D), F32)],
        compiler_params=_params(("arbitrary", "arbitrary")),
    )(ds_a, ds_a, ds_a, proj_a, proj_a, proj_a, conv_w)


_INV_SQRT2 = 1.0 / math.sqrt(2.0)
_INV_SQRT_2PI = 1.0 / math.sqrt(2.0 * math.pi)


def ffn_up_conv_f(h, w_up, b_up, fcw, fcb, name):
    T, D = h.shape
    F = fcb.shape[1]
    tm = _pick(T, 256, 8)
    tc = _pick(F, 256)
    halo = HALO_BF16

    def body(hp_ref, hm_ref, hn_ref, w_ref, b_ref, cw_ref, cb_ref, up_ref, f_ref):
        i = pl.program_id(0)
        h_ext = _ext_rows(hp_ref, hm_ref, hn_ref, i, tm, T, dtype=BF16)
        h_main = hm_ref[...]
        rows = i * tm - halo + lax.broadcasted_iota(jnp.int32, (tm + 2 * halo, 1), 0)
        inside = (rows >= 0) & (rows < T)
        for c in range(F // tc):
            cols = slice(c * tc, (c + 1) * tc)
            gcols = slice(F + c * tc, F + (c + 1) * tc)
            a_ext = jnp.dot(h_ext, w_ref[:, cols], preferred_element_type=F32) + b_ref[:, cols]
            a_ext = jnp.where(inside, a_ext, 0.0)
            gate = jnp.dot(h_main, w_ref[:, gcols], preferred_element_type=F32) + b_ref[:, gcols]
            up_ref[:, cols] = _main(a_ext, tm, halo)
            up_ref[:, gcols] = gate
            ca = _main(_prev_row(a_ext) * cw_ref[0:1, cols] + a_ext * cw_ref[1:2, cols]
                       + _next_row(a_ext) * cw_ref[2:3, cols], tm, halo) + cb_ref[:, cols]
            gl = 0.5 * ca * (1.0 + lax.erf(ca * _INV_SQRT2))
            f_ref[:, cols] = (gl * gate).astype(BF16)

    prev, main, nxt = _halo_specs(tm, D, T, lambda j: 0, "ij", halo)
    whole = lambda x: pl.BlockSpec(x.shape, lambda i, j: (0, 0))
    return pl.pallas_call(
        body, name=name, grid=(T // tm, 1),
        in_specs=[prev, main, nxt, whole(w_up), whole(b_up), whole(fcw), whole(fcb)],
        out_specs=[pl.BlockSpec((tm, 2 * F), lambda i, j: (i, 0)), pl.BlockSpec((tm, F), lambda i, j: (i, 0))],
        out_shape=[jax.ShapeDtypeStruct((T, 2 * F), F32), jax.ShapeDtypeStruct((T, F), BF16)],
        compiler_params=_params(("parallel", "arbitrary")),
    )(h, h, h, w_up, b_up, fcw, fcb)


def conv_f_bwd(dy, w_down, up, fcw, fcb, name):
    T, F2 = up.shape
    F = F2 // 2
    D = dy.shape[1]
    tm = _pick(T, 256, 8)
    tc = _pick(F, 256)

    def body(yp_ref, ym_ref, yn_ref, wd_ref, up_ref, um_ref, un_ref, w_ref, b_ref,
             da_ref, dg_ref, csa_ref, csg_ref, dfb_ref, dfw_ref):
        i = pl.program_id(0)
        first, last = i == 0, i == T // tm - 1

        @pl.when(first)
        def _():
            csa_ref[...] = jnp.zeros_like(csa_ref)
            csg_ref[...] = jnp.zeros_like(csg_ref)
            dfb_ref[...] = jnp.zeros_like(dfb_ref)
            dfw_ref[...] = jnp.zeros_like(dfw_ref)

        def ext(cols):
            return jnp.concatenate([jnp.where(first, 0.0, up_ref[:, cols]), um_ref[:, cols],
                                    jnp.where(last, 0.0, un_ref[:, cols])], axis=0)

        dy_ext = _ext_rows(yp_ref, ym_ref, yn_ref, i, tm, T, dtype=BF16)
        for c in range(F // tc):
            cols = slice(c * tc, (c + 1) * tc)
            dfe = lax.dot_general(dy_ext, wd_ref[cols, :], NT_DIMS, preferred_element_type=F32)
            dfe = dfe[HALO_BF16 - HALO:HALO_BF16 + tm + HALO]
            a = ext(cols)
            gate = ext(slice(F + c * tc, F + (c + 1) * tc))
            a_prev, a_next = _prev_row(a), _next_row(a)
            ca = a_prev * w_ref[0:1, cols] + a * w_ref[1:2, cols] + a_next * w_ref[2:3, cols] + b_ref[:, cols]
            cdf = 0.5 * (1.0 + lax.erf(ca * _INV_SQRT2))
            gl = ca * cdf
            gp = cdf + ca * (jnp.exp(-0.5 * ca * ca) * _INV_SQRT_2PI)
            dgate = _main(dfe * gl, tm)
            dca = dfe * gate * gp
            da = _main(_next_row(dca) * w_ref[0:1, cols] + dca * w_ref[1:2, cols] + _prev_row(dca) * w_ref[2:3, cols],
                       tm)
            da_ref[:, cols] = da.astype(BF16)
            dg_ref[:, cols] = dgate.astype(BF16)
            csa_ref[:, cols] += jnp.sum(da, 0, keepdims=True)
            csg_ref[:, cols] += jnp.sum(dgate, 0, keepdims=True)
            dca_m = _main(dca, tm)
            dfb_ref[:, cols] += jnp.sum(dca_m, 0, keepdims=True)
            dfw_ref[0:1, cols] += jnp.sum(dca_m * _main(a_prev, tm), 0, keepdims=True)
            dfw_ref[1:2, cols] += jnp.sum(dca_m * _main(a, tm), 0, keepdims=True)
            dfw_ref[2:3, cols] += jnp.sum(dca_m * _main(a_next, tm), 0, keepdims=True)

    uprev, umain, unxt = _halo_specs(tm, F2, T, lambda j: 0, "ij")
    yprev, ymain, ynxt = _halo_specs(tm, D, T, lambda j: 0, "ij", HALO_BF16)
    whole = lambda shape: pl.BlockSpec(shape, lambda i, j: (0, 0))
    tile = pl.BlockSpec((tm, F), lambda i, j: (i, 0))
    return pl.pallas_call(
        body, name=name, grid=(T // tm, 1),
        in_specs=[yprev, ymain, ynxt, whole((F, D)), uprev, umain, unxt, whole((3, F)), whole((1, F))],
        out_specs=[tile, tile, whole((1, F)), whole((1, F)), whole((1, F)), whole((3, F))],
        out_shape=[jax.ShapeDtypeStruct((T, F), BF16), jax.ShapeDtypeStruct((T, F), BF16),
                   jax.ShapeDtypeStruct((1, F), F32), jax.ShapeDtypeStruct((1, F), F32),
                   jax.ShapeDtypeStruct((1, F), F32), jax.ShapeDtypeStruct((3, F), F32)],
        compiler_params=_params(("arbitrary", "arbitrary")),
    )(dy, dy, dy, w_down, up, up, up, fcw, fcb)


def gate_fwd(proj_g, y_a, y_b, name):
    T, D = y_a.shape
    tm = _pick(T, 512, 8)

    def body(g_ref, a_ref, b_ref, o_ref):
        sa = jax.nn.sigmoid(g_ref[:, :D].astype(F32))
        sb = jax.nn.sigmoid(g_ref[:, D:].astype(F32))
        o_ref[...] = (sa * a_ref[...].astype(F32) + sb * b_ref[...].astype(F32)).astype(BF16)

    row = pl.BlockSpec((tm, D), lambda i: (i, 0))
    return pl.pallas_call(
        body, name=name, grid=(T // tm,),
        in_specs=[pl.BlockSpec((tm, 2 * D), lambda i: (i, 0)), row, row],
        out_specs=row,
        out_shape=jax.ShapeDtypeStruct((T, D), BF16),
        compiler_params=_params(("parallel",)),
    )(proj_g, y_a, y_b)


def gate_bwd(dz, proj_g, y_a, y_b, name):
    T, D = y_a.shape
    tm = _pick(T, 512, 8)

    def body(dz_ref, g_ref, a_ref, b_ref, da_ref, db_ref, dg_ref):
        dzv = dz_ref[...].astype(F32)
        sa = jax.nn.sigmoid(g_ref[:, :D].astype(F32))
        sb = jax.nn.sigmoid(g_ref[:, D:].astype(F32))
        da_ref[...] = (dzv * sa).astype(BF16)
        db_ref[...] = (dzv * sb).astype(BF16)
        dg_ref[:, :D] = (dzv * a_ref[...].astype(F32) * (sa * (1.0 - sa))).astype(BF16)
        dg_ref[:, D:] = (dzv * b_ref[...].astype(F32) * (sb * (1.0 - sb))).astype(BF16)

    row = pl.BlockSpec((tm, D), lambda i: (i, 0))
    wide = pl.BlockSpec((tm, 2 * D), lambda i: (i, 0))
    return pl.pallas_call(
        body, name=name, grid=(T // tm,),
        in_specs=[row, wide, row, row],
        out_specs=[row, row, wide],
        out_shape=[jax.ShapeDtypeStruct((T, D), BF16), jax.ShapeDtypeStruct((T, D), BF16),
                   jax.ShapeDtypeStruct((T, 2 * D), BF16)],
        compiler_params=_params(("parallel",)),
    )(dz, proj_g, y_a, y_b)


ATT_WIN = ATT_TQ + 2 * RADIUS
ATT_STEP = 1024
FAR = 1e32


def _att_window(qs, L):
    ks = pl.multiple_of(jnp.clip(qs - RADIUS, 0, L - ATT_WIN), RADIUS)
    return ks, jnp.where(qs == 0, 0, jnp.where(qs == L - ATT_TQ, 2, 1))


def _fill_bias_tables(bias_ref, sl_ref, hp, d):
    col_row = (lax.broadcasted_iota(jnp.int32, (ATT_TQ, ATT_WIN), 1)
               - lax.broadcasted_iota(jnp.int32, (ATT_TQ, ATT_WIN), 0))
    for v in range(3):
        ad = jnp.abs(col_row - v * RADIUS)
        dist = jnp.where(ad <= RADIUS, (ad * d).astype(F32), FAR)
        bias_ref[v, 0:ATT_TQ, :] = sl_ref[hp * 2] * dist
        bias_ref[v, ATT_TQ:2 * ATT_TQ, :] = sl_ref[hp * 2 + 1] * dist


def _head_masks():
    lane = lax.broadcasted_iota(jnp.int32, (1, LANES), 1)
    return [lane < HEAD_DIM, lane >= HEAD_DIM]


def _stack_heads(x, masks):
    zero = jnp.zeros_like(x)
    return jnp.concatenate([jnp.where(masks[0], x, zero), jnp.where(masks[1], x, zero)], axis=0)


def _unstack_heads(x2, masks):
    n = x2.shape[0] // 2
    return jnp.where(masks[0], x2[:n], x2[n:])


def _att_step(L):
    step = min(ATT_STEP, L)
    assert L % step == 0 and step % ATT_TQ == 0 and L >= ATT_WIN
    return step


def _residues_per_step(d, L):
    rps = max(1, min(d, ATT_STEP // L))
    assert d % rps == 0
    return rps


def att_fwd(qkv, group, name):
    d, L, _ = qkv.shape
    step = _att_step(L)
    rps = _residues_per_step(d, L)
    cg = GROUP_W // LANES
    slopes = jnp.asarray(_alibi_slopes()[group])
    scale = HEAD_DIM ** -0.5

    def body(sl_ref, q_ref, k_ref, v_ref, o_ref, l_ref, bias_ref, s_ref, p_ref):
        hp = pl.program_id(1)
        i = pl.program_id(2)

        @pl.when(i == 0)
        def _():
            _fill_bias_tables(bias_ref, sl_ref, hp, d)

        masks = _head_masks()
        per = step // ATT_TQ
        tiles = [(rr, t) for rr in range(rps) for t in range(per)]
        windows = [_att_window(i * step + t * ATT_TQ, L) for t in range(per)]
        for n, (rr, t) in enumerate(tiles):
            rows = slice(t * ATT_TQ, (t + 1) * ATT_TQ)
            ks, table = windows[t]
            q2 = _stack_heads(q_ref[rr, rows, :] * scale, masks)
            kw = k_ref[rr, pl.ds(ks, ATT_WIN), :]
            s_ref[n] = lax.dot_general(q2, kw, NT_DIMS, preferred_element_type=F32) - bias_ref[table]
        for n, (rr, t) in enumerate(tiles):
            rows = slice(t * ATT_TQ, (t + 1) * ATT_TQ)
            s = s_ref[n]
            m = jnp.max(s, -1, keepdims=True)
            p = jnp.exp(s - m)
            den = jnp.sum(p, -1, keepdims=True)
            p_ref[n] = (p / den).astype(BF16)
            l_ref[rr, rows, :] = _unstack_heads(m + jnp.log(den), masks)
        for n, (rr, t) in enumerate(tiles):
            rows = slice(t * ATT_TQ, (t + 1) * ATT_TQ)
            vw = v_ref[rr, pl.ds(windows[t][0], ATT_WIN), :]
            o2 = jnp.dot(p_ref[n], vw, preferred_element_type=F32)
            o_ref[rr, rows, :] = _unstack_heads(o2, masks)

    n_tiles = rps * step // ATT_TQ
    out_spec = pl.BlockSpec((rps, step, LANES), lambda r, hp, i: (r, i, hp))
    return pl.pallas_call(
        body, name=name, grid=(d // rps, cg, L // step),
        in_specs=[pl.BlockSpec(memory_space=pltpu.SMEM),
                  pl.BlockSpec((rps, step, LANES), lambda r, hp, i: (r, i, hp)),
                  pl.BlockSpec((rps, L, LANES), lambda r, hp, i: (r, 0, cg + hp)),
                  pl.BlockSpec((rps, L, LANES), lambda r, hp, i: (r, 0, 2 * cg + hp))],
        out_specs=[out_spec, out_spec],
        out_shape=[jax.ShapeDtypeStruct((d, L, GROUP_W), F32)] * 2,
        scratch_shapes=[pltpu.VMEM((3, 2 * ATT_TQ, ATT_WIN), F32),
                        pltpu.VMEM((n_tiles, 2 * ATT_TQ, ATT_WIN), F32),
                        pltpu.VMEM((n_tiles, 2 * ATT_TQ, ATT_WIN), BF16)],
        compiler_params=_params(("arbitrary", "arbitrary", "arbitrary")),
    )(slopes, qkv, qkv, qkv)


def att_bwd(qkv, do, lse, dmat, group, name, after=None):
    d, L, _ = qkv.shape
    step = _att_step(L)
    rps = _residues_per_step(d, L)
    nq = L // step
    cg = GROUP_W // LANES
    slopes = jnp.asarray(_alibi_slopes()[group])
    scale = HEAD_DIM ** -0.5

    def body(sl_ref, q_ref, k_ref, v_ref, do_ref, l_ref, dm_ref, *rest):
        dq_ref, dk_ref, dv_ref, dk_acc, dv_acc, bias_ref, s_ref, dp_ref, p_ref, ds_ref = rest[len(rest) - 10:]
        hp = pl.program_id(1)
        i = pl.program_id(2)

        @pl.when(i == 0)
        def _():
            dk_acc[...] = jnp.zeros_like(dk_acc)
            dv_acc[...] = jnp.zeros_like(dv_acc)
            _fill_bias_tables(bias_ref, sl_ref, hp, d)

        masks = _head_masks()

        def head_cols(x):
            return jnp.concatenate([jnp.max(jnp.where(hm, x, -jnp.inf), -1, keepdims=True) for hm in masks], axis=0)

        per = step // ATT_TQ
        tiles = [(rr, t) for rr in range(rps) for t in range(per)]
        windows = [_att_window(i * step + t * ATT_TQ, L) for t in range(per)]

        def stacked(ref, rr, t, factor=None):
            x = ref[rr, t * ATT_TQ:(t + 1) * ATT_TQ, :]
            return _stack_heads(x if factor is None else x * factor, masks)

        for n, (rr, t) in enumerate(tiles):
            ks, table = windows[t]
            q2 = stacked(q_ref, rr, t, scale)
            s_ref[n] = lax.dot_general(q2, k_ref[rr, pl.ds(ks, ATT_WIN), :], NT_DIMS,
                                       preferred_element_type=F32) - bias_ref[table]
            dp_ref[n] = lax.dot_general(stacked(do_ref, rr, t), v_ref[rr, pl.ds(ks, ATT_WIN), :], NT_DIMS,
                                        preferred_element_type=F32)
        for n, (rr, t) in enumerate(tiles):
            rows = slice(t * ATT_TQ, (t + 1) * ATT_TQ)
            p = jnp.exp(s_ref[n] - head_cols(l_ref[rr, rows, :]))
            p_ref[n] = p.astype(BF16)
            ds_ref[n] = (p * (dp_ref[n] - head_cols(dm_ref[rr, rows, :]))).astype(BF16)
        for n, (rr, t) in enumerate(tiles):
            rows = slice(t * ATT_TQ, (t + 1) * ATT_TQ)
            ks = windows[t][0]
            ds = ds_ref[n]
            dq2 = jnp.dot(ds, k_ref[rr, pl.ds(ks, ATT_WIN), :], preferred_element_type=F32)
            dq_ref[rr, rows, :] = (_unstack_heads(dq2, masks) * scale).astype(BF16)
            dk_acc[rr, pl.ds(ks, ATT_WIN), :] += lax.dot_general(ds, stacked(q_ref, rr, t, scale), TN_DIMS,
                                                                 preferred_element_type=F32)
            dv_acc[rr, pl.ds(ks, ATT_WIN), :] += lax.dot_general(p_ref[n], stacked(do_ref, rr, t), TN_DIMS,
                                                                 preferred_element_type=F32)

        @pl.when(i == nq - 1)
        def _():
            dk_ref[...] = dk_acc[...].astype(BF16)
            dv_ref[...] = dv_acc[...].astype(BF16)

    tile = pl.BlockSpec((rps, step, LANES), lambda r, hp, i: (r, i, hp))
    whole = pl.BlockSpec((rps, L, LANES), lambda r, hp, i: (r, 0, hp))
    return pl.pallas_call(
        body, name=name, grid=(d // rps, cg, nq),
        in_specs=[pl.BlockSpec(memory_space=pltpu.SMEM), tile,
                  pl.BlockSpec((rps, L, LANES), lambda r, hp, i: (r, 0, cg + hp)),
                  pl.BlockSpec((rps, L, LANES), lambda r, hp, i: (r, 0, 2 * cg + hp)),
                  tile, tile, tile] + ([] if after is None else [pl.BlockSpec((8, LANES), lambda r, hp, i: (0, 0))]),
        out_specs=[tile, whole, whole],
        out_shape=[jax.ShapeDtypeStruct((d, L, GROUP_W), BF16)] * 3,
        scratch_shapes=[pltpu.VMEM((rps, L, LANES), F32), pltpu.VMEM((rps, L, LANES), F32),
                        pltpu.VMEM((3, 2 * ATT_TQ, ATT_WIN), F32)]
        + [pltpu.VMEM((rps * step // ATT_TQ, 2 * ATT_TQ, ATT_WIN), dt) for dt in (F32, F32, BF16, BF16)],
        compiler_params=_params(("arbitrary", "arbitrary", "arbitrary")),
    )(slopes, qkv, qkv, qkv, do, lse, dmat, *([] if after is None else [after]))


def _group_weights(ls):
    m = jnp.maximum(jnp.maximum(ls[0], ls[1]), ls[2])
    es = [jnp.exp(l - m) for l in ls]
    tot = es[0] + es[1] + es[2]
    return [e / tot for e in es]


def combine_fwd(outs, lses, name):
    T = outs[0].shape[0] * outs[0].shape[1]
    tm = _pick(T, 512, 8)
    n_scr = 2 * (len(DILATIONS) - 1)

    def body(*refs):
        o_refs, l_refs, c_ref, scr = refs[:3], refs[3:6], refs[6], refs[7:]
        o = [_load_natural(o_refs[g], d, scr[g - 1] if g else None) for g, d in enumerate(DILATIONS)]
        l = [_load_natural(l_refs[g], d, scr[g + 1] if g else None) for g, d in enumerate(DILATIONS)]
        w = _group_weights(l)
        c_ref[...] = (w[0] * o[0] + w[1] * o[1] + w[2] * o[2]).astype(BF16)

    specs = [_residue_spec(tm, d, GROUP_W) for d in DILATIONS]
    return pl.pallas_call(
        body, name=name, grid=(T // tm,),
        in_specs=specs + specs, out_specs=pl.BlockSpec((tm, GROUP_W), lambda i: (i, 0)),
        out_shape=jax.ShapeDtypeStruct((T, GROUP_W), BF16),
        scratch_shapes=[_residue_scratch(tm, GROUP_W)] * n_scr,
        compiler_params=_params(("parallel",)),
    )(*outs, *lses)


def combine_bwd(dcomb, outs, lses, name):
    T = dcomb.shape[0]
    tm = _pick(T, 256, 8)
    head = np.arange(GROUP_W) // HEAD_DIM
    seg = jnp.asarray((head[:, None] == head[None, :]).astype(np.float32)).astype(BF16)
    ng = len(DILATIONS)
    n_scr = 4 * (ng - 1)

    def body(*refs):
        dc_ref, o_refs, l_refs, e_ref = refs[0], refs[1:1 + ng], refs[1 + ng:1 + 2 * ng], refs[1 + 2 * ng]
        do_refs, dm_refs = refs[2 + 2 * ng:2 + 3 * ng], refs[2 + 3 * ng:2 + 4 * ng]
        scr = refs[2 + 4 * ng:]
        o = [_load_natural(o_refs[g], d, scr[4 * (g - 1)] if g else None) for g, d in enumerate(DILATIONS)]
        l = [_load_natural(l_refs[g], d, scr[4 * (g - 1) + 1] if g else None) for g, d in enumerate(DILATIONS)]
        w = _group_weights(l)
        dc = dc_ref[...].astype(F32)
        e = e_ref[...]
        prod = dc * (w[0] * o[0] + w[1] * o[1] + w[2] * o[2])
        tot = jnp.zeros_like(dc)
        for _ in range(3):
            part = prod.astype(BF16)
            tot = tot + jnp.dot(part, e, preferred_element_type=F32)
            prod = prod - part.astype(F32)
        for g, d in enumerate(DILATIONS):
            _store_by_residue(w[g] * dc, do_refs[g], d, scr[4 * (g - 1) + 2] if g else None)
            _store_by_residue(w[g] * tot, dm_refs[g], d, scr[4 * (g - 1) + 3] if g else None)

    specs = [_residue_spec(tm, d, GROUP_W) for d in DILATIONS]
    res = pl.pallas_call(
        body, name=name, grid=(T // tm,),
        in_specs=[pl.BlockSpec((tm, GROUP_W), lambda i: (i, 0))] + specs + specs
        + [pl.BlockSpec((GROUP_W, GROUP_W), lambda i: (0, 0))],
        out_specs=specs + specs,
        out_shape=[jax.ShapeDtypeStruct(o.shape, BF16) for o in outs] + [jax.ShapeDtypeStruct(o.shape, F32) for o in outs],
        scratch_shapes=[_residue_scratch(tm, GROUP_W)] * n_scr,
        compiler_params=_params(("parallel",)),
    )(dcomb, *outs, *lses, seg)
    return res[:ng], res[ng:]


def _position():
    return lax.axis_index("x"), lax.axis_index("y"), lax.axis_index("c")


def _other_chips(x, y):
    return [(1 - x, y), (x, 1 - y), (1 - x, 1 - y)]


def _remote(src, dst, send_sems, recv_sems, k, to):
    return pltpu.make_async_remote_copy(src_ref=src, dst_ref=dst, send_sem=send_sems.at[k], recv_sem=recv_sems.at[k],
                                        device_id=to, device_id_type=MESH)


GATHER_SEMS = 10
SPLIT_ROWS = 32


def _gather_plan(ins, outs, send_sems, recv_sems, local_sems):
    x, y, c = _position()
    sibling = (x, y, 1 - c)
    nbr_x, nbr_y, diag = (1 - x, y, c), (x, 1 - y, c), (1 - x, 1 - y, c)
    local, begin, stages, last = [], [], [], []
    for a in range(len(ins)):
        k0 = GATHER_SEMS * a
        rows = ins[a].shape[0]
        half = rows // 2

        def block(dev):
            return outs[a].at[4 * dev[0] + 2 * dev[1] + dev[2]]

        def part(ref, h):
            return ref.at[pl.ds(h * half, half)]

        def copy(k, src, dst, to):
            return _remote(src, dst, send_sems, recv_sems, k0 + k, to)

        me = (x, y, c)
        local.append(pltpu.make_async_copy(ins[a], block(me), local_sems.at[a]))
        begin.append(copy(0, ins[a], block(me), sibling))
        pass_on = [copy(7 + j, block(dev), block(dev), sibling) for j, dev in enumerate((nbr_x, nbr_y, diag))]
        if rows >= SPLIT_ROWS and rows % SPLIT_ROWS == 0:
            for h in range(2):
                begin.append(copy(1 + h, part(ins[a], h), part(block(me), h), nbr_x))
                begin.append(copy(3 + h, part(ins[a], h), part(block(me), h), nbr_y))
            from_x = [copy(1 + h, part(block(nbr_x), h), part(block(nbr_x), h), sibling) for h in range(2)]
            from_y = [copy(3 + h, part(block(nbr_y), h), part(block(nbr_y), h), sibling) for h in range(2)]
            fwd_0 = copy(5, part(block(nbr_x), 0), part(block(nbr_x), 0), nbr_y)
            fwd_1 = copy(6, part(block(nbr_y), 1), part(block(nbr_y), 1), nbr_x)
            got_0 = copy(5, part(block(diag), 0), part(block(diag), 0), sibling)
            got_1 = copy(6, part(block(diag), 1), part(block(diag), 1), sibling)
            stages.append(([from_x[0]], [fwd_0]))
            stages.append(([from_y[1]], [fwd_1]))
            stages.append(([from_x[1]], [pass_on[0]]))
            stages.append(([from_y[0]], [pass_on[1]]))
            stages.append(([got_0, got_1], [pass_on[2]]))
        else:
            for j, dev in enumerate((nbr_x, nbr_y, diag)):
                begin.append(copy(1 + 2 * j, ins[a], block(me), dev))
                stages.append(([copy(1 + 2 * j, block(dev), block(dev), sibling)], [pass_on[j]]))
        other = (x, y, 1 - c)
        last.append(copy(0, block(other), block(other), sibling))
        for j, dev in enumerate((nbr_x, nbr_y, diag)):
            theirs = (dev[0], dev[1], 1 - c)
            last.append(copy(7 + j, block(theirs), block(theirs), sibling))
    return local, begin, stages, last


def _gather_begin(ins, outs, send_sems, recv_sems, local_sems):
    local, begin, _, _ = _gather_plan(ins, outs, send_sems, recv_sems, local_sems)
    for cp in local + begin:
        cp.start()


def _gather_finish(ins, outs, send_sems, recv_sems, local_sems):
    local, begin, stages, last = _gather_plan(ins, outs, send_sems, recv_sems, local_sems)
    started = []
    for arrivals, onward in stages:
        for cp in arrivals:
            cp.wait_recv()
        for cp in onward:
            cp.start()
            started.append(cp)
    for cp in last:
        cp.wait_recv()
    for cp in begin + started:
        cp.wait_send()
    for cp in local:
        cp.wait()


def _gather_scratch(n):
    return [pltpu.SemaphoreType.DMA((GATHER_SEMS * n,)), pltpu.SemaphoreType.DMA((GATHER_SEMS * n,)),
            pltpu.SemaphoreType.DMA((n,))]


_HBM = pl.BlockSpec(memory_space=pltpu.HBM)
_SEM = pl.BlockSpec(memory_space=pltpu.SEMAPHORE)
_DATAFLOW = pltpu.SideEffectType.DATAFLOW_SIDE_EFFECTING


def _to_all_plan(srcs, lands, send_sems, recv_sems):
    x, y, c = _position()
    me = 4 * x + 2 * y + c
    copies = []
    for a in range(len(srcs)):
        for k in range(1, N_DEV):
            fx, fy, fc = (k >> 2) & 1, (k >> 1) & 1, k & 1
            to = (1 - x if fx else x, 1 - y if fy else y, 1 - c if fc else c)
            copies.append(_remote(srcs[a], lands[a].at[me], send_sems, recv_sems, (N_DEV - 1) * a + k - 1, to))
    return copies


def _to_sibling_plan(srcs, lands, send_sems, recv_sems):
    x, y, c = _position()
    copies = []
    for a in range(len(srcs)):
        for q in range(4):
            copies.append(_remote(srcs[a].at[2 * q + (1 - c)], lands[a].at[q], send_sems, recv_sems, 4 * a + q,
                                  (x, y, 1 - c)))
    return copies


def _to_chips_plan(srcs, lands, send_sems, recv_sems):
    x, y, c = _position()
    copies = []
    for a in range(len(srcs)):
        for j, (cx, cy) in enumerate(_other_chips(x, y)):
            copies.append(_remote(srcs[a].at[2 * cx + cy], lands[a].at[j], send_sems, recv_sems, 3 * a + j, (cx, cy, c)))
    return copies


def copies_start(srcs, land_shapes, plan, per_array, name):
    n = len(srcs)
    n_sem = per_array * n
    lands = [lax.empty(s.shape, s.dtype) for s in land_shapes]

    def body(*refs):
        src_refs, land_refs = refs[:n], refs[n:2 * n]
        send_sems, recv_sems = refs[2 * n], refs[2 * n + 1]
        token = refs[-1]
        for cp in plan(src_refs, land_refs, send_sems, recv_sems):
            cp.start()
        token[...] = jnp.zeros_like(token)

    out = pl.pallas_call(
        body, name=name,
        out_shape=(pltpu.SemaphoreType.DMA((n_sem,)), pltpu.SemaphoreType.DMA((n_sem,)))
        + tuple(pltpu.HBM(s.shape, s.dtype) for s in srcs)
        + tuple(pltpu.HBM(s.shape, s.dtype) for s in land_shapes)
        + (jax.ShapeDtypeStruct((8, LANES), F32),),
        in_specs=[_HBM] * (2 * n),
        out_specs=(_SEM, _SEM) + (_HBM,) * (2 * n) + (pl.BlockSpec(memory_space=pltpu.VMEM),),
        input_output_aliases={i: 2 + i for i in range(2 * n)},
        compiler_params=pltpu.CompilerParams(has_side_effects=_DATAFLOW),
    )(*[pltpu.with_memory_space_constraint(s, pltpu.HBM) for s in srcs],
      *[pltpu.with_memory_space_constraint(l, pltpu.HBM) for l in lands])
    return out[:-1], out[-1]


def copies_wait(handles, plan, after, name):
    send_sems, recv_sems = handles[0], handles[1]
    n = (len(handles) - 2) // 2
    thru = handles[2:]

    def body(*refs):
        src_refs, land_refs = refs[:n], refs[n:2 * n]
        send_sems, recv_sems = refs[2 * n], refs[2 * n + 1]
        copies = plan(src_refs, land_refs, send_sems, recv_sems)
        for cp in copies:
            cp.wait_recv()
        for cp in copies:
            cp.wait_send()

    out = pl.pallas_call(
        body, name=name,
        out_shape=tuple(pltpu.HBM(t.shape, t.dtype) for t in thru),
        in_specs=[_HBM] * (2 * n) + [_SEM, _SEM, pl.BlockSpec(memory_space=pl.ANY)],
        out_specs=(_HBM,) * (2 * n),
        input_output_aliases={i: i for i in range(2 * n)},
        compiler_params=pltpu.CompilerParams(has_side_effects=_DATAFLOW),
    )(*thru, send_sems, recv_sems, after)
    return out[:n], out[n:]


def all_sum_small(vec, name):
    R = vec.shape[0]

    def body(v_ref, tot_ref, all_ref, send_sems, recv_sems):
        x, y, c = _position()
        me = 4 * x + 2 * y + c
        all_ref[me] = v_ref[...]
        copies = []
        for k in range(1, N_DEV):
            fx, fy, fc = (k >> 2) & 1, (k >> 1) & 1, k & 1
            to = (1 - x if fx else x, 1 - y if fy else y, 1 - c if fc else c)
            cp = _remote(v_ref, all_ref.at[me], send_sems, recv_sems, k - 1, to)
            cp.start()
            copies.append(cp)
        for cp in copies:
            cp.wait_recv()
        for cp in copies:
            cp.wait_send()
        tot = all_ref[0]
        for j in range(1, N_DEV):
            tot = tot + all_ref[j]
        tot_ref[...] = tot

    vmem = pl.BlockSpec(memory_space=pltpu.VMEM)
    return pl.pallas_call(
        body, name=name,
        in_specs=[vmem], out_specs=vmem,
        out_shape=jax.ShapeDtypeStruct((R, LANES), F32),
        scratch_shapes=[pltpu.VMEM((N_DEV, R, LANES), F32),
                        pltpu.SemaphoreType.DMA((N_DEV - 1,)), pltpu.SemaphoreType.DMA((N_DEV - 1,))],
        compiler_params=pltpu.CompilerParams(vmem_limit_bytes=VMEM_LIMIT),
    )(vec)


def pair_add(parts, theirs, place, name):
    _, R, C = theirs.shape
    tr = _pick(R, 256, 8)

    def body(place_ref, a_ref, b_ref, o_ref):
        o_ref[...] = (a_ref[...].astype(F32) + b_ref[...].astype(F32)).astype(BF16)

    blk = pl.BlockSpec((None, tr, C), lambda q, i, place_ref: (q, i, 0))
    return pl.pallas_call(
        body, name=name,
        grid_spec=pltpu.PrefetchScalarGridSpec(
            num_scalar_prefetch=1, grid=(4, R // tr),
            in_specs=[pl.BlockSpec((None, tr, C), lambda q, i, place_ref: (2 * q + place_ref[2], i, 0)), blk],
            out_specs=blk),
        out_shape=jax.ShapeDtypeStruct(theirs.shape, BF16),
        compiler_params=_params(("parallel", "parallel")),
    )(place, parts, theirs)


def _adamw_math(w, g, m, v):
    m = ADAM_B1 * m + (1.0 - ADAM_B1) * g
    v = ADAM_B2 * v + (1.0 - ADAM_B2) * jnp.square(g)
    m_hat = m / (1.0 - ADAM_B1 ** ADAM_STEP)
    v_hat = v / (1.0 - ADAM_B2 ** ADAM_STEP)
    delta = -ADAM_LR * (m_hat / (jnp.sqrt(v_hat) + ADAM_EPS) + ADAM_WD * w)
    return delta, m, v


def adamw_sharded(w, m, v, parts, sib, others, place, name):
    R, C = w.shape
    tr = _pick(R, 256, 8)

    def body(place_ref, w_ref, m_ref, v_ref, a_ref, b_ref, o_ref, g_ref, d_ref, nm_ref, nv_ref):
        g = a_ref[...].astype(F32) + b_ref[...].astype(F32)
        for j in range(3):
            g = g + o_ref[j].astype(F32)
        delta, nm, nv = _adamw_math(w_ref[...], g, m_ref[...], v_ref[...])
        g_ref[...] = g
        d_ref[...] = delta
        nm_ref[...] = nm
        nv_ref[...] = nv

    row = pl.BlockSpec((tr, C), lambda i, place_ref: (i, 0))
    return pl.pallas_call(
        body, name=name,
        grid_spec=pltpu.PrefetchScalarGridSpec(
            num_scalar_prefetch=1, grid=(R // tr,),
            in_specs=[row] * 3 + [pl.BlockSpec((None, tr, C), lambda i, place_ref: (place_ref[0], i, 0)),
                                  pl.BlockSpec((None, tr, C), lambda i, place_ref: (place_ref[1], i, 0)),
                                  pl.BlockSpec((3, tr, C), lambda i, place_ref: (0, i, 0))],
            out_specs=[row] * 4),
        out_shape=[jax.ShapeDtypeStruct((R, C), F32)] * 4,
        compiler_params=_params(("parallel",)),
    )(place, w, m, v, parts, sib, others)


def adamw_packed(w, g, m, v, name):
    R = w.shape[0]

    def body(w_ref, g_ref, m_ref, v_ref, d_ref, nm_ref, nv_ref):
        delta, nm, nv = _adamw_math(w_ref[...], g_ref[...], m_ref[...], v_ref[...])
        d_ref[...] = delta
        nm_ref[...] = nm
        nv_ref[...] = nv

    full = pl.BlockSpec((R, LANES), lambda i: (0, 0))
    return pl.pallas_call(
        body, name=name, grid=(1,),
        in_specs=[full] * 4, out_specs=[full] * 3,
        out_shape=[jax.ShapeDtypeStruct((R, LANES), F32)] * 3,
        compiler_params=_params(("arbitrary",)),
    )(w, g, m, v)


def _pack(arrays):
    flat = []
    sizes = []
    for a in arrays:
        f = a.reshape(-1).astype(F32)
        pad = (-f.shape[0]) % LANES
        if pad:
            f = jnp.concatenate([f, jnp.zeros((pad,), F32)])
        flat.append(f)
        sizes.append(f.shape[0])
    rows = sum(sizes) // LANES
    pad_rows = (-rows) % 8
    if pad_rows:
        flat.append(jnp.zeros((pad_rows * LANES,), F32))
    return jnp.concatenate(flat).reshape(-1, LANES), sizes


def _unpack(packed, sizes, shapes):
    flat = packed.reshape(-1)
    out = []
    off = 0
    for size, shape in zip(sizes, shapes):
        n = int(np.prod(shape))
        out.append(flat[off:off + n].reshape(shape))
        off += size
    return out


def _to_blocks(full, axis):
    if axis == 0:
        return full.reshape(N_DEV, full.shape[0] // N_DEV, full.shape[1])
    r, n = full.shape
    return full.reshape(r, N_DEV, n // N_DEV).transpose(1, 0, 2)


def _from_blocks(blocks, axis):
    if axis == 0:
        return blocks.reshape(blocks.shape[0] * blocks.shape[1], blocks.shape[2])
    return blocks.transpose(1, 0, 2).reshape(blocks.shape[1], blocks.shape[0] * blocks.shape[2])


def kernel(x, ln0_g, ln0_b, w_in, b_in, conv_w, w_a, w_b, w_o, b_o, ln1_g, ln1_b, w_up, b_up, ffn_conv_w, ffn_conv_b, w_down, b_down, ln2_g, ln2_b, loss_target, m_ln0_g, m_ln0_b, m_w_in, m_b_in, m_conv_w, m_w_a, m_w_b, m_w_o, m_b_o, m_ln1_g, m_ln1_b, m_w_up, m_b_up, m_ffn_conv_w, m_ffn_conv_b, m_w_down, m_b_down, m_ln2_g, m_ln2_b, v_ln0_g, v_ln0_b, v_w_in, v_b_in, v_conv_w, v_w_a, v_w_b, v_w_o, v_b_o, v_ln1_g, v_ln1_b, v_w_up, v_b_up, v_ffn_conv_w, v_ffn_conv_b, v_w_down, v_b_down, v_ln2_g, v_ln2_b):
    T, D = x.shape[1], x.shape[2]
    F = ffn_conv_b.shape[-1]
    xs = x.reshape(T, D)
    tgt = loss_target.reshape(T, D)
    dev = 4 * lax.axis_index("x") + 2 * lax.axis_index("y") + lax.axis_index("c")
    chip = 2 * lax.axis_index("x") + lax.axis_index("y")
    core = lax.axis_index("c")
    place = jnp.stack([dev, chip, core]).astype(jnp.int32)

    big = dict(w_in=(w_in[0], 1), w_a=(w_a[0], 0), w_b=(w_b[0], 1), w_o=(w_o[0], 0), w_up=(w_up[0], 1),
               w_down=(w_down[0], 0))
    names = list(big)
    shards = {k: big[k][0].astype(BF16) for k in names}
    ln0g, ln0b = ln0_g.reshape(1, D), ln0_b.reshape(1, D)
    h0, h0b, *rest = ln_fwd(xs, None, ln0g, ln0b, "ln0_fwd_gather_w_in", dilations=DILATIONS[1:],
                            gather=[shards["w_in"], conv_w[0], ffn_conv_w[0]])
    h0_res = [h0b] + [h.reshape(T, D) for h in rest[:2]]
    g_in, g_conv, g_fcw = rest[2:]
    full = {"w_in": _from_blocks(g_in, 1)}
    conv_full = _from_blocks(g_conv, 1)
    fcw_full = _from_blocks(g_fcw, 1)
    late_groups = (("w_a", "w_b", "w_o"), ("w_up", "w_down"))
    late_handles = []
    token = conv_full[:1, :1] * 0.0
    for n, keys in enumerate(late_groups):
        srcs = [shards[k] + token[0, 0].astype(BF16) for k in keys]
        handles, token = copies_start(srcs, [jax.ShapeDtypeStruct((N_DEV,) + s.shape, BF16) for s in srcs],
                                      _to_all_plan, N_DEV - 1, f"gather_late_{n}_start")
        late_handles.append(handles)

    def late_weights(n, after):
        _, lands = copies_wait(late_handles[n], _to_all_plan, after, f"gather_late_{n}_wait")
        for k, land in zip(late_groups[n], lands):
            full[k] = _from_blocks(lax.dynamic_update_index_in_dim(land, shards[k], dev, 0), big[k][1])

    o_q = 3 * D
    o_g = o_q + 3 * QKV_W
    w_pa, w_qkv, w_pg = full["w_in"][:, :o_q], full["w_in"][:, o_q:o_g], full["w_in"][:, o_g:]
    b_pa, b_qkv, b_pg = b_in[:, :o_q], b_in[:, o_q:o_g], b_in[:, o_g:]

    proj_a = mm_nn(h0b, w_pa, b_pa, ACT, "proj_conv", after=token)
    proj_g = mm_nn(h0b, w_pg, b_pg, ACT, "proj_gates")
    zero_d = jnp.zeros((1, D), F32)
    s_a = conv_a_fwd(proj_a, conv_full, "conv_a_fwd")
    late_weights(0, s_a)
    y_a = mm_nn(s_a, full["w_a"], zero_d, ACT, "branch_a_out")

    def group_cols(m, g):
        return jnp.concatenate([m[:, s * QKV_W + g * GROUP_W:s * QKV_W + (g + 1) * GROUP_W] for s in range(3)], 1)

    w_grp = [group_cols(w_qkv, g) for g in range(3)]
    qkvs, outs, lses = [], [], []
    for g, d in enumerate(DILATIONS):
        qkv = mm_nn(h0_res[g], w_grp[g], group_cols(b_qkv, g), BF16, f"proj_qkv_{g}").reshape(d, T // d, 3 * GROUP_W)
        o, l = att_fwd(qkv, g, f"att_fwd_{g}")
        qkvs.append(qkv)
        outs.append(o)
        lses.append(l)
    comb = combine_fwd(outs, lses, "combine_fwd")
    y_b = mm_nn(comb, full["w_b"], zero_d, ACT, "branch_b_out")
    z = gate_fwd(proj_g, y_a, y_b, "gate_fwd")
    h1, h1b, mix = ln_fwd(h0, ("nn", z, full["w_o"], b_o), ln1_g, ln1_b, "mix_out_ln1_fwd")
    late_weights(1, h1b)
    up, f_act = ffn_up_conv_f(h1b, full["w_up"], b_up, fcw_full, ffn_conv_b, "ffn_up_conv_f")

    dr2, dr2b, d_ln2_g, d_ln2_b, d_b_down, loss_part = ln_bwd(
        h1, ("nn", f_act, full["w_down"], b_down), ln2_g, ln2_b, None, None, tgt, "ffn_down_ln2_loss_bwd")
    dw_down, _ = mm_tn(f_act, dr2b, "dw_down")
    d_a, d_gate, cs_a, cs_gate, d_fcb, d_fcw = conv_f_bwd(dr2b, full["w_down"], up, fcw_full, ffn_conv_b,
                                                          "d_ffn_act_conv_f_bwd")
    dw_up_a, _ = mm_tn(h1b, d_a, "dw_up_a")
    dw_up_g, _ = mm_tn(h1b, d_gate, "dw_up_gate")
    dr1, dr1b, d_ln1_g, d_ln1_b, d_b_o, _ = ln_bwd(h0, mix, ln1_g, ln1_b, dr2, ("nt", [d_a, d_gate], full["w_up"]), None,
                                                   "d_h1_ln1_bwd")
    dw_o, _ = mm_tn(z, dr1b, "dw_o")
    dz = mm_nt(dr1b, full["w_o"], None, "d_z", out_dtype=ACT)
    dy_a, dy_b, dproj_g = gate_bwd(dz, proj_g, y_a, y_b, "gate_bwd")
    dw_a, _ = mm_tn(s_a, dy_a, "dw_a")
    ds_a = mm_nt(dy_a, full["w_a"], None, "d_s_a", out_dtype=ACT)
    dproj_a, d_conv = conv_a_bwd(ds_a, proj_a, conv_full, "conv_a_bwd")
    dw_b, _ = mm_tn(comb, dy_b, "dw_b")

    rs_mine, rs_sib, rs_handles = {}, {}, {}

    sib_handles = {}

    def to_sibling_start(keys, grads, tag):
        parts = [_to_blocks(grads[k], big[k][1]) for k in keys]
        handles, tok = copies_start(parts, [jax.ShapeDtypeStruct((4,) + p.shape[1:], BF16) for p in parts],
                                    _to_sibling_plan, 4, f"grads_to_sibling_{tag}_start")
        sib_handles[tag] = (keys, handles)
        return tok

    def to_chips_start(tag, after):
        keys, handles = sib_handles[tag]
        parts, from_sib = copies_wait(handles, _to_sibling_plan, after, f"grads_to_sibling_{tag}_wait")
        sums = [pair_add(a, b, place, f"chip_sum_{k}") for k, a, b in zip(keys, parts, from_sib)]
        handles, tok = copies_start(sums, [jax.ShapeDtypeStruct((3,) + s.shape[1:], BF16) for s in sums],
                                    _to_chips_plan, 3, f"grads_to_chips_{tag}_start")
        for k, a, b in zip(keys, parts, from_sib):
            rs_mine[k], rs_sib[k] = a, b
        rs_handles[tag] = (keys, handles)
        return tok

    tok_a = to_sibling_start(("w_a", "w_b", "w_o", "w_up", "w_down"),
                             dict(w_a=dw_a, w_b=dw_b, w_o=dw_o, w_up=jnp.concatenate([dw_up_a, dw_up_g], 1),
                                  w_down=dw_down), "a")
    dcomb = mm_nt(dy_b, full["w_b"], None, "d_comb", after=tok_a, out_dtype=ACT)
    dos, dms = combine_bwd(dcomb, outs, lses, "combine_bwd")
    tok_a = to_chips_start("a", dms[0])
    dw_grp, cs_grp, dqkvs = [], [], []
    for g, d in enumerate(DILATIONS):
        dq, dk, dv = att_bwd(qkvs[g], dos[g], lses[g], dms[g], g, f"att_bwd_{g}", after=tok_a if g == 0 else None)
        dqkv = [t.reshape(T, GROUP_W) for t in (dq, dk, dv)]
        dwg, csg = mm_tn(h0_res[g], dqkv, f"dw_in_qkv_{g}")
        dqkvs.append(dqkv)
        dw_grp.append(dwg)
        cs_grp.append(csg)
    dw_pa, cs_pa = mm_tn(h0b, dproj_a, "dw_in_conv")
    dw_pg, cs_pg = mm_tn(h0b, dproj_g, "dw_in_gates")

    def ungroup(parts):
        return jnp.concatenate([p[:, s * GROUP_W:(s + 1) * GROUP_W] for s in range(3) for p in parts], 1)

    db_in_parts = [cs_pa, ungroup(cs_grp), cs_pg]
    tok_b = to_sibling_start(("w_in",), dict(w_in=jnp.concatenate([dw_pa, ungroup(dw_grp), dw_pg], 1)), "b")
    dh0 = mm_nt(dproj_a, w_pa, None, "d_h0_conv", after=tok_b)
    tok_b = to_chips_start("b", dh0)
    dh0 = mm_nt(dproj_g, w_pg, dh0, "d_h0_gates", after=tok_b)
    dh0_res = [(mm_nt(dqkvs[g], w_grp[g], None, f"d_h0_qkv_{g}").reshape(d, T // d, D), d)
               for g, d in enumerate(DILATIONS) if g > 0]
    dx, _, d_ln0_g, d_ln0_b, _, _ = ln_bwd(xs, None, ln0g, ln0b, dr1, ("nt", dqkvs[0], w_grp[0]), None, "d_h0_ln0_bwd",
                                           by_residue=[(dh0.reshape(1, T, D), 1)] + dh0_res)

    small = [d_ln0_g, d_ln0_b, jnp.concatenate(db_in_parts, 1), d_conv, d_b_o, d_ln1_g, d_ln1_b,
             jnp.concatenate([cs_a, cs_gate], 1), d_fcw, d_fcb, d_b_down, d_ln2_g, d_ln2_b, loss_part]
    packed, sizes = _pack(small)
    total = all_sum_small(packed, "sum_small")
    (g_ln0_g, g_ln0_b, g_b_in, g_conv_full, g_b_o, g_ln1_g, g_ln1_b, g_b_up, g_fcw_full, g_fcb, g_b_down, g_ln2_g,
     g_ln2_b, loss) = _unpack(total, sizes, [a.shape for a in small])
    cw = conv_w.shape[-1]
    fw = ffn_conv_w.shape[-1]
    g_conv = lax.dynamic_slice_in_dim(g_conv_full, dev * cw, cw, 1)
    g_fcw = lax.dynamic_slice_in_dim(g_fcw_full, dev * fw, fw, 1)

    from_chips = {}
    for tag, (keys, handles) in rs_handles.items():
        _, lands = copies_wait(handles, _to_chips_plan, total, f"grads_to_chips_{tag}_wait")
        from_chips.update(zip(keys, lands))

    moments = dict(w_in=(m_w_in, v_w_in), w_a=(m_w_a, v_w_a), w_b=(m_w_b, v_w_b), w_o=(m_w_o, v_w_o),
                   w_up=(m_w_up, v_w_up), w_down=(m_w_down, v_w_down))
    res_big = {}
    for k in names:
        res_big[k] = adamw_sharded(big[k][0], moments[k][0][0], moments[k][1][0], rs_mine[k], rs_sib[k], from_chips[k],
                                   place, f"adamw_{k}")

    small_names = ["ln0_g", "ln0_b", "b_in", "conv_w", "b_o", "ln1_g", "ln1_b", "b_up", "ffn_conv_w", "ffn_conv_b",
                   "b_down", "ln2_g", "ln2_b"]
    small_w = [ln0_g, ln0_b, b_in, conv_w, b_o, ln1_g, ln1_b, b_up, ffn_conv_w, ffn_conv_b, b_down, ln2_g, ln2_b]
    small_m = [m_ln0_g, m_ln0_b, m_b_in, m_conv_w, m_b_o, m_ln1_g, m_ln1_b, m_b_up, m_ffn_conv_w, m_ffn_conv_b,
               m_b_down, m_ln2_g, m_ln2_b]
    small_v = [v_ln0_g, v_ln0_b, v_b_in, v_conv_w, v_b_o, v_ln1_g, v_ln1_b, v_b_up, v_ffn_conv_w, v_ffn_conv_b,
               v_b_down, v_ln2_g, v_ln2_b]
    small_g = [g_ln0_g, g_ln0_b, g_b_in, g_conv, g_b_o, g_ln1_g, g_ln1_b, g_b_up, g_fcw, g_fcb, g_b_down, g_ln2_g,
               g_ln2_b]
    shapes = [w.shape for w in small_w]
    small_g = [g.reshape(s) for g, s in zip(small_g, shapes)]
    pw, psz = _pack(small_w)
    pg, _ = _pack(small_g)
    pm, _ = _pack(small_m)
    pv, _ = _pack(small_v)
    pd, pnm, pnv = adamw_packed(pw, pg, pm, pv, "adamw_small")
    res_small = {k: (g, d_, m_, v_) for k, g, d_, m_, v_ in zip(
        small_names, small_g, _unpack(pd, psz, shapes), _unpack(pnm, psz, shapes), _unpack(pnv, psz, shapes))}

    order = ["ln0_g", "ln0_b", "w_in", "b_in", "conv_w", "w_a", "w_b", "w_o", "b_o", "ln1_g", "ln1_b", "w_up", "b_up",
             "ffn_conv_w", "ffn_conv_b", "w_down", "b_down", "ln2_g", "ln2_b"]

    def result(k, j):
        if k in res_big:
            return res_big[k][j][None]
        return res_small[k][j]

    out = [loss.reshape(()), dx.reshape(x.shape)]
    for j in range(4):
        out += [result(k, j) for k in order]
    return tuple(out)
```

```python
import math

import numpy as np
import jax
import jax.numpy as jnp
from jax import lax
from jax.experimental import pallas as pl
from jax.experimental.pallas import tpu as pltpu

F32 = jnp.float32
BF16 = jnp.bfloat16
ACT = BF16

N_DEV = 8
LN_EPS = 1e-5
ALPHA = (2.0 * 1) ** 0.25
HEAD_DIM = 64
GROUP_W = 512
QKV_W = 3 * GROUP_W
DILATIONS = (1, 4, 16)
RADIUS = 64
LANES = 128
HALO = 8
HALO_BF16 = 16
ATT_TQ = 128

ADAM_LR = 0.001
ADAM_B1 = 0.9
ADAM_B2 = 0.999
ADAM_EPS = 1e-08
ADAM_WD = 0.01
ADAM_STEP = 10

VMEM_LIMIT = 52 * 1024 * 1024
OUT_TILE_BYTES = 8 * 1024 * 1024
MESH = pl.DeviceIdType.MESH
NT_DIMS = (((1,), (1,)), ((), ()))
TN_DIMS = (((0,), (0,)), ((), ()))


def _pick(n, target, align=LANES):
    if n <= target:
        return n
    best = None
    for t in range(align, target + 1, align):
        if n % t == 0:
            best = t
    assert best is not None, (n, target, align)
    return best


def _params(sems=None):
    return pltpu.CompilerParams(dimension_semantics=sems, vmem_limit_bytes=VMEM_LIMIT)


def _alibi_slopes():
    n = 3 * 8
    return np.exp2(-8.0 * np.arange(1, n + 1, dtype=np.float64) / n).astype(np.float32).reshape(3, 8)


def _ln_stats(r):
    mu = jnp.mean(r, -1, keepdims=True)
    xc = r - mu
    var = jnp.mean(xc * xc, -1, keepdims=True)
    rstd = lax.rsqrt(var + LN_EPS)
    return xc, rstd


def _load_natural(ref, d, scr):
    if d == 1:
        return ref[0].astype(F32)
    n, C = ref.shape[1], ref.shape[2]
    for c in range(C // LANES):
        for r in range(d):
            scr[c, pl.ds(r, n, stride=d), :] = ref[r, :, c * LANES:(c + 1) * LANES].astype(F32)
    return jnp.concatenate([scr[c] for c in range(C // LANES)], axis=1)


def _store_by_residue(val, ref, d, scr):
    if d == 1:
        ref[0] = val.astype(ref.dtype)
        return
    n, C = ref.shape[1], ref.shape[2]
    for c in range(C // LANES):
        scr[c] = val[:, c * LANES:(c + 1) * LANES]
    for c in range(C // LANES):
        for r in range(d):
            ref[r, :, c * LANES:(c + 1) * LANES] = scr[c, pl.ds(r, n, stride=d), :].astype(ref.dtype)


def _residue_spec(tm, d, C):
    return pl.BlockSpec((d, tm // d, C), lambda i: (0, i, 0))


def _residue_scratch(tm, C):
    return pltpu.VMEM((C // LANES, tm, LANES), F32)


def ln_fwd(a, res, g, b, name, dilations=(), gather=()):
    T, D = a.shape
    res_mm = isinstance(res, tuple)
    tm = _pick(T, 256 if res_mm else 512, 8)
    res_ins = list(res[1:]) if res_mm else ([] if res is None else [res])
    nd = len(dilations)
    ng = len(gather)
    n_in = 1 + len(res_ins) + 2
    last = T // tm - 1

    def body(*refs):
        a_ref = refs[0]
        r = a_ref[...]
        if res_mm:
            res_val = jnp.dot(refs[1][...], refs[2][...], preferred_element_type=F32) + refs[3][...]
            refs[-1 - n_scratch][...] = res_val
            r = ALPHA * r + res_val
        elif res_ins:
            r = ALPHA * r + refs[1][...]
        g_ref, b_ref = refs[n_in - 2], refs[n_in - 1]
        shard_refs = refs[n_in:n_in + ng]
        h_ref, hb_ref = refs[n_in + ng], refs[n_in + ng + 1]
        p_refs = refs[n_in + ng + 2:n_in + ng + 2 + nd]
        full_refs = refs[n_in + ng + 2 + nd:n_in + 2 * ng + 2 + nd]
        scratch = refs[len(refs) - n_scratch:]
        sems = scratch[len(scratch) - 3:] if ng else ()

        if ng:
            @pl.when(pl.program_id(0) == 0)
            def _():
                _gather_begin(shard_refs, full_refs, *sems)

        xc, rstd = _ln_stats(r)
        h = xc * rstd * g_ref[...] + b_ref[...]
        h_ref[...] = h
        hb_ref[...] = h.astype(BF16)
        for d, p_ref in zip(dilations, p_refs):
            _store_by_residue(h, p_ref, d, scratch[0])

        if ng:
            @pl.when(pl.program_id(0) == last)
            def _():
                _gather_finish(shard_refs, full_refs, *sems)

    row = pl.BlockSpec((tm, D), lambda i: (i, 0))
    vec = pl.BlockSpec((1, D), lambda i: (0, 0))
    hbm = pl.BlockSpec(memory_space=pl.ANY)
    if res_mm:
        res_specs = [pl.BlockSpec((tm, res[1].shape[1]), lambda i: (i, 0)), pl.BlockSpec(res[2].shape, lambda i: (0, 0)), vec]
    else:
        res_specs = [row] * len(res_ins)
    scratch_shapes = ([_residue_scratch(tm, D)] if nd else []) + (_gather_scratch(ng) if ng else [])
    n_scratch = len(scratch_shapes)
    ins = [a] + res_ins + [g, b] + list(gather)
    return pl.pallas_call(
        body, name=name, grid=(T // tm,),
        in_specs=[row] + res_specs + [vec, vec] + [hbm] * ng,
        out_specs=[row, row] + [_residue_spec(tm, d, D) for d in dilations] + [hbm] * ng + ([row] if res_mm else []),
        out_shape=[jax.ShapeDtypeStruct((T, D), F32), jax.ShapeDtypeStruct((T, D), BF16)]
        + [jax.ShapeDtypeStruct((d, T // d, D), BF16) for d in dilations]
        + [jax.ShapeDtypeStruct((N_DEV,) + s.shape, s.dtype) for s in gather]
        + ([jax.ShapeDtypeStruct((T, D), F32)] if res_mm else []),
        scratch_shapes=scratch_shapes,
        compiler_params=_params(("arbitrary",) if ng else ("parallel",)),
    )(*ins)


def ln_bwd(a, res, g, b, d1, d2, tgt, name, by_residue=()):
    T, D = a.shape
    tm = _pick(T, 256, 8)
    loss_mode = tgt is not None
    nres = len(by_residue)
    row = pl.BlockSpec((tm, D), lambda i: (i, 0))
    vec = pl.BlockSpec((1, D), lambda i: (0, 0))
    one = pl.BlockSpec((1, 1), lambda i: (0, 0))

    def rows_of(x):
        return pl.BlockSpec((tm, x.shape[1]), lambda i: (i, 0))

    def whole(x):
        return pl.BlockSpec(x.shape, lambda i: (0, 0))

    ins, in_specs, slots = [], [], {}

    def operand(key, arrays, specs):
        slots[key] = (len(ins), len(arrays))
        ins.extend(arrays)
        in_specs.extend(specs)

    operand("a", [a], [row])
    if isinstance(res, tuple):
        _, x, w, bias = res
        operand("res_mm", [x, w, bias], [rows_of(x), whole(w), vec])
    elif res is not None:
        operand("res", [res], [row])
    operand("gb", [g, b], [vec, vec])
    if loss_mode:
        operand("tgt", [tgt], [row])
    else:
        operand("d1", [d1], [row])
        if isinstance(d2, tuple):
            _, pieces, w = d2
            operand("d2_mm", list(pieces) + [w], [rows_of(p) for p in pieces] + [whole(w)])
        else:
            operand("d2", [d2], [row])
    operand("by_residue", [e for e, _ in by_residue], [_residue_spec(tm, d, D) for _, d in by_residue])
    n_in = len(ins)

    def body(*refs):
        def get(key):
            first, count = slots[key]
            return refs[first:first + count]

        dr_ref, drb_ref, dg_ref, db_ref, ds_ref, loss_ref = refs[n_in:n_in + 6]
        i = pl.program_id(0)

        @pl.when(i == 0)
        def _():
            dg_ref[...] = jnp.zeros_like(dg_ref)
            db_ref[...] = jnp.zeros_like(db_ref)
            ds_ref[...] = jnp.zeros_like(ds_ref)
            loss_ref[...] = jnp.zeros_like(loss_ref)

        r = get("a")[0][...]
        if "res_mm" in slots:
            x_ref, w_ref, bias_ref = get("res_mm")
            r = ALPHA * r + (jnp.dot(x_ref[...], w_ref[...], preferred_element_type=F32) + bias_ref[...])
        elif "res" in slots:
            r = ALPHA * r + get("res")[0][...]
        g_ref, b_ref = get("gb")
        xc, rstd = _ln_stats(r)
        xhat = xc * rstd
        gam = g_ref[...]
        if loss_mode:
            err = xhat * gam + b_ref[...] - get("tgt")[0][...]
            dy = err * (1.0 / D)
            row_loss = jnp.mean(err * err, -1, keepdims=True)
            loss_ref[...] += 0.5 * jnp.sum(row_loss, 0, keepdims=True)
        else:
            if "d2_mm" in slots:
                *p_refs, w_ref = get("d2_mm")
                av = p_refs[0][...] if len(p_refs) == 1 else jnp.concatenate([p[...] for p in p_refs], axis=1)
                d2v = lax.dot_general(av, w_ref[...], NT_DIMS, preferred_element_type=F32)
            else:
                d2v = get("d2")[0][...]
            dy = ALPHA * get("d1")[0][...] + d2v
        for (_, d), e_ref in zip(by_residue, get("by_residue")):
            dy = dy + _load_natural(e_ref, d, refs[-1])
        dyg = dy * gam
        c1 = jnp.mean(dyg, -1, keepdims=True)
        c2 = jnp.mean(dyg * xhat, -1, keepdims=True)
        dr = rstd * (dyg - c1 - xhat * c2)
        dr_ref[...] = dr
        drb_ref[...] = dr.astype(BF16)
        dg_ref[...] += jnp.sum(dy * xhat, 0, keepdims=True)
        db_ref[...] += jnp.sum(dy, 0, keepdims=True)
        ds_ref[...] += jnp.sum(dr, 0, keepdims=True)

    return pl.pallas_call(
        body, name=name, grid=(T // tm,),
        in_specs=in_specs,
        out_specs=[row, row, vec, vec, vec, one],
        out_shape=[jax.ShapeDtypeStruct((T, D), F32), jax.ShapeDtypeStruct((T, D), BF16),
                   jax.ShapeDtypeStruct((1, D), F32), jax.ShapeDtypeStruct((1, D), F32),
                   jax.ShapeDtypeStruct((1, D), F32), jax.ShapeDtypeStruct((1, 1), F32)],
        scratch_shapes=[_residue_scratch(tm, D)] if nres else [],
        compiler_params=_params(("arbitrary",)),
    )(*ins)


_TOKEN_SPEC = pl.BlockSpec((8, LANES), lambda i: (0, 0))


def mm_nn(a, w, bias, out_dtype, name, after=None):
    M, K = a.shape
    N = w.shape[1]
    tm = _pick(M, max(256, min(1024, OUT_TILE_BYTES // (N * jnp.dtype(out_dtype).itemsize))), 8)
    tc = _pick(N, 512)

    def body(a_ref, w_ref, b_ref, *rest):
        o_ref = rest[-1]
        av = a_ref[...]
        for j in range(N // tc):
            cols = slice(j * tc, (j + 1) * tc)
            acc = jnp.dot(av, w_ref[:, cols], preferred_element_type=F32)
            o_ref[:, cols] = (acc + b_ref[:, cols]).astype(out_dtype)

    return pl.pallas_call(
        body, name=name, grid=(M // tm,),
        in_specs=[pl.BlockSpec((tm, K), lambda i: (i, 0)),
                  pl.BlockSpec((K, N), lambda i: (0, 0)),
                  pl.BlockSpec((1, N), lambda i: (0, 0))] + ([] if after is None else [_TOKEN_SPEC]),
        out_specs=pl.BlockSpec((tm, N), lambda i: (i, 0)),
        out_shape=jax.ShapeDtypeStruct((M, N), out_dtype),
        compiler_params=_params(("parallel",)),
    )(a, w, bias, *([] if after is None else [after]))


def mm_nt(a, w, acc_in, name, after=None, w_block=0, out_dtype=F32):
    pieces = list(a) if isinstance(a, (list, tuple)) else [a]
    M = pieces[0].shape[0]
    widths = [p.shape[1] for p in pieces]
    K = sum(widths)
    N = w.shape[0]
    tm = _pick(M, 1024, 8)
    tc = _pick(N, 512)
    has_acc = acc_in is not None
    n_a = len(pieces)

    def body(*refs):
        a_refs, w_ref = refs[:n_a], refs[n_a]
        c_ref = refs[n_a + 1] if has_acc else None
        o_ref = refs[-1]
        av = a_refs[0][...] if n_a == 1 else jnp.concatenate([r[...] for r in a_refs], axis=1)
        for j in range(N // tc):
            cols = slice(j * tc, (j + 1) * tc)
            acc = lax.dot_general(av, w_ref[cols, :], NT_DIMS, preferred_element_type=F32)
            if has_acc:
                acc = acc + c_ref[:, cols]
            o_ref[:, cols] = acc.astype(out_dtype)

    out_spec = pl.BlockSpec((tm, N), lambda i: (i, 0))
    in_specs = [pl.BlockSpec((tm, kw), lambda i: (i, 0)) for kw in widths]
    in_specs.append(pl.BlockSpec((N, K), lambda i: (0, w_block)))
    ins = pieces + [w]
    if has_acc:
        in_specs.append(out_spec)
        ins.append(acc_in)
    if after is not None:
        in_specs.append(_TOKEN_SPEC)
        ins.append(after)
    return pl.pallas_call(
        body, name=name, grid=(M // tm,),
        in_specs=in_specs, out_specs=out_spec,
        out_shape=jax.ShapeDtypeStruct((M, N), out_dtype),
        compiler_params=_params(("parallel",)),
    )(*ins)


def mm_tn(a, b, name, out_dtype=BF16):
    pieces = list(b) if isinstance(b, (list, tuple)) else [b]
    T, M = a.shape
    widths = [p.shape[1] for p in pieces]
    N = sum(widths)
    tk = _pick(T, 1024, 8)
    nk = T // tk
    tc = _pick(M, 256)
    n_b = len(pieces)

    def body(*refs):
        a_ref, b_refs = refs[0], refs[1:1 + n_b]
        o_ref, cs_ref, acc_ref = refs[1 + n_b:]
        k = pl.program_id(0)

        @pl.when(k == 0)
        def _():
            acc_ref[...] = jnp.zeros_like(acc_ref)
            cs_ref[...] = jnp.zeros_like(cs_ref)

        bv = b_refs[0][...] if n_b == 1 else jnp.concatenate([r[...] for r in b_refs], axis=1)
        cs_ref[...] += jnp.sum(bv.astype(F32), 0, keepdims=True)
        for mi in range(M // tc):
            rows = slice(mi * tc, (mi + 1) * tc)
            acc_ref[rows, :] += lax.dot_general(a_ref[:, rows], bv, TN_DIMS, preferred_element_type=F32)

        @pl.when(k == nk - 1)
        def _():
            o_ref[...] = acc_ref[...].astype(out_dtype)

    return pl.pallas_call(
        body, name=name, grid=(nk,),
        in_specs=[pl.BlockSpec((tk, M), lambda k: (k, 0))] + [pl.BlockSpec((tk, wd), lambda k: (k, 0)) for wd in widths],
        out_specs=[pl.BlockSpec((M, N), lambda k: (0, 0)), pl.BlockSpec((1, N), lambda k: (0, 0))],
        out_shape=[jax.ShapeDtypeStruct((M, N), out_dtype), jax.ShapeDtypeStruct((1, N), F32)],
        scratch_shapes=[pltpu.VMEM((M, N), F32)],
        compiler_params=_params(("arbitrary",)),
    )(a, *pieces)


def _ext_rows(prev_ref, main_ref, next_ref, i, tm, T, dtype=F32):
    before = jnp.where(i == 0, 0.0, prev_ref[...])
    after = jnp.where(i == T // tm - 1, 0.0, next_ref[...])
    return jnp.concatenate([before, main_ref[...], after], axis=0).astype(dtype)


def _prev_row(x):
    return pltpu.roll(x, 1, 0)


def _next_row(x):
    return pltpu.roll(x, x.shape[0] - 1, 0)


def _conv3(u, w_ref):
    return _prev_row(u) * w_ref[0:1, :] + u * w_ref[1:2, :] + _next_row(u) * w_ref[2:3, :]


def _main(x, tm, halo=HALO):
    return x[halo:halo + tm]


def _halo_specs(tm, tc, T, col, order, halo=HALO):
    r = tm // halo
    last = T // halo - 1
    if order == "ij":
        return (pl.BlockSpec((halo, tc), lambda i, j: (jnp.maximum(i * r - 1, 0), col(j))),
                pl.BlockSpec((tm, tc), lambda i, j: (i, col(j))),
                pl.BlockSpec((halo, tc), lambda i, j: (jnp.minimum((i + 1) * r, last), col(j))))
    return (pl.BlockSpec((halo, tc), lambda j, i: (jnp.maximum(i * r - 1, 0), col(j))),
            pl.BlockSpec((tm, tc), lambda j, i: (i, col(j))),
            pl.BlockSpec((halo, tc), lambda j, i: (jnp.minimum((i + 1) * r, last), col(j))))


def conv_a_fwd(proj_a, conv_w, name):
    T, D3 = proj_a.shape
    D = D3 // 3
    tm = _pick(T, 256, 8)

    def body(p_ref, m_ref, n_ref, w_ref, o_ref):
        i = pl.program_id(0)
        ext = _ext_rows(p_ref, m_ref, n_ref, i, tm, T)
        u = ext[:, D:2 * D] * ext[:, 2 * D:]
        cu = _conv3(u, w_ref)
        o_ref[...] = (m_ref[:, :D].astype(F32) * _main(cu, tm, HALO_BF16)).astype(BF16)

    prev, main, nxt = _halo_specs(tm, D3, T, lambda j: 0, "ij", HALO_BF16)
    return pl.pallas_call(
        body, name=name, grid=(T // tm, 1),
        in_specs=[prev, main, nxt, pl.BlockSpec((3, D), lambda i, j: (0, 0))],
        out_specs=pl.BlockSpec((tm, D), lambda i, j: (i, 0)),
        out_shape=jax.ShapeDtypeStruct((T, D), BF16),
        compiler_params=_params(("parallel", "arbitrary")),
    )(proj_a, proj_a, proj_a, conv_w)


def conv_a_bwd(ds_a, proj_a, conv_w, name):
    T, D3 = proj_a.shape
    D = D3 // 3
    tm = _pick(T, 256, 8)

    def body(dp_ref, dm_ref, dn_ref, p_ref, m_ref, n_ref, w_ref, o_ref, dw_ref):
        i = pl.program_id(0)

        @pl.when(i == 0)
        def _():
            dw_ref[...] = jnp.zeros_like(dw_ref)

        ext = _ext_rows(p_ref, m_ref, n_ref, i, tm, T)
        dsa = _ext_rows(dp_ref, dm_ref, dn_ref, i, tm, T)
        gb, gc, hin = ext[:, :D], ext[:, D:2 * D], ext[:, 2 * D:]
        u = gc * hin
        u_prev, u_next = _prev_row(u), _next_row(u)
        cu = u_prev * w_ref[0:1, :] + u * w_ref[1:2, :] + u_next * w_ref[2:3, :]
        dcu = dsa * gb
        du = _next_row(dcu) * w_ref[0:1, :] + dcu * w_ref[1:2, :] + _prev_row(dcu) * w_ref[2:3, :]
        h = HALO_BF16
        o_ref[:, :D] = _main(dsa * cu, tm, h).astype(BF16)
        o_ref[:, D:2 * D] = _main(du * hin, tm, h).astype(BF16)
        o_ref[:, 2 * D:] = _main(du * gc, tm, h).astype(BF16)
        dcu_m = _main(dcu, tm, h)
        dw_ref[0:1, :] += jnp.sum(dcu_m * _main(u_prev, tm, h), 0, keepdims=True)
        dw_ref[1:2, :] += jnp.sum(dcu_m * _main(u, tm, h), 0, keepdims=True)
        dw_ref[2:3, :] += jnp.sum(dcu_m * _main(u_next, tm, h), 0, keepdims=True)

    dprev, dmain, dnxt = _halo_specs(tm, D, T, lambda j: 0, "ij", HALO_BF16)
    prev, main, nxt = _halo_specs(tm, D3, T, lambda j: 0, "ij", HALO_BF16)
    return pl.pallas_call(
        body, name=name, grid=(T // tm, 1),
        in_specs=[dprev, dmain, dnxt, prev, main, nxt, pl.BlockSpec((3, D), lambda i, j: (0, 0))],
        out_specs=[pl.BlockSpec((tm, D3), lambda i, j: (i, 0)), pl.BlockSpec((3, D), lambda i, j: (0, 0))],
        out_shape=[jax.ShapeDtypeStruct((T, D3), BF16), jax.ShapeDtypeStruct((3, D), F32)],
        compiler_params=_params(("arbitrary", "arbitrary")),
    )(ds_a, ds_a, ds_a, proj_a, proj_a, proj_a, conv_w)


_INV_SQRT2 = 1.0 / math.sqrt(2.0)
_INV_SQRT_2PI = 1.0 / math.sqrt(2.0 * math.pi)


def ffn_up_conv_f(h, w_up, b_up, fcw, fcb, name):
    T, D = h.shape
    F = fcb.shape[1]
    tm = _pick(T, 256, 8)
    tc = _pick(F, 256)
    halo = HALO_BF16

    def body(hp_ref, hm_ref, hn_ref, w_ref, b_ref, cw_ref, cb_ref, up_ref, f_ref):
        i = pl.program_id(0)
        h_ext = _ext_rows(hp_ref, hm_ref, hn_ref, i, tm, T, dtype=BF16)
        h_main = hm_ref[...]
        rows = i * tm - halo + lax.broadcasted_iota(jnp.int32, (tm + 2 * halo, 1), 0)
        inside = (rows >= 0) & (rows < T)
        for c in range(F // tc):
            cols = slice(c * tc, (c + 1) * tc)
            gcols = slice(F + c * tc, F + (c + 1) * tc)
            a_ext = jnp.dot(h_ext, w_ref[:, cols], preferred_element_type=F32) + b_ref[:, cols]
            a_ext = jnp.where(inside, a_ext, 0.0)
            gate = jnp.dot(h_main, w_ref[:, gcols], preferred_element_type=F32) + b_ref[:, gcols]
            up_ref[:, cols] = _main(a_ext, tm, halo)
            up_ref[:, gcols] = gate
            ca = _main(_prev_row(a_ext) * cw_ref[0:1, cols] + a_ext * cw_ref[1:2, cols]
                       + _next_row(a_ext) * cw_ref[2:3, cols], tm, halo) + cb_ref[:, cols]
            gl = 0.5 * ca * (1.0 + lax.erf(ca * _INV_SQRT2))
            f_ref[:, cols] = (gl * gate).astype(BF16)

    prev, main, nxt = _halo_specs(tm, D, T, lambda j: 0, "ij", halo)
    whole = lambda x: pl.BlockSpec(x.shape, lambda i, j: (0, 0))
    return pl.pallas_call(
        body, name=name, grid=(T // tm, 1),
        in_specs=[prev, main, nxt, whole(w_up), whole(b_up), whole(fcw), whole(fcb)],
        out_specs=[pl.BlockSpec((tm, 2 * F), lambda i, j: (i, 0)), pl.BlockSpec((tm, F), lambda i, j: (i, 0))],
        out_shape=[jax.ShapeDtypeStruct((T, 2 * F), F32), jax.ShapeDtypeStruct((T, F), BF16)],
        compiler_params=_params(("parallel", "arbitrary")),
    )(h, h, h, w_up, b_up, fcw, fcb)


def conv_f_bwd(dy, w_down, up, fcw, fcb, name):
    T, F2 = up.shape
    F = F2 // 2
    D = dy.shape[1]
    tm = _pick(T, 256, 8)
    tc = _pick(F, 256)

    def body(yp_ref, ym_ref, yn_ref, wd_ref, up_ref, um_ref, un_ref, w_ref, b_ref,
             da_ref, dg_ref, csa_ref, csg_ref, dfb_ref, dfw_ref):
        i = pl.program_id(0)
        first, last = i == 0, i == T // tm - 1

        @pl.when(first)
        def _():
            csa_ref[...] = jnp.zeros_like(csa_ref)
            csg_ref[...] = jnp.zeros_like(csg_ref)
            dfb_ref[...] = jnp.zeros_like(dfb_ref)
            dfw_ref[...] = jnp.zeros_like(dfw_ref)

        def ext(cols):
            return jnp.concatenate([jnp.where(first, 0.0, up_ref[:, cols]), um_ref[:, cols],
                                    jnp.where(last, 0.0, un_ref[:, cols])], axis=0)

        dy_ext = _ext_rows(yp_ref, ym_ref, yn_ref, i, tm, T, dtype=BF16)
        for c in range(F // tc):
            cols = slice(c * tc, (c + 1) * tc)
            dfe = lax.dot_general(dy_ext, wd_ref[cols, :], NT_DIMS, preferred_element_type=F32)
            dfe = dfe[HALO_BF16 - HALO:HALO_BF16 + tm + HALO]
            a = ext(cols)
            gate = ext(slice(F + c * tc, F + (c + 1) * tc))
            a_prev, a_next = _prev_row(a), _next_row(a)
            ca = a_prev * w_ref[0:1, cols] + a * w_ref[1:2, cols] + a_next * w_ref[2:3, cols] + b_ref[:, cols]
            cdf = 0.5 * (1.0 + lax.erf(ca * _INV_SQRT2))
            gl = ca * cdf
            gp = cdf + ca * (jnp.exp(-0.5 * ca * ca) * _INV_SQRT_2PI)
            dgate = _main(dfe * gl, tm)
            dca = dfe * gate * gp
            da = _main(_next_row(dca) * w_ref[0:1, cols] + dca * w_ref[1:2, cols] + _prev_row(dca) * w_ref[2:3, cols],
                       tm)
            da_ref[:, cols] = da.astype(BF16)
            dg_ref[:, cols] = dgate.astype(BF16)
            csa_ref[:, cols] += jnp.sum(da, 0, keepdims=True)
            csg_ref[:, cols] += jnp.sum(dgate, 0, keepdims=True)
            dca_m = _main(dca, tm)
            dfb_ref[:, cols] += jnp.sum(dca_m, 0, keepdims=True)
            dfw_ref[0:1, cols] += jnp.sum(dca_m * _main(a_prev, tm), 0, keepdims=True)
            dfw_ref[1:2, cols] += jnp.sum(dca_m * _main(a, tm), 0, keepdims=True)
            dfw_ref[2:3, cols] += jnp.sum(dca_m * _main(a_next, tm), 0, keepdims=True)

    uprev, umain, unxt = _halo_specs(tm, F2, T, lambda j: 0, "ij")
    yprev, ymain, ynxt = _halo_specs(tm, D, T, lambda j: 0, "ij", HALO_BF16)
    whole = lambda shape: pl.BlockSpec(shape, lambda i, j: (0, 0))
    tile = pl.BlockSpec((tm, F), lambda i, j: (i, 0))
    return pl.pallas_call(
        body, name=name, grid=(T // tm, 1),
        in_specs=[yprev, ymain, ynxt, whole((F, D)), uprev, umain, unxt, whole((3, F)), whole((1, F))],
        out_specs=[tile, tile, whole((1, F)), whole((1, F)), whole((1, F)), whole((3, F))],
        out_shape=[jax.ShapeDtypeStruct((T, F), BF16), jax.ShapeDtypeStruct((T, F), BF16),
                   jax.ShapeDtypeStruct((1, F), F32), jax.ShapeDtypeStruct((1, F), F32),
                   jax.ShapeDtypeStruct((1, F), F32), jax.ShapeDtypeStruct((3, F), F32)],
        compiler_params=_params(("arbitrary", "arbitrary")),
    )(dy, dy, dy, w_down, up, up, up, fcw, fcb)


def gate_fwd(proj_g, y_a, y_b, name):
    T, D = y_a.shape
    tm = _pick(T, 512, 8)

    def body(g_ref, a_ref, b_ref, o_ref):
        sa = jax.nn.sigmoid(g_ref[:, :D].astype(F32))
        sb = jax.nn.sigmoid(g_ref[:, D:].astype(F32))
        o_ref[...] = (sa * a_ref[...].astype(F32) + sb * b_ref[...].astype(F32)).astype(BF16)

    row = pl.BlockSpec((tm, D), lambda i: (i, 0))
    return pl.pallas_call(
        body, name=name, grid=(T // tm,),
        in_specs=[pl.BlockSpec((tm, 2 * D), lambda i: (i, 0)), row, row],
        out_specs=row,
        out_shape=jax.ShapeDtypeStruct((T, D), BF16),
        compiler_params=_params(("parallel",)),
    )(proj_g, y_a, y_b)


def gate_bwd(dz, proj_g, y_a, y_b, name):
    T, D = y_a.shape
    tm = _pick(T, 512, 8)

    def body(dz_ref, g_ref, a_ref, b_ref, da_ref, db_ref, dg_ref):
        dzv = dz_ref[...].astype(F32)
        sa = jax.nn.sigmoid(g_ref[:, :D].astype(F32))
        sb = jax.nn.sigmoid(g_ref[:, D:].astype(F32))
        da_ref[...] = (dzv * sa).astype(BF16)
        db_ref[...] = (dzv * sb).astype(BF16)
        dg_ref[:, :D] = (dzv * a_ref[...].astype(F32) * (sa * (1.0 - sa))).astype(BF16)
        dg_ref[:, D:] = (dzv * b_ref[...].astype(F32) * (sb * (1.0 - sb))).astype(BF16)

    row = pl.BlockSpec((tm, D), lambda i: (i, 0))
    wide = pl.BlockSpec((tm, 2 * D), lambda i: (i, 0))
    return pl.pallas_call(
        body, name=name, grid=(T // tm,),
        in_specs=[row, wide, row, row],
        out_specs=[row, row, wide],
        out_shape=[jax.ShapeDtypeStruct((T, D), BF16), jax.ShapeDtypeStruct((T, D), BF16),
                   jax.ShapeDtypeStruct((T, 2 * D), BF16)],
        compiler_params=_params(("parallel",)),
    )(dz, proj_g, y_a, y_b)


ATT_WIN = ATT_TQ + 2 * RADIUS
ATT_STEP = 1024
FAR = 1e32


def _att_window(qs, L):
    ks = pl.multiple_of(jnp.clip(qs - RADIUS, 0, L - ATT_WIN), RADIUS)
    return ks, jnp.where(qs == 0, 0, jnp.where(qs == L - ATT_TQ, 2, 1))


def _fill_bias_tables(bias_ref, sl_ref, hp, d):
    col_row = (lax.broadcasted_iota(jnp.int32, (ATT_TQ, ATT_WIN), 1)
               - lax.broadcasted_iota(jnp.int32, (ATT_TQ, ATT_WIN), 0))
    for v in range(3):
        ad = jnp.abs(col_row - v * RADIUS)
        dist = jnp.where(ad <= RADIUS, (ad * d).astype(F32), FAR)
        bias_ref[v, 0:ATT_TQ, :] = sl_ref[hp * 2] * dist
        bias_ref[v, ATT_TQ:2 * ATT_TQ, :] = sl_ref[hp * 2 + 1] * dist


def _head_masks():
    lane = lax.broadcasted_iota(jnp.int32, (1, LANES), 1)
    return [lane < HEAD_DIM, lane >= HEAD_DIM]


def _stack_heads(x, masks):
    zero = jnp.zeros_like(x)
    return jnp.concatenate([jnp.where(masks[0], x, zero), jnp.where(masks[1], x, zero)], axis=0)


def _unstack_heads(x2, masks):
    n = x2.shape[0] // 2
    return jnp.where(masks[0], x2[:n], x2[n:])


def _att_step(L):
    step = min(ATT_STEP, L)
    assert L % step == 0 and step % ATT_TQ == 0 and L >= ATT_WIN
    return step


def _residues_per_step(d, L):
    rps = max(1, min(d, ATT_STEP // L))
    assert d % rps == 0
    return rps


def att_fwd(qkv, group, name):
    d, L, _ = qkv.shape
    step = _att_step(L)
    rps = _residues_per_step(d, L)
    cg = GROUP_W // LANES
    slopes = jnp.asarray(_alibi_slopes()[group])
    scale = HEAD_DIM ** -0.5

    def body(sl_ref, q_ref, k_ref, v_ref, o_ref, l_ref, bias_ref, s_ref, p_ref):
        hp = pl.program_id(1)
        i = pl.program_id(2)

        @pl.when(i == 0)
        def _():
            _fill_bias_tables(bias_ref, sl_ref, hp, d)

        masks = _head_masks()
        per = step // ATT_TQ
        tiles = [(rr, t) for rr in range(rps) for t in range(per)]
        windows = [_att_window(i * step + t * ATT_TQ, L) for t in range(per)]
        for n, (rr, t) in enumerate(tiles):
            rows = slice(t * ATT_TQ, (t + 1) * ATT_TQ)
            ks, table = windows[t]
            q2 = _stack_heads(q_ref[rr, rows, :] * scale, masks)
            kw = k_ref[rr, pl.ds(ks, ATT_WIN), :]
            s_ref[n] = lax.dot_general(q2, kw, NT_DIMS, preferred_element_type=F32) - bias_ref[table]
        for n, (rr, t) in enumerate(tiles):
            rows = slice(t * ATT_TQ, (t + 1) * ATT_TQ)
            s = s_ref[n]
            m = jnp.max(s, -1, keepdims=True)
            p = jnp.exp(s - m)
            den = jnp.sum(p, -1, keepdims=True)
            p_ref[n] = (p / den).astype(BF16)
            l_ref[rr, rows, :] = _unstack_heads(m + jnp.log(den), masks)
        for n, (rr, t) in enumerate(tiles):
            rows = slice(t * ATT_TQ, (t + 1) * ATT_TQ)
            vw = v_ref[rr, pl.ds(windows[t][0], ATT_WIN), :]
            o2 = jnp.dot(p_ref[n], vw, preferred_element_type=F32)
            o_ref[rr, rows, :] = _unstack_heads(o2, masks).astype(ACT)

    n_tiles = rps * step // ATT_TQ
    out_spec = pl.BlockSpec((rps, step, LANES), lambda r, hp, i: (r, i, hp))
    return pl.pallas_call(
        body, name=name, grid=(d // rps, cg, L // step),
        in_specs=[pl.BlockSpec(memory_space=pltpu.SMEM),
                  pl.BlockSpec((rps, step, LANES), lambda r, hp, i: (r, i, hp)),
                  pl.BlockSpec((rps, L, LANES), lambda r, hp, i: (r, 0, cg + hp)),
                  pl.BlockSpec((rps, L, LANES), lambda r, hp, i: (r, 0, 2 * cg + hp))],
        out_specs=[out_spec, out_spec],
        out_shape=[jax.ShapeDtypeStruct((d, L, GROUP_W), ACT), jax.ShapeDtypeStruct((d, L, GROUP_W), F32)],
        scratch_shapes=[pltpu.VMEM((3, 2 * ATT_TQ, ATT_WIN), F32),
                        pltpu.VMEM((n_tiles, 2 * ATT_TQ, ATT_WIN), F32),
                        pltpu.VMEM((n_tiles, 2 * ATT_TQ, ATT_WIN), BF16)],
        compiler_params=_params(("arbitrary", "arbitrary", "arbitrary")),
    )(slopes, qkv, qkv, qkv)


def att_bwd(qkv, do, lse, dmat, group, name, after=None):
    d, L, _ = qkv.shape
    step = _att_step(L)
    rps = _residues_per_step(d, L)
    nq = L // step
    cg = GROUP_W // LANES
    slopes = jnp.asarray(_alibi_slopes()[group])
    scale = HEAD_DIM ** -0.5

    def body(sl_ref, q_ref, k_ref, v_ref, do_ref, l_ref, dm_ref, *rest):
        dq_ref, dk_ref, dv_ref, dk_acc, dv_acc, bias_ref, s_ref, dp_ref, p_ref, ds_ref = rest[len(rest) - 10:]
        hp = pl.program_id(1)
        i = pl.program_id(2)

        @pl.when(i == 0)
        def _():
            dk_acc[...] = jnp.zeros_like(dk_acc)
            dv_acc[...] = jnp.zeros_like(dv_acc)
            _fill_bias_tables(bias_ref, sl_ref, hp, d)

        masks = _head_masks()

        def head_cols(x):
            return jnp.concatenate([jnp.max(jnp.where(hm, x, -jnp.inf), -1, keepdims=True) for hm in masks], axis=0)

        per = step // ATT_TQ
        tiles = [(rr, t) for rr in range(rps) for t in range(per)]
        windows = [_att_window(i * step + t * ATT_TQ, L) for t in range(per)]

        def stacked(ref, rr, t, factor=None):
            x = ref[rr, t * ATT_TQ:(t + 1) * ATT_TQ, :]
            return _stack_heads(x if factor is None else x * factor, masks)

        for n, (rr, t) in enumerate(tiles):
            ks, table = windows[t]
            q2 = stacked(q_ref, rr, t, scale)
            s_ref[n] = lax.dot_general(q2, k_ref[rr, pl.ds(ks, ATT_WIN), :], NT_DIMS,
                                       preferred_element_type=F32) - bias_ref[table]
            dp_ref[n] = lax.dot_general(stacked(do_ref, rr, t), v_ref[rr, pl.ds(ks, ATT_WIN), :], NT_DIMS,
                                        preferred_element_type=F32)
        for n, (rr, t) in enumerate(tiles):
            rows = slice(t * ATT_TQ, (t + 1) * ATT_TQ)
            p = jnp.exp(s_ref[n] - head_cols(l_ref[rr, rows, :]))
            p_ref[n] = p.astype(BF16)
            ds_ref[n] = (p * (dp_ref[n] - head_cols(dm_ref[rr, rows, :]))).astype(BF16)
        for n, (rr, t) in enumerate(tiles):
            rows = slice(t * ATT_TQ, (t + 1) * ATT_TQ)
            ks = windows[t][0]
            ds = ds_ref[n]
            dq2 = jnp.dot(ds, k_ref[rr, pl.ds(ks, ATT_WIN), :], preferred_element_type=F32)
            dq_ref[rr, rows, :] = (_unstack_heads(dq2, masks) * scale).astype(BF16)
            dk_acc[rr, pl.ds(ks, ATT_WIN), :] += lax.dot_general(ds, stacked(q_ref, rr, t, scale), TN_DIMS,
                                                                 preferred_element_type=F32)
            dv_acc[rr, pl.ds(ks, ATT_WIN), :] += lax.dot_general(p_ref[n], stacked(do_ref, rr, t), TN_DIMS,
                                                                 preferred_element_type=F32)

        @pl.when(i == nq - 1)
        def _():
            dk_ref[...] = dk_acc[...].astype(BF16)
            dv_ref[...] = dv_acc[...].astype(BF16)

    tile = pl.BlockSpec((rps, step, LANES), lambda r, hp, i: (r, i, hp))
    whole = pl.BlockSpec((rps, L, LANES), lambda r, hp, i: (r, 0, hp))
    return pl.pallas_call(
        body, name=name, grid=(d // rps, cg, nq),
        in_specs=[pl.BlockSpec(memory_space=pltpu.SMEM), tile,
                  pl.BlockSpec((rps, L, LANES), lambda r, hp, i: (r, 0, cg + hp)),
                  pl.BlockSpec((rps, L, LANES), lambda r, hp, i: (r, 0, 2 * cg + hp)),
                  tile, tile, tile] + ([] if after is None else [pl.BlockSpec((8, LANES), lambda r, hp, i: (0, 0))]),
        out_specs=[tile, whole, whole],
        out_shape=[jax.ShapeDtypeStruct((d, L, GROUP_W), BF16)] * 3,
        scratch_shapes=[pltpu.VMEM((rps, L, LANES), F32), pltpu.VMEM((rps, L, LANES), F32),
                        pltpu.VMEM((3, 2 * ATT_TQ, ATT_WIN), F32)]
        + [pltpu.VMEM((rps * step // ATT_TQ, 2 * ATT_TQ, ATT_WIN), dt) for dt in (F32, F32, BF16, BF16)],
        compiler_params=_params(("arbitrary", "arbitrary", "arbitrary")),
    )(slopes, qkv, qkv, qkv, do, lse, dmat, *([] if after is None else [after]))


def _group_weights(ls):
    m = jnp.maximum(jnp.maximum(ls[0], ls[1]), ls[2])
    es = [jnp.exp(l - m) for l in ls]
    tot = es[0] + es[1] + es[2]
    return [e / tot for e in es]


def combine_fwd(outs, lses, name):
    T = outs[0].shape[0] * outs[0].shape[1]
    tm = _pick(T, 512, 8)
    n_scr = 2 * (len(DILATIONS) - 1)

    def body(*refs):
        o_refs, l_refs, c_ref, scr = refs[:3], refs[3:6], refs[6], refs[7:]
        o = [_load_natural(o_refs[g], d, scr[g - 1] if g else None) for g, d in enumerate(DILATIONS)]
        l = [_load_natural(l_refs[g], d, scr[g + 1] if g else None) for g, d in enumerate(DILATIONS)]
        w = _group_weights(l)
        c_ref[...] = (w[0] * o[0] + w[1] * o[1] + w[2] * o[2]).astype(BF16)

    specs = [_residue_spec(tm, d, GROUP_W) for d in DILATIONS]
    return pl.pallas_call(
        body, name=name, grid=(T // tm,),
        in_specs=specs + specs, out_specs=pl.BlockSpec((tm, GROUP_W), lambda i: (i, 0)),
        out_shape=jax.ShapeDtypeStruct((T, GROUP_W), BF16),
        scratch_shapes=[_residue_scratch(tm, GROUP_W)] * n_scr,
        compiler_params=_params(("parallel",)),
    )(*outs, *lses)


def combine_bwd(dcomb, outs, lses, name):
    T = dcomb.shape[0]
    tm = _pick(T, 256, 8)
    head = np.arange(GROUP_W) // HEAD_DIM
    seg = jnp.asarray((head[:, None] == head[None, :]).astype(np.float32)).astype(BF16)
    ng = len(DILATIONS)
    n_scr = 4 * (ng - 1)

    def body(*refs):
        dc_ref, o_refs, l_refs, e_ref = refs[0], refs[1:1 + ng], refs[1 + ng:1 + 2 * ng], refs[1 + 2 * ng]
        do_refs, dm_refs = refs[2 + 2 * ng:2 + 3 * ng], refs[2 + 3 * ng:2 + 4 * ng]
        scr = refs[2 + 4 * ng:]
        o = [_load_natural(o_refs[g], d, scr[4 * (g - 1)] if g else None) for g, d in enumerate(DILATIONS)]
        l = [_load_natural(l_refs[g], d, scr[4 * (g - 1) + 1] if g else None) for g, d in enumerate(DILATIONS)]
        w = _group_weights(l)
        dc = dc_ref[...].astype(F32)
        e = e_ref[...]
        prod = dc * (w[0] * o[0] + w[1] * o[1] + w[2] * o[2])
        tot = jnp.zeros_like(dc)
        for _ in range(3):
            part = prod.astype(BF16)
            tot = tot + jnp.dot(part, e, preferred_element_type=F32)
            prod = prod - part.astype(F32)
        for g, d in enumerate(DILATIONS):
            _store_by_residue(w[g] * dc, do_refs[g], d, scr[4 * (g - 1) + 2] if g else None)
            _store_by_residue(w[g] * tot, dm_refs[g], d, scr[4 * (g - 1) + 3] if g else None)

    specs = [_residue_spec(tm, d, GROUP_W) for d in DILATIONS]
    res = pl.pallas_call(
        body, name=name, grid=(T // tm,),
        in_specs=[pl.BlockSpec((tm, GROUP_W), lambda i: (i, 0))] + specs + specs
        + [pl.BlockSpec((GROUP_W, GROUP_W), lambda i: (0, 0))],
        out_specs=specs + specs,
        out_shape=[jax.ShapeDtypeStruct(o.shape, BF16) for o in outs] + [jax.ShapeDtypeStruct(o.shape, F32) for o in outs],
        scratch_shapes=[_residue_scratch(tm, GROUP_W)] * n_scr,
        compiler_params=_params(("parallel",)),
    )(dcomb, *outs, *lses, seg)
    return res[:ng], res[ng:]


def _position():
    return lax.axis_index("x"), lax.axis_index("y"), lax.axis_index("c")


def _other_chips(x, y):
    return [(1 - x, y), (x, 1 - y), (1 - x, 1 - y)]


def _remote(src, dst, send_sems, recv_sems, k, to):
    return pltpu.make_async_remote_copy(src_ref=src, dst_ref=dst, send_sem=send_sems.at[k], recv_sem=recv_sems.at[k],
                                        device_id=to, device_id_type=MESH)


GATHER_SEMS = 10
SPLIT_ROWS = 32


def _gather_plan(ins, outs, send_sems, recv_sems, local_sems):
    x, y, c = _position()
    sibling = (x, y, 1 - c)
    nbr_x, nbr_y, diag = (1 - x, y, c), (x, 1 - y, c), (1 - x, 1 - y, c)
    local, begin, stages, last = [], [], [], []
    for a in range(len(ins)):
        k0 = GATHER_SEMS * a
        rows = ins[a].shape[0]
        half = rows // 2

        def block(dev):
            return outs[a].at[4 * dev[0] + 2 * dev[1] + dev[2]]

        def part(ref, h):
            return ref.at[pl.ds(h * half, half)]

        def copy(k, src, dst, to):
            return _remote(src, dst, send_sems, recv_sems, k0 + k, to)

        me = (x, y, c)
        local.append(pltpu.make_async_copy(ins[a], block(me), local_sems.at[a]))
        begin.append(copy(0, ins[a], block(me), sibling))
        pass_on = [copy(7 + j, block(dev), block(dev), sibling) for j, dev in enumerate((nbr_x, nbr_y, diag))]
        if rows >= SPLIT_ROWS and rows % SPLIT_ROWS == 0:
            for h in range(2):
                begin.append(copy(1 + h, part(ins[a], h), part(block(me), h), nbr_x))
                begin.append(copy(3 + h, part(ins[a], h), part(block(me), h), nbr_y))
            from_x = [copy(1 + h, part(block(nbr_x), h), part(block(nbr_x), h), sibling) for h in range(2)]
            from_y = [copy(3 + h, part(block(nbr_y), h), part(block(nbr_y), h), sibling) for h in range(2)]
            fwd_0 = copy(5, part(block(nbr_x), 0), part(block(nbr_x), 0), nbr_y)
            fwd_1 = copy(6, part(block(nbr_y), 1), part(block(nbr_y), 1), nbr_x)
            got_0 = copy(5, part(block(diag), 0), part(block(diag), 0), sibling)
            got_1 = copy(6, part(block(diag), 1), part(block(diag), 1), sibling)
            stages.append(([from_x[0]], [fwd_0]))
            stages.append(([from_y[1]], [fwd_1]))
            stages.append(([from_x[1]], [pass_on[0]]))
            stages.append(([from_y[0]], [pass_on[1]]))
            stages.append(([got_0, got_1], [pass_on[2]]))
        else:
            for j, dev in enumerate((nbr_x, nbr_y, diag)):
                begin.append(copy(1 + 2 * j, ins[a], block(me), dev))
                stages.append(([copy(1 + 2 * j, block(dev), block(dev), sibling)], [pass_on[j]]))
        other = (x, y, 1 - c)
        last.append(copy(0, block(other), block(other), sibling))
        for j, dev in enumerate((nbr_x, nbr_y, diag)):
            theirs = (dev[0], dev[1], 1 - c)
            last.append(copy(7 + j, block(theirs), block(theirs), sibling))
    return local, begin, stages, last


def _gather_begin(ins, outs, send_sems, recv_sems, local_sems):
    local, begin, _, _ = _gather_plan(ins, outs, send_sems, recv_sems, local_sems)
    for cp in local + begin:
        cp.start()


def _gather_finish(ins, outs, send_sems, recv_sems, local_sems):
    local, begin, stages, last = _gather_plan(ins, outs, send_sems, recv_sems, local_sems)
    started = []
    for arrivals, onward in stages:
        for cp in arrivals:
            cp.wait_recv()
        for cp in onward:
            cp.start()
            started.append(cp)
    for cp in last:
        cp.wait_recv()
    for cp in begin + started:
        cp.wait_send()
    for cp in local:
        cp.wait()


def _gather_scratch(n):
    return [pltpu.SemaphoreType.DMA((GATHER_SEMS * n,)), pltpu.SemaphoreType.DMA((GATHER_SEMS * n,)),
            pltpu.SemaphoreType.DMA((n,))]


_HBM = pl.BlockSpec(memory_space=pltpu.HBM)
_SEM = pl.BlockSpec(memory_space=pltpu.SEMAPHORE)
_DATAFLOW = pltpu.SideEffectType.DATAFLOW_SIDE_EFFECTING


def _to_all_plan(srcs, lands, send_sems, recv_sems):
    x, y, c = _position()
    me = 4 * x + 2 * y + c
    copies = []
    for a in range(len(srcs)):
        for k in range(1, N_DEV):
            fx, fy, fc = (k >> 2) & 1, (k >> 1) & 1, k & 1
            to = (1 - x if fx else x, 1 - y if fy else y, 1 - c if fc else c)
            copies.append(_remote(srcs[a], lands[a].at[me], send_sems, recv_sems, (N_DEV - 1) * a + k - 1, to))
    return copies


def _to_sibling_plan(srcs, lands, send_sems, recv_sems):
    x, y, c = _position()
    copies = []
    for a in range(len(srcs)):
        for q in range(4):
            copies.append(_remote(srcs[a].at[2 * q + (1 - c)], lands[a].at[q], send_sems, recv_sems, 4 * a + q,
                                  (x, y, 1 - c)))
    return copies


def _to_chips_plan(srcs, lands, send_sems, recv_sems):
    x, y, c = _position()
    copies = []
    for a in range(len(srcs)):
        for j, (cx, cy) in enumerate(_other_chips(x, y)):
            copies.append(_remote(srcs[a].at[2 * cx + cy], lands[a].at[j], send_sems, recv_sems, 3 * a + j, (cx, cy, c)))
    return copies


def copies_start(srcs, land_shapes, plan, per_array, name):
    n = len(srcs)
    n_sem = per_array * n
    lands = [lax.empty(s.shape, s.dtype) for s in land_shapes]

    def body(*refs):
        src_refs, land_refs = refs[:n], refs[n:2 * n]
        send_sems, recv_sems = refs[2 * n], refs[2 * n + 1]
        token = refs[-1]
        for cp in plan(src_refs, land_refs, send_sems, recv_sems):
            cp.start()
        token[...] = jnp.zeros_like(token)

    out = pl.pallas_call(
        body, name=name,
        out_shape=(pltpu.SemaphoreType.DMA((n_sem,)), pltpu.SemaphoreType.DMA((n_sem,)))
        + tuple(pltpu.HBM(s.shape, s.dtype) for s in srcs)
        + tuple(pltpu.HBM(s.shape, s.dtype) for s in land_shapes)
        + (jax.ShapeDtypeStruct((8, LANES), F32),),
        in_specs=[_HBM] * (2 * n),
        out_specs=(_SEM, _SEM) + (_HBM,) * (2 * n) + (pl.BlockSpec(memory_space=pltpu.VMEM),),
        input_output_aliases={i: 2 + i for i in range(2 * n)},
        compiler_params=pltpu.CompilerParams(has_side_effects=_DATAFLOW),
    )(*[pltpu.with_memory_space_constraint(s, pltpu.HBM) for s in srcs],
      *[pltpu.with_memory_space_constraint(l, pltpu.HBM) for l in lands])
    return out[:-1], out[-1]


def copies_wait(handles, plan, after, name):
    send_sems, recv_sems = handles[0], handles[1]
    n = (len(handles) - 2) // 2
    thru = handles[2:]

    def body(*refs):
        src_refs, land_refs = refs[:n], refs[n:2 * n]
        send_sems, recv_sems = refs[2 * n], refs[2 * n + 1]
        copies = plan(src_refs, land_refs, send_sems, recv_sems)
        for cp in copies:
            cp.wait_recv()
        for cp in copies:
            cp.wait_send()

    out = pl.pallas_call(
        body, name=name,
        out_shape=tuple(pltpu.HBM(t.shape, t.dtype) for t in thru),
        in_specs=[_HBM] * (2 * n) + [_SEM, _SEM, pl.BlockSpec(memory_space=pl.ANY)],
        out_specs=(_HBM,) * (2 * n),
        input_output_aliases={i: i for i in range(2 * n)},
        compiler_params=pltpu.CompilerParams(has_side_effects=_DATAFLOW),
    )(*thru, send_sems, recv_sems, after)
    return out[:n], out[n:]


def all_sum_small(vec, name):
    R = vec.shape[0]

    def body(v_ref, tot_ref, all_ref, send_sems, recv_sems):
        x, y, c = _position()
        me = 4 * x + 2 * y + c
        all_ref[me] = v_ref[...]
        copies = []
        for k in range(1, N_DEV):
            fx, fy, fc = (k >> 2) & 1, (k >> 1) & 1, k & 1
            to = (1 - x if fx else x, 1 - y if fy else y, 1 - c if fc else c)
            cp = _remote(v_ref, all_ref.at[me], send_sems, recv_sems, k - 1, to)
            cp.start()
            copies.append(cp)
        for cp in copies:
            cp.wait_recv()
        for cp in copies:
            cp.wait_send()
        tot = all_ref[0]
        for j in range(1, N_DEV):
            tot = tot + all_ref[j]
        tot_ref[...] = tot

    vmem = pl.BlockSpec(memory_space=pltpu.VMEM)
    return pl.pallas_call(
        body, name=name,
        in_specs=[vmem], out_specs=vmem,
        out_shape=jax.ShapeDtypeStruct((R, LANES), F32),
        scratch_shapes=[pltpu.VMEM((N_DEV, R, LANES), F32),
                        pltpu.SemaphoreType.DMA((N_DEV - 1,)), pltpu.SemaphoreType.DMA((N_DEV - 1,))],
        compiler_params=pltpu.CompilerParams(vmem_limit_bytes=VMEM_LIMIT),
    )(vec)


def pair_add(parts, theirs, place, name):
    _, R, C = theirs.shape
    tr = _pick(R, 1024, 8)

    def body(place_ref, a_ref, b_ref, o_ref):
        o_ref[...] = (a_ref[...].astype(F32) + b_ref[...].astype(F32)).astype(BF16)

    blk = pl.BlockSpec((None, tr, C), lambda q, i, place_ref: (q, i, 0))
    return pl.pallas_call(
        body, name=name,
        grid_spec=pltpu.PrefetchScalarGridSpec(
            num_scalar_prefetch=1, grid=(4, R // tr),
            in_specs=[pl.BlockSpec((None, tr, C), lambda q, i, place_ref: (2 * q + place_ref[2], i, 0)), blk],
            out_specs=blk),
        out_shape=jax.ShapeDtypeStruct(theirs.shape, BF16),
        compiler_params=_params(("parallel", "parallel")),
    )(place, parts, theirs)


def _adamw_math(w, g, m, v):
    m = ADAM_B1 * m + (1.0 - ADAM_B1) * g
    v = ADAM_B2 * v + (1.0 - ADAM_B2) * jnp.square(g)
    m_hat = m / (1.0 - ADAM_B1 ** ADAM_STEP)
    v_hat = v / (1.0 - ADAM_B2 ** ADAM_STEP)
    delta = -ADAM_LR * (m_hat / (jnp.sqrt(v_hat) + ADAM_EPS) + ADAM_WD * w)
    return delta, m, v


def adamw_sharded(w, m, v, parts, sib, others, place, name):
    R, C = w.shape
    tr = _pick(R, 256, 8)

    def body(place_ref, w_ref, m_ref, v_ref, a_ref, b_ref, o_ref, g_ref, d_ref, nm_ref, nv_ref):
        g = a_ref[...].astype(F32) + b_ref[...].astype(F32)
        for j in range(3):
            g = g + o_ref[j].astype(F32)
        delta, nm, nv = _adamw_math(w_ref[...], g, m_ref[...], v_ref[...])
        g_ref[...] = g
        d_ref[...] = delta
        nm_ref[...] = nm
        nv_ref[...] = nv

    row = pl.BlockSpec((tr, C), lambda i, place_ref: (i, 0))
    return pl.pallas_call(
        body, name=name,
        grid_spec=pltpu.PrefetchScalarGridSpec(
            num_scalar_prefetch=1, grid=(R // tr,),
            in_specs=[row] * 3 + [pl.BlockSpec((None, tr, C), lambda i, place_ref: (place_ref[0], i, 0)),
                                  pl.BlockSpec((None, tr, C), lambda i, place_ref: (place_ref[1], i, 0)),
                                  pl.BlockSpec((3, tr, C), lambda i, place_ref: (0, i, 0))],
            out_specs=[row] * 4),
        out_shape=[jax.ShapeDtypeStruct((R, C), F32)] * 4,
        compiler_params=_params(("parallel",)),
    )(place, w, m, v, parts, sib, others)


def adamw_packed(w, g, m, v, name):
    R = w.shape[0]

    def body(w_ref, g_ref, m_ref, v_ref, d_ref, nm_ref, nv_ref):
        delta, nm, nv = _adamw_math(w_ref[...], g_ref[...], m_ref[...], v_ref[...])
        d_ref[...] = delta
        nm_ref[...] = nm
        nv_ref[...] = nv

    full = pl.BlockSpec((R, LANES), lambda i: (0, 0))
    return pl.pallas_call(
        body, name=name, grid=(1,),
        in_specs=[full] * 4, out_specs=[full] * 3,
        out_shape=[jax.ShapeDtypeStruct((R, LANES), F32)] * 3,
        compiler_params=_params(("arbitrary",)),
    )(w, g, m, v)


def _pack(arrays):
    flat = []
    sizes = []
    for a in arrays:
        f = a.reshape(-1).astype(F32)
        pad = (-f.shape[0]) % LANES
        if pad:
            f = jnp.concatenate([f, jnp.zeros((pad,), F32)])
        flat.append(f)
        sizes.append(f.shape[0])
    rows = sum(sizes) // LANES
    pad_rows = (-rows) % 8
    if pad_rows:
        flat.append(jnp.zeros((pad_rows * LANES,), F32))
    return jnp.concatenate(flat).reshape(-1, LANES), sizes


def _unpack(packed, sizes, shapes):
    flat = packed.reshape(-1)
    out = []
    off = 0
    for size, shape in zip(sizes, shapes):
        n = int(np.prod(shape))
        out.append(flat[off:off + n].reshape(shape))
        off += size
    return out


def _to_blocks(full, axis):
    if axis == 0:
        return full.reshape(N_DEV, full.shape[0] // N_DEV, full.shape[1])
    r, n = full.shape
    return full.reshape(r, N_DEV, n // N_DEV).transpose(1, 0, 2)


def _from_blocks(blocks, axis):
    if axis == 0:
        return blocks.reshape(blocks.shape[0] * blocks.shape[1], blocks.shape[2])
    return blocks.transpose(1, 0, 2).reshape(blocks.shape[1], blocks.shape[0] * blocks.shape[2])


def kernel(x, ln0_g, ln0_b, w_in, b_in, conv_w, w_a, w_b, w_o, b_o, ln1_g, ln1_b, w_up, b_up, ffn_conv_w, ffn_conv_b, w_down, b_down, ln2_g, ln2_b, loss_target, m_ln0_g, m_ln0_b, m_w_in, m_b_in, m_conv_w, m_w_a, m_w_b, m_w_o, m_b_o, m_ln1_g, m_ln1_b, m_w_up, m_b_up, m_ffn_conv_w, m_ffn_conv_b, m_w_down, m_b_down, m_ln2_g, m_ln2_b, v_ln0_g, v_ln0_b, v_w_in, v_b_in, v_conv_w, v_w_a, v_w_b, v_w_o, v_b_o, v_ln1_g, v_ln1_b, v_w_up, v_b_up, v_ffn_conv_w, v_ffn_conv_b, v_w_down, v_b_down, v_ln2_g, v_ln2_b):
    T, D = x.shape[1], x.shape[2]
    F = ffn_conv_b.shape[-1]
    xs = x.reshape(T, D)
    tgt = loss_target.reshape(T, D)
    dev = 4 * lax.axis_index("x") + 2 * lax.axis_index("y") + lax.axis_index("c")
    chip = 2 * lax.axis_index("x") + lax.axis_index("y")
    core = lax.axis_index("c")
    place = jnp.stack([dev, chip, core]).astype(jnp.int32)

    big = dict(w_in=(w_in[0], 1), w_a=(w_a[0], 0), w_b=(w_b[0], 1), w_o=(w_o[0], 0), w_up=(w_up[0], 1),
               w_down=(w_down[0], 0))
    names = list(big)
    shards = {k: big[k][0].astype(BF16) for k in names}
    ln0g, ln0b = ln0_g.reshape(1, D), ln0_b.reshape(1, D)
    h0, h0b, *rest = ln_fwd(xs, None, ln0g, ln0b, "ln0_fwd_gather_w_in", dilations=DILATIONS[1:],
                            gather=[shards["w_in"], conv_w[0], ffn_conv_w[0]])
    h0_res = [h0b] + [h.reshape(T, D) for h in rest[:2]]
    g_in, g_conv, g_fcw = rest[2:]
    full = {"w_in": _from_blocks(g_in, 1)}
    conv_full = _from_blocks(g_conv, 1)
    fcw_full = _from_blocks(g_fcw, 1)
    late_groups = (("w_a", "w_b", "w_o"), ("w_up", "w_down"))
    late_handles = []
    token = conv_full[:1, :1] * 0.0
    for n, keys in enumerate(late_groups):
        srcs = [shards[k] + token[0, 0].astype(BF16) for k in keys]
        handles, token = copies_start(srcs, [jax.ShapeDtypeStruct((N_DEV,) + s.shape, BF16) for s in srcs],
                                      _to_all_plan, N_DEV - 1, f"gather_late_{n}_start")
        late_handles.append(handles)

    def late_weights(n, after):
        _, lands = copies_wait(late_handles[n], _to_all_plan, after, f"gather_late_{n}_wait")
        for k, land in zip(late_groups[n], lands):
            full[k] = _from_blocks(lax.dynamic_update_index_in_dim(land, shards[k], dev, 0), big[k][1])

    o_q = 3 * D
    o_g = o_q + 3 * QKV_W
    w_pa, w_qkv, w_pg = full["w_in"][:, :o_q], full["w_in"][:, o_q:o_g], full["w_in"][:, o_g:]
    b_pa, b_qkv, b_pg = b_in[:, :o_q], b_in[:, o_q:o_g], b_in[:, o_g:]

    proj_a = mm_nn(h0b, w_pa, b_pa, ACT, "proj_conv", after=token)
    proj_g = mm_nn(h0b, w_pg, b_pg, ACT, "proj_gates")
    zero_d = jnp.zeros((1, D), F32)
    s_a = conv_a_fwd(proj_a, conv_full, "conv_a_fwd")
    late_weights(0, s_a)
    y_a = mm_nn(s_a, full["w_a"], zero_d, ACT, "branch_a_out")

    def group_cols(m, g):
        return jnp.concatenate([m[:, s * QKV_W + g * GROUP_W:s * QKV_W + (g + 1) * GROUP_W] for s in range(3)], 1)

    w_grp = [group_cols(w_qkv, g) for g in range(3)]
    qkvs, outs, lses = [], [], []
    for g, d in enumerate(DILATIONS):
        qkv = mm_nn(h0_res[g], w_grp[g], group_cols(b_qkv, g), BF16, f"proj_qkv_{g}").reshape(d, T // d, 3 * GROUP_W)
        o, l = att_fwd(qkv, g, f"att_fwd_{g}")
        qkvs.append(qkv)
        outs.append(o)
        lses.append(l)
    comb = combine_fwd(outs, lses, "combine_fwd")
    y_b = mm_nn(comb, full["w_b"], zero_d, ACT, "branch_b_out")
    z = gate_fwd(proj_g, y_a, y_b, "gate_fwd")
    h1, h1b, mix = ln_fwd(h0, ("nn", z, full["w_o"], b_o), ln1_g, ln1_b, "mix_out_ln1_fwd")
    late_weights(1, h1b)
    up, f_act = ffn_up_conv_f(h1b, full["w_up"], b_up, fcw_full, ffn_conv_b, "ffn_up_conv_f")

    dr2, dr2b, d_ln2_g, d_ln2_b, d_b_down, loss_part = ln_bwd(
        h1, ("nn", f_act, full["w_down"], b_down), ln2_g, ln2_b, None, None, tgt, "ffn_down_ln2_loss_bwd")
    dw_down, _ = mm_tn(f_act, dr2b, "dw_down")
    d_a, d_gate, cs_a, cs_gate, d_fcb, d_fcw = conv_f_bwd(dr2b, full["w_down"], up, fcw_full, ffn_conv_b,
                                                          "d_ffn_act_conv_f_bwd")
    dw_up_a, _ = mm_tn(h1b, d_a, "dw_up_a")
    dw_up_g, _ = mm_tn(h1b, d_gate, "dw_up_gate")
    dr1, dr1b, d_ln1_g, d_ln1_b, d_b_o, _ = ln_bwd(h0, mix, ln1_g, ln1_b, dr2, ("nt", [d_a, d_gate], full["w_up"]), None,
                                                   "d_h1_ln1_bwd")
    dw_o, _ = mm_tn(z, dr1b, "dw_o")
    dz = mm_nt(dr1b, full["w_o"], None, "d_z", out_dtype=ACT)
    dy_a, dy_b, dproj_g = gate_bwd(dz, proj_g, y_a, y_b, "gate_bwd")
    dw_a, _ = mm_tn(s_a, dy_a, "dw_a")
    ds_a = mm_nt(dy_a, full["w_a"], None, "d_s_a", out_dtype=ACT)
    dproj_a, d_conv = conv_a_bwd(ds_a, proj_a, conv_full, "conv_a_bwd")
    dw_b, _ = mm_tn(comb, dy_b, "dw_b")

    rs_mine, rs_sib, rs_handles = {}, {}, {}

    sib_handles = {}

    def to_sibling_start(keys, grads, tag):
        parts = [_to_blocks(grads[k], big[k][1]) for k in keys]
        handles, tok = copies_start(parts, [jax.ShapeDtypeStruct((4,) + p.shape[1:], BF16) for p in parts],
                                    _to_sibling_plan, 4, f"grads_to_sibling_{tag}_start")
        sib_handles[tag] = (keys, handles)
        return tok

    def to_chips_start(tag, after):
        keys, handles = sib_handles[tag]
        parts, from_sib = copies_wait(handles, _to_sibling_plan, after, f"grads_to_sibling_{tag}_wait")
        sums = [pair_add(a, b, place, f"chip_sum_{k}") for k, a, b in zip(keys, parts, from_sib)]
        handles, tok = copies_start(sums, [jax.ShapeDtypeStruct((3,) + s.shape[1:], BF16) for s in sums],
                                    _to_chips_plan, 3, f"grads_to_chips_{tag}_start")
        for k, a, b in zip(keys, parts, from_sib):
            rs_mine[k], rs_sib[k] = a, b
        rs_handles[tag] = (keys, handles)
        return tok

    tok_a = to_sibling_start(("w_a", "w_b", "w_o", "w_up", "w_down"),
                             dict(w_a=dw_a, w_b=dw_b, w_o=dw_o, w_up=jnp.concatenate([dw_up_a, dw_up_g], 1),
                                  w_down=dw_down), "a")
    dcomb = mm_nt(dy_b, full["w_b"], None, "d_comb", after=tok_a, out_dtype=ACT)
    dos, dms = combine_bwd(dcomb, outs, lses, "combine_bwd")
    tok_a = to_chips_start("a", dms[0])
    dw_grp, cs_grp, dqkvs = [], [], []
    for g, d in enumerate(DILATIONS):
        dq, dk, dv = att_bwd(qkvs[g], dos[g], lses[g], dms[g], g, f"att_bwd_{g}", after=tok_a if g == 0 else None)
        dqkv = [t.reshape(T, GROUP_W) for t in (dq, dk, dv)]
        dwg, csg = mm_tn(h0_res[g], dqkv, f"dw_in_qkv_{g}")
        dqkvs.append(dqkv)
        dw_grp.append(dwg)
        cs_grp.append(csg)
    dw_pa, cs_pa = mm_tn(h0b, dproj_a, "dw_in_conv")
    dw_pg, cs_pg = mm_tn(h0b, dproj_g, "dw_in_gates")

    def ungroup(parts):
        return jnp.concatenate([p[:, s * GROUP_W:(s + 1) * GROUP_W] for s in range(3) for p in parts], 1)

    db_in_parts = [cs_pa, ungroup(cs_grp), cs_pg]
    tok_b = to_sibling_start(("w_in",), dict(w_in=jnp.concatenate([dw_pa, ungroup(dw_grp), dw_pg], 1)), "b")
    dh0 = mm_nt(dproj_a, w_pa, None, "d_h0_conv", after=tok_b)
    tok_b = to_chips_start("b", dh0)
    dh0 = mm_nt(dproj_g, w_pg, dh0, "d_h0_gates", after=tok_b)
    dh0_res = [(mm_nt(dqkvs[g], w_grp[g], None, f"d_h0_qkv_{g}").reshape(d, T // d, D), d)
               for g, d in enumerate(DILATIONS) if g > 0]
    dx, _, d_ln0_g, d_ln0_b, _, _ = ln_bwd(xs, None, ln0g, ln0b, dr1, ("nt", dqkvs[0], w_grp[0]), None, "d_h0_ln0_bwd",
                                           by_residue=[(dh0.reshape(1, T, D), 1)] + dh0_res)

    small = [d_ln0_g, d_ln0_b, jnp.concatenate(db_in_parts, 1), d_conv, d_b_o, d_ln1_g, d_ln1_b,
             jnp.concatenate([cs_a, cs_gate], 1), d_fcw, d_fcb, d_b_down, d_ln2_g, d_ln2_b, loss_part]
    packed, sizes = _pack(small)
    total = all_sum_small(packed, "sum_small")
    (g_ln0_g, g_ln0_b, g_b_in, g_conv_full, g_b_o, g_ln1_g, g_ln1_b, g_b_up, g_fcw_full, g_fcb, g_b_down, g_ln2_g,
     g_ln2_b, loss) = _unpack(total, sizes, [a.shape for a in small])
    cw = conv_w.shape[-1]
    fw = ffn_conv_w.shape[-1]
    g_conv = lax.dynamic_slice_in_dim(g_conv_full, dev * cw, cw, 1)
    g_fcw = lax.dynamic_slice_in_dim(g_fcw_full, dev * fw, fw, 1)

    from_chips = {}
    for tag, (keys, handles) in rs_handles.items():
        _, lands = copies_wait(handles, _to_chips_plan, total, f"grads_to_chips_{tag}_wait")
        from_chips.update(zip(keys, lands))

    moments = dict(w_in=(m_w_in, v_w_in), w_a=(m_w_a, v_w_a), w_b=(m_w_b, v_w_b), w_o=(m_w_o, v_w_o),
                   w_up=(m_w_up, v_w_up), w_down=(m_w_down, v_w_down))
    res_big = {}
    for k in names:
        res_big[k] = adamw_sharded(big[k][0], moments[k][0][0], moments[k][1][0], rs_mine[k], rs_sib[k], from_chips[k],
                                   place, f"adamw_{k}")

    small_names = ["ln0_g", "ln0_b", "b_in", "conv_w", "b_o", "ln1_g", "ln1_b", "b_up", "ffn_conv_w", "ffn_conv_b",
                   "b_down", "ln2_g", "ln2_b"]
    small_w = [ln0_g, ln0_b, b_in, conv_w, b_o, ln1_g, ln1_b, b_up, ffn_conv_w, ffn_conv_b, b_down, ln2_g, ln2_b]
    small_m = [m_ln0_g, m_ln0_b, m_b_in, m_conv_w, m_b_o, m_ln1_g, m_ln1_b, m_b_up, m_ffn_conv_w, m_ffn_conv_b,
               m_b_down, m_ln2_g, m_ln2_b]
    small_v = [v_ln0_g, v_ln0_b, v_b_in, v_conv_w, v_b_o, v_ln1_g, v_ln1_b, v_b_up, v_ffn_conv_w, v_ffn_conv_b,
               v_b_down, v_ln2_g, v_ln2_b]
    small_g = [g_ln0_g, g_ln0_b, g_b_in, g_conv, g_b_o, g_ln1_g, g_ln1_b, g_b_up, g_fcw, g_fcb, g_b_down, g_ln2_g,
               g_ln2_b]
    shapes = [w.shape for w in small_w]
    small_g = [g.reshape(s) for g, s in zip(small_g, shapes)]
    pw, psz = _pack(small_w)
    pg, _ = _pack(small_g)
    pm, _ = _pack(small_m)
    pv, _ = _pack(small_v)
    pd, pnm, pnv = adamw_packed(pw, pg, pm, pv, "adamw_small")
    res_small = {k: (g, d_, m_, v_) for k, g, d_, m_, v_ in zip(
        small_names, small_g, _unpack(pd, psz, shapes), _unpack(pnm, psz, shapes), _unpack(pnv, psz, shapes))}

    order = ["ln0_g", "ln0_b", "w_in", "b_in", "conv_w", "w_a", "w_b", "w_o", "b_o", "ln1_g", "ln1_b", "w_up", "b_up",
             "ffn_conv_w", "ffn_conv_b", "w_down", "b_down", "ln2_g", "ln2_b"]

    def result(k, j):
        if k in res_big:
            return res_big[k][j][None]
        return res_small[k][j]

    out = [loss.reshape(()), dx.reshape(x.shape)]
    for j in range(4):
        out += [result(k, j) for k in order]
    return tuple(out)
```

```python
import math

import numpy as np
import jax
import jax.numpy as jnp
from jax import lax
from jax.experimental import pallas as pl
from jax.experimental.pallas import tpu as pltpu

F32 = jnp.float32
BF16 = jnp.bfloat16
ACT = BF16

N_DEV = 8
LN_EPS = 1e-5
ALPHA = (2.0 * 1) ** 0.25
HEAD_DIM = 64
GROUP_W = 512
QKV_W = 3 * GROUP_W
DILATIONS = (1, 4, 16)
RADIUS = 64
LANES = 128
HALO = 8
HALO_BF16 = 16
ATT_TQ = 128

ADAM_LR = 0.001
ADAM_B1 = 0.9
ADAM_B2 = 0.999
ADAM_EPS = 1e-08
ADAM_WD = 0.01
ADAM_STEP = 10

VMEM_LIMIT = 52 * 1024 * 1024
OUT_TILE_BYTES = 8 * 1024 * 1024
MESH = pl.DeviceIdType.MESH
NT_DIMS = (((1,), (1,)), ((), ()))
TN_DIMS = (((0,), (0,)), ((), ()))


def _pick(n, target, align=LANES):
    if n <= target:
        return n
    best = None
    for t in range(align, target + 1, align):
        if n % t == 0:
            best = t
    assert best is not None, (n, target, align)
    return best


def _params(sems=None):
    return pltpu.CompilerParams(dimension_semantics=sems, vmem_limit_bytes=VMEM_LIMIT)


def _alibi_slopes():
    n = 3 * 8
    return np.exp2(-8.0 * np.arange(1, n + 1, dtype=np.float64) / n).astype(np.float32).reshape(3, 8)


def _ln_stats(r):
    mu = jnp.mean(r, -1, keepdims=True)
    xc = r - mu
    var = jnp.mean(xc * xc, -1, keepdims=True)
    rstd = lax.rsqrt(var + LN_EPS)
    return xc, rstd


def _load_natural(ref, d, scr):
    if d == 1:
        return ref[0].astype(F32)
    n, C = ref.shape[1], ref.shape[2]
    for c in range(C // LANES):
        for r in range(d):
            scr[c, pl.ds(r, n, stride=d), :] = ref[r, :, c * LANES:(c + 1) * LANES].astype(F32)
    return jnp.concatenate([scr[c] for c in range(C // LANES)], axis=1)


def _store_by_residue(val, ref, d, scr):
    if d == 1:
        ref[0] = val.astype(ref.dtype)
        return
    n, C = ref.shape[1], ref.shape[2]
    for c in range(C // LANES):
        scr[c] = val[:, c * LANES:(c + 1) * LANES]
    for c in range(C // LANES):
        for r in range(d):
            ref[r, :, c * LANES:(c + 1) * LANES] = scr[c, pl.ds(r, n, stride=d), :].astype(ref.dtype)


def _residue_spec(tm, d, C):
    return pl.BlockSpec((d, tm // d, C), lambda i: (0, i, 0))


def _residue_scratch(tm, C):
    return pltpu.VMEM((C // LANES, tm, LANES), F32)


def ln_fwd(a, res, g, b, name, dilations=(), gather=()):
    T, D = a.shape
    res_mm = isinstance(res, tuple)
    tm = _pick(T, 256 if res_mm else 512, 8)
    res_ins = list(res[1:]) if res_mm else ([] if res is None else [res])
    nd = len(dilations)
    ng = len(gather)
    n_in = 1 + len(res_ins) + 2
    last = T // tm - 1

    def body(*refs):
        a_ref = refs[0]
        r = a_ref[...]
        if res_mm:
            res_val = jnp.dot(refs[1][...], refs[2][...], preferred_element_type=F32) + refs[3][...]
            refs[-1 - n_scratch][...] = res_val
            r = ALPHA * r + res_val
        elif res_ins:
            r = ALPHA * r + refs[1][...]
        g_ref, b_ref = refs[n_in - 2], refs[n_in - 1]
        shard_refs = refs[n_in:n_in + ng]
        h_ref, hb_ref = refs[n_in + ng], refs[n_in + ng + 1]
        p_refs = refs[n_in + ng + 2:n_in + ng + 2 + nd]
        full_refs = refs[n_in + ng + 2 + nd:n_in + 2 * ng + 2 + nd]
        scratch = refs[len(refs) - n_scratch:]
        sems = scratch[len(scratch) - 3:] if ng else ()

        if ng:
            @pl.when(pl.program_id(0) == 0)
            def _():
                _gather_begin(shard_refs, full_refs, *sems)

        xc, rstd = _ln_stats(r)
        h = xc * rstd * g_ref[...] + b_ref[...]
        h_ref[...] = h
        hb_ref[...] = h.astype(BF16)
        for d, p_ref in zip(dilations, p_refs):
            _store_by_residue(h, p_ref, d, scratch[0])

        if ng:
            @pl.when(pl.program_id(0) == last)
            def _():
                _gather_finish(shard_refs, full_refs, *sems)

    row = pl.BlockSpec((tm, D), lambda i: (i, 0))
    vec = pl.BlockSpec((1, D), lambda i: (0, 0))
    hbm = pl.BlockSpec(memory_space=pl.ANY)
    if res_mm:
        res_specs = [pl.BlockSpec((tm, res[1].shape[1]), lambda i: (i, 0)), pl.BlockSpec(res[2].shape, lambda i: (0, 0)), vec]
    else:
        res_specs = [row] * len(res_ins)
    scratch_shapes = ([_residue_scratch(tm, D)] if nd else []) + (_gather_scratch(ng) if ng else [])
    n_scratch = len(scratch_shapes)
    ins = [a] + res_ins + [g, b] + list(gather)
    return pl.pallas_call(
        body, name=name, grid=(T // tm,),
        in_specs=[row] + res_specs + [vec, vec] + [hbm] * ng,
        out_specs=[row, row] + [_residue_spec(tm, d, D) for d in dilations] + [hbm] * ng + ([row] if res_mm else []),
        out_shape=[jax.ShapeDtypeStruct((T, D), F32), jax.ShapeDtypeStruct((T, D), BF16)]
        + [jax.ShapeDtypeStruct((d, T // d, D), BF16) for d in dilations]
        + [jax.ShapeDtypeStruct((N_DEV,) + s.shape, s.dtype) for s in gather]
        + ([jax.ShapeDtypeStruct((T, D), F32)] if res_mm else []),
        scratch_shapes=scratch_shapes,
        compiler_params=_params(("arbitrary",) if ng else ("parallel",)),
    )(*ins)


def ln_bwd(a, res, g, b, d1, d2, tgt, name, by_residue=()):
    T, D = a.shape
    tm = _pick(T, 256, 8)
    loss_mode = tgt is not None
    nres = len(by_residue)
    row = pl.BlockSpec((tm, D), lambda i: (i, 0))
    vec = pl.BlockSpec((1, D), lambda i: (0, 0))
    one = pl.BlockSpec((1, 1), lambda i: (0, 0))

    def rows_of(x):
        return pl.BlockSpec((tm, x.shape[1]), lambda i: (i, 0))

    def whole(x):
        return pl.BlockSpec(x.shape, lambda i: (0, 0))

    ins, in_specs, slots = [], [], {}

    def operand(key, arrays, specs):
        slots[key] = (len(ins), len(arrays))
        ins.extend(arrays)
        in_specs.extend(specs)

    operand("a", [a], [row])
    if isinstance(res, tuple):
        _, x, w, bias = res
        operand("res_mm", [x, w, bias], [rows_of(x), whole(w), vec])
    elif res is not None:
        operand("res", [res], [row])
    operand("gb", [g, b], [vec, vec])
    if loss_mode:
        operand("tgt", [tgt], [row])
    else:
        operand("d1", [d1], [row])
        if isinstance(d2, tuple):
            _, pieces, w = d2
            operand("d2_mm", list(pieces) + [w], [rows_of(p) for p in pieces] + [whole(w)])
        else:
            operand("d2", [d2], [row])
    operand("by_residue", [e for e, _ in by_residue], [_residue_spec(tm, d, D) for _, d in by_residue])
    n_in = len(ins)

    def body(*refs):
        def get(key):
            first, count = slots[key]
            return refs[first:first + count]

        dr_ref, drb_ref, dg_ref, db_ref, ds_ref, loss_ref = refs[n_in:n_in + 6]
        i = pl.program_id(0)

        @pl.when(i == 0)
        def _():
            dg_ref[...] = jnp.zeros_like(dg_ref)
            db_ref[...] = jnp.zeros_like(db_ref)
            ds_ref[...] = jnp.zeros_like(ds_ref)
            loss_ref[...] = jnp.zeros_like(loss_ref)

        r = get("a")[0][...]
        if "res_mm" in slots:
            x_ref, w_ref, bias_ref = get("res_mm")
            r = ALPHA * r + (jnp.dot(x_ref[...], w_ref[...], preferred_element_type=F32) + bias_ref[...])
        elif "res" in slots:
            r = ALPHA * r + get("res")[0][...]
        g_ref, b_ref = get("gb")
        xc, rstd = _ln_stats(r)
        xhat = xc * rstd
        gam = g_ref[...]
        if loss_mode:
            err = xhat * gam + b_ref[...] - get("tgt")[0][...]
            dy = err * (1.0 / D)
            row_loss = jnp.mean(err * err, -1, keepdims=True)
            loss_ref[...] += 0.5 * jnp.sum(row_loss, 0, keepdims=True)
        else:
            if "d2_mm" in slots:
                *p_refs, w_ref = get("d2_mm")
                av = p_refs[0][...] if len(p_refs) == 1 else jnp.concatenate([p[...] for p in p_refs], axis=1)
                d2v = lax.dot_general(av, w_ref[...], NT_DIMS, preferred_element_type=F32)
            else:
                d2v = get("d2")[0][...]
            dy = ALPHA * get("d1")[0][...] + d2v
        for (_, d), e_ref in zip(by_residue, get("by_residue")):
            dy = dy + _load_natural(e_ref, d, refs[-1])
        dyg = dy * gam
        c1 = jnp.mean(dyg, -1, keepdims=True)
        c2 = jnp.mean(dyg * xhat, -1, keepdims=True)
        dr = rstd * (dyg - c1 - xhat * c2)
        dr_ref[...] = dr
        drb_ref[...] = dr.astype(BF16)
        dg_ref[...] += jnp.sum(dy * xhat, 0, keepdims=True)
        db_ref[...] += jnp.sum(dy, 0, keepdims=True)
        ds_ref[...] += jnp.sum(dr, 0, keepdims=True)

    return pl.pallas_call(
        body, name=name, grid=(T // tm,),
        in_specs=in_specs,
        out_specs=[row, row, vec, vec, vec, one],
        out_shape=[jax.ShapeDtypeStruct((T, D), F32), jax.ShapeDtypeStruct((T, D), BF16),
                   jax.ShapeDtypeStruct((1, D), F32), jax.ShapeDtypeStruct((1, D), F32),
                   jax.ShapeDtypeStruct((1, D), F32), jax.ShapeDtypeStruct((1, 1), F32)],
        scratch_shapes=[_residue_scratch(tm, D)] if nres else [],
        compiler_params=_params(("arbitrary",)),
    )(*ins)


_TOKEN_SPEC = pl.BlockSpec((8, LANES), lambda i: (0, 0))


def mm_nn(a, w, bias, out_dtype, name, after=None):
    M, K = a.shape
    N = w.shape[1]
    tm = _pick(M, max(256, min(1024, OUT_TILE_BYTES // (N * jnp.dtype(out_dtype).itemsize))), 8)
    tc = _pick(N, 512)

    def body(a_ref, w_ref, b_ref, *rest):
        o_ref = rest[-1]
        av = a_ref[...]
        for j in range(N // tc):
            cols = slice(j * tc, (j + 1) * tc)
            acc = jnp.dot(av, w_ref[:, cols], preferred_element_type=F32)
            o_ref[:, cols] = (acc + b_ref[:, cols]).astype(out_dtype)

    return pl.pallas_call(
        body, name=name, grid=(M // tm,),
        in_specs=[pl.BlockSpec((tm, K), lambda i: (i, 0)),
                  pl.BlockSpec((K, N), lambda i: (0, 0)),
                  pl.BlockSpec((1, N), lambda i: (0, 0))] + ([] if after is None else [_TOKEN_SPEC]),
        out_specs=pl.BlockSpec((tm, N), lambda i: (i, 0)),
        out_shape=jax.ShapeDtypeStruct((M, N), out_dtype),
        compiler_params=_params(("parallel",)),
    )(a, w, bias, *([] if after is None else [after]))


def mm_nt(a, w, acc_in, name, after=None, w_block=0, out_dtype=F32):
    pieces = list(a) if isinstance(a, (list, tuple)) else [a]
    M = pieces[0].shape[0]
    widths = [p.shape[1] for p in pieces]
    K = sum(widths)
    N = w.shape[0]
    tm = _pick(M, 1024, 8)
    tc = _pick(N, 512)
    has_acc = acc_in is not None
    n_a = len(pieces)

    def body(*refs):
        a_refs, w_ref = refs[:n_a], refs[n_a]
        c_ref = refs[n_a + 1] if has_acc else None
        o_ref = refs[-1]
        av = a_refs[0][...] if n_a == 1 else jnp.concatenate([r[...] for r in a_refs], axis=1)
        for j in range(N // tc):
            cols = slice(j * tc, (j + 1) * tc)
            acc = lax.dot_general(av, w_ref[cols, :], NT_DIMS, preferred_element_type=F32)
            if has_acc:
                acc = acc + c_ref[:, cols]
            o_ref[:, cols] = acc.astype(out_dtype)

    out_spec = pl.BlockSpec((tm, N), lambda i: (i, 0))
    in_specs = [pl.BlockSpec((tm, kw), lambda i: (i, 0)) for kw in widths]
    in_specs.append(pl.BlockSpec((N, K), lambda i: (0, w_block)))
    ins = pieces + [w]
    if has_acc:
        in_specs.append(out_spec)
        ins.append(acc_in)
    if after is not None:
        in_specs.append(_TOKEN_SPEC)
        ins.append(after)
    return pl.pallas_call(
        body, name=name, grid=(M // tm,),
        in_specs=in_specs, out_specs=out_spec,
        out_shape=jax.ShapeDtypeStruct((M, N), out_dtype),
        compiler_params=_params(("parallel",)),
    )(*ins)


def mm_tn(a, b, name, out_dtype=BF16):
    pieces = list(b) if isinstance(b, (list, tuple)) else [b]
    T, M = a.shape
    widths = [p.shape[1] for p in pieces]
    N = sum(widths)
    tk = _pick(T, 1024, 8)
    nk = T // tk
    tc = _pick(M, 256)
    n_b = len(pieces)

    def body(*refs):
        a_ref, b_refs = refs[0], refs[1:1 + n_b]
        o_ref, cs_ref, acc_ref = refs[1 + n_b:]
        k = pl.program_id(0)

        @pl.when(k == 0)
        def _():
            acc_ref[...] = jnp.zeros_like(acc_ref)
            cs_ref[...] = jnp.zeros_like(cs_ref)

        bv = b_refs[0][...] if n_b == 1 else jnp.concatenate([r[...] for r in b_refs], axis=1)
        cs_ref[...] += jnp.sum(bv.astype(F32), 0, keepdims=True)
        for mi in range(M // tc):
            rows = slice(mi * tc, (mi + 1) * tc)
            acc_ref[rows, :] += lax.dot_general(a_ref[:, rows], bv, TN_DIMS, preferred_element_type=F32)

        @pl.when(k == nk - 1)
        def _():
            o_ref[...] = acc_ref[...].astype(out_dtype)

    return pl.pallas_call(
        body, name=name, grid=(nk,),
        in_specs=[pl.BlockSpec((tk, M), lambda k: (k, 0))] + [pl.BlockSpec((tk, wd), lambda k: (k, 0)) for wd in widths],
        out_specs=[pl.BlockSpec((M, N), lambda k: (0, 0)), pl.BlockSpec((1, N), lambda k: (0, 0))],
        out_shape=[jax.ShapeDtypeStruct((M, N), out_dtype), jax.ShapeDtypeStruct((1, N), F32)],
        scratch_shapes=[pltpu.VMEM((M, N), F32)],
        compiler_params=_params(("arbitrary",)),
    )(a, *pieces)


def _ext_rows(prev_ref, main_ref, next_ref, i, tm, T, dtype=F32):
    before = jnp.where(i == 0, 0.0, prev_ref[...])
    after = jnp.where(i == T // tm - 1, 0.0, next_ref[...])
    return jnp.concatenate([before, main_ref[...], after], axis=0).astype(dtype)


def _prev_row(x):
    return pltpu.roll(x, 1, 0)


def _next_row(x):
    return pltpu.roll(x, x.shape[0] - 1, 0)


def _conv3(u, w_ref):
    return _prev_row(u) * w_ref[0:1, :] + u * w_ref[1:2, :] + _next_row(u) * w_ref[2:3, :]


def _main(x, tm, halo=HALO):
    return x[halo:halo + tm]


def _halo_specs(tm, tc, T, col, order, halo=HALO):
    r = tm // halo
    last = T // halo - 1
    if order == "ij":
        return (pl.BlockSpec((halo, tc), lambda i, j: (jnp.maximum(i * r - 1, 0), col(j))),
                pl.BlockSpec((tm, tc), lambda i, j: (i, col(j))),
                pl.BlockSpec((halo, tc), lambda i, j: (jnp.minimum((i + 1) * r, last), col(j))))
    return (pl.BlockSpec((halo, tc), lambda j, i: (jnp.maximum(i * r - 1, 0), col(j))),
            pl.BlockSpec((tm, tc), lambda j, i: (i, col(j))),
            pl.BlockSpec((halo, tc), lambda j, i: (jnp.minimum((i + 1) * r, last), col(j))))


def conv_a_fwd(proj_a, conv_w, name):
    T, D3 = proj_a.shape
    D = D3 // 3
    tm = _pick(T, 256, 8)

    def body(p_ref, m_ref, n_ref, w_ref, o_ref):
        i = pl.program_id(0)
        ext = _ext_rows(p_ref, m_ref, n_ref, i, tm, T)
        u = ext[:, D:2 * D] * ext[:, 2 * D:]
        cu = _conv3(u, w_ref)
        o_ref[...] = (m_ref[:, :D].astype(F32) * _main(cu, tm, HALO_BF16)).astype(BF16)

    prev, main, nxt = _halo_specs(tm, D3, T, lambda j: 0, "ij", HALO_BF16)
    return pl.pallas_call(
        body, name=name, grid=(T // tm, 1),
        in_specs=[prev, main, nxt, pl.BlockSpec((3, D), lambda i, j: (0, 0))],
        out_specs=pl.BlockSpec((tm, D), lambda i, j: (i, 0)),
        out_shape=jax.ShapeDtypeStruct((T, D), BF16),
        compiler_params=_params(("parallel", "arbitrary")),
    )(proj_a, proj_a, proj_a, conv_w)


def conv_a_bwd(dy_a, w_a, proj_a, conv_w, name):
    T, D3 = proj_a.shape
    D = D3 // 3
    tm = _pick(T, 256, 8)

    def body(dp_ref, dm_ref, dn_ref, wa_ref, p_ref, m_ref, n_ref, w_ref, o_ref, dw_ref):
        i = pl.program_id(0)

        @pl.when(i == 0)
        def _():
            dw_ref[...] = jnp.zeros_like(dw_ref)

        ext = _ext_rows(p_ref, m_ref, n_ref, i, tm, T)
        dsa = lax.dot_general(_ext_rows(dp_ref, dm_ref, dn_ref, i, tm, T, dtype=BF16), wa_ref[...], NT_DIMS,
                              preferred_element_type=F32)
        gb, gc, hin = ext[:, :D], ext[:, D:2 * D], ext[:, 2 * D:]
        u = gc * hin
        u_prev, u_next = _prev_row(u), _next_row(u)
        cu = u_prev * w_ref[0:1, :] + u * w_ref[1:2, :] + u_next * w_ref[2:3, :]
        dcu = dsa * gb
        du = _next_row(dcu) * w_ref[0:1, :] + dcu * w_ref[1:2, :] + _prev_row(dcu) * w_ref[2:3, :]
        h = HALO_BF16
        o_ref[:, :D] = _main(dsa * cu, tm, h).astype(BF16)
        o_ref[:, D:2 * D] = _main(du * hin, tm, h).astype(BF16)
        o_ref[:, 2 * D:] = _main(du * gc, tm, h).astype(BF16)
        dcu_m = _main(dcu, tm, h)
        dw_ref[0:1, :] += jnp.sum(dcu_m * _main(u_prev, tm, h), 0, keepdims=True)
        dw_ref[1:2, :] += jnp.sum(dcu_m * _main(u, tm, h), 0, keepdims=True)
        dw_ref[2:3, :] += jnp.sum(dcu_m * _main(u_next, tm, h), 0, keepdims=True)

    dprev, dmain, dnxt = _halo_specs(tm, dy_a.shape[1], T, lambda j: 0, "ij", HALO_BF16)
    prev, main, nxt = _halo_specs(tm, D3, T, lambda j: 0, "ij", HALO_BF16)
    return pl.pallas_call(
        body, name=name, grid=(T // tm, 1),
        in_specs=[dprev, dmain, dnxt, pl.BlockSpec(w_a.shape, lambda i, j: (0, 0)), prev, main, nxt,
                  pl.BlockSpec((3, D), lambda i, j: (0, 0))],
        out_specs=[pl.BlockSpec((tm, D3), lambda i, j: (i, 0)), pl.BlockSpec((3, D), lambda i, j: (0, 0))],
        out_shape=[jax.ShapeDtypeStruct((T, D3), BF16), jax.ShapeDtypeStruct((3, D), F32)],
        compiler_params=_params(("arbitrary", "arbitrary")),
    )(dy_a, dy_a, dy_a, w_a, proj_a, proj_a, proj_a, conv_w)


_INV_SQRT2 = 1.0 / math.sqrt(2.0)
_INV_SQRT_2PI = 1.0 / math.sqrt(2.0 * math.pi)


def ffn_up_conv_f(h, w_up, b_up, fcw, fcb, name):
    T, D = h.shape
    F = fcb.shape[1]
    tm = _pick(T, 256, 8)
    tc = _pick(F, 256)
    halo = HALO_BF16

    def body(hp_ref, hm_ref, hn_ref, w_ref, b_ref, cw_ref, cb_ref, up_ref, f_ref):
        i = pl.program_id(0)
        h_ext = _ext_rows(hp_ref, hm_ref, hn_ref, i, tm, T, dtype=BF16)
        h_main = hm_ref[...]
        rows = i * tm - halo + lax.broadcasted_iota(jnp.int32, (tm + 2 * halo, 1), 0)
        inside = (rows >= 0) & (rows < T)
        for c in range(F // tc):
            cols = slice(c * tc, (c + 1) * tc)
            gcols = slice(F + c * tc, F + (c + 1) * tc)
            a_ext = jnp.dot(h_ext, w_ref[:, cols], preferred_element_type=F32) + b_ref[:, cols]
            a_ext = jnp.where(inside, a_ext, 0.0)
            gate = jnp.dot(h_main, w_ref[:, gcols], preferred_element_type=F32) + b_ref[:, gcols]
            up_ref[:, cols] = _main(a_ext, tm, halo)
            up_ref[:, gcols] = gate
            ca = _main(_prev_row(a_ext) * cw_ref[0:1, cols] + a_ext * cw_ref[1:2, cols]
                       + _next_row(a_ext) * cw_ref[2:3, cols], tm, halo) + cb_ref[:, cols]
            gl = 0.5 * ca * (1.0 + lax.erf(ca * _INV_SQRT2))
            f_ref[:, cols] = (gl * gate).astype(BF16)

    prev, main, nxt = _halo_specs(tm, D, T, lambda j: 0, "ij", halo)
    whole = lambda x: pl.BlockSpec(x.shape, lambda i, j: (0, 0))
    return pl.pallas_call(
        body, name=name, grid=(T // tm, 1),
        in_specs=[prev, main, nxt, whole(w_up), whole(b_up), whole(fcw), whole(fcb)],
        out_specs=[pl.BlockSpec((tm, 2 * F), lambda i, j: (i, 0)), pl.BlockSpec((tm, F), lambda i, j: (i, 0))],
        out_shape=[jax.ShapeDtypeStruct((T, 2 * F), F32), jax.ShapeDtypeStruct((T, F), BF16)],
        compiler_params=_params(("parallel", "arbitrary")),
    )(h, h, h, w_up, b_up, fcw, fcb)


def conv_f_bwd(dy, w_down, up, fcw, fcb, name):
    T, F2 = up.shape
    F = F2 // 2
    D = dy.shape[1]
    tm = _pick(T, 256, 8)
    tc = _pick(F, 256)

    def body(yp_ref, ym_ref, yn_ref, wd_ref, up_ref, um_ref, un_ref, w_ref, b_ref,
             da_ref, dg_ref, csa_ref, csg_ref, dfb_ref, dfw_ref):
        i = pl.program_id(0)
        first, last = i == 0, i == T // tm - 1

        @pl.when(first)
        def _():
            csa_ref[...] = jnp.zeros_like(csa_ref)
            csg_ref[...] = jnp.zeros_like(csg_ref)
            dfb_ref[...] = jnp.zeros_like(dfb_ref)
            dfw_ref[...] = jnp.zeros_like(dfw_ref)

        def ext(cols):
            return jnp.concatenate([jnp.where(first, 0.0, up_ref[:, cols]), um_ref[:, cols],
                                    jnp.where(last, 0.0, un_ref[:, cols])], axis=0)

        dy_ext = _ext_rows(yp_ref, ym_ref, yn_ref, i, tm, T, dtype=BF16)
        for c in range(F // tc):
            cols = slice(c * tc, (c + 1) * tc)
            dfe = lax.dot_general(dy_ext, wd_ref[cols, :], NT_DIMS, preferred_element_type=F32)
            dfe = dfe[HALO_BF16 - HALO:HALO_BF16 + tm + HALO]
            a = ext(cols)
            gate = ext(slice(F + c * tc, F + (c + 1) * tc))
            a_prev, a_next = _prev_row(a), _next_row(a)
            ca = a_prev * w_ref[0:1, cols] + a * w_ref[1:2, cols] + a_next * w_ref[2:3, cols] + b_ref[:, cols]
            cdf = 0.5 * (1.0 + lax.erf(ca * _INV_SQRT2))
            gl = ca * cdf
            gp = cdf + ca * (jnp.exp(-0.5 * ca * ca) * _INV_SQRT_2PI)
            dgate = _main(dfe * gl, tm)
            dca = dfe * gate * gp
            da = _main(_next_row(dca) * w_ref[0:1, cols] + dca * w_ref[1:2, cols] + _prev_row(dca) * w_ref[2:3, cols],
                       tm)
            da_ref[:, cols] = da.astype(BF16)
            dg_ref[:, cols] = dgate.astype(BF16)
            csa_ref[:, cols] += jnp.sum(da, 0, keepdims=True)
            csg_ref[:, cols] += jnp.sum(dgate, 0, keepdims=True)
            dca_m = _main(dca, tm)
            dfb_ref[:, cols] += jnp.sum(dca_m, 0, keepdims=True)
            dfw_ref[0:1, cols] += jnp.sum(dca_m * _main(a_prev, tm), 0, keepdims=True)
            dfw_ref[1:2, cols] += jnp.sum(dca_m * _main(a, tm), 0, keepdims=True)
            dfw_ref[2:3, cols] += jnp.sum(dca_m * _main(a_next, tm), 0, keepdims=True)

    uprev, umain, unxt = _halo_specs(tm, F2, T, lambda j: 0, "ij")
    yprev, ymain, ynxt = _halo_specs(tm, D, T, lambda j: 0, "ij", HALO_BF16)
    whole = lambda shape: pl.BlockSpec(shape, lambda i, j: (0, 0))
    tile = pl.BlockSpec((tm, F), lambda i, j: (i, 0))
    return pl.pallas_call(
        body, name=name, grid=(T // tm, 1),
        in_specs=[yprev, ymain, ynxt, whole((F, D)), uprev, umain, unxt, whole((3, F)), whole((1, F))],
        out_specs=[tile, tile, whole((1, F)), whole((1, F)), whole((1, F)), whole((3, F))],
        out_shape=[jax.ShapeDtypeStruct((T, F), BF16), jax.ShapeDtypeStruct((T, F), BF16),
                   jax.ShapeDtypeStruct((1, F), F32), jax.ShapeDtypeStruct((1, F), F32),
                   jax.ShapeDtypeStruct((1, F), F32), jax.ShapeDtypeStruct((3, F), F32)],
        compiler_params=_params(("arbitrary", "arbitrary")),
    )(dy, dy, dy, w_down, up, up, up, fcw, fcb)


def gate_fwd(proj_g, y_a, y_b, name):
    T, D = y_a.shape
    tm = _pick(T, 512, 8)

    def body(g_ref, a_ref, b_ref, o_ref):
        sa = jax.nn.sigmoid(g_ref[:, :D].astype(F32))
        sb = jax.nn.sigmoid(g_ref[:, D:].astype(F32))
        o_ref[...] = (sa * a_ref[...].astype(F32) + sb * b_ref[...].astype(F32)).astype(BF16)

    row = pl.BlockSpec((tm, D), lambda i: (i, 0))
    return pl.pallas_call(
        body, name=name, grid=(T // tm,),
        in_specs=[pl.BlockSpec((tm, 2 * D), lambda i: (i, 0)), row, row],
        out_specs=row,
        out_shape=jax.ShapeDtypeStruct((T, D), BF16),
        compiler_params=_params(("parallel",)),
    )(proj_g, y_a, y_b)


def gate_bwd(dmix, w_o, proj_g, y_a, y_b, name):
    T, D = y_a.shape
    tm = _pick(T, 512, 8)

    def body(dz_ref, w_ref, g_ref, a_ref, b_ref, da_ref, db_ref, dg_ref):
        dzv = lax.dot_general(dz_ref[...], w_ref[...], NT_DIMS, preferred_element_type=F32)
        sa = jax.nn.sigmoid(g_ref[:, :D].astype(F32))
        sb = jax.nn.sigmoid(g_ref[:, D:].astype(F32))
        da_ref[...] = (dzv * sa).astype(BF16)
        db_ref[...] = (dzv * sb).astype(BF16)
        dg_ref[:, :D] = (dzv * a_ref[...].astype(F32) * (sa * (1.0 - sa))).astype(BF16)
        dg_ref[:, D:] = (dzv * b_ref[...].astype(F32) * (sb * (1.0 - sb))).astype(BF16)

    row = pl.BlockSpec((tm, D), lambda i: (i, 0))
    wide = pl.BlockSpec((tm, 2 * D), lambda i: (i, 0))
    return pl.pallas_call(
        body, name=name, grid=(T // tm,),
        in_specs=[pl.BlockSpec((tm, dmix.shape[1]), lambda i: (i, 0)), pl.BlockSpec(w_o.shape, lambda i: (0, 0)),
                  wide, row, row],
        out_specs=[row, row, wide],
        out_shape=[jax.ShapeDtypeStruct((T, D), BF16), jax.ShapeDtypeStruct((T, D), BF16),
                   jax.ShapeDtypeStruct((T, 2 * D), BF16)],
        compiler_params=_params(("parallel",)),
    )(dmix, w_o, proj_g, y_a, y_b)


ATT_WIN = ATT_TQ + 2 * RADIUS
ATT_STEP = 1024
FAR = 1e32


def _att_window(qs, L):
    ks = pl.multiple_of(jnp.clip(qs - RADIUS, 0, L - ATT_WIN), RADIUS)
    return ks, jnp.where(qs == 0, 0, jnp.where(qs == L - ATT_TQ, 2, 1))


def _fill_bias_tables(bias_ref, sl_ref, hp, d):
    col_row = (lax.broadcasted_iota(jnp.int32, (ATT_TQ, ATT_WIN), 1)
               - lax.broadcasted_iota(jnp.int32, (ATT_TQ, ATT_WIN), 0))
    for v in range(3):
        ad = jnp.abs(col_row - v * RADIUS)
        dist = jnp.where(ad <= RADIUS, (ad * d).astype(F32), FAR)
        bias_ref[v, 0:ATT_TQ, :] = sl_ref[hp * 2] * dist
        bias_ref[v, ATT_TQ:2 * ATT_TQ, :] = sl_ref[hp * 2 + 1] * dist


def _head_masks():
    lane = lax.broadcasted_iota(jnp.int32, (1, LANES), 1)
    return [lane < HEAD_DIM, lane >= HEAD_DIM]


def _stack_heads(x, masks):
    zero = jnp.zeros_like(x)
    return jnp.concatenate([jnp.where(masks[0], x, zero), jnp.where(masks[1], x, zero)], axis=0)


def _unstack_heads(x2, masks):
    n = x2.shape[0] // 2
    return jnp.where(masks[0], x2[:n], x2[n:])


def _att_step(L):
    step = min(ATT_STEP, L)
    assert L % step == 0 and step % ATT_TQ == 0 and L >= ATT_WIN
    return step


def _residues_per_step(d, L):
    rps = max(1, min(d, ATT_STEP // L))
    assert d % rps == 0
    return rps


def att_fwd(qkv, group, name):
    d, L, _ = qkv.shape
    step = _att_step(L)
    rps = _residues_per_step(d, L)
    cg = GROUP_W // LANES
    slopes = jnp.asarray(_alibi_slopes()[group])
    scale = HEAD_DIM ** -0.5

    def body(sl_ref, q_ref, k_ref, v_ref, o_ref, l_ref, bias_ref, s_ref, p_ref):
        hp = pl.program_id(1)
        i = pl.program_id(2)

        @pl.when(i == 0)
        def _():
            _fill_bias_tables(bias_ref, sl_ref, hp, d)

        masks = _head_masks()
        per = step // ATT_TQ
        tiles = [(rr, t) for rr in range(rps) for t in range(per)]
        windows = [_att_window(i * step + t * ATT_TQ, L) for t in range(per)]
        for n, (rr, t) in enumerate(tiles):
            rows = slice(t * ATT_TQ, (t + 1) * ATT_TQ)
            ks, table = windows[t]
            q2 = _stack_heads(q_ref[rr, rows, :] * scale, masks)
            kw = k_ref[rr, pl.ds(ks, ATT_WIN), :]
            s_ref[n] = lax.dot_general(q2, kw, NT_DIMS, preferred_element_type=F32) - bias_ref[table]
        for n, (rr, t) in enumerate(tiles):
            rows = slice(t * ATT_TQ, (t + 1) * ATT_TQ)
            s = s_ref[n]
            m = jnp.max(s, -1, keepdims=True)
            p = jnp.exp(s - m)
            den = jnp.sum(p, -1, keepdims=True)
            p_ref[n] = (p / den).astype(BF16)
            l_ref[rr, rows, :] = _unstack_heads(m + jnp.log(den), masks)
        for n, (rr, t) in enumerate(tiles):
            rows = slice(t * ATT_TQ, (t + 1) * ATT_TQ)
            vw = v_ref[rr, pl.ds(windows[t][0], ATT_WIN), :]
            o2 = jnp.dot(p_ref[n], vw, preferred_element_type=F32)
            o_ref[rr, rows, :] = _unstack_heads(o2, masks).astype(ACT)

    n_tiles = rps * step // ATT_TQ
    out_spec = pl.BlockSpec((rps, step, LANES), lambda r, hp, i: (r, i, hp))
    return pl.pallas_call(
        body, name=name, grid=(d // rps, cg, L // step),
        in_specs=[pl.BlockSpec(memory_space=pltpu.SMEM),
                  pl.BlockSpec((rps, step, LANES), lambda r, hp, i: (r, i, hp)),
                  pl.BlockSpec((rps, L, LANES), lambda r, hp, i: (r, 0, cg + hp)),
                  pl.BlockSpec((rps, L, LANES), lambda r, hp, i: (r, 0, 2 * cg + hp))],
        out_specs=[out_spec, out_spec],
        out_shape=[jax.ShapeDtypeStruct((d, L, GROUP_W), ACT), jax.ShapeDtypeStruct((d, L, GROUP_W), F32)],
        scratch_shapes=[pltpu.VMEM((3, 2 * ATT_TQ, ATT_WIN), F32),
                        pltpu.VMEM((n_tiles, 2 * ATT_TQ, ATT_WIN), F32),
                        pltpu.VMEM((n_tiles, 2 * ATT_TQ, ATT_WIN), BF16)],
        compiler_params=_params(("arbitrary", "arbitrary", "arbitrary")),
    )(slopes, qkv, qkv, qkv)


def att_bwd(qkv, do, lse, dmat, group, name, after=None):
    d, L, _ = qkv.shape
    step = _att_step(L)
    rps = _residues_per_step(d, L)
    nq = L // step
    cg = GROUP_W // LANES
    slopes = jnp.asarray(_alibi_slopes()[group])
    scale = HEAD_DIM ** -0.5

    def body(sl_ref, q_ref, k_ref, v_ref, do_ref, l_ref, dm_ref, *rest):
        dq_ref, dk_ref, dv_ref, dk_acc, dv_acc, bias_ref, s_ref, dp_ref, p_ref, ds_ref = rest[len(rest) - 10:]
        hp = pl.program_id(1)
        i = pl.program_id(2)

        @pl.when(i == 0)
        def _():
            dk_acc[...] = jnp.zeros_like(dk_acc)
            dv_acc[...] = jnp.zeros_like(dv_acc)
            _fill_bias_tables(bias_ref, sl_ref, hp, d)

        masks = _head_masks()

        def head_cols(x):
            return jnp.concatenate([jnp.max(jnp.where(hm, x, -jnp.inf), -1, keepdims=True) for hm in masks], axis=0)

        per = step // ATT_TQ
        tiles = [(rr, t) for rr in range(rps) for t in range(per)]
        windows = [_att_window(i * step + t * ATT_TQ, L) for t in range(per)]

        def stacked(ref, rr, t, factor=None):
            x = ref[rr, t * ATT_TQ:(t + 1) * ATT_TQ, :]
            return _stack_heads(x if factor is None else x * factor, masks)

        for n, (rr, t) in enumerate(tiles):
            ks, table = windows[t]
            q2 = stacked(q_ref, rr, t, scale)
            s_ref[n] = lax.dot_general(q2, k_ref[rr, pl.ds(ks, ATT_WIN), :], NT_DIMS,
                                       preferred_element_type=F32) - bias_ref[table]
            dp_ref[n] = lax.dot_general(stacked(do_ref, rr, t), v_ref[rr, pl.ds(ks, ATT_WIN), :], NT_DIMS,
                                        preferred_element_type=F32)
        for n, (rr, t) in enumerate(tiles):
            rows = slice(t * ATT_TQ, (t + 1) * ATT_TQ)
            p = jnp.exp(s_ref[n] - head_cols(l_ref[rr, rows, :]))
            p_ref[n] = p.astype(BF16)
            ds_ref[n] = (p * (dp_ref[n] - head_cols(dm_ref[rr, rows, :]))).astype(BF16)
        for n, (rr, t) in enumerate(tiles):
            rows = slice(t * ATT_TQ, (t + 1) * ATT_TQ)
            ks = windows[t][0]
            ds = ds_ref[n]
            dq2 = jnp.dot(ds, k_ref[rr, pl.ds(ks, ATT_WIN), :], preferred_element_type=F32)
            dq_ref[rr, rows, :] = (_unstack_heads(dq2, masks) * scale).astype(BF16)
            dk_acc[rr, pl.ds(ks, ATT_WIN), :] += lax.dot_general(ds, stacked(q_ref, rr, t, scale), TN_DIMS,
                                                                 preferred_element_type=F32)
            dv_acc[rr, pl.ds(ks, ATT_WIN), :] += lax.dot_general(p_ref[n], stacked(do_ref, rr, t), TN_DIMS,
                                                                 preferred_element_type=F32)

        @pl.when(i == nq - 1)
        def _():
            dk_ref[...] = dk_acc[...].astype(BF16)
            dv_ref[...] = dv_acc[...].astype(BF16)

    tile = pl.BlockSpec((rps, step, LANES), lambda r, hp, i: (r, i, hp))
    whole = pl.BlockSpec((rps, L, LANES), lambda r, hp, i: (r, 0, hp))
    return pl.pallas_call(
        body, name=name, grid=(d // rps, cg, nq),
        in_specs=[pl.BlockSpec(memory_space=pltpu.SMEM), tile,
                  pl.BlockSpec((rps, L, LANES), lambda r, hp, i: (r, 0, cg + hp)),
                  pl.BlockSpec((rps, L, LANES), lambda r, hp, i: (r, 0, 2 * cg + hp)),
                  tile, tile, tile] + ([] if after is None else [pl.BlockSpec((8, LANES), lambda r, hp, i: (0, 0))]),
        out_specs=[tile, whole, whole],
        out_shape=[jax.ShapeDtypeStruct((d, L, GROUP_W), BF16)] * 3,
        scratch_shapes=[pltpu.VMEM((rps, L, LANES), F32), pltpu.VMEM((rps, L, LANES), F32),
                        pltpu.VMEM((3, 2 * ATT_TQ, ATT_WIN), F32)]
        + [pltpu.VMEM((rps * step // ATT_TQ, 2 * ATT_TQ, ATT_WIN), dt) for dt in (F32, F32, BF16, BF16)],
        compiler_params=_params(("arbitrary", "arbitrary", "arbitrary")),
    )(slopes, qkv, qkv, qkv, do, lse, dmat, *([] if after is None else [after]))


def _group_weights(ls):
    m = jnp.maximum(jnp.maximum(ls[0], ls[1]), ls[2])
    es = [jnp.exp(l - m) for l in ls]
    tot = es[0] + es[1] + es[2]
    return [e / tot for e in es]


def combine_fwd(outs, lses, name):
    T = outs[0].shape[0] * outs[0].shape[1]
    tm = _pick(T, 512, 8)
    n_scr = 2 * (len(DILATIONS) - 1)

    def body(*refs):
        o_refs, l_refs, c_ref, scr = refs[:3], refs[3:6], refs[6], refs[7:]
        o = [_load_natural(o_refs[g], d, scr[g - 1] if g else None) for g, d in enumerate(DILATIONS)]
        l = [_load_natural(l_refs[g], d, scr[g + 1] if g else None) for g, d in enumerate(DILATIONS)]
        w = _group_weights(l)
        c_ref[...] = (w[0] * o[0] + w[1] * o[1] + w[2] * o[2]).astype(BF16)

    specs = [_residue_spec(tm, d, GROUP_W) for d in DILATIONS]
    return pl.pallas_call(
        body, name=name, grid=(T // tm,),
        in_specs=specs + specs, out_specs=pl.BlockSpec((tm, GROUP_W), lambda i: (i, 0)),
        out_shape=jax.ShapeDtypeStruct((T, GROUP_W), BF16),
        scratch_shapes=[_residue_scratch(tm, GROUP_W)] * n_scr,
        compiler_params=_params(("parallel",)),
    )(*outs, *lses)


def combine_bwd(dcomb, outs, lses, name):
    T = dcomb.shape[0]
    tm = _pick(T, 256, 8)
    head = np.arange(GROUP_W) // HEAD_DIM
    seg = jnp.asarray((head[:, None] == head[None, :]).astype(np.float32)).astype(BF16)
    ng = len(DILATIONS)
    n_scr = 4 * (ng - 1)

    def body(*refs):
        dc_ref, o_refs, l_refs, e_ref = refs[0], refs[1:1 + ng], refs[1 + ng:1 + 2 * ng], refs[1 + 2 * ng]
        do_refs, dm_refs = refs[2 + 2 * ng:2 + 3 * ng], refs[2 + 3 * ng:2 + 4 * ng]
        scr = refs[2 + 4 * ng:]
        o = [_load_natural(o_refs[g], d, scr[4 * (g - 1)] if g else None) for g, d in enumerate(DILATIONS)]
        l = [_load_natural(l_refs[g], d, scr[4 * (g - 1) + 1] if g else None) for g, d in enumerate(DILATIONS)]
        w = _group_weights(l)
        dc = dc_ref[...].astype(F32)
        e = e_ref[...]
        prod = dc * (w[0] * o[0] + w[1] * o[1] + w[2] * o[2])
        tot = jnp.zeros_like(dc)
        for _ in range(3):
            part = prod.astype(BF16)
            tot = tot + jnp.dot(part, e, preferred_element_type=F32)
            prod = prod - part.astype(F32)
        for g, d in enumerate(DILATIONS):
            _store_by_residue(w[g] * dc, do_refs[g], d, scr[4 * (g - 1) + 2] if g else None)
            _store_by_residue(w[g] * tot, dm_refs[g], d, scr[4 * (g - 1) + 3] if g else None)

    specs = [_residue_spec(tm, d, GROUP_W) for d in DILATIONS]
    res = pl.pallas_call(
        body, name=name, grid=(T // tm,),
        in_specs=[pl.BlockSpec((tm, GROUP_W), lambda i: (i, 0))] + specs + specs
        + [pl.BlockSpec((GROUP_W, GROUP_W), lambda i: (0, 0))],
        out_specs=specs + specs,
        out_shape=[jax.ShapeDtypeStruct(o.shape, BF16) for o in outs] + [jax.ShapeDtypeStruct(o.shape, F32) for o in outs],
        scratch_shapes=[_residue_scratch(tm, GROUP_W)] * n_scr,
        compiler_params=_params(("parallel",)),
    )(dcomb, *outs, *lses, seg)
    return res[:ng], res[ng:]


def _position():
    return lax.axis_index("x"), lax.axis_index("y"), lax.axis_index("c")


def _other_chips(x, y):
    return [(1 - x, y), (x, 1 - y), (1 - x, 1 - y)]


def _remote(src, dst, send_sems, recv_sems, k, to):
    return pltpu.make_async_remote_copy(src_ref=src, dst_ref=dst, send_sem=send_sems.at[k], recv_sem=recv_sems.at[k],
                                        device_id=to, device_id_type=MESH)


GATHER_SEMS = 10
SPLIT_ROWS = 32


def _gather_plan(ins, outs, send_sems, recv_sems, local_sems):
    x, y, c = _position()
    sibling = (x, y, 1 - c)
    nbr_x, nbr_y, diag = (1 - x, y, c), (x, 1 - y, c), (1 - x, 1 - y, c)
    local, begin, stages, last = [], [], [], []
    for a in range(len(ins)):
        k0 = GATHER_SEMS * a
        rows = ins[a].shape[0]
        half = rows // 2

        def block(dev):
            return outs[a].at[4 * dev[0] + 2 * dev[1] + dev[2]]

        def part(ref, h):
            return ref.at[pl.ds(h * half, half)]

        def copy(k, src, dst, to):
            return _remote(src, dst, send_sems, recv_sems, k0 + k, to)

        me = (x, y, c)
        local.append(pltpu.make_async_copy(ins[a], block(me), local_sems.at[a]))
        begin.append(copy(0, ins[a], block(me), sibling))
        pass_on = [copy(7 + j, block(dev), block(dev), sibling) for j, dev in enumerate((nbr_x, nbr_y, diag))]
        if rows >= SPLIT_ROWS and rows % SPLIT_ROWS == 0:
            for h in range(2):
                begin.append(copy(1 + h, part(ins[a], h), part(block(me), h), nbr_x))
                begin.append(copy(3 + h, part(ins[a], h), part(block(me), h), nbr_y))
            from_x = [copy(1 + h, part(block(nbr_x), h), part(block(nbr_x), h), sibling) for h in range(2)]
            from_y = [copy(3 + h, part(block(nbr_y), h), part(block(nbr_y), h), sibling) for h in range(2)]
            fwd_0 = copy(5, part(block(nbr_x), 0), part(block(nbr_x), 0), nbr_y)
            fwd_1 = copy(6, part(block(nbr_y), 1), part(block(nbr_y), 1), nbr_x)
            got_0 = copy(5, part(block(diag), 0), part(block(diag), 0), sibling)
            got_1 = copy(6, part(block(diag), 1), part(block(diag), 1), sibling)
            stages.append(([from_x[0]], [fwd_0]))
            stages.append(([from_y[1]], [fwd_1]))
            stages.append(([from_x[1]], [pass_on[0]]))
            stages.append(([from_y[0]], [pass_on[1]]))
            stages.append(([got_0, got_1], [pass_on[2]]))
        else:
            for j, dev in enumerate((nbr_x, nbr_y, diag)):
                begin.append(copy(1 + 2 * j, ins[a], block(me), dev))
                stages.append(([copy(1 + 2 * j, block(dev), block(dev), sibling)], [pass_on[j]]))
        other = (x, y, 1 - c)
        last.append(copy(0, block(other), block(other), sibling))
        for j, dev in enumerate((nbr_x, nbr_y, diag)):
            theirs = (dev[0], dev[1], 1 - c)
            last.append(copy(7 + j, block(theirs), block(theirs), sibling))
    return local, begin, stages, last


def _gather_begin(ins, outs, send_sems, recv_sems, local_sems):
    local, begin, _, _ = _gather_plan(ins, outs, send_sems, recv_sems, local_sems)
    for cp in local + begin:
        cp.start()


def _gather_finish(ins, outs, send_sems, recv_sems, local_sems):
    local, begin, stages, last = _gather_plan(ins, outs, send_sems, recv_sems, local_sems)
    started = []
    for arrivals, onward in stages:
        for cp in arrivals:
            cp.wait_recv()
        for cp in onward:
            cp.start()
            started.append(cp)
    for cp in last:
        cp.wait_recv()
    for cp in begin + started:
        cp.wait_send()
    for cp in local:
        cp.wait()


def _gather_scratch(n):
    return [pltpu.SemaphoreType.DMA((GATHER_SEMS * n,)), pltpu.SemaphoreType.DMA((GATHER_SEMS * n,)),
            pltpu.SemaphoreType.DMA((n,))]


_HBM = pl.BlockSpec(memory_space=pltpu.HBM)
_SEM = pl.BlockSpec(memory_space=pltpu.SEMAPHORE)
_DATAFLOW = pltpu.SideEffectType.DATAFLOW_SIDE_EFFECTING


def _to_all_plan(srcs, lands, send_sems, recv_sems):
    x, y, c = _position()
    me = 4 * x + 2 * y + c
    copies = []
    for a in range(len(srcs)):
        for k in range(1, N_DEV):
            fx, fy, fc = (k >> 2) & 1, (k >> 1) & 1, k & 1
            to = (1 - x if fx else x, 1 - y if fy else y, 1 - c if fc else c)
            copies.append(_remote(srcs[a], lands[a].at[me], send_sems, recv_sems, (N_DEV - 1) * a + k - 1, to))
    return copies


def _to_sibling_plan(srcs, lands, send_sems, recv_sems):
    x, y, c = _position()
    copies = []
    for a in range(len(srcs)):
        for q in range(4):
            copies.append(_remote(srcs[a].at[2 * q + (1 - c)], lands[a].at[q], send_sems, recv_sems, 4 * a + q,
                                  (x, y, 1 - c)))
    return copies


def _to_chips_plan(srcs, lands, send_sems, recv_sems):
    x, y, c = _position()
    copies = []
    for a in range(len(srcs)):
        for j, (cx, cy) in enumerate(_other_chips(x, y)):
            copies.append(_remote(srcs[a].at[2 * cx + cy], lands[a].at[j], send_sems, recv_sems, 3 * a + j, (cx, cy, c)))
    return copies


def copies_start(srcs, land_shapes, plan, per_array, name):
    n = len(srcs)
    n_sem = per_array * n
    lands = [lax.empty(s.shape, s.dtype) for s in land_shapes]

    def body(*refs):
        src_refs, land_refs = refs[:n], refs[n:2 * n]
        send_sems, recv_sems = refs[2 * n], refs[2 * n + 1]
        token = refs[-1]
        for cp in plan(src_refs, land_refs, send_sems, recv_sems):
            cp.start()
        token[...] = jnp.zeros_like(token)

    out = pl.pallas_call(
        body, name=name,
        out_shape=(pltpu.SemaphoreType.DMA((n_sem,)), pltpu.SemaphoreType.DMA((n_sem,)))
        + tuple(pltpu.HBM(s.shape, s.dtype) for s in srcs)
        + tuple(pltpu.HBM(s.shape, s.dtype) for s in land_shapes)
        + (jax.ShapeDtypeStruct((8, LANES), F32),),
        in_specs=[_HBM] * (2 * n),
        out_specs=(_SEM, _SEM) + (_HBM,) * (2 * n) + (pl.BlockSpec(memory_space=pltpu.VMEM),),
        input_output_aliases={i: 2 + i for i in range(2 * n)},
        compiler_params=pltpu.CompilerParams(has_side_effects=_DATAFLOW),
    )(*[pltpu.with_memory_space_constraint(s, pltpu.HBM) for s in srcs],
      *[pltpu.with_memory_space_constraint(l, pltpu.HBM) for l in lands])
    return out[:-1], out[-1]


def copies_wait(handles, plan, after, name):
    send_sems, recv_sems = handles[0], handles[1]
    n = (len(handles) - 2) // 2
    thru = handles[2:]

    def body(*refs):
        src_refs, land_refs = refs[:n], refs[n:2 * n]
        send_sems, recv_sems = refs[2 * n], refs[2 * n + 1]
        copies = plan(src_refs, land_refs, send_sems, recv_sems)
        for cp in copies:
            cp.wait_recv()
        for cp in copies:
            cp.wait_send()

    out = pl.pallas_call(
        body, name=name,
        out_shape=tuple(pltpu.HBM(t.shape, t.dtype) for t in thru),
        in_specs=[_HBM] * (2 * n) + [_SEM, _SEM, pl.BlockSpec(memory_space=pl.ANY)],
        out_specs=(_HBM,) * (2 * n),
        input_output_aliases={i: i for i in range(2 * n)},
        compiler_params=pltpu.CompilerParams(has_side_effects=_DATAFLOW),
    )(*thru, send_sems, recv_sems, after)
    return out[:n], out[n:]


def all_sum_small(vec, name):
    R = vec.shape[0]

    def body(v_ref, tot_ref, all_ref, send_sems, recv_sems):
        x, y, c = _position()
        me = 4 * x + 2 * y + c
        all_ref[me] = v_ref[...]
        copies = []
        for k in range(1, N_DEV):
            fx, fy, fc = (k >> 2) & 1, (k >> 1) & 1, k & 1
            to = (1 - x if fx else x, 1 - y if fy else y, 1 - c if fc else c)
            cp = _remote(v_ref, all_ref.at[me], send_sems, recv_sems, k - 1, to)
            cp.start()
            copies.append(cp)
        for cp in copies:
            cp.wait_recv()
        for cp in copies:
            cp.wait_send()
        tot = all_ref[0]
        for j in range(1, N_DEV):
            tot = tot + all_ref[j]
        tot_ref[...] = tot

    vmem = pl.BlockSpec(memory_space=pltpu.VMEM)
    return pl.pallas_call(
        body, name=name,
        in_specs=[vmem], out_specs=vmem,
        out_shape=jax.ShapeDtypeStruct((R, LANES), F32),
        scratch_shapes=[pltpu.VMEM((N_DEV, R, LANES), F32),
                        pltpu.SemaphoreType.DMA((N_DEV - 1,)), pltpu.SemaphoreType.DMA((N_DEV - 1,))],
        compiler_params=pltpu.CompilerParams(vmem_limit_bytes=VMEM_LIMIT),
    )(vec)


def pair_add(parts, theirs, place, name):
    _, R, C = theirs.shape
    tr = _pick(R, 1024, 8)

    def body(place_ref, a_ref, b_ref, o_ref):
        o_ref[...] = (a_ref[...].astype(F32) + b_ref[...].astype(F32)).astype(BF16)

    blk = pl.BlockSpec((None, tr, C), lambda q, i, place_ref: (q, i, 0))
    return pl.pallas_call(
        body, name=name,
        grid_spec=pltpu.PrefetchScalarGridSpec(
            num_scalar_prefetch=1, grid=(4, R // tr),
            in_specs=[pl.BlockSpec((None, tr, C), lambda q, i, place_ref: (2 * q + place_ref[2], i, 0)), blk],
            out_specs=blk),
        out_shape=jax.ShapeDtypeStruct(theirs.shape, BF16),
        compiler_params=_params(("parallel", "parallel")),
    )(place, parts, theirs)


def _adamw_math(w, g, m, v):
    m = ADAM_B1 * m + (1.0 - ADAM_B1) * g
    v = ADAM_B2 * v + (1.0 - ADAM_B2) * jnp.square(g)
    m_hat = m / (1.0 - ADAM_B1 ** ADAM_STEP)
    v_hat = v / (1.0 - ADAM_B2 ** ADAM_STEP)
    delta = -ADAM_LR * (m_hat / (jnp.sqrt(v_hat) + ADAM_EPS) + ADAM_WD * w)
    return delta, m, v


def adamw_sharded(w, m, v, parts, sib, others, place, name):
    R, C = w.shape
    tr = _pick(R, 256, 8)

    def body(place_ref, w_ref, m_ref, v_ref, a_ref, b_ref, o_ref, g_ref, d_ref, nm_ref, nv_ref):
        g = a_ref[...].astype(F32) + b_ref[...].astype(F32)
        for j in range(3):
            g = g + o_ref[j].astype(F32)
        delta, nm, nv = _adamw_math(w_ref[...], g, m_ref[...], v_ref[...])
        g_ref[...] = g
        d_ref[...] = delta
        nm_ref[...] = nm
        nv_ref[...] = nv

    row = pl.BlockSpec((tr, C), lambda i, place_ref: (i, 0))
    return pl.pallas_call(
        body, name=name,
        grid_spec=pltpu.PrefetchScalarGridSpec(
            num_scalar_prefetch=1, grid=(R // tr,),
            in_specs=[row] * 3 + [pl.BlockSpec((None, tr, C), lambda i, place_ref: (place_ref[0], i, 0)),
                                  pl.BlockSpec((None, tr, C), lambda i, place_ref: (place_ref[1], i, 0)),
                                  pl.BlockSpec((3, tr, C), lambda i, place_ref: (0, i, 0))],
            out_specs=[row] * 4),
        out_shape=[jax.ShapeDtypeStruct((R, C), F32)] * 4,
        compiler_params=_params(("parallel",)),
    )(place, w, m, v, parts, sib, others)


def adamw_packed(w, g, m, v, name):
    R = w.shape[0]

    def body(w_ref, g_ref, m_ref, v_ref, d_ref, nm_ref, nv_ref):
        delta, nm, nv = _adamw_math(w_ref[...], g_ref[...], m_ref[...], v_ref[...])
        d_ref[...] = delta
        nm_ref[...] = nm
        nv_ref[...] = nv

    full = pl.BlockSpec((R, LANES), lambda i: (0, 0))
    return pl.pallas_call(
        body, name=name, grid=(1,),
        in_specs=[full] * 4, out_specs=[full] * 3,
        out_shape=[jax.ShapeDtypeStruct((R, LANES), F32)] * 3,
        compiler_params=_params(("arbitrary",)),
    )(w, g, m, v)


def _pack(arrays):
    flat = []
    sizes = []
    for a in arrays:
        f = a.reshape(-1).astype(F32)
        pad = (-f.shape[0]) % LANES
        if pad:
            f = jnp.concatenate([f, jnp.zeros((pad,), F32)])
        flat.append(f)
        sizes.append(f.shape[0])
    rows = sum(sizes) // LANES
    pad_rows = (-rows) % 8
    if pad_rows:
        flat.append(jnp.zeros((pad_rows * LANES,), F32))
    return jnp.concatenate(flat).reshape(-1, LANES), sizes


def _unpack(packed, sizes, shapes):
    flat = packed.reshape(-1)
    out = []
    off = 0
    for size, shape in zip(sizes, shapes):
        n = int(np.prod(shape))
        out.append(flat[off:off + n].reshape(shape))
        off += size
    return out


def _to_blocks(full, axis):
    if axis == 0:
        return full.reshape(N_DEV, full.shape[0] // N_DEV, full.shape[1])
    r, n = full.shape
    return full.reshape(r, N_DEV, n // N_DEV).transpose(1, 0, 2)


def _from_blocks(blocks, axis):
    if axis == 0:
        return blocks.reshape(blocks.shape[0] * blocks.shape[1], blocks.shape[2])
    return blocks.transpose(1, 0, 2).reshape(blocks.shape[1], blocks.shape[0] * blocks.shape[2])


def kernel(x, ln0_g, ln0_b, w_in, b_in, conv_w, w_a, w_b, w_o, b_o, ln1_g, ln1_b, w_up, b_up, ffn_conv_w, ffn_conv_b, w_down, b_down, ln2_g, ln2_b, loss_target, m_ln0_g, m_ln0_b, m_w_in, m_b_in, m_conv_w, m_w_a, m_w_b, m_w_o, m_b_o, m_ln1_g, m_ln1_b, m_w_up, m_b_up, m_ffn_conv_w, m_ffn_conv_b, m_w_down, m_b_down, m_ln2_g, m_ln2_b, v_ln0_g, v_ln0_b, v_w_in, v_b_in, v_conv_w, v_w_a, v_w_b, v_w_o, v_b_o, v_ln1_g, v_ln1_b, v_w_up, v_b_up, v_ffn_conv_w, v_ffn_conv_b, v_w_down, v_b_down, v_ln2_g, v_ln2_b):
    T, D = x.shape[1], x.shape[2]
    F = ffn_conv_b.shape[-1]
    xs = x.reshape(T, D)
    tgt = loss_target.reshape(T, D)
    dev = 4 * lax.axis_index("x") + 2 * lax.axis_index("y") + lax.axis_index("c")
    chip = 2 * lax.axis_index("x") + lax.axis_index("y")
    core = lax.axis_index("c")
    place = jnp.stack([dev, chip, core]).astype(jnp.int32)

    big = dict(w_in=(w_in[0], 1), w_a=(w_a[0], 0), w_b=(w_b[0], 1), w_o=(w_o[0], 0), w_up=(w_up[0], 1),
               w_down=(w_down[0], 0))
    names = list(big)
    shards = {k: big[k][0].astype(BF16) for k in names}
    ln0g, ln0b = ln0_g.reshape(1, D), ln0_b.reshape(1, D)
    h0, h0b, *rest = ln_fwd(xs, None, ln0g, ln0b, "ln0_fwd_gather_w_in", dilations=DILATIONS[1:],
                            gather=[shards["w_in"], conv_w[0], ffn_conv_w[0]])
    h0_res = [h0b] + [h.reshape(T, D) for h in rest[:2]]
    g_in, g_conv, g_fcw = rest[2:]
    full = {"w_in": _from_blocks(g_in, 1)}
    conv_full = _from_blocks(g_conv, 1)
    fcw_full = _from_blocks(g_fcw, 1)
    late_groups = (("w_a", "w_b", "w_o"), ("w_up", "w_down"))
    late_handles = []
    token = conv_full[:1, :1] * 0.0
    for n, keys in enumerate(late_groups):
        srcs = [shards[k] + token[0, 0].astype(BF16) for k in keys]
        handles, token = copies_start(srcs, [jax.ShapeDtypeStruct((N_DEV,) + s.shape, BF16) for s in srcs],
                                      _to_all_plan, N_DEV - 1, f"gather_late_{n}_start")
        late_handles.append(handles)

    def late_weights(n, after):
        _, lands = copies_wait(late_handles[n], _to_all_plan, after, f"gather_late_{n}_wait")
        for k, land in zip(late_groups[n], lands):
            full[k] = _from_blocks(lax.dynamic_update_index_in_dim(land, shards[k], dev, 0), big[k][1])

    o_q = 3 * D
    o_g = o_q + 3 * QKV_W
    w_pa, w_qkv, w_pg = full["w_in"][:, :o_q], full["w_in"][:, o_q:o_g], full["w_in"][:, o_g:]
    b_pa, b_qkv, b_pg = b_in[:, :o_q], b_in[:, o_q:o_g], b_in[:, o_g:]

    proj_a = mm_nn(h0b, w_pa, b_pa, ACT, "proj_conv", after=token)
    proj_g = mm_nn(h0b, w_pg, b_pg, ACT, "proj_gates")
    zero_d = jnp.zeros((1, D), F32)
    s_a = conv_a_fwd(proj_a, conv_full, "conv_a_fwd")
    late_weights(0, s_a)
    y_a = mm_nn(s_a, full["w_a"], zero_d, ACT, "branch_a_out")

    def group_cols(m, g):
        return jnp.concatenate([m[:, s * QKV_W + g * GROUP_W:s * QKV_W + (g + 1) * GROUP_W] for s in range(3)], 1)

    w_grp = [group_cols(w_qkv, g) for g in range(3)]
    qkvs, outs, lses = [], [], []
    for g, d in enumerate(DILATIONS):
        qkv = mm_nn(h0_res[g], w_grp[g], group_cols(b_qkv, g), BF16, f"proj_qkv_{g}").reshape(d, T // d, 3 * GROUP_W)
        o, l = att_fwd(qkv, g, f"att_fwd_{g}")
        qkvs.append(qkv)
        outs.append(o)
        lses.append(l)
    comb = combine_fwd(outs, lses, "combine_fwd")
    y_b = mm_nn(comb, full["w_b"], zero_d, ACT, "branch_b_out")
    z = gate_fwd(proj_g, y_a, y_b, "gate_fwd")
    h1, h1b, mix = ln_fwd(h0, ("nn", z, full["w_o"], b_o), ln1_g, ln1_b, "mix_out_ln1_fwd")
    late_weights(1, h1b)
    up, f_act = ffn_up_conv_f(h1b, full["w_up"], b_up, fcw_full, ffn_conv_b, "ffn_up_conv_f")

    dr2, dr2b, d_ln2_g, d_ln2_b, d_b_down, loss_part = ln_bwd(
        h1, ("nn", f_act, full["w_down"], b_down), ln2_g, ln2_b, None, None, tgt, "ffn_down_ln2_loss_bwd")
    dw_down, _ = mm_tn(f_act, dr2b, "dw_down")
    d_a, d_gate, cs_a, cs_gate, d_fcb, d_fcw = conv_f_bwd(dr2b, full["w_down"], up, fcw_full, ffn_conv_b,
                                                          "d_ffn_act_conv_f_bwd")
    dw_up_a, _ = mm_tn(h1b, d_a, "dw_up_a")
    dw_up_g, _ = mm_tn(h1b, d_gate, "dw_up_gate")
    dr1, dr1b, d_ln1_g, d_ln1_b, d_b_o, _ = ln_bwd(h0, mix, ln1_g, ln1_b, dr2, ("nt", [d_a, d_gate], full["w_up"]), None,
                                                   "d_h1_ln1_bwd")
    dw_o, _ = mm_tn(z, dr1b, "dw_o")
    dy_a, dy_b, dproj_g = gate_bwd(dr1b, full["w_o"], proj_g, y_a, y_b, "d_z_gate_bwd")
    dw_a, _ = mm_tn(s_a, dy_a, "dw_a")
    dproj_a, d_conv = conv_a_bwd(dy_a, full["w_a"], proj_a, conv_full, "d_s_a_conv_a_bwd")
    dw_b, _ = mm_tn(comb, dy_b, "dw_b")

    rs_mine, rs_sib, rs_handles = {}, {}, {}

    sib_handles = {}

    def to_sibling_start(keys, grads, tag):
        parts = [_to_blocks(grads[k], big[k][1]) for k in keys]
        handles, tok = copies_start(parts, [jax.ShapeDtypeStruct((4,) + p.shape[1:], BF16) for p in parts],
                                    _to_sibling_plan, 4, f"grads_to_sibling_{tag}_start")
        sib_handles[tag] = (keys, handles)
        return tok

    def to_chips_start(tag, after):
        keys, handles = sib_handles[tag]
        parts, from_sib = copies_wait(handles, _to_sibling_plan, after, f"grads_to_sibling_{tag}_wait")
        sums = [pair_add(a, b, place, f"chip_sum_{k}") for k, a, b in zip(keys, parts, from_sib)]
        handles, tok = copies_start(sums, [jax.ShapeDtypeStruct((3,) + s.shape[1:], BF16) for s in sums],
                                    _to_chips_plan, 3, f"grads_to_chips_{tag}_start")
        for k, a, b in zip(keys, parts, from_sib):
            rs_mine[k], rs_sib[k] = a, b
        rs_handles[tag] = (keys, handles)
        return tok

    tok_a = to_sibling_start(("w_a", "w_b", "w_o", "w_up", "w_down"),
                             dict(w_a=dw_a, w_b=dw_b, w_o=dw_o, w_up=jnp.concatenate([dw_up_a, dw_up_g], 1),
                                  w_down=dw_down), "a")
    dcomb = mm_nt(dy_b, full["w_b"], None, "d_comb", after=tok_a, out_dtype=ACT)
    dos, dms = combine_bwd(dcomb, outs, lses, "combine_bwd")
    tok_a = to_chips_start("a", dms[0])
    dw_grp, cs_grp, dqkvs = [], [], []
    for g, d in enumerate(DILATIONS):
        dq, dk, dv = att_bwd(qkvs[g], dos[g], lses[g], dms[g], g, f"att_bwd_{g}", after=tok_a if g == 0 else None)
        dqkv = [t.reshape(T, GROUP_W) for t in (dq, dk, dv)]
        dwg, csg = mm_tn(h0_res[g], dqkv, f"dw_in_qkv_{g}")
        dqkvs.append(dqkv)
        dw_grp.append(dwg)
        cs_grp.append(csg)
    dw_pa, cs_pa = mm_tn(h0b, dproj_a, "dw_in_conv")
    dw_pg, cs_pg = mm_tn(h0b, dproj_g, "dw_in_gates")

    def ungroup(parts):
        return jnp.concatenate([p[:, s * GROUP_W:(s + 1) * GROUP_W] for s in range(3) for p in parts], 1)

    db_in_parts = [cs_pa, ungroup(cs_grp), cs_pg]
    tok_b = to_sibling_start(("w_in",), dict(w_in=jnp.concatenate([dw_pa, ungroup(dw_grp), dw_pg], 1)), "b")
    dh0 = mm_nt(dproj_a, w_pa, None, "d_h0_conv", after=tok_b)
    tok_b = to_chips_start("b", dh0)
    dh0 = mm_nt(dproj_g, w_pg, dh0, "d_h0_gates", after=tok_b)
    dh0_res = [(mm_nt(dqkvs[g], w_grp[g], None, f"d_h0_qkv_{g}").reshape(d, T // d, D), d)
               for g, d in enumerate(DILATIONS) if g > 0]
    dx, _, d_ln0_g, d_ln0_b, _, _ = ln_bwd(xs, None, ln0g, ln0b, dr1, ("nt", dqkvs[0], w_grp[0]), None, "d_h0_ln0_bwd",
                                           by_residue=[(dh0.reshape(1, T, D), 1)] + dh0_res)

    small = [d_ln0_g, d_ln0_b, jnp.concatenate(db_in_parts, 1), d_conv, d_b_o, d_ln1_g, d_ln1_b,
             jnp.concatenate([cs_a, cs_gate], 1), d_fcw, d_fcb, d_b_down, d_ln2_g, d_ln2_b, loss_part]
    packed, sizes = _pack(small)
    total = all_sum_small(packed, "sum_small")
    (g_ln0_g, g_ln0_b, g_b_in, g_conv_full, g_b_o, g_ln1_g, g_ln1_b, g_b_up, g_fcw_full, g_fcb, g_b_down, g_ln2_g,
     g_ln2_b, loss) = _unpack(total, sizes, [a.shape for a in small])
    cw = conv_w.shape[-1]
    fw = ffn_conv_w.shape[-1]
    g_conv = lax.dynamic_slice_in_dim(g_conv_full, dev * cw, cw, 1)
    g_fcw = lax.dynamic_slice_in_dim(g_fcw_full, dev * fw, fw, 1)

    from_chips = {}
    for tag, (keys, handles) in rs_handles.items():
        _, lands = copies_wait(handles, _to_chips_plan, total, f"grads_to_chips_{tag}_wait")
        from_chips.update(zip(keys, lands))

    moments = dict(w_in=(m_w_in, v_w_in), w_a=(m_w_a, v_w_a), w_b=(m_w_b, v_w_b), w_o=(m_w_o, v_w_o),
                   w_up=(m_w_up, v_w_up), w_down=(m_w_down, v_w_down))
    res_big = {}
    for k in names:
        res_big[k] = adamw_sharded(big[k][0], moments[k][0][0], moments[k][1][0], rs_mine[k], rs_sib[k], from_chips[k],
                                   place, f"adamw_{k}")

    small_names = ["ln0_g", "ln0_b", "b_in", "conv_w", "b_o", "ln1_g", "ln1_b", "b_up", "ffn_conv_w", "ffn_conv_b",
                   "b_down", "ln2_g", "ln2_b"]
    small_w = [ln0_g, ln0_b, b_in, conv_w, b_o, ln1_g, ln1_b, b_up, ffn_conv_w, ffn_conv_b, b_down, ln2_g, ln2_b]
    small_m = [m_ln0_g, m_ln0_b, m_b_in, m_conv_w, m_b_o, m_ln1_g, m_ln1_b, m_b_up, m_ffn_conv_w, m_ffn_conv_b,
               m_b_down, m_ln2_g, m_ln2_b]
    small_v = [v_ln0_g, v_ln0_b, v_b_in, v_conv_w, v_b_o, v_ln1_g, v_ln1_b, v_b_up, v_ffn_conv_w, v_ffn_conv_b,
               v_b_down, v_ln2_g, v_ln2_b]
    small_g = [g_ln0_g, g_ln0_b, g_b_in, g_conv, g_b_o, g_ln1_g, g_ln1_b, g_b_up, g_fcw, g_fcb, g_b_down, g_ln2_g,
               g_ln2_b]
    shapes = [w.shape for w in small_w]
    small_g = [g.reshape(s) for g, s in zip(small_g, shapes)]
    pw, psz = _pack(small_w)
    pg, _ = _pack(small_g)
    pm, _ = _pack(small_m)
    pv, _ = _pack(small_v)
    pd, pnm, pnv = adamw_packed(pw, pg, pm, pv, "adamw_small")
    res_small = {k: (g, d_, m_, v_) for k, g, d_, m_, v_ in zip(
        small_names, small_g, _unpack(pd, psz, shapes), _unpack(pnm, psz, shapes), _unpack(pnv, psz, shapes))}

    order = ["ln0_g", "ln0_b", "w_in", "b_in", "conv_w", "w_a", "w_b", "w_o", "b_o", "ln1_g", "ln1_b", "w_up", "b_up",
             "ffn_conv_w", "ffn_conv_b", "w_down", "b_down", "ln2_g", "ln2_b"]

    def result(k, j):
        if k in res_big:
            return res_big[k][j][None]
        return res_small[k][j]

    out = [loss.reshape(()), dx.reshape(x.shape)]
    for j in range(4):
        out += [result(k, j) for k in order]
    return tuple(out)
```

```python
import math

import numpy as np
import jax
import jax.numpy as jnp
from jax import lax
from jax.experimental import pallas as pl
from jax.experimental.pallas import tpu as pltpu

F32 = jnp.float32
BF16 = jnp.bfloat16
ACT = BF16

N_DEV = 8
LN_EPS = 1e-5
ALPHA = (2.0 * 1) ** 0.25
HEAD_DIM = 64
GROUP_W = 512
QKV_W = 3 * GROUP_W
DILATIONS = (1, 4, 16)
RADIUS = 64
LANES = 128
HALO = 8
HALO_BF16 = 16
ATT_TQ = 128

ADAM_LR = 0.001
ADAM_B1 = 0.9
ADAM_B2 = 0.999
ADAM_EPS = 1e-08
ADAM_WD = 0.01
ADAM_STEP = 10

VMEM_LIMIT = 52 * 1024 * 1024
OUT_TILE_BYTES = 8 * 1024 * 1024
MESH = pl.DeviceIdType.MESH
NT_DIMS = (((1,), (1,)), ((), ()))
TN_DIMS = (((0,), (0,)), ((), ()))


def _pick(n, target, align=LANES):
    if n <= target:
        return n
    best = None
    for t in range(align, target + 1, align):
        if n % t == 0:
            best = t
    assert best is not None, (n, target, align)
    return best


def _params(sems=None):
    return pltpu.CompilerParams(dimension_semantics=sems, vmem_limit_bytes=VMEM_LIMIT)


def _alibi_slopes():
    n = 3 * 8
    return np.exp2(-8.0 * np.arange(1, n + 1, dtype=np.float64) / n).astype(np.float32).reshape(3, 8)


def _ln_stats(r):
    mu = jnp.mean(r, -1, keepdims=True)
    xc = r - mu
    var = jnp.mean(xc * xc, -1, keepdims=True)
    rstd = lax.rsqrt(var + LN_EPS)
    return xc, rstd


def _load_natural(ref, d, scr):
    if d == 1:
        return ref[0].astype(F32)
    n, C = ref.shape[1], ref.shape[2]
    for c in range(C // LANES):
        for r in range(d):
            scr[c, pl.ds(r, n, stride=d), :] = ref[r, :, c * LANES:(c + 1) * LANES].astype(F32)
    return jnp.concatenate([scr[c] for c in range(C // LANES)], axis=1)


def _store_by_residue(val, ref, d, scr):
    if d == 1:
        ref[0] = val.astype(ref.dtype)
        return
    n, C = ref.shape[1], ref.shape[2]
    for c in range(C // LANES):
        scr[c] = val[:, c * LANES:(c + 1) * LANES]
    for c in range(C // LANES):
        for r in range(d):
            ref[r, :, c * LANES:(c + 1) * LANES] = scr[c, pl.ds(r, n, stride=d), :].astype(ref.dtype)


def _residue_spec(tm, d, C):
    return pl.BlockSpec((d, tm // d, C), lambda i: (0, i, 0))


def _residue_scratch(tm, C):
    return pltpu.VMEM((C // LANES, tm, LANES), F32)


def ln_fwd(a, res, g, b, name, dilations=(), gather=()):
    T, D = a.shape
    res_mm = isinstance(res, tuple)
    tm = _pick(T, 256 if res_mm else 512, 8)
    res_ins = list(res[1:]) if res_mm else ([] if res is None else [res])
    nd = len(dilations)
    ng = len(gather)
    n_in = 1 + len(res_ins) + 2
    last = T // tm - 1

    def body(*refs):
        a_ref = refs[0]
        r = a_ref[...]
        if res_mm:
            res_val = jnp.dot(refs[1][...], refs[2][...], preferred_element_type=F32) + refs[3][...]
            refs[-1 - n_scratch][...] = res_val
            r = ALPHA * r + res_val
        elif res_ins:
            r = ALPHA * r + refs[1][...]
        g_ref, b_ref = refs[n_in - 2], refs[n_in - 1]
        shard_refs = refs[n_in:n_in + ng]
        h_ref, hb_ref = refs[n_in + ng], refs[n_in + ng + 1]
        p_refs = refs[n_in + ng + 2:n_in + ng + 2 + nd]
        full_refs = refs[n_in + ng + 2 + nd:n_in + 2 * ng + 2 + nd]
        scratch = refs[len(refs) - n_scratch:]
        sems = scratch[len(scratch) - 3:] if ng else ()

        if ng:
            @pl.when(pl.program_id(0) == 0)
            def _():
                _gather_begin(shard_refs, full_refs, *sems)

        xc, rstd = _ln_stats(r)
        h = xc * rstd * g_ref[...] + b_ref[...]
        h_ref[...] = h
        hb_ref[...] = h.astype(BF16)
        for d, p_ref in zip(dilations, p_refs):
            _store_by_residue(h, p_ref, d, scratch[0])

        if ng:
            @pl.when(pl.program_id(0) == last)
            def _():
                _gather_finish(shard_refs, full_refs, *sems)

    row = pl.BlockSpec((tm, D), lambda i: (i, 0))
    vec = pl.BlockSpec((1, D), lambda i: (0, 0))
    hbm = pl.BlockSpec(memory_space=pl.ANY)
    if res_mm:
        res_specs = [pl.BlockSpec((tm, res[1].shape[1]), lambda i: (i, 0)), pl.BlockSpec(res[2].shape, lambda i: (0, 0)), vec]
    else:
        res_specs = [row] * len(res_ins)
    scratch_shapes = ([_residue_scratch(tm, D)] if nd else []) + (_gather_scratch(ng) if ng else [])
    n_scratch = len(scratch_shapes)
    ins = [a] + res_ins + [g, b] + list(gather)
    return pl.pallas_call(
        body, name=name, grid=(T // tm,),
        in_specs=[row] + res_specs + [vec, vec] + [hbm] * ng,
        out_specs=[row, row] + [_residue_spec(tm, d, D) for d in dilations] + [hbm] * ng + ([row] if res_mm else []),
        out_shape=[jax.ShapeDtypeStruct((T, D), F32), jax.ShapeDtypeStruct((T, D), BF16)]
        + [jax.ShapeDtypeStruct((d, T // d, D), BF16) for d in dilations]
        + [jax.ShapeDtypeStruct((N_DEV,) + s.shape, s.dtype) for s in gather]
        + ([jax.ShapeDtypeStruct((T, D), F32)] if res_mm else []),
        scratch_shapes=scratch_shapes,
        compiler_params=_params(("arbitrary",) if ng else ("parallel",)),
    )(*ins)


def ln_bwd(a, res, g, b, d1, d2, tgt, name, by_residue=()):
    T, D = a.shape
    tm = _pick(T, 256, 8)
    loss_mode = tgt is not None
    nres = len(by_residue)
    row = pl.BlockSpec((tm, D), lambda i: (i, 0))
    vec = pl.BlockSpec((1, D), lambda i: (0, 0))
    one = pl.BlockSpec((1, 1), lambda i: (0, 0))

    def rows_of(x):
        return pl.BlockSpec((tm, x.shape[1]), lambda i: (i, 0))

    def whole(x):
        return pl.BlockSpec(x.shape, lambda i: (0, 0))

    ins, in_specs, slots = [], [], {}

    def operand(key, arrays, specs):
        slots[key] = (len(ins), len(arrays))
        ins.extend(arrays)
        in_specs.extend(specs)

    operand("a", [a], [row])
    if isinstance(res, tuple):
        _, x, w, bias = res
        operand("res_mm", [x, w, bias], [rows_of(x), whole(w), vec])
    elif res is not None:
        operand("res", [res], [row])
    operand("gb", [g, b], [vec, vec])
    if loss_mode:
        operand("tgt", [tgt], [row])
    else:
        operand("d1", [d1], [row])
        if isinstance(d2, tuple):
            _, pieces, w = d2
            operand("d2_mm", list(pieces) + [w], [rows_of(p) for p in pieces] + [whole(w)])
        else:
            operand("d2", [d2], [row])
    operand("by_residue", [e for e, _ in by_residue], [_residue_spec(tm, d, D) for _, d in by_residue])
    n_in = len(ins)

    def body(*refs):
        def get(key):
            first, count = slots[key]
            return refs[first:first + count]

        dr_ref, drb_ref, dg_ref, db_ref, ds_ref, loss_ref = refs[n_in:n_in + 6]
        i = pl.program_id(0)

        @pl.when(i == 0)
        def _():
            dg_ref[...] = jnp.zeros_like(dg_ref)
            db_ref[...] = jnp.zeros_like(db_ref)
            ds_ref[...] = jnp.zeros_like(ds_ref)
            loss_ref[...] = jnp.zeros_like(loss_ref)

        r = get("a")[0][...]
        if "res_mm" in slots:
            x_ref, w_ref, bias_ref = get("res_mm")
            r = ALPHA * r + (jnp.dot(x_ref[...], w_ref[...], preferred_element_type=F32) + bias_ref[...])
        elif "res" in slots:
            r = ALPHA * r + get("res")[0][...]
        g_ref, b_ref = get("gb")
        xc, rstd = _ln_stats(r)
        xhat = xc * rstd
        gam = g_ref[...]
        if loss_mode:
            err = xhat * gam + b_ref[...] - get("tgt")[0][...]
            dy = err * (1.0 / D)
            row_loss = jnp.mean(err * err, -1, keepdims=True)
            loss_ref[...] += 0.5 * jnp.sum(row_loss, 0, keepdims=True)
        else:
            if "d2_mm" in slots:
                *p_refs, w_ref = get("d2_mm")
                av = p_refs[0][...] if len(p_refs) == 1 else jnp.concatenate([p[...] for p in p_refs], axis=1)
                d2v = lax.dot_general(av, w_ref[...], NT_DIMS, preferred_element_type=F32)
            else:
                d2v = get("d2")[0][...]
            dy = ALPHA * get("d1")[0][...] + d2v
        for (_, d), e_ref in zip(by_residue, get("by_residue")):
            dy = dy + _load_natural(e_ref, d, refs[-1])
        dyg = dy * gam
        c1 = jnp.mean(dyg, -1, keepdims=True)
        c2 = jnp.mean(dyg * xhat, -1, keepdims=True)
        dr = rstd * (dyg - c1 - xhat * c2)
        dr_ref[...] = dr
        drb_ref[...] = dr.astype(BF16)
        dg_ref[...] += jnp.sum(dy * xhat, 0, keepdims=True)
        db_ref[...] += jnp.sum(dy, 0, keepdims=True)
        ds_ref[...] += jnp.sum(dr, 0, keepdims=True)

    return pl.pallas_call(
        body, name=name, grid=(T // tm,),
        in_specs=in_specs,
        out_specs=[row, row, vec, vec, vec, one],
        out_shape=[jax.ShapeDtypeStruct((T, D), F32), jax.ShapeDtypeStruct((T, D), BF16),
                   jax.ShapeDtypeStruct((1, D), F32), jax.ShapeDtypeStruct((1, D), F32),
                   jax.ShapeDtypeStruct((1, D), F32), jax.ShapeDtypeStruct((1, 1), F32)],
        scratch_shapes=[_residue_scratch(tm, D)] if nres else [],
        compiler_params=_params(("arbitrary",)),
    )(*ins)


_TOKEN_SPEC = pl.BlockSpec((8, LANES), lambda i: (0, 0))


def mm_nn(a, w, bias, out_dtype, name, after=None):
    M, K = a.shape
    N = w.shape[1]
    tm = _pick(M, max(256, min(1024, OUT_TILE_BYTES // (N * jnp.dtype(out_dtype).itemsize))), 8)
    tc = _pick(N, 512)

    def body(a_ref, w_ref, b_ref, *rest):
        o_ref = rest[-1]
        av = a_ref[...]
        for j in range(N // tc):
            cols = slice(j * tc, (j + 1) * tc)
            acc = jnp.dot(av, w_ref[:, cols], preferred_element_type=F32)
            o_ref[:, cols] = (acc + b_ref[:, cols]).astype(out_dtype)

    return pl.pallas_call(
        body, name=name, grid=(M // tm,),
        in_specs=[pl.BlockSpec((tm, K), lambda i: (i, 0)),
                  pl.BlockSpec((K, N), lambda i: (0, 0)),
                  pl.BlockSpec((1, N), lambda i: (0, 0))] + ([] if after is None else [_TOKEN_SPEC]),
        out_specs=pl.BlockSpec((tm, N), lambda i: (i, 0)),
        out_shape=jax.ShapeDtypeStruct((M, N), out_dtype),
        compiler_params=_params(("parallel",)),
    )(a, w, bias, *([] if after is None else [after]))


def mm_nt(a, w, acc_in, name, after=None, w_block=0, out_dtype=F32):
    pieces = list(a) if isinstance(a, (list, tuple)) else [a]
    M = pieces[0].shape[0]
    widths = [p.shape[1] for p in pieces]
    K = sum(widths)
    N = w.shape[0]
    tm = _pick(M, 1024, 8)
    tc = _pick(N, 512)
    has_acc = acc_in is not None
    n_a = len(pieces)

    def body(*refs):
        a_refs, w_ref = refs[:n_a], refs[n_a]
        c_ref = refs[n_a + 1] if has_acc else None
        o_ref = refs[-1]
        av = a_refs[0][...] if n_a == 1 else jnp.concatenate([r[...] for r in a_refs], axis=1)
        for j in range(N // tc):
            cols = slice(j * tc, (j + 1) * tc)
            acc = lax.dot_general(av, w_ref[cols, :], NT_DIMS, preferred_element_type=F32)
            if has_acc:
                acc = acc + c_ref[:, cols]
            o_ref[:, cols] = acc.astype(out_dtype)

    out_spec = pl.BlockSpec((tm, N), lambda i: (i, 0))
    in_specs = [pl.BlockSpec((tm, kw), lambda i: (i, 0)) for kw in widths]
    in_specs.append(pl.BlockSpec((N, K), lambda i: (0, w_block)))
    ins = pieces + [w]
    if has_acc:
        in_specs.append(out_spec)
        ins.append(acc_in)
    if after is not None:
        in_specs.append(_TOKEN_SPEC)
        ins.append(after)
    return pl.pallas_call(
        body, name=name, grid=(M // tm,),
        in_specs=in_specs, out_specs=out_spec,
        out_shape=jax.ShapeDtypeStruct((M, N), out_dtype),
        compiler_params=_params(("parallel",)),
    )(*ins)


def mm_tn(a, b, name, out_dtype=BF16):
    pieces = list(b) if isinstance(b, (list, tuple)) else [b]
    T, M = a.shape
    widths = [p.shape[1] for p in pieces]
    N = sum(widths)
    tk = _pick(T, 1024, 8)
    nk = T // tk
    tc = _pick(M, 256)
    n_b = len(pieces)

    def body(*refs):
        a_ref, b_refs = refs[0], refs[1:1 + n_b]
        o_ref, cs_ref, acc_ref = refs[1 + n_b:]
        k = pl.program_id(0)

        @pl.when(k == 0)
        def _():
            acc_ref[...] = jnp.zeros_like(acc_ref)
            cs_ref[...] = jnp.zeros_like(cs_ref)

        bv = b_refs[0][...] if n_b == 1 else jnp.concatenate([r[...] for r in b_refs], axis=1)
        cs_ref[...] += jnp.sum(bv.astype(F32), 0, keepdims=True)
        for mi in range(M // tc):
            rows = slice(mi * tc, (mi + 1) * tc)
            acc_ref[rows, :] += lax.dot_general(a_ref[:, rows], bv, TN_DIMS, preferred_element_type=F32)

        @pl.when(k == nk - 1)
        def _():
            o_ref[...] = acc_ref[...].astype(out_dtype)

    return pl.pallas_call(
        body, name=name, grid=(nk,),
        in_specs=[pl.BlockSpec((tk, M), lambda k: (k, 0))] + [pl.BlockSpec((tk, wd), lambda k: (k, 0)) for wd in widths],
        out_specs=[pl.BlockSpec((M, N), lambda k: (0, 0)), pl.BlockSpec((1, N), lambda k: (0, 0))],
        out_shape=[jax.ShapeDtypeStruct((M, N), out_dtype), jax.ShapeDtypeStruct((1, N), F32)],
        scratch_shapes=[pltpu.VMEM((M, N), F32)],
        compiler_params=_params(("arbitrary",)),
    )(a, *pieces)


def _ext_rows(prev_ref, main_ref, next_ref, i, tm, T, dtype=F32):
    before = jnp.where(i == 0, 0.0, prev_ref[...])
    after = jnp.where(i == T // tm - 1, 0.0, next_ref[...])
    return jnp.concatenate([before, main_ref[...], after], axis=0).astype(dtype)


def _prev_row(x):
    return pltpu.roll(x, 1, 0)


def _next_row(x):
    return pltpu.roll(x, x.shape[0] - 1, 0)


def _conv3(u, w_ref):
    return _prev_row(u) * w_ref[0:1, :] + u * w_ref[1:2, :] + _next_row(u) * w_ref[2:3, :]


def _main(x, tm, halo=HALO):
    return x[halo:halo + tm]


def _halo_specs(tm, tc, T, col, order, halo=HALO):
    r = tm // halo
    last = T // halo - 1
    if order == "ij":
        return (pl.BlockSpec((halo, tc), lambda i, j: (jnp.maximum(i * r - 1, 0), col(j))),
                pl.BlockSpec((tm, tc), lambda i, j: (i, col(j))),
                pl.BlockSpec((halo, tc), lambda i, j: (jnp.minimum((i + 1) * r, last), col(j))))
    return (pl.BlockSpec((halo, tc), lambda j, i: (jnp.maximum(i * r - 1, 0), col(j))),
            pl.BlockSpec((tm, tc), lambda j, i: (i, col(j))),
            pl.BlockSpec((halo, tc), lambda j, i: (jnp.minimum((i + 1) * r, last), col(j))))


def conv_a_fwd(proj_a, conv_w, name):
    T, D3 = proj_a.shape
    D = D3 // 3
    tm = _pick(T, 256, 8)

    def body(p_ref, m_ref, n_ref, w_ref, o_ref):
        i = pl.program_id(0)
        ext = _ext_rows(p_ref, m_ref, n_ref, i, tm, T)
        u = ext[:, D:2 * D] * ext[:, 2 * D:]
        cu = _conv3(u, w_ref)
        o_ref[...] = (m_ref[:, :D].astype(F32) * _main(cu, tm, HALO_BF16)).astype(BF16)

    prev, main, nxt = _halo_specs(tm, D3, T, lambda j: 0, "ij", HALO_BF16)
    return pl.pallas_call(
        body, name=name, grid=(T // tm, 1),
        in_specs=[prev, main, nxt, pl.BlockSpec((3, D), lambda i, j: (0, 0))],
        out_specs=pl.BlockSpec((tm, D), lambda i, j: (i, 0)),
        out_shape=jax.ShapeDtypeStruct((T, D), BF16),
        compiler_params=_params(("parallel", "arbitrary")),
    )(proj_a, proj_a, proj_a, conv_w)


def conv_a_bwd(dy_a, w_a, proj_a, conv_w, name):
    T, D3 = proj_a.shape
    D = D3 // 3
    tm = _pick(T, 256, 8)

    def body(dp_ref, dm_ref, dn_ref, wa_ref, p_ref, m_ref, n_ref, w_ref, o_ref, dw_ref):
        i = pl.program_id(0)

        @pl.when(i == 0)
        def _():
            dw_ref[...] = jnp.zeros_like(dw_ref)

        ext = _ext_rows(p_ref, m_ref, n_ref, i, tm, T)
        dsa = lax.dot_general(_ext_rows(dp_ref, dm_ref, dn_ref, i, tm, T, dtype=BF16), wa_ref[...], NT_DIMS,
                              preferred_element_type=F32)
        gb, gc, hin = ext[:, :D], ext[:, D:2 * D], ext[:, 2 * D:]
        u = gc * hin
        u_prev, u_next = _prev_row(u), _next_row(u)
        cu = u_prev * w_ref[0:1, :] + u * w_ref[1:2, :] + u_next * w_ref[2:3, :]
        dcu = dsa * gb
        du = _next_row(dcu) * w_ref[0:1, :] + dcu * w_ref[1:2, :] + _prev_row(dcu) * w_ref[2:3, :]
        h = HALO_BF16
        o_ref[:, :D] = _main(dsa * cu, tm, h).astype(BF16)
        o_ref[:, D:2 * D] = _main(du * hin, tm, h).astype(BF16)
        o_ref[:, 2 * D:] = _main(du * gc, tm, h).astype(BF16)
        dcu_m = _main(dcu, tm, h)
        dw_ref[0:1, :] += jnp.sum(dcu_m * _main(u_prev, tm, h), 0, keepdims=True)
        dw_ref[1:2, :] += jnp.sum(dcu_m * _main(u, tm, h), 0, keepdims=True)
        dw_ref[2:3, :] += jnp.sum(dcu_m * _main(u_next, tm, h), 0, keepdims=True)

    dprev, dmain, dnxt = _halo_specs(tm, dy_a.shape[1], T, lambda j: 0, "ij", HALO_BF16)
    prev, main, nxt = _halo_specs(tm, D3, T, lambda j: 0, "ij", HALO_BF16)
    return pl.pallas_call(
        body, name=name, grid=(T // tm, 1),
        in_specs=[dprev, dmain, dnxt, pl.BlockSpec(w_a.shape, lambda i, j: (0, 0)), prev, main, nxt,
                  pl.BlockSpec((3, D), lambda i, j: (0, 0))],
        out_specs=[pl.BlockSpec((tm, D3), lambda i, j: (i, 0)), pl.BlockSpec((3, D), lambda i, j: (0, 0))],
        out_shape=[jax.ShapeDtypeStruct((T, D3), BF16), jax.ShapeDtypeStruct((3, D), F32)],
        compiler_params=_params(("arbitrary", "arbitrary")),
    )(dy_a, dy_a, dy_a, w_a, proj_a, proj_a, proj_a, conv_w)


_INV_SQRT2 = 1.0 / math.sqrt(2.0)
_INV_SQRT_2PI = 1.0 / math.sqrt(2.0 * math.pi)


def ffn_up_conv_f(h, w_up, b_up, fcw, fcb, name):
    T, D = h.shape
    F = fcb.shape[1]
    tm = _pick(T, 256, 8)
    tc = _pick(F, 256)
    halo = HALO_BF16

    def body(hp_ref, hm_ref, hn_ref, w_ref, b_ref, cw_ref, cb_ref, up_ref, f_ref):
        i = pl.program_id(0)
        h_ext = _ext_rows(hp_ref, hm_ref, hn_ref, i, tm, T, dtype=BF16)
        h_main = hm_ref[...]
        rows = i * tm - halo + lax.broadcasted_iota(jnp.int32, (tm + 2 * halo, 1), 0)
        inside = (rows >= 0) & (rows < T)
        for c in range(F // tc):
            cols = slice(c * tc, (c + 1) * tc)
            gcols = slice(F + c * tc, F + (c + 1) * tc)
            a_ext = jnp.dot(h_ext, w_ref[:, cols], preferred_element_type=F32) + b_ref[:, cols]
            a_ext = jnp.where(inside, a_ext, 0.0)
            gate = jnp.dot(h_main, w_ref[:, gcols], preferred_element_type=F32) + b_ref[:, gcols]
            up_ref[:, cols] = _main(a_ext, tm, halo)
            up_ref[:, gcols] = gate
            ca = _main(_prev_row(a_ext) * cw_ref[0:1, cols] + a_ext * cw_ref[1:2, cols]
                       + _next_row(a_ext) * cw_ref[2:3, cols], tm, halo) + cb_ref[:, cols]
            gl = 0.5 * ca * (1.0 + lax.erf(ca * _INV_SQRT2))
            f_ref[:, cols] = (gl * gate).astype(BF16)

    prev, main, nxt = _halo_specs(tm, D, T, lambda j: 0, "ij", halo)
    whole = lambda x: pl.BlockSpec(x.shape, lambda i, j: (0, 0))
    return pl.pallas_call(
        body, name=name, grid=(T // tm, 1),
        in_specs=[prev, main, nxt, whole(w_up), whole(b_up), whole(fcw), whole(fcb)],
        out_specs=[pl.BlockSpec((tm, 2 * F), lambda i, j: (i, 0)), pl.BlockSpec((tm, F), lambda i, j: (i, 0))],
        out_shape=[jax.ShapeDtypeStruct((T, 2 * F), F32), jax.ShapeDtypeStruct((T, F), BF16)],
        compiler_params=_params(("parallel", "arbitrary")),
    )(h, h, h, w_up, b_up, fcw, fcb)


def conv_f_bwd(dy, w_down, up, fcw, fcb, name):
    T, F2 = up.shape
    F = F2 // 2
    D = dy.shape[1]
    tm = _pick(T, 256, 8)
    tc = _pick(F, 256)

    def body(yp_ref, ym_ref, yn_ref, wd_ref, up_ref, um_ref, un_ref, w_ref, b_ref,
             da_ref, dg_ref, csa_ref, csg_ref, dfb_ref, dfw_ref):
        i = pl.program_id(0)
        first, last = i == 0, i == T // tm - 1

        @pl.when(first)
        def _():
            csa_ref[...] = jnp.zeros_like(csa_ref)
            csg_ref[...] = jnp.zeros_like(csg_ref)
            dfb_ref[...] = jnp.zeros_like(dfb_ref)
            dfw_ref[...] = jnp.zeros_like(dfw_ref)

        def ext(cols):
            return jnp.concatenate([jnp.where(first, 0.0, up_ref[:, cols]), um_ref[:, cols],
                                    jnp.where(last, 0.0, un_ref[:, cols])], axis=0)

        dy_ext = _ext_rows(yp_ref, ym_ref, yn_ref, i, tm, T, dtype=BF16)
        for c in range(F // tc):
            cols = slice(c * tc, (c + 1) * tc)
            dfe = lax.dot_general(dy_ext, wd_ref[cols, :], NT_DIMS, preferred_element_type=F32)
            dfe = dfe[HALO_BF16 - HALO:HALO_BF16 + tm + HALO]
            a = ext(cols)
            gate = ext(slice(F + c * tc, F + (c + 1) * tc))
            a_prev, a_next = _prev_row(a), _next_row(a)
            ca = a_prev * w_ref[0:1, cols] + a * w_ref[1:2, cols] + a_next * w_ref[2:3, cols] + b_ref[:, cols]
            cdf = 0.5 * (1.0 + lax.erf(ca * _INV_SQRT2))
            gl = ca * cdf
            gp = cdf + ca * (jnp.exp(-0.5 * ca * ca) * _INV_SQRT_2PI)
            dgate = _main(dfe * gl, tm)
            dca = dfe * gate * gp
            da = _main(_next_row(dca) * w_ref[0:1, cols] + dca * w_ref[1:2, cols] + _prev_row(dca) * w_ref[2:3, cols],
                       tm)
            da_ref[:, cols] = da.astype(BF16)
            dg_ref[:, cols] = dgate.astype(BF16)
            csa_ref[:, cols] += jnp.sum(da, 0, keepdims=True)
            csg_ref[:, cols] += jnp.sum(dgate, 0, keepdims=True)
            dca_m = _main(dca, tm)
            dfb_ref[:, cols] += jnp.sum(dca_m, 0, keepdims=True)
            dfw_ref[0:1, cols] += jnp.sum(dca_m * _main(a_prev, tm), 0, keepdims=True)
            dfw_ref[1:2, cols] += jnp.sum(dca_m * _main(a, tm), 0, keepdims=True)
            dfw_ref[2:3, cols] += jnp.sum(dca_m * _main(a_next, tm), 0, keepdims=True)

    uprev, umain, unxt = _halo_specs(tm, F2, T, lambda j: 0, "ij")
    yprev, ymain, ynxt = _halo_specs(tm, D, T, lambda j: 0, "ij", HALO_BF16)
    whole = lambda shape: pl.BlockSpec(shape, lambda i, j: (0, 0))
    tile = pl.BlockSpec((tm, F), lambda i, j: (i, 0))
    return pl.pallas_call(
        body, name=name, grid=(T // tm, 1),
        in_specs=[yprev, ymain, ynxt, whole((F, D)), uprev, umain, unxt, whole((3, F)), whole((1, F))],
        out_specs=[tile, tile, whole((1, F)), whole((1, F)), whole((1, F)), whole((3, F))],
        out_shape=[jax.ShapeDtypeStruct((T, F), BF16), jax.ShapeDtypeStruct((T, F), BF16),
                   jax.ShapeDtypeStruct((1, F), F32), jax.ShapeDtypeStruct((1, F), F32),
                   jax.ShapeDtypeStruct((1, F), F32), jax.ShapeDtypeStruct((3, F), F32)],
        compiler_params=_params(("arbitrary", "arbitrary")),
    )(dy, dy, dy, w_down, up, up, up, fcw, fcb)


def gate_fwd(proj_g, y_a, y_b, name):
    T, D = y_a.shape
    tm = _pick(T, 512, 8)

    def body(g_ref, a_ref, b_ref, o_ref):
        sa = jax.nn.sigmoid(g_ref[:, :D].astype(F32))
        sb = jax.nn.sigmoid(g_ref[:, D:].astype(F32))
        o_ref[...] = (sa * a_ref[...].astype(F32) + sb * b_ref[...].astype(F32)).astype(BF16)

    row = pl.BlockSpec((tm, D), lambda i: (i, 0))
    return pl.pallas_call(
        body, name=name, grid=(T // tm,),
        in_specs=[pl.BlockSpec((tm, 2 * D), lambda i: (i, 0)), row, row],
        out_specs=row,
        out_shape=jax.ShapeDtypeStruct((T, D), BF16),
        compiler_params=_params(("parallel",)),
    )(proj_g, y_a, y_b)


def gate_bwd(dmix, w_o, proj_g, y_a, y_b, name):
    T, D = y_a.shape
    tm = _pick(T, 512, 8)

    def body(dz_ref, w_ref, g_ref, a_ref, b_ref, da_ref, db_ref, dg_ref):
        dzv = lax.dot_general(dz_ref[...], w_ref[...], NT_DIMS, preferred_element_type=F32)
        sa = jax.nn.sigmoid(g_ref[:, :D].astype(F32))
        sb = jax.nn.sigmoid(g_ref[:, D:].astype(F32))
        da_ref[...] = (dzv * sa).astype(BF16)
        db_ref[...] = (dzv * sb).astype(BF16)
        dg_ref[:, :D] = (dzv * a_ref[...].astype(F32) * (sa * (1.0 - sa))).astype(BF16)
        dg_ref[:, D:] = (dzv * b_ref[...].astype(F32) * (sb * (1.0 - sb))).astype(BF16)

    row = pl.BlockSpec((tm, D), lambda i: (i, 0))
    wide = pl.BlockSpec((tm, 2 * D), lambda i: (i, 0))
    return pl.pallas_call(
        body, name=name, grid=(T // tm,),
        in_specs=[pl.BlockSpec((tm, dmix.shape[1]), lambda i: (i, 0)), pl.BlockSpec(w_o.shape, lambda i: (0, 0)),
                  wide, row, row],
        out_specs=[row, row, wide],
        out_shape=[jax.ShapeDtypeStruct((T, D), BF16), jax.ShapeDtypeStruct((T, D), BF16),
                   jax.ShapeDtypeStruct((T, 2 * D), BF16)],
        compiler_params=_params(("parallel",)),
    )(dmix, w_o, proj_g, y_a, y_b)


ATT_WIN = ATT_TQ + 2 * RADIUS
ATT_STEP = 2048
FAR = 1e32


def _att_window(qs, L):
    ks = pl.multiple_of(jnp.clip(qs - RADIUS, 0, L - ATT_WIN), RADIUS)
    return ks, jnp.where(qs == 0, 0, jnp.where(qs == L - ATT_TQ, 2, 1))


def _fill_bias_tables(bias_ref, sl_ref, hp, d):
    col_row = (lax.broadcasted_iota(jnp.int32, (ATT_TQ, ATT_WIN), 1)
               - lax.broadcasted_iota(jnp.int32, (ATT_TQ, ATT_WIN), 0))
    for v in range(3):
        ad = jnp.abs(col_row - v * RADIUS)
        dist = jnp.where(ad <= RADIUS, (ad * d).astype(F32), FAR)
        bias_ref[v, 0:ATT_TQ, :] = sl_ref[hp * 2] * dist
        bias_ref[v, ATT_TQ:2 * ATT_TQ, :] = sl_ref[hp * 2 + 1] * dist


def _head_masks():
    lane = lax.broadcasted_iota(jnp.int32, (1, LANES), 1)
    return [lane < HEAD_DIM, lane >= HEAD_DIM]


def _stack_heads(x, masks):
    zero = jnp.zeros_like(x)
    return jnp.concatenate([jnp.where(masks[0], x, zero), jnp.where(masks[1], x, zero)], axis=0)


def _unstack_heads(x2, masks):
    n = x2.shape[0] // 2
    return jnp.where(masks[0], x2[:n], x2[n:])


def _att_step(L):
    step = min(ATT_STEP, L)
    assert L % step == 0 and step % ATT_TQ == 0 and L >= ATT_WIN
    return step


def _residues_per_step(d, L):
    rps = max(1, min(d, ATT_STEP // L))
    assert d % rps == 0
    return rps


def att_fwd(qkv, group, name):
    d, L, _ = qkv.shape
    step = _att_step(L)
    rps = _residues_per_step(d, L)
    cg = GROUP_W // LANES
    slopes = jnp.asarray(_alibi_slopes()[group])
    scale = HEAD_DIM ** -0.5

    def body(sl_ref, q_ref, k_ref, v_ref, o_ref, l_ref, bias_ref, s_ref, p_ref):
        hp = pl.program_id(1)
        i = pl.program_id(2)

        @pl.when(i == 0)
        def _():
            _fill_bias_tables(bias_ref, sl_ref, hp, d)

        masks = _head_masks()
        per = step // ATT_TQ
        tiles = [(rr, t) for rr in range(rps) for t in range(per)]
        windows = [_att_window(i * step + t * ATT_TQ, L) for t in range(per)]
        for n, (rr, t) in enumerate(tiles):
            rows = slice(t * ATT_TQ, (t + 1) * ATT_TQ)
            ks, table = windows[t]
            q2 = _stack_heads(q_ref[rr, rows, :] * scale, masks)
            kw = k_ref[rr, pl.ds(ks, ATT_WIN), :]
            s_ref[n] = lax.dot_general(q2, kw, NT_DIMS, preferred_element_type=F32) - bias_ref[table]
        for n, (rr, t) in enumerate(tiles):
            rows = slice(t * ATT_TQ, (t + 1) * ATT_TQ)
            s = s_ref[n]
            m = jnp.max(s, -1, keepdims=True)
            p = jnp.exp(s - m)
            den = jnp.sum(p, -1, keepdims=True)
            p_ref[n] = (p / den).astype(BF16)
            l_ref[rr, rows, :] = _unstack_heads(m + jnp.log(den), masks)
        for n, (rr, t) in enumerate(tiles):
            rows = slice(t * ATT_TQ, (t + 1) * ATT_TQ)
            vw = v_ref[rr, pl.ds(windows[t][0], ATT_WIN), :]
            o2 = jnp.dot(p_ref[n], vw, preferred_element_type=F32)
            o_ref[rr, rows, :] = _unstack_heads(o2, masks).astype(ACT)

    n_tiles = rps * step // ATT_TQ
    out_spec = pl.BlockSpec((rps, step, LANES), lambda r, hp, i: (r, i, hp))
    return pl.pallas_call(
        body, name=name, grid=(d // rps, cg, L // step),
        in_specs=[pl.BlockSpec(memory_space=pltpu.SMEM),
                  pl.BlockSpec((rps, step, LANES), lambda r, hp, i: (r, i, hp)),
                  pl.BlockSpec((rps, L, LANES), lambda r, hp, i: (r, 0, cg + hp)),
                  pl.BlockSpec((rps, L, LANES), lambda r, hp, i: (r, 0, 2 * cg + hp))],
        out_specs=[out_spec, out_spec],
        out_shape=[jax.ShapeDtypeStruct((d, L, GROUP_W), ACT), jax.ShapeDtypeStruct((d, L, GROUP_W), F32)],
        scratch_shapes=[pltpu.VMEM((3, 2 * ATT_TQ, ATT_WIN), F32),
                        pltpu.VMEM((n_tiles, 2 * ATT_TQ, ATT_WIN), F32),
                        pltpu.VMEM((n_tiles, 2 * ATT_TQ, ATT_WIN), BF16)],
        compiler_params=_params(("arbitrary", "arbitrary", "arbitrary")),
    )(slopes, qkv, qkv, qkv)


def att_bwd(qkv, do, lse, dmat, group, name, after=None):
    d, L, _ = qkv.shape
    step = _att_step(L)
    rps = _residues_per_step(d, L)
    nq = L // step
    cg = GROUP_W // LANES
    slopes = jnp.asarray(_alibi_slopes()[group])
    scale = HEAD_DIM ** -0.5

    def body(sl_ref, q_ref, k_ref, v_ref, do_ref, l_ref, dm_ref, *rest):
        dq_ref, dk_ref, dv_ref, dk_acc, dv_acc, bias_ref, s_ref, dp_ref, p_ref, ds_ref = rest[len(rest) - 10:]
        hp = pl.program_id(1)
        i = pl.program_id(2)

        @pl.when(i == 0)
        def _():
            dk_acc[...] = jnp.zeros_like(dk_acc)
            dv_acc[...] = jnp.zeros_like(dv_acc)
            _fill_bias_tables(bias_ref, sl_ref, hp, d)

        masks = _head_masks()

        def head_cols(x):
            return jnp.concatenate([jnp.max(jnp.where(hm, x, -jnp.inf), -1, keepdims=True) for hm in masks], axis=0)

        per = step // ATT_TQ
        tiles = [(rr, t) for rr in range(rps) for t in range(per)]
        windows = [_att_window(i * step + t * ATT_TQ, L) for t in range(per)]

        def stacked(ref, rr, t, factor=None):
            x = ref[rr, t * ATT_TQ:(t + 1) * ATT_TQ, :]
            return _stack_heads(x if factor is None else x * factor, masks)

        for n, (rr, t) in enumerate(tiles):
            ks, table = windows[t]
            q2 = stacked(q_ref, rr, t, scale)
            s_ref[n] = lax.dot_general(q2, k_ref[rr, pl.ds(ks, ATT_WIN), :], NT_DIMS,
                                       preferred_element_type=F32) - bias_ref[table]
            dp_ref[n] = lax.dot_general(stacked(do_ref, rr, t), v_ref[rr, pl.ds(ks, ATT_WIN), :], NT_DIMS,
                                        preferred_element_type=F32)
        for n, (rr, t) in enumerate(tiles):
            rows = slice(t * ATT_TQ, (t + 1) * ATT_TQ)
            p = jnp.exp(s_ref[n] - head_cols(l_ref[rr, rows, :]))
            p_ref[n] = p.astype(BF16)
            ds_ref[n] = (p * (dp_ref[n] - head_cols(dm_ref[rr, rows, :]))).astype(BF16)
        for n, (rr, t) in enumerate(tiles):
            rows = slice(t * ATT_TQ, (t + 1) * ATT_TQ)
            ks = windows[t][0]
            ds = ds_ref[n]
            dq2 = jnp.dot(ds, k_ref[rr, pl.ds(ks, ATT_WIN), :], preferred_element_type=F32)
            dq_ref[rr, rows, :] = (_unstack_heads(dq2, masks) * scale).astype(BF16)
            dk_acc[rr, pl.ds(ks, ATT_WIN), :] += lax.dot_general(ds, stacked(q_ref, rr, t, scale), TN_DIMS,
                                                                 preferred_element_type=F32)
            dv_acc[rr, pl.ds(ks, ATT_WIN), :] += lax.dot_general(p_ref[n], stacked(do_ref, rr, t), TN_DIMS,
                                                                 preferred_element_type=F32)

        @pl.when(i == nq - 1)
        def _():
            dk_ref[...] = dk_acc[...].astype(BF16)
            dv_ref[...] = dv_acc[...].astype(BF16)

    tile = pl.BlockSpec((rps, step, LANES), lambda r, hp, i: (r, i, hp))
    whole = pl.BlockSpec((rps, L, LANES), lambda r, hp, i: (r, 0, hp))
    return pl.pallas_call(
        body, name=name, grid=(d // rps, cg, nq),
        in_specs=[pl.BlockSpec(memory_space=pltpu.SMEM), tile,
                  pl.BlockSpec((rps, L, LANES), lambda r, hp, i: (r, 0, cg + hp)),
                  pl.BlockSpec((rps, L, LANES), lambda r, hp, i: (r, 0, 2 * cg + hp)),
                  tile, tile, tile] + ([] if after is None else [pl.BlockSpec((8, LANES), lambda r, hp, i: (0, 0))]),
        out_specs=[tile, whole, whole],
        out_shape=[jax.ShapeDtypeStruct((d, L, GROUP_W), BF16)] * 3,
        scratch_shapes=[pltpu.VMEM((rps, L, LANES), F32), pltpu.VMEM((rps, L, LANES), F32),
                        pltpu.VMEM((3, 2 * ATT_TQ, ATT_WIN), F32)]
        + [pltpu.VMEM((rps * step // ATT_TQ, 2 * ATT_TQ, ATT_WIN), dt) for dt in (F32, F32, BF16, BF16)],
        compiler_params=_params(("arbitrary", "arbitrary", "arbitrary")),
    )(slopes, qkv, qkv, qkv, do, lse, dmat, *([] if after is None else [after]))


def _group_weights(ls):
    m = jnp.maximum(jnp.maximum(ls[0], ls[1]), ls[2])
    es = [jnp.exp(l - m) for l in ls]
    tot = es[0] + es[1] + es[2]
    return [e / tot for e in es]


def combine_fwd(outs, lses, name):
    T = outs[0].shape[0] * outs[0].shape[1]
    tm = _pick(T, 512, 8)
    n_scr = 2 * (len(DILATIONS) - 1)

    def body(*refs):
        o_refs, l_refs, c_ref, scr = refs[:3], refs[3:6], refs[6], refs[7:]
        o = [_load_natural(o_refs[g], d, scr[g - 1] if g else None) for g, d in enumerate(DILATIONS)]
        l = [_load_natural(l_refs[g], d, scr[g + 1] if g else None) for g, d in enumerate(DILATIONS)]
        w = _group_weights(l)
        c_ref[...] = (w[0] * o[0] + w[1] * o[1] + w[2] * o[2]).astype(BF16)

    specs = [_residue_spec(tm, d, GROUP_W) for d in DILATIONS]
    return pl.pallas_call(
        body, name=name, grid=(T // tm,),
        in_specs=specs + specs, out_specs=pl.BlockSpec((tm, GROUP_W), lambda i: (i, 0)),
        out_shape=jax.ShapeDtypeStruct((T, GROUP_W), BF16),
        scratch_shapes=[_residue_scratch(tm, GROUP_W)] * n_scr,
        compiler_params=_params(("parallel",)),
    )(*outs, *lses)


def combine_bwd(dcomb, outs, lses, name):
    T = dcomb.shape[0]
    tm = _pick(T, 256, 8)
    head = np.arange(GROUP_W) // HEAD_DIM
    seg = jnp.asarray((head[:, None] == head[None, :]).astype(np.float32)).astype(BF16)
    ng = len(DILATIONS)
    n_scr = 4 * (ng - 1)

    def body(*refs):
        dc_ref, o_refs, l_refs, e_ref = refs[0], refs[1:1 + ng], refs[1 + ng:1 + 2 * ng], refs[1 + 2 * ng]
        do_refs, dm_refs = refs[2 + 2 * ng:2 + 3 * ng], refs[2 + 3 * ng:2 + 4 * ng]
        scr = refs[2 + 4 * ng:]
        o = [_load_natural(o_refs[g], d, scr[4 * (g - 1)] if g else None) for g, d in enumerate(DILATIONS)]
        l = [_load_natural(l_refs[g], d, scr[4 * (g - 1) + 1] if g else None) for g, d in enumerate(DILATIONS)]
        w = _group_weights(l)
        dc = dc_ref[...].astype(F32)
        e = e_ref[...]
        prod = dc * (w[0] * o[0] + w[1] * o[1] + w[2] * o[2])
        tot = jnp.zeros_like(dc)
        for _ in range(3):
            part = prod.astype(BF16)
            tot = tot + jnp.dot(part, e, preferred_element_type=F32)
            prod = prod - part.astype(F32)
        for g, d in enumerate(DILATIONS):
            _store_by_residue(w[g] * dc, do_refs[g], d, scr[4 * (g - 1) + 2] if g else None)
            _store_by_residue(w[g] * tot, dm_refs[g], d, scr[4 * (g - 1) + 3] if g else None)

    specs = [_residue_spec(tm, d, GROUP_W) for d in DILATIONS]
    res = pl.pallas_call(
        body, name=name, grid=(T // tm,),
        in_specs=[pl.BlockSpec((tm, GROUP_W), lambda i: (i, 0))] + specs + specs
        + [pl.BlockSpec((GROUP_W, GROUP_W), lambda i: (0, 0))],
        out_specs=specs + specs,
        out_shape=[jax.ShapeDtypeStruct(o.shape, BF16) for o in outs] + [jax.ShapeDtypeStruct(o.shape, F32) for o in outs],
        scratch_shapes=[_residue_scratch(tm, GROUP_W)] * n_scr,
        compiler_params=_params(("parallel",)),
    )(dcomb, *outs, *lses, seg)
    return res[:ng], res[ng:]


def _position():
    return lax.axis_index("x"), lax.axis_index("y"), lax.axis_index("c")


def _other_chips(x, y):
    return [(1 - x, y), (x, 1 - y), (1 - x, 1 - y)]


def _remote(src, dst, send_sems, recv_sems, k, to):
    return pltpu.make_async_remote_copy(src_ref=src, dst_ref=dst, send_sem=send_sems.at[k], recv_sem=recv_sems.at[k],
                                        device_id=to, device_id_type=MESH)


GATHER_SEMS = 10
SPLIT_ROWS = 32


def _gather_plan(ins, outs, send_sems, recv_sems, local_sems):
    x, y, c = _position()
    sibling = (x, y, 1 - c)
    nbr_x, nbr_y, diag = (1 - x, y, c), (x, 1 - y, c), (1 - x, 1 - y, c)
    local, begin, stages, last = [], [], [], []
    for a in range(len(ins)):
        k0 = GATHER_SEMS * a
        rows = ins[a].shape[0]
        half = rows // 2

        def block(dev):
            return outs[a].at[4 * dev[0] + 2 * dev[1] + dev[2]]

        def part(ref, h):
            return ref.at[pl.ds(h * half, half)]

        def copy(k, src, dst, to):
            return _remote(src, dst, send_sems, recv_sems, k0 + k, to)

        me = (x, y, c)
        local.append(pltpu.make_async_copy(ins[a], block(me), local_sems.at[a]))
        begin.append(copy(0, ins[a], block(me), sibling))
        pass_on = [copy(7 + j, block(dev), block(dev), sibling) for j, dev in enumerate((nbr_x, nbr_y, diag))]
        if rows >= SPLIT_ROWS and rows % SPLIT_ROWS == 0:
            for h in range(2):
                begin.append(copy(1 + h, part(ins[a], h), part(block(me), h), nbr_x))
                begin.append(copy(3 + h, part(ins[a], h), part(block(me), h), nbr_y))
            from_x = [copy(1 + h, part(block(nbr_x), h), part(block(nbr_x), h), sibling) for h in range(2)]
            from_y = [copy(3 + h, part(block(nbr_y), h), part(block(nbr_y), h), sibling) for h in range(2)]
            fwd_0 = copy(5, part(block(nbr_x), 0), part(block(nbr_x), 0), nbr_y)
            fwd_1 = copy(6, part(block(nbr_y), 1), part(block(nbr_y), 1), nbr_x)
            got_0 = copy(5, part(block(diag), 0), part(block(diag), 0), sibling)
            got_1 = copy(6, part(block(diag), 1), part(block(diag), 1), sibling)
            stages.append(([from_x[0]], [fwd_0]))
            stages.append(([from_y[1]], [fwd_1]))
            stages.append(([from_x[1]], [pass_on[0]]))
            stages.append(([from_y[0]], [pass_on[1]]))
            stages.append(([got_0, got_1], [pass_on[2]]))
        else:
            for j, dev in enumerate((nbr_x, nbr_y, diag)):
                begin.append(copy(1 + 2 * j, ins[a], block(me), dev))
                stages.append(([copy(1 + 2 * j, block(dev), block(dev), sibling)], [pass_on[j]]))
        other = (x, y, 1 - c)
        last.append(copy(0, block(other), block(other), sibling))
        for j, dev in enumerate((nbr_x, nbr_y, diag)):
            theirs = (dev[0], dev[1], 1 - c)
            last.append(copy(7 + j, block(theirs), block(theirs), sibling))
    return local, begin, stages, last


def _gather_begin(ins, outs, send_sems, recv_sems, local_sems):
    local, begin, _, _ = _gather_plan(ins, outs, send_sems, recv_sems, local_sems)
    for cp in local + begin:
        cp.start()


def _gather_finish(ins, outs, send_sems, recv_sems, local_sems):
    local, begin, stages, last = _gather_plan(ins, outs, send_sems, recv_sems, local_sems)
    started = []
    for arrivals, onward in stages:
        for cp in arrivals:
            cp.wait_recv()
        for cp in onward:
            cp.start()
            started.append(cp)
    for cp in last:
        cp.wait_recv()
    for cp in begin + started:
        cp.wait_send()
    for cp in local:
        cp.wait()


def _gather_scratch(n):
    return [pltpu.SemaphoreType.DMA((GATHER_SEMS * n,)), pltpu.SemaphoreType.DMA((GATHER_SEMS * n,)),
            pltpu.SemaphoreType.DMA((n,))]


_HBM = pl.BlockSpec(memory_space=pltpu.HBM)
_SEM = pl.BlockSpec(memory_space=pltpu.SEMAPHORE)
_DATAFLOW = pltpu.SideEffectType.DATAFLOW_SIDE_EFFECTING


def _to_all_plan(srcs, lands, send_sems, recv_sems):
    x, y, c = _position()
    me = 4 * x + 2 * y + c
    copies = []
    for a in range(len(srcs)):
        for k in range(1, N_DEV):
            fx, fy, fc = (k >> 2) & 1, (k >> 1) & 1, k & 1
            to = (1 - x if fx else x, 1 - y if fy else y, 1 - c if fc else c)
            copies.append(_remote(srcs[a], lands[a].at[me], send_sems, recv_sems, (N_DEV - 1) * a + k - 1, to))
    return copies


def _to_sibling_plan(srcs, lands, send_sems, recv_sems):
    x, y, c = _position()
    copies = []
    for a in range(len(srcs)):
        for q in range(4):
            copies.append(_remote(srcs[a].at[2 * q + (1 - c)], lands[a].at[q], send_sems, recv_sems, 4 * a + q,
                                  (x, y, 1 - c)))
    return copies


def _to_chips_plan(srcs, lands, send_sems, recv_sems):
    x, y, c = _position()
    copies = []
    for a in range(len(srcs)):
        for j, (cx, cy) in enumerate(_other_chips(x, y)):
            copies.append(_remote(srcs[a].at[2 * cx + cy], lands[a].at[j], send_sems, recv_sems, 3 * a + j, (cx, cy, c)))
    return copies


def copies_start(srcs, land_shapes, plan, per_array, name):
    n = len(srcs)
    n_sem = per_array * n
    lands = [lax.empty(s.shape, s.dtype) for s in land_shapes]

    def body(*refs):
        src_refs, land_refs = refs[:n], refs[n:2 * n]
        send_sems, recv_sems = refs[2 * n], refs[2 * n + 1]
        token = refs[-1]
        for cp in plan(src_refs, land_refs, send_sems, recv_sems):
            cp.start()
        token[...] = jnp.zeros_like(token)

    out = pl.pallas_call(
        body, name=name,
        out_shape=(pltpu.SemaphoreType.DMA((n_sem,)), pltpu.SemaphoreType.DMA((n_sem,)))
        + tuple(pltpu.HBM(s.shape, s.dtype) for s in srcs)
        + tuple(pltpu.HBM(s.shape, s.dtype) for s in land_shapes)
        + (jax.ShapeDtypeStruct((8, LANES), F32),),
        in_specs=[_HBM] * (2 * n),
        out_specs=(_SEM, _SEM) + (_HBM,) * (2 * n) + (pl.BlockSpec(memory_space=pltpu.VMEM),),
        input_output_aliases={i: 2 + i for i in range(2 * n)},
        compiler_params=pltpu.CompilerParams(has_side_effects=_DATAFLOW),
    )(*[pltpu.with_memory_space_constraint(s, pltpu.HBM) for s in srcs],
      *[pltpu.with_memory_space_constraint(l, pltpu.HBM) for l in lands])
    return out[:-1], out[-1]


def copies_wait(handles, plan, after, name):
    send_sems, recv_sems = handles[0], handles[1]
    n = (len(handles) - 2) // 2
    thru = handles[2:]

    def body(*refs):
        src_refs, land_refs = refs[:n], refs[n:2 * n]
        send_sems, recv_sems = refs[2 * n], refs[2 * n + 1]
        copies = plan(src_refs, land_refs, send_sems, recv_sems)
        for cp in copies:
            cp.wait_recv()
        for cp in copies:
            cp.wait_send()

    out = pl.pallas_call(
        body, name=name,
        out_shape=tuple(pltpu.HBM(t.shape, t.dtype) for t in thru),
        in_specs=[_HBM] * (2 * n) + [_SEM, _SEM, pl.BlockSpec(memory_space=pl.ANY)],
        out_specs=(_HBM,) * (2 * n),
        input_output_aliases={i: i for i in range(2 * n)},
        compiler_params=pltpu.CompilerParams(has_side_effects=_DATAFLOW),
    )(*thru, send_sems, recv_sems, after)
    return out[:n], out[n:]


def all_sum_small(vec, name):
    R = vec.shape[0]

    def body(v_ref, tot_ref, all_ref, send_sems, recv_sems):
        x, y, c = _position()
        me = 4 * x + 2 * y + c
        all_ref[me] = v_ref[...]
        copies = []
        for k in range(1, N_DEV):
            fx, fy, fc = (k >> 2) & 1, (k >> 1) & 1, k & 1
            to = (1 - x if fx else x, 1 - y if fy else y, 1 - c if fc else c)
            cp = _remote(v_ref, all_ref.at[me], send_sems, recv_sems, k - 1, to)
            cp.start()
            copies.append(cp)
        for cp in copies:
            cp.wait_recv()
        for cp in copies:
            cp.wait_send()
        tot = all_ref[0]
        for j in range(1, N_DEV):
            tot = tot + all_ref[j]
        tot_ref[...] = tot

    vmem = pl.BlockSpec(memory_space=pltpu.VMEM)
    return pl.pallas_call(
        body, name=name,
        in_specs=[vmem], out_specs=vmem,
        out_shape=jax.ShapeDtypeStruct((R, LANES), F32),
        scratch_shapes=[pltpu.VMEM((N_DEV, R, LANES), F32),
                        pltpu.SemaphoreType.DMA((N_DEV - 1,)), pltpu.SemaphoreType.DMA((N_DEV - 1,))],
        compiler_params=pltpu.CompilerParams(vmem_limit_bytes=VMEM_LIMIT),
    )(vec)


def pair_add(parts, theirs, place, name):
    _, R, C = theirs.shape
    tr = _pick(R, 1024, 8)

    def body(place_ref, a_ref, b_ref, o_ref):
        o_ref[...] = (a_ref[...].astype(F32) + b_ref[...].astype(F32)).astype(BF16)

    blk = pl.BlockSpec((None, tr, C), lambda q, i, place_ref: (q, i, 0))
    return pl.pallas_call(
        body, name=name,
        grid_spec=pltpu.PrefetchScalarGridSpec(
            num_scalar_prefetch=1, grid=(4, R // tr),
            in_specs=[pl.BlockSpec((None, tr, C), lambda q, i, place_ref: (2 * q + place_ref[2], i, 0)), blk],
            out_specs=blk),
        out_shape=jax.ShapeDtypeStruct(theirs.shape, BF16),
        compiler_params=_params(("parallel", "parallel")),
    )(place, parts, theirs)


def _adamw_math(w, g, m, v):
    m = ADAM_B1 * m + (1.0 - ADAM_B1) * g
    v = ADAM_B2 * v + (1.0 - ADAM_B2) * jnp.square(g)
    m_hat = m / (1.0 - ADAM_B1 ** ADAM_STEP)
    v_hat = v / (1.0 - ADAM_B2 ** ADAM_STEP)
    delta = -ADAM_LR * (m_hat / (jnp.sqrt(v_hat) + ADAM_EPS) + ADAM_WD * w)
    return delta, m, v


def adamw_sharded(w, m, v, parts, sib, others, place, name):
    R, C = w.shape
    tr = _pick(R, 256, 8)

    def body(place_ref, w_ref, m_ref, v_ref, a_ref, b_ref, o_ref, g_ref, d_ref, nm_ref, nv_ref):
        g = a_ref[...].astype(F32) + b_ref[...].astype(F32)
        for j in range(3):
            g = g + o_ref[j].astype(F32)
        delta, nm, nv = _adamw_math(w_ref[...], g, m_ref[...], v_ref[...])
        g_ref[...] = g
        d_ref[...] = delta
        nm_ref[...] = nm
        nv_ref[...] = nv

    row = pl.BlockSpec((tr, C), lambda i, place_ref: (i, 0))
    return pl.pallas_call(
        body, name=name,
        grid_spec=pltpu.PrefetchScalarGridSpec(
            num_scalar_prefetch=1, grid=(R // tr,),
            in_specs=[row] * 3 + [pl.BlockSpec((None, tr, C), lambda i, place_ref: (place_ref[0], i, 0)),
                                  pl.BlockSpec((None, tr, C), lambda i, place_ref: (place_ref[1], i, 0)),
                                  pl.BlockSpec((3, tr, C), lambda i, place_ref: (0, i, 0))],
            out_specs=[row] * 4),
        out_shape=[jax.ShapeDtypeStruct((R, C), F32)] * 4,
        compiler_params=_params(("parallel",)),
    )(place, w, m, v, parts, sib, others)


def adamw_packed(w, g, m, v, name):
    R = w.shape[0]

    def body(w_ref, g_ref, m_ref, v_ref, d_ref, nm_ref, nv_ref):
        delta, nm, nv = _adamw_math(w_ref[...], g_ref[...], m_ref[...], v_ref[...])
        d_ref[...] = delta
        nm_ref[...] = nm
        nv_ref[...] = nv

    full = pl.BlockSpec((R, LANES), lambda i: (0, 0))
    return pl.pallas_call(
        body, name=name, grid=(1,),
        in_specs=[full] * 4, out_specs=[full] * 3,
        out_shape=[jax.ShapeDtypeStruct((R, LANES), F32)] * 3,
        compiler_params=_params(("arbitrary",)),
    )(w, g, m, v)


def _pack(arrays):
    flat = []
    sizes = []
    for a in arrays:
        f = a.reshape(-1).astype(F32)
        pad = (-f.shape[0]) % LANES
        if pad:
            f = jnp.concatenate([f, jnp.zeros((pad,), F32)])
        flat.append(f)
        sizes.append(f.shape[0])
    rows = sum(sizes) // LANES
    pad_rows = (-rows) % 8
    if pad_rows:
        flat.append(jnp.zeros((pad_rows * LANES,), F32))
    return jnp.concatenate(flat).reshape(-1, LANES), sizes


def _unpack(packed, sizes, shapes):
    flat = packed.reshape(-1)
    out = []
    off = 0
    for size, shape in zip(sizes, shapes):
        n = int(np.prod(shape))
        out.append(flat[off:off + n].reshape(shape))
        off += size
    return out


def _to_blocks(full, axis):
    if axis == 0:
        return full.reshape(N_DEV, full.shape[0] // N_DEV, full.shape[1])
    r, n = full.shape
    return full.reshape(r, N_DEV, n // N_DEV).transpose(1, 0, 2)


def _from_blocks(blocks, axis):
    if axis == 0:
        return blocks.reshape(blocks.shape[0] * blocks.shape[1], blocks.shape[2])
    return blocks.transpose(1, 0, 2).reshape(blocks.shape[1], blocks.shape[0] * blocks.shape[2])


def kernel(x, ln0_g, ln0_b, w_in, b_in, conv_w, w_a, w_b, w_o, b_o, ln1_g, ln1_b, w_up, b_up, ffn_conv_w, ffn_conv_b, w_down, b_down, ln2_g, ln2_b, loss_target, m_ln0_g, m_ln0_b, m_w_in, m_b_in, m_conv_w, m_w_a, m_w_b, m_w_o, m_b_o, m_ln1_g, m_ln1_b, m_w_up, m_b_up, m_ffn_conv_w, m_ffn_conv_b, m_w_down, m_b_down, m_ln2_g, m_ln2_b, v_ln0_g, v_ln0_b, v_w_in, v_b_in, v_conv_w, v_w_a, v_w_b, v_w_o, v_b_o, v_ln1_g, v_ln1_b, v_w_up, v_b_up, v_ffn_conv_w, v_ffn_conv_b, v_w_down, v_b_down, v_ln2_g, v_ln2_b):
    T, D = x.shape[1], x.shape[2]
    F = ffn_conv_b.shape[-1]
    xs = x.reshape(T, D)
    tgt = loss_target.reshape(T, D)
    dev = 4 * lax.axis_index("x") + 2 * lax.axis_index("y") + lax.axis_index("c")
    chip = 2 * lax.axis_index("x") + lax.axis_index("y")
    core = lax.axis_index("c")
    place = jnp.stack([dev, chip, core]).astype(jnp.int32)

    big = dict(w_in=(w_in[0], 1), w_a=(w_a[0], 0), w_b=(w_b[0], 1), w_o=(w_o[0], 0), w_up=(w_up[0], 1),
               w_down=(w_down[0], 0))
    names = list(big)
    shards = {k: big[k][0].astype(BF16) for k in names}
    ln0g, ln0b = ln0_g.reshape(1, D), ln0_b.reshape(1, D)
    h0, h0b, *rest = ln_fwd(xs, None, ln0g, ln0b, "ln0_fwd_gather_w_in", dilations=DILATIONS[1:],
                            gather=[shards["w_in"], conv_w[0], ffn_conv_w[0]])
    h0_res = [h0b] + [h.reshape(T, D) for h in rest[:2]]
    g_in, g_conv, g_fcw = rest[2:]
    full = {"w_in": _from_blocks(g_in, 1)}
    conv_full = _from_blocks(g_conv, 1)
    fcw_full = _from_blocks(g_fcw, 1)
    late_groups = (("w_a", "w_b", "w_o"), ("w_up", "w_down"))
    late_handles = []
    token = conv_full[:1, :1] * 0.0
    for n, keys in enumerate(late_groups):
        srcs = [shards[k] + token[0, 0].astype(BF16) for k in keys]
        handles, token = copies_start(srcs, [jax.ShapeDtypeStruct((N_DEV,) + s.shape, BF16) for s in srcs],
                                      _to_all_plan, N_DEV - 1, f"gather_late_{n}_start")
        late_handles.append(handles)

    def late_weights(n, after):
        _, lands = copies_wait(late_handles[n], _to_all_plan, after, f"gather_late_{n}_wait")
        for k, land in zip(late_groups[n], lands):
            full[k] = _from_blocks(lax.dynamic_update_index_in_dim(land, shards[k], dev, 0), big[k][1])

    o_q = 3 * D
    o_g = o_q + 3 * QKV_W
    w_pa, w_qkv, w_pg = full["w_in"][:, :o_q], full["w_in"][:, o_q:o_g], full["w_in"][:, o_g:]
    b_pa, b_qkv, b_pg = b_in[:, :o_q], b_in[:, o_q:o_g], b_in[:, o_g:]

    proj_a = mm_nn(h0b, w_pa, b_pa, ACT, "proj_conv", after=token)
    proj_g = mm_nn(h0b, w_pg, b_pg, ACT, "proj_gates")
    zero_d = jnp.zeros((1, D), F32)
    s_a = conv_a_fwd(proj_a, conv_full, "conv_a_fwd")
    late_weights(0, s_a)
    y_a = mm_nn(s_a, full["w_a"], zero_d, ACT, "branch_a_out")

    def group_cols(m, g):
        return jnp.concatenate([m[:, s * QKV_W + g * GROUP_W:s * QKV_W + (g + 1) * GROUP_W] for s in range(3)], 1)

    w_grp = [group_cols(w_qkv, g) for g in range(3)]
    qkvs, outs, lses = [], [], []
    for g, d in enumerate(DILATIONS):
        qkv = mm_nn(h0_res[g], w_grp[g], group_cols(b_qkv, g), BF16, f"proj_qkv_{g}").reshape(d, T // d, 3 * GROUP_W)
        o, l = att_fwd(qkv, g, f"att_fwd_{g}")
        qkvs.append(qkv)
        outs.append(o)
        lses.append(l)
    comb = combine_fwd(outs, lses, "combine_fwd")
    y_b = mm_nn(comb, full["w_b"], zero_d, ACT, "branch_b_out")
    z = gate_fwd(proj_g, y_a, y_b, "gate_fwd")
    h1, h1b, mix = ln_fwd(h0, ("nn", z, full["w_o"], b_o), ln1_g, ln1_b, "mix_out_ln1_fwd")
    late_weights(1, h1b)
    up, f_act = ffn_up_conv_f(h1b, full["w_up"], b_up, fcw_full, ffn_conv_b, "ffn_up_conv_f")

    dr2, dr2b, d_ln2_g, d_ln2_b, d_b_down, loss_part = ln_bwd(
        h1, ("nn", f_act, full["w_down"], b_down), ln2_g, ln2_b, None, None, tgt, "ffn_down_ln2_loss_bwd")
    dw_down, _ = mm_tn(f_act, dr2b, "dw_down")
    d_a, d_gate, cs_a, cs_gate, d_fcb, d_fcw = conv_f_bwd(dr2b, full["w_down"], up, fcw_full, ffn_conv_b,
                                                          "d_ffn_act_conv_f_bwd")
    dw_up_a, _ = mm_tn(h1b, d_a, "dw_up_a")
    dw_up_g, _ = mm_tn(h1b, d_gate, "dw_up_gate")
    dr1, dr1b, d_ln1_g, d_ln1_b, d_b_o, _ = ln_bwd(h0, mix, ln1_g, ln1_b, dr2, ("nt", [d_a, d_gate], full["w_up"]), None,
                                                   "d_h1_ln1_bwd")
    dw_o, _ = mm_tn(z, dr1b, "dw_o")
    dy_a, dy_b, dproj_g = gate_bwd(dr1b, full["w_o"], proj_g, y_a, y_b, "d_z_gate_bwd")
    dw_a, _ = mm_tn(s_a, dy_a, "dw_a")
    dproj_a, d_conv = conv_a_bwd(dy_a, full["w_a"], proj_a, conv_full, "d_s_a_conv_a_bwd")
    dw_b, _ = mm_tn(comb, dy_b, "dw_b")

    rs_mine, rs_sib, rs_handles = {}, {}, {}

    sib_handles = {}

    def to_sibling_start(keys, grads, tag):
        parts = [_to_blocks(grads[k], big[k][1]) for k in keys]
        handles, tok = copies_start(parts, [jax.ShapeDtypeStruct((4,) + p.shape[1:], BF16) for p in parts],
                                    _to_sibling_plan, 4, f"grads_to_sibling_{tag}_start")
        sib_handles[tag] = (keys, handles)
        return tok

    def to_chips_start(tag, after):
        keys, handles = sib_handles[tag]
        parts, from_sib = copies_wait(handles, _to_sibling_plan, after, f"grads_to_sibling_{tag}_wait")
        sums = [pair_add(a, b, place, f"chip_sum_{k}") for k, a, b in zip(keys, parts, from_sib)]
        handles, tok = copies_start(sums, [jax.ShapeDtypeStruct((3,) + s.shape[1:], BF16) for s in sums],
                                    _to_chips_plan, 3, f"grads_to_chips_{tag}_start")
        for k, a, b in zip(keys, parts, from_sib):
            rs_mine[k], rs_sib[k] = a, b
        rs_handles[tag] = (keys, handles)
        return tok

    tok_a = to_sibling_start(("w_a", "w_b", "w_o", "w_up", "w_down"),
                             dict(w_a=dw_a, w_b=dw_b, w_o=dw_o, w_up=jnp.concatenate([dw_up_a, dw_up_g], 1),
                                  w_down=dw_down), "a")
    dcomb = mm_nt(dy_b, full["w_b"], None, "d_comb", after=tok_a, out_dtype=ACT)
    dos, dms = combine_bwd(dcomb, outs, lses, "combine_bwd")
    tok_a = to_chips_start("a", dms[0])
    dw_grp, cs_grp, dqkvs = [], [], []
    for g, d in enumerate(DILATIONS):
        dq, dk, dv = att_bwd(qkvs[g], dos[g], lses[g], dms[g], g, f"att_bwd_{g}", after=tok_a if g == 0 else None)
        dqkv = [t.reshape(T, GROUP_W) for t in (dq, dk, dv)]
        dwg, csg = mm_tn(h0_res[g], dqkv, f"dw_in_qkv_{g}")
        dqkvs.append(dqkv)
        dw_grp.append(dwg)
        cs_grp.append(csg)
    dw_pa, cs_pa = mm_tn(h0b, dproj_a, "dw_in_conv")
    dw_pg, cs_pg = mm_tn(h0b, dproj_g, "dw_in_gates")

    def ungroup(parts):
        return jnp.concatenate([p[:, s * GROUP_W:(s + 1) * GROUP_W] for s in range(3) for p in parts], 1)

    db_in_parts = [cs_pa, ungroup(cs_grp), cs_pg]
    tok_b = to_sibling_start(("w_in",), dict(w_in=jnp.concatenate([dw_pa, ungroup(dw_grp), dw_pg], 1)), "b")
    dh0 = mm_nt(dproj_a, w_pa, None, "d_h0_conv", after=tok_b)
    tok_b = to_chips_start("b", dh0)
    dh0 = mm_nt(dproj_g, w_pg, dh0, "d_h0_gates", after=tok_b)
    dh0_res = [(mm_nt(dqkvs[g], w_grp[g], None, f"d_h0_qkv_{g}").reshape(d, T // d, D), d)
               for g, d in enumerate(DILATIONS) if g > 0]
    dx, _, d_ln0_g, d_ln0_b, _, _ = ln_bwd(xs, None, ln0g, ln0b, dr1, ("nt", dqkvs[0], w_grp[0]), None, "d_h0_ln0_bwd",
                                           by_residue=[(dh0.reshape(1, T, D), 1)] + dh0_res)

    small = [d_ln0_g, d_ln0_b, jnp.concatenate(db_in_parts, 1), d_conv, d_b_o, d_ln1_g, d_ln1_b,
             jnp.concatenate([cs_a, cs_gate], 1), d_fcw, d_fcb, d_b_down, d_ln2_g, d_ln2_b, loss_part]
    packed, sizes = _pack(small)
    total = all_sum_small(packed, "sum_small")
    (g_ln0_g, g_ln0_b, g_b_in, g_conv_full, g_b_o, g_ln1_g, g_ln1_b, g_b_up, g_fcw_full, g_fcb, g_b_down, g_ln2_g,
     g_ln2_b, loss) = _unpack(total, sizes, [a.shape for a in small])
    cw = conv_w.shape[-1]
    fw = ffn_conv_w.shape[-1]
    g_conv = lax.dynamic_slice_in_dim(g_conv_full, dev * cw, cw, 1)
    g_fcw = lax.dynamic_slice_in_dim(g_fcw_full, dev * fw, fw, 1)

    from_chips = {}
    for tag, (keys, handles) in rs_handles.items():
        _, lands = copies_wait(handles, _to_chips_plan, total, f"grads_to_chips_{tag}_wait")
        from_chips.update(zip(keys, lands))

    moments = dict(w_in=(m_w_in, v_w_in), w_a=(m_w_a, v_w_a), w_b=(m_w_b, v_w_b), w_o=(m_w_o, v_w_o),
                   w_up=(m_w_up, v_w_up), w_down=(m_w_down, v_w_down))
    res_big = {}
    for k in names:
        res_big[k] = adamw_sharded(big[k][0], moments[k][0][0], moments[k][1][0], rs_mine[k], rs_sib[k], from_chips[k],
                                   place, f"adamw_{k}")

    small_names = ["ln0_g", "ln0_b", "b_in", "conv_w", "b_o", "ln1_g", "ln1_b", "b_up", "ffn_conv_w", "ffn_conv_b",
                   "b_down", "ln2_g", "ln2_b"]
    small_w = [ln0_g, ln0_b, b_in, conv_w, b_o, ln1_g, ln1_b, b_up, ffn_conv_w, ffn_conv_b, b_down, ln2_g, ln2_b]
    small_m = [m_ln0_g, m_ln0_b, m_b_in, m_conv_w, m_b_o, m_ln1_g, m_ln1_b, m_b_up, m_ffn_conv_w, m_ffn_conv_b,
               m_b_down, m_ln2_g, m_ln2_b]
    small_v = [v_ln0_g, v_ln0_b, v_b_in, v_conv_w, v_b_o, v_ln1_g, v_ln1_b, v_b_up, v_ffn_conv_w, v_ffn_conv_b,
               v_b_down, v_ln2_g, v_ln2_b]
    small_g = [g_ln0_g, g_ln0_b, g_b_in, g_conv, g_b_o, g_ln1_g, g_ln1_b, g_b_up, g_fcw, g_fcb, g_b_down, g_ln2_g,
               g_ln2_b]
    shapes = [w.shape for w in small_w]
    small_g = [g.reshape(s) for g, s in zip(small_g, shapes)]
    pw, psz = _pack(small_w)
    pg, _ = _pack(small_g)
    pm, _ = _pack(small_m)
    pv, _ = _pack(small_v)
    pd, pnm, pnv = adamw_packed(pw, pg, pm, pv, "adamw_small")
    res_small = {k: (g, d_, m_, v_) for k, g, d_, m_, v_ in zip(
        small_names, small_g, _unpack(pd, psz, shapes), _unpack(pnm, psz, shapes), _unpack(pnv, psz, shapes))}

    order = ["ln0_g", "ln0_b", "w_in", "b_in", "conv_w", "w_a", "w_b", "w_o", "b_o", "ln1_g", "ln1_b", "w_up", "b_up",
             "ffn_conv_w", "ffn_conv_b", "w_down", "b_down", "ln2_g", "ln2_b"]

    def result(k, j):
        if k in res_big:
            return res_big[k][j][None]
        return res_small[k][j]

    out = [loss.reshape(()), dx.reshape(x.shape)]
    for j in range(4):
        out += [result(k, j) for k in order]
    return tuple(out)
```

```python
import math

import numpy as np
import jax
import jax.numpy as jnp
from jax import lax
from jax.experimental import pallas as pl
from jax.experimental.pallas import tpu as pltpu

F32 = jnp.float32
BF16 = jnp.bfloat16
ACT = BF16

N_DEV = 8
LN_EPS = 1e-5
ALPHA = (2.0 * 1) ** 0.25
HEAD_DIM = 64
GROUP_W = 512
QKV_W = 3 * GROUP_W
DILATIONS = (1, 4, 16)
RADIUS = 64
LANES = 128
HALO = 8
HALO_BF16 = 16
ATT_TQ = 128

ADAM_LR = 0.001
ADAM_B1 = 0.9
ADAM_B2 = 0.999
ADAM_EPS = 1e-08
ADAM_WD = 0.01
ADAM_STEP = 10

VMEM_LIMIT = 52 * 1024 * 1024
OUT_TILE_BYTES = 8 * 1024 * 1024
MAX_K_TALL_TILE = 4096
MESH = pl.DeviceIdType.MESH
NT_DIMS = (((1,), (1,)), ((), ()))
TN_DIMS = (((0,), (0,)), ((), ()))


def _pick(n, target, align=LANES):
    if n <= target:
        return n
    best = None
    for t in range(align, target + 1, align):
        if n % t == 0:
            best = t
    assert best is not None, (n, target, align)
    return best


def _params(sems=None):
    return pltpu.CompilerParams(dimension_semantics=sems, vmem_limit_bytes=VMEM_LIMIT)


def _alibi_slopes():
    n = 3 * 8
    return np.exp2(-8.0 * np.arange(1, n + 1, dtype=np.float64) / n).astype(np.float32).reshape(3, 8)


def _ln_stats(r):
    mu = jnp.mean(r, -1, keepdims=True)
    xc = r - mu
    var = jnp.mean(xc * xc, -1, keepdims=True)
    rstd = lax.rsqrt(var + LN_EPS)
    return xc, rstd


def _load_natural(ref, d, scr):
    if d == 1:
        return ref[0].astype(F32)
    n, C = ref.shape[1], ref.shape[2]
    for c in range(C // LANES):
        for r in range(d):
            scr[c, pl.ds(r, n, stride=d), :] = ref[r, :, c * LANES:(c + 1) * LANES].astype(F32)
    return jnp.concatenate([scr[c] for c in range(C // LANES)], axis=1)


def _store_by_residue(val, ref, d, scr):
    if d == 1:
        ref[0] = val.astype(ref.dtype)
        return
    n, C = ref.shape[1], ref.shape[2]
    for c in range(C // LANES):
        scr[c] = val[:, c * LANES:(c + 1) * LANES]
    for c in range(C // LANES):
        for r in range(d):
            ref[r, :, c * LANES:(c + 1) * LANES] = scr[c, pl.ds(r, n, stride=d), :].astype(ref.dtype)


def _residue_spec(tm, d, C):
    return pl.BlockSpec((d, tm // d, C), lambda i: (0, i, 0))


def _residue_scratch(tm, C):
    return pltpu.VMEM((C // LANES, tm, LANES), F32)


def ln_fwd(a, res, g, b, name, dilations=(), gather=()):
    T, D = a.shape
    res_mm = isinstance(res, tuple)
    tm = _pick(T, 512, 8)
    res_ins = list(res[1:]) if res_mm else ([] if res is None else [res])
    nd = len(dilations)
    ng = len(gather)
    n_in = 1 + len(res_ins) + 2
    last = T // tm - 1

    def body(*refs):
        a_ref = refs[0]
        r = a_ref[...]
        if res_mm:
            res_val = jnp.dot(refs[1][...], refs[2][...], preferred_element_type=F32) + refs[3][...]
            refs[-1 - n_scratch][...] = res_val
            r = ALPHA * r + res_val
        elif res_ins:
            r = ALPHA * r + refs[1][...]
        g_ref, b_ref = refs[n_in - 2], refs[n_in - 1]
        shard_refs = refs[n_in:n_in + ng]
        h_ref, hb_ref = refs[n_in + ng], refs[n_in + ng + 1]
        p_refs = refs[n_in + ng + 2:n_in + ng + 2 + nd]
        full_refs = refs[n_in + ng + 2 + nd:n_in + 2 * ng + 2 + nd]
        scratch = refs[len(refs) - n_scratch:]
        sems = scratch[len(scratch) - 3:] if ng else ()

        if ng:
            @pl.when(pl.program_id(0) == 0)
            def _():
                _gather_begin(shard_refs, full_refs, *sems)

        xc, rstd = _ln_stats(r)
        h = xc * rstd * g_ref[...] + b_ref[...]
        h_ref[...] = h
        hb_ref[...] = h.astype(BF16)
        for d, p_ref in zip(dilations, p_refs):
            _store_by_residue(h, p_ref, d, scratch[0])

        if ng:
            @pl.when(pl.program_id(0) == last)
            def _():
                _gather_finish(shard_refs, full_refs, *sems)

    row = pl.BlockSpec((tm, D), lambda i: (i, 0))
    vec = pl.BlockSpec((1, D), lambda i: (0, 0))
    hbm = pl.BlockSpec(memory_space=pl.ANY)
    if res_mm:
        res_specs = [pl.BlockSpec((tm, res[1].shape[1]), lambda i: (i, 0)), pl.BlockSpec(res[2].shape, lambda i: (0, 0)), vec]
    else:
        res_specs = [row] * len(res_ins)
    scratch_shapes = ([_residue_scratch(tm, D)] if nd else []) + (_gather_scratch(ng) if ng else [])
    n_scratch = len(scratch_shapes)
    ins = [a] + res_ins + [g, b] + list(gather)
    return pl.pallas_call(
        body, name=name, grid=(T // tm,),
        in_specs=[row] + res_specs + [vec, vec] + [hbm] * ng,
        out_specs=[row, row] + [_residue_spec(tm, d, D) for d in dilations] + [hbm] * ng + ([row] if res_mm else []),
        out_shape=[jax.ShapeDtypeStruct((T, D), F32), jax.ShapeDtypeStruct((T, D), BF16)]
        + [jax.ShapeDtypeStruct((d, T // d, D), BF16) for d in dilations]
        + [jax.ShapeDtypeStruct((N_DEV,) + s.shape, s.dtype) for s in gather]
        + ([jax.ShapeDtypeStruct((T, D), F32)] if res_mm else []),
        scratch_shapes=scratch_shapes,
        compiler_params=_params(("arbitrary",) if ng else ("parallel",)),
    )(*ins)


def ln_bwd(a, res, g, b, d1, d2, tgt, name, by_residue=()):
    T, D = a.shape
    wide_product = isinstance(d2, tuple) and sum(p.shape[1] for p in d2[1]) > MAX_K_TALL_TILE
    fused = isinstance(res, tuple) or isinstance(d2, tuple)
    tm = _pick(T, 512 if fused and not wide_product else 256, 8)
    loss_mode = tgt is not None
    nres = len(by_residue)
    row = pl.BlockSpec((tm, D), lambda i: (i, 0))
    vec = pl.BlockSpec((1, D), lambda i: (0, 0))
    one = pl.BlockSpec((1, 1), lambda i: (0, 0))

    def rows_of(x):
        return pl.BlockSpec((tm, x.shape[1]), lambda i: (i, 0))

    def whole(x):
        return pl.BlockSpec(x.shape, lambda i: (0, 0))

    ins, in_specs, slots = [], [], {}

    def operand(key, arrays, specs):
        slots[key] = (len(ins), len(arrays))
        ins.extend(arrays)
        in_specs.extend(specs)

    operand("a", [a], [row])
    if isinstance(res, tuple):
        _, x, w, bias = res
        operand("res_mm", [x, w, bias], [rows_of(x), whole(w), vec])
    elif res is not None:
        operand("res", [res], [row])
    operand("gb", [g, b], [vec, vec])
    if loss_mode:
        operand("tgt", [tgt], [row])
    else:
        operand("d1", [d1], [row])
        if isinstance(d2, tuple):
            _, pieces, w = d2
            operand("d2_mm", list(pieces) + [w], [rows_of(p) for p in pieces] + [whole(w)])
        else:
            operand("d2", [d2], [row])
    operand("by_residue", [e for e, _ in by_residue], [_residue_spec(tm, d, D) for _, d in by_residue])
    n_in = len(ins)

    def body(*refs):
        def get(key):
            first, count = slots[key]
            return refs[first:first + count]

        dr_ref, drb_ref, dg_ref, db_ref, ds_ref, loss_ref = refs[n_in:n_in + 6]
        i = pl.program_id(0)

        @pl.when(i == 0)
        def _():
            dg_ref[...] = jnp.zeros_like(dg_ref)
            db_ref[...] = jnp.zeros_like(db_ref)
            ds_ref[...] = jnp.zeros_like(ds_ref)
            loss_ref[...] = jnp.zeros_like(loss_ref)

        r = get("a")[0][...]
        if "res_mm" in slots:
            x_ref, w_ref, bias_ref = get("res_mm")
            r = ALPHA * r + (jnp.dot(x_ref[...], w_ref[...], preferred_element_type=F32) + bias_ref[...])
        elif "res" in slots:
            r = ALPHA * r + get("res")[0][...]
        g_ref, b_ref = get("gb")
        xc, rstd = _ln_stats(r)
        xhat = xc * rstd
        gam = g_ref[...]
        if loss_mode:
            err = xhat * gam + b_ref[...] - get("tgt")[0][...]
            dy = err * (1.0 / D)
            row_loss = jnp.mean(err * err, -1, keepdims=True)
            loss_ref[...] += 0.5 * jnp.sum(row_loss, 0, keepdims=True)
        else:
            if "d2_mm" in slots:
                *p_refs, w_ref = get("d2_mm")
                av = p_refs[0][...] if len(p_refs) == 1 else jnp.concatenate([p[...] for p in p_refs], axis=1)
                d2v = lax.dot_general(av, w_ref[...], NT_DIMS, preferred_element_type=F32)
            else:
                d2v = get("d2")[0][...]
            dy = ALPHA * get("d1")[0][...] + d2v
        for (_, d), e_ref in zip(by_residue, get("by_residue")):
            dy = dy + _load_natural(e_ref, d, refs[-1])
        dyg = dy * gam
        c1 = jnp.mean(dyg, -1, keepdims=True)
        c2 = jnp.mean(dyg * xhat, -1, keepdims=True)
        dr = rstd * (dyg - c1 - xhat * c2)
        dr_ref[...] = dr
        drb_ref[...] = dr.astype(BF16)
        dg_ref[...] += jnp.sum(dy * xhat, 0, keepdims=True)
        db_ref[...] += jnp.sum(dy, 0, keepdims=True)
        ds_ref[...] += jnp.sum(dr, 0, keepdims=True)

    return pl.pallas_call(
        body, name=name, grid=(T // tm,),
        in_specs=in_specs,
        out_specs=[row, row, vec, vec, vec, one],
        out_shape=[jax.ShapeDtypeStruct((T, D), F32), jax.ShapeDtypeStruct((T, D), BF16),
                   jax.ShapeDtypeStruct((1, D), F32), jax.ShapeDtypeStruct((1, D), F32),
                   jax.ShapeDtypeStruct((1, D), F32), jax.ShapeDtypeStruct((1, 1), F32)],
        scratch_shapes=[_residue_scratch(tm, D)] if nres else [],
        compiler_params=_params(("arbitrary",)),
    )(*ins)


_TOKEN_SPEC = pl.BlockSpec((8, LANES), lambda i: (0, 0))


def mm_nn(a, w, bias, out_dtype, name, after=None):
    M, K = a.shape
    N = w.shape[1]
    tm = _pick(M, max(256, min(1024, OUT_TILE_BYTES // (N * jnp.dtype(out_dtype).itemsize))), 8)
    tc = _pick(N, 512)

    def body(a_ref, w_ref, b_ref, *rest):
        o_ref = rest[-1]
        av = a_ref[...]
        for j in range(N // tc):
            cols = slice(j * tc, (j + 1) * tc)
            acc = jnp.dot(av, w_ref[:, cols], preferred_element_type=F32)
            o_ref[:, cols] = (acc + b_ref[:, cols]).astype(out_dtype)

    return pl.pallas_call(
        body, name=name, grid=(M // tm,),
        in_specs=[pl.BlockSpec((tm, K), lambda i: (i, 0)),
                  pl.BlockSpec((K, N), lambda i: (0, 0)),
                  pl.BlockSpec((1, N), lambda i: (0, 0))] + ([] if after is None else [_TOKEN_SPEC]),
        out_specs=pl.BlockSpec((tm, N), lambda i: (i, 0)),
        out_shape=jax.ShapeDtypeStruct((M, N), out_dtype),
        compiler_params=_params(("parallel",)),
    )(a, w, bias, *([] if after is None else [after]))


def mm_nt(a, w, acc_in, name, after=None, w_block=0, out_dtype=F32):
    pieces = list(a) if isinstance(a, (list, tuple)) else [a]
    M = pieces[0].shape[0]
    widths = [p.shape[1] for p in pieces]
    K = sum(widths)
    N = w.shape[0]
    tm = _pick(M, 1024, 8)
    tc = _pick(N, 512)
    has_acc = acc_in is not None
    n_a = len(pieces)

    def body(*refs):
        a_refs, w_ref = refs[:n_a], refs[n_a]
        c_ref = refs[n_a + 1] if has_acc else None
        o_ref = refs[-1]
        av = a_refs[0][...] if n_a == 1 else jnp.concatenate([r[...] for r in a_refs], axis=1)
        for j in range(N // tc):
            cols = slice(j * tc, (j + 1) * tc)
            acc = lax.dot_general(av, w_ref[cols, :], NT_DIMS, preferred_element_type=F32)
            if has_acc:
                acc = acc + c_ref[:, cols]
            o_ref[:, cols] = acc.astype(out_dtype)

    out_spec = pl.BlockSpec((tm, N), lambda i: (i, 0))
    in_specs = [pl.BlockSpec((tm, kw), lambda i: (i, 0)) for kw in widths]
    in_specs.append(pl.BlockSpec((N, K), lambda i: (0, w_block)))
    ins = pieces + [w]
    if has_acc:
        in_specs.append(out_spec)
        ins.append(acc_in)
    if after is not None:
        in_specs.append(_TOKEN_SPEC)
        ins.append(after)
    return pl.pallas_call(
        body, name=name, grid=(M // tm,),
        in_specs=in_specs, out_specs=out_spec,
        out_shape=jax.ShapeDtypeStruct((M, N), out_dtype),
        compiler_params=_params(("parallel",)),
    )(*ins)


def mm_tn(a, b, name, out_dtype=BF16):
    pieces = list(b) if isinstance(b, (list, tuple)) else [b]
    T, M = a.shape
    widths = [p.shape[1] for p in pieces]
    N = sum(widths)
    tk = _pick(T, 1024, 8)
    nk = T // tk
    tc = _pick(M, 256)
    n_b = len(pieces)

    def body(*refs):
        a_ref, b_refs = refs[0], refs[1:1 + n_b]
        o_ref, cs_ref, acc_ref = refs[1 + n_b:]
        k = pl.program_id(0)

        @pl.when(k == 0)
        def _():
            acc_ref[...] = jnp.zeros_like(acc_ref)
            cs_ref[...] = jnp.zeros_like(cs_ref)

        bv = b_refs[0][...] if n_b == 1 else jnp.concatenate([r[...] for r in b_refs], axis=1)
        cs_ref[...] += jnp.sum(bv.astype(F32), 0, keepdims=True)
        for mi in range(M // tc):
            rows = slice(mi * tc, (mi + 1) * tc)
            acc_ref[rows, :] += lax.dot_general(a_ref[:, rows], bv, TN_DIMS, preferred_element_type=F32)

        @pl.when(k == nk - 1)
        def _():
            o_ref[...] = acc_ref[...].astype(out_dtype)

    return pl.pallas_call(
        body, name=name, grid=(nk,),
        in_specs=[pl.BlockSpec((tk, M), lambda k: (k, 0))] + [pl.BlockSpec((tk, wd), lambda k: (k, 0)) for wd in widths],
        out_specs=[pl.BlockSpec((M, N), lambda k: (0, 0)), pl.BlockSpec((1, N), lambda k: (0, 0))],
        out_shape=[jax.ShapeDtypeStruct((M, N), out_dtype), jax.ShapeDtypeStruct((1, N), F32)],
        scratch_shapes=[pltpu.VMEM((M, N), F32)],
        compiler_params=_params(("arbitrary",)),
    )(a, *pieces)


def _ext_rows(prev_ref, main_ref, next_ref, i, tm, T, dtype=F32):
    before = jnp.where(i == 0, 0.0, prev_ref[...])
    after = jnp.where(i == T // tm - 1, 0.0, next_ref[...])
    return jnp.concatenate([before, main_ref[...], after], axis=0).astype(dtype)


def _prev_row(x):
    return pltpu.roll(x, 1, 0)


def _next_row(x):
    return pltpu.roll(x, x.shape[0] - 1, 0)


def _conv3(u, w_ref):
    return _prev_row(u) * w_ref[0:1, :] + u * w_ref[1:2, :] + _next_row(u) * w_ref[2:3, :]


def _main(x, tm, halo=HALO):
    return x[halo:halo + tm]


def _halo_specs(tm, tc, T, col, order, halo=HALO):
    r = tm // halo
    last = T // halo - 1
    if order == "ij":
        return (pl.BlockSpec((halo, tc), lambda i, j: (jnp.maximum(i * r - 1, 0), col(j))),
                pl.BlockSpec((tm, tc), lambda i, j: (i, col(j))),
                pl.BlockSpec((halo, tc), lambda i, j: (jnp.minimum((i + 1) * r, last), col(j))))
    return (pl.BlockSpec((halo, tc), lambda j, i: (jnp.maximum(i * r - 1, 0), col(j))),
            pl.BlockSpec((tm, tc), lambda j, i: (i, col(j))),
            pl.BlockSpec((halo, tc), lambda j, i: (jnp.minimum((i + 1) * r, last), col(j))))


def conv_a_fwd(proj_a, conv_w, name):
    T, D3 = proj_a.shape
    D = D3 // 3
    tm = _pick(T, 256, 8)

    def body(p_ref, m_ref, n_ref, w_ref, o_ref):
        i = pl.program_id(0)
        ext = _ext_rows(p_ref, m_ref, n_ref, i, tm, T)
        u = ext[:, D:2 * D] * ext[:, 2 * D:]
        cu = _conv3(u, w_ref)
        o_ref[...] = (m_ref[:, :D].astype(F32) * _main(cu, tm, HALO_BF16)).astype(BF16)

    prev, main, nxt = _halo_specs(tm, D3, T, lambda j: 0, "ij", HALO_BF16)
    return pl.pallas_call(
        body, name=name, grid=(T // tm, 1),
        in_specs=[prev, main, nxt, pl.BlockSpec((3, D), lambda i, j: (0, 0))],
        out_specs=pl.BlockSpec((tm, D), lambda i, j: (i, 0)),
        out_shape=jax.ShapeDtypeStruct((T, D), BF16),
        compiler_params=_params(("parallel", "arbitrary")),
    )(proj_a, proj_a, proj_a, conv_w)


def conv_a_bwd(dy_a, w_a, proj_a, conv_w, name):
    T, D3 = proj_a.shape
    D = D3 // 3
    tm = _pick(T, 256, 8)

    def body(dp_ref, dm_ref, dn_ref, wa_ref, p_ref, m_ref, n_ref, w_ref, o_ref, dw_ref):
        i = pl.program_id(0)

        @pl.when(i == 0)
        def _():
            dw_ref[...] = jnp.zeros_like(dw_ref)

        ext = _ext_rows(p_ref, m_ref, n_ref, i, tm, T)
        dsa = lax.dot_general(_ext_rows(dp_ref, dm_ref, dn_ref, i, tm, T, dtype=BF16), wa_ref[...], NT_DIMS,
                              preferred_element_type=F32)
        gb, gc, hin = ext[:, :D], ext[:, D:2 * D], ext[:, 2 * D:]
        u = gc * hin
        u_prev, u_next = _prev_row(u), _next_row(u)
        cu = u_prev * w_ref[0:1, :] + u * w_ref[1:2, :] + u_next * w_ref[2:3, :]
        dcu = dsa * gb
        du = _next_row(dcu) * w_ref[0:1, :] + dcu * w_ref[1:2, :] + _prev_row(dcu) * w_ref[2:3, :]
        h = HALO_BF16
        o_ref[:, :D] = _main(dsa * cu, tm, h).astype(BF16)
        o_ref[:, D:2 * D] = _main(du * hin, tm, h).astype(BF16)
        o_ref[:, 2 * D:] = _main(du * gc, tm, h).astype(BF16)
        dcu_m = _main(dcu, tm, h)
        dw_ref[0:1, :] += jnp.sum(dcu_m * _main(u_prev, tm, h), 0, keepdims=True)
        dw_ref[1:2, :] += jnp.sum(dcu_m * _main(u, tm, h), 0, keepdims=True)
        dw_ref[2:3, :] += jnp.sum(dcu_m * _main(u_next, tm, h), 0, keepdims=True)

    dprev, dmain, dnxt = _halo_specs(tm, dy_a.shape[1], T, lambda j: 0, "ij", HALO_BF16)
    prev, main, nxt = _halo_specs(tm, D3, T, lambda j: 0, "ij", HALO_BF16)
    return pl.pallas_call(
        body, name=name, grid=(T // tm, 1),
        in_specs=[dprev, dmain, dnxt, pl.BlockSpec(w_a.shape, lambda i, j: (0, 0)), prev, main, nxt,
                  pl.BlockSpec((3, D), lambda i, j: (0, 0))],
        out_specs=[pl.BlockSpec((tm, D3), lambda i, j: (i, 0)), pl.BlockSpec((3, D), lambda i, j: (0, 0))],
        out_shape=[jax.ShapeDtypeStruct((T, D3), BF16), jax.ShapeDtypeStruct((3, D), F32)],
        compiler_params=_params(("arbitrary", "arbitrary")),
    )(dy_a, dy_a, dy_a, w_a, proj_a, proj_a, proj_a, conv_w)


_INV_SQRT2 = 1.0 / math.sqrt(2.0)
_INV_SQRT_2PI = 1.0 / math.sqrt(2.0 * math.pi)


def ffn_up_conv_f(h, w_up, b_up, fcw, fcb, name):
    T, D = h.shape
    F = fcb.shape[1]
    tm = _pick(T, 256, 8)
    tc = _pick(F, 256)
    halo = HALO_BF16

    def body(hp_ref, hm_ref, hn_ref, w_ref, b_ref, cw_ref, cb_ref, up_ref, f_ref):
        i = pl.program_id(0)
        h_ext = _ext_rows(hp_ref, hm_ref, hn_ref, i, tm, T, dtype=BF16)
        h_main = hm_ref[...]
        rows = i * tm - halo + lax.broadcasted_iota(jnp.int32, (tm + 2 * halo, 1), 0)
        inside = (rows >= 0) & (rows < T)
        for c in range(F // tc):
            cols = slice(c * tc, (c + 1) * tc)
            gcols = slice(F + c * tc, F + (c + 1) * tc)
            a_ext = jnp.dot(h_ext, w_ref[:, cols], preferred_element_type=F32) + b_ref[:, cols]
            a_ext = jnp.where(inside, a_ext, 0.0)
            gate = jnp.dot(h_main, w_ref[:, gcols], preferred_element_type=F32) + b_ref[:, gcols]
            up_ref[:, cols] = _main(a_ext, tm, halo)
            up_ref[:, gcols] = gate
            ca = _main(_prev_row(a_ext) * cw_ref[0:1, cols] + a_ext * cw_ref[1:2, cols]
                       + _next_row(a_ext) * cw_ref[2:3, cols], tm, halo) + cb_ref[:, cols]
            gl = 0.5 * ca * (1.0 + lax.erf(ca * _INV_SQRT2))
            f_ref[:, cols] = (gl * gate).astype(BF16)

    prev, main, nxt = _halo_specs(tm, D, T, lambda j: 0, "ij", halo)
    whole = lambda x: pl.BlockSpec(x.shape, lambda i, j: (0, 0))
    return pl.pallas_call(
        body, name=name, grid=(T // tm, 1),
        in_specs=[prev, main, nxt, whole(w_up), whole(b_up), whole(fcw), whole(fcb)],
        out_specs=[pl.BlockSpec((tm, 2 * F), lambda i, j: (i, 0)), pl.BlockSpec((tm, F), lambda i, j: (i, 0))],
        out_shape=[jax.ShapeDtypeStruct((T, 2 * F), F32), jax.ShapeDtypeStruct((T, F), BF16)],
        compiler_params=_params(("parallel", "arbitrary")),
    )(h, h, h, w_up, b_up, fcw, fcb)


def conv_f_bwd(dy, w_down, up, fcw, fcb, name):
    T, F2 = up.shape
    F = F2 // 2
    D = dy.shape[1]
    tm = _pick(T, 256, 8)
    tc = _pick(F, 256)

    def body(yp_ref, ym_ref, yn_ref, wd_ref, up_ref, um_ref, un_ref, w_ref, b_ref,
             da_ref, dg_ref, csa_ref, csg_ref, dfb_ref, dfw_ref):
        i = pl.program_id(0)
        first, last = i == 0, i == T // tm - 1

        @pl.when(first)
        def _():
            csa_ref[...] = jnp.zeros_like(csa_ref)
            csg_ref[...] = jnp.zeros_like(csg_ref)
            dfb_ref[...] = jnp.zeros_like(dfb_ref)
            dfw_ref[...] = jnp.zeros_like(dfw_ref)

        def ext(cols):
            return jnp.concatenate([jnp.where(first, 0.0, up_ref[:, cols]), um_ref[:, cols],
                                    jnp.where(last, 0.0, un_ref[:, cols])], axis=0)

        dy_ext = _ext_rows(yp_ref, ym_ref, yn_ref, i, tm, T, dtype=BF16)
        for c in range(F // tc):
            cols = slice(c * tc, (c + 1) * tc)
            dfe = lax.dot_general(dy_ext, wd_ref[cols, :], NT_DIMS, preferred_element_type=F32)
            dfe = dfe[HALO_BF16 - HALO:HALO_BF16 + tm + HALO]
            a = ext(cols)
            gate = ext(slice(F + c * tc, F + (c + 1) * tc))
            a_prev, a_next = _prev_row(a), _next_row(a)
            ca = a_prev * w_ref[0:1, cols] + a * w_ref[1:2, cols] + a_next * w_ref[2:3, cols] + b_ref[:, cols]
            cdf = 0.5 * (1.0 + lax.erf(ca * _INV_SQRT2))
            gl = ca * cdf
            gp = cdf + ca * (jnp.exp(-0.5 * ca * ca) * _INV_SQRT_2PI)
            dgate = _main(dfe * gl, tm)
            dca = dfe * gate * gp
            da = _main(_next_row(dca) * w_ref[0:1, cols] + dca * w_ref[1:2, cols] + _prev_row(dca) * w_ref[2:3, cols],
                       tm)
            da_ref[:, cols] = da.astype(BF16)
            dg_ref[:, cols] = dgate.astype(BF16)
            csa_ref[:, cols] += jnp.sum(da, 0, keepdims=True)
            csg_ref[:, cols] += jnp.sum(dgate, 0, keepdims=True)
            dca_m = _main(dca, tm)
            dfb_ref[:, cols] += jnp.sum(dca_m, 0, keepdims=True)
            dfw_ref[0:1, cols] += jnp.sum(dca_m * _main(a_prev, tm), 0, keepdims=True)
            dfw_ref[1:2, cols] += jnp.sum(dca_m * _main(a, tm), 0, keepdims=True)
            dfw_ref[2:3, cols] += jnp.sum(dca_m * _main(a_next, tm), 0, keepdims=True)

    uprev, umain, unxt = _halo_specs(tm, F2, T, lambda j: 0, "ij")
    yprev, ymain, ynxt = _halo_specs(tm, D, T, lambda j: 0, "ij", HALO_BF16)
    whole = lambda shape: pl.BlockSpec(shape, lambda i, j: (0, 0))
    tile = pl.BlockSpec((tm, F), lambda i, j: (i, 0))
    return pl.pallas_call(
        body, name=name, grid=(T // tm, 1),
        in_specs=[yprev, ymain, ynxt, whole((F, D)), uprev, umain, unxt, whole((3, F)), whole((1, F))],
        out_specs=[tile, tile, whole((1, F)), whole((1, F)), whole((1, F)), whole((3, F))],
        out_shape=[jax.ShapeDtypeStruct((T, F), BF16), jax.ShapeDtypeStruct((T, F), BF16),
                   jax.ShapeDtypeStruct((1, F), F32), jax.ShapeDtypeStruct((1, F), F32),
                   jax.ShapeDtypeStruct((1, F), F32), jax.ShapeDtypeStruct((3, F), F32)],
        compiler_params=_params(("arbitrary", "arbitrary")),
    )(dy, dy, dy, w_down, up, up, up, fcw, fcb)


def gate_fwd(proj_g, y_a, y_b, name):
    T, D = y_a.shape
    tm = _pick(T, 512, 8)

    def body(g_ref, a_ref, b_ref, o_ref):
        sa = jax.nn.sigmoid(g_ref[:, :D].astype(F32))
        sb = jax.nn.sigmoid(g_ref[:, D:].astype(F32))
        o_ref[...] = (sa * a_ref[...].astype(F32) + sb * b_ref[...].astype(F32)).astype(BF16)

    row = pl.BlockSpec((tm, D), lambda i: (i, 0))
    return pl.pallas_call(
        body, name=name, grid=(T // tm,),
        in_specs=[pl.BlockSpec((tm, 2 * D), lambda i: (i, 0)), row, row],
        out_specs=row,
        out_shape=jax.ShapeDtypeStruct((T, D), BF16),
        compiler_params=_params(("parallel",)),
    )(proj_g, y_a, y_b)


def gate_bwd(dmix, w_o, proj_g, y_a, y_b, name):
    T, D = y_a.shape
    tm = _pick(T, 512, 8)

    def body(dz_ref, w_ref, g_ref, a_ref, b_ref, da_ref, db_ref, dg_ref):
        dzv = lax.dot_general(dz_ref[...], w_ref[...], NT_DIMS, preferred_element_type=F32)
        sa = jax.nn.sigmoid(g_ref[:, :D].astype(F32))
        sb = jax.nn.sigmoid(g_ref[:, D:].astype(F32))
        da_ref[...] = (dzv * sa).astype(BF16)
        db_ref[...] = (dzv * sb).astype(BF16)
        dg_ref[:, :D] = (dzv * a_ref[...].astype(F32) * (sa * (1.0 - sa))).astype(BF16)
        dg_ref[:, D:] = (dzv * b_ref[...].astype(F32) * (sb * (1.0 - sb))).astype(BF16)

    row = pl.BlockSpec((tm, D), lambda i: (i, 0))
    wide = pl.BlockSpec((tm, 2 * D), lambda i: (i, 0))
    return pl.pallas_call(
        body, name=name, grid=(T // tm,),
        in_specs=[pl.BlockSpec((tm, dmix.shape[1]), lambda i: (i, 0)), pl.BlockSpec(w_o.shape, lambda i: (0, 0)),
                  wide, row, row],
        out_specs=[row, row, wide],
        out_shape=[jax.ShapeDtypeStruct((T, D), BF16), jax.ShapeDtypeStruct((T, D), BF16),
                   jax.ShapeDtypeStruct((T, 2 * D), BF16)],
        compiler_params=_params(("parallel",)),
    )(dmix, w_o, proj_g, y_a, y_b)


ATT_WIN = ATT_TQ + 2 * RADIUS
ATT_STEP = 2048
FAR = 1e32


def _att_window(qs, L):
    ks = pl.multiple_of(jnp.clip(qs - RADIUS, 0, L - ATT_WIN), RADIUS)
    return ks, jnp.where(qs == 0, 0, jnp.where(qs == L - ATT_TQ, 2, 1))


def _fill_bias_tables(bias_ref, sl_ref, hp, d):
    col_row = (lax.broadcasted_iota(jnp.int32, (ATT_TQ, ATT_WIN), 1)
               - lax.broadcasted_iota(jnp.int32, (ATT_TQ, ATT_WIN), 0))
    for v in range(3):
        ad = jnp.abs(col_row - v * RADIUS)
        dist = jnp.where(ad <= RADIUS, (ad * d).astype(F32), FAR)
        bias_ref[v, 0:ATT_TQ, :] = sl_ref[hp * 2] * dist
        bias_ref[v, ATT_TQ:2 * ATT_TQ, :] = sl_ref[hp * 2 + 1] * dist


def _head_masks():
    lane = lax.broadcasted_iota(jnp.int32, (1, LANES), 1)
    return [lane < HEAD_DIM, lane >= HEAD_DIM]


def _stack_heads(x, masks):
    zero = jnp.zeros_like(x)
    return jnp.concatenate([jnp.where(masks[0], x, zero), jnp.where(masks[1], x, zero)], axis=0)


def _unstack_heads(x2, masks):
    n = x2.shape[0] // 2
    return jnp.where(masks[0], x2[:n], x2[n:])


def _att_step(L):
    step = min(ATT_STEP, L)
    assert L % step == 0 and step % ATT_TQ == 0 and L >= ATT_WIN
    return step


def _residues_per_step(d, L):
    rps = max(1, min(d, ATT_STEP // L))
    assert d % rps == 0
    return rps


def att_fwd(qkv, group, name):
    d, L, _ = qkv.shape
    step = _att_step(L)
    rps = _residues_per_step(d, L)
    cg = GROUP_W // LANES
    slopes = jnp.asarray(_alibi_slopes()[group])
    scale = HEAD_DIM ** -0.5

    def body(sl_ref, q_ref, k_ref, v_ref, o_ref, l_ref, bias_ref, s_ref, p_ref):
        hp = pl.program_id(1)
        i = pl.program_id(2)

        @pl.when(i == 0)
        def _():
            _fill_bias_tables(bias_ref, sl_ref, hp, d)

        masks = _head_masks()
        per = step // ATT_TQ
        tiles = [(rr, t) for rr in range(rps) for t in range(per)]
        windows = [_att_window(i * step + t * ATT_TQ, L) for t in range(per)]
        for n, (rr, t) in enumerate(tiles):
            rows = slice(t * ATT_TQ, (t + 1) * ATT_TQ)
            ks, table = windows[t]
            q2 = _stack_heads(q_ref[rr, rows, :] * scale, masks)
            kw = k_ref[rr, pl.ds(ks, ATT_WIN), :]
            s_ref[n] = lax.dot_general(q2, kw, NT_DIMS, preferred_element_type=F32) - bias_ref[table]
        for n, (rr, t) in enumerate(tiles):
            rows = slice(t * ATT_TQ, (t + 1) * ATT_TQ)
            s = s_ref[n]
            m = jnp.max(s, -1, keepdims=True)
            p = jnp.exp(s - m)
            den = jnp.sum(p, -1, keepdims=True)
            p_ref[n] = (p / den).astype(BF16)
            l_ref[rr, rows, :] = _unstack_heads(m + jnp.log(den), masks)
        for n, (rr, t) in enumerate(tiles):
            rows = slice(t * ATT_TQ, (t + 1) * ATT_TQ)
            vw = v_ref[rr, pl.ds(windows[t][0], ATT_WIN), :]
            o2 = jnp.dot(p_ref[n], vw, preferred_element_type=F32)
            o_ref[rr, rows, :] = _unstack_heads(o2, masks).astype(ACT)

    n_tiles = rps * step // ATT_TQ
    out_spec = pl.BlockSpec((rps, step, LANES), lambda r, hp, i: (r, i, hp))
    return pl.pallas_call(
        body, name=name, grid=(d // rps, cg, L // step),
        in_specs=[pl.BlockSpec(memory_space=pltpu.SMEM),
                  pl.BlockSpec((rps, step, LANES), lambda r, hp, i: (r, i, hp)),
                  pl.BlockSpec((rps, L, LANES), lambda r, hp, i: (r, 0, cg + hp)),
                  pl.BlockSpec((rps, L, LANES), lambda r, hp, i: (r, 0, 2 * cg + hp))],
        out_specs=[out_spec, out_spec],
        out_shape=[jax.ShapeDtypeStruct((d, L, GROUP_W), ACT), jax.ShapeDtypeStruct((d, L, GROUP_W), F32)],
        scratch_shapes=[pltpu.VMEM((3, 2 * ATT_TQ, ATT_WIN), F32),
                        pltpu.VMEM((n_tiles, 2 * ATT_TQ, ATT_WIN), F32),
                        pltpu.VMEM((n_tiles, 2 * ATT_TQ, ATT_WIN), BF16)],
        compiler_params=_params(("arbitrary", "arbitrary", "arbitrary")),
    )(slopes, qkv, qkv, qkv)


def att_bwd(qkv, do, lse, dmat, group, name, after=None):
    d, L, _ = qkv.shape
    step = _att_step(L)
    rps = _residues_per_step(d, L)
    nq = L // step
    cg = GROUP_W // LANES
    slopes = jnp.asarray(_alibi_slopes()[group])
    scale = HEAD_DIM ** -0.5

    def body(sl_ref, q_ref, k_ref, v_ref, do_ref, l_ref, dm_ref, *rest):
        dq_ref, dk_ref, dv_ref, dk_acc, dv_acc, bias_ref, s_ref, dp_ref, p_ref, ds_ref = rest[len(rest) - 10:]
        hp = pl.program_id(1)
        i = pl.program_id(2)

        @pl.when(i == 0)
        def _():
            dk_acc[...] = jnp.zeros_like(dk_acc)
            dv_acc[...] = jnp.zeros_like(dv_acc)
            _fill_bias_tables(bias_ref, sl_ref, hp, d)

        masks = _head_masks()

        def head_cols(x):
            return jnp.concatenate([jnp.max(jnp.where(hm, x, -jnp.inf), -1, keepdims=True) for hm in masks], axis=0)

        per = step // ATT_TQ
        tiles = [(rr, t) for rr in range(rps) for t in range(per)]
        windows = [_att_window(i * step + t * ATT_TQ, L) for t in range(per)]

        def stacked(ref, rr, t, factor=None):
            x = ref[rr, t * ATT_TQ:(t + 1) * ATT_TQ, :]
            return _stack_heads(x if factor is None else x * factor, masks)

        for n, (rr, t) in enumerate(tiles):
            ks, table = windows[t]
            q2 = stacked(q_ref, rr, t, scale)
            s_ref[n] = lax.dot_general(q2, k_ref[rr, pl.ds(ks, ATT_WIN), :], NT_DIMS,
                                       preferred_element_type=F32) - bias_ref[table]
            dp_ref[n] = lax.dot_general(stacked(do_ref, rr, t), v_ref[rr, pl.ds(ks, ATT_WIN), :], NT_DIMS,
                                        preferred_element_type=F32)
        for n, (rr, t) in enumerate(tiles):
            rows = slice(t * ATT_TQ, (t + 1) * ATT_TQ)
            p = jnp.exp(s_ref[n] - head_cols(l_ref[rr, rows, :]))
            p_ref[n] = p.astype(BF16)
            ds_ref[n] = (p * (dp_ref[n] - head_cols(dm_ref[rr, rows, :]))).astype(BF16)
        for n, (rr, t) in enumerate(tiles):
            rows = slice(t * ATT_TQ, (t + 1) * ATT_TQ)
            ks = windows[t][0]
            ds = ds_ref[n]
            dq2 = jnp.dot(ds, k_ref[rr, pl.ds(ks, ATT_WIN), :], preferred_element_type=F32)
            dq_ref[rr, rows, :] = (_unstack_heads(dq2, masks) * scale).astype(BF16)
            dk_acc[rr, pl.ds(ks, ATT_WIN), :] += lax.dot_general(ds, stacked(q_ref, rr, t, scale), TN_DIMS,
                                                                 preferred_element_type=F32)
            dv_acc[rr, pl.ds(ks, ATT_WIN), :] += lax.dot_general(p_ref[n], stacked(do_ref, rr, t), TN_DIMS,
                                                                 preferred_element_type=F32)

        @pl.when(i == nq - 1)
        def _():
            dk_ref[...] = dk_acc[...].astype(BF16)
            dv_ref[...] = dv_acc[...].astype(BF16)

    tile = pl.BlockSpec((rps, step, LANES), lambda r, hp, i: (r, i, hp))
    whole = pl.BlockSpec((rps, L, LANES), lambda r, hp, i: (r, 0, hp))
    return pl.pallas_call(
        body, name=name, grid=(d // rps, cg, nq),
        in_specs=[pl.BlockSpec(memory_space=pltpu.SMEM), tile,
                  pl.BlockSpec((rps, L, LANES), lambda r, hp, i: (r, 0, cg + hp)),
                  pl.BlockSpec((rps, L, LANES), lambda r, hp, i: (r, 0, 2 * cg + hp)),
                  tile, tile, tile] + ([] if after is None else [pl.BlockSpec((8, LANES), lambda r, hp, i: (0, 0))]),
        out_specs=[tile, whole, whole],
        out_shape=[jax.ShapeDtypeStruct((d, L, GROUP_W), BF16)] * 3,
        scratch_shapes=[pltpu.VMEM((rps, L, LANES), F32), pltpu.VMEM((rps, L, LANES), F32),
                        pltpu.VMEM((3, 2 * ATT_TQ, ATT_WIN), F32)]
        + [pltpu.VMEM((rps * step // ATT_TQ, 2 * ATT_TQ, ATT_WIN), dt) for dt in (F32, F32, BF16, BF16)],
        compiler_params=_params(("arbitrary", "arbitrary", "arbitrary")),
    )(slopes, qkv, qkv, qkv, do, lse, dmat, *([] if after is None else [after]))


def _group_weights(ls):
    m = jnp.maximum(jnp.maximum(ls[0], ls[1]), ls[2])
    es = [jnp.exp(l - m) for l in ls]
    tot = es[0] + es[1] + es[2]
    return [e / tot for e in es]


def combine_fwd(outs, lses, name):
    T = outs[0].shape[0] * outs[0].shape[1]
    tm = _pick(T, 512, 8)
    n_scr = 2 * (len(DILATIONS) - 1)

    def body(*refs):
        o_refs, l_refs, c_ref, scr = refs[:3], refs[3:6], refs[6], refs[7:]
        o = [_load_natural(o_refs[g], d, scr[g - 1] if g else None) for g, d in enumerate(DILATIONS)]
        l = [_load_natural(l_refs[g], d, scr[g + 1] if g else None) for g, d in enumerate(DILATIONS)]
        w = _group_weights(l)
        c_ref[...] = (w[0] * o[0] + w[1] * o[1] + w[2] * o[2]).astype(BF16)

    specs = [_residue_spec(tm, d, GROUP_W) for d in DILATIONS]
    return pl.pallas_call(
        body, name=name, grid=(T // tm,),
        in_specs=specs + specs, out_specs=pl.BlockSpec((tm, GROUP_W), lambda i: (i, 0)),
        out_shape=jax.ShapeDtypeStruct((T, GROUP_W), BF16),
        scratch_shapes=[_residue_scratch(tm, GROUP_W)] * n_scr,
        compiler_params=_params(("parallel",)),
    )(*outs, *lses)


def combine_bwd(dcomb, outs, lses, name):
    T = dcomb.shape[0]
    tm = _pick(T, 256, 8)
    head = np.arange(GROUP_W) // HEAD_DIM
    seg = jnp.asarray((head[:, None] == head[None, :]).astype(np.float32)).astype(BF16)
    ng = len(DILATIONS)
    n_scr = 4 * (ng - 1)

    def body(*refs):
        dc_ref, o_refs, l_refs, e_ref = refs[0], refs[1:1 + ng], refs[1 + ng:1 + 2 * ng], refs[1 + 2 * ng]
        do_refs, dm_refs = refs[2 + 2 * ng:2 + 3 * ng], refs[2 + 3 * ng:2 + 4 * ng]
        scr = refs[2 + 4 * ng:]
        o = [_load_natural(o_refs[g], d, scr[4 * (g - 1)] if g else None) for g, d in enumerate(DILATIONS)]
        l = [_load_natural(l_refs[g], d, scr[4 * (g - 1) + 1] if g else None) for g, d in enumerate(DILATIONS)]
        w = _group_weights(l)
        dc = dc_ref[...].astype(F32)
        e = e_ref[...]
        prod = dc * (w[0] * o[0] + w[1] * o[1] + w[2] * o[2])
        tot = jnp.zeros_like(dc)
        for _ in range(3):
            part = prod.astype(BF16)
            tot = tot + jnp.dot(part, e, preferred_element_type=F32)
            prod = prod - part.astype(F32)
        for g, d in enumerate(DILATIONS):
            _store_by_residue(w[g] * dc, do_refs[g], d, scr[4 * (g - 1) + 2] if g else None)
            _store_by_residue(w[g] * tot, dm_refs[g], d, scr[4 * (g - 1) + 3] if g else None)

    specs = [_residue_spec(tm, d, GROUP_W) for d in DILATIONS]
    res = pl.pallas_call(
        body, name=name, grid=(T // tm,),
        in_specs=[pl.BlockSpec((tm, GROUP_W), lambda i: (i, 0))] + specs + specs
        + [pl.BlockSpec((GROUP_W, GROUP_W), lambda i: (0, 0))],
        out_specs=specs + specs,
        out_shape=[jax.ShapeDtypeStruct(o.shape, BF16) for o in outs] + [jax.ShapeDtypeStruct(o.shape, F32) for o in outs],
        scratch_shapes=[_residue_scratch(tm, GROUP_W)] * n_scr,
        compiler_params=_params(("parallel",)),
    )(dcomb, *outs, *lses, seg)
    return res[:ng], res[ng:]


def _position():
    return lax.axis_index("x"), lax.axis_index("y"), lax.axis_index("c")


def _other_chips(x, y):
    return [(1 - x, y), (x, 1 - y), (1 - x, 1 - y)]


def _remote(src, dst, send_sems, recv_sems, k, to):
    return pltpu.make_async_remote_copy(src_ref=src, dst_ref=dst, send_sem=send_sems.at[k], recv_sem=recv_sems.at[k],
                                        device_id=to, device_id_type=MESH)


GATHER_SEMS = 10
SPLIT_ROWS = 32


def _gather_plan(ins, outs, send_sems, recv_sems, local_sems):
    x, y, c = _position()
    sibling = (x, y, 1 - c)
    nbr_x, nbr_y, diag = (1 - x, y, c), (x, 1 - y, c), (1 - x, 1 - y, c)
    local, begin, stages, last = [], [], [], []
    for a in range(len(ins)):
        k0 = GATHER_SEMS * a
        rows = ins[a].shape[0]
        half = rows // 2

        def block(dev):
            return outs[a].at[4 * dev[0] + 2 * dev[1] + dev[2]]

        def part(ref, h):
            return ref.at[pl.ds(h * half, half)]

        def copy(k, src, dst, to):
            return _remote(src, dst, send_sems, recv_sems, k0 + k, to)

        me = (x, y, c)
        local.append(pltpu.make_async_copy(ins[a], block(me), local_sems.at[a]))
        begin.append(copy(0, ins[a], block(me), sibling))
        pass_on = [copy(7 + j, block(dev), block(dev), sibling) for j, dev in enumerate((nbr_x, nbr_y, diag))]
        if rows >= SPLIT_ROWS and rows % SPLIT_ROWS == 0:
            for h in range(2):
                begin.append(copy(1 + h, part(ins[a], h), part(block(me), h), nbr_x))
                begin.append(copy(3 + h, part(ins[a], h), part(block(me), h), nbr_y))
            from_x = [copy(1 + h, part(block(nbr_x), h), part(block(nbr_x), h), sibling) for h in range(2)]
            from_y = [copy(3 + h, part(block(nbr_y), h), part(block(nbr_y), h), sibling) for h in range(2)]
            fwd_0 = copy(5, part(block(nbr_x), 0), part(block(nbr_x), 0), nbr_y)
            fwd_1 = copy(6, part(block(nbr_y), 1), part(block(nbr_y), 1), nbr_x)
            got_0 = copy(5, part(block(diag), 0), part(block(diag), 0), sibling)
            got_1 = copy(6, part(block(diag), 1), part(block(diag), 1), sibling)
            stages.append(([from_x[0]], [fwd_0]))
            stages.append(([from_y[1]], [fwd_1]))
            stages.append(([from_x[1]], [pass_on[0]]))
            stages.append(([from_y[0]], [pass_on[1]]))
            stages.append(([got_0, got_1], [pass_on[2]]))
        else:
            for j, dev in enumerate((nbr_x, nbr_y, diag)):
                begin.append(copy(1 + 2 * j, ins[a], block(me), dev))
                stages.append(([copy(1 + 2 * j, block(dev), block(dev), sibling)], [pass_on[j]]))
        other = (x, y, 1 - c)
        last.append(copy(0, block(other), block(other), sibling))
        for j, dev in enumerate((nbr_x, nbr_y, diag)):
            theirs = (dev[0], dev[1], 1 - c)
            last.append(copy(7 + j, block(theirs), block(theirs), sibling))
    return local, begin, stages, last


def _gather_begin(ins, outs, send_sems, recv_sems, local_sems):
    local, begin, _, _ = _gather_plan(ins, outs, send_sems, recv_sems, local_sems)
    for cp in local + begin:
        cp.start()


def _gather_finish(ins, outs, send_sems, recv_sems, local_sems):
    local, begin, stages, last = _gather_plan(ins, outs, send_sems, recv_sems, local_sems)
    started = []
    for arrivals, onward in stages:
        for cp in arrivals:
            cp.wait_recv()
        for cp in onward:
            cp.start()
            started.append(cp)
    for cp in last:
        cp.wait_recv()
    for cp in begin + started:
        cp.wait_send()
    for cp in local:
        cp.wait()


def _gather_scratch(n):
    return [pltpu.SemaphoreType.DMA((GATHER_SEMS * n,)), pltpu.SemaphoreType.DMA((GATHER_SEMS * n,)),
            pltpu.SemaphoreType.DMA((n,))]


_HBM = pl.BlockSpec(memory_space=pltpu.HBM)
_SEM = pl.BlockSpec(memory_space=pltpu.SEMAPHORE)
_DATAFLOW = pltpu.SideEffectType.DATAFLOW_SIDE_EFFECTING


def _to_all_plan(srcs, lands, send_sems, recv_sems):
    x, y, c = _position()
    me = 4 * x + 2 * y + c
    copies = []
    for a in range(len(srcs)):
        for k in range(1, N_DEV):
            fx, fy, fc = (k >> 2) & 1, (k >> 1) & 1, k & 1
            to = (1 - x if fx else x, 1 - y if fy else y, 1 - c if fc else c)
            copies.append(_remote(srcs[a], lands[a].at[me], send_sems, recv_sems, (N_DEV - 1) * a + k - 1, to))
    return copies


def _to_sibling_plan(srcs, lands, send_sems, recv_sems):
    x, y, c = _position()
    copies = []
    for a in range(len(srcs)):
        for q in range(4):
            copies.append(_remote(srcs[a].at[2 * q + (1 - c)], lands[a].at[q], send_sems, recv_sems, 4 * a + q,
                                  (x, y, 1 - c)))
    return copies


def _to_chips_plan(srcs, lands, send_sems, recv_sems):
    x, y, c = _position()
    copies = []
    for a in range(len(srcs)):
        for j, (cx, cy) in enumerate(_other_chips(x, y)):
            copies.append(_remote(srcs[a].at[2 * cx + cy], lands[a].at[j], send_sems, recv_sems, 3 * a + j, (cx, cy, c)))
    return copies


def copies_start(srcs, land_shapes, plan, per_array, name):
    n = len(srcs)
    n_sem = per_array * n
    lands = [lax.empty(s.shape, s.dtype) for s in land_shapes]

    def body(*refs):
        src_refs, land_refs = refs[:n], refs[n:2 * n]
        send_sems, recv_sems = refs[2 * n], refs[2 * n + 1]
        token = refs[-1]
        for cp in plan(src_refs, land_refs, send_sems, recv_sems):
            cp.start()
        token[...] = jnp.zeros_like(token)

    out = pl.pallas_call(
        body, name=name,
        out_shape=(pltpu.SemaphoreType.DMA((n_sem,)), pltpu.SemaphoreType.DMA((n_sem,)))
        + tuple(pltpu.HBM(s.shape, s.dtype) for s in srcs)
        + tuple(pltpu.HBM(s.shape, s.dtype) for s in land_shapes)
        + (jax.ShapeDtypeStruct((8, LANES), F32),),
        in_specs=[_HBM] * (2 * n),
        out_specs=(_SEM, _SEM) + (_HBM,) * (2 * n) + (pl.BlockSpec(memory_space=pltpu.VMEM),),
        input_output_aliases={i: 2 + i for i in range(2 * n)},
        compiler_params=pltpu.CompilerParams(has_side_effects=_DATAFLOW),
    )(*[pltpu.with_memory_space_constraint(s, pltpu.HBM) for s in srcs],
      *[pltpu.with_memory_space_constraint(l, pltpu.HBM) for l in lands])
    return out[:-1], out[-1]


def copies_wait(handles, plan, after, name):
    send_sems, recv_sems = handles[0], handles[1]
    n = (len(handles) - 2) // 2
    thru = handles[2:]

    def body(*refs):
        src_refs, land_refs = refs[:n], refs[n:2 * n]
        send_sems, recv_sems = refs[2 * n], refs[2 * n + 1]
        copies = plan(src_refs, land_refs, send_sems, recv_sems)
        for cp in copies:
            cp.wait_recv()
        for cp in copies:
            cp.wait_send()

    out = pl.pallas_call(
        body, name=name,
        out_shape=tuple(pltpu.HBM(t.shape, t.dtype) for t in thru),
        in_specs=[_HBM] * (2 * n) + [_SEM, _SEM, pl.BlockSpec(memory_space=pl.ANY)],
        out_specs=(_HBM,) * (2 * n),
        input_output_aliases={i: i for i in range(2 * n)},
        compiler_params=pltpu.CompilerParams(has_side_effects=_DATAFLOW),
    )(*thru, send_sems, recv_sems, after)
    return out[:n], out[n:]


def all_sum_small(vec, name):
    R = vec.shape[0]

    def body(v_ref, tot_ref, all_ref, send_sems, recv_sems):
        x, y, c = _position()
        me = 4 * x + 2 * y + c
        all_ref[me] = v_ref[...]
        copies = []
        for k in range(1, N_DEV):
            fx, fy, fc = (k >> 2) & 1, (k >> 1) & 1, k & 1
            to = (1 - x if fx else x, 1 - y if fy else y, 1 - c if fc else c)
            cp = _remote(v_ref, all_ref.at[me], send_sems, recv_sems, k - 1, to)
            cp.start()
            copies.append(cp)
        for cp in copies:
            cp.wait_recv()
        for cp in copies:
            cp.wait_send()
        tot = all_ref[0]
        for j in range(1, N_DEV):
            tot = tot + all_ref[j]
        tot_ref[...] = tot

    vmem = pl.BlockSpec(memory_space=pltpu.VMEM)
    return pl.pallas_call(
        body, name=name,
        in_specs=[vmem], out_specs=vmem,
        out_shape=jax.ShapeDtypeStruct((R, LANES), F32),
        scratch_shapes=[pltpu.VMEM((N_DEV, R, LANES), F32),
                        pltpu.SemaphoreType.DMA((N_DEV - 1,)), pltpu.SemaphoreType.DMA((N_DEV - 1,))],
        compiler_params=pltpu.CompilerParams(vmem_limit_bytes=VMEM_LIMIT),
    )(vec)


def pair_add(parts, theirs, place, name):
    _, R, C = theirs.shape
    tr = _pick(R, 1024, 8)

    def body(place_ref, a_ref, b_ref, o_ref):
        o_ref[...] = (a_ref[...].astype(F32) + b_ref[...].astype(F32)).astype(BF16)

    blk = pl.BlockSpec((None, tr, C), lambda q, i, place_ref: (q, i, 0))
    return pl.pallas_call(
        body, name=name,
        grid_spec=pltpu.PrefetchScalarGridSpec(
            num_scalar_prefetch=1, grid=(4, R // tr),
            in_specs=[pl.BlockSpec((None, tr, C), lambda q, i, place_ref: (2 * q + place_ref[2], i, 0)), blk],
            out_specs=blk),
        out_shape=jax.ShapeDtypeStruct(theirs.shape, BF16),
        compiler_params=_params(("parallel", "parallel")),
    )(place, parts, theirs)


def _adamw_math(w, g, m, v):
    m = ADAM_B1 * m + (1.0 - ADAM_B1) * g
    v = ADAM_B2 * v + (1.0 - ADAM_B2) * jnp.square(g)
    m_hat = m / (1.0 - ADAM_B1 ** ADAM_STEP)
    v_hat = v / (1.0 - ADAM_B2 ** ADAM_STEP)
    delta = -ADAM_LR * (m_hat / (jnp.sqrt(v_hat) + ADAM_EPS) + ADAM_WD * w)
    return delta, m, v


def adamw_sharded(w, m, v, parts, sib, others, place, name):
    R, C = w.shape
    tr = _pick(R, 256, 8)

    def body(place_ref, w_ref, m_ref, v_ref, a_ref, b_ref, o_ref, g_ref, d_ref, nm_ref, nv_ref):
        g = a_ref[...].astype(F32) + b_ref[...].astype(F32)
        for j in range(3):
            g = g + o_ref[j].astype(F32)
        delta, nm, nv = _adamw_math(w_ref[...], g, m_ref[...], v_ref[...])
        g_ref[...] = g
        d_ref[...] = delta
        nm_ref[...] = nm
        nv_ref[...] = nv

    row = pl.BlockSpec((tr, C), lambda i, place_ref: (i, 0))
    return pl.pallas_call(
        body, name=name,
        grid_spec=pltpu.PrefetchScalarGridSpec(
            num_scalar_prefetch=1, grid=(R // tr,),
            in_specs=[row] * 3 + [pl.BlockSpec((None, tr, C), lambda i, place_ref: (place_ref[0], i, 0)),
                                  pl.BlockSpec((None, tr, C), lambda i, place_ref: (place_ref[1], i, 0)),
                                  pl.BlockSpec((3, tr, C), lambda i, place_ref: (0, i, 0))],
            out_specs=[row] * 4),
        out_shape=[jax.ShapeDtypeStruct((R, C), F32)] * 4,
        compiler_params=_params(("parallel",)),
    )(place, w, m, v, parts, sib, others)


def adamw_packed(w, g, m, v, name):
    R = w.shape[0]

    def body(w_ref, g_ref, m_ref, v_ref, d_ref, nm_ref, nv_ref):
        delta, nm, nv = _adamw_math(w_ref[...], g_ref[...], m_ref[...], v_ref[...])
        d_ref[...] = delta
        nm_ref[...] = nm
        nv_ref[...] = nv

    full = pl.BlockSpec((R, LANES), lambda i: (0, 0))
    return pl.pallas_call(
        body, name=name, grid=(1,),
        in_specs=[full] * 4, out_specs=[full] * 3,
        out_shape=[jax.ShapeDtypeStruct((R, LANES), F32)] * 3,
        compiler_params=_params(("arbitrary",)),
    )(w, g, m, v)


def _pack(arrays):
    flat = []
    sizes = []
    for a in arrays:
        f = a.reshape(-1).astype(F32)
        pad = (-f.shape[0]) % LANES
        if pad:
            f = jnp.concatenate([f, jnp.zeros((pad,), F32)])
        flat.append(f)
        sizes.append(f.shape[0])
    rows = sum(sizes) // LANES
    pad_rows = (-rows) % 8
    if pad_rows:
        flat.append(jnp.zeros((pad_rows * LANES,), F32))
    return jnp.concatenate(flat).reshape(-1, LANES), sizes


def _unpack(packed, sizes, shapes):
    flat = packed.reshape(-1)
    out = []
    off = 0
    for size, shape in zip(sizes, shapes):
        n = int(np.prod(shape))
        out.append(flat[off:off + n].reshape(shape))
        off += size
    return out


def _to_blocks(full, axis):
    if axis == 0:
        return full.reshape(N_DEV, full.shape[0] // N_DEV, full.shape[1])
    r, n = full.shape
    return full.reshape(r, N_DEV, n // N_DEV).transpose(1, 0, 2)


def _from_blocks(blocks, axis):
    if axis == 0:
        return blocks.reshape(blocks.shape[0] * blocks.shape[1], blocks.shape[2])
    return blocks.transpose(1, 0, 2).reshape(blocks.shape[1], blocks.shape[0] * blocks.shape[2])


def kernel(x, ln0_g, ln0_b, w_in, b_in, conv_w, w_a, w_b, w_o, b_o, ln1_g, ln1_b, w_up, b_up, ffn_conv_w, ffn_conv_b, w_down, b_down, ln2_g, ln2_b, loss_target, m_ln0_g, m_ln0_b, m_w_in, m_b_in, m_conv_w, m_w_a, m_w_b, m_w_o, m_b_o, m_ln1_g, m_ln1_b, m_w_up, m_b_up, m_ffn_conv_w, m_ffn_conv_b, m_w_down, m_b_down, m_ln2_g, m_ln2_b, v_ln0_g, v_ln0_b, v_w_in, v_b_in, v_conv_w, v_w_a, v_w_b, v_w_o, v_b_o, v_ln1_g, v_ln1_b, v_w_up, v_b_up, v_ffn_conv_w, v_ffn_conv_b, v_w_down, v_b_down, v_ln2_g, v_ln2_b):
    T, D = x.shape[1], x.shape[2]
    F = ffn_conv_b.shape[-1]
    xs = x.reshape(T, D)
    tgt = loss_target.reshape(T, D)
    dev = 4 * lax.axis_index("x") + 2 * lax.axis_index("y") + lax.axis_index("c")
    chip = 2 * lax.axis_index("x") + lax.axis_index("y")
    core = lax.axis_index("c")
    place = jnp.stack([dev, chip, core]).astype(jnp.int32)

    big = dict(w_in=(w_in[0], 1), w_a=(w_a[0], 0), w_b=(w_b[0], 1), w_o=(w_o[0], 0), w_up=(w_up[0], 1),
               w_down=(w_down[0], 0))
    names = list(big)
    shards = {k: big[k][0].astype(BF16) for k in names}
    ln0g, ln0b = ln0_g.reshape(1, D), ln0_b.reshape(1, D)
    h0, h0b, *rest = ln_fwd(xs, None, ln0g, ln0b, "ln0_fwd_gather_w_in", dilations=DILATIONS[1:],
                            gather=[shards["w_in"], conv_w[0], ffn_conv_w[0]])
    h0_res = [h0b] + [h.reshape(T, D) for h in rest[:2]]
    g_in, g_conv, g_fcw = rest[2:]
    full = {"w_in": _from_blocks(g_in, 1)}
    conv_full = _from_blocks(g_conv, 1)
    fcw_full = _from_blocks(g_fcw, 1)
    late_groups = (("w_a", "w_b", "w_o"), ("w_up", "w_down"))
    late_handles = []
    token = conv_full[:1, :1] * 0.0
    for n, keys in enumerate(late_groups):
        srcs = [shards[k] + token[0, 0].astype(BF16) for k in keys]
        handles, token = copies_start(srcs, [jax.ShapeDtypeStruct((N_DEV,) + s.shape, BF16) for s in srcs],
                                      _to_all_plan, N_DEV - 1, f"gather_late_{n}_start")
        late_handles.append(handles)

    def late_weights(n, after):
        _, lands = copies_wait(late_handles[n], _to_all_plan, after, f"gather_late_{n}_wait")
        for k, land in zip(late_groups[n], lands):
            full[k] = _from_blocks(lax.dynamic_update_index_in_dim(land, shards[k], dev, 0), big[k][1])

    o_q = 3 * D
    o_g = o_q + 3 * QKV_W
    w_pa, w_qkv, w_pg = full["w_in"][:, :o_q], full["w_in"][:, o_q:o_g], full["w_in"][:, o_g:]
    b_pa, b_qkv, b_pg = b_in[:, :o_q], b_in[:, o_q:o_g], b_in[:, o_g:]

    proj_a = mm_nn(h0b, w_pa, b_pa, ACT, "proj_conv", after=token)
    proj_g = mm_nn(h0b, w_pg, b_pg, ACT, "proj_gates")
    zero_d = jnp.zeros((1, D), F32)
    s_a = conv_a_fwd(proj_a, conv_full, "conv_a_fwd")
    late_weights(0, s_a)
    y_a = mm_nn(s_a, full["w_a"], zero_d, ACT, "branch_a_out")

    def group_cols(m, g):
        return jnp.concatenate([m[:, s * QKV_W + g * GROUP_W:s * QKV_W + (g + 1) * GROUP_W] for s in range(3)], 1)

    w_grp = [group_cols(w_qkv, g) for g in range(3)]
    qkvs, outs, lses = [], [], []
    for g, d in enumerate(DILATIONS):
        qkv = mm_nn(h0_res[g], w_grp[g], group_cols(b_qkv, g), BF16, f"proj_qkv_{g}").reshape(d, T // d, 3 * GROUP_W)
        o, l = att_fwd(qkv, g, f"att_fwd_{g}")
        qkvs.append(qkv)
        outs.append(o)
        lses.append(l)
    comb = combine_fwd(outs, lses, "combine_fwd")
    y_b = mm_nn(comb, full["w_b"], zero_d, ACT, "branch_b_out")
    z = gate_fwd(proj_g, y_a, y_b, "gate_fwd")
    h1, h1b, mix = ln_fwd(h0, ("nn", z, full["w_o"], b_o), ln1_g, ln1_b, "mix_out_ln1_fwd")
    late_weights(1, h1b)
    up, f_act = ffn_up_conv_f(h1b, full["w_up"], b_up, fcw_full, ffn_conv_b, "ffn_up_conv_f")

    dr2, dr2b, d_ln2_g, d_ln2_b, d_b_down, loss_part = ln_bwd(
        h1, ("nn", f_act, full["w_down"], b_down), ln2_g, ln2_b, None, None, tgt, "ffn_down_ln2_loss_bwd")
    dw_down, _ = mm_tn(f_act, dr2b, "dw_down")
    d_a, d_gate, cs_a, cs_gate, d_fcb, d_fcw = conv_f_bwd(dr2b, full["w_down"], up, fcw_full, ffn_conv_b,
                                                          "d_ffn_act_conv_f_bwd")
    dw_up_a, _ = mm_tn(h1b, d_a, "dw_up_a")
    dw_up_g, _ = mm_tn(h1b, d_gate, "dw_up_gate")
    dr1, dr1b, d_ln1_g, d_ln1_b, d_b_o, _ = ln_bwd(h0, mix, ln1_g, ln1_b, dr2, ("nt", [d_a, d_gate], full["w_up"]), None,
                                                   "d_h1_ln1_bwd")
    dw_o, _ = mm_tn(z, dr1b, "dw_o")
    dy_a, dy_b, dproj_g = gate_bwd(dr1b, full["w_o"], proj_g, y_a, y_b, "d_z_gate_bwd")
    dw_a, _ = mm_tn(s_a, dy_a, "dw_a")
    dproj_a, d_conv = conv_a_bwd(dy_a, full["w_a"], proj_a, conv_full, "d_s_a_conv_a_bwd")
    dw_b, _ = mm_tn(comb, dy_b, "dw_b")

    rs_mine, rs_sib, rs_handles = {}, {}, {}

    sib_handles = {}

    def to_sibling_start(keys, grads, tag):
        parts = [_to_blocks(grads[k], big[k][1]) for k in keys]
        handles, tok = copies_start(parts, [jax.ShapeDtypeStruct((4,) + p.shape[1:], BF16) for p in parts],
                                    _to_sibling_plan, 4, f"grads_to_sibling_{tag}_start")
        sib_handles[tag] = (keys, handles)
        return tok

    def to_chips_start(tag, after):
        keys, handles = sib_handles[tag]
        parts, from_sib = copies_wait(handles, _to_sibling_plan, after, f"grads_to_sibling_{tag}_wait")
        sums = [pair_add(a, b, place, f"chip_sum_{k}") for k, a, b in zip(keys, parts, from_sib)]
        handles, tok = copies_start(sums, [jax.ShapeDtypeStruct((3,) + s.shape[1:], BF16) for s in sums],
                                    _to_chips_plan, 3, f"grads_to_chips_{tag}_start")
        for k, a, b in zip(keys, parts, from_sib):
            rs_mine[k], rs_sib[k] = a, b
        rs_handles[tag] = (keys, handles)
        return tok

    tok_a = to_sibling_start(("w_a", "w_b", "w_o", "w_up", "w_down"),
                             dict(w_a=dw_a, w_b=dw_b, w_o=dw_o, w_up=jnp.concatenate([dw_up_a, dw_up_g], 1),
                                  w_down=dw_down), "a")
    dcomb = mm_nt(dy_b, full["w_b"], None, "d_comb", after=tok_a, out_dtype=ACT)
    dos, dms = combine_bwd(dcomb, outs, lses, "combine_bwd")
    tok_a = to_chips_start("a", dms[0])
    dw_grp, cs_grp, dqkvs = [], [], []
    for g, d in enumerate(DILATIONS):
        dq, dk, dv = att_bwd(qkvs[g], dos[g], lses[g], dms[g], g, f"att_bwd_{g}", after=tok_a if g == 0 else None)
        dqkv = [t.reshape(T, GROUP_W) for t in (dq, dk, dv)]
        dwg, csg = mm_tn(h0_res[g], dqkv, f"dw_in_qkv_{g}")
        dqkvs.append(dqkv)
        dw_grp.append(dwg)
        cs_grp.append(csg)
    dw_pa, cs_pa = mm_tn(h0b, dproj_a, "dw_in_conv")
    dw_pg, cs_pg = mm_tn(h0b, dproj_g, "dw_in_gates")

    def ungroup(parts):
        return jnp.concatenate([p[:, s * GROUP_W:(s + 1) * GROUP_W] for s in range(3) for p in parts], 1)

    db_in_parts = [cs_pa, ungroup(cs_grp), cs_pg]
    tok_b = to_sibling_start(("w_in",), dict(w_in=jnp.concatenate([dw_pa, ungroup(dw_grp), dw_pg], 1)), "b")
    dh0 = mm_nt(dproj_a, w_pa, None, "d_h0_conv", after=tok_b)
    tok_b = to_chips_start("b", dh0)
    dh0 = mm_nt(dproj_g, w_pg, dh0, "d_h0_gates", after=tok_b)
    dh0_res = [(mm_nt(dqkvs[g], w_grp[g], None, f"d_h0_qkv_{g}").reshape(d, T // d, D), d)
               for g, d in enumerate(DILATIONS) if g > 0]
    dx, _, d_ln0_g, d_ln0_b, _, _ = ln_bwd(xs, None, ln0g, ln0b, dr1, ("nt", dqkvs[0], w_grp[0]), None, "d_h0_ln0_bwd",
                                           by_residue=[(dh0.reshape(1, T, D), 1)] + dh0_res)

    small = [d_ln0_g, d_ln0_b, jnp.concatenate(db_in_parts, 1), d_conv, d_b_o, d_ln1_g, d_ln1_b,
             jnp.concatenate([cs_a, cs_gate], 1), d_fcw, d_fcb, d_b_down, d_ln2_g, d_ln2_b, loss_part]
    packed, sizes = _pack(small)
    total = all_sum_small(packed, "sum_small")
    (g_ln0_g, g_ln0_b, g_b_in, g_conv_full, g_b_o, g_ln1_g, g_ln1_b, g_b_up, g_fcw_full, g_fcb, g_b_down, g_ln2_g,
     g_ln2_b, loss) = _unpack(total, sizes, [a.shape for a in small])
    cw = conv_w.shape[-1]
    fw = ffn_conv_w.shape[-1]
    g_conv = lax.dynamic_slice_in_dim(g_conv_full, dev * cw, cw, 1)
    g_fcw = lax.dynamic_slice_in_dim(g_fcw_full, dev * fw, fw, 1)

    from_chips = {}
    for tag, (keys, handles) in rs_handles.items():
        _, lands = copies_wait(handles, _to_chips_plan, total, f"grads_to_chips_{tag}_wait")
        from_chips.update(zip(keys, lands))

    moments = dict(w_in=(m_w_in, v_w_in), w_a=(m_w_a, v_w_a), w_b=(m_w_b, v_w_b), w_o=(m_w_o, v_w_o),
                   w_up=(m_w_up, v_w_up), w_down=(m_w_down, v_w_down))
    res_big = {}
    for k in names:
        res_big[k] = adamw_sharded(big[k][0], moments[k][0][0], moments[k][1][0], rs_mine[k], rs_sib[k], from_chips[k],
                                   place, f"adamw_{k}")

    small_names = ["ln0_g", "ln0_b", "b_in", "conv_w", "b_o", "ln1_g", "ln1_b", "b_up", "ffn_conv_w", "ffn_conv_b",
                   "b_down", "ln2_g", "ln2_b"]
    small_w = [ln0_g, ln0_b, b_in, conv_w, b_o, ln1_g, ln1_b, b_up, ffn_conv_w, ffn_conv_b, b_down, ln2_g, ln2_b]
    small_m = [m_ln0_g, m_ln0_b, m_b_in, m_conv_w, m_b_o, m_ln1_g, m_ln1_b, m_b_up, m_ffn_conv_w, m_ffn_conv_b,
               m_b_down, m_ln2_g, m_ln2_b]
    small_v = [v_ln0_g, v_ln0_b, v_b_in, v_conv_w, v_b_o, v_ln1_g, v_ln1_b, v_b_up, v_ffn_conv_w, v_ffn_conv_b,
               v_b_down, v_ln2_g, v_ln2_b]
    small_g = [g_ln0_g, g_ln0_b, g_b_in, g_conv, g_b_o, g_ln1_g, g_ln1_b, g_b_up, g_fcw, g_fcb, g_b_down, g_ln2_g,
               g_ln2_b]
    shapes = [w.shape for w in small_w]
    small_g = [g.reshape(s) for g, s in zip(small_g, shapes)]
    pw, psz = _pack(small_w)
    pg, _ = _pack(small_g)
    pm, _ = _pack(small_m)
    pv, _ = _pack(small_v)
    pd, pnm, pnv = adamw_packed(pw, pg, pm, pv, "adamw_small")
    res_small = {k: (g, d_, m_, v_) for k, g, d_, m_, v_ in zip(
        small_names, small_g, _unpack(pd, psz, shapes), _unpack(pnm, psz, shapes), _unpack(pnv, psz, shapes))}

    order = ["ln0_g", "ln0_b", "w_in", "b_in", "conv_w", "w_a", "w_b", "w_o", "b_o", "ln1_g", "ln1_b", "w_up", "b_up",
             "ffn_conv_w", "ffn_conv_b", "w_down", "b_down", "ln2_g", "ln2_b"]

    def result(k, j):
        if k in res_big:
            return res_big[k][j][None]
        return res_small[k][j]

    out = [loss.reshape(()), dx.reshape(x.shape)]
    for j in range(4):
        out += [result(k, j) for k in order]
    return tuple(out)
```

```python
import math

import numpy as np
import jax
import jax.numpy as jnp
from jax import lax
from jax.experimental import pallas as pl
from jax.experimental.pallas import tpu as pltpu

F32 = jnp.float32
BF16 = jnp.bfloat16
ACT = BF16

N_DEV = 8
LN_EPS = 1e-5
ALPHA = (2.0 * 1) ** 0.25
HEAD_DIM = 64
GROUP_W = 512
QKV_W = 3 * GROUP_W
DILATIONS = (1, 4, 16)
RADIUS = 64
LANES = 128
HALO = 8
HALO_BF16 = 16
ATT_TQ = 128

ADAM_LR = 0.001
ADAM_B1 = 0.9
ADAM_B2 = 0.999
ADAM_EPS = 1e-08
ADAM_WD = 0.01
ADAM_STEP = 10

VMEM_LIMIT = 58 * 1024 * 1024
OUT_TILE_BYTES = 8 * 1024 * 1024
MAX_K_TALL_TILE = 8192
MESH = pl.DeviceIdType.MESH
NT_DIMS = (((1,), (1,)), ((), ()))
TN_DIMS = (((0,), (0,)), ((), ()))


def _pick(n, target, align=LANES):
    if n <= target:
        return n
    best = None
    for t in range(align, target + 1, align):
        if n % t == 0:
            best = t
    assert best is not None, (n, target, align)
    return best


def _params(sems=None):
    return pltpu.CompilerParams(dimension_semantics=sems, vmem_limit_bytes=VMEM_LIMIT)


def _alibi_slopes():
    n = 3 * 8
    return np.exp2(-8.0 * np.arange(1, n + 1, dtype=np.float64) / n).astype(np.float32).reshape(3, 8)


def _ln_stats(r):
    mu = jnp.mean(r, -1, keepdims=True)
    xc = r - mu
    var = jnp.mean(xc * xc, -1, keepdims=True)
    rstd = lax.rsqrt(var + LN_EPS)
    return xc, rstd


def _load_natural(ref, d, scr):
    if d == 1:
        return ref[0].astype(F32)
    n, C = ref.shape[1], ref.shape[2]
    for c in range(C // LANES):
        for r in range(d):
            scr[c, pl.ds(r, n, stride=d), :] = ref[r, :, c * LANES:(c + 1) * LANES].astype(F32)
    return jnp.concatenate([scr[c] for c in range(C // LANES)], axis=1)


def _store_by_residue(val, ref, d, scr):
    if d == 1:
        ref[0] = val.astype(ref.dtype)
        return
    n, C = ref.shape[1], ref.shape[2]
    for c in range(C // LANES):
        scr[c] = val[:, c * LANES:(c + 1) * LANES]
    for c in range(C // LANES):
        for r in range(d):
            ref[r, :, c * LANES:(c + 1) * LANES] = scr[c, pl.ds(r, n, stride=d), :].astype(ref.dtype)


def _residue_spec(tm, d, C):
    return pl.BlockSpec((d, tm // d, C), lambda i: (0, i, 0))


def _residue_scratch(tm, C):
    return pltpu.VMEM((C // LANES, tm, LANES), F32)


def ln_fwd(a, res, g, b, name, dilations=(), gather=()):
    T, D = a.shape
    res_mm = isinstance(res, tuple)
    tm = _pick(T, 512, 8)
    res_ins = list(res[1:]) if res_mm else ([] if res is None else [res])
    nd = len(dilations)
    ng = len(gather)
    n_in = 1 + len(res_ins) + 2
    last = T // tm - 1

    def body(*refs):
        a_ref = refs[0]
        r = a_ref[...]
        if res_mm:
            res_val = jnp.dot(refs[1][...], refs[2][...], preferred_element_type=F32) + refs[3][...]
            refs[-1 - n_scratch][...] = res_val
            r = ALPHA * r + res_val
        elif res_ins:
            r = ALPHA * r + refs[1][...]
        g_ref, b_ref = refs[n_in - 2], refs[n_in - 1]
        shard_refs = refs[n_in:n_in + ng]
        h_ref, hb_ref = refs[n_in + ng], refs[n_in + ng + 1]
        p_refs = refs[n_in + ng + 2:n_in + ng + 2 + nd]
        full_refs = refs[n_in + ng + 2 + nd:n_in + 2 * ng + 2 + nd]
        scratch = refs[len(refs) - n_scratch:]
        sems = scratch[len(scratch) - 3:] if ng else ()

        if ng:
            @pl.when(pl.program_id(0) == 0)
            def _():
                _gather_begin(shard_refs, full_refs, *sems)

        xc, rstd = _ln_stats(r)
        h = xc * rstd * g_ref[...] + b_ref[...]
        h_ref[...] = h
        hb_ref[...] = h.astype(BF16)
        for d, p_ref in zip(dilations, p_refs):
            _store_by_residue(h, p_ref, d, scratch[0])

        if ng:
            @pl.when(pl.program_id(0) == last)
            def _():
                _gather_finish(shard_refs, full_refs, *sems)

    row = pl.BlockSpec((tm, D), lambda i: (i, 0))
    vec = pl.BlockSpec((1, D), lambda i: (0, 0))
    hbm = pl.BlockSpec(memory_space=pl.ANY)
    if res_mm:
        res_specs = [pl.BlockSpec((tm, res[1].shape[1]), lambda i: (i, 0)), pl.BlockSpec(res[2].shape, lambda i: (0, 0)), vec]
    else:
        res_specs = [row] * len(res_ins)
    scratch_shapes = ([_residue_scratch(tm, D)] if nd else []) + (_gather_scratch(ng) if ng else [])
    n_scratch = len(scratch_shapes)
    ins = [a] + res_ins + [g, b] + list(gather)
    return pl.pallas_call(
        body, name=name, grid=(T // tm,),
        in_specs=[row] + res_specs + [vec, vec] + [hbm] * ng,
        out_specs=[row, row] + [_residue_spec(tm, d, D) for d in dilations] + [hbm] * ng + ([row] if res_mm else []),
        out_shape=[jax.ShapeDtypeStruct((T, D), F32), jax.ShapeDtypeStruct((T, D), BF16)]
        + [jax.ShapeDtypeStruct((d, T // d, D), BF16) for d in dilations]
        + [jax.ShapeDtypeStruct((N_DEV,) + s.shape, s.dtype) for s in gather]
        + ([jax.ShapeDtypeStruct((T, D), F32)] if res_mm else []),
        scratch_shapes=scratch_shapes,
        compiler_params=_params(("arbitrary",) if ng else ("parallel",)),
    )(*ins)


def ln_bwd(a, res, g, b, d1, d2, tgt, name, by_residue=()):
    T, D = a.shape
    wide_product = isinstance(d2, tuple) and sum(p.shape[1] for p in d2[1]) > MAX_K_TALL_TILE
    fused = isinstance(res, tuple) or isinstance(d2, tuple)
    tm = _pick(T, 512 if fused and not wide_product else 256, 8)
    loss_mode = tgt is not None
    nres = len(by_residue)
    row = pl.BlockSpec((tm, D), lambda i: (i, 0))
    vec = pl.BlockSpec((1, D), lambda i: (0, 0))
    one = pl.BlockSpec((1, 1), lambda i: (0, 0))

    def rows_of(x):
        return pl.BlockSpec((tm, x.shape[1]), lambda i: (i, 0))

    def whole(x):
        return pl.BlockSpec(x.shape, lambda i: (0, 0))

    ins, in_specs, slots = [], [], {}

    def operand(key, arrays, specs):
        slots[key] = (len(ins), len(arrays))
        ins.extend(arrays)
        in_specs.extend(specs)

    operand("a", [a], [row])
    if isinstance(res, tuple):
        _, x, w, bias = res
        operand("res_mm", [x, w, bias], [rows_of(x), whole(w), vec])
    elif res is not None:
        operand("res", [res], [row])
    operand("gb", [g, b], [vec, vec])
    if loss_mode:
        operand("tgt", [tgt], [row])
    else:
        operand("d1", [d1], [row])
        if isinstance(d2, tuple):
            _, pieces, w = d2
            operand("d2_mm", list(pieces) + [w], [rows_of(p) for p in pieces] + [whole(w)])
        else:
            operand("d2", [d2], [row])
    operand("by_residue", [e for e, _ in by_residue], [_residue_spec(tm, d, D) for _, d in by_residue])
    n_in = len(ins)

    def body(*refs):
        def get(key):
            first, count = slots[key]
            return refs[first:first + count]

        dr_ref, drb_ref, dg_ref, db_ref, ds_ref, loss_ref = refs[n_in:n_in + 6]
        i = pl.program_id(0)

        @pl.when(i == 0)
        def _():
            dg_ref[...] = jnp.zeros_like(dg_ref)
            db_ref[...] = jnp.zeros_like(db_ref)
            ds_ref[...] = jnp.zeros_like(ds_ref)
            loss_ref[...] = jnp.zeros_like(loss_ref)

        r = get("a")[0][...]
        if "res_mm" in slots:
            x_ref, w_ref, bias_ref = get("res_mm")
            r = ALPHA * r + (jnp.dot(x_ref[...], w_ref[...], preferred_element_type=F32) + bias_ref[...])
        elif "res" in slots:
            r = ALPHA * r + get("res")[0][...]
        g_ref, b_ref = get("gb")
        xc, rstd = _ln_stats(r)
        xhat = xc * rstd
        gam = g_ref[...]
        if loss_mode:
            err = xhat * gam + b_ref[...] - get("tgt")[0][...]
            dy = err * (1.0 / D)
            row_loss = jnp.mean(err * err, -1, keepdims=True)
            loss_ref[...] += 0.5 * jnp.sum(row_loss, 0, keepdims=True)
        else:
            if "d2_mm" in slots:
                *p_refs, w_ref = get("d2_mm")
                av = p_refs[0][...] if len(p_refs) == 1 else jnp.concatenate([p[...] for p in p_refs], axis=1)
                d2v = lax.dot_general(av, w_ref[...], NT_DIMS, preferred_element_type=F32)
            else:
                d2v = get("d2")[0][...]
            dy = ALPHA * get("d1")[0][...] + d2v
        for (_, d), e_ref in zip(by_residue, get("by_residue")):
            dy = dy + _load_natural(e_ref, d, refs[-1])
        dyg = dy * gam
        c1 = jnp.mean(dyg, -1, keepdims=True)
        c2 = jnp.mean(dyg * xhat, -1, keepdims=True)
        dr = rstd * (dyg - c1 - xhat * c2)
        dr_ref[...] = dr
        drb_ref[...] = dr.astype(BF16)
        dg_ref[...] += jnp.sum(dy * xhat, 0, keepdims=True)
        db_ref[...] += jnp.sum(dy, 0, keepdims=True)
        ds_ref[...] += jnp.sum(dr, 0, keepdims=True)

    return pl.pallas_call(
        body, name=name, grid=(T // tm,),
        in_specs=in_specs,
        out_specs=[row, row, vec, vec, vec, one],
        out_shape=[jax.ShapeDtypeStruct((T, D), F32), jax.ShapeDtypeStruct((T, D), BF16),
                   jax.ShapeDtypeStruct((1, D), F32), jax.ShapeDtypeStruct((1, D), F32),
                   jax.ShapeDtypeStruct((1, D), F32), jax.ShapeDtypeStruct((1, 1), F32)],
        scratch_shapes=[_residue_scratch(tm, D)] if nres else [],
        compiler_params=_params(("arbitrary",)),
    )(*ins)


_TOKEN_SPEC = pl.BlockSpec((8, LANES), lambda i: (0, 0))


def mm_nn(a, w, bias, out_dtype, name, after=None):
    M, K = a.shape
    N = w.shape[1]
    tm = _pick(M, max(256, min(1024, OUT_TILE_BYTES // (N * jnp.dtype(out_dtype).itemsize))), 8)
    tc = _pick(N, 512)

    def body(a_ref, w_ref, b_ref, *rest):
        o_ref = rest[-1]
        av = a_ref[...]
        for j in range(N // tc):
            cols = slice(j * tc, (j + 1) * tc)
            acc = jnp.dot(av, w_ref[:, cols], preferred_element_type=F32)
            o_ref[:, cols] = (acc + b_ref[:, cols]).astype(out_dtype)

    return pl.pallas_call(
        body, name=name, grid=(M // tm,),
        in_specs=[pl.BlockSpec((tm, K), lambda i: (i, 0)),
                  pl.BlockSpec((K, N), lambda i: (0, 0)),
                  pl.BlockSpec((1, N), lambda i: (0, 0))] + ([] if after is None else [_TOKEN_SPEC]),
        out_specs=pl.BlockSpec((tm, N), lambda i: (i, 0)),
        out_shape=jax.ShapeDtypeStruct((M, N), out_dtype),
        compiler_params=_params(("parallel",)),
    )(a, w, bias, *([] if after is None else [after]))


def mm_nt(a, w, acc_in, name, after=None, w_block=0, out_dtype=F32):
    pieces = list(a) if isinstance(a, (list, tuple)) else [a]
    M = pieces[0].shape[0]
    widths = [p.shape[1] for p in pieces]
    K = sum(widths)
    N = w.shape[0]
    tm = _pick(M, 1024, 8)
    tc = _pick(N, 512)
    has_acc = acc_in is not None
    n_a = len(pieces)

    def body(*refs):
        a_refs, w_ref = refs[:n_a], refs[n_a]
        c_ref = refs[n_a + 1] if has_acc else None
        o_ref = refs[-1]
        av = a_refs[0][...] if n_a == 1 else jnp.concatenate([r[...] for r in a_refs], axis=1)
        for j in range(N // tc):
            cols = slice(j * tc, (j + 1) * tc)
            acc = lax.dot_general(av, w_ref[cols, :], NT_DIMS, preferred_element_type=F32)
            if has_acc:
                acc = acc + c_ref[:, cols]
            o_ref[:, cols] = acc.astype(out_dtype)

    out_spec = pl.BlockSpec((tm, N), lambda i: (i, 0))
    in_specs = [pl.BlockSpec((tm, kw), lambda i: (i, 0)) for kw in widths]
    in_specs.append(pl.BlockSpec((N, K), lambda i: (0, w_block)))
    ins = pieces + [w]
    if has_acc:
        in_specs.append(out_spec)
        ins.append(acc_in)
    if after is not None:
        in_specs.append(_TOKEN_SPEC)
        ins.append(after)
    return pl.pallas_call(
        body, name=name, grid=(M // tm,),
        in_specs=in_specs, out_specs=out_spec,
        out_shape=jax.ShapeDtypeStruct((M, N), out_dtype),
        compiler_params=_params(("parallel",)),
    )(*ins)


def mm_tn(a, b, name, out_dtype=BF16):
    pieces = list(b) if isinstance(b, (list, tuple)) else [b]
    T, M = a.shape
    widths = [p.shape[1] for p in pieces]
    N = sum(widths)
    tk = _pick(T, 1024, 8)
    nk = T // tk
    tc = _pick(M, 256)
    n_b = len(pieces)

    def body(*refs):
        a_ref, b_refs = refs[0], refs[1:1 + n_b]
        o_ref, cs_ref, acc_ref = refs[1 + n_b:]
        k = pl.program_id(0)

        @pl.when(k == 0)
        def _():
            acc_ref[...] = jnp.zeros_like(acc_ref)
            cs_ref[...] = jnp.zeros_like(cs_ref)

        bv = b_refs[0][...] if n_b == 1 else jnp.concatenate([r[...] for r in b_refs], axis=1)
        cs_ref[...] += jnp.sum(bv.astype(F32), 0, keepdims=True)
        for mi in range(M // tc):
            rows = slice(mi * tc, (mi + 1) * tc)
            acc_ref[rows, :] += lax.dot_general(a_ref[:, rows], bv, TN_DIMS, preferred_element_type=F32)

        @pl.when(k == nk - 1)
        def _():
            o_ref[...] = acc_ref[...].astype(out_dtype)

    return pl.pallas_call(
        body, name=name, grid=(nk,),
        in_specs=[pl.BlockSpec((tk, M), lambda k: (k, 0))] + [pl.BlockSpec((tk, wd), lambda k: (k, 0)) for wd in widths],
        out_specs=[pl.BlockSpec((M, N), lambda k: (0, 0)), pl.BlockSpec((1, N), lambda k: (0, 0))],
        out_shape=[jax.ShapeDtypeStruct((M, N), out_dtype), jax.ShapeDtypeStruct((1, N), F32)],
        scratch_shapes=[pltpu.VMEM((M, N), F32)],
        compiler_params=_params(("arbitrary",)),
    )(a, *pieces)


def _ext_rows(prev_ref, main_ref, next_ref, i, tm, T, dtype=F32):
    before = jnp.where(i == 0, 0.0, prev_ref[...])
    after = jnp.where(i == T // tm - 1, 0.0, next_ref[...])
    return jnp.concatenate([before, main_ref[...], after], axis=0).astype(dtype)


def _prev_row(x):
    return pltpu.roll(x, 1, 0)


def _next_row(x):
    return pltpu.roll(x, x.shape[0] - 1, 0)


def _conv3(u, w_ref):
    return _prev_row(u) * w_ref[0:1, :] + u * w_ref[1:2, :] + _next_row(u) * w_ref[2:3, :]


def _main(x, tm, halo=HALO):
    return x[halo:halo + tm]


def _halo_specs(tm, tc, T, col, order, halo=HALO):
    r = tm // halo
    last = T // halo - 1
    if order == "ij":
        return (pl.BlockSpec((halo, tc), lambda i, j: (jnp.maximum(i * r - 1, 0), col(j))),
                pl.BlockSpec((tm, tc), lambda i, j: (i, col(j))),
                pl.BlockSpec((halo, tc), lambda i, j: (jnp.minimum((i + 1) * r, last), col(j))))
    return (pl.BlockSpec((halo, tc), lambda j, i: (jnp.maximum(i * r - 1, 0), col(j))),
            pl.BlockSpec((tm, tc), lambda j, i: (i, col(j))),
            pl.BlockSpec((halo, tc), lambda j, i: (jnp.minimum((i + 1) * r, last), col(j))))


def conv_a_fwd(proj_a, conv_w, name):
    T, D3 = proj_a.shape
    D = D3 // 3
    tm = _pick(T, 256, 8)

    def body(p_ref, m_ref, n_ref, w_ref, o_ref):
        i = pl.program_id(0)
        ext = _ext_rows(p_ref, m_ref, n_ref, i, tm, T)
        u = ext[:, D:2 * D] * ext[:, 2 * D:]
        cu = _conv3(u, w_ref)
        o_ref[...] = (m_ref[:, :D].astype(F32) * _main(cu, tm, HALO_BF16)).astype(BF16)

    prev, main, nxt = _halo_specs(tm, D3, T, lambda j: 0, "ij", HALO_BF16)
    return pl.pallas_call(
        body, name=name, grid=(T // tm, 1),
        in_specs=[prev, main, nxt, pl.BlockSpec((3, D), lambda i, j: (0, 0))],
        out_specs=pl.BlockSpec((tm, D), lambda i, j: (i, 0)),
        out_shape=jax.ShapeDtypeStruct((T, D), BF16),
        compiler_params=_params(("parallel", "arbitrary")),
    )(proj_a, proj_a, proj_a, conv_w)


def conv_a_bwd(dy_a, w_a, proj_a, conv_w, name):
    T, D3 = proj_a.shape
    D = D3 // 3
    tm = _pick(T, 256, 8)

    def body(dp_ref, dm_ref, dn_ref, wa_ref, p_ref, m_ref, n_ref, w_ref, o_ref, dw_ref):
        i = pl.program_id(0)

        @pl.when(i == 0)
        def _():
            dw_ref[...] = jnp.zeros_like(dw_ref)

        ext = _ext_rows(p_ref, m_ref, n_ref, i, tm, T)
        dsa = lax.dot_general(_ext_rows(dp_ref, dm_ref, dn_ref, i, tm, T, dtype=BF16), wa_ref[...], NT_DIMS,
                              preferred_element_type=F32)
        gb, gc, hin = ext[:, :D], ext[:, D:2 * D], ext[:, 2 * D:]
        u = gc * hin
        u_prev, u_next = _prev_row(u), _next_row(u)
        cu = u_prev * w_ref[0:1, :] + u * w_ref[1:2, :] + u_next * w_ref[2:3, :]
        dcu = dsa * gb
        du = _next_row(dcu) * w_ref[0:1, :] + dcu * w_ref[1:2, :] + _prev_row(dcu) * w_ref[2:3, :]
        h = HALO_BF16
        o_ref[:, :D] = _main(dsa * cu, tm, h).astype(BF16)
        o_ref[:, D:2 * D] = _main(du * hin, tm, h).astype(BF16)
        o_ref[:, 2 * D:] = _main(du * gc, tm, h).astype(BF16)
        dcu_m = _main(dcu, tm, h)
        dw_ref[0:1, :] += jnp.sum(dcu_m * _main(u_prev, tm, h), 0, keepdims=True)
        dw_ref[1:2, :] += jnp.sum(dcu_m * _main(u, tm, h), 0, keepdims=True)
        dw_ref[2:3, :] += jnp.sum(dcu_m * _main(u_next, tm, h), 0, keepdims=True)

    dprev, dmain, dnxt = _halo_specs(tm, dy_a.shape[1], T, lambda j: 0, "ij", HALO_BF16)
    prev, main, nxt = _halo_specs(tm, D3, T, lambda j: 0, "ij", HALO_BF16)
    return pl.pallas_call(
        body, name=name, grid=(T // tm, 1),
        in_specs=[dprev, dmain, dnxt, pl.BlockSpec(w_a.shape, lambda i, j: (0, 0)), prev, main, nxt,
                  pl.BlockSpec((3, D), lambda i, j: (0, 0))],
        out_specs=[pl.BlockSpec((tm, D3), lambda i, j: (i, 0)), pl.BlockSpec((3, D), lambda i, j: (0, 0))],
        out_shape=[jax.ShapeDtypeStruct((T, D3), BF16), jax.ShapeDtypeStruct((3, D), F32)],
        compiler_params=_params(("arbitrary", "arbitrary")),
    )(dy_a, dy_a, dy_a, w_a, proj_a, proj_a, proj_a, conv_w)


_INV_SQRT2 = 1.0 / math.sqrt(2.0)
_INV_SQRT_2PI = 1.0 / math.sqrt(2.0 * math.pi)


def ffn_up_conv_f(h, w_up, b_up, fcw, fcb, name):
    T, D = h.shape
    F = fcb.shape[1]
    tm = _pick(T, 256, 8)
    tc = _pick(F, 256)
    halo = HALO_BF16

    def body(hp_ref, hm_ref, hn_ref, w_ref, b_ref, cw_ref, cb_ref, up_ref, f_ref):
        i = pl.program_id(0)
        h_ext = _ext_rows(hp_ref, hm_ref, hn_ref, i, tm, T, dtype=BF16)
        h_main = hm_ref[...]
        rows = i * tm - halo + lax.broadcasted_iota(jnp.int32, (tm + 2 * halo, 1), 0)
        inside = (rows >= 0) & (rows < T)
        for c in range(F // tc):
            cols = slice(c * tc, (c + 1) * tc)
            gcols = slice(F + c * tc, F + (c + 1) * tc)
            a_ext = jnp.dot(h_ext, w_ref[:, cols], preferred_element_type=F32) + b_ref[:, cols]
            a_ext = jnp.where(inside, a_ext, 0.0)
            gate = jnp.dot(h_main, w_ref[:, gcols], preferred_element_type=F32) + b_ref[:, gcols]
            up_ref[:, cols] = _main(a_ext, tm, halo)
            up_ref[:, gcols] = gate
            ca = _main(_prev_row(a_ext) * cw_ref[0:1, cols] + a_ext * cw_ref[1:2, cols]
                       + _next_row(a_ext) * cw_ref[2:3, cols], tm, halo) + cb_ref[:, cols]
            gl = 0.5 * ca * (1.0 + lax.erf(ca * _INV_SQRT2))
            f_ref[:, cols] = (gl * gate).astype(BF16)

    prev, main, nxt = _halo_specs(tm, D, T, lambda j: 0, "ij", halo)
    whole = lambda x: pl.BlockSpec(x.shape, lambda i, j: (0, 0))
    return pl.pallas_call(
        body, name=name, grid=(T // tm, 1),
        in_specs=[prev, main, nxt, whole(w_up), whole(b_up), whole(fcw), whole(fcb)],
        out_specs=[pl.BlockSpec((tm, 2 * F), lambda i, j: (i, 0)), pl.BlockSpec((tm, F), lambda i, j: (i, 0))],
        out_shape=[jax.ShapeDtypeStruct((T, 2 * F), F32), jax.ShapeDtypeStruct((T, F), BF16)],
        compiler_params=_params(("parallel", "arbitrary")),
    )(h, h, h, w_up, b_up, fcw, fcb)


def conv_f_bwd(dy, w_down, up, fcw, fcb, name):
    T, F2 = up.shape
    F = F2 // 2
    D = dy.shape[1]
    tm = _pick(T, 256, 8)
    tc = _pick(F, 256)

    def body(yp_ref, ym_ref, yn_ref, wd_ref, up_ref, um_ref, un_ref, w_ref, b_ref,
             da_ref, dg_ref, csa_ref, csg_ref, dfb_ref, dfw_ref):
        i = pl.program_id(0)
        first, last = i == 0, i == T // tm - 1

        @pl.when(first)
        def _():
            csa_ref[...] = jnp.zeros_like(csa_ref)
            csg_ref[...] = jnp.zeros_like(csg_ref)
            dfb_ref[...] = jnp.zeros_like(dfb_ref)
            dfw_ref[...] = jnp.zeros_like(dfw_ref)

        def ext(cols):
            return jnp.concatenate([jnp.where(first, 0.0, up_ref[:, cols]), um_ref[:, cols],
                                    jnp.where(last, 0.0, un_ref[:, cols])], axis=0)

        dy_ext = _ext_rows(yp_ref, ym_ref, yn_ref, i, tm, T, dtype=BF16)
        for c in range(F // tc):
            cols = slice(c * tc, (c + 1) * tc)
            dfe = lax.dot_general(dy_ext, wd_ref[cols, :], NT_DIMS, preferred_element_type=F32)
            dfe = dfe[HALO_BF16 - HALO:HALO_BF16 + tm + HALO]
            a = ext(cols)
            gate = ext(slice(F + c * tc, F + (c + 1) * tc))
            a_prev, a_next = _prev_row(a), _next_row(a)
            ca = a_prev * w_ref[0:1, cols] + a * w_ref[1:2, cols] + a_next * w_ref[2:3, cols] + b_ref[:, cols]
            cdf = 0.5 * (1.0 + lax.erf(ca * _INV_SQRT2))
            gl = ca * cdf
            gp = cdf + ca * (jnp.exp(-0.5 * ca * ca) * _INV_SQRT_2PI)
            dgate = _main(dfe * gl, tm)
            dca = dfe * gate * gp
            da = _main(_next_row(dca) * w_ref[0:1, cols] + dca * w_ref[1:2, cols] + _prev_row(dca) * w_ref[2:3, cols],
                       tm)
            da_ref[:, cols] = da.astype(BF16)
            dg_ref[:, cols] = dgate.astype(BF16)
            csa_ref[:, cols] += jnp.sum(da, 0, keepdims=True)
            csg_ref[:, cols] += jnp.sum(dgate, 0, keepdims=True)
            dca_m = _main(dca, tm)
            dfb_ref[:, cols] += jnp.sum(dca_m, 0, keepdims=True)
            dfw_ref[0:1, cols] += jnp.sum(dca_m * _main(a_prev, tm), 0, keepdims=True)
            dfw_ref[1:2, cols] += jnp.sum(dca_m * _main(a, tm), 0, keepdims=True)
            dfw_ref[2:3, cols] += jnp.sum(dca_m * _main(a_next, tm), 0, keepdims=True)

    uprev, umain, unxt = _halo_specs(tm, F2, T, lambda j: 0, "ij")
    yprev, ymain, ynxt = _halo_specs(tm, D, T, lambda j: 0, "ij", HALO_BF16)
    whole = lambda shape: pl.BlockSpec(shape, lambda i, j: (0, 0))
    tile = pl.BlockSpec((tm, F), lambda i, j: (i, 0))
    return pl.pallas_call(
        body, name=name, grid=(T // tm, 1),
        in_specs=[yprev, ymain, ynxt, whole((F, D)), uprev, umain, unxt, whole((3, F)), whole((1, F))],
        out_specs=[tile, tile, whole((1, F)), whole((1, F)), whole((1, F)), whole((3, F))],
        out_shape=[jax.ShapeDtypeStruct((T, F), BF16), jax.ShapeDtypeStruct((T, F), BF16),
                   jax.ShapeDtypeStruct((1, F), F32), jax.ShapeDtypeStruct((1, F), F32),
                   jax.ShapeDtypeStruct((1, F), F32), jax.ShapeDtypeStruct((3, F), F32)],
        compiler_params=_params(("arbitrary", "arbitrary")),
    )(dy, dy, dy, w_down, up, up, up, fcw, fcb)


def gate_fwd(proj_g, y_a, y_b, name):
    T, D = y_a.shape
    tm = _pick(T, 512, 8)

    def body(g_ref, a_ref, b_ref, o_ref):
        sa = jax.nn.sigmoid(g_ref[:, :D].astype(F32))
        sb = jax.nn.sigmoid(g_ref[:, D:].astype(F32))
        o_ref[...] = (sa * a_ref[...].astype(F32) + sb * b_ref[...].astype(F32)).astype(BF16)

    row = pl.BlockSpec((tm, D), lambda i: (i, 0))
    return pl.pallas_call(
        body, name=name, grid=(T // tm,),
        in_specs=[pl.BlockSpec((tm, 2 * D), lambda i: (i, 0)), row, row],
        out_specs=row,
        out_shape=jax.ShapeDtypeStruct((T, D), BF16),
        compiler_params=_params(("parallel",)),
    )(proj_g, y_a, y_b)


def gate_bwd(dmix, w_o, proj_g, y_a, y_b, name):
    T, D = y_a.shape
    tm = _pick(T, 512, 8)

    def body(dz_ref, w_ref, g_ref, a_ref, b_ref, da_ref, db_ref, dg_ref):
        dzv = lax.dot_general(dz_ref[...], w_ref[...], NT_DIMS, preferred_element_type=F32)
        sa = jax.nn.sigmoid(g_ref[:, :D].astype(F32))
        sb = jax.nn.sigmoid(g_ref[:, D:].astype(F32))
        da_ref[...] = (dzv * sa).astype(BF16)
        db_ref[...] = (dzv * sb).astype(BF16)
        dg_ref[:, :D] = (dzv * a_ref[...].astype(F32) * (sa * (1.0 - sa))).astype(BF16)
        dg_ref[:, D:] = (dzv * b_ref[...].astype(F32) * (sb * (1.0 - sb))).astype(BF16)

    row = pl.BlockSpec((tm, D), lambda i: (i, 0))
    wide = pl.BlockSpec((tm, 2 * D), lambda i: (i, 0))
    return pl.pallas_call(
        body, name=name, grid=(T // tm,),
        in_specs=[pl.BlockSpec((tm, dmix.shape[1]), lambda i: (i, 0)), pl.BlockSpec(w_o.shape, lambda i: (0, 0)),
                  wide, row, row],
        out_specs=[row, row, wide],
        out_shape=[jax.ShapeDtypeStruct((T, D), BF16), jax.ShapeDtypeStruct((T, D), BF16),
                   jax.ShapeDtypeStruct((T, 2 * D), BF16)],
        compiler_params=_params(("parallel",)),
    )(dmix, w_o, proj_g, y_a, y_b)


ATT_WIN = ATT_TQ + 2 * RADIUS
ATT_STEP = 2048
FAR = 1e32


def _att_window(qs, L):
    ks = pl.multiple_of(jnp.clip(qs - RADIUS, 0, L - ATT_WIN), RADIUS)
    return ks, jnp.where(qs == 0, 0, jnp.where(qs == L - ATT_TQ, 2, 1))


def _fill_bias_tables(bias_ref, sl_ref, hp, d):
    col_row = (lax.broadcasted_iota(jnp.int32, (ATT_TQ, ATT_WIN), 1)
               - lax.broadcasted_iota(jnp.int32, (ATT_TQ, ATT_WIN), 0))
    for v in range(3):
        ad = jnp.abs(col_row - v * RADIUS)
        dist = jnp.where(ad <= RADIUS, (ad * d).astype(F32), FAR)
        bias_ref[v, 0:ATT_TQ, :] = sl_ref[hp * 2] * dist
        bias_ref[v, ATT_TQ:2 * ATT_TQ, :] = sl_ref[hp * 2 + 1] * dist


def _head_masks():
    lane = lax.broadcasted_iota(jnp.int32, (1, LANES), 1)
    return [lane < HEAD_DIM, lane >= HEAD_DIM]


def _stack_heads(x, masks):
    zero = jnp.zeros_like(x)
    return jnp.concatenate([jnp.where(masks[0], x, zero), jnp.where(masks[1], x, zero)], axis=0)


def _unstack_heads(x2, masks):
    n = x2.shape[0] // 2
    return jnp.where(masks[0], x2[:n], x2[n:])


def _att_step(L):
    step = min(ATT_STEP, L)
    assert L % step == 0 and step % ATT_TQ == 0 and L >= ATT_WIN
    return step


def _residues_per_step(d, L):
    rps = max(1, min(d, ATT_STEP // L))
    assert d % rps == 0
    return rps


def att_fwd(qkv, group, name):
    d, L, _ = qkv.shape
    step = _att_step(L)
    rps = _residues_per_step(d, L)
    cg = GROUP_W // LANES
    slopes = jnp.asarray(_alibi_slopes()[group])
    scale = HEAD_DIM ** -0.5

    def body(sl_ref, q_ref, k_ref, v_ref, o_ref, l_ref, bias_ref, s_ref, p_ref):
        hp = pl.program_id(1)
        i = pl.program_id(2)

        @pl.when(i == 0)
        def _():
            _fill_bias_tables(bias_ref, sl_ref, hp, d)

        masks = _head_masks()
        per = step // ATT_TQ
        tiles = [(rr, t) for rr in range(rps) for t in range(per)]
        windows = [_att_window(i * step + t * ATT_TQ, L) for t in range(per)]
        for n, (rr, t) in enumerate(tiles):
            rows = slice(t * ATT_TQ, (t + 1) * ATT_TQ)
            ks, table = windows[t]
            q2 = _stack_heads(q_ref[rr, rows, :] * scale, masks)
            kw = k_ref[rr, pl.ds(ks, ATT_WIN), :]
            s_ref[n] = lax.dot_general(q2, kw, NT_DIMS, preferred_element_type=F32) - bias_ref[table]
        for n, (rr, t) in enumerate(tiles):
            rows = slice(t * ATT_TQ, (t + 1) * ATT_TQ)
            s = s_ref[n]
            m = jnp.max(s, -1, keepdims=True)
            p = jnp.exp(s - m)
            den = jnp.sum(p, -1, keepdims=True)
            p_ref[n] = (p / den).astype(BF16)
            l_ref[rr, rows, :] = _unstack_heads(m + jnp.log(den), masks)
        for n, (rr, t) in enumerate(tiles):
            rows = slice(t * ATT_TQ, (t + 1) * ATT_TQ)
            vw = v_ref[rr, pl.ds(windows[t][0], ATT_WIN), :]
            o2 = jnp.dot(p_ref[n], vw, preferred_element_type=F32)
            o_ref[rr, rows, :] = _unstack_heads(o2, masks).astype(ACT)

    n_tiles = rps * step // ATT_TQ
    out_spec = pl.BlockSpec((rps, step, LANES), lambda r, hp, i: (r, i, hp))
    return pl.pallas_call(
        body, name=name, grid=(d // rps, cg, L // step),
        in_specs=[pl.BlockSpec(memory_space=pltpu.SMEM),
                  pl.BlockSpec((rps, step, LANES), lambda r, hp, i: (r, i, hp)),
                  pl.BlockSpec((rps, L, LANES), lambda r, hp, i: (r, 0, cg + hp)),
                  pl.BlockSpec((rps, L, LANES), lambda r, hp, i: (r, 0, 2 * cg + hp))],
        out_specs=[out_spec, out_spec],
        out_shape=[jax.ShapeDtypeStruct((d, L, GROUP_W), ACT), jax.ShapeDtypeStruct((d, L, GROUP_W), F32)],
        scratch_shapes=[pltpu.VMEM((3, 2 * ATT_TQ, ATT_WIN), F32),
                        pltpu.VMEM((n_tiles, 2 * ATT_TQ, ATT_WIN), F32),
                        pltpu.VMEM((n_tiles, 2 * ATT_TQ, ATT_WIN), BF16)],
        compiler_params=_params(("arbitrary", "arbitrary", "arbitrary")),
    )(slopes, qkv, qkv, qkv)


def att_bwd(qkv, do, lse, dmat, group, name, after=None):
    d, L, _ = qkv.shape
    step = _att_step(L)
    rps = _residues_per_step(d, L)
    nq = L // step
    cg = GROUP_W // LANES
    slopes = jnp.asarray(_alibi_slopes()[group])
    scale = HEAD_DIM ** -0.5

    def body(sl_ref, q_ref, k_ref, v_ref, do_ref, l_ref, dm_ref, *rest):
        dq_ref, dk_ref, dv_ref, dk_acc, dv_acc, bias_ref, s_ref, dp_ref, p_ref, ds_ref = rest[len(rest) - 10:]
        hp = pl.program_id(1)
        i = pl.program_id(2)

        @pl.when(i == 0)
        def _():
            dk_acc[...] = jnp.zeros_like(dk_acc)
            dv_acc[...] = jnp.zeros_like(dv_acc)
            _fill_bias_tables(bias_ref, sl_ref, hp, d)

        masks = _head_masks()

        def head_cols(x):
            return jnp.concatenate([jnp.max(jnp.where(hm, x, -jnp.inf), -1, keepdims=True) for hm in masks], axis=0)

        per = step // ATT_TQ
        tiles = [(rr, t) for rr in range(rps) for t in range(per)]
        windows = [_att_window(i * step + t * ATT_TQ, L) for t in range(per)]

        def stacked(ref, rr, t, factor=None):
            x = ref[rr, t * ATT_TQ:(t + 1) * ATT_TQ, :]
            return _stack_heads(x if factor is None else x * factor, masks)

        for n, (rr, t) in enumerate(tiles):
            ks, table = windows[t]
            q2 = stacked(q_ref, rr, t, scale)
            s_ref[n] = lax.dot_general(q2, k_ref[rr, pl.ds(ks, ATT_WIN), :], NT_DIMS,
                                       preferred_element_type=F32) - bias_ref[table]
            dp_ref[n] = lax.dot_general(stacked(do_ref, rr, t), v_ref[rr, pl.ds(ks, ATT_WIN), :], NT_DIMS,
                                        preferred_element_type=F32)
        for n, (rr, t) in enumerate(tiles):
            rows = slice(t * ATT_TQ, (t + 1) * ATT_TQ)
            p = jnp.exp(s_ref[n] - head_cols(l_ref[rr, rows, :]))
            p_ref[n] = p.astype(BF16)
            ds_ref[n] = (p * (dp_ref[n] - head_cols(dm_ref[rr, rows, :]))).astype(BF16)
        for n, (rr, t) in enumerate(tiles):
            rows = slice(t * ATT_TQ, (t + 1) * ATT_TQ)
            ks = windows[t][0]
            ds = ds_ref[n]
            dq2 = jnp.dot(ds, k_ref[rr, pl.ds(ks, ATT_WIN), :], preferred_element_type=F32)
            dq_ref[rr, rows, :] = (_unstack_heads(dq2, masks) * scale).astype(BF16)
            dk_acc[rr, pl.ds(ks, ATT_WIN), :] += lax.dot_general(ds, stacked(q_ref, rr, t, scale), TN_DIMS,
                                                                 preferred_element_type=F32)
            dv_acc[rr, pl.ds(ks, ATT_WIN), :] += lax.dot_general(p_ref[n], stacked(do_ref, rr, t), TN_DIMS,
                                                                 preferred_element_type=F32)

        @pl.when(i == nq - 1)
        def _():
            dk_ref[...] = dk_acc[...].astype(BF16)
            dv_ref[...] = dv_acc[...].astype(BF16)

    tile = pl.BlockSpec((rps, step, LANES), lambda r, hp, i: (r, i, hp))
    whole = pl.BlockSpec((rps, L, LANES), lambda r, hp, i: (r, 0, hp))
    return pl.pallas_call(
        body, name=name, grid=(d // rps, cg, nq),
        in_specs=[pl.BlockSpec(memory_space=pltpu.SMEM), tile,
                  pl.BlockSpec((rps, L, LANES), lambda r, hp, i: (r, 0, cg + hp)),
                  pl.BlockSpec((rps, L, LANES), lambda r, hp, i: (r, 0, 2 * cg + hp)),
                  tile, tile, tile] + ([] if after is None else [pl.BlockSpec((8, LANES), lambda r, hp, i: (0, 0))]),
        out_specs=[tile, whole, whole],
        out_shape=[jax.ShapeDtypeStruct((d, L, GROUP_W), BF16)] * 3,
        scratch_shapes=[pltpu.VMEM((rps, L, LANES), F32), pltpu.VMEM((rps, L, LANES), F32),
                        pltpu.VMEM((3, 2 * ATT_TQ, ATT_WIN), F32)]
        + [pltpu.VMEM((rps * step // ATT_TQ, 2 * ATT_TQ, ATT_WIN), dt) for dt in (F32, F32, BF16, BF16)],
        compiler_params=_params(("arbitrary", "arbitrary", "arbitrary")),
    )(slopes, qkv, qkv, qkv, do, lse, dmat, *([] if after is None else [after]))


def _group_weights(ls):
    m = jnp.maximum(jnp.maximum(ls[0], ls[1]), ls[2])
    es = [jnp.exp(l - m) for l in ls]
    tot = es[0] + es[1] + es[2]
    return [e / tot for e in es]


def combine_fwd(outs, lses, name):
    T = outs[0].shape[0] * outs[0].shape[1]
    tm = _pick(T, 512, 8)
    n_scr = 2 * (len(DILATIONS) - 1)

    def body(*refs):
        o_refs, l_refs, c_ref, scr = refs[:3], refs[3:6], refs[6], refs[7:]
        o = [_load_natural(o_refs[g], d, scr[g - 1] if g else None) for g, d in enumerate(DILATIONS)]
        l = [_load_natural(l_refs[g], d, scr[g + 1] if g else None) for g, d in enumerate(DILATIONS)]
        w = _group_weights(l)
        c_ref[...] = (w[0] * o[0] + w[1] * o[1] + w[2] * o[2]).astype(BF16)

    specs = [_residue_spec(tm, d, GROUP_W) for d in DILATIONS]
    return pl.pallas_call(
        body, name=name, grid=(T // tm,),
        in_specs=specs + specs, out_specs=pl.BlockSpec((tm, GROUP_W), lambda i: (i, 0)),
        out_shape=jax.ShapeDtypeStruct((T, GROUP_W), BF16),
        scratch_shapes=[_residue_scratch(tm, GROUP_W)] * n_scr,
        compiler_params=_params(("parallel",)),
    )(*outs, *lses)


def combine_bwd(dcomb, outs, lses, name):
    T = dcomb.shape[0]
    tm = _pick(T, 256, 8)
    head = np.arange(GROUP_W) // HEAD_DIM
    seg = jnp.asarray((head[:, None] == head[None, :]).astype(np.float32)).astype(BF16)
    ng = len(DILATIONS)
    n_scr = 4 * (ng - 1)

    def body(*refs):
        dc_ref, o_refs, l_refs, e_ref = refs[0], refs[1:1 + ng], refs[1 + ng:1 + 2 * ng], refs[1 + 2 * ng]
        do_refs, dm_refs = refs[2 + 2 * ng:2 + 3 * ng], refs[2 + 3 * ng:2 + 4 * ng]
        scr = refs[2 + 4 * ng:]
        o = [_load_natural(o_refs[g], d, scr[4 * (g - 1)] if g else None) for g, d in enumerate(DILATIONS)]
        l = [_load_natural(l_refs[g], d, scr[4 * (g - 1) + 1] if g else None) for g, d in enumerate(DILATIONS)]
        w = _group_weights(l)
        dc = dc_ref[...].astype(F32)
        e = e_ref[...]
        prod = dc * (w[0] * o[0] + w[1] * o[1] + w[2] * o[2])
        tot = jnp.zeros_like(dc)
        for _ in range(3):
            part = prod.astype(BF16)
            tot = tot + jnp.dot(part, e, preferred_element_type=F32)
            prod = prod - part.astype(F32)
        for g, d in enumerate(DILATIONS):
            _store_by_residue(w[g] * dc, do_refs[g], d, scr[4 * (g - 1) + 2] if g else None)
            _store_by_residue(w[g] * tot, dm_refs[g], d, scr[4 * (g - 1) + 3] if g else None)

    specs = [_residue_spec(tm, d, GROUP_W) for d in DILATIONS]
    res = pl.pallas_call(
        body, name=name, grid=(T // tm,),
        in_specs=[pl.BlockSpec((tm, GROUP_W), lambda i: (i, 0))] + specs + specs
        + [pl.BlockSpec((GROUP_W, GROUP_W), lambda i: (0, 0))],
        out_specs=specs + specs,
        out_shape=[jax.ShapeDtypeStruct(o.shape, BF16) for o in outs] + [jax.ShapeDtypeStruct(o.shape, F32) for o in outs],
        scratch_shapes=[_residue_scratch(tm, GROUP_W)] * n_scr,
        compiler_params=_params(("parallel",)),
    )(dcomb, *outs, *lses, seg)
    return res[:ng], res[ng:]


def _position():
    return lax.axis_index("x"), lax.axis_index("y"), lax.axis_index("c")


def _other_chips(x, y):
    return [(1 - x, y), (x, 1 - y), (1 - x, 1 - y)]


def _remote(src, dst, send_sems, recv_sems, k, to):
    return pltpu.make_async_remote_copy(src_ref=src, dst_ref=dst, send_sem=send_sems.at[k], recv_sem=recv_sems.at[k],
                                        device_id=to, device_id_type=MESH)


GATHER_SEMS = 10
SPLIT_ROWS = 32


def _gather_plan(ins, outs, send_sems, recv_sems, local_sems):
    x, y, c = _position()
    sibling = (x, y, 1 - c)
    nbr_x, nbr_y, diag = (1 - x, y, c), (x, 1 - y, c), (1 - x, 1 - y, c)
    local, begin, stages, last = [], [], [], []
    for a in range(len(ins)):
        k0 = GATHER_SEMS * a
        rows = ins[a].shape[0]
        half = rows // 2

        def block(dev):
            return outs[a].at[4 * dev[0] + 2 * dev[1] + dev[2]]

        def part(ref, h):
            return ref.at[pl.ds(h * half, half)]

        def copy(k, src, dst, to):
            return _remote(src, dst, send_sems, recv_sems, k0 + k, to)

        me = (x, y, c)
        local.append(pltpu.make_async_copy(ins[a], block(me), local_sems.at[a]))
        begin.append(copy(0, ins[a], block(me), sibling))
        pass_on = [copy(7 + j, block(dev), block(dev), sibling) for j, dev in enumerate((nbr_x, nbr_y, diag))]
        if rows >= SPLIT_ROWS and rows % SPLIT_ROWS == 0:
            for h in range(2):
                begin.append(copy(1 + h, part(ins[a], h), part(block(me), h), nbr_x))
                begin.append(copy(3 + h, part(ins[a], h), part(block(me), h), nbr_y))
            from_x = [copy(1 + h, part(block(nbr_x), h), part(block(nbr_x), h), sibling) for h in range(2)]
            from_y = [copy(3 + h, part(block(nbr_y), h), part(block(nbr_y), h), sibling) for h in range(2)]
            fwd_0 = copy(5, part(block(nbr_x), 0), part(block(nbr_x), 0), nbr_y)
            fwd_1 = copy(6, part(block(nbr_y), 1), part(block(nbr_y), 1), nbr_x)
            got_0 = copy(5, part(block(diag), 0), part(block(diag), 0), sibling)
            got_1 = copy(6, part(block(diag), 1), part(block(diag), 1), sibling)
            stages.append(([from_x[0]], [fwd_0]))
            stages.append(([from_y[1]], [fwd_1]))
            stages.append(([from_x[1]], [pass_on[0]]))
            stages.append(([from_y[0]], [pass_on[1]]))
            stages.append(([got_0, got_1], [pass_on[2]]))
        else:
            for j, dev in enumerate((nbr_x, nbr_y, diag)):
                begin.append(copy(1 + 2 * j, ins[a], block(me), dev))
                stages.append(([copy(1 + 2 * j, block(dev), block(dev), sibling)], [pass_on[j]]))
        other = (x, y, 1 - c)
        last.append(copy(0, block(other), block(other), sibling))
        for j, dev in enumerate((nbr_x, nbr_y, diag)):
            theirs = (dev[0], dev[1], 1 - c)
            last.append(copy(7 + j, block(theirs), block(theirs), sibling))
    return local, begin, stages, last


def _gather_begin(ins, outs, send_sems, recv_sems, local_sems):
    local, begin, _, _ = _gather_plan(ins, outs, send_sems, recv_sems, local_sems)
    for cp in local + begin:
        cp.start()


def _gather_finish(ins, outs, send_sems, recv_sems, local_sems):
    local, begin, stages, last = _gather_plan(ins, outs, send_sems, recv_sems, local_sems)
    started = []
    for arrivals, onward in stages:
        for cp in arrivals:
            cp.wait_recv()
        for cp in onward:
            cp.start()
            started.append(cp)
    for cp in last:
        cp.wait_recv()
    for cp in begin + started:
        cp.wait_send()
    for cp in local:
        cp.wait()


def _gather_scratch(n):
    return [pltpu.SemaphoreType.DMA((GATHER_SEMS * n,)), pltpu.SemaphoreType.DMA((GATHER_SEMS * n,)),
            pltpu.SemaphoreType.DMA((n,))]


_HBM = pl.BlockSpec(memory_space=pltpu.HBM)
_SEM = pl.BlockSpec(memory_space=pltpu.SEMAPHORE)
_DATAFLOW = pltpu.SideEffectType.DATAFLOW_SIDE_EFFECTING


def _to_all_plan(srcs, lands, send_sems, recv_sems):
    x, y, c = _position()
    me = 4 * x + 2 * y + c
    copies = []
    for a in range(len(srcs)):
        for k in range(1, N_DEV):
            fx, fy, fc = (k >> 2) & 1, (k >> 1) & 1, k & 1
            to = (1 - x if fx else x, 1 - y if fy else y, 1 - c if fc else c)
            copies.append(_remote(srcs[a], lands[a].at[me], send_sems, recv_sems, (N_DEV - 1) * a + k - 1, to))
    return copies


def _to_sibling_plan(srcs, lands, send_sems, recv_sems):
    x, y, c = _position()
    copies = []
    for a in range(len(srcs)):
        for q in range(4):
            copies.append(_remote(srcs[a].at[2 * q + (1 - c)], lands[a].at[q], send_sems, recv_sems, 4 * a + q,
                                  (x, y, 1 - c)))
    return copies


def _to_chips_plan(srcs, lands, send_sems, recv_sems):
    x, y, c = _position()
    copies = []
    for a in range(len(srcs)):
        for j, (cx, cy) in enumerate(_other_chips(x, y)):
            copies.append(_remote(srcs[a].at[2 * cx + cy], lands[a].at[j], send_sems, recv_sems, 3 * a + j, (cx, cy, c)))
    return copies


def copies_start(srcs, land_shapes, plan, per_array, name):
    n = len(srcs)
    n_sem = per_array * n
    lands = [lax.empty(s.shape, s.dtype) for s in land_shapes]

    def body(*refs):
        src_refs, land_refs = refs[:n], refs[n:2 * n]
        send_sems, recv_sems = refs[2 * n], refs[2 * n + 1]
        token = refs[-1]
        for cp in plan(src_refs, land_refs, send_sems, recv_sems):
            cp.start()
        token[...] = jnp.zeros_like(token)

    out = pl.pallas_call(
        body, name=name,
        out_shape=(pltpu.SemaphoreType.DMA((n_sem,)), pltpu.SemaphoreType.DMA((n_sem,)))
        + tuple(pltpu.HBM(s.shape, s.dtype) for s in srcs)
        + tuple(pltpu.HBM(s.shape, s.dtype) for s in land_shapes)
        + (jax.ShapeDtypeStruct((8, LANES), F32),),
        in_specs=[_HBM] * (2 * n),
        out_specs=(_SEM, _SEM) + (_HBM,) * (2 * n) + (pl.BlockSpec(memory_space=pltpu.VMEM),),
        input_output_aliases={i: 2 + i for i in range(2 * n)},
        compiler_params=pltpu.CompilerParams(has_side_effects=_DATAFLOW),
    )(*[pltpu.with_memory_space_constraint(s, pltpu.HBM) for s in srcs],
      *[pltpu.with_memory_space_constraint(l, pltpu.HBM) for l in lands])
    return out[:-1], out[-1]


def copies_wait(handles, plan, after, name):
    send_sems, recv_sems = handles[0], handles[1]
    n = (len(handles) - 2) // 2
    thru = handles[2:]

    def body(*refs):
        src_refs, land_refs = refs[:n], refs[n:2 * n]
        send_sems, recv_sems = refs[2 * n], refs[2 * n + 1]
        copies = plan(src_refs, land_refs, send_sems, recv_sems)
        for cp in copies:
            cp.wait_recv()
        for cp in copies:
            cp.wait_send()

    out = pl.pallas_call(
        body, name=name,
        out_shape=tuple(pltpu.HBM(t.shape, t.dtype) for t in thru),
        in_specs=[_HBM] * (2 * n) + [_SEM, _SEM, pl.BlockSpec(memory_space=pl.ANY)],
        out_specs=(_HBM,) * (2 * n),
        input_output_aliases={i: i for i in range(2 * n)},
        compiler_params=pltpu.CompilerParams(has_side_effects=_DATAFLOW),
    )(*thru, send_sems, recv_sems, after)
    return out[:n], out[n:]


def all_sum_small(vec, name):
    R = vec.shape[0]

    def body(v_ref, tot_ref, all_ref, send_sems, recv_sems):
        x, y, c = _position()
        me = 4 * x + 2 * y + c
        all_ref[me] = v_ref[...]
        copies = []
        for k in range(1, N_DEV):
            fx, fy, fc = (k >> 2) & 1, (k >> 1) & 1, k & 1
            to = (1 - x if fx else x, 1 - y if fy else y, 1 - c if fc else c)
            cp = _remote(v_ref, all_ref.at[me], send_sems, recv_sems, k - 1, to)
            cp.start()
            copies.append(cp)
        for cp in copies:
            cp.wait_recv()
        for cp in copies:
            cp.wait_send()
        tot = all_ref[0]
        for j in range(1, N_DEV):
            tot = tot + all_ref[j]
        tot_ref[...] = tot

    vmem = pl.BlockSpec(memory_space=pltpu.VMEM)
    return pl.pallas_call(
        body, name=name,
        in_specs=[vmem], out_specs=vmem,
        out_shape=jax.ShapeDtypeStruct((R, LANES), F32),
        scratch_shapes=[pltpu.VMEM((N_DEV, R, LANES), F32),
                        pltpu.SemaphoreType.DMA((N_DEV - 1,)), pltpu.SemaphoreType.DMA((N_DEV - 1,))],
        compiler_params=pltpu.CompilerParams(vmem_limit_bytes=VMEM_LIMIT),
    )(vec)


def pair_add(parts, theirs, place, name):
    _, R, C = theirs.shape
    tr = _pick(R, 1024, 8)

    def body(place_ref, a_ref, b_ref, o_ref):
        o_ref[...] = (a_ref[...].astype(F32) + b_ref[...].astype(F32)).astype(BF16)

    blk = pl.BlockSpec((None, tr, C), lambda q, i, place_ref: (q, i, 0))
    return pl.pallas_call(
        body, name=name,
        grid_spec=pltpu.PrefetchScalarGridSpec(
            num_scalar_prefetch=1, grid=(4, R // tr),
            in_specs=[pl.BlockSpec((None, tr, C), lambda q, i, place_ref: (2 * q + place_ref[2], i, 0)), blk],
            out_specs=blk),
        out_shape=jax.ShapeDtypeStruct(theirs.shape, BF16),
        compiler_params=_params(("parallel", "parallel")),
    )(place, parts, theirs)


def _adamw_math(w, g, m, v):
    m = ADAM_B1 * m + (1.0 - ADAM_B1) * g
    v = ADAM_B2 * v + (1.0 - ADAM_B2) * jnp.square(g)
    m_hat = m / (1.0 - ADAM_B1 ** ADAM_STEP)
    v_hat = v / (1.0 - ADAM_B2 ** ADAM_STEP)
    delta = -ADAM_LR * (m_hat / (jnp.sqrt(v_hat) + ADAM_EPS) + ADAM_WD * w)
    return delta, m, v


def adamw_sharded(w, m, v, parts, sib, others, place, name):
    R, C = w.shape
    tr = _pick(R, 256, 8)

    def body(place_ref, w_ref, m_ref, v_ref, a_ref, b_ref, o_ref, g_ref, d_ref, nm_ref, nv_ref):
        g = a_ref[...].astype(F32) + b_ref[...].astype(F32)
        for j in range(3):
            g = g + o_ref[j].astype(F32)
        delta, nm, nv = _adamw_math(w_ref[...], g, m_ref[...], v_ref[...])
        g_ref[...] = g
        d_ref[...] = delta
        nm_ref[...] = nm
        nv_ref[...] = nv

    row = pl.BlockSpec((tr, C), lambda i, place_ref: (i, 0))
    return pl.pallas_call(
        body, name=name,
        grid_spec=pltpu.PrefetchScalarGridSpec(
            num_scalar_prefetch=1, grid=(R // tr,),
            in_specs=[row] * 3 + [pl.BlockSpec((None, tr, C), lambda i, place_ref: (place_ref[0], i, 0)),
                                  pl.BlockSpec((None, tr, C), lambda i, place_ref: (place_ref[1], i, 0)),
                                  pl.BlockSpec((3, tr, C), lambda i, place_ref: (0, i, 0))],
            out_specs=[row] * 4),
        out_shape=[jax.ShapeDtypeStruct((R, C), F32)] * 4,
        compiler_params=_params(("parallel",)),
    )(place, w, m, v, parts, sib, others)


def adamw_packed(w, g, m, v, name):
    R = w.shape[0]

    def body(w_ref, g_ref, m_ref, v_ref, d_ref, nm_ref, nv_ref):
        delta, nm, nv = _adamw_math(w_ref[...], g_ref[...], m_ref[...], v_ref[...])
        d_ref[...] = delta
        nm_ref[...] = nm
        nv_ref[...] = nv

    full = pl.BlockSpec((R, LANES), lambda i: (0, 0))
    return pl.pallas_call(
        body, name=name, grid=(1,),
        in_specs=[full] * 4, out_specs=[full] * 3,
        out_shape=[jax.ShapeDtypeStruct((R, LANES), F32)] * 3,
        compiler_params=_params(("arbitrary",)),
    )(w, g, m, v)


def _pack(arrays):
    flat = []
    sizes = []
    for a in arrays:
        f = a.reshape(-1).astype(F32)
        pad = (-f.shape[0]) % LANES
        if pad:
            f = jnp.concatenate([f, jnp.zeros((pad,), F32)])
        flat.append(f)
        sizes.append(f.shape[0])
    rows = sum(sizes) // LANES
    pad_rows = (-rows) % 8
    if pad_rows:
        flat.append(jnp.zeros((pad_rows * LANES,), F32))
    return jnp.concatenate(flat).reshape(-1, LANES), sizes


def _unpack(packed, sizes, shapes):
    flat = packed.reshape(-1)
    out = []
    off = 0
    for size, shape in zip(sizes, shapes):
        n = int(np.prod(shape))
        out.append(flat[off:off + n].reshape(shape))
        off += size
    return out


def _to_blocks(full, axis):
    if axis == 0:
        return full.reshape(N_DEV, full.shape[0] // N_DEV, full.shape[1])
    r, n = full.shape
    return full.reshape(r, N_DEV, n // N_DEV).transpose(1, 0, 2)


def _from_blocks(blocks, axis):
    if axis == 0:
        return blocks.reshape(blocks.shape[0] * blocks.shape[1], blocks.shape[2])
    return blocks.transpose(1, 0, 2).reshape(blocks.shape[1], blocks.shape[0] * blocks.shape[2])


def kernel(x, ln0_g, ln0_b, w_in, b_in, conv_w, w_a, w_b, w_o, b_o, ln1_g, ln1_b, w_up, b_up, ffn_conv_w, ffn_conv_b, w_down, b_down, ln2_g, ln2_b, loss_target, m_ln0_g, m_ln0_b, m_w_in, m_b_in, m_conv_w, m_w_a, m_w_b, m_w_o, m_b_o, m_ln1_g, m_ln1_b, m_w_up, m_b_up, m_ffn_conv_w, m_ffn_conv_b, m_w_down, m_b_down, m_ln2_g, m_ln2_b, v_ln0_g, v_ln0_b, v_w_in, v_b_in, v_conv_w, v_w_a, v_w_b, v_w_o, v_b_o, v_ln1_g, v_ln1_b, v_w_up, v_b_up, v_ffn_conv_w, v_ffn_conv_b, v_w_down, v_b_down, v_ln2_g, v_ln2_b):
    T, D = x.shape[1], x.shape[2]
    F = ffn_conv_b.shape[-1]
    xs = x.reshape(T, D)
    tgt = loss_target.reshape(T, D)
    dev = 4 * lax.axis_index("x") + 2 * lax.axis_index("y") + lax.axis_index("c")
    chip = 2 * lax.axis_index("x") + lax.axis_index("y")
    core = lax.axis_index("c")
    place = jnp.stack([dev, chip, core]).astype(jnp.int32)

    big = dict(w_in=(w_in[0], 1), w_a=(w_a[0], 0), w_b=(w_b[0], 1), w_o=(w_o[0], 0), w_up=(w_up[0], 1),
               w_down=(w_down[0], 0))
    names = list(big)
    shards = {k: big[k][0].astype(BF16) for k in names}
    ln0g, ln0b = ln0_g.reshape(1, D), ln0_b.reshape(1, D)
    h0, h0b, *rest = ln_fwd(xs, None, ln0g, ln0b, "ln0_fwd_gather_w_in", dilations=DILATIONS[1:],
                            gather=[shards["w_in"], conv_w[0], ffn_conv_w[0]])
    h0_res = [h0b] + [h.reshape(T, D) for h in rest[:2]]
    g_in, g_conv, g_fcw = rest[2:]
    full = {"w_in": _from_blocks(g_in, 1)}
    conv_full = _from_blocks(g_conv, 1)
    fcw_full = _from_blocks(g_fcw, 1)
    late_groups = (("w_a", "w_b", "w_o"), ("w_up", "w_down"))
    late_handles = []
    token = conv_full[:1, :1] * 0.0
    for n, keys in enumerate(late_groups):
        srcs = [shards[k] + token[0, 0].astype(BF16) for k in keys]
        handles, token = copies_start(srcs, [jax.ShapeDtypeStruct((N_DEV,) + s.shape, BF16) for s in srcs],
                                      _to_all_plan, N_DEV - 1, f"gather_late_{n}_start")
        late_handles.append(handles)

    def late_weights(n, after):
        _, lands = copies_wait(late_handles[n], _to_all_plan, after, f"gather_late_{n}_wait")
        for k, land in zip(late_groups[n], lands):
            full[k] = _from_blocks(lax.dynamic_update_index_in_dim(land, shards[k], dev, 0), big[k][1])

    o_q = 3 * D
    o_g = o_q + 3 * QKV_W
    w_pa, w_qkv, w_pg = full["w_in"][:, :o_q], full["w_in"][:, o_q:o_g], full["w_in"][:, o_g:]
    b_pa, b_qkv, b_pg = b_in[:, :o_q], b_in[:, o_q:o_g], b_in[:, o_g:]

    proj_a = mm_nn(h0b, w_pa, b_pa, ACT, "proj_conv", after=token)
    proj_g = mm_nn(h0b, w_pg, b_pg, ACT, "proj_gates")
    zero_d = jnp.zeros((1, D), F32)
    s_a = conv_a_fwd(proj_a, conv_full, "conv_a_fwd")
    late_weights(0, s_a)
    y_a = mm_nn(s_a, full["w_a"], zero_d, ACT, "branch_a_out")

    def group_cols(m, g):
        return jnp.concatenate([m[:, s * QKV_W + g * GROUP_W:s * QKV_W + (g + 1) * GROUP_W] for s in range(3)], 1)

    w_grp = [group_cols(w_qkv, g) for g in range(3)]
    qkvs, outs, lses = [], [], []
    for g, d in enumerate(DILATIONS):
        qkv = mm_nn(h0_res[g], w_grp[g], group_cols(b_qkv, g), BF16, f"proj_qkv_{g}").reshape(d, T // d, 3 * GROUP_W)
        o, l = att_fwd(qkv, g, f"att_fwd_{g}")
        qkvs.append(qkv)
        outs.append(o)
        lses.append(l)
    comb = combine_fwd(outs, lses, "combine_fwd")
    y_b = mm_nn(comb, full["w_b"], zero_d, ACT, "branch_b_out")
    z = gate_fwd(proj_g, y_a, y_b, "gate_fwd")
    h1, h1b, mix = ln_fwd(h0, ("nn", z, full["w_o"], b_o), ln1_g, ln1_b, "mix_out_ln1_fwd")
    late_weights(1, h1b)
    up, f_act = ffn_up_conv_f(h1b, full["w_up"], b_up, fcw_full, ffn_conv_b, "ffn_up_conv_f")

    dr2, dr2b, d_ln2_g, d_ln2_b, d_b_down, loss_part = ln_bwd(
        h1, ("nn", f_act, full["w_down"], b_down), ln2_g, ln2_b, None, None, tgt, "ffn_down_ln2_loss_bwd")
    dw_down, _ = mm_tn(f_act, dr2b, "dw_down")
    d_a, d_gate, cs_a, cs_gate, d_fcb, d_fcw = conv_f_bwd(dr2b, full["w_down"], up, fcw_full, ffn_conv_b,
                                                          "d_ffn_act_conv_f_bwd")
    dw_up_a, _ = mm_tn(h1b, d_a, "dw_up_a")
    dw_up_g, _ = mm_tn(h1b, d_gate, "dw_up_gate")
    dr1, dr1b, d_ln1_g, d_ln1_b, d_b_o, _ = ln_bwd(h0, mix, ln1_g, ln1_b, dr2, ("nt", [d_a, d_gate], full["w_up"]), None,
                                                   "d_h1_ln1_bwd")
    dw_o, _ = mm_tn(z, dr1b, "dw_o")
    dy_a, dy_b, dproj_g = gate_bwd(dr1b, full["w_o"], proj_g, y_a, y_b, "d_z_gate_bwd")
    dw_a, _ = mm_tn(s_a, dy_a, "dw_a")
    dproj_a, d_conv = conv_a_bwd(dy_a, full["w_a"], proj_a, conv_full, "d_s_a_conv_a_bwd")
    dw_b, _ = mm_tn(comb, dy_b, "dw_b")

    rs_mine, rs_sib, rs_handles = {}, {}, {}

    sib_handles = {}

    def to_sibling_start(keys, grads, tag):
        parts = [_to_blocks(grads[k], big[k][1]) for k in keys]
        handles, tok = copies_start(parts, [jax.ShapeDtypeStruct((4,) + p.shape[1:], BF16) for p in parts],
                                    _to_sibling_plan, 4, f"grads_to_sibling_{tag}_start")
        sib_handles[tag] = (keys, handles)
        return tok

    def to_chips_start(tag, after):
        keys, handles = sib_handles[tag]
        parts, from_sib = copies_wait(handles, _to_sibling_plan, after, f"grads_to_sibling_{tag}_wait")
        sums = [pair_add(a, b, place, f"chip_sum_{k}") for k, a, b in zip(keys, parts, from_sib)]
        handles, tok = copies_start(sums, [jax.ShapeDtypeStruct((3,) + s.shape[1:], BF16) for s in sums],
                                    _to_chips_plan, 3, f"grads_to_chips_{tag}_start")
        for k, a, b in zip(keys, parts, from_sib):
            rs_mine[k], rs_sib[k] = a, b
        rs_handles[tag] = (keys, handles)
        return tok

    tok_a = to_sibling_start(("w_a", "w_b", "w_o", "w_up", "w_down"),
                             dict(w_a=dw_a, w_b=dw_b, w_o=dw_o, w_up=jnp.concatenate([dw_up_a, dw_up_g], 1),
                                  w_down=dw_down), "a")
    dcomb = mm_nt(dy_b, full["w_b"], None, "d_comb", after=tok_a, out_dtype=ACT)
    dos, dms = combine_bwd(dcomb, outs, lses, "combine_bwd")
    tok_a = to_chips_start("a", dms[0])
    dw_grp, cs_grp, dqkvs = [], [], []
    for g, d in enumerate(DILATIONS):
        dq, dk, dv = att_bwd(qkvs[g], dos[g], lses[g], dms[g], g, f"att_bwd_{g}", after=tok_a if g == 0 else None)
        dqkv = [t.reshape(T, GROUP_W) for t in (dq, dk, dv)]
        dwg, csg = mm_tn(h0_res[g], dqkv, f"dw_in_qkv_{g}")
        dqkvs.append(dqkv)
        dw_grp.append(dwg)
        cs_grp.append(csg)
    dw_pa, cs_pa = mm_tn(h0b, dproj_a, "dw_in_conv")
    dw_pg, cs_pg = mm_tn(h0b, dproj_g, "dw_in_gates")

    def ungroup(parts):
        return jnp.concatenate([p[:, s * GROUP_W:(s + 1) * GROUP_W] for s in range(3) for p in parts], 1)

    db_in_parts = [cs_pa, ungroup(cs_grp), cs_pg]
    tok_b = to_sibling_start(("w_in",), dict(w_in=jnp.concatenate([dw_pa, ungroup(dw_grp), dw_pg], 1)), "b")
    dh0 = mm_nt(dproj_a, w_pa, None, "d_h0_conv", after=tok_b)
    tok_b = to_chips_start("b", dh0)
    dh0 = mm_nt(dproj_g, w_pg, dh0, "d_h0_gates", after=tok_b)
    dh0_res = [(mm_nt(dqkvs[g], w_grp[g], None, f"d_h0_qkv_{g}").reshape(d, T // d, D), d)
               for g, d in enumerate(DILATIONS) if g > 0]
    dx, _, d_ln0_g, d_ln0_b, _, _ = ln_bwd(xs, None, ln0g, ln0b, dr1, ("nt", dqkvs[0], w_grp[0]), None, "d_h0_ln0_bwd",
                                           by_residue=[(dh0.reshape(1, T, D), 1)] + dh0_res)

    small = [d_ln0_g, d_ln0_b, jnp.concatenate(db_in_parts, 1), d_conv, d_b_o, d_ln1_g, d_ln1_b,
             jnp.concatenate([cs_a, cs_gate], 1), d_fcw, d_fcb, d_b_down, d_ln2_g, d_ln2_b, loss_part]
    packed, sizes = _pack(small)
    total = all_sum_small(packed, "sum_small")
    (g_ln0_g, g_ln0_b, g_b_in, g_conv_full, g_b_o, g_ln1_g, g_ln1_b, g_b_up, g_fcw_full, g_fcb, g_b_down, g_ln2_g,
     g_ln2_b, loss) = _unpack(total, sizes, [a.shape for a in small])
    cw = conv_w.shape[-1]
    fw = ffn_conv_w.shape[-1]
    g_conv = lax.dynamic_slice_in_dim(g_conv_full, dev * cw, cw, 1)
    g_fcw = lax.dynamic_slice_in_dim(g_fcw_full, dev * fw, fw, 1)

    from_chips = {}
    for tag, (keys, handles) in rs_handles.items():
        _, lands = copies_wait(handles, _to_chips_plan, total, f"grads_to_chips_{tag}_wait")
        from_chips.update(zip(keys, lands))

    moments = dict(w_in=(m_w_in, v_w_in), w_a=(m_w_a, v_w_a), w_b=(m_w_b, v_w_b), w_o=(m_w_o, v_w_o),
                   w_up=(m_w_up, v_w_up), w_down=(m_w_down, v_w_down))
    res_big = {}
    for k in names:
        res_big[k] = adamw_sharded(big[k][0], moments[k][0][0], moments[k][1][0], rs_mine[k], rs_sib[k], from_chips[k],
                                   place, f"adamw_{k}")

    small_names = ["ln0_g", "ln0_b", "b_in", "conv_w", "b_o", "ln1_g", "ln1_b", "b_up", "ffn_conv_w", "ffn_conv_b",
                   "b_down", "ln2_g", "ln2_b"]
    small_w = [ln0_g, ln0_b, b_in, conv_w, b_o, ln1_g, ln1_b, b_up, ffn_conv_w, ffn_conv_b, b_down, ln2_g, ln2_b]
    small_m = [m_ln0_g, m_ln0_b, m_b_in, m_conv_w, m_b_o, m_ln1_g, m_ln1_b, m_b_up, m_ffn_conv_w, m_ffn_conv_b,
               m_b_down, m_ln2_g, m_ln2_b]
    small_v = [v_ln0_g, v_ln0_b, v_b_in, v_conv_w, v_b_o, v_ln1_g, v_ln1_b, v_b_up, v_ffn_conv_w, v_ffn_conv_b,
               v_b_down, v_ln2_g, v_ln2_b]
    small_g = [g_ln0_g, g_ln0_b, g_b_in, g_conv, g_b_o, g_ln1_g, g_ln1_b, g_b_up, g_fcw, g_fcb, g_b_down, g_ln2_g,
               g_ln2_b]
    shapes = [w.shape for w in small_w]
    small_g = [g.reshape(s) for g, s in zip(small_g, shapes)]
    pw, psz = _pack(small_w)
    pg, _ = _pack(small_g)
    pm, _ = _pack(small_m)
    pv, _ = _pack(small_v)
    pd, pnm, pnv = adamw_packed(pw, pg, pm, pv, "adamw_small")
    res_small = {k: (g, d_, m_, v_) for k, g, d_, m_, v_ in zip(
        small_names, small_g, _unpack(pd, psz, shapes), _unpack(pnm, psz, shapes), _unpack(pnv, psz, shapes))}

    order = ["ln0_g", "ln0_b", "w_in", "b_in", "conv_w", "w_a", "w_b", "w_o", "b_o", "ln1_g", "ln1_b", "w_up", "b_up",
             "ffn_conv_w", "ffn_conv_b", "w_down", "b_down", "ln2_g", "ln2_b"]

    def result(k, j):
        if k in res_big:
            return res_big[k][j][None]
        return res_small[k][j]

    out = [loss.reshape(()), dx.reshape(x.shape)]
    for j in range(4):
        out += [result(k, j) for k in order]
    return tuple(out)
```

```python
import math

import numpy as np
import jax
import jax.numpy as jnp
from jax import lax
from jax.experimental import pallas as pl
from jax.experimental.pallas import tpu as pltpu

F32 = jnp.float32
BF16 = jnp.bfloat16
ACT = BF16

N_DEV = 8
LN_EPS = 1e-5
ALPHA = (2.0 * 1) ** 0.25
HEAD_DIM = 64
GROUP_W = 512
QKV_W = 3 * GROUP_W
DILATIONS = (1, 4, 16)
RADIUS = 64
LANES = 128
HALO = 8
HALO_BF16 = 16
ATT_TQ = 128

ADAM_LR = 0.001
ADAM_B1 = 0.9
ADAM_B2 = 0.999
ADAM_EPS = 1e-08
ADAM_WD = 0.01
ADAM_STEP = 10

VMEM_LIMIT = 52 * 1024 * 1024
OUT_TILE_BYTES = 8 * 1024 * 1024
MAX_K_TALL_TILE = 4096
MESH = pl.DeviceIdType.MESH
NT_DIMS = (((1,), (1,)), ((), ()))
TN_DIMS = (((0,), (0,)), ((), ()))


def _pick(n, target, align=LANES):
    if n <= target:
        return n
    best = None
    for t in range(align, target + 1, align):
        if n % t == 0:
            best = t
    assert best is not None, (n, target, align)
    return best


def _params(sems=None):
    return pltpu.CompilerParams(dimension_semantics=sems, vmem_limit_bytes=VMEM_LIMIT)


def _alibi_slopes():
    n = 3 * 8
    return np.exp2(-8.0 * np.arange(1, n + 1, dtype=np.float64) / n).astype(np.float32).reshape(3, 8)


def _ln_stats(r):
    mu = jnp.mean(r, -1, keepdims=True)
    xc = r - mu
    var = jnp.mean(xc * xc, -1, keepdims=True)
    rstd = lax.rsqrt(var + LN_EPS)
    return xc, rstd


def _load_natural(ref, d, scr):
    if d == 1:
        return ref[0].astype(F32)
    n, C = ref.shape[1], ref.shape[2]
    for c in range(C // LANES):
        for r in range(d):
            scr[c, pl.ds(r, n, stride=d), :] = ref[r, :, c * LANES:(c + 1) * LANES].astype(F32)
    return jnp.concatenate([scr[c] for c in range(C // LANES)], axis=1)


def _store_by_residue(val, ref, d, scr):
    if d == 1:
        ref[0] = val.astype(ref.dtype)
        return
    n, C = ref.shape[1], ref.shape[2]
    for c in range(C // LANES):
        scr[c] = val[:, c * LANES:(c + 1) * LANES]
    for c in range(C // LANES):
        for r in range(d):
            ref[r, :, c * LANES:(c + 1) * LANES] = scr[c, pl.ds(r, n, stride=d), :].astype(ref.dtype)


def _residue_spec(tm, d, C):
    return pl.BlockSpec((d, tm // d, C), lambda i: (0, i, 0))


def _residue_scratch(tm, C):
    return pltpu.VMEM((C // LANES, tm, LANES), F32)


def ln_fwd(a, res, g, b, name, dilations=(), gather=()):
    T, D = a.shape
    res_mm = isinstance(res, tuple)
    tm = _pick(T, 512, 8)
    res_ins = list(res[1:]) if res_mm else ([] if res is None else [res])
    nd = len(dilations)
    ng = len(gather)
    n_in = 1 + len(res_ins) + 2
    last = T // tm - 1

    def body(*refs):
        a_ref = refs[0]
        r = a_ref[...]
        if res_mm:
            res_val = jnp.dot(refs[1][...], refs[2][...], preferred_element_type=F32) + refs[3][...]
            refs[-1 - n_scratch][...] = res_val
            r = ALPHA * r + res_val
        elif res_ins:
            r = ALPHA * r + refs[1][...]
        g_ref, b_ref = refs[n_in - 2], refs[n_in - 1]
        shard_refs = refs[n_in:n_in + ng]
        h_ref, hb_ref = refs[n_in + ng], refs[n_in + ng + 1]
        p_refs = refs[n_in + ng + 2:n_in + ng + 2 + nd]
        full_refs = refs[n_in + ng + 2 + nd:n_in + 2 * ng + 2 + nd]
        scratch = refs[len(refs) - n_scratch:]
        sems = scratch[len(scratch) - 3:] if ng else ()

        if ng:
            @pl.when(pl.program_id(0) == 0)
            def _():
                _gather_begin(shard_refs, full_refs, *sems)

        xc, rstd = _ln_stats(r)
        h = xc * rstd * g_ref[...] + b_ref[...]
        h_ref[...] = h
        hb_ref[...] = h.astype(BF16)
        for d, p_ref in zip(dilations, p_refs):
            _store_by_residue(h, p_ref, d, scratch[0])

        if ng:
            @pl.when(pl.program_id(0) == last)
            def _():
                _gather_finish(shard_refs, full_refs, *sems)

    row = pl.BlockSpec((tm, D), lambda i: (i, 0))
    vec = pl.BlockSpec((1, D), lambda i: (0, 0))
    hbm = pl.BlockSpec(memory_space=pl.ANY)
    if res_mm:
        res_specs = [pl.BlockSpec((tm, res[1].shape[1]), lambda i: (i, 0)), pl.BlockSpec(res[2].shape, lambda i: (0, 0)), vec]
    else:
        res_specs = [row] * len(res_ins)
    scratch_shapes = ([_residue_scratch(tm, D)] if nd else []) + (_gather_scratch(ng) if ng else [])
    n_scratch = len(scratch_shapes)
    ins = [a] + res_ins + [g, b] + list(gather)
    return pl.pallas_call(
        body, name=name, grid=(T // tm,),
        in_specs=[row] + res_specs + [vec, vec] + [hbm] * ng,
        out_specs=[row, row] + [_residue_spec(tm, d, D) for d in dilations] + [hbm] * ng + ([row] if res_mm else []),
        out_shape=[jax.ShapeDtypeStruct((T, D), F32), jax.ShapeDtypeStruct((T, D), BF16)]
        + [jax.ShapeDtypeStruct((d, T // d, D), BF16) for d in dilations]
        + [jax.ShapeDtypeStruct((N_DEV,) + s.shape, s.dtype) for s in gather]
        + ([jax.ShapeDtypeStruct((T, D), F32)] if res_mm else []),
        scratch_shapes=scratch_shapes,
        compiler_params=_params(("arbitrary",) if ng else ("parallel",)),
    )(*ins)


def ln_bwd(a, res, g, b, d1, d2, tgt, name, by_residue=()):
    T, D = a.shape
    wide_product = isinstance(d2, tuple) and sum(p.shape[1] for p in d2[1]) > MAX_K_TALL_TILE
    fused = isinstance(res, tuple) or isinstance(d2, tuple)
    tm = _pick(T, 512 if fused and not wide_product else 256, 8)
    loss_mode = tgt is not None
    nres = len(by_residue)
    row = pl.BlockSpec((tm, D), lambda i: (i, 0))
    vec = pl.BlockSpec((1, D), lambda i: (0, 0))
    one = pl.BlockSpec((1, 1), lambda i: (0, 0))

    def rows_of(x):
        return pl.BlockSpec((tm, x.shape[1]), lambda i: (i, 0))

    def whole(x):
        return pl.BlockSpec(x.shape, lambda i: (0, 0))

    ins, in_specs, slots = [], [], {}

    def operand(key, arrays, specs):
        slots[key] = (len(ins), len(arrays))
        ins.extend(arrays)
        in_specs.extend(specs)

    operand("a", [a], [row])
    if isinstance(res, tuple):
        _, x, w, bias = res
        operand("res_mm", [x, w, bias], [rows_of(x), whole(w), vec])
    elif res is not None:
        operand("res", [res], [row])
    operand("gb", [g, b], [vec, vec])
    if loss_mode:
        operand("tgt", [tgt], [row])
    else:
        operand("d1", [d1], [row])
        if isinstance(d2, tuple):
            _, pieces, w = d2
            operand("d2_mm", list(pieces) + [w], [rows_of(p) for p in pieces] + [whole(w)])
        else:
            operand("d2", [d2], [row])
    operand("by_residue", [e for e, _ in by_residue], [_residue_spec(tm, d, D) for _, d in by_residue])
    n_in = len(ins)

    def body(*refs):
        def get(key):
            first, count = slots[key]
            return refs[first:first + count]

        dr_ref, drb_ref, dg_ref, db_ref, ds_ref, loss_ref = refs[n_in:n_in + 6]
        i = pl.program_id(0)

        @pl.when(i == 0)
        def _():
            dg_ref[...] = jnp.zeros_like(dg_ref)
            db_ref[...] = jnp.zeros_like(db_ref)
            ds_ref[...] = jnp.zeros_like(ds_ref)
            loss_ref[...] = jnp.zeros_like(loss_ref)

        r = get("a")[0][...]
        if "res_mm" in slots:
            x_ref, w_ref, bias_ref = get("res_mm")
            r = ALPHA * r + (jnp.dot(x_ref[...], w_ref[...], preferred_element_type=F32) + bias_ref[...])
        elif "res" in slots:
            r = ALPHA * r + get("res")[0][...]
        g_ref, b_ref = get("gb")
        xc, rstd = _ln_stats(r)
        xhat = xc * rstd
        gam = g_ref[...]
        if loss_mode:
            err = xhat * gam + b_ref[...] - get("tgt")[0][...]
            dy = err * (1.0 / D)
            row_loss = jnp.mean(err * err, -1, keepdims=True)
            loss_ref[...] += 0.5 * jnp.sum(row_loss, 0, keepdims=True)
        else:
            if "d2_mm" in slots:
                *p_refs, w_ref = get("d2_mm")
                av = p_refs[0][...] if len(p_refs) == 1 else jnp.concatenate([p[...] for p in p_refs], axis=1)
                d2v = lax.dot_general(av, w_ref[...], NT_DIMS, preferred_element_type=F32)
            else:
                d2v = get("d2")[0][...]
            dy = ALPHA * get("d1")[0][...] + d2v
        for (_, d), e_ref in zip(by_residue, get("by_residue")):
            dy = dy + _load_natural(e_ref, d, refs[-1])
        dyg = dy * gam
        c1 = jnp.mean(dyg, -1, keepdims=True)
        c2 = jnp.mean(dyg * xhat, -1, keepdims=True)
        dr = rstd * (dyg - c1 - xhat * c2)
        dr_ref[...] = dr
        drb_ref[...] = dr.astype(BF16)
        dg_ref[...] += jnp.sum(dy * xhat, 0, keepdims=True)
        db_ref[...] += jnp.sum(dy, 0, keepdims=True)
        ds_ref[...] += jnp.sum(dr, 0, keepdims=True)

    return pl.pallas_call(
        body, name=name, grid=(T // tm,),
        in_specs=in_specs,
        out_specs=[row, row, vec, vec, vec, one],
        out_shape=[jax.ShapeDtypeStruct((T, D), F32), jax.ShapeDtypeStruct((T, D), BF16),
                   jax.ShapeDtypeStruct((1, D), F32), jax.ShapeDtypeStruct((1, D), F32),
                   jax.ShapeDtypeStruct((1, D), F32), jax.ShapeDtypeStruct((1, 1), F32)],
        scratch_shapes=[_residue_scratch(tm, D)] if nres else [],
        compiler_params=_params(("arbitrary",)),
    )(*ins)


_TOKEN_SPEC = pl.BlockSpec((8, LANES), lambda i: (0, 0))


def mm_nn(a, w, bias, out_dtype, name, after=None):
    M, K = a.shape
    N = w.shape[1]
    tm = _pick(M, max(256, min(1024, OUT_TILE_BYTES // (N * jnp.dtype(out_dtype).itemsize))), 8)
    tc = _pick(N, 512)

    def body(a_ref, w_ref, b_ref, *rest):
        o_ref = rest[-1]
        av = a_ref[...]
        for j in range(N // tc):
            cols = slice(j * tc, (j + 1) * tc)
            acc = jnp.dot(av, w_ref[:, cols], preferred_element_type=F32)
            o_ref[:, cols] = (acc + b_ref[:, cols]).astype(out_dtype)

    return pl.pallas_call(
        body, name=name, grid=(M // tm,),
        in_specs=[pl.BlockSpec((tm, K), lambda i: (i, 0)),
                  pl.BlockSpec((K, N), lambda i: (0, 0)),
                  pl.BlockSpec((1, N), lambda i: (0, 0))] + ([] if after is None else [_TOKEN_SPEC]),
        out_specs=pl.BlockSpec((tm, N), lambda i: (i, 0)),
        out_shape=jax.ShapeDtypeStruct((M, N), out_dtype),
        compiler_params=_params(("parallel",)),
    )(a, w, bias, *([] if after is None else [after]))


def mm_nt(a, w, acc_in, name, after=None, w_block=0, out_dtype=F32):
    pieces = list(a) if isinstance(a, (list, tuple)) else [a]
    M = pieces[0].shape[0]
    widths = [p.shape[1] for p in pieces]
    K = sum(widths)
    N = w.shape[0]
    tm = _pick(M, 1024, 8)
    tc = _pick(N, 512)
    has_acc = acc_in is not None
    n_a = len(pieces)

    def body(*refs):
        a_refs, w_ref = refs[:n_a], refs[n_a]
        c_ref = refs[n_a + 1] if has_acc else None
        o_ref = refs[-1]
        av = a_refs[0][...] if n_a == 1 else jnp.concatenate([r[...] for r in a_refs], axis=1)
        for j in range(N // tc):
            cols = slice(j * tc, (j + 1) * tc)
            acc = lax.dot_general(av, w_ref[cols, :], NT_DIMS, preferred_element_type=F32)
            if has_acc:
                acc = acc + c_ref[:, cols]
            o_ref[:, cols] = acc.astype(out_dtype)

    out_spec = pl.BlockSpec((tm, N), lambda i: (i, 0))
    in_specs = [pl.BlockSpec((tm, kw), lambda i: (i, 0)) for kw in widths]
    in_specs.append(pl.BlockSpec((N, K), lambda i: (0, w_block)))
    ins = pieces + [w]
    if has_acc:
        in_specs.append(out_spec)
        ins.append(acc_in)
    if after is not None:
        in_specs.append(_TOKEN_SPEC)
        ins.append(after)
    return pl.pallas_call(
        body, name=name, grid=(M // tm,),
        in_specs=in_specs, out_specs=out_spec,
        out_shape=jax.ShapeDtypeStruct((M, N), out_dtype),
        compiler_params=_params(("parallel",)),
    )(*ins)


def mm_tn(a, b, name, out_dtype=BF16):
    pieces = list(b) if isinstance(b, (list, tuple)) else [b]
    T, M = a.shape
    widths = [p.shape[1] for p in pieces]
    N = sum(widths)
    tk = _pick(T, 1024, 8)
    nk = T // tk
    tc = _pick(M, 256)
    n_b = len(pieces)

    def body(*refs):
        a_ref, b_refs = refs[0], refs[1:1 + n_b]
        o_ref, cs_ref, acc_ref = refs[1 + n_b:]
        k = pl.program_id(0)

        @pl.when(k == 0)
        def _():
            acc_ref[...] = jnp.zeros_like(acc_ref)
            cs_ref[...] = jnp.zeros_like(cs_ref)

        bv = b_refs[0][...] if n_b == 1 else jnp.concatenate([r[...] for r in b_refs], axis=1)
        cs_ref[...] += jnp.sum(bv.astype(F32), 0, keepdims=True)
        for mi in range(M // tc):
            rows = slice(mi * tc, (mi + 1) * tc)
            acc_ref[rows, :] += lax.dot_general(a_ref[:, rows], bv, TN_DIMS, preferred_element_type=F32)

        @pl.when(k == nk - 1)
        def _():
            o_ref[...] = acc_ref[...].astype(out_dtype)

    return pl.pallas_call(
        body, name=name, grid=(nk,),
        in_specs=[pl.BlockSpec((tk, M), lambda k: (k, 0))] + [pl.BlockSpec((tk, wd), lambda k: (k, 0)) for wd in widths],
        out_specs=[pl.BlockSpec((M, N), lambda k: (0, 0)), pl.BlockSpec((1, N), lambda k: (0, 0))],
        out_shape=[jax.ShapeDtypeStruct((M, N), out_dtype), jax.ShapeDtypeStruct((1, N), F32)],
        scratch_shapes=[pltpu.VMEM((M, N), F32)],
        compiler_params=_params(("arbitrary",)),
    )(a, *pieces)


def _ext_rows(prev_ref, main_ref, next_ref, i, tm, T, dtype=F32):
    before = jnp.where(i == 0, 0.0, prev_ref[...])
    after = jnp.where(i == T // tm - 1, 0.0, next_ref[...])
    return jnp.concatenate([before, main_ref[...], after], axis=0).astype(dtype)


def _prev_row(x):
    return pltpu.roll(x, 1, 0)


def _next_row(x):
    return pltpu.roll(x, x.shape[0] - 1, 0)


def _conv3(u, w_ref):
    return _prev_row(u) * w_ref[0:1, :] + u * w_ref[1:2, :] + _next_row(u) * w_ref[2:3, :]


def _main(x, tm, halo=HALO):
    return x[halo:halo + tm]


def _halo_specs(tm, tc, T, col, order, halo=HALO):
    r = tm // halo
    last = T // halo - 1
    if order == "ij":
        return (pl.BlockSpec((halo, tc), lambda i, j: (jnp.maximum(i * r - 1, 0), col(j))),
                pl.BlockSpec((tm, tc), lambda i, j: (i, col(j))),
                pl.BlockSpec((halo, tc), lambda i, j: (jnp.minimum((i + 1) * r, last), col(j))))
    return (pl.BlockSpec((halo, tc), lambda j, i: (jnp.maximum(i * r - 1, 0), col(j))),
            pl.BlockSpec((tm, tc), lambda j, i: (i, col(j))),
            pl.BlockSpec((halo, tc), lambda j, i: (jnp.minimum((i + 1) * r, last), col(j))))


def conv_a_fwd(proj_a, conv_w, name):
    T, D3 = proj_a.shape
    D = D3 // 3
    tm = _pick(T, 256, 8)

    def body(p_ref, m_ref, n_ref, w_ref, o_ref):
        i = pl.program_id(0)
        ext = _ext_rows(p_ref, m_ref, n_ref, i, tm, T)
        u = ext[:, D:2 * D] * ext[:, 2 * D:]
        cu = _conv3(u, w_ref)
        o_ref[...] = (m_ref[:, :D].astype(F32) * _main(cu, tm, HALO_BF16)).astype(BF16)

    prev, main, nxt = _halo_specs(tm, D3, T, lambda j: 0, "ij", HALO_BF16)
    return pl.pallas_call(
        body, name=name, grid=(T // tm, 1),
        in_specs=[prev, main, nxt, pl.BlockSpec((3, D), lambda i, j: (0, 0))],
        out_specs=pl.BlockSpec((tm, D), lambda i, j: (i, 0)),
        out_shape=jax.ShapeDtypeStruct((T, D), BF16),
        compiler_params=_params(("parallel", "arbitrary")),
    )(proj_a, proj_a, proj_a, conv_w)


def conv_a_bwd(dy_a, w_a, proj_a, conv_w, name):
    T, D3 = proj_a.shape
    D = D3 // 3
    tm = _pick(T, 256, 8)

    def body(dp_ref, dm_ref, dn_ref, wa_ref, p_ref, m_ref, n_ref, w_ref, o_ref, dw_ref):
        i = pl.program_id(0)

        @pl.when(i == 0)
        def _():
            dw_ref[...] = jnp.zeros_like(dw_ref)

        ext = _ext_rows(p_ref, m_ref, n_ref, i, tm, T)
        dsa = lax.dot_general(_ext_rows(dp_ref, dm_ref, dn_ref, i, tm, T, dtype=BF16), wa_ref[...], NT_DIMS,
                              preferred_element_type=F32)
        gb, gc, hin = ext[:, :D], ext[:, D:2 * D], ext[:, 2 * D:]
        u = gc * hin
        u_prev, u_next = _prev_row(u), _next_row(u)
        cu = u_prev * w_ref[0:1, :] + u * w_ref[1:2, :] + u_next * w_ref[2:3, :]
        dcu = dsa * gb
        du = _next_row(dcu) * w_ref[0:1, :] + dcu * w_ref[1:2, :] + _prev_row(dcu) * w_ref[2:3, :]
        h = HALO_BF16
        o_ref[:, :D] = _main(dsa * cu, tm, h).astype(BF16)
        o_ref[:, D:2 * D] = _main(du * hin, tm, h).astype(BF16)
        o_ref[:, 2 * D:] = _main(du * gc, tm, h).astype(BF16)
        dcu_m = _main(dcu, tm, h)
        dw_ref[0:1, :] += jnp.sum(dcu_m * _main(u_prev, tm, h), 0, keepdims=True)
        dw_ref[1:2, :] += jnp.sum(dcu_m * _main(u, tm, h), 0, keepdims=True)
        dw_ref[2:3, :] += jnp.sum(dcu_m * _main(u_next, tm, h), 0, keepdims=True)

    dprev, dmain, dnxt = _halo_specs(tm, dy_a.shape[1], T, lambda j: 0, "ij", HALO_BF16)
    prev, main, nxt = _halo_specs(tm, D3, T, lambda j: 0, "ij", HALO_BF16)
    return pl.pallas_call(
        body, name=name, grid=(T // tm, 1),
        in_specs=[dprev, dmain, dnxt, pl.BlockSpec(w_a.shape, lambda i, j: (0, 0)), prev, main, nxt,
                  pl.BlockSpec((3, D), lambda i, j: (0, 0))],
        out_specs=[pl.BlockSpec((tm, D3), lambda i, j: (i, 0)), pl.BlockSpec((3, D), lambda i, j: (0, 0))],
        out_shape=[jax.ShapeDtypeStruct((T, D3), BF16), jax.ShapeDtypeStruct((3, D), F32)],
        compiler_params=_params(("arbitrary", "arbitrary")),
    )(dy_a, dy_a, dy_a, w_a, proj_a, proj_a, proj_a, conv_w)


_INV_SQRT2 = 1.0 / math.sqrt(2.0)
_INV_SQRT_2PI = 1.0 / math.sqrt(2.0 * math.pi)


def ffn_up_conv_f(h, w_up, b_up, fcw, fcb, name):
    T, D = h.shape
    F = fcb.shape[1]
    tm = _pick(T, 256, 8)
    tc = _pick(F, 256)
    halo = HALO_BF16

    def body(hp_ref, hm_ref, hn_ref, w_ref, b_ref, cw_ref, cb_ref, up_ref, f_ref):
        i = pl.program_id(0)
        h_ext = _ext_rows(hp_ref, hm_ref, hn_ref, i, tm, T, dtype=BF16)
        h_main = hm_ref[...]
        rows = i * tm - halo + lax.broadcasted_iota(jnp.int32, (tm + 2 * halo, 1), 0)
        inside = (rows >= 0) & (rows < T)
        for c in range(F // tc):
            cols = slice(c * tc, (c + 1) * tc)
            gcols = slice(F + c * tc, F + (c + 1) * tc)
            a_ext = jnp.dot(h_ext, w_ref[:, cols], preferred_element_type=F32) + b_ref[:, cols]
            a_ext = jnp.where(inside, a_ext, 0.0)
            gate = jnp.dot(h_main, w_ref[:, gcols], preferred_element_type=F32) + b_ref[:, gcols]
            up_ref[:, cols] = _main(a_ext, tm, halo)
            up_ref[:, gcols] = gate
            ca = _main(_prev_row(a_ext) * cw_ref[0:1, cols] + a_ext * cw_ref[1:2, cols]
                       + _next_row(a_ext) * cw_ref[2:3, cols], tm, halo) + cb_ref[:, cols]
            gl = 0.5 * ca * (1.0 + lax.erf(ca * _INV_SQRT2))
            f_ref[:, cols] = (gl * gate).astype(BF16)

    prev, main, nxt = _halo_specs(tm, D, T, lambda j: 0, "ij", halo)
    whole = lambda x: pl.BlockSpec(x.shape, lambda i, j: (0, 0))
    return pl.pallas_call(
        body, name=name, grid=(T // tm, 1),
        in_specs=[prev, main, nxt, whole(w_up), whole(b_up), whole(fcw), whole(fcb)],
        out_specs=[pl.BlockSpec((tm, 2 * F), lambda i, j: (i, 0)), pl.BlockSpec((tm, F), lambda i, j: (i, 0))],
        out_shape=[jax.ShapeDtypeStruct((T, 2 * F), F32), jax.ShapeDtypeStruct((T, F), BF16)],
        compiler_params=_params(("parallel", "arbitrary")),
    )(h, h, h, w_up, b_up, fcw, fcb)


def conv_f_bwd(dy, w_down, up, fcw, fcb, name):
    T, F2 = up.shape
    F = F2 // 2
    D = dy.shape[1]
    tm = _pick(T, 256, 8)
    tc = _pick(F, 256)

    def body(yp_ref, ym_ref, yn_ref, wd_ref, up_ref, um_ref, un_ref, w_ref, b_ref,
             da_ref, dg_ref, csa_ref, csg_ref, dfb_ref, dfw_ref):
        i = pl.program_id(0)
        first, last = i == 0, i == T // tm - 1

        @pl.when(first)
        def _():
            csa_ref[...] = jnp.zeros_like(csa_ref)
            csg_ref[...] = jnp.zeros_like(csg_ref)
            dfb_ref[...] = jnp.zeros_like(dfb_ref)
            dfw_ref[...] = jnp.zeros_like(dfw_ref)

        def ext(cols):
            return jnp.concatenate([jnp.where(first, 0.0, up_ref[:, cols]), um_ref[:, cols],
                                    jnp.where(last, 0.0, un_ref[:, cols])], axis=0)

        dy_ext = _ext_rows(yp_ref, ym_ref, yn_ref, i, tm, T, dtype=BF16)
        for c in range(F // tc):
            cols = slice(c * tc, (c + 1) * tc)
            dfe = lax.dot_general(dy_ext, wd_ref[cols, :], NT_DIMS, preferred_element_type=F32)
            dfe = dfe[HALO_BF16 - HALO:HALO_BF16 + tm + HALO]
            a = ext(cols)
            gate = ext(slice(F + c * tc, F + (c + 1) * tc))
            a_prev, a_next = _prev_row(a), _next_row(a)
            ca = a_prev * w_ref[0:1, cols] + a * w_ref[1:2, cols] + a_next * w_ref[2:3, cols] + b_ref[:, cols]
            cdf = 0.5 * (1.0 + lax.erf(ca * _INV_SQRT2))
            gl = ca * cdf
            gp = cdf + ca * (jnp.exp(-0.5 * ca * ca) * _INV_SQRT_2PI)
            dgate = _main(dfe * gl, tm)
            dca = dfe * gate * gp
            da = _main(_next_row(dca) * w_ref[0:1, cols] + dca * w_ref[1:2, cols] + _prev_row(dca) * w_ref[2:3, cols],
                       tm)
            da_ref[:, cols] = da.astype(BF16)
            dg_ref[:, cols] = dgate.astype(BF16)
            csa_ref[:, cols] += jnp.sum(da, 0, keepdims=True)
            csg_ref[:, cols] += jnp.sum(dgate, 0, keepdims=True)
            dca_m = _main(dca, tm)
            dfb_ref[:, cols] += jnp.sum(dca_m, 0, keepdims=True)
            dfw_ref[0:1, cols] += jnp.sum(dca_m * _main(a_prev, tm), 0, keepdims=True)
            dfw_ref[1:2, cols] += jnp.sum(dca_m * _main(a, tm), 0, keepdims=True)
            dfw_ref[2:3, cols] += jnp.sum(dca_m * _main(a_next, tm), 0, keepdims=True)

    uprev, umain, unxt = _halo_specs(tm, F2, T, lambda j: 0, "ij")
    yprev, ymain, ynxt = _halo_specs(tm, D, T, lambda j: 0, "ij", HALO_BF16)
    whole = lambda shape: pl.BlockSpec(shape, lambda i, j: (0, 0))
    tile = pl.BlockSpec((tm, F), lambda i, j: (i, 0))
    return pl.pallas_call(
        body, name=name, grid=(T // tm, 1),
        in_specs=[yprev, ymain, ynxt, whole((F, D)), uprev, umain, unxt, whole((3, F)), whole((1, F))],
        out_specs=[tile, tile, whole((1, F)), whole((1, F)), whole((1, F)), whole((3, F))],
        out_shape=[jax.ShapeDtypeStruct((T, F), BF16), jax.ShapeDtypeStruct((T, F), BF16),
                   jax.ShapeDtypeStruct((1, F), F32), jax.ShapeDtypeStruct((1, F), F32),
                   jax.ShapeDtypeStruct((1, F), F32), jax.ShapeDtypeStruct((3, F), F32)],
        compiler_params=_params(("arbitrary", "arbitrary")),
    )(dy, dy, dy, w_down, up, up, up, fcw, fcb)


def gate_fwd(proj_g, y_a, y_b, name):
    T, D = y_a.shape
    tm = _pick(T, 512, 8)

    def body(g_ref, a_ref, b_ref, o_ref):
        sa = jax.nn.sigmoid(g_ref[:, :D].astype(F32))
        sb = jax.nn.sigmoid(g_ref[:, D:].astype(F32))
        o_ref[...] = (sa * a_ref[...].astype(F32) + sb * b_ref[...].astype(F32)).astype(BF16)

    row = pl.BlockSpec((tm, D), lambda i: (i, 0))
    return pl.pallas_call(
        body, name=name, grid=(T // tm,),
        in_specs=[pl.BlockSpec((tm, 2 * D), lambda i: (i, 0)), row, row],
        out_specs=row,
        out_shape=jax.ShapeDtypeStruct((T, D), BF16),
        compiler_params=_params(("parallel",)),
    )(proj_g, y_a, y_b)


def gate_bwd(dmix, w_o, proj_g, y_a, y_b, name):
    T, D = y_a.shape
    tm = _pick(T, 512, 8)

    def body(dz_ref, w_ref, g_ref, a_ref, b_ref, da_ref, db_ref, dg_ref):
        dzv = lax.dot_general(dz_ref[...], w_ref[...], NT_DIMS, preferred_element_type=F32)
        sa = jax.nn.sigmoid(g_ref[:, :D].astype(F32))
        sb = jax.nn.sigmoid(g_ref[:, D:].astype(F32))
        da_ref[...] = (dzv * sa).astype(BF16)
        db_ref[...] = (dzv * sb).astype(BF16)
        dg_ref[:, :D] = (dzv * a_ref[...].astype(F32) * (sa * (1.0 - sa))).astype(BF16)
        dg_ref[:, D:] = (dzv * b_ref[...].astype(F32) * (sb * (1.0 - sb))).astype(BF16)

    row = pl.BlockSpec((tm, D), lambda i: (i, 0))
    wide = pl.BlockSpec((tm, 2 * D), lambda i: (i, 0))
    return pl.pallas_call(
        body, name=name, grid=(T // tm,),
        in_specs=[pl.BlockSpec((tm, dmix.shape[1]), lambda i: (i, 0)), pl.BlockSpec(w_o.shape, lambda i: (0, 0)),
                  wide, row, row],
        out_specs=[row, row, wide],
        out_shape=[jax.ShapeDtypeStruct((T, D), BF16), jax.ShapeDtypeStruct((T, D), BF16),
                   jax.ShapeDtypeStruct((T, 2 * D), BF16)],
        compiler_params=_params(("parallel",)),
    )(dmix, w_o, proj_g, y_a, y_b)


ATT_WIN = ATT_TQ + 2 * RADIUS
ATT_STEP = 2048
FAR = 1e32


def _att_window(qs, L):
    ks = pl.multiple_of(jnp.clip(qs - RADIUS, 0, L - ATT_WIN), RADIUS)
    return ks, jnp.where(qs == 0, 0, jnp.where(qs == L - ATT_TQ, 2, 1))


def _fill_bias_tables(bias_ref, sl_ref, hp, d):
    col_row = (lax.broadcasted_iota(jnp.int32, (ATT_TQ, ATT_WIN), 1)
               - lax.broadcasted_iota(jnp.int32, (ATT_TQ, ATT_WIN), 0))
    for v in range(3):
        ad = jnp.abs(col_row - v * RADIUS)
        dist = jnp.where(ad <= RADIUS, (ad * d).astype(F32), FAR)
        bias_ref[v, 0:ATT_TQ, :] = sl_ref[hp * 2] * dist
        bias_ref[v, ATT_TQ:2 * ATT_TQ, :] = sl_ref[hp * 2 + 1] * dist


def _head_masks():
    lane = lax.broadcasted_iota(jnp.int32, (1, LANES), 1)
    return [lane < HEAD_DIM, lane >= HEAD_DIM]


def _stack_heads(x, masks):
    zero = jnp.zeros_like(x)
    return jnp.concatenate([jnp.where(masks[0], x, zero), jnp.where(masks[1], x, zero)], axis=0)


def _unstack_heads(x2, masks):
    n = x2.shape[0] // 2
    return jnp.where(masks[0], x2[:n], x2[n:])


def _att_step(L):
    step = min(ATT_STEP, L)
    assert L % step == 0 and step % ATT_TQ == 0 and L >= ATT_WIN
    return step


def _residues_per_step(d, L):
    rps = max(1, min(d, ATT_STEP // L))
    assert d % rps == 0
    return rps


def att_fwd(qkv, group, name):
    d, L, _ = qkv.shape
    step = _att_step(L)
    rps = _residues_per_step(d, L)
    cg = GROUP_W // LANES
    slopes = jnp.asarray(_alibi_slopes()[group])
    scale = HEAD_DIM ** -0.5

    def body(sl_ref, q_ref, k_ref, v_ref, o_ref, l_ref, bias_ref, s_ref, p_ref):
        hp = pl.program_id(1)
        i = pl.program_id(2)

        @pl.when(i == 0)
        def _():
            _fill_bias_tables(bias_ref, sl_ref, hp, d)

        masks = _head_masks()
        per = step // ATT_TQ
        tiles = [(rr, t) for rr in range(rps) for t in range(per)]
        windows = [_att_window(i * step + t * ATT_TQ, L) for t in range(per)]
        for n, (rr, t) in enumerate(tiles):
            rows = slice(t * ATT_TQ, (t + 1) * ATT_TQ)
            ks, table = windows[t]
            q2 = _stack_heads(q_ref[rr, rows, :] * scale, masks)
            kw = k_ref[rr, pl.ds(ks, ATT_WIN), :]
            s_ref[n] = lax.dot_general(q2, kw, NT_DIMS, preferred_element_type=F32) - bias_ref[table]
        for n, (rr, t) in enumerate(tiles):
            rows = slice(t * ATT_TQ, (t + 1) * ATT_TQ)
            s = s_ref[n]
            m = jnp.max(s, -1, keepdims=True)
            p = jnp.exp(s - m)
            den = jnp.sum(p, -1, keepdims=True)
            p_ref[n] = (p / den).astype(BF16)
            l_ref[rr, rows, :] = _unstack_heads(m + jnp.log(den), masks)
        for n, (rr, t) in enumerate(tiles):
            rows = slice(t * ATT_TQ, (t + 1) * ATT_TQ)
            vw = v_ref[rr, pl.ds(windows[t][0], ATT_WIN), :]
            o2 = jnp.dot(p_ref[n], vw, preferred_element_type=F32)
            o_ref[rr, rows, :] = _unstack_heads(o2, masks).astype(ACT)

    n_tiles = rps * step // ATT_TQ
    out_spec = pl.BlockSpec((rps, step, LANES), lambda r, hp, i: (r, i, hp))
    return pl.pallas_call(
        body, name=name, grid=(d // rps, cg, L // step),
        in_specs=[pl.BlockSpec(memory_space=pltpu.SMEM),
                  pl.BlockSpec((rps, step, LANES), lambda r, hp, i: (r, i, hp)),
                  pl.BlockSpec((rps, L, LANES), lambda r, hp, i: (r, 0, cg + hp)),
                  pl.BlockSpec((rps, L, LANES), lambda r, hp, i: (r, 0, 2 * cg + hp))],
        out_specs=[out_spec, out_spec],
        out_shape=[jax.ShapeDtypeStruct((d, L, GROUP_W), ACT), jax.ShapeDtypeStruct((d, L, GROUP_W), F32)],
        scratch_shapes=[pltpu.VMEM((3, 2 * ATT_TQ, ATT_WIN), F32),
                        pltpu.VMEM((n_tiles, 2 * ATT_TQ, ATT_WIN), F32),
                        pltpu.VMEM((n_tiles, 2 * ATT_TQ, ATT_WIN), BF16)],
        compiler_params=_params(("arbitrary", "arbitrary", "arbitrary")),
    )(slopes, qkv, qkv, qkv)


def att_bwd(qkv, do, lse, dmat, group, name, after=None):
    d, L, _ = qkv.shape
    step = _att_step(L)
    rps = _residues_per_step(d, L)
    nq = L // step
    cg = GROUP_W // LANES
    slopes = jnp.asarray(_alibi_slopes()[group])
    scale = HEAD_DIM ** -0.5

    def body(sl_ref, q_ref, k_ref, v_ref, do_ref, l_ref, dm_ref, *rest):
        dq_ref, dk_ref, dv_ref, dk_acc, dv_acc, bias_ref, s_ref, dp_ref, p_ref, ds_ref = rest[len(rest) - 10:]
        hp = pl.program_id(1)
        i = pl.program_id(2)

        @pl.when(i == 0)
        def _():
            dk_acc[...] = jnp.zeros_like(dk_acc)
            dv_acc[...] = jnp.zeros_like(dv_acc)
            _fill_bias_tables(bias_ref, sl_ref, hp, d)

        masks = _head_masks()

        def head_cols(x):
            return jnp.concatenate([jnp.max(jnp.where(hm, x, -jnp.inf), -1, keepdims=True) for hm in masks], axis=0)

        per = step // ATT_TQ
        tiles = [(rr, t) for rr in range(rps) for t in range(per)]
        windows = [_att_window(i * step + t * ATT_TQ, L) for t in range(per)]

        def stacked(ref, rr, t, factor=None):
            x = ref[rr, t * ATT_TQ:(t + 1) * ATT_TQ, :]
            return _stack_heads(x if factor is None else x * factor, masks)

        for n, (rr, t) in enumerate(tiles):
            ks, table = windows[t]
            q2 = stacked(q_ref, rr, t, scale)
            s_ref[n] = lax.dot_general(q2, k_ref[rr, pl.ds(ks, ATT_WIN), :], NT_DIMS,
                                       preferred_element_type=F32) - bias_ref[table]
            dp_ref[n] = lax.dot_general(stacked(do_ref, rr, t), v_ref[rr, pl.ds(ks, ATT_WIN), :], NT_DIMS,
                                        preferred_element_type=F32)
        for n, (rr, t) in enumerate(tiles):
            rows = slice(t * ATT_TQ, (t + 1) * ATT_TQ)
            p = jnp.exp(s_ref[n] - head_cols(l_ref[rr, rows, :]))
            p_ref[n] = p.astype(BF16)
            ds_ref[n] = (p * (dp_ref[n] - head_cols(dm_ref[rr, rows, :]))).astype(BF16)
        for n, (rr, t) in enumerate(tiles):
            rows = slice(t * ATT_TQ, (t + 1) * ATT_TQ)
            ks = windows[t][0]
            ds = ds_ref[n]
            dq2 = jnp.dot(ds, k_ref[rr, pl.ds(ks, ATT_WIN), :], preferred_element_type=F32)
            dq_ref[rr, rows, :] = (_unstack_heads(dq2, masks) * scale).astype(BF16)
            dk_acc[rr, pl.ds(ks, ATT_WIN), :] += lax.dot_general(ds, stacked(q_ref, rr, t, scale), TN_DIMS,
                                                                 preferred_element_type=F32)
            dv_acc[rr, pl.ds(ks, ATT_WIN), :] += lax.dot_general(p_ref[n], stacked(do_ref, rr, t), TN_DIMS,
                                                                 preferred_element_type=F32)

        @pl.when(i == nq - 1)
        def _():
            dk_ref[...] = dk_acc[...].astype(BF16)
            dv_ref[...] = dv_acc[...].astype(BF16)

    tile = pl.BlockSpec((rps, step, LANES), lambda r, hp, i: (r, i, hp))
    whole = pl.BlockSpec((rps, L, LANES), lambda r, hp, i: (r, 0, hp))
    return pl.pallas_call(
        body, name=name, grid=(d // rps, cg, nq),
        in_specs=[pl.BlockSpec(memory_space=pltpu.SMEM), tile,
                  pl.BlockSpec((rps, L, LANES), lambda r, hp, i: (r, 0, cg + hp)),
                  pl.BlockSpec((rps, L, LANES), lambda r, hp, i: (r, 0, 2 * cg + hp)),
                  tile, tile, tile] + ([] if after is None else [pl.BlockSpec((8, LANES), lambda r, hp, i: (0, 0))]),
        out_specs=[tile, whole, whole],
        out_shape=[jax.ShapeDtypeStruct((d, L, GROUP_W), BF16)] * 3,
        scratch_shapes=[pltpu.VMEM((rps, L, LANES), F32), pltpu.VMEM((rps, L, LANES), F32),
                        pltpu.VMEM((3, 2 * ATT_TQ, ATT_WIN), F32)]
        + [pltpu.VMEM((rps * step // ATT_TQ, 2 * ATT_TQ, ATT_WIN), dt) for dt in (F32, F32, BF16, BF16)],
        compiler_params=_params(("arbitrary", "arbitrary", "arbitrary")),
    )(slopes, qkv, qkv, qkv, do, lse, dmat, *([] if after is None else [after]))


def _group_weights(ls):
    m = jnp.maximum(jnp.maximum(ls[0], ls[1]), ls[2])
    es = [jnp.exp(l - m) for l in ls]
    tot = es[0] + es[1] + es[2]
    return [e / tot for e in es]


def combine_fwd(outs, lses, name):
    T = outs[0].shape[0] * outs[0].shape[1]
    tm = _pick(T, 512, 8)
    n_scr = 2 * (len(DILATIONS) - 1)

    def body(*refs):
        o_refs, l_refs, c_ref, scr = refs[:3], refs[3:6], refs[6], refs[7:]
        o = [_load_natural(o_refs[g], d, scr[g - 1] if g else None) for g, d in enumerate(DILATIONS)]
        l = [_load_natural(l_refs[g], d, scr[g + 1] if g else None) for g, d in enumerate(DILATIONS)]
        w = _group_weights(l)
        c_ref[...] = (w[0] * o[0] + w[1] * o[1] + w[2] * o[2]).astype(BF16)

    specs = [_residue_spec(tm, d, GROUP_W) for d in DILATIONS]
    return pl.pallas_call(
        body, name=name, grid=(T // tm,),
        in_specs=specs + specs, out_specs=pl.BlockSpec((tm, GROUP_W), lambda i: (i, 0)),
        out_shape=jax.ShapeDtypeStruct((T, GROUP_W), BF16),
        scratch_shapes=[_residue_scratch(tm, GROUP_W)] * n_scr,
        compiler_params=_params(("parallel",)),
    )(*outs, *lses)


def combine_bwd(dcomb, outs, lses, name):
    T = dcomb.shape[0]
    tm = _pick(T, 256, 8)
    head = np.arange(GROUP_W) // HEAD_DIM
    seg = jnp.asarray((head[:, None] == head[None, :]).astype(np.float32)).astype(BF16)
    ng = len(DILATIONS)
    n_scr = 4 * (ng - 1)

    def body(*refs):
        dc_ref, o_refs, l_refs, e_ref = refs[0], refs[1:1 + ng], refs[1 + ng:1 + 2 * ng], refs[1 + 2 * ng]
        do_refs, dm_refs = refs[2 + 2 * ng:2 + 3 * ng], refs[2 + 3 * ng:2 + 4 * ng]
        scr = refs[2 + 4 * ng:]
        def load(ref, d, buf, c, cols):
            if d == 1:
                return ref[0, :, cols].astype(F32)
            n = ref.shape[1]
            for r in range(d):
                buf[c, pl.ds(r, n, stride=d), :] = ref[r, :, cols].astype(F32)
            return buf[c]

        def store(val, ref, d, buf, c, cols):
            if d == 1:
                ref[0, :, cols] = val.astype(ref.dtype)
                return
            n = ref.shape[1]
            buf[c] = val
            for r in range(d):
                ref[r, :, cols] = buf[c, pl.ds(r, n, stride=d), :].astype(ref.dtype)

        for c in range(GROUP_W // LANES):
            cols = slice(c * LANES, (c + 1) * LANES)
            o = [load(o_refs[g], d, scr[4 * (g - 1)] if g else None, c, cols) for g, d in enumerate(DILATIONS)]
            l = [load(l_refs[g], d, scr[4 * (g - 1) + 1] if g else None, c, cols) for g, d in enumerate(DILATIONS)]
            w = _group_weights(l)
            dc = dc_ref[:, cols].astype(F32)
            e = e_ref[cols, cols]
            prod = dc * (w[0] * o[0] + w[1] * o[1] + w[2] * o[2])
            tot = jnp.zeros_like(dc)
            for _ in range(3):
                part = prod.astype(BF16)
                tot = tot + jnp.dot(part, e, preferred_element_type=F32)
                prod = prod - part.astype(F32)
            for g, d in enumerate(DILATIONS):
                store(w[g] * dc, do_refs[g], d, scr[4 * (g - 1) + 2] if g else None, c, cols)
                store(w[g] * tot, dm_refs[g], d, scr[4 * (g - 1) + 3] if g else None, c, cols)

    specs = [_residue_spec(tm, d, GROUP_W) for d in DILATIONS]
    res = pl.pallas_call(
        body, name=name, grid=(T // tm,),
        in_specs=[pl.BlockSpec((tm, GROUP_W), lambda i: (i, 0))] + specs + specs
        + [pl.BlockSpec((GROUP_W, GROUP_W), lambda i: (0, 0))],
        out_specs=specs + specs,
        out_shape=[jax.ShapeDtypeStruct(o.shape, BF16) for o in outs] + [jax.ShapeDtypeStruct(o.shape, F32) for o in outs],
        scratch_shapes=[_residue_scratch(tm, GROUP_W)] * n_scr,
        compiler_params=_params(("parallel",)),
    )(dcomb, *outs, *lses, seg)
    return res[:ng], res[ng:]


def _position():
    return lax.axis_index("x"), lax.axis_index("y"), lax.axis_index("c")


def _other_chips(x, y):
    return [(1 - x, y), (x, 1 - y), (1 - x, 1 - y)]


def _remote(src, dst, send_sems, recv_sems, k, to):
    return pltpu.make_async_remote_copy(src_ref=src, dst_ref=dst, send_sem=send_sems.at[k], recv_sem=recv_sems.at[k],
                                        device_id=to, device_id_type=MESH)


GATHER_SEMS = 10
SPLIT_ROWS = 32


def _gather_plan(ins, outs, send_sems, recv_sems, local_sems):
    x, y, c = _position()
    sibling = (x, y, 1 - c)
    nbr_x, nbr_y, diag = (1 - x, y, c), (x, 1 - y, c), (1 - x, 1 - y, c)
    local, begin, stages, last = [], [], [], []
    for a in range(len(ins)):
        k0 = GATHER_SEMS * a
        rows = ins[a].shape[0]
        half = rows // 2

        def block(dev):
            return outs[a].at[4 * dev[0] + 2 * dev[1] + dev[2]]

        def part(ref, h):
            return ref.at[pl.ds(h * half, half)]

        def copy(k, src, dst, to):
            return _remote(src, dst, send_sems, recv_sems, k0 + k, to)

        me = (x, y, c)
        local.append(pltpu.make_async_copy(ins[a], block(me), local_sems.at[a]))
        begin.append(copy(0, ins[a], block(me), sibling))
        pass_on = [copy(7 + j, block(dev), block(dev), sibling) for j, dev in enumerate((nbr_x, nbr_y, diag))]
        if rows >= SPLIT_ROWS and rows % SPLIT_ROWS == 0:
            for h in range(2):
                begin.append(copy(1 + h, part(ins[a], h), part(block(me), h), nbr_x))
                begin.append(copy(3 + h, part(ins[a], h), part(block(me), h), nbr_y))
            from_x = [copy(1 + h, part(block(nbr_x), h), part(block(nbr_x), h), sibling) for h in range(2)]
            from_y = [copy(3 + h, part(block(nbr_y), h), part(block(nbr_y), h), sibling) for h in range(2)]
            fwd_0 = copy(5, part(block(nbr_x), 0), part(block(nbr_x), 0), nbr_y)
            fwd_1 = copy(6, part(block(nbr_y), 1), part(block(nbr_y), 1), nbr_x)
            got_0 = copy(5, part(block(diag), 0), part(block(diag), 0), sibling)
            got_1 = copy(6, part(block(diag), 1), part(block(diag), 1), sibling)
            stages.append(([from_x[0]], [fwd_0]))
            stages.append(([from_y[1]], [fwd_1]))
            stages.append(([from_x[1]], [pass_on[0]]))
            stages.append(([from_y[0]], [pass_on[1]]))
            stages.append(([got_0, got_1], [pass_on[2]]))
        else:
            for j, dev in enumerate((nbr_x, nbr_y, diag)):
                begin.append(copy(1 + 2 * j, ins[a], block(me), dev))
                stages.append(([copy(1 + 2 * j, block(dev), block(dev), sibling)], [pass_on[j]]))
        other = (x, y, 1 - c)
        last.append(copy(0, block(other), block(other), sibling))
        for j, dev in enumerate((nbr_x, nbr_y, diag)):
            theirs = (dev[0], dev[1], 1 - c)
            last.append(copy(7 + j, block(theirs), block(theirs), sibling))
    return local, begin, stages, last


def _gather_begin(ins, outs, send_sems, recv_sems, local_sems):
    local, begin, _, _ = _gather_plan(ins, outs, send_sems, recv_sems, local_sems)
    for cp in local + begin:
        cp.start()


def _gather_finish(ins, outs, send_sems, recv_sems, local_sems):
    local, begin, stages, last = _gather_plan(ins, outs, send_sems, recv_sems, local_sems)
    started = []
    for arrivals, onward in stages:
        for cp in arrivals:
            cp.wait_recv()
        for cp in onward:
            cp.start()
            started.append(cp)
    for cp in last:
        cp.wait_recv()
    for cp in begin + started:
        cp.wait_send()
    for cp in local:
        cp.wait()


def _gather_scratch(n):
    return [pltpu.SemaphoreType.DMA((GATHER_SEMS * n,)), pltpu.SemaphoreType.DMA((GATHER_SEMS * n,)),
            pltpu.SemaphoreType.DMA((n,))]


_HBM = pl.BlockSpec(memory_space=pltpu.HBM)
_SEM = pl.BlockSpec(memory_space=pltpu.SEMAPHORE)
_DATAFLOW = pltpu.SideEffectType.DATAFLOW_SIDE_EFFECTING


def _to_all_plan(srcs, lands, send_sems, recv_sems):
    x, y, c = _position()
    me = 4 * x + 2 * y + c
    copies = []
    for a in range(len(srcs)):
        for k in range(1, N_DEV):
            fx, fy, fc = (k >> 2) & 1, (k >> 1) & 1, k & 1
            to = (1 - x if fx else x, 1 - y if fy else y, 1 - c if fc else c)
            copies.append(_remote(srcs[a], lands[a].at[me], send_sems, recv_sems, (N_DEV - 1) * a + k - 1, to))
    return copies


def _to_sibling_plan(srcs, lands, send_sems, recv_sems):
    x, y, c = _position()
    copies = []
    for a in range(len(srcs)):
        for q in range(4):
            copies.append(_remote(srcs[a].at[2 * q + (1 - c)], lands[a].at[q], send_sems, recv_sems, 4 * a + q,
                                  (x, y, 1 - c)))
    return copies


def _to_chips_plan(srcs, lands, send_sems, recv_sems):
    x, y, c = _position()
    copies = []
    for a in range(len(srcs)):
        for j, (cx, cy) in enumerate(_other_chips(x, y)):
            copies.append(_remote(srcs[a].at[2 * cx + cy], lands[a].at[j], send_sems, recv_sems, 3 * a + j, (cx, cy, c)))
    return copies


def copies_start(srcs, land_shapes, plan, per_array, name):
    n = len(srcs)
    n_sem = per_array * n
    lands = [lax.empty(s.shape, s.dtype) for s in land_shapes]

    def body(*refs):
        src_refs, land_refs = refs[:n], refs[n:2 * n]
        send_sems, recv_sems = refs[2 * n], refs[2 * n + 1]
        token = refs[-1]
        for cp in plan(src_refs, land_refs, send_sems, recv_sems):
            cp.start()
        token[...] = jnp.zeros_like(token)

    out = pl.pallas_call(
        body, name=name,
        out_shape=(pltpu.SemaphoreType.DMA((n_sem,)), pltpu.SemaphoreType.DMA((n_sem,)))
        + tuple(pltpu.HBM(s.shape, s.dtype) for s in srcs)
        + tuple(pltpu.HBM(s.shape, s.dtype) for s in land_shapes)
        + (jax.ShapeDtypeStruct((8, LANES), F32),),
        in_specs=[_HBM] * (2 * n),
        out_specs=(_SEM, _SEM) + (_HBM,) * (2 * n) + (pl.BlockSpec(memory_space=pltpu.VMEM),),
        input_output_aliases={i: 2 + i for i in range(2 * n)},
        compiler_params=pltpu.CompilerParams(has_side_effects=_DATAFLOW),
    )(*[pltpu.with_memory_space_constraint(s, pltpu.HBM) for s in srcs],
      *[pltpu.with_memory_space_constraint(l, pltpu.HBM) for l in lands])
    return out[:-1], out[-1]


def copies_wait(handles, plan, after, name):
    send_sems, recv_sems = handles[0], handles[1]
    n = (len(handles) - 2) // 2
    thru = handles[2:]

    def body(*refs):
        src_refs, land_refs = refs[:n], refs[n:2 * n]
        send_sems, recv_sems = refs[2 * n], refs[2 * n + 1]
        copies = plan(src_refs, land_refs, send_sems, recv_sems)
        for cp in copies:
            cp.wait_recv()
        for cp in copies:
            cp.wait_send()

    out = pl.pallas_call(
        body, name=name,
        out_shape=tuple(pltpu.HBM(t.shape, t.dtype) for t in thru),
        in_specs=[_HBM] * (2 * n) + [_SEM, _SEM, pl.BlockSpec(memory_space=pl.ANY)],
        out_specs=(_HBM,) * (2 * n),
        input_output_aliases={i: i for i in range(2 * n)},
        compiler_params=pltpu.CompilerParams(has_side_effects=_DATAFLOW),
    )(*thru, send_sems, recv_sems, after)
    return out[:n], out[n:]


def all_sum_small(vec, name):
    R = vec.shape[0]

    def body(v_ref, tot_ref, all_ref, send_sems, recv_sems):
        x, y, c = _position()
        me = 4 * x + 2 * y + c
        all_ref[me] = v_ref[...]
        copies = []
        for k in range(1, N_DEV):
            fx, fy, fc = (k >> 2) & 1, (k >> 1) & 1, k & 1
            to = (1 - x if fx else x, 1 - y if fy else y, 1 - c if fc else c)
            cp = _remote(v_ref, all_ref.at[me], send_sems, recv_sems, k - 1, to)
            cp.start()
            copies.append(cp)
        for cp in copies:
            cp.wait_recv()
        for cp in copies:
            cp.wait_send()
        tot = all_ref[0]
        for j in range(1, N_DEV):
            tot = tot + all_ref[j]
        tot_ref[...] = tot

    vmem = pl.BlockSpec(memory_space=pltpu.VMEM)
    return pl.pallas_call(
        body, name=name,
        in_specs=[vmem], out_specs=vmem,
        out_shape=jax.ShapeDtypeStruct((R, LANES), F32),
        scratch_shapes=[pltpu.VMEM((N_DEV, R, LANES), F32),
                        pltpu.SemaphoreType.DMA((N_DEV - 1,)), pltpu.SemaphoreType.DMA((N_DEV - 1,))],
        compiler_params=pltpu.CompilerParams(vmem_limit_bytes=VMEM_LIMIT),
    )(vec)


def pair_add(parts, theirs, place, name):
    _, R, C = theirs.shape
    tr = _pick(R, 1024, 8)

    def body(place_ref, a_ref, b_ref, o_ref):
        o_ref[...] = (a_ref[...].astype(F32) + b_ref[...].astype(F32)).astype(BF16)

    blk = pl.BlockSpec((None, tr, C), lambda q, i, place_ref: (q, i, 0))
    return pl.pallas_call(
        body, name=name,
        grid_spec=pltpu.PrefetchScalarGridSpec(
            num_scalar_prefetch=1, grid=(4, R // tr),
            in_specs=[pl.BlockSpec((None, tr, C), lambda q, i, place_ref: (2 * q + place_ref[2], i, 0)), blk],
            out_specs=blk),
        out_shape=jax.ShapeDtypeStruct(theirs.shape, BF16),
        compiler_params=_params(("parallel", "parallel")),
    )(place, parts, theirs)


def _adamw_math(w, g, m, v):
    m = ADAM_B1 * m + (1.0 - ADAM_B1) * g
    v = ADAM_B2 * v + (1.0 - ADAM_B2) * jnp.square(g)
    m_hat = m / (1.0 - ADAM_B1 ** ADAM_STEP)
    v_hat = v / (1.0 - ADAM_B2 ** ADAM_STEP)
    delta = -ADAM_LR * (m_hat / (jnp.sqrt(v_hat) + ADAM_EPS) + ADAM_WD * w)
    return delta, m, v


def adamw_sharded(w, m, v, parts, sib, others, place, name):
    R, C = w.shape
    tr = _pick(R, 256, 8)

    def body(place_ref, w_ref, m_ref, v_ref, a_ref, b_ref, o_ref, g_ref, d_ref, nm_ref, nv_ref):
        g = a_ref[...].astype(F32) + b_ref[...].astype(F32)
        for j in range(3):
            g = g + o_ref[j].astype(F32)
        delta, nm, nv = _adamw_math(w_ref[...], g, m_ref[...], v_ref[...])
        g_ref[...] = g
        d_ref[...] = delta
        nm_ref[...] = nm
        nv_ref[...] = nv

    row = pl.BlockSpec((tr, C), lambda i, place_ref: (i, 0))
    return pl.pallas_call(
        body, name=name,
        grid_spec=pltpu.PrefetchScalarGridSpec(
            num_scalar_prefetch=1, grid=(R // tr,),
            in_specs=[row] * 3 + [pl.BlockSpec((None, tr, C), lambda i, place_ref: (place_ref[0], i, 0)),
                                  pl.BlockSpec((None, tr, C), lambda i, place_ref: (place_ref[1], i, 0)),
                                  pl.BlockSpec((3, tr, C), lambda i, place_ref: (0, i, 0))],
            out_specs=[row] * 4),
        out_shape=[jax.ShapeDtypeStruct((R, C), F32)] * 4,
        compiler_params=_params(("parallel",)),
    )(place, w, m, v, parts, sib, others)


def adamw_packed(w, g, m, v, name):
    R = w.shape[0]

    def body(w_ref, g_ref, m_ref, v_ref, d_ref, nm_ref, nv_ref):
        delta, nm, nv = _adamw_math(w_ref[...], g_ref[...], m_ref[...], v_ref[...])
        d_ref[...] = delta
        nm_ref[...] = nm
        nv_ref[...] = nv

    full = pl.BlockSpec((R, LANES), lambda i: (0, 0))
    return pl.pallas_call(
        body, name=name, grid=(1,),
        in_specs=[full] * 4, out_specs=[full] * 3,
        out_shape=[jax.ShapeDtypeStruct((R, LANES), F32)] * 3,
        compiler_params=_params(("arbitrary",)),
    )(w, g, m, v)


def _pack(arrays):
    flat = []
    sizes = []
    for a in arrays:
        f = a.reshape(-1).astype(F32)
        pad = (-f.shape[0]) % LANES
        if pad:
            f = jnp.concatenate([f, jnp.zeros((pad,), F32)])
        flat.append(f)
        sizes.append(f.shape[0])
    rows = sum(sizes) // LANES
    pad_rows = (-rows) % 8
    if pad_rows:
        flat.append(jnp.zeros((pad_rows * LANES,), F32))
    return jnp.concatenate(flat).reshape(-1, LANES), sizes


def _unpack(packed, sizes, shapes):
    flat = packed.reshape(-1)
    out = []
    off = 0
    for size, shape in zip(sizes, shapes):
        n = int(np.prod(shape))
        out.append(flat[off:off + n].reshape(shape))
        off += size
    return out


def _to_blocks(full, axis):
    if axis == 0:
        return full.reshape(N_DEV, full.shape[0] // N_DEV, full.shape[1])
    r, n = full.shape
    return full.reshape(r, N_DEV, n // N_DEV).transpose(1, 0, 2)


def _from_blocks(blocks, axis):
    if axis == 0:
        return blocks.reshape(blocks.shape[0] * blocks.shape[1], blocks.shape[2])
    return blocks.transpose(1, 0, 2).reshape(blocks.shape[1], blocks.shape[0] * blocks.shape[2])


def kernel(x, ln0_g, ln0_b, w_in, b_in, conv_w, w_a, w_b, w_o, b_o, ln1_g, ln1_b, w_up, b_up, ffn_conv_w, ffn_conv_b, w_down, b_down, ln2_g, ln2_b, loss_target, m_ln0_g, m_ln0_b, m_w_in, m_b_in, m_conv_w, m_w_a, m_w_b, m_w_o, m_b_o, m_ln1_g, m_ln1_b, m_w_up, m_b_up, m_ffn_conv_w, m_ffn_conv_b, m_w_down, m_b_down, m_ln2_g, m_ln2_b, v_ln0_g, v_ln0_b, v_w_in, v_b_in, v_conv_w, v_w_a, v_w_b, v_w_o, v_b_o, v_ln1_g, v_ln1_b, v_w_up, v_b_up, v_ffn_conv_w, v_ffn_conv_b, v_w_down, v_b_down, v_ln2_g, v_ln2_b):
    T, D = x.shape[1], x.shape[2]
    F = ffn_conv_b.shape[-1]
    xs = x.reshape(T, D)
    tgt = loss_target.reshape(T, D)
    dev = 4 * lax.axis_index("x") + 2 * lax.axis_index("y") + lax.axis_index("c")
    chip = 2 * lax.axis_index("x") + lax.axis_index("y")
    core = lax.axis_index("c")
    place = jnp.stack([dev, chip, core]).astype(jnp.int32)

    big = dict(w_in=(w_in[0], 1), w_a=(w_a[0], 0), w_b=(w_b[0], 1), w_o=(w_o[0], 0), w_up=(w_up[0], 1),
               w_down=(w_down[0], 0))
    names = list(big)
    shards = {k: big[k][0].astype(BF16) for k in names}
    ln0g, ln0b = ln0_g.reshape(1, D), ln0_b.reshape(1, D)
    h0, h0b, *rest = ln_fwd(xs, None, ln0g, ln0b, "ln0_fwd_gather_w_in", dilations=DILATIONS[1:],
                            gather=[shards["w_in"], conv_w[0], ffn_conv_w[0]])
    h0_res = [h0b] + [h.reshape(T, D) for h in rest[:2]]
    g_in, g_conv, g_fcw = rest[2:]
    full = {"w_in": _from_blocks(g_in, 1)}
    conv_full = _from_blocks(g_conv, 1)
    fcw_full = _from_blocks(g_fcw, 1)
    late_groups = (("w_a", "w_b", "w_o"), ("w_up", "w_down"))
    late_handles = []
    token = conv_full[:1, :1] * 0.0
    for n, keys in enumerate(late_groups):
        srcs = [shards[k] + token[0, 0].astype(BF16) for k in keys]
        handles, token = copies_start(srcs, [jax.ShapeDtypeStruct((N_DEV,) + s.shape, BF16) for s in srcs],
                                      _to_all_plan, N_DEV - 1, f"gather_late_{n}_start")
        late_handles.append(handles)

    def late_weights(n, after):
        _, lands = copies_wait(late_handles[n], _to_all_plan, after, f"gather_late_{n}_wait")
        for k, land in zip(late_groups[n], lands):
            full[k] = _from_blocks(lax.dynamic_update_index_in_dim(land, shards[k], dev, 0), big[k][1])

    o_q = 3 * D
    o_g = o_q + 3 * QKV_W
    w_pa, w_qkv, w_pg = full["w_in"][:, :o_q], full["w_in"][:, o_q:o_g], full["w_in"][:, o_g:]
    b_pa, b_qkv, b_pg = b_in[:, :o_q], b_in[:, o_q:o_g], b_in[:, o_g:]

    proj_a = mm_nn(h0b, w_pa, b_pa, ACT, "proj_conv", after=token)
    proj_g = mm_nn(h0b, w_pg, b_pg, ACT, "proj_gates")
    zero_d = jnp.zeros((1, D), F32)
    s_a = conv_a_fwd(proj_a, conv_full, "conv_a_fwd")
    late_weights(0, s_a)
    y_a = mm_nn(s_a, full["w_a"], zero_d, ACT, "branch_a_out")

    def group_cols(m, g):
        return jnp.concatenate([m[:, s * QKV_W + g * GROUP_W:s * QKV_W + (g + 1) * GROUP_W] for s in range(3)], 1)

    w_grp = [group_cols(w_qkv, g) for g in range(3)]
    qkvs, outs, lses = [], [], []
    for g, d in enumerate(DILATIONS):
        qkv = mm_nn(h0_res[g], w_grp[g], group_cols(b_qkv, g), BF16, f"proj_qkv_{g}").reshape(d, T // d, 3 * GROUP_W)
        o, l = att_fwd(qkv, g, f"att_fwd_{g}")
        qkvs.append(qkv)
        outs.append(o)
        lses.append(l)
    comb = combine_fwd(outs, lses, "combine_fwd")
    y_b = mm_nn(comb, full["w_b"], zero_d, ACT, "branch_b_out")
    z = gate_fwd(proj_g, y_a, y_b, "gate_fwd")
    h1, h1b, mix = ln_fwd(h0, ("nn", z, full["w_o"], b_o), ln1_g, ln1_b, "mix_out_ln1_fwd")
    late_weights(1, h1b)
    up, f_act = ffn_up_conv_f(h1b, full["w_up"], b_up, fcw_full, ffn_conv_b, "ffn_up_conv_f")

    dr2, dr2b, d_ln2_g, d_ln2_b, d_b_down, loss_part = ln_bwd(
        h1, ("nn", f_act, full["w_down"], b_down), ln2_g, ln2_b, None, None, tgt, "ffn_down_ln2_loss_bwd")
    dw_down, _ = mm_tn(f_act, dr2b, "dw_down")
    d_a, d_gate, cs_a, cs_gate, d_fcb, d_fcw = conv_f_bwd(dr2b, full["w_down"], up, fcw_full, ffn_conv_b,
                                                          "d_ffn_act_conv_f_bwd")
    dw_up_a, _ = mm_tn(h1b, d_a, "dw_up_a")
    dw_up_g, _ = mm_tn(h1b, d_gate, "dw_up_gate")
    dr1, dr1b, d_ln1_g, d_ln1_b, d_b_o, _ = ln_bwd(h0, mix, ln1_g, ln1_b, dr2, ("nt", [d_a, d_gate], full["w_up"]), None,
                                                   "d_h1_ln1_bwd")
    dw_o, _ = mm_tn(z, dr1b, "dw_o")
    dy_a, dy_b, dproj_g = gate_bwd(dr1b, full["w_o"], proj_g, y_a, y_b, "d_z_gate_bwd")
    dw_a, _ = mm_tn(s_a, dy_a, "dw_a")
    dproj_a, d_conv = conv_a_bwd(dy_a, full["w_a"], proj_a, conv_full, "d_s_a_conv_a_bwd")
    dw_b, _ = mm_tn(comb, dy_b, "dw_b")

    rs_mine, rs_sib, rs_handles = {}, {}, {}

    sib_handles = {}

    def to_sibling_start(keys, grads, tag):
        parts = [_to_blocks(grads[k], big[k][1]) for k in keys]
        handles, tok = copies_start(parts, [jax.ShapeDtypeStruct((4,) + p.shape[1:], BF16) for p in parts],
                                    _to_sibling_plan, 4, f"grads_to_sibling_{tag}_start")
        sib_handles[tag] = (keys, handles)
        return tok

    def to_chips_start(tag, after):
        keys, handles = sib_handles[tag]
        parts, from_sib = copies_wait(handles, _to_sibling_plan, after, f"grads_to_sibling_{tag}_wait")
        sums = [pair_add(a, b, place, f"chip_sum_{k}") for k, a, b in zip(keys, parts, from_sib)]
        handles, tok = copies_start(sums, [jax.ShapeDtypeStruct((3,) + s.shape[1:], BF16) for s in sums],
                                    _to_chips_plan, 3, f"grads_to_chips_{tag}_start")
        for k, a, b in zip(keys, parts, from_sib):
            rs_mine[k], rs_sib[k] = a, b
        rs_handles[tag] = (keys, handles)
        return tok

    tok_a = to_sibling_start(("w_a", "w_b", "w_o", "w_up", "w_down"),
                             dict(w_a=dw_a, w_b=dw_b, w_o=dw_o, w_up=jnp.concatenate([dw_up_a, dw_up_g], 1),
                                  w_down=dw_down), "a")
    dcomb = mm_nt(dy_b, full["w_b"], None, "d_comb", after=tok_a, out_dtype=ACT)
    dos, dms = combine_bwd(dcomb, outs, lses, "combine_bwd")
    tok_a = to_chips_start("a", dms[0])
    dw_grp, cs_grp, dqkvs = [], [], []
    for g, d in enumerate(DILATIONS):
        dq, dk, dv = att_bwd(qkvs[g], dos[g], lses[g], dms[g], g, f"att_bwd_{g}", after=tok_a if g == 0 else None)
        dqkv = [t.reshape(T, GROUP_W) for t in (dq, dk, dv)]
        dwg, csg = mm_tn(h0_res[g], dqkv, f"dw_in_qkv_{g}")
        dqkvs.append(dqkv)
        dw_grp.append(dwg)
        cs_grp.append(csg)
    dw_pa, cs_pa = mm_tn(h0b, dproj_a, "dw_in_conv")
    dw_pg, cs_pg = mm_tn(h0b, dproj_g, "dw_in_gates")

    def ungroup(parts):
        return jnp.concatenate([p[:, s * GROUP_W:(s + 1) * GROUP_W] for s in range(3) for p in parts], 1)

    db_in_parts = [cs_pa, ungroup(cs_grp), cs_pg]
    tok_b = to_sibling_start(("w_in",), dict(w_in=jnp.concatenate([dw_pa, ungroup(dw_grp), dw_pg], 1)), "b")
    dh0 = mm_nt(dproj_a, w_pa, None, "d_h0_conv", after=tok_b)
    tok_b = to_chips_start("b", dh0)
    dh0 = mm_nt(dproj_g, w_pg, dh0, "d_h0_gates", after=tok_b)
    dh0_res = [(mm_nt(dqkvs[g], w_grp[g], None, f"d_h0_qkv_{g}").reshape(d, T // d, D), d)
               for g, d in enumerate(DILATIONS) if g > 0]
    dx, _, d_ln0_g, d_ln0_b, _, _ = ln_bwd(xs, None, ln0g, ln0b, dr1, ("nt", dqkvs[0], w_grp[0]), None, "d_h0_ln0_bwd",
                                           by_residue=[(dh0.reshape(1, T, D), 1)] + dh0_res)

    small = [d_ln0_g, d_ln0_b, jnp.concatenate(db_in_parts, 1), d_conv, d_b_o, d_ln1_g, d_ln1_b,
             jnp.concatenate([cs_a, cs_gate], 1), d_fcw, d_fcb, d_b_down, d_ln2_g, d_ln2_b, loss_part]
    packed, sizes = _pack(small)
    total = all_sum_small(packed, "sum_small")
    (g_ln0_g, g_ln0_b, g_b_in, g_conv_full, g_b_o, g_ln1_g, g_ln1_b, g_b_up, g_fcw_full, g_fcb, g_b_down, g_ln2_g,
     g_ln2_b, loss) = _unpack(total, sizes, [a.shape for a in small])
    cw = conv_w.shape[-1]
    fw = ffn_conv_w.shape[-1]
    g_conv = lax.dynamic_slice_in_dim(g_conv_full, dev * cw, cw, 1)
    g_fcw = lax.dynamic_slice_in_dim(g_fcw_full, dev * fw, fw, 1)

    from_chips = {}
    for tag, (keys, handles) in rs_handles.items():
        _, lands = copies_wait(handles, _to_chips_plan, total, f"grads_to_chips_{tag}_wait")
        from_chips.update(zip(keys, lands))

    moments = dict(w_in=(m_w_in, v_w_in), w_a=(m_w_a, v_w_a), w_b=(m_w_b, v_w_b), w_o=(m_w_o, v_w_o),
                   w_up=(m_w_up, v_w_up), w_down=(m_w_down, v_w_down))
    res_big = {}
    for k in names:
        res_big[k] = adamw_sharded(big[k][0], moments[k][0][0], moments[k][1][0], rs_mine[k], rs_sib[k], from_chips[k],
                                   place, f"adamw_{k}")

    small_names = ["ln0_g", "ln0_b", "b_in", "conv_w", "b_o", "ln1_g", "ln1_b", "b_up", "ffn_conv_w", "ffn_conv_b",
                   "b_down", "ln2_g", "ln2_b"]
    small_w = [ln0_g, ln0_b, b_in, conv_w, b_o, ln1_g, ln1_b, b_up, ffn_conv_w, ffn_conv_b, b_down, ln2_g, ln2_b]
    small_m = [m_ln0_g, m_ln0_b, m_b_in, m_conv_w, m_b_o, m_ln1_g, m_ln1_b, m_b_up, m_ffn_conv_w, m_ffn_conv_b,
               m_b_down, m_ln2_g, m_ln2_b]
    small_v = [v_ln0_g, v_ln0_b, v_b_in, v_conv_w, v_b_o, v_ln1_g, v_ln1_b, v_b_up, v_ffn_conv_w, v_ffn_conv_b,
               v_b_down, v_ln2_g, v_ln2_b]
    small_g = [g_ln0_g, g_ln0_b, g_b_in, g_conv, g_b_o, g_ln1_g, g_ln1_b, g_b_up, g_fcw, g_fcb, g_b_down, g_ln2_g,
               g_ln2_b]
    shapes = [w.shape for w in small_w]
    small_g = [g.reshape(s) for g, s in zip(small_g, shapes)]
    pw, psz = _pack(small_w)
    pg, _ = _pack(small_g)
    pm, _ = _pack(small_m)
    pv, _ = _pack(small_v)
    pd, pnm, pnv = adamw_packed(pw, pg, pm, pv, "adamw_small")
    res_small = {k: (g, d_, m_, v_) for k, g, d_, m_, v_ in zip(
        small_names, small_g, _unpack(pd, psz, shapes), _unpack(pnm, psz, shapes), _unpack(pnv, psz, shapes))}

    order = ["ln0_g", "ln0_b", "w_in", "b_in", "conv_w", "w_a", "w_b", "w_o", "b_o", "ln1_g", "ln1_b", "w_up", "b_up",
             "ffn_conv_w", "ffn_conv_b", "w_down", "b_down", "ln2_g", "ln2_b"]

    def result(k, j):
        if k in res_big:
            return res_big[k][j][None]
        return res_small[k][j]

    out = [loss.reshape(()), dx.reshape(x.shape)]
    for j in range(4):
        out += [result(k, j) for k in order]
    return tuple(out)
```

```python
import math

import numpy as np
import jax
import jax.numpy as jnp
from jax import lax
from jax.experimental import pallas as pl
from jax.experimental.pallas import tpu as pltpu

F32 = jnp.float32
BF16 = jnp.bfloat16
ACT = BF16

N_DEV = 8
LN_EPS = 1e-5
ALPHA = (2.0 * 1) ** 0.25
HEAD_DIM = 64
GROUP_W = 512
QKV_W = 3 * GROUP_W
DILATIONS = (1, 4, 16)
RADIUS = 64
LANES = 128
HALO = 8
HALO_BF16 = 16
ATT_TQ = 128

ADAM_LR = 0.001
ADAM_B1 = 0.9
ADAM_B2 = 0.999
ADAM_EPS = 1e-08
ADAM_WD = 0.01
ADAM_STEP = 10

VMEM_LIMIT = 52 * 1024 * 1024
OUT_TILE_BYTES = 8 * 1024 * 1024
MAX_K_TALL_TILE = 4096
MESH = pl.DeviceIdType.MESH
NT_DIMS = (((1,), (1,)), ((), ()))
TN_DIMS = (((0,), (0,)), ((), ()))


def _pick(n, target, align=LANES):
    if n <= target:
        return n
    best = None
    for t in range(align, target + 1, align):
        if n % t == 0:
            best = t
    assert best is not None, (n, target, align)
    return best


def _params(sems=None):
    return pltpu.CompilerParams(dimension_semantics=sems, vmem_limit_bytes=VMEM_LIMIT)


def _alibi_slopes():
    n = 3 * 8
    return np.exp2(-8.0 * np.arange(1, n + 1, dtype=np.float64) / n).astype(np.float32).reshape(3, 8)


def _ln_stats(r):
    mu = jnp.mean(r, -1, keepdims=True)
    xc = r - mu
    var = jnp.mean(xc * xc, -1, keepdims=True)
    rstd = lax.rsqrt(var + LN_EPS)
    return xc, rstd


def _load_natural(ref, d, scr):
    if d == 1:
        return ref[0].astype(F32)
    n, C = ref.shape[1], ref.shape[2]
    for c in range(C // LANES):
        for r in range(d):
            scr[c, pl.ds(r, n, stride=d), :] = ref[r, :, c * LANES:(c + 1) * LANES].astype(F32)
    return jnp.concatenate([scr[c] for c in range(C // LANES)], axis=1)


def _store_by_residue(val, ref, d, scr):
    if d == 1:
        ref[0] = val.astype(ref.dtype)
        return
    n, C = ref.shape[1], ref.shape[2]
    for c in range(C // LANES):
        scr[c] = val[:, c * LANES:(c + 1) * LANES]
    for c in range(C // LANES):
        for r in range(d):
            ref[r, :, c * LANES:(c + 1) * LANES] = scr[c, pl.ds(r, n, stride=d), :].astype(ref.dtype)


def _residue_spec(tm, d, C):
    return pl.BlockSpec((d, tm // d, C), lambda i: (0, i, 0))


def _residue_scratch(tm, C):
    return pltpu.VMEM((C // LANES, tm, LANES), F32)


def ln_fwd(a, res, g, b, name, dilations=(), gather=()):
    T, D = a.shape
    res_mm = isinstance(res, tuple)
    tm = _pick(T, 512, 8)
    res_ins = list(res[1:]) if res_mm else ([] if res is None else [res])
    nd = len(dilations)
    ng = len(gather)
    n_in = 1 + len(res_ins) + 2
    last = T // tm - 1

    def body(*refs):
        a_ref = refs[0]
        r = a_ref[...]
        if res_mm:
            res_val = jnp.dot(refs[1][...], refs[2][...], preferred_element_type=F32) + refs[3][...]
            refs[-1 - n_scratch][...] = res_val
            r = ALPHA * r + res_val
        elif res_ins:
            r = ALPHA * r + refs[1][...]
        g_ref, b_ref = refs[n_in - 2], refs[n_in - 1]
        shard_refs = refs[n_in:n_in + ng]
        h_ref, hb_ref = refs[n_in + ng], refs[n_in + ng + 1]
        p_refs = refs[n_in + ng + 2:n_in + ng + 2 + nd]
        full_refs = refs[n_in + ng + 2 + nd:n_in + 2 * ng + 2 + nd]
        scratch = refs[len(refs) - n_scratch:]
        sems = scratch[len(scratch) - 3:] if ng else ()

        if ng:
            @pl.when(pl.program_id(0) == 0)
            def _():
                _gather_begin(shard_refs, full_refs, *sems)

        xc, rstd = _ln_stats(r)
        h = xc * rstd * g_ref[...] + b_ref[...]
        h_ref[...] = h
        hb_ref[...] = h.astype(BF16)
        for d, p_ref in zip(dilations, p_refs):
            _store_by_residue(h, p_ref, d, scratch[0])

        if ng:
            @pl.when(pl.program_id(0) == last)
            def _():
                _gather_finish(shard_refs, full_refs, *sems)

    row = pl.BlockSpec((tm, D), lambda i: (i, 0))
    vec = pl.BlockSpec((1, D), lambda i: (0, 0))
    hbm = pl.BlockSpec(memory_space=pl.ANY)
    if res_mm:
        res_specs = [pl.BlockSpec((tm, res[1].shape[1]), lambda i: (i, 0)), pl.BlockSpec(res[2].shape, lambda i: (0, 0)), vec]
    else:
        res_specs = [row] * len(res_ins)
    scratch_shapes = ([_residue_scratch(tm, D)] if nd else []) + (_gather_scratch(ng) if ng else [])
    n_scratch = len(scratch_shapes)
    ins = [a] + res_ins + [g, b] + list(gather)
    return pl.pallas_call(
        body, name=name, grid=(T // tm,),
        in_specs=[row] + res_specs + [vec, vec] + [hbm] * ng,
        out_specs=[row, row] + [_residue_spec(tm, d, D) for d in dilations] + [hbm] * ng + ([row] if res_mm else []),
        out_shape=[jax.ShapeDtypeStruct((T, D), F32), jax.ShapeDtypeStruct((T, D), BF16)]
        + [jax.ShapeDtypeStruct((d, T // d, D), BF16) for d in dilations]
        + [jax.ShapeDtypeStruct((N_DEV,) + s.shape, s.dtype) for s in gather]
        + ([jax.ShapeDtypeStruct((T, D), F32)] if res_mm else []),
        scratch_shapes=scratch_shapes,
        compiler_params=_params(("arbitrary",) if ng else ("parallel",)),
    )(*ins)


def ln_bwd(a, res, g, b, d1, d2, tgt, name, by_residue=()):
    T, D = a.shape
    wide_product = isinstance(d2, tuple) and sum(p.shape[1] for p in d2[1]) > MAX_K_TALL_TILE
    fused = isinstance(res, tuple) or isinstance(d2, tuple)
    tm = _pick(T, 512 if fused and not wide_product else 256, 8)
    loss_mode = tgt is not None
    nres = len(by_residue)
    row = pl.BlockSpec((tm, D), lambda i: (i, 0))
    vec = pl.BlockSpec((1, D), lambda i: (0, 0))
    one = pl.BlockSpec((1, 1), lambda i: (0, 0))

    def rows_of(x):
        return pl.BlockSpec((tm, x.shape[1]), lambda i: (i, 0))

    def whole(x):
        return pl.BlockSpec(x.shape, lambda i: (0, 0))

    ins, in_specs, slots = [], [], {}

    def operand(key, arrays, specs):
        slots[key] = (len(ins), len(arrays))
        ins.extend(arrays)
        in_specs.extend(specs)

    operand("a", [a], [row])
    if isinstance(res, tuple):
        _, x, w, bias = res
        operand("res_mm", [x, w, bias], [rows_of(x), whole(w), vec])
    elif res is not None:
        operand("res", [res], [row])
    operand("gb", [g, b], [vec, vec])
    if loss_mode:
        operand("tgt", [tgt], [row])
    else:
        operand("d1", [d1], [row])
        if isinstance(d2, tuple):
            _, pieces, w = d2
            operand("d2_mm", list(pieces) + [w], [rows_of(p) for p in pieces] + [whole(w)])
        else:
            operand("d2", [d2], [row])
    operand("by_residue", [e for e, _ in by_residue], [_residue_spec(tm, d, D) for _, d in by_residue])
    n_in = len(ins)

    def body(*refs):
        def get(key):
            first, count = slots[key]
            return refs[first:first + count]

        dr_ref, drb_ref, dg_ref, db_ref, ds_ref, loss_ref = refs[n_in:n_in + 6]
        i = pl.program_id(0)

        @pl.when(i == 0)
        def _():
            dg_ref[...] = jnp.zeros_like(dg_ref)
            db_ref[...] = jnp.zeros_like(db_ref)
            ds_ref[...] = jnp.zeros_like(ds_ref)
            loss_ref[...] = jnp.zeros_like(loss_ref)

        r = get("a")[0][...]
        if "res_mm" in slots:
            x_ref, w_ref, bias_ref = get("res_mm")
            r = ALPHA * r + (jnp.dot(x_ref[...], w_ref[...], preferred_element_type=F32) + bias_ref[...])
        elif "res" in slots:
            r = ALPHA * r + get("res")[0][...]
        g_ref, b_ref = get("gb")
        xc, rstd = _ln_stats(r)
        xhat = xc * rstd
        gam = g_ref[...]
        if loss_mode:
            err = xhat * gam + b_ref[...] - get("tgt")[0][...]
            dy = err * (1.0 / D)
            row_loss = jnp.mean(err * err, -1, keepdims=True)
            loss_ref[...] += 0.5 * jnp.sum(row_loss, 0, keepdims=True)
        else:
            if "d2_mm" in slots:
                *p_refs, w_ref = get("d2_mm")
                av = p_refs[0][...] if len(p_refs) == 1 else jnp.concatenate([p[...] for p in p_refs], axis=1)
                d2v = lax.dot_general(av, w_ref[...], NT_DIMS, preferred_element_type=F32)
            else:
                d2v = get("d2")[0][...]
            dy = ALPHA * get("d1")[0][...] + d2v
        for (_, d), e_ref in zip(by_residue, get("by_residue")):
            dy = dy + _load_natural(e_ref, d, refs[-1])
        dyg = dy * gam
        c1 = jnp.mean(dyg, -1, keepdims=True)
        c2 = jnp.mean(dyg * xhat, -1, keepdims=True)
        dr = rstd * (dyg - c1 - xhat * c2)
        dr_ref[...] = dr
        drb_ref[...] = dr.astype(BF16)
        dg_ref[...] += jnp.sum(dy * xhat, 0, keepdims=True)
        db_ref[...] += jnp.sum(dy, 0, keepdims=True)
        ds_ref[...] += jnp.sum(dr, 0, keepdims=True)

    return pl.pallas_call(
        body, name=name, grid=(T // tm,),
        in_specs=in_specs,
        out_specs=[row, row, vec, vec, vec, one],
        out_shape=[jax.ShapeDtypeStruct((T, D), F32), jax.ShapeDtypeStruct((T, D), BF16),
                   jax.ShapeDtypeStruct((1, D), F32), jax.ShapeDtypeStruct((1, D), F32),
                   jax.ShapeDtypeStruct((1, D), F32), jax.ShapeDtypeStruct((1, 1), F32)],
        scratch_shapes=[_residue_scratch(tm, D)] if nres else [],
        compiler_params=_params(("arbitrary",)),
    )(*ins)


_TOKEN_SPEC = pl.BlockSpec((8, LANES), lambda i: (0, 0))


def mm_nn(a, w, bias, out_dtype, name, after=None):
    M, K = a.shape
    N = w.shape[1]
    tm = _pick(M, max(256, min(1024, OUT_TILE_BYTES // (N * jnp.dtype(out_dtype).itemsize))), 8)
    tc = _pick(N, 512)

    def body(a_ref, w_ref, b_ref, *rest):
        o_ref = rest[-1]
        av = a_ref[...]
        for j in range(N // tc):
            cols = slice(j * tc, (j + 1) * tc)
            acc = jnp.dot(av, w_ref[:, cols], preferred_element_type=F32)
            o_ref[:, cols] = (acc + b_ref[:, cols]).astype(out_dtype)

    return pl.pallas_call(
        body, name=name, grid=(M // tm,),
        in_specs=[pl.BlockSpec((tm, K), lambda i: (i, 0)),
                  pl.BlockSpec((K, N), lambda i: (0, 0)),
                  pl.BlockSpec((1, N), lambda i: (0, 0))] + ([] if after is None else [_TOKEN_SPEC]),
        out_specs=pl.BlockSpec((tm, N), lambda i: (i, 0)),
        out_shape=jax.ShapeDtypeStruct((M, N), out_dtype),
        compiler_params=_params(("parallel",)),
    )(a, w, bias, *([] if after is None else [after]))


def mm_nt(a, w, acc_in, name, after=None, w_block=0, out_dtype=F32):
    pieces = list(a) if isinstance(a, (list, tuple)) else [a]
    M = pieces[0].shape[0]
    widths = [p.shape[1] for p in pieces]
    K = sum(widths)
    N = w.shape[0]
    tm = _pick(M, 1024, 8)
    tc = _pick(N, 512)
    has_acc = acc_in is not None
    n_a = len(pieces)

    def body(*refs):
        a_refs, w_ref = refs[:n_a], refs[n_a]
        c_ref = refs[n_a + 1] if has_acc else None
        o_ref = refs[-1]
        av = a_refs[0][...] if n_a == 1 else jnp.concatenate([r[...] for r in a_refs], axis=1)
        for j in range(N // tc):
            cols = slice(j * tc, (j + 1) * tc)
            acc = lax.dot_general(av, w_ref[cols, :], NT_DIMS, preferred_element_type=F32)
            if has_acc:
                acc = acc + c_ref[:, cols]
            o_ref[:, cols] = acc.astype(out_dtype)

    out_spec = pl.BlockSpec((tm, N), lambda i: (i, 0))
    in_specs = [pl.BlockSpec((tm, kw), lambda i: (i, 0)) for kw in widths]
    in_specs.append(pl.BlockSpec((N, K), lambda i: (0, w_block)))
    ins = pieces + [w]
    if has_acc:
        in_specs.append(out_spec)
        ins.append(acc_in)
    if after is not None:
        in_specs.append(_TOKEN_SPEC)
        ins.append(after)
    return pl.pallas_call(
        body, name=name, grid=(M // tm,),
        in_specs=in_specs, out_specs=out_spec,
        out_shape=jax.ShapeDtypeStruct((M, N), out_dtype),
        compiler_params=_params(("parallel",)),
    )(*ins)


def mm_tn(a, b, name, out_dtype=BF16):
    pieces = list(b) if isinstance(b, (list, tuple)) else [b]
    T, M = a.shape
    widths = [p.shape[1] for p in pieces]
    N = sum(widths)
    tk = _pick(T, 1024, 8)
    nk = T // tk
    tc = _pick(M, 256)
    n_b = len(pieces)

    def body(*refs):
        a_ref, b_refs = refs[0], refs[1:1 + n_b]
        o_ref, cs_ref, acc_ref = refs[1 + n_b:]
        k = pl.program_id(0)

        @pl.when(k == 0)
        def _():
            acc_ref[...] = jnp.zeros_like(acc_ref)
            cs_ref[...] = jnp.zeros_like(cs_ref)

        bv = b_refs[0][...] if n_b == 1 else jnp.concatenate([r[...] for r in b_refs], axis=1)
        cs_ref[...] += jnp.sum(bv.astype(F32), 0, keepdims=True)
        for mi in range(M // tc):
            rows = slice(mi * tc, (mi + 1) * tc)
            acc_ref[rows, :] += lax.dot_general(a_ref[:, rows], bv, TN_DIMS, preferred_element_type=F32)

        @pl.when(k == nk - 1)
        def _():
            o_ref[...] = acc_ref[...].astype(out_dtype)

    return pl.pallas_call(
        body, name=name, grid=(nk,),
        in_specs=[pl.BlockSpec((tk, M), lambda k: (k, 0))] + [pl.BlockSpec((tk, wd), lambda k: (k, 0)) for wd in widths],
        out_specs=[pl.BlockSpec((M, N), lambda k: (0, 0)), pl.BlockSpec((1, N), lambda k: (0, 0))],
        out_shape=[jax.ShapeDtypeStruct((M, N), out_dtype), jax.ShapeDtypeStruct((1, N), F32)],
        scratch_shapes=[pltpu.VMEM((M, N), F32)],
        compiler_params=_params(("arbitrary",)),
    )(a, *pieces)


def _ext_rows(prev_ref, main_ref, next_ref, i, tm, T, dtype=F32):
    before = jnp.where(i == 0, 0.0, prev_ref[...])
    after = jnp.where(i == T // tm - 1, 0.0, next_ref[...])
    return jnp.concatenate([before, main_ref[...], after], axis=0).astype(dtype)


def _prev_row(x):
    return pltpu.roll(x, 1, 0)


def _next_row(x):
    return pltpu.roll(x, x.shape[0] - 1, 0)


def _conv3(u, w_ref):
    return _prev_row(u) * w_ref[0:1, :] + u * w_ref[1:2, :] + _next_row(u) * w_ref[2:3, :]


def _main(x, tm, halo=HALO):
    return x[halo:halo + tm]


def _halo_specs(tm, tc, T, col, order, halo=HALO):
    r = tm // halo
    last = T // halo - 1
    if order == "ij":
        return (pl.BlockSpec((halo, tc), lambda i, j: (jnp.maximum(i * r - 1, 0), col(j))),
                pl.BlockSpec((tm, tc), lambda i, j: (i, col(j))),
                pl.BlockSpec((halo, tc), lambda i, j: (jnp.minimum((i + 1) * r, last), col(j))))
    return (pl.BlockSpec((halo, tc), lambda j, i: (jnp.maximum(i * r - 1, 0), col(j))),
            pl.BlockSpec((tm, tc), lambda j, i: (i, col(j))),
            pl.BlockSpec((halo, tc), lambda j, i: (jnp.minimum((i + 1) * r, last), col(j))))


def conv_a_fwd(proj_a, conv_w, name):
    T, D3 = proj_a.shape
    D = D3 // 3
    tm = _pick(T, 256, 8)

    def body(p_ref, m_ref, n_ref, w_ref, o_ref):
        i = pl.program_id(0)
        ext = _ext_rows(p_ref, m_ref, n_ref, i, tm, T)
        u = ext[:, D:2 * D] * ext[:, 2 * D:]
        cu = _conv3(u, w_ref)
        o_ref[...] = (m_ref[:, :D].astype(F32) * _main(cu, tm, HALO_BF16)).astype(BF16)

    prev, main, nxt = _halo_specs(tm, D3, T, lambda j: 0, "ij", HALO_BF16)
    return pl.pallas_call(
        body, name=name, grid=(T // tm, 1),
        in_specs=[prev, main, nxt, pl.BlockSpec((3, D), lambda i, j: (0, 0))],
        out_specs=pl.BlockSpec((tm, D), lambda i, j: (i, 0)),
        out_shape=jax.ShapeDtypeStruct((T, D), BF16),
        compiler_params=_params(("parallel", "arbitrary")),
    )(proj_a, proj_a, proj_a, conv_w)


def conv_a_bwd(dy_a, w_a, proj_a, conv_w, name):
    T, D3 = proj_a.shape
    D = D3 // 3
    tm = _pick(T, 256, 8)

    def body(dp_ref, dm_ref, dn_ref, wa_ref, p_ref, m_ref, n_ref, w_ref, o_ref, dw_ref):
        i = pl.program_id(0)

        @pl.when(i == 0)
        def _():
            dw_ref[...] = jnp.zeros_like(dw_ref)

        ext = _ext_rows(p_ref, m_ref, n_ref, i, tm, T)
        dsa = lax.dot_general(_ext_rows(dp_ref, dm_ref, dn_ref, i, tm, T, dtype=BF16), wa_ref[...], NT_DIMS,
                              preferred_element_type=F32)
        gb, gc, hin = ext[:, :D], ext[:, D:2 * D], ext[:, 2 * D:]
        u = gc * hin
        u_prev, u_next = _prev_row(u), _next_row(u)
        cu = u_prev * w_ref[0:1, :] + u * w_ref[1:2, :] + u_next * w_ref[2:3, :]
        dcu = dsa * gb
        du = _next_row(dcu) * w_ref[0:1, :] + dcu * w_ref[1:2, :] + _prev_row(dcu) * w_ref[2:3, :]
        h = HALO_BF16
        o_ref[:, :D] = _main(dsa * cu, tm, h).astype(BF16)
        o_ref[:, D:2 * D] = _main(du * hin, tm, h).astype(BF16)
        o_ref[:, 2 * D:] = _main(du * gc, tm, h).astype(BF16)
        dcu_m = _main(dcu, tm, h)
        dw_ref[0:1, :] += jnp.sum(dcu_m * _main(u_prev, tm, h), 0, keepdims=True)
        dw_ref[1:2, :] += jnp.sum(dcu_m * _main(u, tm, h), 0, keepdims=True)
        dw_ref[2:3, :] += jnp.sum(dcu_m * _main(u_next, tm, h), 0, keepdims=True)

    dprev, dmain, dnxt = _halo_specs(tm, dy_a.shape[1], T, lambda j: 0, "ij", HALO_BF16)
    prev, main, nxt = _halo_specs(tm, D3, T, lambda j: 0, "ij", HALO_BF16)
    return pl.pallas_call(
        body, name=name, grid=(T // tm, 1),
        in_specs=[dprev, dmain, dnxt, pl.BlockSpec(w_a.shape, lambda i, j: (0, 0)), prev, main, nxt,
                  pl.BlockSpec((3, D), lambda i, j: (0, 0))],
        out_specs=[pl.BlockSpec((tm, D3), lambda i, j: (i, 0)), pl.BlockSpec((3, D), lambda i, j: (0, 0))],
        out_shape=[jax.ShapeDtypeStruct((T, D3), BF16), jax.ShapeDtypeStruct((3, D), F32)],
        compiler_params=_params(("arbitrary", "arbitrary")),
    )(dy_a, dy_a, dy_a, w_a, proj_a, proj_a, proj_a, conv_w)


_INV_SQRT2 = 1.0 / math.sqrt(2.0)
_INV_SQRT_2PI = 1.0 / math.sqrt(2.0 * math.pi)


def ffn_up_conv_f(h, w_up, b_up, fcw, fcb, name):
    T, D = h.shape
    F = fcb.shape[1]
    tm = _pick(T, 256, 8)
    tc = _pick(F, 256)
    halo = HALO_BF16

    def body(hp_ref, hm_ref, hn_ref, w_ref, b_ref, cw_ref, cb_ref, up_ref, f_ref):
        i = pl.program_id(0)
        h_ext = _ext_rows(hp_ref, hm_ref, hn_ref, i, tm, T, dtype=BF16)
        h_main = hm_ref[...]
        rows = i * tm - halo + lax.broadcasted_iota(jnp.int32, (tm + 2 * halo, 1), 0)
        inside = (rows >= 0) & (rows < T)
        for c in range(F // tc):
            cols = slice(c * tc, (c + 1) * tc)
            gcols = slice(F + c * tc, F + (c + 1) * tc)
            a_ext = jnp.dot(h_ext, w_ref[:, cols], preferred_element_type=F32) + b_ref[:, cols]
            a_ext = jnp.where(inside, a_ext, 0.0)
            gate = jnp.dot(h_main, w_ref[:, gcols], preferred_element_type=F32) + b_ref[:, gcols]
            up_ref[:, cols] = _main(a_ext, tm, halo)
            up_ref[:, gcols] = gate
            ca = _main(_prev_row(a_ext) * cw_ref[0:1, cols] + a_ext * cw_ref[1:2, cols]
                       + _next_row(a_ext) * cw_ref[2:3, cols], tm, halo) + cb_ref[:, cols]
            gl = 0.5 * ca * (1.0 + lax.erf(ca * _INV_SQRT2))
            f_ref[:, cols] = (gl * gate).astype(BF16)

    prev, main, nxt = _halo_specs(tm, D, T, lambda j: 0, "ij", halo)
    whole = lambda x: pl.BlockSpec(x.shape, lambda i, j: (0, 0))
    return pl.pallas_call(
        body, name=name, grid=(T // tm, 1),
        in_specs=[prev, main, nxt, whole(w_up), whole(b_up), whole(fcw), whole(fcb)],
        out_specs=[pl.BlockSpec((tm, 2 * F), lambda i, j: (i, 0)), pl.BlockSpec((tm, F), lambda i, j: (i, 0))],
        out_shape=[jax.ShapeDtypeStruct((T, 2 * F), F32), jax.ShapeDtypeStruct((T, F), BF16)],
        compiler_params=_params(("parallel", "arbitrary")),
    )(h, h, h, w_up, b_up, fcw, fcb)


def conv_f_bwd(dy, w_down, up, fcw, fcb, name):
    T, F2 = up.shape
    F = F2 // 2
    D = dy.shape[1]
    tm = _pick(T, 256, 8)
    tc = _pick(F, 256)

    def body(yp_ref, ym_ref, yn_ref, wd_ref, up_ref, um_ref, un_ref, w_ref, b_ref,
             da_ref, dg_ref, csa_ref, csg_ref, dfb_ref, dfw_ref):
        i = pl.program_id(0)
        first, last = i == 0, i == T // tm - 1

        @pl.when(first)
        def _():
            csa_ref[...] = jnp.zeros_like(csa_ref)
            csg_ref[...] = jnp.zeros_like(csg_ref)
            dfb_ref[...] = jnp.zeros_like(dfb_ref)
            dfw_ref[...] = jnp.zeros_like(dfw_ref)

        def ext(cols):
            return jnp.concatenate([jnp.where(first, 0.0, up_ref[:, cols]), um_ref[:, cols],
                                    jnp.where(last, 0.0, un_ref[:, cols])], axis=0)

        dy_ext = _ext_rows(yp_ref, ym_ref, yn_ref, i, tm, T, dtype=BF16)
        for c in range(F // tc):
            cols = slice(c * tc, (c + 1) * tc)
            dfe = lax.dot_general(dy_ext, wd_ref[cols, :], NT_DIMS, preferred_element_type=F32)
            dfe = dfe[HALO_BF16 - HALO:HALO_BF16 + tm + HALO]
            a = ext(cols)
            gate = ext(slice(F + c * tc, F + (c + 1) * tc))
            a_prev, a_next = _prev_row(a), _next_row(a)
            ca = a_prev * w_ref[0:1, cols] + a * w_ref[1:2, cols] + a_next * w_ref[2:3, cols] + b_ref[:, cols]
            cdf = 0.5 * (1.0 + lax.erf(ca * _INV_SQRT2))
            gl = ca * cdf
            gp = cdf + ca * (jnp.exp(-0.5 * ca * ca) * _INV_SQRT_2PI)
            dgate = _main(dfe * gl, tm)
            dca = dfe * gate * gp
            da = _main(_next_row(dca) * w_ref[0:1, cols] + dca * w_ref[1:2, cols] + _prev_row(dca) * w_ref[2:3, cols],
                       tm)
            da_ref[:, cols] = da.astype(BF16)
            dg_ref[:, cols] = dgate.astype(BF16)
            csa_ref[:, cols] += jnp.sum(da, 0, keepdims=True)
            csg_ref[:, cols] += jnp.sum(dgate, 0, keepdims=True)
            dca_m = _main(dca, tm)
            dfb_ref[:, cols] += jnp.sum(dca_m, 0, keepdims=True)
            dfw_ref[0:1, cols] += jnp.sum(dca_m * _main(a_prev, tm), 0, keepdims=True)
            dfw_ref[1:2, cols] += jnp.sum(dca_m * _main(a, tm), 0, keepdims=True)
            dfw_ref[2:3, cols] += jnp.sum(dca_m * _main(a_next, tm), 0, keepdims=True)

    uprev, umain, unxt = _halo_specs(tm, F2, T, lambda j: 0, "ij")
    yprev, ymain, ynxt = _halo_specs(tm, D, T, lambda j: 0, "ij", HALO_BF16)
    whole = lambda shape: pl.BlockSpec(shape, lambda i, j: (0, 0))
    tile = pl.BlockSpec((tm, F), lambda i, j: (i, 0))
    return pl.pallas_call(
        body, name=name, grid=(T // tm, 1),
        in_specs=[yprev, ymain, ynxt, whole((F, D)), uprev, umain, unxt, whole((3, F)), whole((1, F))],
        out_specs=[tile, tile, whole((1, F)), whole((1, F)), whole((1, F)), whole((3, F))],
        out_shape=[jax.ShapeDtypeStruct((T, F), BF16), jax.ShapeDtypeStruct((T, F), BF16),
                   jax.ShapeDtypeStruct((1, F), F32), jax.ShapeDtypeStruct((1, F), F32),
                   jax.ShapeDtypeStruct((1, F), F32), jax.ShapeDtypeStruct((3, F), F32)],
        compiler_params=_params(("arbitrary", "arbitrary")),
    )(dy, dy, dy, w_down, up, up, up, fcw, fcb)


def gate_fwd(proj_g, y_a, y_b, name):
    T, D = y_a.shape
    tm = _pick(T, 512, 8)

    def body(g_ref, a_ref, b_ref, o_ref):
        sa = jax.nn.sigmoid(g_ref[:, :D].astype(F32))
        sb = jax.nn.sigmoid(g_ref[:, D:].astype(F32))
        o_ref[...] = (sa * a_ref[...].astype(F32) + sb * b_ref[...].astype(F32)).astype(BF16)

    row = pl.BlockSpec((tm, D), lambda i: (i, 0))
    return pl.pallas_call(
        body, name=name, grid=(T // tm,),
        in_specs=[pl.BlockSpec((tm, 2 * D), lambda i: (i, 0)), row, row],
        out_specs=row,
        out_shape=jax.ShapeDtypeStruct((T, D), BF16),
        compiler_params=_params(("parallel",)),
    )(proj_g, y_a, y_b)


def gate_bwd(dmix, w_o, proj_g, y_a, y_b, name):
    T, D = y_a.shape
    tm = _pick(T, 512, 8)

    def body(dz_ref, w_ref, g_ref, a_ref, b_ref, da_ref, db_ref, dg_ref):
        dzv = lax.dot_general(dz_ref[...], w_ref[...], NT_DIMS, preferred_element_type=F32)
        sa = jax.nn.sigmoid(g_ref[:, :D].astype(F32))
        sb = jax.nn.sigmoid(g_ref[:, D:].astype(F32))
        da_ref[...] = (dzv * sa).astype(BF16)
        db_ref[...] = (dzv * sb).astype(BF16)
        dg_ref[:, :D] = (dzv * a_ref[...].astype(F32) * (sa * (1.0 - sa))).astype(BF16)
        dg_ref[:, D:] = (dzv * b_ref[...].astype(F32) * (sb * (1.0 - sb))).astype(BF16)

    row = pl.BlockSpec((tm, D), lambda i: (i, 0))
    wide = pl.BlockSpec((tm, 2 * D), lambda i: (i, 0))
    return pl.pallas_call(
        body, name=name, grid=(T // tm,),
        in_specs=[pl.BlockSpec((tm, dmix.shape[1]), lambda i: (i, 0)), pl.BlockSpec(w_o.shape, lambda i: (0, 0)),
                  wide, row, row],
        out_specs=[row, row, wide],
        out_shape=[jax.ShapeDtypeStruct((T, D), BF16), jax.ShapeDtypeStruct((T, D), BF16),
                   jax.ShapeDtypeStruct((T, 2 * D), BF16)],
        compiler_params=_params(("parallel",)),
    )(dmix, w_o, proj_g, y_a, y_b)


ATT_WIN = ATT_TQ + 2 * RADIUS
ATT_STEP = 2048
FAR = 1e32


def _att_window(qs, L):
    ks = pl.multiple_of(jnp.clip(qs - RADIUS, 0, L - ATT_WIN), RADIUS)
    return ks, jnp.where(qs == 0, 0, jnp.where(qs == L - ATT_TQ, 2, 1))


def _fill_bias_tables(bias_ref, sl_ref, hp, d):
    col_row = (lax.broadcasted_iota(jnp.int32, (ATT_TQ, ATT_WIN), 1)
               - lax.broadcasted_iota(jnp.int32, (ATT_TQ, ATT_WIN), 0))
    for v in range(3):
        ad = jnp.abs(col_row - v * RADIUS)
        dist = jnp.where(ad <= RADIUS, (ad * d).astype(F32), FAR)
        bias_ref[v, 0:ATT_TQ, :] = sl_ref[hp * 2] * dist
        bias_ref[v, ATT_TQ:2 * ATT_TQ, :] = sl_ref[hp * 2 + 1] * dist


def _head_masks():
    lane = lax.broadcasted_iota(jnp.int32, (1, LANES), 1)
    return [lane < HEAD_DIM, lane >= HEAD_DIM]


def _stack_heads(x, masks):
    zero = jnp.zeros_like(x)
    return jnp.concatenate([jnp.where(masks[0], x, zero), jnp.where(masks[1], x, zero)], axis=0)


def _unstack_heads(x2, masks):
    n = x2.shape[0] // 2
    return jnp.where(masks[0], x2[:n], x2[n:])


def _att_step(L):
    step = min(ATT_STEP, L)
    assert L % step == 0 and step % ATT_TQ == 0 and L >= ATT_WIN
    return step


def _residues_per_step(d, L):
    rps = max(1, min(d, ATT_STEP // L))
    assert d % rps == 0
    return rps


def att_fwd(qkv, group, name):
    d, L, _ = qkv.shape
    step = _att_step(L)
    rps = _residues_per_step(d, L)
    cg = GROUP_W // LANES
    slopes = jnp.asarray(_alibi_slopes()[group])
    scale = HEAD_DIM ** -0.5

    def body(sl_ref, q_ref, k_ref, v_ref, o_ref, l_ref, bias_ref, s_ref, p_ref):
        hp = pl.program_id(1)
        i = pl.program_id(2)

        @pl.when(i == 0)
        def _():
            _fill_bias_tables(bias_ref, sl_ref, hp, d)

        masks = _head_masks()
        per = step // ATT_TQ
        tiles = [(rr, t) for rr in range(rps) for t in range(per)]
        windows = [_att_window(i * step + t * ATT_TQ, L) for t in range(per)]
        for n, (rr, t) in enumerate(tiles):
            rows = slice(t * ATT_TQ, (t + 1) * ATT_TQ)
            ks, table = windows[t]
            q2 = _stack_heads(q_ref[rr, rows, :] * scale, masks)
            kw = k_ref[rr, pl.ds(ks, ATT_WIN), :]
            s_ref[n] = lax.dot_general(q2, kw, NT_DIMS, preferred_element_type=F32) - bias_ref[table]
        for n, (rr, t) in enumerate(tiles):
            rows = slice(t * ATT_TQ, (t + 1) * ATT_TQ)
            s = s_ref[n]
            m = jnp.max(s, -1, keepdims=True)
            p = jnp.exp(s - m)
            den = jnp.sum(p, -1, keepdims=True)
            p_ref[n] = (p / den).astype(BF16)
            l_ref[rr, rows, :] = _unstack_heads(m + jnp.log(den), masks)
        for n, (rr, t) in enumerate(tiles):
            rows = slice(t * ATT_TQ, (t + 1) * ATT_TQ)
            vw = v_ref[rr, pl.ds(windows[t][0], ATT_WIN), :]
            o2 = jnp.dot(p_ref[n], vw, preferred_element_type=F32)
            o_ref[rr, rows, :] = _unstack_heads(o2, masks).astype(ACT)

    n_tiles = rps * step // ATT_TQ
    out_spec = pl.BlockSpec((rps, step, LANES), lambda r, hp, i: (r, i, hp))
    return pl.pallas_call(
        body, name=name, grid=(d // rps, cg, L // step),
        in_specs=[pl.BlockSpec(memory_space=pltpu.SMEM),
                  pl.BlockSpec((rps, step, LANES), lambda r, hp, i: (r, i, hp)),
                  pl.BlockSpec((rps, L, LANES), lambda r, hp, i: (r, 0, cg + hp)),
                  pl.BlockSpec((rps, L, LANES), lambda r, hp, i: (r, 0, 2 * cg + hp))],
        out_specs=[out_spec, out_spec],
        out_shape=[jax.ShapeDtypeStruct((d, L, GROUP_W), ACT), jax.ShapeDtypeStruct((d, L, GROUP_W), F32)],
        scratch_shapes=[pltpu.VMEM((3, 2 * ATT_TQ, ATT_WIN), F32),
                        pltpu.VMEM((n_tiles, 2 * ATT_TQ, ATT_WIN), F32),
                        pltpu.VMEM((n_tiles, 2 * ATT_TQ, ATT_WIN), BF16)],
        compiler_params=_params(("arbitrary", "arbitrary", "arbitrary")),
    )(slopes, qkv, qkv, qkv)


def att_bwd(qkv, do, lse, dmat, group, name, after=None):
    d, L, _ = qkv.shape
    step = _att_step(L)
    rps = _residues_per_step(d, L)
    nq = L // step
    cg = GROUP_W // LANES
    slopes = jnp.asarray(_alibi_slopes()[group])
    scale = HEAD_DIM ** -0.5

    def body(sl_ref, q_ref, k_ref, v_ref, do_ref, l_ref, dm_ref, *rest):
        dq_ref, dk_ref, dv_ref, dk_acc, dv_acc, bias_ref, s_ref, dp_ref, p_ref, ds_ref = rest[len(rest) - 10:]
        hp = pl.program_id(1)
        i = pl.program_id(2)

        @pl.when(i == 0)
        def _():
            dk_acc[...] = jnp.zeros_like(dk_acc)
            dv_acc[...] = jnp.zeros_like(dv_acc)
            _fill_bias_tables(bias_ref, sl_ref, hp, d)

        masks = _head_masks()

        def head_cols(x):
            return jnp.concatenate([jnp.max(jnp.where(hm, x, -jnp.inf), -1, keepdims=True) for hm in masks], axis=0)

        per = step // ATT_TQ
        tiles = [(rr, t) for rr in range(rps) for t in range(per)]
        windows = [_att_window(i * step + t * ATT_TQ, L) for t in range(per)]

        def stacked(ref, rr, t, factor=None):
            x = ref[rr, t * ATT_TQ:(t + 1) * ATT_TQ, :]
            return _stack_heads(x if factor is None else x * factor, masks)

        for n, (rr, t) in enumerate(tiles):
            ks, table = windows[t]
            q2 = stacked(q_ref, rr, t, scale)
            s_ref[n] = lax.dot_general(q2, k_ref[rr, pl.ds(ks, ATT_WIN), :], NT_DIMS,
                                       preferred_element_type=F32) - bias_ref[table]
            dp_ref[n] = lax.dot_general(stacked(do_ref, rr, t), v_ref[rr, pl.ds(ks, ATT_WIN), :], NT_DIMS,
                                        preferred_element_type=F32)
        for n, (rr, t) in enumerate(tiles):
            rows = slice(t * ATT_TQ, (t + 1) * ATT_TQ)
            p = jnp.exp(s_ref[n] - head_cols(l_ref[rr, rows, :]))
            p_ref[n] = p.astype(BF16)
            ds_ref[n] = (p * (dp_ref[n] - head_cols(dm_ref[rr, rows, :]))).astype(BF16)
        for n, (rr, t) in enumerate(tiles):
            rows = slice(t * ATT_TQ, (t + 1) * ATT_TQ)
            ks = windows[t][0]
            ds = ds_ref[n]
            dq2 = jnp.dot(ds, k_ref[rr, pl.ds(ks, ATT_WIN), :], preferred_element_type=F32)
            dq_ref[rr, rows, :] = (_unstack_heads(dq2, masks) * scale).astype(BF16)
            dk_acc[rr, pl.ds(ks, ATT_WIN), :] += lax.dot_general(ds, stacked(q_ref, rr, t, scale), TN_DIMS,
                                                                 preferred_element_type=F32)
            dv_acc[rr, pl.ds(ks, ATT_WIN), :] += lax.dot_general(p_ref[n], stacked(do_ref, rr, t), TN_DIMS,
                                                                 preferred_element_type=F32)

        @pl.when(i == nq - 1)
        def _():
            dk_ref[...] = dk_acc[...].astype(BF16)
            dv_ref[...] = dv_acc[...].astype(BF16)

    tile = pl.BlockSpec((rps, step, LANES), lambda r, hp, i: (r, i, hp))
    whole = pl.BlockSpec((rps, L, LANES), lambda r, hp, i: (r, 0, hp))
    return pl.pallas_call(
        body, name=name, grid=(d // rps, cg, nq),
        in_specs=[pl.BlockSpec(memory_space=pltpu.SMEM), tile,
                  pl.BlockSpec((rps, L, LANES), lambda r, hp, i: (r, 0, cg + hp)),
                  pl.BlockSpec((rps, L, LANES), lambda r, hp, i: (r, 0, 2 * cg + hp)),
                  tile, tile, tile] + ([] if after is None else [pl.BlockSpec((8, LANES), lambda r, hp, i: (0, 0))]),
        out_specs=[tile, whole, whole],
        out_shape=[jax.ShapeDtypeStruct((d, L, GROUP_W), BF16)] * 3,
        scratch_shapes=[pltpu.VMEM((rps, L, LANES), F32), pltpu.VMEM((rps, L, LANES), F32),
                        pltpu.VMEM((3, 2 * ATT_TQ, ATT_WIN), F32)]
        + [pltpu.VMEM((rps * step // ATT_TQ, 2 * ATT_TQ, ATT_WIN), dt) for dt in (F32, F32, BF16, BF16)],
        compiler_params=_params(("arbitrary", "arbitrary", "arbitrary")),
    )(slopes, qkv, qkv, qkv, do, lse, dmat, *([] if after is None else [after]))


def _group_weights(ls):
    m = jnp.maximum(jnp.maximum(ls[0], ls[1]), ls[2])
    es = [jnp.exp(l - m) for l in ls]
    tot = es[0] + es[1] + es[2]
    return [e / tot for e in es]


def combine_fwd(outs, lses, name):
    T = outs[0].shape[0] * outs[0].shape[1]
    tm = _pick(T, 512, 8)
    n_scr = 2 * (len(DILATIONS) - 1)

    def body(*refs):
        o_refs, l_refs, c_ref, scr = refs[:3], refs[3:6], refs[6], refs[7:]
        def load(ref, d, buf, c, cols):
            if d == 1:
                return ref[0, :, cols].astype(F32)
            n = ref.shape[1]
            for r in range(d):
                buf[c, pl.ds(r, n, stride=d), :] = ref[r, :, cols].astype(F32)
            return buf[c]

        for c in range(GROUP_W // LANES):
            cols = slice(c * LANES, (c + 1) * LANES)
            o = [load(o_refs[g], d, scr[g - 1] if g else None, c, cols) for g, d in enumerate(DILATIONS)]
            l = [load(l_refs[g], d, scr[g + 1] if g else None, c, cols) for g, d in enumerate(DILATIONS)]
            w = _group_weights(l)
            c_ref[:, cols] = (w[0] * o[0] + w[1] * o[1] + w[2] * o[2]).astype(BF16)

    specs = [_residue_spec(tm, d, GROUP_W) for d in DILATIONS]
    return pl.pallas_call(
        body, name=name, grid=(T // tm,),
        in_specs=specs + specs, out_specs=pl.BlockSpec((tm, GROUP_W), lambda i: (i, 0)),
        out_shape=jax.ShapeDtypeStruct((T, GROUP_W), BF16),
        scratch_shapes=[_residue_scratch(tm, GROUP_W)] * n_scr,
        compiler_params=_params(("parallel",)),
    )(*outs, *lses)


def combine_bwd(dcomb, outs, lses, name):
    T = dcomb.shape[0]
    tm = _pick(T, 256, 8)
    head = np.arange(GROUP_W) // HEAD_DIM
    seg = jnp.asarray((head[:, None] == head[None, :]).astype(np.float32)).astype(BF16)
    ng = len(DILATIONS)
    n_scr = 4 * (ng - 1)

    def body(*refs):
        dc_ref, o_refs, l_refs, e_ref = refs[0], refs[1:1 + ng], refs[1 + ng:1 + 2 * ng], refs[1 + 2 * ng]
        do_refs, dm_refs = refs[2 + 2 * ng:2 + 3 * ng], refs[2 + 3 * ng:2 + 4 * ng]
        scr = refs[2 + 4 * ng:]
        def load(ref, d, buf, c, cols):
            if d == 1:
                return ref[0, :, cols].astype(F32)
            n = ref.shape[1]
            for r in range(d):
                buf[c, pl.ds(r, n, stride=d), :] = ref[r, :, cols].astype(F32)
            return buf[c]

        def store(val, ref, d, buf, c, cols):
            if d == 1:
                ref[0, :, cols] = val.astype(ref.dtype)
                return
            n = ref.shape[1]
            buf[c] = val
            for r in range(d):
                ref[r, :, cols] = buf[c, pl.ds(r, n, stride=d), :].astype(ref.dtype)

        for c in range(GROUP_W // LANES):
            cols = slice(c * LANES, (c + 1) * LANES)
            o = [load(o_refs[g], d, scr[4 * (g - 1)] if g else None, c, cols) for g, d in enumerate(DILATIONS)]
            l = [load(l_refs[g], d, scr[4 * (g - 1) + 1] if g else None, c, cols) for g, d in enumerate(DILATIONS)]
            w = _group_weights(l)
            dc = dc_ref[:, cols].astype(F32)
            e = e_ref[cols, cols]
            prod = dc * (w[0] * o[0] + w[1] * o[1] + w[2] * o[2])
            tot = jnp.zeros_like(dc)
            for _ in range(3):
                part = prod.astype(BF16)
                tot = tot + jnp.dot(part, e, preferred_element_type=F32)
                prod = prod - part.astype(F32)
            for g, d in enumerate(DILATIONS):
                store(w[g] * dc, do_refs[g], d, scr[4 * (g - 1) + 2] if g else None, c, cols)
                store(w[g] * tot, dm_refs[g], d, scr[4 * (g - 1) + 3] if g else None, c, cols)

    specs = [_residue_spec(tm, d, GROUP_W) for d in DILATIONS]
    res = pl.pallas_call(
        body, name=name, grid=(T // tm,),
        in_specs=[pl.BlockSpec((tm, GROUP_W), lambda i: (i, 0))] + specs + specs
        + [pl.BlockSpec((GROUP_W, GROUP_W), lambda i: (0, 0))],
        out_specs=specs + specs,
        out_shape=[jax.ShapeDtypeStruct(o.shape, BF16) for o in outs] + [jax.ShapeDtypeStruct(o.shape, F32) for o in outs],
        scratch_shapes=[_residue_scratch(tm, GROUP_W)] * n_scr,
        compiler_params=_params(("parallel",)),
    )(dcomb, *outs, *lses, seg)
    return res[:ng], res[ng:]


def _position():
    return lax.axis_index("x"), lax.axis_index("y"), lax.axis_index("c")


def _other_chips(x, y):
    return [(1 - x, y), (x, 1 - y), (1 - x, 1 - y)]


def _remote(src, dst, send_sems, recv_sems, k, to):
    return pltpu.make_async_remote_copy(src_ref=src, dst_ref=dst, send_sem=send_sems.at[k], recv_sem=recv_sems.at[k],
                                        device_id=to, device_id_type=MESH)


GATHER_SEMS = 10
SPLIT_ROWS = 32


def _gather_plan(ins, outs, send_sems, recv_sems, local_sems):
    x, y, c = _position()
    sibling = (x, y, 1 - c)
    nbr_x, nbr_y, diag = (1 - x, y, c), (x, 1 - y, c), (1 - x, 1 - y, c)
    local, begin, stages, last = [], [], [], []
    for a in range(len(ins)):
        k0 = GATHER_SEMS * a
        rows = ins[a].shape[0]
        half = rows // 2

        def block(dev):
            return outs[a].at[4 * dev[0] + 2 * dev[1] + dev[2]]

        def part(ref, h):
            return ref.at[pl.ds(h * half, half)]

        def copy(k, src, dst, to):
            return _remote(src, dst, send_sems, recv_sems, k0 + k, to)

        me = (x, y, c)
        local.append(pltpu.make_async_copy(ins[a], block(me), local_sems.at[a]))
        begin.append(copy(0, ins[a], block(me), sibling))
        pass_on = [copy(7 + j, block(dev), block(dev), sibling) for j, dev in enumerate((nbr_x, nbr_y, diag))]
        if rows >= SPLIT_ROWS and rows % SPLIT_ROWS == 0:
            for h in range(2):
                begin.append(copy(1 + h, part(ins[a], h), part(block(me), h), nbr_x))
                begin.append(copy(3 + h, part(ins[a], h), part(block(me), h), nbr_y))
            from_x = [copy(1 + h, part(block(nbr_x), h), part(block(nbr_x), h), sibling) for h in range(2)]
            from_y = [copy(3 + h, part(block(nbr_y), h), part(block(nbr_y), h), sibling) for h in range(2)]
            fwd_0 = copy(5, part(block(nbr_x), 0), part(block(nbr_x), 0), nbr_y)
            fwd_1 = copy(6, part(block(nbr_y), 1), part(block(nbr_y), 1), nbr_x)
            got_0 = copy(5, part(block(diag), 0), part(block(diag), 0), sibling)
            got_1 = copy(6, part(block(diag), 1), part(block(diag), 1), sibling)
            stages.append(([from_x[0]], [fwd_0]))
            stages.append(([from_y[1]], [fwd_1]))
            stages.append(([from_x[1]], [pass_on[0]]))
            stages.append(([from_y[0]], [pass_on[1]]))
            stages.append(([got_0, got_1], [pass_on[2]]))
        else:
            for j, dev in enumerate((nbr_x, nbr_y, diag)):
                begin.append(copy(1 + 2 * j, ins[a], block(me), dev))
                stages.append(([copy(1 + 2 * j, block(dev), block(dev), sibling)], [pass_on[j]]))
        other = (x, y, 1 - c)
        last.append(copy(0, block(other), block(other), sibling))
        for j, dev in enumerate((nbr_x, nbr_y, diag)):
            theirs = (dev[0], dev[1], 1 - c)
            last.append(copy(7 + j, block(theirs), block(theirs), sibling))
    return local, begin, stages, last


def _gather_begin(ins, outs, send_sems, recv_sems, local_sems):
    local, begin, _, _ = _gather_plan(ins, outs, send_sems, recv_sems, local_sems)
    for cp in local + begin:
        cp.start()


def _gather_finish(ins, outs, send_sems, recv_sems, local_sems):
    local, begin, stages, last = _gather_plan(ins, outs, send_sems, recv_sems, local_sems)
    started = []
    for arrivals, onward in stages:
        for cp in arrivals:
            cp.wait_recv()
        for cp in onward:
            cp.start()
            started.append(cp)
    for cp in last:
        cp.wait_recv()
    for cp in begin + started:
        cp.wait_send()
    for cp in local:
        cp.wait()


def _gather_scratch(n):
    return [pltpu.SemaphoreType.DMA((GATHER_SEMS * n,)), pltpu.SemaphoreType.DMA((GATHER_SEMS * n,)),
            pltpu.SemaphoreType.DMA((n,))]


_HBM = pl.BlockSpec(memory_space=pltpu.HBM)
_SEM = pl.BlockSpec(memory_space=pltpu.SEMAPHORE)
_DATAFLOW = pltpu.SideEffectType.DATAFLOW_SIDE_EFFECTING


def _to_all_plan(srcs, lands, send_sems, recv_sems):
    x, y, c = _position()
    me = 4 * x + 2 * y + c
    copies = []
    for a in range(len(srcs)):
        for k in range(1, N_DEV):
            fx, fy, fc = (k >> 2) & 1, (k >> 1) & 1, k & 1
            to = (1 - x if fx else x, 1 - y if fy else y, 1 - c if fc else c)
            copies.append(_remote(srcs[a], lands[a].at[me], send_sems, recv_sems, (N_DEV - 1) * a + k - 1, to))
    return copies


def _to_sibling_plan(srcs, lands, send_sems, recv_sems):
    x, y, c = _position()
    copies = []
    for a in range(len(srcs)):
        for q in range(4):
            copies.append(_remote(srcs[a].at[2 * q + (1 - c)], lands[a].at[q], send_sems, recv_sems, 4 * a + q,
                                  (x, y, 1 - c)))
    return copies


def _to_chips_plan(srcs, lands, send_sems, recv_sems):
    x, y, c = _position()
    copies = []
    for a in range(len(srcs)):
        for j, (cx, cy) in enumerate(_other_chips(x, y)):
            copies.append(_remote(srcs[a].at[2 * cx + cy], lands[a].at[j], send_sems, recv_sems, 3 * a + j, (cx, cy, c)))
    return copies


def copies_start(srcs, land_shapes, plan, per_array, name):
    n = len(srcs)
    n_sem = per_array * n
    lands = [lax.empty(s.shape, s.dtype) for s in land_shapes]

    def body(*refs):
        src_refs, land_refs = refs[:n], refs[n:2 * n]
        send_sems, recv_sems = refs[2 * n], refs[2 * n + 1]
        token = refs[-1]
        for cp in plan(src_refs, land_refs, send_sems, recv_sems):
            cp.start()
        token[...] = jnp.zeros_like(token)

    out = pl.pallas_call(
        body, name=name,
        out_shape=(pltpu.SemaphoreType.DMA((n_sem,)), pltpu.SemaphoreType.DMA((n_sem,)))
        + tuple(pltpu.HBM(s.shape, s.dtype) for s in srcs)
        + tuple(pltpu.HBM(s.shape, s.dtype) for s in land_shapes)
        + (jax.ShapeDtypeStruct((8, LANES), F32),),
        in_specs=[_HBM] * (2 * n),
        out_specs=(_SEM, _SEM) + (_HBM,) * (2 * n) + (pl.BlockSpec(memory_space=pltpu.VMEM),),
        input_output_aliases={i: 2 + i for i in range(2 * n)},
        compiler_params=pltpu.CompilerParams(has_side_effects=_DATAFLOW),
    )(*[pltpu.with_memory_space_constraint(s, pltpu.HBM) for s in srcs],
      *[pltpu.with_memory_space_constraint(l, pltpu.HBM) for l in lands])
    return out[:-1], out[-1]


def copies_wait(handles, plan, after, name):
    send_sems, recv_sems = handles[0], handles[1]
    n = (len(handles) - 2) // 2
    thru = handles[2:]

    def body(*refs):
        src_refs, land_refs = refs[:n], refs[n:2 * n]
        send_sems, recv_sems = refs[2 * n], refs[2 * n + 1]
        copies = plan(src_refs, land_refs, send_sems, recv_sems)
        for cp in copies:
            cp.wait_recv()
        for cp in copies:
            cp.wait_send()

    out = pl.pallas_call(
        body, name=name,
        out_shape=tuple(pltpu.HBM(t.shape, t.dtype) for t in thru),
        in_specs=[_HBM] * (2 * n) + [_SEM, _SEM, pl.BlockSpec(memory_space=pl.ANY)],
        out_specs=(_HBM,) * (2 * n),
        input_output_aliases={i: i for i in range(2 * n)},
        compiler_params=pltpu.CompilerParams(has_side_effects=_DATAFLOW),
    )(*thru, send_sems, recv_sems, after)
    return out[:n], out[n:]


def all_sum_small(vec, name):
    R = vec.shape[0]

    def body(v_ref, tot_ref, all_ref, send_sems, recv_sems):
        x, y, c = _position()
        me = 4 * x + 2 * y + c
        all_ref[me] = v_ref[...]
        copies = []
        for k in range(1, N_DEV):
            fx, fy, fc = (k >> 2) & 1, (k >> 1) & 1, k & 1
            to = (1 - x if fx else x, 1 - y if fy else y, 1 - c if fc else c)
            cp = _remote(v_ref, all_ref.at[me], send_sems, recv_sems, k - 1, to)
            cp.start()
            copies.append(cp)
        for cp in copies:
            cp.wait_recv()
        for cp in copies:
            cp.wait_send()
        tot = all_ref[0]
        for j in range(1, N_DEV):
            tot = tot + all_ref[j]
        tot_ref[...] = tot

    vmem = pl.BlockSpec(memory_space=pltpu.VMEM)
    return pl.pallas_call(
        body, name=name,
        in_specs=[vmem], out_specs=vmem,
        out_shape=jax.ShapeDtypeStruct((R, LANES), F32),
        scratch_shapes=[pltpu.VMEM((N_DEV, R, LANES), F32),
                        pltpu.SemaphoreType.DMA((N_DEV - 1,)), pltpu.SemaphoreType.DMA((N_DEV - 1,))],
        compiler_params=pltpu.CompilerParams(vmem_limit_bytes=VMEM_LIMIT),
    )(vec)


def pair_add(parts, theirs, place, name):
    _, R, C = theirs.shape
    tr = _pick(R, 1024, 8)

    def body(place_ref, a_ref, b_ref, o_ref):
        o_ref[...] = (a_ref[...].astype(F32) + b_ref[...].astype(F32)).astype(BF16)

    blk = pl.BlockSpec((None, tr, C), lambda q, i, place_ref: (q, i, 0))
    return pl.pallas_call(
        body, name=name,
        grid_spec=pltpu.PrefetchScalarGridSpec(
            num_scalar_prefetch=1, grid=(4, R // tr),
            in_specs=[pl.BlockSpec((None, tr, C), lambda q, i, place_ref: (2 * q + place_ref[2], i, 0)), blk],
            out_specs=blk),
        out_shape=jax.ShapeDtypeStruct(theirs.shape, BF16),
        compiler_params=_params(("parallel", "parallel")),
    )(place, parts, theirs)


def _adamw_math(w, g, m, v):
    m = ADAM_B1 * m + (1.0 - ADAM_B1) * g
    v = ADAM_B2 * v + (1.0 - ADAM_B2) * jnp.square(g)
    m_hat = m / (1.0 - ADAM_B1 ** ADAM_STEP)
    v_hat = v / (1.0 - ADAM_B2 ** ADAM_STEP)
    delta = -ADAM_LR * (m_hat / (jnp.sqrt(v_hat) + ADAM_EPS) + ADAM_WD * w)
    return delta, m, v


def adamw_sharded(w, m, v, parts, sib, others, place, name):
    R, C = w.shape
    tr = _pick(R, 256, 8)

    def body(place_ref, w_ref, m_ref, v_ref, a_ref, b_ref, o_ref, g_ref, d_ref, nm_ref, nv_ref):
        g = a_ref[...].astype(F32) + b_ref[...].astype(F32)
        for j in range(3):
            g = g + o_ref[j].astype(F32)
        delta, nm, nv = _adamw_math(w_ref[...], g, m_ref[...], v_ref[...])
        g_ref[...] = g
        d_ref[...] = delta
        nm_ref[...] = nm
        nv_ref[...] = nv

    row = pl.BlockSpec((tr, C), lambda i, place_ref: (i, 0))
    return pl.pallas_call(
        body, name=name,
        grid_spec=pltpu.PrefetchScalarGridSpec(
            num_scalar_prefetch=1, grid=(R // tr,),
            in_specs=[row] * 3 + [pl.BlockSpec((None, tr, C), lambda i, place_ref: (place_ref[0], i, 0)),
                                  pl.BlockSpec((None, tr, C), lambda i, place_ref: (place_ref[1], i, 0)),
                                  pl.BlockSpec((3, tr, C), lambda i, place_ref: (0, i, 0))],
            out_specs=[row] * 4),
        out_shape=[jax.ShapeDtypeStruct((R, C), F32)] * 4,
        compiler_params=_params(("parallel",)),
    )(place, w, m, v, parts, sib, others)


def adamw_packed(w, g, m, v, name):
    R = w.shape[0]

    def body(w_ref, g_ref, m_ref, v_ref, d_ref, nm_ref, nv_ref):
        delta, nm, nv = _adamw_math(w_ref[...], g_ref[...], m_ref[...], v_ref[...])
        d_ref[...] = delta
        nm_ref[...] = nm
        nv_ref[...] = nv

    full = pl.BlockSpec((R, LANES), lambda i: (0, 0))
    return pl.pallas_call(
        body, name=name, grid=(1,),
        in_specs=[full] * 4, out_specs=[full] * 3,
        out_shape=[jax.ShapeDtypeStruct((R, LANES), F32)] * 3,
        compiler_params=_params(("arbitrary",)),
    )(w, g, m, v)


def _pack(arrays):
    flat = []
    sizes = []
    for a in arrays:
        f = a.reshape(-1).astype(F32)
        pad = (-f.shape[0]) % LANES
        if pad:
            f = jnp.concatenate([f, jnp.zeros((pad,), F32)])
        flat.append(f)
        sizes.append(f.shape[0])
    rows = sum(sizes) // LANES
    pad_rows = (-rows) % 8
    if pad_rows:
        flat.append(jnp.zeros((pad_rows * LANES,), F32))
    return jnp.concatenate(flat).reshape(-1, LANES), sizes


def _unpack(packed, sizes, shapes):
    flat = packed.reshape(-1)
    out = []
    off = 0
    for size, shape in zip(sizes, shapes):
        n = int(np.prod(shape))
        out.append(flat[off:off + n].reshape(shape))
        off += size
    return out


def _to_blocks(full, axis):
    if axis == 0:
        return full.reshape(N_DEV, full.shape[0] // N_DEV, full.shape[1])
    r, n = full.shape
    return full.reshape(r, N_DEV, n // N_DEV).transpose(1, 0, 2)


def _from_blocks(blocks, axis):
    if axis == 0:
        return blocks.reshape(blocks.shape[0] * blocks.shape[1], blocks.shape[2])
    return blocks.transpose(1, 0, 2).reshape(blocks.shape[1], blocks.shape[0] * blocks.shape[2])


def kernel(x, ln0_g, ln0_b, w_in, b_in, conv_w, w_a, w_b, w_o, b_o, ln1_g, ln1_b, w_up, b_up, ffn_conv_w, ffn_conv_b, w_down, b_down, ln2_g, ln2_b, loss_target, m_ln0_g, m_ln0_b, m_w_in, m_b_in, m_conv_w, m_w_a, m_w_b, m_w_o, m_b_o, m_ln1_g, m_ln1_b, m_w_up, m_b_up, m_ffn_conv_w, m_ffn_conv_b, m_w_down, m_b_down, m_ln2_g, m_ln2_b, v_ln0_g, v_ln0_b, v_w_in, v_b_in, v_conv_w, v_w_a, v_w_b, v_w_o, v_b_o, v_ln1_g, v_ln1_b, v_w_up, v_b_up, v_ffn_conv_w, v_ffn_conv_b, v_w_down, v_b_down, v_ln2_g, v_ln2_b):
    T, D = x.shape[1], x.shape[2]
    F = ffn_conv_b.shape[-1]
    xs = x.reshape(T, D)
    tgt = loss_target.reshape(T, D)
    dev = 4 * lax.axis_index("x") + 2 * lax.axis_index("y") + lax.axis_index("c")
    chip = 2 * lax.axis_index("x") + lax.axis_index("y")
    core = lax.axis_index("c")
    place = jnp.stack([dev, chip, core]).astype(jnp.int32)

    big = dict(w_in=(w_in[0], 1), w_a=(w_a[0], 0), w_b=(w_b[0], 1), w_o=(w_o[0], 0), w_up=(w_up[0], 1),
               w_down=(w_down[0], 0))
    names = list(big)
    shards = {k: big[k][0].astype(BF16) for k in names}
    ln0g, ln0b = ln0_g.reshape(1, D), ln0_b.reshape(1, D)
    h0, h0b, *rest = ln_fwd(xs, None, ln0g, ln0b, "ln0_fwd_gather_w_in", dilations=DILATIONS[1:],
                            gather=[shards["w_in"], conv_w[0], ffn_conv_w[0]])
    h0_res = [h0b] + [h.reshape(T, D) for h in rest[:2]]
    g_in, g_conv, g_fcw = rest[2:]
    full = {"w_in": _from_blocks(g_in, 1)}
    conv_full = _from_blocks(g_conv, 1)
    fcw_full = _from_blocks(g_fcw, 1)
    late_groups = (("w_a", "w_b", "w_o"), ("w_up", "w_down"))
    late_handles = []
    token = conv_full[:1, :1] * 0.0
    for n, keys in enumerate(late_groups):
        srcs = [shards[k] + token[0, 0].astype(BF16) for k in keys]
        handles, token = copies_start(srcs, [jax.ShapeDtypeStruct((N_DEV,) + s.shape, BF16) for s in srcs],
                                      _to_all_plan, N_DEV - 1, f"gather_late_{n}_start")
        late_handles.append(handles)

    def late_weights(n, after):
        _, lands = copies_wait(late_handles[n], _to_all_plan, after, f"gather_late_{n}_wait")
        for k, land in zip(late_groups[n], lands):
            full[k] = _from_blocks(lax.dynamic_update_index_in_dim(land, shards[k], dev, 0), big[k][1])

    o_q = 3 * D
    o_g = o_q + 3 * QKV_W
    w_pa, w_qkv, w_pg = full["w_in"][:, :o_q], full["w_in"][:, o_q:o_g], full["w_in"][:, o_g:]
    b_pa, b_qkv, b_pg = b_in[:, :o_q], b_in[:, o_q:o_g], b_in[:, o_g:]

    proj_a = mm_nn(h0b, w_pa, b_pa, ACT, "proj_conv", after=token)
    proj_g = mm_nn(h0b, w_pg, b_pg, ACT, "proj_gates")
    zero_d = jnp.zeros((1, D), F32)
    s_a = conv_a_fwd(proj_a, conv_full, "conv_a_fwd")
    late_weights(0, s_a)
    y_a = mm_nn(s_a, full["w_a"], zero_d, ACT, "branch_a_out")

    def group_cols(m, g):
        return jnp.concatenate([m[:, s * QKV_W + g * GROUP_W:s * QKV_W + (g + 1) * GROUP_W] for s in range(3)], 1)

    w_grp = [group_cols(w_qkv, g) for g in range(3)]
    qkvs, outs, lses = [], [], []
    for g, d in enumerate(DILATIONS):
        qkv = mm_nn(h0_res[g], w_grp[g], group_cols(b_qkv, g), BF16, f"proj_qkv_{g}").reshape(d, T // d, 3 * GROUP_W)
        o, l = att_fwd(qkv, g, f"att_fwd_{g}")
        qkvs.append(qkv)
        outs.append(o)
        lses.append(l)
    comb = combine_fwd(outs, lses, "combine_fwd")
    y_b = mm_nn(comb, full["w_b"], zero_d, ACT, "branch_b_out")
    z = gate_fwd(proj_g, y_a, y_b, "gate_fwd")
    h1, h1b, mix = ln_fwd(h0, ("nn", z, full["w_o"], b_o), ln1_g, ln1_b, "mix_out_ln1_fwd")
    late_weights(1, h1b)
    up, f_act = ffn_up_conv_f(h1b, full["w_up"], b_up, fcw_full, ffn_conv_b, "ffn_up_conv_f")

    dr2, dr2b, d_ln2_g, d_ln2_b, d_b_down, loss_part = ln_bwd(
        h1, ("nn", f_act, full["w_down"], b_down), ln2_g, ln2_b, None, None, tgt, "ffn_down_ln2_loss_bwd")
    dw_down, _ = mm_tn(f_act, dr2b, "dw_down")
    d_a, d_gate, cs_a, cs_gate, d_fcb, d_fcw = conv_f_bwd(dr2b, full["w_down"], up, fcw_full, ffn_conv_b,
                                                          "d_ffn_act_conv_f_bwd")
    dw_up_a, _ = mm_tn(h1b, d_a, "dw_up_a")
    dw_up_g, _ = mm_tn(h1b, d_gate, "dw_up_gate")
    dr1, dr1b, d_ln1_g, d_ln1_b, d_b_o, _ = ln_bwd(h0, mix, ln1_g, ln1_b, dr2, ("nt", [d_a, d_gate], full["w_up"]), None,
                                                   "d_h1_ln1_bwd")
    dw_o, _ = mm_tn(z, dr1b, "dw_o")
    dy_a, dy_b, dproj_g = gate_bwd(dr1b, full["w_o"], proj_g, y_a, y_b, "d_z_gate_bwd")
    dw_a, _ = mm_tn(s_a, dy_a, "dw_a")
    dproj_a, d_conv = conv_a_bwd(dy_a, full["w_a"], proj_a, conv_full, "d_s_a_conv_a_bwd")
    dw_b, _ = mm_tn(comb, dy_b, "dw_b")

    rs_mine, rs_sib, rs_handles = {}, {}, {}

    sib_handles = {}

    def to_sibling_start(keys, grads, tag):
        parts = [_to_blocks(grads[k], big[k][1]) for k in keys]
        handles, tok = copies_start(parts, [jax.ShapeDtypeStruct((4,) + p.shape[1:], BF16) for p in parts],
                                    _to_sibling_plan, 4, f"grads_to_sibling_{tag}_start")
        sib_handles[tag] = (keys, handles)
        return tok

    def to_chips_start(tag, after):
        keys, handles = sib_handles[tag]
        parts, from_sib = copies_wait(handles, _to_sibling_plan, after, f"grads_to_sibling_{tag}_wait")
        sums = [pair_add(a, b, place, f"chip_sum_{k}") for k, a, b in zip(keys, parts, from_sib)]
        handles, tok = copies_start(sums, [jax.ShapeDtypeStruct((3,) + s.shape[1:], BF16) for s in sums],
                                    _to_chips_plan, 3, f"grads_to_chips_{tag}_start")
        for k, a, b in zip(keys, parts, from_sib):
            rs_mine[k], rs_sib[k] = a, b
        rs_handles[tag] = (keys, handles)
        return tok

    tok_a = to_sibling_start(("w_a", "w_b", "w_o", "w_up", "w_down"),
                             dict(w_a=dw_a, w_b=dw_b, w_o=dw_o, w_up=jnp.concatenate([dw_up_a, dw_up_g], 1),
                                  w_down=dw_down), "a")
    dcomb = mm_nt(dy_b, full["w_b"], None, "d_comb", after=tok_a, out_dtype=ACT)
    dos, dms = combine_bwd(dcomb, outs, lses, "combine_bwd")
    tok_a = to_chips_start("a", dms[0])
    dw_grp, cs_grp, dqkvs = [], [], []
    for g, d in enumerate(DILATIONS):
        dq, dk, dv = att_bwd(qkvs[g], dos[g], lses[g], dms[g], g, f"att_bwd_{g}", after=tok_a if g == 0 else None)
        dqkv = [t.reshape(T, GROUP_W) for t in (dq, dk, dv)]
        dwg, csg = mm_tn(h0_res[g], dqkv, f"dw_in_qkv_{g}")
        dqkvs.append(dqkv)
        dw_grp.append(dwg)
        cs_grp.append(csg)
    dw_pa, cs_pa = mm_tn(h0b, dproj_a, "dw_in_conv")
    dw_pg, cs_pg = mm_tn(h0b, dproj_g, "dw_in_gates")

    def ungroup(parts):
        return jnp.concatenate([p[:, s * GROUP_W:(s + 1) * GROUP_W] for s in range(3) for p in parts], 1)

    db_in_parts = [cs_pa, ungroup(cs_grp), cs_pg]
    tok_b = to_sibling_start(("w_in",), dict(w_in=jnp.concatenate([dw_pa, ungroup(dw_grp), dw_pg], 1)), "b")
    dh0 = mm_nt(dproj_a, w_pa, None, "d_h0_conv", after=tok_b)
    tok_b = to_chips_start("b", dh0)
    dh0 = mm_nt(dproj_g, w_pg, dh0, "d_h0_gates", after=tok_b)
    dh0_res = [(mm_nt(dqkvs[g], w_grp[g], None, f"d_h0_qkv_{g}").reshape(d, T // d, D), d)
               for g, d in enumerate(DILATIONS) if g > 0]
    dx, _, d_ln0_g, d_ln0_b, _, _ = ln_bwd(xs, None, ln0g, ln0b, dr1, ("nt", dqkvs[0], w_grp[0]), None, "d_h0_ln0_bwd",
                                           by_residue=[(dh0.reshape(1, T, D), 1)] + dh0_res)

    small = [d_ln0_g, d_ln0_b, jnp.concatenate(db_in_parts, 1), d_conv, d_b_o, d_ln1_g, d_ln1_b,
             jnp.concatenate([cs_a, cs_gate], 1), d_fcw, d_fcb, d_b_down, d_ln2_g, d_ln2_b, loss_part]
    packed, sizes = _pack(small)
    total = all_sum_small(packed, "sum_small")
    (g_ln0_g, g_ln0_b, g_b_in, g_conv_full, g_b_o, g_ln1_g, g_ln1_b, g_b_up, g_fcw_full, g_fcb, g_b_down, g_ln2_g,
     g_ln2_b, loss) = _unpack(total, sizes, [a.shape for a in small])
    cw = conv_w.shape[-1]
    fw = ffn_conv_w.shape[-1]
    g_conv = lax.dynamic_slice_in_dim(g_conv_full, dev * cw, cw, 1)
    g_fcw = lax.dynamic_slice_in_dim(g_fcw_full, dev * fw, fw, 1)

    from_chips = {}
    for tag, (keys, handles) in rs_handles.items():
        _, lands = copies_wait(handles, _to_chips_plan, total, f"grads_to_chips_{tag}_wait")
        from_chips.update(zip(keys, lands))

    moments = dict(w_in=(m_w_in, v_w_in), w_a=(m_w_a, v_w_a), w_b=(m_w_b, v_w_b), w_o=(m_w_o, v_w_o),
                   w_up=(m_w_up, v_w_up), w_down=(m_w_down, v_w_down))
    res_big = {}
    for k in names:
        res_big[k] = adamw_sharded(big[k][0], moments[k][0][0], moments[k][1][0], rs_mine[k], rs_sib[k], from_chips[k],
                                   place, f"adamw_{k}")

    small_names = ["ln0_g", "ln0_b", "b_in", "conv_w", "b_o", "ln1_g", "ln1_b", "b_up", "ffn_conv_w", "ffn_conv_b",
                   "b_down", "ln2_g", "ln2_b"]
    small_w = [ln0_g, ln0_b, b_in, conv_w, b_o, ln1_g, ln1_b, b_up, ffn_conv_w, ffn_conv_b, b_down, ln2_g, ln2_b]
    small_m = [m_ln0_g, m_ln0_b, m_b_in, m_conv_w, m_b_o, m_ln1_g, m_ln1_b, m_b_up, m_ffn_conv_w, m_ffn_conv_b,
               m_b_down, m_ln2_g, m_ln2_b]
    small_v = [v_ln0_g, v_ln0_b, v_b_in, v_conv_w, v_b_o, v_ln1_g, v_ln1_b, v_b_up, v_ffn_conv_w, v_ffn_conv_b,
               v_b_down, v_ln2_g, v_ln2_b]
    small_g = [g_ln0_g, g_ln0_b, g_b_in, g_conv, g_b_o, g_ln1_g, g_ln1_b, g_b_up, g_fcw, g_fcb, g_b_down, g_ln2_g,
               g_ln2_b]
    shapes = [w.shape for w in small_w]
    small_g = [g.reshape(s) for g, s in zip(small_g, shapes)]
    pw, psz = _pack(small_w)
    pg, _ = _pack(small_g)
    pm, _ = _pack(small_m)
    pv, _ = _pack(small_v)
    pd, pnm, pnv = adamw_packed(pw, pg, pm, pv, "adamw_small")
    res_small = {k: (g, d_, m_, v_) for k, g, d_, m_, v_ in zip(
        small_names, small_g, _unpack(pd, psz, shapes), _unpack(pnm, psz, shapes), _unpack(pnv, psz, shapes))}

    order = ["ln0_g", "ln0_b", "w_in", "b_in", "conv_w", "w_a", "w_b", "w_o", "b_o", "ln1_g", "ln1_b", "w_up", "b_up",
             "ffn_conv_w", "ffn_conv_b", "w_down", "b_down", "ln2_g", "ln2_b"]

    def result(k, j):
        if k in res_big:
            return res_big[k][j][None]
        return res_small[k][j]

    out = [loss.reshape(()), dx.reshape(x.shape)]
    for j in range(4):
        out += [result(k, j) for k in order]
    return tuple(out)
```
